```python
import math
import jax, jax.numpy as jnp
from jax import lax
import numpy as np

D_MODEL = 1024
BATCH = 8
SEQ = 4096
DEPTH = 1

N_MEM = 256
MEM_HEADS = 4
MEM_HEAD_DIM = D_MODEL // MEM_HEADS
HEAD_DIM = 64
SWA_HEADS = D_MODEL // HEAD_DIM
SWA_KV_HEADS = 4
WINDOW = 128
BLOCK = 128
REL_BUCKETS = 32
REL_MAX_DIST = 128
CONV_WIDTH = D_MODEL
CONV_K = 3
D_FF = 2816
EPS = 1e-6
NEG_INF = -1e30
POS_PAD = 1 << 30

Q_WIDTH = SWA_HEADS * HEAD_DIM
KV_WIDTH = SWA_KV_HEADS * HEAD_DIM
IN_WIDTHS = (Q_WIDTH, KV_WIDTH, KV_WIDTH, CONV_WIDTH, CONV_WIDTH, CONV_WIDTH, D_MODEL, D_MODEL)
IN_WIDTH = Q_WIDTH + 2 * KV_WIDTH + 3 * CONV_WIDTH + 2 * D_MODEL

kernel_name = "hybrid_gated_swa_shortconv_macaron"


def rms_norm(x, g):
    xf = x.astype(jnp.float32)
    y = xf * lax.rsqrt(jnp.mean(xf * xf, axis=-1, keepdims=True) + EPS)
    return (y * g.astype(jnp.float32)).astype(x.dtype)


def swiglu_ffn(h, w_gu, w_down):
    gate, up = jnp.split(h @ w_gu, 2, axis=-1)
    return (jax.nn.silu(gate) * up) @ w_down


def split_cols(t, widths):
    outs, start = [], 0
    for w in widths:
        outs.append(t[..., start:start + w])
        start += w
    return outs


def t5_causal_bucket(rel):
    n = jnp.maximum(rel, 0)
    max_exact = REL_BUCKETS // 2
    nf = jnp.maximum(n, 1).astype(jnp.float32)
    large = max_exact + (jnp.log(nf / max_exact) / math.log(REL_MAX_DIST / max_exact)
                         * (REL_BUCKETS - max_exact)).astype(jnp.int32)
    large = jnp.minimum(large, REL_BUCKETS - 1)
    return jnp.where(n < max_exact, n, large)


def with_prev_block(t, fill):
    b, s = t.shape[0], t.shape[1]
    nb = s // BLOCK
    tb = t.reshape((b, nb, BLOCK) + t.shape[2:])
    pad = jnp.full_like(tb[:, :1], fill)
    prev = jnp.concatenate([pad, tb[:, :-1]], axis=1)
    return jnp.concatenate([prev, tb], axis=2)


def sliding_window_gqa(q, k, v, positions, rel_bias, sinks):
    b, s, _ = q.shape
    nb = s // BLOCK
    grp = SWA_HEADS // SWA_KV_HEADS
    qb = q.reshape(b, nb, BLOCK, SWA_KV_HEADS, grp, HEAD_DIM)
    kb = with_prev_block(k.reshape(b, s, SWA_KV_HEADS, HEAD_DIM), 0)
    vb = with_prev_block(v.reshape(b, s, SWA_KV_HEADS, HEAD_DIM), 0)
    pq = positions.reshape(b, nb, BLOCK)
    pk = with_prev_block(positions, POS_PAD)
    rel = pq[:, :, :, None] - pk[:, :, None, :]
    visible = (rel >= 0) & (rel < WINDOW)
    bias = jnp.moveaxis(rel_bias[t5_causal_bucket(rel)], -1, 2)
    bias = bias.reshape(b, nb, SWA_KV_HEADS, grp, BLOCK, 2 * BLOCK).astype(jnp.float32)
    logits = jnp.einsum('bnqhgd,bnkhd->bnhgqk', qb, kb).astype(jnp.float32) * (HEAD_DIM ** -0.5)
    logits = jnp.where(visible[:, :, None, None], logits + bias, NEG_INF)
    sink = jnp.broadcast_to(sinks.astype(jnp.float32).reshape(1, 1, SWA_KV_HEADS, grp, 1, 1),
                            logits.shape[:-1] + (1,))
    probs = jax.nn.softmax(jnp.concatenate([logits, sink], axis=-1), axis=-1)[..., :-1]
    out = jnp.einsum('bnhgqk,bnkhd->bnqhgd', probs.astype(vb.dtype), vb)
    return out.reshape(b, s, Q_WIDTH)


def short_conv(u, w_conv):
    s = u.shape[1]
    up = jnp.pad(u, ((0, 0), (CONV_K - 1, 0), (0, 0)))
    return sum(w_conv[j] * up[:, j:j + s] for j in range(CONV_K))


def memory_cross_attention(h, mem_h, w_q, w_kv, w_o):
    b, s, _ = h.shape
    m = mem_h.shape[1]
    q = (h @ w_q).reshape(b, s, MEM_HEADS, MEM_HEAD_DIM)
    k, v = jnp.split(mem_h @ w_kv, 2, axis=-1)
    k = k.reshape(b, m, MEM_HEADS, MEM_HEAD_DIM)
    v = v.reshape(b, m, MEM_HEADS, MEM_HEAD_DIM)
    logits = jnp.einsum('bshd,bmhd->bhsm', q, k).astype(jnp.float32) * (MEM_HEAD_DIM ** -0.5)
    probs = jax.nn.softmax(logits, axis=-1)
    o = jnp.einsum('bhsm,bmhd->bshd', probs.astype(v.dtype), v).reshape(b, s, D_MODEL)
    return o @ w_o


def _fwd_setup_inputs(seed: int = 0) -> dict:
    key = jax.random.key(seed)
    ks = jax.random.split(key, 24)
    f32 = jnp.float32

    def dense(k, shape, fan_in):
        return jax.random.normal(k, shape, f32) * (fan_in ** -0.5)

    def gain(k, shape):
        return 1.0 + 0.05 * jax.random.normal(k, shape, f32)

    L = DEPTH
    x = jax.random.normal(ks[0], (BATCH, SEQ, D_MODEL), f32)
    mem = jax.random.normal(ks[1], (BATCH, N_MEM, D_MODEL), f32)
    offsets = jax.random.randint(ks[2], (BATCH, 1), 0, 1024, dtype=jnp.int32)
    positions = offsets + jnp.arange(SEQ, dtype=jnp.int32)[None, :]
    return {
        "x": x,
        "mem": mem,
        "positions": positions,
        "rel_bias": 0.5 * jax.random.normal(ks[3], (REL_BUCKETS, SWA_HEADS), f32),
        "ffn1_norm": gain(ks[4], (L, D_MODEL)),
        "ffn1_w_gu": dense(ks[5], (L, D_MODEL, 2 * D_FF), D_MODEL),
        "ffn1_w_down": dense(ks[6], (L, D_FF, D_MODEL), D_FF),
        "mix_norm": gain(ks[7], (L, D_MODEL)),
        "w_in": dense(ks[8], (L, D_MODEL, IN_WIDTH), D_MODEL),
        "sinks": 0.5 * jax.random.normal(ks[9], (L, SWA_HEADS), f32),
        "conv_w": dense(ks[10], (L, CONV_K, CONV_WIDTH), CONV_K),
        "w_out": dense(ks[11], (L, D_MODEL, D_MODEL), D_MODEL),
        "xattn_norm": gain(ks[12], (L, D_MODEL)),
        "mem_norm": gain(ks[13], (L, D_MODEL)),
        "xattn_wq": dense(ks[14], (L, D_MODEL, D_MODEL), D_MODEL),
        "xattn_wkv": dense(ks[15], (L, D_MODEL, 2 * D_MODEL), D_MODEL),
        "xattn_wo": dense(ks[16], (L, D_MODEL, D_MODEL), D_MODEL),
        "ffn2_norm": gain(ks[17], (L, D_MODEL)),
        "ffn2_w_gu": dense(ks[18], (L, D_MODEL, 2 * D_FF), D_MODEL),
        "ffn2_w_down": dense(ks[19], (L, D_FF, D_MODEL), D_FF),
        "final_norm": gain(ks[20], (D_MODEL,)),
    }


def _fwd_reference(x, mem, positions, rel_bias, ffn1_norm, ffn1_w_gu, ffn1_w_down, mix_norm, w_in,
              sinks, conv_w, w_out, xattn_norm, mem_norm, xattn_wq, xattn_wkv, xattn_wo,
              ffn2_norm, ffn2_w_gu, ffn2_w_down, final_norm):
    for l in range(DEPTH):
        x = x + 0.5 * swiglu_ffn(rms_norm(x, ffn1_norm[l]), ffn1_w_gu[l], ffn1_w_down[l])
        h = rms_norm(x, mix_norm[l])
        q, k, v, c_pre, b_post, u, g_attn, g_conv = split_cols(h @ w_in[l], IN_WIDTHS)
        attn = sliding_window_gqa(q, k, v, positions, rel_bias, sinks[l])
        conv = b_post * short_conv(c_pre * u, conv_w[l])
        merged = jax.nn.sigmoid(g_attn) * attn + jax.nn.sigmoid(g_conv) * conv
        x = x + merged @ w_out[l]
        x = x + memory_cross_attention(rms_norm(x, xattn_norm[l]), rms_norm(mem, mem_norm[l]),
                                       xattn_wq[l], xattn_wkv[l], xattn_wo[l])
        x = x + 0.5 * swiglu_ffn(rms_norm(x, ffn2_norm[l]), ffn2_w_gu[l], ffn2_w_down[l])
    return rms_norm(x, final_norm)


import jax as _jax
import jax.numpy as _jnp

TWIN_FORMAT = 'train_step'
FWD_PARAMS = ['x', 'mem', 'positions', 'rel_bias', 'ffn1_norm', 'ffn1_w_gu', 'ffn1_w_down', 'mix_norm', 'w_in', 'sinks', 'conv_w', 'w_out', 'xattn_norm', 'mem_norm', 'xattn_wq', 'xattn_wkv', 'xattn_wo', 'ffn2_norm', 'ffn2_w_gu', 'ffn2_w_down', 'final_norm']
TWIN_WEIGHTS = ['rel_bias', 'ffn1_norm', 'ffn1_w_gu', 'ffn1_w_down', 'mix_norm', 'w_in', 'sinks', 'conv_w', 'w_out', 'xattn_norm', 'mem_norm', 'xattn_wq', 'xattn_wkv', 'xattn_wo', 'ffn2_norm', 'ffn2_w_gu', 'ffn2_w_down', 'final_norm']
TWIN_DIFF_INPUT = 'x'
TWIN_INPUTS = ['x', 'mem', 'positions', 'rel_bias', 'ffn1_norm', 'ffn1_w_gu', 'ffn1_w_down', 'mix_norm', 'w_in', 'sinks', 'conv_w', 'w_out', 'xattn_norm', 'mem_norm', 'xattn_wq', 'xattn_wkv', 'xattn_wo', 'ffn2_norm', 'ffn2_w_gu', 'ffn2_w_down', 'final_norm', 'loss_target', 'm_rel_bias', 'm_ffn1_norm', 'm_ffn1_w_gu', 'm_ffn1_w_down', 'm_mix_norm', 'm_w_in', 'm_sinks', 'm_conv_w', 'm_w_out', 'm_xattn_norm', 'm_mem_norm', 'm_xattn_wq', 'm_xattn_wkv', 'm_xattn_wo', 'm_ffn2_norm', 'm_ffn2_w_gu', 'm_ffn2_w_down', 'm_final_norm', 'v_rel_bias', 'v_ffn1_norm', 'v_ffn1_w_gu', 'v_ffn1_w_down', 'v_mix_norm', 'v_w_in', 'v_sinks', 'v_conv_w', 'v_w_out', 'v_xattn_norm', 'v_mem_norm', 'v_xattn_wq', 'v_xattn_wkv', 'v_xattn_wo', 'v_ffn2_norm', 'v_ffn2_w_gu', 'v_ffn2_w_down', 'v_final_norm']
TWIN_OUTPUTS = ['loss', 'grad_x', 'grad_rel_bias', 'grad_ffn1_norm', 'grad_ffn1_w_gu', 'grad_ffn1_w_down', 'grad_mix_norm', 'grad_w_in', 'grad_sinks', 'grad_conv_w', 'grad_w_out', 'grad_xattn_norm', 'grad_mem_norm', 'grad_xattn_wq', 'grad_xattn_wkv', 'grad_xattn_wo', 'grad_ffn2_norm', 'grad_ffn2_w_gu', 'grad_ffn2_w_down', 'grad_final_norm', 'delta_rel_bias', 'delta_ffn1_norm', 'delta_ffn1_w_gu', 'delta_ffn1_w_down', 'delta_mix_norm', 'delta_w_in', 'delta_sinks', 'delta_conv_w', 'delta_w_out', 'delta_xattn_norm', 'delta_mem_norm', 'delta_xattn_wq', 'delta_xattn_wkv', 'delta_xattn_wo', 'delta_ffn2_norm', 'delta_ffn2_w_gu', 'delta_ffn2_w_down', 'delta_final_norm', 'new_m_rel_bias', 'new_m_ffn1_norm', 'new_m_ffn1_w_gu', 'new_m_ffn1_w_down', 'new_m_mix_norm', 'new_m_w_in', 'new_m_sinks', 'new_m_conv_w', 'new_m_w_out', 'new_m_xattn_norm', 'new_m_mem_norm', 'new_m_xattn_wq', 'new_m_xattn_wkv', 'new_m_xattn_wo', 'new_m_ffn2_norm', 'new_m_ffn2_w_gu', 'new_m_ffn2_w_down', 'new_m_final_norm', 'new_v_rel_bias', 'new_v_ffn1_norm', 'new_v_ffn1_w_gu', 'new_v_ffn1_w_down', 'new_v_mix_norm', 'new_v_w_in', 'new_v_sinks', 'new_v_conv_w', 'new_v_w_out', 'new_v_xattn_norm', 'new_v_mem_norm', 'new_v_xattn_wq', 'new_v_xattn_wkv', 'new_v_xattn_wo', 'new_v_ffn2_norm', 'new_v_ffn2_w_gu', 'new_v_ffn2_w_down', 'new_v_final_norm']
TWIN_LEAF_KINDS = {'loss': 'loss', 'grad_x': 'grad_x', 'grad_rel_bias': 'grad_w', 'grad_ffn1_norm': 'grad_w', 'grad_ffn1_w_gu': 'grad_w', 'grad_ffn1_w_down': 'grad_w', 'grad_mix_norm': 'grad_w', 'grad_w_in': 'grad_w', 'grad_sinks': 'grad_w', 'grad_conv_w': 'grad_w', 'grad_w_out': 'grad_w', 'grad_xattn_norm': 'grad_w', 'grad_mem_norm': 'grad_w', 'grad_xattn_wq': 'grad_w', 'grad_xattn_wkv': 'grad_w', 'grad_xattn_wo': 'grad_w', 'grad_ffn2_norm': 'grad_w', 'grad_ffn2_w_gu': 'grad_w', 'grad_ffn2_w_down': 'grad_w', 'grad_final_norm': 'grad_w', 'delta_rel_bias': 'delta_w', 'delta_ffn1_norm': 'delta_w', 'delta_ffn1_w_gu': 'delta_w', 'delta_ffn1_w_down': 'delta_w', 'delta_mix_norm': 'delta_w', 'delta_w_in': 'delta_w', 'delta_sinks': 'delta_w', 'delta_conv_w': 'delta_w', 'delta_w_out': 'delta_w', 'delta_xattn_norm': 'delta_w', 'delta_mem_norm': 'delta_w', 'delta_xattn_wq': 'delta_w', 'delta_xattn_wkv': 'delta_w', 'delta_xattn_wo': 'delta_w', 'delta_ffn2_norm': 'delta_w', 'delta_ffn2_w_gu': 'delta_w', 'delta_ffn2_w_down': 'delta_w', 'delta_final_norm': 'delta_w', 'new_m_rel_bias': 'new_m', 'new_m_ffn1_norm': 'new_m', 'new_m_ffn1_w_gu': 'new_m', 'new_m_ffn1_w_down': 'new_m', 'new_m_mix_norm': 'new_m', 'new_m_w_in': 'new_m', 'new_m_sinks': 'new_m', 'new_m_conv_w': 'new_m', 'new_m_w_out': 'new_m', 'new_m_xattn_norm': 'new_m', 'new_m_mem_norm': 'new_m', 'new_m_xattn_wq': 'new_m', 'new_m_xattn_wkv': 'new_m', 'new_m_xattn_wo': 'new_m', 'new_m_ffn2_norm': 'new_m', 'new_m_ffn2_w_gu': 'new_m', 'new_m_ffn2_w_down': 'new_m', 'new_m_final_norm': 'new_m', 'new_v_rel_bias': 'new_v', 'new_v_ffn1_norm': 'new_v', 'new_v_ffn1_w_gu': 'new_v', 'new_v_ffn1_w_down': 'new_v', 'new_v_mix_norm': 'new_v', 'new_v_w_in': 'new_v', 'new_v_sinks': 'new_v', 'new_v_conv_w': 'new_v', 'new_v_w_out': 'new_v', 'new_v_xattn_norm': 'new_v', 'new_v_mem_norm': 'new_v', 'new_v_xattn_wq': 'new_v', 'new_v_xattn_wkv': 'new_v', 'new_v_xattn_wo': 'new_v', 'new_v_ffn2_norm': 'new_v', 'new_v_ffn2_w_gu': 'new_v', 'new_v_ffn2_w_down': 'new_v', 'new_v_final_norm': 'new_v'}


def _forward(args):
    return _fwd_reference(*[args[k] for k in FWD_PARAMS])


def _output_shape():
    def fwd():
        inp = _fwd_setup_inputs(0)
        return _fwd_reference(*[inp[k] for k in FWD_PARAMS])
    out = _jax.eval_shape(fwd)
    return out.shape, out.dtype

N_MICROBATCH = 1
ADAM_LR = 0.001
ADAM_B1 = 0.9
ADAM_B2 = 0.999
ADAM_EPS = 1e-08
ADAM_WD = 0.01
ADAM_STEP = 10
PER_EXAMPLE_BATCH_AXIS = {'x': 0, 'mem': 0, 'positions': 0, 'loss_target': 0}
SHARED_INPUTS = []
_WEIGHT_DTYPES = {'rel_bias': _jnp.float32, 'ffn1_norm': _jnp.float32, 'ffn1_w_gu': _jnp.float32, 'ffn1_w_down': _jnp.float32, 'mix_norm': _jnp.float32, 'w_in': _jnp.float32, 'sinks': _jnp.float32, 'conv_w': _jnp.float32, 'w_out': _jnp.float32, 'xattn_norm': _jnp.float32, 'mem_norm': _jnp.float32, 'xattn_wq': _jnp.float32, 'xattn_wkv': _jnp.float32, 'xattn_wo': _jnp.float32, 'ffn2_norm': _jnp.float32, 'ffn2_w_gu': _jnp.float32, 'ffn2_w_down': _jnp.float32, 'final_norm': _jnp.float32}
MOMENT_SCALE = {'rel_bias': 2.098216e-02, 'ffn1_norm': 9.030707e-02, 'ffn1_w_gu': 3.887136e-02, 'ffn1_w_down': 6.355670e-02, 'mix_norm': 1.554989e-01, 'w_in': 5.925884e-02, 'sinks': 1.375366e-02, 'conv_w': 8.758607e-02, 'w_out': 8.627617e-02, 'xattn_norm': 1.721445e-02, 'mem_norm': 2.458842e-02, 'xattn_wq': 1.668125e-02, 'xattn_wkv': 1.679548e-02, 'xattn_wo': 1.697147e-02, 'ffn2_norm': 6.317956e-02, 'ffn2_w_gu': 2.707644e-02, 'ffn2_w_down': 4.433700e-02, 'final_norm': 3.211567e+01}


def _to_microbatches(a, axis):
    t = _jnp.moveaxis(a, axis, 0)
    t = t.reshape((N_MICROBATCH, t.shape[0] // N_MICROBATCH) + t.shape[1:])
    return _jnp.moveaxis(t, 1, axis + 1)


def setup_inputs(seed: int = 0) -> dict:
    inp = _fwd_setup_inputs(seed)
    key = _jax.random.fold_in(_jax.random.key(seed), 7919)
    shape, _ = _output_shape()
    out = dict(inp)
    out["loss_target"] = _jax.random.normal(_jax.random.fold_in(key, 0), shape, _jnp.float32)
    for i, name in enumerate(TWIN_WEIGHTS):
        w = inp[name].astype(_jnp.float32)
        if MOMENT_SCALE is None:
            s = _jnp.sqrt(_jnp.mean(_jnp.square(w)) + 1e-30)
        else:
            s = MOMENT_SCALE[name]
        km, kv = _jax.random.split(_jax.random.fold_in(key, i + 1))
        out[name] = w
        out["m_" + name] = s * _jax.random.normal(km, w.shape, _jnp.float32)
        out["v_" + name] = (s * s) * _jax.random.uniform(kv, w.shape, _jnp.float32, 0.5, 1.5)
    if N_MICROBATCH > 1:
        for name, axis in PER_EXAMPLE_BATCH_AXIS.items():
            out[name] = _to_microbatches(out[name], axis)
    return {'x': out['x'], 'mem': out['mem'], 'positions': out['positions'], 'rel_bias': out['rel_bias'], 'ffn1_norm': out['ffn1_norm'], 'ffn1_w_gu': out['ffn1_w_gu'], 'ffn1_w_down': out['ffn1_w_down'], 'mix_norm': out['mix_norm'], 'w_in': out['w_in'], 'sinks': out['sinks'], 'conv_w': out['conv_w'], 'w_out': out['w_out'], 'xattn_norm': out['xattn_norm'], 'mem_norm': out['mem_norm'], 'xattn_wq': out['xattn_wq'], 'xattn_wkv': out['xattn_wkv'], 'xattn_wo': out['xattn_wo'], 'ffn2_norm': out['ffn2_norm'], 'ffn2_w_gu': out['ffn2_w_gu'], 'ffn2_w_down': out['ffn2_w_down'], 'final_norm': out['final_norm'], 'loss_target': out['loss_target'], 'm_rel_bias': out['m_rel_bias'], 'm_ffn1_norm': out['m_ffn1_norm'], 'm_ffn1_w_gu': out['m_ffn1_w_gu'], 'm_ffn1_w_down': out['m_ffn1_w_down'], 'm_mix_norm': out['m_mix_norm'], 'm_w_in': out['m_w_in'], 'm_sinks': out['m_sinks'], 'm_conv_w': out['m_conv_w'], 'm_w_out': out['m_w_out'], 'm_xattn_norm': out['m_xattn_norm'], 'm_mem_norm': out['m_mem_norm'], 'm_xattn_wq': out['m_xattn_wq'], 'm_xattn_wkv': out['m_xattn_wkv'], 'm_xattn_wo': out['m_xattn_wo'], 'm_ffn2_norm': out['m_ffn2_norm'], 'm_ffn2_w_gu': out['m_ffn2_w_gu'], 'm_ffn2_w_down': out['m_ffn2_w_down'], 'm_final_norm': out['m_final_norm'], 'v_rel_bias': out['v_rel_bias'], 'v_ffn1_norm': out['v_ffn1_norm'], 'v_ffn1_w_gu': out['v_ffn1_w_gu'], 'v_ffn1_w_down': out['v_ffn1_w_down'], 'v_mix_norm': out['v_mix_norm'], 'v_w_in': out['v_w_in'], 'v_sinks': out['v_sinks'], 'v_conv_w': out['v_conv_w'], 'v_w_out': out['v_w_out'], 'v_xattn_norm': out['v_xattn_norm'], 'v_mem_norm': out['v_mem_norm'], 'v_xattn_wq': out['v_xattn_wq'], 'v_xattn_wkv': out['v_xattn_wkv'], 'v_xattn_wo': out['v_xattn_wo'], 'v_ffn2_norm': out['v_ffn2_norm'], 'v_ffn2_w_gu': out['v_ffn2_w_gu'], 'v_ffn2_w_down': out['v_ffn2_w_down'], 'v_final_norm': out['v_final_norm']}


def _loss(weights, diff, rest, loss_target):
    with _jax.named_scope("forward"):
        args = {**rest, TWIN_DIFF_INPUT: diff, **{k: w.astype(_WEIGHT_DTYPES[k]) for k, w in weights.items()}}
        y = _forward(args)
    with _jax.named_scope("loss_head"):
        err = _jnp.square(y.astype(_jnp.float32) - loss_target)
        return 0.5 * _jnp.sum(_jnp.mean(err, axis=-1)) if err.ndim else 0.5 * err


def _adamw(w, g, m, v):
    m = ADAM_B1 * m + (1.0 - ADAM_B1) * g
    v = ADAM_B2 * v + (1.0 - ADAM_B2) * _jnp.square(g)
    m_hat = m / (1.0 - ADAM_B1 ** ADAM_STEP)
    v_hat = v / (1.0 - ADAM_B2 ** ADAM_STEP)
    delta = -ADAM_LR * (m_hat / (_jnp.sqrt(v_hat) + ADAM_EPS) + ADAM_WD * w)
    return delta, m, v


def reference(x, mem, positions, rel_bias, ffn1_norm, ffn1_w_gu, ffn1_w_down, mix_norm, w_in, sinks, conv_w, w_out, xattn_norm, mem_norm, xattn_wq, xattn_wkv, xattn_wo, ffn2_norm, ffn2_w_gu, ffn2_w_down, final_norm, loss_target, m_rel_bias, m_ffn1_norm, m_ffn1_w_gu, m_ffn1_w_down, m_mix_norm, m_w_in, m_sinks, m_conv_w, m_w_out, m_xattn_norm, m_mem_norm, m_xattn_wq, m_xattn_wkv, m_xattn_wo, m_ffn2_norm, m_ffn2_w_gu, m_ffn2_w_down, m_final_norm, v_rel_bias, v_ffn1_norm, v_ffn1_w_gu, v_ffn1_w_down, v_mix_norm, v_w_in, v_sinks, v_conv_w, v_w_out, v_xattn_norm, v_mem_norm, v_xattn_wq, v_xattn_wkv, v_xattn_wo, v_ffn2_norm, v_ffn2_w_gu, v_ffn2_w_down, v_final_norm):
    given = dict(x=x, mem=mem, positions=positions, rel_bias=rel_bias, ffn1_norm=ffn1_norm, ffn1_w_gu=ffn1_w_gu, ffn1_w_down=ffn1_w_down, mix_norm=mix_norm, w_in=w_in, sinks=sinks, conv_w=conv_w, w_out=w_out, xattn_norm=xattn_norm, mem_norm=mem_norm, xattn_wq=xattn_wq, xattn_wkv=xattn_wkv, xattn_wo=xattn_wo, ffn2_norm=ffn2_norm, ffn2_w_gu=ffn2_w_gu, ffn2_w_down=ffn2_w_down, final_norm=final_norm, loss_target=loss_target, m_rel_bias=m_rel_bias, m_ffn1_norm=m_ffn1_norm, m_ffn1_w_gu=m_ffn1_w_gu, m_ffn1_w_down=m_ffn1_w_down, m_mix_norm=m_mix_norm, m_w_in=m_w_in, m_sinks=m_sinks, m_conv_w=m_conv_w, m_w_out=m_w_out, m_xattn_norm=m_xattn_norm, m_mem_norm=m_mem_norm, m_xattn_wq=m_xattn_wq, m_xattn_wkv=m_xattn_wkv, m_xattn_wo=m_xattn_wo, m_ffn2_norm=m_ffn2_norm, m_ffn2_w_gu=m_ffn2_w_gu, m_ffn2_w_down=m_ffn2_w_down, m_final_norm=m_final_norm, v_rel_bias=v_rel_bias, v_ffn1_norm=v_ffn1_norm, v_ffn1_w_gu=v_ffn1_w_gu, v_ffn1_w_down=v_ffn1_w_down, v_mix_norm=v_mix_norm, v_w_in=v_w_in, v_sinks=v_sinks, v_conv_w=v_conv_w, v_w_out=v_w_out, v_xattn_norm=v_xattn_norm, v_mem_norm=v_mem_norm, v_xattn_wq=v_xattn_wq, v_xattn_wkv=v_xattn_wkv, v_xattn_wo=v_xattn_wo, v_ffn2_norm=v_ffn2_norm, v_ffn2_w_gu=v_ffn2_w_gu, v_ffn2_w_down=v_ffn2_w_down, v_final_norm=v_final_norm)
    weights = {n: given[n] for n in TWIN_WEIGHTS}
    shared = {n: given[n] for n in SHARED_INPUTS}
    per_example = {n: given[n] for n in ['x', 'mem', 'positions']}
    grad_fn = _jax.value_and_grad(_loss, argnums=(0, 1))

    def one_microbatch(ex, loss_target):
        ex = dict(ex)
        diff = ex.pop(TWIN_DIFF_INPUT)
        return grad_fn(weights, diff, {**shared, **ex}, loss_target)

    if N_MICROBATCH == 1:
        loss, (grad_w, grad_x) = one_microbatch(per_example, given["loss_target"])
    else:
        def body(carry, xs):
            loss_sum, grad_sum = carry
            l_k, (gw_k, gx_k) = one_microbatch(xs[0], xs[1])
            with _jax.named_scope("update"):
                return (loss_sum + l_k, _jax.tree.map(_jnp.add, grad_sum, gw_k)), gx_k

        init = (_jnp.zeros((), _jnp.float32), _jax.tree.map(_jnp.zeros_like, weights))
        (loss, grad_w), grad_x = _jax.lax.scan(body, init, (per_example, given["loss_target"]))
    with _jax.named_scope("update"):
        delta_w, new_m, new_v = {}, {}, {}
        for n in TWIN_WEIGHTS:
            delta_w[n], new_m[n], new_v[n] = _adamw(weights[n], grad_w[n], given["m_" + n], given["v_" + n])
    return (loss, grad_x, *[grad_w[n] for n in TWIN_WEIGHTS], *[delta_w[n] for n in TWIN_WEIGHTS],
            *[new_m[n] for n in TWIN_WEIGHTS], *[new_v[n] for n in TWIN_WEIGHTS])
```

```python
import math

import numpy as np
import jax
import jax.numpy as jnp
from jax import lax
from jax.experimental import pallas as pl
from jax.experimental.pallas import tpu as pltpu

F32, BF16 = jnp.float32, jnp.bfloat16
MESH = pl.DeviceIdType.MESH

D = 1024
N_DEV = 8
D_FF = 2816
FS = D_FF // 4
HEAD = 64
N_HEADS, N_KV = 16, 4
BLK = 128
XH, XHD = 4, 256
REL_BUCKETS, REL_EXACT, REL_MAX_DIST = 32, 16, 128
EPS, NEG = 1e-6, -1e30
ADAM_LR, ADAM_B1, ADAM_B2, ADAM_EPS, ADAM_WD, ADAM_STEP = 0.001, 0.9, 0.999, 1e-08, 0.01, 10
VMEM_LIMIT_V7X = 56 * 2**20
SMALL_ROWS = 16
ROW_RELB, ROW_LOSS, ROW_CONV = 6, 7, 8


def _bucket_thresholds():
    n = np.arange(REL_MAX_DIST)
    nf = np.maximum(n, 1).astype(np.float32)
    large = REL_EXACT + (np.log(nf / np.float32(REL_EXACT)) / np.float32(math.log(REL_MAX_DIST / REL_EXACT))
                         * np.float32(REL_BUCKETS - REL_EXACT)).astype(np.int32)
    b = np.where(n < REL_EXACT, n, np.minimum(large, REL_BUCKETS - 1))
    return [int(np.argmax(b >= REL_EXACT + k)) for k in range(1, REL_BUCKETS - REL_EXACT)]


BUCKET_THRESHOLDS = _bucket_thresholds()


def _pcall(body, *, name, grid, in_specs, out_specs, out_shape, scratch=()):
    return pl.pallas_call(
        body, name=name, grid=grid, in_specs=in_specs, out_specs=out_specs, out_shape=out_shape,
        scratch_shapes=list(scratch),
        compiler_params=pltpu.CompilerParams(dimension_semantics=("arbitrary",) * len(grid),
                                             vmem_limit_bytes=VMEM_LIMIT_V7X),
    )


def _dot(a, b):
    return jnp.dot(a, b, preferred_element_type=F32)


def _dot_nt(a, b):
    return lax.dot_general(a, b, (((1,), (1,)), ((), ())), preferred_element_type=F32)


def _dot_tn(a, b):
    return lax.dot_general(a, b, (((0,), (0,)), ((), ())), preferred_element_type=F32)


def _sds(shape, dtype):
    return jax.ShapeDtypeStruct(tuple(shape), dtype)


def _rmsnorm(x, g, name):
    m, d = x.shape
    tm = min(512, m)

    def body(x_ref, g_ref, h_ref):
        xv = x_ref[...]
        r = lax.rsqrt(jnp.mean(xv * xv, axis=-1, keepdims=True) + EPS)
        h_ref[...] = (xv * r * g_ref[...]).astype(BF16)

    return _pcall(body, name=name, grid=(m // tm,),
                  in_specs=[pl.BlockSpec((tm, d), lambda i: (i, 0)), pl.BlockSpec((1, d), lambda i: (0, 0))],
                  out_specs=pl.BlockSpec((tm, d), lambda i: (i, 0)), out_shape=_sds((m, d), BF16))(x, g)


def _mm_nn(a, b, name, tm=512):
    m, k = a.shape
    nj, _, n = b.shape
    tm = min(tm, m)

    def body(a_ref, b_ref, o_ref):
        o_ref[...] = _dot(a_ref[...], b_ref[...]).astype(BF16)

    return _pcall(body, name=name, grid=(nj, m // tm),
                  in_specs=[pl.BlockSpec((tm, k), lambda j, i: (i, 0)),
                            pl.BlockSpec((None, k, n), lambda j, i: (j, 0, 0))],
                  out_specs=pl.BlockSpec((None, tm, n), lambda j, i: (j, i, 0)),
                  out_shape=_sds((nj, m, n), BF16))(a, b)


def _ffn_up(h, w4, name, tm=512):
    s, d = h.shape
    tm = min(tm, s)

    def body(h_ref, w_ref, gu_ref, a_ref):
        hv = h_ref[...]
        g = _dot(hv, w_ref[0])
        u = _dot(hv, w_ref[1])
        gu_ref[0] = g.astype(BF16)
        gu_ref[1] = u.astype(BF16)
        a_ref[...] = (g * jax.nn.sigmoid(g) * u).astype(BF16)

    return _pcall(body, name=name, grid=(4, s // tm),
                  in_specs=[pl.BlockSpec((tm, d), lambda p, i: (i, 0)),
                            pl.BlockSpec((2, None, d, FS), lambda p, i: (0, p, 0, 0))],
                  out_specs=[pl.BlockSpec((2, None, tm, FS), lambda p, i: (0, p, i, 0)),
                             pl.BlockSpec((None, tm, FS), lambda p, i: (p, i, 0))],
                  out_shape=[_sds((2, 4, s, FS), BF16), _sds((4, s, FS), BF16)])(h, w4)


def _mm_res_norm(a, w, xres, gain, scale, name, tm=512):
    npart, s, kp = a.shape
    tm = min(tm, s)

    def body(a_ref, w_ref, x_ref, g_ref, xo_ref, h_ref):
        acc = _dot(a_ref[0], w_ref[0])
        for p in range(1, npart):
            acc = acc + _dot(a_ref[p], w_ref[p])
        xn = x_ref[...] + scale * acc
        xo_ref[...] = xn
        r = lax.rsqrt(jnp.mean(xn * xn, axis=-1, keepdims=True) + EPS)
        h_ref[...] = (xn * r * g_ref[...]).astype(BF16)

    return _pcall(body, name=name, grid=(s // tm,),
                  in_specs=[pl.BlockSpec((npart, tm, kp), lambda i: (0, i, 0)),
                            pl.BlockSpec((npart, kp, D), lambda i: (0, 0, 0)),
                            pl.BlockSpec((tm, D), lambda i: (i, 0)),
                            pl.BlockSpec((1, D), lambda i: (0, 0))],
                  out_specs=[pl.BlockSpec((tm, D), lambda i: (i, 0)), pl.BlockSpec((tm, D), lambda i: (i, 0))],
                  out_shape=[_sds((s, D), F32), _sds((s, D), BF16)])(a, w, xres, gain)


def _ffn_down_loss(a, w, xres, gain, target, name, tm=512):
    npart, s, kp = a.shape
    tm = min(tm, s)

    def body(a_ref, w_ref, x_ref, g_ref, t_ref, dx_ref, dxb_ref, loss_ref, dg_ref):
        i = pl.program_id(0)
        acc = _dot(a_ref[0], w_ref[0])
        for p in range(1, npart):
            acc = acc + _dot(a_ref[p], w_ref[p])
        xn = x_ref[...] + 0.5 * acc
        r = lax.rsqrt(jnp.mean(xn * xn, axis=-1, keepdims=True) + EPS)
        xh = xn * r
        gv = g_ref[...]
        err = xh * gv - t_ref[...]
        part = 0.5 * jnp.sum(jnp.mean(err * err, axis=-1, keepdims=True), axis=0, keepdims=True)
        dy = err * (1.0 / D)
        dyg = dy * gv
        dxn = r * (dyg - xh * jnp.mean(dyg * xh, axis=-1, keepdims=True))
        dx_ref[...] = dxn
        dxb_ref[...] = dxn.astype(BF16)

        @pl.when(i == 0)
        def _():
            loss_ref[...] = jnp.zeros_like(loss_ref)
            dg_ref[...] = jnp.zeros_like(dg_ref)

        loss_ref[...] += jnp.broadcast_to(part, loss_ref.shape)
        dg_ref[...] += jnp.sum(dy * xh, axis=0, keepdims=True)

    return _pcall(body, name=name, grid=(s // tm,),
                  in_specs=[pl.BlockSpec((npart, tm, kp), lambda i: (0, i, 0)),
                            pl.BlockSpec((npart, kp, D), lambda i: (0, 0, 0)),
                            pl.BlockSpec((tm, D), lambda i: (i, 0)),
                            pl.BlockSpec((1, D), lambda i: (0, 0)),
                            pl.BlockSpec((tm, D), lambda i: (i, 0))],
                  out_specs=[pl.BlockSpec((tm, D), lambda i: (i, 0)), pl.BlockSpec((tm, D), lambda i: (i, 0)),
                             pl.BlockSpec((8, 128), lambda i: (0, 0)), pl.BlockSpec((1, D), lambda i: (0, 0))],
                  out_shape=[_sds((s, D), F32), _sds((s, D), BF16), _sds((8, 128), F32), _sds((1, D), F32)],
                  )(a, w, xres, gain, target)


def _band_tiles():
    i = lax.broadcasted_iota(jnp.int32, (BLK, 2 * BLK), 0)
    j = lax.broadcasted_iota(jnp.int32, (BLK, 2 * BLK), 1)
    rel = BLK + i - j
    large = jnp.full_like(rel, REL_EXACT)
    for t in BUCKET_THRESHOLDS:
        large = large + (rel >= t).astype(jnp.int32)
    bucket = jnp.where(rel < REL_EXACT, rel, large)
    visible = jnp.logical_and(rel >= 0, rel < BLK)
    return bucket, visible


def _bias_build(rel_bias, name):
    def body(rb_ref, o_ref):
        bucket, visible = _band_tiles()

        def per_head(h, carry):
            acc = jnp.zeros((BLK, 2 * BLK), F32)
            for b in range(REL_BUCKETS):
                acc = jnp.where(bucket == b, rb_ref[b, h], acc)
            o_ref[h] = jnp.where(visible, acc, NEG)
            return carry

        lax.fori_loop(0, N_HEADS, per_head, 0)

    return _pcall(body, name=name, grid=(1,),
                  in_specs=[pl.BlockSpec(memory_space=pltpu.SMEM)],
                  out_specs=pl.BlockSpec((N_HEADS, BLK, 2 * BLK), lambda i: (0, 0, 0)),
                  out_shape=_sds((N_HEADS, BLK, 2 * BLK), F32))(rel_bias)


def _bias_bwd(dbias, name):
    def body(db_ref, o_ref):
        bucket, _ = _band_tiles()
        lane = lax.broadcasted_iota(jnp.int32, (N_HEADS, 128), 1)
        row = lax.broadcasted_iota(jnp.int32, (N_HEADS, 128), 0)

        def per_bucket(b, out):
            mb = (bucket == b).astype(F32)

            def per_head(h, out):
                s = jnp.sum(db_ref[h] * mb, keepdims=True)
                return jnp.where(jnp.logical_and(lane == b, row == h), s, out)

            return lax.fori_loop(0, N_HEADS, per_head, out)

        o_ref[...] = lax.fori_loop(0, REL_BUCKETS, per_bucket, jnp.zeros((N_HEADS, 128), F32))

    return _pcall(body, name=name, grid=(1,),
                  in_specs=[pl.BlockSpec((N_HEADS, BLK, 2 * BLK), lambda i: (0, 0, 0))],
                  out_specs=pl.BlockSpec((N_HEADS, 128), lambda i: (0, 0)),
                  out_shape=_sds((N_HEADS, 128), F32))(dbias)


def _kv_group(kp_ref, kc_ref, g):
    ks = slice(HEAD * g, HEAD * (g + 1))
    vs = slice(N_KV * HEAD + HEAD * g, N_KV * HEAD + HEAD * (g + 1))
    kg = jnp.concatenate([kp_ref[:, ks], kc_ref[:, ks]], axis=0)
    vg = jnp.concatenate([kp_ref[:, vs], kc_ref[:, vs]], axis=0)
    return kg, vg


def _swa_fwd(pb, biasm, sinks, name):
    _, s, _ = pb.shape
    nb = s // BLK
    kvw = 2 * N_KV * HEAD

    def body(q_ref, kc_ref, kp_ref, b_ref, sk_ref, o_ref, lse_ref):
        n = pl.program_id(0)
        col = lax.broadcasted_iota(jnp.int32, (BLK, 2 * BLK), 1)
        no_prev = jnp.logical_and(n == 0, col < BLK)
        lane = lax.broadcasted_iota(jnp.int32, (BLK, 128), 1)
        lse_t = jnp.zeros((BLK, 128), F32)
        for g in range(N_KV):
            kg, vg = _kv_group(kp_ref, kc_ref, g)
            for r in range(N_HEADS // N_KV):
                h = g * (N_HEADS // N_KV) + r
                hs = slice(HEAD * h, HEAD * (h + 1))
                sc = _dot_nt(q_ref[:, hs], kg) * (HEAD ** -0.5) + b_ref[h]
                sc = jnp.where(no_prev, NEG, sc)
                sk = sk_ref[0, h]
                m = jnp.maximum(jnp.max(sc, axis=1, keepdims=True), sk)
                p = jnp.exp(sc - m)
                l = jnp.sum(p, axis=1, keepdims=True) + jnp.exp(sk - m)
                o = _dot(p.astype(BF16), vg) * (1.0 / l)
                o_ref[:, hs] = o.astype(BF16)
                lse_t = jnp.where(lane == h, m + jnp.log(l), lse_t)
        lse_ref[...] = lse_t

    return _pcall(body, name=name, grid=(nb,),
                  in_specs=[pl.BlockSpec((None, BLK, D), lambda n: (0, n, 0)),
                            pl.BlockSpec((None, BLK, kvw), lambda n: (1, n, 0)),
                            pl.BlockSpec((None, BLK, kvw), lambda n: (1, jnp.maximum(n - 1, 0), 0)),
                            pl.BlockSpec((N_HEADS, BLK, 2 * BLK), lambda n: (0, 0, 0)),
                            pl.BlockSpec(memory_space=pltpu.SMEM)],
                  out_specs=[pl.BlockSpec((BLK, D), lambda n: (n, 0)), pl.BlockSpec((BLK, 128), lambda n: (n, 0))],
                  out_shape=[_sds((s, D), BF16), _sds((s, 128), F32)])(pb, pb, pb, biasm, sinks)


def _swa_bwd(pb, attn, dattn, lse, biasm, sinks, name):
    _, s, _ = pb.shape
    nb = s // BLK
    kvw = 2 * N_KV * HEAD
    grp = N_HEADS // N_KV

    def body(q_ref, kc_ref, kp_ref, o_ref, do_ref, lse_ref, b_ref, sk_ref, dpb_ref, dbias_ref, dsk_ref,
             dq_hold, kv_hold, dq_new, kv_prev, kv_cur):
        n = pl.program_id(0)

        @pl.when(n == 0)
        def _():
            dbias_ref[...] = jnp.zeros_like(dbias_ref)
            dsk_ref[...] = jnp.zeros_like(dsk_ref)
            dq_hold[...] = jnp.zeros_like(dq_hold)
            kv_hold[...] = jnp.zeros_like(kv_hold)

        @pl.when(n < nb)
        def _():
            col = lax.broadcasted_iota(jnp.int32, (BLK, 2 * BLK), 1)
            no_prev = jnp.logical_and(n == 0, col < BLK)
            lane = lax.broadcasted_iota(jnp.int32, (1, 128), 1)
            dsk = jnp.zeros((1, 128), F32)
            for g in range(N_KV):
                kg, vg = _kv_group(kp_ref, kc_ref, g)
                dk_g = jnp.zeros((2 * BLK, HEAD), F32)
                dv_g = jnp.zeros((2 * BLK, HEAD), F32)
                for r in range(grp):
                    h = g * grp + r
                    hs = slice(HEAD * h, HEAD * (h + 1))
                    qh = q_ref[:, hs]
                    dob = do_ref[:, hs]
                    lse_h = lse_ref[:, h:h + 1]
                    sc = _dot_nt(qh, kg) * (HEAD ** -0.5) + b_ref[h]
                    sc = jnp.where(no_prev, NEG, sc)
                    p = jnp.exp(sc - lse_h)
                    dp = _dot_nt(dob, vg)
                    delta = jnp.sum(dob.astype(F32) * o_ref[:, hs].astype(F32), axis=1, keepdims=True)
                    ds = p * (dp - delta)
                    dbias_ref[h] += ds
                    p_sink = jnp.exp(sk_ref[0, h] - lse_h)
                    dsk = jnp.where(lane == h, dsk - jnp.sum(p_sink * delta, keepdims=True), dsk)
                    dsb = (ds * (HEAD ** -0.5)).astype(BF16)
                    dq_new[:, hs] = _dot(dsb, kg)
                    dk_g = dk_g + _dot_tn(dsb, qh)
                    dv_g = dv_g + _dot_tn(p.astype(BF16), dob)
                ks = slice(HEAD * g, HEAD * (g + 1))
                vs = slice(N_KV * HEAD + HEAD * g, N_KV * HEAD + HEAD * (g + 1))
                kv_prev[:, ks] = dk_g[:BLK]
                kv_cur[:, ks] = dk_g[BLK:]
                kv_prev[:, vs] = dv_g[:BLK]
                kv_cur[:, vs] = dv_g[BLK:]
            dsk_ref[...] += dsk

        @pl.when(n == nb)
        def _():
            kv_prev[...] = jnp.zeros_like(kv_prev)

        dpb_ref[0] = dq_hold[...].astype(BF16)
        dpb_ref[1, :, 0:kvw] = (kv_hold[...] + kv_prev[...]).astype(BF16)
        dpb_ref[1, :, kvw:D] = jnp.zeros((BLK, D - kvw), BF16)

        @pl.when(n < nb)
        def _():
            dq_hold[...] = dq_new[...]
            kv_hold[...] = kv_cur[...]

    def cur(n):
        return jnp.minimum(n, nb - 1)

    return _pcall(body, name=name, grid=(nb + 1,),
                  in_specs=[pl.BlockSpec((None, BLK, D), lambda n: (0, cur(n), 0)),
                            pl.BlockSpec((None, BLK, kvw), lambda n: (1, cur(n), 0)),
                            pl.BlockSpec((None, BLK, kvw), lambda n: (1, jnp.maximum(cur(n) - 1, 0), 0)),
                            pl.BlockSpec((BLK, D), lambda n: (cur(n), 0)),
                            pl.BlockSpec((BLK, D), lambda n: (cur(n), 0)),
                            pl.BlockSpec((BLK, 128), lambda n: (cur(n), 0)),
                            pl.BlockSpec((N_HEADS, BLK, 2 * BLK), lambda n: (0, 0, 0)),
                            pl.BlockSpec(memory_space=pltpu.SMEM)],
                  out_specs=[pl.BlockSpec((2, BLK, D), lambda n: (0, jnp.maximum(n - 1, 0), 0)),
                             pl.BlockSpec((N_HEADS, BLK, 2 * BLK), lambda n: (0, 0, 0)),
                             pl.BlockSpec((1, 128), lambda n: (0, 0))],
                  out_shape=[_sds((2, s, D), BF16), _sds((N_HEADS, BLK, 2 * BLK), F32), _sds((1, 128), F32)],
                  scratch=[pltpu.VMEM((BLK, D), F32), pltpu.VMEM((BLK, kvw), F32), pltpu.VMEM((BLK, D), F32),
                           pltpu.VMEM((BLK, kvw), F32), pltpu.VMEM((BLK, kvw), F32)],
                  )(pb, pb, pb, attn, dattn, lse, biasm, sinks)


HALO = 16
CW = 512


def _conv_taps(cu, halo_cu, first_tile):
    row = lax.broadcasted_iota(jnp.int32, cu.shape, 0)
    halo_cu = jnp.where(first_tile, 0.0, halo_cu)
    c1 = jnp.where(row == 0, halo_cu[HALO - 1:HALO], pltpu.roll(cu, 1, 0))
    c2 = jnp.where(row == 0, halo_cu[HALO - 2:HALO - 1],
                   jnp.where(row == 1, halo_cu[HALO - 1:HALO], pltpu.roll(cu, 2, 0)))
    return c1, c2


def _conv_merge_fwd(pa, attn, convw, name, ts=512):
    _, s, _ = pa.shape
    ts = min(ts, s)
    hb = ts // HALO

    def body(pa_ref, hp_ref, at_ref, w_ref, o_ref):
        i = pl.program_id(1)
        cu = pa_ref[0].astype(F32) * pa_ref[2].astype(F32)
        c1, c2 = _conv_taps(cu, hp_ref[0].astype(F32) * hp_ref[2].astype(F32), i == 0)
        w = w_ref[...]
        c3 = w[0:1] * c2 + w[1:2] * c1 + w[2:3] * cu
        conv = pa_ref[1].astype(F32) * c3
        o_ref[...] = (jax.nn.sigmoid(pa_ref[3].astype(F32)) * at_ref[...].astype(F32)
                      + jax.nn.sigmoid(pa_ref[4].astype(F32)) * conv).astype(BF16)

    return _pcall(body, name=name, grid=(D // CW, s // ts),
                  in_specs=[pl.BlockSpec((5, ts, CW), lambda c, i: (0, i, c)),
                            pl.BlockSpec((5, HALO, CW), lambda c, i: (0, jnp.maximum(i * hb - 1, 0), c)),
                            pl.BlockSpec((ts, CW), lambda c, i: (i, c)),
                            pl.BlockSpec((8, CW), lambda c, i: (0, c))],
                  out_specs=pl.BlockSpec((ts, CW), lambda c, i: (i, c)),
                  out_shape=_sds((s, D), BF16))(pa, pa, attn, convw)


def _conv_merge_bwd(dmerged, pa, attn, convw, name, ts=512):
    _, s, _ = pa.shape
    ts = min(ts, s)
    hb = ts // HALO
    last_hb = s // HALO - 1

    def body(dm_ref, pa_ref, at_ref, w_ref, hp_ref, hn_ref, dmn_ref, dat_ref, dpa_ref, dw_ref):
        i = pl.program_id(1)
        last = i == pl.num_programs(1) - 1
        dm = dm_ref[...].astype(F32)
        cp, bp, u = pa_ref[0].astype(F32), pa_ref[1].astype(F32), pa_ref[2].astype(F32)
        sa = jax.nn.sigmoid(pa_ref[3].astype(F32))
        sc = jax.nn.sigmoid(pa_ref[4].astype(F32))
        at = at_ref[...].astype(F32)
        cu = cp * u
        c1, c2 = _conv_taps(cu, hp_ref[0].astype(F32) * hp_ref[2].astype(F32), i == 0)
        w = w_ref[...]
        c3 = w[0:1] * c2 + w[1:2] * c1 + w[2:3] * cu
        dconv = dm * sc
        dc3 = dconv * bp
        nxt = dmn_ref[...].astype(F32) * jax.nn.sigmoid(hn_ref[4].astype(F32)) * hn_ref[1].astype(F32)
        nxt = jnp.where(last, 0.0, nxt)
        row = lax.broadcasted_iota(jnp.int32, dc3.shape, 0)
        d1 = jnp.where(row == ts - 1, nxt[0:1], pltpu.roll(dc3, ts - 1, 0))
        d2 = jnp.where(row == ts - 2, nxt[0:1], jnp.where(row == ts - 1, nxt[1:2], pltpu.roll(dc3, ts - 2, 0)))
        dcu = w[2:3] * dc3 + w[1:2] * d1 + w[0:1] * d2
        dat_ref[...] = (dm * sa).astype(BF16)
        dpa_ref[0] = (dcu * u).astype(BF16)
        dpa_ref[1] = (dconv * c3).astype(BF16)
        dpa_ref[2] = (dcu * cp).astype(BF16)
        dpa_ref[3] = (dm * at * sa * (1.0 - sa)).astype(BF16)
        dpa_ref[4] = (dm * bp * c3 * sc * (1.0 - sc)).astype(BF16)

        @pl.when(i == 0)
        def _():
            dw_ref[...] = jnp.zeros_like(dw_ref)

        dw_ref[0:1, :] += jnp.sum(dc3 * c2, axis=0, keepdims=True)
        dw_ref[1:2, :] += jnp.sum(dc3 * c1, axis=0, keepdims=True)
        dw_ref[2:3, :] += jnp.sum(dc3 * cu, axis=0, keepdims=True)

    return _pcall(body, name=name, grid=(D // CW, s // ts),
                  in_specs=[pl.BlockSpec((ts, CW), lambda c, i: (i, c)),
                            pl.BlockSpec((5, ts, CW), lambda c, i: (0, i, c)),
                            pl.BlockSpec((ts, CW), lambda c, i: (i, c)),
                            pl.BlockSpec((8, CW), lambda c, i: (0, c)),
                            pl.BlockSpec((5, HALO, CW), lambda c, i: (0, jnp.maximum(i * hb - 1, 0), c)),
                            pl.BlockSpec((5, HALO, CW), lambda c, i: (0, jnp.minimum((i + 1) * hb, last_hb), c)),
                            pl.BlockSpec((HALO, CW), lambda c, i: (jnp.minimum((i + 1) * hb, last_hb), c))],
                  out_specs=[pl.BlockSpec((ts, CW), lambda c, i: (i, c)),
                             pl.BlockSpec((5, ts, CW), lambda c, i: (0, i, c)),
                             pl.BlockSpec((8, CW), lambda c, i: (0, c))],
                  out_shape=[_sds((s, D), BF16), _sds((5, s, D), BF16), _sds((8, D), F32)],
                  )(dmerged, pa, attn, convw, pa, pa, dmerged)


def _xattn_fwd(q, kv, name, tq=512):
    s, _ = q.shape
    nm = kv.shape[0]
    tq = min(tq, s)

    def body(q_ref, kv_ref, o_ref, lse_ref):
        lane = lax.broadcasted_iota(jnp.int32, (tq, 128), 1)
        lse_t = jnp.zeros((tq, 128), F32)
        for h in range(XH):
            hs = slice(XHD * h, XHD * (h + 1))
            vs = slice(D + XHD * h, D + XHD * (h + 1))
            sc = _dot_nt(q_ref[:, hs], kv_ref[:, hs]) * (XHD ** -0.5)
            m = jnp.max(sc, axis=1, keepdims=True)
            p = jnp.exp(sc - m)
            l = jnp.sum(p, axis=1, keepdims=True)
            o_ref[:, hs] = (_dot(p.astype(BF16), kv_ref[:, vs]) * (1.0 / l)).astype(BF16)
            lse_t = jnp.where(lane == h, m + jnp.log(l), lse_t)
        lse_ref[...] = lse_t

    return _pcall(body, name=name, grid=(s // tq,),
                  in_specs=[pl.BlockSpec((tq, D), lambda i: (i, 0)), pl.BlockSpec((nm, 2 * D), lambda i: (0, 0))],
                  out_specs=[pl.BlockSpec((tq, D), lambda i: (i, 0)), pl.BlockSpec((tq, 128), lambda i: (i, 0))],
                  out_shape=[_sds((s, D), BF16), _sds((s, 128), F32)])(q, kv)


def _xattn_bwd(q, kv, o, do, lse, name, tq=512):
    s, _ = q.shape
    nm = kv.shape[0]
    tq = min(tq, s)

    def body(q_ref, kv_ref, o_ref, do_ref, lse_ref, dq_ref, dkv_ref):
        @pl.when(pl.program_id(0) == 0)
        def _():
            dkv_ref[...] = jnp.zeros_like(dkv_ref)

        for h in range(XH):
            hs = slice(XHD * h, XHD * (h + 1))
            vs = slice(D + XHD * h, D + XHD * (h + 1))
            qh, kh, vh, dob = q_ref[:, hs], kv_ref[:, hs], kv_ref[:, vs], do_ref[:, hs]
            p = jnp.exp(_dot_nt(qh, kh) * (XHD ** -0.5) - lse_ref[:, h:h + 1])
            dp = _dot_nt(dob, vh)
            delta = jnp.sum(dob.astype(F32) * o_ref[:, hs].astype(F32), axis=1, keepdims=True)
            dsb = (p * (dp - delta) * (XHD ** -0.5)).astype(BF16)
            dq_ref[:, hs] = _dot(dsb, kh).astype(BF16)
            dkv_ref[:, hs] += _dot_tn(dsb, qh)
            dkv_ref[:, vs] += _dot_tn(p.astype(BF16), dob)

    return _pcall(body, name=name, grid=(s // tq,),
                  in_specs=[pl.BlockSpec((tq, D), lambda i: (i, 0)), pl.BlockSpec((nm, 2 * D), lambda i: (0, 0)),
                            pl.BlockSpec((tq, D), lambda i: (i, 0)), pl.BlockSpec((tq, D), lambda i: (i, 0)),
                            pl.BlockSpec((tq, 128), lambda i: (i, 0))],
                  out_specs=[pl.BlockSpec((tq, D), lambda i: (i, 0)), pl.BlockSpec((nm, 2 * D), lambda i: (0, 0))],
                  out_shape=[_sds((s, D), BF16), _sds((nm, 2 * D), F32)])(q, kv, o, do, lse)


def _ffn_down_bwd(dxb, wd4, gu4, name, tm=512):
    s, _ = dxb.shape
    tm = min(tm, s)

    def body(dx_ref, w_ref, gu_ref, o_ref):
        da = 0.5 * _dot_nt(dx_ref[...], w_ref[...])
        g = gu_ref[0].astype(F32)
        u = gu_ref[1].astype(F32)
        sg = jax.nn.sigmoid(g)
        o_ref[0] = (da * u * sg * (1.0 + g * (1.0 - sg))).astype(BF16)
        o_ref[1] = (da * g * sg).astype(BF16)

    return _pcall(body, name=name, grid=(4, s // tm),
                  in_specs=[pl.BlockSpec((tm, D), lambda p, i: (i, 0)),
                            pl.BlockSpec((None, FS, D), lambda p, i: (p, 0, 0)),
                            pl.BlockSpec((2, None, tm, FS), lambda p, i: (0, p, i, 0))],
                  out_specs=pl.BlockSpec((2, None, tm, FS), lambda p, i: (0, p, i, 0)),
                  out_shape=_sds((2, 4, s, FS), BF16))(dxb, wd4, gu4)


def _mm_tn(a, b, name, scale=1.0, tk=1024, tn=None):
    pa_n, s, m = a.shape
    pb_n, _, n = b.shape
    po = max(pa_n, pb_n)
    tk = min(tk, s)
    tn = n if tn is None else tn
    nk = s // tk

    def body(a_ref, b_ref, o_ref, acc_ref):
        k = pl.program_id(2)

        @pl.when(k == 0)
        def _():
            acc_ref[...] = jnp.zeros_like(acc_ref)

        acc_ref[...] += _dot_tn(a_ref[...], b_ref[...])

        @pl.when(k == nk - 1)
        def _():
            o_ref[...] = (scale * acc_ref[...]).astype(BF16)

    return _pcall(body, name=name, grid=(po, n // tn, nk),
                  in_specs=[pl.BlockSpec((None, tk, m), lambda o, j, k: (o if pa_n > 1 else 0, k, 0)),
                            pl.BlockSpec((None, tk, tn), lambda o, j, k: (o if pb_n > 1 else 0, k, j))],
                  out_specs=pl.BlockSpec((None, m, tn), lambda o, j, k: (o, 0, j)),
                  out_shape=_sds((po, m, n), BF16), scratch=[pltpu.VMEM((m, tn), F32)])(a, b)


def _mm_nt(a, b, name, out_dtype, tm=512):
    nj, s, k = a.shape
    n = b.shape[1]
    tm = min(tm, s)

    def body(a_ref, b_ref, o_ref, acc_ref):
        j = pl.program_id(1)

        @pl.when(j == 0)
        def _():
            acc_ref[...] = jnp.zeros_like(acc_ref)

        acc_ref[...] += _dot_nt(a_ref[...], b_ref[...])

        @pl.when(j == nj - 1)
        def _():
            o_ref[...] = acc_ref[...].astype(out_dtype)

    return _pcall(body, name=name, grid=(s // tm, nj),
                  in_specs=[pl.BlockSpec((None, tm, k), lambda i, j: (j, i, 0)),
                            pl.BlockSpec((None, n, k), lambda i, j: (j, 0, 0))],
                  out_specs=pl.BlockSpec((tm, n), lambda i, j: (i, 0)),
                  out_shape=_sds((s, n), out_dtype), scratch=[pltpu.VMEM((tm, n), F32)])(a, b)


def _mm_nt_rms_bwd(a, b, name, *, x, gain, dres, addend=None, tm=512):
    nj, s, k = a.shape
    n = b.shape[1]
    tm = min(tm, s)
    has_add = addend is not None

    def body(*refs):
        a_ref, b_ref, x_ref, g_ref, r_ref = refs[:5]
        add_ref = refs[5] if has_add else None
        dx_ref, dxb_ref, dg_ref, acc_ref = refs[5 + has_add:]
        i, j = pl.program_id(0), pl.program_id(1)

        @pl.when(j == 0)
        def _():
            acc_ref[...] = jnp.zeros_like(acc_ref)

        @pl.when(jnp.logical_and(i == 0, j == 0))
        def _():
            dg_ref[...] = jnp.zeros_like(dg_ref)

        acc_ref[...] += _dot_nt(a_ref[...], b_ref[...])

        @pl.when(j == nj - 1)
        def _():
            dh = acc_ref[...]
            if has_add:
                dh = dh + add_ref[...]
            xv = x_ref[...]
            r = lax.rsqrt(jnp.mean(xv * xv, axis=-1, keepdims=True) + EPS)
            xh = xv * r
            dyg = dh * g_ref[...]
            dx = r_ref[...] + r * (dyg - xh * jnp.mean(dyg * xh, axis=-1, keepdims=True))
            dx_ref[...] = dx
            dxb_ref[...] = dx.astype(BF16)
            dg_ref[...] += jnp.sum(dh * xh, axis=0, keepdims=True)

    row = pl.BlockSpec((tm, n), lambda i, j: (i, 0))
    in_specs = [pl.BlockSpec((None, tm, k), lambda i, j: (j, i, 0)),
                pl.BlockSpec((None, n, k), lambda i, j: (j, 0, 0)),
                row, pl.BlockSpec((1, n), lambda i, j: (0, 0)), row] + ([row] if has_add else [])
    args = (a, b, x, gain, dres) + ((addend,) if has_add else ())
    return _pcall(body, name=name, grid=(s // tm, nj), in_specs=in_specs,
                  out_specs=[row, row, pl.BlockSpec((1, n), lambda i, j: (0, 0))],
                  out_shape=[_sds((s, n), F32), _sds((s, n), BF16), _sds((1, n), F32)],
                  scratch=[pltpu.VMEM((tm, n), F32)])(*args)


def _adam(w, g, m, v):
    m2 = ADAM_B1 * m + (1.0 - ADAM_B1) * g
    v2 = ADAM_B2 * v + (1.0 - ADAM_B2) * (g * g)
    m_hat = m2 / (1.0 - ADAM_B1 ** ADAM_STEP)
    v_hat = v2 / (1.0 - ADAM_B2 ** ADAM_STEP)
    delta = -ADAM_LR * (m_hat / (jnp.sqrt(v_hat) + ADAM_EPS) + ADAM_WD * w)
    return delta, m2, v2


def _adamw(parts, w, m, v, name):
    _, r, c = parts.shape
    tr = r if r <= 256 else (176 if r == 352 else 256)

    def body(p_ref, w_ref, m_ref, v_ref, g_ref, d_ref, m2_ref, v2_ref):
        g = p_ref[0].astype(F32)
        for i in range(1, N_DEV):
            g = g + p_ref[i].astype(F32)
        delta, m2, v2 = _adam(w_ref[...], g, m_ref[...], v_ref[...])
        g_ref[...] = g
        d_ref[...] = delta
        m2_ref[...] = m2
        v2_ref[...] = v2

    blk = pl.BlockSpec((tr, c), lambda i: (i, 0))
    return _pcall(body, name=name, grid=(r // tr,),
                  in_specs=[pl.BlockSpec((N_DEV, tr, c), lambda i: (0, i, 0)), blk, blk, blk],
                  out_specs=[blk] * 4, out_shape=[_sds((r, c), F32)] * 4)(parts, w, m, v)


def _position():
    return lax.axis_index("x"), lax.axis_index("y"), lax.axis_index("c")


def _slot(px, py, pc):
    return 4 * px + 2 * py + pc


HBM_SPEC = pl.BlockSpec(memory_space=pl.ANY)


def _all_gather(shards, name):
    na = len(shards)

    def body(*refs):
        ins, outs = refs[:na], refs[na:2 * na]
        send_sems, recv_sems, local_sems = refs[2 * na:]
        x, y, c = _position()
        me, sibling = (x, y, c), (x, y, 1 - c)
        chips = [(1 - x, y), (x, 1 - y), (1 - x, 1 - y)]

        def copy(a, k, block, to, src=None):
            rows = outs[a].at[_slot(*block)]
            return pltpu.make_async_remote_copy(src_ref=rows if src is None else src, dst_ref=rows,
                                                send_sem=send_sems.at[k, a], recv_sem=recv_sems.at[k, a],
                                                device_id=to, device_id_type=MESH)

        mine = [pltpu.make_async_copy(ins[a], outs[a].at[_slot(*me)], local_sems.at[a]) for a in range(na)]
        for cp in mine:
            cp.start()
        first = [copy(a, 0, me, sibling, src=ins[a]) for a in range(na)]
        for j, chip in enumerate(chips):
            first += [copy(a, 1 + j, me, (*chip, c), src=ins[a]) for a in range(na)]
        for cp in first:
            cp.start()
        passed = []
        for j, chip in enumerate(chips):
            for a in range(na):
                copy(a, 1 + j, (*chip, c), me).wait_recv()
                fwd = copy(a, 4 + j, (*chip, c), sibling)
                fwd.start()
                passed.append(fwd)
        for a in range(na):
            copy(a, 0, sibling, me).wait_recv()
        for j, chip in enumerate(chips):
            for a in range(na):
                copy(a, 4 + j, (*chip, 1 - c), me).wait_recv()
        for cp in first + passed:
            cp.wait_send()
        for cp in mine:
            cp.wait()

    return pl.pallas_call(
        body, name=name,
        in_specs=[HBM_SPEC] * na, out_specs=[HBM_SPEC] * na,
        out_shape=[_sds((N_DEV,) + a.shape, a.dtype) for a in shards],
        scratch_shapes=[pltpu.SemaphoreType.DMA((7, na)), pltpu.SemaphoreType.DMA((7, na)),
                        pltpu.SemaphoreType.DMA((na,))],
    )(*shards)


def _exchange(scattered, replicated, name):
    arrays = list(scattered) + list(replicated)
    na, ns = len(arrays), len(scattered)

    def body(*refs):
        ins, outs = refs[:na], refs[na:2 * na]
        send_sems, recv_sems, local_sems = refs[2 * na:]
        me = _slot(*_position())

        def source(a, j):
            return ins[a].at[j] if a < ns else ins[a]

        def copy(a, j, i):
            return pltpu.make_async_remote_copy(src_ref=source(a, j), dst_ref=outs[a].at[i],
                                                send_sem=send_sems.at[j, a], recv_sem=recv_sems.at[i, a],
                                                device_id=(j >> 2, (j >> 1) & 1, j & 1), device_id_type=MESH)

        mine = [pltpu.make_async_copy(source(a, me), outs[a].at[me], local_sems.at[a]) for a in range(na)]
        for cp in mine:
            cp.start()
        for j in range(N_DEV):
            @pl.when(me != j)
            def _():
                for a in range(na):
                    copy(a, j, me).start()
        for i in range(N_DEV):
            @pl.when(me != i)
            def _():
                for a in range(na):
                    copy(a, i, i).wait_recv()
        for j in range(N_DEV):
            @pl.when(me != j)
            def _():
                for a in range(na):
                    copy(a, j, me).wait_send()
        for cp in mine:
            cp.wait()

    return pl.pallas_call(
        body, name=name,
        in_specs=[HBM_SPEC] * na, out_specs=[HBM_SPEC] * na,
        out_shape=[_sds((N_DEV,) + a.shape[-2:], a.dtype) for a in arrays],
        scratch_shapes=[pltpu.SemaphoreType.DMA((N_DEV, na)), pltpu.SemaphoreType.DMA((N_DEV, na)),
                        pltpu.SemaphoreType.DMA((na,))],
    )(*arrays)


def _forward_backward(x, mem, target, g, rel_bias, sinks, convw, w):
    s = x.shape[0]
    h1 = _rmsnorm(x, g["ffn1"], "norm_ffn1")
    gu1, a1 = _ffn_up(h1, w["gu1"].reshape(2, 4, D, FS), "ffn1_up")
    x1, h2 = _mm_res_norm(a1, w["d1"], x, g["mix"], 0.5, "ffn1_down")
    pa = _mm_nn(h2, w["wa"], "in_proj_a")
    pb = _mm_nn(h2, w["wb"], "in_proj_b")
    biasm = _bias_build(rel_bias, "bias_build")
    attn, lse = _swa_fwd(pb, biasm, sinks, "swa_fwd")
    merged = _conv_merge_fwd(pa, attn, convw, "conv_merge_fwd")
    x2, h3 = _mm_res_norm(merged[None], w["out"][None], x1, g["xattn"], 1.0, "out_proj")
    q2 = _mm_nn(h3, w["q"][None], "xattn_q")[0]
    mh = _rmsnorm(mem, g["mem"], "norm_mem")
    kv2 = _mm_nn(mh, w["kv"][None], "xattn_kv")[0]
    o, lse2 = _xattn_fwd(q2, kv2, "xattn_fwd")
    x3, h4 = _mm_res_norm(o[None], w["o"][None], x2, g["ffn2"], 1.0, "xattn_o")
    gu2, a2 = _ffn_up(h4, w["gu2"].reshape(2, 4, D, FS), "ffn2_up")
    dx4, dx4b, loss, d_final = _ffn_down_loss(a2, w["d2"], x3, g["final"], target, "ffn2_down_loss")
    dw_d2 = _mm_tn(a2, dx4b[None], "dw_ffn2_down", scale=0.5)
    dgu2 = _ffn_down_bwd(dx4b, w["d2"], gu2, "ffn2_down_bwd").reshape(8, s, FS)
    dw_gu2 = _mm_tn(h4[None], dgu2, "dw_ffn2_up")
    dx3, dx3b, d_ffn2 = _mm_nt_rms_bwd(dgu2, w["gu2"], "ffn2_up_bwd", x=x3, gain=g["ffn2"], dres=dx4)
    do = _mm_nt(dx3b[None], w["o"][None], "xattn_o_bwd", BF16)
    dw_o = _mm_tn(o[None], dx3b[None], "dw_xattn_o")
    dq2, dkv2 = _xattn_bwd(q2, kv2, o, do, lse2, "xattn_bwd")
    dkv2b = dkv2.astype(BF16)
    dw_q = _mm_tn(h3[None], dq2[None], "dw_xattn_q")
    dx2, dx2b, d_xattn = _mm_nt_rms_bwd(dq2[None], w["q"][None], "xattn_q_bwd", x=x2, gain=g["xattn"], dres=dx3)
    dw_kv = _mm_tn(mh[None], dkv2b[None], "dw_xattn_kv")
    _, _, d_mem = _mm_nt_rms_bwd(dkv2b[None], w["kv"][None], "xattn_kv_bwd", x=mem, gain=g["mem"],
                                 dres=jnp.zeros_like(mem))
    dmerged = _mm_nt(dx2b[None], w["out"][None], "out_proj_bwd", BF16)
    dw_out = _mm_tn(merged[None], dx2b[None], "dw_out_proj")
    dattn, dpa, d_convw = _conv_merge_bwd(dmerged, pa, attn, convw, "conv_merge_bwd")
    dpb, dbias, d_sinks = _swa_bwd(pb, attn, dattn, lse, biasm, sinks, "swa_bwd")
    d_relb = _bias_bwd(dbias, "bias_bwd")
    dw_a = _mm_tn(h2[None], dpa, "dw_in_proj_a")
    dw_b = _mm_tn(h2[None], dpb, "dw_in_proj_b")
    dh2_b = _mm_nt(dpb, w["wb"], "in_proj_b_bwd", F32)
    dx1, dx1b, d_mix = _mm_nt_rms_bwd(dpa, w["wa"], "in_proj_a_bwd", x=x1, gain=g["mix"], dres=dx2, addend=dh2_b)
    dw_d1 = _mm_tn(a1, dx1b[None], "dw_ffn1_down", scale=0.5)
    dgu1 = _ffn_down_bwd(dx1b, w["d1"], gu1, "ffn1_down_bwd").reshape(8, s, FS)
    dw_gu1 = _mm_tn(h1[None], dgu1, "dw_ffn1_up")
    dx0, _, d_ffn1 = _mm_nt_rms_bwd(dgu1, w["gu1"], "ffn1_up_bwd", x=x, gain=g["ffn1"], dres=dx1)

    dws = dict(gu1=dw_gu1, d1=dw_d1, wa=dw_a, wb=dw_b, out=dw_out[0], q=dw_q[0], kv=dw_kv[0], o=dw_o[0],
               gu2=dw_gu2, d2=dw_d2)
    relb_row = jnp.concatenate([d_relb[:, :REL_BUCKETS].T.reshape(1, REL_BUCKETS * N_HEADS), d_sinks[:, :N_HEADS],
                                jnp.zeros((1, D - REL_BUCKETS * N_HEADS - N_HEADS), F32)], axis=1)
    loss_row = jnp.concatenate([loss[0:1, 0:1], jnp.zeros((1, D - 1), F32)], axis=1)
    small = jnp.concatenate([d_ffn1, d_mix, d_xattn, d_mem, d_ffn2, d_final, relb_row, loss_row, d_convw[0:3],
                             jnp.zeros((SMALL_ROWS - ROW_CONV - 3, D), F32)], axis=0)
    return dx0, dws, small


def _pack_small(norms, final, relb, sinks, conv_local, me):
    relb_row = jnp.concatenate([relb.reshape(1, -1), sinks.reshape(1, -1),
                                jnp.zeros((1, D - REL_BUCKETS * N_HEADS - N_HEADS), F32)], axis=1)
    conv_rows = lax.dynamic_update_slice(jnp.zeros((3, D), F32), conv_local.reshape(3, -1), (0, 128 * me))
    return jnp.concatenate(list(norms) + [final.reshape(1, D), relb_row, jnp.zeros((1, D), F32), conv_rows,
                                          jnp.zeros((SMALL_ROWS - ROW_CONV - 3, D), F32)], axis=0)


def kernel(x, mem, positions, rel_bias, ffn1_norm, ffn1_w_gu, ffn1_w_down, mix_norm, w_in, sinks, conv_w, w_out, xattn_norm, mem_norm, xattn_wq, xattn_wkv, xattn_wo, ffn2_norm, ffn2_w_gu, ffn2_w_down, final_norm, loss_target, m_rel_bias, m_ffn1_norm, m_ffn1_w_gu, m_ffn1_w_down, m_mix_norm, m_w_in, m_sinks, m_conv_w, m_w_out, m_xattn_norm, m_mem_norm, m_xattn_wq, m_xattn_wkv, m_xattn_wo, m_ffn2_norm, m_ffn2_w_gu, m_ffn2_w_down, m_final_norm, v_rel_bias, v_ffn1_norm, v_ffn1_w_gu, v_ffn1_w_down, v_mix_norm, v_w_in, v_sinks, v_conv_w, v_w_out, v_xattn_norm, v_mem_norm, v_xattn_wq, v_xattn_wkv, v_xattn_wo, v_ffn2_norm, v_ffn2_w_gu, v_ffn2_w_down, v_final_norm):
    del positions
    me = _slot(*_position())
    big = dict(gu1=(ffn1_w_gu, m_ffn1_w_gu, v_ffn1_w_gu), d1=(ffn1_w_down, m_ffn1_w_down, v_ffn1_w_down),
               win=(w_in, m_w_in, v_w_in), out=(w_out, m_w_out, v_w_out), q=(xattn_wq, m_xattn_wq, v_xattn_wq),
               kv=(xattn_wkv, m_xattn_wkv, v_xattn_wkv), o=(xattn_wo, m_xattn_wo, v_xattn_wo),
               gu2=(ffn2_w_gu, m_ffn2_w_gu, v_ffn2_w_gu), d2=(ffn2_w_down, m_ffn2_w_down, v_ffn2_w_down))
    order = list(big)
    conv_tile = jnp.concatenate([conv_w[0], jnp.zeros((5, 128), F32)], axis=0)
    gathered = _all_gather([big[k][0][0].astype(BF16) for k in order] + [conv_tile], "all_gather_weights")
    gw = dict(zip(order, gathered[:-1]))
    convw = jnp.concatenate([gathered[-1][:, :3, :].transpose(1, 0, 2).reshape(3, D), jnp.zeros((5, D), F32)], axis=0)
    w_in_full = gw["win"].transpose(1, 0, 2).reshape(D, -1)
    nq, nkv = N_HEADS * HEAD, 2 * N_KV * HEAD
    w = dict(gu1=gw["gu1"], gu2=gw["gu2"], d1=gw["d1"].reshape(4, FS, D), d2=gw["d2"].reshape(4, FS, D),
             wa=w_in_full[:, nq + nkv:].reshape(D, 5, D).transpose(1, 0, 2),
             wb=jnp.stack([w_in_full[:, :nq], jnp.pad(w_in_full[:, nq:nq + nkv], ((0, 0), (0, D - nkv)))]),
             out=gw["out"].reshape(D, D), q=gw["q"].reshape(D, D), o=gw["o"].reshape(D, D),
             kv=gw["kv"].transpose(1, 0, 2).reshape(D, 2 * D))
    gains = dict(ffn1=ffn1_norm, mix=mix_norm, xattn=xattn_norm, mem=mem_norm, ffn2=ffn2_norm,
                 final=final_norm.reshape(1, D))
    dx, dws, small = _forward_backward(x[0], mem[0], loss_target[0], gains, rel_bias, sinks, convw, w)
    dw_in = jnp.concatenate([dws["wb"][0], dws["wb"][1][:, :nkv], dws["wa"].transpose(1, 0, 2).reshape(D, 5 * D)],
                            axis=1).reshape(D, N_DEV, -1).transpose(1, 0, 2)
    parts = dict(gu1=dws["gu1"], d1=dws["d1"].reshape(N_DEV, -1, D), win=dw_in,
                 out=dws["out"].reshape(N_DEV, -1, D), q=dws["q"].reshape(N_DEV, -1, D),
                 kv=dws["kv"].reshape(D, N_DEV, -1).transpose(1, 0, 2), o=dws["o"].reshape(N_DEV, -1, D),
                 gu2=dws["gu2"], d2=dws["d2"].reshape(N_DEV, -1, D))
    received = _exchange([parts[k] for k in order], [small], "exchange_gradients")
    big_out = {k: _adamw(r, *(t[0] for t in big[k]), "adamw_" + k) for k, r in zip(order, received[:-1])}
    packed = [_pack_small(norms, final, relb, sk, conv, me) for norms, final, relb, sk, conv in (
        ((ffn1_norm, mix_norm, xattn_norm, mem_norm, ffn2_norm), final_norm, rel_bias, sinks, conv_w),
        ((m_ffn1_norm, m_mix_norm, m_xattn_norm, m_mem_norm, m_ffn2_norm), m_final_norm, m_rel_bias, m_sinks, m_conv_w),
        ((v_ffn1_norm, v_mix_norm, v_xattn_norm, v_mem_norm, v_ffn2_norm), v_final_norm, v_rel_bias, v_sinks, v_conv_w))]
    small_out = _adamw(received[-1], *packed, "adamw_small")

    def unpack(t):
        conv = lax.dynamic_slice(t[ROW_CONV:ROW_CONV + 3], (0, 128 * me), (3, 128))[None]
        nrel = REL_BUCKETS * N_HEADS
        return dict(ffn1_norm=t[0:1], mix_norm=t[1:2], xattn_norm=t[2:3], mem_norm=t[3:4], ffn2_norm=t[4:5],
                    final_norm=t[5], rel_bias=t[ROW_RELB, :nrel].reshape(REL_BUCKETS, N_HEADS),
                    sinks=t[ROW_RELB:ROW_RELB + 1, nrel:nrel + N_HEADS], conv_w=conv)

    names = dict(gu1="ffn1_w_gu", d1="ffn1_w_down", win="w_in", out="w_out", q="xattn_wq", kv="xattn_wkv",
                 o="xattn_wo", gu2="ffn2_w_gu", d2="ffn2_w_down")
    results = []
    for idx in range(4):
        leaves = unpack(small_out[idx])
        leaves.update({names[k]: big_out[k][idx][None] for k in order})
        results.append(leaves)
    weights = ("rel_bias", "ffn1_norm", "ffn1_w_gu", "ffn1_w_down", "mix_norm", "w_in", "sinks", "conv_w", "w_out",
               "xattn_norm", "mem_norm", "xattn_wq", "xattn_wkv", "xattn_wo", "ffn2_norm", "ffn2_w_gu", "ffn2_w_down",
               "final_norm")
    loss = small_out[0][ROW_LOSS, 0]
    return (loss, dx[None], *[leaves[n] for leaves in results for n in weights])
```

```python
import math

import numpy as np
import jax
import jax.numpy as jnp
from jax import lax
from jax.experimental import pallas as pl
from jax.experimental.pallas import tpu as pltpu

F32, BF16 = jnp.float32, jnp.bfloat16
MESH = pl.DeviceIdType.MESH

D = 1024
N_DEV = 8
D_FF = 2816
FS = D_FF // 4
HEAD = 64
N_HEADS, N_KV = 16, 4
BLK = 128
XH, XHD = 4, 256
REL_BUCKETS, REL_EXACT, REL_MAX_DIST = 32, 16, 128
EPS, NEG = 1e-6, -1e30
ADAM_LR, ADAM_B1, ADAM_B2, ADAM_EPS, ADAM_WD, ADAM_STEP = 0.001, 0.9, 0.999, 1e-08, 0.01, 10
VMEM_LIMIT_V7X = 56 * 2**20
SMALL_ROWS = 16
ROW_RELB, ROW_LOSS, ROW_CONV = 6, 7, 8


def _bucket_thresholds():
    n = np.arange(REL_MAX_DIST)
    nf = np.maximum(n, 1).astype(np.float32)
    large = REL_EXACT + (np.log(nf / np.float32(REL_EXACT)) / np.float32(math.log(REL_MAX_DIST / REL_EXACT))
                         * np.float32(REL_BUCKETS - REL_EXACT)).astype(np.int32)
    b = np.where(n < REL_EXACT, n, np.minimum(large, REL_BUCKETS - 1))
    return [int(np.argmax(b >= REL_EXACT + k)) for k in range(1, REL_BUCKETS - REL_EXACT)]


BUCKET_THRESHOLDS = _bucket_thresholds()


HBM_SPEC = pl.BlockSpec(memory_space=pl.ANY)


class _Carry:
    def __init__(self, ins, outs, sems, start, finish, mid=None):
        self.ins, self.outs, self.sems = list(ins), list(outs), list(sems)
        self.start, self.finish, self.mid = start, finish, mid


def _pcall(body, *, name, grid, in_specs, out_specs, out_shape, scratch=(), carry=None):
    params = pltpu.CompilerParams(dimension_semantics=("arbitrary",) * len(grid), vmem_limit_bytes=VMEM_LIMIT_V7X)
    if carry is None:
        return pl.pallas_call(body, name=name, grid=grid, in_specs=in_specs, out_specs=out_specs,
                              out_shape=out_shape, scratch_shapes=list(scratch), compiler_params=params)
    single = not isinstance(out_shape, (list, tuple))
    own_specs, own_shapes = ([out_specs], [out_shape]) if single else (list(out_specs), list(out_shape))
    n_in, n_out, n_scr = len(in_specs), len(own_shapes), len(scratch)
    n_cin, n_cout = len(carry.ins), len(carry.outs)
    steps = math.prod(grid)

    def carrying(*refs):
        ins, refs = refs[:n_in], refs[n_in:]
        cins, refs = refs[:n_cin], refs[n_cin:]
        outs, refs = refs[:n_out], refs[n_out:]
        couts, refs = refs[:n_cout], refs[n_cout:]
        scr, csems = refs[:n_scr], refs[n_scr:]
        step = 0
        for axis, size in enumerate(grid):
            step = step * size + pl.program_id(axis)

        @pl.when(step == 0)
        def _():
            carry.start(cins, couts, csems)

        body(*ins, *outs, *scr)
        if carry.mid is not None:
            @pl.when(step == steps // 2)
            def _():
                carry.mid(cins, couts, csems)

        @pl.when(step == steps - 1)
        def _():
            carry.finish(cins, couts, csems)

    call = pl.pallas_call(carrying, name=name, grid=grid, in_specs=list(in_specs) + [HBM_SPEC] * n_cin,
                          out_specs=own_specs + [HBM_SPEC] * n_cout, out_shape=own_shapes + carry.outs,
                          scratch_shapes=list(scratch) + carry.sems, compiler_params=params)

    def run(*args):
        res = call(*args, *carry.ins)
        return (res[0] if single else res[:n_out]), res[n_out:]

    return run


def _run_alone(carry, name):
    n_cin, n_cout = len(carry.ins), len(carry.outs)

    def body(*refs):
        cins, couts, csems = refs[:n_cin], refs[n_cin:n_cin + n_cout], refs[n_cin + n_cout:]
        carry.start(cins, couts, csems)
        if carry.mid is not None:
            carry.mid(cins, couts, csems)
        carry.finish(cins, couts, csems)

    return pl.pallas_call(body, name=name, in_specs=[HBM_SPEC] * n_cin, out_specs=[HBM_SPEC] * n_cout,
                          out_shape=carry.outs, scratch_shapes=carry.sems)(*carry.ins)


def _dot(a, b):
    return jnp.dot(a, b, preferred_element_type=F32)


def _dot_nt(a, b):
    return lax.dot_general(a, b, (((1,), (1,)), ((), ())), preferred_element_type=F32)


def _dot_tn(a, b):
    return lax.dot_general(a, b, (((0,), (0,)), ((), ())), preferred_element_type=F32)


def _sds(shape, dtype):
    return jax.ShapeDtypeStruct(tuple(shape), dtype)


def _carried(call, args, carry):
    return call(*args) if carry is not None else (call(*args), ())


def _rmsnorm(x, g, name):
    m, d = x.shape
    tm = min(512, m)

    def body(x_ref, g_ref, h_ref):
        xv = x_ref[...]
        r = lax.rsqrt(jnp.mean(xv * xv, axis=-1, keepdims=True) + EPS)
        h_ref[...] = (xv * r * g_ref[...]).astype(BF16)

    return _pcall(body, name=name, grid=(m // tm,),
                  in_specs=[pl.BlockSpec((tm, d), lambda i: (i, 0)), pl.BlockSpec((1, d), lambda i: (0, 0))],
                  out_specs=pl.BlockSpec((tm, d), lambda i: (i, 0)), out_shape=_sds((m, d), BF16))(x, g)


def _mm_nn(a, b, name, tm=512, carry=None):
    m, k = a.shape
    nj, _, n = b.shape
    tm = min(tm, m)

    def body(a_ref, b_ref, o_ref):
        o_ref[...] = _dot(a_ref[...], b_ref[...]).astype(BF16)

    call = _pcall(body, name=name, grid=(nj, m // tm),
                  in_specs=[pl.BlockSpec((tm, k), lambda j, i: (i, 0)),
                            pl.BlockSpec((None, k, n), lambda j, i: (j, 0, 0))],
                  out_specs=pl.BlockSpec((None, tm, n), lambda j, i: (j, i, 0)),
                  out_shape=_sds((nj, m, n), BF16), carry=carry)
    return _carried(call, (a, b), carry)


def _ffn_up(h, w4, name, tm=512, carry=None):
    s, d = h.shape
    tm = min(tm, s)

    def body(h_ref, w_ref, gu_ref, a_ref):
        hv = h_ref[...]
        g = _dot(hv, w_ref[0])
        u = _dot(hv, w_ref[1])
        gu_ref[0] = g.astype(BF16)
        gu_ref[1] = u.astype(BF16)
        a_ref[...] = (g * jax.nn.sigmoid(g) * u).astype(BF16)

    call = _pcall(body, name=name, grid=(4, s // tm),
                  in_specs=[pl.BlockSpec((tm, d), lambda p, i: (i, 0)),
                            pl.BlockSpec((2, None, d, FS), lambda p, i: (0, p, 0, 0))],
                  out_specs=[pl.BlockSpec((2, None, tm, FS), lambda p, i: (0, p, i, 0)),
                             pl.BlockSpec((None, tm, FS), lambda p, i: (p, i, 0))],
                  out_shape=[_sds((2, 4, s, FS), BF16), _sds((4, s, FS), BF16)], carry=carry)
    return _carried(call, (h, w4), carry)


def _mm_res_norm(a, w, xres, gain, scale, name, tm=512, carry=None):
    npart, s, kp = a.shape
    tm = min(tm, s)

    def body(a_ref, w_ref, x_ref, g_ref, xo_ref, h_ref):
        acc = _dot(a_ref[0], w_ref[0])
        for p in range(1, npart):
            acc = acc + _dot(a_ref[p], w_ref[p])
        xn = x_ref[...] + scale * acc
        xo_ref[...] = xn
        r = lax.rsqrt(jnp.mean(xn * xn, axis=-1, keepdims=True) + EPS)
        h_ref[...] = (xn * r * g_ref[...]).astype(BF16)

    call = _pcall(body, name=name, grid=(s // tm,),
                  in_specs=[pl.BlockSpec((npart, tm, kp), lambda i: (0, i, 0)),
                            pl.BlockSpec((npart, kp, D), lambda i: (0, 0, 0)),
                            pl.BlockSpec((tm, D), lambda i: (i, 0)),
                            pl.BlockSpec((1, D), lambda i: (0, 0))],
                  out_specs=[pl.BlockSpec((tm, D), lambda i: (i, 0)), pl.BlockSpec((tm, D), lambda i: (i, 0))],
                  out_shape=[_sds((s, D), F32), _sds((s, D), BF16)], carry=carry)
    return _carried(call, (a, w, xres, gain), carry)


def _ffn_down_loss(a, w, xres, gain, target, name, tm=512):
    npart, s, kp = a.shape
    tm = min(tm, s)

    def body(a_ref, w_ref, x_ref, g_ref, t_ref, dx_ref, dxb_ref, loss_ref, dg_ref):
        i = pl.program_id(0)
        acc = _dot(a_ref[0], w_ref[0])
        for p in range(1, npart):
            acc = acc + _dot(a_ref[p], w_ref[p])
        xn = x_ref[...] + 0.5 * acc
        r = lax.rsqrt(jnp.mean(xn * xn, axis=-1, keepdims=True) + EPS)
        xh = xn * r
        gv = g_ref[...]
        err = xh * gv - t_ref[...]
        part = 0.5 * jnp.sum(jnp.mean(err * err, axis=-1, keepdims=True), axis=0, keepdims=True)
        dy = err * (1.0 / D)
        dyg = dy * gv
        dxn = r * (dyg - xh * jnp.mean(dyg * xh, axis=-1, keepdims=True))
        dx_ref[...] = dxn
        dxb_ref[...] = dxn.astype(BF16)

        @pl.when(i == 0)
        def _():
            loss_ref[...] = jnp.zeros_like(loss_ref)
            dg_ref[...] = jnp.zeros_like(dg_ref)

        loss_ref[...] += jnp.broadcast_to(part, loss_ref.shape)
        dg_ref[...] += jnp.sum(dy * xh, axis=0, keepdims=True)

    return _pcall(body, name=name, grid=(s // tm,),
                  in_specs=[pl.BlockSpec((npart, tm, kp), lambda i: (0, i, 0)),
                            pl.BlockSpec((npart, kp, D), lambda i: (0, 0, 0)),
                            pl.BlockSpec((tm, D), lambda i: (i, 0)),
                            pl.BlockSpec((1, D), lambda i: (0, 0)),
                            pl.BlockSpec((tm, D), lambda i: (i, 0))],
                  out_specs=[pl.BlockSpec((tm, D), lambda i: (i, 0)), pl.BlockSpec((tm, D), lambda i: (i, 0)),
                             pl.BlockSpec((8, 128), lambda i: (0, 0)), pl.BlockSpec((1, D), lambda i: (0, 0))],
                  out_shape=[_sds((s, D), F32), _sds((s, D), BF16), _sds((8, 128), F32), _sds((1, D), F32)],
                  )(a, w, xres, gain, target)


def _band_tiles():
    i = lax.broadcasted_iota(jnp.int32, (BLK, 2 * BLK), 0)
    j = lax.broadcasted_iota(jnp.int32, (BLK, 2 * BLK), 1)
    rel = BLK + i - j
    large = jnp.full_like(rel, REL_EXACT)
    for t in BUCKET_THRESHOLDS:
        large = large + (rel >= t).astype(jnp.int32)
    bucket = jnp.where(rel < REL_EXACT, rel, large)
    visible = jnp.logical_and(rel >= 0, rel < BLK)
    return bucket, visible


def _bias_build(rel_bias, name):
    def body(rb_ref, o_ref):
        bucket, visible = _band_tiles()

        def per_head(h, carry):
            acc = jnp.zeros((BLK, 2 * BLK), F32)
            for b in range(REL_BUCKETS):
                acc = jnp.where(bucket == b, rb_ref[b, h], acc)
            o_ref[h] = jnp.where(visible, acc, NEG)
            return carry

        lax.fori_loop(0, N_HEADS, per_head, 0)

    return _pcall(body, name=name, grid=(1,),
                  in_specs=[pl.BlockSpec(memory_space=pltpu.SMEM)],
                  out_specs=pl.BlockSpec((N_HEADS, BLK, 2 * BLK), lambda i: (0, 0, 0)),
                  out_shape=_sds((N_HEADS, BLK, 2 * BLK), F32))(rel_bias)


def _bias_bwd(dbias, name):
    def body(db_ref, o_ref):
        bucket, _ = _band_tiles()
        lane = lax.broadcasted_iota(jnp.int32, (N_HEADS, 128), 1)

        def per_bucket(b, out):
            mb = (bucket == b).astype(F32)
            per_col = jnp.sum(db_ref[...] * mb[None, :, :], axis=1)
            return jnp.where(lane == b, jnp.sum(per_col, axis=1, keepdims=True), out)

        o_ref[...] = lax.fori_loop(0, REL_BUCKETS, per_bucket, jnp.zeros((N_HEADS, 128), F32))

    return _pcall(body, name=name, grid=(1,),
                  in_specs=[pl.BlockSpec((N_HEADS, BLK, 2 * BLK), lambda i: (0, 0, 0))],
                  out_specs=pl.BlockSpec((N_HEADS, 128), lambda i: (0, 0)),
                  out_shape=_sds((N_HEADS, 128), F32))(dbias)


def _kv_group(kp_ref, kc_ref, g):
    ks = slice(HEAD * g, HEAD * (g + 1))
    vs = slice(N_KV * HEAD + HEAD * g, N_KV * HEAD + HEAD * (g + 1))
    kg = jnp.concatenate([kp_ref[:, ks], kc_ref[:, ks]], axis=0)
    vg = jnp.concatenate([kp_ref[:, vs], kc_ref[:, vs]], axis=0)
    return kg, vg


def _swa_fwd(pb, biasm, sinks, name, carry=None):
    _, s, _ = pb.shape
    nb = s // BLK
    kvw = 2 * N_KV * HEAD

    def body(q_ref, kc_ref, kp_ref, b_ref, sk_ref, o_ref, lse_ref):
        n = pl.program_id(0)
        col = lax.broadcasted_iota(jnp.int32, (BLK, 2 * BLK), 1)
        no_prev = jnp.logical_and(n == 0, col < BLK)
        lane = lax.broadcasted_iota(jnp.int32, (BLK, 128), 1)
        lse_t = jnp.zeros((BLK, 128), F32)
        for g in range(N_KV):
            kg, vg = _kv_group(kp_ref, kc_ref, g)
            for r in range(N_HEADS // N_KV):
                h = g * (N_HEADS // N_KV) + r
                hs = slice(HEAD * h, HEAD * (h + 1))
                sc = _dot_nt(q_ref[:, hs], kg) * (HEAD ** -0.5) + b_ref[h]
                sc = jnp.where(no_prev, NEG, sc)
                sk = sk_ref[0, h]
                m = jnp.maximum(jnp.max(sc, axis=1, keepdims=True), sk)
                p = jnp.exp(sc - m)
                l = jnp.sum(p, axis=1, keepdims=True) + jnp.exp(sk - m)
                o = _dot(p.astype(BF16), vg) * (1.0 / l)
                o_ref[:, hs] = o.astype(BF16)
                lse_t = jnp.where(lane == h, m + jnp.log(l), lse_t)
        lse_ref[...] = lse_t

    call = _pcall(body, name=name, grid=(nb,),
                  in_specs=[pl.BlockSpec((None, BLK, D), lambda n: (0, n, 0)),
                            pl.BlockSpec((None, BLK, kvw), lambda n: (1, n, 0)),
                            pl.BlockSpec((None, BLK, kvw), lambda n: (1, jnp.maximum(n - 1, 0), 0)),
                            pl.BlockSpec((N_HEADS, BLK, 2 * BLK), lambda n: (0, 0, 0)),
                            pl.BlockSpec(memory_space=pltpu.SMEM)],
                  out_specs=[pl.BlockSpec((BLK, D), lambda n: (n, 0)), pl.BlockSpec((BLK, 128), lambda n: (n, 0))],
                  out_shape=[_sds((s, D), BF16), _sds((s, 128), F32)], carry=carry)
    return _carried(call, (pb, pb, pb, biasm, sinks), carry)


def _swa_bwd(pb, attn, dattn, lse, biasm, sinks, name, carry=None):
    _, s, _ = pb.shape
    nb = s // BLK
    kvw = 2 * N_KV * HEAD
    grp = N_HEADS // N_KV

    def body(q_ref, kc_ref, kp_ref, o_ref, do_ref, lse_ref, b_ref, sk_ref, dpb_ref, dbias_ref, dsk_ref,
             dq_hold, kv_hold, dq_new, kv_prev, kv_cur):
        n = pl.program_id(0)

        @pl.when(n == 0)
        def _():
            dbias_ref[...] = jnp.zeros_like(dbias_ref)
            dsk_ref[...] = jnp.zeros_like(dsk_ref)
            dq_hold[...] = jnp.zeros_like(dq_hold)
            kv_hold[...] = jnp.zeros_like(kv_hold)

        @pl.when(n < nb)
        def _():
            col = lax.broadcasted_iota(jnp.int32, (BLK, 2 * BLK), 1)
            no_prev = jnp.logical_and(n == 0, col < BLK)
            lane = lax.broadcasted_iota(jnp.int32, (1, 128), 1)
            dsk = jnp.zeros((1, 128), F32)
            for g in range(N_KV):
                kg, vg = _kv_group(kp_ref, kc_ref, g)
                dk_g = jnp.zeros((2 * BLK, HEAD), F32)
                dv_g = jnp.zeros((2 * BLK, HEAD), F32)
                for r in range(grp):
                    h = g * grp + r
                    hs = slice(HEAD * h, HEAD * (h + 1))
                    qh = q_ref[:, hs]
                    dob = do_ref[:, hs]
                    lse_h = lse_ref[:, h:h + 1]
                    sc = _dot_nt(qh, kg) * (HEAD ** -0.5) + b_ref[h]
                    sc = jnp.where(no_prev, NEG, sc)
                    p = jnp.exp(sc - lse_h)
                    dp = _dot_nt(dob, vg)
                    delta = jnp.sum(dob.astype(F32) * o_ref[:, hs].astype(F32), axis=1, keepdims=True)
                    ds = p * (dp - delta)
                    dbias_ref[h] += ds
                    p_sink = jnp.exp(sk_ref[0, h] - lse_h)
                    dsk = jnp.where(lane == h, dsk - jnp.sum(p_sink * delta, keepdims=True), dsk)
                    dsb = (ds * (HEAD ** -0.5)).astype(BF16)
                    dq_new[:, hs] = _dot(dsb, kg)
                    dk_g = dk_g + _dot_tn(dsb, qh)
                    dv_g = dv_g + _dot_tn(p.astype(BF16), dob)
                ks = slice(HEAD * g, HEAD * (g + 1))
                vs = slice(N_KV * HEAD + HEAD * g, N_KV * HEAD + HEAD * (g + 1))
                kv_prev[:, ks] = dk_g[:BLK]
                kv_cur[:, ks] = dk_g[BLK:]
                kv_prev[:, vs] = dv_g[:BLK]
                kv_cur[:, vs] = dv_g[BLK:]
            dsk_ref[...] += dsk

        @pl.when(n == nb)
        def _():
            kv_prev[...] = jnp.zeros_like(kv_prev)

        dpb_ref[0] = dq_hold[...].astype(BF16)
        dpb_ref[1, :, 0:kvw] = (kv_hold[...] + kv_prev[...]).astype(BF16)
        dpb_ref[1, :, kvw:D] = jnp.zeros((BLK, D - kvw), BF16)

        @pl.when(n < nb)
        def _():
            dq_hold[...] = dq_new[...]
            kv_hold[...] = kv_cur[...]

    def cur(n):
        return jnp.minimum(n, nb - 1)

    call = _pcall(body, name=name, grid=(nb + 1,), carry=carry,
                  in_specs=[pl.BlockSpec((None, BLK, D), lambda n: (0, cur(n), 0)),
                            pl.BlockSpec((None, BLK, kvw), lambda n: (1, cur(n), 0)),
                            pl.BlockSpec((None, BLK, kvw), lambda n: (1, jnp.maximum(cur(n) - 1, 0), 0)),
                            pl.BlockSpec((BLK, D), lambda n: (cur(n), 0)),
                            pl.BlockSpec((BLK, D), lambda n: (cur(n), 0)),
                            pl.BlockSpec((BLK, 128), lambda n: (cur(n), 0)),
                            pl.BlockSpec((N_HEADS, BLK, 2 * BLK), lambda n: (0, 0, 0)),
                            pl.BlockSpec(memory_space=pltpu.SMEM)],
                  out_specs=[pl.BlockSpec((2, BLK, D), lambda n: (0, jnp.maximum(n - 1, 0), 0)),
                             pl.BlockSpec((N_HEADS, BLK, 2 * BLK), lambda n: (0, 0, 0)),
                             pl.BlockSpec((1, 128), lambda n: (0, 0))],
                  out_shape=[_sds((2, s, D), BF16), _sds((N_HEADS, BLK, 2 * BLK), F32), _sds((1, 128), F32)],
                  scratch=[pltpu.VMEM((BLK, D), F32), pltpu.VMEM((BLK, kvw), F32), pltpu.VMEM((BLK, D), F32),
                           pltpu.VMEM((BLK, kvw), F32), pltpu.VMEM((BLK, kvw), F32)])
    return _carried(call, (pb, pb, pb, attn, dattn, lse, biasm, sinks), carry)


HALO = 16
CW = 512


def _conv_taps(cu, halo_cu, first_tile):
    row = lax.broadcasted_iota(jnp.int32, cu.shape, 0)
    halo_cu = jnp.where(first_tile, 0.0, halo_cu)
    c1 = jnp.where(row == 0, halo_cu[HALO - 1:HALO], pltpu.roll(cu, 1, 0))
    c2 = jnp.where(row == 0, halo_cu[HALO - 2:HALO - 1],
                   jnp.where(row == 1, halo_cu[HALO - 1:HALO], pltpu.roll(cu, 2, 0)))
    return c1, c2


def _conv_merge_fwd(pa, attn, convw, name, ts=512):
    _, s, _ = pa.shape
    ts = min(ts, s)
    hb = ts // HALO

    def body(pa_ref, hp_ref, at_ref, w_ref, o_ref):
        i = pl.program_id(1)
        cu = pa_ref[0].astype(F32) * pa_ref[2].astype(F32)
        c1, c2 = _conv_taps(cu, hp_ref[0].astype(F32) * hp_ref[2].astype(F32), i == 0)
        w = w_ref[...]
        c3 = w[0:1] * c2 + w[1:2] * c1 + w[2:3] * cu
        conv = pa_ref[1].astype(F32) * c3
        o_ref[...] = (jax.nn.sigmoid(pa_ref[3].astype(F32)) * at_ref[...].astype(F32)
                      + jax.nn.sigmoid(pa_ref[4].astype(F32)) * conv).astype(BF16)

    return _pcall(body, name=name, grid=(D // CW, s // ts),
                  in_specs=[pl.BlockSpec((5, ts, CW), lambda c, i: (0, i, c)),
                            pl.BlockSpec((5, HALO, CW), lambda c, i: (0, jnp.maximum(i * hb - 1, 0), c)),
                            pl.BlockSpec((ts, CW), lambda c, i: (i, c)),
                            pl.BlockSpec((8, CW), lambda c, i: (0, c))],
                  out_specs=pl.BlockSpec((ts, CW), lambda c, i: (i, c)),
                  out_shape=_sds((s, D), BF16))(pa, pa, attn, convw)


def _conv_merge_bwd(dmerged, pa, attn, convw, name, ts=512, carry=None):
    _, s, _ = pa.shape
    ts = min(ts, s)
    hb = ts // HALO
    last_hb = s // HALO - 1

    def body(dm_ref, pa_ref, at_ref, w_ref, hp_ref, hn_ref, dmn_ref, dat_ref, dpa_ref, dw_ref):
        i = pl.program_id(1)
        last = i == pl.num_programs(1) - 1
        dm = dm_ref[...].astype(F32)
        cp, bp, u = pa_ref[0].astype(F32), pa_ref[1].astype(F32), pa_ref[2].astype(F32)
        sa = jax.nn.sigmoid(pa_ref[3].astype(F32))
        sc = jax.nn.sigmoid(pa_ref[4].astype(F32))
        at = at_ref[...].astype(F32)
        cu = cp * u
        c1, c2 = _conv_taps(cu, hp_ref[0].astype(F32) * hp_ref[2].astype(F32), i == 0)
        w = w_ref[...]
        c3 = w[0:1] * c2 + w[1:2] * c1 + w[2:3] * cu
        dconv = dm * sc
        dc3 = dconv * bp
        nxt = dmn_ref[...].astype(F32) * jax.nn.sigmoid(hn_ref[4].astype(F32)) * hn_ref[1].astype(F32)
        nxt = jnp.where(last, 0.0, nxt)
        row = lax.broadcasted_iota(jnp.int32, dc3.shape, 0)
        d1 = jnp.where(row == ts - 1, nxt[0:1], pltpu.roll(dc3, ts - 1, 0))
        d2 = jnp.where(row == ts - 2, nxt[0:1], jnp.where(row == ts - 1, nxt[1:2], pltpu.roll(dc3, ts - 2, 0)))
        dcu = w[2:3] * dc3 + w[1:2] * d1 + w[0:1] * d2
        dat_ref[...] = (dm * sa).astype(BF16)
        dpa_ref[0] = (dcu * u).astype(BF16)
        dpa_ref[1] = (dconv * c3).astype(BF16)
        dpa_ref[2] = (dcu * cp).astype(BF16)
        dpa_ref[3] = (dm * at * sa * (1.0 - sa)).astype(BF16)
        dpa_ref[4] = (dm * bp * c3 * sc * (1.0 - sc)).astype(BF16)

        @pl.when(i == 0)
        def _():
            dw_ref[...] = jnp.zeros_like(dw_ref)

        dw_ref[0:1, :] += jnp.sum(dc3 * c2, axis=0, keepdims=True)
        dw_ref[1:2, :] += jnp.sum(dc3 * c1, axis=0, keepdims=True)
        dw_ref[2:3, :] += jnp.sum(dc3 * cu, axis=0, keepdims=True)

    call = _pcall(body, name=name, grid=(D // CW, s // ts), carry=carry,
                  in_specs=[pl.BlockSpec((ts, CW), lambda c, i: (i, c)),
                            pl.BlockSpec((5, ts, CW), lambda c, i: (0, i, c)),
                            pl.BlockSpec((ts, CW), lambda c, i: (i, c)),
                            pl.BlockSpec((8, CW), lambda c, i: (0, c)),
                            pl.BlockSpec((5, HALO, CW), lambda c, i: (0, jnp.maximum(i * hb - 1, 0), c)),
                            pl.BlockSpec((5, HALO, CW), lambda c, i: (0, jnp.minimum((i + 1) * hb, last_hb), c)),
                            pl.BlockSpec((HALO, CW), lambda c, i: (jnp.minimum((i + 1) * hb, last_hb), c))],
                  out_specs=[pl.BlockSpec((ts, CW), lambda c, i: (i, c)),
                             pl.BlockSpec((5, ts, CW), lambda c, i: (0, i, c)),
                             pl.BlockSpec((8, CW), lambda c, i: (0, c))],
                  out_shape=[_sds((s, D), BF16), _sds((5, s, D), BF16), _sds((8, D), F32)])
    return _carried(call, (dmerged, pa, attn, convw, pa, pa, dmerged), carry)


def _xattn_fwd(q, kv, name, tq=512):
    s, _ = q.shape
    nm = kv.shape[0]
    tq = min(tq, s)

    def body(q_ref, kv_ref, o_ref, lse_ref):
        lane = lax.broadcasted_iota(jnp.int32, (tq, 128), 1)
        lse_t = jnp.zeros((tq, 128), F32)
        for h in range(XH):
            hs = slice(XHD * h, XHD * (h + 1))
            vs = slice(D + XHD * h, D + XHD * (h + 1))
            sc = _dot_nt(q_ref[:, hs], kv_ref[:, hs]) * (XHD ** -0.5)
            m = jnp.max(sc, axis=1, keepdims=True)
            p = jnp.exp(sc - m)
            l = jnp.sum(p, axis=1, keepdims=True)
            o_ref[:, hs] = (_dot(p.astype(BF16), kv_ref[:, vs]) * (1.0 / l)).astype(BF16)
            lse_t = jnp.where(lane == h, m + jnp.log(l), lse_t)
        lse_ref[...] = lse_t

    return _pcall(body, name=name, grid=(s // tq,),
                  in_specs=[pl.BlockSpec((tq, D), lambda i: (i, 0)), pl.BlockSpec((nm, 2 * D), lambda i: (0, 0))],
                  out_specs=[pl.BlockSpec((tq, D), lambda i: (i, 0)), pl.BlockSpec((tq, 128), lambda i: (i, 0))],
                  out_shape=[_sds((s, D), BF16), _sds((s, 128), F32)])(q, kv)


def _xattn_bwd(q, kv, o, do, lse, name, tq=512, carry=None):
    s, _ = q.shape
    nm = kv.shape[0]
    tq = min(tq, s)

    def body(q_ref, kv_ref, o_ref, do_ref, lse_ref, dq_ref, dkv_ref):
        @pl.when(pl.program_id(0) == 0)
        def _():
            dkv_ref[...] = jnp.zeros_like(dkv_ref)

        for h in range(XH):
            hs = slice(XHD * h, XHD * (h + 1))
            vs = slice(D + XHD * h, D + XHD * (h + 1))
            qh, kh, vh, dob = q_ref[:, hs], kv_ref[:, hs], kv_ref[:, vs], do_ref[:, hs]
            p = jnp.exp(_dot_nt(qh, kh) * (XHD ** -0.5) - lse_ref[:, h:h + 1])
            dp = _dot_nt(dob, vh)
            delta = jnp.sum(dob.astype(F32) * o_ref[:, hs].astype(F32), axis=1, keepdims=True)
            dsb = (p * (dp - delta) * (XHD ** -0.5)).astype(BF16)
            dq_ref[:, hs] = _dot(dsb, kh).astype(BF16)
            dkv_ref[:, hs] += _dot_tn(dsb, qh)
            dkv_ref[:, vs] += _dot_tn(p.astype(BF16), dob)

    call = _pcall(body, name=name, grid=(s // tq,), carry=carry,
                  in_specs=[pl.BlockSpec((tq, D), lambda i: (i, 0)), pl.BlockSpec((nm, 2 * D), lambda i: (0, 0)),
                            pl.BlockSpec((tq, D), lambda i: (i, 0)), pl.BlockSpec((tq, D), lambda i: (i, 0)),
                            pl.BlockSpec((tq, 128), lambda i: (i, 0))],
                  out_specs=[pl.BlockSpec((tq, D), lambda i: (i, 0)), pl.BlockSpec((nm, 2 * D), lambda i: (0, 0))],
                  out_shape=[_sds((s, D), BF16), _sds((nm, 2 * D), F32)])
    return _carried(call, (q, kv, o, do, lse), carry)


def _ffn_down_bwd(dxb, wd4, gu4, name, tm=512, carry=None):
    s, _ = dxb.shape
    tm = min(tm, s)

    def body(dx_ref, w_ref, gu_ref, o_ref):
        da = 0.5 * _dot_nt(dx_ref[...], w_ref[...])
        g = gu_ref[0].astype(F32)
        u = gu_ref[1].astype(F32)
        sg = jax.nn.sigmoid(g)
        o_ref[0] = (da * u * sg * (1.0 + g * (1.0 - sg))).astype(BF16)
        o_ref[1] = (da * g * sg).astype(BF16)

    call = _pcall(body, name=name, grid=(4, s // tm), carry=carry,
                  in_specs=[pl.BlockSpec((tm, D), lambda p, i: (i, 0)),
                            pl.BlockSpec((None, FS, D), lambda p, i: (p, 0, 0)),
                            pl.BlockSpec((2, None, tm, FS), lambda p, i: (0, p, i, 0))],
                  out_specs=pl.BlockSpec((2, None, tm, FS), lambda p, i: (0, p, i, 0)),
                  out_shape=_sds((2, 4, s, FS), BF16))
    return _carried(call, (dxb, wd4, gu4), carry)


def _mm_tn(a, b, name, scale=1.0, tk=1024, tn=None):
    pa_n, s, m = a.shape
    pb_n, _, n = b.shape
    po = max(pa_n, pb_n)
    tk = min(tk, s)
    tn = n if tn is None else tn
    nk = s // tk

    def body(a_ref, b_ref, o_ref, acc_ref):
        k = pl.program_id(2)

        @pl.when(k == 0)
        def _():
            acc_ref[...] = jnp.zeros_like(acc_ref)

        acc_ref[...] += _dot_tn(a_ref[...], b_ref[...])

        @pl.when(k == nk - 1)
        def _():
            o_ref[...] = (scale * acc_ref[...]).astype(BF16)

    return _pcall(body, name=name, grid=(po, n // tn, nk),
                  in_specs=[pl.BlockSpec((None, tk, m), lambda o, j, k: (o if pa_n > 1 else 0, k, 0)),
                            pl.BlockSpec((None, tk, tn), lambda o, j, k: (o if pb_n > 1 else 0, k, j))],
                  out_specs=pl.BlockSpec((None, m, tn), lambda o, j, k: (o, 0, j)),
                  out_shape=_sds((po, m, n), BF16), scratch=[pltpu.VMEM((m, tn), F32)])(a, b)


def _mm_nt(a, b, name, out_dtype, tm=512):
    nj, s, k = a.shape
    n = b.shape[1]
    tm = min(tm, s)

    def body(a_ref, b_ref, o_ref, acc_ref):
        j = pl.program_id(1)

        @pl.when(j == 0)
        def _():
            acc_ref[...] = jnp.zeros_like(acc_ref)

        acc_ref[...] += _dot_nt(a_ref[...], b_ref[...])

        @pl.when(j == nj - 1)
        def _():
            o_ref[...] = acc_ref[...].astype(out_dtype)

    return _pcall(body, name=name, grid=(s // tm, nj),
                  in_specs=[pl.BlockSpec((None, tm, k), lambda i, j: (j, i, 0)),
                            pl.BlockSpec((None, n, k), lambda i, j: (j, 0, 0))],
                  out_specs=pl.BlockSpec((tm, n), lambda i, j: (i, 0)),
                  out_shape=_sds((s, n), out_dtype), scratch=[pltpu.VMEM((tm, n), F32)])(a, b)


def _mm_nt_rms_bwd(a, b, name, *, x, gain, dres, addend=None, tm=512, carry=None):
    nj, s, k = a.shape
    n = b.shape[1]
    tm = min(tm, s)
    has_add = addend is not None

    def body(*refs):
        a_ref, b_ref, x_ref, g_ref, r_ref = refs[:5]
        add_ref = refs[5] if has_add else None
        dx_ref, dxb_ref, dg_ref, acc_ref = refs[5 + has_add:]
        i, j = pl.program_id(0), pl.program_id(1)

        @pl.when(j == 0)
        def _():
            acc_ref[...] = jnp.zeros_like(acc_ref)

        @pl.when(jnp.logical_and(i == 0, j == 0))
        def _():
            dg_ref[...] = jnp.zeros_like(dg_ref)

        acc_ref[...] += _dot_nt(a_ref[...], b_ref[...])

        @pl.when(j == nj - 1)
        def _():
            dh = acc_ref[...]
            if has_add:
                dh = dh + add_ref[...]
            xv = x_ref[...]
            r = lax.rsqrt(jnp.mean(xv * xv, axis=-1, keepdims=True) + EPS)
            xh = xv * r
            dyg = dh * g_ref[...]
            dx = r_ref[...] + r * (dyg - xh * jnp.mean(dyg * xh, axis=-1, keepdims=True))
            dx_ref[...] = dx
            dxb_ref[...] = dx.astype(BF16)
            dg_ref[...] += jnp.sum(dh * xh, axis=0, keepdims=True)

    row = pl.BlockSpec((tm, n), lambda i, j: (i, 0))
    in_specs = [pl.BlockSpec((None, tm, k), lambda i, j: (j, i, 0)),
                pl.BlockSpec((None, n, k), lambda i, j: (j, 0, 0)),
                row, pl.BlockSpec((1, n), lambda i, j: (0, 0)), row] + ([row] if has_add else [])
    args = (a, b, x, gain, dres) + ((addend,) if has_add else ())
    call = _pcall(body, name=name, grid=(s // tm, nj), in_specs=in_specs, carry=carry,
                  out_specs=[row, row, pl.BlockSpec((1, n), lambda i, j: (0, 0))],
                  out_shape=[_sds((s, n), F32), _sds((s, n), BF16), _sds((1, n), F32)],
                  scratch=[pltpu.VMEM((tm, n), F32)])
    return _carried(call, args, carry)


def _adam(w, g, m, v):
    m2 = ADAM_B1 * m + (1.0 - ADAM_B1) * g
    v2 = ADAM_B2 * v + (1.0 - ADAM_B2) * (g * g)
    m_hat = m2 / (1.0 - ADAM_B1 ** ADAM_STEP)
    v_hat = v2 / (1.0 - ADAM_B2 ** ADAM_STEP)
    delta = -ADAM_LR * (m_hat / (jnp.sqrt(v_hat) + ADAM_EPS) + ADAM_WD * w)
    return delta, m2, v2


def _adamw(parts, w, m, v, name):
    _, r, c = parts.shape
    tr = r if r <= 256 else (176 if r == 352 else 256)

    def body(p_ref, w_ref, m_ref, v_ref, g_ref, d_ref, m2_ref, v2_ref):
        g = p_ref[0].astype(F32)
        for i in range(1, N_DEV):
            g = g + p_ref[i].astype(F32)
        delta, m2, v2 = _adam(w_ref[...], g, m_ref[...], v_ref[...])
        g_ref[...] = g
        d_ref[...] = delta
        m2_ref[...] = m2
        v2_ref[...] = v2

    blk = pl.BlockSpec((tr, c), lambda i: (i, 0))
    return _pcall(body, name=name, grid=(r // tr,),
                  in_specs=[pl.BlockSpec((N_DEV, tr, c), lambda i: (0, i, 0)), blk, blk, blk],
                  out_specs=[blk] * 4, out_shape=[_sds((r, c), F32)] * 4)(parts, w, m, v)


def _position():
    return lax.axis_index("x"), lax.axis_index("y"), lax.axis_index("c")


def _slot(px, py, pc):
    return 4 * px + 2 * py + pc


def _gather_carry(shards):
    na = len(shards)

    def plan(ins, outs, sems):
        send_sems, recv_sems, local_sems = sems
        x, y, c = _position()
        me, sibling = (x, y, c), (x, y, 1 - c)
        chips = [(1 - x, y), (x, 1 - y), (1 - x, 1 - y)]

        def copy(a, k, block, to, src=None):
            rows = outs[a].at[_slot(*block)]
            return pltpu.make_async_remote_copy(src_ref=rows if src is None else src, dst_ref=rows,
                                                send_sem=send_sems.at[k, a], recv_sem=recv_sems.at[k, a],
                                                device_id=to, device_id_type=MESH)

        mine = [pltpu.make_async_copy(ins[a], outs[a].at[_slot(*me)], local_sems.at[a]) for a in range(na)]
        first = [copy(a, 0, me, sibling, src=ins[a]) for a in range(na)]
        for j, chip in enumerate(chips):
            first += [copy(a, 1 + j, me, (*chip, c), src=ins[a]) for a in range(na)]
        landed = [[copy(a, 1 + j, (*chip, c), me) for a in range(na)] for j, chip in enumerate(chips)]
        passed = [[copy(a, 4 + j, (*chip, c), sibling) for a in range(na)] for j, chip in enumerate(chips)]
        from_sibling = [copy(a, 0, sibling, me) for a in range(na)]
        for j, chip in enumerate(chips):
            from_sibling += [copy(a, 4 + j, (*chip, 1 - c), me) for a in range(na)]
        return mine, first, landed, passed, from_sibling

    def start(ins, outs, sems):
        mine, first, _, _, _ = plan(ins, outs, sems)
        for cp in mine + first:
            cp.start()

    def mid(ins, outs, sems):
        _, _, landed, passed, _ = plan(ins, outs, sems)
        for over_ici, onward in zip(landed, passed):
            for cp, fwd in zip(over_ici, onward):
                cp.wait_recv()
                fwd.start()

    def finish(ins, outs, sems):
        mine, first, _, passed, from_sibling = plan(ins, outs, sems)
        for cp in from_sibling:
            cp.wait_recv()
        for cp in first + [fwd for onward in passed for fwd in onward]:
            cp.wait_send()
        for cp in mine:
            cp.wait()

    return _Carry(shards, [_sds((N_DEV,) + a.shape, a.dtype) for a in shards],
                  [pltpu.SemaphoreType.DMA((7, na)), pltpu.SemaphoreType.DMA((7, na)),
                   pltpu.SemaphoreType.DMA((na,))], start, finish, mid)


def _exchange_carry(scattered, replicated=()):
    arrays = list(scattered) + list(replicated)
    na, ns = len(arrays), len(scattered)

    def plan(ins, outs, sems):
        send_sems, recv_sems, local_sems = sems
        me = _slot(*_position())

        def source(a, j):
            return ins[a].at[j] if a < ns else ins[a]

        def copy(a, j, i):
            return pltpu.make_async_remote_copy(src_ref=source(a, j), dst_ref=outs[a].at[i],
                                                send_sem=send_sems.at[j, a], recv_sem=recv_sems.at[i, a],
                                                device_id=(j >> 2, (j >> 1) & 1, j & 1), device_id_type=MESH)

        mine = [pltpu.make_async_copy(source(a, me), outs[a].at[me], local_sems.at[a]) for a in range(na)]
        return me, copy, mine

    def start(ins, outs, sems):
        me, copy, mine = plan(ins, outs, sems)
        for cp in mine:
            cp.start()
        for j in range(N_DEV):
            @pl.when(me != j)
            def _():
                for a in range(na):
                    copy(a, j, me).start()

    def finish(ins, outs, sems):
        me, copy, mine = plan(ins, outs, sems)
        for i in range(N_DEV):
            @pl.when(me != i)
            def _():
                for a in range(na):
                    copy(a, i, i).wait_recv()
        for j in range(N_DEV):
            @pl.when(me != j)
            def _():
                for a in range(na):
                    copy(a, j, me).wait_send()
        for cp in mine:
            cp.wait()

    return _Carry(arrays, [_sds((N_DEV,) + a.shape[-2:], a.dtype) for a in arrays],
                  [pltpu.SemaphoreType.DMA((N_DEV, na)), pltpu.SemaphoreType.DMA((N_DEV, na)),
                   pltpu.SemaphoreType.DMA((na,))], start, finish)


NQ, NKV = N_HEADS * HEAD, 2 * N_KV * HEAD


class _Mesh:
    def __init__(self, shards):
        self.shards, self.full, self.received, self.cache = shards, {}, {}, {}

    def fetch(self, names):
        return _gather_carry([self.shards[n] for n in names])

    def fetched(self, names, results):
        self.full.update(zip(names, results))

    def send(self, name, parts):
        del name
        return _exchange_carry([parts])

    def sent(self, name, results):
        self.received[name] = results[0]

    def w(self, key):
        if key not in self.cache:
            self.cache[key] = self._layout(key)
        return self.cache[key]

    def _layout(self, key):
        if key in ("gu1", "gu2"):
            return self.full[key]
        if key in ("d1", "d2"):
            return self.full[key].reshape(4, FS, D)
        if key in ("out", "q", "o"):
            return self.full[key].reshape(D, D)
        if key == "kv":
            return self.full["kv"].transpose(1, 0, 2).reshape(D, 2 * D)
        if key == "convw":
            rows = self.full["conv"][:, :3, :].transpose(1, 0, 2).reshape(3, D)
            return jnp.concatenate([rows, jnp.zeros((5, D), F32)], axis=0)
        w_in = self.full["win"].transpose(1, 0, 2).reshape(D, -1)
        if key == "wa":
            return w_in[:, NQ + NKV:].reshape(D, 5, D).transpose(1, 0, 2)
        assert key == "wb", key
        return jnp.stack([w_in[:, :NQ], jnp.pad(w_in[:, NQ:NQ + NKV], ((0, 0), (0, D - NKV)))])


def _w_in_parts(dw_a, dw_b):
    full = jnp.concatenate([dw_b[0], dw_b[1][:, :NKV], dw_a.transpose(1, 0, 2).reshape(D, 5 * D)], axis=1)
    return full.reshape(D, N_DEV, -1).transpose(1, 0, 2)


def _forward_backward(x, mem, target, g, rel_bias, sinks, ex):
    s = x.shape[0]
    h1 = _rmsnorm(x, g["ffn1"], "norm_ffn1")
    (gu1, a1), got = _ffn_up(h1, ex.w("gu1").reshape(2, 4, D, FS), "ffn1_up", carry=ex.fetch(["win"]))
    ex.fetched(["win"], got)
    (x1, h2), got = _mm_res_norm(a1, ex.w("d1"), x, g["mix"], 0.5, "ffn1_down", carry=ex.fetch(["out", "q", "o"]))
    ex.fetched(["out", "q", "o"], got)
    pa, got = _mm_nn(h2, ex.w("wa"), "in_proj_a", carry=ex.fetch(["gu2"]))
    ex.fetched(["gu2"], got)
    pb, _ = _mm_nn(h2, ex.w("wb"), "in_proj_b")
    biasm = _bias_build(rel_bias, "bias_build")
    (attn, lse), got = _swa_fwd(pb, biasm, sinks, "swa_fwd", carry=ex.fetch(["kv", "d2"]))
    ex.fetched(["kv", "d2"], got)
    merged = _conv_merge_fwd(pa, attn, ex.w("convw"), "conv_merge_fwd")
    (x2, h3), _ = _mm_res_norm(merged[None], ex.w("out")[None], x1, g["xattn"], 1.0, "out_proj")
    q2 = _mm_nn(h3, ex.w("q")[None], "xattn_q")[0][0]
    mh = _rmsnorm(mem, g["mem"], "norm_mem")
    kv2 = _mm_nn(mh, ex.w("kv")[None], "xattn_kv")[0][0]
    o, lse2 = _xattn_fwd(q2, kv2, "xattn_fwd")
    (x3, h4), _ = _mm_res_norm(o[None], ex.w("o")[None], x2, g["ffn2"], 1.0, "xattn_o")
    (gu2, a2), _ = _ffn_up(h4, ex.w("gu2").reshape(2, 4, D, FS), "ffn2_up")
    dx4, dx4b, loss, d_final = _ffn_down_loss(a2, ex.w("d2"), x3, g["final"], target, "ffn2_down_loss")
    dw_d2 = _mm_tn(a2, dx4b[None], "dw_ffn2_down", scale=0.5)
    dgu2, got = _ffn_down_bwd(dx4b, ex.w("d2"), gu2, "ffn2_down_bwd", carry=ex.send("d2", dw_d2.reshape(N_DEV, -1, D)))
    ex.sent("d2", got)
    dgu2 = dgu2.reshape(8, s, FS)
    dw_gu2 = _mm_tn(h4[None], dgu2, "dw_ffn2_up")
    (dx3, dx3b, d_ffn2), got = _mm_nt_rms_bwd(dgu2, ex.w("gu2"), "ffn2_up_bwd", x=x3, gain=g["ffn2"], dres=dx4,
                                              carry=ex.send("gu2", dw_gu2))
    ex.sent("gu2", got)
    do = _mm_nt(dx3b[None], ex.w("o")[None], "xattn_o_bwd", BF16)
    dw_o = _mm_tn(o[None], dx3b[None], "dw_xattn_o")
    (dq2, dkv2), got = _xattn_bwd(q2, kv2, o, do, lse2, "xattn_bwd", carry=ex.send("o", dw_o.reshape(N_DEV, -1, D)))
    ex.sent("o", got)
    dkv2b = dkv2.astype(BF16)
    dw_q = _mm_tn(h3[None], dq2[None], "dw_xattn_q")
    (dx2, dx2b, d_xattn), got = _mm_nt_rms_bwd(dq2[None], ex.w("q")[None], "xattn_q_bwd", x=x2, gain=g["xattn"],
                                               dres=dx3, carry=ex.send("q", dw_q.reshape(N_DEV, -1, D)))
    ex.sent("q", got)
    dw_kv = _mm_tn(mh[None], dkv2b[None], "dw_xattn_kv")[0]
    (_, _, d_mem), _ = _mm_nt_rms_bwd(dkv2b[None], ex.w("kv")[None], "xattn_kv_bwd", x=mem, gain=g["mem"],
                                      dres=jnp.zeros_like(mem))
    dmerged = _mm_nt(dx2b[None], ex.w("out")[None], "out_proj_bwd", BF16)
    dw_out = _mm_tn(merged[None], dx2b[None], "dw_out_proj")
    (dattn, dpa, d_convw), got = _conv_merge_bwd(
        dmerged, pa, attn, ex.w("convw"), "conv_merge_bwd",
        carry=ex.send("kv", dw_kv.reshape(D, N_DEV, -1).transpose(1, 0, 2)))
    ex.sent("kv", got)
    (dpb, dbias, d_sinks), got = _swa_bwd(pb, attn, dattn, lse, biasm, sinks, "swa_bwd",
                                          carry=ex.send("out", dw_out.reshape(N_DEV, -1, D)))
    ex.sent("out", got)
    d_relb = _bias_bwd(dbias, "bias_bwd")
    dw_a = _mm_tn(h2[None], dpa, "dw_in_proj_a")
    dw_b = _mm_tn(h2[None], dpb, "dw_in_proj_b")
    dh2_b = _mm_nt(dpb, ex.w("wb"), "in_proj_b_bwd", F32)
    (dx1, dx1b, d_mix), got = _mm_nt_rms_bwd(dpa, ex.w("wa"), "in_proj_a_bwd", x=x1, gain=g["mix"], dres=dx2,
                                             addend=dh2_b, carry=ex.send("win", _w_in_parts(dw_a, dw_b)))
    ex.sent("win", got)
    dw_d1 = _mm_tn(a1, dx1b[None], "dw_ffn1_down", scale=0.5)
    dgu1, got = _ffn_down_bwd(dx1b, ex.w("d1"), gu1, "ffn1_down_bwd", carry=ex.send("d1", dw_d1.reshape(N_DEV, -1, D)))
    ex.sent("d1", got)
    dgu1 = dgu1.reshape(8, s, FS)
    dw_gu1 = _mm_tn(h1[None], dgu1, "dw_ffn1_up")
    (dx0, _, d_ffn1), got = _mm_nt_rms_bwd(dgu1, ex.w("gu1"), "ffn1_up_bwd", x=x, gain=g["ffn1"], dres=dx1,
                                           carry=ex.send("gu1", dw_gu1))
    ex.sent("gu1", got)

    relb_row = jnp.concatenate([d_relb[:, :REL_BUCKETS].T.reshape(1, REL_BUCKETS * N_HEADS), d_sinks[:, :N_HEADS],
                                jnp.zeros((1, D - REL_BUCKETS * N_HEADS - N_HEADS), F32)], axis=1)
    loss_row = jnp.concatenate([loss[0:1, 0:1], jnp.zeros((1, D - 1), F32)], axis=1)
    small = jnp.concatenate([d_ffn1, d_mix, d_xattn, d_mem, d_ffn2, d_final, relb_row, loss_row, d_convw[0:3],
                             jnp.zeros((SMALL_ROWS - ROW_CONV - 3, D), F32)], axis=0)
    return dx0, small


def _pack_small(norms, final, relb, sinks, conv_local, me):
    relb_row = jnp.concatenate([relb.reshape(1, -1), sinks.reshape(1, -1),
                                jnp.zeros((1, D - REL_BUCKETS * N_HEADS - N_HEADS), F32)], axis=1)
    conv_rows = lax.dynamic_update_slice(jnp.zeros((3, D), F32), conv_local.reshape(3, -1), (0, 128 * me))
    return jnp.concatenate(list(norms) + [final.reshape(1, D), relb_row, jnp.zeros((1, D), F32), conv_rows,
                                          jnp.zeros((SMALL_ROWS - ROW_CONV - 3, D), F32)], axis=0)


def kernel(x, mem, positions, rel_bias, ffn1_norm, ffn1_w_gu, ffn1_w_down, mix_norm, w_in, sinks, conv_w, w_out, xattn_norm, mem_norm, xattn_wq, xattn_wkv, xattn_wo, ffn2_norm, ffn2_w_gu, ffn2_w_down, final_norm, loss_target, m_rel_bias, m_ffn1_norm, m_ffn1_w_gu, m_ffn1_w_down, m_mix_norm, m_w_in, m_sinks, m_conv_w, m_w_out, m_xattn_norm, m_mem_norm, m_xattn_wq, m_xattn_wkv, m_xattn_wo, m_ffn2_norm, m_ffn2_w_gu, m_ffn2_w_down, m_final_norm, v_rel_bias, v_ffn1_norm, v_ffn1_w_gu, v_ffn1_w_down, v_mix_norm, v_w_in, v_sinks, v_conv_w, v_w_out, v_xattn_norm, v_mem_norm, v_xattn_wq, v_xattn_wkv, v_xattn_wo, v_ffn2_norm, v_ffn2_w_gu, v_ffn2_w_down, v_final_norm):
    del positions
    me = _slot(*_position())
    big = dict(gu1=(ffn1_w_gu, m_ffn1_w_gu, v_ffn1_w_gu), d1=(ffn1_w_down, m_ffn1_w_down, v_ffn1_w_down),
               win=(w_in, m_w_in, v_w_in), out=(w_out, m_w_out, v_w_out), q=(xattn_wq, m_xattn_wq, v_xattn_wq),
               kv=(xattn_wkv, m_xattn_wkv, v_xattn_wkv), o=(xattn_wo, m_xattn_wo, v_xattn_wo),
               gu2=(ffn2_w_gu, m_ffn2_w_gu, v_ffn2_w_gu), d2=(ffn2_w_down, m_ffn2_w_down, v_ffn2_w_down))
    order = list(big)
    shards = {k: big[k][0][0].astype(BF16) for k in order}
    shards["conv"] = jnp.concatenate([conv_w[0], jnp.zeros((5, 128), F32)], axis=0)
    ex = _Mesh(shards)
    first = ["gu1", "d1", "conv"]
    ex.fetched(first, _run_alone(ex.fetch(first), "gather_ffn1"))
    gains = dict(ffn1=ffn1_norm, mix=mix_norm, xattn=xattn_norm, mem=mem_norm, ffn2=ffn2_norm,
                 final=final_norm.reshape(1, D))
    dx, small = _forward_backward(x[0], mem[0], loss_target[0], gains, rel_bias, sinks, ex)
    small_parts = _run_alone(_exchange_carry([], [small]), "exchange_small")[0]
    big_out = {k: _adamw(ex.received[k], *(t[0] for t in big[k]), "adamw_" + k) for k in order}
    packed = [_pack_small(norms, final, relb, sk, conv, me) for norms, final, relb, sk, conv in (
        ((ffn1_norm, mix_norm, xattn_norm, mem_norm, ffn2_norm), final_norm, rel_bias, sinks, conv_w),
        ((m_ffn1_norm, m_mix_norm, m_xattn_norm, m_mem_norm, m_ffn2_norm), m_final_norm, m_rel_bias, m_sinks, m_conv_w),
        ((v_ffn1_norm, v_mix_norm, v_xattn_norm, v_mem_norm, v_ffn2_norm), v_final_norm, v_rel_bias, v_sinks, v_conv_w))]
    small_out = _adamw(small_parts, *packed, "adamw_small")

    def unpack(t):
        conv = lax.dynamic_slice(t[ROW_CONV:ROW_CONV + 3], (0, 128 * me), (3, 128))[None]
        nrel = REL_BUCKETS * N_HEADS
        return dict(ffn1_norm=t[0:1], mix_norm=t[1:2], xattn_norm=t[2:3], mem_norm=t[3:4], ffn2_norm=t[4:5],
                    final_norm=t[5], rel_bias=t[ROW_RELB, :nrel].reshape(REL_BUCKETS, N_HEADS),
                    sinks=t[ROW_RELB:ROW_RELB + 1, nrel:nrel + N_HEADS], conv_w=conv)

    names = dict(gu1="ffn1_w_gu", d1="ffn1_w_down", win="w_in", out="w_out", q="xattn_wq", kv="xattn_wkv",
                 o="xattn_wo", gu2="ffn2_w_gu", d2="ffn2_w_down")
    results = []
    for idx in range(4):
        leaves = unpack(small_out[idx])
        leaves.update({names[k]: big_out[k][idx][None] for k in order})
        results.append(leaves)
    weights = ("rel_bias", "ffn1_norm", "ffn1_w_gu", "ffn1_w_down", "mix_norm", "w_in", "sinks", "conv_w", "w_out",
               "xattn_norm", "mem_norm", "xattn_wq", "xattn_wkv", "xattn_wo", "ffn2_norm", "ffn2_w_gu", "ffn2_w_down",
               "final_norm")
    loss = small_out[0][ROW_LOSS, 0]
    return (loss, dx[None], *[leaves[n] for leaves in results for n in weights])
```

```python
import math

import numpy as np
import jax
import jax.numpy as jnp
from jax import lax
from jax.experimental import pallas as pl
from jax.experimental.pallas import tpu as pltpu

F32, BF16 = jnp.float32, jnp.bfloat16
MESH = pl.DeviceIdType.MESH

D = 1024
N_DEV = 8
D_FF = 2816
FS = D_FF // 4
HEAD = 64
N_HEADS, N_KV = 16, 4
BLK = 128
XH, XHD = 4, 256
REL_BUCKETS, REL_EXACT, REL_MAX_DIST = 32, 16, 128
EPS, NEG = 1e-6, -1e30
ADAM_LR, ADAM_B1, ADAM_B2, ADAM_EPS, ADAM_WD, ADAM_STEP = 0.001, 0.9, 0.999, 1e-08, 0.01, 10
VMEM_LIMIT_V7X = 56 * 2**20
SMALL_ROWS = 16
ROW_RELB, ROW_LOSS, ROW_CONV = 6, 7, 8


def _bucket_thresholds():
    n = np.arange(REL_MAX_DIST)
    nf = np.maximum(n, 1).astype(np.float32)
    large = REL_EXACT + (np.log(nf / np.float32(REL_EXACT)) / np.float32(math.log(REL_MAX_DIST / REL_EXACT))
                         * np.float32(REL_BUCKETS - REL_EXACT)).astype(np.int32)
    b = np.where(n < REL_EXACT, n, np.minimum(large, REL_BUCKETS - 1))
    return [int(np.argmax(b >= REL_EXACT + k)) for k in range(1, REL_BUCKETS - REL_EXACT)]


BUCKET_THRESHOLDS = _bucket_thresholds()


HBM_SPEC = pl.BlockSpec(memory_space=pl.ANY)


class _Carry:
    def __init__(self, ins, outs, sems, start, finish, mid=None, aliases=None):
        self.ins, self.outs, self.sems = list(ins), list(outs), list(sems)
        self.start, self.finish, self.mid, self.aliases = start, finish, mid, dict(aliases or {})


def _pcall(body, *, name, grid, in_specs, out_specs, out_shape, scratch=(), carry=None):
    params = pltpu.CompilerParams(dimension_semantics=("arbitrary",) * len(grid), vmem_limit_bytes=VMEM_LIMIT_V7X)
    if carry is None:
        return pl.pallas_call(body, name=name, grid=grid, in_specs=in_specs, out_specs=out_specs,
                              out_shape=out_shape, scratch_shapes=list(scratch), compiler_params=params)
    single = not isinstance(out_shape, (list, tuple))
    own_specs, own_shapes = ([out_specs], [out_shape]) if single else (list(out_specs), list(out_shape))
    n_in, n_out, n_scr = len(in_specs), len(own_shapes), len(scratch)
    n_cin, n_cout = len(carry.ins), len(carry.outs)
    steps = math.prod(grid)
    mid_step = max(steps - 1 - max(steps // 8, 1), 0)

    def carrying(*refs):
        ins, refs = refs[:n_in], refs[n_in:]
        cins, refs = refs[:n_cin], refs[n_cin:]
        outs, refs = refs[:n_out], refs[n_out:]
        couts, refs = refs[:n_cout], refs[n_cout:]
        scr, csems = refs[:n_scr], refs[n_scr:]
        step = 0
        for axis, size in enumerate(grid):
            step = step * size + pl.program_id(axis)

        @pl.when(step == 0)
        def _():
            carry.start(cins, couts, csems)

        body(*ins, *outs, *scr)
        if carry.mid is not None:
            @pl.when(step == mid_step)
            def _():
                carry.mid(cins, couts, csems)

        @pl.when(step == steps - 1)
        def _():
            carry.finish(cins, couts, csems)

    call = pl.pallas_call(carrying, name=name, grid=grid, in_specs=list(in_specs) + [HBM_SPEC] * n_cin,
                          out_specs=own_specs + [HBM_SPEC] * n_cout, out_shape=own_shapes + carry.outs,
                          scratch_shapes=list(scratch) + carry.sems, compiler_params=params,
                          input_output_aliases={n_in + i: n_out + o for i, o in carry.aliases.items()})

    def run(*args):
        res = call(*args, *carry.ins)
        return (res[0] if single else res[:n_out]), res[n_out:]

    return run


def _run_alone(carry, name):
    n_cin, n_cout = len(carry.ins), len(carry.outs)

    def body(*refs):
        cins, couts, csems = refs[:n_cin], refs[n_cin:n_cin + n_cout], refs[n_cin + n_cout:]
        carry.start(cins, couts, csems)
        if carry.mid is not None:
            carry.mid(cins, couts, csems)
        carry.finish(cins, couts, csems)

    return pl.pallas_call(body, name=name, in_specs=[HBM_SPEC] * n_cin, out_specs=[HBM_SPEC] * n_cout,
                          out_shape=carry.outs, scratch_shapes=carry.sems,
                          input_output_aliases=carry.aliases)(*carry.ins)


def _dot(a, b):
    return jnp.dot(a, b, preferred_element_type=F32)


def _dot_nt(a, b):
    return lax.dot_general(a, b, (((1,), (1,)), ((), ())), preferred_element_type=F32)


def _dot_tn(a, b):
    return lax.dot_general(a, b, (((0,), (0,)), ((), ())), preferred_element_type=F32)


def _sds(shape, dtype):
    return jax.ShapeDtypeStruct(tuple(shape), dtype)


def _carried(call, args, carry):
    return call(*args) if carry is not None else (call(*args), ())


def _rmsnorm(x, g, name, carry=None):
    m, d = x.shape
    tm = min(512, m)

    def body(x_ref, g_ref, h_ref):
        xv = x_ref[...]
        r = lax.rsqrt(jnp.mean(xv * xv, axis=-1, keepdims=True) + EPS)
        h_ref[...] = (xv * r * g_ref[...]).astype(BF16)

    call = _pcall(body, name=name, grid=(m // tm,), carry=carry,
                  in_specs=[pl.BlockSpec((tm, d), lambda i: (i, 0)), pl.BlockSpec((1, d), lambda i: (0, 0))],
                  out_specs=pl.BlockSpec((tm, d), lambda i: (i, 0)), out_shape=_sds((m, d), BF16))
    return _carried(call, (x, g), carry)


def _mm_nn(a, b, name, tm=512, carry=None):
    m, k = a.shape
    nj, _, n = b.shape
    tm = min(tm, m)

    def body(a_ref, b_ref, o_ref):
        o_ref[...] = _dot(a_ref[...], b_ref[...]).astype(BF16)

    call = _pcall(body, name=name, grid=(nj, m // tm),
                  in_specs=[pl.BlockSpec((tm, k), lambda j, i: (i, 0)),
                            pl.BlockSpec((None, k, n), lambda j, i: (j, 0, 0))],
                  out_specs=pl.BlockSpec((None, tm, n), lambda j, i: (j, i, 0)),
                  out_shape=_sds((nj, m, n), BF16), carry=carry)
    return _carried(call, (a, b), carry)


def _ffn_up(h, w4, name, tm=512, carry=None):
    s, d = h.shape
    tm = min(tm, s)

    def body(h_ref, w_ref, gu_ref, a_ref):
        hv = h_ref[...]
        g = _dot(hv, w_ref[0])
        u = _dot(hv, w_ref[1])
        gu_ref[0] = g.astype(BF16)
        gu_ref[1] = u.astype(BF16)
        a_ref[...] = (g * jax.nn.sigmoid(g) * u).astype(BF16)

    call = _pcall(body, name=name, grid=(4, s // tm),
                  in_specs=[pl.BlockSpec((tm, d), lambda p, i: (i, 0)),
                            pl.BlockSpec((2, None, d, FS), lambda p, i: (0, p, 0, 0))],
                  out_specs=[pl.BlockSpec((2, None, tm, FS), lambda p, i: (0, p, i, 0)),
                             pl.BlockSpec((None, tm, FS), lambda p, i: (p, i, 0))],
                  out_shape=[_sds((2, 4, s, FS), BF16), _sds((4, s, FS), BF16)], carry=carry)
    return _carried(call, (h, w4), carry)


def _mm_res_norm(a, w, xres, gain, scale, name, tm=512, carry=None):
    npart, s, kp = a.shape
    tm = min(tm, s)

    def body(a_ref, w_ref, x_ref, g_ref, xo_ref, h_ref):
        acc = _dot(a_ref[0], w_ref[0])
        for p in range(1, npart):
            acc = acc + _dot(a_ref[p], w_ref[p])
        xn = x_ref[...] + scale * acc
        xo_ref[...] = xn
        r = lax.rsqrt(jnp.mean(xn * xn, axis=-1, keepdims=True) + EPS)
        h_ref[...] = (xn * r * g_ref[...]).astype(BF16)

    call = _pcall(body, name=name, grid=(s // tm,),
                  in_specs=[pl.BlockSpec((npart, tm, kp), lambda i: (0, i, 0)),
                            pl.BlockSpec((npart, kp, D), lambda i: (0, 0, 0)),
                            pl.BlockSpec((tm, D), lambda i: (i, 0)),
                            pl.BlockSpec((1, D), lambda i: (0, 0))],
                  out_specs=[pl.BlockSpec((tm, D), lambda i: (i, 0)), pl.BlockSpec((tm, D), lambda i: (i, 0))],
                  out_shape=[_sds((s, D), F32), _sds((s, D), BF16)], carry=carry)
    return _carried(call, (a, w, xres, gain), carry)


def _ffn_down_loss(a, w, xres, gain, target, name, tm=512):
    npart, s, kp = a.shape
    tm = min(tm, s)

    def body(a_ref, w_ref, x_ref, g_ref, t_ref, dx_ref, dxb_ref, loss_ref, dg_ref):
        i = pl.program_id(0)
        acc = _dot(a_ref[0], w_ref[0])
        for p in range(1, npart):
            acc = acc + _dot(a_ref[p], w_ref[p])
        xn = x_ref[...] + 0.5 * acc
        r = lax.rsqrt(jnp.mean(xn * xn, axis=-1, keepdims=True) + EPS)
        xh = xn * r
        gv = g_ref[...]
        err = xh * gv - t_ref[...]
        part = 0.5 * jnp.sum(jnp.mean(err * err, axis=-1, keepdims=True), axis=0, keepdims=True)
        dy = err * (1.0 / D)
        dyg = dy * gv
        dxn = r * (dyg - xh * jnp.mean(dyg * xh, axis=-1, keepdims=True))
        dx_ref[...] = dxn
        dxb_ref[...] = dxn.astype(BF16)

        @pl.when(i == 0)
        def _():
            loss_ref[...] = jnp.zeros_like(loss_ref)
            dg_ref[...] = jnp.zeros_like(dg_ref)

        loss_ref[...] += jnp.broadcast_to(part, loss_ref.shape)
        dg_ref[...] += jnp.sum(dy * xh, axis=0, keepdims=True)

    return _pcall(body, name=name, grid=(s // tm,),
                  in_specs=[pl.BlockSpec((npart, tm, kp), lambda i: (0, i, 0)),
                            pl.BlockSpec((npart, kp, D), lambda i: (0, 0, 0)),
                            pl.BlockSpec((tm, D), lambda i: (i, 0)),
                            pl.BlockSpec((1, D), lambda i: (0, 0)),
                            pl.BlockSpec((tm, D), lambda i: (i, 0))],
                  out_specs=[pl.BlockSpec((tm, D), lambda i: (i, 0)), pl.BlockSpec((tm, D), lambda i: (i, 0)),
                             pl.BlockSpec((8, 128), lambda i: (0, 0)), pl.BlockSpec((1, D), lambda i: (0, 0))],
                  out_shape=[_sds((s, D), F32), _sds((s, D), BF16), _sds((8, 128), F32), _sds((1, D), F32)],
                  )(a, w, xres, gain, target)


def _band_tiles():
    i = lax.broadcasted_iota(jnp.int32, (BLK, 2 * BLK), 0)
    j = lax.broadcasted_iota(jnp.int32, (BLK, 2 * BLK), 1)
    rel = BLK + i - j
    large = jnp.full_like(rel, REL_EXACT)
    for t in BUCKET_THRESHOLDS:
        large = large + (rel >= t).astype(jnp.int32)
    bucket = jnp.where(rel < REL_EXACT, rel, large)
    visible = jnp.logical_and(rel >= 0, rel < BLK)
    return bucket, visible


def _bias_build(rel_bias, name):
    def body(rb_ref, o_ref):
        bucket, visible = _band_tiles()

        def per_head(h, carry):
            acc = jnp.zeros((BLK, 2 * BLK), F32)
            for b in range(REL_BUCKETS):
                acc = jnp.where(bucket == b, rb_ref[b, h], acc)
            o_ref[h] = jnp.where(visible, acc, NEG)
            return carry

        lax.fori_loop(0, N_HEADS, per_head, 0)

    return _pcall(body, name=name, grid=(1,),
                  in_specs=[pl.BlockSpec(memory_space=pltpu.SMEM)],
                  out_specs=pl.BlockSpec((N_HEADS, BLK, 2 * BLK), lambda i: (0, 0, 0)),
                  out_shape=_sds((N_HEADS, BLK, 2 * BLK), F32))(rel_bias)


def _bias_bwd(dbias, name):
    def body(db_ref, o_ref):
        bucket, _ = _band_tiles()
        lane = lax.broadcasted_iota(jnp.int32, (N_HEADS, 128), 1)

        def per_bucket(b, out):
            mb = (bucket == b).astype(F32)
            per_col = jnp.sum(db_ref[...] * mb[None, :, :], axis=1)
            return jnp.where(lane == b, jnp.sum(per_col, axis=1, keepdims=True), out)

        o_ref[...] = lax.fori_loop(0, REL_BUCKETS, per_bucket, jnp.zeros((N_HEADS, 128), F32))

    return _pcall(body, name=name, grid=(1,),
                  in_specs=[pl.BlockSpec((N_HEADS, BLK, 2 * BLK), lambda i: (0, 0, 0))],
                  out_specs=pl.BlockSpec((N_HEADS, 128), lambda i: (0, 0)),
                  out_shape=_sds((N_HEADS, 128), F32))(dbias)


def _kv_group(kp_ref, kc_ref, g):
    ks = slice(HEAD * g, HEAD * (g + 1))
    vs = slice(N_KV * HEAD + HEAD * g, N_KV * HEAD + HEAD * (g + 1))
    kg = jnp.concatenate([kp_ref[:, ks], kc_ref[:, ks]], axis=0)
    vg = jnp.concatenate([kp_ref[:, vs], kc_ref[:, vs]], axis=0)
    return kg, vg


def _swa_fwd(pb, biasm, sinks, name, carry=None):
    _, s, _ = pb.shape
    nb = s // BLK
    kvw = 2 * N_KV * HEAD

    def body(q_ref, kc_ref, kp_ref, b_ref, sk_ref, o_ref, lse_ref):
        n = pl.program_id(0)
        col = lax.broadcasted_iota(jnp.int32, (BLK, 2 * BLK), 1)
        no_prev = jnp.logical_and(n == 0, col < BLK)
        lane = lax.broadcasted_iota(jnp.int32, (BLK, 128), 1)
        lse_t = jnp.zeros((BLK, 128), F32)
        for g in range(N_KV):
            kg, vg = _kv_group(kp_ref, kc_ref, g)
            for r in range(N_HEADS // N_KV):
                h = g * (N_HEADS // N_KV) + r
                hs = slice(HEAD * h, HEAD * (h + 1))
                sc = _dot_nt(q_ref[:, hs], kg) * (HEAD ** -0.5) + b_ref[h]
                sc = jnp.where(no_prev, NEG, sc)
                sk = sk_ref[0, h]
                m = jnp.maximum(jnp.max(sc, axis=1, keepdims=True), sk)
                p = jnp.exp(sc - m)
                l = jnp.sum(p, axis=1, keepdims=True) + jnp.exp(sk - m)
                o = _dot(p.astype(BF16), vg) * (1.0 / l)
                o_ref[:, hs] = o.astype(BF16)
                lse_t = jnp.where(lane == h, m + jnp.log(l), lse_t)
        lse_ref[...] = lse_t

    call = _pcall(body, name=name, grid=(nb,),
                  in_specs=[pl.BlockSpec((None, BLK, D), lambda n: (0, n, 0)),
                            pl.BlockSpec((None, BLK, kvw), lambda n: (1, n, 0)),
                            pl.BlockSpec((None, BLK, kvw), lambda n: (1, jnp.maximum(n - 1, 0), 0)),
                            pl.BlockSpec((N_HEADS, BLK, 2 * BLK), lambda n: (0, 0, 0)),
                            pl.BlockSpec(memory_space=pltpu.SMEM)],
                  out_specs=[pl.BlockSpec((BLK, D), lambda n: (n, 0)), pl.BlockSpec((BLK, 128), lambda n: (n, 0))],
                  out_shape=[_sds((s, D), BF16), _sds((s, 128), F32)], carry=carry)
    return _carried(call, (pb, pb, pb, biasm, sinks), carry)


def _swa_bwd(pb, attn, dattn, lse, biasm, sinks, name, carry=None):
    _, s, _ = pb.shape
    nb = s // BLK
    kvw = 2 * N_KV * HEAD
    grp = N_HEADS // N_KV

    def body(q_ref, kc_ref, kp_ref, o_ref, do_ref, lse_ref, b_ref, sk_ref, dpb_ref, dbias_ref, dsk_ref,
             dq_hold, kv_hold, dq_new, kv_prev, kv_cur):
        n = pl.program_id(0)

        @pl.when(n == 0)
        def _():
            dbias_ref[...] = jnp.zeros_like(dbias_ref)
            dsk_ref[...] = jnp.zeros_like(dsk_ref)
            dq_hold[...] = jnp.zeros_like(dq_hold)
            kv_hold[...] = jnp.zeros_like(kv_hold)

        @pl.when(n < nb)
        def _():
            col = lax.broadcasted_iota(jnp.int32, (BLK, 2 * BLK), 1)
            no_prev = jnp.logical_and(n == 0, col < BLK)
            lane = lax.broadcasted_iota(jnp.int32, (1, 128), 1)
            dsk = jnp.zeros((1, 128), F32)
            for g in range(N_KV):
                kg, vg = _kv_group(kp_ref, kc_ref, g)
                dk_g = jnp.zeros((2 * BLK, HEAD), F32)
                dv_g = jnp.zeros((2 * BLK, HEAD), F32)
                for r in range(grp):
                    h = g * grp + r
                    hs = slice(HEAD * h, HEAD * (h + 1))
                    qh = q_ref[:, hs]
                    dob = do_ref[:, hs]
                    lse_h = lse_ref[:, h:h + 1]
                    sc = _dot_nt(qh, kg) * (HEAD ** -0.5) + b_ref[h]
                    sc = jnp.where(no_prev, NEG, sc)
                    p = jnp.exp(sc - lse_h)
                    dp = _dot_nt(dob, vg)
                    delta = jnp.sum(dob.astype(F32) * o_ref[:, hs].astype(F32), axis=1, keepdims=True)
                    ds = p * (dp - delta)
                    dbias_ref[h] += ds
                    p_sink = jnp.exp(sk_ref[0, h] - lse_h)
                    dsk = jnp.where(lane == h, dsk - jnp.sum(p_sink * delta, keepdims=True), dsk)
                    dsb = (ds * (HEAD ** -0.5)).astype(BF16)
                    dq_new[:, hs] = _dot(dsb, kg)
                    dk_g = dk_g + _dot_tn(dsb, qh)
                    dv_g = dv_g + _dot_tn(p.astype(BF16), dob)
                ks = slice(HEAD * g, HEAD * (g + 1))
                vs = slice(N_KV * HEAD + HEAD * g, N_KV * HEAD + HEAD * (g + 1))
                kv_prev[:, ks] = dk_g[:BLK]
                kv_cur[:, ks] = dk_g[BLK:]
                kv_prev[:, vs] = dv_g[:BLK]
                kv_cur[:, vs] = dv_g[BLK:]
            dsk_ref[...] += dsk

        @pl.when(n == nb)
        def _():
            kv_prev[...] = jnp.zeros_like(kv_prev)

        dpb_ref[0] = dq_hold[...].astype(BF16)
        dpb_ref[1, :, 0:kvw] = (kv_hold[...] + kv_prev[...]).astype(BF16)
        dpb_ref[1, :, kvw:D] = jnp.zeros((BLK, D - kvw), BF16)

        @pl.when(n < nb)
        def _():
            dq_hold[...] = dq_new[...]
            kv_hold[...] = kv_cur[...]

    def cur(n):
        return jnp.minimum(n, nb - 1)

    call = _pcall(body, name=name, grid=(nb + 1,), carry=carry,
                  in_specs=[pl.BlockSpec((None, BLK, D), lambda n: (0, cur(n), 0)),
                            pl.BlockSpec((None, BLK, kvw), lambda n: (1, cur(n), 0)),
                            pl.BlockSpec((None, BLK, kvw), lambda n: (1, jnp.maximum(cur(n) - 1, 0), 0)),
                            pl.BlockSpec((BLK, D), lambda n: (cur(n), 0)),
                            pl.BlockSpec((BLK, D), lambda n: (cur(n), 0)),
                            pl.BlockSpec((BLK, 128), lambda n: (cur(n), 0)),
                            pl.BlockSpec((N_HEADS, BLK, 2 * BLK), lambda n: (0, 0, 0)),
                            pl.BlockSpec(memory_space=pltpu.SMEM)],
                  out_specs=[pl.BlockSpec((2, BLK, D), lambda n: (0, jnp.maximum(n - 1, 0), 0)),
                             pl.BlockSpec((N_HEADS, BLK, 2 * BLK), lambda n: (0, 0, 0)),
                             pl.BlockSpec((1, 128), lambda n: (0, 0))],
                  out_shape=[_sds((2, s, D), BF16), _sds((N_HEADS, BLK, 2 * BLK), F32), _sds((1, 128), F32)],
                  scratch=[pltpu.VMEM((BLK, D), F32), pltpu.VMEM((BLK, kvw), F32), pltpu.VMEM((BLK, D), F32),
                           pltpu.VMEM((BLK, kvw), F32), pltpu.VMEM((BLK, kvw), F32)])
    return _carried(call, (pb, pb, pb, attn, dattn, lse, biasm, sinks), carry)


HALO = 16
CW = 512


def _conv_taps(cu, halo_cu, first_tile):
    row = lax.broadcasted_iota(jnp.int32, cu.shape, 0)
    halo_cu = jnp.where(first_tile, 0.0, halo_cu)
    c1 = jnp.where(row == 0, halo_cu[HALO - 1:HALO], pltpu.roll(cu, 1, 0))
    c2 = jnp.where(row == 0, halo_cu[HALO - 2:HALO - 1],
                   jnp.where(row == 1, halo_cu[HALO - 1:HALO], pltpu.roll(cu, 2, 0)))
    return c1, c2


def _conv_merge_fwd(pa, attn, convw, name, ts=512):
    _, s, _ = pa.shape
    ts = min(ts, s)
    hb = ts // HALO

    def body(pa_ref, hp_ref, at_ref, w_ref, o_ref):
        i = pl.program_id(1)
        cu = pa_ref[0].astype(F32) * pa_ref[2].astype(F32)
        c1, c2 = _conv_taps(cu, hp_ref[0].astype(F32) * hp_ref[2].astype(F32), i == 0)
        w = w_ref[...]
        c3 = w[0:1] * c2 + w[1:2] * c1 + w[2:3] * cu
        conv = pa_ref[1].astype(F32) * c3
        o_ref[...] = (jax.nn.sigmoid(pa_ref[3].astype(F32)) * at_ref[...].astype(F32)
                      + jax.nn.sigmoid(pa_ref[4].astype(F32)) * conv).astype(BF16)

    return _pcall(body, name=name, grid=(D // CW, s // ts),
                  in_specs=[pl.BlockSpec((5, ts, CW), lambda c, i: (0, i, c)),
                            pl.BlockSpec((5, HALO, CW), lambda c, i: (0, jnp.maximum(i * hb - 1, 0), c)),
                            pl.BlockSpec((ts, CW), lambda c, i: (i, c)),
                            pl.BlockSpec((8, CW), lambda c, i: (0, c))],
                  out_specs=pl.BlockSpec((ts, CW), lambda c, i: (i, c)),
                  out_shape=_sds((s, D), BF16))(pa, pa, attn, convw)


def _conv_merge_bwd(dmerged, pa, attn, convw, name, ts=512, carry=None):
    _, s, _ = pa.shape
    ts = min(ts, s)
    hb = ts // HALO
    last_hb = s // HALO - 1

    def body(dm_ref, pa_ref, at_ref, w_ref, hp_ref, hn_ref, dmn_ref, dat_ref, dpa_ref, dw_ref):
        i = pl.program_id(1)
        last = i == pl.num_programs(1) - 1
        dm = dm_ref[...].astype(F32)
        cp, bp, u = pa_ref[0].astype(F32), pa_ref[1].astype(F32), pa_ref[2].astype(F32)
        sa = jax.nn.sigmoid(pa_ref[3].astype(F32))
        sc = jax.nn.sigmoid(pa_ref[4].astype(F32))
        at = at_ref[...].astype(F32)
        cu = cp * u
        c1, c2 = _conv_taps(cu, hp_ref[0].astype(F32) * hp_ref[2].astype(F32), i == 0)
        w = w_ref[...]
        c3 = w[0:1] * c2 + w[1:2] * c1 + w[2:3] * cu
        dconv = dm * sc
        dc3 = dconv * bp
        nxt = dmn_ref[...].astype(F32) * jax.nn.sigmoid(hn_ref[4].astype(F32)) * hn_ref[1].astype(F32)
        nxt = jnp.where(last, 0.0, nxt)
        row = lax.broadcasted_iota(jnp.int32, dc3.shape, 0)
        d1 = jnp.where(row == ts - 1, nxt[0:1], pltpu.roll(dc3, ts - 1, 0))
        d2 = jnp.where(row == ts - 2, nxt[0:1], jnp.where(row == ts - 1, nxt[1:2], pltpu.roll(dc3, ts - 2, 0)))
        dcu = w[2:3] * dc3 + w[1:2] * d1 + w[0:1] * d2
        dat_ref[...] = (dm * sa).astype(BF16)
        dpa_ref[0] = (dcu * u).astype(BF16)
        dpa_ref[1] = (dconv * c3).astype(BF16)
        dpa_ref[2] = (dcu * cp).astype(BF16)
        dpa_ref[3] = (dm * at * sa * (1.0 - sa)).astype(BF16)
        dpa_ref[4] = (dm * bp * c3 * sc * (1.0 - sc)).astype(BF16)

        @pl.when(i == 0)
        def _():
            dw_ref[...] = jnp.zeros_like(dw_ref)

        dw_ref[0:1, :] += jnp.sum(dc3 * c2, axis=0, keepdims=True)
        dw_ref[1:2, :] += jnp.sum(dc3 * c1, axis=0, keepdims=True)
        dw_ref[2:3, :] += jnp.sum(dc3 * cu, axis=0, keepdims=True)

    call = _pcall(body, name=name, grid=(D // CW, s // ts), carry=carry,
                  in_specs=[pl.BlockSpec((ts, CW), lambda c, i: (i, c)),
                            pl.BlockSpec((5, ts, CW), lambda c, i: (0, i, c)),
                            pl.BlockSpec((ts, CW), lambda c, i: (i, c)),
                            pl.BlockSpec((8, CW), lambda c, i: (0, c)),
                            pl.BlockSpec((5, HALO, CW), lambda c, i: (0, jnp.maximum(i * hb - 1, 0), c)),
                            pl.BlockSpec((5, HALO, CW), lambda c, i: (0, jnp.minimum((i + 1) * hb, last_hb), c)),
                            pl.BlockSpec((HALO, CW), lambda c, i: (jnp.minimum((i + 1) * hb, last_hb), c))],
                  out_specs=[pl.BlockSpec((ts, CW), lambda c, i: (i, c)),
                             pl.BlockSpec((5, ts, CW), lambda c, i: (0, i, c)),
                             pl.BlockSpec((8, CW), lambda c, i: (0, c))],
                  out_shape=[_sds((s, D), BF16), _sds((5, s, D), BF16), _sds((8, D), F32)])
    return _carried(call, (dmerged, pa, attn, convw, pa, pa, dmerged), carry)


def _xattn_fwd(q, kv, name, tq=512):
    s, _ = q.shape
    nm = kv.shape[0]
    tq = min(tq, s)

    def body(q_ref, kv_ref, o_ref, lse_ref):
        lane = lax.broadcasted_iota(jnp.int32, (tq, 128), 1)
        lse_t = jnp.zeros((tq, 128), F32)
        for h in range(XH):
            hs = slice(XHD * h, XHD * (h + 1))
            vs = slice(D + XHD * h, D + XHD * (h + 1))
            sc = _dot_nt(q_ref[:, hs], kv_ref[:, hs]) * (XHD ** -0.5)
            m = jnp.max(sc, axis=1, keepdims=True)
            p = jnp.exp(sc - m)
            l = jnp.sum(p, axis=1, keepdims=True)
            o_ref[:, hs] = (_dot(p.astype(BF16), kv_ref[:, vs]) * (1.0 / l)).astype(BF16)
            lse_t = jnp.where(lane == h, m + jnp.log(l), lse_t)
        lse_ref[...] = lse_t

    return _pcall(body, name=name, grid=(s // tq,),
                  in_specs=[pl.BlockSpec((tq, D), lambda i: (i, 0)), pl.BlockSpec((nm, 2 * D), lambda i: (0, 0))],
                  out_specs=[pl.BlockSpec((tq, D), lambda i: (i, 0)), pl.BlockSpec((tq, 128), lambda i: (i, 0))],
                  out_shape=[_sds((s, D), BF16), _sds((s, 128), F32)])(q, kv)


def _xattn_bwd(q, kv, o, do, lse, name, tq=512, carry=None):
    s, _ = q.shape
    nm = kv.shape[0]
    tq = min(tq, s)

    def body(q_ref, kv_ref, o_ref, do_ref, lse_ref, dq_ref, dkv_ref):
        @pl.when(pl.program_id(0) == 0)
        def _():
            dkv_ref[...] = jnp.zeros_like(dkv_ref)

        for h in range(XH):
            hs = slice(XHD * h, XHD * (h + 1))
            vs = slice(D + XHD * h, D + XHD * (h + 1))
            qh, kh, vh, dob = q_ref[:, hs], kv_ref[:, hs], kv_ref[:, vs], do_ref[:, hs]
            p = jnp.exp(_dot_nt(qh, kh) * (XHD ** -0.5) - lse_ref[:, h:h + 1])
            dp = _dot_nt(dob, vh)
            delta = jnp.sum(dob.astype(F32) * o_ref[:, hs].astype(F32), axis=1, keepdims=True)
            dsb = (p * (dp - delta) * (XHD ** -0.5)).astype(BF16)
            dq_ref[:, hs] = _dot(dsb, kh).astype(BF16)
            dkv_ref[:, hs] += _dot_tn(dsb, qh)
            dkv_ref[:, vs] += _dot_tn(p.astype(BF16), dob)

    call = _pcall(body, name=name, grid=(s // tq,), carry=carry,
                  in_specs=[pl.BlockSpec((tq, D), lambda i: (i, 0)), pl.BlockSpec((nm, 2 * D), lambda i: (0, 0)),
                            pl.BlockSpec((tq, D), lambda i: (i, 0)), pl.BlockSpec((tq, D), lambda i: (i, 0)),
                            pl.BlockSpec((tq, 128), lambda i: (i, 0))],
                  out_specs=[pl.BlockSpec((tq, D), lambda i: (i, 0)), pl.BlockSpec((nm, 2 * D), lambda i: (0, 0))],
                  out_shape=[_sds((s, D), BF16), _sds((nm, 2 * D), F32)])
    return _carried(call, (q, kv, o, do, lse), carry)


def _ffn_down_bwd(dxb, wd4, gu4, name, tm=512, carry=None):
    s, _ = dxb.shape
    tm = min(tm, s)

    def body(dx_ref, w_ref, gu_ref, o_ref):
        da = 0.5 * _dot_nt(dx_ref[...], w_ref[...])
        g = gu_ref[0].astype(F32)
        u = gu_ref[1].astype(F32)
        sg = jax.nn.sigmoid(g)
        o_ref[0] = (da * u * sg * (1.0 + g * (1.0 - sg))).astype(BF16)
        o_ref[1] = (da * g * sg).astype(BF16)

    call = _pcall(body, name=name, grid=(4, s // tm), carry=carry,
                  in_specs=[pl.BlockSpec((tm, D), lambda p, i: (i, 0)),
                            pl.BlockSpec((None, FS, D), lambda p, i: (p, 0, 0)),
                            pl.BlockSpec((2, None, tm, FS), lambda p, i: (0, p, i, 0))],
                  out_specs=pl.BlockSpec((2, None, tm, FS), lambda p, i: (0, p, i, 0)),
                  out_shape=_sds((2, 4, s, FS), BF16))
    return _carried(call, (dxb, wd4, gu4), carry)


def _mm_tn(a, b, name, scale=1.0, tk=1024, tn=None, carry=None):
    pa_n, s, m = a.shape
    pb_n, _, n = b.shape
    po = max(pa_n, pb_n)
    tk = min(tk, s)
    tn = n if tn is None else tn
    nk = s // tk

    def body(a_ref, b_ref, o_ref, acc_ref):
        k = pl.program_id(2)

        @pl.when(k == 0)
        def _():
            acc_ref[...] = jnp.zeros_like(acc_ref)

        acc_ref[...] += _dot_tn(a_ref[...], b_ref[...])

        @pl.when(k == nk - 1)
        def _():
            o_ref[...] = (scale * acc_ref[...]).astype(BF16)

    call = _pcall(body, name=name, grid=(po, n // tn, nk), carry=carry,
                  in_specs=[pl.BlockSpec((None, tk, m), lambda o, j, k: (o if pa_n > 1 else 0, k, 0)),
                            pl.BlockSpec((None, tk, tn), lambda o, j, k: (o if pb_n > 1 else 0, k, j))],
                  out_specs=pl.BlockSpec((None, m, tn), lambda o, j, k: (o, 0, j)),
                  out_shape=_sds((po, m, n), BF16), scratch=[pltpu.VMEM((m, tn), F32)])
    return _carried(call, (a, b), carry)


def _mm_nt(a, b, name, out_dtype, tm=512, carry=None):
    nj, s, k = a.shape
    n = b.shape[1]
    tm = min(tm, s)

    def body(a_ref, b_ref, o_ref, acc_ref):
        j = pl.program_id(1)

        @pl.when(j == 0)
        def _():
            acc_ref[...] = jnp.zeros_like(acc_ref)

        acc_ref[...] += _dot_nt(a_ref[...], b_ref[...])

        @pl.when(j == nj - 1)
        def _():
            o_ref[...] = acc_ref[...].astype(out_dtype)

    call = _pcall(body, name=name, grid=(s // tm, nj), carry=carry,
                  in_specs=[pl.BlockSpec((None, tm, k), lambda i, j: (j, i, 0)),
                            pl.BlockSpec((None, n, k), lambda i, j: (j, 0, 0))],
                  out_specs=pl.BlockSpec((tm, n), lambda i, j: (i, 0)),
                  out_shape=_sds((s, n), out_dtype), scratch=[pltpu.VMEM((tm, n), F32)])
    return _carried(call, (a, b), carry)


def _mm_nt_rms_bwd(a, b, name, *, x, gain, dres, addend=None, tm=512, carry=None):
    nj, s, k = a.shape
    n = b.shape[1]
    tm = min(tm, s)
    has_add = addend is not None

    def body(*refs):
        a_ref, b_ref, x_ref, g_ref, r_ref = refs[:5]
        add_ref = refs[5] if has_add else None
        dx_ref, dxb_ref, dg_ref, acc_ref = refs[5 + has_add:]
        i, j = pl.program_id(0), pl.program_id(1)

        @pl.when(j == 0)
        def _():
            acc_ref[...] = jnp.zeros_like(acc_ref)

        @pl.when(jnp.logical_and(i == 0, j == 0))
        def _():
            dg_ref[...] = jnp.zeros_like(dg_ref)

        acc_ref[...] += _dot_nt(a_ref[...], b_ref[...])

        @pl.when(j == nj - 1)
        def _():
            dh = acc_ref[...]
            if has_add:
                dh = dh + add_ref[...]
            xv = x_ref[...]
            r = lax.rsqrt(jnp.mean(xv * xv, axis=-1, keepdims=True) + EPS)
            xh = xv * r
            dyg = dh * g_ref[...]
            dx = r_ref[...] + r * (dyg - xh * jnp.mean(dyg * xh, axis=-1, keepdims=True))
            dx_ref[...] = dx
            dxb_ref[...] = dx.astype(BF16)
            dg_ref[...] += jnp.sum(dh * xh, axis=0, keepdims=True)

    row = pl.BlockSpec((tm, n), lambda i, j: (i, 0))
    in_specs = [pl.BlockSpec((None, tm, k), lambda i, j: (j, i, 0)),
                pl.BlockSpec((None, n, k), lambda i, j: (j, 0, 0)),
                row, pl.BlockSpec((1, n), lambda i, j: (0, 0)), row] + ([row] if has_add else [])
    args = (a, b, x, gain, dres) + ((addend,) if has_add else ())
    call = _pcall(body, name=name, grid=(s // tm, nj), in_specs=in_specs, carry=carry,
                  out_specs=[row, row, pl.BlockSpec((1, n), lambda i, j: (0, 0))],
                  out_shape=[_sds((s, n), F32), _sds((s, n), BF16), _sds((1, n), F32)],
                  scratch=[pltpu.VMEM((tm, n), F32)])
    return _carried(call, args, carry)


def _adam(w, g, m, v):
    m2 = ADAM_B1 * m + (1.0 - ADAM_B1) * g
    v2 = ADAM_B2 * v + (1.0 - ADAM_B2) * (g * g)
    m_hat = m2 / (1.0 - ADAM_B1 ** ADAM_STEP)
    v_hat = v2 / (1.0 - ADAM_B2 ** ADAM_STEP)
    delta = -ADAM_LR * (m_hat / (jnp.sqrt(v_hat) + ADAM_EPS) + ADAM_WD * w)
    return delta, m2, v2


def _adamw(parts, w, m, v, name):
    _, r, c = parts.shape
    tr = r if r <= 256 else (176 if r == 352 else 256)

    def body(p_ref, w_ref, m_ref, v_ref, g_ref, d_ref, m2_ref, v2_ref):
        g = p_ref[0].astype(F32)
        for i in range(1, N_DEV):
            g = g + p_ref[i].astype(F32)
        delta, m2, v2 = _adam(w_ref[...], g, m_ref[...], v_ref[...])
        g_ref[...] = g
        d_ref[...] = delta
        m2_ref[...] = m2
        v2_ref[...] = v2

    blk = pl.BlockSpec((tr, c), lambda i: (i, 0))
    return _pcall(body, name=name, grid=(r // tr,),
                  in_specs=[pl.BlockSpec((N_DEV, tr, c), lambda i: (0, i, 0)), blk, blk, blk],
                  out_specs=[blk] * 4, out_shape=[_sds((r, c), F32)] * 4)(parts, w, m, v)


def _position():
    return lax.axis_index("x"), lax.axis_index("y"), lax.axis_index("c")


def _slot(px, py, pc):
    return 4 * px + 2 * py + pc


def _row_window(ref, rows):
    r0, r1 = rows
    return ref if (r0, r1) == (0, ref.shape[0]) else ref.at[pl.ds(r0, r1 - r0)]


def _split_items(items):
    sources = [src for src, _, _ in items]
    begun = [(a, dest) for a, (_, _, dest) in enumerate(items) if dest is not None]
    aliases = {len(sources) + k: a for k, (a, _) in enumerate(begun)}
    return sources + [dest for _, dest in begun], [rows for _, rows, _ in items], aliases


def _gather_carry(items):
    na = len(items)
    carry_ins, windows, aliases = _split_items(items)

    def plan(ins, outs, sems):
        send_sems, recv_sems, local_sems = sems
        x, y, c = _position()
        me, sibling = (x, y, c), (x, y, 1 - c)
        chips = [(1 - x, y), (x, 1 - y), (1 - x, 1 - y)]
        ins = [_row_window(ins[a], windows[a]) for a in range(na)]

        def block_rows(a, block):
            return _row_window(outs[a].at[_slot(*block)], windows[a])

        def copy(a, k, block, to, src=None):
            rows = block_rows(a, block)
            return pltpu.make_async_remote_copy(src_ref=rows if src is None else src, dst_ref=rows,
                                                send_sem=send_sems.at[k, a], recv_sem=recv_sems.at[k, a],
                                                device_id=to, device_id_type=MESH)

        mine = [pltpu.make_async_copy(ins[a], block_rows(a, me), local_sems.at[a]) for a in range(na)]
        first = [copy(a, 0, me, sibling, src=ins[a]) for a in range(na)]
        for j, chip in enumerate(chips):
            first += [copy(a, 1 + j, me, (*chip, c), src=ins[a]) for a in range(na)]
        landed = [[copy(a, 1 + j, (*chip, c), me) for a in range(na)] for j, chip in enumerate(chips)]
        passed = [[copy(a, 4 + j, (*chip, c), sibling) for a in range(na)] for j, chip in enumerate(chips)]
        from_sibling = [copy(a, 0, sibling, me) for a in range(na)]
        for j, chip in enumerate(chips):
            from_sibling += [copy(a, 4 + j, (*chip, 1 - c), me) for a in range(na)]
        return mine, first, landed, passed, from_sibling

    def start(ins, outs, sems):
        mine, first, _, _, _ = plan(ins, outs, sems)
        for cp in mine + first:
            cp.start()

    def mid(ins, outs, sems):
        _, _, landed, passed, _ = plan(ins, outs, sems)
        for over_ici, onward in zip(landed, passed):
            for cp, fwd in zip(over_ici, onward):
                cp.wait_recv()
                fwd.start()

    def finish(ins, outs, sems):
        mine, first, _, passed, from_sibling = plan(ins, outs, sems)
        for cp in from_sibling:
            cp.wait_recv()
        for cp in first + [fwd for onward in passed for fwd in onward]:
            cp.wait_send()
        for cp in mine:
            cp.wait()

    return _Carry(carry_ins, [_sds((N_DEV,) + src.shape, src.dtype) for src, _, _ in items],
                  [pltpu.SemaphoreType.DMA((7, na)), pltpu.SemaphoreType.DMA((7, na)),
                   pltpu.SemaphoreType.DMA((na,))], start, finish, mid, aliases)


def _exchange_carry(scattered, replicated=()):
    items = list(scattered) + [(a, (0, a.shape[0]), None) for a in replicated]
    na, ns = len(items), len(scattered)
    carry_ins, windows, aliases = _split_items(items)

    def plan(ins, outs, sems):
        send_sems, recv_sems, local_sems = sems
        me = _slot(*_position())

        def source(a, j):
            return _row_window(ins[a].at[j] if a < ns else ins[a], windows[a])

        def copy(a, j, i):
            return pltpu.make_async_remote_copy(src_ref=source(a, j), dst_ref=_row_window(outs[a].at[i], windows[a]),
                                                send_sem=send_sems.at[j, a], recv_sem=recv_sems.at[i, a],
                                                device_id=(j >> 2, (j >> 1) & 1, j & 1), device_id_type=MESH)

        mine = [pltpu.make_async_copy(source(a, me), _row_window(outs[a].at[me], windows[a]), local_sems.at[a])
                for a in range(na)]
        return me, copy, mine

    def start(ins, outs, sems):
        me, copy, mine = plan(ins, outs, sems)
        for cp in mine:
            cp.start()
        for j in range(N_DEV):
            @pl.when(me != j)
            def _():
                for a in range(na):
                    copy(a, j, me).start()

    def finish(ins, outs, sems):
        me, copy, mine = plan(ins, outs, sems)
        for i in range(N_DEV):
            @pl.when(me != i)
            def _():
                for a in range(na):
                    copy(a, i, i).wait_recv()
        for j in range(N_DEV):
            @pl.when(me != j)
            def _():
                for a in range(na):
                    copy(a, j, me).wait_send()
        for cp in mine:
            cp.wait()

    return _Carry(carry_ins, [_sds((N_DEV,) + src.shape[-2:], src.dtype) for src, _, _ in items],
                  [pltpu.SemaphoreType.DMA((N_DEV, na)), pltpu.SemaphoreType.DMA((N_DEV, na)),
                   pltpu.SemaphoreType.DMA((na,))], start, finish, None, aliases)


NQ, NKV = N_HEADS * HEAD, 2 * N_KV * HEAD


class _Mesh:
    def __init__(self, shards):
        self.shards, self.full, self.received, self.cache = shards, {}, {}, {}

    def fetch(self, wanted):
        items = []
        for want in wanted:
            name, r0, r1 = want if isinstance(want, tuple) else (want, 0, self.shards[want].shape[0])
            items.append((self.shards[name], (r0, r1), self.full.get(name)))
        return _gather_carry(items)

    def fetched(self, wanted, results):
        self.full.update(zip([want[0] if isinstance(want, tuple) else want for want in wanted], results))

    def send(self, *payloads):
        return _exchange_carry([(parts, rows or (0, parts.shape[1]), self.received.get(name))
                                for name, parts, rows in payloads])

    def sent(self, names, results):
        self.received.update(zip(names, results))

    def w(self, key):
        if key not in self.cache:
            self.cache[key] = self._layout(key)
        return self.cache[key]

    def _layout(self, key):
        if key in ("gu1", "gu2"):
            return self.full[key]
        if key in ("d1", "d2"):
            return self.full[key].reshape(4, FS, D)
        if key in ("out", "q", "o"):
            return self.full[key].reshape(D, D)
        if key == "kv":
            return self.full["kv"].transpose(1, 0, 2).reshape(D, 2 * D)
        if key == "convw":
            rows = self.full["conv"][:, :3, :].transpose(1, 0, 2).reshape(3, D)
            return jnp.concatenate([rows, jnp.zeros((5, D), F32)], axis=0)
        w_in = self.full["win"].transpose(1, 0, 2).reshape(D, -1)
        if key == "wa":
            return w_in[:, NQ + NKV:].reshape(D, 5, D).transpose(1, 0, 2)
        assert key == "wb", key
        return jnp.stack([w_in[:, :NQ], jnp.pad(w_in[:, NQ:NQ + NKV], ((0, 0), (0, D - NKV)))])


def _w_in_parts(dw_a, dw_b):
    full = jnp.concatenate([dw_b[0], dw_b[1][:, :NKV], dw_a.transpose(1, 0, 2).reshape(D, 5 * D)], axis=1)
    return full.reshape(D, N_DEV, -1).transpose(1, 0, 2)


def _forward_backward(x, mem, target, g, rel_bias, sinks, ex):
    s = x.shape[0]
    def fetching(wanted, call, *args, **kw):
        res, got = call(*args, carry=ex.fetch(wanted), **kw)
        ex.fetched(wanted, got)
        return res

    h1 = fetching(["gu1", "d1", "conv"], _rmsnorm, x, g["ffn1"], "norm_ffn1")
    gu1, a1 = fetching([("win", 0, 896)], _ffn_up, h1, ex.w("gu1").reshape(2, 4, D, FS), "ffn1_up")
    x1, h2 = fetching([("win", 896, D), "out", "q"], _mm_res_norm, a1, ex.w("d1"), x, g["mix"], 0.5, "ffn1_down")
    pa = fetching(["gu2"], _mm_nn, h2, ex.w("wa"), "in_proj_a")
    pb = fetching(["o"], _mm_nn, h2, ex.w("wb"), "in_proj_b")
    biasm = _bias_build(rel_bias, "bias_build")
    attn, lse = fetching(["kv", "d2"], _swa_fwd, pb, biasm, sinks, "swa_fwd")
    merged = _conv_merge_fwd(pa, attn, ex.w("convw"), "conv_merge_fwd")
    (x2, h3), _ = _mm_res_norm(merged[None], ex.w("out")[None], x1, g["xattn"], 1.0, "out_proj")
    q2 = _mm_nn(h3, ex.w("q")[None], "xattn_q")[0][0]
    mh, _ = _rmsnorm(mem, g["mem"], "norm_mem")
    kv2 = _mm_nn(mh, ex.w("kv")[None], "xattn_kv")[0][0]
    o, lse2 = _xattn_fwd(q2, kv2, "xattn_fwd")
    (x3, h4), _ = _mm_res_norm(o[None], ex.w("o")[None], x2, g["ffn2"], 1.0, "xattn_o")
    (gu2, a2), _ = _ffn_up(h4, ex.w("gu2").reshape(2, 4, D, FS), "ffn2_up")
    dx4, dx4b, loss, d_final = _ffn_down_loss(a2, ex.w("d2"), x3, g["final"], target, "ffn2_down_loss")
    def sending(payloads, call, *args, **kw):
        res, got = call(*args, carry=ex.send(*payloads), **kw)
        ex.sent([name for name, _, _ in payloads], got)
        return res

    dw_d2 = _mm_tn(a2, dx4b[None], "dw_ffn2_down", scale=0.5)[0].reshape(N_DEV, -1, D)
    dgu2 = sending([("d2", dw_d2, None)], _ffn_down_bwd, dx4b, ex.w("d2"), gu2, "ffn2_down_bwd").reshape(8, s, FS)
    dw_gu2 = _mm_tn(h4[None], dgu2, "dw_ffn2_up")[0]
    dx3, dx3b, d_ffn2 = sending([("gu2", dw_gu2, (0, 896))], _mm_nt_rms_bwd, dgu2, ex.w("gu2"), "ffn2_up_bwd",
                                x=x3, gain=g["ffn2"], dres=dx4)
    do = sending([("gu2", dw_gu2, (896, D))], _mm_nt, dx3b[None], ex.w("o")[None], "xattn_o_bwd", BF16)
    dw_o = _mm_tn(o[None], dx3b[None], "dw_xattn_o")[0].reshape(N_DEV, -1, D)
    dq2, dkv2 = sending([("o", dw_o, None)], _xattn_bwd, q2, kv2, o, do, lse2, "xattn_bwd")
    dkv2b = dkv2.astype(BF16)
    dw_q = _mm_tn(h3[None], dq2[None], "dw_xattn_q")[0].reshape(N_DEV, -1, D)
    dx2, dx2b, d_xattn = sending([("q", dw_q, None)], _mm_nt_rms_bwd, dq2[None], ex.w("q")[None], "xattn_q_bwd",
                                 x=x2, gain=g["xattn"], dres=dx3)
    dw_kv = _mm_tn(mh[None], dkv2b[None], "dw_xattn_kv")[0][0].reshape(D, N_DEV, -1).transpose(1, 0, 2)
    (_, _, d_mem), _ = _mm_nt_rms_bwd(dkv2b[None], ex.w("kv")[None], "xattn_kv_bwd", x=mem, gain=g["mem"],
                                      dres=jnp.zeros_like(mem))
    dmerged = sending([("kv", dw_kv, (0, 352))], _mm_nt, dx2b[None], ex.w("out")[None], "out_proj_bwd", BF16)
    dw_out = sending([("kv", dw_kv, (352, 768))], _mm_tn, merged[None], dx2b[None], "dw_out_proj").reshape(N_DEV, -1, D)
    dattn, dpa, d_convw = sending([("kv", dw_kv, (768, D)), ("out", dw_out, None)], _conv_merge_bwd,
                                  dmerged, pa, attn, ex.w("convw"), "conv_merge_bwd")
    (dpb, dbias, d_sinks), _ = _swa_bwd(pb, attn, dattn, lse, biasm, sinks, "swa_bwd")
    d_relb = _bias_bwd(dbias, "bias_bwd")
    dw_in = _w_in_parts(_mm_tn(h2[None], dpa, "dw_in_proj_a")[0], _mm_tn(h2[None], dpb, "dw_in_proj_b")[0])
    dh2_b = sending([("win", dw_in, (0, 256))], _mm_nt, dpb, ex.w("wb"), "in_proj_b_bwd", F32)
    dx1, dx1b, d_mix = sending([("win", dw_in, (256, 896))], _mm_nt_rms_bwd, dpa, ex.w("wa"), "in_proj_a_bwd",
                               x=x1, gain=g["mix"], dres=dx2, addend=dh2_b)
    dw_d1 = sending([("win", dw_in, (896, D))], _mm_tn, a1, dx1b[None], "dw_ffn1_down", scale=0.5)
    dw_d1 = dw_d1.reshape(N_DEV, -1, D)
    dgu1 = sending([("d1", dw_d1, None)], _ffn_down_bwd, dx1b, ex.w("d1"), gu1, "ffn1_down_bwd").reshape(8, s, FS)
    dw_gu1 = _mm_tn(h1[None], dgu1, "dw_ffn1_up")[0]
    dx0, _, d_ffn1 = sending([("gu1", dw_gu1, None)], _mm_nt_rms_bwd, dgu1, ex.w("gu1"), "ffn1_up_bwd",
                             x=x, gain=g["ffn1"], dres=dx1)

    relb_row = jnp.concatenate([d_relb[:, :REL_BUCKETS].T.reshape(1, REL_BUCKETS * N_HEADS), d_sinks[:, :N_HEADS],
                                jnp.zeros((1, D - REL_BUCKETS * N_HEADS - N_HEADS), F32)], axis=1)
    loss_row = jnp.concatenate([loss[0:1, 0:1], jnp.zeros((1, D - 1), F32)], axis=1)
    small = jnp.concatenate([d_ffn1, d_mix, d_xattn, d_mem, d_ffn2, d_final, relb_row, loss_row, d_convw[0:3],
                             jnp.zeros((SMALL_ROWS - ROW_CONV - 3, D), F32)], axis=0)
    return dx0, small


def _pack_small(norms, final, relb, sinks, conv_local, me):
    relb_row = jnp.concatenate([relb.reshape(1, -1), sinks.reshape(1, -1),
                                jnp.zeros((1, D - REL_BUCKETS * N_HEADS - N_HEADS), F32)], axis=1)
    conv_rows = lax.dynamic_update_slice(jnp.zeros((3, D), F32), conv_local.reshape(3, -1), (0, 128 * me))
    return jnp.concatenate(list(norms) + [final.reshape(1, D), relb_row, jnp.zeros((1, D), F32), conv_rows,
                                          jnp.zeros((SMALL_ROWS - ROW_CONV - 3, D), F32)], axis=0)


def kernel(x, mem, positions, rel_bias, ffn1_norm, ffn1_w_gu, ffn1_w_down, mix_norm, w_in, sinks, conv_w, w_out, xattn_norm, mem_norm, xattn_wq, xattn_wkv, xattn_wo, ffn2_norm, ffn2_w_gu, ffn2_w_down, final_norm, loss_target, m_rel_bias, m_ffn1_norm, m_ffn1_w_gu, m_ffn1_w_down, m_mix_norm, m_w_in, m_sinks, m_conv_w, m_w_out, m_xattn_norm, m_mem_norm, m_xattn_wq, m_xattn_wkv, m_xattn_wo, m_ffn2_norm, m_ffn2_w_gu, m_ffn2_w_down, m_final_norm, v_rel_bias, v_ffn1_norm, v_ffn1_w_gu, v_ffn1_w_down, v_mix_norm, v_w_in, v_sinks, v_conv_w, v_w_out, v_xattn_norm, v_mem_norm, v_xattn_wq, v_xattn_wkv, v_xattn_wo, v_ffn2_norm, v_ffn2_w_gu, v_ffn2_w_down, v_final_norm):
    del positions
    me = _slot(*_position())
    big = dict(gu1=(ffn1_w_gu, m_ffn1_w_gu, v_ffn1_w_gu), d1=(ffn1_w_down, m_ffn1_w_down, v_ffn1_w_down),
               win=(w_in, m_w_in, v_w_in), out=(w_out, m_w_out, v_w_out), q=(xattn_wq, m_xattn_wq, v_xattn_wq),
               kv=(xattn_wkv, m_xattn_wkv, v_xattn_wkv), o=(xattn_wo, m_xattn_wo, v_xattn_wo),
               gu2=(ffn2_w_gu, m_ffn2_w_gu, v_ffn2_w_gu), d2=(ffn2_w_down, m_ffn2_w_down, v_ffn2_w_down))
    order = list(big)
    shards = {k: big[k][0][0].astype(BF16) for k in order}
    shards["conv"] = jnp.concatenate([conv_w[0], jnp.zeros((5, 128), F32)], axis=0)
    ex = _Mesh(shards)
    gains = dict(ffn1=ffn1_norm, mix=mix_norm, xattn=xattn_norm, mem=mem_norm, ffn2=ffn2_norm,
                 final=final_norm.reshape(1, D))
    dx, small = _forward_backward(x[0], mem[0], loss_target[0], gains, rel_bias, sinks, ex)
    small_parts = _run_alone(_exchange_carry([], [small]), "exchange_small")[0]
    big_out = {k: _adamw(ex.received[k], *(t[0] for t in big[k]), "adamw_" + k) for k in order}
    packed = [_pack_small(norms, final, relb, sk, conv, me) for norms, final, relb, sk, conv in (
        ((ffn1_norm, mix_norm, xattn_norm, mem_norm, ffn2_norm), final_norm, rel_bias, sinks, conv_w),
        ((m_ffn1_norm, m_mix_norm, m_xattn_norm, m_mem_norm, m_ffn2_norm), m_final_norm, m_rel_bias, m_sinks, m_conv_w),
        ((v_ffn1_norm, v_mix_norm, v_xattn_norm, v_mem_norm, v_ffn2_norm), v_final_norm, v_rel_bias, v_sinks, v_conv_w))]
    small_out = _adamw(small_parts, *packed, "adamw_small")

    def unpack(t):
        conv = lax.dynamic_slice(t[ROW_CONV:ROW_CONV + 3], (0, 128 * me), (3, 128))[None]
        nrel = REL_BUCKETS * N_HEADS
        return dict(ffn1_norm=t[0:1], mix_norm=t[1:2], xattn_norm=t[2:3], mem_norm=t[3:4], ffn2_norm=t[4:5],
                    final_norm=t[5], rel_bias=t[ROW_RELB, :nrel].reshape(REL_BUCKETS, N_HEADS),
                    sinks=t[ROW_RELB:ROW_RELB + 1, nrel:nrel + N_HEADS], conv_w=conv)

    names = dict(gu1="ffn1_w_gu", d1="ffn1_w_down", win="w_in", out="w_out", q="xattn_wq", kv="xattn_wkv",
                 o="xattn_wo", gu2="ffn2_w_gu", d2="ffn2_w_down")
    results = []
    for idx in range(4):
        leaves = unpack(small_out[idx])
        leaves.update({names[k]: big_out[k][idx][None] for k in order})
        results.append(leaves)
    weights = ("rel_bias", "ffn1_norm", "ffn1_w_gu", "ffn1_w_down", "mix_norm", "w_in", "sinks", "conv_w", "w_out",
               "xattn_norm", "mem_norm", "xattn_wq", "xattn_wkv", "xattn_wo", "ffn2_norm", "ffn2_w_gu", "ffn2_w_down",
               "final_norm")
    loss = small_out[0][ROW_LOSS, 0]
    return (loss, dx[None], *[leaves[n] for leaves in results for n in weights])
```

```python
import math

import numpy as np
import jax
import jax.numpy as jnp
from jax import lax
from jax.experimental import pallas as pl
from jax.experimental.pallas import tpu as pltpu

F32, BF16 = jnp.float32, jnp.bfloat16
MESH = pl.DeviceIdType.MESH

D = 1024
N_DEV = 8
D_FF = 2816
FS = D_FF // 4
HEAD = 64
N_HEADS, N_KV = 16, 4
BLK = 128
XH, XHD = 4, 256
REL_BUCKETS, REL_EXACT, REL_MAX_DIST = 32, 16, 128
EPS, NEG = 1e-6, -1e30
ADAM_LR, ADAM_B1, ADAM_B2, ADAM_EPS, ADAM_WD, ADAM_STEP = 0.001, 0.9, 0.999, 1e-08, 0.01, 10
VMEM_LIMIT_V7X = 56 * 2**20
SMALL_ROWS = 16
ROW_RELB, ROW_LOSS, ROW_CONV = 6, 7, 8


def _bucket_thresholds():
    n = np.arange(REL_MAX_DIST)
    nf = np.maximum(n, 1).astype(np.float32)
    large = REL_EXACT + (np.log(nf / np.float32(REL_EXACT)) / np.float32(math.log(REL_MAX_DIST / REL_EXACT))
                         * np.float32(REL_BUCKETS - REL_EXACT)).astype(np.int32)
    b = np.where(n < REL_EXACT, n, np.minimum(large, REL_BUCKETS - 1))
    return [int(np.argmax(b >= REL_EXACT + k)) for k in range(1, REL_BUCKETS - REL_EXACT)]


BUCKET_THRESHOLDS = _bucket_thresholds()


HBM_SPEC = pl.BlockSpec(memory_space=pl.ANY)


class _Carry:
    def __init__(self, ins, outs, sems, start, finish, mid=None, aliases=None):
        self.ins, self.outs, self.sems = list(ins), list(outs), list(sems)
        self.start, self.finish, self.mid, self.aliases = start, finish, mid, dict(aliases or {})


def _pcall(body, *, name, grid, in_specs, out_specs, out_shape, scratch=(), carry=None):
    params = pltpu.CompilerParams(dimension_semantics=("arbitrary",) * len(grid), vmem_limit_bytes=VMEM_LIMIT_V7X)
    if carry is None:
        return pl.pallas_call(body, name=name, grid=grid, in_specs=in_specs, out_specs=out_specs,
                              out_shape=out_shape, scratch_shapes=list(scratch), compiler_params=params)
    single = not isinstance(out_shape, (list, tuple))
    own_specs, own_shapes = ([out_specs], [out_shape]) if single else (list(out_specs), list(out_shape))
    n_in, n_out, n_scr = len(in_specs), len(own_shapes), len(scratch)
    n_cin, n_cout = len(carry.ins), len(carry.outs)
    steps = math.prod(grid)
    mid_step = max(steps - 1 - max(steps // 8, 1), 0)

    def carrying(*refs):
        ins, refs = refs[:n_in], refs[n_in:]
        cins, refs = refs[:n_cin], refs[n_cin:]
        outs, refs = refs[:n_out], refs[n_out:]
        couts, refs = refs[:n_cout], refs[n_cout:]
        scr, csems = refs[:n_scr], refs[n_scr:]
        step = 0
        for axis, size in enumerate(grid):
            step = step * size + pl.program_id(axis)

        @pl.when(step == 0)
        def _():
            carry.start(cins, couts, csems)

        body(*ins, *outs, *scr)
        if carry.mid is not None:
            @pl.when(step == mid_step)
            def _():
                carry.mid(cins, couts, csems)

        @pl.when(step == steps - 1)
        def _():
            carry.finish(cins, couts, csems)

    call = pl.pallas_call(carrying, name=name, grid=grid, in_specs=list(in_specs) + [HBM_SPEC] * n_cin,
                          out_specs=own_specs + [HBM_SPEC] * n_cout, out_shape=own_shapes + carry.outs,
                          scratch_shapes=list(scratch) + carry.sems, compiler_params=params,
                          input_output_aliases={n_in + i: n_out + o for i, o in carry.aliases.items()})

    def run(*args):
        res = call(*args, *carry.ins)
        return (res[0] if single else res[:n_out]), res[n_out:]

    return run


def _run_alone(carry, name):
    n_cin, n_cout = len(carry.ins), len(carry.outs)

    def body(*refs):
        cins, couts, csems = refs[:n_cin], refs[n_cin:n_cin + n_cout], refs[n_cin + n_cout:]
        carry.start(cins, couts, csems)
        if carry.mid is not None:
            carry.mid(cins, couts, csems)
        carry.finish(cins, couts, csems)

    return pl.pallas_call(body, name=name, in_specs=[HBM_SPEC] * n_cin, out_specs=[HBM_SPEC] * n_cout,
                          out_shape=carry.outs, scratch_shapes=carry.sems,
                          input_output_aliases=carry.aliases)(*carry.ins)


def _dot(a, b):
    return jnp.dot(a, b, preferred_element_type=F32)


def _dot_nt(a, b):
    return lax.dot_general(a, b, (((1,), (1,)), ((), ())), preferred_element_type=F32)


def _dot_tn(a, b):
    return lax.dot_general(a, b, (((0,), (0,)), ((), ())), preferred_element_type=F32)


def _sds(shape, dtype):
    return jax.ShapeDtypeStruct(tuple(shape), dtype)


def _carried(call, args, carry):
    return call(*args) if carry is not None else (call(*args), ())


def _rmsnorm(x, g, name, carry=None):
    m, d = x.shape
    tm = min(512, m)

    def body(x_ref, g_ref, h_ref):
        xv = x_ref[...]
        r = lax.rsqrt(jnp.mean(xv * xv, axis=-1, keepdims=True) + EPS)
        h_ref[...] = (xv * r * g_ref[...]).astype(BF16)

    call = _pcall(body, name=name, grid=(m // tm,), carry=carry,
                  in_specs=[pl.BlockSpec((tm, d), lambda i: (i, 0)), pl.BlockSpec((1, d), lambda i: (0, 0))],
                  out_specs=pl.BlockSpec((tm, d), lambda i: (i, 0)), out_shape=_sds((m, d), BF16))
    return _carried(call, (x, g), carry)


def _mm_nn(a, b, name, tm=512, bt=False, carry=None):
    m, k = a.shape
    nj = b.shape[0]
    n = b.shape[1] if bt else b.shape[2]
    tm = min(tm, m)
    dot = _dot_nt if bt else _dot

    def body(a_ref, b_ref, o_ref):
        o_ref[...] = dot(a_ref[...], b_ref[...]).astype(BF16)

    call = _pcall(body, name=name, grid=(nj, m // tm),
                  in_specs=[pl.BlockSpec((tm, k), lambda j, i: (i, 0)),
                            pl.BlockSpec((None,) + b.shape[1:], lambda j, i: (j, 0, 0))],
                  out_specs=pl.BlockSpec((None, tm, n), lambda j, i: (j, i, 0)),
                  out_shape=_sds((nj, m, n), BF16), carry=carry)
    return _carried(call, (a, b), carry)


def _ffn_up(h, w4, name, tm=512, carry=None):
    s, d = h.shape
    tm = min(tm, s)

    def body(h_ref, w_ref, gu_ref, a_ref):
        hv = h_ref[...]
        g = _dot_nt(hv, w_ref[0])
        u = _dot_nt(hv, w_ref[1])
        gu_ref[0] = g.astype(BF16)
        gu_ref[1] = u.astype(BF16)
        a_ref[...] = (g * jax.nn.sigmoid(g) * u).astype(BF16)

    call = _pcall(body, name=name, grid=(4, s // tm),
                  in_specs=[pl.BlockSpec((tm, d), lambda p, i: (i, 0)),
                            pl.BlockSpec((2, None, FS, d), lambda p, i: (0, p, 0, 0))],
                  out_specs=[pl.BlockSpec((2, None, tm, FS), lambda p, i: (0, p, i, 0)),
                             pl.BlockSpec((None, tm, FS), lambda p, i: (p, i, 0))],
                  out_shape=[_sds((2, 4, s, FS), BF16), _sds((4, s, FS), BF16)], carry=carry)
    return _carried(call, (h, w4), carry)


def _mm_res_norm(a, w, xres, gain, scale, name, tm=512, carry=None):
    npart, s, kp = a.shape
    tm = min(tm, s)

    def body(a_ref, w_ref, x_ref, g_ref, xo_ref, h_ref):
        acc = _dot(a_ref[0], w_ref[0])
        for p in range(1, npart):
            acc = acc + _dot(a_ref[p], w_ref[p])
        xn = x_ref[...] + scale * acc
        xo_ref[...] = xn
        r = lax.rsqrt(jnp.mean(xn * xn, axis=-1, keepdims=True) + EPS)
        h_ref[...] = (xn * r * g_ref[...]).astype(BF16)

    call = _pcall(body, name=name, grid=(s // tm,),
                  in_specs=[pl.BlockSpec((npart, tm, kp), lambda i: (0, i, 0)),
                            pl.BlockSpec((npart, kp, D), lambda i: (0, 0, 0)),
                            pl.BlockSpec((tm, D), lambda i: (i, 0)),
                            pl.BlockSpec((1, D), lambda i: (0, 0))],
                  out_specs=[pl.BlockSpec((tm, D), lambda i: (i, 0)), pl.BlockSpec((tm, D), lambda i: (i, 0))],
                  out_shape=[_sds((s, D), F32), _sds((s, D), BF16)], carry=carry)
    return _carried(call, (a, w, xres, gain), carry)


def _ffn_down_loss(a, w, xres, gain, target, name, tm=512):
    npart, s, kp = a.shape
    tm = min(tm, s)

    def body(a_ref, w_ref, x_ref, g_ref, t_ref, dx_ref, dxb_ref, loss_ref, dg_ref):
        i = pl.program_id(0)
        acc = _dot(a_ref[0], w_ref[0])
        for p in range(1, npart):
            acc = acc + _dot(a_ref[p], w_ref[p])
        xn = x_ref[...] + 0.5 * acc
        r = lax.rsqrt(jnp.mean(xn * xn, axis=-1, keepdims=True) + EPS)
        xh = xn * r
        gv = g_ref[...]
        err = xh * gv - t_ref[...]
        part = 0.5 * jnp.sum(jnp.mean(err * err, axis=-1, keepdims=True), axis=0, keepdims=True)
        dy = err * (1.0 / D)
        dyg = dy * gv
        dxn = r * (dyg - xh * jnp.mean(dyg * xh, axis=-1, keepdims=True))
        dx_ref[...] = dxn
        dxb_ref[...] = dxn.astype(BF16)

        @pl.when(i == 0)
        def _():
            loss_ref[...] = jnp.zeros_like(loss_ref)
            dg_ref[...] = jnp.zeros_like(dg_ref)

        loss_ref[...] += jnp.broadcast_to(part, loss_ref.shape)
        dg_ref[...] += jnp.sum(dy * xh, axis=0, keepdims=True)

    return _pcall(body, name=name, grid=(s // tm,),
                  in_specs=[pl.BlockSpec((npart, tm, kp), lambda i: (0, i, 0)),
                            pl.BlockSpec((npart, kp, D), lambda i: (0, 0, 0)),
                            pl.BlockSpec((tm, D), lambda i: (i, 0)),
                            pl.BlockSpec((1, D), lambda i: (0, 0)),
                            pl.BlockSpec((tm, D), lambda i: (i, 0))],
                  out_specs=[pl.BlockSpec((tm, D), lambda i: (i, 0)), pl.BlockSpec((tm, D), lambda i: (i, 0)),
                             pl.BlockSpec((8, 128), lambda i: (0, 0)), pl.BlockSpec((1, D), lambda i: (0, 0))],
                  out_shape=[_sds((s, D), F32), _sds((s, D), BF16), _sds((8, 128), F32), _sds((1, D), F32)],
                  )(a, w, xres, gain, target)


def _band_tiles():
    i = lax.broadcasted_iota(jnp.int32, (BLK, 2 * BLK), 0)
    j = lax.broadcasted_iota(jnp.int32, (BLK, 2 * BLK), 1)
    rel = BLK + i - j
    large = jnp.full_like(rel, REL_EXACT)
    for t in BUCKET_THRESHOLDS:
        large = large + (rel >= t).astype(jnp.int32)
    bucket = jnp.where(rel < REL_EXACT, rel, large)
    visible = jnp.logical_and(rel >= 0, rel < BLK)
    return bucket, visible


def _bias_build(rel_bias, name):
    def body(rb_ref, o_ref):
        bucket, visible = _band_tiles()

        def per_head(h, carry):
            acc = jnp.zeros((BLK, 2 * BLK), F32)
            for b in range(REL_BUCKETS):
                acc = jnp.where(bucket == b, rb_ref[b, h], acc)
            o_ref[h] = jnp.where(visible, acc, NEG)
            return carry

        lax.fori_loop(0, N_HEADS, per_head, 0)

    return _pcall(body, name=name, grid=(1,),
                  in_specs=[pl.BlockSpec(memory_space=pltpu.SMEM)],
                  out_specs=pl.BlockSpec((N_HEADS, BLK, 2 * BLK), lambda i: (0, 0, 0)),
                  out_shape=_sds((N_HEADS, BLK, 2 * BLK), F32))(rel_bias)


def _bias_bwd(dbias, name):
    def body(db_ref, o_ref):
        bucket, _ = _band_tiles()
        lane = lax.broadcasted_iota(jnp.int32, (N_HEADS, 128), 1)

        def per_bucket(b, out):
            mb = (bucket == b).astype(F32)
            per_col = jnp.sum(db_ref[...] * mb[None, :, :], axis=1)
            return jnp.where(lane == b, jnp.sum(per_col, axis=1, keepdims=True), out)

        o_ref[...] = lax.fori_loop(0, REL_BUCKETS, per_bucket, jnp.zeros((N_HEADS, 128), F32))

    return _pcall(body, name=name, grid=(1,),
                  in_specs=[pl.BlockSpec((N_HEADS, BLK, 2 * BLK), lambda i: (0, 0, 0))],
                  out_specs=pl.BlockSpec((N_HEADS, 128), lambda i: (0, 0)),
                  out_shape=_sds((N_HEADS, 128), F32))(dbias)


def _kv_group(kp_ref, kc_ref, g):
    ks = slice(HEAD * g, HEAD * (g + 1))
    vs = slice(N_KV * HEAD + HEAD * g, N_KV * HEAD + HEAD * (g + 1))
    kg = jnp.concatenate([kp_ref[:, ks], kc_ref[:, ks]], axis=0)
    vg = jnp.concatenate([kp_ref[:, vs], kc_ref[:, vs]], axis=0)
    return kg, vg


def _swa_fwd(pb, biasm, sinks, name, carry=None):
    _, s, _ = pb.shape
    nb = s // BLK
    kvw = 2 * N_KV * HEAD

    def body(q_ref, kc_ref, kp_ref, b_ref, sk_ref, o_ref, lse_ref):
        n = pl.program_id(0)
        col = lax.broadcasted_iota(jnp.int32, (BLK, 2 * BLK), 1)
        no_prev = jnp.logical_and(n == 0, col < BLK)
        lane = lax.broadcasted_iota(jnp.int32, (BLK, 128), 1)
        lse_t = jnp.zeros((BLK, 128), F32)
        for g in range(N_KV):
            kg, vg = _kv_group(kp_ref, kc_ref, g)
            for r in range(N_HEADS // N_KV):
                h = g * (N_HEADS // N_KV) + r
                hs = slice(HEAD * h, HEAD * (h + 1))
                sc = _dot_nt(q_ref[:, hs], kg) * (HEAD ** -0.5) + b_ref[h]
                sc = jnp.where(no_prev, NEG, sc)
                sk = sk_ref[0, h]
                m = jnp.maximum(jnp.max(sc, axis=1, keepdims=True), sk)
                p = jnp.exp(sc - m)
                l = jnp.sum(p, axis=1, keepdims=True) + jnp.exp(sk - m)
                o = _dot(p.astype(BF16), vg) * (1.0 / l)
                o_ref[:, hs] = o.astype(BF16)
                lse_t = jnp.where(lane == h, m + jnp.log(l), lse_t)
        lse_ref[...] = lse_t

    call = _pcall(body, name=name, grid=(nb,),
                  in_specs=[pl.BlockSpec((None, BLK, D), lambda n: (0, n, 0)),
                            pl.BlockSpec((None, BLK, kvw), lambda n: (1, n, 0)),
                            pl.BlockSpec((None, BLK, kvw), lambda n: (1, jnp.maximum(n - 1, 0), 0)),
                            pl.BlockSpec((N_HEADS, BLK, 2 * BLK), lambda n: (0, 0, 0)),
                            pl.BlockSpec(memory_space=pltpu.SMEM)],
                  out_specs=[pl.BlockSpec((BLK, D), lambda n: (n, 0)), pl.BlockSpec((BLK, 128), lambda n: (n, 0))],
                  out_shape=[_sds((s, D), BF16), _sds((s, 128), F32)], carry=carry)
    return _carried(call, (pb, pb, pb, biasm, sinks), carry)


def _swa_bwd(pb, attn, dattn, lse, biasm, sinks, name, carry=None):
    _, s, _ = pb.shape
    nb = s // BLK
    kvw = 2 * N_KV * HEAD
    grp = N_HEADS // N_KV

    def body(q_ref, kc_ref, kp_ref, o_ref, do_ref, lse_ref, b_ref, sk_ref, dpb_ref, dbias_ref, dsk_ref,
             dq_hold, kv_hold, dq_new, kv_prev, kv_cur):
        n = pl.program_id(0)

        @pl.when(n == 0)
        def _():
            dbias_ref[...] = jnp.zeros_like(dbias_ref)
            dsk_ref[...] = jnp.zeros_like(dsk_ref)
            dq_hold[...] = jnp.zeros_like(dq_hold)
            kv_hold[...] = jnp.zeros_like(kv_hold)

        @pl.when(n < nb)
        def _():
            col = lax.broadcasted_iota(jnp.int32, (BLK, 2 * BLK), 1)
            no_prev = jnp.logical_and(n == 0, col < BLK)
            lane = lax.broadcasted_iota(jnp.int32, (1, 128), 1)
            dsk = jnp.zeros((1, 128), F32)
            for g in range(N_KV):
                kg, vg = _kv_group(kp_ref, kc_ref, g)
                dk_g = jnp.zeros((2 * BLK, HEAD), F32)
                dv_g = jnp.zeros((2 * BLK, HEAD), F32)
                for r in range(grp):
                    h = g * grp + r
                    hs = slice(HEAD * h, HEAD * (h + 1))
                    qh = q_ref[:, hs]
                    dob = do_ref[:, hs]
                    lse_h = lse_ref[:, h:h + 1]
                    sc = _dot_nt(qh, kg) * (HEAD ** -0.5) + b_ref[h]
                    sc = jnp.where(no_prev, NEG, sc)
                    p = jnp.exp(sc - lse_h)
                    dp = _dot_nt(dob, vg)
                    delta = jnp.sum(dob.astype(F32) * o_ref[:, hs].astype(F32), axis=1, keepdims=True)
                    ds = p * (dp - delta)
                    dbias_ref[h] += ds
                    p_sink = jnp.exp(sk_ref[0, h] - lse_h)
                    dsk = jnp.where(lane == h, dsk - jnp.sum(p_sink * delta, keepdims=True), dsk)
                    dsb = (ds * (HEAD ** -0.5)).astype(BF16)
                    dq_new[:, hs] = _dot(dsb, kg)
                    dk_g = dk_g + _dot_tn(dsb, qh)
                    dv_g = dv_g + _dot_tn(p.astype(BF16), dob)
                ks = slice(HEAD * g, HEAD * (g + 1))
                vs = slice(N_KV * HEAD + HEAD * g, N_KV * HEAD + HEAD * (g + 1))
                kv_prev[:, ks] = dk_g[:BLK]
                kv_cur[:, ks] = dk_g[BLK:]
                kv_prev[:, vs] = dv_g[:BLK]
                kv_cur[:, vs] = dv_g[BLK:]
            dsk_ref[...] += dsk

        @pl.when(n == nb)
        def _():
            kv_prev[...] = jnp.zeros_like(kv_prev)

        dpb_ref[0] = dq_hold[...].astype(BF16)
        dpb_ref[1, :, 0:kvw] = (kv_hold[...] + kv_prev[...]).astype(BF16)
        dpb_ref[1, :, kvw:D] = jnp.zeros((BLK, D - kvw), BF16)

        @pl.when(n < nb)
        def _():
            dq_hold[...] = dq_new[...]
            kv_hold[...] = kv_cur[...]

    def cur(n):
        return jnp.minimum(n, nb - 1)

    call = _pcall(body, name=name, grid=(nb + 1,), carry=carry,
                  in_specs=[pl.BlockSpec((None, BLK, D), lambda n: (0, cur(n), 0)),
                            pl.BlockSpec((None, BLK, kvw), lambda n: (1, cur(n), 0)),
                            pl.BlockSpec((None, BLK, kvw), lambda n: (1, jnp.maximum(cur(n) - 1, 0), 0)),
                            pl.BlockSpec((BLK, D), lambda n: (cur(n), 0)),
                            pl.BlockSpec((BLK, D), lambda n: (cur(n), 0)),
                            pl.BlockSpec((BLK, 128), lambda n: (cur(n), 0)),
                            pl.BlockSpec((N_HEADS, BLK, 2 * BLK), lambda n: (0, 0, 0)),
                            pl.BlockSpec(memory_space=pltpu.SMEM)],
                  out_specs=[pl.BlockSpec((2, BLK, D), lambda n: (0, jnp.maximum(n - 1, 0), 0)),
                             pl.BlockSpec((N_HEADS, BLK, 2 * BLK), lambda n: (0, 0, 0)),
                             pl.BlockSpec((1, 128), lambda n: (0, 0))],
                  out_shape=[_sds((2, s, D), BF16), _sds((N_HEADS, BLK, 2 * BLK), F32), _sds((1, 128), F32)],
                  scratch=[pltpu.VMEM((BLK, D), F32), pltpu.VMEM((BLK, kvw), F32), pltpu.VMEM((BLK, D), F32),
                           pltpu.VMEM((BLK, kvw), F32), pltpu.VMEM((BLK, kvw), F32)])
    return _carried(call, (pb, pb, pb, attn, dattn, lse, biasm, sinks), carry)


HALO = 16
CW = 512


def _conv_taps(cu, halo_cu, first_tile):
    row = lax.broadcasted_iota(jnp.int32, cu.shape, 0)
    halo_cu = jnp.where(first_tile, 0.0, halo_cu)
    c1 = jnp.where(row == 0, halo_cu[HALO - 1:HALO], pltpu.roll(cu, 1, 0))
    c2 = jnp.where(row == 0, halo_cu[HALO - 2:HALO - 1],
                   jnp.where(row == 1, halo_cu[HALO - 1:HALO], pltpu.roll(cu, 2, 0)))
    return c1, c2


def _conv_merge_fwd(pa, attn, convw, name, ts=512):
    _, s, _ = pa.shape
    ts = min(ts, s)
    hb = ts // HALO

    def body(pa_ref, hp_ref, at_ref, w_ref, o_ref):
        i = pl.program_id(1)
        cu = pa_ref[0].astype(F32) * pa_ref[2].astype(F32)
        c1, c2 = _conv_taps(cu, hp_ref[0].astype(F32) * hp_ref[2].astype(F32), i == 0)
        w = w_ref[...]
        c3 = w[0:1] * c2 + w[1:2] * c1 + w[2:3] * cu
        conv = pa_ref[1].astype(F32) * c3
        o_ref[...] = (jax.nn.sigmoid(pa_ref[3].astype(F32)) * at_ref[...].astype(F32)
                      + jax.nn.sigmoid(pa_ref[4].astype(F32)) * conv).astype(BF16)

    return _pcall(body, name=name, grid=(D // CW, s // ts),
                  in_specs=[pl.BlockSpec((5, ts, CW), lambda c, i: (0, i, c)),
                            pl.BlockSpec((5, HALO, CW), lambda c, i: (0, jnp.maximum(i * hb - 1, 0), c)),
                            pl.BlockSpec((ts, CW), lambda c, i: (i, c)),
                            pl.BlockSpec((8, CW), lambda c, i: (0, c))],
                  out_specs=pl.BlockSpec((ts, CW), lambda c, i: (i, c)),
                  out_shape=_sds((s, D), BF16))(pa, pa, attn, convw)


def _conv_merge_bwd(dmerged, pa, attn, convw, name, ts=512, carry=None):
    _, s, _ = pa.shape
    ts = min(ts, s)
    hb = ts // HALO
    last_hb = s // HALO - 1

    def body(dm_ref, pa_ref, at_ref, w_ref, hp_ref, hn_ref, dmn_ref, dat_ref, dpa_ref, dw_ref):
        i = pl.program_id(1)
        last = i == pl.num_programs(1) - 1
        dm = dm_ref[...].astype(F32)
        cp, bp, u = pa_ref[0].astype(F32), pa_ref[1].astype(F32), pa_ref[2].astype(F32)
        sa = jax.nn.sigmoid(pa_ref[3].astype(F32))
        sc = jax.nn.sigmoid(pa_ref[4].astype(F32))
        at = at_ref[...].astype(F32)
        cu = cp * u
        c1, c2 = _conv_taps(cu, hp_ref[0].astype(F32) * hp_ref[2].astype(F32), i == 0)
        w = w_ref[...]
        c3 = w[0:1] * c2 + w[1:2] * c1 + w[2:3] * cu
        dconv = dm * sc
        dc3 = dconv * bp
        nxt = dmn_ref[...].astype(F32) * jax.nn.sigmoid(hn_ref[4].astype(F32)) * hn_ref[1].astype(F32)
        nxt = jnp.where(last, 0.0, nxt)
        row = lax.broadcasted_iota(jnp.int32, dc3.shape, 0)
        d1 = jnp.where(row == ts - 1, nxt[0:1], pltpu.roll(dc3, ts - 1, 0))
        d2 = jnp.where(row == ts - 2, nxt[0:1], jnp.where(row == ts - 1, nxt[1:2], pltpu.roll(dc3, ts - 2, 0)))
        dcu = w[2:3] * dc3 + w[1:2] * d1 + w[0:1] * d2
        dat_ref[...] = (dm * sa).astype(BF16)
        dpa_ref[0] = (dcu * u).astype(BF16)
        dpa_ref[1] = (dconv * c3).astype(BF16)
        dpa_ref[2] = (dcu * cp).astype(BF16)
        dpa_ref[3] = (dm * at * sa * (1.0 - sa)).astype(BF16)
        dpa_ref[4] = (dm * bp * c3 * sc * (1.0 - sc)).astype(BF16)

        @pl.when(i == 0)
        def _():
            dw_ref[...] = jnp.zeros_like(dw_ref)

        dw_ref[0:1, :] += jnp.sum(dc3 * c2, axis=0, keepdims=True)
        dw_ref[1:2, :] += jnp.sum(dc3 * c1, axis=0, keepdims=True)
        dw_ref[2:3, :] += jnp.sum(dc3 * cu, axis=0, keepdims=True)

    call = _pcall(body, name=name, grid=(D // CW, s // ts), carry=carry,
                  in_specs=[pl.BlockSpec((ts, CW), lambda c, i: (i, c)),
                            pl.BlockSpec((5, ts, CW), lambda c, i: (0, i, c)),
                            pl.BlockSpec((ts, CW), lambda c, i: (i, c)),
                            pl.BlockSpec((8, CW), lambda c, i: (0, c)),
                            pl.BlockSpec((5, HALO, CW), lambda c, i: (0, jnp.maximum(i * hb - 1, 0), c)),
                            pl.BlockSpec((5, HALO, CW), lambda c, i: (0, jnp.minimum((i + 1) * hb, last_hb), c)),
                            pl.BlockSpec((HALO, CW), lambda c, i: (jnp.minimum((i + 1) * hb, last_hb), c))],
                  out_specs=[pl.BlockSpec((ts, CW), lambda c, i: (i, c)),
                             pl.BlockSpec((5, ts, CW), lambda c, i: (0, i, c)),
                             pl.BlockSpec((8, CW), lambda c, i: (0, c))],
                  out_shape=[_sds((s, D), BF16), _sds((5, s, D), BF16), _sds((8, D), F32)])
    return _carried(call, (dmerged, pa, attn, convw, pa, pa, dmerged), carry)


def _xattn_fwd(q, kv, name, tq=512):
    s, _ = q.shape
    nm = kv.shape[0]
    tq = min(tq, s)

    def body(q_ref, kv_ref, o_ref, lse_ref):
        lane = lax.broadcasted_iota(jnp.int32, (tq, 128), 1)
        lse_t = jnp.zeros((tq, 128), F32)
        for h in range(XH):
            hs = slice(XHD * h, XHD * (h + 1))
            vs = slice(D + XHD * h, D + XHD * (h + 1))
            sc = _dot_nt(q_ref[:, hs], kv_ref[:, hs]) * (XHD ** -0.5)
            m = jnp.max(sc, axis=1, keepdims=True)
            p = jnp.exp(sc - m)
            l = jnp.sum(p, axis=1, keepdims=True)
            o_ref[:, hs] = (_dot(p.astype(BF16), kv_ref[:, vs]) * (1.0 / l)).astype(BF16)
            lse_t = jnp.where(lane == h, m + jnp.log(l), lse_t)
        lse_ref[...] = lse_t

    return _pcall(body, name=name, grid=(s // tq,),
                  in_specs=[pl.BlockSpec((tq, D), lambda i: (i, 0)), pl.BlockSpec((nm, 2 * D), lambda i: (0, 0))],
                  out_specs=[pl.BlockSpec((tq, D), lambda i: (i, 0)), pl.BlockSpec((tq, 128), lambda i: (i, 0))],
                  out_shape=[_sds((s, D), BF16), _sds((s, 128), F32)])(q, kv)


def _xattn_bwd(q, kv, o, do, lse, name, tq=512, carry=None):
    s, _ = q.shape
    nm = kv.shape[0]
    tq = min(tq, s)

    def body(q_ref, kv_ref, o_ref, do_ref, lse_ref, dq_ref, dkv_ref):
        @pl.when(pl.program_id(0) == 0)
        def _():
            dkv_ref[...] = jnp.zeros_like(dkv_ref)

        for h in range(XH):
            hs = slice(XHD * h, XHD * (h + 1))
            vs = slice(D + XHD * h, D + XHD * (h + 1))
            qh, kh, vh, dob = q_ref[:, hs], kv_ref[:, hs], kv_ref[:, vs], do_ref[:, hs]
            p = jnp.exp(_dot_nt(qh, kh) * (XHD ** -0.5) - lse_ref[:, h:h + 1])
            dp = _dot_nt(dob, vh)
            delta = jnp.sum(dob.astype(F32) * o_ref[:, hs].astype(F32), axis=1, keepdims=True)
            dsb = (p * (dp - delta) * (XHD ** -0.5)).astype(BF16)
            dq_ref[:, hs] = _dot(dsb, kh).astype(BF16)
            dkv_ref[:, hs] += _dot_tn(dsb, qh)
            dkv_ref[:, vs] += _dot_tn(p.astype(BF16), dob)

    call = _pcall(body, name=name, grid=(s // tq,), carry=carry,
                  in_specs=[pl.BlockSpec((tq, D), lambda i: (i, 0)), pl.BlockSpec((nm, 2 * D), lambda i: (0, 0)),
                            pl.BlockSpec((tq, D), lambda i: (i, 0)), pl.BlockSpec((tq, D), lambda i: (i, 0)),
                            pl.BlockSpec((tq, 128), lambda i: (i, 0))],
                  out_specs=[pl.BlockSpec((tq, D), lambda i: (i, 0)), pl.BlockSpec((nm, 2 * D), lambda i: (0, 0))],
                  out_shape=[_sds((s, D), BF16), _sds((nm, 2 * D), F32)])
    return _carried(call, (q, kv, o, do, lse), carry)


def _ffn_down_bwd(dxb, wd4, gu4, name, tm=512, carry=None):
    s, _ = dxb.shape
    tm = min(tm, s)

    def body(dx_ref, w_ref, gu_ref, o_ref):
        da = 0.5 * _dot_nt(dx_ref[...], w_ref[...])
        g = gu_ref[0].astype(F32)
        u = gu_ref[1].astype(F32)
        sg = jax.nn.sigmoid(g)
        o_ref[0] = (da * u * sg * (1.0 + g * (1.0 - sg))).astype(BF16)
        o_ref[1] = (da * g * sg).astype(BF16)

    call = _pcall(body, name=name, grid=(4, s // tm), carry=carry,
                  in_specs=[pl.BlockSpec((tm, D), lambda p, i: (i, 0)),
                            pl.BlockSpec((None, FS, D), lambda p, i: (p, 0, 0)),
                            pl.BlockSpec((2, None, tm, FS), lambda p, i: (0, p, i, 0))],
                  out_specs=pl.BlockSpec((2, None, tm, FS), lambda p, i: (0, p, i, 0)),
                  out_shape=_sds((2, 4, s, FS), BF16))
    return _carried(call, (dxb, wd4, gu4), carry)


def _mm_tn(a, b, name, scale=1.0, tk=1024, tn=None, carry=None):
    pa_n, s, m = a.shape
    pb_n, _, n = b.shape
    po = max(pa_n, pb_n)
    tk = min(tk, s)
    tn = n if tn is None else tn
    nk = s // tk

    def body(a_ref, b_ref, o_ref, acc_ref):
        k = pl.program_id(2)

        @pl.when(k == 0)
        def _():
            acc_ref[...] = jnp.zeros_like(acc_ref)

        acc_ref[...] += _dot_tn(a_ref[...], b_ref[...])

        @pl.when(k == nk - 1)
        def _():
            o_ref[...] = (scale * acc_ref[...]).astype(BF16)

    call = _pcall(body, name=name, grid=(po, n // tn, nk), carry=carry,
                  in_specs=[pl.BlockSpec((None, tk, m), lambda o, j, k: (o if pa_n > 1 else 0, k, 0)),
                            pl.BlockSpec((None, tk, tn), lambda o, j, k: (o if pb_n > 1 else 0, k, j))],
                  out_specs=pl.BlockSpec((None, m, tn), lambda o, j, k: (o, 0, j)),
                  out_shape=_sds((po, m, n), BF16), scratch=[pltpu.VMEM((m, tn), F32)])
    return _carried(call, (a, b), carry)


def _sum_dots(a_ref, b_ref, nj, bt):
    dot = _dot_nt if bt else _dot
    acc = dot(a_ref[0], b_ref[0])
    for j in range(1, nj):
        acc = acc + dot(a_ref[j], b_ref[j])
    return acc


def _mm_acc(a, b, name, out_dtype, tm=512, bt=False, carry=None):
    nj, s, k = a.shape
    n = b.shape[1] if bt else b.shape[2]
    tm = min(tm, s)

    def body(a_ref, b_ref, o_ref):
        o_ref[...] = _sum_dots(a_ref, b_ref, nj, bt).astype(out_dtype)

    call = _pcall(body, name=name, grid=(s // tm,), carry=carry,
                  in_specs=[pl.BlockSpec((nj, tm, k), lambda i: (0, i, 0)),
                            pl.BlockSpec(b.shape, lambda i: (0, 0, 0))],
                  out_specs=pl.BlockSpec((tm, n), lambda i: (i, 0)), out_shape=_sds((s, n), out_dtype))
    return _carried(call, (a, b), carry)


def _mm_acc_rms_bwd(a, b, name, *, x, gain, dres, addend=None, tm=512, bt=False, carry=None):
    nj, s, k = a.shape
    n = b.shape[1] if bt else b.shape[2]
    tm = min(tm, s)
    has_add = addend is not None

    def body(*refs):
        a_ref, b_hbm, x_ref, g_ref, r_ref = refs[:5]
        add_ref = refs[5] if has_add else None
        dx_ref, dxb_ref, dg_ref, b_ref, b_sem = refs[5 + has_add:]

        @pl.when(pl.program_id(0) == 0)
        def _():
            load = pltpu.make_async_copy(b_hbm, b_ref, b_sem)
            load.start()
            dg_ref[...] = jnp.zeros_like(dg_ref)
            load.wait()

        dh = _sum_dots(a_ref, b_ref, nj, bt)
        if has_add:
            dh = dh + add_ref[...]
        xv = x_ref[...]
        r = lax.rsqrt(jnp.mean(xv * xv, axis=-1, keepdims=True) + EPS)
        xh = xv * r
        dyg = dh * g_ref[...]
        dx = r_ref[...] + r * (dyg - xh * jnp.mean(dyg * xh, axis=-1, keepdims=True))
        dx_ref[...] = dx
        dxb_ref[...] = dx.astype(BF16)
        dg_ref[...] += jnp.sum(dh * xh, axis=0, keepdims=True)

    row = pl.BlockSpec((tm, n), lambda i: (i, 0))
    in_specs = [pl.BlockSpec((nj, tm, k), lambda i: (0, i, 0)), HBM_SPEC,
                row, pl.BlockSpec((1, n), lambda i: (0, 0)), row] + ([row] if has_add else [])
    args = (a, b, x, gain, dres) + ((addend,) if has_add else ())
    call = _pcall(body, name=name, grid=(s // tm,), in_specs=in_specs, carry=carry,
                  out_specs=[row, row, pl.BlockSpec((1, n), lambda i: (0, 0))],
                  out_shape=[_sds((s, n), F32), _sds((s, n), BF16), _sds((1, n), F32)],
                  scratch=[pltpu.VMEM(b.shape, b.dtype), pltpu.SemaphoreType.DMA(())])
    return _carried(call, args, carry)


def _adam(w, g, m, v):
    m2 = ADAM_B1 * m + (1.0 - ADAM_B1) * g
    v2 = ADAM_B2 * v + (1.0 - ADAM_B2) * (g * g)
    m_hat = m2 / (1.0 - ADAM_B1 ** ADAM_STEP)
    v_hat = v2 / (1.0 - ADAM_B2 ** ADAM_STEP)
    delta = -ADAM_LR * (m_hat / (jnp.sqrt(v_hat) + ADAM_EPS) + ADAM_WD * w)
    return delta, m2, v2


def _adamw(parts, w, m, v, name):
    _, r, c = parts.shape
    tr = max(t for t in range(16, 257, 16) if r % t == 0)

    def body(p_ref, w_ref, m_ref, v_ref, g_ref, d_ref, m2_ref, v2_ref):
        g = p_ref[0].astype(F32)
        for i in range(1, N_DEV):
            g = g + p_ref[i].astype(F32)
        delta, m2, v2 = _adam(w_ref[...], g, m_ref[...], v_ref[...])
        g_ref[...] = g
        d_ref[...] = delta
        m2_ref[...] = m2
        v2_ref[...] = v2

    blk = pl.BlockSpec((tr, c), lambda i: (i, 0))
    return _pcall(body, name=name, grid=(r // tr,),
                  in_specs=[pl.BlockSpec((N_DEV, tr, c), lambda i: (0, i, 0)), blk, blk, blk],
                  out_specs=[blk] * 4, out_shape=[_sds((r, c), F32)] * 4)(parts, w, m, v)


def _position():
    return lax.axis_index("x"), lax.axis_index("y"), lax.axis_index("c")


def _slot(px, py, pc):
    return 4 * px + 2 * py + pc


def _row_window(ref, rows):
    r0, r1 = rows
    return ref if (r0, r1) == (0, ref.shape[0]) else ref.at[pl.ds(r0, r1 - r0)]


def _split_items(items):
    sources = [src for src, _, _ in items]
    begun = [(a, dest) for a, (_, _, dest) in enumerate(items) if dest is not None]
    aliases = {len(sources) + k: a for k, (a, _) in enumerate(begun)}
    return sources + [dest for _, dest in begun], [rows for _, rows, _ in items], aliases


def _gather_carry(items):
    na = len(items)
    carry_ins, windows, aliases = _split_items(items)

    def plan(ins, outs, sems):
        send_sems, recv_sems, local_sems = sems
        x, y, c = _position()
        me, sibling = (x, y, c), (x, y, 1 - c)
        chips = [(1 - x, y), (x, 1 - y), (1 - x, 1 - y)]
        ins = [_row_window(ins[a], windows[a]) for a in range(na)]

        def block_rows(a, block):
            return _row_window(outs[a].at[_slot(*block)], windows[a])

        def copy(a, k, block, to, src=None):
            rows = block_rows(a, block)
            return pltpu.make_async_remote_copy(src_ref=rows if src is None else src, dst_ref=rows,
                                                send_sem=send_sems.at[k, a], recv_sem=recv_sems.at[k, a],
                                                device_id=to, device_id_type=MESH)

        mine = [pltpu.make_async_copy(ins[a], block_rows(a, me), local_sems.at[a]) for a in range(na)]
        first = [copy(a, 0, me, sibling, src=ins[a]) for a in range(na)]
        for j, chip in enumerate(chips):
            first += [copy(a, 1 + j, me, (*chip, c), src=ins[a]) for a in range(na)]
        landed = [[copy(a, 1 + j, (*chip, c), me) for a in range(na)] for j, chip in enumerate(chips)]
        passed = [[copy(a, 4 + j, (*chip, c), sibling) for a in range(na)] for j, chip in enumerate(chips)]
        from_sibling = [copy(a, 0, sibling, me) for a in range(na)]
        for j, chip in enumerate(chips):
            from_sibling += [copy(a, 4 + j, (*chip, 1 - c), me) for a in range(na)]
        return mine, first, landed, passed, from_sibling

    def start(ins, outs, sems):
        mine, first, _, _, _ = plan(ins, outs, sems)
        for cp in mine + first:
            cp.start()

    def mid(ins, outs, sems):
        _, _, landed, passed, _ = plan(ins, outs, sems)
        for over_ici, onward in zip(landed, passed):
            for cp, fwd in zip(over_ici, onward):
                cp.wait_recv()
                fwd.start()

    def finish(ins, outs, sems):
        mine, first, _, passed, from_sibling = plan(ins, outs, sems)
        for cp in from_sibling:
            cp.wait_recv()
        for cp in first + [fwd for onward in passed for fwd in onward]:
            cp.wait_send()
        for cp in mine:
            cp.wait()

    return _Carry(carry_ins, [_sds((N_DEV,) + src.shape, src.dtype) for src, _, _ in items],
                  [pltpu.SemaphoreType.DMA((7, na)), pltpu.SemaphoreType.DMA((7, na)),
                   pltpu.SemaphoreType.DMA((na,))], start, finish, mid, aliases)


def _exchange_carry(scattered, replicated=()):
    items = list(scattered) + [(a, (0, a.shape[0]), None) for a in replicated]
    na, ns = len(items), len(scattered)
    carry_ins, windows, aliases = _split_items(items)

    def plan(ins, outs, sems):
        send_sems, recv_sems, local_sems = sems
        me = _slot(*_position())

        def source(a, j):
            return _row_window(ins[a].at[j] if a < ns else ins[a], windows[a])

        def copy(a, j, i):
            return pltpu.make_async_remote_copy(src_ref=source(a, j), dst_ref=_row_window(outs[a].at[i], windows[a]),
                                                send_sem=send_sems.at[j, a], recv_sem=recv_sems.at[i, a],
                                                device_id=(j >> 2, (j >> 1) & 1, j & 1), device_id_type=MESH)

        mine = [pltpu.make_async_copy(source(a, me), _row_window(outs[a].at[me], windows[a]), local_sems.at[a])
                for a in range(na)]
        return me, copy, mine

    def start(ins, outs, sems):
        me, copy, mine = plan(ins, outs, sems)
        for cp in mine:
            cp.start()
        for j in range(N_DEV):
            @pl.when(me != j)
            def _():
                for a in range(na):
                    copy(a, j, me).start()

    def finish(ins, outs, sems):
        me, copy, mine = plan(ins, outs, sems)
        for i in range(N_DEV):
            @pl.when(me != i)
            def _():
                for a in range(na):
                    copy(a, i, i).wait_recv()
        for j in range(N_DEV):
            @pl.when(me != j)
            def _():
                for a in range(na):
                    copy(a, j, me).wait_send()
        for cp in mine:
            cp.wait()

    return _Carry(carry_ins, [_sds((N_DEV,) + src.shape[-2:], src.dtype) for src, _, _ in items],
                  [pltpu.SemaphoreType.DMA((N_DEV, na)), pltpu.SemaphoreType.DMA((N_DEV, na)),
                   pltpu.SemaphoreType.DMA((na,))], start, finish, None, aliases)


NQ, NKV = N_HEADS * HEAD, 2 * N_KV * HEAD


class _Mesh:
    def __init__(self, shards):
        self.shards, self.full, self.received, self.cache = shards, {}, {}, {}

    def fetch(self, wanted):
        items = []
        for want in wanted:
            name, r0, r1 = want if isinstance(want, tuple) else (want, 0, self.shards[want].shape[0])
            items.append((self.shards[name], (r0, r1), self.full.get(name)))
        return _gather_carry(items)

    def fetched(self, wanted, results):
        self.full.update(zip([want[0] if isinstance(want, tuple) else want for want in wanted], results))

    def send(self, *payloads):
        return _exchange_carry([(parts, rows or (0, parts.shape[1]), self.received.get(name))
                                for name, parts, rows in payloads])

    def sent(self, names, results):
        self.received.update(zip(names, results))

    def w(self, key):
        if key not in self.cache:
            self.cache[key] = self._layout(key)
        return self.cache[key]

    def _layout(self, key):
        if key in ("gu1", "gu2"):
            return self.full[key]
        if key in ("d1", "d2"):
            return self.full[key].reshape(4, FS, D)
        if key in ("out", "q", "o"):
            return self.full[key].reshape(D, D)
        if key == "kv":
            return self.full["kv"].transpose(1, 0, 2).reshape(D, 2 * D)
        if key == "convw":
            rows = self.full["conv"][:, :3, :].transpose(1, 0, 2).reshape(3, D)
            return jnp.concatenate([rows, jnp.zeros((5, D), F32)], axis=0)
        w_in_t = self.full["win"].reshape(-1, D)
        if key == "wa":
            return w_in_t[NQ + NKV:].reshape(5, D, D)
        assert key == "wb", key
        return jnp.stack([w_in_t[:NQ], jnp.pad(w_in_t[NQ:NQ + NKV], ((0, D - NKV), (0, 0)))])


def _w_in_parts(dw_a, dw_b):
    return jnp.concatenate([dw_b[0], dw_b[1][:NKV], dw_a.reshape(5 * D, D)], axis=0).reshape(N_DEV, -1, D)


def _forward_backward(x, mem, target, g, rel_bias, sinks, ex):
    s = x.shape[0]
    def fetching(wanted, call, *args, **kw):
        res, got = call(*args, carry=ex.fetch(wanted), **kw)
        ex.fetched(wanted, got)
        return res

    h1 = fetching(["gu1", "d1", "conv"], _rmsnorm, x, g["ffn1"], "norm_ffn1")
    gu1, a1 = fetching([("win", 0, 720)], _ffn_up, h1, ex.w("gu1").reshape(2, 4, FS, D), "ffn1_up")
    x1, h2 = fetching([("win", 720, 832), "out", "q"], _mm_res_norm, a1, ex.w("d1"), x, g["mix"], 0.5, "ffn1_down")
    pa = fetching(["gu2"], _mm_nn, h2, ex.w("wa"), "in_proj_a", bt=True)
    pb = fetching(["o"], _mm_nn, h2, ex.w("wb"), "in_proj_b", bt=True)
    biasm = _bias_build(rel_bias, "bias_build")
    attn, lse = fetching(["kv", "d2"], _swa_fwd, pb, biasm, sinks, "swa_fwd")
    merged = _conv_merge_fwd(pa, attn, ex.w("convw"), "conv_merge_fwd")
    (x2, h3), _ = _mm_res_norm(merged[None], ex.w("out")[None], x1, g["xattn"], 1.0, "out_proj")
    q2 = _mm_nn(h3, ex.w("q")[None], "xattn_q")[0][0]
    mh, _ = _rmsnorm(mem, g["mem"], "norm_mem")
    kv2 = _mm_nn(mh, ex.w("kv")[None], "xattn_kv")[0][0]
    o, lse2 = _xattn_fwd(q2, kv2, "xattn_fwd")
    (x3, h4), _ = _mm_res_norm(o[None], ex.w("o")[None], x2, g["ffn2"], 1.0, "xattn_o")
    (gu2, a2), _ = _ffn_up(h4, ex.w("gu2").reshape(2, 4, FS, D), "ffn2_up")
    dx4, dx4b, loss, d_final = _ffn_down_loss(a2, ex.w("d2"), x3, g["final"], target, "ffn2_down_loss")
    def sending(payloads, call, *args, **kw):
        res, got = call(*args, carry=ex.send(*payloads), **kw)
        ex.sent([name for name, _, _ in payloads], got)
        return res

    dw_d2 = _mm_tn(a2, dx4b[None], "dw_ffn2_down", scale=0.5)[0].reshape(N_DEV, -1, D)
    dgu2 = sending([("d2", dw_d2, None)], _ffn_down_bwd, dx4b, ex.w("d2"), gu2, "ffn2_down_bwd").reshape(8, s, FS)
    dw_gu2 = _mm_tn(dgu2, h4[None], "dw_ffn2_up")[0]
    dx3, dx3b, d_ffn2 = sending([("gu2", dw_gu2, (0, 608))], _mm_acc_rms_bwd, dgu2, ex.w("gu2"), "ffn2_up_bwd",
                                x=x3, gain=g["ffn2"], dres=dx4)
    do = sending([("gu2", dw_gu2, (608, FS))], _mm_acc, dx3b[None], ex.w("o")[None], "xattn_o_bwd", BF16, bt=True)
    dw_o = _mm_tn(o[None], dx3b[None], "dw_xattn_o")[0].reshape(N_DEV, -1, D)
    dq2, dkv2 = sending([("o", dw_o, None)], _xattn_bwd, q2, kv2, o, do, lse2, "xattn_bwd")
    dkv2b = dkv2.astype(BF16)
    dw_q = _mm_tn(h3[None], dq2[None], "dw_xattn_q")[0].reshape(N_DEV, -1, D)
    dx2, dx2b, d_xattn = sending([("q", dw_q, None)], _mm_acc_rms_bwd, dq2[None], ex.w("q")[None], "xattn_q_bwd",
                                 x=x2, gain=g["xattn"], dres=dx3, bt=True)
    dw_kv = _mm_tn(mh[None], dkv2b[None], "dw_xattn_kv")[0][0].reshape(D, N_DEV, -1).transpose(1, 0, 2)
    (_, _, d_mem), _ = _mm_acc_rms_bwd(dkv2b[None], ex.w("kv")[None], "xattn_kv_bwd", x=mem, gain=g["mem"],
                                       dres=jnp.zeros_like(mem), bt=True)
    dmerged = sending([("kv", dw_kv, (0, 352))], _mm_acc, dx2b[None], ex.w("out")[None], "out_proj_bwd", BF16,
                      bt=True)
    dw_out = sending([("kv", dw_kv, (352, 768))], _mm_tn, merged[None], dx2b[None], "dw_out_proj").reshape(N_DEV, -1, D)
    dattn, dpa, d_convw = sending([("kv", dw_kv, (768, D)), ("out", dw_out, None)], _conv_merge_bwd,
                                  dmerged, pa, attn, ex.w("convw"), "conv_merge_bwd")
    (dpb, dbias, d_sinks), _ = _swa_bwd(pb, attn, dattn, lse, biasm, sinks, "swa_bwd")
    d_relb = _bias_bwd(dbias, "bias_bwd")
    dw_in = _w_in_parts(_mm_tn(dpa, h2[None], "dw_in_proj_a")[0], _mm_tn(dpb, h2[None], "dw_in_proj_b")[0])
    dh2_b = sending([("win", dw_in, (0, 208))], _mm_acc, dpb, ex.w("wb"), "in_proj_b_bwd", F32)
    dx1, dx1b, d_mix = sending([("win", dw_in, (208, 720))], _mm_acc_rms_bwd, dpa, ex.w("wa"), "in_proj_a_bwd",
                               x=x1, gain=g["mix"], dres=dx2, addend=dh2_b)
    dw_d1 = sending([("win", dw_in, (720, 832))], _mm_tn, a1, dx1b[None], "dw_ffn1_down", scale=0.5)
    dw_d1 = dw_d1.reshape(N_DEV, -1, D)
    dgu1 = sending([("d1", dw_d1, None)], _ffn_down_bwd, dx1b, ex.w("d1"), gu1, "ffn1_down_bwd").reshape(8, s, FS)
    dw_gu1 = _mm_tn(dgu1, h1[None], "dw_ffn1_up")[0]
    dx0, _, d_ffn1 = sending([("gu1", dw_gu1, None)], _mm_acc_rms_bwd, dgu1, ex.w("gu1"), "ffn1_up_bwd",
                             x=x, gain=g["ffn1"], dres=dx1)

    relb_row = jnp.concatenate([d_relb[:, :REL_BUCKETS].T.reshape(1, REL_BUCKETS * N_HEADS), d_sinks[:, :N_HEADS],
                                jnp.zeros((1, D - REL_BUCKETS * N_HEADS - N_HEADS), F32)], axis=1)
    loss_row = jnp.concatenate([loss[0:1, 0:1], jnp.zeros((1, D - 1), F32)], axis=1)
    small = jnp.concatenate([d_ffn1, d_mix, d_xattn, d_mem, d_ffn2, d_final, relb_row, loss_row, d_convw[0:3],
                             jnp.zeros((SMALL_ROWS - ROW_CONV - 3, D), F32)], axis=0)
    return dx0, small


def _pack_small(norms, final, relb, sinks, conv_local, me):
    relb_row = jnp.concatenate([relb.reshape(1, -1), sinks.reshape(1, -1),
                                jnp.zeros((1, D - REL_BUCKETS * N_HEADS - N_HEADS), F32)], axis=1)
    conv_rows = lax.dynamic_update_slice(jnp.zeros((3, D), F32), conv_local.reshape(3, -1), (0, 128 * me))
    return jnp.concatenate(list(norms) + [final.reshape(1, D), relb_row, jnp.zeros((1, D), F32), conv_rows,
                                          jnp.zeros((SMALL_ROWS - ROW_CONV - 3, D), F32)], axis=0)


def kernel(x, mem, positions, rel_bias, ffn1_norm, ffn1_w_gu, ffn1_w_down, mix_norm, w_in, sinks, conv_w, w_out, xattn_norm, mem_norm, xattn_wq, xattn_wkv, xattn_wo, ffn2_norm, ffn2_w_gu, ffn2_w_down, final_norm, loss_target, m_rel_bias, m_ffn1_norm, m_ffn1_w_gu, m_ffn1_w_down, m_mix_norm, m_w_in, m_sinks, m_conv_w, m_w_out, m_xattn_norm, m_mem_norm, m_xattn_wq, m_xattn_wkv, m_xattn_wo, m_ffn2_norm, m_ffn2_w_gu, m_ffn2_w_down, m_final_norm, v_rel_bias, v_ffn1_norm, v_ffn1_w_gu, v_ffn1_w_down, v_mix_norm, v_w_in, v_sinks, v_conv_w, v_w_out, v_xattn_norm, v_mem_norm, v_xattn_wq, v_xattn_wkv, v_xattn_wo, v_ffn2_norm, v_ffn2_w_gu, v_ffn2_w_down, v_final_norm):
    del positions
    me = _slot(*_position())
    big = dict(gu1=(ffn1_w_gu, m_ffn1_w_gu, v_ffn1_w_gu), d1=(ffn1_w_down, m_ffn1_w_down, v_ffn1_w_down),
               win=(w_in, m_w_in, v_w_in), out=(w_out, m_w_out, v_w_out), q=(xattn_wq, m_xattn_wq, v_xattn_wq),
               kv=(xattn_wkv, m_xattn_wkv, v_xattn_wkv), o=(xattn_wo, m_xattn_wo, v_xattn_wo),
               gu2=(ffn2_w_gu, m_ffn2_w_gu, v_ffn2_w_gu), d2=(ffn2_w_down, m_ffn2_w_down, v_ffn2_w_down))
    order = list(big)
    transposed = ("gu1", "gu2", "win")
    local = {k: tuple(t[0].T if k in transposed else t[0] for t in big[k]) for k in order}
    shards = {k: local[k][0].astype(BF16) for k in order}
    shards["conv"] = jnp.concatenate([conv_w[0], jnp.zeros((5, 128), F32)], axis=0)
    ex = _Mesh(shards)
    gains = dict(ffn1=ffn1_norm, mix=mix_norm, xattn=xattn_norm, mem=mem_norm, ffn2=ffn2_norm,
                 final=final_norm.reshape(1, D))
    dx, small = _forward_backward(x[0], mem[0], loss_target[0], gains, rel_bias, sinks, ex)
    small_parts = _run_alone(_exchange_carry([], [small]), "exchange_small")[0]
    big_out = {k: _adamw(ex.received[k], *local[k], "adamw_" + k) for k in order}
    big_out = {k: [t.T if k in transposed else t for t in big_out[k]] for k in order}
    packed = [_pack_small(norms, final, relb, sk, conv, me) for norms, final, relb, sk, conv in (
        ((ffn1_norm, mix_norm, xattn_norm, mem_norm, ffn2_norm), final_norm, rel_bias, sinks, conv_w),
        ((m_ffn1_norm, m_mix_norm, m_xattn_norm, m_mem_norm, m_ffn2_norm), m_final_norm, m_rel_bias, m_sinks, m_conv_w),
        ((v_ffn1_norm, v_mix_norm, v_xattn_norm, v_mem_norm, v_ffn2_norm), v_final_norm, v_rel_bias, v_sinks, v_conv_w))]
    small_out = _adamw(small_parts, *packed, "adamw_small")

    def unpack(t):
        conv = lax.dynamic_slice(t[ROW_CONV:ROW_CONV + 3], (0, 128 * me), (3, 128))[None]
        nrel = REL_BUCKETS * N_HEADS
        return dict(ffn1_norm=t[0:1], mix_norm=t[1:2], xattn_norm=t[2:3], mem_norm=t[3:4], ffn2_norm=t[4:5],
                    final_norm=t[5], rel_bias=t[ROW_RELB, :nrel].reshape(REL_BUCKETS, N_HEADS),
                    sinks=t[ROW_RELB:ROW_RELB + 1, nrel:nrel + N_HEADS], conv_w=conv)

    names = dict(gu1="ffn1_w_gu", d1="ffn1_w_down", win="w_in", out="w_out", q="xattn_wq", kv="xattn_wkv",
                 o="xattn_wo", gu2="ffn2_w_gu", d2="ffn2_w_down")
    results = []
    for idx in range(4):
        leaves = unpack(small_out[idx])
        leaves.update({names[k]: big_out[k][idx][None] for k in order})
        results.append(leaves)
    weights = ("rel_bias", "ffn1_norm", "ffn1_w_gu", "ffn1_w_down", "mix_norm", "w_in", "sinks", "conv_w", "w_out",
               "xattn_norm", "mem_norm", "xattn_wq", "xattn_wkv", "xattn_wo", "ffn2_norm", "ffn2_w_gu", "ffn2_w_down",
               "final_norm")
    loss = small_out[0][ROW_LOSS, 0]
    return (loss, dx[None], *[leaves[n] for leaves in results for n in weights])
```

```python
import math

import numpy as np
import jax
import jax.numpy as jnp
from jax import lax
from jax.experimental import pallas as pl
from jax.experimental.pallas import tpu as pltpu

F32, BF16 = jnp.float32, jnp.bfloat16
MESH = pl.DeviceIdType.MESH

D = 1024
N_DEV = 8
D_FF = 2816
FS = D_FF // 4
HEAD = 64
N_HEADS, N_KV = 16, 4
BLK = 128
XH, XHD = 4, 256
REL_BUCKETS, REL_EXACT, REL_MAX_DIST = 32, 16, 128
EPS, NEG = 1e-6, -1e30
ADAM_LR, ADAM_B1, ADAM_B2, ADAM_EPS, ADAM_WD, ADAM_STEP = 0.001, 0.9, 0.999, 1e-08, 0.01, 10
VMEM_LIMIT_V7X = 56 * 2**20
SMALL_ROWS = 16
ROW_RELB, ROW_LOSS, ROW_CONV = 6, 7, 8


def _bucket_thresholds():
    n = np.arange(REL_MAX_DIST)
    nf = np.maximum(n, 1).astype(np.float32)
    large = REL_EXACT + (np.log(nf / np.float32(REL_EXACT)) / np.float32(math.log(REL_MAX_DIST / REL_EXACT))
                         * np.float32(REL_BUCKETS - REL_EXACT)).astype(np.int32)
    b = np.where(n < REL_EXACT, n, np.minimum(large, REL_BUCKETS - 1))
    return [int(np.argmax(b >= REL_EXACT + k)) for k in range(1, REL_BUCKETS - REL_EXACT)]


BUCKET_THRESHOLDS = _bucket_thresholds()


HBM_SPEC = pl.BlockSpec(memory_space=pl.ANY)


class _Carry:
    def __init__(self, ins, outs, sems, start, finish, mid=None, aliases=None):
        self.ins, self.outs, self.sems = list(ins), list(outs), list(sems)
        self.start, self.finish, self.mid, self.aliases = start, finish, mid, dict(aliases or {})


def _pcall(body, *, name, grid, in_specs, out_specs, out_shape, scratch=(), carry=None):
    params = pltpu.CompilerParams(dimension_semantics=("arbitrary",) * len(grid), vmem_limit_bytes=VMEM_LIMIT_V7X)
    if carry is None:
        return pl.pallas_call(body, name=name, grid=grid, in_specs=in_specs, out_specs=out_specs,
                              out_shape=out_shape, scratch_shapes=list(scratch), compiler_params=params)
    single = not isinstance(out_shape, (list, tuple))
    own_specs, own_shapes = ([out_specs], [out_shape]) if single else (list(out_specs), list(out_shape))
    n_in, n_out, n_scr = len(in_specs), len(own_shapes), len(scratch)
    n_cin, n_cout = len(carry.ins), len(carry.outs)
    steps = math.prod(grid)
    mid_step = max(steps - 1 - max(steps // 8, 1), 0)

    def carrying(*refs):
        ins, refs = refs[:n_in], refs[n_in:]
        cins, refs = refs[:n_cin], refs[n_cin:]
        outs, refs = refs[:n_out], refs[n_out:]
        couts, refs = refs[:n_cout], refs[n_cout:]
        scr, csems = refs[:n_scr], refs[n_scr:]
        step = 0
        for axis, size in enumerate(grid):
            step = step * size + pl.program_id(axis)

        @pl.when(step == 0)
        def _():
            carry.start(cins, couts, csems)

        body(*ins, *outs, *scr)
        if carry.mid is not None:
            @pl.when(step == mid_step)
            def _():
                carry.mid(cins, couts, csems)

        @pl.when(step == steps - 1)
        def _():
            carry.finish(cins, couts, csems)

    call = pl.pallas_call(carrying, name=name, grid=grid, in_specs=list(in_specs) + [HBM_SPEC] * n_cin,
                          out_specs=own_specs + [HBM_SPEC] * n_cout, out_shape=own_shapes + carry.outs,
                          scratch_shapes=list(scratch) + carry.sems, compiler_params=params,
                          input_output_aliases={n_in + i: n_out + o for i, o in carry.aliases.items()})

    def run(*args):
        res = call(*args, *carry.ins)
        return (res[0] if single else res[:n_out]), res[n_out:]

    return run


def _run_alone(carry, name):
    n_cin, n_cout = len(carry.ins), len(carry.outs)

    def body(*refs):
        cins, couts, csems = refs[:n_cin], refs[n_cin:n_cin + n_cout], refs[n_cin + n_cout:]
        carry.start(cins, couts, csems)
        if carry.mid is not None:
            carry.mid(cins, couts, csems)
        carry.finish(cins, couts, csems)

    return pl.pallas_call(body, name=name, in_specs=[HBM_SPEC] * n_cin, out_specs=[HBM_SPEC] * n_cout,
                          out_shape=carry.outs, scratch_shapes=carry.sems,
                          input_output_aliases=carry.aliases)(*carry.ins)


def _dot(a, b):
    return jnp.dot(a, b, preferred_element_type=F32)


def _dot_nt(a, b):
    return lax.dot_general(a, b, (((1,), (1,)), ((), ())), preferred_element_type=F32)


def _dot_tn(a, b):
    return lax.dot_general(a, b, (((0,), (0,)), ((), ())), preferred_element_type=F32)


def _sds(shape, dtype):
    return jax.ShapeDtypeStruct(tuple(shape), dtype)


def _carried(call, args, carry):
    return call(*args) if carry is not None else (call(*args), ())


def _rmsnorm(x, g, name, carry=None):
    m, d = x.shape
    tm = min(512, m)

    def body(x_ref, g_ref, h_ref):
        xv = x_ref[...]
        r = lax.rsqrt(jnp.mean(xv * xv, axis=-1, keepdims=True) + EPS)
        h_ref[...] = (xv * r * g_ref[...]).astype(BF16)

    call = _pcall(body, name=name, grid=(m // tm,), carry=carry,
                  in_specs=[pl.BlockSpec((tm, d), lambda i: (i, 0)), pl.BlockSpec((1, d), lambda i: (0, 0))],
                  out_specs=pl.BlockSpec((tm, d), lambda i: (i, 0)), out_shape=_sds((m, d), BF16))
    return _carried(call, (x, g), carry)


def _mm_nn(a, b, name, tm=512, bt=False, carry=None):
    m, k = a.shape
    nj = b.shape[0]
    n = b.shape[1] if bt else b.shape[2]
    tm = min(tm, m)
    dot = _dot_nt if bt else _dot

    def body(a_ref, b_ref, o_ref):
        o_ref[...] = dot(a_ref[...], b_ref[...]).astype(BF16)

    call = _pcall(body, name=name, grid=(nj, m // tm),
                  in_specs=[pl.BlockSpec((tm, k), lambda j, i: (i, 0)),
                            pl.BlockSpec((None,) + b.shape[1:], lambda j, i: (j, 0, 0))],
                  out_specs=pl.BlockSpec((None, tm, n), lambda j, i: (j, i, 0)),
                  out_shape=_sds((nj, m, n), BF16), carry=carry)
    return _carried(call, (a, b), carry)


def _ffn_up(h, w4, name, tm=512, carry=None):
    s, d = h.shape
    tm = min(tm, s)

    def body(h_ref, w_ref, gu_ref, a_ref):
        hv = h_ref[...]
        g = _dot_nt(hv, w_ref[0])
        u = _dot_nt(hv, w_ref[1])
        gu_ref[0] = g.astype(BF16)
        gu_ref[1] = u.astype(BF16)
        a_ref[...] = (g * jax.nn.sigmoid(g) * u).astype(BF16)

    call = _pcall(body, name=name, grid=(4, s // tm),
                  in_specs=[pl.BlockSpec((tm, d), lambda p, i: (i, 0)),
                            pl.BlockSpec((2, None, FS, d), lambda p, i: (0, p, 0, 0))],
                  out_specs=[pl.BlockSpec((2, None, tm, FS), lambda p, i: (0, p, i, 0)),
                             pl.BlockSpec((None, tm, FS), lambda p, i: (p, i, 0))],
                  out_shape=[_sds((2, 4, s, FS), BF16), _sds((4, s, FS), BF16)], carry=carry)
    return _carried(call, (h, w4), carry)


def _mm_res_norm(a, w, xres, gain, scale, name, tm=512, carry=None):
    npart, s, kp = a.shape
    tm = min(tm, s)

    def body(a_ref, w_ref, x_ref, g_ref, xo_ref, h_ref):
        acc = _dot(a_ref[0], w_ref[0])
        for p in range(1, npart):
            acc = acc + _dot(a_ref[p], w_ref[p])
        xn = x_ref[...] + scale * acc
        xo_ref[...] = xn
        r = lax.rsqrt(jnp.mean(xn * xn, axis=-1, keepdims=True) + EPS)
        h_ref[...] = (xn * r * g_ref[...]).astype(BF16)

    call = _pcall(body, name=name, grid=(s // tm,),
                  in_specs=[pl.BlockSpec((npart, tm, kp), lambda i: (0, i, 0)),
                            pl.BlockSpec((npart, kp, D), lambda i: (0, 0, 0)),
                            pl.BlockSpec((tm, D), lambda i: (i, 0)),
                            pl.BlockSpec((1, D), lambda i: (0, 0))],
                  out_specs=[pl.BlockSpec((tm, D), lambda i: (i, 0)), pl.BlockSpec((tm, D), lambda i: (i, 0))],
                  out_shape=[_sds((s, D), F32), _sds((s, D), BF16)], carry=carry)
    return _carried(call, (a, w, xres, gain), carry)


def _ffn_down_loss(a, w, xres, gain, target, name, tm=512):
    npart, s, kp = a.shape
    tm = min(tm, s)

    def body(a_ref, w_ref, x_ref, g_ref, t_ref, dx_ref, dxb_ref, loss_ref, dg_ref):
        i = pl.program_id(0)
        acc = _dot(a_ref[0], w_ref[0])
        for p in range(1, npart):
            acc = acc + _dot(a_ref[p], w_ref[p])
        xn = x_ref[...] + 0.5 * acc
        r = lax.rsqrt(jnp.mean(xn * xn, axis=-1, keepdims=True) + EPS)
        xh = xn * r
        gv = g_ref[...]
        err = xh * gv - t_ref[...]
        part = 0.5 * jnp.sum(jnp.mean(err * err, axis=-1, keepdims=True), axis=0, keepdims=True)
        dy = err * (1.0 / D)
        dyg = dy * gv
        dxn = r * (dyg - xh * jnp.mean(dyg * xh, axis=-1, keepdims=True))
        dx_ref[...] = dxn
        dxb_ref[...] = dxn.astype(BF16)

        @pl.when(i == 0)
        def _():
            loss_ref[...] = jnp.zeros_like(loss_ref)
            dg_ref[...] = jnp.zeros_like(dg_ref)

        loss_ref[...] += jnp.broadcast_to(part, loss_ref.shape)
        dg_ref[...] += jnp.sum(dy * xh, axis=0, keepdims=True)

    return _pcall(body, name=name, grid=(s // tm,),
                  in_specs=[pl.BlockSpec((npart, tm, kp), lambda i: (0, i, 0)),
                            pl.BlockSpec((npart, kp, D), lambda i: (0, 0, 0)),
                            pl.BlockSpec((tm, D), lambda i: (i, 0)),
                            pl.BlockSpec((1, D), lambda i: (0, 0)),
                            pl.BlockSpec((tm, D), lambda i: (i, 0))],
                  out_specs=[pl.BlockSpec((tm, D), lambda i: (i, 0)), pl.BlockSpec((tm, D), lambda i: (i, 0)),
                             pl.BlockSpec((8, 128), lambda i: (0, 0)), pl.BlockSpec((1, D), lambda i: (0, 0))],
                  out_shape=[_sds((s, D), F32), _sds((s, D), BF16), _sds((8, 128), F32), _sds((1, D), F32)],
                  )(a, w, xres, gain, target)


def _band_tiles():
    i = lax.broadcasted_iota(jnp.int32, (BLK, 2 * BLK), 0)
    j = lax.broadcasted_iota(jnp.int32, (BLK, 2 * BLK), 1)
    rel = BLK + i - j
    large = jnp.full_like(rel, REL_EXACT)
    for t in BUCKET_THRESHOLDS:
        large = large + (rel >= t).astype(jnp.int32)
    bucket = jnp.where(rel < REL_EXACT, rel, large)
    visible = jnp.logical_and(rel >= 0, rel < BLK)
    return bucket, visible


def _bias_build(rel_bias, name):
    def body(rb_ref, o_ref):
        bucket, visible = _band_tiles()

        def per_head(h, carry):
            acc = jnp.zeros((BLK, 2 * BLK), F32)
            for b in range(REL_BUCKETS):
                acc = jnp.where(bucket == b, rb_ref[b, h], acc)
            o_ref[h] = jnp.where(visible, acc, NEG)
            return carry

        lax.fori_loop(0, N_HEADS, per_head, 0)

    return _pcall(body, name=name, grid=(1,),
                  in_specs=[pl.BlockSpec(memory_space=pltpu.SMEM)],
                  out_specs=pl.BlockSpec((N_HEADS, BLK, 2 * BLK), lambda i: (0, 0, 0)),
                  out_shape=_sds((N_HEADS, BLK, 2 * BLK), F32))(rel_bias)


def _bias_bwd(dbias, name):
    def body(db_ref, o_ref):
        bucket, _ = _band_tiles()
        lane = lax.broadcasted_iota(jnp.int32, (N_HEADS, 128), 1)

        def per_bucket(b, out):
            mb = (bucket == b).astype(F32)
            per_col = jnp.sum(db_ref[...] * mb[None, :, :], axis=1)
            return jnp.where(lane == b, jnp.sum(per_col, axis=1, keepdims=True), out)

        o_ref[...] = lax.fori_loop(0, REL_BUCKETS, per_bucket, jnp.zeros((N_HEADS, 128), F32))

    return _pcall(body, name=name, grid=(1,),
                  in_specs=[pl.BlockSpec((N_HEADS, BLK, 2 * BLK), lambda i: (0, 0, 0))],
                  out_specs=pl.BlockSpec((N_HEADS, 128), lambda i: (0, 0)),
                  out_shape=_sds((N_HEADS, 128), F32))(dbias)


def _kv_group(kp_ref, kc_ref, g):
    ks = slice(HEAD * g, HEAD * (g + 1))
    vs = slice(N_KV * HEAD + HEAD * g, N_KV * HEAD + HEAD * (g + 1))
    kg = jnp.concatenate([kp_ref[:, ks], kc_ref[:, ks]], axis=0)
    vg = jnp.concatenate([kp_ref[:, vs], kc_ref[:, vs]], axis=0)
    return kg, vg


def _swa_fwd(pb, biasm, sinks, name, carry=None):
    _, s, _ = pb.shape
    nb = s // BLK
    kvw = 2 * N_KV * HEAD

    def body(q_ref, kc_ref, kp_ref, b_ref, sk_ref, o_ref, lse_ref):
        n = pl.program_id(0)
        col = lax.broadcasted_iota(jnp.int32, (BLK, 2 * BLK), 1)
        no_prev = jnp.logical_and(n == 0, col < BLK)
        lane = lax.broadcasted_iota(jnp.int32, (BLK, 128), 1)
        lse_t = jnp.zeros((BLK, 128), F32)
        for g in range(N_KV):
            kg, vg = _kv_group(kp_ref, kc_ref, g)
            for r in range(N_HEADS // N_KV):
                h = g * (N_HEADS // N_KV) + r
                hs = slice(HEAD * h, HEAD * (h + 1))
                sc = _dot_nt(q_ref[:, hs], kg) * (HEAD ** -0.5) + b_ref[h]
                sc = jnp.where(no_prev, NEG, sc)
                sk = sk_ref[0, h]
                m = jnp.maximum(jnp.max(sc, axis=1, keepdims=True), sk)
                p = jnp.exp(sc - m)
                l = jnp.sum(p, axis=1, keepdims=True) + jnp.exp(sk - m)
                o = _dot(p.astype(BF16), vg) * (1.0 / l)
                o_ref[:, hs] = o.astype(BF16)
                lse_t = jnp.where(lane == h, m + jnp.log(l), lse_t)
        lse_ref[...] = lse_t

    call = _pcall(body, name=name, grid=(nb,),
                  in_specs=[pl.BlockSpec((None, BLK, D), lambda n: (0, n, 0)),
                            pl.BlockSpec((None, BLK, kvw), lambda n: (1, n, 0)),
                            pl.BlockSpec((None, BLK, kvw), lambda n: (1, jnp.maximum(n - 1, 0), 0)),
                            pl.BlockSpec((N_HEADS, BLK, 2 * BLK), lambda n: (0, 0, 0)),
                            pl.BlockSpec(memory_space=pltpu.SMEM)],
                  out_specs=[pl.BlockSpec((BLK, D), lambda n: (n, 0)), pl.BlockSpec((BLK, 128), lambda n: (n, 0))],
                  out_shape=[_sds((s, D), BF16), _sds((s, 128), F32)], carry=carry)
    return _carried(call, (pb, pb, pb, biasm, sinks), carry)


def _swa_bwd(pb, attn, dattn, lse, biasm, sinks, name, carry=None):
    _, s, _ = pb.shape
    nb = s // BLK
    kvw = 2 * N_KV * HEAD
    grp = N_HEADS // N_KV

    def body(q_ref, kc_ref, kp_ref, o_ref, do_ref, lse_ref, b_ref, sk_ref, dpb_ref, dbias_ref, dsk_ref,
             dq_hold, kv_hold, dq_new, kv_prev, kv_cur):
        n = pl.program_id(0)

        @pl.when(n == 0)
        def _():
            dbias_ref[...] = jnp.zeros_like(dbias_ref)
            dsk_ref[...] = jnp.zeros_like(dsk_ref)
            dq_hold[...] = jnp.zeros_like(dq_hold)
            kv_hold[...] = jnp.zeros_like(kv_hold)

        @pl.when(n < nb)
        def _():
            col = lax.broadcasted_iota(jnp.int32, (BLK, 2 * BLK), 1)
            no_prev = jnp.logical_and(n == 0, col < BLK)
            lane = lax.broadcasted_iota(jnp.int32, (1, 128), 1)
            dsk = jnp.zeros((1, 128), F32)
            for g in range(N_KV):
                kg, vg = _kv_group(kp_ref, kc_ref, g)
                dk_g = jnp.zeros((2 * BLK, HEAD), F32)
                dv_g = jnp.zeros((2 * BLK, HEAD), F32)
                for r in range(grp):
                    h = g * grp + r
                    hs = slice(HEAD * h, HEAD * (h + 1))
                    qh = q_ref[:, hs]
                    dob = do_ref[:, hs]
                    lse_h = lse_ref[:, h:h + 1]
                    sc = _dot_nt(qh, kg) * (HEAD ** -0.5) + b_ref[h]
                    sc = jnp.where(no_prev, NEG, sc)
                    p = jnp.exp(sc - lse_h)
                    dp = _dot_nt(dob, vg)
                    delta = jnp.sum(dob.astype(F32) * o_ref[:, hs].astype(F32), axis=1, keepdims=True)
                    ds = p * (dp - delta)
                    dbias_ref[h] += ds
                    p_sink = jnp.exp(sk_ref[0, h] - lse_h)
                    dsk = jnp.where(lane == h, dsk - jnp.sum(p_sink * delta, keepdims=True), dsk)
                    dsb = (ds * (HEAD ** -0.5)).astype(BF16)
                    dq_new[:, hs] = _dot(dsb, kg)
                    dk_g = dk_g + _dot_tn(dsb, qh)
                    dv_g = dv_g + _dot_tn(p.astype(BF16), dob)
                ks = slice(HEAD * g, HEAD * (g + 1))
                vs = slice(N_KV * HEAD + HEAD * g, N_KV * HEAD + HEAD * (g + 1))
                kv_prev[:, ks] = dk_g[:BLK]
                kv_cur[:, ks] = dk_g[BLK:]
                kv_prev[:, vs] = dv_g[:BLK]
                kv_cur[:, vs] = dv_g[BLK:]
            dsk_ref[...] += dsk

        @pl.when(n == nb)
        def _():
            kv_prev[...] = jnp.zeros_like(kv_prev)

        dpb_ref[0] = dq_hold[...].astype(BF16)
        dpb_ref[1, :, 0:kvw] = (kv_hold[...] + kv_prev[...]).astype(BF16)
        dpb_ref[1, :, kvw:D] = jnp.zeros((BLK, D - kvw), BF16)

        @pl.when(n < nb)
        def _():
            dq_hold[...] = dq_new[...]
            kv_hold[...] = kv_cur[...]

    def cur(n):
        return jnp.minimum(n, nb - 1)

    call = _pcall(body, name=name, grid=(nb + 1,), carry=carry,
                  in_specs=[pl.BlockSpec((None, BLK, D), lambda n: (0, cur(n), 0)),
                            pl.BlockSpec((None, BLK, kvw), lambda n: (1, cur(n), 0)),
                            pl.BlockSpec((None, BLK, kvw), lambda n: (1, jnp.maximum(cur(n) - 1, 0), 0)),
                            pl.BlockSpec((BLK, D), lambda n: (cur(n), 0)),
                            pl.BlockSpec((BLK, D), lambda n: (cur(n), 0)),
                            pl.BlockSpec((BLK, 128), lambda n: (cur(n), 0)),
                            pl.BlockSpec((N_HEADS, BLK, 2 * BLK), lambda n: (0, 0, 0)),
                            pl.BlockSpec(memory_space=pltpu.SMEM)],
                  out_specs=[pl.BlockSpec((2, BLK, D), lambda n: (0, jnp.maximum(n - 1, 0), 0)),
                             pl.BlockSpec((N_HEADS, BLK, 2 * BLK), lambda n: (0, 0, 0)),
                             pl.BlockSpec((1, 128), lambda n: (0, 0))],
                  out_shape=[_sds((2, s, D), BF16), _sds((N_HEADS, BLK, 2 * BLK), F32), _sds((1, 128), F32)],
                  scratch=[pltpu.VMEM((BLK, D), F32), pltpu.VMEM((BLK, kvw), F32), pltpu.VMEM((BLK, D), F32),
                           pltpu.VMEM((BLK, kvw), F32), pltpu.VMEM((BLK, kvw), F32)])
    return _carried(call, (pb, pb, pb, attn, dattn, lse, biasm, sinks), carry)


HALO = 16
CW = 512


def _conv_taps(cu, halo_cu, first_tile):
    row = lax.broadcasted_iota(jnp.int32, cu.shape, 0)
    halo_cu = jnp.where(first_tile, 0.0, halo_cu)
    c1 = jnp.where(row == 0, halo_cu[HALO - 1:HALO], pltpu.roll(cu, 1, 0))
    c2 = jnp.where(row == 0, halo_cu[HALO - 2:HALO - 1],
                   jnp.where(row == 1, halo_cu[HALO - 1:HALO], pltpu.roll(cu, 2, 0)))
    return c1, c2


def _conv_merge_fwd(pa, attn, convw, name, ts=512):
    _, s, _ = pa.shape
    ts = min(ts, s)
    hb = ts // HALO

    def body(pa_ref, hp_ref, at_ref, w_ref, o_ref):
        i = pl.program_id(1)
        cu = pa_ref[0].astype(F32) * pa_ref[2].astype(F32)
        c1, c2 = _conv_taps(cu, hp_ref[0].astype(F32) * hp_ref[2].astype(F32), i == 0)
        w = w_ref[...]
        c3 = w[0:1] * c2 + w[1:2] * c1 + w[2:3] * cu
        conv = pa_ref[1].astype(F32) * c3
        o_ref[...] = (jax.nn.sigmoid(pa_ref[3].astype(F32)) * at_ref[...].astype(F32)
                      + jax.nn.sigmoid(pa_ref[4].astype(F32)) * conv).astype(BF16)

    return _pcall(body, name=name, grid=(D // CW, s // ts),
                  in_specs=[pl.BlockSpec((5, ts, CW), lambda c, i: (0, i, c)),
                            pl.BlockSpec((5, HALO, CW), lambda c, i: (0, jnp.maximum(i * hb - 1, 0), c)),
                            pl.BlockSpec((ts, CW), lambda c, i: (i, c)),
                            pl.BlockSpec((8, CW), lambda c, i: (0, c))],
                  out_specs=pl.BlockSpec((ts, CW), lambda c, i: (i, c)),
                  out_shape=_sds((s, D), BF16))(pa, pa, attn, convw)


def _conv_merge_bwd(dmerged, pa, attn, convw, name, ts=512, carry=None):
    _, s, _ = pa.shape
    ts = min(ts, s)
    hb = ts // HALO
    last_hb = s // HALO - 1

    def body(dm_ref, pa_ref, at_ref, w_ref, hp_ref, hn_ref, dmn_ref, dat_ref, dpa_ref, dw_ref):
        i = pl.program_id(1)
        last = i == pl.num_programs(1) - 1
        dm = dm_ref[...].astype(F32)
        cp, bp, u = pa_ref[0].astype(F32), pa_ref[1].astype(F32), pa_ref[2].astype(F32)
        sa = jax.nn.sigmoid(pa_ref[3].astype(F32))
        sc = jax.nn.sigmoid(pa_ref[4].astype(F32))
        at = at_ref[...].astype(F32)
        cu = cp * u
        c1, c2 = _conv_taps(cu, hp_ref[0].astype(F32) * hp_ref[2].astype(F32), i == 0)
        w = w_ref[...]
        c3 = w[0:1] * c2 + w[1:2] * c1 + w[2:3] * cu
        dconv = dm * sc
        dc3 = dconv * bp
        nxt = dmn_ref[...].astype(F32) * jax.nn.sigmoid(hn_ref[4].astype(F32)) * hn_ref[1].astype(F32)
        nxt = jnp.where(last, 0.0, nxt)
        row = lax.broadcasted_iota(jnp.int32, dc3.shape, 0)
        d1 = jnp.where(row == ts - 1, nxt[0:1], pltpu.roll(dc3, ts - 1, 0))
        d2 = jnp.where(row == ts - 2, nxt[0:1], jnp.where(row == ts - 1, nxt[1:2], pltpu.roll(dc3, ts - 2, 0)))
        dcu = w[2:3] * dc3 + w[1:2] * d1 + w[0:1] * d2
        dat_ref[...] = (dm * sa).astype(BF16)
        dpa_ref[0] = (dcu * u).astype(BF16)
        dpa_ref[1] = (dconv * c3).astype(BF16)
        dpa_ref[2] = (dcu * cp).astype(BF16)
        dpa_ref[3] = (dm * at * sa * (1.0 - sa)).astype(BF16)
        dpa_ref[4] = (dm * bp * c3 * sc * (1.0 - sc)).astype(BF16)

        @pl.when(i == 0)
        def _():
            dw_ref[...] = jnp.zeros_like(dw_ref)

        dw_ref[0:1, :] += jnp.sum(dc3 * c2, axis=0, keepdims=True)
        dw_ref[1:2, :] += jnp.sum(dc3 * c1, axis=0, keepdims=True)
        dw_ref[2:3, :] += jnp.sum(dc3 * cu, axis=0, keepdims=True)

    call = _pcall(body, name=name, grid=(D // CW, s // ts), carry=carry,
                  in_specs=[pl.BlockSpec((ts, CW), lambda c, i: (i, c)),
                            pl.BlockSpec((5, ts, CW), lambda c, i: (0, i, c)),
                            pl.BlockSpec((ts, CW), lambda c, i: (i, c)),
                            pl.BlockSpec((8, CW), lambda c, i: (0, c)),
                            pl.BlockSpec((5, HALO, CW), lambda c, i: (0, jnp.maximum(i * hb - 1, 0), c)),
                            pl.BlockSpec((5, HALO, CW), lambda c, i: (0, jnp.minimum((i + 1) * hb, last_hb), c)),
                            pl.BlockSpec((HALO, CW), lambda c, i: (jnp.minimum((i + 1) * hb, last_hb), c))],
                  out_specs=[pl.BlockSpec((ts, CW), lambda c, i: (i, c)),
                             pl.BlockSpec((5, ts, CW), lambda c, i: (0, i, c)),
                             pl.BlockSpec((8, CW), lambda c, i: (0, c))],
                  out_shape=[_sds((s, D), BF16), _sds((5, s, D), BF16), _sds((8, D), F32)])
    return _carried(call, (dmerged, pa, attn, convw, pa, pa, dmerged), carry)


def _xattn_fwd(q, kv, name, tq=512):
    s, _ = q.shape
    nm = kv.shape[1]
    tq = min(tq, s)

    def body(q_ref, kv_ref, o_ref, lse_ref):
        lane = lax.broadcasted_iota(jnp.int32, (tq, 128), 1)
        lse_t = jnp.zeros((tq, 128), F32)
        for h in range(XH):
            hs = slice(XHD * h, XHD * (h + 1))
            sc = _dot_nt(q_ref[:, hs], kv_ref[h]) * (XHD ** -0.5)
            m = jnp.max(sc, axis=1, keepdims=True)
            p = jnp.exp(sc - m)
            l = jnp.sum(p, axis=1, keepdims=True)
            o_ref[:, hs] = (_dot(p.astype(BF16), kv_ref[XH + h]) * (1.0 / l)).astype(BF16)
            lse_t = jnp.where(lane == h, m + jnp.log(l), lse_t)
        lse_ref[...] = lse_t

    return _pcall(body, name=name, grid=(s // tq,),
                  in_specs=[pl.BlockSpec((tq, D), lambda i: (i, 0)), pl.BlockSpec((2 * XH, nm, XHD), lambda i: (0, 0, 0))],
                  out_specs=[pl.BlockSpec((tq, D), lambda i: (i, 0)), pl.BlockSpec((tq, 128), lambda i: (i, 0))],
                  out_shape=[_sds((s, D), BF16), _sds((s, 128), F32)])(q, kv)


def _xattn_bwd(q, kv, o, do, lse, name, tq=512, carry=None):
    s, _ = q.shape
    nm = kv.shape[1]
    tq = min(tq, s)

    def body(q_ref, kv_ref, o_ref, do_ref, lse_ref, dq_ref, dkv_ref):
        @pl.when(pl.program_id(0) == 0)
        def _():
            dkv_ref[...] = jnp.zeros_like(dkv_ref)

        for h in range(XH):
            hs = slice(XHD * h, XHD * (h + 1))
            qh, kh, vh, dob = q_ref[:, hs], kv_ref[h], kv_ref[XH + h], do_ref[:, hs]
            p = jnp.exp(_dot_nt(qh, kh) * (XHD ** -0.5) - lse_ref[:, h:h + 1])
            dp = _dot_nt(dob, vh)
            delta = jnp.sum(dob.astype(F32) * o_ref[:, hs].astype(F32), axis=1, keepdims=True)
            dsb = (p * (dp - delta) * (XHD ** -0.5)).astype(BF16)
            dq_ref[:, hs] = _dot(dsb, kh).astype(BF16)
            dkv_ref[h] += _dot_tn(dsb, qh)
            dkv_ref[XH + h] += _dot_tn(p.astype(BF16), dob)

    call = _pcall(body, name=name, grid=(s // tq,), carry=carry,
                  in_specs=[pl.BlockSpec((tq, D), lambda i: (i, 0)), pl.BlockSpec((2 * XH, nm, XHD), lambda i: (0, 0, 0)),
                            pl.BlockSpec((tq, D), lambda i: (i, 0)), pl.BlockSpec((tq, D), lambda i: (i, 0)),
                            pl.BlockSpec((tq, 128), lambda i: (i, 0))],
                  out_specs=[pl.BlockSpec((tq, D), lambda i: (i, 0)), pl.BlockSpec((2 * XH, nm, XHD), lambda i: (0, 0, 0))],
                  out_shape=[_sds((s, D), BF16), _sds((2 * XH, nm, XHD), F32)])
    return _carried(call, (q, kv, o, do, lse), carry)


def _ffn_down_bwd(dxb, wd4, gu4, name, tm=512, carry=None):
    s, _ = dxb.shape
    tm = min(tm, s)

    def body(dx_ref, w_ref, gu_ref, o_ref):
        da = 0.5 * _dot_nt(dx_ref[...], w_ref[...])
        g = gu_ref[0].astype(F32)
        u = gu_ref[1].astype(F32)
        sg = jax.nn.sigmoid(g)
        o_ref[0] = (da * u * sg * (1.0 + g * (1.0 - sg))).astype(BF16)
        o_ref[1] = (da * g * sg).astype(BF16)

    call = _pcall(body, name=name, grid=(4, s // tm), carry=carry,
                  in_specs=[pl.BlockSpec((tm, D), lambda p, i: (i, 0)),
                            pl.BlockSpec((None, FS, D), lambda p, i: (p, 0, 0)),
                            pl.BlockSpec((2, None, tm, FS), lambda p, i: (0, p, i, 0))],
                  out_specs=pl.BlockSpec((2, None, tm, FS), lambda p, i: (0, p, i, 0)),
                  out_shape=_sds((2, 4, s, FS), BF16))
    return _carried(call, (dxb, wd4, gu4), carry)


def _mm_tn(a, b, name, scale=1.0, tk=1024, tn=None, carry=None):
    pa_n, s, m = a.shape
    pb_n, _, n = b.shape
    po = max(pa_n, pb_n)
    tk = min(tk, s)
    tn = n if tn is None else tn
    nk = s // tk

    def body(a_ref, b_ref, o_ref, acc_ref):
        k = pl.program_id(2)

        @pl.when(k == 0)
        def _():
            acc_ref[...] = jnp.zeros_like(acc_ref)

        acc_ref[...] += _dot_tn(a_ref[...], b_ref[...])

        @pl.when(k == nk - 1)
        def _():
            o_ref[...] = (scale * acc_ref[...]).astype(BF16)

    call = _pcall(body, name=name, grid=(po, n // tn, nk), carry=carry,
                  in_specs=[pl.BlockSpec((None, tk, m), lambda o, j, k: (o if pa_n > 1 else 0, k, 0)),
                            pl.BlockSpec((None, tk, tn), lambda o, j, k: (o if pb_n > 1 else 0, k, j))],
                  out_specs=pl.BlockSpec((None, m, tn), lambda o, j, k: (o, 0, j)),
                  out_shape=_sds((po, m, n), BF16), scratch=[pltpu.VMEM((m, tn), F32)])
    return _carried(call, (a, b), carry)


def _sum_dots(a_ref, b_ref, nj, bt):
    dot = _dot_nt if bt else _dot
    acc = dot(a_ref[0], b_ref[0])
    for j in range(1, nj):
        acc = acc + dot(a_ref[j], b_ref[j])
    return acc


def _mm_acc(a, b, name, out_dtype, tm=512, bt=False, carry=None):
    nj, s, k = a.shape
    n = b.shape[1] if bt else b.shape[2]
    tm = min(tm, s)

    def body(a_ref, b_ref, o_ref):
        o_ref[...] = _sum_dots(a_ref, b_ref, nj, bt).astype(out_dtype)

    call = _pcall(body, name=name, grid=(s // tm,), carry=carry,
                  in_specs=[pl.BlockSpec((nj, tm, k), lambda i: (0, i, 0)),
                            pl.BlockSpec(b.shape, lambda i: (0, 0, 0))],
                  out_specs=pl.BlockSpec((tm, n), lambda i: (i, 0)), out_shape=_sds((s, n), out_dtype))
    return _carried(call, (a, b), carry)


def _mm_acc_rms_bwd(a, b, name, *, x, gain, dres, addend=None, tm=512, bt=False, carry=None):
    nj, s, k = a.shape
    n = b.shape[1] if bt else b.shape[2]
    tm = min(tm, s)
    has_add = addend is not None

    def body(*refs):
        a_ref, b_hbm, x_ref, g_ref, r_ref = refs[:5]
        add_ref = refs[5] if has_add else None
        dx_ref, dxb_ref, dg_ref, b_ref, b_sem = refs[5 + has_add:]

        @pl.when(pl.program_id(0) == 0)
        def _():
            load = pltpu.make_async_copy(b_hbm, b_ref, b_sem)
            load.start()
            dg_ref[...] = jnp.zeros_like(dg_ref)
            load.wait()

        dh = _sum_dots(a_ref, b_ref, nj, bt)
        if has_add:
            dh = dh + add_ref[...]
        xv = x_ref[...]
        r = lax.rsqrt(jnp.mean(xv * xv, axis=-1, keepdims=True) + EPS)
        xh = xv * r
        dyg = dh * g_ref[...]
        dx = r_ref[...] + r * (dyg - xh * jnp.mean(dyg * xh, axis=-1, keepdims=True))
        dx_ref[...] = dx
        dxb_ref[...] = dx.astype(BF16)
        dg_ref[...] += jnp.sum(dh * xh, axis=0, keepdims=True)

    row = pl.BlockSpec((tm, n), lambda i: (i, 0))
    in_specs = [pl.BlockSpec((nj, tm, k), lambda i: (0, i, 0)), HBM_SPEC,
                row, pl.BlockSpec((1, n), lambda i: (0, 0)), row] + ([row] if has_add else [])
    args = (a, b, x, gain, dres) + ((addend,) if has_add else ())
    call = _pcall(body, name=name, grid=(s // tm,), in_specs=in_specs, carry=carry,
                  out_specs=[row, row, pl.BlockSpec((1, n), lambda i: (0, 0))],
                  out_shape=[_sds((s, n), F32), _sds((s, n), BF16), _sds((1, n), F32)],
                  scratch=[pltpu.VMEM(b.shape, b.dtype), pltpu.SemaphoreType.DMA(())])
    return _carried(call, args, carry)


def _adam(w, g, m, v):
    m2 = ADAM_B1 * m + (1.0 - ADAM_B1) * g
    v2 = ADAM_B2 * v + (1.0 - ADAM_B2) * (g * g)
    m_hat = m2 / (1.0 - ADAM_B1 ** ADAM_STEP)
    v_hat = v2 / (1.0 - ADAM_B2 ** ADAM_STEP)
    delta = -ADAM_LR * (m_hat / (jnp.sqrt(v_hat) + ADAM_EPS) + ADAM_WD * w)
    return delta, m2, v2


def _adamw(parts, w, m, v, name):
    _, r, c = parts.shape
    tr = max(t for t in range(16, 257, 16) if r % t == 0)

    def body(p_ref, w_ref, m_ref, v_ref, g_ref, d_ref, m2_ref, v2_ref):
        g = p_ref[0].astype(F32)
        for i in range(1, N_DEV):
            g = g + p_ref[i].astype(F32)
        delta, m2, v2 = _adam(w_ref[...], g, m_ref[...], v_ref[...])
        g_ref[...] = g
        d_ref[...] = delta
        m2_ref[...] = m2
        v2_ref[...] = v2

    blk = pl.BlockSpec((tr, c), lambda i: (i, 0))
    return _pcall(body, name=name, grid=(r // tr,),
                  in_specs=[pl.BlockSpec((N_DEV, tr, c), lambda i: (0, i, 0)), blk, blk, blk],
                  out_specs=[blk] * 4, out_shape=[_sds((r, c), F32)] * 4)(parts, w, m, v)


def _position():
    return lax.axis_index("x"), lax.axis_index("y"), lax.axis_index("c")


def _slot(px, py, pc):
    return 4 * px + 2 * py + pc


def _row_window(ref, rows):
    r0, r1 = rows
    return ref if (r0, r1) == (0, ref.shape[0]) else ref.at[pl.ds(r0, r1 - r0)]


def _split_items(items):
    sources = [src for src, _, _ in items]
    begun = [(a, dest) for a, (_, _, dest) in enumerate(items) if dest is not None]
    aliases = {len(sources) + k: a for k, (a, _) in enumerate(begun)}
    return sources + [dest for _, dest in begun], [rows for _, rows, _ in items], aliases


def _gather_carry(items):
    na = len(items)
    carry_ins, windows, aliases = _split_items(items)

    def plan(ins, outs, sems):
        send_sems, recv_sems, local_sems = sems
        x, y, c = _position()
        me, sibling = (x, y, c), (x, y, 1 - c)
        chips = [(1 - x, y), (x, 1 - y), (1 - x, 1 - y)]
        ins = [_row_window(ins[a], windows[a]) for a in range(na)]

        def block_rows(a, block):
            return _row_window(outs[a].at[_slot(*block)], windows[a])

        def copy(a, k, block, to, src=None):
            rows = block_rows(a, block)
            return pltpu.make_async_remote_copy(src_ref=rows if src is None else src, dst_ref=rows,
                                                send_sem=send_sems.at[k, a], recv_sem=recv_sems.at[k, a],
                                                device_id=to, device_id_type=MESH)

        mine = [pltpu.make_async_copy(ins[a], block_rows(a, me), local_sems.at[a]) for a in range(na)]
        first = [copy(a, 0, me, sibling, src=ins[a]) for a in range(na)]
        for j, chip in enumerate(chips):
            first += [copy(a, 1 + j, me, (*chip, c), src=ins[a]) for a in range(na)]
        landed = [[copy(a, 1 + j, (*chip, c), me) for a in range(na)] for j, chip in enumerate(chips)]
        passed = [[copy(a, 4 + j, (*chip, c), sibling) for a in range(na)] for j, chip in enumerate(chips)]
        from_sibling = [copy(a, 0, sibling, me) for a in range(na)]
        for j, chip in enumerate(chips):
            from_sibling += [copy(a, 4 + j, (*chip, 1 - c), me) for a in range(na)]
        return mine, first, landed, passed, from_sibling

    def start(ins, outs, sems):
        mine, first, _, _, _ = plan(ins, outs, sems)
        for cp in mine + first:
            cp.start()

    def mid(ins, outs, sems):
        _, _, landed, passed, _ = plan(ins, outs, sems)
        for over_ici, onward in zip(landed, passed):
            for cp, fwd in zip(over_ici, onward):
                cp.wait_recv()
                fwd.start()

    def finish(ins, outs, sems):
        mine, first, _, passed, from_sibling = plan(ins, outs, sems)
        for cp in from_sibling:
            cp.wait_recv()
        for cp in first + [fwd for onward in passed for fwd in onward]:
            cp.wait_send()
        for cp in mine:
            cp.wait()

    return _Carry(carry_ins, [_sds((N_DEV,) + src.shape, src.dtype) for src, _, _ in items],
                  [pltpu.SemaphoreType.DMA((7, na)), pltpu.SemaphoreType.DMA((7, na)),
                   pltpu.SemaphoreType.DMA((na,))], start, finish, mid, aliases)


def _exchange_carry(scattered, replicated=()):
    items = list(scattered) + [(a, (0, a.shape[0]), None) for a in replicated]
    na, ns = len(items), len(scattered)
    carry_ins, windows, aliases = _split_items(items)

    def plan(ins, outs, sems):
        send_sems, recv_sems, local_sems = sems
        me = _slot(*_position())

        def source(a, j):
            return _row_window(ins[a].at[j] if a < ns else ins[a], windows[a])

        def copy(a, j, i):
            return pltpu.make_async_remote_copy(src_ref=source(a, j), dst_ref=_row_window(outs[a].at[i], windows[a]),
                                                send_sem=send_sems.at[j, a], recv_sem=recv_sems.at[i, a],
                                                device_id=(j >> 2, (j >> 1) & 1, j & 1), device_id_type=MESH)

        def own(a, j):
            return pltpu.make_async_copy(source(a, j), _row_window(outs[a].at[j], windows[a]), local_sems.at[a])

        return me, copy, own

    def start(ins, outs, sems):
        me, copy, own = plan(ins, outs, sems)
        for a in range(na):
            for j in range(N_DEV):
                @pl.when(me == j)
                def _():
                    own(a, j).start()

                @pl.when(me != j)
                def _():
                    copy(a, j, me).start()

    def finish(ins, outs, sems):
        me, copy, own = plan(ins, outs, sems)
        for a in range(na):
            for j in range(N_DEV):
                @pl.when(me == j)
                def _():
                    for i in range(N_DEV):
                        if i != j:
                            copy(a, j, i).wait_recv()
                    own(a, j).wait()

                @pl.when(me != j)
                def _():
                    copy(a, j, me).wait_send()

    return _Carry(carry_ins, [_sds((N_DEV,) + src.shape[-2:], src.dtype) for src, _, _ in items],
                  [pltpu.SemaphoreType.DMA((N_DEV, na)), pltpu.SemaphoreType.DMA((N_DEV, na)),
                   pltpu.SemaphoreType.DMA((na,))], start, finish, None, aliases)


NQ, NKV = N_HEADS * HEAD, 2 * N_KV * HEAD


class _Mesh:
    def __init__(self, shards):
        self.shards, self.full, self.received, self.cache = shards, {}, {}, {}

    def fetch(self, wanted):
        items = []
        for want in wanted:
            name, r0, r1 = want if isinstance(want, tuple) else (want, 0, self.shards[want].shape[0])
            items.append((self.shards[name], (r0, r1), self.full.get(name)))
        return _gather_carry(items)

    def fetched(self, wanted, results):
        self.full.update(zip([want[0] if isinstance(want, tuple) else want for want in wanted], results))

    def send(self, *payloads):
        return _exchange_carry([(parts, rows or (0, parts.shape[1]), self.received.get(name))
                                for name, parts, rows in payloads])

    def sent(self, names, results):
        self.received.update(zip(names, results))

    def w(self, key):
        if key not in self.cache:
            self.cache[key] = self._layout(key)
        return self.cache[key]

    def _layout(self, key):
        if key in ("gu1", "gu2"):
            return self.full[key]
        if key in ("d1", "d2"):
            return self.full[key].reshape(4, FS, D)
        if key in ("out", "q", "o"):
            return self.full[key].reshape(D, D)
        if key == "kv":
            return self.full["kv"]
        if key == "convw":
            rows = self.full["conv"][:, :3, :].transpose(1, 0, 2).reshape(3, D)
            return jnp.concatenate([rows, jnp.zeros((5, D), F32)], axis=0)
        w_in_t = self.full["win"].reshape(-1, D)
        if key == "wa":
            return w_in_t[NQ + NKV:].reshape(5, D, D)
        assert key == "wb", key
        return jnp.stack([w_in_t[:NQ], jnp.pad(w_in_t[NQ:NQ + NKV], ((0, D - NKV), (0, 0)))])


def _w_in_parts(dw_a, dw_b):
    return jnp.concatenate([dw_b[0], dw_b[1][:NKV], dw_a.reshape(5 * D, D)], axis=0).reshape(N_DEV, -1, D)


def _forward_backward(x, mem, target, g, rel_bias, sinks, ex):
    s = x.shape[0]
    def fetching(wanted, call, *args, **kw):
        res, got = call(*args, carry=ex.fetch(wanted), **kw)
        ex.fetched(wanted, got)
        return res

    h1 = fetching(["gu1", "d1", "conv"], _rmsnorm, x, g["ffn1"], "norm_ffn1")
    gu1, a1 = fetching([("win", 0, 720)], _ffn_up, h1, ex.w("gu1").reshape(2, 4, FS, D), "ffn1_up")
    x1, h2 = fetching([("win", 720, 832), "out", "q"], _mm_res_norm, a1, ex.w("d1"), x, g["mix"], 0.5, "ffn1_down")
    pa = fetching(["gu2"], _mm_nn, h2, ex.w("wa"), "in_proj_a", bt=True)
    pb = fetching(["o"], _mm_nn, h2, ex.w("wb"), "in_proj_b", bt=True)
    biasm = _bias_build(rel_bias, "bias_build")
    attn, lse = fetching(["kv", "d2"], _swa_fwd, pb, biasm, sinks, "swa_fwd")
    merged = _conv_merge_fwd(pa, attn, ex.w("convw"), "conv_merge_fwd")
    (x2, h3), _ = _mm_res_norm(merged[None], ex.w("out")[None], x1, g["xattn"], 1.0, "out_proj")
    q2 = _mm_nn(h3, ex.w("q")[None], "xattn_q")[0][0]
    mh, _ = _rmsnorm(mem, g["mem"], "norm_mem")
    kv2 = _mm_nn(mh, ex.w("kv"), "xattn_kv")[0]
    o, lse2 = _xattn_fwd(q2, kv2, "xattn_fwd")
    (x3, h4), _ = _mm_res_norm(o[None], ex.w("o")[None], x2, g["ffn2"], 1.0, "xattn_o")
    (gu2, a2), _ = _ffn_up(h4, ex.w("gu2").reshape(2, 4, FS, D), "ffn2_up")
    dx4, dx4b, loss, d_final = _ffn_down_loss(a2, ex.w("d2"), x3, g["final"], target, "ffn2_down_loss")
    def sending(payloads, call, *args, **kw):
        res, got = call(*args, carry=ex.send(*payloads), **kw)
        ex.sent([name for name, _, _ in payloads], got)
        return res

    dw_d2 = _mm_tn(a2, dx4b[None], "dw_ffn2_down", scale=0.5)[0].reshape(N_DEV, -1, D)
    dgu2 = sending([("d2", dw_d2, None)], _ffn_down_bwd, dx4b, ex.w("d2"), gu2, "ffn2_down_bwd").reshape(8, s, FS)
    dw_gu2 = _mm_tn(dgu2, h4[None], "dw_ffn2_up")[0]
    dx3, dx3b, d_ffn2 = sending([("gu2", dw_gu2, (0, 400))], _mm_acc_rms_bwd, dgu2, ex.w("gu2"), "ffn2_up_bwd",
                                x=x3, gain=g["ffn2"], dres=dx4)
    do, _ = _mm_acc(dx3b[None], ex.w("o")[None], "xattn_o_bwd", BF16, bt=True)
    dw_o = _mm_tn(o[None], dx3b[None], "dw_xattn_o")[0].reshape(N_DEV, -1, D)
    dq2, dkv2 = sending([("o", dw_o, None)], _xattn_bwd, q2, kv2, o, do, lse2, "xattn_bwd")
    dkv2b = dkv2.astype(BF16)
    dw_q = _mm_tn(h3[None], dq2[None], "dw_xattn_q")[0].reshape(N_DEV, -1, D)
    dx2, dx2b, d_xattn = sending([("q", dw_q, None)], _mm_acc_rms_bwd, dq2[None], ex.w("q")[None], "xattn_q_bwd",
                                 x=x2, gain=g["xattn"], dres=dx3, bt=True)
    dw_kv = _mm_tn(mh[None], dkv2b, "dw_xattn_kv")[0]
    (_, _, d_mem), _ = _mm_acc_rms_bwd(dkv2b, ex.w("kv"), "xattn_kv_bwd", x=mem, gain=g["mem"],
                                       dres=jnp.zeros_like(mem), bt=True)
    dmerged, _ = _mm_acc(dx2b[None], ex.w("out")[None], "out_proj_bwd", BF16, bt=True)
    dw_out = _mm_tn(merged[None], dx2b[None], "dw_out_proj")[0].reshape(N_DEV, -1, D)
    dattn, dpa, d_convw = sending([("kv", dw_kv, None)], _conv_merge_bwd,
                                  dmerged, pa, attn, ex.w("convw"), "conv_merge_bwd")
    dpb, dbias, d_sinks = sending([("gu2", dw_gu2, (400, FS)), ("out", dw_out, None)], _swa_bwd,
                                  pb, attn, dattn, lse, biasm, sinks, "swa_bwd")
    d_relb = _bias_bwd(dbias, "bias_bwd")
    dw_in = _w_in_parts(_mm_tn(dpa, h2[None], "dw_in_proj_a")[0], _mm_tn(dpb, h2[None], "dw_in_proj_b")[0])
    dh2_b = sending([("win", dw_in, (0, 208))], _mm_acc, dpb, ex.w("wb"), "in_proj_b_bwd", F32)
    dx1, dx1b, d_mix = sending([("win", dw_in, (208, 672))], _mm_acc_rms_bwd, dpa, ex.w("wa"), "in_proj_a_bwd",
                               x=x1, gain=g["mix"], dres=dx2, addend=dh2_b)
    dw_d1 = sending([("win", dw_in, (672, 832))], _mm_tn, a1, dx1b[None], "dw_ffn1_down", scale=0.5)
    dw_d1 = dw_d1.reshape(N_DEV, -1, D)
    dgu1 = sending([("d1", dw_d1, None)], _ffn_down_bwd, dx1b, ex.w("d1"), gu1, "ffn1_down_bwd").reshape(8, s, FS)
    dw_gu1 = _mm_tn(dgu1, h1[None], "dw_ffn1_up")[0]
    dx0, _, d_ffn1 = sending([("gu1", dw_gu1, None)], _mm_acc_rms_bwd, dgu1, ex.w("gu1"), "ffn1_up_bwd",
                             x=x, gain=g["ffn1"], dres=dx1)

    relb_row = jnp.concatenate([d_relb[:, :REL_BUCKETS].T.reshape(1, REL_BUCKETS * N_HEADS), d_sinks[:, :N_HEADS],
                                jnp.zeros((1, D - REL_BUCKETS * N_HEADS - N_HEADS), F32)], axis=1)
    loss_row = jnp.concatenate([loss[0:1, 0:1], jnp.zeros((1, D - 1), F32)], axis=1)
    small = jnp.concatenate([d_ffn1, d_mix, d_xattn, d_mem, d_ffn2, d_final, relb_row, loss_row, d_convw[0:3],
                             jnp.zeros((SMALL_ROWS - ROW_CONV - 3, D), F32)], axis=0)
    return dx0, small


def _pack_small(norms, final, relb, sinks, conv_local, me):
    relb_row = jnp.concatenate([relb.reshape(1, -1), sinks.reshape(1, -1),
                                jnp.zeros((1, D - REL_BUCKETS * N_HEADS - N_HEADS), F32)], axis=1)
    conv_rows = lax.dynamic_update_slice(jnp.zeros((3, D), F32), conv_local.reshape(3, -1), (0, 128 * me))
    return jnp.concatenate(list(norms) + [final.reshape(1, D), relb_row, jnp.zeros((1, D), F32), conv_rows,
                                          jnp.zeros((SMALL_ROWS - ROW_CONV - 3, D), F32)], axis=0)


def kernel(x, mem, positions, rel_bias, ffn1_norm, ffn1_w_gu, ffn1_w_down, mix_norm, w_in, sinks, conv_w, w_out, xattn_norm, mem_norm, xattn_wq, xattn_wkv, xattn_wo, ffn2_norm, ffn2_w_gu, ffn2_w_down, final_norm, loss_target, m_rel_bias, m_ffn1_norm, m_ffn1_w_gu, m_ffn1_w_down, m_mix_norm, m_w_in, m_sinks, m_conv_w, m_w_out, m_xattn_norm, m_mem_norm, m_xattn_wq, m_xattn_wkv, m_xattn_wo, m_ffn2_norm, m_ffn2_w_gu, m_ffn2_w_down, m_final_norm, v_rel_bias, v_ffn1_norm, v_ffn1_w_gu, v_ffn1_w_down, v_mix_norm, v_w_in, v_sinks, v_conv_w, v_w_out, v_xattn_norm, v_mem_norm, v_xattn_wq, v_xattn_wkv, v_xattn_wo, v_ffn2_norm, v_ffn2_w_gu, v_ffn2_w_down, v_final_norm):
    del positions
    me = _slot(*_position())
    big = dict(gu1=(ffn1_w_gu, m_ffn1_w_gu, v_ffn1_w_gu), d1=(ffn1_w_down, m_ffn1_w_down, v_ffn1_w_down),
               win=(w_in, m_w_in, v_w_in), out=(w_out, m_w_out, v_w_out), q=(xattn_wq, m_xattn_wq, v_xattn_wq),
               kv=(xattn_wkv, m_xattn_wkv, v_xattn_wkv), o=(xattn_wo, m_xattn_wo, v_xattn_wo),
               gu2=(ffn2_w_gu, m_ffn2_w_gu, v_ffn2_w_gu), d2=(ffn2_w_down, m_ffn2_w_down, v_ffn2_w_down))
    order = list(big)
    transposed = ("gu1", "gu2", "win")
    local = {k: tuple(t[0].T if k in transposed else t[0] for t in big[k]) for k in order}
    shards = {k: local[k][0].astype(BF16) for k in order}
    shards["conv"] = jnp.concatenate([conv_w[0], jnp.zeros((5, 128), F32)], axis=0)
    ex = _Mesh(shards)
    gains = dict(ffn1=ffn1_norm, mix=mix_norm, xattn=xattn_norm, mem=mem_norm, ffn2=ffn2_norm,
                 final=final_norm.reshape(1, D))
    dx, small = _forward_backward(x[0], mem[0], loss_target[0], gains, rel_bias, sinks, ex)
    small_parts = _run_alone(_exchange_carry([], [small]), "exchange_small")[0]
    big_out = {k: _adamw(ex.received[k], *local[k], "adamw_" + k) for k in order}
    big_out = {k: [t.T if k in transposed else t for t in big_out[k]] for k in order}
    packed = [_pack_small(norms, final, relb, sk, conv, me) for norms, final, relb, sk, conv in (
        ((ffn1_norm, mix_norm, xattn_norm, mem_norm, ffn2_norm), final_norm, rel_bias, sinks, conv_w),
        ((m_ffn1_norm, m_mix_norm, m_xattn_norm, m_mem_norm, m_ffn2_norm), m_final_norm, m_rel_bias, m_sinks, m_conv_w),
        ((v_ffn1_norm, v_mix_norm, v_xattn_norm, v_mem_norm, v_ffn2_norm), v_final_norm, v_rel_bias, v_sinks, v_conv_w))]
    small_out = _adamw(small_parts, *packed, "adamw_small")

    def unpack(t):
        conv = lax.dynamic_slice(t[ROW_CONV:ROW_CONV + 3], (0, 128 * me), (3, 128))[None]
        nrel = REL_BUCKETS * N_HEADS
        return dict(ffn1_norm=t[0:1], mix_norm=t[1:2], xattn_norm=t[2:3], mem_norm=t[3:4], ffn2_norm=t[4:5],
                    final_norm=t[5], rel_bias=t[ROW_RELB, :nrel].reshape(REL_BUCKETS, N_HEADS),
                    sinks=t[ROW_RELB:ROW_RELB + 1, nrel:nrel + N_HEADS], conv_w=conv)

    names = dict(gu1="ffn1_w_gu", d1="ffn1_w_down", win="w_in", out="w_out", q="xattn_wq", kv="xattn_wkv",
                 o="xattn_wo", gu2="ffn2_w_gu", d2="ffn2_w_down")
    results = []
    for idx in range(4):
        leaves = unpack(small_out[idx])
        leaves.update({names[k]: big_out[k][idx][None] for k in order})
        results.append(leaves)
    weights = ("rel_bias", "ffn1_norm", "ffn1_w_gu", "ffn1_w_down", "mix_norm", "w_in", "sinks", "conv_w", "w_out",
               "xattn_norm", "mem_norm", "xattn_wq", "xattn_wkv", "xattn_wo", "ffn2_norm", "ffn2_w_gu", "ffn2_w_down",
               "final_norm")
    loss = small_out[0][ROW_LOSS, 0]
    return (loss, dx[None], *[leaves[n] for leaves in results for n in weights])
```

```python
import math

import numpy as np
import jax
import jax.numpy as jnp
from jax import lax
from jax.experimental import pallas as pl
from jax.experimental.pallas import tpu as pltpu

F32, BF16 = jnp.float32, jnp.bfloat16
MESH = pl.DeviceIdType.MESH

D = 1024
N_DEV = 8
D_FF = 2816
FS = D_FF // 4
HEAD = 64
N_HEADS, N_KV = 16, 4
BLK = 128
XH, XHD = 4, 256
REL_BUCKETS, REL_EXACT, REL_MAX_DIST = 32, 16, 128
EPS, NEG = 1e-6, -1e30
ADAM_LR, ADAM_B1, ADAM_B2, ADAM_EPS, ADAM_WD, ADAM_STEP = 0.001, 0.9, 0.999, 1e-08, 0.01, 10
VMEM_LIMIT_V7X = 56 * 2**20
SMALL_ROWS = 16
ROW_RELB, ROW_LOSS, ROW_CONV = 6, 7, 8


def _bucket_thresholds():
    n = np.arange(REL_MAX_DIST)
    nf = np.maximum(n, 1).astype(np.float32)
    large = REL_EXACT + (np.log(nf / np.float32(REL_EXACT)) / np.float32(math.log(REL_MAX_DIST / REL_EXACT))
                         * np.float32(REL_BUCKETS - REL_EXACT)).astype(np.int32)
    b = np.where(n < REL_EXACT, n, np.minimum(large, REL_BUCKETS - 1))
    return [int(np.argmax(b >= REL_EXACT + k)) for k in range(1, REL_BUCKETS - REL_EXACT)]


BUCKET_THRESHOLDS = _bucket_thresholds()


HBM_SPEC = pl.BlockSpec(memory_space=pl.ANY)


class _Carry:
    def __init__(self, ins, outs, sems, start, finish, mid=None, aliases=None):
        self.ins, self.outs, self.sems = list(ins), list(outs), list(sems)
        self.start, self.finish, self.mid, self.aliases = start, finish, mid, dict(aliases or {})


def _pcall(body, *, name, grid, in_specs, out_specs, out_shape, scratch=(), carry=None):
    params = pltpu.CompilerParams(dimension_semantics=("arbitrary",) * len(grid), vmem_limit_bytes=VMEM_LIMIT_V7X)
    if carry is None:
        return pl.pallas_call(body, name=name, grid=grid, in_specs=in_specs, out_specs=out_specs,
                              out_shape=out_shape, scratch_shapes=list(scratch), compiler_params=params)
    single = not isinstance(out_shape, (list, tuple))
    own_specs, own_shapes = ([out_specs], [out_shape]) if single else (list(out_specs), list(out_shape))
    n_in, n_out, n_scr = len(in_specs), len(own_shapes), len(scratch)
    n_cin, n_cout = len(carry.ins), len(carry.outs)
    steps = math.prod(grid)
    mid_step = max(steps - 1 - max(steps // 8, 1), 0)

    def carrying(*refs):
        ins, refs = refs[:n_in], refs[n_in:]
        cins, refs = refs[:n_cin], refs[n_cin:]
        outs, refs = refs[:n_out], refs[n_out:]
        couts, refs = refs[:n_cout], refs[n_cout:]
        scr, csems = refs[:n_scr], refs[n_scr:]
        step = 0
        for axis, size in enumerate(grid):
            step = step * size + pl.program_id(axis)

        @pl.when(step == 0)
        def _():
            carry.start(cins, couts, csems)

        body(*ins, *outs, *scr)
        if carry.mid is not None:
            @pl.when(step == mid_step)
            def _():
                carry.mid(cins, couts, csems)

        @pl.when(step == steps - 1)
        def _():
            carry.finish(cins, couts, csems)

    call = pl.pallas_call(carrying, name=name, grid=grid, in_specs=list(in_specs) + [HBM_SPEC] * n_cin,
                          out_specs=own_specs + [HBM_SPEC] * n_cout, out_shape=own_shapes + carry.outs,
                          scratch_shapes=list(scratch) + carry.sems, compiler_params=params,
                          input_output_aliases={n_in + i: n_out + o for i, o in carry.aliases.items()})

    def run(*args):
        res = call(*args, *carry.ins)
        return (res[0] if single else res[:n_out]), res[n_out:]

    return run


def _run_alone(carry, name):
    n_cin, n_cout = len(carry.ins), len(carry.outs)

    def body(*refs):
        cins, couts, csems = refs[:n_cin], refs[n_cin:n_cin + n_cout], refs[n_cin + n_cout:]
        carry.start(cins, couts, csems)
        if carry.mid is not None:
            carry.mid(cins, couts, csems)
        carry.finish(cins, couts, csems)

    return pl.pallas_call(body, name=name, in_specs=[HBM_SPEC] * n_cin, out_specs=[HBM_SPEC] * n_cout,
                          out_shape=carry.outs, scratch_shapes=carry.sems,
                          input_output_aliases=carry.aliases)(*carry.ins)


def _dot(a, b):
    return jnp.dot(a, b, preferred_element_type=F32)


def _dot_nt(a, b):
    return lax.dot_general(a, b, (((1,), (1,)), ((), ())), preferred_element_type=F32)


def _dot_tn(a, b):
    return lax.dot_general(a, b, (((0,), (0,)), ((), ())), preferred_element_type=F32)


def _sds(shape, dtype):
    return jax.ShapeDtypeStruct(tuple(shape), dtype)


def _carried(call, args, carry):
    return call(*args) if carry is not None else (call(*args), ())


def _rmsnorm(x, g, name, carry=None):
    m, d = x.shape
    tm = min(512, m)

    def body(x_ref, g_ref, h_ref):
        xv = x_ref[...]
        r = lax.rsqrt(jnp.mean(xv * xv, axis=-1, keepdims=True) + EPS)
        h_ref[...] = (xv * r * g_ref[...]).astype(BF16)

    call = _pcall(body, name=name, grid=(m // tm,), carry=carry,
                  in_specs=[pl.BlockSpec((tm, d), lambda i: (i, 0)), pl.BlockSpec((1, d), lambda i: (0, 0))],
                  out_specs=pl.BlockSpec((tm, d), lambda i: (i, 0)), out_shape=_sds((m, d), BF16))
    return _carried(call, (x, g), carry)


def _mm_nn(a, b, name, tm=512, bt=False, carry=None):
    m, k = a.shape
    nj = b.shape[0]
    n = b.shape[1] if bt else b.shape[2]
    tm = min(tm, m)
    dot = _dot_nt if bt else _dot

    def body(a_ref, b_ref, o_ref):
        o_ref[...] = dot(a_ref[...], b_ref[...]).astype(BF16)

    call = _pcall(body, name=name, grid=(nj, m // tm),
                  in_specs=[pl.BlockSpec((tm, k), lambda j, i: (i, 0)),
                            pl.BlockSpec((None,) + b.shape[1:], lambda j, i: (j, 0, 0))],
                  out_specs=pl.BlockSpec((None, tm, n), lambda j, i: (j, i, 0)),
                  out_shape=_sds((nj, m, n), BF16), carry=carry)
    return _carried(call, (a, b), carry)


def _ffn_up(h, w4, name, tm=512, carry=None):
    s, d = h.shape
    tm = min(tm, s)

    def body(h_ref, w_ref, gu_ref, a_ref):
        hv = h_ref[...]
        g = _dot_nt(hv, w_ref[0])
        u = _dot_nt(hv, w_ref[1])
        gu_ref[0] = g.astype(BF16)
        gu_ref[1] = u.astype(BF16)
        a_ref[...] = (g * jax.nn.sigmoid(g) * u).astype(BF16)

    call = _pcall(body, name=name, grid=(4, s // tm),
                  in_specs=[pl.BlockSpec((tm, d), lambda p, i: (i, 0)),
                            pl.BlockSpec((2, None, FS, d), lambda p, i: (0, p, 0, 0))],
                  out_specs=[pl.BlockSpec((2, None, tm, FS), lambda p, i: (0, p, i, 0)),
                             pl.BlockSpec((None, tm, FS), lambda p, i: (p, i, 0))],
                  out_shape=[_sds((2, 4, s, FS), BF16), _sds((4, s, FS), BF16)], carry=carry)
    return _carried(call, (h, w4), carry)


def _mm_res_norm(a, w, xres, gain, scale, name, tm=512, carry=None):
    npart, s, kp = a.shape
    tm = min(tm, s)

    def body(a_ref, w_ref, x_ref, g_ref, xo_ref, h_ref):
        acc = _dot(a_ref[0], w_ref[0])
        for p in range(1, npart):
            acc = acc + _dot(a_ref[p], w_ref[p])
        xn = x_ref[...] + scale * acc
        xo_ref[...] = xn
        r = lax.rsqrt(jnp.mean(xn * xn, axis=-1, keepdims=True) + EPS)
        h_ref[...] = (xn * r * g_ref[...]).astype(BF16)

    call = _pcall(body, name=name, grid=(s // tm,),
                  in_specs=[pl.BlockSpec((npart, tm, kp), lambda i: (0, i, 0)),
                            pl.BlockSpec((npart, kp, D), lambda i: (0, 0, 0)),
                            pl.BlockSpec((tm, D), lambda i: (i, 0)),
                            pl.BlockSpec((1, D), lambda i: (0, 0))],
                  out_specs=[pl.BlockSpec((tm, D), lambda i: (i, 0)), pl.BlockSpec((tm, D), lambda i: (i, 0))],
                  out_shape=[_sds((s, D), F32), _sds((s, D), BF16)], carry=carry)
    return _carried(call, (a, w, xres, gain), carry)


def _ffn_down_loss(a, w, xres, gain, target, name, tm=512):
    npart, s, kp = a.shape
    tm = min(tm, s)

    def body(a_ref, w_ref, x_ref, g_ref, t_ref, dx_ref, dxb_ref, loss_ref, dg_ref):
        i = pl.program_id(0)
        acc = _dot(a_ref[0], w_ref[0])
        for p in range(1, npart):
            acc = acc + _dot(a_ref[p], w_ref[p])
        xn = x_ref[...] + 0.5 * acc
        r = lax.rsqrt(jnp.mean(xn * xn, axis=-1, keepdims=True) + EPS)
        xh = xn * r
        gv = g_ref[...]
        err = xh * gv - t_ref[...]
        part = 0.5 * jnp.sum(jnp.mean(err * err, axis=-1, keepdims=True), axis=0, keepdims=True)
        dy = err * (1.0 / D)
        dyg = dy * gv
        dxn = r * (dyg - xh * jnp.mean(dyg * xh, axis=-1, keepdims=True))
        dx_ref[...] = dxn
        dxb_ref[...] = dxn.astype(BF16)

        @pl.when(i == 0)
        def _():
            loss_ref[...] = jnp.zeros_like(loss_ref)
            dg_ref[...] = jnp.zeros_like(dg_ref)

        loss_ref[...] += jnp.broadcast_to(part, loss_ref.shape)
        dg_ref[...] += jnp.sum(dy * xh, axis=0, keepdims=True)

    return _pcall(body, name=name, grid=(s // tm,),
                  in_specs=[pl.BlockSpec((npart, tm, kp), lambda i: (0, i, 0)),
                            pl.BlockSpec((npart, kp, D), lambda i: (0, 0, 0)),
                            pl.BlockSpec((tm, D), lambda i: (i, 0)),
                            pl.BlockSpec((1, D), lambda i: (0, 0)),
                            pl.BlockSpec((tm, D), lambda i: (i, 0))],
                  out_specs=[pl.BlockSpec((tm, D), lambda i: (i, 0)), pl.BlockSpec((tm, D), lambda i: (i, 0)),
                             pl.BlockSpec((8, 128), lambda i: (0, 0)), pl.BlockSpec((1, D), lambda i: (0, 0))],
                  out_shape=[_sds((s, D), F32), _sds((s, D), BF16), _sds((8, 128), F32), _sds((1, D), F32)],
                  )(a, w, xres, gain, target)


def _window_tiles():
    i = lax.broadcasted_iota(jnp.int32, (BLK, BLK), 0)
    j = lax.broadcasted_iota(jnp.int32, (BLK, BLK), 1)
    rel = (i - j) & (BLK - 1)
    large = jnp.full_like(rel, REL_EXACT)
    for t in BUCKET_THRESHOLDS:
        large = large + (rel >= t).astype(jnp.int32)
    return j <= i, jnp.where(rel < REL_EXACT, rel, large)


def _bias_build(rel_bias, name):
    def body(rb_ref, o_ref):
        _, bucket = _window_tiles()

        def per_head(h, carry):
            acc = jnp.zeros((BLK, BLK), F32)
            for b in range(REL_BUCKETS):
                acc = jnp.where(bucket == b, rb_ref[b, h], acc)
            o_ref[h] = acc
            return carry

        lax.fori_loop(0, N_HEADS, per_head, 0)

    return _pcall(body, name=name, grid=(1,),
                  in_specs=[pl.BlockSpec(memory_space=pltpu.SMEM)],
                  out_specs=pl.BlockSpec((N_HEADS, BLK, BLK), lambda i: (0, 0, 0)),
                  out_shape=_sds((N_HEADS, BLK, BLK), F32))(rel_bias)


def _bias_bwd(dbias, name):
    def body(db_ref, o_ref):
        _, bucket = _window_tiles()
        lane = lax.broadcasted_iota(jnp.int32, (N_HEADS, 128), 1)

        def per_bucket(b, out):
            mb = (bucket == b).astype(F32)
            per_col = jnp.sum(db_ref[...] * mb[None, :, :], axis=1)
            return jnp.where(lane == b, jnp.sum(per_col, axis=1, keepdims=True), out)

        o_ref[...] = lax.fori_loop(0, REL_BUCKETS, per_bucket, jnp.zeros((N_HEADS, 128), F32))

    return _pcall(body, name=name, grid=(1,),
                  in_specs=[pl.BlockSpec((N_HEADS, BLK, BLK), lambda i: (0, 0, 0))],
                  out_specs=pl.BlockSpec((N_HEADS, 128), lambda i: (0, 0)),
                  out_shape=_sds((N_HEADS, 128), F32))(dbias)


PAIR = 2 * HEAD
GROUP = N_HEADS // N_KV
SWA_SCALE = HEAD ** -0.5


def _window_masks(n):
    i = lax.broadcasted_iota(jnp.int32, (GROUP * BLK, BLK), 0) & (BLK - 1)
    j = lax.broadcasted_iota(jnp.int32, (GROUP * BLK, BLK), 1)
    return j <= i, jnp.logical_and(n == 0, j > i), j < HEAD


def _kv_twice(ref, base, g, low):
    slab = ref[:, base + PAIR * (g // 2): base + PAIR * (g // 2 + 1)]
    swapped = pltpu.roll(slab, HEAD, 1)
    return jnp.where(low, slab, swapped) if g % 2 == 0 else jnp.where(low, swapped, slab)


def _stack_heads(ref, g, low):
    parts = []
    for r in range(2):
        slab = ref[:, PAIR * (2 * g + r): PAIR * (2 * g + r + 1)]
        zero = jnp.zeros_like(slab)
        parts += [jnp.where(low, slab, zero), jnp.where(low, zero, slab)]
    return jnp.concatenate(parts, axis=0)


def _unstack_heads(t, low):
    return [jnp.where(low, t[2 * r * BLK:(2 * r + 1) * BLK], t[(2 * r + 1) * BLK:(2 * r + 2) * BLK])
            for r in range(2)]


def _head_rows(t, k):
    return t[k * BLK:(k + 1) * BLK]


def _per_head_column(values):
    head = lax.broadcasted_iota(jnp.int32, (GROUP * BLK, 1), 0) // BLK
    col = jnp.full((GROUP * BLK, 1), values[0], F32)
    for k in range(1, GROUP):
        col = jnp.where(head == k, values[k], col)
    return col


def _window_logits(q4, kc, kp, bias4, own, absent):
    sc = jnp.where(own, _dot_nt(q4, kc), _dot_nt(q4, kp)) * SWA_SCALE + bias4
    return jnp.where(absent, NEG, sc)


def _split_window(t, own):
    zero = jnp.zeros_like(t)
    return jnp.where(own, t, zero), jnp.where(own, zero, t)


def _swa_fwd(pb, bias, sinks, name, carry=None):
    _, s, _ = pb.shape
    nb = s // BLK
    kvw = 2 * N_KV * HEAD

    def body(q_ref, kc_ref, kp_ref, b_ref, sk_ref, o_ref, lse_ref):
        own, absent, low4 = _window_masks(pl.program_id(0))
        low = low4[:BLK]
        lane = lax.broadcasted_iota(jnp.int32, (BLK, 128), 1)
        lse_t = jnp.zeros((BLK, 128), F32)
        for g in range(N_KV):
            q4 = _stack_heads(q_ref, g, low)
            kc, kp = _kv_twice(kc_ref, 0, g, low), _kv_twice(kp_ref, 0, g, low)
            vc, vp = _kv_twice(kc_ref, N_KV * HEAD, g, low), _kv_twice(kp_ref, N_KV * HEAD, g, low)
            bias4 = b_ref[GROUP * g:GROUP * (g + 1)].reshape(GROUP * BLK, BLK)
            sc = _window_logits(q4, kc, kp, bias4, own, absent)
            sk = _per_head_column([sk_ref[0, GROUP * g + k] for k in range(GROUP)])
            m = jnp.maximum(jnp.max(sc, axis=1, keepdims=True), sk)
            p = jnp.exp(sc - m)
            l = jnp.sum(p, axis=1, keepdims=True) + jnp.exp(sk - m)
            p_own, p_prev = _split_window(p.astype(BF16), own)
            out = (_dot(p_own, vc) + _dot(p_prev, vp)) * (1.0 / l)
            for r, slab in enumerate(_unstack_heads(out, low)):
                o_ref[:, PAIR * (2 * g + r): PAIR * (2 * g + r + 1)] = slab.astype(BF16)
            lse4 = m + jnp.log(l)
            for k in range(GROUP):
                lse_t = jnp.where(lane == GROUP * g + k, _head_rows(lse4, k), lse_t)
        lse_ref[...] = lse_t

    call = _pcall(body, name=name, grid=(nb,),
                  in_specs=[pl.BlockSpec((None, BLK, D), lambda n: (0, n, 0)),
                            pl.BlockSpec((None, BLK, kvw), lambda n: (1, n, 0)),
                            pl.BlockSpec((None, BLK, kvw), lambda n: (1, jnp.maximum(n - 1, 0), 0)),
                            pl.BlockSpec((N_HEADS, BLK, BLK), lambda n: (0, 0, 0)),
                            pl.BlockSpec(memory_space=pltpu.SMEM)],
                  out_specs=[pl.BlockSpec((BLK, D), lambda n: (n, 0)), pl.BlockSpec((BLK, 128), lambda n: (n, 0))],
                  out_shape=[_sds((s, D), BF16), _sds((s, 128), F32)], carry=carry)
    return _carried(call, (pb, pb, pb, bias, sinks), carry)


def _fold_halves(t, g, low):
    folded = jnp.where(low, t, 0.0) + pltpu.roll(jnp.where(low, 0.0, t), HEAD, 1)
    return folded if g % 2 == 0 else pltpu.roll(folded, HEAD, 1)


def _swa_bwd(pb, attn, dattn, lse, bias, sinks, name, carry=None):
    _, s, _ = pb.shape
    nb = s // BLK
    kvw = 2 * N_KV * HEAD
    voff = N_KV * HEAD

    def body(q_ref, kc_ref, kp_ref, o_ref, do_ref, lse_ref, b_ref, skrow_ref, dpb_ref, dbias_ref, dsk_ref,
             dq_hold, kv_hold, dq_new, kv_prev, kv_cur):
        n = pl.program_id(0)

        @pl.when(n == 0)
        def _():
            dbias_ref[...] = jnp.zeros_like(dbias_ref)
            dsk_ref[...] = jnp.zeros_like(dsk_ref)
            dq_hold[...] = jnp.zeros_like(dq_hold)
            kv_hold[...] = jnp.zeros_like(kv_hold)

        @pl.when(n < nb)
        def _():
            own, absent, low4 = _window_masks(n)
            low = low4[:BLK]
            lane = lax.broadcasted_iota(jnp.int32, (BLK, 128), 1)
            delta_t = jnp.zeros((BLK, 128), F32)
            ones = jnp.ones((PAIR, 128), BF16)
            for pair_of_kv in range(N_KV // 2):
                slab_grads = [jnp.zeros((BLK, PAIR), F32) for _ in range(4)]
                for g in (2 * pair_of_kv, 2 * pair_of_kv + 1):
                    q4, do4 = _stack_heads(q_ref, g, low), _stack_heads(do_ref, g, low)
                    kc, kp = _kv_twice(kc_ref, 0, g, low), _kv_twice(kp_ref, 0, g, low)
                    vc, vp = _kv_twice(kc_ref, voff, g, low), _kv_twice(kp_ref, voff, g, low)
                    o_slabs = [o_ref[:, PAIR * (2 * g + r): PAIR * (2 * g + r + 1)] for r in range(2)]
                    o4 = jnp.concatenate([o_slabs[0], o_slabs[0], o_slabs[1], o_slabs[1]], axis=0)
                    delta = _dot(do4 * o4, ones)
                    heads = range(GROUP * g, GROUP * (g + 1))
                    lse4 = jnp.concatenate([lse_ref[:, h:h + 1] for h in heads], axis=0)
                    bias4 = b_ref[GROUP * g:GROUP * (g + 1)].reshape(GROUP * BLK, BLK)
                    p = jnp.exp(_window_logits(q4, kc, kp, bias4, own, absent) - lse4)
                    dp = jnp.where(own, _dot_nt(do4, vc), _dot_nt(do4, vp))
                    ds = p * (dp - delta)
                    dbias_ref[GROUP * g:GROUP * (g + 1)] += ds.reshape(GROUP, BLK, BLK)
                    for k, h in enumerate(heads):
                        delta_t = jnp.where(lane == h, _head_rows(delta, k), delta_t)
                    ds_own, ds_prev = _split_window((ds * SWA_SCALE).astype(BF16), own)
                    p_own, p_prev = _split_window(p.astype(BF16), own)
                    dq4 = _dot(ds_own, kc) + _dot(ds_prev, kp)
                    for r, slab in enumerate(_unstack_heads(dq4, low)):
                        dq_new[:, PAIR * (2 * g + r): PAIR * (2 * g + r + 1)] = slab
                    grads = [_dot_tn(ds_own, q4), _dot_tn(ds_prev, q4), _dot_tn(p_own, do4), _dot_tn(p_prev, do4)]
                    slab_grads = [t + _fold_halves(dk, g, low) for t, dk in zip(slab_grads, grads)]
                ks = slice(PAIR * pair_of_kv, PAIR * (pair_of_kv + 1))
                vs = slice(voff + PAIR * pair_of_kv, voff + PAIR * (pair_of_kv + 1))
                kv_cur[:, ks], kv_prev[:, ks], kv_cur[:, vs], kv_prev[:, vs] = slab_grads
            dsk_ref[...] -= jnp.sum(jnp.exp(skrow_ref[...] - lse_ref[...]) * delta_t, axis=0, keepdims=True)

        @pl.when(n == nb)
        def _():
            kv_prev[...] = jnp.zeros_like(kv_prev)

        dpb_ref[0] = dq_hold[...].astype(BF16)
        dpb_ref[1, :, 0:kvw] = (kv_hold[...] + kv_prev[...]).astype(BF16)
        dpb_ref[1, :, kvw:D] = jnp.zeros((BLK, D - kvw), BF16)

        @pl.when(n < nb)
        def _():
            dq_hold[...] = dq_new[...]
            kv_hold[...] = kv_cur[...]

    def cur(n):
        return jnp.minimum(n, nb - 1)

    call = _pcall(body, name=name, grid=(nb + 1,), carry=carry,
                  in_specs=[pl.BlockSpec((None, BLK, D), lambda n: (0, cur(n), 0)),
                            pl.BlockSpec((None, BLK, kvw), lambda n: (1, cur(n), 0)),
                            pl.BlockSpec((None, BLK, kvw), lambda n: (1, jnp.maximum(cur(n) - 1, 0), 0)),
                            pl.BlockSpec((BLK, D), lambda n: (cur(n), 0)),
                            pl.BlockSpec((BLK, D), lambda n: (cur(n), 0)),
                            pl.BlockSpec((BLK, 128), lambda n: (cur(n), 0)),
                            pl.BlockSpec((N_HEADS, BLK, BLK), lambda n: (0, 0, 0)),
                            pl.BlockSpec((1, 128), lambda n: (0, 0))],
                  out_specs=[pl.BlockSpec((2, BLK, D), lambda n: (0, jnp.maximum(n - 1, 0), 0)),
                             pl.BlockSpec((N_HEADS, BLK, BLK), lambda n: (0, 0, 0)),
                             pl.BlockSpec((1, 128), lambda n: (0, 0))],
                  out_shape=[_sds((2, s, D), BF16), _sds((N_HEADS, BLK, BLK), F32), _sds((1, 128), F32)],
                  scratch=[pltpu.VMEM((BLK, D), F32), pltpu.VMEM((BLK, kvw), F32), pltpu.VMEM((BLK, D), F32),
                           pltpu.VMEM((BLK, kvw), F32), pltpu.VMEM((BLK, kvw), F32)])
    sink_row = jnp.pad(sinks, ((0, 0), (0, 128 - N_HEADS)))
    return _carried(call, (pb, pb, pb, attn, dattn, lse, bias, sink_row), carry)


HALO = 16
CW = 512


def _conv_taps(cu, halo_cu, first_tile):
    row = lax.broadcasted_iota(jnp.int32, cu.shape, 0)
    halo_cu = jnp.where(first_tile, 0.0, halo_cu)
    c1 = jnp.where(row == 0, halo_cu[HALO - 1:HALO], pltpu.roll(cu, 1, 0))
    c2 = jnp.where(row == 0, halo_cu[HALO - 2:HALO - 1],
                   jnp.where(row == 1, halo_cu[HALO - 1:HALO], pltpu.roll(cu, 2, 0)))
    return c1, c2


def _conv_merge_fwd(pa, attn, convw, name, ts=512):
    _, s, _ = pa.shape
    ts = min(ts, s)
    hb = ts // HALO

    def body(pa_ref, hp_ref, at_ref, w_ref, o_ref):
        i = pl.program_id(1)
        cu = pa_ref[0].astype(F32) * pa_ref[2].astype(F32)
        c1, c2 = _conv_taps(cu, hp_ref[0].astype(F32) * hp_ref[2].astype(F32), i == 0)
        w = w_ref[...]
        c3 = w[0:1] * c2 + w[1:2] * c1 + w[2:3] * cu
        conv = pa_ref[1].astype(F32) * c3
        o_ref[...] = (jax.nn.sigmoid(pa_ref[3].astype(F32)) * at_ref[...].astype(F32)
                      + jax.nn.sigmoid(pa_ref[4].astype(F32)) * conv).astype(BF16)

    return _pcall(body, name=name, grid=(D // CW, s // ts),
                  in_specs=[pl.BlockSpec((5, ts, CW), lambda c, i: (0, i, c)),
                            pl.BlockSpec((5, HALO, CW), lambda c, i: (0, jnp.maximum(i * hb - 1, 0), c)),
                            pl.BlockSpec((ts, CW), lambda c, i: (i, c)),
                            pl.BlockSpec((8, CW), lambda c, i: (0, c))],
                  out_specs=pl.BlockSpec((ts, CW), lambda c, i: (i, c)),
                  out_shape=_sds((s, D), BF16))(pa, pa, attn, convw)


def _conv_merge_bwd(dmerged, pa, attn, convw, name, ts=512, carry=None):
    _, s, _ = pa.shape
    ts = min(ts, s)
    hb = ts // HALO
    last_hb = s // HALO - 1

    def body(dm_ref, pa_ref, at_ref, w_ref, hp_ref, hn_ref, dmn_ref, dat_ref, dpa_ref, dw_ref):
        i = pl.program_id(1)
        last = i == pl.num_programs(1) - 1
        dm = dm_ref[...].astype(F32)
        cp, bp, u = pa_ref[0].astype(F32), pa_ref[1].astype(F32), pa_ref[2].astype(F32)
        sa = jax.nn.sigmoid(pa_ref[3].astype(F32))
        sc = jax.nn.sigmoid(pa_ref[4].astype(F32))
        at = at_ref[...].astype(F32)
        cu = cp * u
        c1, c2 = _conv_taps(cu, hp_ref[0].astype(F32) * hp_ref[2].astype(F32), i == 0)
        w = w_ref[...]
        c3 = w[0:1] * c2 + w[1:2] * c1 + w[2:3] * cu
        dconv = dm * sc
        dc3 = dconv * bp
        nxt = dmn_ref[...].astype(F32) * jax.nn.sigmoid(hn_ref[4].astype(F32)) * hn_ref[1].astype(F32)
        nxt = jnp.where(last, 0.0, nxt)
        row = lax.broadcasted_iota(jnp.int32, dc3.shape, 0)
        d1 = jnp.where(row == ts - 1, nxt[0:1], pltpu.roll(dc3, ts - 1, 0))
        d2 = jnp.where(row == ts - 2, nxt[0:1], jnp.where(row == ts - 1, nxt[1:2], pltpu.roll(dc3, ts - 2, 0)))
        dcu = w[2:3] * dc3 + w[1:2] * d1 + w[0:1] * d2
        dat_ref[...] = (dm * sa).astype(BF16)
        dpa_ref[0] = (dcu * u).astype(BF16)
        dpa_ref[1] = (dconv * c3).astype(BF16)
        dpa_ref[2] = (dcu * cp).astype(BF16)
        dpa_ref[3] = (dm * at * sa * (1.0 - sa)).astype(BF16)
        dpa_ref[4] = (dm * bp * c3 * sc * (1.0 - sc)).astype(BF16)

        @pl.when(i == 0)
        def _():
            dw_ref[...] = jnp.zeros_like(dw_ref)

        dw_ref[0:1, :] += jnp.sum(dc3 * c2, axis=0, keepdims=True)
        dw_ref[1:2, :] += jnp.sum(dc3 * c1, axis=0, keepdims=True)
        dw_ref[2:3, :] += jnp.sum(dc3 * cu, axis=0, keepdims=True)

    call = _pcall(body, name=name, grid=(D // CW, s // ts), carry=carry,
                  in_specs=[pl.BlockSpec((ts, CW), lambda c, i: (i, c)),
                            pl.BlockSpec((5, ts, CW), lambda c, i: (0, i, c)),
                            pl.BlockSpec((ts, CW), lambda c, i: (i, c)),
                            pl.BlockSpec((8, CW), lambda c, i: (0, c)),
                            pl.BlockSpec((5, HALO, CW), lambda c, i: (0, jnp.maximum(i * hb - 1, 0), c)),
                            pl.BlockSpec((5, HALO, CW), lambda c, i: (0, jnp.minimum((i + 1) * hb, last_hb), c)),
                            pl.BlockSpec((HALO, CW), lambda c, i: (jnp.minimum((i + 1) * hb, last_hb), c))],
                  out_specs=[pl.BlockSpec((ts, CW), lambda c, i: (i, c)),
                             pl.BlockSpec((5, ts, CW), lambda c, i: (0, i, c)),
                             pl.BlockSpec((8, CW), lambda c, i: (0, c))],
                  out_shape=[_sds((s, D), BF16), _sds((5, s, D), BF16), _sds((8, D), F32)])
    return _carried(call, (dmerged, pa, attn, convw, pa, pa, dmerged), carry)


def _xattn_fwd(q, kv, name, tq=512):
    s, _ = q.shape
    nm = kv.shape[1]
    tq = min(tq, s)

    def body(q_ref, kv_ref, o_ref, lse_ref):
        lane = lax.broadcasted_iota(jnp.int32, (tq, 128), 1)
        lse_t = jnp.zeros((tq, 128), F32)
        for h in range(XH):
            hs = slice(XHD * h, XHD * (h + 1))
            sc = _dot_nt(q_ref[:, hs], kv_ref[h]) * (XHD ** -0.5)
            m = jnp.max(sc, axis=1, keepdims=True)
            p = jnp.exp(sc - m)
            l = jnp.sum(p, axis=1, keepdims=True)
            o_ref[:, hs] = (_dot(p.astype(BF16), kv_ref[XH + h]) * (1.0 / l)).astype(BF16)
            lse_t = jnp.where(lane == h, m + jnp.log(l), lse_t)
        lse_ref[...] = lse_t

    return _pcall(body, name=name, grid=(s // tq,),
                  in_specs=[pl.BlockSpec((tq, D), lambda i: (i, 0)), pl.BlockSpec((2 * XH, nm, XHD), lambda i: (0, 0, 0))],
                  out_specs=[pl.BlockSpec((tq, D), lambda i: (i, 0)), pl.BlockSpec((tq, 128), lambda i: (i, 0))],
                  out_shape=[_sds((s, D), BF16), _sds((s, 128), F32)])(q, kv)


def _xattn_bwd(q, kv, o, do, lse, name, tq=512, carry=None):
    s, _ = q.shape
    nm = kv.shape[1]
    tq = min(tq, s)

    def body(q_ref, kv_ref, o_ref, do_ref, lse_ref, dq_ref, dkv_ref):
        @pl.when(pl.program_id(0) == 0)
        def _():
            dkv_ref[...] = jnp.zeros_like(dkv_ref)

        for h in range(XH):
            hs = slice(XHD * h, XHD * (h + 1))
            qh, kh, vh, dob = q_ref[:, hs], kv_ref[h], kv_ref[XH + h], do_ref[:, hs]
            p = jnp.exp(_dot_nt(qh, kh) * (XHD ** -0.5) - lse_ref[:, h:h + 1])
            dp = _dot_nt(dob, vh)
            delta = jnp.sum(dob.astype(F32) * o_ref[:, hs].astype(F32), axis=1, keepdims=True)
            dsb = (p * (dp - delta) * (XHD ** -0.5)).astype(BF16)
            dq_ref[:, hs] = _dot(dsb, kh).astype(BF16)
            dkv_ref[h] += _dot_tn(dsb, qh)
            dkv_ref[XH + h] += _dot_tn(p.astype(BF16), dob)

    call = _pcall(body, name=name, grid=(s // tq,), carry=carry,
                  in_specs=[pl.BlockSpec((tq, D), lambda i: (i, 0)), pl.BlockSpec((2 * XH, nm, XHD), lambda i: (0, 0, 0)),
                            pl.BlockSpec((tq, D), lambda i: (i, 0)), pl.BlockSpec((tq, D), lambda i: (i, 0)),
                            pl.BlockSpec((tq, 128), lambda i: (i, 0))],
                  out_specs=[pl.BlockSpec((tq, D), lambda i: (i, 0)), pl.BlockSpec((2 * XH, nm, XHD), lambda i: (0, 0, 0))],
                  out_shape=[_sds((s, D), BF16), _sds((2 * XH, nm, XHD), F32)])
    return _carried(call, (q, kv, o, do, lse), carry)


def _ffn_down_bwd(dxb, wd4, gu4, name, tm=512, carry=None):
    s, _ = dxb.shape
    tm = min(tm, s)

    def body(dx_ref, w_ref, gu_ref, o_ref):
        da = 0.5 * _dot_nt(dx_ref[...], w_ref[...])
        g = gu_ref[0].astype(F32)
        u = gu_ref[1].astype(F32)
        sg = jax.nn.sigmoid(g)
        o_ref[0] = (da * u * sg * (1.0 + g * (1.0 - sg))).astype(BF16)
        o_ref[1] = (da * g * sg).astype(BF16)

    call = _pcall(body, name=name, grid=(4, s // tm), carry=carry,
                  in_specs=[pl.BlockSpec((tm, D), lambda p, i: (i, 0)),
                            pl.BlockSpec((None, FS, D), lambda p, i: (p, 0, 0)),
                            pl.BlockSpec((2, None, tm, FS), lambda p, i: (0, p, i, 0))],
                  out_specs=pl.BlockSpec((2, None, tm, FS), lambda p, i: (0, p, i, 0)),
                  out_shape=_sds((2, 4, s, FS), BF16))
    return _carried(call, (dxb, wd4, gu4), carry)


def _mm_tn(a, b, name, scale=1.0, tk=1024, tn=None, carry=None):
    pa_n, s, m = a.shape
    pb_n, _, n = b.shape
    po = max(pa_n, pb_n)
    tk = min(tk, s)
    tn = n if tn is None else tn
    nk = s // tk

    def body(a_ref, b_ref, o_ref, acc_ref):
        k = pl.program_id(2)

        @pl.when(k == 0)
        def _():
            acc_ref[...] = jnp.zeros_like(acc_ref)

        acc_ref[...] += _dot_tn(a_ref[...], b_ref[...])

        @pl.when(k == nk - 1)
        def _():
            o_ref[...] = (scale * acc_ref[...]).astype(BF16)

    call = _pcall(body, name=name, grid=(po, n // tn, nk), carry=carry,
                  in_specs=[pl.BlockSpec((None, tk, m), lambda o, j, k: (o if pa_n > 1 else 0, k, 0)),
                            pl.BlockSpec((None, tk, tn), lambda o, j, k: (o if pb_n > 1 else 0, k, j))],
                  out_specs=pl.BlockSpec((None, m, tn), lambda o, j, k: (o, 0, j)),
                  out_shape=_sds((po, m, n), BF16), scratch=[pltpu.VMEM((m, tn), F32)])
    return _carried(call, (a, b), carry)


def _sum_dots(a_ref, b_ref, nj, bt):
    dot = _dot_nt if bt else _dot
    acc = dot(a_ref[0], b_ref[0])
    for j in range(1, nj):
        acc = acc + dot(a_ref[j], b_ref[j])
    return acc


def _mm_acc(a, b, name, out_dtype, tm=512, bt=False, carry=None):
    nj, s, k = a.shape
    n = b.shape[1] if bt else b.shape[2]
    tm = min(tm, s)

    def body(a_ref, b_ref, o_ref):
        o_ref[...] = _sum_dots(a_ref, b_ref, nj, bt).astype(out_dtype)

    call = _pcall(body, name=name, grid=(s // tm,), carry=carry,
                  in_specs=[pl.BlockSpec((nj, tm, k), lambda i: (0, i, 0)),
                            pl.BlockSpec(b.shape, lambda i: (0, 0, 0))],
                  out_specs=pl.BlockSpec((tm, n), lambda i: (i, 0)), out_shape=_sds((s, n), out_dtype))
    return _carried(call, (a, b), carry)


def _mm_acc_rms_bwd(a, b, name, *, x, gain, dres, addend=None, tm=512, bt=False, carry=None):
    nj, s, k = a.shape
    n = b.shape[1] if bt else b.shape[2]
    tm = min(tm, s)
    has_add = addend is not None

    def body(*refs):
        a_ref, b_hbm, x_ref, g_ref, r_ref = refs[:5]
        add_ref = refs[5] if has_add else None
        dx_ref, dxb_ref, dg_ref, b_ref, b_sem = refs[5 + has_add:]

        @pl.when(pl.program_id(0) == 0)
        def _():
            load = pltpu.make_async_copy(b_hbm, b_ref, b_sem)
            load.start()
            dg_ref[...] = jnp.zeros_like(dg_ref)
            load.wait()

        dh = _sum_dots(a_ref, b_ref, nj, bt)
        if has_add:
            dh = dh + add_ref[...]
        xv = x_ref[...]
        r = lax.rsqrt(jnp.mean(xv * xv, axis=-1, keepdims=True) + EPS)
        xh = xv * r
        dyg = dh * g_ref[...]
        dx = r_ref[...] + r * (dyg - xh * jnp.mean(dyg * xh, axis=-1, keepdims=True))
        dx_ref[...] = dx
        dxb_ref[...] = dx.astype(BF16)
        dg_ref[...] += jnp.sum(dh * xh, axis=0, keepdims=True)

    row = pl.BlockSpec((tm, n), lambda i: (i, 0))
    in_specs = [pl.BlockSpec((nj, tm, k), lambda i: (0, i, 0)), HBM_SPEC,
                row, pl.BlockSpec((1, n), lambda i: (0, 0)), row] + ([row] if has_add else [])
    args = (a, b, x, gain, dres) + ((addend,) if has_add else ())
    call = _pcall(body, name=name, grid=(s // tm,), in_specs=in_specs, carry=carry,
                  out_specs=[row, row, pl.BlockSpec((1, n), lambda i: (0, 0))],
                  out_shape=[_sds((s, n), F32), _sds((s, n), BF16), _sds((1, n), F32)],
                  scratch=[pltpu.VMEM(b.shape, b.dtype), pltpu.SemaphoreType.DMA(())])
    return _carried(call, args, carry)


def _adam(w, g, m, v):
    m2 = ADAM_B1 * m + (1.0 - ADAM_B1) * g
    v2 = ADAM_B2 * v + (1.0 - ADAM_B2) * (g * g)
    m_hat = m2 / (1.0 - ADAM_B1 ** ADAM_STEP)
    v_hat = v2 / (1.0 - ADAM_B2 ** ADAM_STEP)
    delta = -ADAM_LR * (m_hat / (jnp.sqrt(v_hat) + ADAM_EPS) + ADAM_WD * w)
    return delta, m2, v2


def _adamw(parts, w, m, v, name):
    _, r, c = parts.shape
    tr = max(t for t in range(16, 257, 16) if r % t == 0)

    def body(p_ref, w_ref, m_ref, v_ref, g_ref, d_ref, m2_ref, v2_ref):
        g = p_ref[0].astype(F32)
        for i in range(1, N_DEV):
            g = g + p_ref[i].astype(F32)
        delta, m2, v2 = _adam(w_ref[...], g, m_ref[...], v_ref[...])
        g_ref[...] = g
        d_ref[...] = delta
        m2_ref[...] = m2
        v2_ref[...] = v2

    blk = pl.BlockSpec((tr, c), lambda i: (i, 0))
    return _pcall(body, name=name, grid=(r // tr,),
                  in_specs=[pl.BlockSpec((N_DEV, tr, c), lambda i: (0, i, 0)), blk, blk, blk],
                  out_specs=[blk] * 4, out_shape=[_sds((r, c), F32)] * 4)(parts, w, m, v)


def _position():
    return lax.axis_index("x"), lax.axis_index("y"), lax.axis_index("c")


def _slot(px, py, pc):
    return 4 * px + 2 * py + pc


def _row_window(ref, rows):
    r0, r1 = rows
    return ref if (r0, r1) == (0, ref.shape[0]) else ref.at[pl.ds(r0, r1 - r0)]


def _split_items(items):
    sources = [src for src, _, _ in items]
    begun = [(a, dest) for a, (_, _, dest) in enumerate(items) if dest is not None]
    aliases = {len(sources) + k: a for k, (a, _) in enumerate(begun)}
    return sources + [dest for _, dest in begun], [rows for _, rows, _ in items], aliases


def _gather_carry(items):
    na = len(items)
    carry_ins, windows, aliases = _split_items(items)

    def plan(ins, outs, sems):
        send_sems, recv_sems, local_sems = sems
        x, y, c = _position()
        me, sibling = (x, y, c), (x, y, 1 - c)
        chips = [(1 - x, y), (x, 1 - y), (1 - x, 1 - y)]
        ins = [_row_window(ins[a], windows[a]) for a in range(na)]

        def block_rows(a, block):
            return _row_window(outs[a].at[_slot(*block)], windows[a])

        def copy(a, k, block, to, src=None):
            rows = block_rows(a, block)
            return pltpu.make_async_remote_copy(src_ref=rows if src is None else src, dst_ref=rows,
                                                send_sem=send_sems.at[k, a], recv_sem=recv_sems.at[k, a],
                                                device_id=to, device_id_type=MESH)

        mine = [pltpu.make_async_copy(ins[a], block_rows(a, me), local_sems.at[a]) for a in range(na)]
        first = [copy(a, 0, me, sibling, src=ins[a]) for a in range(na)]
        for j, chip in enumerate(chips):
            first += [copy(a, 1 + j, me, (*chip, c), src=ins[a]) for a in range(na)]
        landed = [[copy(a, 1 + j, (*chip, c), me) for a in range(na)] for j, chip in enumerate(chips)]
        passed = [[copy(a, 4 + j, (*chip, c), sibling) for a in range(na)] for j, chip in enumerate(chips)]
        from_sibling = [copy(a, 0, sibling, me) for a in range(na)]
        for j, chip in enumerate(chips):
            from_sibling += [copy(a, 4 + j, (*chip, 1 - c), me) for a in range(na)]
        return mine, first, landed, passed, from_sibling

    def start(ins, outs, sems):
        mine, first, _, _, _ = plan(ins, outs, sems)
        for cp in mine + first:
            cp.start()

    def mid(ins, outs, sems):
        _, _, landed, passed, _ = plan(ins, outs, sems)
        for over_ici, onward in zip(landed, passed):
            for cp, fwd in zip(over_ici, onward):
                cp.wait_recv()
                fwd.start()

    def finish(ins, outs, sems):
        mine, first, _, passed, from_sibling = plan(ins, outs, sems)
        for cp in from_sibling:
            cp.wait_recv()
        for cp in first + [fwd for onward in passed for fwd in onward]:
            cp.wait_send()
        for cp in mine:
            cp.wait()

    return _Carry(carry_ins, [_sds((N_DEV,) + src.shape, src.dtype) for src, _, _ in items],
                  [pltpu.SemaphoreType.DMA((7, na)), pltpu.SemaphoreType.DMA((7, na)),
                   pltpu.SemaphoreType.DMA((na,))], start, finish, mid, aliases)


def _exchange_carry(scattered, replicated=()):
    items = list(scattered) + [(a, (0, a.shape[0]), None) for a in replicated]
    na, ns = len(items), len(scattered)
    carry_ins, windows, aliases = _split_items(items)

    def plan(ins, outs, sems):
        send_sems, recv_sems, local_sems = sems
        me = _slot(*_position())

        def source(a, j):
            return _row_window(ins[a].at[j] if a < ns else ins[a], windows[a])

        def copy(a, j, i):
            return pltpu.make_async_remote_copy(src_ref=source(a, j), dst_ref=_row_window(outs[a].at[i], windows[a]),
                                                send_sem=send_sems.at[j, a], recv_sem=recv_sems.at[i, a],
                                                device_id=(j >> 2, (j >> 1) & 1, j & 1), device_id_type=MESH)

        def own(a, j):
            return pltpu.make_async_copy(source(a, j), _row_window(outs[a].at[j], windows[a]), local_sems.at[a])

        return me, copy, own

    def start(ins, outs, sems):
        me, copy, own = plan(ins, outs, sems)
        for a in range(na):
            for j in range(N_DEV):
                @pl.when(me == j)
                def _():
                    own(a, j).start()

                @pl.when(me != j)
                def _():
                    copy(a, j, me).start()

    def finish(ins, outs, sems):
        me, copy, own = plan(ins, outs, sems)
        for a in range(na):
            for j in range(N_DEV):
                @pl.when(me == j)
                def _():
                    for i in range(N_DEV):
                        if i != j:
                            copy(a, j, i).wait_recv()
                    own(a, j).wait()

                @pl.when(me != j)
                def _():
                    copy(a, j, me).wait_send()

    return _Carry(carry_ins, [_sds((N_DEV,) + src.shape[-2:], src.dtype) for src, _, _ in items],
                  [pltpu.SemaphoreType.DMA((N_DEV, na)), pltpu.SemaphoreType.DMA((N_DEV, na)),
                   pltpu.SemaphoreType.DMA((na,))], start, finish, None, aliases)


NQ, NKV = N_HEADS * HEAD, 2 * N_KV * HEAD


class _Mesh:
    def __init__(self, shards):
        self.shards, self.full, self.received, self.cache = shards, {}, {}, {}

    def fetch(self, wanted):
        items = []
        for want in wanted:
            name, r0, r1 = want if isinstance(want, tuple) else (want, 0, self.shards[want].shape[0])
            items.append((self.shards[name], (r0, r1), self.full.get(name)))
        return _gather_carry(items)

    def fetched(self, wanted, results):
        self.full.update(zip([want[0] if isinstance(want, tuple) else want for want in wanted], results))

    def send(self, *payloads):
        return _exchange_carry([(parts, rows or (0, parts.shape[1]), self.received.get(name))
                                for name, parts, rows in payloads])

    def sent(self, names, results):
        self.received.update(zip(names, results))

    def w(self, key):
        if key not in self.cache:
            self.cache[key] = self._layout(key)
        return self.cache[key]

    def _layout(self, key):
        if key in ("gu1", "gu2"):
            return self.full[key]
        if key in ("d1", "d2"):
            return self.full[key].reshape(4, FS, D)
        if key in ("out", "q", "o"):
            return self.full[key].reshape(D, D)
        if key == "kv":
            return self.full["kv"]
        if key == "convw":
            rows = self.full["conv"][:, :3, :].transpose(1, 0, 2).reshape(3, D)
            return jnp.concatenate([rows, jnp.zeros((5, D), F32)], axis=0)
        w_in_t = self.full["win"].reshape(-1, D)
        if key == "wa":
            return w_in_t[NQ + NKV:].reshape(5, D, D)
        assert key == "wb", key
        return jnp.stack([w_in_t[:NQ], jnp.pad(w_in_t[NQ:NQ + NKV], ((0, D - NKV), (0, 0)))])


def _w_in_parts(dw_a, dw_b):
    return jnp.concatenate([dw_b[0], dw_b[1][:NKV], dw_a.reshape(5 * D, D)], axis=0).reshape(N_DEV, -1, D)


def _forward_backward(x, mem, target, g, rel_bias, sinks, ex):
    s = x.shape[0]
    def fetching(wanted, call, *args, **kw):
        res, got = call(*args, carry=ex.fetch(wanted), **kw)
        ex.fetched(wanted, got)
        return res

    h1 = fetching(["gu1", "d1", "conv"], _rmsnorm, x, g["ffn1"], "norm_ffn1")
    gu1, a1 = fetching([("win", 0, 720)], _ffn_up, h1, ex.w("gu1").reshape(2, 4, FS, D), "ffn1_up")
    x1, h2 = fetching([("win", 720, 832), "out", "q"], _mm_res_norm, a1, ex.w("d1"), x, g["mix"], 0.5, "ffn1_down")
    pa = fetching(["gu2"], _mm_nn, h2, ex.w("wa"), "in_proj_a", bt=True)
    pb = fetching(["o"], _mm_nn, h2, ex.w("wb"), "in_proj_b", bt=True)
    biasm = _bias_build(rel_bias, "bias_build")
    attn, lse = fetching(["kv", "d2"], _swa_fwd, pb, biasm, sinks, "swa_fwd")
    merged = _conv_merge_fwd(pa, attn, ex.w("convw"), "conv_merge_fwd")
    (x2, h3), _ = _mm_res_norm(merged[None], ex.w("out")[None], x1, g["xattn"], 1.0, "out_proj")
    q2 = _mm_nn(h3, ex.w("q")[None], "xattn_q")[0][0]
    mh, _ = _rmsnorm(mem, g["mem"], "norm_mem")
    kv2 = _mm_nn(mh, ex.w("kv"), "xattn_kv")[0]
    o, lse2 = _xattn_fwd(q2, kv2, "xattn_fwd")
    (x3, h4), _ = _mm_res_norm(o[None], ex.w("o")[None], x2, g["ffn2"], 1.0, "xattn_o")
    (gu2, a2), _ = _ffn_up(h4, ex.w("gu2").reshape(2, 4, FS, D), "ffn2_up")
    dx4, dx4b, loss, d_final = _ffn_down_loss(a2, ex.w("d2"), x3, g["final"], target, "ffn2_down_loss")
    def sending(payloads, call, *args, **kw):
        res, got = call(*args, carry=ex.send(*payloads), **kw)
        ex.sent([name for name, _, _ in payloads], got)
        return res

    dw_d2 = _mm_tn(a2, dx4b[None], "dw_ffn2_down", scale=0.5)[0].reshape(N_DEV, -1, D)
    dgu2 = sending([("d2", dw_d2, None)], _ffn_down_bwd, dx4b, ex.w("d2"), gu2, "ffn2_down_bwd").reshape(8, s, FS)
    dw_gu2 = _mm_tn(dgu2, h4[None], "dw_ffn2_up")[0]
    dx3, dx3b, d_ffn2 = sending([("gu2", dw_gu2, (0, 400))], _mm_acc_rms_bwd, dgu2, ex.w("gu2"), "ffn2_up_bwd",
                                x=x3, gain=g["ffn2"], dres=dx4)
    do, _ = _mm_acc(dx3b[None], ex.w("o")[None], "xattn_o_bwd", BF16, bt=True)
    dw_o = _mm_tn(o[None], dx3b[None], "dw_xattn_o")[0].reshape(N_DEV, -1, D)
    dq2, dkv2 = sending([("o", dw_o, None)], _xattn_bwd, q2, kv2, o, do, lse2, "xattn_bwd")
    dkv2b = dkv2.astype(BF16)
    dw_q = _mm_tn(h3[None], dq2[None], "dw_xattn_q")[0].reshape(N_DEV, -1, D)
    dx2, dx2b, d_xattn = sending([("q", dw_q, None)], _mm_acc_rms_bwd, dq2[None], ex.w("q")[None], "xattn_q_bwd",
                                 x=x2, gain=g["xattn"], dres=dx3, bt=True)
    dw_kv = _mm_tn(mh[None], dkv2b, "dw_xattn_kv")[0]
    (_, _, d_mem), _ = _mm_acc_rms_bwd(dkv2b, ex.w("kv"), "xattn_kv_bwd", x=mem, gain=g["mem"],
                                       dres=jnp.zeros_like(mem), bt=True)
    dmerged, _ = _mm_acc(dx2b[None], ex.w("out")[None], "out_proj_bwd", BF16, bt=True)
    dw_out = _mm_tn(merged[None], dx2b[None], "dw_out_proj")[0].reshape(N_DEV, -1, D)
    dattn, dpa, d_convw = sending([("kv", dw_kv, None)], _conv_merge_bwd,
                                  dmerged, pa, attn, ex.w("convw"), "conv_merge_bwd")
    dpb, dbias, d_sinks = sending([("gu2", dw_gu2, (400, FS)), ("out", dw_out, None)], _swa_bwd,
                                  pb, attn, dattn, lse, biasm, sinks, "swa_bwd")
    d_relb = _bias_bwd(dbias, "bias_bwd")
    dw_in = _w_in_parts(_mm_tn(dpa, h2[None], "dw_in_proj_a")[0], _mm_tn(dpb, h2[None], "dw_in_proj_b")[0])
    dh2_b = sending([("win", dw_in, (0, 208))], _mm_acc, dpb, ex.w("wb"), "in_proj_b_bwd", F32)
    dx1, dx1b, d_mix = sending([("win", dw_in, (208, 672))], _mm_acc_rms_bwd, dpa, ex.w("wa"), "in_proj_a_bwd",
                               x=x1, gain=g["mix"], dres=dx2, addend=dh2_b)
    dw_d1 = sending([("win", dw_in, (672, 832))], _mm_tn, a1, dx1b[None], "dw_ffn1_down", scale=0.5)
    dw_d1 = dw_d1.reshape(N_DEV, -1, D)
    dgu1 = sending([("d1", dw_d1, None)], _ffn_down_bwd, dx1b, ex.w("d1"), gu1, "ffn1_down_bwd").reshape(8, s, FS)
    dw_gu1 = _mm_tn(dgu1, h1[None], "dw_ffn1_up")[0]
    dx0, _, d_ffn1 = sending([("gu1", dw_gu1, None)], _mm_acc_rms_bwd, dgu1, ex.w("gu1"), "ffn1_up_bwd",
                             x=x, gain=g["ffn1"], dres=dx1)

    relb_row = jnp.concatenate([d_relb[:, :REL_BUCKETS].T.reshape(1, REL_BUCKETS * N_HEADS), d_sinks[:, :N_HEADS],
                                jnp.zeros((1, D - REL_BUCKETS * N_HEADS - N_HEADS), F32)], axis=1)
    loss_row = jnp.concatenate([loss[0:1, 0:1], jnp.zeros((1, D - 1), F32)], axis=1)
    small = jnp.concatenate([d_ffn1, d_mix, d_xattn, d_mem, d_ffn2, d_final, relb_row, loss_row, d_convw[0:3],
                             jnp.zeros((SMALL_ROWS - ROW_CONV - 3, D), F32)], axis=0)
    return dx0, small


def _pack_small(norms, final, relb, sinks, conv_local, me):
    relb_row = jnp.concatenate([relb.reshape(1, -1), sinks.reshape(1, -1),
                                jnp.zeros((1, D - REL_BUCKETS * N_HEADS - N_HEADS), F32)], axis=1)
    conv_rows = lax.dynamic_update_slice(jnp.zeros((3, D), F32), conv_local.reshape(3, -1), (0, 128 * me))
    return jnp.concatenate(list(norms) + [final.reshape(1, D), relb_row, jnp.zeros((1, D), F32), conv_rows,
                                          jnp.zeros((SMALL_ROWS - ROW_CONV - 3, D), F32)], axis=0)


def kernel(x, mem, positions, rel_bias, ffn1_norm, ffn1_w_gu, ffn1_w_down, mix_norm, w_in, sinks, conv_w, w_out, xattn_norm, mem_norm, xattn_wq, xattn_wkv, xattn_wo, ffn2_norm, ffn2_w_gu, ffn2_w_down, final_norm, loss_target, m_rel_bias, m_ffn1_norm, m_ffn1_w_gu, m_ffn1_w_down, m_mix_norm, m_w_in, m_sinks, m_conv_w, m_w_out, m_xattn_norm, m_mem_norm, m_xattn_wq, m_xattn_wkv, m_xattn_wo, m_ffn2_norm, m_ffn2_w_gu, m_ffn2_w_down, m_final_norm, v_rel_bias, v_ffn1_norm, v_ffn1_w_gu, v_ffn1_w_down, v_mix_norm, v_w_in, v_sinks, v_conv_w, v_w_out, v_xattn_norm, v_mem_norm, v_xattn_wq, v_xattn_wkv, v_xattn_wo, v_ffn2_norm, v_ffn2_w_gu, v_ffn2_w_down, v_final_norm):
    del positions
    me = _slot(*_position())
    big = dict(gu1=(ffn1_w_gu, m_ffn1_w_gu, v_ffn1_w_gu), d1=(ffn1_w_down, m_ffn1_w_down, v_ffn1_w_down),
               win=(w_in, m_w_in, v_w_in), out=(w_out, m_w_out, v_w_out), q=(xattn_wq, m_xattn_wq, v_xattn_wq),
               kv=(xattn_wkv, m_xattn_wkv, v_xattn_wkv), o=(xattn_wo, m_xattn_wo, v_xattn_wo),
               gu2=(ffn2_w_gu, m_ffn2_w_gu, v_ffn2_w_gu), d2=(ffn2_w_down, m_ffn2_w_down, v_ffn2_w_down))
    order = list(big)
    transposed = ("gu1", "gu2", "win")
    local = {k: tuple(t[0].T if k in transposed else t[0] for t in big[k]) for k in order}
    shards = {k: local[k][0].astype(BF16) for k in order}
    shards["conv"] = jnp.concatenate([conv_w[0], jnp.zeros((5, 128), F32)], axis=0)
    ex = _Mesh(shards)
    gains = dict(ffn1=ffn1_norm, mix=mix_norm, xattn=xattn_norm, mem=mem_norm, ffn2=ffn2_norm,
                 final=final_norm.reshape(1, D))
    dx, small = _forward_backward(x[0], mem[0], loss_target[0], gains, rel_bias, sinks, ex)
    small_parts = _run_alone(_exchange_carry([], [small]), "exchange_small")[0]
    big_out = {k: _adamw(ex.received[k], *local[k], "adamw_" + k) for k in order}
    big_out = {k: [t.T if k in transposed else t for t in big_out[k]] for k in order}
    packed = [_pack_small(norms, final, relb, sk, conv, me) for norms, final, relb, sk, conv in (
        ((ffn1_norm, mix_norm, xattn_norm, mem_norm, ffn2_norm), final_norm, rel_bias, sinks, conv_w),
        ((m_ffn1_norm, m_mix_norm, m_xattn_norm, m_mem_norm, m_ffn2_norm), m_final_norm, m_rel_bias, m_sinks, m_conv_w),
        ((v_ffn1_norm, v_mix_norm, v_xattn_norm, v_mem_norm, v_ffn2_norm), v_final_norm, v_rel_bias, v_sinks, v_conv_w))]
    small_out = _adamw(small_parts, *packed, "adamw_small")

    def unpack(t):
        conv = lax.dynamic_slice(t[ROW_CONV:ROW_CONV + 3], (0, 128 * me), (3, 128))[None]
        nrel = REL_BUCKETS * N_HEADS
        return dict(ffn1_norm=t[0:1], mix_norm=t[1:2], xattn_norm=t[2:3], mem_norm=t[3:4], ffn2_norm=t[4:5],
                    final_norm=t[5], rel_bias=t[ROW_RELB, :nrel].reshape(REL_BUCKETS, N_HEADS),
                    sinks=t[ROW_RELB:ROW_RELB + 1, nrel:nrel + N_HEADS], conv_w=conv)

    names = dict(gu1="ffn1_w_gu", d1="ffn1_w_down", win="w_in", out="w_out", q="xattn_wq", kv="xattn_wkv",
                 o="xattn_wo", gu2="ffn2_w_gu", d2="ffn2_w_down")
    results = []
    for idx in range(4):
        leaves = unpack(small_out[idx])
        leaves.update({names[k]: big_out[k][idx][None] for k in order})
        results.append(leaves)
    weights = ("rel_bias", "ffn1_norm", "ffn1_w_gu", "ffn1_w_down", "mix_norm", "w_in", "sinks", "conv_w", "w_out",
               "xattn_norm", "mem_norm", "xattn_wq", "xattn_wkv", "xattn_wo", "ffn2_norm", "ffn2_w_gu", "ffn2_w_down",
               "final_norm")
    loss = small_out[0][ROW_LOSS, 0]
    return (loss, dx[None], *[leaves[n] for leaves in results for n in weights])
```

```python
import math

import numpy as np
import jax
import jax.numpy as jnp
from jax import lax
from jax.experimental import pallas as pl
from jax.experimental.pallas import tpu as pltpu

F32, BF16 = jnp.float32, jnp.bfloat16
MESH = pl.DeviceIdType.MESH

D = 1024
N_DEV = 8
D_FF = 2816
FS = D_FF // 4
HEAD = 64
N_HEADS, N_KV = 16, 4
BLK = 128
XH, XHD = 4, 256
REL_BUCKETS, REL_EXACT, REL_MAX_DIST = 32, 16, 128
EPS, NEG = 1e-6, -1e30
ADAM_LR, ADAM_B1, ADAM_B2, ADAM_EPS, ADAM_WD, ADAM_STEP = 0.001, 0.9, 0.999, 1e-08, 0.01, 10
VMEM_LIMIT_V7X = 56 * 2**20
SMALL_ROWS = 16
ROW_RELB, ROW_LOSS, ROW_CONV = 6, 7, 8


def _bucket_thresholds():
    n = np.arange(REL_MAX_DIST)
    nf = np.maximum(n, 1).astype(np.float32)
    large = REL_EXACT + (np.log(nf / np.float32(REL_EXACT)) / np.float32(math.log(REL_MAX_DIST / REL_EXACT))
                         * np.float32(REL_BUCKETS - REL_EXACT)).astype(np.int32)
    b = np.where(n < REL_EXACT, n, np.minimum(large, REL_BUCKETS - 1))
    return [int(np.argmax(b >= REL_EXACT + k)) for k in range(1, REL_BUCKETS - REL_EXACT)]


BUCKET_THRESHOLDS = _bucket_thresholds()


HBM_SPEC = pl.BlockSpec(memory_space=pl.ANY)


class _Carry:
    def __init__(self, ins, outs, sems, start, finish, mid=None, aliases=None):
        self.ins, self.outs, self.sems = list(ins), list(outs), list(sems)
        self.start, self.finish, self.mid, self.aliases = start, finish, mid, dict(aliases or {})


def _pcall(body, *, name, grid, in_specs, out_specs, out_shape, scratch=(), carry=None):
    params = pltpu.CompilerParams(dimension_semantics=("arbitrary",) * len(grid), vmem_limit_bytes=VMEM_LIMIT_V7X)
    if carry is None:
        return pl.pallas_call(body, name=name, grid=grid, in_specs=in_specs, out_specs=out_specs,
                              out_shape=out_shape, scratch_shapes=list(scratch), compiler_params=params)
    single = not isinstance(out_shape, (list, tuple))
    own_specs, own_shapes = ([out_specs], [out_shape]) if single else (list(out_specs), list(out_shape))
    n_in, n_out, n_scr = len(in_specs), len(own_shapes), len(scratch)
    n_cin, n_cout = len(carry.ins), len(carry.outs)
    steps = math.prod(grid)
    mid_step = max(steps - 1 - max(steps // 8, 1), 0)

    def carrying(*refs):
        ins, refs = refs[:n_in], refs[n_in:]
        cins, refs = refs[:n_cin], refs[n_cin:]
        outs, refs = refs[:n_out], refs[n_out:]
        couts, refs = refs[:n_cout], refs[n_cout:]
        scr, csems = refs[:n_scr], refs[n_scr:]
        step = 0
        for axis, size in enumerate(grid):
            step = step * size + pl.program_id(axis)

        @pl.when(step == 0)
        def _():
            carry.start(cins, couts, csems)

        body(*ins, *outs, *scr)
        if carry.mid is not None:
            @pl.when(step == mid_step)
            def _():
                carry.mid(cins, couts, csems)

        @pl.when(step == steps - 1)
        def _():
            carry.finish(cins, couts, csems)

    call = pl.pallas_call(carrying, name=name, grid=grid, in_specs=list(in_specs) + [HBM_SPEC] * n_cin,
                          out_specs=own_specs + [HBM_SPEC] * n_cout, out_shape=own_shapes + carry.outs,
                          scratch_shapes=list(scratch) + carry.sems, compiler_params=params,
                          input_output_aliases={n_in + i: n_out + o for i, o in carry.aliases.items()})

    def run(*args):
        res = call(*args, *carry.ins)
        return (res[0] if single else res[:n_out]), res[n_out:]

    return run


def _run_alone(carry, name):
    n_cin, n_cout = len(carry.ins), len(carry.outs)

    def body(*refs):
        cins, couts, csems = refs[:n_cin], refs[n_cin:n_cin + n_cout], refs[n_cin + n_cout:]
        carry.start(cins, couts, csems)
        if carry.mid is not None:
            carry.mid(cins, couts, csems)
        carry.finish(cins, couts, csems)

    return pl.pallas_call(body, name=name, in_specs=[HBM_SPEC] * n_cin, out_specs=[HBM_SPEC] * n_cout,
                          out_shape=carry.outs, scratch_shapes=carry.sems,
                          input_output_aliases=carry.aliases)(*carry.ins)


def _dot(a, b):
    return jnp.dot(a, b, preferred_element_type=F32)


def _dot_nt(a, b):
    return lax.dot_general(a, b, (((1,), (1,)), ((), ())), preferred_element_type=F32)


def _dot_tn(a, b):
    return lax.dot_general(a, b, (((0,), (0,)), ((), ())), preferred_element_type=F32)


def _sds(shape, dtype):
    return jax.ShapeDtypeStruct(tuple(shape), dtype)


def _carried(call, args, carry):
    return call(*args) if carry is not None else (call(*args), ())


def _rmsnorm(x, g, name, carry=None):
    m, d = x.shape
    tm = min(512, m)

    def body(x_ref, g_ref, h_ref):
        xv = x_ref[...]
        r = lax.rsqrt(jnp.mean(xv * xv, axis=-1, keepdims=True) + EPS)
        h_ref[...] = (xv * r * g_ref[...]).astype(BF16)

    call = _pcall(body, name=name, grid=(m // tm,), carry=carry,
                  in_specs=[pl.BlockSpec((tm, d), lambda i: (i, 0)), pl.BlockSpec((1, d), lambda i: (0, 0))],
                  out_specs=pl.BlockSpec((tm, d), lambda i: (i, 0)), out_shape=_sds((m, d), BF16))
    return _carried(call, (x, g), carry)


def _mm_nn(a, b, name, tm=1024, bt=False, carry=None):
    m, k = a.shape
    nj = b.shape[0]
    n = b.shape[1] if bt else b.shape[2]
    tm = min(tm, m)
    dot = _dot_nt if bt else _dot

    def body(a_ref, b_ref, o_ref):
        o_ref[...] = dot(a_ref[...], b_ref[...]).astype(BF16)

    call = _pcall(body, name=name, grid=(nj, m // tm),
                  in_specs=[pl.BlockSpec((tm, k), lambda j, i: (i, 0)),
                            pl.BlockSpec((None,) + b.shape[1:], lambda j, i: (j, 0, 0))],
                  out_specs=pl.BlockSpec((None, tm, n), lambda j, i: (j, i, 0)),
                  out_shape=_sds((nj, m, n), BF16), carry=carry)
    return _carried(call, (a, b), carry)


def _ffn_up(h, w4, name, tm=1024, carry=None):
    s, d = h.shape
    tm = min(tm, s)

    def body(h_ref, w_ref, gu_ref, a_ref):
        hv = h_ref[...]
        g = _dot_nt(hv, w_ref[0])
        u = _dot_nt(hv, w_ref[1])
        gu_ref[0] = g.astype(BF16)
        gu_ref[1] = u.astype(BF16)
        a_ref[...] = (g * jax.nn.sigmoid(g) * u).astype(BF16)

    call = _pcall(body, name=name, grid=(4, s // tm),
                  in_specs=[pl.BlockSpec((tm, d), lambda p, i: (i, 0)),
                            pl.BlockSpec((2, None, FS, d), lambda p, i: (0, p, 0, 0))],
                  out_specs=[pl.BlockSpec((2, None, tm, FS), lambda p, i: (0, p, i, 0)),
                             pl.BlockSpec((None, tm, FS), lambda p, i: (p, i, 0))],
                  out_shape=[_sds((2, 4, s, FS), BF16), _sds((4, s, FS), BF16)], carry=carry)
    return _carried(call, (h, w4), carry)


def _mm_res_norm(a, w, xres, gain, scale, name, tm=512, carry=None):
    npart, s, kp = a.shape
    tm = min(tm, s)

    def body(a_ref, w_ref, x_ref, g_ref, xo_ref, h_ref):
        acc = _dot(a_ref[0], w_ref[0])
        for p in range(1, npart):
            acc = acc + _dot(a_ref[p], w_ref[p])
        xn = x_ref[...] + scale * acc
        xo_ref[...] = xn
        r = lax.rsqrt(jnp.mean(xn * xn, axis=-1, keepdims=True) + EPS)
        h_ref[...] = (xn * r * g_ref[...]).astype(BF16)

    call = _pcall(body, name=name, grid=(s // tm,),
                  in_specs=[pl.BlockSpec((npart, tm, kp), lambda i: (0, i, 0)),
                            pl.BlockSpec((npart, kp, D), lambda i: (0, 0, 0)),
                            pl.BlockSpec((tm, D), lambda i: (i, 0)),
                            pl.BlockSpec((1, D), lambda i: (0, 0))],
                  out_specs=[pl.BlockSpec((tm, D), lambda i: (i, 0)), pl.BlockSpec((tm, D), lambda i: (i, 0))],
                  out_shape=[_sds((s, D), F32), _sds((s, D), BF16)], carry=carry)
    return _carried(call, (a, w, xres, gain), carry)


def _ffn_down_loss(a, w, xres, gain, target, name, tm=512):
    npart, s, kp = a.shape
    tm = min(tm, s)

    def body(a_ref, w_ref, x_ref, g_ref, t_ref, dx_ref, dxb_ref, loss_ref, dg_ref):
        i = pl.program_id(0)
        acc = _dot(a_ref[0], w_ref[0])
        for p in range(1, npart):
            acc = acc + _dot(a_ref[p], w_ref[p])
        xn = x_ref[...] + 0.5 * acc
        r = lax.rsqrt(jnp.mean(xn * xn, axis=-1, keepdims=True) + EPS)
        xh = xn * r
        gv = g_ref[...]
        err = xh * gv - t_ref[...]
        part = 0.5 * jnp.sum(jnp.mean(err * err, axis=-1, keepdims=True), axis=0, keepdims=True)
        dy = err * (1.0 / D)
        dyg = dy * gv
        dxn = r * (dyg - xh * jnp.mean(dyg * xh, axis=-1, keepdims=True))
        dx_ref[...] = dxn
        dxb_ref[...] = dxn.astype(BF16)

        @pl.when(i == 0)
        def _():
            loss_ref[...] = jnp.zeros_like(loss_ref)
            dg_ref[...] = jnp.zeros_like(dg_ref)

        loss_ref[...] += jnp.broadcast_to(part, loss_ref.shape)
        dg_ref[...] += jnp.sum(dy * xh, axis=0, keepdims=True)

    return _pcall(body, name=name, grid=(s // tm,),
                  in_specs=[pl.BlockSpec((npart, tm, kp), lambda i: (0, i, 0)),
                            pl.BlockSpec((npart, kp, D), lambda i: (0, 0, 0)),
                            pl.BlockSpec((tm, D), lambda i: (i, 0)),
                            pl.BlockSpec((1, D), lambda i: (0, 0)),
                            pl.BlockSpec((tm, D), lambda i: (i, 0))],
                  out_specs=[pl.BlockSpec((tm, D), lambda i: (i, 0)), pl.BlockSpec((tm, D), lambda i: (i, 0)),
                             pl.BlockSpec((8, 128), lambda i: (0, 0)), pl.BlockSpec((1, D), lambda i: (0, 0))],
                  out_shape=[_sds((s, D), F32), _sds((s, D), BF16), _sds((8, 128), F32), _sds((1, D), F32)],
                  )(a, w, xres, gain, target)


def _window_tiles():
    i = lax.broadcasted_iota(jnp.int32, (BLK, BLK), 0)
    j = lax.broadcasted_iota(jnp.int32, (BLK, BLK), 1)
    rel = (i - j) & (BLK - 1)
    large = jnp.full_like(rel, REL_EXACT)
    for t in BUCKET_THRESHOLDS:
        large = large + (rel >= t).astype(jnp.int32)
    return j <= i, jnp.where(rel < REL_EXACT, rel, large)


def _bias_build(rel_bias, name):
    def body(rb_ref, o_ref):
        _, bucket = _window_tiles()

        def per_head(h, carry):
            acc = jnp.zeros((BLK, BLK), F32)
            for b in range(REL_BUCKETS):
                acc = jnp.where(bucket == b, rb_ref[b, h], acc)
            o_ref[h] = acc
            return carry

        lax.fori_loop(0, N_HEADS, per_head, 0)

    return _pcall(body, name=name, grid=(1,),
                  in_specs=[pl.BlockSpec(memory_space=pltpu.SMEM)],
                  out_specs=pl.BlockSpec((N_HEADS, BLK, BLK), lambda i: (0, 0, 0)),
                  out_shape=_sds((N_HEADS, BLK, BLK), F32))(rel_bias)


def _bias_bwd(dbias, name):
    def body(db_ref, o_ref):
        _, bucket = _window_tiles()
        lane = lax.broadcasted_iota(jnp.int32, (N_HEADS, 128), 1)

        def per_bucket(b, out):
            mb = (bucket == b).astype(F32)
            per_col = jnp.sum(db_ref[...] * mb[None, :, :], axis=1)
            return jnp.where(lane == b, jnp.sum(per_col, axis=1, keepdims=True), out)

        o_ref[...] = lax.fori_loop(0, REL_BUCKETS, per_bucket, jnp.zeros((N_HEADS, 128), F32))

    return _pcall(body, name=name, grid=(1,),
                  in_specs=[pl.BlockSpec((N_HEADS, BLK, BLK), lambda i: (0, 0, 0))],
                  out_specs=pl.BlockSpec((N_HEADS, 128), lambda i: (0, 0)),
                  out_shape=_sds((N_HEADS, 128), F32))(dbias)


PAIR = 2 * HEAD
GROUP = N_HEADS // N_KV
SWA_SCALE = HEAD ** -0.5


def _window_masks(n):
    i = lax.broadcasted_iota(jnp.int32, (GROUP * BLK, BLK), 0) & (BLK - 1)
    j = lax.broadcasted_iota(jnp.int32, (GROUP * BLK, BLK), 1)
    return j <= i, jnp.logical_and(n == 0, j > i), j < HEAD


def _kv_twice(ref, base, g, low):
    slab = ref[:, base + PAIR * (g // 2): base + PAIR * (g // 2 + 1)]
    swapped = pltpu.roll(slab, HEAD, 1)
    return jnp.where(low, slab, swapped) if g % 2 == 0 else jnp.where(low, swapped, slab)


def _stack_heads(ref, g, low):
    parts = []
    for r in range(2):
        slab = ref[:, PAIR * (2 * g + r): PAIR * (2 * g + r + 1)]
        zero = jnp.zeros_like(slab)
        parts += [jnp.where(low, slab, zero), jnp.where(low, zero, slab)]
    return jnp.concatenate(parts, axis=0)


def _unstack_heads(t, low):
    return [jnp.where(low, t[2 * r * BLK:(2 * r + 1) * BLK], t[(2 * r + 1) * BLK:(2 * r + 2) * BLK])
            for r in range(2)]


def _head_rows(t, k):
    return t[k * BLK:(k + 1) * BLK]


def _per_head_column(values):
    head = lax.broadcasted_iota(jnp.int32, (GROUP * BLK, 1), 0) // BLK
    col = jnp.full((GROUP * BLK, 1), values[0], F32)
    for k in range(1, GROUP):
        col = jnp.where(head == k, values[k], col)
    return col


def _window_logits(q4, kc, kp, bias4, own, absent):
    sc = jnp.where(own, _dot_nt(q4, kc), _dot_nt(q4, kp)) * SWA_SCALE + bias4
    return jnp.where(absent, NEG, sc)


def _split_window(t, own):
    zero = jnp.zeros_like(t)
    return jnp.where(own, t, zero), jnp.where(own, zero, t)


def _swa_fwd(pb, bias, sinks, name, carry=None):
    _, s, _ = pb.shape
    nb = s // BLK
    kvw = 2 * N_KV * HEAD

    def body(q_ref, kc_ref, kp_ref, b_ref, sk_ref, o_ref, lse_ref):
        own, absent, low4 = _window_masks(pl.program_id(0))
        low = low4[:BLK]
        lane = lax.broadcasted_iota(jnp.int32, (BLK, 128), 1)
        lse_t = jnp.zeros((BLK, 128), F32)
        for g in range(N_KV):
            q4 = _stack_heads(q_ref, g, low)
            kc, kp = _kv_twice(kc_ref, 0, g, low), _kv_twice(kp_ref, 0, g, low)
            vc, vp = _kv_twice(kc_ref, N_KV * HEAD, g, low), _kv_twice(kp_ref, N_KV * HEAD, g, low)
            bias4 = b_ref[GROUP * g:GROUP * (g + 1)].reshape(GROUP * BLK, BLK)
            sc = _window_logits(q4, kc, kp, bias4, own, absent)
            sk = _per_head_column([sk_ref[0, GROUP * g + k] for k in range(GROUP)])
            m = jnp.maximum(jnp.max(sc, axis=1, keepdims=True), sk)
            p = jnp.exp(sc - m)
            l = jnp.sum(p, axis=1, keepdims=True) + jnp.exp(sk - m)
            p_own, p_prev = _split_window(p.astype(BF16), own)
            out = (_dot(p_own, vc) + _dot(p_prev, vp)) * (1.0 / l)
            for r, slab in enumerate(_unstack_heads(out, low)):
                o_ref[:, PAIR * (2 * g + r): PAIR * (2 * g + r + 1)] = slab.astype(BF16)
            lse4 = m + jnp.log(l)
            for k in range(GROUP):
                lse_t = jnp.where(lane == GROUP * g + k, _head_rows(lse4, k), lse_t)
        lse_ref[...] = lse_t

    call = _pcall(body, name=name, grid=(nb,),
                  in_specs=[pl.BlockSpec((None, BLK, D), lambda n: (0, n, 0)),
                            pl.BlockSpec((None, BLK, kvw), lambda n: (1, n, 0)),
                            pl.BlockSpec((None, BLK, kvw), lambda n: (1, jnp.maximum(n - 1, 0), 0)),
                            pl.BlockSpec((N_HEADS, BLK, BLK), lambda n: (0, 0, 0)),
                            pl.BlockSpec(memory_space=pltpu.SMEM)],
                  out_specs=[pl.BlockSpec((BLK, D), lambda n: (n, 0)), pl.BlockSpec((BLK, 128), lambda n: (n, 0))],
                  out_shape=[_sds((s, D), BF16), _sds((s, 128), F32)], carry=carry)
    return _carried(call, (pb, pb, pb, bias, sinks), carry)


def _fold_halves(t, g, low):
    folded = jnp.where(low, t, 0.0) + pltpu.roll(jnp.where(low, 0.0, t), HEAD, 1)
    return folded if g % 2 == 0 else pltpu.roll(folded, HEAD, 1)


def _swa_bwd(pb, attn, dattn, lse, bias, sinks, name, carry=None):
    _, s, _ = pb.shape
    nb = s // BLK
    kvw = 2 * N_KV * HEAD
    voff = N_KV * HEAD

    def body(q_ref, kc_ref, kp_ref, o_ref, do_ref, lse_ref, b_ref, skrow_ref, dpb_ref, dbias_ref, dsk_ref,
             dq_hold, kv_hold, dq_new, kv_prev, kv_cur):
        n = pl.program_id(0)

        @pl.when(n == 0)
        def _():
            dbias_ref[...] = jnp.zeros_like(dbias_ref)
            dsk_ref[...] = jnp.zeros_like(dsk_ref)
            dq_hold[...] = jnp.zeros_like(dq_hold)
            kv_hold[...] = jnp.zeros_like(kv_hold)

        @pl.when(n < nb)
        def _():
            own, absent, low4 = _window_masks(n)
            low = low4[:BLK]
            lane = lax.broadcasted_iota(jnp.int32, (BLK, 128), 1)
            delta_t = jnp.zeros((BLK, 128), F32)
            ones = jnp.ones((PAIR, 128), BF16)
            for pair_of_kv in range(N_KV // 2):
                slab_grads = [jnp.zeros((BLK, PAIR), F32) for _ in range(4)]
                for g in (2 * pair_of_kv, 2 * pair_of_kv + 1):
                    q4, do4 = _stack_heads(q_ref, g, low), _stack_heads(do_ref, g, low)
                    kc, kp = _kv_twice(kc_ref, 0, g, low), _kv_twice(kp_ref, 0, g, low)
                    vc, vp = _kv_twice(kc_ref, voff, g, low), _kv_twice(kp_ref, voff, g, low)
                    o_slabs = [o_ref[:, PAIR * (2 * g + r): PAIR * (2 * g + r + 1)] for r in range(2)]
                    o4 = jnp.concatenate([o_slabs[0], o_slabs[0], o_slabs[1], o_slabs[1]], axis=0)
                    delta = _dot(do4 * o4, ones)
                    heads = range(GROUP * g, GROUP * (g + 1))
                    lse4 = jnp.concatenate([lse_ref[:, h:h + 1] for h in heads], axis=0)
                    bias4 = b_ref[GROUP * g:GROUP * (g + 1)].reshape(GROUP * BLK, BLK)
                    p = jnp.exp(_window_logits(q4, kc, kp, bias4, own, absent) - lse4)
                    dp = jnp.where(own, _dot_nt(do4, vc), _dot_nt(do4, vp))
                    ds = p * (dp - delta)
                    dbias_ref[GROUP * g:GROUP * (g + 1)] += ds.reshape(GROUP, BLK, BLK)
                    for k, h in enumerate(heads):
                        delta_t = jnp.where(lane == h, _head_rows(delta, k), delta_t)
                    ds_own, ds_prev = _split_window((ds * SWA_SCALE).astype(BF16), own)
                    p_own, p_prev = _split_window(p.astype(BF16), own)
                    dq4 = _dot(ds_own, kc) + _dot(ds_prev, kp)
                    for r, slab in enumerate(_unstack_heads(dq4, low)):
                        dq_new[:, PAIR * (2 * g + r): PAIR * (2 * g + r + 1)] = slab
                    grads = [_dot_tn(ds_own, q4), _dot_tn(ds_prev, q4), _dot_tn(p_own, do4), _dot_tn(p_prev, do4)]
                    slab_grads = [t + _fold_halves(dk, g, low) for t, dk in zip(slab_grads, grads)]
                ks = slice(PAIR * pair_of_kv, PAIR * (pair_of_kv + 1))
                vs = slice(voff + PAIR * pair_of_kv, voff + PAIR * (pair_of_kv + 1))
                kv_cur[:, ks], kv_prev[:, ks], kv_cur[:, vs], kv_prev[:, vs] = slab_grads
            dsk_ref[...] -= jnp.sum(jnp.exp(skrow_ref[...] - lse_ref[...]) * delta_t, axis=0, keepdims=True)

        @pl.when(n == nb)
        def _():
            kv_prev[...] = jnp.zeros_like(kv_prev)

        dpb_ref[0] = dq_hold[...].astype(BF16)
        dpb_ref[1, :, 0:kvw] = (kv_hold[...] + kv_prev[...]).astype(BF16)
        dpb_ref[1, :, kvw:D] = jnp.zeros((BLK, D - kvw), BF16)

        @pl.when(n < nb)
        def _():
            dq_hold[...] = dq_new[...]
            kv_hold[...] = kv_cur[...]

    def cur(n):
        return jnp.minimum(n, nb - 1)

    call = _pcall(body, name=name, grid=(nb + 1,), carry=carry,
                  in_specs=[pl.BlockSpec((None, BLK, D), lambda n: (0, cur(n), 0)),
                            pl.BlockSpec((None, BLK, kvw), lambda n: (1, cur(n), 0)),
                            pl.BlockSpec((None, BLK, kvw), lambda n: (1, jnp.maximum(cur(n) - 1, 0), 0)),
                            pl.BlockSpec((BLK, D), lambda n: (cur(n), 0)),
                            pl.BlockSpec((BLK, D), lambda n: (cur(n), 0)),
                            pl.BlockSpec((BLK, 128), lambda n: (cur(n), 0)),
                            pl.BlockSpec((N_HEADS, BLK, BLK), lambda n: (0, 0, 0)),
                            pl.BlockSpec((1, 128), lambda n: (0, 0))],
                  out_specs=[pl.BlockSpec((2, BLK, D), lambda n: (0, jnp.maximum(n - 1, 0), 0)),
                             pl.BlockSpec((N_HEADS, BLK, BLK), lambda n: (0, 0, 0)),
                             pl.BlockSpec((1, 128), lambda n: (0, 0))],
                  out_shape=[_sds((2, s, D), BF16), _sds((N_HEADS, BLK, BLK), F32), _sds((1, 128), F32)],
                  scratch=[pltpu.VMEM((BLK, D), F32), pltpu.VMEM((BLK, kvw), F32), pltpu.VMEM((BLK, D), F32),
                           pltpu.VMEM((BLK, kvw), F32), pltpu.VMEM((BLK, kvw), F32)])
    sink_row = jnp.pad(sinks, ((0, 0), (0, 128 - N_HEADS)))
    return _carried(call, (pb, pb, pb, attn, dattn, lse, bias, sink_row), carry)


HALO = 16
CW = 512


def _conv_taps(cu, halo_cu, first_tile):
    row = lax.broadcasted_iota(jnp.int32, cu.shape, 0)
    halo_cu = jnp.where(first_tile, 0.0, halo_cu)
    c1 = jnp.where(row == 0, halo_cu[HALO - 1:HALO], pltpu.roll(cu, 1, 0))
    c2 = jnp.where(row == 0, halo_cu[HALO - 2:HALO - 1],
                   jnp.where(row == 1, halo_cu[HALO - 1:HALO], pltpu.roll(cu, 2, 0)))
    return c1, c2


def _conv_merge_fwd(pa, attn, convw, name, ts=512):
    _, s, _ = pa.shape
    ts = min(ts, s)
    hb = ts // HALO

    def body(pa_ref, hp_ref, at_ref, w_ref, o_ref):
        i = pl.program_id(1)
        cu = pa_ref[0].astype(F32) * pa_ref[2].astype(F32)
        c1, c2 = _conv_taps(cu, hp_ref[0].astype(F32) * hp_ref[2].astype(F32), i == 0)
        w = w_ref[...]
        c3 = w[0:1] * c2 + w[1:2] * c1 + w[2:3] * cu
        conv = pa_ref[1].astype(F32) * c3
        o_ref[...] = (jax.nn.sigmoid(pa_ref[3].astype(F32)) * at_ref[...].astype(F32)
                      + jax.nn.sigmoid(pa_ref[4].astype(F32)) * conv).astype(BF16)

    return _pcall(body, name=name, grid=(D // CW, s // ts),
                  in_specs=[pl.BlockSpec((5, ts, CW), lambda c, i: (0, i, c)),
                            pl.BlockSpec((5, HALO, CW), lambda c, i: (0, jnp.maximum(i * hb - 1, 0), c)),
                            pl.BlockSpec((ts, CW), lambda c, i: (i, c)),
                            pl.BlockSpec((8, CW), lambda c, i: (0, c))],
                  out_specs=pl.BlockSpec((ts, CW), lambda c, i: (i, c)),
                  out_shape=_sds((s, D), BF16))(pa, pa, attn, convw)


def _conv_merge_bwd(dmerged, pa, attn, convw, name, ts=512, carry=None):
    _, s, _ = pa.shape
    ts = min(ts, s)
    hb = ts // HALO
    last_hb = s // HALO - 1

    def body(dm_ref, pa_ref, at_ref, w_ref, hp_ref, hn_ref, dmn_ref, dat_ref, dpa_ref, dw_ref):
        i = pl.program_id(1)
        last = i == pl.num_programs(1) - 1
        dm = dm_ref[...].astype(F32)
        cp, bp, u = pa_ref[0].astype(F32), pa_ref[1].astype(F32), pa_ref[2].astype(F32)
        sa = jax.nn.sigmoid(pa_ref[3].astype(F32))
        sc = jax.nn.sigmoid(pa_ref[4].astype(F32))
        at = at_ref[...].astype(F32)
        cu = cp * u
        c1, c2 = _conv_taps(cu, hp_ref[0].astype(F32) * hp_ref[2].astype(F32), i == 0)
        w = w_ref[...]
        c3 = w[0:1] * c2 + w[1:2] * c1 + w[2:3] * cu
        dconv = dm * sc
        dc3 = dconv * bp
        nxt = dmn_ref[...].astype(F32) * jax.nn.sigmoid(hn_ref[4].astype(F32)) * hn_ref[1].astype(F32)
        nxt = jnp.where(last, 0.0, nxt)
        row = lax.broadcasted_iota(jnp.int32, dc3.shape, 0)
        d1 = jnp.where(row == ts - 1, nxt[0:1], pltpu.roll(dc3, ts - 1, 0))
        d2 = jnp.where(row == ts - 2, nxt[0:1], jnp.where(row == ts - 1, nxt[1:2], pltpu.roll(dc3, ts - 2, 0)))
        dcu = w[2:3] * dc3 + w[1:2] * d1 + w[0:1] * d2
        dat_ref[...] = (dm * sa).astype(BF16)
        dpa_ref[0] = (dcu * u).astype(BF16)
        dpa_ref[1] = (dconv * c3).astype(BF16)
        dpa_ref[2] = (dcu * cp).astype(BF16)
        dpa_ref[3] = (dm * at * sa * (1.0 - sa)).astype(BF16)
        dpa_ref[4] = (dm * bp * c3 * sc * (1.0 - sc)).astype(BF16)

        @pl.when(i == 0)
        def _():
            dw_ref[...] = jnp.zeros_like(dw_ref)

        dw_ref[0:1, :] += jnp.sum(dc3 * c2, axis=0, keepdims=True)
        dw_ref[1:2, :] += jnp.sum(dc3 * c1, axis=0, keepdims=True)
        dw_ref[2:3, :] += jnp.sum(dc3 * cu, axis=0, keepdims=True)

    call = _pcall(body, name=name, grid=(D // CW, s // ts), carry=carry,
                  in_specs=[pl.BlockSpec((ts, CW), lambda c, i: (i, c)),
                            pl.BlockSpec((5, ts, CW), lambda c, i: (0, i, c)),
                            pl.BlockSpec((ts, CW), lambda c, i: (i, c)),
                            pl.BlockSpec((8, CW), lambda c, i: (0, c)),
                            pl.BlockSpec((5, HALO, CW), lambda c, i: (0, jnp.maximum(i * hb - 1, 0), c)),
                            pl.BlockSpec((5, HALO, CW), lambda c, i: (0, jnp.minimum((i + 1) * hb, last_hb), c)),
                            pl.BlockSpec((HALO, CW), lambda c, i: (jnp.minimum((i + 1) * hb, last_hb), c))],
                  out_specs=[pl.BlockSpec((ts, CW), lambda c, i: (i, c)),
                             pl.BlockSpec((5, ts, CW), lambda c, i: (0, i, c)),
                             pl.BlockSpec((8, CW), lambda c, i: (0, c))],
                  out_shape=[_sds((s, D), BF16), _sds((5, s, D), BF16), _sds((8, D), F32)])
    return _carried(call, (dmerged, pa, attn, convw, pa, pa, dmerged), carry)


def _xattn_fwd(q, kv, name, tq=512):
    s, _ = q.shape
    nm = kv.shape[1]
    tq = min(tq, s)

    def body(q_ref, kv_ref, o_ref, lse_ref):
        lane = lax.broadcasted_iota(jnp.int32, (tq, 128), 1)
        lse_t = jnp.zeros((tq, 128), F32)
        for h in range(XH):
            hs = slice(XHD * h, XHD * (h + 1))
            sc = _dot_nt(q_ref[:, hs], kv_ref[h]) * (XHD ** -0.5)
            m = jnp.max(sc, axis=1, keepdims=True)
            p = jnp.exp(sc - m)
            l = jnp.sum(p, axis=1, keepdims=True)
            o_ref[:, hs] = (_dot(p.astype(BF16), kv_ref[XH + h]) * (1.0 / l)).astype(BF16)
            lse_t = jnp.where(lane == h, m + jnp.log(l), lse_t)
        lse_ref[...] = lse_t

    return _pcall(body, name=name, grid=(s // tq,),
                  in_specs=[pl.BlockSpec((tq, D), lambda i: (i, 0)), pl.BlockSpec((2 * XH, nm, XHD), lambda i: (0, 0, 0))],
                  out_specs=[pl.BlockSpec((tq, D), lambda i: (i, 0)), pl.BlockSpec((tq, 128), lambda i: (i, 0))],
                  out_shape=[_sds((s, D), BF16), _sds((s, 128), F32)])(q, kv)


def _xattn_bwd(q, kv, o, do, lse, name, tq=512, carry=None):
    s, _ = q.shape
    nm = kv.shape[1]
    tq = min(tq, s)

    def body(q_ref, kv_ref, o_ref, do_ref, lse_ref, dq_ref, dkv_ref):
        @pl.when(pl.program_id(0) == 0)
        def _():
            dkv_ref[...] = jnp.zeros_like(dkv_ref)

        for h in range(XH):
            hs = slice(XHD * h, XHD * (h + 1))
            qh, kh, vh, dob = q_ref[:, hs], kv_ref[h], kv_ref[XH + h], do_ref[:, hs]
            p = jnp.exp(_dot_nt(qh, kh) * (XHD ** -0.5) - lse_ref[:, h:h + 1])
            dp = _dot_nt(dob, vh)
            delta = jnp.sum(dob.astype(F32) * o_ref[:, hs].astype(F32), axis=1, keepdims=True)
            dsb = (p * (dp - delta) * (XHD ** -0.5)).astype(BF16)
            dq_ref[:, hs] = _dot(dsb, kh).astype(BF16)
            dkv_ref[h] += _dot_tn(dsb, qh)
            dkv_ref[XH + h] += _dot_tn(p.astype(BF16), dob)

    call = _pcall(body, name=name, grid=(s // tq,), carry=carry,
                  in_specs=[pl.BlockSpec((tq, D), lambda i: (i, 0)), pl.BlockSpec((2 * XH, nm, XHD), lambda i: (0, 0, 0)),
                            pl.BlockSpec((tq, D), lambda i: (i, 0)), pl.BlockSpec((tq, D), lambda i: (i, 0)),
                            pl.BlockSpec((tq, 128), lambda i: (i, 0))],
                  out_specs=[pl.BlockSpec((tq, D), lambda i: (i, 0)), pl.BlockSpec((2 * XH, nm, XHD), lambda i: (0, 0, 0))],
                  out_shape=[_sds((s, D), BF16), _sds((2 * XH, nm, XHD), F32)])
    return _carried(call, (q, kv, o, do, lse), carry)


def _ffn_down_bwd(dxb, wd4, gu4, name, tm=1024, carry=None):
    s, _ = dxb.shape
    tm = min(tm, s)

    def body(dx_ref, w_ref, gu_ref, o_ref):
        da = 0.5 * _dot_nt(dx_ref[...], w_ref[...])
        g = gu_ref[0].astype(F32)
        u = gu_ref[1].astype(F32)
        sg = jax.nn.sigmoid(g)
        o_ref[0] = (da * u * sg * (1.0 + g * (1.0 - sg))).astype(BF16)
        o_ref[1] = (da * g * sg).astype(BF16)

    call = _pcall(body, name=name, grid=(4, s // tm), carry=carry,
                  in_specs=[pl.BlockSpec((tm, D), lambda p, i: (i, 0)),
                            pl.BlockSpec((None, FS, D), lambda p, i: (p, 0, 0)),
                            pl.BlockSpec((2, None, tm, FS), lambda p, i: (0, p, i, 0))],
                  out_specs=pl.BlockSpec((2, None, tm, FS), lambda p, i: (0, p, i, 0)),
                  out_shape=_sds((2, 4, s, FS), BF16))
    return _carried(call, (dxb, wd4, gu4), carry)


def _mm_tn(a, b, name, scale=1.0, carry=None):
    pa_n, s, m = a.shape
    pb_n, _, n = b.shape
    po = max(pa_n, pb_n)
    tn = n if po >= 4 else min(n, 256)

    def body(a_ref, b_ref, o_ref):
        o_ref[...] = (scale * _dot_tn(a_ref[...], b_ref[...])).astype(BF16)

    call = _pcall(body, name=name, grid=(po, n // tn), carry=carry,
                  in_specs=[pl.BlockSpec((None, s, m), lambda o, j: (o if pa_n > 1 else 0, 0, 0)),
                            pl.BlockSpec((None, s, tn), lambda o, j: (o if pb_n > 1 else 0, 0, j))],
                  out_specs=pl.BlockSpec((None, m, tn), lambda o, j: (o, 0, j)),
                  out_shape=_sds((po, m, n), BF16))
    return _carried(call, (a, b), carry)


def _sum_dots(a_ref, b_ref, nj, bt):
    dot = _dot_nt if bt else _dot
    acc = dot(a_ref[0], b_ref[0])
    for j in range(1, nj):
        acc = acc + dot(a_ref[j], b_ref[j])
    return acc


def _mm_acc(a, b, name, out_dtype, tm=512, bt=False, carry=None):
    nj, s, k = a.shape
    n = b.shape[1] if bt else b.shape[2]
    tm = min(tm, s)

    def body(a_ref, b_ref, o_ref):
        o_ref[...] = _sum_dots(a_ref, b_ref, nj, bt).astype(out_dtype)

    call = _pcall(body, name=name, grid=(s // tm,), carry=carry,
                  in_specs=[pl.BlockSpec((nj, tm, k), lambda i: (0, i, 0)),
                            pl.BlockSpec(b.shape, lambda i: (0, 0, 0))],
                  out_specs=pl.BlockSpec((tm, n), lambda i: (i, 0)), out_shape=_sds((s, n), out_dtype))
    return _carried(call, (a, b), carry)


def _mm_acc_rms_bwd(a, b, name, *, x, gain, dres, addend=None, tm=512, bt=False, carry=None):
    nj, s, k = a.shape
    n = b.shape[1] if bt else b.shape[2]
    tm = min(tm, s)
    has_add = addend is not None

    def body(*refs):
        a_ref, b_hbm, x_ref, g_ref, r_ref = refs[:5]
        add_ref = refs[5] if has_add else None
        dx_ref, dxb_ref, dg_ref, b_ref, b_sem = refs[5 + has_add:]

        @pl.when(pl.program_id(0) == 0)
        def _():
            load = pltpu.make_async_copy(b_hbm, b_ref, b_sem)
            load.start()
            dg_ref[...] = jnp.zeros_like(dg_ref)
            load.wait()

        dh = _sum_dots(a_ref, b_ref, nj, bt)
        if has_add:
            dh = dh + add_ref[...]
        xv = x_ref[...]
        r = lax.rsqrt(jnp.mean(xv * xv, axis=-1, keepdims=True) + EPS)
        xh = xv * r
        dyg = dh * g_ref[...]
        dx = r_ref[...] + r * (dyg - xh * jnp.mean(dyg * xh, axis=-1, keepdims=True))
        dx_ref[...] = dx
        dxb_ref[...] = dx.astype(BF16)
        dg_ref[...] += jnp.sum(dh * xh, axis=0, keepdims=True)

    row = pl.BlockSpec((tm, n), lambda i: (i, 0))
    in_specs = [pl.BlockSpec((nj, tm, k), lambda i: (0, i, 0)), HBM_SPEC,
                row, pl.BlockSpec((1, n), lambda i: (0, 0)), row] + ([row] if has_add else [])
    args = (a, b, x, gain, dres) + ((addend,) if has_add else ())
    call = _pcall(body, name=name, grid=(s // tm,), in_specs=in_specs, carry=carry,
                  out_specs=[row, row, pl.BlockSpec((1, n), lambda i: (0, 0))],
                  out_shape=[_sds((s, n), F32), _sds((s, n), BF16), _sds((1, n), F32)],
                  scratch=[pltpu.VMEM(b.shape, b.dtype), pltpu.SemaphoreType.DMA(())])
    return _carried(call, args, carry)


def _adam(w, g, m, v):
    m2 = ADAM_B1 * m + (1.0 - ADAM_B1) * g
    v2 = ADAM_B2 * v + (1.0 - ADAM_B2) * (g * g)
    m_hat = m2 / (1.0 - ADAM_B1 ** ADAM_STEP)
    v_hat = v2 / (1.0 - ADAM_B2 ** ADAM_STEP)
    delta = -ADAM_LR * (m_hat / (jnp.sqrt(v_hat) + ADAM_EPS) + ADAM_WD * w)
    return delta, m2, v2


def _adamw(parts, w, m, v, name):
    _, r, c = parts.shape
    tr = max(t for t in range(16, 257, 16) if r % t == 0)

    def body(p_ref, w_ref, m_ref, v_ref, g_ref, d_ref, m2_ref, v2_ref):
        g = p_ref[0].astype(F32)
        for i in range(1, N_DEV):
            g = g + p_ref[i].astype(F32)
        delta, m2, v2 = _adam(w_ref[...], g, m_ref[...], v_ref[...])
        g_ref[...] = g
        d_ref[...] = delta
        m2_ref[...] = m2
        v2_ref[...] = v2

    blk = pl.BlockSpec((tr, c), lambda i: (i, 0))
    return _pcall(body, name=name, grid=(r // tr,),
                  in_specs=[pl.BlockSpec((N_DEV, tr, c), lambda i: (0, i, 0)), blk, blk, blk],
                  out_specs=[blk] * 4, out_shape=[_sds((r, c), F32)] * 4)(parts, w, m, v)


def _position():
    return lax.axis_index("x"), lax.axis_index("y"), lax.axis_index("c")


def _slot(px, py, pc):
    return 4 * px + 2 * py + pc


def _row_window(ref, rows):
    r0, r1 = rows
    return ref if (r0, r1) == (0, ref.shape[0]) else ref.at[pl.ds(r0, r1 - r0)]


def _split_items(items):
    sources = [src for src, _, _ in items]
    begun = [(a, dest) for a, (_, _, dest) in enumerate(items) if dest is not None]
    aliases = {len(sources) + k: a for k, (a, _) in enumerate(begun)}
    return sources + [dest for _, dest in begun], [rows for _, rows, _ in items], aliases


def _gather_carry(items):
    na = len(items)
    carry_ins, windows, aliases = _split_items(items)

    def plan(ins, outs, sems):
        send_sems, recv_sems, local_sems = sems
        x, y, c = _position()
        me, sibling = (x, y, c), (x, y, 1 - c)
        chips = [(1 - x, y), (x, 1 - y), (1 - x, 1 - y)]
        ins = [_row_window(ins[a], windows[a]) for a in range(na)]

        def block_rows(a, block):
            return _row_window(outs[a].at[_slot(*block)], windows[a])

        def copy(a, k, block, to, src=None):
            rows = block_rows(a, block)
            return pltpu.make_async_remote_copy(src_ref=rows if src is None else src, dst_ref=rows,
                                                send_sem=send_sems.at[k, a], recv_sem=recv_sems.at[k, a],
                                                device_id=to, device_id_type=MESH)

        mine = [pltpu.make_async_copy(ins[a], block_rows(a, me), local_sems.at[a]) for a in range(na)]
        first = [copy(a, 0, me, sibling, src=ins[a]) for a in range(na)]
        for j, chip in enumerate(chips):
            first += [copy(a, 1 + j, me, (*chip, c), src=ins[a]) for a in range(na)]
        landed = [[copy(a, 1 + j, (*chip, c), me) for a in range(na)] for j, chip in enumerate(chips)]
        passed = [[copy(a, 4 + j, (*chip, c), sibling) for a in range(na)] for j, chip in enumerate(chips)]
        from_sibling = [copy(a, 0, sibling, me) for a in range(na)]
        for j, chip in enumerate(chips):
            from_sibling += [copy(a, 4 + j, (*chip, 1 - c), me) for a in range(na)]
        return mine, first, landed, passed, from_sibling

    def start(ins, outs, sems):
        mine, first, _, _, _ = plan(ins, outs, sems)
        for cp in mine + first:
            cp.start()

    def mid(ins, outs, sems):
        _, _, landed, passed, _ = plan(ins, outs, sems)
        for over_ici, onward in zip(landed, passed):
            for cp, fwd in zip(over_ici, onward):
                cp.wait_recv()
                fwd.start()

    def finish(ins, outs, sems):
        mine, first, _, passed, from_sibling = plan(ins, outs, sems)
        for cp in from_sibling:
            cp.wait_recv()
        for cp in first + [fwd for onward in passed for fwd in onward]:
            cp.wait_send()
        for cp in mine:
            cp.wait()

    return _Carry(carry_ins, [_sds((N_DEV,) + src.shape, src.dtype) for src, _, _ in items],
                  [pltpu.SemaphoreType.DMA((7, na)), pltpu.SemaphoreType.DMA((7, na)),
                   pltpu.SemaphoreType.DMA((na,))], start, finish, mid, aliases)


def _exchange_carry(scattered, replicated=()):
    items = list(scattered) + [(a, (0, a.shape[0]), None) for a in replicated]
    na, ns = len(items), len(scattered)
    carry_ins, windows, aliases = _split_items(items)

    def plan(ins, outs, sems):
        send_sems, recv_sems, local_sems = sems
        me = _slot(*_position())

        def source(a, j):
            return _row_window(ins[a].at[j] if a < ns else ins[a], windows[a])

        def copy(a, j, i):
            return pltpu.make_async_remote_copy(src_ref=source(a, j), dst_ref=_row_window(outs[a].at[i], windows[a]),
                                                send_sem=send_sems.at[j, a], recv_sem=recv_sems.at[i, a],
                                                device_id=(j >> 2, (j >> 1) & 1, j & 1), device_id_type=MESH)

        def own(a, j):
            return pltpu.make_async_copy(source(a, j), _row_window(outs[a].at[j], windows[a]), local_sems.at[a])

        return me, copy, own

    def start(ins, outs, sems):
        me, copy, own = plan(ins, outs, sems)
        for a in range(na):
            for j in range(N_DEV):
                @pl.when(me == j)
                def _():
                    own(a, j).start()

                @pl.when(me != j)
                def _():
                    copy(a, j, me).start()

    def finish(ins, outs, sems):
        me, copy, own = plan(ins, outs, sems)
        for a in range(na):
            for j in range(N_DEV):
                @pl.when(me == j)
                def _():
                    for i in range(N_DEV):
                        if i != j:
                            copy(a, j, i).wait_recv()
                    own(a, j).wait()

                @pl.when(me != j)
                def _():
                    copy(a, j, me).wait_send()

    return _Carry(carry_ins, [_sds((N_DEV,) + src.shape[-2:], src.dtype) for src, _, _ in items],
                  [pltpu.SemaphoreType.DMA((N_DEV, na)), pltpu.SemaphoreType.DMA((N_DEV, na)),
                   pltpu.SemaphoreType.DMA((na,))], start, finish, None, aliases)


NQ, NKV = N_HEADS * HEAD, 2 * N_KV * HEAD


class _Mesh:
    def __init__(self, shards):
        self.shards, self.full, self.received, self.cache = shards, {}, {}, {}

    def fetch(self, wanted):
        items = []
        for want in wanted:
            name, r0, r1 = want if isinstance(want, tuple) else (want, 0, self.shards[want].shape[0])
            items.append((self.shards[name], (r0, r1), self.full.get(name)))
        return _gather_carry(items)

    def fetched(self, wanted, results):
        self.full.update(zip([want[0] if isinstance(want, tuple) else want for want in wanted], results))

    def send(self, *payloads):
        return _exchange_carry([(parts, rows or (0, parts.shape[1]), self.received.get(name))
                                for name, parts, rows in payloads])

    def sent(self, names, results):
        self.received.update(zip(names, results))

    def w(self, key):
        if key not in self.cache:
            self.cache[key] = self._layout(key)
        return self.cache[key]

    def _layout(self, key):
        if key in ("gu1", "gu2"):
            return self.full[key]
        if key in ("d1", "d2"):
            return self.full[key].reshape(4, FS, D)
        if key in ("out", "q", "o"):
            return self.full[key].reshape(D, D)
        if key == "kv":
            return self.full["kv"]
        if key == "convw":
            rows = self.full["conv"][:, :3, :].transpose(1, 0, 2).reshape(3, D)
            return jnp.concatenate([rows, jnp.zeros((5, D), F32)], axis=0)
        w_in_t = self.full["win"].reshape(-1, D)
        if key == "wa":
            return w_in_t[NQ + NKV:].reshape(5, D, D)
        assert key == "wb", key
        return jnp.stack([w_in_t[:NQ], jnp.pad(w_in_t[NQ:NQ + NKV], ((0, D - NKV), (0, 0)))])


def _w_in_parts(dw_a, dw_b):
    return jnp.concatenate([dw_b[0], dw_b[1][:NKV], dw_a.reshape(5 * D, D)], axis=0).reshape(N_DEV, -1, D)


def _forward_backward(x, mem, target, g, rel_bias, sinks, ex):
    s = x.shape[0]
    def fetching(wanted, call, *args, **kw):
        res, got = call(*args, carry=ex.fetch(wanted), **kw)
        ex.fetched(wanted, got)
        return res

    h1 = fetching(["gu1", "conv"], _rmsnorm, x, g["ffn1"], "norm_ffn1")
    gu1, a1 = fetching(["d1", ("win", 0, 400)], _ffn_up, h1, ex.w("gu1").reshape(2, 4, FS, D), "ffn1_up")
    x1, h2 = fetching([("win", 400, 832)], _mm_res_norm, a1, ex.w("d1"), x, g["mix"], 0.5, "ffn1_down")
    pa = fetching(["gu2"], _mm_nn, h2, ex.w("wa"), "in_proj_a", bt=True)
    pb = fetching(["out", "q"], _mm_nn, h2, ex.w("wb"), "in_proj_b", bt=True)
    biasm = _bias_build(rel_bias, "bias_build")
    attn, lse = fetching(["kv", "d2", "o"], _swa_fwd, pb, biasm, sinks, "swa_fwd")
    merged = _conv_merge_fwd(pa, attn, ex.w("convw"), "conv_merge_fwd")
    (x2, h3), _ = _mm_res_norm(merged[None], ex.w("out")[None], x1, g["xattn"], 1.0, "out_proj")
    q2 = _mm_nn(h3, ex.w("q")[None], "xattn_q")[0][0]
    mh, _ = _rmsnorm(mem, g["mem"], "norm_mem")
    kv2 = _mm_nn(mh, ex.w("kv"), "xattn_kv")[0]
    o, lse2 = _xattn_fwd(q2, kv2, "xattn_fwd")
    (x3, h4), _ = _mm_res_norm(o[None], ex.w("o")[None], x2, g["ffn2"], 1.0, "xattn_o")
    (gu2, a2), _ = _ffn_up(h4, ex.w("gu2").reshape(2, 4, FS, D), "ffn2_up")
    dx4, dx4b, loss, d_final = _ffn_down_loss(a2, ex.w("d2"), x3, g["final"], target, "ffn2_down_loss")
    def sending(payloads, call, *args, **kw):
        res, got = call(*args, carry=ex.send(*payloads), **kw)
        ex.sent([name for name, _, _ in payloads], got)
        return res

    dw_d2 = _mm_tn(a2, dx4b[None], "dw_ffn2_down", scale=0.5)[0].reshape(N_DEV, -1, D)
    dgu2 = sending([("d2", dw_d2, None)], _ffn_down_bwd, dx4b, ex.w("d2"), gu2, "ffn2_down_bwd").reshape(8, s, FS)
    dw_gu2 = _mm_tn(dgu2, h4[None], "dw_ffn2_up")[0]
    dx3, dx3b, d_ffn2 = sending([("gu2", dw_gu2, (0, 400))], _mm_acc_rms_bwd, dgu2, ex.w("gu2"), "ffn2_up_bwd",
                                x=x3, gain=g["ffn2"], dres=dx4)
    do, _ = _mm_acc(dx3b[None], ex.w("o")[None], "xattn_o_bwd", BF16, bt=True)
    dw_o = _mm_tn(o[None], dx3b[None], "dw_xattn_o")[0].reshape(N_DEV, -1, D)
    dq2, dkv2 = sending([("o", dw_o, None)], _xattn_bwd, q2, kv2, o, do, lse2, "xattn_bwd")
    dkv2b = dkv2.astype(BF16)
    dw_q = _mm_tn(h3[None], dq2[None], "dw_xattn_q")[0].reshape(N_DEV, -1, D)
    dx2, dx2b, d_xattn = sending([("q", dw_q, None)], _mm_acc_rms_bwd, dq2[None], ex.w("q")[None], "xattn_q_bwd",
                                 x=x2, gain=g["xattn"], dres=dx3, bt=True)
    dw_kv = _mm_tn(mh[None], dkv2b, "dw_xattn_kv")[0]
    (_, _, d_mem), _ = _mm_acc_rms_bwd(dkv2b, ex.w("kv"), "xattn_kv_bwd", x=mem, gain=g["mem"],
                                       dres=jnp.zeros_like(mem), bt=True)
    dmerged, _ = _mm_acc(dx2b[None], ex.w("out")[None], "out_proj_bwd", BF16, bt=True)
    dw_out = _mm_tn(merged[None], dx2b[None], "dw_out_proj")[0].reshape(N_DEV, -1, D)
    dattn, dpa, d_convw = sending([("kv", dw_kv, None)], _conv_merge_bwd,
                                  dmerged, pa, attn, ex.w("convw"), "conv_merge_bwd")
    dpb, dbias, d_sinks = sending([("gu2", dw_gu2, (400, FS)), ("out", dw_out, None)], _swa_bwd,
                                  pb, attn, dattn, lse, biasm, sinks, "swa_bwd")
    d_relb = _bias_bwd(dbias, "bias_bwd")
    dw_in = _w_in_parts(_mm_tn(dpa, h2[None], "dw_in_proj_a")[0], _mm_tn(dpb, h2[None], "dw_in_proj_b")[0])
    dh2_b = sending([("win", dw_in, (0, 208))], _mm_acc, dpb, ex.w("wb"), "in_proj_b_bwd", F32)
    dx1, dx1b, d_mix = sending([("win", dw_in, (208, 672))], _mm_acc_rms_bwd, dpa, ex.w("wa"), "in_proj_a_bwd",
                               x=x1, gain=g["mix"], dres=dx2, addend=dh2_b)
    dw_d1 = sending([("win", dw_in, (672, 832))], _mm_tn, a1, dx1b[None], "dw_ffn1_down", scale=0.5)
    dw_d1 = dw_d1.reshape(N_DEV, -1, D)
    dgu1 = sending([("d1", dw_d1, None)], _ffn_down_bwd, dx1b, ex.w("d1"), gu1, "ffn1_down_bwd").reshape(8, s, FS)
    dw_gu1 = _mm_tn(dgu1, h1[None], "dw_ffn1_up")[0]
    dx0, _, d_ffn1 = sending([("gu1", dw_gu1, None)], _mm_acc_rms_bwd, dgu1, ex.w("gu1"), "ffn1_up_bwd",
                             x=x, gain=g["ffn1"], dres=dx1)

    relb_row = jnp.concatenate([d_relb[:, :REL_BUCKETS].T.reshape(1, REL_BUCKETS * N_HEADS), d_sinks[:, :N_HEADS],
                                jnp.zeros((1, D - REL_BUCKETS * N_HEADS - N_HEADS), F32)], axis=1)
    loss_row = jnp.concatenate([loss[0:1, 0:1], jnp.zeros((1, D - 1), F32)], axis=1)
    small = jnp.concatenate([d_ffn1, d_mix, d_xattn, d_mem, d_ffn2, d_final, relb_row, loss_row, d_convw[0:3],
                             jnp.zeros((SMALL_ROWS - ROW_CONV - 3, D), F32)], axis=0)
    return dx0, small


def _pack_small(norms, final, relb, sinks, conv_local, me):
    relb_row = jnp.concatenate([relb.reshape(1, -1), sinks.reshape(1, -1),
                                jnp.zeros((1, D - REL_BUCKETS * N_HEADS - N_HEADS), F32)], axis=1)
    conv_rows = lax.dynamic_update_slice(jnp.zeros((3, D), F32), conv_local.reshape(3, -1), (0, 128 * me))
    return jnp.concatenate(list(norms) + [final.reshape(1, D), relb_row, jnp.zeros((1, D), F32), conv_rows,
                                          jnp.zeros((SMALL_ROWS - ROW_CONV - 3, D), F32)], axis=0)


def kernel(x, mem, positions, rel_bias, ffn1_norm, ffn1_w_gu, ffn1_w_down, mix_norm, w_in, sinks, conv_w, w_out, xattn_norm, mem_norm, xattn_wq, xattn_wkv, xattn_wo, ffn2_norm, ffn2_w_gu, ffn2_w_down, final_norm, loss_target, m_rel_bias, m_ffn1_norm, m_ffn1_w_gu, m_ffn1_w_down, m_mix_norm, m_w_in, m_sinks, m_conv_w, m_w_out, m_xattn_norm, m_mem_norm, m_xattn_wq, m_xattn_wkv, m_xattn_wo, m_ffn2_norm, m_ffn2_w_gu, m_ffn2_w_down, m_final_norm, v_rel_bias, v_ffn1_norm, v_ffn1_w_gu, v_ffn1_w_down, v_mix_norm, v_w_in, v_sinks, v_conv_w, v_w_out, v_xattn_norm, v_mem_norm, v_xattn_wq, v_xattn_wkv, v_xattn_wo, v_ffn2_norm, v_ffn2_w_gu, v_ffn2_w_down, v_final_norm):
    del positions
    me = _slot(*_position())
    big = dict(gu1=(ffn1_w_gu, m_ffn1_w_gu, v_ffn1_w_gu), d1=(ffn1_w_down, m_ffn1_w_down, v_ffn1_w_down),
               win=(w_in, m_w_in, v_w_in), out=(w_out, m_w_out, v_w_out), q=(xattn_wq, m_xattn_wq, v_xattn_wq),
               kv=(xattn_wkv, m_xattn_wkv, v_xattn_wkv), o=(xattn_wo, m_xattn_wo, v_xattn_wo),
               gu2=(ffn2_w_gu, m_ffn2_w_gu, v_ffn2_w_gu), d2=(ffn2_w_down, m_ffn2_w_down, v_ffn2_w_down))
    order = list(big)
    transposed = ("gu1", "gu2", "win")
    local = {k: tuple(t[0].T if k in transposed else t[0] for t in big[k]) for k in order}
    shards = {k: local[k][0].astype(BF16) for k in order}
    shards["conv"] = jnp.concatenate([conv_w[0], jnp.zeros((5, 128), F32)], axis=0)
    ex = _Mesh(shards)
    gains = dict(ffn1=ffn1_norm, mix=mix_norm, xattn=xattn_norm, mem=mem_norm, ffn2=ffn2_norm,
                 final=final_norm.reshape(1, D))
    dx, small = _forward_backward(x[0], mem[0], loss_target[0], gains, rel_bias, sinks, ex)
    small_parts = _run_alone(_exchange_carry([], [small]), "exchange_small")[0]
    big_out = {k: _adamw(ex.received[k], *local[k], "adamw_" + k) for k in order}
    big_out = {k: [t.T if k in transposed else t for t in big_out[k]] for k in order}
    packed = [_pack_small(norms, final, relb, sk, conv, me) for norms, final, relb, sk, conv in (
        ((ffn1_norm, mix_norm, xattn_norm, mem_norm, ffn2_norm), final_norm, rel_bias, sinks, conv_w),
        ((m_ffn1_norm, m_mix_norm, m_xattn_norm, m_mem_norm, m_ffn2_norm), m_final_norm, m_rel_bias, m_sinks, m_conv_w),
        ((v_ffn1_norm, v_mix_norm, v_xattn_norm, v_mem_norm, v_ffn2_norm), v_final_norm, v_rel_bias, v_sinks, v_conv_w))]
    small_out = _adamw(small_parts, *packed, "adamw_small")

    def unpack(t):
        conv = lax.dynamic_slice(t[ROW_CONV:ROW_CONV + 3], (0, 128 * me), (3, 128))[None]
        nrel = REL_BUCKETS * N_HEADS
        return dict(ffn1_norm=t[0:1], mix_norm=t[1:2], xattn_norm=t[2:3], mem_norm=t[3:4], ffn2_norm=t[4:5],
                    final_norm=t[5], rel_bias=t[ROW_RELB, :nrel].reshape(REL_BUCKETS, N_HEADS),
                    sinks=t[ROW_RELB:ROW_RELB + 1, nrel:nrel + N_HEADS], conv_w=conv)

    names = dict(gu1="ffn1_w_gu", d1="ffn1_w_down", win="w_in", out="w_out", q="xattn_wq", kv="xattn_wkv",
                 o="xattn_wo", gu2="ffn2_w_gu", d2="ffn2_w_down")
    results = []
    for idx in range(4):
        leaves = unpack(small_out[idx])
        leaves.update({names[k]: big_out[k][idx][None] for k in order})
        results.append(leaves)
    weights = ("rel_bias", "ffn1_norm", "ffn1_w_gu", "ffn1_w_down", "mix_norm", "w_in", "sinks", "conv_w", "w_out",
               "xattn_norm", "mem_norm", "xattn_wq", "xattn_wkv", "xattn_wo", "ffn2_norm", "ffn2_w_gu", "ffn2_w_down",
               "final_norm")
    loss = small_out[0][ROW_LOSS, 0]
    return (loss, dx[None], *[leaves[n] for leaves in results for n in weights])
```

```python
import math

import numpy as np
import jax
import jax.numpy as jnp
from jax import lax
from jax.experimental import pallas as pl
from jax.experimental.pallas import tpu as pltpu

F32, BF16 = jnp.float32, jnp.bfloat16
MESH = pl.DeviceIdType.MESH

D = 1024
N_DEV = 8
D_FF = 2816
FS = D_FF // 4
HEAD = 64
N_HEADS, N_KV = 16, 4
BLK = 128
XH, XHD = 4, 256
REL_BUCKETS, REL_EXACT, REL_MAX_DIST = 32, 16, 128
EPS, NEG = 1e-6, -1e30
ADAM_LR, ADAM_B1, ADAM_B2, ADAM_EPS, ADAM_WD, ADAM_STEP = 0.001, 0.9, 0.999, 1e-08, 0.01, 10
VMEM_LIMIT_V7X = 56 * 2**20
SMALL_ROWS = 16
ROW_RELB, ROW_LOSS, ROW_CONV = 6, 7, 8


def _bucket_thresholds():
    n = np.arange(REL_MAX_DIST)
    nf = np.maximum(n, 1).astype(np.float32)
    large = REL_EXACT + (np.log(nf / np.float32(REL_EXACT)) / np.float32(math.log(REL_MAX_DIST / REL_EXACT))
                         * np.float32(REL_BUCKETS - REL_EXACT)).astype(np.int32)
    b = np.where(n < REL_EXACT, n, np.minimum(large, REL_BUCKETS - 1))
    return [int(np.argmax(b >= REL_EXACT + k)) for k in range(1, REL_BUCKETS - REL_EXACT)]


BUCKET_THRESHOLDS = _bucket_thresholds()


HBM_SPEC = pl.BlockSpec(memory_space=pl.ANY)


class _Carry:
    def __init__(self, ins, outs, sems, start, finish, mid=None, aliases=None):
        self.ins, self.outs, self.sems = list(ins), list(outs), list(sems)
        self.start, self.finish, self.mid, self.aliases = start, finish, mid, dict(aliases or {})


def _pcall(body, *, name, grid, in_specs, out_specs, out_shape, scratch=(), carry=None):
    params = pltpu.CompilerParams(dimension_semantics=("arbitrary",) * len(grid), vmem_limit_bytes=VMEM_LIMIT_V7X)
    if carry is None:
        return pl.pallas_call(body, name=name, grid=grid, in_specs=in_specs, out_specs=out_specs,
                              out_shape=out_shape, scratch_shapes=list(scratch), compiler_params=params)
    single = not isinstance(out_shape, (list, tuple))
    own_specs, own_shapes = ([out_specs], [out_shape]) if single else (list(out_specs), list(out_shape))
    n_in, n_out, n_scr = len(in_specs), len(own_shapes), len(scratch)
    n_cin, n_cout = len(carry.ins), len(carry.outs)
    steps = math.prod(grid)
    mid_step = max(steps - 1 - max(steps // 8, 1), 0)

    def carrying(*refs):
        ins, refs = refs[:n_in], refs[n_in:]
        cins, refs = refs[:n_cin], refs[n_cin:]
        outs, refs = refs[:n_out], refs[n_out:]
        couts, refs = refs[:n_cout], refs[n_cout:]
        scr, csems = refs[:n_scr], refs[n_scr:]
        step = 0
        for axis, size in enumerate(grid):
            step = step * size + pl.program_id(axis)

        @pl.when(step == 0)
        def _():
            carry.start(cins, couts, csems)

        body(*ins, *outs, *scr)
        if carry.mid is not None:
            @pl.when(step == mid_step)
            def _():
                carry.mid(cins, couts, csems)

        @pl.when(step == steps - 1)
        def _():
            carry.finish(cins, couts, csems)

    call = pl.pallas_call(carrying, name=name, grid=grid, in_specs=list(in_specs) + [HBM_SPEC] * n_cin,
                          out_specs=own_specs + [HBM_SPEC] * n_cout, out_shape=own_shapes + carry.outs,
                          scratch_shapes=list(scratch) + carry.sems, compiler_params=params,
                          input_output_aliases={n_in + i: n_out + o for i, o in carry.aliases.items()})

    def run(*args):
        res = call(*args, *carry.ins)
        return (res[0] if single else res[:n_out]), res[n_out:]

    return run


def _run_alone(carry, name):
    n_cin, n_cout = len(carry.ins), len(carry.outs)

    def body(*refs):
        cins, couts, csems = refs[:n_cin], refs[n_cin:n_cin + n_cout], refs[n_cin + n_cout:]
        carry.start(cins, couts, csems)
        if carry.mid is not None:
            carry.mid(cins, couts, csems)
        carry.finish(cins, couts, csems)

    return pl.pallas_call(body, name=name, in_specs=[HBM_SPEC] * n_cin, out_specs=[HBM_SPEC] * n_cout,
                          out_shape=carry.outs, scratch_shapes=carry.sems,
                          input_output_aliases=carry.aliases)(*carry.ins)


def _dot(a, b):
    return jnp.dot(a, b, preferred_element_type=F32)


def _dot_nt(a, b):
    return lax.dot_general(a, b, (((1,), (1,)), ((), ())), preferred_element_type=F32)


def _dot_tn(a, b):
    return lax.dot_general(a, b, (((0,), (0,)), ((), ())), preferred_element_type=F32)


def _sds(shape, dtype):
    return jax.ShapeDtypeStruct(tuple(shape), dtype)


ROW_CHUNK = 256


def _row_chunks(tm):
    return [slice(r, min(r + ROW_CHUNK, tm)) for r in range(0, tm, ROW_CHUNK)]


def _carried(call, args, carry):
    return call(*args) if carry is not None else (call(*args), ())


def _rmsnorm(x, g, name, carry=None):
    m, d = x.shape
    tm = min(512, m)

    def body(x_ref, g_ref, h_ref):
        xv = x_ref[...]
        r = lax.rsqrt(jnp.mean(xv * xv, axis=-1, keepdims=True) + EPS)
        h_ref[...] = (xv * r * g_ref[...]).astype(BF16)

    call = _pcall(body, name=name, grid=(m // tm,), carry=carry,
                  in_specs=[pl.BlockSpec((tm, d), lambda i: (i, 0)), pl.BlockSpec((1, d), lambda i: (0, 0))],
                  out_specs=pl.BlockSpec((tm, d), lambda i: (i, 0)), out_shape=_sds((m, d), BF16))
    return _carried(call, (x, g), carry)


def _mm_nn(a, b, name, tm=1024, bt=False, carry=None):
    m, k = a.shape
    nj = b.shape[0]
    n = b.shape[1] if bt else b.shape[2]
    tm = min(tm, m)
    dot = _dot_nt if bt else _dot

    def body(a_ref, b_ref, o_ref):
        o_ref[...] = dot(a_ref[...], b_ref[...]).astype(BF16)

    call = _pcall(body, name=name, grid=(nj, m // tm),
                  in_specs=[pl.BlockSpec((tm, k), lambda j, i: (i, 0)),
                            pl.BlockSpec((None,) + b.shape[1:], lambda j, i: (j, 0, 0))],
                  out_specs=pl.BlockSpec((None, tm, n), lambda j, i: (j, i, 0)),
                  out_shape=_sds((nj, m, n), BF16), carry=carry)
    return _carried(call, (a, b), carry)


def _ffn_up(h, w4, name, tm=1024, carry=None):
    s, d = h.shape
    tm = min(tm, s)

    def body(h_ref, w_ref, gu_ref, a_ref):
        for rows in _row_chunks(tm):
            hv = h_ref[rows, :]
            g = _dot_nt(hv, w_ref[0])
            u = _dot_nt(hv, w_ref[1])
            gu_ref[0, rows, :] = g.astype(BF16)
            gu_ref[1, rows, :] = u.astype(BF16)
            a_ref[rows, :] = (g * jax.nn.sigmoid(g) * u).astype(BF16)

    call = _pcall(body, name=name, grid=(4, s // tm),
                  in_specs=[pl.BlockSpec((tm, d), lambda p, i: (i, 0)),
                            pl.BlockSpec((2, None, FS, d), lambda p, i: (0, p, 0, 0))],
                  out_specs=[pl.BlockSpec((2, None, tm, FS), lambda p, i: (0, p, i, 0)),
                             pl.BlockSpec((None, tm, FS), lambda p, i: (p, i, 0))],
                  out_shape=[_sds((2, 4, s, FS), BF16), _sds((4, s, FS), BF16)], carry=carry)
    return _carried(call, (h, w4), carry)


def _mm_res_norm(a, w, xres, gain, scale, name, tm=512, carry=None):
    npart, s, kp = a.shape
    tm = min(tm, s)

    def body(a_ref, w_ref, x_ref, g_ref, xo_ref, h_ref):
        for rows in _row_chunks(tm):
            acc = _dot(a_ref[0, rows, :], w_ref[0])
            for p in range(1, npart):
                acc = acc + _dot(a_ref[p, rows, :], w_ref[p])
            xn = x_ref[rows, :] + scale * acc
            xo_ref[rows, :] = xn
            r = lax.rsqrt(jnp.mean(xn * xn, axis=-1, keepdims=True) + EPS)
            h_ref[rows, :] = (xn * r * g_ref[...]).astype(BF16)

    call = _pcall(body, name=name, grid=(s // tm,),
                  in_specs=[pl.BlockSpec((npart, tm, kp), lambda i: (0, i, 0)),
                            pl.BlockSpec((npart, kp, D), lambda i: (0, 0, 0)),
                            pl.BlockSpec((tm, D), lambda i: (i, 0)),
                            pl.BlockSpec((1, D), lambda i: (0, 0))],
                  out_specs=[pl.BlockSpec((tm, D), lambda i: (i, 0)), pl.BlockSpec((tm, D), lambda i: (i, 0))],
                  out_shape=[_sds((s, D), F32), _sds((s, D), BF16)], carry=carry)
    return _carried(call, (a, w, xres, gain), carry)


def _ffn_down_loss(a, w, xres, gain, target, name, tm=512):
    npart, s, kp = a.shape
    tm = min(tm, s)

    def body(a_ref, w_ref, x_ref, g_ref, t_ref, dx_ref, dxb_ref, loss_ref, dg_ref):
        @pl.when(pl.program_id(0) == 0)
        def _():
            loss_ref[...] = jnp.zeros_like(loss_ref)
            dg_ref[...] = jnp.zeros_like(dg_ref)

        for rows in _row_chunks(tm):
            acc = _dot(a_ref[0, rows, :], w_ref[0])
            for p in range(1, npart):
                acc = acc + _dot(a_ref[p, rows, :], w_ref[p])
            xn = x_ref[rows, :] + 0.5 * acc
            r = lax.rsqrt(jnp.mean(xn * xn, axis=-1, keepdims=True) + EPS)
            xh = xn * r
            gv = g_ref[...]
            err = xh * gv - t_ref[rows, :]
            part = 0.5 * jnp.sum(jnp.mean(err * err, axis=-1, keepdims=True), axis=0, keepdims=True)
            dy = err * (1.0 / D)
            dyg = dy * gv
            dxn = r * (dyg - xh * jnp.mean(dyg * xh, axis=-1, keepdims=True))
            dx_ref[rows, :] = dxn
            dxb_ref[rows, :] = dxn.astype(BF16)
            loss_ref[...] += jnp.broadcast_to(part, loss_ref.shape)
            dg_ref[...] += jnp.sum(dy * xh, axis=0, keepdims=True)

    return _pcall(body, name=name, grid=(s // tm,),
                  in_specs=[pl.BlockSpec((npart, tm, kp), lambda i: (0, i, 0)),
                            pl.BlockSpec((npart, kp, D), lambda i: (0, 0, 0)),
                            pl.BlockSpec((tm, D), lambda i: (i, 0)),
                            pl.BlockSpec((1, D), lambda i: (0, 0)),
                            pl.BlockSpec((tm, D), lambda i: (i, 0))],
                  out_specs=[pl.BlockSpec((tm, D), lambda i: (i, 0)), pl.BlockSpec((tm, D), lambda i: (i, 0)),
                             pl.BlockSpec((8, 128), lambda i: (0, 0)), pl.BlockSpec((1, D), lambda i: (0, 0))],
                  out_shape=[_sds((s, D), F32), _sds((s, D), BF16), _sds((8, 128), F32), _sds((1, D), F32)],
                  )(a, w, xres, gain, target)


def _window_tiles():
    i = lax.broadcasted_iota(jnp.int32, (BLK, BLK), 0)
    j = lax.broadcasted_iota(jnp.int32, (BLK, BLK), 1)
    rel = (i - j) & (BLK - 1)
    large = jnp.full_like(rel, REL_EXACT)
    for t in BUCKET_THRESHOLDS:
        large = large + (rel >= t).astype(jnp.int32)
    return j <= i, jnp.where(rel < REL_EXACT, rel, large)


def _bias_build(rel_bias, name):
    def body(rb_ref, o_ref):
        _, bucket = _window_tiles()

        def per_head(h, carry):
            acc = jnp.zeros((BLK, BLK), F32)
            for b in range(REL_BUCKETS):
                acc = jnp.where(bucket == b, rb_ref[b, h], acc)
            o_ref[h] = acc
            return carry

        lax.fori_loop(0, N_HEADS, per_head, 0)

    return _pcall(body, name=name, grid=(1,),
                  in_specs=[pl.BlockSpec(memory_space=pltpu.SMEM)],
                  out_specs=pl.BlockSpec((N_HEADS, BLK, BLK), lambda i: (0, 0, 0)),
                  out_shape=_sds((N_HEADS, BLK, BLK), F32))(rel_bias)


def _bias_bwd(dbias, name):
    def body(db_ref, o_ref):
        _, bucket = _window_tiles()
        lane = lax.broadcasted_iota(jnp.int32, (N_HEADS, 128), 1)

        def per_bucket(b, out):
            mb = (bucket == b).astype(F32)
            per_col = jnp.sum(db_ref[...] * mb[None, :, :], axis=1)
            return jnp.where(lane == b, jnp.sum(per_col, axis=1, keepdims=True), out)

        o_ref[...] = lax.fori_loop(0, REL_BUCKETS, per_bucket, jnp.zeros((N_HEADS, 128), F32))

    return _pcall(body, name=name, grid=(1,),
                  in_specs=[pl.BlockSpec((N_HEADS, BLK, BLK), lambda i: (0, 0, 0))],
                  out_specs=pl.BlockSpec((N_HEADS, 128), lambda i: (0, 0)),
                  out_shape=_sds((N_HEADS, 128), F32))(dbias)


PAIR = 2 * HEAD
GROUP = N_HEADS // N_KV
SWA_SCALE = HEAD ** -0.5


def _window_masks(n):
    i = lax.broadcasted_iota(jnp.int32, (GROUP * BLK, BLK), 0) & (BLK - 1)
    j = lax.broadcasted_iota(jnp.int32, (GROUP * BLK, BLK), 1)
    return j <= i, jnp.logical_and(n == 0, j > i), j < HEAD


def _kv_twice(ref, base, g, low):
    slab = ref[:, base + PAIR * (g // 2): base + PAIR * (g // 2 + 1)]
    swapped = pltpu.roll(slab, HEAD, 1)
    return jnp.where(low, slab, swapped) if g % 2 == 0 else jnp.where(low, swapped, slab)


def _stack_heads(ref, g, low):
    parts = []
    for r in range(2):
        slab = ref[:, PAIR * (2 * g + r): PAIR * (2 * g + r + 1)]
        zero = jnp.zeros_like(slab)
        parts += [jnp.where(low, slab, zero), jnp.where(low, zero, slab)]
    return jnp.concatenate(parts, axis=0)


def _unstack_heads(t, low):
    return [jnp.where(low, t[2 * r * BLK:(2 * r + 1) * BLK], t[(2 * r + 1) * BLK:(2 * r + 2) * BLK])
            for r in range(2)]


def _head_rows(t, k):
    return t[k * BLK:(k + 1) * BLK]


def _per_head_column(values):
    head = lax.broadcasted_iota(jnp.int32, (GROUP * BLK, 1), 0) // BLK
    col = jnp.full((GROUP * BLK, 1), values[0], F32)
    for k in range(1, GROUP):
        col = jnp.where(head == k, values[k], col)
    return col


def _window_logits(q4, kc, kp, bias4, own, absent):
    sc = jnp.where(own, _dot_nt(q4, kc), _dot_nt(q4, kp)) * SWA_SCALE + bias4
    return jnp.where(absent, NEG, sc)


def _split_window(t, own):
    zero = jnp.zeros_like(t)
    return jnp.where(own, t, zero), jnp.where(own, zero, t)


def _swa_fwd(pb, bias, sinks, name, carry=None):
    _, s, _ = pb.shape
    nb = s // BLK
    kvw = 2 * N_KV * HEAD

    def body(q_ref, kc_ref, kp_ref, b_ref, sk_ref, o_ref, lse_ref):
        own, absent, low4 = _window_masks(pl.program_id(0))
        low = low4[:BLK]
        lane = lax.broadcasted_iota(jnp.int32, (BLK, 128), 1)
        lse_t = jnp.zeros((BLK, 128), F32)
        for g in range(N_KV):
            q4 = _stack_heads(q_ref, g, low)
            kc, kp = _kv_twice(kc_ref, 0, g, low), _kv_twice(kp_ref, 0, g, low)
            vc, vp = _kv_twice(kc_ref, N_KV * HEAD, g, low), _kv_twice(kp_ref, N_KV * HEAD, g, low)
            bias4 = b_ref[GROUP * g:GROUP * (g + 1)].reshape(GROUP * BLK, BLK)
            sc = _window_logits(q4, kc, kp, bias4, own, absent)
            sk = _per_head_column([sk_ref[0, GROUP * g + k] for k in range(GROUP)])
            m = jnp.maximum(jnp.max(sc, axis=1, keepdims=True), sk)
            p = jnp.exp(sc - m)
            l = jnp.sum(p, axis=1, keepdims=True) + jnp.exp(sk - m)
            p_own, p_prev = _split_window(p.astype(BF16), own)
            out = (_dot(p_own, vc) + _dot(p_prev, vp)) * (1.0 / l)
            for r, slab in enumerate(_unstack_heads(out, low)):
                o_ref[:, PAIR * (2 * g + r): PAIR * (2 * g + r + 1)] = slab.astype(BF16)
            lse4 = m + jnp.log(l)
            for k in range(GROUP):
                lse_t = jnp.where(lane == GROUP * g + k, _head_rows(lse4, k), lse_t)
        lse_ref[...] = lse_t

    call = _pcall(body, name=name, grid=(nb,),
                  in_specs=[pl.BlockSpec((None, BLK, D), lambda n: (0, n, 0)),
                            pl.BlockSpec((None, BLK, kvw), lambda n: (1, n, 0)),
                            pl.BlockSpec((None, BLK, kvw), lambda n: (1, jnp.maximum(n - 1, 0), 0)),
                            pl.BlockSpec((N_HEADS, BLK, BLK), lambda n: (0, 0, 0)),
                            pl.BlockSpec(memory_space=pltpu.SMEM)],
                  out_specs=[pl.BlockSpec((BLK, D), lambda n: (n, 0)), pl.BlockSpec((BLK, 128), lambda n: (n, 0))],
                  out_shape=[_sds((s, D), BF16), _sds((s, 128), F32)], carry=carry)
    return _carried(call, (pb, pb, pb, bias, sinks), carry)


def _fold_halves(t, g, low):
    folded = jnp.where(low, t, 0.0) + pltpu.roll(jnp.where(low, 0.0, t), HEAD, 1)
    return folded if g % 2 == 0 else pltpu.roll(folded, HEAD, 1)


def _swa_bwd(pb, attn, dattn, lse, bias, sinks, name, carry=None):
    _, s, _ = pb.shape
    nb = s // BLK
    kvw = 2 * N_KV * HEAD
    voff = N_KV * HEAD

    def body(q_ref, kc_ref, kp_ref, o_ref, do_ref, lse_ref, b_ref, skrow_ref, dpb_ref, dbias_ref, dsk_ref,
             dq_hold, kv_hold, dq_new, kv_prev, kv_cur):
        n = pl.program_id(0)

        @pl.when(n == 0)
        def _():
            dbias_ref[...] = jnp.zeros_like(dbias_ref)
            dsk_ref[...] = jnp.zeros_like(dsk_ref)
            dq_hold[...] = jnp.zeros_like(dq_hold)
            kv_hold[...] = jnp.zeros_like(kv_hold)

        @pl.when(n < nb)
        def _():
            own, absent, low4 = _window_masks(n)
            low = low4[:BLK]
            lane = lax.broadcasted_iota(jnp.int32, (BLK, 128), 1)
            delta_t = jnp.zeros((BLK, 128), F32)
            ones = jnp.ones((PAIR, 128), BF16)
            for pair_of_kv in range(N_KV // 2):
                slab_grads = [jnp.zeros((BLK, PAIR), F32) for _ in range(4)]
                for g in (2 * pair_of_kv, 2 * pair_of_kv + 1):
                    q4, do4 = _stack_heads(q_ref, g, low), _stack_heads(do_ref, g, low)
                    kc, kp = _kv_twice(kc_ref, 0, g, low), _kv_twice(kp_ref, 0, g, low)
                    vc, vp = _kv_twice(kc_ref, voff, g, low), _kv_twice(kp_ref, voff, g, low)
                    o_slabs = [o_ref[:, PAIR * (2 * g + r): PAIR * (2 * g + r + 1)] for r in range(2)]
                    o4 = jnp.concatenate([o_slabs[0], o_slabs[0], o_slabs[1], o_slabs[1]], axis=0)
                    delta = _dot(do4 * o4, ones)
                    heads = range(GROUP * g, GROUP * (g + 1))
                    lse4 = jnp.concatenate([lse_ref[:, h:h + 1] for h in heads], axis=0)
                    bias4 = b_ref[GROUP * g:GROUP * (g + 1)].reshape(GROUP * BLK, BLK)
                    p = jnp.exp(_window_logits(q4, kc, kp, bias4, own, absent) - lse4)
                    dp = jnp.where(own, _dot_nt(do4, vc), _dot_nt(do4, vp))
                    ds = p * (dp - delta)
                    dbias_ref[GROUP * g:GROUP * (g + 1)] += ds.reshape(GROUP, BLK, BLK)
                    for k, h in enumerate(heads):
                        delta_t = jnp.where(lane == h, _head_rows(delta, k), delta_t)
                    ds_own, ds_prev = _split_window((ds * SWA_SCALE).astype(BF16), own)
                    p_own, p_prev = _split_window(p.astype(BF16), own)
                    dq4 = _dot(ds_own, kc) + _dot(ds_prev, kp)
                    for r, slab in enumerate(_unstack_heads(dq4, low)):
                        dq_new[:, PAIR * (2 * g + r): PAIR * (2 * g + r + 1)] = slab
                    grads = [_dot_tn(ds_own, q4), _dot_tn(ds_prev, q4), _dot_tn(p_own, do4), _dot_tn(p_prev, do4)]
                    slab_grads = [t + _fold_halves(dk, g, low) for t, dk in zip(slab_grads, grads)]
                ks = slice(PAIR * pair_of_kv, PAIR * (pair_of_kv + 1))
                vs = slice(voff + PAIR * pair_of_kv, voff + PAIR * (pair_of_kv + 1))
                kv_cur[:, ks], kv_prev[:, ks], kv_cur[:, vs], kv_prev[:, vs] = slab_grads
            dsk_ref[...] -= jnp.sum(jnp.exp(skrow_ref[...] - lse_ref[...]) * delta_t, axis=0, keepdims=True)

        @pl.when(n == nb)
        def _():
            kv_prev[...] = jnp.zeros_like(kv_prev)

        dpb_ref[0] = dq_hold[...].astype(BF16)
        dpb_ref[1, :, 0:kvw] = (kv_hold[...] + kv_prev[...]).astype(BF16)
        dpb_ref[1, :, kvw:D] = jnp.zeros((BLK, D - kvw), BF16)

        @pl.when(n < nb)
        def _():
            dq_hold[...] = dq_new[...]
            kv_hold[...] = kv_cur[...]

    def cur(n):
        return jnp.minimum(n, nb - 1)

    call = _pcall(body, name=name, grid=(nb + 1,), carry=carry,
                  in_specs=[pl.BlockSpec((None, BLK, D), lambda n: (0, cur(n), 0)),
                            pl.BlockSpec((None, BLK, kvw), lambda n: (1, cur(n), 0)),
                            pl.BlockSpec((None, BLK, kvw), lambda n: (1, jnp.maximum(cur(n) - 1, 0), 0)),
                            pl.BlockSpec((BLK, D), lambda n: (cur(n), 0)),
                            pl.BlockSpec((BLK, D), lambda n: (cur(n), 0)),
                            pl.BlockSpec((BLK, 128), lambda n: (cur(n), 0)),
                            pl.BlockSpec((N_HEADS, BLK, BLK), lambda n: (0, 0, 0)),
                            pl.BlockSpec((1, 128), lambda n: (0, 0))],
                  out_specs=[pl.BlockSpec((2, BLK, D), lambda n: (0, jnp.maximum(n - 1, 0), 0)),
                             pl.BlockSpec((N_HEADS, BLK, BLK), lambda n: (0, 0, 0)),
                             pl.BlockSpec((1, 128), lambda n: (0, 0))],
                  out_shape=[_sds((2, s, D), BF16), _sds((N_HEADS, BLK, BLK), F32), _sds((1, 128), F32)],
                  scratch=[pltpu.VMEM((BLK, D), F32), pltpu.VMEM((BLK, kvw), F32), pltpu.VMEM((BLK, D), F32),
                           pltpu.VMEM((BLK, kvw), F32), pltpu.VMEM((BLK, kvw), F32)])
    sink_row = jnp.pad(sinks, ((0, 0), (0, 128 - N_HEADS)))
    return _carried(call, (pb, pb, pb, attn, dattn, lse, bias, sink_row), carry)


HALO = 16
CW = 512


def _conv_taps(cu, halo_cu, first_tile):
    row = lax.broadcasted_iota(jnp.int32, cu.shape, 0)
    halo_cu = jnp.where(first_tile, 0.0, halo_cu)
    c1 = jnp.where(row == 0, halo_cu[HALO - 1:HALO], pltpu.roll(cu, 1, 0))
    c2 = jnp.where(row == 0, halo_cu[HALO - 2:HALO - 1],
                   jnp.where(row == 1, halo_cu[HALO - 1:HALO], pltpu.roll(cu, 2, 0)))
    return c1, c2


def _conv_merge_fwd(pa, attn, convw, name, ts=512):
    _, s, _ = pa.shape
    ts = min(ts, s)
    hb = ts // HALO

    def body(pa_ref, hp_ref, at_ref, w_ref, o_ref):
        i = pl.program_id(1)
        cu = pa_ref[0].astype(F32) * pa_ref[2].astype(F32)
        c1, c2 = _conv_taps(cu, hp_ref[0].astype(F32) * hp_ref[2].astype(F32), i == 0)
        w = w_ref[...]
        c3 = w[0:1] * c2 + w[1:2] * c1 + w[2:3] * cu
        conv = pa_ref[1].astype(F32) * c3
        o_ref[...] = (jax.nn.sigmoid(pa_ref[3].astype(F32)) * at_ref[...].astype(F32)
                      + jax.nn.sigmoid(pa_ref[4].astype(F32)) * conv).astype(BF16)

    return _pcall(body, name=name, grid=(D // CW, s // ts),
                  in_specs=[pl.BlockSpec((5, ts, CW), lambda c, i: (0, i, c)),
                            pl.BlockSpec((5, HALO, CW), lambda c, i: (0, jnp.maximum(i * hb - 1, 0), c)),
                            pl.BlockSpec((ts, CW), lambda c, i: (i, c)),
                            pl.BlockSpec((8, CW), lambda c, i: (0, c))],
                  out_specs=pl.BlockSpec((ts, CW), lambda c, i: (i, c)),
                  out_shape=_sds((s, D), BF16))(pa, pa, attn, convw)


def _conv_merge_bwd(dmerged, pa, attn, convw, name, ts=512, carry=None):
    _, s, _ = pa.shape
    ts = min(ts, s)
    hb = ts // HALO
    last_hb = s // HALO - 1

    def body(dm_ref, pa_ref, at_ref, w_ref, hp_ref, hn_ref, dmn_ref, dat_ref, dpa_ref, dw_ref):
        i = pl.program_id(1)
        last = i == pl.num_programs(1) - 1
        dm = dm_ref[...].astype(F32)
        cp, bp, u = pa_ref[0].astype(F32), pa_ref[1].astype(F32), pa_ref[2].astype(F32)
        sa = jax.nn.sigmoid(pa_ref[3].astype(F32))
        sc = jax.nn.sigmoid(pa_ref[4].astype(F32))
        at = at_ref[...].astype(F32)
        cu = cp * u
        c1, c2 = _conv_taps(cu, hp_ref[0].astype(F32) * hp_ref[2].astype(F32), i == 0)
        w = w_ref[...]
        c3 = w[0:1] * c2 + w[1:2] * c1 + w[2:3] * cu
        dconv = dm * sc
        dc3 = dconv * bp
        nxt = dmn_ref[...].astype(F32) * jax.nn.sigmoid(hn_ref[4].astype(F32)) * hn_ref[1].astype(F32)
        nxt = jnp.where(last, 0.0, nxt)
        row = lax.broadcasted_iota(jnp.int32, dc3.shape, 0)
        d1 = jnp.where(row == ts - 1, nxt[0:1], pltpu.roll(dc3, ts - 1, 0))
        d2 = jnp.where(row == ts - 2, nxt[0:1], jnp.where(row == ts - 1, nxt[1:2], pltpu.roll(dc3, ts - 2, 0)))
        dcu = w[2:3] * dc3 + w[1:2] * d1 + w[0:1] * d2
        dat_ref[...] = (dm * sa).astype(BF16)
        dpa_ref[0] = (dcu * u).astype(BF16)
        dpa_ref[1] = (dconv * c3).astype(BF16)
        dpa_ref[2] = (dcu * cp).astype(BF16)
        dpa_ref[3] = (dm * at * sa * (1.0 - sa)).astype(BF16)
        dpa_ref[4] = (dm * bp * c3 * sc * (1.0 - sc)).astype(BF16)

        @pl.when(i == 0)
        def _():
            dw_ref[...] = jnp.zeros_like(dw_ref)

        dw_ref[0:1, :] += jnp.sum(dc3 * c2, axis=0, keepdims=True)
        dw_ref[1:2, :] += jnp.sum(dc3 * c1, axis=0, keepdims=True)
        dw_ref[2:3, :] += jnp.sum(dc3 * cu, axis=0, keepdims=True)

    call = _pcall(body, name=name, grid=(D // CW, s // ts), carry=carry,
                  in_specs=[pl.BlockSpec((ts, CW), lambda c, i: (i, c)),
                            pl.BlockSpec((5, ts, CW), lambda c, i: (0, i, c)),
                            pl.BlockSpec((ts, CW), lambda c, i: (i, c)),
                            pl.BlockSpec((8, CW), lambda c, i: (0, c)),
                            pl.BlockSpec((5, HALO, CW), lambda c, i: (0, jnp.maximum(i * hb - 1, 0), c)),
                            pl.BlockSpec((5, HALO, CW), lambda c, i: (0, jnp.minimum((i + 1) * hb, last_hb), c)),
                            pl.BlockSpec((HALO, CW), lambda c, i: (jnp.minimum((i + 1) * hb, last_hb), c))],
                  out_specs=[pl.BlockSpec((ts, CW), lambda c, i: (i, c)),
                             pl.BlockSpec((5, ts, CW), lambda c, i: (0, i, c)),
                             pl.BlockSpec((8, CW), lambda c, i: (0, c))],
                  out_shape=[_sds((s, D), BF16), _sds((5, s, D), BF16), _sds((8, D), F32)])
    return _carried(call, (dmerged, pa, attn, convw, pa, pa, dmerged), carry)


def _xattn_fwd(q, kv, name, tq=512):
    s, _ = q.shape
    nm = kv.shape[1]
    tq = min(tq, s)

    def body(q_ref, kv_ref, o_ref, lse_ref):
        lane = lax.broadcasted_iota(jnp.int32, (tq, 128), 1)
        lse_t = jnp.zeros((tq, 128), F32)
        for h in range(XH):
            hs = slice(XHD * h, XHD * (h + 1))
            sc = _dot_nt(q_ref[:, hs], kv_ref[h]) * (XHD ** -0.5)
            m = jnp.max(sc, axis=1, keepdims=True)
            p = jnp.exp(sc - m)
            l = jnp.sum(p, axis=1, keepdims=True)
            o_ref[:, hs] = (_dot(p.astype(BF16), kv_ref[XH + h]) * (1.0 / l)).astype(BF16)
            lse_t = jnp.where(lane == h, m + jnp.log(l), lse_t)
        lse_ref[...] = lse_t

    return _pcall(body, name=name, grid=(s // tq,),
                  in_specs=[pl.BlockSpec((tq, D), lambda i: (i, 0)), pl.BlockSpec((2 * XH, nm, XHD), lambda i: (0, 0, 0))],
                  out_specs=[pl.BlockSpec((tq, D), lambda i: (i, 0)), pl.BlockSpec((tq, 128), lambda i: (i, 0))],
                  out_shape=[_sds((s, D), BF16), _sds((s, 128), F32)])(q, kv)


def _xattn_bwd(q, kv, o, do, lse, name, tq=512, carry=None):
    s, _ = q.shape
    nm = kv.shape[1]
    tq = min(tq, s)

    def body(q_ref, kv_ref, o_ref, do_ref, lse_ref, dq_ref, dkv_ref):
        @pl.when(pl.program_id(0) == 0)
        def _():
            dkv_ref[...] = jnp.zeros_like(dkv_ref)

        for h in range(XH):
            hs = slice(XHD * h, XHD * (h + 1))
            qh, kh, vh, dob = q_ref[:, hs], kv_ref[h], kv_ref[XH + h], do_ref[:, hs]
            p = jnp.exp(_dot_nt(qh, kh) * (XHD ** -0.5) - lse_ref[:, h:h + 1])
            dp = _dot_nt(dob, vh)
            delta = jnp.sum(dob.astype(F32) * o_ref[:, hs].astype(F32), axis=1, keepdims=True)
            dsb = (p * (dp - delta) * (XHD ** -0.5)).astype(BF16)
            dq_ref[:, hs] = _dot(dsb, kh).astype(BF16)
            dkv_ref[h] += _dot_tn(dsb, qh)
            dkv_ref[XH + h] += _dot_tn(p.astype(BF16), dob)

    call = _pcall(body, name=name, grid=(s // tq,), carry=carry,
                  in_specs=[pl.BlockSpec((tq, D), lambda i: (i, 0)), pl.BlockSpec((2 * XH, nm, XHD), lambda i: (0, 0, 0)),
                            pl.BlockSpec((tq, D), lambda i: (i, 0)), pl.BlockSpec((tq, D), lambda i: (i, 0)),
                            pl.BlockSpec((tq, 128), lambda i: (i, 0))],
                  out_specs=[pl.BlockSpec((tq, D), lambda i: (i, 0)), pl.BlockSpec((2 * XH, nm, XHD), lambda i: (0, 0, 0))],
                  out_shape=[_sds((s, D), BF16), _sds((2 * XH, nm, XHD), F32)])
    return _carried(call, (q, kv, o, do, lse), carry)


def _ffn_down_bwd(dxb, wd4, gu4, name, tm=1024, carry=None):
    s, _ = dxb.shape
    tm = min(tm, s)

    def body(dx_ref, w_ref, gu_ref, o_ref):
        for rows in _row_chunks(tm):
            da = _dot_nt(dx_ref[rows, :], w_ref[...])
            g = gu_ref[0, rows, :].astype(F32)
            u = gu_ref[1, rows, :].astype(F32)
            sg = jax.nn.sigmoid(g)
            t = da * sg
            o_ref[0, rows, :] = (t * u * (1.0 + g - g * sg)).astype(BF16)
            o_ref[1, rows, :] = (t * g).astype(BF16)

    call = _pcall(body, name=name, grid=(4, s // tm), carry=carry,
                  in_specs=[pl.BlockSpec((tm, D), lambda p, i: (i, 0)),
                            pl.BlockSpec((None, FS, D), lambda p, i: (p, 0, 0)),
                            pl.BlockSpec((2, None, tm, FS), lambda p, i: (0, p, i, 0))],
                  out_specs=pl.BlockSpec((2, None, tm, FS), lambda p, i: (0, p, i, 0)),
                  out_shape=_sds((2, 4, s, FS), BF16))
    return _carried(call, (dxb, wd4, gu4), carry)


def _mm_tn(a, b, name, scale=1.0, carry=None):
    pa_n, s, m = a.shape
    pb_n, _, n = b.shape
    po = max(pa_n, pb_n)
    tn = n if po >= 4 else min(n, 256)

    def body(a_ref, b_ref, o_ref):
        o_ref[...] = (scale * _dot_tn(a_ref[...], b_ref[...])).astype(BF16)

    call = _pcall(body, name=name, grid=(po, n // tn), carry=carry,
                  in_specs=[pl.BlockSpec((None, s, m), lambda o, j: (o if pa_n > 1 else 0, 0, 0)),
                            pl.BlockSpec((None, s, tn), lambda o, j: (o if pb_n > 1 else 0, 0, j))],
                  out_specs=pl.BlockSpec((None, m, tn), lambda o, j: (o, 0, j)),
                  out_shape=_sds((po, m, n), BF16))
    return _carried(call, (a, b), carry)


def _sum_dots(a_ref, b_ref, nj, bt, rows=slice(None)):
    dot = _dot_nt if bt else _dot
    acc = dot(a_ref[0, rows, :], b_ref[0])
    for j in range(1, nj):
        acc = acc + dot(a_ref[j, rows, :], b_ref[j])
    return acc


def _mm_acc(a, b, name, out_dtype, tm=512, bt=False, carry=None):
    nj, s, k = a.shape
    n = b.shape[1] if bt else b.shape[2]
    tm = min(tm, s)

    def body(a_ref, b_ref, o_ref):
        o_ref[...] = _sum_dots(a_ref, b_ref, nj, bt).astype(out_dtype)

    call = _pcall(body, name=name, grid=(s // tm,), carry=carry,
                  in_specs=[pl.BlockSpec((nj, tm, k), lambda i: (0, i, 0)),
                            pl.BlockSpec(b.shape, lambda i: (0, 0, 0))],
                  out_specs=pl.BlockSpec((tm, n), lambda i: (i, 0)), out_shape=_sds((s, n), out_dtype))
    return _carried(call, (a, b), carry)


def _mm_acc_rms_bwd(a, b, name, *, x, gain, dres, addend=None, scale=None, tm=512, bt=False, carry=None):
    nj, s, k = a.shape
    n = b.shape[1] if bt else b.shape[2]
    tm = min(tm, s)
    has_add = addend is not None

    def body(*refs):
        a_ref, b_hbm, x_ref, g_ref, r_ref = refs[:5]
        add_ref = refs[5] if has_add else None
        dx_ref, dxb_ref, dg_ref, b_ref, b_sem = refs[5 + has_add:]

        @pl.when(pl.program_id(0) == 0)
        def _():
            load = pltpu.make_async_copy(b_hbm, b_ref, b_sem)
            load.start()
            dg_ref[...] = jnp.zeros_like(dg_ref)
            load.wait()

        for rows in _row_chunks(tm):
            dh = _sum_dots(a_ref, b_ref, nj, bt, rows)
            if scale is not None:
                dh = scale * dh
            if has_add:
                dh = dh + add_ref[rows, :]
            xv = x_ref[rows, :]
            r = lax.rsqrt(jnp.mean(xv * xv, axis=-1, keepdims=True) + EPS)
            xh = xv * r
            dyg = dh * g_ref[...]
            dx = r_ref[rows, :] + r * (dyg - xh * jnp.mean(dyg * xh, axis=-1, keepdims=True))
            dx_ref[rows, :] = dx
            dxb_ref[rows, :] = dx.astype(BF16)
            dg_ref[...] += jnp.sum(dh * xh, axis=0, keepdims=True)

    row = pl.BlockSpec((tm, n), lambda i: (i, 0))
    in_specs = [pl.BlockSpec((nj, tm, k), lambda i: (0, i, 0)), HBM_SPEC,
                row, pl.BlockSpec((1, n), lambda i: (0, 0)), row] + ([row] if has_add else [])
    args = (a, b, x, gain, dres) + ((addend,) if has_add else ())
    call = _pcall(body, name=name, grid=(s // tm,), in_specs=in_specs, carry=carry,
                  out_specs=[row, row, pl.BlockSpec((1, n), lambda i: (0, 0))],
                  out_shape=[_sds((s, n), F32), _sds((s, n), BF16), _sds((1, n), F32)],
                  scratch=[pltpu.VMEM(b.shape, b.dtype), pltpu.SemaphoreType.DMA(())])
    return _carried(call, args, carry)


def _adam(w, g, m, v):
    m2 = ADAM_B1 * m + (1.0 - ADAM_B1) * g
    v2 = ADAM_B2 * v + (1.0 - ADAM_B2) * (g * g)
    m_hat = m2 / (1.0 - ADAM_B1 ** ADAM_STEP)
    v_hat = v2 / (1.0 - ADAM_B2 ** ADAM_STEP)
    delta = -ADAM_LR * (m_hat / (jnp.sqrt(v_hat) + ADAM_EPS) + ADAM_WD * w)
    return delta, m2, v2


def _adamw(parts, w, m, v, name):
    _, r, c = parts.shape
    tr = max(t for t in range(16, 257, 16) if r % t == 0)

    def body(p_ref, w_ref, m_ref, v_ref, g_ref, d_ref, m2_ref, v2_ref):
        g = p_ref[0].astype(F32)
        for i in range(1, N_DEV):
            g = g + p_ref[i].astype(F32)
        delta, m2, v2 = _adam(w_ref[...], g, m_ref[...], v_ref[...])
        g_ref[...] = g
        d_ref[...] = delta
        m2_ref[...] = m2
        v2_ref[...] = v2

    blk = pl.BlockSpec((tr, c), lambda i: (i, 0))
    return _pcall(body, name=name, grid=(r // tr,),
                  in_specs=[pl.BlockSpec((N_DEV, tr, c), lambda i: (0, i, 0)), blk, blk, blk],
                  out_specs=[blk] * 4, out_shape=[_sds((r, c), F32)] * 4)(parts, w, m, v)


def _position():
    return lax.axis_index("x"), lax.axis_index("y"), lax.axis_index("c")


def _slot(px, py, pc):
    return 4 * px + 2 * py + pc


def _row_window(ref, rows):
    r0, r1 = rows
    return ref if (r0, r1) == (0, ref.shape[0]) else ref.at[pl.ds(r0, r1 - r0)]


def _split_items(items):
    sources = [src for src, _, _ in items]
    begun = [(a, dest) for a, (_, _, dest) in enumerate(items) if dest is not None]
    aliases = {len(sources) + k: a for k, (a, _) in enumerate(begun)}
    return sources + [dest for _, dest in begun], [rows for _, rows, _ in items], aliases


def _gather_carry(items):
    na = len(items)
    carry_ins, windows, aliases = _split_items(items)

    def plan(ins, outs, sems):
        send_sems, recv_sems, local_sems = sems
        x, y, c = _position()
        me, sibling = (x, y, c), (x, y, 1 - c)
        chips = [(1 - x, y), (x, 1 - y), (1 - x, 1 - y)]
        ins = [_row_window(ins[a], windows[a]) for a in range(na)]

        def block_rows(a, block):
            return _row_window(outs[a].at[_slot(*block)], windows[a])

        def copy(a, k, block, to, src=None):
            rows = block_rows(a, block)
            return pltpu.make_async_remote_copy(src_ref=rows if src is None else src, dst_ref=rows,
                                                send_sem=send_sems.at[k, a], recv_sem=recv_sems.at[k, a],
                                                device_id=to, device_id_type=MESH)

        mine = [pltpu.make_async_copy(ins[a], block_rows(a, me), local_sems.at[a]) for a in range(na)]
        first = [copy(a, 0, me, sibling, src=ins[a]) for a in range(na)]
        for j, chip in enumerate(chips):
            first += [copy(a, 1 + j, me, (*chip, c), src=ins[a]) for a in range(na)]
        landed = [[copy(a, 1 + j, (*chip, c), me) for a in range(na)] for j, chip in enumerate(chips)]
        passed = [[copy(a, 4 + j, (*chip, c), sibling) for a in range(na)] for j, chip in enumerate(chips)]
        from_sibling = [copy(a, 0, sibling, me) for a in range(na)]
        for j, chip in enumerate(chips):
            from_sibling += [copy(a, 4 + j, (*chip, 1 - c), me) for a in range(na)]
        return mine, first, landed, passed, from_sibling

    def start(ins, outs, sems):
        mine, first, _, _, _ = plan(ins, outs, sems)
        for cp in mine + first:
            cp.start()

    def mid(ins, outs, sems):
        _, _, landed, passed, _ = plan(ins, outs, sems)
        for over_ici, onward in zip(landed, passed):
            for cp, fwd in zip(over_ici, onward):
                cp.wait_recv()
                fwd.start()

    def finish(ins, outs, sems):
        mine, first, _, passed, from_sibling = plan(ins, outs, sems)
        for cp in from_sibling:
            cp.wait_recv()
        for cp in first + [fwd for onward in passed for fwd in onward]:
            cp.wait_send()
        for cp in mine:
            cp.wait()

    return _Carry(carry_ins, [_sds((N_DEV,) + src.shape, src.dtype) for src, _, _ in items],
                  [pltpu.SemaphoreType.DMA((7, na)), pltpu.SemaphoreType.DMA((7, na)),
                   pltpu.SemaphoreType.DMA((na,))], start, finish, mid, aliases)


def _exchange_carry(scattered, replicated=()):
    items = list(scattered) + [(a, (0, a.shape[0]), None) for a in replicated]
    na, ns = len(items), len(scattered)
    carry_ins, windows, aliases = _split_items(items)

    def plan(ins, outs, sems):
        send_sems, recv_sems, local_sems = sems
        me = _slot(*_position())

        def source(a, j):
            return _row_window(ins[a].at[j] if a < ns else ins[a], windows[a])

        def copy(a, j, i):
            return pltpu.make_async_remote_copy(src_ref=source(a, j), dst_ref=_row_window(outs[a].at[i], windows[a]),
                                                send_sem=send_sems.at[j, a], recv_sem=recv_sems.at[i, a],
                                                device_id=(j >> 2, (j >> 1) & 1, j & 1), device_id_type=MESH)

        def own(a, j):
            return pltpu.make_async_copy(source(a, j), _row_window(outs[a].at[j], windows[a]), local_sems.at[a])

        return me, copy, own

    def start(ins, outs, sems):
        me, copy, own = plan(ins, outs, sems)
        for a in range(na):
            for j in range(N_DEV):
                @pl.when(me == j)
                def _():
                    own(a, j).start()

                @pl.when(me != j)
                def _():
                    copy(a, j, me).start()

    def finish(ins, outs, sems):
        me, copy, own = plan(ins, outs, sems)
        for a in range(na):
            for j in range(N_DEV):
                @pl.when(me == j)
                def _():
                    for i in range(N_DEV):
                        if i != j:
                            copy(a, j, i).wait_recv()
                    own(a, j).wait()

                @pl.when(me != j)
                def _():
                    copy(a, j, me).wait_send()

    return _Carry(carry_ins, [_sds((N_DEV,) + src.shape[-2:], src.dtype) for src, _, _ in items],
                  [pltpu.SemaphoreType.DMA((N_DEV, na)), pltpu.SemaphoreType.DMA((N_DEV, na)),
                   pltpu.SemaphoreType.DMA((na,))], start, finish, None, aliases)


NQ, NKV = N_HEADS * HEAD, 2 * N_KV * HEAD


class _Mesh:
    def __init__(self, shards):
        self.shards, self.full, self.received, self.cache = shards, {}, {}, {}

    def fetch(self, wanted):
        items = []
        for want in wanted:
            name, r0, r1 = want if isinstance(want, tuple) else (want, 0, self.shards[want].shape[0])
            items.append((self.shards[name], (r0, r1), self.full.get(name)))
        return _gather_carry(items)

    def fetched(self, wanted, results):
        self.full.update(zip([want[0] if isinstance(want, tuple) else want for want in wanted], results))

    def send(self, *payloads):
        return _exchange_carry([(parts, rows or (0, parts.shape[1]), self.received.get(name))
                                for name, parts, rows in payloads])

    def sent(self, names, results):
        self.received.update(zip(names, results))

    def w(self, key):
        if key not in self.cache:
            self.cache[key] = self._layout(key)
        return self.cache[key]

    def _layout(self, key):
        if key in ("gu1", "gu2"):
            return self.full[key]
        if key in ("d1", "d2"):
            return self.full[key].reshape(4, FS, D)
        if key in ("out", "q", "o"):
            return self.full[key].reshape(D, D)
        if key == "kv":
            return self.full["kv"]
        if key == "convw":
            rows = self.full["conv"][:, :3, :].transpose(1, 0, 2).reshape(3, D)
            return jnp.concatenate([rows, jnp.zeros((5, D), F32)], axis=0)
        w_in_t = self.full["win"].reshape(-1, D)
        if key == "wa":
            return w_in_t[NQ + NKV:].reshape(5, D, D)
        assert key == "wb", key
        return jnp.stack([w_in_t[:NQ], jnp.pad(w_in_t[NQ:NQ + NKV], ((0, D - NKV), (0, 0)))])


def _w_in_parts(dw_a, dw_b):
    return jnp.concatenate([dw_b[0], dw_b[1][:NKV], dw_a.reshape(5 * D, D)], axis=0).reshape(N_DEV, -1, D)


def _forward_backward(x, mem, target, g, rel_bias, sinks, ex):
    s = x.shape[0]
    def fetching(wanted, call, *args, **kw):
        res, got = call(*args, carry=ex.fetch(wanted), **kw)
        ex.fetched(wanted, got)
        return res

    h1 = fetching(["gu1", "conv"], _rmsnorm, x, g["ffn1"], "norm_ffn1")
    gu1, a1 = fetching(["d1", ("win", 0, 400)], _ffn_up, h1, ex.w("gu1").reshape(2, 4, FS, D), "ffn1_up")
    x1, h2 = fetching([("win", 400, 832)], _mm_res_norm, a1, ex.w("d1"), x, g["mix"], 0.5, "ffn1_down")
    pa = fetching(["gu2"], _mm_nn, h2, ex.w("wa"), "in_proj_a", bt=True)
    pb = fetching(["out", "q"], _mm_nn, h2, ex.w("wb"), "in_proj_b", bt=True)
    biasm = _bias_build(rel_bias, "bias_build")
    attn, lse = fetching(["kv", "d2", "o"], _swa_fwd, pb, biasm, sinks, "swa_fwd")
    merged = _conv_merge_fwd(pa, attn, ex.w("convw"), "conv_merge_fwd")
    (x2, h3), _ = _mm_res_norm(merged[None], ex.w("out")[None], x1, g["xattn"], 1.0, "out_proj")
    q2 = _mm_nn(h3, ex.w("q")[None], "xattn_q")[0][0]
    mh, _ = _rmsnorm(mem, g["mem"], "norm_mem")
    kv2 = _mm_nn(mh, ex.w("kv"), "xattn_kv")[0]
    o, lse2 = _xattn_fwd(q2, kv2, "xattn_fwd")
    (x3, h4), _ = _mm_res_norm(o[None], ex.w("o")[None], x2, g["ffn2"], 1.0, "xattn_o")
    (gu2, a2), _ = _ffn_up(h4, ex.w("gu2").reshape(2, 4, FS, D), "ffn2_up")
    dx4, dx4b, loss, d_final = _ffn_down_loss(a2, ex.w("d2"), x3, g["final"], target, "ffn2_down_loss")
    def sending(payloads, call, *args, **kw):
        res, got = call(*args, carry=ex.send(*payloads), **kw)
        ex.sent([name for name, _, _ in payloads], got)
        return res

    dw_d2 = _mm_tn(a2, dx4b[None], "dw_ffn2_down", scale=0.5)[0].reshape(N_DEV, -1, D)
    dgu2 = sending([("d2", dw_d2, None)], _ffn_down_bwd, dx4b, ex.w("d2"), gu2, "ffn2_down_bwd").reshape(8, s, FS)
    dw_gu2 = _mm_tn(dgu2, h4[None], "dw_ffn2_up", scale=0.5)[0]
    dx3, dx3b, d_ffn2 = sending([("gu2", dw_gu2, (0, 400))], _mm_acc_rms_bwd, dgu2, ex.w("gu2"), "ffn2_up_bwd",
                                x=x3, gain=g["ffn2"], dres=dx4, scale=0.5)
    do, _ = _mm_acc(dx3b[None], ex.w("o")[None], "xattn_o_bwd", BF16, bt=True)
    dw_o = _mm_tn(o[None], dx3b[None], "dw_xattn_o")[0].reshape(N_DEV, -1, D)
    dq2, dkv2 = sending([("o", dw_o, None)], _xattn_bwd, q2, kv2, o, do, lse2, "xattn_bwd")
    dkv2b = dkv2.astype(BF16)
    dw_q = _mm_tn(h3[None], dq2[None], "dw_xattn_q")[0].reshape(N_DEV, -1, D)
    dx2, dx2b, d_xattn = sending([("q", dw_q, None)], _mm_acc_rms_bwd, dq2[None], ex.w("q")[None], "xattn_q_bwd",
                                 x=x2, gain=g["xattn"], dres=dx3, bt=True)
    dw_kv = _mm_tn(mh[None], dkv2b, "dw_xattn_kv")[0]
    (_, _, d_mem), _ = _mm_acc_rms_bwd(dkv2b, ex.w("kv"), "xattn_kv_bwd", x=mem, gain=g["mem"],
                                       dres=jnp.zeros_like(mem), bt=True)
    dmerged, _ = _mm_acc(dx2b[None], ex.w("out")[None], "out_proj_bwd", BF16, bt=True)
    dw_out = _mm_tn(merged[None], dx2b[None], "dw_out_proj")[0].reshape(N_DEV, -1, D)
    dattn, dpa, d_convw = sending([("kv", dw_kv, None)], _conv_merge_bwd,
                                  dmerged, pa, attn, ex.w("convw"), "conv_merge_bwd")
    dpb, dbias, d_sinks = sending([("gu2", dw_gu2, (400, FS)), ("out", dw_out, None)], _swa_bwd,
                                  pb, attn, dattn, lse, biasm, sinks, "swa_bwd")
    d_relb = _bias_bwd(dbias, "bias_bwd")
    dw_in = _w_in_parts(_mm_tn(dpa, h2[None], "dw_in_proj_a")[0], _mm_tn(dpb, h2[None], "dw_in_proj_b")[0])
    dh2_b = sending([("win", dw_in, (0, 208))], _mm_acc, dpb, ex.w("wb"), "in_proj_b_bwd", F32)
    dx1, dx1b, d_mix = sending([("win", dw_in, (208, 672))], _mm_acc_rms_bwd, dpa, ex.w("wa"), "in_proj_a_bwd",
                               x=x1, gain=g["mix"], dres=dx2, addend=dh2_b)
    dw_d1 = sending([("win", dw_in, (672, 832))], _mm_tn, a1, dx1b[None], "dw_ffn1_down", scale=0.5)
    dw_d1 = dw_d1.reshape(N_DEV, -1, D)
    dgu1 = sending([("d1", dw_d1, None)], _ffn_down_bwd, dx1b, ex.w("d1"), gu1, "ffn1_down_bwd").reshape(8, s, FS)
    dw_gu1 = _mm_tn(dgu1, h1[None], "dw_ffn1_up", scale=0.5)[0]
    dx0, _, d_ffn1 = sending([("gu1", dw_gu1, None)], _mm_acc_rms_bwd, dgu1, ex.w("gu1"), "ffn1_up_bwd",
                             x=x, gain=g["ffn1"], dres=dx1, scale=0.5)

    relb_row = jnp.concatenate([d_relb[:, :REL_BUCKETS].T.reshape(1, REL_BUCKETS * N_HEADS), d_sinks[:, :N_HEADS],
                                jnp.zeros((1, D - REL_BUCKETS * N_HEADS - N_HEADS), F32)], axis=1)
    loss_row = jnp.concatenate([loss[0:1, 0:1], jnp.zeros((1, D - 1), F32)], axis=1)
    small = jnp.concatenate([d_ffn1, d_mix, d_xattn, d_mem, d_ffn2, d_final, relb_row, loss_row, d_convw[0:3],
                             jnp.zeros((SMALL_ROWS - ROW_CONV - 3, D), F32)], axis=0)
    return dx0, small


def _pack_small(norms, final, relb, sinks, conv_local, me):
    relb_row = jnp.concatenate([relb.reshape(1, -1), sinks.reshape(1, -1),
                                jnp.zeros((1, D - REL_BUCKETS * N_HEADS - N_HEADS), F32)], axis=1)
    conv_rows = lax.dynamic_update_slice(jnp.zeros((3, D), F32), conv_local.reshape(3, -1), (0, 128 * me))
    return jnp.concatenate(list(norms) + [final.reshape(1, D), relb_row, jnp.zeros((1, D), F32), conv_rows,
                                          jnp.zeros((SMALL_ROWS - ROW_CONV - 3, D), F32)], axis=0)


def kernel(x, mem, positions, rel_bias, ffn1_norm, ffn1_w_gu, ffn1_w_down, mix_norm, w_in, sinks, conv_w, w_out, xattn_norm, mem_norm, xattn_wq, xattn_wkv, xattn_wo, ffn2_norm, ffn2_w_gu, ffn2_w_down, final_norm, loss_target, m_rel_bias, m_ffn1_norm, m_ffn1_w_gu, m_ffn1_w_down, m_mix_norm, m_w_in, m_sinks, m_conv_w, m_w_out, m_xattn_norm, m_mem_norm, m_xattn_wq, m_xattn_wkv, m_xattn_wo, m_ffn2_norm, m_ffn2_w_gu, m_ffn2_w_down, m_final_norm, v_rel_bias, v_ffn1_norm, v_ffn1_w_gu, v_ffn1_w_down, v_mix_norm, v_w_in, v_sinks, v_conv_w, v_w_out, v_xattn_norm, v_mem_norm, v_xattn_wq, v_xattn_wkv, v_xattn_wo, v_ffn2_norm, v_ffn2_w_gu, v_ffn2_w_down, v_final_norm):
    del positions
    me = _slot(*_position())
    big = dict(gu1=(ffn1_w_gu, m_ffn1_w_gu, v_ffn1_w_gu), d1=(ffn1_w_down, m_ffn1_w_down, v_ffn1_w_down),
               win=(w_in, m_w_in, v_w_in), out=(w_out, m_w_out, v_w_out), q=(xattn_wq, m_xattn_wq, v_xattn_wq),
               kv=(xattn_wkv, m_xattn_wkv, v_xattn_wkv), o=(xattn_wo, m_xattn_wo, v_xattn_wo),
               gu2=(ffn2_w_gu, m_ffn2_w_gu, v_ffn2_w_gu), d2=(ffn2_w_down, m_ffn2_w_down, v_ffn2_w_down))
    order = list(big)
    transposed = ("gu1", "gu2", "win")
    local = {k: tuple(t[0].T if k in transposed else t[0] for t in big[k]) for k in order}
    shards = {k: local[k][0].astype(BF16) for k in order}
    shards["conv"] = jnp.concatenate([conv_w[0], jnp.zeros((5, 128), F32)], axis=0)
    ex = _Mesh(shards)
    gains = dict(ffn1=ffn1_norm, mix=mix_norm, xattn=xattn_norm, mem=mem_norm, ffn2=ffn2_norm,
                 final=final_norm.reshape(1, D))
    dx, small = _forward_backward(x[0], mem[0], loss_target[0], gains, rel_bias, sinks, ex)
    small_parts = _run_alone(_exchange_carry([], [small]), "exchange_small")[0]
    big_out = {k: _adamw(ex.received[k], *local[k], "adamw_" + k) for k in order}
    big_out = {k: [t.T if k in transposed else t for t in big_out[k]] for k in order}
    packed = [_pack_small(norms, final, relb, sk, conv, me) for norms, final, relb, sk, conv in (
        ((ffn1_norm, mix_norm, xattn_norm, mem_norm, ffn2_norm), final_norm, rel_bias, sinks, conv_w),
        ((m_ffn1_norm, m_mix_norm, m_xattn_norm, m_mem_norm, m_ffn2_norm), m_final_norm, m_rel_bias, m_sinks, m_conv_w),
        ((v_ffn1_norm, v_mix_norm, v_xattn_norm, v_mem_norm, v_ffn2_norm), v_final_norm, v_rel_bias, v_sinks, v_conv_w))]
    small_out = _adamw(small_parts, *packed, "adamw_small")

    def unpack(t):
        conv = lax.dynamic_slice(t[ROW_CONV:ROW_CONV + 3], (0, 128 * me), (3, 128))[None]
        nrel = REL_BUCKETS * N_HEADS
        return dict(ffn1_norm=t[0:1], mix_norm=t[1:2], xattn_norm=t[2:3], mem_norm=t[3:4], ffn2_norm=t[4:5],
                    final_norm=t[5], rel_bias=t[ROW_RELB, :nrel].reshape(REL_BUCKETS, N_HEADS),
                    sinks=t[ROW_RELB:ROW_RELB + 1, nrel:nrel + N_HEADS], conv_w=conv)

    names = dict(gu1="ffn1_w_gu", d1="ffn1_w_down", win="w_in", out="w_out", q="xattn_wq", kv="xattn_wkv",
                 o="xattn_wo", gu2="ffn2_w_gu", d2="ffn2_w_down")
    results = []
    for idx in range(4):
        leaves = unpack(small_out[idx])
        leaves.update({names[k]: big_out[k][idx][None] for k in order})
        results.append(leaves)
    weights = ("rel_bias", "ffn1_norm", "ffn1_w_gu", "ffn1_w_down", "mix_norm", "w_in", "sinks", "conv_w", "w_out",
               "xattn_norm", "mem_norm", "xattn_wq", "xattn_wkv", "xattn_wo", "ffn2_norm", "ffn2_w_gu", "ffn2_w_down",
               "final_norm")
    loss = small_out[0][ROW_LOSS, 0]
    return (loss, dx[None], *[leaves[n] for leaves in results for n in weights])
```

```python
import math

import numpy as np
import jax
import jax.numpy as jnp
from jax import lax
from jax.experimental import pallas as pl
from jax.experimental.pallas import tpu as pltpu

F32, BF16 = jnp.float32, jnp.bfloat16
MESH = pl.DeviceIdType.MESH

D = 1024
N_DEV = 8
D_FF = 2816
FS = D_FF // 4
HEAD = 64
N_HEADS, N_KV = 16, 4
BLK = 128
XH, XHD = 4, 256
REL_BUCKETS, REL_EXACT, REL_MAX_DIST = 32, 16, 128
EPS, NEG = 1e-6, -1e30
ADAM_LR, ADAM_B1, ADAM_B2, ADAM_EPS, ADAM_WD, ADAM_STEP = 0.001, 0.9, 0.999, 1e-08, 0.01, 10
VMEM_LIMIT_V7X = 56 * 2**20
SMALL_ROWS = 16
ROW_RELB, ROW_LOSS, ROW_CONV = 6, 7, 8


def _bucket_thresholds():
    n = np.arange(REL_MAX_DIST)
    nf = np.maximum(n, 1).astype(np.float32)
    large = REL_EXACT + (np.log(nf / np.float32(REL_EXACT)) / np.float32(math.log(REL_MAX_DIST / REL_EXACT))
                         * np.float32(REL_BUCKETS - REL_EXACT)).astype(np.int32)
    b = np.where(n < REL_EXACT, n, np.minimum(large, REL_BUCKETS - 1))
    return [int(np.argmax(b >= REL_EXACT + k)) for k in range(1, REL_BUCKETS - REL_EXACT)]


BUCKET_THRESHOLDS = _bucket_thresholds()


HBM_SPEC = pl.BlockSpec(memory_space=pl.ANY)


class _Carry:
    def __init__(self, ins, outs, sems, start, finish, mid=None, aliases=None):
        self.ins, self.outs, self.sems = list(ins), list(outs), list(sems)
        self.start, self.finish, self.mid, self.aliases = start, finish, mid, dict(aliases or {})


def _pcall(body, *, name, grid, in_specs, out_specs, out_shape, scratch=(), carry=None):
    params = pltpu.CompilerParams(dimension_semantics=("arbitrary",) * len(grid), vmem_limit_bytes=VMEM_LIMIT_V7X)
    if carry is None:
        return pl.pallas_call(body, name=name, grid=grid, in_specs=in_specs, out_specs=out_specs,
                              out_shape=out_shape, scratch_shapes=list(scratch), compiler_params=params)
    single = not isinstance(out_shape, (list, tuple))
    own_specs, own_shapes = ([out_specs], [out_shape]) if single else (list(out_specs), list(out_shape))
    n_in, n_out, n_scr = len(in_specs), len(own_shapes), len(scratch)
    n_cin, n_cout = len(carry.ins), len(carry.outs)
    steps = math.prod(grid)
    mid_step = max(steps - 1 - max(steps // 8, 1), 0)

    def carrying(*refs):
        ins, refs = refs[:n_in], refs[n_in:]
        cins, refs = refs[:n_cin], refs[n_cin:]
        outs, refs = refs[:n_out], refs[n_out:]
        couts, refs = refs[:n_cout], refs[n_cout:]
        scr, csems = refs[:n_scr], refs[n_scr:]
        step = 0
        for axis, size in enumerate(grid):
            step = step * size + pl.program_id(axis)

        @pl.when(step == 0)
        def _():
            carry.start(cins, couts, csems)

        body(*ins, *outs, *scr)
        if carry.mid is not None:
            @pl.when(step == mid_step)
            def _():
                carry.mid(cins, couts, csems)

        @pl.when(step == steps - 1)
        def _():
            carry.finish(cins, couts, csems)

    call = pl.pallas_call(carrying, name=name, grid=grid, in_specs=list(in_specs) + [HBM_SPEC] * n_cin,
                          out_specs=own_specs + [HBM_SPEC] * n_cout, out_shape=own_shapes + carry.outs,
                          scratch_shapes=list(scratch) + carry.sems, compiler_params=params,
                          input_output_aliases={n_in + i: n_out + o for i, o in carry.aliases.items()})

    def run(*args):
        res = call(*args, *carry.ins)
        return (res[0] if single else res[:n_out]), res[n_out:]

    return run


def _run_alone(carry, name):
    n_cin, n_cout = len(carry.ins), len(carry.outs)

    def body(*refs):
        cins, couts, csems = refs[:n_cin], refs[n_cin:n_cin + n_cout], refs[n_cin + n_cout:]
        carry.start(cins, couts, csems)
        if carry.mid is not None:
            carry.mid(cins, couts, csems)
        carry.finish(cins, couts, csems)

    return pl.pallas_call(body, name=name, in_specs=[HBM_SPEC] * n_cin, out_specs=[HBM_SPEC] * n_cout,
                          out_shape=carry.outs, scratch_shapes=carry.sems,
                          input_output_aliases=carry.aliases)(*carry.ins)


def _dot(a, b):
    return jnp.dot(a, b, preferred_element_type=F32)


def _dot_nt(a, b):
    return lax.dot_general(a, b, (((1,), (1,)), ((), ())), preferred_element_type=F32)


def _dot_tn(a, b):
    return lax.dot_general(a, b, (((0,), (0,)), ((), ())), preferred_element_type=F32)


def _sds(shape, dtype):
    return jax.ShapeDtypeStruct(tuple(shape), dtype)


ROW_CHUNK = 256


def _row_chunks(tm):
    return [slice(r, min(r + ROW_CHUNK, tm)) for r in range(0, tm, ROW_CHUNK)]


def _carried(call, args, carry):
    return call(*args) if carry is not None else (call(*args), ())


def _rmsnorm(x, g, name, carry=None):
    m, d = x.shape
    tm = min(512, m)

    def body(x_ref, g_ref, h_ref):
        xv = x_ref[...]
        r = lax.rsqrt(jnp.mean(xv * xv, axis=-1, keepdims=True) + EPS)
        h_ref[...] = (xv * r * g_ref[...]).astype(BF16)

    call = _pcall(body, name=name, grid=(m // tm,), carry=carry,
                  in_specs=[pl.BlockSpec((tm, d), lambda i: (i, 0)), pl.BlockSpec((1, d), lambda i: (0, 0))],
                  out_specs=pl.BlockSpec((tm, d), lambda i: (i, 0)), out_shape=_sds((m, d), BF16))
    return _carried(call, (x, g), carry)


def _mm_nn(a, b, name, tm=1024, bt=False, carry=None):
    m, k = a.shape
    nj = b.shape[0]
    n = b.shape[1] if bt else b.shape[2]
    tm = min(tm, m)
    dot = _dot_nt if bt else _dot

    def body(a_ref, b_ref, o_ref):
        o_ref[...] = dot(a_ref[...], b_ref[...]).astype(BF16)

    call = _pcall(body, name=name, grid=(nj, m // tm),
                  in_specs=[pl.BlockSpec((tm, k), lambda j, i: (i, 0)),
                            pl.BlockSpec((None,) + b.shape[1:], lambda j, i: (j, 0, 0))],
                  out_specs=pl.BlockSpec((None, tm, n), lambda j, i: (j, i, 0)),
                  out_shape=_sds((nj, m, n), BF16), carry=carry)
    return _carried(call, (a, b), carry)


def _load_once(src_hbm, dst_vmem, sem):
    @pl.when(pl.program_id(0) == 0)
    def _():
        load = pltpu.make_async_copy(src_hbm, dst_vmem, sem)
        load.start()
        load.wait()


def _resident(w):
    return [pltpu.VMEM(w.shape, w.dtype), pltpu.SemaphoreType.DMA(())]


def _ffn_up(h, w4, name, tm=512, carry=None):
    s, d = h.shape
    tm = min(tm, s)

    def body(h_ref, w_hbm, gu_ref, a_ref, w_ref, w_sem):
        _load_once(w_hbm, w_ref, w_sem)
        for p in range(4):
            for rows in _row_chunks(tm):
                hv = h_ref[rows, :]
                g = _dot_nt(hv, w_ref[0, p])
                u = _dot_nt(hv, w_ref[1, p])
                gu_ref[0, p, rows, :] = g.astype(BF16)
                gu_ref[1, p, rows, :] = u.astype(BF16)
                a_ref[p, rows, :] = (g * jax.nn.sigmoid(g) * u).astype(BF16)

    call = _pcall(body, name=name, grid=(s // tm,),
                  in_specs=[pl.BlockSpec((tm, d), lambda i: (i, 0)), HBM_SPEC],
                  out_specs=[pl.BlockSpec((2, 4, tm, FS), lambda i: (0, 0, i, 0)),
                             pl.BlockSpec((4, tm, FS), lambda i: (0, i, 0))],
                  out_shape=[_sds((2, 4, s, FS), BF16), _sds((4, s, FS), BF16)], scratch=_resident(w4), carry=carry)
    return _carried(call, (h, w4), carry)


def _in_proj(h, wa, wb, name, tm=512, carry=None):
    s, d = h.shape
    tm = min(tm, s)

    def body(h_ref, wa_hbm, wb_hbm, pa_ref, pb_ref, wa_ref, wa_sem, wb_ref, wb_sem):
        _load_once(wa_hbm, wa_ref, wa_sem)
        _load_once(wb_hbm, wb_ref, wb_sem)
        hv = h_ref[...]
        for w_ref, o_ref in ((wb_ref, pb_ref), (wa_ref, pa_ref)):
            for j in range(w_ref.shape[0]):
                o_ref[j] = _dot_nt(hv, w_ref[j]).astype(BF16)

    call = _pcall(body, name=name, grid=(s // tm,), carry=carry,
                  in_specs=[pl.BlockSpec((tm, d), lambda i: (i, 0)), HBM_SPEC, HBM_SPEC],
                  out_specs=[pl.BlockSpec((wa.shape[0], tm, d), lambda i: (0, i, 0)),
                             pl.BlockSpec((wb.shape[0], tm, d), lambda i: (0, i, 0))],
                  out_shape=[_sds((wa.shape[0], s, d), BF16), _sds((wb.shape[0], s, d), BF16)],
                  scratch=_resident(wa) + _resident(wb))
    return _carried(call, (h, wa, wb), carry)


def _mm_res_norm(a, w, xres, gain, scale, name, tm=512, carry=None):
    npart, s, kp = a.shape
    tm = min(tm, s)

    def body(a_ref, w_ref, x_ref, g_ref, xo_ref, h_ref):
        for rows in _row_chunks(tm):
            acc = _dot(a_ref[0, rows, :], w_ref[0])
            for p in range(1, npart):
                acc = acc + _dot(a_ref[p, rows, :], w_ref[p])
            xn = x_ref[rows, :] + scale * acc
            xo_ref[rows, :] = xn
            r = lax.rsqrt(jnp.mean(xn * xn, axis=-1, keepdims=True) + EPS)
            h_ref[rows, :] = (xn * r * g_ref[...]).astype(BF16)

    call = _pcall(body, name=name, grid=(s // tm,),
                  in_specs=[pl.BlockSpec((npart, tm, kp), lambda i: (0, i, 0)),
                            pl.BlockSpec((npart, kp, D), lambda i: (0, 0, 0)),
                            pl.BlockSpec((tm, D), lambda i: (i, 0)),
                            pl.BlockSpec((1, D), lambda i: (0, 0))],
                  out_specs=[pl.BlockSpec((tm, D), lambda i: (i, 0)), pl.BlockSpec((tm, D), lambda i: (i, 0))],
                  out_shape=[_sds((s, D), F32), _sds((s, D), BF16)], carry=carry)
    return _carried(call, (a, w, xres, gain), carry)


def _ffn_down_loss(a, w, xres, gain, target, name, tm=512):
    npart, s, kp = a.shape
    tm = min(tm, s)

    def body(a_ref, w_ref, x_ref, g_ref, t_ref, dx_ref, dxb_ref, loss_ref, dg_ref):
        @pl.when(pl.program_id(0) == 0)
        def _():
            loss_ref[...] = jnp.zeros_like(loss_ref)
            dg_ref[...] = jnp.zeros_like(dg_ref)

        for rows in _row_chunks(tm):
            acc = _dot(a_ref[0, rows, :], w_ref[0])
            for p in range(1, npart):
                acc = acc + _dot(a_ref[p, rows, :], w_ref[p])
            xn = x_ref[rows, :] + 0.5 * acc
            r = lax.rsqrt(jnp.mean(xn * xn, axis=-1, keepdims=True) + EPS)
            xh = xn * r
            gv = g_ref[...]
            err = xh * gv - t_ref[rows, :]
            part = 0.5 * jnp.sum(jnp.mean(err * err, axis=-1, keepdims=True), axis=0, keepdims=True)
            dy = err * (1.0 / D)
            dyg = dy * gv
            dxn = r * (dyg - xh * jnp.mean(dyg * xh, axis=-1, keepdims=True))
            dx_ref[rows, :] = dxn
            dxb_ref[rows, :] = dxn.astype(BF16)
            loss_ref[...] += jnp.broadcast_to(part, loss_ref.shape)
            dg_ref[...] += jnp.sum(dy * xh, axis=0, keepdims=True)

    return _pcall(body, name=name, grid=(s // tm,),
                  in_specs=[pl.BlockSpec((npart, tm, kp), lambda i: (0, i, 0)),
                            pl.BlockSpec((npart, kp, D), lambda i: (0, 0, 0)),
                            pl.BlockSpec((tm, D), lambda i: (i, 0)),
                            pl.BlockSpec((1, D), lambda i: (0, 0)),
                            pl.BlockSpec((tm, D), lambda i: (i, 0))],
                  out_specs=[pl.BlockSpec((tm, D), lambda i: (i, 0)), pl.BlockSpec((tm, D), lambda i: (i, 0)),
                             pl.BlockSpec((8, 128), lambda i: (0, 0)), pl.BlockSpec((1, D), lambda i: (0, 0))],
                  out_shape=[_sds((s, D), F32), _sds((s, D), BF16), _sds((8, 128), F32), _sds((1, D), F32)],
                  )(a, w, xres, gain, target)


def _window_tiles():
    i = lax.broadcasted_iota(jnp.int32, (BLK, BLK), 0)
    j = lax.broadcasted_iota(jnp.int32, (BLK, BLK), 1)
    rel = (i - j) & (BLK - 1)
    large = jnp.full_like(rel, REL_EXACT)
    for t in BUCKET_THRESHOLDS:
        large = large + (rel >= t).astype(jnp.int32)
    return j <= i, jnp.where(rel < REL_EXACT, rel, large)


def _bias_build(rel_bias, name):
    def body(rb_ref, o_ref):
        _, bucket = _window_tiles()

        def per_head(h, carry):
            acc = jnp.zeros((BLK, BLK), F32)
            for b in range(REL_BUCKETS):
                acc = jnp.where(bucket == b, rb_ref[b, h], acc)
            o_ref[h] = acc
            return carry

        lax.fori_loop(0, N_HEADS, per_head, 0)

    return _pcall(body, name=name, grid=(1,),
                  in_specs=[pl.BlockSpec(memory_space=pltpu.SMEM)],
                  out_specs=pl.BlockSpec((N_HEADS, BLK, BLK), lambda i: (0, 0, 0)),
                  out_shape=_sds((N_HEADS, BLK, BLK), F32))(rel_bias)


def _bias_bwd(dbias, name):
    def body(db_ref, o_ref):
        _, bucket = _window_tiles()
        lane = lax.broadcasted_iota(jnp.int32, (N_HEADS, 128), 1)

        def per_bucket(b, out):
            mb = (bucket == b).astype(F32)
            per_col = jnp.sum(db_ref[...] * mb[None, :, :], axis=1)
            return jnp.where(lane == b, jnp.sum(per_col, axis=1, keepdims=True), out)

        o_ref[...] = lax.fori_loop(0, REL_BUCKETS, per_bucket, jnp.zeros((N_HEADS, 128), F32))

    return _pcall(body, name=name, grid=(1,),
                  in_specs=[pl.BlockSpec((N_HEADS, BLK, BLK), lambda i: (0, 0, 0))],
                  out_specs=pl.BlockSpec((N_HEADS, 128), lambda i: (0, 0)),
                  out_shape=_sds((N_HEADS, 128), F32))(dbias)


PAIR = 2 * HEAD
GROUP = N_HEADS // N_KV
SWA_SCALE = HEAD ** -0.5


def _window_masks(n):
    i = lax.broadcasted_iota(jnp.int32, (GROUP * BLK, BLK), 0) & (BLK - 1)
    j = lax.broadcasted_iota(jnp.int32, (GROUP * BLK, BLK), 1)
    return j <= i, jnp.logical_and(n == 0, j > i), j < HEAD


def _kv_twice(ref, base, g, low):
    slab = ref[:, base + PAIR * (g // 2): base + PAIR * (g // 2 + 1)]
    swapped = pltpu.roll(slab, HEAD, 1)
    return jnp.where(low, slab, swapped) if g % 2 == 0 else jnp.where(low, swapped, slab)


def _stack_heads(ref, g, low):
    parts = []
    for r in range(2):
        slab = ref[:, PAIR * (2 * g + r): PAIR * (2 * g + r + 1)]
        zero = jnp.zeros_like(slab)
        parts += [jnp.where(low, slab, zero), jnp.where(low, zero, slab)]
    return jnp.concatenate(parts, axis=0)


def _unstack_heads(t, low):
    return [jnp.where(low, t[2 * r * BLK:(2 * r + 1) * BLK], t[(2 * r + 1) * BLK:(2 * r + 2) * BLK])
            for r in range(2)]


def _head_rows(t, k):
    return t[k * BLK:(k + 1) * BLK]


def _per_head_column(values):
    head = lax.broadcasted_iota(jnp.int32, (GROUP * BLK, 1), 0) // BLK
    col = jnp.full((GROUP * BLK, 1), values[0], F32)
    for k in range(1, GROUP):
        col = jnp.where(head == k, values[k], col)
    return col


def _window_logits(q4, kc, kp, bias4, own, absent):
    sc = jnp.where(own, _dot_nt(q4, kc), _dot_nt(q4, kp)) * SWA_SCALE + bias4
    return jnp.where(absent, NEG, sc)


def _split_window(t, own):
    zero = jnp.zeros_like(t)
    return jnp.where(own, t, zero), jnp.where(own, zero, t)


def _swa_fwd(pb, bias, sinks, name, carry=None):
    _, s, _ = pb.shape
    nb = s // BLK
    kvw = 2 * N_KV * HEAD

    def body(q_ref, kc_ref, kp_ref, b_ref, sk_ref, o_ref, lse_ref):
        own, absent, low4 = _window_masks(pl.program_id(0))
        low = low4[:BLK]
        lane = lax.broadcasted_iota(jnp.int32, (BLK, 128), 1)
        lse_t = jnp.zeros((BLK, 128), F32)
        for g in range(N_KV):
            q4 = _stack_heads(q_ref, g, low)
            kc, kp = _kv_twice(kc_ref, 0, g, low), _kv_twice(kp_ref, 0, g, low)
            vc, vp = _kv_twice(kc_ref, N_KV * HEAD, g, low), _kv_twice(kp_ref, N_KV * HEAD, g, low)
            bias4 = b_ref[GROUP * g:GROUP * (g + 1)].reshape(GROUP * BLK, BLK)
            sc = _window_logits(q4, kc, kp, bias4, own, absent)
            sk = _per_head_column([sk_ref[0, GROUP * g + k] for k in range(GROUP)])
            m = jnp.maximum(jnp.max(sc, axis=1, keepdims=True), sk)
            p = jnp.exp(sc - m)
            l = jnp.sum(p, axis=1, keepdims=True) + jnp.exp(sk - m)
            p_own, p_prev = _split_window(p.astype(BF16), own)
            out = (_dot(p_own, vc) + _dot(p_prev, vp)) * (1.0 / l)
            for r, slab in enumerate(_unstack_heads(out, low)):
                o_ref[:, PAIR * (2 * g + r): PAIR * (2 * g + r + 1)] = slab.astype(BF16)
            lse4 = m + jnp.log(l)
            for k in range(GROUP):
                lse_t = jnp.where(lane == GROUP * g + k, _head_rows(lse4, k), lse_t)
        lse_ref[...] = lse_t

    call = _pcall(body, name=name, grid=(nb,),
                  in_specs=[pl.BlockSpec((None, BLK, D), lambda n: (0, n, 0)),
                            pl.BlockSpec((None, BLK, kvw), lambda n: (1, n, 0)),
                            pl.BlockSpec((None, BLK, kvw), lambda n: (1, jnp.maximum(n - 1, 0), 0)),
                            pl.BlockSpec((N_HEADS, BLK, BLK), lambda n: (0, 0, 0)),
                            pl.BlockSpec(memory_space=pltpu.SMEM)],
                  out_specs=[pl.BlockSpec((BLK, D), lambda n: (n, 0)), pl.BlockSpec((BLK, 128), lambda n: (n, 0))],
                  out_shape=[_sds((s, D), BF16), _sds((s, 128), F32)], carry=carry)
    return _carried(call, (pb, pb, pb, bias, sinks), carry)


def _fold_halves(t, g, low):
    folded = jnp.where(low, t, 0.0) + pltpu.roll(jnp.where(low, 0.0, t), HEAD, 1)
    return folded if g % 2 == 0 else pltpu.roll(folded, HEAD, 1)


def _swa_bwd(pb, attn, dattn, lse, bias, sinks, name, carry=None):
    _, s, _ = pb.shape
    nb = s // BLK
    kvw = 2 * N_KV * HEAD
    voff = N_KV * HEAD

    def body(q_ref, kc_ref, kp_ref, o_ref, do_ref, lse_ref, b_ref, skrow_ref, dpb_ref, dbias_ref, dsk_ref,
             dq_hold, kv_hold, dq_new, kv_prev, kv_cur):
        n = pl.program_id(0)

        @pl.when(n == 0)
        def _():
            dbias_ref[...] = jnp.zeros_like(dbias_ref)
            dsk_ref[...] = jnp.zeros_like(dsk_ref)
            dq_hold[...] = jnp.zeros_like(dq_hold)
            kv_hold[...] = jnp.zeros_like(kv_hold)

        @pl.when(n < nb)
        def _():
            own, absent, low4 = _window_masks(n)
            low = low4[:BLK]
            lane = lax.broadcasted_iota(jnp.int32, (BLK, 128), 1)
            delta_t = jnp.zeros((BLK, 128), F32)
            ones = jnp.ones((PAIR, 128), BF16)
            for pair_of_kv in range(N_KV // 2):
                slab_grads = [jnp.zeros((BLK, PAIR), F32) for _ in range(4)]
                for g in (2 * pair_of_kv, 2 * pair_of_kv + 1):
                    q4, do4 = _stack_heads(q_ref, g, low), _stack_heads(do_ref, g, low)
                    kc, kp = _kv_twice(kc_ref, 0, g, low), _kv_twice(kp_ref, 0, g, low)
                    vc, vp = _kv_twice(kc_ref, voff, g, low), _kv_twice(kp_ref, voff, g, low)
                    o_slabs = [o_ref[:, PAIR * (2 * g + r): PAIR * (2 * g + r + 1)] for r in range(2)]
                    o4 = jnp.concatenate([o_slabs[0], o_slabs[0], o_slabs[1], o_slabs[1]], axis=0)
                    delta = _dot(do4 * o4, ones)
                    heads = range(GROUP * g, GROUP * (g + 1))
                    lse4 = jnp.concatenate([lse_ref[:, h:h + 1] for h in heads], axis=0)
                    bias4 = b_ref[GROUP * g:GROUP * (g + 1)].reshape(GROUP * BLK, BLK)
                    p = jnp.exp(_window_logits(q4, kc, kp, bias4, own, absent) - lse4)
                    dp = jnp.where(own, _dot_nt(do4, vc), _dot_nt(do4, vp))
                    ds = p * (dp - delta)
                    dbias_ref[GROUP * g:GROUP * (g + 1)] += ds.reshape(GROUP, BLK, BLK)
                    for k, h in enumerate(heads):
                        delta_t = jnp.where(lane == h, _head_rows(delta, k), delta_t)
                    ds_own, ds_prev = _split_window((ds * SWA_SCALE).astype(BF16), own)
                    p_own, p_prev = _split_window(p.astype(BF16), own)
                    dq4 = _dot(ds_own, kc) + _dot(ds_prev, kp)
                    for r, slab in enumerate(_unstack_heads(dq4, low)):
                        dq_new[:, PAIR * (2 * g + r): PAIR * (2 * g + r + 1)] = slab
                    grads = [_dot_tn(ds_own, q4), _dot_tn(ds_prev, q4), _dot_tn(p_own, do4), _dot_tn(p_prev, do4)]
                    slab_grads = [t + _fold_halves(dk, g, low) for t, dk in zip(slab_grads, grads)]
                ks = slice(PAIR * pair_of_kv, PAIR * (pair_of_kv + 1))
                vs = slice(voff + PAIR * pair_of_kv, voff + PAIR * (pair_of_kv + 1))
                kv_cur[:, ks], kv_prev[:, ks], kv_cur[:, vs], kv_prev[:, vs] = slab_grads
            dsk_ref[...] -= jnp.sum(jnp.exp(skrow_ref[...] - lse_ref[...]) * delta_t, axis=0, keepdims=True)

        @pl.when(n == nb)
        def _():
            kv_prev[...] = jnp.zeros_like(kv_prev)

        dpb_ref[0] = dq_hold[...].astype(BF16)
        dpb_ref[1, :, 0:kvw] = (kv_hold[...] + kv_prev[...]).astype(BF16)
        dpb_ref[1, :, kvw:D] = jnp.zeros((BLK, D - kvw), BF16)

        @pl.when(n < nb)
        def _():
            dq_hold[...] = dq_new[...]
            kv_hold[...] = kv_cur[...]

    def cur(n):
        return jnp.minimum(n, nb - 1)

    call = _pcall(body, name=name, grid=(nb + 1,), carry=carry,
                  in_specs=[pl.BlockSpec((None, BLK, D), lambda n: (0, cur(n), 0)),
                            pl.BlockSpec((None, BLK, kvw), lambda n: (1, cur(n), 0)),
                            pl.BlockSpec((None, BLK, kvw), lambda n: (1, jnp.maximum(cur(n) - 1, 0), 0)),
                            pl.BlockSpec((BLK, D), lambda n: (cur(n), 0)),
                            pl.BlockSpec((BLK, D), lambda n: (cur(n), 0)),
                            pl.BlockSpec((BLK, 128), lambda n: (cur(n), 0)),
                            pl.BlockSpec((N_HEADS, BLK, BLK), lambda n: (0, 0, 0)),
                            pl.BlockSpec((1, 128), lambda n: (0, 0))],
                  out_specs=[pl.BlockSpec((2, BLK, D), lambda n: (0, jnp.maximum(n - 1, 0), 0)),
                             pl.BlockSpec((N_HEADS, BLK, BLK), lambda n: (0, 0, 0)),
                             pl.BlockSpec((1, 128), lambda n: (0, 0))],
                  out_shape=[_sds((2, s, D), BF16), _sds((N_HEADS, BLK, BLK), F32), _sds((1, 128), F32)],
                  scratch=[pltpu.VMEM((BLK, D), F32), pltpu.VMEM((BLK, kvw), F32), pltpu.VMEM((BLK, D), F32),
                           pltpu.VMEM((BLK, kvw), F32), pltpu.VMEM((BLK, kvw), F32)])
    sink_row = jnp.pad(sinks, ((0, 0), (0, 128 - N_HEADS)))
    return _carried(call, (pb, pb, pb, attn, dattn, lse, bias, sink_row), carry)


HALO = 16
CW = D


def _conv_taps(cu, halo_cu, first_tile):
    row = lax.broadcasted_iota(jnp.int32, cu.shape, 0)
    halo_cu = jnp.where(first_tile, 0.0, halo_cu)
    c1 = jnp.where(row == 0, halo_cu[HALO - 1:HALO], pltpu.roll(cu, 1, 0))
    c2 = jnp.where(row == 0, halo_cu[HALO - 2:HALO - 1],
                   jnp.where(row == 1, halo_cu[HALO - 1:HALO], pltpu.roll(cu, 2, 0)))
    return c1, c2


def _conv_merge_fwd(pa, attn, convw, name, ts=256):
    _, s, _ = pa.shape
    ts = min(ts, s)
    hb = ts // HALO

    def body(pa_ref, hp_ref, at_ref, w_ref, o_ref):
        i = pl.program_id(1)
        cu = pa_ref[0].astype(F32) * pa_ref[2].astype(F32)
        c1, c2 = _conv_taps(cu, hp_ref[0].astype(F32) * hp_ref[2].astype(F32), i == 0)
        w = w_ref[...]
        c3 = w[0:1] * c2 + w[1:2] * c1 + w[2:3] * cu
        conv = pa_ref[1].astype(F32) * c3
        o_ref[...] = (jax.nn.sigmoid(pa_ref[3].astype(F32)) * at_ref[...].astype(F32)
                      + jax.nn.sigmoid(pa_ref[4].astype(F32)) * conv).astype(BF16)

    return _pcall(body, name=name, grid=(D // CW, s // ts),
                  in_specs=[pl.BlockSpec((5, ts, CW), lambda c, i: (0, i, c)),
                            pl.BlockSpec((5, HALO, CW), lambda c, i: (0, jnp.maximum(i * hb - 1, 0), c)),
                            pl.BlockSpec((ts, CW), lambda c, i: (i, c)),
                            pl.BlockSpec((8, CW), lambda c, i: (0, c))],
                  out_specs=pl.BlockSpec((ts, CW), lambda c, i: (i, c)),
                  out_shape=_sds((s, D), BF16))(pa, pa, attn, convw)


def _conv_merge_bwd(dmerged, pa, attn, convw, name, ts=256, carry=None):
    _, s, _ = pa.shape
    ts = min(ts, s)
    hb = ts // HALO
    last_hb = s // HALO - 1

    def body(dm_ref, pa_ref, at_ref, w_ref, hp_ref, hn_ref, dmn_ref, dat_ref, dpa_ref, dw_ref):
        i = pl.program_id(1)
        last = i == pl.num_programs(1) - 1
        dm = dm_ref[...].astype(F32)
        cp, bp, u = pa_ref[0].astype(F32), pa_ref[1].astype(F32), pa_ref[2].astype(F32)
        sa = jax.nn.sigmoid(pa_ref[3].astype(F32))
        sc = jax.nn.sigmoid(pa_ref[4].astype(F32))
        at = at_ref[...].astype(F32)
        cu = cp * u
        c1, c2 = _conv_taps(cu, hp_ref[0].astype(F32) * hp_ref[2].astype(F32), i == 0)
        w = w_ref[...]
        c3 = w[0:1] * c2 + w[1:2] * c1 + w[2:3] * cu
        dconv = dm * sc
        dc3 = dconv * bp
        nxt = dmn_ref[...].astype(F32) * jax.nn.sigmoid(hn_ref[4].astype(F32)) * hn_ref[1].astype(F32)
        nxt = jnp.where(last, 0.0, nxt)
        row = lax.broadcasted_iota(jnp.int32, dc3.shape, 0)
        d1 = jnp.where(row == ts - 1, nxt[0:1], pltpu.roll(dc3, ts - 1, 0))
        d2 = jnp.where(row == ts - 2, nxt[0:1], jnp.where(row == ts - 1, nxt[1:2], pltpu.roll(dc3, ts - 2, 0)))
        dcu = w[2:3] * dc3 + w[1:2] * d1 + w[0:1] * d2
        dat_ref[...] = (dm * sa).astype(BF16)
        dpa_ref[0] = (dcu * u).astype(BF16)
        dpa_ref[1] = (dconv * c3).astype(BF16)
        dpa_ref[2] = (dcu * cp).astype(BF16)
        dpa_ref[3] = (dm * at * sa * (1.0 - sa)).astype(BF16)
        dpa_ref[4] = (dm * bp * c3 * sc * (1.0 - sc)).astype(BF16)

        @pl.when(i == 0)
        def _():
            dw_ref[...] = jnp.zeros_like(dw_ref)

        dw_ref[0:1, :] += jnp.sum(dc3 * c2, axis=0, keepdims=True)
        dw_ref[1:2, :] += jnp.sum(dc3 * c1, axis=0, keepdims=True)
        dw_ref[2:3, :] += jnp.sum(dc3 * cu, axis=0, keepdims=True)

    call = _pcall(body, name=name, grid=(D // CW, s // ts), carry=carry,
                  in_specs=[pl.BlockSpec((ts, CW), lambda c, i: (i, c)),
                            pl.BlockSpec((5, ts, CW), lambda c, i: (0, i, c)),
                            pl.BlockSpec((ts, CW), lambda c, i: (i, c)),
                            pl.BlockSpec((8, CW), lambda c, i: (0, c)),
                            pl.BlockSpec((5, HALO, CW), lambda c, i: (0, jnp.maximum(i * hb - 1, 0), c)),
                            pl.BlockSpec((5, HALO, CW), lambda c, i: (0, jnp.minimum((i + 1) * hb, last_hb), c)),
                            pl.BlockSpec((HALO, CW), lambda c, i: (jnp.minimum((i + 1) * hb, last_hb), c))],
                  out_specs=[pl.BlockSpec((ts, CW), lambda c, i: (i, c)),
                             pl.BlockSpec((5, ts, CW), lambda c, i: (0, i, c)),
                             pl.BlockSpec((8, CW), lambda c, i: (0, c))],
                  out_shape=[_sds((s, D), BF16), _sds((5, s, D), BF16), _sds((8, D), F32)])
    return _carried(call, (dmerged, pa, attn, convw, pa, pa, dmerged), carry)


def _xattn_fwd(q, kv, name, tq=512):
    s, _ = q.shape
    nm = kv.shape[1]
    tq = min(tq, s)

    def body(q_ref, kv_ref, o_ref, lse_ref):
        lane = lax.broadcasted_iota(jnp.int32, (tq, 128), 1)
        lse_t = jnp.zeros((tq, 128), F32)
        for h in range(XH):
            hs = slice(XHD * h, XHD * (h + 1))
            sc = _dot_nt(q_ref[:, hs], kv_ref[h]) * (XHD ** -0.5)
            m = jnp.max(sc, axis=1, keepdims=True)
            p = jnp.exp(sc - m)
            l = jnp.sum(p, axis=1, keepdims=True)
            o_ref[:, hs] = (_dot(p.astype(BF16), kv_ref[XH + h]) * (1.0 / l)).astype(BF16)
            lse_t = jnp.where(lane == h, m + jnp.log(l), lse_t)
        lse_ref[...] = lse_t

    return _pcall(body, name=name, grid=(s // tq,),
                  in_specs=[pl.BlockSpec((tq, D), lambda i: (i, 0)), pl.BlockSpec((2 * XH, nm, XHD), lambda i: (0, 0, 0))],
                  out_specs=[pl.BlockSpec((tq, D), lambda i: (i, 0)), pl.BlockSpec((tq, 128), lambda i: (i, 0))],
                  out_shape=[_sds((s, D), BF16), _sds((s, 128), F32)])(q, kv)


def _xattn_bwd(q, kv, o, do, lse, name, tq=512, carry=None):
    s, _ = q.shape
    nm = kv.shape[1]
    tq = min(tq, s)

    def body(q_ref, kv_ref, o_ref, do_ref, lse_ref, dq_ref, dkv_ref):
        @pl.when(pl.program_id(0) == 0)
        def _():
            dkv_ref[...] = jnp.zeros_like(dkv_ref)

        for h in range(XH):
            hs = slice(XHD * h, XHD * (h + 1))
            qh, kh, vh, dob = q_ref[:, hs], kv_ref[h], kv_ref[XH + h], do_ref[:, hs]
            p = jnp.exp(_dot_nt(qh, kh) * (XHD ** -0.5) - lse_ref[:, h:h + 1])
            dp = _dot_nt(dob, vh)
            delta = jnp.sum(dob.astype(F32) * o_ref[:, hs].astype(F32), axis=1, keepdims=True)
            dsb = (p * (dp - delta) * (XHD ** -0.5)).astype(BF16)
            dq_ref[:, hs] = _dot(dsb, kh).astype(BF16)
            dkv_ref[h] += _dot_tn(dsb, qh)
            dkv_ref[XH + h] += _dot_tn(p.astype(BF16), dob)

    call = _pcall(body, name=name, grid=(s // tq,), carry=carry,
                  in_specs=[pl.BlockSpec((tq, D), lambda i: (i, 0)), pl.BlockSpec((2 * XH, nm, XHD), lambda i: (0, 0, 0)),
                            pl.BlockSpec((tq, D), lambda i: (i, 0)), pl.BlockSpec((tq, D), lambda i: (i, 0)),
                            pl.BlockSpec((tq, 128), lambda i: (i, 0))],
                  out_specs=[pl.BlockSpec((tq, D), lambda i: (i, 0)), pl.BlockSpec((2 * XH, nm, XHD), lambda i: (0, 0, 0))],
                  out_shape=[_sds((s, D), BF16), _sds((2 * XH, nm, XHD), F32)])
    return _carried(call, (q, kv, o, do, lse), carry)


def _ffn_down_bwd(dxb, wd4, gu4, name, tm=512, carry=None):
    s, _ = dxb.shape
    tm = min(tm, s)

    def body(dx_ref, w_hbm, gu_ref, o_ref, w_ref, w_sem):
        _load_once(w_hbm, w_ref, w_sem)
        for p in range(4):
            for rows in _row_chunks(tm):
                da = _dot_nt(dx_ref[rows, :], w_ref[p])
                g = gu_ref[0, p, rows, :].astype(F32)
                u = gu_ref[1, p, rows, :].astype(F32)
                sg = jax.nn.sigmoid(g)
                t = da * sg
                o_ref[0, p, rows, :] = (t * u * (1.0 + g - g * sg)).astype(BF16)
                o_ref[1, p, rows, :] = (t * g).astype(BF16)

    block = pl.BlockSpec((2, 4, tm, FS), lambda i: (0, 0, i, 0))
    call = _pcall(body, name=name, grid=(s // tm,), carry=carry,
                  in_specs=[pl.BlockSpec((tm, D), lambda i: (i, 0)), HBM_SPEC, block],
                  out_specs=block, out_shape=_sds((2, 4, s, FS), BF16), scratch=_resident(wd4))
    return _carried(call, (dxb, wd4, gu4), carry)


def _mm_tn(a, b, name, scale=1.0, carry=None):
    pa_n, s, m = a.shape
    pb_n, _, n = b.shape
    po = max(pa_n, pb_n)
    tn = n if po >= 4 else min(n, 256)

    def body(a_ref, b_ref, o_ref):
        o_ref[...] = (scale * _dot_tn(a_ref[...], b_ref[...])).astype(BF16)

    call = _pcall(body, name=name, grid=(po, n // tn), carry=carry,
                  in_specs=[pl.BlockSpec((None, s, m), lambda o, j: (o if pa_n > 1 else 0, 0, 0)),
                            pl.BlockSpec((None, s, tn), lambda o, j: (o if pb_n > 1 else 0, 0, j))],
                  out_specs=pl.BlockSpec((None, m, tn), lambda o, j: (o, 0, j)),
                  out_shape=_sds((po, m, n), BF16))
    return _carried(call, (a, b), carry)


def _sum_dots(a_ref, b_ref, nj, bt, rows=slice(None)):
    dot = _dot_nt if bt else _dot
    acc = dot(a_ref[0, rows, :], b_ref[0])
    for j in range(1, nj):
        acc = acc + dot(a_ref[j, rows, :], b_ref[j])
    return acc


def _mm_acc(a, b, name, out_dtype, tm=512, bt=False, carry=None):
    nj, s, k = a.shape
    n = b.shape[1] if bt else b.shape[2]
    tm = min(tm, s)

    def body(a_ref, b_ref, o_ref):
        o_ref[...] = _sum_dots(a_ref, b_ref, nj, bt).astype(out_dtype)

    call = _pcall(body, name=name, grid=(s // tm,), carry=carry,
                  in_specs=[pl.BlockSpec((nj, tm, k), lambda i: (0, i, 0)),
                            pl.BlockSpec(b.shape, lambda i: (0, 0, 0))],
                  out_specs=pl.BlockSpec((tm, n), lambda i: (i, 0)), out_shape=_sds((s, n), out_dtype))
    return _carried(call, (a, b), carry)


def _mm_acc_rms_bwd(pairs, name, *, x, gain, dres, scale=None, tm=512, bt=False, carry=None):
    npairs = len(pairs)
    s = pairs[0][0].shape[1]
    n = pairs[0][1].shape[1] if bt else pairs[0][1].shape[2]
    tm = min(tm, s)

    def body(*refs):
        a_refs, b_hbms = refs[:npairs], refs[npairs:2 * npairs]
        x_ref, g_ref, r_ref, dx_ref, dxb_ref, dg_ref = refs[2 * npairs:2 * npairs + 6]
        resident = refs[2 * npairs + 6:]
        b_refs = resident[0::2]
        for b_hbm, b_ref, b_sem in zip(b_hbms, b_refs, resident[1::2]):
            _load_once(b_hbm, b_ref, b_sem)

        @pl.when(pl.program_id(0) == 0)
        def _():
            dg_ref[...] = jnp.zeros_like(dg_ref)

        for rows in _row_chunks(tm):
            dh = _sum_dots(a_refs[0], b_refs[0], pairs[0][0].shape[0], bt, rows)
            for a_ref, b_ref, (a, _) in zip(a_refs[1:], b_refs[1:], pairs[1:]):
                dh = dh + _sum_dots(a_ref, b_ref, a.shape[0], bt, rows)
            if scale is not None:
                dh = scale * dh
            xv = x_ref[rows, :]
            r = lax.rsqrt(jnp.mean(xv * xv, axis=-1, keepdims=True) + EPS)
            xh = xv * r
            dyg = dh * g_ref[...]
            dx = r_ref[rows, :] + r * (dyg - xh * jnp.mean(dyg * xh, axis=-1, keepdims=True))
            dx_ref[rows, :] = dx
            dxb_ref[rows, :] = dx.astype(BF16)
            dg_ref[...] += jnp.sum(dh * xh, axis=0, keepdims=True)

    row = pl.BlockSpec((tm, n), lambda i: (i, 0))
    in_specs = ([pl.BlockSpec((a.shape[0], tm, a.shape[2]), lambda i: (0, i, 0)) for a, _ in pairs]
                + [HBM_SPEC] * npairs + [row, pl.BlockSpec((1, n), lambda i: (0, 0)), row])
    args = tuple(a for a, _ in pairs) + tuple(b for _, b in pairs) + (x, gain, dres)
    call = _pcall(body, name=name, grid=(s // tm,), in_specs=in_specs, carry=carry,
                  out_specs=[row, row, pl.BlockSpec((1, n), lambda i: (0, 0))],
                  out_shape=[_sds((s, n), F32), _sds((s, n), BF16), _sds((1, n), F32)],
                  scratch=[t for _, b in pairs for t in _resident(b)])
    return _carried(call, args, carry)


def _adam(w, g, m, v):
    m2 = ADAM_B1 * m + (1.0 - ADAM_B1) * g
    v2 = ADAM_B2 * v + (1.0 - ADAM_B2) * (g * g)
    m_hat = m2 / (1.0 - ADAM_B1 ** ADAM_STEP)
    v_hat = v2 / (1.0 - ADAM_B2 ** ADAM_STEP)
    delta = -ADAM_LR * (m_hat / (jnp.sqrt(v_hat) + ADAM_EPS) + ADAM_WD * w)
    return delta, m2, v2


def _adamw(parts, w, m, v, name):
    _, r, c = parts.shape
    tr = max(t for t in range(16, 257, 16) if r % t == 0)

    def body(p_ref, w_ref, m_ref, v_ref, g_ref, d_ref, m2_ref, v2_ref):
        g = p_ref[0].astype(F32)
        for i in range(1, N_DEV):
            g = g + p_ref[i].astype(F32)
        delta, m2, v2 = _adam(w_ref[...], g, m_ref[...], v_ref[...])
        g_ref[...] = g
        d_ref[...] = delta
        m2_ref[...] = m2
        v2_ref[...] = v2

    blk = pl.BlockSpec((tr, c), lambda i: (i, 0))
    return _pcall(body, name=name, grid=(r // tr,),
                  in_specs=[pl.BlockSpec((N_DEV, tr, c), lambda i: (0, i, 0)), blk, blk, blk],
                  out_specs=[blk] * 4, out_shape=[_sds((r, c), F32)] * 4)(parts, w, m, v)


def _position():
    return lax.axis_index("x"), lax.axis_index("y"), lax.axis_index("c")


def _slot(px, py, pc):
    return 4 * px + 2 * py + pc


def _row_window(ref, rows):
    r0, r1 = rows
    return ref if (r0, r1) == (0, ref.shape[0]) else ref.at[pl.ds(r0, r1 - r0)]


def _split_items(items):
    sources = [src for src, _, _ in items]
    begun = [(a, dest) for a, (_, _, dest) in enumerate(items) if dest is not None]
    aliases = {len(sources) + k: a for k, (a, _) in enumerate(begun)}
    return sources + [dest for _, dest in begun], [rows for _, rows, _ in items], aliases


def _gather_carry(items):
    na = len(items)
    carry_ins, windows, aliases = _split_items(items)

    def plan(ins, outs, sems):
        send_sems, recv_sems, local_sems = sems
        x, y, c = _position()
        me, sibling = (x, y, c), (x, y, 1 - c)
        chips = [(1 - x, y), (x, 1 - y), (1 - x, 1 - y)]
        ins = [_row_window(ins[a], windows[a]) for a in range(na)]

        def block_rows(a, block):
            return _row_window(outs[a].at[_slot(*block)], windows[a])

        def copy(a, k, block, to, src=None):
            rows = block_rows(a, block)
            return pltpu.make_async_remote_copy(src_ref=rows if src is None else src, dst_ref=rows,
                                                send_sem=send_sems.at[k, a], recv_sem=recv_sems.at[k, a],
                                                device_id=to, device_id_type=MESH)

        mine = [pltpu.make_async_copy(ins[a], block_rows(a, me), local_sems.at[a]) for a in range(na)]
        first = [copy(a, 0, me, sibling, src=ins[a]) for a in range(na)]
        for j, chip in enumerate(chips):
            first += [copy(a, 1 + j, me, (*chip, c), src=ins[a]) for a in range(na)]
        landed = [[copy(a, 1 + j, (*chip, c), me) for a in range(na)] for j, chip in enumerate(chips)]
        passed = [[copy(a, 4 + j, (*chip, c), sibling) for a in range(na)] for j, chip in enumerate(chips)]
        from_sibling = [copy(a, 0, sibling, me) for a in range(na)]
        for j, chip in enumerate(chips):
            from_sibling += [copy(a, 4 + j, (*chip, 1 - c), me) for a in range(na)]
        return mine, first, landed, passed, from_sibling

    def start(ins, outs, sems):
        mine, first, _, _, _ = plan(ins, outs, sems)
        for cp in mine + first:
            cp.start()

    def mid(ins, outs, sems):
        _, _, landed, passed, _ = plan(ins, outs, sems)
        for over_ici, onward in zip(landed, passed):
            for cp, fwd in zip(over_ici, onward):
                cp.wait_recv()
                fwd.start()

    def finish(ins, outs, sems):
        mine, first, _, passed, from_sibling = plan(ins, outs, sems)
        for cp in from_sibling:
            cp.wait_recv()
        for cp in first + [fwd for onward in passed for fwd in onward]:
            cp.wait_send()
        for cp in mine:
            cp.wait()

    return _Carry(carry_ins, [_sds((N_DEV,) + src.shape, src.dtype) for src, _, _ in items],
                  [pltpu.SemaphoreType.DMA((7, na)), pltpu.SemaphoreType.DMA((7, na)),
                   pltpu.SemaphoreType.DMA((na,))], start, finish, mid, aliases)


def _exchange_carry(scattered, replicated=()):
    items = list(scattered) + [(a, (0, a.shape[0]), None) for a in replicated]
    na, ns = len(items), len(scattered)
    carry_ins, windows, aliases = _split_items(items)

    def plan(ins, outs, sems):
        send_sems, recv_sems, local_sems = sems
        me = _slot(*_position())

        def source(a, j):
            return _row_window(ins[a].at[j] if a < ns else ins[a], windows[a])

        def copy(a, j, i):
            return pltpu.make_async_remote_copy(src_ref=source(a, j), dst_ref=_row_window(outs[a].at[i], windows[a]),
                                                send_sem=send_sems.at[j, a], recv_sem=recv_sems.at[i, a],
                                                device_id=(j >> 2, (j >> 1) & 1, j & 1), device_id_type=MESH)

        def own(a, j):
            return pltpu.make_async_copy(source(a, j), _row_window(outs[a].at[j], windows[a]), local_sems.at[a])

        return me, copy, own

    def start(ins, outs, sems):
        me, copy, own = plan(ins, outs, sems)
        for a in range(na):
            for j in range(N_DEV):
                @pl.when(me == j)
                def _():
                    own(a, j).start()

                @pl.when(me != j)
                def _():
                    copy(a, j, me).start()

    def finish(ins, outs, sems):
        me, copy, own = plan(ins, outs, sems)
        for a in range(na):
            for j in range(N_DEV):
                @pl.when(me == j)
                def _():
                    for i in range(N_DEV):
                        if i != j:
                            copy(a, j, i).wait_recv()
                    own(a, j).wait()

                @pl.when(me != j)
                def _():
                    copy(a, j, me).wait_send()

    return _Carry(carry_ins, [_sds((N_DEV,) + src.shape[-2:], src.dtype) for src, _, _ in items],
                  [pltpu.SemaphoreType.DMA((N_DEV, na)), pltpu.SemaphoreType.DMA((N_DEV, na)),
                   pltpu.SemaphoreType.DMA((na,))], start, finish, None, aliases)


NQ, NKV = N_HEADS * HEAD, 2 * N_KV * HEAD


class _Mesh:
    def __init__(self, shards):
        self.shards, self.full, self.received, self.cache = shards, {}, {}, {}

    def fetch(self, wanted):
        items = []
        for want in wanted:
            name, r0, r1 = want if isinstance(want, tuple) else (want, 0, self.shards[want].shape[0])
            items.append((self.shards[name], (r0, r1), self.full.get(name)))
        return _gather_carry(items)

    def fetched(self, wanted, results):
        self.full.update(zip([want[0] if isinstance(want, tuple) else want for want in wanted], results))

    def send(self, *payloads):
        return _exchange_carry([(parts, rows or (0, parts.shape[1]), self.received.get(name))
                                for name, parts, rows in payloads])

    def sent(self, names, results):
        self.received.update(zip(names, results))

    def w(self, key):
        if key not in self.cache:
            self.cache[key] = self._layout(key)
        return self.cache[key]

    def _layout(self, key):
        if key in ("gu1", "gu2"):
            return self.full[key]
        if key in ("d1", "d2"):
            return self.full[key].reshape(4, FS, D)
        if key in ("out", "q", "o"):
            return self.full[key].reshape(D, D)
        if key == "kv":
            return self.full["kv"]
        if key == "convw":
            rows = self.full["conv"][:, :3, :].transpose(1, 0, 2).reshape(3, D)
            return jnp.concatenate([rows, jnp.zeros((5, D), F32)], axis=0)
        w_in_t = self.full["win"].reshape(-1, D)
        if key == "wa":
            return w_in_t[NQ + NKV:].reshape(5, D, D)
        assert key == "wb", key
        return jnp.stack([w_in_t[:NQ], jnp.pad(w_in_t[NQ:NQ + NKV], ((0, D - NKV), (0, 0)))])


def _w_in_parts(dw_a, dw_b):
    return jnp.concatenate([dw_b[0], dw_b[1][:NKV], dw_a.reshape(5 * D, D)], axis=0).reshape(N_DEV, -1, D)


def _forward_backward(x, mem, target, g, rel_bias, sinks, ex):
    s = x.shape[0]
    def fetching(wanted, call, *args, **kw):
        res, got = call(*args, carry=ex.fetch(wanted), **kw)
        ex.fetched(wanted, got)
        return res

    h1 = fetching(["gu1", "conv"], _rmsnorm, x, g["ffn1"], "norm_ffn1")
    gu1, a1 = fetching(["d1", ("win", 0, 400)], _ffn_up, h1, ex.w("gu1").reshape(2, 4, FS, D), "ffn1_up")
    x1, h2 = fetching([("win", 400, 832)], _mm_res_norm, a1, ex.w("d1"), x, g["mix"], 0.5, "ffn1_down")
    pa, pb = fetching(["gu2", "out", "q"], _in_proj, h2, ex.w("wa"), ex.w("wb"), "in_proj")
    biasm = _bias_build(rel_bias, "bias_build")
    attn, lse = fetching(["kv", "d2", "o"], _swa_fwd, pb, biasm, sinks, "swa_fwd")
    merged = _conv_merge_fwd(pa, attn, ex.w("convw"), "conv_merge_fwd")
    (x2, h3), _ = _mm_res_norm(merged[None], ex.w("out")[None], x1, g["xattn"], 1.0, "out_proj")
    q2 = _mm_nn(h3, ex.w("q")[None], "xattn_q")[0][0]
    mh, _ = _rmsnorm(mem, g["mem"], "norm_mem")
    kv2 = _mm_nn(mh, ex.w("kv"), "xattn_kv")[0]
    o, lse2 = _xattn_fwd(q2, kv2, "xattn_fwd")
    (x3, h4), _ = _mm_res_norm(o[None], ex.w("o")[None], x2, g["ffn2"], 1.0, "xattn_o")
    (gu2, a2), _ = _ffn_up(h4, ex.w("gu2").reshape(2, 4, FS, D), "ffn2_up")
    dx4, dx4b, loss, d_final = _ffn_down_loss(a2, ex.w("d2"), x3, g["final"], target, "ffn2_down_loss")
    def sending(payloads, call, *args, **kw):
        res, got = call(*args, carry=ex.send(*payloads), **kw)
        ex.sent([name for name, _, _ in payloads], got)
        return res

    dw_d2 = _mm_tn(a2, dx4b[None], "dw_ffn2_down", scale=0.5)[0].reshape(N_DEV, -1, D)
    dgu2 = sending([("d2", dw_d2, None)], _ffn_down_bwd, dx4b, ex.w("d2"), gu2, "ffn2_down_bwd").reshape(8, s, FS)
    dw_gu2 = _mm_tn(dgu2, h4[None], "dw_ffn2_up", scale=0.5)[0]
    dx3, dx3b, d_ffn2 = sending([("gu2", dw_gu2, (0, 400))], _mm_acc_rms_bwd, [(dgu2, ex.w("gu2"))], "ffn2_up_bwd",
                                x=x3, gain=g["ffn2"], dres=dx4, scale=0.5)
    do, _ = _mm_acc(dx3b[None], ex.w("o")[None], "xattn_o_bwd", BF16, bt=True)
    dw_o = _mm_tn(o[None], dx3b[None], "dw_xattn_o")[0].reshape(N_DEV, -1, D)
    dq2, dkv2 = sending([("o", dw_o, None)], _xattn_bwd, q2, kv2, o, do, lse2, "xattn_bwd")
    dkv2b = dkv2.astype(BF16)
    dw_q = _mm_tn(h3[None], dq2[None], "dw_xattn_q")[0].reshape(N_DEV, -1, D)
    dx2, dx2b, d_xattn = sending([("q", dw_q, None)], _mm_acc_rms_bwd, [(dq2[None], ex.w("q")[None])],
                                 "xattn_q_bwd", x=x2, gain=g["xattn"], dres=dx3, bt=True)
    dw_kv = _mm_tn(mh[None], dkv2b, "dw_xattn_kv")[0]
    (_, _, d_mem), _ = _mm_acc_rms_bwd([(dkv2b, ex.w("kv"))], "xattn_kv_bwd", x=mem, gain=g["mem"],
                                       dres=jnp.zeros_like(mem), bt=True)
    dmerged, _ = _mm_acc(dx2b[None], ex.w("out")[None], "out_proj_bwd", BF16, bt=True)
    dw_out = _mm_tn(merged[None], dx2b[None], "dw_out_proj")[0].reshape(N_DEV, -1, D)
    dattn, dpa, d_convw = sending([("kv", dw_kv, None)], _conv_merge_bwd,
                                  dmerged, pa, attn, ex.w("convw"), "conv_merge_bwd")
    dpb, dbias, d_sinks = sending([("gu2", dw_gu2, (400, FS)), ("out", dw_out, None)], _swa_bwd,
                                  pb, attn, dattn, lse, biasm, sinks, "swa_bwd")
    d_relb = _bias_bwd(dbias, "bias_bwd")
    dw_in = _w_in_parts(_mm_tn(dpa, h2[None], "dw_in_proj_a")[0], _mm_tn(dpb, h2[None], "dw_in_proj_b")[0])
    dx1, dx1b, d_mix = sending([("win", dw_in, (0, 672))], _mm_acc_rms_bwd,
                               [(dpa, ex.w("wa")), (dpb, ex.w("wb"))], "in_proj_bwd",
                               x=x1, gain=g["mix"], dres=dx2)
    dw_d1 = sending([("win", dw_in, (672, 832))], _mm_tn, a1, dx1b[None], "dw_ffn1_down", scale=0.5)
    dw_d1 = dw_d1.reshape(N_DEV, -1, D)
    dgu1 = sending([("d1", dw_d1, None)], _ffn_down_bwd, dx1b, ex.w("d1"), gu1, "ffn1_down_bwd").reshape(8, s, FS)
    dw_gu1 = _mm_tn(dgu1, h1[None], "dw_ffn1_up", scale=0.5)[0]
    dx0, _, d_ffn1 = sending([("gu1", dw_gu1, None)], _mm_acc_rms_bwd, [(dgu1, ex.w("gu1"))], "ffn1_up_bwd",
                             x=x, gain=g["ffn1"], dres=dx1, scale=0.5)

    relb_row = jnp.concatenate([d_relb[:, :REL_BUCKETS].T.reshape(1, REL_BUCKETS * N_HEADS), d_sinks[:, :N_HEADS],
                                jnp.zeros((1, D - REL_BUCKETS * N_HEADS - N_HEADS), F32)], axis=1)
    loss_row = jnp.concatenate([loss[0:1, 0:1], jnp.zeros((1, D - 1), F32)], axis=1)
    small = jnp.concatenate([d_ffn1, d_mix, d_xattn, d_mem, d_ffn2, d_final, relb_row, loss_row, d_convw[0:3],
                             jnp.zeros((SMALL_ROWS - ROW_CONV - 3, D), F32)], axis=0)
    return dx0, small


def _pack_small(norms, final, relb, sinks, conv_local, me):
    relb_row = jnp.concatenate([relb.reshape(1, -1), sinks.reshape(1, -1),
                                jnp.zeros((1, D - REL_BUCKETS * N_HEADS - N_HEADS), F32)], axis=1)
    conv_rows = lax.dynamic_update_slice(jnp.zeros((3, D), F32), conv_local.reshape(3, -1), (0, 128 * me))
    return jnp.concatenate(list(norms) + [final.reshape(1, D), relb_row, jnp.zeros((1, D), F32), conv_rows,
                                          jnp.zeros((SMALL_ROWS - ROW_CONV - 3, D), F32)], axis=0)


def kernel(x, mem, positions, rel_bias, ffn1_norm, ffn1_w_gu, ffn1_w_down, mix_norm, w_in, sinks, conv_w, w_out, xattn_norm, mem_norm, xattn_wq, xattn_wkv, xattn_wo, ffn2_norm, ffn2_w_gu, ffn2_w_down, final_norm, loss_target, m_rel_bias, m_ffn1_norm, m_ffn1_w_gu, m_ffn1_w_down, m_mix_norm, m_w_in, m_sinks, m_conv_w, m_w_out, m_xattn_norm, m_mem_norm, m_xattn_wq, m_xattn_wkv, m_xattn_wo, m_ffn2_norm, m_ffn2_w_gu, m_ffn2_w_down, m_final_norm, v_rel_bias, v_ffn1_norm, v_ffn1_w_gu, v_ffn1_w_down, v_mix_norm, v_w_in, v_sinks, v_conv_w, v_w_out, v_xattn_norm, v_mem_norm, v_xattn_wq, v_xattn_wkv, v_xattn_wo, v_ffn2_norm, v_ffn2_w_gu, v_ffn2_w_down, v_final_norm):
    del positions
    me = _slot(*_position())
    big = dict(gu1=(ffn1_w_gu, m_ffn1_w_gu, v_ffn1_w_gu), d1=(ffn1_w_down, m_ffn1_w_down, v_ffn1_w_down),
               win=(w_in, m_w_in, v_w_in), out=(w_out, m_w_out, v_w_out), q=(xattn_wq, m_xattn_wq, v_xattn_wq),
               kv=(xattn_wkv, m_xattn_wkv, v_xattn_wkv), o=(xattn_wo, m_xattn_wo, v_xattn_wo),
               gu2=(ffn2_w_gu, m_ffn2_w_gu, v_ffn2_w_gu), d2=(ffn2_w_down, m_ffn2_w_down, v_ffn2_w_down))
    order = list(big)
    transposed = ("gu1", "gu2", "win")
    local = {k: tuple(t[0].T if k in transposed else t[0] for t in big[k]) for k in order}
    shards = {k: local[k][0].astype(BF16) for k in order}
    shards["conv"] = jnp.concatenate([conv_w[0], jnp.zeros((5, 128), F32)], axis=0)
    ex = _Mesh(shards)
    gains = dict(ffn1=ffn1_norm, mix=mix_norm, xattn=xattn_norm, mem=mem_norm, ffn2=ffn2_norm,
                 final=final_norm.reshape(1, D))
    dx, small = _forward_backward(x[0], mem[0], loss_target[0], gains, rel_bias, sinks, ex)
    small_parts = _run_alone(_exchange_carry([], [small]), "exchange_small")[0]
    big_out = {k: _adamw(ex.received[k], *local[k], "adamw_" + k) for k in order}
    big_out = {k: [t.T if k in transposed else t for t in big_out[k]] for k in order}
    packed = [_pack_small(norms, final, relb, sk, conv, me) for norms, final, relb, sk, conv in (
        ((ffn1_norm, mix_norm, xattn_norm, mem_norm, ffn2_norm), final_norm, rel_bias, sinks, conv_w),
        ((m_ffn1_norm, m_mix_norm, m_xattn_norm, m_mem_norm, m_ffn2_norm), m_final_norm, m_rel_bias, m_sinks, m_conv_w),
        ((v_ffn1_norm, v_mix_norm, v_xattn_norm, v_mem_norm, v_ffn2_norm), v_final_norm, v_rel_bias, v_sinks, v_conv_w))]
    small_out = _adamw(small_parts, *packed, "adamw_small")

    def unpack(t):
        conv = lax.dynamic_slice(t[ROW_CONV:ROW_CONV + 3], (0, 128 * me), (3, 128))[None]
        nrel = REL_BUCKETS * N_HEADS
        return dict(ffn1_norm=t[0:1], mix_norm=t[1:2], xattn_norm=t[2:3], mem_norm=t[3:4], ffn2_norm=t[4:5],
                    final_norm=t[5], rel_bias=t[ROW_RELB, :nrel].reshape(REL_BUCKETS, N_HEADS),
                    sinks=t[ROW_RELB:ROW_RELB + 1, nrel:nrel + N_HEADS], conv_w=conv)

    names = dict(gu1="ffn1_w_gu", d1="ffn1_w_down", win="w_in", out="w_out", q="xattn_wq", kv="xattn_wkv",
                 o="xattn_wo", gu2="ffn2_w_gu", d2="ffn2_w_down")
    results = []
    for idx in range(4):
        leaves = unpack(small_out[idx])
        leaves.update({names[k]: big_out[k][idx][None] for k in order})
        results.append(leaves)
    weights = ("rel_bias", "ffn1_norm", "ffn1_w_gu", "ffn1_w_down", "mix_norm", "w_in", "sinks", "conv_w", "w_out",
               "xattn_norm", "mem_norm", "xattn_wq", "xattn_wkv", "xattn_wo", "ffn2_norm", "ffn2_w_gu", "ffn2_w_down",
               "final_norm")
    loss = small_out[0][ROW_LOSS, 0]
    return (loss, dx[None], *[leaves[n] for leaves in results for n in weights])
```

```python
import math

import numpy as np
import jax
import jax.numpy as jnp
from jax import lax
from jax.experimental import pallas as pl
from jax.experimental.pallas import tpu as pltpu

F32, BF16 = jnp.float32, jnp.bfloat16
MESH = pl.DeviceIdType.MESH

D = 1024
N_DEV = 8
D_FF = 2816
FS = D_FF // 4
HEAD = 64
N_HEADS, N_KV = 16, 4
BLK = 128
XH, XHD = 4, 256
REL_BUCKETS, REL_EXACT, REL_MAX_DIST = 32, 16, 128
EPS, NEG = 1e-6, -1e30
ADAM_LR, ADAM_B1, ADAM_B2, ADAM_EPS, ADAM_WD, ADAM_STEP = 0.001, 0.9, 0.999, 1e-08, 0.01, 10
VMEM_LIMIT_V7X = 56 * 2**20
SMALL_ROWS = 16
ROW_RELB, ROW_LOSS, ROW_CONV = 6, 7, 8


def _bucket_thresholds():
    n = np.arange(REL_MAX_DIST)
    nf = np.maximum(n, 1).astype(np.float32)
    large = REL_EXACT + (np.log(nf / np.float32(REL_EXACT)) / np.float32(math.log(REL_MAX_DIST / REL_EXACT))
                         * np.float32(REL_BUCKETS - REL_EXACT)).astype(np.int32)
    b = np.where(n < REL_EXACT, n, np.minimum(large, REL_BUCKETS - 1))
    return [int(np.argmax(b >= REL_EXACT + k)) for k in range(1, REL_BUCKETS - REL_EXACT)]


BUCKET_THRESHOLDS = _bucket_thresholds()


HBM_SPEC = pl.BlockSpec(memory_space=pl.ANY)


class _Carry:
    def __init__(self, ins, outs, sems, start, finish, mid=None, aliases=None):
        self.ins, self.outs, self.sems = list(ins), list(outs), list(sems)
        self.start, self.finish, self.mid, self.aliases = start, finish, mid, dict(aliases or {})


def _pcall(body, *, name, grid, in_specs, out_specs, out_shape, scratch=(), carry=None):
    params = pltpu.CompilerParams(dimension_semantics=("arbitrary",) * len(grid), vmem_limit_bytes=VMEM_LIMIT_V7X)
    if carry is None:
        return pl.pallas_call(body, name=name, grid=grid, in_specs=in_specs, out_specs=out_specs,
                              out_shape=out_shape, scratch_shapes=list(scratch), compiler_params=params)
    single = not isinstance(out_shape, (list, tuple))
    own_specs, own_shapes = ([out_specs], [out_shape]) if single else (list(out_specs), list(out_shape))
    n_in, n_out, n_scr = len(in_specs), len(own_shapes), len(scratch)
    n_cin, n_cout = len(carry.ins), len(carry.outs)
    steps = math.prod(grid)
    mid_step = max(steps - 1 - max(steps // 8, 1), 0)

    def carrying(*refs):
        ins, refs = refs[:n_in], refs[n_in:]
        cins, refs = refs[:n_cin], refs[n_cin:]
        outs, refs = refs[:n_out], refs[n_out:]
        couts, refs = refs[:n_cout], refs[n_cout:]
        scr, csems = refs[:n_scr], refs[n_scr:]
        step = 0
        for axis, size in enumerate(grid):
            step = step * size + pl.program_id(axis)

        @pl.when(step == 0)
        def _():
            carry.start(cins, couts, csems)

        body(*ins, *outs, *scr)
        if carry.mid is not None:
            @pl.when(step == mid_step)
            def _():
                carry.mid(cins, couts, csems)

        @pl.when(step == steps - 1)
        def _():
            carry.finish(cins, couts, csems)

    call = pl.pallas_call(carrying, name=name, grid=grid, in_specs=list(in_specs) + [HBM_SPEC] * n_cin,
                          out_specs=own_specs + [HBM_SPEC] * n_cout, out_shape=own_shapes + carry.outs,
                          scratch_shapes=list(scratch) + carry.sems, compiler_params=params,
                          input_output_aliases={n_in + i: n_out + o for i, o in carry.aliases.items()})

    def run(*args):
        res = call(*args, *carry.ins)
        return (res[0] if single else res[:n_out]), res[n_out:]

    return run


def _run_alone(carry, name):
    n_cin, n_cout = len(carry.ins), len(carry.outs)

    def body(*refs):
        cins, couts, csems = refs[:n_cin], refs[n_cin:n_cin + n_cout], refs[n_cin + n_cout:]
        carry.start(cins, couts, csems)
        if carry.mid is not None:
            carry.mid(cins, couts, csems)
        carry.finish(cins, couts, csems)

    return pl.pallas_call(body, name=name, in_specs=[HBM_SPEC] * n_cin, out_specs=[HBM_SPEC] * n_cout,
                          out_shape=carry.outs, scratch_shapes=carry.sems,
                          input_output_aliases=carry.aliases)(*carry.ins)


def _dot(a, b):
    return jnp.dot(a, b, preferred_element_type=F32)


def _dot_nt(a, b):
    return lax.dot_general(a, b, (((1,), (1,)), ((), ())), preferred_element_type=F32)


def _dot_tn(a, b):
    return lax.dot_general(a, b, (((0,), (0,)), ((), ())), preferred_element_type=F32)


def _sds(shape, dtype):
    return jax.ShapeDtypeStruct(tuple(shape), dtype)


ROW_CHUNK = 256


def _row_chunks(tm):
    return [slice(r, min(r + ROW_CHUNK, tm)) for r in range(0, tm, ROW_CHUNK)]


def _carried(call, args, carry):
    return call(*args) if carry is not None else (call(*args), ())


def _rmsnorm(x, g, name, carry=None):
    m, d = x.shape
    tm = min(512, m)

    def body(x_ref, g_ref, h_ref):
        xv = x_ref[...]
        r = lax.rsqrt(jnp.mean(xv * xv, axis=-1, keepdims=True) + EPS)
        h_ref[...] = (xv * r * g_ref[...]).astype(BF16)

    call = _pcall(body, name=name, grid=(m // tm,), carry=carry,
                  in_specs=[pl.BlockSpec((tm, d), lambda i: (i, 0)), pl.BlockSpec((1, d), lambda i: (0, 0))],
                  out_specs=pl.BlockSpec((tm, d), lambda i: (i, 0)), out_shape=_sds((m, d), BF16))
    return _carried(call, (x, g), carry)


def _mm_nn(a, b, name, tm=1024, bt=False, carry=None):
    m, k = a.shape
    nj = b.shape[0]
    n = b.shape[1] if bt else b.shape[2]
    tm = min(tm, m)
    dot = _dot_nt if bt else _dot

    def body(a_ref, b_ref, o_ref):
        o_ref[...] = dot(a_ref[...], b_ref[...]).astype(BF16)

    call = _pcall(body, name=name, grid=(nj, m // tm),
                  in_specs=[pl.BlockSpec((tm, k), lambda j, i: (i, 0)),
                            pl.BlockSpec((None,) + b.shape[1:], lambda j, i: (j, 0, 0))],
                  out_specs=pl.BlockSpec((None, tm, n), lambda j, i: (j, i, 0)),
                  out_shape=_sds((nj, m, n), BF16), carry=carry)
    return _carried(call, (a, b), carry)


def _load_once(src_hbm, dst_vmem, sem):
    @pl.when(pl.program_id(0) == 0)
    def _():
        load = pltpu.make_async_copy(src_hbm, dst_vmem, sem)
        load.start()
        load.wait()


def _resident(w):
    return [pltpu.VMEM(w.shape, w.dtype), pltpu.SemaphoreType.DMA(())]


def _ffn_up(h, w4, name, tm=512, carry=None):
    s, d = h.shape
    tm = min(tm, s)

    def body(h_ref, w_hbm, gu_ref, a_ref, w_ref, w_sem):
        _load_once(w_hbm, w_ref, w_sem)
        for p in range(4):
            for rows in _row_chunks(tm):
                hv = h_ref[rows, :]
                g = _dot_nt(hv, w_ref[0, p])
                u = _dot_nt(hv, w_ref[1, p])
                gu_ref[0, p, rows, :] = g.astype(BF16)
                gu_ref[1, p, rows, :] = u.astype(BF16)
                a_ref[p, rows, :] = (g * jax.nn.sigmoid(g) * u).astype(BF16)

    call = _pcall(body, name=name, grid=(s // tm,),
                  in_specs=[pl.BlockSpec((tm, d), lambda i: (i, 0)), HBM_SPEC],
                  out_specs=[pl.BlockSpec((2, 4, tm, FS), lambda i: (0, 0, i, 0)),
                             pl.BlockSpec((4, tm, FS), lambda i: (0, i, 0))],
                  out_shape=[_sds((2, 4, s, FS), BF16), _sds((4, s, FS), BF16)], scratch=_resident(w4), carry=carry)
    return _carried(call, (h, w4), carry)


def _in_proj(h, wa, wb, name, tm=512, carry=None):
    s, d = h.shape
    tm = min(tm, s)

    def body(h_ref, wa_hbm, wb_hbm, pa_ref, pb_ref, wa_ref, wa_sem, wb_ref, wb_sem):
        _load_once(wa_hbm, wa_ref, wa_sem)
        _load_once(wb_hbm, wb_ref, wb_sem)
        hv = h_ref[...]
        for w_ref, o_ref in ((wb_ref, pb_ref), (wa_ref, pa_ref)):
            for j in range(w_ref.shape[0]):
                o_ref[j] = _dot_nt(hv, w_ref[j]).astype(BF16)

    call = _pcall(body, name=name, grid=(s // tm,), carry=carry,
                  in_specs=[pl.BlockSpec((tm, d), lambda i: (i, 0)), HBM_SPEC, HBM_SPEC],
                  out_specs=[pl.BlockSpec((wa.shape[0], tm, d), lambda i: (0, i, 0)),
                             pl.BlockSpec((wb.shape[0], tm, d), lambda i: (0, i, 0))],
                  out_shape=[_sds((wa.shape[0], s, d), BF16), _sds((wb.shape[0], s, d), BF16)],
                  scratch=_resident(wa) + _resident(wb))
    return _carried(call, (h, wa, wb), carry)


def _mm_res_norm(a, w, xres, gain, scale, name, tm=512, carry=None):
    npart, s, kp = a.shape
    tm = min(tm, s)

    def body(a_ref, w_ref, x_ref, g_ref, xo_ref, h_ref):
        for rows in _row_chunks(tm):
            acc = _dot(a_ref[0, rows, :], w_ref[0])
            for p in range(1, npart):
                acc = acc + _dot(a_ref[p, rows, :], w_ref[p])
            xn = x_ref[rows, :] + scale * acc
            xo_ref[rows, :] = xn
            r = lax.rsqrt(jnp.mean(xn * xn, axis=-1, keepdims=True) + EPS)
            h_ref[rows, :] = (xn * r * g_ref[...]).astype(BF16)

    call = _pcall(body, name=name, grid=(s // tm,),
                  in_specs=[pl.BlockSpec((npart, tm, kp), lambda i: (0, i, 0)),
                            pl.BlockSpec((npart, kp, D), lambda i: (0, 0, 0)),
                            pl.BlockSpec((tm, D), lambda i: (i, 0)),
                            pl.BlockSpec((1, D), lambda i: (0, 0))],
                  out_specs=[pl.BlockSpec((tm, D), lambda i: (i, 0)), pl.BlockSpec((tm, D), lambda i: (i, 0))],
                  out_shape=[_sds((s, D), F32), _sds((s, D), BF16)], carry=carry)
    return _carried(call, (a, w, xres, gain), carry)


def _ffn_down_loss(a, w, xres, gain, target, name, tm=512):
    npart, s, kp = a.shape
    tm = min(tm, s)

    def body(a_ref, w_ref, x_ref, g_ref, t_ref, dx_ref, dxb_ref, loss_ref, dg_ref):
        @pl.when(pl.program_id(0) == 0)
        def _():
            loss_ref[...] = jnp.zeros_like(loss_ref)
            dg_ref[...] = jnp.zeros_like(dg_ref)

        for rows in _row_chunks(tm):
            acc = _dot(a_ref[0, rows, :], w_ref[0])
            for p in range(1, npart):
                acc = acc + _dot(a_ref[p, rows, :], w_ref[p])
            xn = x_ref[rows, :] + 0.5 * acc
            r = lax.rsqrt(jnp.mean(xn * xn, axis=-1, keepdims=True) + EPS)
            xh = xn * r
            gv = g_ref[...]
            err = xh * gv - t_ref[rows, :]
            part = 0.5 * jnp.sum(jnp.mean(err * err, axis=-1, keepdims=True), axis=0, keepdims=True)
            dy = err * (1.0 / D)
            dyg = dy * gv
            dxn = r * (dyg - xh * jnp.mean(dyg * xh, axis=-1, keepdims=True))
            dx_ref[rows, :] = dxn
            dxb_ref[rows, :] = dxn.astype(BF16)
            loss_ref[...] += jnp.broadcast_to(part, loss_ref.shape)
            dg_ref[...] += jnp.sum(dy * xh, axis=0, keepdims=True)

    return _pcall(body, name=name, grid=(s // tm,),
                  in_specs=[pl.BlockSpec((npart, tm, kp), lambda i: (0, i, 0)),
                            pl.BlockSpec((npart, kp, D), lambda i: (0, 0, 0)),
                            pl.BlockSpec((tm, D), lambda i: (i, 0)),
                            pl.BlockSpec((1, D), lambda i: (0, 0)),
                            pl.BlockSpec((tm, D), lambda i: (i, 0))],
                  out_specs=[pl.BlockSpec((tm, D), lambda i: (i, 0)), pl.BlockSpec((tm, D), lambda i: (i, 0)),
                             pl.BlockSpec((8, 128), lambda i: (0, 0)), pl.BlockSpec((1, D), lambda i: (0, 0))],
                  out_shape=[_sds((s, D), F32), _sds((s, D), BF16), _sds((8, 128), F32), _sds((1, D), F32)],
                  )(a, w, xres, gain, target)


def _window_tiles():
    i = lax.broadcasted_iota(jnp.int32, (BLK, BLK), 0)
    j = lax.broadcasted_iota(jnp.int32, (BLK, BLK), 1)
    rel = (i - j) & (BLK - 1)
    large = jnp.full_like(rel, REL_EXACT)
    for t in BUCKET_THRESHOLDS:
        large = large + (rel >= t).astype(jnp.int32)
    return j <= i, jnp.where(rel < REL_EXACT, rel, large)


def _bias_build(rel_bias, name):
    def body(rb_ref, o_ref):
        _, bucket = _window_tiles()

        def per_head(h, carry):
            acc = jnp.zeros((BLK, BLK), F32)
            for b in range(REL_BUCKETS):
                acc = jnp.where(bucket == b, rb_ref[b, h], acc)
            o_ref[h] = acc
            return carry

        lax.fori_loop(0, N_HEADS, per_head, 0)

    return _pcall(body, name=name, grid=(1,),
                  in_specs=[pl.BlockSpec(memory_space=pltpu.SMEM)],
                  out_specs=pl.BlockSpec((N_HEADS, BLK, BLK), lambda i: (0, 0, 0)),
                  out_shape=_sds((N_HEADS, BLK, BLK), F32))(rel_bias)


def _bias_bwd(dbias, name):
    def body(db_ref, o_ref):
        _, bucket = _window_tiles()
        lane = lax.broadcasted_iota(jnp.int32, (N_HEADS, 128), 1)

        def per_bucket(b, out):
            mb = (bucket == b).astype(F32)
            per_col = jnp.sum(db_ref[...] * mb[None, :, :], axis=1)
            return jnp.where(lane == b, jnp.sum(per_col, axis=1, keepdims=True), out)

        o_ref[...] = lax.fori_loop(0, REL_BUCKETS, per_bucket, jnp.zeros((N_HEADS, 128), F32))

    return _pcall(body, name=name, grid=(1,),
                  in_specs=[pl.BlockSpec((N_HEADS, BLK, BLK), lambda i: (0, 0, 0))],
                  out_specs=pl.BlockSpec((N_HEADS, 128), lambda i: (0, 0)),
                  out_shape=_sds((N_HEADS, 128), F32))(dbias)


PAIR = 2 * HEAD
GROUP = N_HEADS // N_KV
SWA_SCALE = HEAD ** -0.5


def _window_masks(n):
    i = lax.broadcasted_iota(jnp.int32, (GROUP * BLK, BLK), 0) & (BLK - 1)
    j = lax.broadcasted_iota(jnp.int32, (GROUP * BLK, BLK), 1)
    return j <= i, jnp.logical_and(n == 0, j > i), j < HEAD


def _kv_twice(ref, base, g, low):
    slab = ref[:, base + PAIR * (g // 2): base + PAIR * (g // 2 + 1)]
    swapped = pltpu.roll(slab, HEAD, 1)
    return jnp.where(low, slab, swapped) if g % 2 == 0 else jnp.where(low, swapped, slab)


def _stack_heads(ref, g, low):
    parts = []
    for r in range(2):
        slab = ref[:, PAIR * (2 * g + r): PAIR * (2 * g + r + 1)]
        zero = jnp.zeros_like(slab)
        parts += [jnp.where(low, slab, zero), jnp.where(low, zero, slab)]
    return jnp.concatenate(parts, axis=0)


def _unstack_heads(t, low):
    return [jnp.where(low, t[2 * r * BLK:(2 * r + 1) * BLK], t[(2 * r + 1) * BLK:(2 * r + 2) * BLK])
            for r in range(2)]


def _head_rows(t, k):
    return t[k * BLK:(k + 1) * BLK]


def _per_head_column(values):
    head = lax.broadcasted_iota(jnp.int32, (GROUP * BLK, 1), 0) // BLK
    col = jnp.full((GROUP * BLK, 1), values[0], F32)
    for k in range(1, GROUP):
        col = jnp.where(head == k, values[k], col)
    return col


def _window_logits(q4, kc, kp, bias4, own, absent):
    sc = jnp.where(own, _dot_nt(q4, kc), _dot_nt(q4, kp)) * SWA_SCALE + bias4
    return jnp.where(absent, NEG, sc)


def _split_window(t, own):
    zero = jnp.zeros_like(t)
    return jnp.where(own, t, zero), jnp.where(own, zero, t)


def _swa_fwd(pb, bias, sinks, name, carry=None):
    _, s, _ = pb.shape
    nb = s // BLK
    kvw = 2 * N_KV * HEAD

    def body(q_ref, kc_ref, kp_ref, b_ref, sk_ref, o_ref, lse_ref):
        own, absent, low4 = _window_masks(pl.program_id(0))
        low = low4[:BLK]
        lane = lax.broadcasted_iota(jnp.int32, (BLK, 128), 1)
        lse_t = jnp.zeros((BLK, 128), F32)
        for g in range(N_KV):
            q4 = _stack_heads(q_ref, g, low)
            kc, kp = _kv_twice(kc_ref, 0, g, low), _kv_twice(kp_ref, 0, g, low)
            vc, vp = _kv_twice(kc_ref, N_KV * HEAD, g, low), _kv_twice(kp_ref, N_KV * HEAD, g, low)
            bias4 = b_ref[GROUP * g:GROUP * (g + 1)].reshape(GROUP * BLK, BLK)
            sc = _window_logits(q4, kc, kp, bias4, own, absent)
            sk = _per_head_column([sk_ref[0, GROUP * g + k] for k in range(GROUP)])
            m = jnp.maximum(jnp.max(sc, axis=1, keepdims=True), sk)
            p = jnp.exp(sc - m)
            l = jnp.sum(p, axis=1, keepdims=True) + jnp.exp(sk - m)
            p_own, p_prev = _split_window(p.astype(BF16), own)
            out = (_dot(p_own, vc) + _dot(p_prev, vp)) * (1.0 / l)
            for r, slab in enumerate(_unstack_heads(out, low)):
                o_ref[:, PAIR * (2 * g + r): PAIR * (2 * g + r + 1)] = slab.astype(BF16)
            lse4 = m + jnp.log(l)
            for k in range(GROUP):
                lse_t = jnp.where(lane == GROUP * g + k, _head_rows(lse4, k), lse_t)
        lse_ref[...] = lse_t

    call = _pcall(body, name=name, grid=(nb,),
                  in_specs=[pl.BlockSpec((None, BLK, D), lambda n: (0, n, 0)),
                            pl.BlockSpec((None, BLK, kvw), lambda n: (1, n, 0)),
                            pl.BlockSpec((None, BLK, kvw), lambda n: (1, jnp.maximum(n - 1, 0), 0)),
                            pl.BlockSpec((N_HEADS, BLK, BLK), lambda n: (0, 0, 0)),
                            pl.BlockSpec(memory_space=pltpu.SMEM)],
                  out_specs=[pl.BlockSpec((BLK, D), lambda n: (n, 0)), pl.BlockSpec((BLK, 128), lambda n: (n, 0))],
                  out_shape=[_sds((s, D), BF16), _sds((s, 128), F32)], carry=carry)
    return _carried(call, (pb, pb, pb, bias, sinks), carry)


def _fold_halves(t, g, low):
    folded = jnp.where(low, t, 0.0) + pltpu.roll(jnp.where(low, 0.0, t), HEAD, 1)
    return folded if g % 2 == 0 else pltpu.roll(folded, HEAD, 1)


def _swa_bwd(pb, attn, dattn, lse, bias, sinks, name, carry=None):
    _, s, _ = pb.shape
    nb = s // BLK
    kvw = 2 * N_KV * HEAD
    voff = N_KV * HEAD

    def body(q_ref, kc_ref, kp_ref, o_ref, do_ref, lse_ref, b_ref, skrow_ref, dpb_ref, dbias_ref, dsk_ref,
             dq_hold, kv_hold, dq_new, kv_prev, kv_cur):
        n = pl.program_id(0)

        @pl.when(n == 0)
        def _():
            dbias_ref[...] = jnp.zeros_like(dbias_ref)
            dsk_ref[...] = jnp.zeros_like(dsk_ref)
            dq_hold[...] = jnp.zeros_like(dq_hold)
            kv_hold[...] = jnp.zeros_like(kv_hold)

        @pl.when(n < nb)
        def _():
            own, absent, low4 = _window_masks(n)
            low = low4[:BLK]
            lane = lax.broadcasted_iota(jnp.int32, (BLK, 128), 1)
            delta_t = jnp.zeros((BLK, 128), F32)
            ones = jnp.ones((PAIR, 128), BF16)
            for pair_of_kv in range(N_KV // 2):
                slab_grads = [jnp.zeros((BLK, PAIR), F32) for _ in range(4)]
                for g in (2 * pair_of_kv, 2 * pair_of_kv + 1):
                    q4, do4 = _stack_heads(q_ref, g, low), _stack_heads(do_ref, g, low)
                    kc, kp = _kv_twice(kc_ref, 0, g, low), _kv_twice(kp_ref, 0, g, low)
                    vc, vp = _kv_twice(kc_ref, voff, g, low), _kv_twice(kp_ref, voff, g, low)
                    o_slabs = [o_ref[:, PAIR * (2 * g + r): PAIR * (2 * g + r + 1)] for r in range(2)]
                    o4 = jnp.concatenate([o_slabs[0], o_slabs[0], o_slabs[1], o_slabs[1]], axis=0)
                    delta = _dot(do4 * o4, ones)
                    heads = range(GROUP * g, GROUP * (g + 1))
                    lse4 = jnp.concatenate([lse_ref[:, h:h + 1] for h in heads], axis=0)
                    bias4 = b_ref[GROUP * g:GROUP * (g + 1)].reshape(GROUP * BLK, BLK)
                    p = jnp.exp(_window_logits(q4, kc, kp, bias4, own, absent) - lse4)
                    dp = jnp.where(own, _dot_nt(do4, vc), _dot_nt(do4, vp))
                    ds = p * (dp - delta)
                    dbias_ref[GROUP * g:GROUP * (g + 1)] += ds.reshape(GROUP, BLK, BLK)
                    for k, h in enumerate(heads):
                        delta_t = jnp.where(lane == h, _head_rows(delta, k), delta_t)
                    ds_own, ds_prev = _split_window((ds * SWA_SCALE).astype(BF16), own)
                    p_own, p_prev = _split_window(p.astype(BF16), own)
                    dq4 = _dot(ds_own, kc) + _dot(ds_prev, kp)
                    for r, slab in enumerate(_unstack_heads(dq4, low)):
                        dq_new[:, PAIR * (2 * g + r): PAIR * (2 * g + r + 1)] = slab
                    grads = [_dot_tn(ds_own, q4), _dot_tn(ds_prev, q4), _dot_tn(p_own, do4), _dot_tn(p_prev, do4)]
                    slab_grads = [t + _fold_halves(dk, g, low) for t, dk in zip(slab_grads, grads)]
                ks = slice(PAIR * pair_of_kv, PAIR * (pair_of_kv + 1))
                vs = slice(voff + PAIR * pair_of_kv, voff + PAIR * (pair_of_kv + 1))
                kv_cur[:, ks], kv_prev[:, ks], kv_cur[:, vs], kv_prev[:, vs] = slab_grads
            dsk_ref[...] -= jnp.sum(jnp.exp(skrow_ref[...] - lse_ref[...]) * delta_t, axis=0, keepdims=True)

        @pl.when(n == nb)
        def _():
            kv_prev[...] = jnp.zeros_like(kv_prev)

        dpb_ref[0] = dq_hold[...].astype(BF16)
        dpb_ref[1, :, 0:kvw] = (kv_hold[...] + kv_prev[...]).astype(BF16)
        dpb_ref[1, :, kvw:D] = jnp.zeros((BLK, D - kvw), BF16)

        @pl.when(n < nb)
        def _():
            dq_hold[...] = dq_new[...]
            kv_hold[...] = kv_cur[...]

    def cur(n):
        return jnp.minimum(n, nb - 1)

    call = _pcall(body, name=name, grid=(nb + 1,), carry=carry,
                  in_specs=[pl.BlockSpec((None, BLK, D), lambda n: (0, cur(n), 0)),
                            pl.BlockSpec((None, BLK, kvw), lambda n: (1, cur(n), 0)),
                            pl.BlockSpec((None, BLK, kvw), lambda n: (1, jnp.maximum(cur(n) - 1, 0), 0)),
                            pl.BlockSpec((BLK, D), lambda n: (cur(n), 0)),
                            pl.BlockSpec((BLK, D), lambda n: (cur(n), 0)),
                            pl.BlockSpec((BLK, 128), lambda n: (cur(n), 0)),
                            pl.BlockSpec((N_HEADS, BLK, BLK), lambda n: (0, 0, 0)),
                            pl.BlockSpec((1, 128), lambda n: (0, 0))],
                  out_specs=[pl.BlockSpec((2, BLK, D), lambda n: (0, jnp.maximum(n - 1, 0), 0)),
                             pl.BlockSpec((N_HEADS, BLK, BLK), lambda n: (0, 0, 0)),
                             pl.BlockSpec((1, 128), lambda n: (0, 0))],
                  out_shape=[_sds((2, s, D), BF16), _sds((N_HEADS, BLK, BLK), F32), _sds((1, 128), F32)],
                  scratch=[pltpu.VMEM((BLK, D), F32), pltpu.VMEM((BLK, kvw), F32), pltpu.VMEM((BLK, D), F32),
                           pltpu.VMEM((BLK, kvw), F32), pltpu.VMEM((BLK, kvw), F32)])
    sink_row = jnp.pad(sinks, ((0, 0), (0, 128 - N_HEADS)))
    return _carried(call, (pb, pb, pb, attn, dattn, lse, bias, sink_row), carry)


HALO = 16
CW = D


def _conv_taps(cu, halo_cu, first_tile):
    row = lax.broadcasted_iota(jnp.int32, cu.shape, 0)
    halo_cu = jnp.where(first_tile, 0.0, halo_cu)
    c1 = jnp.where(row == 0, halo_cu[HALO - 1:HALO], pltpu.roll(cu, 1, 0))
    c2 = jnp.where(row == 0, halo_cu[HALO - 2:HALO - 1],
                   jnp.where(row == 1, halo_cu[HALO - 1:HALO], pltpu.roll(cu, 2, 0)))
    return c1, c2


def _conv_merge_fwd(pa, attn, convw, name, ts=256):
    _, s, _ = pa.shape
    ts = min(ts, s)
    hb = ts // HALO

    def body(pa_ref, hp_ref, at_ref, w_ref, o_ref):
        i = pl.program_id(1)
        cu = pa_ref[0].astype(F32) * pa_ref[2].astype(F32)
        c1, c2 = _conv_taps(cu, hp_ref[0].astype(F32) * hp_ref[2].astype(F32), i == 0)
        w = w_ref[...]
        c3 = w[0:1] * c2 + w[1:2] * c1 + w[2:3] * cu
        conv = pa_ref[1].astype(F32) * c3
        o_ref[...] = (jax.nn.sigmoid(pa_ref[3].astype(F32)) * at_ref[...].astype(F32)
                      + jax.nn.sigmoid(pa_ref[4].astype(F32)) * conv).astype(BF16)

    return _pcall(body, name=name, grid=(D // CW, s // ts),
                  in_specs=[pl.BlockSpec((5, ts, CW), lambda c, i: (0, i, c)),
                            pl.BlockSpec((5, HALO, CW), lambda c, i: (0, jnp.maximum(i * hb - 1, 0), c)),
                            pl.BlockSpec((ts, CW), lambda c, i: (i, c)),
                            pl.BlockSpec((8, CW), lambda c, i: (0, c))],
                  out_specs=pl.BlockSpec((ts, CW), lambda c, i: (i, c)),
                  out_shape=_sds((s, D), BF16))(pa, pa, attn, convw)


def _conv_merge_bwd(dmerged, pa, attn, convw, name, ts=256, carry=None):
    _, s, _ = pa.shape
    ts = min(ts, s)
    hb = ts // HALO
    last_hb = s // HALO - 1

    def body(dm_ref, pa_ref, at_ref, w_ref, hp_ref, hn_ref, dmn_ref, dat_ref, dpa_ref, dw_ref):
        i = pl.program_id(1)
        last = i == pl.num_programs(1) - 1
        dm = dm_ref[...].astype(F32)
        cp, bp, u = pa_ref[0].astype(F32), pa_ref[1].astype(F32), pa_ref[2].astype(F32)
        sa = jax.nn.sigmoid(pa_ref[3].astype(F32))
        sc = jax.nn.sigmoid(pa_ref[4].astype(F32))
        at = at_ref[...].astype(F32)
        cu = cp * u
        c1, c2 = _conv_taps(cu, hp_ref[0].astype(F32) * hp_ref[2].astype(F32), i == 0)
        w = w_ref[...]
        c3 = w[0:1] * c2 + w[1:2] * c1 + w[2:3] * cu
        dconv = dm * sc
        dc3 = dconv * bp
        nxt = dmn_ref[...].astype(F32) * jax.nn.sigmoid(hn_ref[4].astype(F32)) * hn_ref[1].astype(F32)
        nxt = jnp.where(last, 0.0, nxt)
        row = lax.broadcasted_iota(jnp.int32, dc3.shape, 0)
        d1 = jnp.where(row == ts - 1, nxt[0:1], pltpu.roll(dc3, ts - 1, 0))
        d2 = jnp.where(row == ts - 2, nxt[0:1], jnp.where(row == ts - 1, nxt[1:2], pltpu.roll(dc3, ts - 2, 0)))
        dcu = w[2:3] * dc3 + w[1:2] * d1 + w[0:1] * d2
        dat_ref[...] = (dm * sa).astype(BF16)
        dpa_ref[0] = (dcu * u).astype(BF16)
        dpa_ref[1] = (dconv * c3).astype(BF16)
        dpa_ref[2] = (dcu * cp).astype(BF16)
        dpa_ref[3] = (dm * at * sa * (1.0 - sa)).astype(BF16)
        dpa_ref[4] = (dm * bp * c3 * sc * (1.0 - sc)).astype(BF16)

        @pl.when(i == 0)
        def _():
            dw_ref[...] = jnp.zeros_like(dw_ref)

        dw_ref[0:1, :] += jnp.sum(dc3 * c2, axis=0, keepdims=True)
        dw_ref[1:2, :] += jnp.sum(dc3 * c1, axis=0, keepdims=True)
        dw_ref[2:3, :] += jnp.sum(dc3 * cu, axis=0, keepdims=True)

    call = _pcall(body, name=name, grid=(D // CW, s // ts), carry=carry,
                  in_specs=[pl.BlockSpec((ts, CW), lambda c, i: (i, c)),
                            pl.BlockSpec((5, ts, CW), lambda c, i: (0, i, c)),
                            pl.BlockSpec((ts, CW), lambda c, i: (i, c)),
                            pl.BlockSpec((8, CW), lambda c, i: (0, c)),
                            pl.BlockSpec((5, HALO, CW), lambda c, i: (0, jnp.maximum(i * hb - 1, 0), c)),
                            pl.BlockSpec((5, HALO, CW), lambda c, i: (0, jnp.minimum((i + 1) * hb, last_hb), c)),
                            pl.BlockSpec((HALO, CW), lambda c, i: (jnp.minimum((i + 1) * hb, last_hb), c))],
                  out_specs=[pl.BlockSpec((ts, CW), lambda c, i: (i, c)),
                             pl.BlockSpec((5, ts, CW), lambda c, i: (0, i, c)),
                             pl.BlockSpec((8, CW), lambda c, i: (0, c))],
                  out_shape=[_sds((s, D), BF16), _sds((5, s, D), BF16), _sds((8, D), F32)])
    return _carried(call, (dmerged, pa, attn, convw, pa, pa, dmerged), carry)


def _xattn_fwd(q, kv, name, tq=512):
    s, _ = q.shape
    nm = kv.shape[1]
    tq = min(tq, s)

    def body(q_ref, kv_ref, o_ref, lse_ref):
        lane = lax.broadcasted_iota(jnp.int32, (tq, 128), 1)
        lse_t = jnp.zeros((tq, 128), F32)
        for h in range(XH):
            hs = slice(XHD * h, XHD * (h + 1))
            sc = _dot_nt(q_ref[:, hs], kv_ref[h]) * (XHD ** -0.5)
            m = jnp.max(sc, axis=1, keepdims=True)
            p = jnp.exp(sc - m)
            l = jnp.sum(p, axis=1, keepdims=True)
            o_ref[:, hs] = (_dot(p.astype(BF16), kv_ref[XH + h]) * (1.0 / l)).astype(BF16)
            lse_t = jnp.where(lane == h, m + jnp.log(l), lse_t)
        lse_ref[...] = lse_t

    return _pcall(body, name=name, grid=(s // tq,),
                  in_specs=[pl.BlockSpec((tq, D), lambda i: (i, 0)), pl.BlockSpec((2 * XH, nm, XHD), lambda i: (0, 0, 0))],
                  out_specs=[pl.BlockSpec((tq, D), lambda i: (i, 0)), pl.BlockSpec((tq, 128), lambda i: (i, 0))],
                  out_shape=[_sds((s, D), BF16), _sds((s, 128), F32)])(q, kv)


def _xattn_bwd(q, kv, o, do, lse, name, tq=512, carry=None):
    s, _ = q.shape
    nm = kv.shape[1]
    tq = min(tq, s)

    def body(q_ref, kv_ref, o_ref, do_ref, lse_ref, dq_ref, dkv_ref):
        @pl.when(pl.program_id(0) == 0)
        def _():
            dkv_ref[...] = jnp.zeros_like(dkv_ref)

        for h in range(XH):
            hs = slice(XHD * h, XHD * (h + 1))
            qh, kh, vh, dob = q_ref[:, hs], kv_ref[h], kv_ref[XH + h], do_ref[:, hs]
            p = jnp.exp(_dot_nt(qh, kh) * (XHD ** -0.5) - lse_ref[:, h:h + 1])
            dp = _dot_nt(dob, vh)
            delta = jnp.sum(dob.astype(F32) * o_ref[:, hs].astype(F32), axis=1, keepdims=True)
            dsb = (p * (dp - delta) * (XHD ** -0.5)).astype(BF16)
            dq_ref[:, hs] = _dot(dsb, kh).astype(BF16)
            dkv_ref[h] += _dot_tn(dsb, qh)
            dkv_ref[XH + h] += _dot_tn(p.astype(BF16), dob)

    call = _pcall(body, name=name, grid=(s // tq,), carry=carry,
                  in_specs=[pl.BlockSpec((tq, D), lambda i: (i, 0)), pl.BlockSpec((2 * XH, nm, XHD), lambda i: (0, 0, 0)),
                            pl.BlockSpec((tq, D), lambda i: (i, 0)), pl.BlockSpec((tq, D), lambda i: (i, 0)),
                            pl.BlockSpec((tq, 128), lambda i: (i, 0))],
                  out_specs=[pl.BlockSpec((tq, D), lambda i: (i, 0)), pl.BlockSpec((2 * XH, nm, XHD), lambda i: (0, 0, 0))],
                  out_shape=[_sds((s, D), BF16), _sds((2 * XH, nm, XHD), F32)])
    return _carried(call, (q, kv, o, do, lse), carry)


def _ffn_down_bwd(dxb, wd4, gu4, name, tm=512, carry=None):
    s, _ = dxb.shape
    tm = min(tm, s)

    def body(dx_ref, w_hbm, gu_ref, o_ref, w_ref, w_sem):
        _load_once(w_hbm, w_ref, w_sem)
        for p in range(4):
            for rows in _row_chunks(tm):
                da = _dot_nt(dx_ref[rows, :], w_ref[p])
                g = gu_ref[0, p, rows, :].astype(F32)
                u = gu_ref[1, p, rows, :].astype(F32)
                sg = jax.nn.sigmoid(g)
                t = da * sg
                o_ref[0, p, rows, :] = (t * u * (1.0 + g - g * sg)).astype(BF16)
                o_ref[1, p, rows, :] = (t * g).astype(BF16)

    block = pl.BlockSpec((2, 4, tm, FS), lambda i: (0, 0, i, 0))
    call = _pcall(body, name=name, grid=(s // tm,), carry=carry,
                  in_specs=[pl.BlockSpec((tm, D), lambda i: (i, 0)), HBM_SPEC, block],
                  out_specs=block, out_shape=_sds((2, 4, s, FS), BF16), scratch=_resident(wd4))
    return _carried(call, (dxb, wd4, gu4), carry)


def _mm_tn(a, b, name, scale=1.0, carry=None):
    pa_n, s, m = a.shape
    pb_n, _, n = b.shape
    po = max(pa_n, pb_n)
    tn = n if po >= 4 else min(n, 256)

    def body(a_ref, b_ref, o_ref):
        o_ref[...] = (scale * _dot_tn(a_ref[...], b_ref[...])).astype(BF16)

    call = _pcall(body, name=name, grid=(po, n // tn), carry=carry,
                  in_specs=[pl.BlockSpec((None, s, m), lambda o, j: (o if pa_n > 1 else 0, 0, 0)),
                            pl.BlockSpec((None, s, tn), lambda o, j: (o if pb_n > 1 else 0, 0, j))],
                  out_specs=pl.BlockSpec((None, m, tn), lambda o, j: (o, 0, j)),
                  out_shape=_sds((po, m, n), BF16))
    return _carried(call, (a, b), carry)


def _sum_dots(a_ref, b_ref, nj, bt, rows=slice(None)):
    dot = _dot_nt if bt else _dot
    acc = dot(a_ref[0, rows, :], b_ref[0])
    for j in range(1, nj):
        acc = acc + dot(a_ref[j, rows, :], b_ref[j])
    return acc


def _mm_acc(a, b, name, out_dtype, tm=512, bt=False, carry=None):
    nj, s, k = a.shape
    n = b.shape[1] if bt else b.shape[2]
    tm = min(tm, s)

    def body(a_ref, b_ref, o_ref):
        o_ref[...] = _sum_dots(a_ref, b_ref, nj, bt).astype(out_dtype)

    call = _pcall(body, name=name, grid=(s // tm,), carry=carry,
                  in_specs=[pl.BlockSpec((nj, tm, k), lambda i: (0, i, 0)),
                            pl.BlockSpec(b.shape, lambda i: (0, 0, 0))],
                  out_specs=pl.BlockSpec((tm, n), lambda i: (i, 0)), out_shape=_sds((s, n), out_dtype))
    return _carried(call, (a, b), carry)


def _mm_acc_rms_bwd(pairs, name, *, x, gain, dres, scale=None, tm=512, bt=False, carry=None):
    npairs = len(pairs)
    s = pairs[0][0].shape[1]
    n = pairs[0][1].shape[1] if bt else pairs[0][1].shape[2]
    tm = min(tm, s)

    def body(*refs):
        a_refs, b_hbms = refs[:npairs], refs[npairs:2 * npairs]
        x_ref, g_ref, r_ref, dx_ref, dxb_ref, dg_ref = refs[2 * npairs:2 * npairs + 6]
        resident = refs[2 * npairs + 6:]
        b_refs = resident[0::2]
        for b_hbm, b_ref, b_sem in zip(b_hbms, b_refs, resident[1::2]):
            _load_once(b_hbm, b_ref, b_sem)

        @pl.when(pl.program_id(0) == 0)
        def _():
            dg_ref[...] = jnp.zeros_like(dg_ref)

        for rows in _row_chunks(tm):
            dh = _sum_dots(a_refs[0], b_refs[0], pairs[0][0].shape[0], bt, rows)
            for a_ref, b_ref, (a, _) in zip(a_refs[1:], b_refs[1:], pairs[1:]):
                dh = dh + _sum_dots(a_ref, b_ref, a.shape[0], bt, rows)
            if scale is not None:
                dh = scale * dh
            xv = x_ref[rows, :]
            r = lax.rsqrt(jnp.mean(xv * xv, axis=-1, keepdims=True) + EPS)
            xh = xv * r
            dyg = dh * g_ref[...]
            dx = r_ref[rows, :] + r * (dyg - xh * jnp.mean(dyg * xh, axis=-1, keepdims=True))
            dx_ref[rows, :] = dx
            dxb_ref[rows, :] = dx.astype(BF16)
            dg_ref[...] += jnp.sum(dh * xh, axis=0, keepdims=True)

    row = pl.BlockSpec((tm, n), lambda i: (i, 0))
    in_specs = ([pl.BlockSpec((a.shape[0], tm, a.shape[2]), lambda i: (0, i, 0)) for a, _ in pairs]
                + [HBM_SPEC] * npairs + [row, pl.BlockSpec((1, n), lambda i: (0, 0)), row])
    args = tuple(a for a, _ in pairs) + tuple(b for _, b in pairs) + (x, gain, dres)
    call = _pcall(body, name=name, grid=(s // tm,), in_specs=in_specs, carry=carry,
                  out_specs=[row, row, pl.BlockSpec((1, n), lambda i: (0, 0))],
                  out_shape=[_sds((s, n), F32), _sds((s, n), BF16), _sds((1, n), F32)],
                  scratch=[t for _, b in pairs for t in _resident(b)])
    return _carried(call, args, carry)


def _adam(w, g, m, v):
    m2 = ADAM_B1 * m + (1.0 - ADAM_B1) * g
    v2 = ADAM_B2 * v + (1.0 - ADAM_B2) * (g * g)
    m_hat = m2 / (1.0 - ADAM_B1 ** ADAM_STEP)
    v_hat = v2 / (1.0 - ADAM_B2 ** ADAM_STEP)
    delta = -ADAM_LR * (m_hat / (jnp.sqrt(v_hat) + ADAM_EPS) + ADAM_WD * w)
    return delta, m2, v2


def _adamw(parts, w, m, v, name):
    _, r, c = parts.shape
    tr = max(t for t in range(16, 257, 16) if r % t == 0)

    def body(p_ref, w_ref, m_ref, v_ref, g_ref, d_ref, m2_ref, v2_ref):
        g = p_ref[0].astype(F32)
        for i in range(1, N_DEV):
            g = g + p_ref[i].astype(F32)
        delta, m2, v2 = _adam(w_ref[...], g, m_ref[...], v_ref[...])
        g_ref[...] = g
        d_ref[...] = delta
        m2_ref[...] = m2
        v2_ref[...] = v2

    blk = pl.BlockSpec((tr, c), lambda i: (i, 0))
    return _pcall(body, name=name, grid=(r // tr,),
                  in_specs=[pl.BlockSpec((N_DEV, tr, c), lambda i: (0, i, 0)), blk, blk, blk],
                  out_specs=[blk] * 4, out_shape=[_sds((r, c), F32)] * 4)(parts, w, m, v)


def _position():
    return lax.axis_index("x"), lax.axis_index("y"), lax.axis_index("c")


def _slot(px, py, pc):
    return 4 * px + 2 * py + pc


def _row_window(ref, rows):
    r0, r1 = rows
    return ref if (r0, r1) == (0, ref.shape[0]) else ref.at[pl.ds(r0, r1 - r0)]


def _split_items(items):
    sources = [src for src, _, _ in items]
    begun = [(a, dest) for a, (_, _, dest) in enumerate(items) if dest is not None]
    aliases = {len(sources) + k: a for k, (a, _) in enumerate(begun)}
    return sources + [dest for _, dest in begun], [rows for _, rows, _ in items], aliases


def _gather_carry(items):
    na = len(items)
    carry_ins, windows, aliases = _split_items(items)

    def plan(ins, outs, sems):
        send_sems, recv_sems, local_sems = sems
        x, y, c = _position()
        me, sibling = (x, y, c), (x, y, 1 - c)
        chips = [(1 - x, y), (x, 1 - y), (1 - x, 1 - y)]
        ins = [_row_window(ins[a], windows[a]) for a in range(na)]

        def block_rows(a, block):
            return _row_window(outs[a].at[_slot(*block)], windows[a])

        def copy(a, k, block, to, src=None):
            rows = block_rows(a, block)
            return pltpu.make_async_remote_copy(src_ref=rows if src is None else src, dst_ref=rows,
                                                send_sem=send_sems.at[k, a], recv_sem=recv_sems.at[k, a],
                                                device_id=to, device_id_type=MESH)

        mine = [pltpu.make_async_copy(ins[a], block_rows(a, me), local_sems.at[a]) for a in range(na)]
        first = [copy(a, 0, me, sibling, src=ins[a]) for a in range(na)]
        for j, chip in enumerate(chips):
            first += [copy(a, 1 + j, me, (*chip, c), src=ins[a]) for a in range(na)]
        landed = [[copy(a, 1 + j, (*chip, c), me) for a in range(na)] for j, chip in enumerate(chips)]
        passed = [[copy(a, 4 + j, (*chip, c), sibling) for a in range(na)] for j, chip in enumerate(chips)]
        from_sibling = [copy(a, 0, sibling, me) for a in range(na)]
        for j, chip in enumerate(chips):
            from_sibling += [copy(a, 4 + j, (*chip, 1 - c), me) for a in range(na)]
        return mine, first, landed, passed, from_sibling

    def start(ins, outs, sems):
        mine, first, _, _, _ = plan(ins, outs, sems)
        for cp in mine + first:
            cp.start()

    def mid(ins, outs, sems):
        _, _, landed, passed, _ = plan(ins, outs, sems)
        for over_ici, onward in zip(landed, passed):
            for cp, fwd in zip(over_ici, onward):
                cp.wait_recv()
                fwd.start()

    def finish(ins, outs, sems):
        mine, first, _, passed, from_sibling = plan(ins, outs, sems)
        for cp in from_sibling:
            cp.wait_recv()
        for cp in first + [fwd for onward in passed for fwd in onward]:
            cp.wait_send()
        for cp in mine:
            cp.wait()

    return _Carry(carry_ins, [_sds((N_DEV,) + src.shape, src.dtype) for src, _, _ in items],
                  [pltpu.SemaphoreType.DMA((7, na)), pltpu.SemaphoreType.DMA((7, na)),
                   pltpu.SemaphoreType.DMA((na,))], start, finish, mid, aliases)


def _exchange_carry(scattered, replicated=()):
    items = list(scattered) + [(a, (0, a.shape[0]), None) for a in replicated]
    na, ns = len(items), len(scattered)
    carry_ins, windows, aliases = _split_items(items)

    def plan(ins, outs, sems):
        send_sems, recv_sems, local_sems = sems
        me = _slot(*_position())

        def source(a, j):
            return _row_window(ins[a].at[j] if a < ns else ins[a], windows[a])

        def copy(a, j, i):
            return pltpu.make_async_remote_copy(src_ref=source(a, j), dst_ref=_row_window(outs[a].at[i], windows[a]),
                                                send_sem=send_sems.at[j, a], recv_sem=recv_sems.at[i, a],
                                                device_id=(j >> 2, (j >> 1) & 1, j & 1), device_id_type=MESH)

        def own(a, j):
            return pltpu.make_async_copy(source(a, j), _row_window(outs[a].at[j], windows[a]), local_sems.at[a])

        return me, copy, own

    def start(ins, outs, sems):
        me, copy, own = plan(ins, outs, sems)
        for a in range(na):
            for j in range(N_DEV):
                @pl.when(me == j)
                def _():
                    own(a, j).start()

                @pl.when(me != j)
                def _():
                    copy(a, j, me).start()

    def finish(ins, outs, sems):
        me, copy, own = plan(ins, outs, sems)
        for a in range(na):
            for j in range(N_DEV):
                @pl.when(me == j)
                def _():
                    for i in range(N_DEV):
                        if i != j:
                            copy(a, j, i).wait_recv()
                    own(a, j).wait()

                @pl.when(me != j)
                def _():
                    copy(a, j, me).wait_send()

    return _Carry(carry_ins, [_sds((N_DEV,) + src.shape[-2:], src.dtype) for src, _, _ in items],
                  [pltpu.SemaphoreType.DMA((N_DEV, na)), pltpu.SemaphoreType.DMA((N_DEV, na)),
                   pltpu.SemaphoreType.DMA((na,))], start, finish, None, aliases)


HBM_ARRAY = pl.BlockSpec(memory_space=pltpu.HBM)
SEMAPHORES = pl.BlockSpec(memory_space=pltpu.SEMAPHORE)
DATAFLOW = pltpu.SideEffectType.DATAFLOW_SIDE_EFFECTING


def _exchange_copy(parts_ref, land_ref, send_sems, recv_sems, me, j):
    return pltpu.make_async_remote_copy(src_ref=parts_ref.at[j], dst_ref=land_ref.at[me], send_sem=send_sems.at[j],
                                        recv_sem=recv_sems.at[me], device_id=(j >> 2, (j >> 1) & 1, j & 1),
                                        device_id_type=MESH)


def _exchange_start(parts, name):
    def body(parts_ref, land_ref, send_sems, recv_sems, parts_thru, land_thru, token):
        me = _slot(*_position())
        for j in range(N_DEV):
            @pl.when(me == j)
            def _():
                pltpu.make_async_copy(parts_ref.at[j], land_ref.at[j], send_sems.at[j]).start()

            @pl.when(me != j)
            def _():
                _exchange_copy(parts_ref, land_ref, send_sems, recv_sems, me, j).start()
        token[...] = jnp.zeros_like(token)

    return pl.pallas_call(
        body, name=name,
        out_shape=(pltpu.SemaphoreType.DMA((N_DEV,)), pltpu.SemaphoreType.DMA((N_DEV,)),
                   pltpu.HBM(parts.shape, parts.dtype), pltpu.HBM(parts.shape, parts.dtype), _sds((8, 128), F32)),
        in_specs=(HBM_ARRAY, HBM_ARRAY),
        out_specs=(SEMAPHORES, SEMAPHORES, HBM_ARRAY, HBM_ARRAY, pl.BlockSpec(memory_space=pltpu.VMEM)),
        input_output_aliases={0: 2, 1: 3}, compiler_params=pltpu.CompilerParams(has_side_effects=DATAFLOW),
    )(pltpu.with_memory_space_constraint(parts, pltpu.HBM),
      pltpu.with_memory_space_constraint(lax.empty(parts.shape, parts.dtype), pltpu.HBM))


def _exchange_wait(send_sems, recv_sems, parts_thru, land_thru, after, name):
    def body(parts_ref, land_ref, send_sems, recv_sems, after_ref, parts_dead, got_ref):
        me = _slot(*_position())
        for j in range(N_DEV):
            @pl.when(me == j)
            def _():
                pltpu.make_async_copy(parts_ref.at[j], land_ref.at[j], send_sems.at[j]).wait()

            @pl.when(me != j)
            def _():
                both = pltpu.make_async_remote_copy(src_ref=parts_ref.at[j], dst_ref=land_ref.at[j],
                                                    send_sem=send_sems.at[j], recv_sem=recv_sems.at[j],
                                                    device_id=(j >> 2, (j >> 1) & 1, j & 1), device_id_type=MESH)
                both.wait_send()
                both.wait_recv()

    return pl.pallas_call(
        body, name=name, out_shape=(pltpu.HBM(parts_thru.shape, parts_thru.dtype),
                                    pltpu.HBM(parts_thru.shape, parts_thru.dtype)),
        in_specs=(HBM_ARRAY, HBM_ARRAY, SEMAPHORES, SEMAPHORES, pl.BlockSpec(memory_space=pl.ANY)),
        out_specs=(HBM_ARRAY, HBM_ARRAY), input_output_aliases={0: 0, 1: 1},
        compiler_params=pltpu.CompilerParams(has_side_effects=DATAFLOW),
    )(parts_thru, land_thru, send_sems, recv_sems, after)[1]


NQ, NKV = N_HEADS * HEAD, 2 * N_KV * HEAD


class _Mesh:
    def __init__(self, shards):
        self.shards, self.full, self.received, self.cache = shards, {}, {}, {}

    def fetch(self, wanted):
        items = []
        for want in wanted:
            name, r0, r1 = want if isinstance(want, tuple) else (want, 0, self.shards[want].shape[0])
            items.append((self.shards[name], (r0, r1), self.full.get(name)))
        return _gather_carry(items)

    def fetched(self, wanted, results):
        self.full.update(zip([want[0] if isinstance(want, tuple) else want for want in wanted], results))

    def send(self, *payloads):
        return _exchange_carry([(parts, rows or (0, parts.shape[1]), self.received.get(name))
                                for name, parts, rows in payloads])

    def sent(self, names, results):
        self.received.update(zip(names, results))

    def send_apart(self, name, parts):
        *self.pending, token = _exchange_start(parts, "exchange_" + name + "_start")
        self.pending_name = name
        return token

    def sent_apart(self, after):
        self.received[self.pending_name] = _exchange_wait(*self.pending, after, "exchange_" + self.pending_name + "_wait")

    def w(self, key):
        if key not in self.cache:
            self.cache[key] = self._layout(key)
        return self.cache[key]

    def _layout(self, key):
        if key in ("gu1", "gu2"):
            return self.full[key]
        if key in ("d1", "d2"):
            return self.full[key].reshape(4, FS, D)
        if key in ("out", "q", "o"):
            return self.full[key].reshape(D, D)
        if key == "kv":
            return self.full["kv"]
        if key == "convw":
            rows = self.full["conv"][:, :3, :].transpose(1, 0, 2).reshape(3, D)
            return jnp.concatenate([rows, jnp.zeros((5, D), F32)], axis=0)
        w_in_t = self.full["win"].reshape(-1, D)
        if key == "wa":
            return w_in_t[NQ + NKV:].reshape(5, D, D)
        assert key == "wb", key
        return jnp.stack([w_in_t[:NQ], jnp.pad(w_in_t[NQ:NQ + NKV], ((0, D - NKV), (0, 0)))])


def _w_in_parts(dw_a, dw_b):
    return jnp.concatenate([dw_b[0], dw_b[1][:NKV], dw_a.reshape(5 * D, D)], axis=0).reshape(N_DEV, -1, D)


def _forward_backward(x, mem, target, g, rel_bias, sinks, ex):
    s = x.shape[0]
    def fetching(wanted, call, *args, **kw):
        res, got = call(*args, carry=ex.fetch(wanted), **kw)
        ex.fetched(wanted, got)
        return res

    h1 = fetching(["gu1", "conv"], _rmsnorm, x, g["ffn1"], "norm_ffn1")
    gu1, a1 = fetching(["d1", ("win", 0, 400)], _ffn_up, h1, ex.w("gu1").reshape(2, 4, FS, D), "ffn1_up")
    x1, h2 = fetching([("win", 400, 832)], _mm_res_norm, a1, ex.w("d1"), x, g["mix"], 0.5, "ffn1_down")
    pa, pb = fetching(["gu2", "out", "q"], _in_proj, h2, ex.w("wa"), ex.w("wb"), "in_proj")
    biasm = _bias_build(rel_bias, "bias_build")
    attn, lse = fetching(["kv", "d2", "o"], _swa_fwd, pb, biasm, sinks, "swa_fwd")
    merged = _conv_merge_fwd(pa, attn, ex.w("convw"), "conv_merge_fwd")
    (x2, h3), _ = _mm_res_norm(merged[None], ex.w("out")[None], x1, g["xattn"], 1.0, "out_proj")
    q2 = _mm_nn(h3, ex.w("q")[None], "xattn_q")[0][0]
    mh, _ = _rmsnorm(mem, g["mem"], "norm_mem")
    kv2 = _mm_nn(mh, ex.w("kv"), "xattn_kv")[0]
    o, lse2 = _xattn_fwd(q2, kv2, "xattn_fwd")
    (x3, h4), _ = _mm_res_norm(o[None], ex.w("o")[None], x2, g["ffn2"], 1.0, "xattn_o")
    (gu2, a2), _ = _ffn_up(h4, ex.w("gu2").reshape(2, 4, FS, D), "ffn2_up")
    dx4, dx4b, loss, d_final = _ffn_down_loss(a2, ex.w("d2"), x3, g["final"], target, "ffn2_down_loss")
    def sending(payloads, call, *args, **kw):
        res, got = call(*args, carry=ex.send(*payloads), **kw)
        ex.sent([name for name, _, _ in payloads], got)
        return res

    dw_d2 = _mm_tn(a2, dx4b[None], "dw_ffn2_down", scale=0.5)[0].reshape(N_DEV, -1, D)
    dgu2 = sending([("d2", dw_d2, None)], _ffn_down_bwd, dx4b, ex.w("d2"), gu2, "ffn2_down_bwd").reshape(8, s, FS)
    dw_gu2 = _mm_tn(dgu2, h4[None], "dw_ffn2_up", scale=0.5)[0]
    dx3, dx3b, d_ffn2 = sending([("gu2", dw_gu2, (0, 400))], _mm_acc_rms_bwd, [(dgu2, ex.w("gu2"))], "ffn2_up_bwd",
                                x=x3, gain=g["ffn2"], dres=dx4, scale=0.5)
    do, _ = _mm_acc(dx3b[None], ex.w("o")[None], "xattn_o_bwd", BF16, bt=True)
    dw_o = _mm_tn(o[None], dx3b[None], "dw_xattn_o")[0].reshape(N_DEV, -1, D)
    dq2, dkv2 = sending([("o", dw_o, None)], _xattn_bwd, q2, kv2, o, do, lse2, "xattn_bwd")
    dkv2b = dkv2.astype(BF16)
    dw_q = _mm_tn(h3[None], dq2[None], "dw_xattn_q")[0].reshape(N_DEV, -1, D)
    dx2, dx2b, d_xattn = sending([("q", dw_q, None)], _mm_acc_rms_bwd, [(dq2[None], ex.w("q")[None])],
                                 "xattn_q_bwd", x=x2, gain=g["xattn"], dres=dx3, bt=True)
    dw_kv = _mm_tn(mh[None], dkv2b, "dw_xattn_kv")[0]
    (_, _, d_mem), _ = _mm_acc_rms_bwd([(dkv2b, ex.w("kv"))], "xattn_kv_bwd", x=mem, gain=g["mem"],
                                       dres=jnp.zeros_like(mem), bt=True)
    dmerged, _ = _mm_acc(dx2b[None], ex.w("out")[None], "out_proj_bwd", BF16, bt=True)
    dw_out = _mm_tn(merged[None], dx2b[None], "dw_out_proj")[0].reshape(N_DEV, -1, D)
    dattn, dpa, d_convw = sending([("kv", dw_kv, None)], _conv_merge_bwd,
                                  dmerged, pa, attn, ex.w("convw"), "conv_merge_bwd")
    dpb, dbias, d_sinks = sending([("gu2", dw_gu2, (400, FS)), ("out", dw_out, None)], _swa_bwd,
                                  pb, attn, dattn, lse, biasm, sinks, "swa_bwd")
    d_relb = _bias_bwd(dbias, "bias_bwd")
    dw_in = _w_in_parts(_mm_tn(dpa, h2[None], "dw_in_proj_a")[0], _mm_tn(dpb, h2[None], "dw_in_proj_b")[0])
    dx1, dx1b, d_mix = sending([("win", dw_in, (0, 672))], _mm_acc_rms_bwd,
                               [(dpa, ex.w("wa")), (dpb, ex.w("wb"))], "in_proj_bwd",
                               x=x1, gain=g["mix"], dres=dx2)
    dw_d1 = sending([("win", dw_in, (672, 832))], _mm_tn, a1, dx1b[None], "dw_ffn1_down", scale=0.5)
    dw_d1 = dw_d1.reshape(N_DEV, -1, D)
    dgu1 = sending([("d1", dw_d1, None)], _ffn_down_bwd, dx1b, ex.w("d1"), gu1, "ffn1_down_bwd").reshape(8, s, FS)
    dw_gu1 = _mm_tn(dgu1, h1[None], "dw_ffn1_up", scale=0.5)[0]
    token = ex.send_apart("gu1", dw_gu1)
    (dx0, _, d_ffn1), _ = _mm_acc_rms_bwd([(dgu1, ex.w("gu1"))], "ffn1_up_bwd", x=x,
                                          gain=g["ffn1"] + token[0:1, 0:1], dres=dx1, scale=0.5)

    relb_row = jnp.concatenate([d_relb[:, :REL_BUCKETS].T.reshape(1, REL_BUCKETS * N_HEADS), d_sinks[:, :N_HEADS],
                                jnp.zeros((1, D - REL_BUCKETS * N_HEADS - N_HEADS), F32)], axis=1)
    loss_row = jnp.concatenate([loss[0:1, 0:1], jnp.zeros((1, D - 1), F32)], axis=1)
    small = jnp.concatenate([d_ffn1, d_mix, d_xattn, d_mem, d_ffn2, d_final, relb_row, loss_row, d_convw[0:3],
                             jnp.zeros((SMALL_ROWS - ROW_CONV - 3, D), F32)], axis=0)
    return dx0, small


def _pack_small(norms, final, relb, sinks, conv_local, me):
    relb_row = jnp.concatenate([relb.reshape(1, -1), sinks.reshape(1, -1),
                                jnp.zeros((1, D - REL_BUCKETS * N_HEADS - N_HEADS), F32)], axis=1)
    conv_rows = lax.dynamic_update_slice(jnp.zeros((3, D), F32), conv_local.reshape(3, -1), (0, 128 * me))
    return jnp.concatenate(list(norms) + [final.reshape(1, D), relb_row, jnp.zeros((1, D), F32), conv_rows,
                                          jnp.zeros((SMALL_ROWS - ROW_CONV - 3, D), F32)], axis=0)


def kernel(x, mem, positions, rel_bias, ffn1_norm, ffn1_w_gu, ffn1_w_down, mix_norm, w_in, sinks, conv_w, w_out, xattn_norm, mem_norm, xattn_wq, xattn_wkv, xattn_wo, ffn2_norm, ffn2_w_gu, ffn2_w_down, final_norm, loss_target, m_rel_bias, m_ffn1_norm, m_ffn1_w_gu, m_ffn1_w_down, m_mix_norm, m_w_in, m_sinks, m_conv_w, m_w_out, m_xattn_norm, m_mem_norm, m_xattn_wq, m_xattn_wkv, m_xattn_wo, m_ffn2_norm, m_ffn2_w_gu, m_ffn2_w_down, m_final_norm, v_rel_bias, v_ffn1_norm, v_ffn1_w_gu, v_ffn1_w_down, v_mix_norm, v_w_in, v_sinks, v_conv_w, v_w_out, v_xattn_norm, v_mem_norm, v_xattn_wq, v_xattn_wkv, v_xattn_wo, v_ffn2_norm, v_ffn2_w_gu, v_ffn2_w_down, v_final_norm):
    del positions
    me = _slot(*_position())
    big = dict(gu1=(ffn1_w_gu, m_ffn1_w_gu, v_ffn1_w_gu), d1=(ffn1_w_down, m_ffn1_w_down, v_ffn1_w_down),
               win=(w_in, m_w_in, v_w_in), out=(w_out, m_w_out, v_w_out), q=(xattn_wq, m_xattn_wq, v_xattn_wq),
               kv=(xattn_wkv, m_xattn_wkv, v_xattn_wkv), o=(xattn_wo, m_xattn_wo, v_xattn_wo),
               gu2=(ffn2_w_gu, m_ffn2_w_gu, v_ffn2_w_gu), d2=(ffn2_w_down, m_ffn2_w_down, v_ffn2_w_down))
    order = list(big)
    transposed = ("gu1", "gu2", "win")
    local = {k: tuple(t[0].T if k in transposed else t[0] for t in big[k]) for k in order}
    shards = {k: local[k][0].astype(BF16) for k in order}
    shards["conv"] = jnp.concatenate([conv_w[0], jnp.zeros((5, 128), F32)], axis=0)
    ex = _Mesh(shards)
    gains = dict(ffn1=ffn1_norm, mix=mix_norm, xattn=xattn_norm, mem=mem_norm, ffn2=ffn2_norm,
                 final=final_norm.reshape(1, D))
    dx, small = _forward_backward(x[0], mem[0], loss_target[0], gains, rel_bias, sinks, ex)
    small_parts = _run_alone(_exchange_carry([], [small]), "exchange_small")[0]
    big_out = {k: _adamw(ex.received[k], *local[k], "adamw_" + k) for k in order if k != "gu1"}
    packed = [_pack_small(norms, final, relb, sk, conv, me) for norms, final, relb, sk, conv in (
        ((ffn1_norm, mix_norm, xattn_norm, mem_norm, ffn2_norm), final_norm, rel_bias, sinks, conv_w),
        ((m_ffn1_norm, m_mix_norm, m_xattn_norm, m_mem_norm, m_ffn2_norm), m_final_norm, m_rel_bias, m_sinks, m_conv_w),
        ((v_ffn1_norm, v_mix_norm, v_xattn_norm, v_mem_norm, v_ffn2_norm), v_final_norm, v_rel_bias, v_sinks, v_conv_w))]
    small_out = _adamw(small_parts, *packed, "adamw_small")
    done = [dx[0:1, 0:1], small_out[1][0:1, 0:1]] + [big_out[k][1][0:1, 0:1] for k in big_out]
    ex.sent_apart(after=sum(done))
    big_out["gu1"] = _adamw(ex.received["gu1"], *local["gu1"], "adamw_gu1")
    big_out = {k: [t.T if k in transposed else t for t in big_out[k]] for k in order}

    def unpack(t):
        conv = lax.dynamic_slice(t[ROW_CONV:ROW_CONV + 3], (0, 128 * me), (3, 128))[None]
        nrel = REL_BUCKETS * N_HEADS
        return dict(ffn1_norm=t[0:1], mix_norm=t[1:2], xattn_norm=t[2:3], mem_norm=t[3:4], ffn2_norm=t[4:5],
                    final_norm=t[5], rel_bias=t[ROW_RELB, :nrel].reshape(REL_BUCKETS, N_HEADS),
                    sinks=t[ROW_RELB:ROW_RELB + 1, nrel:nrel + N_HEADS], conv_w=conv)

    names = dict(gu1="ffn1_w_gu", d1="ffn1_w_down", win="w_in", out="w_out", q="xattn_wq", kv="xattn_wkv",
                 o="xattn_wo", gu2="ffn2_w_gu", d2="ffn2_w_down")
    results = []
    for idx in range(4):
        leaves = unpack(small_out[idx])
        leaves.update({names[k]: big_out[k][idx][None] for k in order})
        results.append(leaves)
    weights = ("rel_bias", "ffn1_norm", "ffn1_w_gu", "ffn1_w_down", "mix_norm", "w_in", "sinks", "conv_w", "w_out",
               "xattn_norm", "mem_norm", "xattn_wq", "xattn_wkv", "xattn_wo", "ffn2_norm", "ffn2_w_gu", "ffn2_w_down",
               "final_norm")
    loss = small_out[0][ROW_LOSS, 0]
    return (loss, dx[None], *[leaves[n] for leaves in results for n in weights])
```

```python
import math

import numpy as np
import jax
import jax.numpy as jnp
from jax import lax
from jax.experimental import pallas as pl
from jax.experimental.pallas import tpu as pltpu

F32, BF16 = jnp.float32, jnp.bfloat16
MESH = pl.DeviceIdType.MESH

D = 1024
N_DEV = 8
D_FF = 2816
FS = D_FF // 4
HEAD = 64
N_HEADS, N_KV = 16, 4
BLK = 128
XH, XHD = 4, 256
REL_BUCKETS, REL_EXACT, REL_MAX_DIST = 32, 16, 128
EPS, NEG = 1e-6, -1e30
ADAM_LR, ADAM_B1, ADAM_B2, ADAM_EPS, ADAM_WD, ADAM_STEP = 0.001, 0.9, 0.999, 1e-08, 0.01, 10
VMEM_LIMIT_V7X = 56 * 2**20
SMALL_ROWS = 16
ROW_RELB, ROW_LOSS, ROW_CONV = 6, 7, 8


def _bucket_thresholds():
    n = np.arange(REL_MAX_DIST)
    nf = np.maximum(n, 1).astype(np.float32)
    large = REL_EXACT + (np.log(nf / np.float32(REL_EXACT)) / np.float32(math.log(REL_MAX_DIST / REL_EXACT))
                         * np.float32(REL_BUCKETS - REL_EXACT)).astype(np.int32)
    b = np.where(n < REL_EXACT, n, np.minimum(large, REL_BUCKETS - 1))
    return [int(np.argmax(b >= REL_EXACT + k)) for k in range(1, REL_BUCKETS - REL_EXACT)]


BUCKET_THRESHOLDS = _bucket_thresholds()


HBM_SPEC = pl.BlockSpec(memory_space=pl.ANY)


class _Carry:
    def __init__(self, ins, outs, sems, start, finish, mid=None, aliases=None):
        self.ins, self.outs, self.sems = list(ins), list(outs), list(sems)
        self.start, self.finish, self.mid, self.aliases = start, finish, mid, dict(aliases or {})


def _pcall(body, *, name, grid, in_specs, out_specs, out_shape, scratch=(), carry=None):
    params = pltpu.CompilerParams(dimension_semantics=("arbitrary",) * len(grid), vmem_limit_bytes=VMEM_LIMIT_V7X)
    if carry is None:
        return pl.pallas_call(body, name=name, grid=grid, in_specs=in_specs, out_specs=out_specs,
                              out_shape=out_shape, scratch_shapes=list(scratch), compiler_params=params)
    single = not isinstance(out_shape, (list, tuple))
    own_specs, own_shapes = ([out_specs], [out_shape]) if single else (list(out_specs), list(out_shape))
    n_in, n_out, n_scr = len(in_specs), len(own_shapes), len(scratch)
    n_cin, n_cout = len(carry.ins), len(carry.outs)
    steps = math.prod(grid)
    mid_step = max(steps - 1 - max(steps // 8, 1), 0)

    def carrying(*refs):
        ins, refs = refs[:n_in], refs[n_in:]
        cins, refs = refs[:n_cin], refs[n_cin:]
        outs, refs = refs[:n_out], refs[n_out:]
        couts, refs = refs[:n_cout], refs[n_cout:]
        scr, csems = refs[:n_scr], refs[n_scr:]
        step = 0
        for axis, size in enumerate(grid):
            step = step * size + pl.program_id(axis)

        @pl.when(step == 0)
        def _():
            carry.start(cins, couts, csems)

        body(*ins, *outs, *scr)
        if carry.mid is not None:
            @pl.when(step == mid_step)
            def _():
                carry.mid(cins, couts, csems)

        @pl.when(step == steps - 1)
        def _():
            carry.finish(cins, couts, csems)

    call = pl.pallas_call(carrying, name=name, grid=grid, in_specs=list(in_specs) + [HBM_SPEC] * n_cin,
                          out_specs=own_specs + [HBM_SPEC] * n_cout, out_shape=own_shapes + carry.outs,
                          scratch_shapes=list(scratch) + carry.sems, compiler_params=params,
                          input_output_aliases={n_in + i: n_out + o for i, o in carry.aliases.items()})

    def run(*args):
        res = call(*args, *carry.ins)
        return (res[0] if single else res[:n_out]), res[n_out:]

    return run


def _run_alone(carry, name):
    n_cin, n_cout = len(carry.ins), len(carry.outs)

    def body(*refs):
        cins, couts, csems = refs[:n_cin], refs[n_cin:n_cin + n_cout], refs[n_cin + n_cout:]
        carry.start(cins, couts, csems)
        if carry.mid is not None:
            carry.mid(cins, couts, csems)
        carry.finish(cins, couts, csems)

    return pl.pallas_call(body, name=name, in_specs=[HBM_SPEC] * n_cin, out_specs=[HBM_SPEC] * n_cout,
                          out_shape=carry.outs, scratch_shapes=carry.sems,
                          input_output_aliases=carry.aliases)(*carry.ins)


def _dot(a, b):
    return jnp.dot(a, b, preferred_element_type=F32)


def _dot_nt(a, b):
    return lax.dot_general(a, b, (((1,), (1,)), ((), ())), preferred_element_type=F32)


def _dot_tn(a, b):
    return lax.dot_general(a, b, (((0,), (0,)), ((), ())), preferred_element_type=F32)


def _sds(shape, dtype):
    return jax.ShapeDtypeStruct(tuple(shape), dtype)


ROW_CHUNK = 256


def _row_chunks(tm):
    return [slice(r, min(r + ROW_CHUNK, tm)) for r in range(0, tm, ROW_CHUNK)]


def _carried(call, args, carry):
    return call(*args) if carry is not None else (call(*args), ())


def _rmsnorm(x, g, name, carry=None):
    m, d = x.shape
    tm = min(512, m)

    def body(x_ref, g_ref, h_ref):
        xv = x_ref[...]
        r = lax.rsqrt(jnp.mean(xv * xv, axis=-1, keepdims=True) + EPS)
        h_ref[...] = (xv * r * g_ref[...]).astype(BF16)

    call = _pcall(body, name=name, grid=(m // tm,), carry=carry,
                  in_specs=[pl.BlockSpec((tm, d), lambda i: (i, 0)), pl.BlockSpec((1, d), lambda i: (0, 0))],
                  out_specs=pl.BlockSpec((tm, d), lambda i: (i, 0)), out_shape=_sds((m, d), BF16))
    return _carried(call, (x, g), carry)


def _mm_nn(a, b, name, tm=1024, bt=False, carry=None):
    m, k = a.shape
    nj = b.shape[0]
    n = b.shape[1] if bt else b.shape[2]
    tm = min(tm, m)
    dot = _dot_nt if bt else _dot

    def body(a_ref, b_ref, o_ref):
        o_ref[...] = dot(a_ref[...], b_ref[...]).astype(BF16)

    call = _pcall(body, name=name, grid=(nj, m // tm),
                  in_specs=[pl.BlockSpec((tm, k), lambda j, i: (i, 0)),
                            pl.BlockSpec((None,) + b.shape[1:], lambda j, i: (j, 0, 0))],
                  out_specs=pl.BlockSpec((None, tm, n), lambda j, i: (j, i, 0)),
                  out_shape=_sds((nj, m, n), BF16), carry=carry)
    return _carried(call, (a, b), carry)


def _load_once(src_hbm, dst_vmem, sem):
    @pl.when(pl.program_id(0) == 0)
    def _():
        load = pltpu.make_async_copy(src_hbm, dst_vmem, sem)
        load.start()
        load.wait()


def _resident(w):
    return [pltpu.VMEM(w.shape, w.dtype), pltpu.SemaphoreType.DMA(())]


def _ffn_up(h, w4, name, tm=512, carry=None):
    s, d = h.shape
    tm = min(tm, s)

    def body(h_ref, w_hbm, gu_ref, a_ref, w_ref, w_sem):
        _load_once(w_hbm, w_ref, w_sem)
        for p in range(4):
            for rows in _row_chunks(tm):
                hv = h_ref[rows, :]
                g = _dot_nt(hv, w_ref[0, p])
                u = _dot_nt(hv, w_ref[1, p])
                gu_ref[0, p, rows, :] = g.astype(BF16)
                gu_ref[1, p, rows, :] = u.astype(BF16)
                a_ref[p, rows, :] = (g * jax.nn.sigmoid(g) * u).astype(BF16)

    call = _pcall(body, name=name, grid=(s // tm,),
                  in_specs=[pl.BlockSpec((tm, d), lambda i: (i, 0)), HBM_SPEC],
                  out_specs=[pl.BlockSpec((2, 4, tm, FS), lambda i: (0, 0, i, 0)),
                             pl.BlockSpec((4, tm, FS), lambda i: (0, i, 0))],
                  out_shape=[_sds((2, 4, s, FS), BF16), _sds((4, s, FS), BF16)], scratch=_resident(w4), carry=carry)
    return _carried(call, (h, w4), carry)


def _in_proj(h, wa, wb, name, tm=512, carry=None):
    s, d = h.shape
    tm = min(tm, s)

    def body(h_ref, wa_hbm, wb_hbm, pa_ref, pb_ref, wa_ref, wa_sem, wb_ref, wb_sem):
        _load_once(wa_hbm, wa_ref, wa_sem)
        _load_once(wb_hbm, wb_ref, wb_sem)
        hv = h_ref[...]
        for w_ref, o_ref in ((wb_ref, pb_ref), (wa_ref, pa_ref)):
            for j in range(w_ref.shape[0]):
                o_ref[j] = _dot_nt(hv, w_ref[j]).astype(BF16)

    call = _pcall(body, name=name, grid=(s // tm,), carry=carry,
                  in_specs=[pl.BlockSpec((tm, d), lambda i: (i, 0)), HBM_SPEC, HBM_SPEC],
                  out_specs=[pl.BlockSpec((wa.shape[0], tm, d), lambda i: (0, i, 0)),
                             pl.BlockSpec((wb.shape[0], tm, d), lambda i: (0, i, 0))],
                  out_shape=[_sds((wa.shape[0], s, d), BF16), _sds((wb.shape[0], s, d), BF16)],
                  scratch=_resident(wa) + _resident(wb))
    return _carried(call, (h, wa, wb), carry)


def _mm_res_norm(a, w, xres, gain, scale, name, tm=512, carry=None):
    npart, s, kp = a.shape
    tm = min(tm, s)

    def body(a_ref, w_ref, x_ref, g_ref, xo_ref, h_ref):
        for rows in _row_chunks(tm):
            acc = _dot(a_ref[0, rows, :], w_ref[0])
            for p in range(1, npart):
                acc = acc + _dot(a_ref[p, rows, :], w_ref[p])
            xn = x_ref[rows, :] + scale * acc
            xo_ref[rows, :] = xn
            r = lax.rsqrt(jnp.mean(xn * xn, axis=-1, keepdims=True) + EPS)
            h_ref[rows, :] = (xn * r * g_ref[...]).astype(BF16)

    call = _pcall(body, name=name, grid=(s // tm,),
                  in_specs=[pl.BlockSpec((npart, tm, kp), lambda i: (0, i, 0)),
                            pl.BlockSpec((npart, kp, D), lambda i: (0, 0, 0)),
                            pl.BlockSpec((tm, D), lambda i: (i, 0)),
                            pl.BlockSpec((1, D), lambda i: (0, 0))],
                  out_specs=[pl.BlockSpec((tm, D), lambda i: (i, 0)), pl.BlockSpec((tm, D), lambda i: (i, 0))],
                  out_shape=[_sds((s, D), F32), _sds((s, D), BF16)], carry=carry)
    return _carried(call, (a, w, xres, gain), carry)


def _ffn_down_loss(a, w, xres, gain, target, name, tm=512):
    npart, s, kp = a.shape
    tm = min(tm, s)

    def body(a_ref, w_ref, x_ref, g_ref, t_ref, dx_ref, dxb_ref, loss_ref, dg_ref):
        @pl.when(pl.program_id(0) == 0)
        def _():
            loss_ref[...] = jnp.zeros_like(loss_ref)
            dg_ref[...] = jnp.zeros_like(dg_ref)

        for rows in _row_chunks(tm):
            acc = _dot(a_ref[0, rows, :], w_ref[0])
            for p in range(1, npart):
                acc = acc + _dot(a_ref[p, rows, :], w_ref[p])
            xn = x_ref[rows, :] + 0.5 * acc
            r = lax.rsqrt(jnp.mean(xn * xn, axis=-1, keepdims=True) + EPS)
            xh = xn * r
            gv = g_ref[...]
            err = xh * gv - t_ref[rows, :]
            part = 0.5 * jnp.sum(jnp.mean(err * err, axis=-1, keepdims=True), axis=0, keepdims=True)
            dy = err * (1.0 / D)
            dyg = dy * gv
            dxn = r * (dyg - xh * jnp.mean(dyg * xh, axis=-1, keepdims=True))
            dx_ref[rows, :] = dxn
            dxb_ref[rows, :] = dxn.astype(BF16)
            loss_ref[...] += jnp.broadcast_to(part, loss_ref.shape)
            dg_ref[...] += jnp.sum(dy * xh, axis=0, keepdims=True)

    return _pcall(body, name=name, grid=(s // tm,),
                  in_specs=[pl.BlockSpec((npart, tm, kp), lambda i: (0, i, 0)),
                            pl.BlockSpec((npart, kp, D), lambda i: (0, 0, 0)),
                            pl.BlockSpec((tm, D), lambda i: (i, 0)),
                            pl.BlockSpec((1, D), lambda i: (0, 0)),
                            pl.BlockSpec((tm, D), lambda i: (i, 0))],
                  out_specs=[pl.BlockSpec((tm, D), lambda i: (i, 0)), pl.BlockSpec((tm, D), lambda i: (i, 0)),
                             pl.BlockSpec((8, 128), lambda i: (0, 0)), pl.BlockSpec((1, D), lambda i: (0, 0))],
                  out_shape=[_sds((s, D), F32), _sds((s, D), BF16), _sds((8, 128), F32), _sds((1, D), F32)],
                  )(a, w, xres, gain, target)


def _window_tiles():
    i = lax.broadcasted_iota(jnp.int32, (BLK, BLK), 0)
    j = lax.broadcasted_iota(jnp.int32, (BLK, BLK), 1)
    rel = (i - j) & (BLK - 1)
    large = jnp.full_like(rel, REL_EXACT)
    for t in BUCKET_THRESHOLDS:
        large = large + (rel >= t).astype(jnp.int32)
    return j <= i, jnp.where(rel < REL_EXACT, rel, large)


def _bias_build(rel_bias, name):
    def body(rb_ref, o_ref):
        _, bucket = _window_tiles()

        def per_head(h, carry):
            acc = jnp.zeros((BLK, BLK), F32)
            for b in range(REL_BUCKETS):
                acc = jnp.where(bucket == b, rb_ref[b, h], acc)
            o_ref[h] = acc
            return carry

        lax.fori_loop(0, N_HEADS, per_head, 0)

    return _pcall(body, name=name, grid=(1,),
                  in_specs=[pl.BlockSpec(memory_space=pltpu.SMEM)],
                  out_specs=pl.BlockSpec((N_HEADS, BLK, BLK), lambda i: (0, 0, 0)),
                  out_shape=_sds((N_HEADS, BLK, BLK), F32))(rel_bias)


def _bias_bwd(dbias, name):
    def body(db_ref, o_ref):
        _, bucket = _window_tiles()
        lane = lax.broadcasted_iota(jnp.int32, (N_HEADS, 128), 1)

        def per_bucket(b, out):
            mb = (bucket == b).astype(F32)
            per_col = jnp.sum(db_ref[...] * mb[None, :, :], axis=1)
            return jnp.where(lane == b, jnp.sum(per_col, axis=1, keepdims=True), out)

        o_ref[...] = lax.fori_loop(0, REL_BUCKETS, per_bucket, jnp.zeros((N_HEADS, 128), F32))

    return _pcall(body, name=name, grid=(1,),
                  in_specs=[pl.BlockSpec((N_HEADS, BLK, BLK), lambda i: (0, 0, 0))],
                  out_specs=pl.BlockSpec((N_HEADS, 128), lambda i: (0, 0)),
                  out_shape=_sds((N_HEADS, 128), F32))(dbias)


PAIR = 2 * HEAD
GROUP = N_HEADS // N_KV
SWA_SCALE = HEAD ** -0.5


def _window_masks(n):
    i = lax.broadcasted_iota(jnp.int32, (GROUP * BLK, BLK), 0) & (BLK - 1)
    j = lax.broadcasted_iota(jnp.int32, (GROUP * BLK, BLK), 1)
    return j <= i, jnp.logical_and(n == 0, j > i), j < HEAD


def _kv_twice(ref, base, g, low):
    slab = ref[:, base + PAIR * (g // 2): base + PAIR * (g // 2 + 1)]
    swapped = pltpu.roll(slab, HEAD, 1)
    return jnp.where(low, slab, swapped) if g % 2 == 0 else jnp.where(low, swapped, slab)


def _stack_heads(ref, g, low):
    parts = []
    for r in range(2):
        slab = ref[:, PAIR * (2 * g + r): PAIR * (2 * g + r + 1)]
        zero = jnp.zeros_like(slab)
        parts += [jnp.where(low, slab, zero), jnp.where(low, zero, slab)]
    return jnp.concatenate(parts, axis=0)


def _unstack_heads(t, low):
    return [jnp.where(low, t[2 * r * BLK:(2 * r + 1) * BLK], t[(2 * r + 1) * BLK:(2 * r + 2) * BLK])
            for r in range(2)]


def _head_rows(t, k):
    return t[k * BLK:(k + 1) * BLK]


def _per_head_column(values):
    head = lax.broadcasted_iota(jnp.int32, (GROUP * BLK, 1), 0) // BLK
    col = jnp.full((GROUP * BLK, 1), values[0], F32)
    for k in range(1, GROUP):
        col = jnp.where(head == k, values[k], col)
    return col


def _window_logits(q4, kc, kp, bias4, own, absent):
    sc = jnp.where(own, _dot_nt(q4, kc), _dot_nt(q4, kp)) * SWA_SCALE + bias4
    return jnp.where(absent, NEG, sc)


def _split_window(t, own):
    zero = jnp.zeros_like(t)
    return jnp.where(own, t, zero), jnp.where(own, zero, t)


def _swa_fwd(pb, bias, sinks, name, carry=None):
    _, s, _ = pb.shape
    nb = s // BLK
    kvw = 2 * N_KV * HEAD

    def body(q_ref, kc_ref, kp_ref, b_ref, sk_ref, o_ref, lse_ref):
        own, absent, low4 = _window_masks(pl.program_id(0))
        low = low4[:BLK]
        lane = lax.broadcasted_iota(jnp.int32, (BLK, 128), 1)
        lse_t = jnp.zeros((BLK, 128), F32)
        for g in range(N_KV):
            q4 = _stack_heads(q_ref, g, low)
            kc, kp = _kv_twice(kc_ref, 0, g, low), _kv_twice(kp_ref, 0, g, low)
            vc, vp = _kv_twice(kc_ref, N_KV * HEAD, g, low), _kv_twice(kp_ref, N_KV * HEAD, g, low)
            bias4 = b_ref[GROUP * g:GROUP * (g + 1)].reshape(GROUP * BLK, BLK)
            sc = _window_logits(q4, kc, kp, bias4, own, absent)
            sk = _per_head_column([sk_ref[0, GROUP * g + k] for k in range(GROUP)])
            m = jnp.maximum(jnp.max(sc, axis=1, keepdims=True), sk)
            p = jnp.exp(sc - m)
            l = jnp.sum(p, axis=1, keepdims=True) + jnp.exp(sk - m)
            p_own, p_prev = _split_window(p.astype(BF16), own)
            out = (_dot(p_own, vc) + _dot(p_prev, vp)) * (1.0 / l)
            for r, slab in enumerate(_unstack_heads(out, low)):
                o_ref[:, PAIR * (2 * g + r): PAIR * (2 * g + r + 1)] = slab.astype(BF16)
            lse4 = m + jnp.log(l)
            for k in range(GROUP):
                lse_t = jnp.where(lane == GROUP * g + k, _head_rows(lse4, k), lse_t)
        lse_ref[...] = lse_t

    call = _pcall(body, name=name, grid=(nb,),
                  in_specs=[pl.BlockSpec((None, BLK, D), lambda n: (0, n, 0)),
                            pl.BlockSpec((None, BLK, kvw), lambda n: (1, n, 0)),
                            pl.BlockSpec((None, BLK, kvw), lambda n: (1, jnp.maximum(n - 1, 0), 0)),
                            pl.BlockSpec((N_HEADS, BLK, BLK), lambda n: (0, 0, 0)),
                            pl.BlockSpec(memory_space=pltpu.SMEM)],
                  out_specs=[pl.BlockSpec((BLK, D), lambda n: (n, 0)), pl.BlockSpec((BLK, 128), lambda n: (n, 0))],
                  out_shape=[_sds((s, D), BF16), _sds((s, 128), F32)], carry=carry)
    return _carried(call, (pb, pb, pb, bias, sinks), carry)


def _fold_halves(t, g, low):
    folded = jnp.where(low, t, 0.0) + pltpu.roll(jnp.where(low, 0.0, t), HEAD, 1)
    return folded if g % 2 == 0 else pltpu.roll(folded, HEAD, 1)


def _swa_bwd(pb, attn, dattn, lse, bias, sinks, name, carry=None):
    _, s, _ = pb.shape
    nb = s // BLK
    kvw = 2 * N_KV * HEAD
    voff = N_KV * HEAD

    def body(q_ref, kc_ref, kp_ref, o_ref, do_ref, lse_ref, b_ref, skrow_ref, dpb_ref, dbias_ref, dsk_ref,
             dq_hold, kv_hold, dq_new, kv_prev, kv_cur):
        n = pl.program_id(0)

        @pl.when(n == 0)
        def _():
            dbias_ref[...] = jnp.zeros_like(dbias_ref)
            dsk_ref[...] = jnp.zeros_like(dsk_ref)
            dq_hold[...] = jnp.zeros_like(dq_hold)
            kv_hold[...] = jnp.zeros_like(kv_hold)

        @pl.when(n < nb)
        def _():
            own, absent, low4 = _window_masks(n)
            low = low4[:BLK]
            lane = lax.broadcasted_iota(jnp.int32, (BLK, 128), 1)
            delta_t = jnp.zeros((BLK, 128), F32)
            ones = jnp.ones((PAIR, 128), BF16)
            for pair_of_kv in range(N_KV // 2):
                slab_grads = [jnp.zeros((BLK, PAIR), F32) for _ in range(4)]
                for g in (2 * pair_of_kv, 2 * pair_of_kv + 1):
                    q4, do4 = _stack_heads(q_ref, g, low), _stack_heads(do_ref, g, low)
                    kc, kp = _kv_twice(kc_ref, 0, g, low), _kv_twice(kp_ref, 0, g, low)
                    vc, vp = _kv_twice(kc_ref, voff, g, low), _kv_twice(kp_ref, voff, g, low)
                    o_slabs = [o_ref[:, PAIR * (2 * g + r): PAIR * (2 * g + r + 1)] for r in range(2)]
                    o4 = jnp.concatenate([o_slabs[0], o_slabs[0], o_slabs[1], o_slabs[1]], axis=0)
                    delta = _dot(do4 * o4, ones)
                    heads = range(GROUP * g, GROUP * (g + 1))
                    lse4 = jnp.concatenate([lse_ref[:, h:h + 1] for h in heads], axis=0)
                    bias4 = b_ref[GROUP * g:GROUP * (g + 1)].reshape(GROUP * BLK, BLK)
                    p = jnp.exp(_window_logits(q4, kc, kp, bias4, own, absent) - lse4)
                    dp = jnp.where(own, _dot_nt(do4, vc), _dot_nt(do4, vp))
                    ds = p * (dp - delta)
                    dbias_ref[GROUP * g:GROUP * (g + 1)] += ds.reshape(GROUP, BLK, BLK)
                    for k, h in enumerate(heads):
                        delta_t = jnp.where(lane == h, _head_rows(delta, k), delta_t)
                    ds_own, ds_prev = _split_window((ds * SWA_SCALE).astype(BF16), own)
                    p_own, p_prev = _split_window(p.astype(BF16), own)
                    dq4 = _dot(ds_own, kc) + _dot(ds_prev, kp)
                    for r, slab in enumerate(_unstack_heads(dq4, low)):
                        dq_new[:, PAIR * (2 * g + r): PAIR * (2 * g + r + 1)] = slab
                    grads = [_dot_tn(ds_own, q4), _dot_tn(ds_prev, q4), _dot_tn(p_own, do4), _dot_tn(p_prev, do4)]
                    slab_grads = [t + _fold_halves(dk, g, low) for t, dk in zip(slab_grads, grads)]
                ks = slice(PAIR * pair_of_kv, PAIR * (pair_of_kv + 1))
                vs = slice(voff + PAIR * pair_of_kv, voff + PAIR * (pair_of_kv + 1))
                kv_cur[:, ks], kv_prev[:, ks], kv_cur[:, vs], kv_prev[:, vs] = slab_grads
            dsk_ref[...] -= jnp.sum(jnp.exp(skrow_ref[...] - lse_ref[...]) * delta_t, axis=0, keepdims=True)

        @pl.when(n == nb)
        def _():
            kv_prev[...] = jnp.zeros_like(kv_prev)

        dpb_ref[0] = dq_hold[...].astype(BF16)
        dpb_ref[1, :, 0:kvw] = (kv_hold[...] + kv_prev[...]).astype(BF16)
        dpb_ref[1, :, kvw:D] = jnp.zeros((BLK, D - kvw), BF16)

        @pl.when(n < nb)
        def _():
            dq_hold[...] = dq_new[...]
            kv_hold[...] = kv_cur[...]

    def cur(n):
        return jnp.minimum(n, nb - 1)

    call = _pcall(body, name=name, grid=(nb + 1,), carry=carry,
                  in_specs=[pl.BlockSpec((None, BLK, D), lambda n: (0, cur(n), 0)),
                            pl.BlockSpec((None, BLK, kvw), lambda n: (1, cur(n), 0)),
                            pl.BlockSpec((None, BLK, kvw), lambda n: (1, jnp.maximum(cur(n) - 1, 0), 0)),
                            pl.BlockSpec((BLK, D), lambda n: (cur(n), 0)),
                            pl.BlockSpec((BLK, D), lambda n: (cur(n), 0)),
                            pl.BlockSpec((BLK, 128), lambda n: (cur(n), 0)),
                            pl.BlockSpec((N_HEADS, BLK, BLK), lambda n: (0, 0, 0)),
                            pl.BlockSpec((1, 128), lambda n: (0, 0))],
                  out_specs=[pl.BlockSpec((2, BLK, D), lambda n: (0, jnp.maximum(n - 1, 0), 0)),
                             pl.BlockSpec((N_HEADS, BLK, BLK), lambda n: (0, 0, 0)),
                             pl.BlockSpec((1, 128), lambda n: (0, 0))],
                  out_shape=[_sds((2, s, D), BF16), _sds((N_HEADS, BLK, BLK), F32), _sds((1, 128), F32)],
                  scratch=[pltpu.VMEM((BLK, D), F32), pltpu.VMEM((BLK, kvw), F32), pltpu.VMEM((BLK, D), F32),
                           pltpu.VMEM((BLK, kvw), F32), pltpu.VMEM((BLK, kvw), F32)])
    sink_row = jnp.pad(sinks, ((0, 0), (0, 128 - N_HEADS)))
    return _carried(call, (pb, pb, pb, attn, dattn, lse, bias, sink_row), carry)


HALO = 16
CW = D


def _conv_taps(cu, halo_cu, first_tile):
    row = lax.broadcasted_iota(jnp.int32, cu.shape, 0)
    halo_cu = jnp.where(first_tile, 0.0, halo_cu)
    c1 = jnp.where(row == 0, halo_cu[HALO - 1:HALO], pltpu.roll(cu, 1, 0))
    c2 = jnp.where(row == 0, halo_cu[HALO - 2:HALO - 1],
                   jnp.where(row == 1, halo_cu[HALO - 1:HALO], pltpu.roll(cu, 2, 0)))
    return c1, c2


def _conv_merge_fwd(pa, attn, convw, name, ts=256):
    _, s, _ = pa.shape
    ts = min(ts, s)
    hb = ts // HALO

    def body(pa_ref, hp_ref, at_ref, w_ref, o_ref):
        i = pl.program_id(1)
        cu = pa_ref[0].astype(F32) * pa_ref[2].astype(F32)
        c1, c2 = _conv_taps(cu, hp_ref[0].astype(F32) * hp_ref[2].astype(F32), i == 0)
        w = w_ref[...]
        c3 = w[0:1] * c2 + w[1:2] * c1 + w[2:3] * cu
        conv = pa_ref[1].astype(F32) * c3
        o_ref[...] = (jax.nn.sigmoid(pa_ref[3].astype(F32)) * at_ref[...].astype(F32)
                      + jax.nn.sigmoid(pa_ref[4].astype(F32)) * conv).astype(BF16)

    return _pcall(body, name=name, grid=(D // CW, s // ts),
                  in_specs=[pl.BlockSpec((5, ts, CW), lambda c, i: (0, i, c)),
                            pl.BlockSpec((5, HALO, CW), lambda c, i: (0, jnp.maximum(i * hb - 1, 0), c)),
                            pl.BlockSpec((ts, CW), lambda c, i: (i, c)),
                            pl.BlockSpec((8, CW), lambda c, i: (0, c))],
                  out_specs=pl.BlockSpec((ts, CW), lambda c, i: (i, c)),
                  out_shape=_sds((s, D), BF16))(pa, pa, attn, convw)


def _conv_merge_bwd(dmerged, pa, attn, convw, name, ts=256, carry=None):
    _, s, _ = pa.shape
    ts = min(ts, s)
    hb = ts // HALO
    last_hb = s // HALO - 1

    def body(dm_ref, pa_ref, at_ref, w_ref, hp_ref, hn_ref, dmn_ref, dat_ref, dpa_ref, dw_ref):
        i = pl.program_id(1)
        last = i == pl.num_programs(1) - 1
        dm = dm_ref[...].astype(F32)
        cp, bp, u = pa_ref[0].astype(F32), pa_ref[1].astype(F32), pa_ref[2].astype(F32)
        sa = jax.nn.sigmoid(pa_ref[3].astype(F32))
        sc = jax.nn.sigmoid(pa_ref[4].astype(F32))
        at = at_ref[...].astype(F32)
        cu = cp * u
        c1, c2 = _conv_taps(cu, hp_ref[0].astype(F32) * hp_ref[2].astype(F32), i == 0)
        w = w_ref[...]
        c3 = w[0:1] * c2 + w[1:2] * c1 + w[2:3] * cu
        dconv = dm * sc
        dc3 = dconv * bp
        nxt = dmn_ref[...].astype(F32) * jax.nn.sigmoid(hn_ref[4].astype(F32)) * hn_ref[1].astype(F32)
        nxt = jnp.where(last, 0.0, nxt)
        row = lax.broadcasted_iota(jnp.int32, dc3.shape, 0)
        d1 = jnp.where(row == ts - 1, nxt[0:1], pltpu.roll(dc3, ts - 1, 0))
        d2 = jnp.where(row == ts - 2, nxt[0:1], jnp.where(row == ts - 1, nxt[1:2], pltpu.roll(dc3, ts - 2, 0)))
        dcu = w[2:3] * dc3 + w[1:2] * d1 + w[0:1] * d2
        dat_ref[...] = (dm * sa).astype(BF16)
        dpa_ref[0] = (dcu * u).astype(BF16)
        dpa_ref[1] = (dconv * c3).astype(BF16)
        dpa_ref[2] = (dcu * cp).astype(BF16)
        dpa_ref[3] = (dm * at * sa * (1.0 - sa)).astype(BF16)
        dpa_ref[4] = (dm * bp * c3 * sc * (1.0 - sc)).astype(BF16)

        @pl.when(i == 0)
        def _():
            dw_ref[...] = jnp.zeros_like(dw_ref)

        dw_ref[0:1, :] += jnp.sum(dc3 * c2, axis=0, keepdims=True)
        dw_ref[1:2, :] += jnp.sum(dc3 * c1, axis=0, keepdims=True)
        dw_ref[2:3, :] += jnp.sum(dc3 * cu, axis=0, keepdims=True)

    call = _pcall(body, name=name, grid=(D // CW, s // ts), carry=carry,
                  in_specs=[pl.BlockSpec((ts, CW), lambda c, i: (i, c)),
                            pl.BlockSpec((5, ts, CW), lambda c, i: (0, i, c)),
                            pl.BlockSpec((ts, CW), lambda c, i: (i, c)),
                            pl.BlockSpec((8, CW), lambda c, i: (0, c)),
                            pl.BlockSpec((5, HALO, CW), lambda c, i: (0, jnp.maximum(i * hb - 1, 0), c)),
                            pl.BlockSpec((5, HALO, CW), lambda c, i: (0, jnp.minimum((i + 1) * hb, last_hb), c)),
                            pl.BlockSpec((HALO, CW), lambda c, i: (jnp.minimum((i + 1) * hb, last_hb), c))],
                  out_specs=[pl.BlockSpec((ts, CW), lambda c, i: (i, c)),
                             pl.BlockSpec((5, ts, CW), lambda c, i: (0, i, c)),
                             pl.BlockSpec((8, CW), lambda c, i: (0, c))],
                  out_shape=[_sds((s, D), BF16), _sds((5, s, D), BF16), _sds((8, D), F32)])
    return _carried(call, (dmerged, pa, attn, convw, pa, pa, dmerged), carry)


def _xattn_fwd(q, kv, name, tq=512):
    s, _ = q.shape
    nm = kv.shape[1]
    tq = min(tq, s)

    def body(q_ref, kv_ref, o_ref, lse_ref):
        lane = lax.broadcasted_iota(jnp.int32, (tq, 128), 1)
        lse_t = jnp.zeros((tq, 128), F32)
        for h in range(XH):
            hs = slice(XHD * h, XHD * (h + 1))
            sc = _dot_nt(q_ref[:, hs], kv_ref[h]) * (XHD ** -0.5)
            m = jnp.max(sc, axis=1, keepdims=True)
            p = jnp.exp(sc - m)
            l = jnp.sum(p, axis=1, keepdims=True)
            o_ref[:, hs] = (_dot(p.astype(BF16), kv_ref[XH + h]) * (1.0 / l)).astype(BF16)
            lse_t = jnp.where(lane == h, m + jnp.log(l), lse_t)
        lse_ref[...] = lse_t

    return _pcall(body, name=name, grid=(s // tq,),
                  in_specs=[pl.BlockSpec((tq, D), lambda i: (i, 0)), pl.BlockSpec((2 * XH, nm, XHD), lambda i: (0, 0, 0))],
                  out_specs=[pl.BlockSpec((tq, D), lambda i: (i, 0)), pl.BlockSpec((tq, 128), lambda i: (i, 0))],
                  out_shape=[_sds((s, D), BF16), _sds((s, 128), F32)])(q, kv)


def _xattn_bwd(q, kv, o, do, lse, name, tq=512, carry=None):
    s, _ = q.shape
    nm = kv.shape[1]
    tq = min(tq, s)

    def body(q_ref, kv_ref, o_ref, do_ref, lse_ref, dq_ref, dkv_ref):
        @pl.when(pl.program_id(0) == 0)
        def _():
            dkv_ref[...] = jnp.zeros_like(dkv_ref)

        for h in range(XH):
            hs = slice(XHD * h, XHD * (h + 1))
            qh, kh, vh, dob = q_ref[:, hs], kv_ref[h], kv_ref[XH + h], do_ref[:, hs]
            p = jnp.exp(_dot_nt(qh, kh) * (XHD ** -0.5) - lse_ref[:, h:h + 1])
            dp = _dot_nt(dob, vh)
            delta = jnp.sum(dob.astype(F32) * o_ref[:, hs].astype(F32), axis=1, keepdims=True)
            dsb = (p * (dp - delta) * (XHD ** -0.5)).astype(BF16)
            dq_ref[:, hs] = _dot(dsb, kh).astype(BF16)
            dkv_ref[h] += _dot_tn(dsb, qh)
            dkv_ref[XH + h] += _dot_tn(p.astype(BF16), dob)

    call = _pcall(body, name=name, grid=(s // tq,), carry=carry,
                  in_specs=[pl.BlockSpec((tq, D), lambda i: (i, 0)), pl.BlockSpec((2 * XH, nm, XHD), lambda i: (0, 0, 0)),
                            pl.BlockSpec((tq, D), lambda i: (i, 0)), pl.BlockSpec((tq, D), lambda i: (i, 0)),
                            pl.BlockSpec((tq, 128), lambda i: (i, 0))],
                  out_specs=[pl.BlockSpec((tq, D), lambda i: (i, 0)), pl.BlockSpec((2 * XH, nm, XHD), lambda i: (0, 0, 0))],
                  out_shape=[_sds((s, D), BF16), _sds((2 * XH, nm, XHD), F32)])
    return _carried(call, (q, kv, o, do, lse), carry)


def _ffn_down_bwd(dxb, wd4, gu4, name, tm=512, carry=None):
    s, _ = dxb.shape
    tm = min(tm, s)

    def body(dx_ref, w_hbm, gu_ref, o_ref, w_ref, w_sem):
        _load_once(w_hbm, w_ref, w_sem)
        for p in range(4):
            for rows in _row_chunks(tm):
                da = _dot_nt(dx_ref[rows, :], w_ref[p])
                g = gu_ref[0, p, rows, :].astype(F32)
                u = gu_ref[1, p, rows, :].astype(F32)
                sg = jax.nn.sigmoid(g)
                t = da * sg
                o_ref[0, p, rows, :] = (t * u * (1.0 + g - g * sg)).astype(BF16)
                o_ref[1, p, rows, :] = (t * g).astype(BF16)

    block = pl.BlockSpec((2, 4, tm, FS), lambda i: (0, 0, i, 0))
    call = _pcall(body, name=name, grid=(s // tm,), carry=carry,
                  in_specs=[pl.BlockSpec((tm, D), lambda i: (i, 0)), HBM_SPEC, block],
                  out_specs=block, out_shape=_sds((2, 4, s, FS), BF16), scratch=_resident(wd4))
    return _carried(call, (dxb, wd4, gu4), carry)


def _mm_tn(a, b, name, scale=1.0, carry=None):
    pa_n, s, m = a.shape
    pb_n, _, n = b.shape
    po = max(pa_n, pb_n)
    tn = n if po >= 4 else min(n, 256)

    def body(a_ref, b_ref, o_ref):
        o_ref[...] = (scale * _dot_tn(a_ref[...], b_ref[...])).astype(BF16)

    call = _pcall(body, name=name, grid=(po, n // tn), carry=carry,
                  in_specs=[pl.BlockSpec((None, s, m), lambda o, j: (o if pa_n > 1 else 0, 0, 0)),
                            pl.BlockSpec((None, s, tn), lambda o, j: (o if pb_n > 1 else 0, 0, j))],
                  out_specs=pl.BlockSpec((None, m, tn), lambda o, j: (o, 0, j)),
                  out_shape=_sds((po, m, n), BF16))
    return _carried(call, (a, b), carry)


def _sum_dots(a_ref, b_ref, nj, bt, rows=slice(None)):
    dot = _dot_nt if bt else _dot
    acc = dot(a_ref[0, rows, :], b_ref[0])
    for j in range(1, nj):
        acc = acc + dot(a_ref[j, rows, :], b_ref[j])
    return acc


def _mm_acc(a, b, name, out_dtype, tm=512, bt=False, carry=None):
    nj, s, k = a.shape
    n = b.shape[1] if bt else b.shape[2]
    tm = min(tm, s)

    def body(a_ref, b_ref, o_ref):
        o_ref[...] = _sum_dots(a_ref, b_ref, nj, bt).astype(out_dtype)

    call = _pcall(body, name=name, grid=(s // tm,), carry=carry,
                  in_specs=[pl.BlockSpec((nj, tm, k), lambda i: (0, i, 0)),
                            pl.BlockSpec(b.shape, lambda i: (0, 0, 0))],
                  out_specs=pl.BlockSpec((tm, n), lambda i: (i, 0)), out_shape=_sds((s, n), out_dtype))
    return _carried(call, (a, b), carry)


def _mm_acc_rms_bwd(pairs, name, *, x, gain, dres, scale=None, tm=512, bt=False, carry=None):
    npairs = len(pairs)
    s = pairs[0][0].shape[1]
    n = pairs[0][1].shape[1] if bt else pairs[0][1].shape[2]
    tm = min(tm, s)

    def body(*refs):
        a_refs, b_hbms = refs[:npairs], refs[npairs:2 * npairs]
        x_ref, g_ref, r_ref, dx_ref, dxb_ref, dg_ref = refs[2 * npairs:2 * npairs + 6]
        resident = refs[2 * npairs + 6:]
        b_refs = resident[0::2]
        for b_hbm, b_ref, b_sem in zip(b_hbms, b_refs, resident[1::2]):
            _load_once(b_hbm, b_ref, b_sem)

        @pl.when(pl.program_id(0) == 0)
        def _():
            dg_ref[...] = jnp.zeros_like(dg_ref)

        for rows in _row_chunks(tm):
            dh = _sum_dots(a_refs[0], b_refs[0], pairs[0][0].shape[0], bt, rows)
            for a_ref, b_ref, (a, _) in zip(a_refs[1:], b_refs[1:], pairs[1:]):
                dh = dh + _sum_dots(a_ref, b_ref, a.shape[0], bt, rows)
            if scale is not None:
                dh = scale * dh
            xv = x_ref[rows, :]
            r = lax.rsqrt(jnp.mean(xv * xv, axis=-1, keepdims=True) + EPS)
            xh = xv * r
            dyg = dh * g_ref[...]
            dx = r_ref[rows, :] + r * (dyg - xh * jnp.mean(dyg * xh, axis=-1, keepdims=True))
            dx_ref[rows, :] = dx
            dxb_ref[rows, :] = dx.astype(BF16)
            dg_ref[...] += jnp.sum(dh * xh, axis=0, keepdims=True)

    row = pl.BlockSpec((tm, n), lambda i: (i, 0))
    in_specs = ([pl.BlockSpec((a.shape[0], tm, a.shape[2]), lambda i: (0, i, 0)) for a, _ in pairs]
                + [HBM_SPEC] * npairs + [row, pl.BlockSpec((1, n), lambda i: (0, 0)), row])
    args = tuple(a for a, _ in pairs) + tuple(b for _, b in pairs) + (x, gain, dres)
    call = _pcall(body, name=name, grid=(s // tm,), in_specs=in_specs, carry=carry,
                  out_specs=[row, row, pl.BlockSpec((1, n), lambda i: (0, 0))],
                  out_shape=[_sds((s, n), F32), _sds((s, n), BF16), _sds((1, n), F32)],
                  scratch=[t for _, b in pairs for t in _resident(b)])
    return _carried(call, args, carry)


def _adam(w, g, m, v):
    m2 = ADAM_B1 * m + (1.0 - ADAM_B1) * g
    v2 = ADAM_B2 * v + (1.0 - ADAM_B2) * (g * g)
    m_hat = m2 / (1.0 - ADAM_B1 ** ADAM_STEP)
    v_hat = v2 / (1.0 - ADAM_B2 ** ADAM_STEP)
    delta = -ADAM_LR * (m_hat / (jnp.sqrt(v_hat) + ADAM_EPS) + ADAM_WD * w)
    return delta, m2, v2


def _adamw(parts, w, m, v, name):
    _, r, c = parts.shape
    tr = max(t for t in range(16, 257, 16) if r % t == 0)

    def body(p_ref, w_ref, m_ref, v_ref, g_ref, d_ref, m2_ref, v2_ref):
        g = p_ref[0].astype(F32)
        for i in range(1, N_DEV):
            g = g + p_ref[i].astype(F32)
        delta, m2, v2 = _adam(w_ref[...], g, m_ref[...], v_ref[...])
        g_ref[...] = g
        d_ref[...] = delta
        m2_ref[...] = m2
        v2_ref[...] = v2

    blk = pl.BlockSpec((tr, c), lambda i: (i, 0))
    return _pcall(body, name=name, grid=(r // tr,),
                  in_specs=[pl.BlockSpec((N_DEV, tr, c), lambda i: (0, i, 0)), blk, blk, blk],
                  out_specs=[blk] * 4, out_shape=[_sds((r, c), F32)] * 4)(parts, w, m, v)


def _position():
    return lax.axis_index("x"), lax.axis_index("y"), lax.axis_index("c")


def _slot(px, py, pc):
    return 4 * px + 2 * py + pc


def _row_window(ref, rows):
    r0, r1 = rows
    return ref if (r0, r1) == (0, ref.shape[0]) else ref.at[pl.ds(r0, r1 - r0)]


def _split_items(items):
    sources = [src for src, _, _ in items]
    begun = [(a, dest) for a, (_, _, dest) in enumerate(items) if dest is not None]
    aliases = {len(sources) + k: a for k, (a, _) in enumerate(begun)}
    return sources + [dest for _, dest in begun], [rows for _, rows, _ in items], aliases


def _gather_carry(items):
    na = len(items)
    carry_ins, windows, aliases = _split_items(items)

    def plan(ins, outs, sems):
        send_sems, recv_sems, local_sems = sems
        x, y, c = _position()
        me, sibling = (x, y, c), (x, y, 1 - c)
        chips = [(1 - x, y), (x, 1 - y), (1 - x, 1 - y)]
        ins = [_row_window(ins[a], windows[a]) for a in range(na)]

        def block_rows(a, block):
            return _row_window(outs[a].at[_slot(*block)], windows[a])

        def copy(a, k, block, to, src=None):
            rows = block_rows(a, block)
            return pltpu.make_async_remote_copy(src_ref=rows if src is None else src, dst_ref=rows,
                                                send_sem=send_sems.at[k, a], recv_sem=recv_sems.at[k, a],
                                                device_id=to, device_id_type=MESH)

        mine = [pltpu.make_async_copy(ins[a], block_rows(a, me), local_sems.at[a]) for a in range(na)]
        first = [copy(a, 0, me, sibling, src=ins[a]) for a in range(na)]
        for j, chip in enumerate(chips):
            first += [copy(a, 1 + j, me, (*chip, c), src=ins[a]) for a in range(na)]
        landed = [[copy(a, 1 + j, (*chip, c), me) for a in range(na)] for j, chip in enumerate(chips)]
        passed = [[copy(a, 4 + j, (*chip, c), sibling) for a in range(na)] for j, chip in enumerate(chips)]
        from_sibling = [copy(a, 0, sibling, me) for a in range(na)]
        for j, chip in enumerate(chips):
            from_sibling += [copy(a, 4 + j, (*chip, 1 - c), me) for a in range(na)]
        return mine, first, landed, passed, from_sibling

    def start(ins, outs, sems):
        mine, first, _, _, _ = plan(ins, outs, sems)
        for cp in mine + first:
            cp.start()

    def mid(ins, outs, sems):
        _, _, landed, passed, _ = plan(ins, outs, sems)
        for over_ici, onward in zip(landed, passed):
            for cp, fwd in zip(over_ici, onward):
                cp.wait_recv()
                fwd.start()

    def finish(ins, outs, sems):
        mine, first, _, passed, from_sibling = plan(ins, outs, sems)
        for cp in from_sibling:
            cp.wait_recv()
        for cp in first + [fwd for onward in passed for fwd in onward]:
            cp.wait_send()
        for cp in mine:
            cp.wait()

    return _Carry(carry_ins, [_sds((N_DEV,) + src.shape, src.dtype) for src, _, _ in items],
                  [pltpu.SemaphoreType.DMA((7, na)), pltpu.SemaphoreType.DMA((7, na)),
                   pltpu.SemaphoreType.DMA((na,))], start, finish, mid, aliases)


def _exchange_carry(scattered, replicated=()):
    items = list(scattered) + [(a, (0, a.shape[0]), None) for a in replicated]
    na, ns = len(items), len(scattered)
    carry_ins, windows, aliases = _split_items(items)

    def plan(ins, outs, sems):
        send_sems, recv_sems, local_sems = sems
        me = _slot(*_position())

        def source(a, j):
            return _row_window(ins[a].at[j] if a < ns else ins[a], windows[a])

        def copy(a, j, i):
            return pltpu.make_async_remote_copy(src_ref=source(a, j), dst_ref=_row_window(outs[a].at[i], windows[a]),
                                                send_sem=send_sems.at[j, a], recv_sem=recv_sems.at[i, a],
                                                device_id=(j >> 2, (j >> 1) & 1, j & 1), device_id_type=MESH)

        def own(a, j):
            return pltpu.make_async_copy(source(a, j), _row_window(outs[a].at[j], windows[a]), local_sems.at[a])

        return me, copy, own

    def start(ins, outs, sems):
        me, copy, own = plan(ins, outs, sems)
        for a in range(na):
            for j in range(N_DEV):
                @pl.when(me == j)
                def _():
                    own(a, j).start()

                @pl.when(me != j)
                def _():
                    copy(a, j, me).start()

    def finish(ins, outs, sems):
        me, copy, own = plan(ins, outs, sems)
        for a in range(na):
            for j in range(N_DEV):
                @pl.when(me == j)
                def _():
                    for i in range(N_DEV):
                        if i != j:
                            copy(a, j, i).wait_recv()
                    own(a, j).wait()

                @pl.when(me != j)
                def _():
                    copy(a, j, me).wait_send()

    return _Carry(carry_ins, [_sds((N_DEV,) + src.shape[-2:], src.dtype) for src, _, _ in items],
                  [pltpu.SemaphoreType.DMA((N_DEV, na)), pltpu.SemaphoreType.DMA((N_DEV, na)),
                   pltpu.SemaphoreType.DMA((na,))], start, finish, None, aliases)


HBM_ARRAY = pl.BlockSpec(memory_space=pltpu.HBM)
SEMAPHORES = pl.BlockSpec(memory_space=pltpu.SEMAPHORE)
DATAFLOW = pltpu.SideEffectType.DATAFLOW_SIDE_EFFECTING


def _exchange_copy(parts_ref, land_ref, send_sems, recv_sems, me, j):
    return pltpu.make_async_remote_copy(src_ref=parts_ref.at[j], dst_ref=land_ref.at[me], send_sem=send_sems.at[j],
                                        recv_sem=recv_sems.at[me], device_id=(j >> 2, (j >> 1) & 1, j & 1),
                                        device_id_type=MESH)


def _exchange_start(parts, name):
    def body(parts_ref, land_ref, send_sems, recv_sems, parts_thru, land_thru, token):
        me = _slot(*_position())
        for j in range(N_DEV):
            @pl.when(me == j)
            def _():
                pltpu.make_async_copy(parts_ref.at[j], land_ref.at[j], send_sems.at[j]).start()

            @pl.when(me != j)
            def _():
                _exchange_copy(parts_ref, land_ref, send_sems, recv_sems, me, j).start()
        token[...] = jnp.zeros_like(token)

    return pl.pallas_call(
        body, name=name,
        out_shape=(pltpu.SemaphoreType.DMA((N_DEV,)), pltpu.SemaphoreType.DMA((N_DEV,)),
                   pltpu.HBM(parts.shape, parts.dtype), pltpu.HBM(parts.shape, parts.dtype), _sds((8, 128), F32)),
        in_specs=(HBM_ARRAY, HBM_ARRAY),
        out_specs=(SEMAPHORES, SEMAPHORES, HBM_ARRAY, HBM_ARRAY, pl.BlockSpec(memory_space=pltpu.VMEM)),
        input_output_aliases={0: 2, 1: 3}, compiler_params=pltpu.CompilerParams(has_side_effects=DATAFLOW),
    )(pltpu.with_memory_space_constraint(parts, pltpu.HBM),
      pltpu.with_memory_space_constraint(lax.empty(parts.shape, parts.dtype), pltpu.HBM))


def _exchange_wait(send_sems, recv_sems, parts_thru, land_thru, after, name):
    def body(parts_ref, land_ref, send_sems, recv_sems, after_ref, parts_dead, got_ref):
        me = _slot(*_position())
        for j in range(N_DEV):
            @pl.when(me == j)
            def _():
                pltpu.make_async_copy(parts_ref.at[j], land_ref.at[j], send_sems.at[j]).wait()

            @pl.when(me != j)
            def _():
                both = pltpu.make_async_remote_copy(src_ref=parts_ref.at[j], dst_ref=land_ref.at[j],
                                                    send_sem=send_sems.at[j], recv_sem=recv_sems.at[j],
                                                    device_id=(j >> 2, (j >> 1) & 1, j & 1), device_id_type=MESH)
                both.wait_send()
                both.wait_recv()

    return pl.pallas_call(
        body, name=name, out_shape=(pltpu.HBM(parts_thru.shape, parts_thru.dtype),
                                    pltpu.HBM(parts_thru.shape, parts_thru.dtype)),
        in_specs=(HBM_ARRAY, HBM_ARRAY, SEMAPHORES, SEMAPHORES, pl.BlockSpec(memory_space=pl.ANY)),
        out_specs=(HBM_ARRAY, HBM_ARRAY), input_output_aliases={0: 0, 1: 1},
        compiler_params=pltpu.CompilerParams(has_side_effects=DATAFLOW),
    )(parts_thru, land_thru, send_sems, recv_sems, after)[1]


NQ, NKV = N_HEADS * HEAD, 2 * N_KV * HEAD


class _Mesh:
    def __init__(self, shards):
        self.shards, self.full, self.received, self.cache = shards, {}, {}, {}

    def fetch(self, wanted):
        items = []
        for want in wanted:
            name, r0, r1 = want if isinstance(want, tuple) else (want, 0, self.shards[want].shape[0])
            items.append((self.shards[name], (r0, r1), self.full.get(name)))
        return _gather_carry(items)

    def fetched(self, wanted, results):
        self.full.update(zip([want[0] if isinstance(want, tuple) else want for want in wanted], results))

    def send(self, *payloads):
        return _exchange_carry([(parts, rows or (0, parts.shape[1]), self.received.get(name))
                                for name, parts, rows in payloads])

    def sent(self, names, results):
        self.received.update(zip(names, results))

    def send_apart(self, name, parts):
        *self.pending, token = _exchange_start(parts, "exchange_" + name + "_start")
        self.pending_name = name
        return token

    def sent_apart(self, after):
        self.received[self.pending_name] = _exchange_wait(*self.pending, after, "exchange_" + self.pending_name + "_wait")

    def w(self, key):
        if key not in self.cache:
            self.cache[key] = self._layout(key)
        return self.cache[key]

    def _layout(self, key):
        if key in ("gu1", "gu2"):
            return self.full[key]
        if key in ("d1", "d2"):
            return self.full[key].reshape(4, FS, D)
        if key in ("out", "q", "o"):
            return self.full[key].reshape(D, D)
        if key == "kv":
            return self.full["kv"]
        if key == "convw":
            rows = self.full["conv"][:, :3, :].transpose(1, 0, 2).reshape(3, D)
            return jnp.concatenate([rows, jnp.zeros((5, D), F32)], axis=0)
        w_in_t = self.full["win"].reshape(-1, D)
        if key == "wa":
            return w_in_t[NQ + NKV:].reshape(5, D, D)
        assert key == "wb", key
        return jnp.stack([w_in_t[:NQ], jnp.pad(w_in_t[NQ:NQ + NKV], ((0, D - NKV), (0, 0)))])


def _w_in_parts(dw_a, dw_b):
    return jnp.concatenate([dw_b[0], dw_b[1][:NKV], dw_a.reshape(5 * D, D)], axis=0).reshape(N_DEV, -1, D)


def _forward_backward(x, mem, target, g, rel_bias, sinks, ex):
    s = x.shape[0]
    def fetching(wanted, call, *args, **kw):
        res, got = call(*args, carry=ex.fetch(wanted), **kw)
        ex.fetched(wanted, got)
        return res

    h1 = fetching(["gu1", "conv"], _rmsnorm, x, g["ffn1"], "norm_ffn1")
    gu1, a1 = fetching(["d1", ("win", 0, 400)], _ffn_up, h1, ex.w("gu1").reshape(2, 4, FS, D), "ffn1_up")
    x1, h2 = fetching([("win", 400, 832)], _mm_res_norm, a1, ex.w("d1"), x, g["mix"], 0.5, "ffn1_down")
    pa, pb = fetching(["gu2", "out", "q"], _in_proj, h2, ex.w("wa"), ex.w("wb"), "in_proj")
    biasm = _bias_build(rel_bias, "bias_build")
    attn, lse = fetching(["kv", "d2", "o"], _swa_fwd, pb, biasm, sinks, "swa_fwd")
    merged = _conv_merge_fwd(pa, attn, ex.w("convw"), "conv_merge_fwd")
    (x2, h3), _ = _mm_res_norm(merged[None], ex.w("out")[None], x1, g["xattn"], 1.0, "out_proj")
    q2 = _mm_nn(h3, ex.w("q")[None], "xattn_q")[0][0]
    mh, _ = _rmsnorm(mem, g["mem"], "norm_mem")
    kv2 = _mm_nn(mh, ex.w("kv"), "xattn_kv")[0]
    o, lse2 = _xattn_fwd(q2, kv2, "xattn_fwd")
    (x3, h4), _ = _mm_res_norm(o[None], ex.w("o")[None], x2, g["ffn2"], 1.0, "xattn_o")
    (gu2, a2), _ = _ffn_up(h4, ex.w("gu2").reshape(2, 4, FS, D), "ffn2_up")
    dx4, dx4b, loss, d_final = _ffn_down_loss(a2, ex.w("d2"), x3, g["final"], target, "ffn2_down_loss")
    def sending(payloads, call, *args, **kw):
        res, got = call(*args, carry=ex.send(*payloads), **kw)
        ex.sent([name for name, _, _ in payloads], got)
        return res

    dw_d2 = _mm_tn(a2, dx4b[None], "dw_ffn2_down", scale=0.5)[0].reshape(N_DEV, -1, D)
    dgu2 = sending([("d2", dw_d2, None)], _ffn_down_bwd, dx4b, ex.w("d2"), gu2, "ffn2_down_bwd").reshape(8, s, FS)
    dw_gu2 = _mm_tn(dgu2, h4[None], "dw_ffn2_up", scale=0.5)[0]
    dx3, dx3b, d_ffn2 = sending([("gu2", dw_gu2, (0, 400))], _mm_acc_rms_bwd, [(dgu2, ex.w("gu2"))], "ffn2_up_bwd",
                                x=x3, gain=g["ffn2"], dres=dx4, scale=0.5)
    do, _ = _mm_acc(dx3b[None], ex.w("o")[None], "xattn_o_bwd", BF16, bt=True)
    dw_o = _mm_tn(o[None], dx3b[None], "dw_xattn_o")[0].reshape(N_DEV, -1, D)
    dq2, dkv2 = sending([("o", dw_o, None)], _xattn_bwd, q2, kv2, o, do, lse2, "xattn_bwd")
    dkv2b = dkv2.astype(BF16)
    dw_q = _mm_tn(h3[None], dq2[None], "dw_xattn_q")[0].reshape(N_DEV, -1, D)
    dx2, dx2b, d_xattn = sending([("q", dw_q, None)], _mm_acc_rms_bwd, [(dq2[None], ex.w("q")[None])],
                                 "xattn_q_bwd", x=x2, gain=g["xattn"], dres=dx3, bt=True)
    dw_kv = _mm_tn(mh[None], dkv2b, "dw_xattn_kv")[0]
    (_, _, d_mem), _ = _mm_acc_rms_bwd([(dkv2b, ex.w("kv"))], "xattn_kv_bwd", x=mem, gain=g["mem"],
                                       dres=jnp.zeros_like(mem), bt=True)
    dmerged, _ = _mm_acc(dx2b[None], ex.w("out")[None], "out_proj_bwd", BF16, bt=True)
    dw_out = _mm_tn(merged[None], dx2b[None], "dw_out_proj")[0].reshape(N_DEV, -1, D)
    dattn, dpa, d_convw = sending([("kv", dw_kv, None)], _conv_merge_bwd,
                                  dmerged, pa, attn, ex.w("convw"), "conv_merge_bwd")
    dpb, dbias, d_sinks = sending([("gu2", dw_gu2, (400, FS)), ("out", dw_out, None)], _swa_bwd,
                                  pb, attn, dattn, lse, biasm, sinks, "swa_bwd")
    d_relb = _bias_bwd(dbias, "bias_bwd")
    dw_in = _w_in_parts(_mm_tn(dpa, h2[None], "dw_in_proj_a")[0], _mm_tn(dpb, h2[None], "dw_in_proj_b")[0])
    dx1, dx1b, d_mix = sending([("win", dw_in, (0, 672))], _mm_acc_rms_bwd,
                               [(dpa, ex.w("wa")), (dpb, ex.w("wb"))], "in_proj_bwd",
                               x=x1, gain=g["mix"], dres=dx2)
    dw_d1 = sending([("win", dw_in, (672, 832))], _mm_tn, a1, dx1b[None], "dw_ffn1_down", scale=0.5)
    dw_d1 = dw_d1.reshape(N_DEV, -1, D)
    dgu1 = sending([("d1", dw_d1, None)], _ffn_down_bwd, dx1b, ex.w("d1"), gu1, "ffn1_down_bwd").reshape(8, s, FS)
    dw_gu1 = _mm_tn(dgu1, h1[None], "dw_ffn1_up", scale=0.5)[0]
    token = ex.send_apart("gu1", dw_gu1)
    (dx0, _, d_ffn1), _ = _mm_acc_rms_bwd([(dgu1, ex.w("gu1"))], "ffn1_up_bwd", x=x,
                                          gain=g["ffn1"] + token[0:1, 0:1], dres=dx1, scale=0.5)

    relb_row = jnp.concatenate([d_relb[:, :REL_BUCKETS].T.reshape(1, REL_BUCKETS * N_HEADS), d_sinks[:, :N_HEADS],
                                jnp.zeros((1, D - REL_BUCKETS * N_HEADS - N_HEADS), F32)], axis=1)
    loss_row = jnp.concatenate([loss[0:1, 0:1], jnp.zeros((1, D - 1), F32)], axis=1)
    small = jnp.concatenate([d_ffn1, d_mix, d_xattn, d_mem, d_ffn2, d_final, relb_row, loss_row, d_convw[0:3],
                             jnp.zeros((SMALL_ROWS - ROW_CONV - 3, D), F32)], axis=0)
    return dx0, small


def _pack_small(norms, final, relb, sinks, conv_local, me):
    relb_row = jnp.concatenate([relb.reshape(1, -1), sinks.reshape(1, -1),
                                jnp.zeros((1, D - REL_BUCKETS * N_HEADS - N_HEADS), F32)], axis=1)
    conv_rows = lax.dynamic_update_slice(jnp.zeros((3, D), F32), conv_local.reshape(3, -1), (0, 128 * me))
    return jnp.concatenate(list(norms) + [final.reshape(1, D), relb_row, jnp.zeros((1, D), F32), conv_rows,
                                          jnp.zeros((SMALL_ROWS - ROW_CONV - 3, D), F32)], axis=0)


def kernel(x, mem, positions, rel_bias, ffn1_norm, ffn1_w_gu, ffn1_w_down, mix_norm, w_in, sinks, conv_w, w_out, xattn_norm, mem_norm, xattn_wq, xattn_wkv, xattn_wo, ffn2_norm, ffn2_w_gu, ffn2_w_down, final_norm, loss_target, m_rel_bias, m_ffn1_norm, m_ffn1_w_gu, m_ffn1_w_down, m_mix_norm, m_w_in, m_sinks, m_conv_w, m_w_out, m_xattn_norm, m_mem_norm, m_xattn_wq, m_xattn_wkv, m_xattn_wo, m_ffn2_norm, m_ffn2_w_gu, m_ffn2_w_down, m_final_norm, v_rel_bias, v_ffn1_norm, v_ffn1_w_gu, v_ffn1_w_down, v_mix_norm, v_w_in, v_sinks, v_conv_w, v_w_out, v_xattn_norm, v_mem_norm, v_xattn_wq, v_xattn_wkv, v_xattn_wo, v_ffn2_norm, v_ffn2_w_gu, v_ffn2_w_down, v_final_norm):
    del positions
    me = _slot(*_position())
    big = dict(gu1=(ffn1_w_gu, m_ffn1_w_gu, v_ffn1_w_gu), d1=(ffn1_w_down, m_ffn1_w_down, v_ffn1_w_down),
               win=(w_in, m_w_in, v_w_in), out=(w_out, m_w_out, v_w_out), q=(xattn_wq, m_xattn_wq, v_xattn_wq),
               kv=(xattn_wkv, m_xattn_wkv, v_xattn_wkv), o=(xattn_wo, m_xattn_wo, v_xattn_wo),
               gu2=(ffn2_w_gu, m_ffn2_w_gu, v_ffn2_w_gu), d2=(ffn2_w_down, m_ffn2_w_down, v_ffn2_w_down))
    order = list(big)
    transposed = ("gu1", "gu2", "win")
    local = {k: tuple(t[0].T if k in transposed else t[0] for t in big[k]) for k in order}
    shards = {k: local[k][0].astype(BF16) for k in order}
    shards["conv"] = jnp.concatenate([conv_w[0], jnp.zeros((5, 128), F32)], axis=0)
    ex = _Mesh(shards)
    gains = dict(ffn1=ffn1_norm, mix=mix_norm, xattn=xattn_norm, mem=mem_norm, ffn2=ffn2_norm,
                 final=final_norm.reshape(1, D))
    dx, small = _forward_backward(x[0], mem[0], loss_target[0], gains, rel_bias, sinks, ex)
    big_out = {k: _adamw(ex.received[k], *local[k], "adamw_" + k) for k in order if k != "gu1"}
    small, _ = lax.optimization_barrier((small, [big_out[k][1] for k in big_out]))
    small_parts = _run_alone(_exchange_carry([], [small]), "exchange_small")[0]
    packed = [_pack_small(norms, final, relb, sk, conv, me) for norms, final, relb, sk, conv in (
        ((ffn1_norm, mix_norm, xattn_norm, mem_norm, ffn2_norm), final_norm, rel_bias, sinks, conv_w),
        ((m_ffn1_norm, m_mix_norm, m_xattn_norm, m_mem_norm, m_ffn2_norm), m_final_norm, m_rel_bias, m_sinks, m_conv_w),
        ((v_ffn1_norm, v_mix_norm, v_xattn_norm, v_mem_norm, v_ffn2_norm), v_final_norm, v_rel_bias, v_sinks, v_conv_w))]
    small_out = _adamw(small_parts, *packed, "adamw_small")
    done = [dx[0:1, 0:1], small_out[1][0:1, 0:1]] + [big_out[k][1][0:1, 0:1] for k in big_out]
    ex.sent_apart(after=sum(done))
    big_out["gu1"] = _adamw(ex.received["gu1"], *local["gu1"], "adamw_gu1")
    big_out = {k: [t.T if k in transposed else t for t in big_out[k]] for k in order}

    def unpack(t):
        conv = lax.dynamic_slice(t[ROW_CONV:ROW_CONV + 3], (0, 128 * me), (3, 128))[None]
        nrel = REL_BUCKETS * N_HEADS
        return dict(ffn1_norm=t[0:1], mix_norm=t[1:2], xattn_norm=t[2:3], mem_norm=t[3:4], ffn2_norm=t[4:5],
                    final_norm=t[5], rel_bias=t[ROW_RELB, :nrel].reshape(REL_BUCKETS, N_HEADS),
                    sinks=t[ROW_RELB:ROW_RELB + 1, nrel:nrel + N_HEADS], conv_w=conv)

    names = dict(gu1="ffn1_w_gu", d1="ffn1_w_down", win="w_in", out="w_out", q="xattn_wq", kv="xattn_wkv",
                 o="xattn_wo", gu2="ffn2_w_gu", d2="ffn2_w_down")
    results = []
    for idx in range(4):
        leaves = unpack(small_out[idx])
        leaves.update({names[k]: big_out[k][idx][None] for k in order})
        results.append(leaves)
    weights = ("rel_bias", "ffn1_norm", "ffn1_w_gu", "ffn1_w_down", "mix_norm", "w_in", "sinks", "conv_w", "w_out",
               "xattn_norm", "mem_norm", "xattn_wq", "xattn_wkv", "xattn_wo", "ffn2_norm", "ffn2_w_gu", "ffn2_w_down",
               "final_norm")
    loss = small_out[0][ROW_LOSS, 0]
    return (loss, dx[None], *[leaves[n] for leaves in results for n in weights])
```

```python
import math

import numpy as np
import jax
import jax.numpy as jnp
from jax import lax
from jax.experimental import pallas as pl
from jax.experimental.pallas import tpu as pltpu

F32, BF16 = jnp.float32, jnp.bfloat16
MESH = pl.DeviceIdType.MESH

D = 1024
N_DEV = 8
D_FF = 2816
FS = D_FF // 4
HEAD = 64
N_HEADS, N_KV = 16, 4
BLK = 128
NQ, NKV = N_HEADS * HEAD, 2 * N_KV * HEAD
XH, XHD = 4, 256
REL_BUCKETS, REL_EXACT, REL_MAX_DIST = 32, 16, 128
EPS, NEG = 1e-6, -1e30
ADAM_LR, ADAM_B1, ADAM_B2, ADAM_EPS, ADAM_WD, ADAM_STEP = 0.001, 0.9, 0.999, 1e-08, 0.01, 10
VMEM_LIMIT_V7X = 56 * 2**20
SMALL_ROWS = 16
ROW_RELB, ROW_LOSS, ROW_CONV = 6, 7, 8


def _bucket_thresholds():
    n = np.arange(REL_MAX_DIST)
    nf = np.maximum(n, 1).astype(np.float32)
    large = REL_EXACT + (np.log(nf / np.float32(REL_EXACT)) / np.float32(math.log(REL_MAX_DIST / REL_EXACT))
                         * np.float32(REL_BUCKETS - REL_EXACT)).astype(np.int32)
    b = np.where(n < REL_EXACT, n, np.minimum(large, REL_BUCKETS - 1))
    return [int(np.argmax(b >= REL_EXACT + k)) for k in range(1, REL_BUCKETS - REL_EXACT)]


BUCKET_THRESHOLDS = _bucket_thresholds()


HBM_SPEC = pl.BlockSpec(memory_space=pl.ANY)


class _Carry:
    def __init__(self, ins, outs, sems, start, finish, mid=None, aliases=None):
        self.ins, self.outs, self.sems = list(ins), list(outs), list(sems)
        self.start, self.finish, self.mid, self.aliases = start, finish, mid, dict(aliases or {})


def _pcall(body, *, name, grid, in_specs, out_specs, out_shape, scratch=(), carry=None, aliases=None):
    params = pltpu.CompilerParams(dimension_semantics=("arbitrary",) * len(grid), vmem_limit_bytes=VMEM_LIMIT_V7X)
    if carry is None:
        return pl.pallas_call(body, name=name, grid=grid, in_specs=in_specs, out_specs=out_specs,
                              out_shape=out_shape, scratch_shapes=list(scratch), compiler_params=params,
                              input_output_aliases=aliases or {})
    assert aliases is None, name
    single = not isinstance(out_shape, (list, tuple))
    own_specs, own_shapes = ([out_specs], [out_shape]) if single else (list(out_specs), list(out_shape))
    n_in, n_out, n_scr = len(in_specs), len(own_shapes), len(scratch)
    n_cin, n_cout = len(carry.ins), len(carry.outs)
    steps = math.prod(grid)
    mid_step = max(steps - 1 - max(steps // 8, 1), 0)

    def carrying(*refs):
        ins, refs = refs[:n_in], refs[n_in:]
        cins, refs = refs[:n_cin], refs[n_cin:]
        outs, refs = refs[:n_out], refs[n_out:]
        couts, refs = refs[:n_cout], refs[n_cout:]
        scr, csems = refs[:n_scr], refs[n_scr:]
        step = 0
        for axis, size in enumerate(grid):
            step = step * size + pl.program_id(axis)

        @pl.when(step == 0)
        def _():
            carry.start(cins, couts, csems)

        body(*ins, *outs, *scr)
        if carry.mid is not None:
            @pl.when(step == mid_step)
            def _():
                carry.mid(cins, couts, csems)

        @pl.when(step == steps - 1)
        def _():
            carry.finish(cins, couts, csems)

    call = pl.pallas_call(carrying, name=name, grid=grid, in_specs=list(in_specs) + [HBM_SPEC] * n_cin,
                          out_specs=own_specs + [HBM_SPEC] * n_cout, out_shape=own_shapes + carry.outs,
                          scratch_shapes=list(scratch) + carry.sems, compiler_params=params,
                          input_output_aliases={n_in + i: n_out + o for i, o in carry.aliases.items()})

    def run(*args):
        res = call(*args, *carry.ins)
        return (res[0] if single else res[:n_out]), res[n_out:]

    return run


def _run_alone(carry, name):
    n_cin, n_cout = len(carry.ins), len(carry.outs)

    def body(*refs):
        cins, couts, csems = refs[:n_cin], refs[n_cin:n_cin + n_cout], refs[n_cin + n_cout:]
        carry.start(cins, couts, csems)
        if carry.mid is not None:
            carry.mid(cins, couts, csems)
        carry.finish(cins, couts, csems)

    return pl.pallas_call(body, name=name, in_specs=[HBM_SPEC] * n_cin, out_specs=[HBM_SPEC] * n_cout,
                          out_shape=carry.outs, scratch_shapes=carry.sems,
                          input_output_aliases=carry.aliases)(*carry.ins)


def _dot(a, b):
    return jnp.dot(a, b, preferred_element_type=F32)


def _dot_nt(a, b):
    return lax.dot_general(a, b, (((1,), (1,)), ((), ())), preferred_element_type=F32)


def _dot_tn(a, b):
    return lax.dot_general(a, b, (((0,), (0,)), ((), ())), preferred_element_type=F32)


def _sds(shape, dtype):
    return jax.ShapeDtypeStruct(tuple(shape), dtype)


ROW_CHUNK = 256


def _row_chunks(tm):
    return [slice(r, min(r + ROW_CHUNK, tm)) for r in range(0, tm, ROW_CHUNK)]


def _carried(call, args, carry):
    return call(*args) if carry is not None else (call(*args), ())


def _rmsnorm(x, g, name, carry=None):
    m, d = x.shape
    tm = min(512, m)

    def body(x_ref, g_ref, h_ref):
        xv = x_ref[...]
        r = lax.rsqrt(jnp.mean(xv * xv, axis=-1, keepdims=True) + EPS)
        h_ref[...] = (xv * r * g_ref[...]).astype(BF16)

    call = _pcall(body, name=name, grid=(m // tm,), carry=carry,
                  in_specs=[pl.BlockSpec((tm, d), lambda i: (i, 0)), pl.BlockSpec((1, d), lambda i: (0, 0))],
                  out_specs=pl.BlockSpec((tm, d), lambda i: (i, 0)), out_shape=_sds((m, d), BF16))
    return _carried(call, (x, g), carry)


def _mm_nn(a, b, name, tm=1024, bt=False, carry=None):
    m, k = a.shape
    nj = b.shape[0]
    n = b.shape[1] if bt else b.shape[2]
    tm = min(tm, m)
    dot = _dot_nt if bt else _dot

    def body(a_ref, b_ref, o_ref):
        o_ref[...] = dot(a_ref[...], b_ref[...]).astype(BF16)

    call = _pcall(body, name=name, grid=(nj, m // tm),
                  in_specs=[pl.BlockSpec((tm, k), lambda j, i: (i, 0)),
                            pl.BlockSpec((None,) + b.shape[1:], lambda j, i: (j, 0, 0))],
                  out_specs=pl.BlockSpec((None, tm, n), lambda j, i: (j, i, 0)),
                  out_shape=_sds((nj, m, n), BF16), carry=carry)
    return _carried(call, (a, b), carry)


def _load_once(src_hbm, dst_vmem, sem):
    @pl.when(pl.program_id(0) == 0)
    def _():
        load = pltpu.make_async_copy(src_hbm, dst_vmem, sem)
        load.start()
        load.wait()


def _resident(w):
    return [pltpu.VMEM(w.shape, w.dtype), pltpu.SemaphoreType.DMA(())]


def _ffn_up(h, w4, name, tm=512, carry=None):
    s, d = h.shape
    tm = min(tm, s)

    def body(h_ref, w_hbm, gu_ref, a_ref, w_ref, w_sem):
        _load_once(w_hbm, w_ref, w_sem)
        for p in range(4):
            for rows in _row_chunks(tm):
                hv = h_ref[rows, :]
                g = _dot_nt(hv, w_ref[0, p])
                u = _dot_nt(hv, w_ref[1, p])
                gu_ref[0, p, rows, :] = g.astype(BF16)
                gu_ref[1, p, rows, :] = u.astype(BF16)
                a_ref[p, rows, :] = (g * jax.nn.sigmoid(g) * u).astype(BF16)

    call = _pcall(body, name=name, grid=(s // tm,),
                  in_specs=[pl.BlockSpec((tm, d), lambda i: (i, 0)), HBM_SPEC],
                  out_specs=[pl.BlockSpec((2, 4, tm, FS), lambda i: (0, 0, i, 0)),
                             pl.BlockSpec((4, tm, FS), lambda i: (0, i, 0))],
                  out_shape=[_sds((2, 4, s, FS), BF16), _sds((4, s, FS), BF16)], scratch=_resident(w4), carry=carry)
    return _carried(call, (h, w4), carry)


N_SEG = 5
IN_PROJ_WEIGHTS = [pltpu.VMEM((NQ, D), BF16), pltpu.VMEM((NKV, D), BF16), pltpu.VMEM((N_SEG, D, D), BF16),
                   pltpu.SemaphoreType.DMA((2 + N_SEG,))]


def _load_in_proj(w_hbm, wq_ref, wkv_ref, wa_ref, sems):
    @pl.when(pl.program_id(0) == 0)
    def _():
        loads = [pltpu.make_async_copy(w_hbm.at[pl.ds(0, NQ)], wq_ref, sems.at[0]),
                 pltpu.make_async_copy(w_hbm.at[pl.ds(NQ, NKV)], wkv_ref, sems.at[1])]
        loads += [pltpu.make_async_copy(w_hbm.at[pl.ds(NQ + NKV + D * j, D)], wa_ref.at[j], sems.at[2 + j])
                  for j in range(N_SEG)]
        for load in loads:
            load.start()
        for load in loads:
            load.wait()


def _in_proj(h, w_in_t, name, tm=512, carry=None):
    s, d = h.shape
    tm = min(tm, s)

    def body(h_ref, w_hbm, pa_ref, q_ref, kv_ref, wq_ref, wkv_ref, wa_ref, sems):
        _load_in_proj(w_hbm, wq_ref, wkv_ref, wa_ref, sems)
        hv = h_ref[...]
        q_ref[...] = _dot_nt(hv, wq_ref[...]).astype(BF16)
        kv_ref[...] = _dot_nt(hv, wkv_ref[...]).astype(BF16)
        for j in range(N_SEG):
            pa_ref[j] = _dot_nt(hv, wa_ref[j]).astype(BF16)

    call = _pcall(body, name=name, grid=(s // tm,), carry=carry,
                  in_specs=[pl.BlockSpec((tm, d), lambda i: (i, 0)), HBM_SPEC],
                  out_specs=[pl.BlockSpec((N_SEG, tm, d), lambda i: (0, i, 0)),
                             pl.BlockSpec((tm, NQ), lambda i: (i, 0)), pl.BlockSpec((tm, NKV), lambda i: (i, 0))],
                  out_shape=[_sds((N_SEG, s, d), BF16), _sds((s, NQ), BF16), _sds((s, NKV), BF16)],
                  scratch=IN_PROJ_WEIGHTS)
    return _carried(call, (h, w_in_t), carry)


def _mm_res_norm(a, w, xres, gain, scale, name, tm=512, carry=None):
    npart, s, kp = a.shape
    tm = min(tm, s)

    def body(a_ref, w_ref, x_ref, g_ref, xo_ref, h_ref):
        for rows in _row_chunks(tm):
            acc = _dot(a_ref[0, rows, :], w_ref[0])
            for p in range(1, npart):
                acc = acc + _dot(a_ref[p, rows, :], w_ref[p])
            xn = x_ref[rows, :] + scale * acc
            xo_ref[rows, :] = xn
            r = lax.rsqrt(jnp.mean(xn * xn, axis=-1, keepdims=True) + EPS)
            h_ref[rows, :] = (xn * r * g_ref[...]).astype(BF16)

    call = _pcall(body, name=name, grid=(s // tm,),
                  in_specs=[pl.BlockSpec((npart, tm, kp), lambda i: (0, i, 0)),
                            pl.BlockSpec((npart, kp, D), lambda i: (0, 0, 0)),
                            pl.BlockSpec((tm, D), lambda i: (i, 0)),
                            pl.BlockSpec((1, D), lambda i: (0, 0))],
                  out_specs=[pl.BlockSpec((tm, D), lambda i: (i, 0)), pl.BlockSpec((tm, D), lambda i: (i, 0))],
                  out_shape=[_sds((s, D), F32), _sds((s, D), BF16)], carry=carry)
    return _carried(call, (a, w, xres, gain), carry)


def _ffn_down_loss(a, w, xres, gain, target, name, tm=512):
    npart, s, kp = a.shape
    tm = min(tm, s)

    def body(a_ref, w_ref, x_ref, g_ref, t_ref, dx_ref, dxb_ref, loss_ref, dg_ref):
        @pl.when(pl.program_id(0) == 0)
        def _():
            loss_ref[...] = jnp.zeros_like(loss_ref)
            dg_ref[...] = jnp.zeros_like(dg_ref)

        for rows in _row_chunks(tm):
            acc = _dot(a_ref[0, rows, :], w_ref[0])
            for p in range(1, npart):
                acc = acc + _dot(a_ref[p, rows, :], w_ref[p])
            xn = x_ref[rows, :] + 0.5 * acc
            r = lax.rsqrt(jnp.mean(xn * xn, axis=-1, keepdims=True) + EPS)
            xh = xn * r
            gv = g_ref[...]
            err = xh * gv - t_ref[rows, :]
            part = 0.5 * jnp.sum(jnp.mean(err * err, axis=-1, keepdims=True), axis=0, keepdims=True)
            dy = err * (1.0 / D)
            dyg = dy * gv
            dxn = r * (dyg - xh * jnp.mean(dyg * xh, axis=-1, keepdims=True))
            dx_ref[rows, :] = dxn
            dxb_ref[rows, :] = dxn.astype(BF16)
            loss_ref[...] += jnp.broadcast_to(part, loss_ref.shape)
            dg_ref[...] += jnp.sum(dy * xh, axis=0, keepdims=True)

    return _pcall(body, name=name, grid=(s // tm,),
                  in_specs=[pl.BlockSpec((npart, tm, kp), lambda i: (0, i, 0)),
                            pl.BlockSpec((npart, kp, D), lambda i: (0, 0, 0)),
                            pl.BlockSpec((tm, D), lambda i: (i, 0)),
                            pl.BlockSpec((1, D), lambda i: (0, 0)),
                            pl.BlockSpec((tm, D), lambda i: (i, 0))],
                  out_specs=[pl.BlockSpec((tm, D), lambda i: (i, 0)), pl.BlockSpec((tm, D), lambda i: (i, 0)),
                             pl.BlockSpec((8, 128), lambda i: (0, 0)), pl.BlockSpec((1, D), lambda i: (0, 0))],
                  out_shape=[_sds((s, D), F32), _sds((s, D), BF16), _sds((8, 128), F32), _sds((1, D), F32)],
                  )(a, w, xres, gain, target)


def _window_tiles():
    i = lax.broadcasted_iota(jnp.int32, (BLK, BLK), 0)
    j = lax.broadcasted_iota(jnp.int32, (BLK, BLK), 1)
    rel = (i - j) & (BLK - 1)
    large = jnp.full_like(rel, REL_EXACT)
    for t in BUCKET_THRESHOLDS:
        large = large + (rel >= t).astype(jnp.int32)
    return j <= i, jnp.where(rel < REL_EXACT, rel, large)


def _bias_build(rel_bias, name):
    def body(rb_ref, o_ref):
        _, bucket = _window_tiles()

        def per_head(h, carry):
            acc = jnp.zeros((BLK, BLK), F32)
            for b in range(REL_BUCKETS):
                acc = jnp.where(bucket == b, rb_ref[b, h], acc)
            o_ref[h] = acc
            return carry

        lax.fori_loop(0, N_HEADS, per_head, 0)

    return _pcall(body, name=name, grid=(1,),
                  in_specs=[pl.BlockSpec(memory_space=pltpu.SMEM)],
                  out_specs=pl.BlockSpec((N_HEADS, BLK, BLK), lambda i: (0, 0, 0)),
                  out_shape=_sds((N_HEADS, BLK, BLK), F32))(rel_bias)


def _bias_bwd(dbias, name):
    def body(db_ref, o_ref):
        _, bucket = _window_tiles()
        lane = lax.broadcasted_iota(jnp.int32, (N_HEADS, 128), 1)

        def per_bucket(b, out):
            mb = (bucket == b).astype(F32)
            per_col = jnp.sum(db_ref[...] * mb[None, :, :], axis=1)
            return jnp.where(lane == b, jnp.sum(per_col, axis=1, keepdims=True), out)

        o_ref[...] = lax.fori_loop(0, REL_BUCKETS, per_bucket, jnp.zeros((N_HEADS, 128), F32))

    return _pcall(body, name=name, grid=(1,),
                  in_specs=[pl.BlockSpec((N_HEADS, BLK, BLK), lambda i: (0, 0, 0))],
                  out_specs=pl.BlockSpec((N_HEADS, 128), lambda i: (0, 0)),
                  out_shape=_sds((N_HEADS, 128), F32))(dbias)


PAIR = 2 * HEAD
GROUP = N_HEADS // N_KV
SWA_SCALE = HEAD ** -0.5


def _window_masks(n):
    i = lax.broadcasted_iota(jnp.int32, (GROUP * BLK, BLK), 0) & (BLK - 1)
    j = lax.broadcasted_iota(jnp.int32, (GROUP * BLK, BLK), 1)
    return j <= i, jnp.logical_and(n == 0, j > i), j < HEAD


def _kv_twice(ref, base, g, low):
    slab = ref[:, base + PAIR * (g // 2): base + PAIR * (g // 2 + 1)]
    swapped = pltpu.roll(slab, HEAD, 1)
    return jnp.where(low, slab, swapped) if g % 2 == 0 else jnp.where(low, swapped, slab)


def _stack_heads(ref, g, low):
    parts = []
    for r in range(2):
        slab = ref[:, PAIR * (2 * g + r): PAIR * (2 * g + r + 1)]
        zero = jnp.zeros_like(slab)
        parts += [jnp.where(low, slab, zero), jnp.where(low, zero, slab)]
    return jnp.concatenate(parts, axis=0)


def _unstack_heads(t, low):
    return [jnp.where(low, t[2 * r * BLK:(2 * r + 1) * BLK], t[(2 * r + 1) * BLK:(2 * r + 2) * BLK])
            for r in range(2)]


def _head_rows(t, k):
    return t[k * BLK:(k + 1) * BLK]


def _per_head_column(values):
    head = lax.broadcasted_iota(jnp.int32, (GROUP * BLK, 1), 0) // BLK
    col = jnp.full((GROUP * BLK, 1), values[0], F32)
    for k in range(1, GROUP):
        col = jnp.where(head == k, values[k], col)
    return col


def _window_logits(q4, kc, kp, bias4, own, absent):
    sc = jnp.where(own, _dot_nt(q4, kc), _dot_nt(q4, kp)) * SWA_SCALE + bias4
    return jnp.where(absent, NEG, sc)


def _split_window(t, own):
    zero = jnp.zeros_like(t)
    return jnp.where(own, t, zero), jnp.where(own, zero, t)


def _swa_fwd(q, kv, bias, sinks, name, carry=None):
    s = q.shape[0]
    nb = s // BLK
    kvw = 2 * N_KV * HEAD

    def body(q_ref, kc_ref, kp_ref, b_ref, sk_ref, o_ref, lse_ref):
        own, absent, low4 = _window_masks(pl.program_id(0))
        low = low4[:BLK]
        lane = lax.broadcasted_iota(jnp.int32, (BLK, 128), 1)
        lse_t = jnp.zeros((BLK, 128), F32)
        for g in range(N_KV):
            q4 = _stack_heads(q_ref, g, low)
            kc, kp = _kv_twice(kc_ref, 0, g, low), _kv_twice(kp_ref, 0, g, low)
            vc, vp = _kv_twice(kc_ref, N_KV * HEAD, g, low), _kv_twice(kp_ref, N_KV * HEAD, g, low)
            bias4 = b_ref[GROUP * g:GROUP * (g + 1)].reshape(GROUP * BLK, BLK)
            sc = _window_logits(q4, kc, kp, bias4, own, absent)
            sk = _per_head_column([sk_ref[0, GROUP * g + k] for k in range(GROUP)])
            m = jnp.maximum(jnp.max(sc, axis=1, keepdims=True), sk)
            p = jnp.exp(sc - m)
            l = jnp.sum(p, axis=1, keepdims=True) + jnp.exp(sk - m)
            p_own, p_prev = _split_window(p.astype(BF16), own)
            out = (_dot(p_own, vc) + _dot(p_prev, vp)) * (1.0 / l)
            for r, slab in enumerate(_unstack_heads(out, low)):
                o_ref[:, PAIR * (2 * g + r): PAIR * (2 * g + r + 1)] = slab.astype(BF16)
            lse4 = m + jnp.log(l)
            for k in range(GROUP):
                lse_t = jnp.where(lane == GROUP * g + k, _head_rows(lse4, k), lse_t)
        lse_ref[...] = lse_t

    call = _pcall(body, name=name, grid=(nb,),
                  in_specs=[pl.BlockSpec((BLK, D), lambda n: (n, 0)),
                            pl.BlockSpec((BLK, kvw), lambda n: (n, 0)),
                            pl.BlockSpec((BLK, kvw), lambda n: (jnp.maximum(n - 1, 0), 0)),
                            pl.BlockSpec((N_HEADS, BLK, BLK), lambda n: (0, 0, 0)),
                            pl.BlockSpec(memory_space=pltpu.SMEM)],
                  out_specs=[pl.BlockSpec((BLK, D), lambda n: (n, 0)), pl.BlockSpec((BLK, 128), lambda n: (n, 0))],
                  out_shape=[_sds((s, D), BF16), _sds((s, 128), F32)], carry=carry)
    return _carried(call, (q, kv, kv, bias, sinks), carry)


def _fold_halves(t, g, low):
    folded = jnp.where(low, t, 0.0) + pltpu.roll(jnp.where(low, 0.0, t), HEAD, 1)
    return folded if g % 2 == 0 else pltpu.roll(folded, HEAD, 1)


def _swa_bwd(q, kv, attn, dattn, lse, bias, sinks, name, carry=None):
    s = q.shape[0]
    nb = s // BLK
    kvw = 2 * N_KV * HEAD
    voff = N_KV * HEAD

    def body(q_ref, kc_ref, kp_ref, o_ref, do_ref, lse_ref, b_ref, skrow_ref, dq_ref, dkv_ref, dbias_ref, dsk_ref,
             dq_hold, kv_hold, dq_new, kv_prev, kv_cur):
        n = pl.program_id(0)

        @pl.when(n == 0)
        def _():
            dbias_ref[...] = jnp.zeros_like(dbias_ref)
            dsk_ref[...] = jnp.zeros_like(dsk_ref)
            dq_hold[...] = jnp.zeros_like(dq_hold)
            kv_hold[...] = jnp.zeros_like(kv_hold)

        @pl.when(n < nb)
        def _():
            own, absent, low4 = _window_masks(n)
            low = low4[:BLK]
            lane = lax.broadcasted_iota(jnp.int32, (BLK, 128), 1)
            delta_t = jnp.zeros((BLK, 128), F32)
            ones = jnp.ones((PAIR, 128), BF16)
            for pair_of_kv in range(N_KV // 2):
                slab_grads = [jnp.zeros((BLK, PAIR), F32) for _ in range(4)]
                for g in (2 * pair_of_kv, 2 * pair_of_kv + 1):
                    q4, do4 = _stack_heads(q_ref, g, low), _stack_heads(do_ref, g, low)
                    kc, kp = _kv_twice(kc_ref, 0, g, low), _kv_twice(kp_ref, 0, g, low)
                    vc, vp = _kv_twice(kc_ref, voff, g, low), _kv_twice(kp_ref, voff, g, low)
                    o_slabs = [o_ref[:, PAIR * (2 * g + r): PAIR * (2 * g + r + 1)] for r in range(2)]
                    o4 = jnp.concatenate([o_slabs[0], o_slabs[0], o_slabs[1], o_slabs[1]], axis=0)
                    delta = _dot(do4 * o4, ones)
                    heads = range(GROUP * g, GROUP * (g + 1))
                    lse4 = jnp.concatenate([lse_ref[:, h:h + 1] for h in heads], axis=0)
                    bias4 = b_ref[GROUP * g:GROUP * (g + 1)].reshape(GROUP * BLK, BLK)
                    p = jnp.exp(_window_logits(q4, kc, kp, bias4, own, absent) - lse4)
                    dp = jnp.where(own, _dot_nt(do4, vc), _dot_nt(do4, vp))
                    ds = p * (dp - delta)
                    dbias_ref[GROUP * g:GROUP * (g + 1)] += ds.reshape(GROUP, BLK, BLK)
                    for k, h in enumerate(heads):
                        delta_t = jnp.where(lane == h, _head_rows(delta, k), delta_t)
                    ds_own, ds_prev = _split_window((ds * SWA_SCALE).astype(BF16), own)
                    p_own, p_prev = _split_window(p.astype(BF16), own)
                    dq4 = _dot(ds_own, kc) + _dot(ds_prev, kp)
                    for r, slab in enumerate(_unstack_heads(dq4, low)):
                        dq_new[:, PAIR * (2 * g + r): PAIR * (2 * g + r + 1)] = slab
                    grads = [_dot_tn(ds_own, q4), _dot_tn(ds_prev, q4), _dot_tn(p_own, do4), _dot_tn(p_prev, do4)]
                    slab_grads = [t + _fold_halves(dk, g, low) for t, dk in zip(slab_grads, grads)]
                ks = slice(PAIR * pair_of_kv, PAIR * (pair_of_kv + 1))
                vs = slice(voff + PAIR * pair_of_kv, voff + PAIR * (pair_of_kv + 1))
                kv_cur[:, ks], kv_prev[:, ks], kv_cur[:, vs], kv_prev[:, vs] = slab_grads
            dsk_ref[...] -= jnp.sum(jnp.exp(skrow_ref[...] - lse_ref[...]) * delta_t, axis=0, keepdims=True)

        @pl.when(n == nb)
        def _():
            kv_prev[...] = jnp.zeros_like(kv_prev)

        dq_ref[...] = dq_hold[...].astype(BF16)
        dkv_ref[...] = (kv_hold[...] + kv_prev[...]).astype(BF16)

        @pl.when(n < nb)
        def _():
            dq_hold[...] = dq_new[...]
            kv_hold[...] = kv_cur[...]

    def cur(n):
        return jnp.minimum(n, nb - 1)

    call = _pcall(body, name=name, grid=(nb + 1,), carry=carry,
                  in_specs=[pl.BlockSpec((BLK, D), lambda n: (cur(n), 0)),
                            pl.BlockSpec((BLK, kvw), lambda n: (cur(n), 0)),
                            pl.BlockSpec((BLK, kvw), lambda n: (jnp.maximum(cur(n) - 1, 0), 0)),
                            pl.BlockSpec((BLK, D), lambda n: (cur(n), 0)),
                            pl.BlockSpec((BLK, D), lambda n: (cur(n), 0)),
                            pl.BlockSpec((BLK, 128), lambda n: (cur(n), 0)),
                            pl.BlockSpec((N_HEADS, BLK, BLK), lambda n: (0, 0, 0)),
                            pl.BlockSpec((1, 128), lambda n: (0, 0))],
                  out_specs=[pl.BlockSpec((BLK, D), lambda n: (jnp.maximum(n - 1, 0), 0)),
                             pl.BlockSpec((BLK, kvw), lambda n: (jnp.maximum(n - 1, 0), 0)),
                             pl.BlockSpec((N_HEADS, BLK, BLK), lambda n: (0, 0, 0)),
                             pl.BlockSpec((1, 128), lambda n: (0, 0))],
                  out_shape=[_sds((s, D), BF16), _sds((s, kvw), BF16), _sds((N_HEADS, BLK, BLK), F32),
                             _sds((1, 128), F32)],
                  scratch=[pltpu.VMEM((BLK, D), F32), pltpu.VMEM((BLK, kvw), F32), pltpu.VMEM((BLK, D), F32),
                           pltpu.VMEM((BLK, kvw), F32), pltpu.VMEM((BLK, kvw), F32)])
    sink_row = jnp.pad(sinks, ((0, 0), (0, 128 - N_HEADS)))
    return _carried(call, (q, kv, kv, attn, dattn, lse, bias, sink_row), carry)


HALO = 16
CW = D


def _conv_taps(cu, halo_cu, first_tile):
    row = lax.broadcasted_iota(jnp.int32, cu.shape, 0)
    halo_cu = jnp.where(first_tile, 0.0, halo_cu)
    c1 = jnp.where(row == 0, halo_cu[HALO - 1:HALO], pltpu.roll(cu, 1, 0))
    c2 = jnp.where(row == 0, halo_cu[HALO - 2:HALO - 1],
                   jnp.where(row == 1, halo_cu[HALO - 1:HALO], pltpu.roll(cu, 2, 0)))
    return c1, c2


def _conv_merge_fwd(pa, attn, convw, name, ts=256):
    _, s, _ = pa.shape
    ts = min(ts, s)
    hb = ts // HALO

    def body(pa_ref, hp_ref, at_ref, w_ref, o_ref):
        i = pl.program_id(1)
        cu = pa_ref[0].astype(F32) * pa_ref[2].astype(F32)
        c1, c2 = _conv_taps(cu, hp_ref[0].astype(F32) * hp_ref[2].astype(F32), i == 0)
        w = w_ref[...]
        c3 = w[0:1] * c2 + w[1:2] * c1 + w[2:3] * cu
        conv = pa_ref[1].astype(F32) * c3
        o_ref[...] = (jax.nn.sigmoid(pa_ref[3].astype(F32)) * at_ref[...].astype(F32)
                      + jax.nn.sigmoid(pa_ref[4].astype(F32)) * conv).astype(BF16)

    return _pcall(body, name=name, grid=(D // CW, s // ts),
                  in_specs=[pl.BlockSpec((5, ts, CW), lambda c, i: (0, i, c)),
                            pl.BlockSpec((5, HALO, CW), lambda c, i: (0, jnp.maximum(i * hb - 1, 0), c)),
                            pl.BlockSpec((ts, CW), lambda c, i: (i, c)),
                            pl.BlockSpec((8, CW), lambda c, i: (0, c))],
                  out_specs=pl.BlockSpec((ts, CW), lambda c, i: (i, c)),
                  out_shape=_sds((s, D), BF16))(pa, pa, attn, convw)


def _conv_merge_bwd(dmerged, pa, attn, convw, name, ts=256, carry=None):
    _, s, _ = pa.shape
    ts = min(ts, s)
    hb = ts // HALO
    last_hb = s // HALO - 1

    def body(dm_ref, pa_ref, at_ref, w_ref, hp_ref, hn_ref, dmn_ref, dat_ref, dpa_ref, dw_ref):
        i = pl.program_id(1)
        last = i == pl.num_programs(1) - 1
        dm = dm_ref[...].astype(F32)
        cp, bp, u = pa_ref[0].astype(F32), pa_ref[1].astype(F32), pa_ref[2].astype(F32)
        sa = jax.nn.sigmoid(pa_ref[3].astype(F32))
        sc = jax.nn.sigmoid(pa_ref[4].astype(F32))
        at = at_ref[...].astype(F32)
        cu = cp * u
        c1, c2 = _conv_taps(cu, hp_ref[0].astype(F32) * hp_ref[2].astype(F32), i == 0)
        w = w_ref[...]
        c3 = w[0:1] * c2 + w[1:2] * c1 + w[2:3] * cu
        dconv = dm * sc
        dc3 = dconv * bp
        nxt = dmn_ref[...].astype(F32) * jax.nn.sigmoid(hn_ref[4].astype(F32)) * hn_ref[1].astype(F32)
        nxt = jnp.where(last, 0.0, nxt)
        row = lax.broadcasted_iota(jnp.int32, dc3.shape, 0)
        d1 = jnp.where(row == ts - 1, nxt[0:1], pltpu.roll(dc3, ts - 1, 0))
        d2 = jnp.where(row == ts - 2, nxt[0:1], jnp.where(row == ts - 1, nxt[1:2], pltpu.roll(dc3, ts - 2, 0)))
        dcu = w[2:3] * dc3 + w[1:2] * d1 + w[0:1] * d2
        dat_ref[...] = (dm * sa).astype(BF16)
        dpa_ref[0] = (dcu * u).astype(BF16)
        dpa_ref[1] = (dconv * c3).astype(BF16)
        dpa_ref[2] = (dcu * cp).astype(BF16)
        dpa_ref[3] = (dm * at * sa * (1.0 - sa)).astype(BF16)
        dpa_ref[4] = (dm * bp * c3 * sc * (1.0 - sc)).astype(BF16)

        @pl.when(i == 0)
        def _():
            dw_ref[...] = jnp.zeros_like(dw_ref)

        dw_ref[0:1, :] += jnp.sum(dc3 * c2, axis=0, keepdims=True)
        dw_ref[1:2, :] += jnp.sum(dc3 * c1, axis=0, keepdims=True)
        dw_ref[2:3, :] += jnp.sum(dc3 * cu, axis=0, keepdims=True)

    call = _pcall(body, name=name, grid=(D // CW, s // ts), carry=carry,
                  in_specs=[pl.BlockSpec((ts, CW), lambda c, i: (i, c)),
                            pl.BlockSpec((5, ts, CW), lambda c, i: (0, i, c)),
                            pl.BlockSpec((ts, CW), lambda c, i: (i, c)),
                            pl.BlockSpec((8, CW), lambda c, i: (0, c)),
                            pl.BlockSpec((5, HALO, CW), lambda c, i: (0, jnp.maximum(i * hb - 1, 0), c)),
                            pl.BlockSpec((5, HALO, CW), lambda c, i: (0, jnp.minimum((i + 1) * hb, last_hb), c)),
                            pl.BlockSpec((HALO, CW), lambda c, i: (jnp.minimum((i + 1) * hb, last_hb), c))],
                  out_specs=[pl.BlockSpec((ts, CW), lambda c, i: (i, c)),
                             pl.BlockSpec((5, ts, CW), lambda c, i: (0, i, c)),
                             pl.BlockSpec((8, CW), lambda c, i: (0, c))],
                  out_shape=[_sds((s, D), BF16), _sds((5, s, D), BF16), _sds((8, D), F32)])
    return _carried(call, (dmerged, pa, attn, convw, pa, pa, dmerged), carry)


def _xattn_fwd(q, kv, name, tq=512):
    s, _ = q.shape
    nm = kv.shape[1]
    tq = min(tq, s)

    def body(q_ref, kv_ref, o_ref, lse_ref):
        lane = lax.broadcasted_iota(jnp.int32, (tq, 128), 1)
        lse_t = jnp.zeros((tq, 128), F32)
        for h in range(XH):
            hs = slice(XHD * h, XHD * (h + 1))
            sc = _dot_nt(q_ref[:, hs], kv_ref[h]) * (XHD ** -0.5)
            m = jnp.max(sc, axis=1, keepdims=True)
            p = jnp.exp(sc - m)
            l = jnp.sum(p, axis=1, keepdims=True)
            o_ref[:, hs] = (_dot(p.astype(BF16), kv_ref[XH + h]) * (1.0 / l)).astype(BF16)
            lse_t = jnp.where(lane == h, m + jnp.log(l), lse_t)
        lse_ref[...] = lse_t

    return _pcall(body, name=name, grid=(s // tq,),
                  in_specs=[pl.BlockSpec((tq, D), lambda i: (i, 0)), pl.BlockSpec((2 * XH, nm, XHD), lambda i: (0, 0, 0))],
                  out_specs=[pl.BlockSpec((tq, D), lambda i: (i, 0)), pl.BlockSpec((tq, 128), lambda i: (i, 0))],
                  out_shape=[_sds((s, D), BF16), _sds((s, 128), F32)])(q, kv)


def _xattn_bwd(q, kv, o, do, lse, name, tq=512, carry=None):
    s, _ = q.shape
    nm = kv.shape[1]
    tq = min(tq, s)

    def body(q_ref, kv_ref, o_ref, do_ref, lse_ref, dq_ref, dkv_ref):
        @pl.when(pl.program_id(0) == 0)
        def _():
            dkv_ref[...] = jnp.zeros_like(dkv_ref)

        for h in range(XH):
            hs = slice(XHD * h, XHD * (h + 1))
            qh, kh, vh, dob = q_ref[:, hs], kv_ref[h], kv_ref[XH + h], do_ref[:, hs]
            p = jnp.exp(_dot_nt(qh, kh) * (XHD ** -0.5) - lse_ref[:, h:h + 1])
            dp = _dot_nt(dob, vh)
            delta = jnp.sum(dob.astype(F32) * o_ref[:, hs].astype(F32), axis=1, keepdims=True)
            dsb = (p * (dp - delta) * (XHD ** -0.5)).astype(BF16)
            dq_ref[:, hs] = _dot(dsb, kh).astype(BF16)
            dkv_ref[h] += _dot_tn(dsb, qh)
            dkv_ref[XH + h] += _dot_tn(p.astype(BF16), dob)

    call = _pcall(body, name=name, grid=(s // tq,), carry=carry,
                  in_specs=[pl.BlockSpec((tq, D), lambda i: (i, 0)), pl.BlockSpec((2 * XH, nm, XHD), lambda i: (0, 0, 0)),
                            pl.BlockSpec((tq, D), lambda i: (i, 0)), pl.BlockSpec((tq, D), lambda i: (i, 0)),
                            pl.BlockSpec((tq, 128), lambda i: (i, 0))],
                  out_specs=[pl.BlockSpec((tq, D), lambda i: (i, 0)), pl.BlockSpec((2 * XH, nm, XHD), lambda i: (0, 0, 0))],
                  out_shape=[_sds((s, D), BF16), _sds((2 * XH, nm, XHD), F32)])
    return _carried(call, (q, kv, o, do, lse), carry)


def _ffn_down_bwd(dxb, wd4, gu4, name, tm=512, carry=None):
    s, _ = dxb.shape
    tm = min(tm, s)

    def body(dx_ref, w_hbm, gu_ref, o_ref, w_ref, w_sem):
        _load_once(w_hbm, w_ref, w_sem)
        for p in range(4):
            for rows in _row_chunks(tm):
                da = _dot_nt(dx_ref[rows, :], w_ref[p])
                g = gu_ref[0, p, rows, :].astype(F32)
                u = gu_ref[1, p, rows, :].astype(F32)
                sg = jax.nn.sigmoid(g)
                t = da * sg
                o_ref[0, p, rows, :] = (t * u * (1.0 + g - g * sg)).astype(BF16)
                o_ref[1, p, rows, :] = (t * g).astype(BF16)

    block = pl.BlockSpec((2, 4, tm, FS), lambda i: (0, 0, i, 0))
    call = _pcall(body, name=name, grid=(s // tm,), carry=carry,
                  in_specs=[pl.BlockSpec((tm, D), lambda i: (i, 0)), HBM_SPEC, block],
                  out_specs=block, out_shape=_sds((2, 4, s, FS), BF16), scratch=_resident(wd4))
    return _carried(call, (dxb, wd4, gu4), carry)


def _mm_tn(a, b, name, scale=1.0, carry=None):
    pa_n, s, m = a.shape
    pb_n, _, n = b.shape
    po = max(pa_n, pb_n)
    tn = n if po >= 4 else min(n, 256)

    def body(a_ref, b_ref, o_ref):
        o_ref[...] = (scale * _dot_tn(a_ref[...], b_ref[...])).astype(BF16)

    call = _pcall(body, name=name, grid=(po, n // tn), carry=carry,
                  in_specs=[pl.BlockSpec((None, s, m), lambda o, j: (o if pa_n > 1 else 0, 0, 0)),
                            pl.BlockSpec((None, s, tn), lambda o, j: (o if pb_n > 1 else 0, 0, j))],
                  out_specs=pl.BlockSpec((None, m, tn), lambda o, j: (o, 0, j)),
                  out_shape=_sds((po, m, n), BF16))
    return _carried(call, (a, b), carry)


def _mm_tn_rows(a, b, name, total_rows, row0, begun=None, tm=512):
    p, s, m = a.shape
    n = b.shape[1]
    tm = min(tm, m)
    tiles = m // tm
    assert row0 % tm == 0 and m % tm == 0, (row0, m, tm)

    def body(a_ref, b_ref, *rest):
        rest[-1][...] = _dot_tn(a_ref[...], b_ref[...]).astype(BF16)

    in_specs = [pl.BlockSpec((None, s, tm), lambda o, i: (o, 0, i)), pl.BlockSpec((s, n), lambda o, i: (0, 0))]
    call = _pcall(body, name=name, grid=(p, tiles), in_specs=in_specs + ([HBM_SPEC] if begun is not None else []),
                  out_specs=pl.BlockSpec((tm, n), lambda o, i: (row0 // tm + o * tiles + i, 0)),
                  out_shape=_sds((total_rows, n), BF16), aliases={2: 0} if begun is not None else None)
    return call(a, b, begun) if begun is not None else call(a, b)


def _sum_dots(a_ref, b_ref, nj, bt, rows=slice(None)):
    dot = _dot_nt if bt else _dot
    acc = dot(a_ref[0, rows, :], b_ref[0])
    for j in range(1, nj):
        acc = acc + dot(a_ref[j, rows, :], b_ref[j])
    return acc


def _mm_acc(a, b, name, out_dtype, tm=512, bt=False, carry=None):
    nj, s, k = a.shape
    n = b.shape[1] if bt else b.shape[2]
    tm = min(tm, s)

    def body(a_ref, b_ref, o_ref):
        o_ref[...] = _sum_dots(a_ref, b_ref, nj, bt).astype(out_dtype)

    call = _pcall(body, name=name, grid=(s // tm,), carry=carry,
                  in_specs=[pl.BlockSpec((nj, tm, k), lambda i: (0, i, 0)),
                            pl.BlockSpec(b.shape, lambda i: (0, 0, 0))],
                  out_specs=pl.BlockSpec((tm, n), lambda i: (i, 0)), out_shape=_sds((s, n), out_dtype))
    return _carried(call, (a, b), carry)


def _rms_bwd_call(name, acts, weights, scratch, load, dh_rows, *, x, gain, dres, tm, carry):
    s, n = x.shape
    tm = min(tm, s)
    n_act, n_w = len(acts), len(weights)

    def body(*refs):
        act_refs, w_refs = refs[:n_act], refs[n_act:n_act + n_w]
        x_ref, g_ref, r_ref, dx_ref, dxb_ref, dg_ref = refs[n_act + n_w:n_act + n_w + 6]
        held = refs[n_act + n_w + 6:]
        load(w_refs, held)

        @pl.when(pl.program_id(0) == 0)
        def _():
            dg_ref[...] = jnp.zeros_like(dg_ref)

        for rows in _row_chunks(tm):
            dh = dh_rows(act_refs, held, rows)
            xv = x_ref[rows, :]
            r = lax.rsqrt(jnp.mean(xv * xv, axis=-1, keepdims=True) + EPS)
            xh = xv * r
            dyg = dh * g_ref[...]
            dx = r_ref[rows, :] + r * (dyg - xh * jnp.mean(dyg * xh, axis=-1, keepdims=True))
            dx_ref[rows, :] = dx
            dxb_ref[rows, :] = dx.astype(BF16)
            dg_ref[...] += jnp.sum(dh * xh, axis=0, keepdims=True)

    def tile(a):
        return (pl.BlockSpec((tm, a.shape[1]), lambda i: (i, 0)) if a.ndim == 2
                else pl.BlockSpec((a.shape[0], tm, a.shape[2]), lambda i: (0, i, 0)))

    row = pl.BlockSpec((tm, n), lambda i: (i, 0))
    in_specs = [tile(a) for a in acts] + [HBM_SPEC] * n_w + [row, pl.BlockSpec((1, n), lambda i: (0, 0)), row]
    call = _pcall(body, name=name, grid=(s // tm,), in_specs=in_specs, carry=carry,
                  out_specs=[row, row, pl.BlockSpec((1, n), lambda i: (0, 0))],
                  out_shape=[_sds((s, n), F32), _sds((s, n), BF16), _sds((1, n), F32)], scratch=scratch)
    return _carried(call, tuple(acts) + tuple(weights) + (x, gain, dres), carry)


def _mm_acc_rms_bwd(a, b, name, *, x, gain, dres, scale=None, tm=512, bt=False, carry=None):
    def load(w_refs, held):
        _load_once(w_refs[0], held[0], held[1])

    def dh_rows(act_refs, held, rows):
        dh = _sum_dots(act_refs[0], held[0], a.shape[0], bt, rows)
        return dh if scale is None else scale * dh

    return _rms_bwd_call(name, [a], [b], _resident(b), load, dh_rows, x=x, gain=gain, dres=dres, tm=tm, carry=carry)


def _in_proj_bwd(dpa, dq, dkv, w_in_t, name, *, x, gain, dres, tm=512, carry=None):
    def load(w_refs, held):
        _load_in_proj(w_refs[0], *held)

    def dh_rows(act_refs, held, rows):
        dpa_ref, dq_ref, dkv_ref = act_refs
        wq_ref, wkv_ref, wa_ref, _ = held
        dh = _dot(dq_ref[rows, :], wq_ref[...]) + _dot(dkv_ref[rows, :], wkv_ref[...])
        return dh + _sum_dots(dpa_ref, wa_ref, N_SEG, False, rows)

    return _rms_bwd_call(name, [dpa, dq, dkv], [w_in_t], IN_PROJ_WEIGHTS, load, dh_rows, x=x, gain=gain, dres=dres,
                         tm=tm, carry=carry)


def _adam(w, g, m, v):
    m2 = ADAM_B1 * m + (1.0 - ADAM_B1) * g
    v2 = ADAM_B2 * v + (1.0 - ADAM_B2) * (g * g)
    m_hat = m2 / (1.0 - ADAM_B1 ** ADAM_STEP)
    v_hat = v2 / (1.0 - ADAM_B2 ** ADAM_STEP)
    delta = -ADAM_LR * (m_hat / (jnp.sqrt(v_hat) + ADAM_EPS) + ADAM_WD * w)
    return delta, m2, v2


def _adamw(parts, w, m, v, name):
    _, r, c = parts.shape
    tr = max(t for t in range(16, 257, 16) if r % t == 0)

    def body(p_ref, w_ref, m_ref, v_ref, g_ref, d_ref, m2_ref, v2_ref):
        g = p_ref[0].astype(F32)
        for i in range(1, N_DEV):
            g = g + p_ref[i].astype(F32)
        delta, m2, v2 = _adam(w_ref[...], g, m_ref[...], v_ref[...])
        g_ref[...] = g
        d_ref[...] = delta
        m2_ref[...] = m2
        v2_ref[...] = v2

    blk = pl.BlockSpec((tr, c), lambda i: (i, 0))
    return _pcall(body, name=name, grid=(r // tr,),
                  in_specs=[pl.BlockSpec((N_DEV, tr, c), lambda i: (0, i, 0)), blk, blk, blk],
                  out_specs=[blk] * 4, out_shape=[_sds((r, c), F32)] * 4)(parts, w, m, v)


def _position():
    return lax.axis_index("x"), lax.axis_index("y"), lax.axis_index("c")


def _slot(px, py, pc):
    return 4 * px + 2 * py + pc


def _row_window(ref, rows):
    r0, r1 = rows
    return ref if (r0, r1) == (0, ref.shape[0]) else ref.at[pl.ds(r0, r1 - r0)]


def _split_items(items):
    sources = [src for src, _, _ in items]
    begun = [(a, dest) for a, (_, _, dest) in enumerate(items) if dest is not None]
    aliases = {len(sources) + k: a for k, (a, _) in enumerate(begun)}
    return sources + [dest for _, dest in begun], [rows for _, rows, _ in items], aliases


def _gather_carry(items):
    na = len(items)
    carry_ins, windows, aliases = _split_items(items)

    def plan(ins, outs, sems):
        send_sems, recv_sems, local_sems = sems
        x, y, c = _position()
        me, sibling = (x, y, c), (x, y, 1 - c)
        chips = [(1 - x, y), (x, 1 - y), (1 - x, 1 - y)]
        ins = [_row_window(ins[a], windows[a]) for a in range(na)]

        def block_rows(a, block):
            return _row_window(outs[a].at[_slot(*block)], windows[a])

        def copy(a, k, block, to, src=None):
            rows = block_rows(a, block)
            return pltpu.make_async_remote_copy(src_ref=rows if src is None else src, dst_ref=rows,
                                                send_sem=send_sems.at[k, a], recv_sem=recv_sems.at[k, a],
                                                device_id=to, device_id_type=MESH)

        mine = [pltpu.make_async_copy(ins[a], block_rows(a, me), local_sems.at[a]) for a in range(na)]
        first = [copy(a, 0, me, sibling, src=ins[a]) for a in range(na)]
        for j, chip in enumerate(chips):
            first += [copy(a, 1 + j, me, (*chip, c), src=ins[a]) for a in range(na)]
        landed = [[copy(a, 1 + j, (*chip, c), me) for a in range(na)] for j, chip in enumerate(chips)]
        passed = [[copy(a, 4 + j, (*chip, c), sibling) for a in range(na)] for j, chip in enumerate(chips)]
        from_sibling = [copy(a, 0, sibling, me) for a in range(na)]
        for j, chip in enumerate(chips):
            from_sibling += [copy(a, 4 + j, (*chip, 1 - c), me) for a in range(na)]
        return mine, first, landed, passed, from_sibling

    def start(ins, outs, sems):
        mine, first, _, _, _ = plan(ins, outs, sems)
        for cp in mine + first:
            cp.start()

    def mid(ins, outs, sems):
        _, _, landed, passed, _ = plan(ins, outs, sems)
        for over_ici, onward in zip(landed, passed):
            for cp, fwd in zip(over_ici, onward):
                cp.wait_recv()
                fwd.start()

    def finish(ins, outs, sems):
        mine, first, _, passed, from_sibling = plan(ins, outs, sems)
        for cp in from_sibling:
            cp.wait_recv()
        for cp in first + [fwd for onward in passed for fwd in onward]:
            cp.wait_send()
        for cp in mine:
            cp.wait()

    return _Carry(carry_ins, [_sds((N_DEV,) + src.shape, src.dtype) for src, _, _ in items],
                  [pltpu.SemaphoreType.DMA((7, na)), pltpu.SemaphoreType.DMA((7, na)),
                   pltpu.SemaphoreType.DMA((na,))], start, finish, mid, aliases)


def _exchange_carry(scattered, replicated=()):
    items = list(scattered) + [(a, (0, a.shape[0]), None) for a in replicated]
    na, ns = len(items), len(scattered)
    carry_ins, windows, aliases = _split_items(items)

    def plan(ins, outs, sems):
        send_sems, recv_sems, local_sems = sems
        me = _slot(*_position())

        def source(a, j):
            return _row_window(ins[a].at[j] if a < ns else ins[a], windows[a])

        def copy(a, j, i):
            return pltpu.make_async_remote_copy(src_ref=source(a, j), dst_ref=_row_window(outs[a].at[i], windows[a]),
                                                send_sem=send_sems.at[j, a], recv_sem=recv_sems.at[i, a],
                                                device_id=(j >> 2, (j >> 1) & 1, j & 1), device_id_type=MESH)

        def own(a, j):
            return pltpu.make_async_copy(source(a, j), _row_window(outs[a].at[j], windows[a]), local_sems.at[a])

        return me, copy, own

    def start(ins, outs, sems):
        me, copy, own = plan(ins, outs, sems)
        for a in range(na):
            for j in range(N_DEV):
                @pl.when(me == j)
                def _():
                    own(a, j).start()

                @pl.when(me != j)
                def _():
                    copy(a, j, me).start()

    def finish(ins, outs, sems):
        me, copy, own = plan(ins, outs, sems)
        for a in range(na):
            for j in range(N_DEV):
                @pl.when(me == j)
                def _():
                    for i in range(N_DEV):
                        if i != j:
                            copy(a, j, i).wait_recv()
                    own(a, j).wait()

                @pl.when(me != j)
                def _():
                    copy(a, j, me).wait_send()

    return _Carry(carry_ins, [_sds((N_DEV,) + src.shape[-2:], src.dtype) for src, _, _ in items],
                  [pltpu.SemaphoreType.DMA((N_DEV, na)), pltpu.SemaphoreType.DMA((N_DEV, na)),
                   pltpu.SemaphoreType.DMA((na,))], start, finish, None, aliases)


HBM_ARRAY = pl.BlockSpec(memory_space=pltpu.HBM)
SEMAPHORES = pl.BlockSpec(memory_space=pltpu.SEMAPHORE)
DATAFLOW = pltpu.SideEffectType.DATAFLOW_SIDE_EFFECTING


def _exchange_copy(parts_ref, land_ref, send_sems, recv_sems, me, j):
    return pltpu.make_async_remote_copy(src_ref=parts_ref.at[j], dst_ref=land_ref.at[me], send_sem=send_sems.at[j],
                                        recv_sem=recv_sems.at[me], device_id=(j >> 2, (j >> 1) & 1, j & 1),
                                        device_id_type=MESH)


def _exchange_start(parts, name):
    def body(parts_ref, land_ref, send_sems, recv_sems, parts_thru, land_thru, token):
        me = _slot(*_position())
        for j in range(N_DEV):
            @pl.when(me == j)
            def _():
                pltpu.make_async_copy(parts_ref.at[j], land_ref.at[j], send_sems.at[j]).start()

            @pl.when(me != j)
            def _():
                _exchange_copy(parts_ref, land_ref, send_sems, recv_sems, me, j).start()
        token[...] = jnp.zeros_like(token)

    return pl.pallas_call(
        body, name=name,
        out_shape=(pltpu.SemaphoreType.DMA((N_DEV,)), pltpu.SemaphoreType.DMA((N_DEV,)),
                   pltpu.HBM(parts.shape, parts.dtype), pltpu.HBM(parts.shape, parts.dtype), _sds((8, 128), F32)),
        in_specs=(HBM_ARRAY, HBM_ARRAY),
        out_specs=(SEMAPHORES, SEMAPHORES, HBM_ARRAY, HBM_ARRAY, pl.BlockSpec(memory_space=pltpu.VMEM)),
        input_output_aliases={0: 2, 1: 3}, compiler_params=pltpu.CompilerParams(has_side_effects=DATAFLOW),
    )(pltpu.with_memory_space_constraint(parts, pltpu.HBM),
      pltpu.with_memory_space_constraint(lax.empty(parts.shape, parts.dtype), pltpu.HBM))


def _exchange_wait(send_sems, recv_sems, parts_thru, land_thru, after, name):
    def body(parts_ref, land_ref, send_sems, recv_sems, after_ref, parts_dead, got_ref):
        me = _slot(*_position())
        for j in range(N_DEV):
            @pl.when(me == j)
            def _():
                pltpu.make_async_copy(parts_ref.at[j], land_ref.at[j], send_sems.at[j]).wait()

            @pl.when(me != j)
            def _():
                both = pltpu.make_async_remote_copy(src_ref=parts_ref.at[j], dst_ref=land_ref.at[j],
                                                    send_sem=send_sems.at[j], recv_sem=recv_sems.at[j],
                                                    device_id=(j >> 2, (j >> 1) & 1, j & 1), device_id_type=MESH)
                both.wait_send()
                both.wait_recv()

    return pl.pallas_call(
        body, name=name, out_shape=(pltpu.HBM(parts_thru.shape, parts_thru.dtype),
                                    pltpu.HBM(parts_thru.shape, parts_thru.dtype)),
        in_specs=(HBM_ARRAY, HBM_ARRAY, SEMAPHORES, SEMAPHORES, pl.BlockSpec(memory_space=pl.ANY)),
        out_specs=(HBM_ARRAY, HBM_ARRAY), input_output_aliases={0: 0, 1: 1},
        compiler_params=pltpu.CompilerParams(has_side_effects=DATAFLOW),
    )(parts_thru, land_thru, send_sems, recv_sems, after)[1]


class _Mesh:
    def __init__(self, shards):
        self.shards, self.full, self.received, self.cache = shards, {}, {}, {}

    def fetch(self, wanted):
        items = []
        for want in wanted:
            name, r0, r1 = want if isinstance(want, tuple) else (want, 0, self.shards[want].shape[0])
            items.append((self.shards[name], (r0, r1), self.full.get(name)))
        return _gather_carry(items)

    def fetched(self, wanted, results):
        self.full.update(zip([want[0] if isinstance(want, tuple) else want for want in wanted], results))

    def send(self, *payloads):
        return _exchange_carry([(parts, rows or (0, parts.shape[1]), self.received.get(name))
                                for name, parts, rows in payloads])

    def sent(self, names, results):
        self.received.update(zip(names, results))

    def send_apart(self, name, parts):
        *self.pending, token = _exchange_start(parts, "exchange_" + name + "_start")
        self.pending_name = name
        return token

    def sent_apart(self, after):
        self.received[self.pending_name] = _exchange_wait(*self.pending, after, "exchange_" + self.pending_name + "_wait")

    def w(self, key):
        if key not in self.cache:
            self.cache[key] = self._layout(key)
        return self.cache[key]

    def _layout(self, key):
        if key in ("gu1", "gu2"):
            return self.full[key]
        if key in ("d1", "d2"):
            return self.full[key].reshape(4, FS, D)
        if key in ("out", "q", "o"):
            return self.full[key].reshape(D, D)
        if key == "kv":
            return self.full["kv"]
        if key == "convw":
            rows = self.full["conv"][:, :3, :].transpose(1, 0, 2).reshape(3, D)
            return jnp.concatenate([rows, jnp.zeros((5, D), F32)], axis=0)
        assert key == "win_t", key
        return self.full["win"].reshape(-1, D)


def _forward_backward(x, mem, target, g, rel_bias, sinks, ex):
    s = x.shape[0]
    def fetching(wanted, call, *args, **kw):
        res, got = call(*args, carry=ex.fetch(wanted), **kw)
        ex.fetched(wanted, got)
        return res

    h1 = fetching(["gu1", "conv"], _rmsnorm, x, g["ffn1"], "norm_ffn1")
    gu1, a1 = fetching(["d1", ("win", 0, 400)], _ffn_up, h1, ex.w("gu1").reshape(2, 4, FS, D), "ffn1_up")
    x1, h2 = fetching([("win", 400, 832)], _mm_res_norm, a1, ex.w("d1"), x, g["mix"], 0.5, "ffn1_down")
    pa, q, kv = fetching(["gu2", "out", "q"], _in_proj, h2, ex.w("win_t"), "in_proj")
    biasm = _bias_build(rel_bias, "bias_build")
    attn, lse = fetching(["kv", "d2", "o"], _swa_fwd, q, kv, biasm, sinks, "swa_fwd")
    merged = _conv_merge_fwd(pa, attn, ex.w("convw"), "conv_merge_fwd")
    (x2, h3), _ = _mm_res_norm(merged[None], ex.w("out")[None], x1, g["xattn"], 1.0, "out_proj")
    q2 = _mm_nn(h3, ex.w("q")[None], "xattn_q")[0][0]
    mh, _ = _rmsnorm(mem, g["mem"], "norm_mem")
    kv2 = _mm_nn(mh, ex.w("kv"), "xattn_kv")[0]
    o, lse2 = _xattn_fwd(q2, kv2, "xattn_fwd")
    (x3, h4), _ = _mm_res_norm(o[None], ex.w("o")[None], x2, g["ffn2"], 1.0, "xattn_o")
    (gu2, a2), _ = _ffn_up(h4, ex.w("gu2").reshape(2, 4, FS, D), "ffn2_up")
    dx4, dx4b, loss, d_final = _ffn_down_loss(a2, ex.w("d2"), x3, g["final"], target, "ffn2_down_loss")
    def sending(payloads, call, *args, **kw):
        res, got = call(*args, carry=ex.send(*payloads), **kw)
        ex.sent([name for name, _, _ in payloads], got)
        return res

    dw_d2 = _mm_tn(a2, dx4b[None], "dw_ffn2_down", scale=0.5)[0].reshape(N_DEV, -1, D)
    dgu2 = sending([("d2", dw_d2, None)], _ffn_down_bwd, dx4b, ex.w("d2"), gu2, "ffn2_down_bwd").reshape(8, s, FS)
    dw_gu2 = _mm_tn(dgu2, h4[None], "dw_ffn2_up", scale=0.5)[0]
    dx3, dx3b, d_ffn2 = sending([("gu2", dw_gu2, (0, 400))], _mm_acc_rms_bwd, dgu2, ex.w("gu2"), "ffn2_up_bwd",
                                x=x3, gain=g["ffn2"], dres=dx4, scale=0.5)
    do, _ = _mm_acc(dx3b[None], ex.w("o")[None], "xattn_o_bwd", BF16, bt=True)
    dw_o = _mm_tn(o[None], dx3b[None], "dw_xattn_o")[0].reshape(N_DEV, -1, D)
    dq2, dkv2 = sending([("o", dw_o, None)], _xattn_bwd, q2, kv2, o, do, lse2, "xattn_bwd")
    dkv2b = dkv2.astype(BF16)
    dw_q = _mm_tn(h3[None], dq2[None], "dw_xattn_q")[0].reshape(N_DEV, -1, D)
    dx2, dx2b, d_xattn = sending([("q", dw_q, None)], _mm_acc_rms_bwd, dq2[None], ex.w("q")[None],
                                 "xattn_q_bwd", x=x2, gain=g["xattn"], dres=dx3, bt=True)
    dw_kv = _mm_tn(mh[None], dkv2b, "dw_xattn_kv")[0]
    (_, _, d_mem), _ = _mm_acc_rms_bwd(dkv2b, ex.w("kv"), "xattn_kv_bwd", x=mem, gain=g["mem"],
                                       dres=jnp.zeros_like(mem), bt=True)
    dmerged, _ = _mm_acc(dx2b[None], ex.w("out")[None], "out_proj_bwd", BF16, bt=True)
    dw_out = _mm_tn(merged[None], dx2b[None], "dw_out_proj")[0].reshape(N_DEV, -1, D)
    dattn, dpa, d_convw = sending([("kv", dw_kv, None)], _conv_merge_bwd,
                                  dmerged, pa, attn, ex.w("convw"), "conv_merge_bwd")
    dq, dkv, dbias, d_sinks = sending([("gu2", dw_gu2, (400, FS)), ("out", dw_out, None)], _swa_bwd,
                                      q, kv, attn, dattn, lse, biasm, sinks, "swa_bwd")
    d_relb = _bias_bwd(dbias, "bias_bwd")
    w_rows = ex.w("win_t").shape[0]
    dw_in = _mm_tn_rows(dpa, h2, "dw_in_proj_a", w_rows, NQ + NKV)
    dw_in = _mm_tn_rows(dq[None], h2, "dw_in_proj_q", w_rows, 0, begun=dw_in)
    dw_in = _mm_tn_rows(dkv[None], h2, "dw_in_proj_kv", w_rows, NQ, begun=dw_in).reshape(N_DEV, -1, D)
    dx1, dx1b, d_mix = sending([("win", dw_in, (0, 672))], _in_proj_bwd, dpa, dq, dkv, ex.w("win_t"), "in_proj_bwd",
                               x=x1, gain=g["mix"], dres=dx2)
    dw_d1 = sending([("win", dw_in, (672, 832))], _mm_tn, a1, dx1b[None], "dw_ffn1_down", scale=0.5)
    dw_d1 = dw_d1.reshape(N_DEV, -1, D)
    dgu1 = sending([("d1", dw_d1, None)], _ffn_down_bwd, dx1b, ex.w("d1"), gu1, "ffn1_down_bwd").reshape(8, s, FS)
    dw_gu1 = _mm_tn(dgu1, h1[None], "dw_ffn1_up", scale=0.5)[0]
    token = ex.send_apart("gu1", dw_gu1)
    (dx0, _, d_ffn1), _ = _mm_acc_rms_bwd(dgu1, ex.w("gu1"), "ffn1_up_bwd", x=x,
                                          gain=g["ffn1"] + token[0:1, 0:1], dres=dx1, scale=0.5)

    relb_row = jnp.concatenate([d_relb[:, :REL_BUCKETS].T.reshape(1, REL_BUCKETS * N_HEADS), d_sinks[:, :N_HEADS],
                                jnp.zeros((1, D - REL_BUCKETS * N_HEADS - N_HEADS), F32)], axis=1)
    loss_row = jnp.concatenate([loss[0:1, 0:1], jnp.zeros((1, D - 1), F32)], axis=1)
    small = jnp.concatenate([d_ffn1, d_mix, d_xattn, d_mem, d_ffn2, d_final, relb_row, loss_row, d_convw[0:3],
                             jnp.zeros((SMALL_ROWS - ROW_CONV - 3, D), F32)], axis=0)
    return dx0, small


def _pack_small(norms, final, relb, sinks, conv_local, me):
    relb_row = jnp.concatenate([relb.reshape(1, -1), sinks.reshape(1, -1),
                                jnp.zeros((1, D - REL_BUCKETS * N_HEADS - N_HEADS), F32)], axis=1)
    conv_rows = lax.dynamic_update_slice(jnp.zeros((3, D), F32), conv_local.reshape(3, -1), (0, 128 * me))
    return jnp.concatenate(list(norms) + [final.reshape(1, D), relb_row, jnp.zeros((1, D), F32), conv_rows,
                                          jnp.zeros((SMALL_ROWS - ROW_CONV - 3, D), F32)], axis=0)


def kernel(x, mem, positions, rel_bias, ffn1_norm, ffn1_w_gu, ffn1_w_down, mix_norm, w_in, sinks, conv_w, w_out, xattn_norm, mem_norm, xattn_wq, xattn_wkv, xattn_wo, ffn2_norm, ffn2_w_gu, ffn2_w_down, final_norm, loss_target, m_rel_bias, m_ffn1_norm, m_ffn1_w_gu, m_ffn1_w_down, m_mix_norm, m_w_in, m_sinks, m_conv_w, m_w_out, m_xattn_norm, m_mem_norm, m_xattn_wq, m_xattn_wkv, m_xattn_wo, m_ffn2_norm, m_ffn2_w_gu, m_ffn2_w_down, m_final_norm, v_rel_bias, v_ffn1_norm, v_ffn1_w_gu, v_ffn1_w_down, v_mix_norm, v_w_in, v_sinks, v_conv_w, v_w_out, v_xattn_norm, v_mem_norm, v_xattn_wq, v_xattn_wkv, v_xattn_wo, v_ffn2_norm, v_ffn2_w_gu, v_ffn2_w_down, v_final_norm):
    del positions
    me = _slot(*_position())
    big = dict(gu1=(ffn1_w_gu, m_ffn1_w_gu, v_ffn1_w_gu), d1=(ffn1_w_down, m_ffn1_w_down, v_ffn1_w_down),
               win=(w_in, m_w_in, v_w_in), out=(w_out, m_w_out, v_w_out), q=(xattn_wq, m_xattn_wq, v_xattn_wq),
               kv=(xattn_wkv, m_xattn_wkv, v_xattn_wkv), o=(xattn_wo, m_xattn_wo, v_xattn_wo),
               gu2=(ffn2_w_gu, m_ffn2_w_gu, v_ffn2_w_gu), d2=(ffn2_w_down, m_ffn2_w_down, v_ffn2_w_down))
    order = list(big)
    transposed = ("gu1", "gu2", "win")
    local = {k: tuple(t[0].T if k in transposed else t[0] for t in big[k]) for k in order}
    shards = {k: local[k][0].astype(BF16) for k in order}
    shards["conv"] = jnp.concatenate([conv_w[0], jnp.zeros((5, 128), F32)], axis=0)
    ex = _Mesh(shards)
    gains = dict(ffn1=ffn1_norm, mix=mix_norm, xattn=xattn_norm, mem=mem_norm, ffn2=ffn2_norm,
                 final=final_norm.reshape(1, D))
    dx, small = _forward_backward(x[0], mem[0], loss_target[0], gains, rel_bias, sinks, ex)
    big_out = {k: _adamw(ex.received[k], *local[k], "adamw_" + k) for k in order if k != "gu1"}
    small, _ = lax.optimization_barrier((small, [big_out[k][1] for k in big_out]))
    small_parts = _run_alone(_exchange_carry([], [small]), "exchange_small")[0]
    packed = [_pack_small(norms, final, relb, sk, conv, me) for norms, final, relb, sk, conv in (
        ((ffn1_norm, mix_norm, xattn_norm, mem_norm, ffn2_norm), final_norm, rel_bias, sinks, conv_w),
        ((m_ffn1_norm, m_mix_norm, m_xattn_norm, m_mem_norm, m_ffn2_norm), m_final_norm, m_rel_bias, m_sinks, m_conv_w),
        ((v_ffn1_norm, v_mix_norm, v_xattn_norm, v_mem_norm, v_ffn2_norm), v_final_norm, v_rel_bias, v_sinks, v_conv_w))]
    small_out = _adamw(small_parts, *packed, "adamw_small")
    done = [dx[0:1, 0:1], small_out[1][0:1, 0:1]] + [big_out[k][1][0:1, 0:1] for k in big_out]
    ex.sent_apart(after=sum(done))
    big_out["gu1"] = _adamw(ex.received["gu1"], *local["gu1"], "adamw_gu1")
    big_out = {k: [t.T if k in transposed else t for t in big_out[k]] for k in order}

    def unpack(t):
        conv = lax.dynamic_slice(t[ROW_CONV:ROW_CONV + 3], (0, 128 * me), (3, 128))[None]
        nrel = REL_BUCKETS * N_HEADS
        return dict(ffn1_norm=t[0:1], mix_norm=t[1:2], xattn_norm=t[2:3], mem_norm=t[3:4], ffn2_norm=t[4:5],
                    final_norm=t[5], rel_bias=t[ROW_RELB, :nrel].reshape(REL_BUCKETS, N_HEADS),
                    sinks=t[ROW_RELB:ROW_RELB + 1, nrel:nrel + N_HEADS], conv_w=conv)

    names = dict(gu1="ffn1_w_gu", d1="ffn1_w_down", win="w_in", out="w_out", q="xattn_wq", kv="xattn_wkv",
                 o="xattn_wo", gu2="ffn2_w_gu", d2="ffn2_w_down")
    results = []
    for idx in range(4):
        leaves = unpack(small_out[idx])
        leaves.update({names[k]: big_out[k][idx][None] for k in order})
        results.append(leaves)
    weights = ("rel_bias", "ffn1_norm", "ffn1_w_gu", "ffn1_w_down", "mix_norm", "w_in", "sinks", "conv_w", "w_out",
               "xattn_norm", "mem_norm", "xattn_wq", "xattn_wkv", "xattn_wo", "ffn2_norm", "ffn2_w_gu", "ffn2_w_down",
               "final_norm")
    loss = small_out[0][ROW_LOSS, 0]
    return (loss, dx[None], *[leaves[n] for leaves in results for n in weights])
```

```python
import math

import numpy as np
import jax
import jax.numpy as jnp
from jax import lax
from jax.experimental import pallas as pl
from jax.experimental.pallas import tpu as pltpu

F32, BF16 = jnp.float32, jnp.bfloat16
MESH = pl.DeviceIdType.MESH

D = 1024
N_DEV = 8
D_FF = 2816
FS = D_FF // 4
HEAD = 64
N_HEADS, N_KV = 16, 4
BLK = 128
NQ, NKV = N_HEADS * HEAD, 2 * N_KV * HEAD
XH, XHD = 4, 256
REL_BUCKETS, REL_EXACT, REL_MAX_DIST = 32, 16, 128
EPS, NEG = 1e-6, -1e30
ADAM_LR, ADAM_B1, ADAM_B2, ADAM_EPS, ADAM_WD, ADAM_STEP = 0.001, 0.9, 0.999, 1e-08, 0.01, 10
VMEM_LIMIT_V7X = 56 * 2**20
SMALL_ROWS = 16
ROW_RELB, ROW_LOSS, ROW_CONV = 6, 7, 8


def _bucket_thresholds():
    n = np.arange(REL_MAX_DIST)
    nf = np.maximum(n, 1).astype(np.float32)
    large = REL_EXACT + (np.log(nf / np.float32(REL_EXACT)) / np.float32(math.log(REL_MAX_DIST / REL_EXACT))
                         * np.float32(REL_BUCKETS - REL_EXACT)).astype(np.int32)
    b = np.where(n < REL_EXACT, n, np.minimum(large, REL_BUCKETS - 1))
    return [int(np.argmax(b >= REL_EXACT + k)) for k in range(1, REL_BUCKETS - REL_EXACT)]


BUCKET_THRESHOLDS = _bucket_thresholds()


HBM_SPEC = pl.BlockSpec(memory_space=pl.ANY)


class _Carry:
    def __init__(self, ins, outs, sems, start, finish, mid=None, aliases=None):
        self.ins, self.outs, self.sems = list(ins), list(outs), list(sems)
        self.start, self.finish, self.mid, self.aliases = start, finish, mid, dict(aliases or {})


def _pcall(body, *, name, grid, in_specs, out_specs, out_shape, scratch=(), carry=None, aliases=None, behind=None):
    params = pltpu.CompilerParams(dimension_semantics=("arbitrary",) * len(grid), vmem_limit_bytes=VMEM_LIMIT_V7X)
    if carry is None and behind is not None:
        n_in = len(in_specs)
        call = pl.pallas_call(lambda *refs: body(*refs[:n_in], *refs[n_in + 1:]), name=name, grid=grid,
                              in_specs=list(in_specs) + [pl.BlockSpec((8, 128), lambda *_: (0, 0))],
                              out_specs=out_specs, out_shape=out_shape, scratch_shapes=list(scratch),
                              compiler_params=params, input_output_aliases=aliases or {})
        return lambda *args: call(*args, behind)
    if carry is None:
        return pl.pallas_call(body, name=name, grid=grid, in_specs=in_specs, out_specs=out_specs,
                              out_shape=out_shape, scratch_shapes=list(scratch), compiler_params=params,
                              input_output_aliases=aliases or {})
    assert aliases is None and behind is None, name
    single = not isinstance(out_shape, (list, tuple))
    own_specs, own_shapes = ([out_specs], [out_shape]) if single else (list(out_specs), list(out_shape))
    n_in, n_out, n_scr = len(in_specs), len(own_shapes), len(scratch)
    n_cin, n_cout = len(carry.ins), len(carry.outs)
    steps = math.prod(grid)
    mid_step = max(steps - 1 - max(steps // 8, 1), 0)

    def carrying(*refs):
        ins, refs = refs[:n_in], refs[n_in:]
        cins, refs = refs[:n_cin], refs[n_cin:]
        outs, refs = refs[:n_out], refs[n_out:]
        couts, refs = refs[:n_cout], refs[n_cout:]
        scr, csems = refs[:n_scr], refs[n_scr:]
        step = 0
        for axis, size in enumerate(grid):
            step = step * size + pl.program_id(axis)

        @pl.when(step == 0)
        def _():
            carry.start(cins, couts, csems)

        body(*ins, *outs, *scr)
        if carry.mid is not None:
            @pl.when(step == mid_step)
            def _():
                carry.mid(cins, couts, csems)

        @pl.when(step == steps - 1)
        def _():
            carry.finish(cins, couts, csems)

    call = pl.pallas_call(carrying, name=name, grid=grid, in_specs=list(in_specs) + [HBM_SPEC] * n_cin,
                          out_specs=own_specs + [HBM_SPEC] * n_cout, out_shape=own_shapes + carry.outs,
                          scratch_shapes=list(scratch) + carry.sems, compiler_params=params,
                          input_output_aliases={n_in + i: n_out + o for i, o in carry.aliases.items()})

    def run(*args):
        res = call(*args, *carry.ins)
        return (res[0] if single else res[:n_out]), res[n_out:]

    return run


def _run_alone(carry, name):
    n_cin, n_cout = len(carry.ins), len(carry.outs)

    def body(*refs):
        cins, couts, csems = refs[:n_cin], refs[n_cin:n_cin + n_cout], refs[n_cin + n_cout:]
        carry.start(cins, couts, csems)
        if carry.mid is not None:
            carry.mid(cins, couts, csems)
        carry.finish(cins, couts, csems)

    return pl.pallas_call(body, name=name, in_specs=[HBM_SPEC] * n_cin, out_specs=[HBM_SPEC] * n_cout,
                          out_shape=carry.outs, scratch_shapes=carry.sems,
                          input_output_aliases=carry.aliases)(*carry.ins)


def _dot(a, b):
    return jnp.dot(a, b, preferred_element_type=F32)


def _dot_nt(a, b):
    return lax.dot_general(a, b, (((1,), (1,)), ((), ())), preferred_element_type=F32)


def _dot_tn(a, b):
    return lax.dot_general(a, b, (((0,), (0,)), ((), ())), preferred_element_type=F32)


def _sds(shape, dtype):
    return jax.ShapeDtypeStruct(tuple(shape), dtype)


ROW_CHUNK = 256


def _row_chunks(tm):
    return [slice(r, min(r + ROW_CHUNK, tm)) for r in range(0, tm, ROW_CHUNK)]


def _carried(call, args, carry):
    return call(*args) if carry is not None else (call(*args), ())


def _rmsnorm(x, g, name, carry=None):
    m, d = x.shape
    tm = min(512, m)

    def body(x_ref, g_ref, h_ref):
        xv = x_ref[...]
        r = lax.rsqrt(jnp.mean(xv * xv, axis=-1, keepdims=True) + EPS)
        h_ref[...] = (xv * r * g_ref[...]).astype(BF16)

    call = _pcall(body, name=name, grid=(m // tm,), carry=carry,
                  in_specs=[pl.BlockSpec((tm, d), lambda i: (i, 0)), pl.BlockSpec((1, d), lambda i: (0, 0))],
                  out_specs=pl.BlockSpec((tm, d), lambda i: (i, 0)), out_shape=_sds((m, d), BF16))
    return _carried(call, (x, g), carry)


def _mm_nn(a, b, name, tm=1024, bt=False, carry=None):
    m, k = a.shape
    nj = b.shape[0]
    n = b.shape[1] if bt else b.shape[2]
    tm = min(tm, m)
    dot = _dot_nt if bt else _dot

    def body(a_ref, b_ref, o_ref):
        o_ref[...] = dot(a_ref[...], b_ref[...]).astype(BF16)

    call = _pcall(body, name=name, grid=(nj, m // tm),
                  in_specs=[pl.BlockSpec((tm, k), lambda j, i: (i, 0)),
                            pl.BlockSpec((None,) + b.shape[1:], lambda j, i: (j, 0, 0))],
                  out_specs=pl.BlockSpec((None, tm, n), lambda j, i: (j, i, 0)),
                  out_shape=_sds((nj, m, n), BF16), carry=carry)
    return _carried(call, (a, b), carry)


def _load_once(src_hbm, dst_vmem, sem):
    @pl.when(pl.program_id(0) == 0)
    def _():
        load = pltpu.make_async_copy(src_hbm, dst_vmem, sem)
        load.start()
        load.wait()


def _resident(w):
    return [pltpu.VMEM(w.shape, w.dtype), pltpu.SemaphoreType.DMA(())]


def _ffn_up(h, w4, name, tm=512, carry=None):
    s, d = h.shape
    tm = min(tm, s)

    def body(h_ref, w_hbm, gu_ref, a_ref, w_ref, w_sem):
        _load_once(w_hbm, w_ref, w_sem)
        for p in range(4):
            for rows in _row_chunks(tm):
                hv = h_ref[rows, :]
                g = _dot_nt(hv, w_ref[0, p])
                u = _dot_nt(hv, w_ref[1, p])
                gu_ref[0, p, rows, :] = g.astype(BF16)
                gu_ref[1, p, rows, :] = u.astype(BF16)
                a_ref[p, rows, :] = (g * jax.nn.sigmoid(g) * u).astype(BF16)

    call = _pcall(body, name=name, grid=(s // tm,),
                  in_specs=[pl.BlockSpec((tm, d), lambda i: (i, 0)), HBM_SPEC],
                  out_specs=[pl.BlockSpec((2, 4, tm, FS), lambda i: (0, 0, i, 0)),
                             pl.BlockSpec((4, tm, FS), lambda i: (0, i, 0))],
                  out_shape=[_sds((2, 4, s, FS), BF16), _sds((4, s, FS), BF16)], scratch=_resident(w4), carry=carry)
    return _carried(call, (h, w4), carry)


N_SEG = 5
IN_PROJ_WEIGHTS = [pltpu.VMEM((NQ, D), BF16), pltpu.VMEM((NKV, D), BF16), pltpu.VMEM((N_SEG, D, D), BF16),
                   pltpu.SemaphoreType.DMA((2 + N_SEG,))]


def _load_in_proj(w_hbm, wq_ref, wkv_ref, wa_ref, sems):
    @pl.when(pl.program_id(0) == 0)
    def _():
        loads = [pltpu.make_async_copy(w_hbm.at[pl.ds(0, NQ)], wq_ref, sems.at[0]),
                 pltpu.make_async_copy(w_hbm.at[pl.ds(NQ, NKV)], wkv_ref, sems.at[1])]
        loads += [pltpu.make_async_copy(w_hbm.at[pl.ds(NQ + NKV + D * j, D)], wa_ref.at[j], sems.at[2 + j])
                  for j in range(N_SEG)]
        for load in loads:
            load.start()
        for load in loads:
            load.wait()


def _in_proj(h, w_in_t, name, tm=512, carry=None):
    s, d = h.shape
    tm = min(tm, s)

    def body(h_ref, w_hbm, pa_ref, q_ref, kv_ref, wq_ref, wkv_ref, wa_ref, sems):
        _load_in_proj(w_hbm, wq_ref, wkv_ref, wa_ref, sems)
        hv = h_ref[...]
        q_ref[...] = _dot_nt(hv, wq_ref[...]).astype(BF16)
        kv_ref[...] = _dot_nt(hv, wkv_ref[...]).astype(BF16)
        for j in range(N_SEG):
            pa_ref[j] = _dot_nt(hv, wa_ref[j]).astype(BF16)

    call = _pcall(body, name=name, grid=(s // tm,), carry=carry,
                  in_specs=[pl.BlockSpec((tm, d), lambda i: (i, 0)), HBM_SPEC],
                  out_specs=[pl.BlockSpec((N_SEG, tm, d), lambda i: (0, i, 0)),
                             pl.BlockSpec((tm, NQ), lambda i: (i, 0)), pl.BlockSpec((tm, NKV), lambda i: (i, 0))],
                  out_shape=[_sds((N_SEG, s, d), BF16), _sds((s, NQ), BF16), _sds((s, NKV), BF16)],
                  scratch=IN_PROJ_WEIGHTS)
    return _carried(call, (h, w_in_t), carry)


def _mm_res_norm(a, w, xres, gain, scale, name, tm=512, carry=None):
    npart, s, kp = a.shape
    tm = min(tm, s)

    def body(a_ref, w_ref, x_ref, g_ref, xo_ref, h_ref):
        for rows in _row_chunks(tm):
            acc = _dot(a_ref[0, rows, :], w_ref[0])
            for p in range(1, npart):
                acc = acc + _dot(a_ref[p, rows, :], w_ref[p])
            xn = x_ref[rows, :] + scale * acc
            xo_ref[rows, :] = xn
            r = lax.rsqrt(jnp.mean(xn * xn, axis=-1, keepdims=True) + EPS)
            h_ref[rows, :] = (xn * r * g_ref[...]).astype(BF16)

    call = _pcall(body, name=name, grid=(s // tm,),
                  in_specs=[pl.BlockSpec((npart, tm, kp), lambda i: (0, i, 0)),
                            pl.BlockSpec((npart, kp, D), lambda i: (0, 0, 0)),
                            pl.BlockSpec((tm, D), lambda i: (i, 0)),
                            pl.BlockSpec((1, D), lambda i: (0, 0))],
                  out_specs=[pl.BlockSpec((tm, D), lambda i: (i, 0)), pl.BlockSpec((tm, D), lambda i: (i, 0))],
                  out_shape=[_sds((s, D), F32), _sds((s, D), BF16)], carry=carry)
    return _carried(call, (a, w, xres, gain), carry)


def _ffn_down_loss(a, w, xres, gain, target, name, tm=512):
    npart, s, kp = a.shape
    tm = min(tm, s)

    def body(a_ref, w_ref, x_ref, g_ref, t_ref, dx_ref, dxb_ref, loss_ref, dg_ref):
        @pl.when(pl.program_id(0) == 0)
        def _():
            loss_ref[...] = jnp.zeros_like(loss_ref)
            dg_ref[...] = jnp.zeros_like(dg_ref)

        for rows in _row_chunks(tm):
            acc = _dot(a_ref[0, rows, :], w_ref[0])
            for p in range(1, npart):
                acc = acc + _dot(a_ref[p, rows, :], w_ref[p])
            xn = x_ref[rows, :] + 0.5 * acc
            r = lax.rsqrt(jnp.mean(xn * xn, axis=-1, keepdims=True) + EPS)
            xh = xn * r
            gv = g_ref[...]
            err = xh * gv - t_ref[rows, :]
            part = 0.5 * jnp.sum(jnp.mean(err * err, axis=-1, keepdims=True), axis=0, keepdims=True)
            dy = err * (1.0 / D)
            dyg = dy * gv
            dxn = r * (dyg - xh * jnp.mean(dyg * xh, axis=-1, keepdims=True))
            dx_ref[rows, :] = dxn
            dxb_ref[rows, :] = dxn.astype(BF16)
            loss_ref[...] += jnp.broadcast_to(part, loss_ref.shape)
            dg_ref[...] += jnp.sum(dy * xh, axis=0, keepdims=True)

    return _pcall(body, name=name, grid=(s // tm,),
                  in_specs=[pl.BlockSpec((npart, tm, kp), lambda i: (0, i, 0)),
                            pl.BlockSpec((npart, kp, D), lambda i: (0, 0, 0)),
                            pl.BlockSpec((tm, D), lambda i: (i, 0)),
                            pl.BlockSpec((1, D), lambda i: (0, 0)),
                            pl.BlockSpec((tm, D), lambda i: (i, 0))],
                  out_specs=[pl.BlockSpec((tm, D), lambda i: (i, 0)), pl.BlockSpec((tm, D), lambda i: (i, 0)),
                             pl.BlockSpec((8, 128), lambda i: (0, 0)), pl.BlockSpec((1, D), lambda i: (0, 0))],
                  out_shape=[_sds((s, D), F32), _sds((s, D), BF16), _sds((8, 128), F32), _sds((1, D), F32)],
                  )(a, w, xres, gain, target)


def _window_tiles():
    i = lax.broadcasted_iota(jnp.int32, (BLK, BLK), 0)
    j = lax.broadcasted_iota(jnp.int32, (BLK, BLK), 1)
    rel = (i - j) & (BLK - 1)
    large = jnp.full_like(rel, REL_EXACT)
    for t in BUCKET_THRESHOLDS:
        large = large + (rel >= t).astype(jnp.int32)
    return j <= i, jnp.where(rel < REL_EXACT, rel, large)


def _bias_build(rel_bias, name):
    def body(rb_ref, o_ref):
        _, bucket = _window_tiles()

        def per_head(h, carry):
            acc = jnp.zeros((BLK, BLK), F32)
            for b in range(REL_BUCKETS):
                acc = jnp.where(bucket == b, rb_ref[b, h], acc)
            o_ref[h] = acc
            return carry

        lax.fori_loop(0, N_HEADS, per_head, 0)

    return _pcall(body, name=name, grid=(1,),
                  in_specs=[pl.BlockSpec(memory_space=pltpu.SMEM)],
                  out_specs=pl.BlockSpec((N_HEADS, BLK, BLK), lambda i: (0, 0, 0)),
                  out_shape=_sds((N_HEADS, BLK, BLK), F32))(rel_bias)


def _bias_bwd(dbias, name):
    def body(db_ref, o_ref):
        _, bucket = _window_tiles()
        lane = lax.broadcasted_iota(jnp.int32, (N_HEADS, 128), 1)

        def per_bucket(b, out):
            mb = (bucket == b).astype(F32)
            per_col = jnp.sum(db_ref[...] * mb[None, :, :], axis=1)
            return jnp.where(lane == b, jnp.sum(per_col, axis=1, keepdims=True), out)

        o_ref[...] = lax.fori_loop(0, REL_BUCKETS, per_bucket, jnp.zeros((N_HEADS, 128), F32))

    return _pcall(body, name=name, grid=(1,),
                  in_specs=[pl.BlockSpec((N_HEADS, BLK, BLK), lambda i: (0, 0, 0))],
                  out_specs=pl.BlockSpec((N_HEADS, 128), lambda i: (0, 0)),
                  out_shape=_sds((N_HEADS, 128), F32))(dbias)


PAIR = 2 * HEAD
GROUP = N_HEADS // N_KV
SWA_SCALE = HEAD ** -0.5


def _window_masks(n):
    i = lax.broadcasted_iota(jnp.int32, (GROUP * BLK, BLK), 0) & (BLK - 1)
    j = lax.broadcasted_iota(jnp.int32, (GROUP * BLK, BLK), 1)
    return j <= i, jnp.logical_and(n == 0, j > i), j < HEAD


def _kv_twice(ref, base, g, low):
    slab = ref[:, base + PAIR * (g // 2): base + PAIR * (g // 2 + 1)]
    swapped = pltpu.roll(slab, HEAD, 1)
    return jnp.where(low, slab, swapped) if g % 2 == 0 else jnp.where(low, swapped, slab)


def _stack_heads(ref, g, low):
    parts = []
    for r in range(2):
        slab = ref[:, PAIR * (2 * g + r): PAIR * (2 * g + r + 1)]
        zero = jnp.zeros_like(slab)
        parts += [jnp.where(low, slab, zero), jnp.where(low, zero, slab)]
    return jnp.concatenate(parts, axis=0)


def _unstack_heads(t, low):
    return [jnp.where(low, t[2 * r * BLK:(2 * r + 1) * BLK], t[(2 * r + 1) * BLK:(2 * r + 2) * BLK])
            for r in range(2)]


def _head_rows(t, k):
    return t[k * BLK:(k + 1) * BLK]


def _per_head_column(values):
    head = lax.broadcasted_iota(jnp.int32, (GROUP * BLK, 1), 0) // BLK
    col = jnp.full((GROUP * BLK, 1), values[0], F32)
    for k in range(1, GROUP):
        col = jnp.where(head == k, values[k], col)
    return col


def _window_logits(q4, kc, kp, bias4, own, absent):
    sc = jnp.where(own, _dot_nt(q4, kc), _dot_nt(q4, kp)) * SWA_SCALE + bias4
    return jnp.where(absent, NEG, sc)


def _split_window(t, own):
    zero = jnp.zeros_like(t)
    return jnp.where(own, t, zero), jnp.where(own, zero, t)


def _swa_fwd(q, kv, bias, sinks, name, carry=None):
    s = q.shape[0]
    nb = s // BLK
    kvw = 2 * N_KV * HEAD

    def body(q_ref, kc_ref, kp_ref, b_ref, sk_ref, o_ref, lse_ref):
        own, absent, low4 = _window_masks(pl.program_id(0))
        low = low4[:BLK]
        lane = lax.broadcasted_iota(jnp.int32, (BLK, 128), 1)
        lse_t = jnp.zeros((BLK, 128), F32)
        for g in range(N_KV):
            q4 = _stack_heads(q_ref, g, low)
            kc, kp = _kv_twice(kc_ref, 0, g, low), _kv_twice(kp_ref, 0, g, low)
            vc, vp = _kv_twice(kc_ref, N_KV * HEAD, g, low), _kv_twice(kp_ref, N_KV * HEAD, g, low)
            bias4 = b_ref[GROUP * g:GROUP * (g + 1)].reshape(GROUP * BLK, BLK)
            sc = _window_logits(q4, kc, kp, bias4, own, absent)
            sk = _per_head_column([sk_ref[0, GROUP * g + k] for k in range(GROUP)])
            m = jnp.maximum(jnp.max(sc, axis=1, keepdims=True), sk)
            p = jnp.exp(sc - m)
            l = jnp.sum(p, axis=1, keepdims=True) + jnp.exp(sk - m)
            p_own, p_prev = _split_window(p.astype(BF16), own)
            out = (_dot(p_own, vc) + _dot(p_prev, vp)) * (1.0 / l)
            for r, slab in enumerate(_unstack_heads(out, low)):
                o_ref[:, PAIR * (2 * g + r): PAIR * (2 * g + r + 1)] = slab.astype(BF16)
            lse4 = m + jnp.log(l)
            for k in range(GROUP):
                lse_t = jnp.where(lane == GROUP * g + k, _head_rows(lse4, k), lse_t)
        lse_ref[...] = lse_t

    call = _pcall(body, name=name, grid=(nb,),
                  in_specs=[pl.BlockSpec((BLK, D), lambda n: (n, 0)),
                            pl.BlockSpec((BLK, kvw), lambda n: (n, 0)),
                            pl.BlockSpec((BLK, kvw), lambda n: (jnp.maximum(n - 1, 0), 0)),
                            pl.BlockSpec((N_HEADS, BLK, BLK), lambda n: (0, 0, 0)),
                            pl.BlockSpec(memory_space=pltpu.SMEM)],
                  out_specs=[pl.BlockSpec((BLK, D), lambda n: (n, 0)), pl.BlockSpec((BLK, 128), lambda n: (n, 0))],
                  out_shape=[_sds((s, D), BF16), _sds((s, 128), F32)], carry=carry)
    return _carried(call, (q, kv, kv, bias, sinks), carry)


def _fold_halves(t, g, low):
    folded = jnp.where(low, t, 0.0) + pltpu.roll(jnp.where(low, 0.0, t), HEAD, 1)
    return folded if g % 2 == 0 else pltpu.roll(folded, HEAD, 1)


def _swa_bwd(q, kv, attn, dattn, lse, bias, sinks, name, carry=None):
    s = q.shape[0]
    nb = s // BLK
    kvw = 2 * N_KV * HEAD
    voff = N_KV * HEAD

    def body(q_ref, kc_ref, kp_ref, o_ref, do_ref, lse_ref, b_ref, skrow_ref, dq_ref, dkv_ref, dbias_ref, dsk_ref,
             dq_hold, kv_hold, dq_new, kv_prev, kv_cur):
        n = pl.program_id(0)

        @pl.when(n == 0)
        def _():
            dbias_ref[...] = jnp.zeros_like(dbias_ref)
            dsk_ref[...] = jnp.zeros_like(dsk_ref)
            dq_hold[...] = jnp.zeros_like(dq_hold)
            kv_hold[...] = jnp.zeros_like(kv_hold)

        @pl.when(n < nb)
        def _():
            own, absent, low4 = _window_masks(n)
            low = low4[:BLK]
            lane = lax.broadcasted_iota(jnp.int32, (BLK, 128), 1)
            delta_t = jnp.zeros((BLK, 128), F32)
            ones = jnp.ones((PAIR, 128), BF16)
            for pair_of_kv in range(N_KV // 2):
                slab_grads = [jnp.zeros((BLK, PAIR), F32) for _ in range(4)]
                for g in (2 * pair_of_kv, 2 * pair_of_kv + 1):
                    q4, do4 = _stack_heads(q_ref, g, low), _stack_heads(do_ref, g, low)
                    kc, kp = _kv_twice(kc_ref, 0, g, low), _kv_twice(kp_ref, 0, g, low)
                    vc, vp = _kv_twice(kc_ref, voff, g, low), _kv_twice(kp_ref, voff, g, low)
                    o_slabs = [o_ref[:, PAIR * (2 * g + r): PAIR * (2 * g + r + 1)] for r in range(2)]
                    o4 = jnp.concatenate([o_slabs[0], o_slabs[0], o_slabs[1], o_slabs[1]], axis=0)
                    delta = _dot(do4 * o4, ones)
                    heads = range(GROUP * g, GROUP * (g + 1))
                    lse4 = jnp.concatenate([lse_ref[:, h:h + 1] for h in heads], axis=0)
                    bias4 = b_ref[GROUP * g:GROUP * (g + 1)].reshape(GROUP * BLK, BLK)
                    p = jnp.exp(_window_logits(q4, kc, kp, bias4, own, absent) - lse4)
                    dp = jnp.where(own, _dot_nt(do4, vc), _dot_nt(do4, vp))
                    ds = p * (dp - delta)
                    dbias_ref[GROUP * g:GROUP * (g + 1)] += ds.reshape(GROUP, BLK, BLK)
                    for k, h in enumerate(heads):
                        delta_t = jnp.where(lane == h, _head_rows(delta, k), delta_t)
                    ds_own, ds_prev = _split_window((ds * SWA_SCALE).astype(BF16), own)
                    p_own, p_prev = _split_window(p.astype(BF16), own)
                    dq4 = _dot(ds_own, kc) + _dot(ds_prev, kp)
                    for r, slab in enumerate(_unstack_heads(dq4, low)):
                        dq_new[:, PAIR * (2 * g + r): PAIR * (2 * g + r + 1)] = slab
                    grads = [_dot_tn(ds_own, q4), _dot_tn(ds_prev, q4), _dot_tn(p_own, do4), _dot_tn(p_prev, do4)]
                    slab_grads = [t + _fold_halves(dk, g, low) for t, dk in zip(slab_grads, grads)]
                ks = slice(PAIR * pair_of_kv, PAIR * (pair_of_kv + 1))
                vs = slice(voff + PAIR * pair_of_kv, voff + PAIR * (pair_of_kv + 1))
                kv_cur[:, ks], kv_prev[:, ks], kv_cur[:, vs], kv_prev[:, vs] = slab_grads
            dsk_ref[...] -= jnp.sum(jnp.exp(skrow_ref[...] - lse_ref[...]) * delta_t, axis=0, keepdims=True)

        @pl.when(n == nb)
        def _():
            kv_prev[...] = jnp.zeros_like(kv_prev)

        dq_ref[...] = dq_hold[...].astype(BF16)
        dkv_ref[...] = (kv_hold[...] + kv_prev[...]).astype(BF16)

        @pl.when(n < nb)
        def _():
            dq_hold[...] = dq_new[...]
            kv_hold[...] = kv_cur[...]

    def cur(n):
        return jnp.minimum(n, nb - 1)

    call = _pcall(body, name=name, grid=(nb + 1,), carry=carry,
                  in_specs=[pl.BlockSpec((BLK, D), lambda n: (cur(n), 0)),
                            pl.BlockSpec((BLK, kvw), lambda n: (cur(n), 0)),
                            pl.BlockSpec((BLK, kvw), lambda n: (jnp.maximum(cur(n) - 1, 0), 0)),
                            pl.BlockSpec((BLK, D), lambda n: (cur(n), 0)),
                            pl.BlockSpec((BLK, D), lambda n: (cur(n), 0)),
                            pl.BlockSpec((BLK, 128), lambda n: (cur(n), 0)),
                            pl.BlockSpec((N_HEADS, BLK, BLK), lambda n: (0, 0, 0)),
                            pl.BlockSpec((1, 128), lambda n: (0, 0))],
                  out_specs=[pl.BlockSpec((BLK, D), lambda n: (jnp.maximum(n - 1, 0), 0)),
                             pl.BlockSpec((BLK, kvw), lambda n: (jnp.maximum(n - 1, 0), 0)),
                             pl.BlockSpec((N_HEADS, BLK, BLK), lambda n: (0, 0, 0)),
                             pl.BlockSpec((1, 128), lambda n: (0, 0))],
                  out_shape=[_sds((s, D), BF16), _sds((s, kvw), BF16), _sds((N_HEADS, BLK, BLK), F32),
                             _sds((1, 128), F32)],
                  scratch=[pltpu.VMEM((BLK, D), F32), pltpu.VMEM((BLK, kvw), F32), pltpu.VMEM((BLK, D), F32),
                           pltpu.VMEM((BLK, kvw), F32), pltpu.VMEM((BLK, kvw), F32)])
    sink_row = jnp.pad(sinks, ((0, 0), (0, 128 - N_HEADS)))
    return _carried(call, (q, kv, kv, attn, dattn, lse, bias, sink_row), carry)


HALO = 16
CW = D


def _conv_taps(cu, halo_cu, first_tile):
    row = lax.broadcasted_iota(jnp.int32, cu.shape, 0)
    halo_cu = jnp.where(first_tile, 0.0, halo_cu)
    c1 = jnp.where(row == 0, halo_cu[HALO - 1:HALO], pltpu.roll(cu, 1, 0))
    c2 = jnp.where(row == 0, halo_cu[HALO - 2:HALO - 1],
                   jnp.where(row == 1, halo_cu[HALO - 1:HALO], pltpu.roll(cu, 2, 0)))
    return c1, c2


def _conv_merge_fwd(pa, attn, convw, name, ts=256, carry=None):
    _, s, _ = pa.shape
    ts = min(ts, s)
    hb = ts // HALO

    def body(pa_ref, hp_ref, at_ref, w_ref, o_ref):
        i = pl.program_id(1)
        cu = pa_ref[0].astype(F32) * pa_ref[2].astype(F32)
        c1, c2 = _conv_taps(cu, hp_ref[0].astype(F32) * hp_ref[2].astype(F32), i == 0)
        w = w_ref[...]
        c3 = w[0:1] * c2 + w[1:2] * c1 + w[2:3] * cu
        conv = pa_ref[1].astype(F32) * c3
        o_ref[...] = (jax.nn.sigmoid(pa_ref[3].astype(F32)) * at_ref[...].astype(F32)
                      + jax.nn.sigmoid(pa_ref[4].astype(F32)) * conv).astype(BF16)

    call = _pcall(body, name=name, grid=(D // CW, s // ts), carry=carry,
                  in_specs=[pl.BlockSpec((5, ts, CW), lambda c, i: (0, i, c)),
                            pl.BlockSpec((5, HALO, CW), lambda c, i: (0, jnp.maximum(i * hb - 1, 0), c)),
                            pl.BlockSpec((ts, CW), lambda c, i: (i, c)),
                            pl.BlockSpec((8, CW), lambda c, i: (0, c))],
                  out_specs=pl.BlockSpec((ts, CW), lambda c, i: (i, c)),
                  out_shape=_sds((s, D), BF16))
    return _carried(call, (pa, pa, attn, convw), carry)


def _conv_merge_bwd(dmerged, pa, attn, convw, name, ts=256, carry=None):
    _, s, _ = pa.shape
    ts = min(ts, s)
    hb = ts // HALO
    last_hb = s // HALO - 1

    def body(dm_ref, pa_ref, at_ref, w_ref, hp_ref, hn_ref, dmn_ref, dat_ref, dpa_ref, dw_ref):
        i = pl.program_id(1)
        last = i == pl.num_programs(1) - 1
        dm = dm_ref[...].astype(F32)
        cp, bp, u = pa_ref[0].astype(F32), pa_ref[1].astype(F32), pa_ref[2].astype(F32)
        sa = jax.nn.sigmoid(pa_ref[3].astype(F32))
        sc = jax.nn.sigmoid(pa_ref[4].astype(F32))
        at = at_ref[...].astype(F32)
        cu = cp * u
        c1, c2 = _conv_taps(cu, hp_ref[0].astype(F32) * hp_ref[2].astype(F32), i == 0)
        w = w_ref[...]
        c3 = w[0:1] * c2 + w[1:2] * c1 + w[2:3] * cu
        dconv = dm * sc
        dc3 = dconv * bp
        nxt = dmn_ref[...].astype(F32) * jax.nn.sigmoid(hn_ref[4].astype(F32)) * hn_ref[1].astype(F32)
        nxt = jnp.where(last, 0.0, nxt)
        row = lax.broadcasted_iota(jnp.int32, dc3.shape, 0)
        d1 = jnp.where(row == ts - 1, nxt[0:1], pltpu.roll(dc3, ts - 1, 0))
        d2 = jnp.where(row == ts - 2, nxt[0:1], jnp.where(row == ts - 1, nxt[1:2], pltpu.roll(dc3, ts - 2, 0)))
        dcu = w[2:3] * dc3 + w[1:2] * d1 + w[0:1] * d2
        dat_ref[...] = (dm * sa).astype(BF16)
        dpa_ref[0] = (dcu * u).astype(BF16)
        dpa_ref[1] = (dconv * c3).astype(BF16)
        dpa_ref[2] = (dcu * cp).astype(BF16)
        dpa_ref[3] = (dm * at * sa * (1.0 - sa)).astype(BF16)
        dpa_ref[4] = (dm * bp * c3 * sc * (1.0 - sc)).astype(BF16)

        @pl.when(i == 0)
        def _():
            dw_ref[...] = jnp.zeros_like(dw_ref)

        dw_ref[0:1, :] += jnp.sum(dc3 * c2, axis=0, keepdims=True)
        dw_ref[1:2, :] += jnp.sum(dc3 * c1, axis=0, keepdims=True)
        dw_ref[2:3, :] += jnp.sum(dc3 * cu, axis=0, keepdims=True)

    call = _pcall(body, name=name, grid=(D // CW, s // ts), carry=carry,
                  in_specs=[pl.BlockSpec((ts, CW), lambda c, i: (i, c)),
                            pl.BlockSpec((5, ts, CW), lambda c, i: (0, i, c)),
                            pl.BlockSpec((ts, CW), lambda c, i: (i, c)),
                            pl.BlockSpec((8, CW), lambda c, i: (0, c)),
                            pl.BlockSpec((5, HALO, CW), lambda c, i: (0, jnp.maximum(i * hb - 1, 0), c)),
                            pl.BlockSpec((5, HALO, CW), lambda c, i: (0, jnp.minimum((i + 1) * hb, last_hb), c)),
                            pl.BlockSpec((HALO, CW), lambda c, i: (jnp.minimum((i + 1) * hb, last_hb), c))],
                  out_specs=[pl.BlockSpec((ts, CW), lambda c, i: (i, c)),
                             pl.BlockSpec((5, ts, CW), lambda c, i: (0, i, c)),
                             pl.BlockSpec((8, CW), lambda c, i: (0, c))],
                  out_shape=[_sds((s, D), BF16), _sds((5, s, D), BF16), _sds((8, D), F32)])
    return _carried(call, (dmerged, pa, attn, convw, pa, pa, dmerged), carry)


def _xattn_fwd(q, kv, name, tq=512):
    s, _ = q.shape
    nm = kv.shape[1]
    tq = min(tq, s)

    def body(q_ref, kv_ref, o_ref, lse_ref):
        lane = lax.broadcasted_iota(jnp.int32, (tq, 128), 1)
        lse_t = jnp.zeros((tq, 128), F32)
        for h in range(XH):
            hs = slice(XHD * h, XHD * (h + 1))
            sc = _dot_nt(q_ref[:, hs], kv_ref[h]) * (XHD ** -0.5)
            m = jnp.max(sc, axis=1, keepdims=True)
            p = jnp.exp(sc - m)
            l = jnp.sum(p, axis=1, keepdims=True)
            o_ref[:, hs] = (_dot(p.astype(BF16), kv_ref[XH + h]) * (1.0 / l)).astype(BF16)
            lse_t = jnp.where(lane == h, m + jnp.log(l), lse_t)
        lse_ref[...] = lse_t

    return _pcall(body, name=name, grid=(s // tq,),
                  in_specs=[pl.BlockSpec((tq, D), lambda i: (i, 0)), pl.BlockSpec((2 * XH, nm, XHD), lambda i: (0, 0, 0))],
                  out_specs=[pl.BlockSpec((tq, D), lambda i: (i, 0)), pl.BlockSpec((tq, 128), lambda i: (i, 0))],
                  out_shape=[_sds((s, D), BF16), _sds((s, 128), F32)])(q, kv)


def _xattn_bwd(q, kv, o, do, lse, name, tq=512, carry=None):
    s, _ = q.shape
    nm = kv.shape[1]
    tq = min(tq, s)

    def body(q_ref, kv_ref, o_ref, do_ref, lse_ref, dq_ref, dkv_ref):
        @pl.when(pl.program_id(0) == 0)
        def _():
            dkv_ref[...] = jnp.zeros_like(dkv_ref)

        for h in range(XH):
            hs = slice(XHD * h, XHD * (h + 1))
            qh, kh, vh, dob = q_ref[:, hs], kv_ref[h], kv_ref[XH + h], do_ref[:, hs]
            p = jnp.exp(_dot_nt(qh, kh) * (XHD ** -0.5) - lse_ref[:, h:h + 1])
            dp = _dot_nt(dob, vh)
            delta = jnp.sum(dob.astype(F32) * o_ref[:, hs].astype(F32), axis=1, keepdims=True)
            dsb = (p * (dp - delta) * (XHD ** -0.5)).astype(BF16)
            dq_ref[:, hs] = _dot(dsb, kh).astype(BF16)
            dkv_ref[h] += _dot_tn(dsb, qh)
            dkv_ref[XH + h] += _dot_tn(p.astype(BF16), dob)

    call = _pcall(body, name=name, grid=(s // tq,), carry=carry,
                  in_specs=[pl.BlockSpec((tq, D), lambda i: (i, 0)), pl.BlockSpec((2 * XH, nm, XHD), lambda i: (0, 0, 0)),
                            pl.BlockSpec((tq, D), lambda i: (i, 0)), pl.BlockSpec((tq, D), lambda i: (i, 0)),
                            pl.BlockSpec((tq, 128), lambda i: (i, 0))],
                  out_specs=[pl.BlockSpec((tq, D), lambda i: (i, 0)), pl.BlockSpec((2 * XH, nm, XHD), lambda i: (0, 0, 0))],
                  out_shape=[_sds((s, D), BF16), _sds((2 * XH, nm, XHD), F32)])
    return _carried(call, (q, kv, o, do, lse), carry)


def _ffn_down_bwd(dxb, wd4, gu4, name, tm=512, carry=None, behind=None):
    s, _ = dxb.shape
    tm = min(tm, s)

    def body(dx_ref, w_hbm, gu_ref, o_ref, w_ref, w_sem):
        _load_once(w_hbm, w_ref, w_sem)
        for p in range(4):
            for rows in _row_chunks(tm):
                da = _dot_nt(dx_ref[rows, :], w_ref[p])
                g = gu_ref[0, p, rows, :].astype(F32)
                u = gu_ref[1, p, rows, :].astype(F32)
                sg = jax.nn.sigmoid(g)
                t = da * sg
                o_ref[0, p, rows, :] = (t * u * (1.0 + g - g * sg)).astype(BF16)
                o_ref[1, p, rows, :] = (t * g).astype(BF16)

    block = pl.BlockSpec((2, 4, tm, FS), lambda i: (0, 0, i, 0))
    call = _pcall(body, name=name, grid=(s // tm,), carry=carry, behind=behind,
                  in_specs=[pl.BlockSpec((tm, D), lambda i: (i, 0)), HBM_SPEC, block],
                  out_specs=block, out_shape=_sds((2, 4, s, FS), BF16), scratch=_resident(wd4))
    return _carried(call, (dxb, wd4, gu4), carry)


def _mm_tn(a, b, name, scale=1.0, carry=None):
    pa_n, s, m = a.shape
    pb_n, _, n = b.shape
    po = max(pa_n, pb_n)
    tn = n if po >= 4 else min(n, 256)

    def body(a_ref, b_ref, o_ref):
        o_ref[...] = (scale * _dot_tn(a_ref[...], b_ref[...])).astype(BF16)

    call = _pcall(body, name=name, grid=(po, n // tn), carry=carry,
                  in_specs=[pl.BlockSpec((None, s, m), lambda o, j: (o if pa_n > 1 else 0, 0, 0)),
                            pl.BlockSpec((None, s, tn), lambda o, j: (o if pb_n > 1 else 0, 0, j))],
                  out_specs=pl.BlockSpec((None, m, tn), lambda o, j: (o, 0, j)),
                  out_shape=_sds((po, m, n), BF16))
    return _carried(call, (a, b), carry)


def _mm_tn_rows(a, b, name, total_rows, row0, begun=None, tm=512):
    p, s, m = a.shape
    n = b.shape[1]
    tm = min(tm, m)
    tiles = m // tm
    assert row0 % tm == 0 and m % tm == 0, (row0, m, tm)

    def body(a_ref, b_ref, *rest):
        rest[-1][...] = _dot_tn(a_ref[...], b_ref[...]).astype(BF16)

    in_specs = [pl.BlockSpec((None, s, tm), lambda o, i: (o, 0, i)), pl.BlockSpec((s, n), lambda o, i: (0, 0))]
    call = _pcall(body, name=name, grid=(p, tiles), in_specs=in_specs + ([HBM_SPEC] if begun is not None else []),
                  out_specs=pl.BlockSpec((tm, n), lambda o, i: (row0 // tm + o * tiles + i, 0)),
                  out_shape=_sds((total_rows, n), BF16), aliases={2: 0} if begun is not None else None)
    return call(a, b, begun) if begun is not None else call(a, b)


def _sum_dots(a_ref, b_ref, nj, bt, rows=slice(None)):
    dot = _dot_nt if bt else _dot
    acc = dot(a_ref[0, rows, :], b_ref[0])
    for j in range(1, nj):
        acc = acc + dot(a_ref[j, rows, :], b_ref[j])
    return acc


def _mm_acc(a, b, name, out_dtype, tm=512, bt=False, carry=None):
    nj, s, k = a.shape
    n = b.shape[1] if bt else b.shape[2]
    tm = min(tm, s)

    def body(a_ref, b_ref, o_ref):
        o_ref[...] = _sum_dots(a_ref, b_ref, nj, bt).astype(out_dtype)

    call = _pcall(body, name=name, grid=(s // tm,), carry=carry,
                  in_specs=[pl.BlockSpec((nj, tm, k), lambda i: (0, i, 0)),
                            pl.BlockSpec(b.shape, lambda i: (0, 0, 0))],
                  out_specs=pl.BlockSpec((tm, n), lambda i: (i, 0)), out_shape=_sds((s, n), out_dtype))
    return _carried(call, (a, b), carry)


def _rms_bwd_call(name, acts, weights, scratch, load, dh_rows, *, x, gain, dres, tm, carry, behind=None):
    s, n = x.shape
    tm = min(tm, s)
    n_act, n_w = len(acts), len(weights)

    def body(*refs):
        act_refs, w_refs = refs[:n_act], refs[n_act:n_act + n_w]
        x_ref, g_ref, r_ref, dx_ref, dxb_ref, dg_ref = refs[n_act + n_w:n_act + n_w + 6]
        held = refs[n_act + n_w + 6:]
        load(w_refs, held)

        @pl.when(pl.program_id(0) == 0)
        def _():
            dg_ref[...] = jnp.zeros_like(dg_ref)

        for rows in _row_chunks(tm):
            dh = dh_rows(act_refs, held, rows)
            xv = x_ref[rows, :]
            r = lax.rsqrt(jnp.mean(xv * xv, axis=-1, keepdims=True) + EPS)
            xh = xv * r
            dyg = dh * g_ref[...]
            dx = r_ref[rows, :] + r * (dyg - xh * jnp.mean(dyg * xh, axis=-1, keepdims=True))
            dx_ref[rows, :] = dx
            dxb_ref[rows, :] = dx.astype(BF16)
            dg_ref[...] += jnp.sum(dh * xh, axis=0, keepdims=True)

    def tile(a):
        return (pl.BlockSpec((tm, a.shape[1]), lambda i: (i, 0)) if a.ndim == 2
                else pl.BlockSpec((a.shape[0], tm, a.shape[2]), lambda i: (0, i, 0)))

    row = pl.BlockSpec((tm, n), lambda i: (i, 0))
    in_specs = [tile(a) for a in acts] + [HBM_SPEC] * n_w + [row, pl.BlockSpec((1, n), lambda i: (0, 0)), row]
    call = _pcall(body, name=name, grid=(s // tm,), in_specs=in_specs, carry=carry, behind=behind,
                  out_specs=[row, row, pl.BlockSpec((1, n), lambda i: (0, 0))],
                  out_shape=[_sds((s, n), F32), _sds((s, n), BF16), _sds((1, n), F32)], scratch=scratch)
    return _carried(call, tuple(acts) + tuple(weights) + (x, gain, dres), carry)


def _mm_acc_rms_bwd(a, b, name, *, x, gain, dres, scale=None, tm=512, bt=False, carry=None, behind=None):
    def load(w_refs, held):
        _load_once(w_refs[0], held[0], held[1])

    def dh_rows(act_refs, held, rows):
        dh = _sum_dots(act_refs[0], held[0], a.shape[0], bt, rows)
        return dh if scale is None else scale * dh

    return _rms_bwd_call(name, [a], [b], _resident(b), load, dh_rows, x=x, gain=gain, dres=dres, tm=tm, carry=carry,
                         behind=behind)


def _in_proj_bwd(dpa, dq, dkv, w_in_t, name, *, x, gain, dres, tm=512, carry=None, behind=None):
    def load(w_refs, held):
        _load_in_proj(w_refs[0], *held)

    def dh_rows(act_refs, held, rows):
        dpa_ref, dq_ref, dkv_ref = act_refs
        wq_ref, wkv_ref, wa_ref, _ = held
        dh = _dot(dq_ref[rows, :], wq_ref[...]) + _dot(dkv_ref[rows, :], wkv_ref[...])
        return dh + _sum_dots(dpa_ref, wa_ref, N_SEG, False, rows)

    return _rms_bwd_call(name, [dpa, dq, dkv], [w_in_t], IN_PROJ_WEIGHTS, load, dh_rows, x=x, gain=gain, dres=dres,
                         tm=tm, carry=carry, behind=behind)


def _adam(w, g, m, v):
    m2 = ADAM_B1 * m + (1.0 - ADAM_B1) * g
    v2 = ADAM_B2 * v + (1.0 - ADAM_B2) * (g * g)
    m_hat = m2 / (1.0 - ADAM_B1 ** ADAM_STEP)
    v_hat = v2 / (1.0 - ADAM_B2 ** ADAM_STEP)
    delta = -ADAM_LR * (m_hat / (jnp.sqrt(v_hat) + ADAM_EPS) + ADAM_WD * w)
    return delta, m2, v2


def _adamw(parts, w, m, v, name, behind=None):
    _, r, c = parts.shape
    tr = max(t for t in range(16, 257, 16) if r % t == 0)

    def body(p_ref, w_ref, m_ref, v_ref, g_ref, d_ref, m2_ref, v2_ref):
        g = p_ref[0].astype(F32)
        for i in range(1, N_DEV):
            g = g + p_ref[i].astype(F32)
        delta, m2, v2 = _adam(w_ref[...], g, m_ref[...], v_ref[...])
        g_ref[...] = g
        d_ref[...] = delta
        m2_ref[...] = m2
        v2_ref[...] = v2

    blk = pl.BlockSpec((tr, c), lambda i: (i, 0))
    return _pcall(body, name=name, grid=(r // tr,), behind=behind,
                  in_specs=[pl.BlockSpec((N_DEV, tr, c), lambda i: (0, i, 0)), blk, blk, blk],
                  out_specs=[blk] * 4, out_shape=[_sds((r, c), F32)] * 4)(parts, w, m, v)


def _position():
    return lax.axis_index("x"), lax.axis_index("y"), lax.axis_index("c")


def _slot(px, py, pc):
    return 4 * px + 2 * py + pc


def _row_window(ref, rows):
    r0, r1 = rows
    return ref if (r0, r1) == (0, ref.shape[0]) else ref.at[pl.ds(r0, r1 - r0)]


def _split_items(items):
    sources = [src for src, _, _ in items]
    begun = [(a, dest) for a, (_, _, dest) in enumerate(items) if dest is not None]
    aliases = {len(sources) + k: a for k, (a, _) in enumerate(begun)}
    return sources + [dest for _, dest in begun], [rows for _, rows, _ in items], aliases


def _gather_carry(items):
    na = len(items)
    carry_ins, windows, aliases = _split_items(items)

    def plan(ins, outs, sems):
        send_sems, recv_sems, local_sems = sems
        x, y, c = _position()
        me, sibling = (x, y, c), (x, y, 1 - c)
        chips = [(1 - x, y), (x, 1 - y), (1 - x, 1 - y)]
        ins = [_row_window(ins[a], windows[a]) for a in range(na)]

        def block_rows(a, block):
            return _row_window(outs[a].at[_slot(*block)], windows[a])

        def copy(a, k, block, to, src=None):
            rows = block_rows(a, block)
            return pltpu.make_async_remote_copy(src_ref=rows if src is None else src, dst_ref=rows,
                                                send_sem=send_sems.at[k, a], recv_sem=recv_sems.at[k, a],
                                                device_id=to, device_id_type=MESH)

        mine = [pltpu.make_async_copy(ins[a], block_rows(a, me), local_sems.at[a]) for a in range(na)]
        first = [copy(a, 0, me, sibling, src=ins[a]) for a in range(na)]
        for j, chip in enumerate(chips):
            first += [copy(a, 1 + j, me, (*chip, c), src=ins[a]) for a in range(na)]
        landed = [[copy(a, 1 + j, (*chip, c), me) for a in range(na)] for j, chip in enumerate(chips)]
        passed = [[copy(a, 4 + j, (*chip, c), sibling) for a in range(na)] for j, chip in enumerate(chips)]
        from_sibling = [copy(a, 0, sibling, me) for a in range(na)]
        for j, chip in enumerate(chips):
            from_sibling += [copy(a, 4 + j, (*chip, 1 - c), me) for a in range(na)]
        return mine, first, landed, passed, from_sibling

    def start(ins, outs, sems):
        mine, first, _, _, _ = plan(ins, outs, sems)
        for cp in mine + first:
            cp.start()

    def mid(ins, outs, sems):
        _, _, landed, passed, _ = plan(ins, outs, sems)
        for over_ici, onward in zip(landed, passed):
            for cp, fwd in zip(over_ici, onward):
                cp.wait_recv()
                fwd.start()

    def finish(ins, outs, sems):
        mine, first, _, passed, from_sibling = plan(ins, outs, sems)
        for cp in from_sibling:
            cp.wait_recv()
        for cp in first + [fwd for onward in passed for fwd in onward]:
            cp.wait_send()
        for cp in mine:
            cp.wait()

    return _Carry(carry_ins, [_sds((N_DEV,) + src.shape, src.dtype) for src, _, _ in items],
                  [pltpu.SemaphoreType.DMA((7, na)), pltpu.SemaphoreType.DMA((7, na)),
                   pltpu.SemaphoreType.DMA((na,))], start, finish, mid, aliases)


def _exchange_carry(scattered, replicated=()):
    items = list(scattered) + [(a, (0, a.shape[0]), None) for a in replicated]
    na, ns = len(items), len(scattered)
    carry_ins, windows, aliases = _split_items(items)

    def plan(ins, outs, sems):
        send_sems, recv_sems, local_sems = sems
        me = _slot(*_position())

        def source(a, j):
            return _row_window(ins[a].at[j] if a < ns else ins[a], windows[a])

        def copy(a, j, i):
            return pltpu.make_async_remote_copy(src_ref=source(a, j), dst_ref=_row_window(outs[a].at[i], windows[a]),
                                                send_sem=send_sems.at[j, a], recv_sem=recv_sems.at[i, a],
                                                device_id=(j >> 2, (j >> 1) & 1, j & 1), device_id_type=MESH)

        def own(a, j):
            return pltpu.make_async_copy(source(a, j), _row_window(outs[a].at[j], windows[a]), local_sems.at[a])

        return me, copy, own

    def start(ins, outs, sems):
        me, copy, own = plan(ins, outs, sems)
        for a in range(na):
            for j in range(N_DEV):
                @pl.when(me == j)
                def _():
                    own(a, j).start()

                @pl.when(me != j)
                def _():
                    copy(a, j, me).start()

    def finish(ins, outs, sems):
        me, copy, own = plan(ins, outs, sems)
        for a in range(na):
            for j in range(N_DEV):
                @pl.when(me == j)
                def _():
                    for i in range(N_DEV):
                        if i != j:
                            copy(a, j, i).wait_recv()
                    own(a, j).wait()

                @pl.when(me != j)
                def _():
                    copy(a, j, me).wait_send()

    return _Carry(carry_ins, [_sds((N_DEV,) + src.shape[-2:], src.dtype) for src, _, _ in items],
                  [pltpu.SemaphoreType.DMA((N_DEV, na)), pltpu.SemaphoreType.DMA((N_DEV, na)),
                   pltpu.SemaphoreType.DMA((na,))], start, finish, None, aliases)


HBM_ARRAY = pl.BlockSpec(memory_space=pltpu.HBM)
SEMAPHORES = pl.BlockSpec(memory_space=pltpu.SEMAPHORE)
DATAFLOW = pltpu.SideEffectType.DATAFLOW_SIDE_EFFECTING


def _exchange_copy(parts_ref, land_ref, send_sems, recv_sems, me, j):
    return pltpu.make_async_remote_copy(src_ref=parts_ref.at[j], dst_ref=land_ref.at[me], send_sem=send_sems.at[j],
                                        recv_sem=recv_sems.at[me], device_id=(j >> 2, (j >> 1) & 1, j & 1),
                                        device_id_type=MESH)


def _exchange_start(parts, name):
    def body(parts_ref, land_ref, send_sems, recv_sems, parts_thru, land_thru, token):
        me = _slot(*_position())
        for j in range(N_DEV):
            @pl.when(me == j)
            def _():
                pltpu.make_async_copy(parts_ref.at[j], land_ref.at[j], send_sems.at[j]).start()

            @pl.when(me != j)
            def _():
                _exchange_copy(parts_ref, land_ref, send_sems, recv_sems, me, j).start()
        token[...] = jnp.zeros_like(token)

    return pl.pallas_call(
        body, name=name,
        out_shape=(pltpu.SemaphoreType.DMA((N_DEV,)), pltpu.SemaphoreType.DMA((N_DEV,)),
                   pltpu.HBM(parts.shape, parts.dtype), pltpu.HBM(parts.shape, parts.dtype), _sds((8, 128), F32)),
        in_specs=(HBM_ARRAY, HBM_ARRAY),
        out_specs=(SEMAPHORES, SEMAPHORES, HBM_ARRAY, HBM_ARRAY, pl.BlockSpec(memory_space=pltpu.VMEM)),
        input_output_aliases={0: 2, 1: 3}, compiler_params=pltpu.CompilerParams(has_side_effects=DATAFLOW),
    )(pltpu.with_memory_space_constraint(parts, pltpu.HBM),
      pltpu.with_memory_space_constraint(lax.empty(parts.shape, parts.dtype), pltpu.HBM))


def _exchange_wait(send_sems, recv_sems, parts_thru, land_thru, after, name):
    def body(parts_ref, land_ref, send_sems, recv_sems, after_ref, parts_dead, got_ref):
        me = _slot(*_position())
        for j in range(N_DEV):
            @pl.when(me == j)
            def _():
                pltpu.make_async_copy(parts_ref.at[j], land_ref.at[j], send_sems.at[j]).wait()

            @pl.when(me != j)
            def _():
                both = pltpu.make_async_remote_copy(src_ref=parts_ref.at[j], dst_ref=land_ref.at[j],
                                                    send_sem=send_sems.at[j], recv_sem=recv_sems.at[j],
                                                    device_id=(j >> 2, (j >> 1) & 1, j & 1), device_id_type=MESH)
                both.wait_send()
                both.wait_recv()

    return pl.pallas_call(
        body, name=name, out_shape=(pltpu.HBM(parts_thru.shape, parts_thru.dtype),
                                    pltpu.HBM(parts_thru.shape, parts_thru.dtype)),
        in_specs=(HBM_ARRAY, HBM_ARRAY, SEMAPHORES, SEMAPHORES, pl.BlockSpec(memory_space=pl.ANY)),
        out_specs=(HBM_ARRAY, HBM_ARRAY), input_output_aliases={0: 0, 1: 1},
        compiler_params=pltpu.CompilerParams(has_side_effects=DATAFLOW),
    )(parts_thru, land_thru, send_sems, recv_sems, after)[1]


class _Mesh:
    def __init__(self, shards):
        self.shards, self.full, self.received, self.cache, self.pending, self.tokens = shards, {}, {}, {}, {}, {}

    def fetch(self, wanted):
        items = []
        for want in wanted:
            name, r0, r1 = want if isinstance(want, tuple) else (want, 0, self.shards[want].shape[0])
            items.append((self.shards[name], (r0, r1), self.full.get(name)))
        return _gather_carry(items)

    def fetched(self, wanted, results):
        self.full.update(zip([want[0] if isinstance(want, tuple) else want for want in wanted], results))

    def send(self, *payloads):
        return _exchange_carry([(parts, rows or (0, parts.shape[1]), self.received.get(name))
                                for name, parts, rows in payloads])

    def sent(self, names, results):
        self.received.update(zip(names, results))

    def send_apart(self, name, parts):
        *self.pending[name], self.tokens[name] = _exchange_start(parts, "exchange_" + name + "_start")
        return self.tokens[name]

    def sent_apart(self, name, after):
        self.received[name] = _exchange_wait(*self.pending.pop(name), after, "exchange_" + name + "_wait")

    def w(self, key):
        if key not in self.cache:
            self.cache[key] = self._layout(key)
        return self.cache[key]

    def _layout(self, key):
        if key in ("gu1", "gu2"):
            return self.full[key]
        if key in ("d1", "d2"):
            return self.full[key].reshape(4, FS, D)
        if key in ("out", "q", "o"):
            return self.full[key].reshape(D, D)
        if key == "kv":
            return self.full["kv"]
        if key == "convw":
            rows = self.full["conv"][:, :3, :].transpose(1, 0, 2).reshape(3, D)
            return jnp.concatenate([rows, jnp.zeros((5, D), F32)], axis=0)
        assert key == "win_t", key
        return self.full["win"].reshape(-1, D)


def _forward_backward(x, mem, target, g, rel_bias, sinks, ex):
    s = x.shape[0]
    def fetching(wanted, call, *args, **kw):
        res, got = call(*args, carry=ex.fetch(wanted), **kw)
        ex.fetched(wanted, got)
        return res

    h1 = fetching(["gu1", "conv"], _rmsnorm, x, g["ffn1"], "norm_ffn1")
    gu1, a1 = fetching(["d1", ("win", 0, 400)], _ffn_up, h1, ex.w("gu1").reshape(2, 4, FS, D), "ffn1_up")
    x1, h2 = fetching([("win", 400, 832)], _mm_res_norm, a1, ex.w("d1"), x, g["mix"], 0.5, "ffn1_down")
    pa, q, kv = fetching(["gu2", "out"], _in_proj, h2, ex.w("win_t"), "in_proj")
    biasm = _bias_build(rel_bias, "bias_build")
    attn, lse = fetching(["kv", "d2", "o"], _swa_fwd, q, kv, biasm, sinks, "swa_fwd")
    merged = fetching(["q"], _conv_merge_fwd, pa, attn, ex.w("convw"), "conv_merge_fwd")
    (x2, h3), _ = _mm_res_norm(merged[None], ex.w("out")[None], x1, g["xattn"], 1.0, "out_proj")
    q2 = _mm_nn(h3, ex.w("q")[None], "xattn_q")[0][0]
    mh, _ = _rmsnorm(mem, g["mem"], "norm_mem")
    kv2 = _mm_nn(mh, ex.w("kv"), "xattn_kv")[0]
    o, lse2 = _xattn_fwd(q2, kv2, "xattn_fwd")
    (x3, h4), _ = _mm_res_norm(o[None], ex.w("o")[None], x2, g["ffn2"], 1.0, "xattn_o")
    (gu2, a2), _ = _ffn_up(h4, ex.w("gu2").reshape(2, 4, FS, D), "ffn2_up")
    dx4, dx4b, loss, d_final = _ffn_down_loss(a2, ex.w("d2"), x3, g["final"], target, "ffn2_down_loss")
    def sending(payloads, call, *args, **kw):
        res, got = call(*args, carry=ex.send(*payloads), **kw)
        ex.sent([name for name, _, _ in payloads], got)
        return res

    dw_d2 = _mm_tn(a2, dx4b[None], "dw_ffn2_down", scale=0.5)[0].reshape(N_DEV, -1, D)
    dgu2 = sending([("d2", dw_d2, None)], _ffn_down_bwd, dx4b, ex.w("d2"), gu2, "ffn2_down_bwd").reshape(8, s, FS)
    dw_gu2 = _mm_tn(dgu2, h4[None], "dw_ffn2_up", scale=0.5)[0]
    dx3, dx3b, d_ffn2 = sending([("gu2", dw_gu2, (0, 400))], _mm_acc_rms_bwd, dgu2, ex.w("gu2"), "ffn2_up_bwd",
                                x=x3, gain=g["ffn2"], dres=dx4, scale=0.5)
    do, _ = _mm_acc(dx3b[None], ex.w("o")[None], "xattn_o_bwd", BF16, bt=True)
    dw_o = _mm_tn(o[None], dx3b[None], "dw_xattn_o")[0].reshape(N_DEV, -1, D)
    dq2, dkv2 = sending([("o", dw_o, None)], _xattn_bwd, q2, kv2, o, do, lse2, "xattn_bwd")
    dkv2b = dkv2.astype(BF16)
    dw_q = _mm_tn(h3[None], dq2[None], "dw_xattn_q")[0].reshape(N_DEV, -1, D)
    dx2, dx2b, d_xattn = sending([("q", dw_q, None)], _mm_acc_rms_bwd, dq2[None], ex.w("q")[None],
                                 "xattn_q_bwd", x=x2, gain=g["xattn"], dres=dx3, bt=True)
    dw_kv = _mm_tn(mh[None], dkv2b, "dw_xattn_kv")[0]
    (_, _, d_mem), _ = _mm_acc_rms_bwd(dkv2b, ex.w("kv"), "xattn_kv_bwd", x=mem, gain=g["mem"],
                                       dres=jnp.zeros_like(mem), bt=True)
    dmerged, _ = _mm_acc(dx2b[None], ex.w("out")[None], "out_proj_bwd", BF16, bt=True)
    dw_out = _mm_tn(merged[None], dx2b[None], "dw_out_proj")[0].reshape(N_DEV, -1, D)
    dattn, dpa, d_convw = sending([("kv", dw_kv, None)], _conv_merge_bwd,
                                  dmerged, pa, attn, ex.w("convw"), "conv_merge_bwd")
    dq, dkv, dbias, d_sinks = sending([("gu2", dw_gu2, (400, FS)), ("out", dw_out, None)], _swa_bwd,
                                      q, kv, attn, dattn, lse, biasm, sinks, "swa_bwd")
    d_relb = _bias_bwd(dbias, "bias_bwd")
    w_rows = ex.w("win_t").shape[0]
    dw_in = _mm_tn_rows(dpa, h2, "dw_in_proj_a", w_rows, NQ + NKV)
    dw_in = _mm_tn_rows(dq[None], h2, "dw_in_proj_q", w_rows, 0, begun=dw_in)
    dw_in = _mm_tn_rows(dkv[None], h2, "dw_in_proj_kv", w_rows, NQ, begun=dw_in).reshape(N_DEV, -1, D)
    (dx1, dx1b, d_mix), _ = _in_proj_bwd(dpa, dq, dkv, ex.w("win_t"), "in_proj_bwd", x=x1, gain=g["mix"], dres=dx2,
                                         behind=ex.send_apart("win", dw_in))
    dw_d1 = _mm_tn(a1, dx1b[None], "dw_ffn1_down", scale=0.5)[0].reshape(N_DEV, -1, D)
    dgu1 = _ffn_down_bwd(dx1b, ex.w("d1"), gu1, "ffn1_down_bwd", behind=ex.send_apart("d1", dw_d1))[0]
    dgu1 = dgu1.reshape(8, s, FS)
    dw_gu1 = _mm_tn(dgu1, h1[None], "dw_ffn1_up", scale=0.5)[0]
    (dx0, _, d_ffn1), _ = _mm_acc_rms_bwd(dgu1, ex.w("gu1"), "ffn1_up_bwd", x=x, gain=g["ffn1"], dres=dx1,
                                          scale=0.5, behind=ex.send_apart("gu1", dw_gu1))

    relb_row = jnp.concatenate([d_relb[:, :REL_BUCKETS].T.reshape(1, REL_BUCKETS * N_HEADS), d_sinks[:, :N_HEADS],
                                jnp.zeros((1, D - REL_BUCKETS * N_HEADS - N_HEADS), F32)], axis=1)
    loss_row = jnp.concatenate([loss[0:1, 0:1], jnp.zeros((1, D - 1), F32)], axis=1)
    small = jnp.concatenate([d_ffn1, d_mix, d_xattn, d_mem, d_ffn2, d_final, relb_row, loss_row, d_convw[0:3],
                             jnp.zeros((SMALL_ROWS - ROW_CONV - 3, D), F32)], axis=0)
    return dx0, small


def _pack_small(norms, final, relb, sinks, conv_local, me):
    relb_row = jnp.concatenate([relb.reshape(1, -1), sinks.reshape(1, -1),
                                jnp.zeros((1, D - REL_BUCKETS * N_HEADS - N_HEADS), F32)], axis=1)
    conv_rows = lax.dynamic_update_slice(jnp.zeros((3, D), F32), conv_local.reshape(3, -1), (0, 128 * me))
    return jnp.concatenate(list(norms) + [final.reshape(1, D), relb_row, jnp.zeros((1, D), F32), conv_rows,
                                          jnp.zeros((SMALL_ROWS - ROW_CONV - 3, D), F32)], axis=0)


def kernel(x, mem, positions, rel_bias, ffn1_norm, ffn1_w_gu, ffn1_w_down, mix_norm, w_in, sinks, conv_w, w_out, xattn_norm, mem_norm, xattn_wq, xattn_wkv, xattn_wo, ffn2_norm, ffn2_w_gu, ffn2_w_down, final_norm, loss_target, m_rel_bias, m_ffn1_norm, m_ffn1_w_gu, m_ffn1_w_down, m_mix_norm, m_w_in, m_sinks, m_conv_w, m_w_out, m_xattn_norm, m_mem_norm, m_xattn_wq, m_xattn_wkv, m_xattn_wo, m_ffn2_norm, m_ffn2_w_gu, m_ffn2_w_down, m_final_norm, v_rel_bias, v_ffn1_norm, v_ffn1_w_gu, v_ffn1_w_down, v_mix_norm, v_w_in, v_sinks, v_conv_w, v_w_out, v_xattn_norm, v_mem_norm, v_xattn_wq, v_xattn_wkv, v_xattn_wo, v_ffn2_norm, v_ffn2_w_gu, v_ffn2_w_down, v_final_norm):
    del positions
    me = _slot(*_position())
    big = dict(gu1=(ffn1_w_gu, m_ffn1_w_gu, v_ffn1_w_gu), d1=(ffn1_w_down, m_ffn1_w_down, v_ffn1_w_down),
               win=(w_in, m_w_in, v_w_in), out=(w_out, m_w_out, v_w_out), q=(xattn_wq, m_xattn_wq, v_xattn_wq),
               kv=(xattn_wkv, m_xattn_wkv, v_xattn_wkv), o=(xattn_wo, m_xattn_wo, v_xattn_wo),
               gu2=(ffn2_w_gu, m_ffn2_w_gu, v_ffn2_w_gu), d2=(ffn2_w_down, m_ffn2_w_down, v_ffn2_w_down))
    order = list(big)
    transposed = ("gu1", "gu2", "win")
    local = {k: tuple(t[0].T if k in transposed else t[0] for t in big[k]) for k in order}
    shards = {k: local[k][0].astype(BF16) for k in order}
    shards["conv"] = jnp.concatenate([conv_w[0], jnp.zeros((5, 128), F32)], axis=0)
    ex = _Mesh(shards)
    gains = dict(ffn1=ffn1_norm, mix=mix_norm, xattn=xattn_norm, mem=mem_norm, ffn2=ffn2_norm,
                 final=final_norm.reshape(1, D))
    dx, small = _forward_backward(x[0], mem[0], loss_target[0], gains, rel_bias, sinks, ex)
    apart = ("d1", "win", "gu1")
    big_out = {k: _adamw(ex.received[k], *local[k], "adamw_" + k, behind=ex.tokens["gu1"])
               for k in order if k not in apart}
    for k in apart[:-1]:
        ex.sent_apart(k, after=sum(big_out[j][1][0:1, 0:1] for j in big_out))
        big_out[k] = _adamw(ex.received[k], *local[k], "adamw_" + k)
    spare = sum(big_out[k][1][0:1, 0:1] for k in big_out)
    small = lax.dynamic_update_slice(small, spare, (SMALL_ROWS - 1, 0))
    small_parts = _run_alone(_exchange_carry([], [small]), "exchange_small")[0]
    packed = [_pack_small(norms, final, relb, sk, conv, me) for norms, final, relb, sk, conv in (
        ((ffn1_norm, mix_norm, xattn_norm, mem_norm, ffn2_norm), final_norm, rel_bias, sinks, conv_w),
        ((m_ffn1_norm, m_mix_norm, m_xattn_norm, m_mem_norm, m_ffn2_norm), m_final_norm, m_rel_bias, m_sinks, m_conv_w),
        ((v_ffn1_norm, v_mix_norm, v_xattn_norm, v_mem_norm, v_ffn2_norm), v_final_norm, v_rel_bias, v_sinks, v_conv_w))]
    small_out = _adamw(small_parts, *packed, "adamw_small")
    done = [dx[0:1, 0:1], small_out[1][0:1, 0:1]] + [big_out[k][1][0:1, 0:1] for k in big_out]
    ex.sent_apart("gu1", after=sum(done))
    big_out["gu1"] = _adamw(ex.received["gu1"], *local["gu1"], "adamw_gu1")
    big_out = {k: [t.T if k in transposed else t for t in big_out[k]] for k in order}

    def unpack(t):
        conv = lax.dynamic_slice(t[ROW_CONV:ROW_CONV + 3], (0, 128 * me), (3, 128))[None]
        nrel = REL_BUCKETS * N_HEADS
        return dict(ffn1_norm=t[0:1], mix_norm=t[1:2], xattn_norm=t[2:3], mem_norm=t[3:4], ffn2_norm=t[4:5],
                    final_norm=t[5], rel_bias=t[ROW_RELB, :nrel].reshape(REL_BUCKETS, N_HEADS),
                    sinks=t[ROW_RELB:ROW_RELB + 1, nrel:nrel + N_HEADS], conv_w=conv)

    names = dict(gu1="ffn1_w_gu", d1="ffn1_w_down", win="w_in", out="w_out", q="xattn_wq", kv="xattn_wkv",
                 o="xattn_wo", gu2="ffn2_w_gu", d2="ffn2_w_down")
    results = []
    for idx in range(4):
        leaves = unpack(small_out[idx])
        leaves.update({names[k]: big_out[k][idx][None] for k in order})
        results.append(leaves)
    weights = ("rel_bias", "ffn1_norm", "ffn1_w_gu", "ffn1_w_down", "mix_norm", "w_in", "sinks", "conv_w", "w_out",
               "xattn_norm", "mem_norm", "xattn_wq", "xattn_wkv", "xattn_wo", "ffn2_norm", "ffn2_w_gu", "ffn2_w_down",
               "final_norm")
    loss = small_out[0][ROW_LOSS, 0]
    return (loss, dx[None], *[leaves[n] for leaves in results for n in weights])
```

```python
import math

import numpy as np
import jax
import jax.numpy as jnp
from jax import lax
from jax.experimental import pallas as pl
from jax.experimental.pallas import tpu as pltpu

F32, BF16 = jnp.float32, jnp.bfloat16
MESH = pl.DeviceIdType.MESH

D = 1024
N_DEV = 8
D_FF = 2816
FS = D_FF // 4
HEAD = 64
N_HEADS, N_KV = 16, 4
BLK = 128
NQ, NKV = N_HEADS * HEAD, 2 * N_KV * HEAD
XH, XHD = 4, 256
REL_BUCKETS, REL_EXACT, REL_MAX_DIST = 32, 16, 128
EPS, NEG = 1e-6, -1e30
ADAM_LR, ADAM_B1, ADAM_B2, ADAM_EPS, ADAM_WD, ADAM_STEP = 0.001, 0.9, 0.999, 1e-08, 0.01, 10
VMEM_LIMIT_V7X = 56 * 2**20
SMALL_ROWS = 16
ROW_RELB, ROW_LOSS, ROW_CONV = 6, 7, 8


def _bucket_thresholds():
    n = np.arange(REL_MAX_DIST)
    nf = np.maximum(n, 1).astype(np.float32)
    large = REL_EXACT + (np.log(nf / np.float32(REL_EXACT)) / np.float32(math.log(REL_MAX_DIST / REL_EXACT))
                         * np.float32(REL_BUCKETS - REL_EXACT)).astype(np.int32)
    b = np.where(n < REL_EXACT, n, np.minimum(large, REL_BUCKETS - 1))
    return [int(np.argmax(b >= REL_EXACT + k)) for k in range(1, REL_BUCKETS - REL_EXACT)]


BUCKET_THRESHOLDS = _bucket_thresholds()


HBM_SPEC = pl.BlockSpec(memory_space=pl.ANY)


class _Carry:
    def __init__(self, ins, outs, sems, start, finish, mid=None, aliases=None):
        self.ins, self.outs, self.sems = list(ins), list(outs), list(sems)
        self.start, self.finish, self.mid, self.aliases = start, finish, mid, dict(aliases or {})


def _pcall(body, *, name, grid, in_specs, out_specs, out_shape, scratch=(), carry=None, aliases=None, behind=None):
    params = pltpu.CompilerParams(dimension_semantics=("arbitrary",) * len(grid), vmem_limit_bytes=VMEM_LIMIT_V7X)
    if carry is None and behind is not None:
        n_in = len(in_specs)
        call = pl.pallas_call(lambda *refs: body(*refs[:n_in], *refs[n_in + 1:]), name=name, grid=grid,
                              in_specs=list(in_specs) + [pl.BlockSpec((8, 128), lambda *_: (0, 0))],
                              out_specs=out_specs, out_shape=out_shape, scratch_shapes=list(scratch),
                              compiler_params=params, input_output_aliases=aliases or {})
        return lambda *args: call(*args, behind)
    if carry is None:
        return pl.pallas_call(body, name=name, grid=grid, in_specs=in_specs, out_specs=out_specs,
                              out_shape=out_shape, scratch_shapes=list(scratch), compiler_params=params,
                              input_output_aliases=aliases or {})
    assert aliases is None and behind is None, name
    single = not isinstance(out_shape, (list, tuple))
    own_specs, own_shapes = ([out_specs], [out_shape]) if single else (list(out_specs), list(out_shape))
    n_in, n_out, n_scr = len(in_specs), len(own_shapes), len(scratch)
    n_cin, n_cout = len(carry.ins), len(carry.outs)
    steps = math.prod(grid)
    mid_step = max(steps - 1 - max(steps // 8, 1), 0)

    def carrying(*refs):
        ins, refs = refs[:n_in], refs[n_in:]
        cins, refs = refs[:n_cin], refs[n_cin:]
        outs, refs = refs[:n_out], refs[n_out:]
        couts, refs = refs[:n_cout], refs[n_cout:]
        scr, csems = refs[:n_scr], refs[n_scr:]
        step = 0
        for axis, size in enumerate(grid):
            step = step * size + pl.program_id(axis)

        @pl.when(step == 0)
        def _():
            carry.start(cins, couts, csems)

        body(*ins, *outs, *scr)
        if carry.mid is not None:
            @pl.when(step == mid_step)
            def _():
                carry.mid(cins, couts, csems)

        @pl.when(step == steps - 1)
        def _():
            carry.finish(cins, couts, csems)

    call = pl.pallas_call(carrying, name=name, grid=grid, in_specs=list(in_specs) + [HBM_SPEC] * n_cin,
                          out_specs=own_specs + [HBM_SPEC] * n_cout, out_shape=own_shapes + carry.outs,
                          scratch_shapes=list(scratch) + carry.sems, compiler_params=params,
                          input_output_aliases={n_in + i: n_out + o for i, o in carry.aliases.items()})

    def run(*args):
        res = call(*args, *carry.ins)
        return (res[0] if single else res[:n_out]), res[n_out:]

    return run


def _run_alone(carry, name):
    n_cin, n_cout = len(carry.ins), len(carry.outs)

    def body(*refs):
        cins, couts, csems = refs[:n_cin], refs[n_cin:n_cin + n_cout], refs[n_cin + n_cout:]
        carry.start(cins, couts, csems)
        if carry.mid is not None:
            carry.mid(cins, couts, csems)
        carry.finish(cins, couts, csems)

    return pl.pallas_call(body, name=name, in_specs=[HBM_SPEC] * n_cin, out_specs=[HBM_SPEC] * n_cout,
                          out_shape=carry.outs, scratch_shapes=carry.sems,
                          input_output_aliases=carry.aliases)(*carry.ins)


def _dot(a, b):
    return jnp.dot(a, b, preferred_element_type=F32)


def _dot_nt(a, b):
    return lax.dot_general(a, b, (((1,), (1,)), ((), ())), preferred_element_type=F32)


def _dot_tn(a, b):
    return lax.dot_general(a, b, (((0,), (0,)), ((), ())), preferred_element_type=F32)


def _sds(shape, dtype):
    return jax.ShapeDtypeStruct(tuple(shape), dtype)


ROW_CHUNK = 256


def _row_chunks(tm):
    return [slice(r, min(r + ROW_CHUNK, tm)) for r in range(0, tm, ROW_CHUNK)]


def _carried(call, args, carry):
    return call(*args) if carry is not None else (call(*args), ())


def _rmsnorm(x, g, name, carry=None):
    m, d = x.shape
    tm = min(512, m)

    def body(x_ref, g_ref, h_ref):
        xv = x_ref[...]
        r = lax.rsqrt(jnp.mean(xv * xv, axis=-1, keepdims=True) + EPS)
        h_ref[...] = (xv * r * g_ref[...]).astype(BF16)

    call = _pcall(body, name=name, grid=(m // tm,), carry=carry,
                  in_specs=[pl.BlockSpec((tm, d), lambda i: (i, 0)), pl.BlockSpec((1, d), lambda i: (0, 0))],
                  out_specs=pl.BlockSpec((tm, d), lambda i: (i, 0)), out_shape=_sds((m, d), BF16))
    return _carried(call, (x, g), carry)


def _mm_nn(a, b, name, tm=1024, bt=False, carry=None):
    m, k = a.shape
    nj = b.shape[0]
    n = b.shape[1] if bt else b.shape[2]
    tm = min(tm, m)
    dot = _dot_nt if bt else _dot

    def body(a_ref, b_ref, o_ref):
        o_ref[...] = dot(a_ref[...], b_ref[...]).astype(BF16)

    call = _pcall(body, name=name, grid=(nj, m // tm),
                  in_specs=[pl.BlockSpec((tm, k), lambda j, i: (i, 0)),
                            pl.BlockSpec((None,) + b.shape[1:], lambda j, i: (j, 0, 0))],
                  out_specs=pl.BlockSpec((None, tm, n), lambda j, i: (j, i, 0)),
                  out_shape=_sds((nj, m, n), BF16), carry=carry)
    return _carried(call, (a, b), carry)


def _load_once(src_hbm, dst_vmem, sem):
    @pl.when(pl.program_id(0) == 0)
    def _():
        load = pltpu.make_async_copy(src_hbm, dst_vmem, sem)
        load.start()
        load.wait()


def _resident(w):
    return [pltpu.VMEM(w.shape, w.dtype), pltpu.SemaphoreType.DMA(())]


def _ffn_up(h, w4, name, tm=512, carry=None):
    s, d = h.shape
    tm = min(tm, s)

    def body(h_ref, w_hbm, gu_ref, a_ref, w_ref, w_sem):
        _load_once(w_hbm, w_ref, w_sem)
        for p in range(4):
            for rows in _row_chunks(tm):
                hv = h_ref[rows, :]
                g = _dot_nt(hv, w_ref[0, p])
                u = _dot_nt(hv, w_ref[1, p])
                gu_ref[0, p, rows, :] = g.astype(BF16)
                gu_ref[1, p, rows, :] = u.astype(BF16)
                a_ref[p, rows, :] = (g * jax.nn.sigmoid(g) * u).astype(BF16)

    call = _pcall(body, name=name, grid=(s // tm,),
                  in_specs=[pl.BlockSpec((tm, d), lambda i: (i, 0)), HBM_SPEC],
                  out_specs=[pl.BlockSpec((2, 4, tm, FS), lambda i: (0, 0, i, 0)),
                             pl.BlockSpec((4, tm, FS), lambda i: (0, i, 0))],
                  out_shape=[_sds((2, 4, s, FS), BF16), _sds((4, s, FS), BF16)], scratch=_resident(w4), carry=carry)
    return _carried(call, (h, w4), carry)


N_SEG = 5
IN_PROJ_WEIGHTS = [pltpu.VMEM((NQ, D), BF16), pltpu.VMEM((NKV, D), BF16), pltpu.VMEM((N_SEG, D, D), BF16),
                   pltpu.SemaphoreType.DMA((2 + N_SEG,))]


def _load_in_proj(w_hbm, wq_ref, wkv_ref, wa_ref, sems):
    @pl.when(pl.program_id(0) == 0)
    def _():
        loads = [pltpu.make_async_copy(w_hbm.at[pl.ds(0, NQ)], wq_ref, sems.at[0]),
                 pltpu.make_async_copy(w_hbm.at[pl.ds(NQ, NKV)], wkv_ref, sems.at[1])]
        loads += [pltpu.make_async_copy(w_hbm.at[pl.ds(NQ + NKV + D * j, D)], wa_ref.at[j], sems.at[2 + j])
                  for j in range(N_SEG)]
        for load in loads:
            load.start()
        for load in loads:
            load.wait()


def _in_proj(h, w_in_t, name, tm=512, carry=None):
    s, d = h.shape
    tm = min(tm, s)

    def body(h_ref, w_hbm, pa_ref, q_ref, kv_ref, wq_ref, wkv_ref, wa_ref, sems):
        _load_in_proj(w_hbm, wq_ref, wkv_ref, wa_ref, sems)
        hv = h_ref[...]
        q_ref[...] = _dot_nt(hv, wq_ref[...]).astype(BF16)
        kv_ref[...] = _dot_nt(hv, wkv_ref[...]).astype(BF16)
        for j in range(N_SEG):
            pa_ref[j] = _dot_nt(hv, wa_ref[j]).astype(BF16)

    call = _pcall(body, name=name, grid=(s // tm,), carry=carry,
                  in_specs=[pl.BlockSpec((tm, d), lambda i: (i, 0)), HBM_SPEC],
                  out_specs=[pl.BlockSpec((N_SEG, tm, d), lambda i: (0, i, 0)),
                             pl.BlockSpec((tm, NQ), lambda i: (i, 0)), pl.BlockSpec((tm, NKV), lambda i: (i, 0))],
                  out_shape=[_sds((N_SEG, s, d), BF16), _sds((s, NQ), BF16), _sds((s, NKV), BF16)],
                  scratch=IN_PROJ_WEIGHTS)
    return _carried(call, (h, w_in_t), carry)


def _mm_res_norm(a, w, xres, gain, scale, name, tm=512, carry=None):
    npart, s, kp = a.shape
    tm = min(tm, s)

    def body(a_ref, w_ref, x_ref, g_ref, xo_ref, h_ref):
        for rows in _row_chunks(tm):
            acc = _dot(a_ref[0, rows, :], w_ref[0])
            for p in range(1, npart):
                acc = acc + _dot(a_ref[p, rows, :], w_ref[p])
            xn = x_ref[rows, :] + scale * acc
            xo_ref[rows, :] = xn
            r = lax.rsqrt(jnp.mean(xn * xn, axis=-1, keepdims=True) + EPS)
            h_ref[rows, :] = (xn * r * g_ref[...]).astype(BF16)

    call = _pcall(body, name=name, grid=(s // tm,),
                  in_specs=[pl.BlockSpec((npart, tm, kp), lambda i: (0, i, 0)),
                            pl.BlockSpec((npart, kp, D), lambda i: (0, 0, 0)),
                            pl.BlockSpec((tm, D), lambda i: (i, 0)),
                            pl.BlockSpec((1, D), lambda i: (0, 0))],
                  out_specs=[pl.BlockSpec((tm, D), lambda i: (i, 0)), pl.BlockSpec((tm, D), lambda i: (i, 0))],
                  out_shape=[_sds((s, D), F32), _sds((s, D), BF16)], carry=carry)
    return _carried(call, (a, w, xres, gain), carry)


def _ffn_down_loss(a, w, xres, gain, target, name, tm=512):
    npart, s, kp = a.shape
    tm = min(tm, s)

    def body(a_ref, w_ref, x_ref, g_ref, t_ref, dx_ref, dxb_ref, loss_ref, dg_ref):
        @pl.when(pl.program_id(0) == 0)
        def _():
            loss_ref[...] = jnp.zeros_like(loss_ref)
            dg_ref[...] = jnp.zeros_like(dg_ref)

        for rows in _row_chunks(tm):
            acc = _dot(a_ref[0, rows, :], w_ref[0])
            for p in range(1, npart):
                acc = acc + _dot(a_ref[p, rows, :], w_ref[p])
            xn = x_ref[rows, :] + 0.5 * acc
            r = lax.rsqrt(jnp.mean(xn * xn, axis=-1, keepdims=True) + EPS)
            xh = xn * r
            gv = g_ref[...]
            err = xh * gv - t_ref[rows, :]
            part = 0.5 * jnp.sum(jnp.mean(err * err, axis=-1, keepdims=True), axis=0, keepdims=True)
            dy = err * (1.0 / D)
            dyg = dy * gv
            dxn = r * (dyg - xh * jnp.mean(dyg * xh, axis=-1, keepdims=True))
            dx_ref[rows, :] = dxn
            dxb_ref[rows, :] = dxn.astype(BF16)
            loss_ref[...] += jnp.broadcast_to(part, loss_ref.shape)
            dg_ref[...] += jnp.sum(dy * xh, axis=0, keepdims=True)

    return _pcall(body, name=name, grid=(s // tm,),
                  in_specs=[pl.BlockSpec((npart, tm, kp), lambda i: (0, i, 0)),
                            pl.BlockSpec((npart, kp, D), lambda i: (0, 0, 0)),
                            pl.BlockSpec((tm, D), lambda i: (i, 0)),
                            pl.BlockSpec((1, D), lambda i: (0, 0)),
                            pl.BlockSpec((tm, D), lambda i: (i, 0))],
                  out_specs=[pl.BlockSpec((tm, D), lambda i: (i, 0)), pl.BlockSpec((tm, D), lambda i: (i, 0)),
                             pl.BlockSpec((8, 128), lambda i: (0, 0)), pl.BlockSpec((1, D), lambda i: (0, 0))],
                  out_shape=[_sds((s, D), F32), _sds((s, D), BF16), _sds((8, 128), F32), _sds((1, D), F32)],
                  )(a, w, xres, gain, target)


def _window_tiles():
    i = lax.broadcasted_iota(jnp.int32, (BLK, BLK), 0)
    j = lax.broadcasted_iota(jnp.int32, (BLK, BLK), 1)
    rel = (i - j) & (BLK - 1)
    large = jnp.full_like(rel, REL_EXACT)
    for t in BUCKET_THRESHOLDS:
        large = large + (rel >= t).astype(jnp.int32)
    return j <= i, jnp.where(rel < REL_EXACT, rel, large)


def _bias_build(rel_bias, name):
    def body(rb_ref, o_ref):
        _, bucket = _window_tiles()

        def per_head(h, carry):
            acc = jnp.zeros((BLK, BLK), F32)
            for b in range(REL_BUCKETS):
                acc = jnp.where(bucket == b, rb_ref[b, h], acc)
            o_ref[h] = acc
            return carry

        lax.fori_loop(0, N_HEADS, per_head, 0)

    return _pcall(body, name=name, grid=(1,),
                  in_specs=[pl.BlockSpec(memory_space=pltpu.SMEM)],
                  out_specs=pl.BlockSpec((N_HEADS, BLK, BLK), lambda i: (0, 0, 0)),
                  out_shape=_sds((N_HEADS, BLK, BLK), F32))(rel_bias)


def _bias_bwd(dbias, name):
    def body(db_ref, o_ref):
        _, bucket = _window_tiles()
        lane = lax.broadcasted_iota(jnp.int32, (N_HEADS, 128), 1)

        def per_bucket(b, out):
            mb = (bucket == b).astype(F32)
            per_col = jnp.sum(db_ref[...] * mb[None, :, :], axis=1)
            return jnp.where(lane == b, jnp.sum(per_col, axis=1, keepdims=True), out)

        o_ref[...] = lax.fori_loop(0, REL_BUCKETS, per_bucket, jnp.zeros((N_HEADS, 128), F32))

    return _pcall(body, name=name, grid=(1,),
                  in_specs=[pl.BlockSpec((N_HEADS, BLK, BLK), lambda i: (0, 0, 0))],
                  out_specs=pl.BlockSpec((N_HEADS, 128), lambda i: (0, 0)),
                  out_shape=_sds((N_HEADS, 128), F32))(dbias)


PAIR = 2 * HEAD
GROUP = N_HEADS // N_KV
SWA_SCALE = HEAD ** -0.5


def _window_masks(n):
    i = lax.broadcasted_iota(jnp.int32, (GROUP * BLK, BLK), 0) & (BLK - 1)
    j = lax.broadcasted_iota(jnp.int32, (GROUP * BLK, BLK), 1)
    return j <= i, jnp.logical_and(n == 0, j > i), j < HEAD


def _kv_twice(ref, base, g, low):
    slab = ref[:, base + PAIR * (g // 2): base + PAIR * (g // 2 + 1)]
    swapped = pltpu.roll(slab, HEAD, 1)
    return jnp.where(low, slab, swapped) if g % 2 == 0 else jnp.where(low, swapped, slab)


def _stack_heads(ref, g, low):
    parts = []
    for r in range(2):
        slab = ref[:, PAIR * (2 * g + r): PAIR * (2 * g + r + 1)]
        zero = jnp.zeros_like(slab)
        parts += [jnp.where(low, slab, zero), jnp.where(low, zero, slab)]
    return jnp.concatenate(parts, axis=0)


def _unstack_heads(t, low):
    return [jnp.where(low, t[2 * r * BLK:(2 * r + 1) * BLK], t[(2 * r + 1) * BLK:(2 * r + 2) * BLK])
            for r in range(2)]


def _head_rows(t, k):
    return t[k * BLK:(k + 1) * BLK]


def _per_head_column(values):
    head = lax.broadcasted_iota(jnp.int32, (GROUP * BLK, 1), 0) // BLK
    col = jnp.full((GROUP * BLK, 1), values[0], F32)
    for k in range(1, GROUP):
        col = jnp.where(head == k, values[k], col)
    return col


def _window_logits(q4, kc, kp, bias4, own, absent):
    sc = jnp.where(own, _dot_nt(q4, kc), _dot_nt(q4, kp)) * SWA_SCALE + bias4
    return jnp.where(absent, NEG, sc)


def _split_window(t, own):
    zero = jnp.zeros_like(t)
    return jnp.where(own, t, zero), jnp.where(own, zero, t)


def _swa_fwd(q, kv, bias, sinks, name, carry=None):
    s = q.shape[0]
    nb = s // BLK
    kvw = 2 * N_KV * HEAD

    def body(q_ref, kc_ref, kp_ref, b_ref, sk_ref, o_ref, lse_ref):
        own, absent, low4 = _window_masks(pl.program_id(0))
        low = low4[:BLK]
        lane = lax.broadcasted_iota(jnp.int32, (BLK, 128), 1)
        lse_t = jnp.zeros((BLK, 128), F32)
        for g in range(N_KV):
            q4 = _stack_heads(q_ref, g, low)
            kc, kp = _kv_twice(kc_ref, 0, g, low), _kv_twice(kp_ref, 0, g, low)
            vc, vp = _kv_twice(kc_ref, N_KV * HEAD, g, low), _kv_twice(kp_ref, N_KV * HEAD, g, low)
            bias4 = b_ref[GROUP * g:GROUP * (g + 1)].reshape(GROUP * BLK, BLK)
            sc = _window_logits(q4, kc, kp, bias4, own, absent)
            sk = _per_head_column([sk_ref[0, GROUP * g + k] for k in range(GROUP)])
            m = jnp.maximum(jnp.max(sc, axis=1, keepdims=True), sk)
            p = jnp.exp(sc - m)
            l = jnp.sum(p, axis=1, keepdims=True) + jnp.exp(sk - m)
            p_own, p_prev = _split_window(p.astype(BF16), own)
            out = (_dot(p_own, vc) + _dot(p_prev, vp)) * (1.0 / l)
            for r, slab in enumerate(_unstack_heads(out, low)):
                o_ref[:, PAIR * (2 * g + r): PAIR * (2 * g + r + 1)] = slab.astype(BF16)
            lse4 = m + jnp.log(l)
            for k in range(GROUP):
                lse_t = jnp.where(lane == GROUP * g + k, _head_rows(lse4, k), lse_t)
        lse_ref[...] = lse_t

    call = _pcall(body, name=name, grid=(nb,),
                  in_specs=[pl.BlockSpec((BLK, D), lambda n: (n, 0)),
                            pl.BlockSpec((BLK, kvw), lambda n: (n, 0)),
                            pl.BlockSpec((BLK, kvw), lambda n: (jnp.maximum(n - 1, 0), 0)),
                            pl.BlockSpec((N_HEADS, BLK, BLK), lambda n: (0, 0, 0)),
                            pl.BlockSpec(memory_space=pltpu.SMEM)],
                  out_specs=[pl.BlockSpec((BLK, D), lambda n: (n, 0)), pl.BlockSpec((BLK, 128), lambda n: (n, 0))],
                  out_shape=[_sds((s, D), BF16), _sds((s, 128), F32)], carry=carry)
    return _carried(call, (q, kv, kv, bias, sinks), carry)


def _fold_halves(t, g, low):
    folded = jnp.where(low, t, 0.0) + pltpu.roll(jnp.where(low, 0.0, t), HEAD, 1)
    return folded if g % 2 == 0 else pltpu.roll(folded, HEAD, 1)


def _swa_bwd(q, kv, attn, dattn, lse, bias, sinks, name, carry=None):
    s = q.shape[0]
    nb = s // BLK
    kvw = 2 * N_KV * HEAD
    voff = N_KV * HEAD

    def body(q_ref, kc_ref, kp_ref, o_ref, do_ref, lse_ref, b_ref, skrow_ref, dq_ref, dkv_ref, dbias_ref, dsk_ref,
             dq_hold, kv_hold, dq_new, kv_prev, kv_cur):
        n = pl.program_id(0)

        @pl.when(n == 0)
        def _():
            dbias_ref[...] = jnp.zeros_like(dbias_ref)
            dsk_ref[...] = jnp.zeros_like(dsk_ref)
            dq_hold[...] = jnp.zeros_like(dq_hold)
            kv_hold[...] = jnp.zeros_like(kv_hold)

        @pl.when(n < nb)
        def _():
            own, absent, low4 = _window_masks(n)
            low = low4[:BLK]
            lane = lax.broadcasted_iota(jnp.int32, (BLK, 128), 1)
            delta_t = jnp.zeros((BLK, 128), F32)
            ones = jnp.ones((PAIR, 128), BF16)
            for pair_of_kv in range(N_KV // 2):
                slab_grads = [jnp.zeros((BLK, PAIR), F32) for _ in range(4)]
                for g in (2 * pair_of_kv, 2 * pair_of_kv + 1):
                    q4, do4 = _stack_heads(q_ref, g, low), _stack_heads(do_ref, g, low)
                    kc, kp = _kv_twice(kc_ref, 0, g, low), _kv_twice(kp_ref, 0, g, low)
                    vc, vp = _kv_twice(kc_ref, voff, g, low), _kv_twice(kp_ref, voff, g, low)
                    o_slabs = [o_ref[:, PAIR * (2 * g + r): PAIR * (2 * g + r + 1)] for r in range(2)]
                    o4 = jnp.concatenate([o_slabs[0], o_slabs[0], o_slabs[1], o_slabs[1]], axis=0)
                    delta = _dot(do4 * o4, ones)
                    heads = range(GROUP * g, GROUP * (g + 1))
                    lse4 = jnp.concatenate([lse_ref[:, h:h + 1] for h in heads], axis=0)
                    bias4 = b_ref[GROUP * g:GROUP * (g + 1)].reshape(GROUP * BLK, BLK)
                    p = jnp.exp(_window_logits(q4, kc, kp, bias4, own, absent) - lse4)
                    dp = jnp.where(own, _dot_nt(do4, vc), _dot_nt(do4, vp))
                    ds = p * (dp - delta)
                    dbias_ref[GROUP * g:GROUP * (g + 1)] += ds.reshape(GROUP, BLK, BLK)
                    for k, h in enumerate(heads):
                        delta_t = jnp.where(lane == h, _head_rows(delta, k), delta_t)
                    ds_own, ds_prev = _split_window((ds * SWA_SCALE).astype(BF16), own)
                    p_own, p_prev = _split_window(p.astype(BF16), own)
                    dq4 = _dot(ds_own, kc) + _dot(ds_prev, kp)
                    for r, slab in enumerate(_unstack_heads(dq4, low)):
                        dq_new[:, PAIR * (2 * g + r): PAIR * (2 * g + r + 1)] = slab
                    grads = [_dot_tn(ds_own, q4), _dot_tn(ds_prev, q4), _dot_tn(p_own, do4), _dot_tn(p_prev, do4)]
                    slab_grads = [t + _fold_halves(dk, g, low) for t, dk in zip(slab_grads, grads)]
                ks = slice(PAIR * pair_of_kv, PAIR * (pair_of_kv + 1))
                vs = slice(voff + PAIR * pair_of_kv, voff + PAIR * (pair_of_kv + 1))
                kv_cur[:, ks], kv_prev[:, ks], kv_cur[:, vs], kv_prev[:, vs] = slab_grads
            dsk_ref[...] -= jnp.sum(jnp.exp(skrow_ref[...] - lse_ref[...]) * delta_t, axis=0, keepdims=True)

        @pl.when(n == nb)
        def _():
            kv_prev[...] = jnp.zeros_like(kv_prev)

        dq_ref[...] = dq_hold[...].astype(BF16)
        dkv_ref[...] = (kv_hold[...] + kv_prev[...]).astype(BF16)

        @pl.when(n < nb)
        def _():
            dq_hold[...] = dq_new[...]
            kv_hold[...] = kv_cur[...]

    def cur(n):
        return jnp.minimum(n, nb - 1)

    call = _pcall(body, name=name, grid=(nb + 1,), carry=carry,
                  in_specs=[pl.BlockSpec((BLK, D), lambda n: (cur(n), 0)),
                            pl.BlockSpec((BLK, kvw), lambda n: (cur(n), 0)),
                            pl.BlockSpec((BLK, kvw), lambda n: (jnp.maximum(cur(n) - 1, 0), 0)),
                            pl.BlockSpec((BLK, D), lambda n: (cur(n), 0)),
                            pl.BlockSpec((BLK, D), lambda n: (cur(n), 0)),
                            pl.BlockSpec((BLK, 128), lambda n: (cur(n), 0)),
                            pl.BlockSpec((N_HEADS, BLK, BLK), lambda n: (0, 0, 0)),
                            pl.BlockSpec((1, 128), lambda n: (0, 0))],
                  out_specs=[pl.BlockSpec((BLK, D), lambda n: (jnp.maximum(n - 1, 0), 0)),
                             pl.BlockSpec((BLK, kvw), lambda n: (jnp.maximum(n - 1, 0), 0)),
                             pl.BlockSpec((N_HEADS, BLK, BLK), lambda n: (0, 0, 0)),
                             pl.BlockSpec((1, 128), lambda n: (0, 0))],
                  out_shape=[_sds((s, D), BF16), _sds((s, kvw), BF16), _sds((N_HEADS, BLK, BLK), F32),
                             _sds((1, 128), F32)],
                  scratch=[pltpu.VMEM((BLK, D), F32), pltpu.VMEM((BLK, kvw), F32), pltpu.VMEM((BLK, D), F32),
                           pltpu.VMEM((BLK, kvw), F32), pltpu.VMEM((BLK, kvw), F32)])
    sink_row = jnp.pad(sinks, ((0, 0), (0, 128 - N_HEADS)))
    return _carried(call, (q, kv, kv, attn, dattn, lse, bias, sink_row), carry)


HALO = 16
CW = D


def _conv_taps(cu, halo_cu, first_tile):
    row = lax.broadcasted_iota(jnp.int32, cu.shape, 0)
    halo_cu = jnp.where(first_tile, 0.0, halo_cu)
    c1 = jnp.where(row == 0, halo_cu[HALO - 1:HALO], pltpu.roll(cu, 1, 0))
    c2 = jnp.where(row == 0, halo_cu[HALO - 2:HALO - 1],
                   jnp.where(row == 1, halo_cu[HALO - 1:HALO], pltpu.roll(cu, 2, 0)))
    return c1, c2


def _conv_merge_fwd(pa, attn, convw, name, ts=256, carry=None):
    _, s, _ = pa.shape
    ts = min(ts, s)
    hb = ts // HALO

    def body(pa_ref, hp_ref, at_ref, w_ref, o_ref):
        i = pl.program_id(1)
        cu = pa_ref[0].astype(F32) * pa_ref[2].astype(F32)
        c1, c2 = _conv_taps(cu, hp_ref[0].astype(F32) * hp_ref[2].astype(F32), i == 0)
        w = w_ref[...]
        c3 = w[0:1] * c2 + w[1:2] * c1 + w[2:3] * cu
        conv = pa_ref[1].astype(F32) * c3
        o_ref[...] = (jax.nn.sigmoid(pa_ref[3].astype(F32)) * at_ref[...].astype(F32)
                      + jax.nn.sigmoid(pa_ref[4].astype(F32)) * conv).astype(BF16)

    call = _pcall(body, name=name, grid=(D // CW, s // ts), carry=carry,
                  in_specs=[pl.BlockSpec((5, ts, CW), lambda c, i: (0, i, c)),
                            pl.BlockSpec((5, HALO, CW), lambda c, i: (0, jnp.maximum(i * hb - 1, 0), c)),
                            pl.BlockSpec((ts, CW), lambda c, i: (i, c)),
                            pl.BlockSpec((8, CW), lambda c, i: (0, c))],
                  out_specs=pl.BlockSpec((ts, CW), lambda c, i: (i, c)),
                  out_shape=_sds((s, D), BF16))
    return _carried(call, (pa, pa, attn, convw), carry)


def _conv_merge_bwd(dmerged, pa, attn, convw, name, ts=256, carry=None):
    _, s, _ = pa.shape
    ts = min(ts, s)
    hb = ts // HALO
    last_hb = s // HALO - 1

    def body(dm_ref, pa_ref, at_ref, w_ref, hp_ref, hn_ref, dmn_ref, dat_ref, dpa_ref, dw_ref):
        i = pl.program_id(1)
        last = i == pl.num_programs(1) - 1
        dm = dm_ref[...].astype(F32)
        cp, bp, u = pa_ref[0].astype(F32), pa_ref[1].astype(F32), pa_ref[2].astype(F32)
        sa = jax.nn.sigmoid(pa_ref[3].astype(F32))
        sc = jax.nn.sigmoid(pa_ref[4].astype(F32))
        at = at_ref[...].astype(F32)
        cu = cp * u
        c1, c2 = _conv_taps(cu, hp_ref[0].astype(F32) * hp_ref[2].astype(F32), i == 0)
        w = w_ref[...]
        c3 = w[0:1] * c2 + w[1:2] * c1 + w[2:3] * cu
        dconv = dm * sc
        dc3 = dconv * bp
        nxt = dmn_ref[...].astype(F32) * jax.nn.sigmoid(hn_ref[4].astype(F32)) * hn_ref[1].astype(F32)
        nxt = jnp.where(last, 0.0, nxt)
        row = lax.broadcasted_iota(jnp.int32, dc3.shape, 0)
        d1 = jnp.where(row == ts - 1, nxt[0:1], pltpu.roll(dc3, ts - 1, 0))
        d2 = jnp.where(row == ts - 2, nxt[0:1], jnp.where(row == ts - 1, nxt[1:2], pltpu.roll(dc3, ts - 2, 0)))
        dcu = w[2:3] * dc3 + w[1:2] * d1 + w[0:1] * d2
        dat_ref[...] = (dm * sa).astype(BF16)
        dpa_ref[0] = (dcu * u).astype(BF16)
        dpa_ref[1] = (dconv * c3).astype(BF16)
        dpa_ref[2] = (dcu * cp).astype(BF16)
        dpa_ref[3] = (dm * at * sa * (1.0 - sa)).astype(BF16)
        dpa_ref[4] = (dm * bp * c3 * sc * (1.0 - sc)).astype(BF16)

        @pl.when(i == 0)
        def _():
            dw_ref[...] = jnp.zeros_like(dw_ref)

        dw_ref[0:1, :] += jnp.sum(dc3 * c2, axis=0, keepdims=True)
        dw_ref[1:2, :] += jnp.sum(dc3 * c1, axis=0, keepdims=True)
        dw_ref[2:3, :] += jnp.sum(dc3 * cu, axis=0, keepdims=True)

    call = _pcall(body, name=name, grid=(D // CW, s // ts), carry=carry,
                  in_specs=[pl.BlockSpec((ts, CW), lambda c, i: (i, c)),
                            pl.BlockSpec((5, ts, CW), lambda c, i: (0, i, c)),
                            pl.BlockSpec((ts, CW), lambda c, i: (i, c)),
                            pl.BlockSpec((8, CW), lambda c, i: (0, c)),
                            pl.BlockSpec((5, HALO, CW), lambda c, i: (0, jnp.maximum(i * hb - 1, 0), c)),
                            pl.BlockSpec((5, HALO, CW), lambda c, i: (0, jnp.minimum((i + 1) * hb, last_hb), c)),
                            pl.BlockSpec((HALO, CW), lambda c, i: (jnp.minimum((i + 1) * hb, last_hb), c))],
                  out_specs=[pl.BlockSpec((ts, CW), lambda c, i: (i, c)),
                             pl.BlockSpec((5, ts, CW), lambda c, i: (0, i, c)),
                             pl.BlockSpec((8, CW), lambda c, i: (0, c))],
                  out_shape=[_sds((s, D), BF16), _sds((5, s, D), BF16), _sds((8, D), F32)])
    return _carried(call, (dmerged, pa, attn, convw, pa, pa, dmerged), carry)


def _xattn_fwd(q, kv, name, tq=512):
    s, _ = q.shape
    nm = kv.shape[1]
    tq = min(tq, s)

    def body(q_ref, kv_ref, o_ref, lse_ref):
        lane = lax.broadcasted_iota(jnp.int32, (tq, 128), 1)
        lse_t = jnp.zeros((tq, 128), F32)
        for h in range(XH):
            hs = slice(XHD * h, XHD * (h + 1))
            sc = _dot_nt(q_ref[:, hs], kv_ref[h]) * (XHD ** -0.5)
            m = jnp.max(sc, axis=1, keepdims=True)
            p = jnp.exp(sc - m)
            l = jnp.sum(p, axis=1, keepdims=True)
            o_ref[:, hs] = (_dot(p.astype(BF16), kv_ref[XH + h]) * (1.0 / l)).astype(BF16)
            lse_t = jnp.where(lane == h, m + jnp.log(l), lse_t)
        lse_ref[...] = lse_t

    return _pcall(body, name=name, grid=(s // tq,),
                  in_specs=[pl.BlockSpec((tq, D), lambda i: (i, 0)), pl.BlockSpec((2 * XH, nm, XHD), lambda i: (0, 0, 0))],
                  out_specs=[pl.BlockSpec((tq, D), lambda i: (i, 0)), pl.BlockSpec((tq, 128), lambda i: (i, 0))],
                  out_shape=[_sds((s, D), BF16), _sds((s, 128), F32)])(q, kv)


def _xattn_bwd(q, kv, o, do, lse, name, tq=512, carry=None):
    s, _ = q.shape
    nm = kv.shape[1]
    tq = min(tq, s)

    def body(q_ref, kv_ref, o_ref, do_ref, lse_ref, dq_ref, dkv_ref):
        @pl.when(pl.program_id(0) == 0)
        def _():
            dkv_ref[...] = jnp.zeros_like(dkv_ref)

        for h in range(XH):
            hs = slice(XHD * h, XHD * (h + 1))
            qh, kh, vh, dob = q_ref[:, hs], kv_ref[h], kv_ref[XH + h], do_ref[:, hs]
            p = jnp.exp(_dot_nt(qh, kh) * (XHD ** -0.5) - lse_ref[:, h:h + 1])
            dp = _dot_nt(dob, vh)
            delta = jnp.sum(dob.astype(F32) * o_ref[:, hs].astype(F32), axis=1, keepdims=True)
            dsb = (p * (dp - delta) * (XHD ** -0.5)).astype(BF16)
            dq_ref[:, hs] = _dot(dsb, kh).astype(BF16)
            dkv_ref[h] += _dot_tn(dsb, qh)
            dkv_ref[XH + h] += _dot_tn(p.astype(BF16), dob)

    call = _pcall(body, name=name, grid=(s // tq,), carry=carry,
                  in_specs=[pl.BlockSpec((tq, D), lambda i: (i, 0)), pl.BlockSpec((2 * XH, nm, XHD), lambda i: (0, 0, 0)),
                            pl.BlockSpec((tq, D), lambda i: (i, 0)), pl.BlockSpec((tq, D), lambda i: (i, 0)),
                            pl.BlockSpec((tq, 128), lambda i: (i, 0))],
                  out_specs=[pl.BlockSpec((tq, D), lambda i: (i, 0)), pl.BlockSpec((2 * XH, nm, XHD), lambda i: (0, 0, 0))],
                  out_shape=[_sds((s, D), BF16), _sds((2 * XH, nm, XHD), F32)])
    return _carried(call, (q, kv, o, do, lse), carry)


def _ffn_down_bwd(dxb, wd4, gu4, name, tm=512, carry=None, behind=None):
    s, _ = dxb.shape
    tm = min(tm, s)

    def body(dx_ref, w_hbm, gu_ref, o_ref, w_ref, w_sem):
        _load_once(w_hbm, w_ref, w_sem)
        for p in range(4):
            for rows in _row_chunks(tm):
                da = _dot_nt(dx_ref[rows, :], w_ref[p])
                g = gu_ref[0, p, rows, :].astype(F32)
                u = gu_ref[1, p, rows, :].astype(F32)
                sg = jax.nn.sigmoid(g)
                t = da * sg
                o_ref[0, p, rows, :] = (t * u * (1.0 + g - g * sg)).astype(BF16)
                o_ref[1, p, rows, :] = (t * g).astype(BF16)

    block = pl.BlockSpec((2, 4, tm, FS), lambda i: (0, 0, i, 0))
    call = _pcall(body, name=name, grid=(s // tm,), carry=carry, behind=behind,
                  in_specs=[pl.BlockSpec((tm, D), lambda i: (i, 0)), HBM_SPEC, block],
                  out_specs=block, out_shape=_sds((2, 4, s, FS), BF16), scratch=_resident(wd4))
    return _carried(call, (dxb, wd4, gu4), carry)


def _mm_tn(a, b, name, scale=1.0, carry=None):
    pa_n, s, m = a.shape
    pb_n, _, n = b.shape
    po = max(pa_n, pb_n)
    tn = n if po >= 4 else min(n, 256)

    def body(a_ref, b_ref, o_ref):
        o_ref[...] = (scale * _dot_tn(a_ref[...], b_ref[...])).astype(BF16)

    call = _pcall(body, name=name, grid=(po, n // tn), carry=carry,
                  in_specs=[pl.BlockSpec((None, s, m), lambda o, j: (o if pa_n > 1 else 0, 0, 0)),
                            pl.BlockSpec((None, s, tn), lambda o, j: (o if pb_n > 1 else 0, 0, j))],
                  out_specs=pl.BlockSpec((None, m, tn), lambda o, j: (o, 0, j)),
                  out_shape=_sds((po, m, n), BF16))
    return _carried(call, (a, b), carry)


def _mm_tn_rows(a, b, name, total_rows, row0, begun=None, tm=512):
    p, s, m = a.shape
    n = b.shape[1]
    tm = min(tm, m)
    tiles = m // tm
    assert row0 % tm == 0 and m % tm == 0, (row0, m, tm)

    def body(a_ref, b_ref, *rest):
        rest[-1][...] = _dot_tn(a_ref[...], b_ref[...]).astype(BF16)

    in_specs = [pl.BlockSpec((None, s, tm), lambda o, i: (o, 0, i)), pl.BlockSpec((s, n), lambda o, i: (0, 0))]
    call = _pcall(body, name=name, grid=(p, tiles), in_specs=in_specs + ([HBM_SPEC] if begun is not None else []),
                  out_specs=pl.BlockSpec((tm, n), lambda o, i: (row0 // tm + o * tiles + i, 0)),
                  out_shape=_sds((total_rows, n), BF16), aliases={2: 0} if begun is not None else None)
    return call(a, b, begun) if begun is not None else call(a, b)


def _sum_dots(a_ref, b_ref, nj, bt, rows=slice(None)):
    dot = _dot_nt if bt else _dot
    acc = dot(a_ref[0, rows, :], b_ref[0])
    for j in range(1, nj):
        acc = acc + dot(a_ref[j, rows, :], b_ref[j])
    return acc


def _mm_acc(a, b, name, out_dtype, tm=512, bt=False, carry=None):
    nj, s, k = a.shape
    n = b.shape[1] if bt else b.shape[2]
    tm = min(tm, s)

    def body(a_ref, b_ref, o_ref):
        o_ref[...] = _sum_dots(a_ref, b_ref, nj, bt).astype(out_dtype)

    call = _pcall(body, name=name, grid=(s // tm,), carry=carry,
                  in_specs=[pl.BlockSpec((nj, tm, k), lambda i: (0, i, 0)),
                            pl.BlockSpec(b.shape, lambda i: (0, 0, 0))],
                  out_specs=pl.BlockSpec((tm, n), lambda i: (i, 0)), out_shape=_sds((s, n), out_dtype))
    return _carried(call, (a, b), carry)


def _rms_bwd_call(name, acts, weights, scratch, load, dh_rows, *, x, gain, dres, tm, carry, behind=None):
    s, n = x.shape
    tm = min(tm, s)
    n_act, n_w = len(acts), len(weights)

    def body(*refs):
        act_refs, w_refs = refs[:n_act], refs[n_act:n_act + n_w]
        x_ref, g_ref, r_ref, dx_ref, dxb_ref, dg_ref = refs[n_act + n_w:n_act + n_w + 6]
        held = refs[n_act + n_w + 6:]
        load(w_refs, held)

        @pl.when(pl.program_id(0) == 0)
        def _():
            dg_ref[...] = jnp.zeros_like(dg_ref)

        for rows in _row_chunks(tm):
            dh = dh_rows(act_refs, held, rows)
            xv = x_ref[rows, :]
            r = lax.rsqrt(jnp.mean(xv * xv, axis=-1, keepdims=True) + EPS)
            xh = xv * r
            dyg = dh * g_ref[...]
            dx = r_ref[rows, :] + r * (dyg - xh * jnp.mean(dyg * xh, axis=-1, keepdims=True))
            dx_ref[rows, :] = dx
            dxb_ref[rows, :] = dx.astype(BF16)
            dg_ref[...] += jnp.sum(dh * xh, axis=0, keepdims=True)

    def tile(a):
        return (pl.BlockSpec((tm, a.shape[1]), lambda i: (i, 0)) if a.ndim == 2
                else pl.BlockSpec((a.shape[0], tm, a.shape[2]), lambda i: (0, i, 0)))

    row = pl.BlockSpec((tm, n), lambda i: (i, 0))
    in_specs = [tile(a) for a in acts] + [HBM_SPEC] * n_w + [row, pl.BlockSpec((1, n), lambda i: (0, 0)), row]
    call = _pcall(body, name=name, grid=(s // tm,), in_specs=in_specs, carry=carry, behind=behind,
                  out_specs=[row, row, pl.BlockSpec((1, n), lambda i: (0, 0))],
                  out_shape=[_sds((s, n), F32), _sds((s, n), BF16), _sds((1, n), F32)], scratch=scratch)
    return _carried(call, tuple(acts) + tuple(weights) + (x, gain, dres), carry)


def _mm_acc_rms_bwd(a, b, name, *, x, gain, dres, scale=None, tm=512, bt=False, carry=None, behind=None):
    def load(w_refs, held):
        _load_once(w_refs[0], held[0], held[1])

    def dh_rows(act_refs, held, rows):
        dh = _sum_dots(act_refs[0], held[0], a.shape[0], bt, rows)
        return dh if scale is None else scale * dh

    return _rms_bwd_call(name, [a], [b], _resident(b), load, dh_rows, x=x, gain=gain, dres=dres, tm=tm, carry=carry,
                         behind=behind)


def _in_proj_bwd(dpa, dq, dkv, w_in_t, name, *, x, gain, dres, tm=512, carry=None, behind=None):
    def load(w_refs, held):
        _load_in_proj(w_refs[0], *held)

    def dh_rows(act_refs, held, rows):
        dpa_ref, dq_ref, dkv_ref = act_refs
        wq_ref, wkv_ref, wa_ref, _ = held
        dh = _dot(dq_ref[rows, :], wq_ref[...]) + _dot(dkv_ref[rows, :], wkv_ref[...])
        return dh + _sum_dots(dpa_ref, wa_ref, N_SEG, False, rows)

    return _rms_bwd_call(name, [dpa, dq, dkv], [w_in_t], IN_PROJ_WEIGHTS, load, dh_rows, x=x, gain=gain, dres=dres,
                         tm=tm, carry=carry, behind=behind)


def _adam(w, g, m, v):
    m2 = ADAM_B1 * m + (1.0 - ADAM_B1) * g
    v2 = ADAM_B2 * v + (1.0 - ADAM_B2) * (g * g)
    m_hat = m2 / (1.0 - ADAM_B1 ** ADAM_STEP)
    v_hat = v2 / (1.0 - ADAM_B2 ** ADAM_STEP)
    delta = -ADAM_LR * (m_hat / (jnp.sqrt(v_hat) + ADAM_EPS) + ADAM_WD * w)
    return delta, m2, v2


def _adamw(parts, w, m, v, name, behind=None):
    _, r, c = parts.shape
    tr = max(t for t in range(16, 257, 16) if r % t == 0)

    def body(p_ref, w_ref, m_ref, v_ref, g_ref, d_ref, m2_ref, v2_ref):
        g = p_ref[0].astype(F32)
        for i in range(1, N_DEV):
            g = g + p_ref[i].astype(F32)
        delta, m2, v2 = _adam(w_ref[...], g, m_ref[...], v_ref[...])
        g_ref[...] = g
        d_ref[...] = delta
        m2_ref[...] = m2
        v2_ref[...] = v2

    blk = pl.BlockSpec((tr, c), lambda i: (i, 0))
    return _pcall(body, name=name, grid=(r // tr,), behind=behind,
                  in_specs=[pl.BlockSpec((N_DEV, tr, c), lambda i: (0, i, 0)), blk, blk, blk],
                  out_specs=[blk] * 4, out_shape=[_sds((r, c), F32)] * 4)(parts, w, m, v)


def _position():
    return lax.axis_index("x"), lax.axis_index("y"), lax.axis_index("c")


def _slot(px, py, pc):
    return 4 * px + 2 * py + pc


def _row_window(ref, rows):
    r0, r1 = rows
    return ref if (r0, r1) == (0, ref.shape[0]) else ref.at[pl.ds(r0, r1 - r0)]


def _split_items(items):
    sources = [src for src, _, _ in items]
    begun = [(a, dest) for a, (_, _, dest) in enumerate(items) if dest is not None]
    aliases = {len(sources) + k: a for k, (a, _) in enumerate(begun)}
    return sources + [dest for _, dest in begun], [rows for _, rows, _ in items], aliases


def _gather_carry(items):
    na = len(items)
    carry_ins, windows, aliases = _split_items(items)

    def plan(ins, outs, sems):
        send_sems, recv_sems, local_sems = sems
        x, y, c = _position()
        me, sibling = (x, y, c), (x, y, 1 - c)
        chips = [(1 - x, y), (x, 1 - y), (1 - x, 1 - y)]
        ins = [_row_window(ins[a], windows[a]) for a in range(na)]

        def block_rows(a, block):
            return _row_window(outs[a].at[_slot(*block)], windows[a])

        def copy(a, k, block, to, src=None):
            rows = block_rows(a, block)
            return pltpu.make_async_remote_copy(src_ref=rows if src is None else src, dst_ref=rows,
                                                send_sem=send_sems.at[k, a], recv_sem=recv_sems.at[k, a],
                                                device_id=to, device_id_type=MESH)

        mine = [pltpu.make_async_copy(ins[a], block_rows(a, me), local_sems.at[a]) for a in range(na)]
        first = [copy(a, 0, me, sibling, src=ins[a]) for a in range(na)]
        for j, chip in enumerate(chips):
            first += [copy(a, 1 + j, me, (*chip, c), src=ins[a]) for a in range(na)]
        landed = [[copy(a, 1 + j, (*chip, c), me) for a in range(na)] for j, chip in enumerate(chips)]
        passed = [[copy(a, 4 + j, (*chip, c), sibling) for a in range(na)] for j, chip in enumerate(chips)]
        from_sibling = [copy(a, 0, sibling, me) for a in range(na)]
        for j, chip in enumerate(chips):
            from_sibling += [copy(a, 4 + j, (*chip, 1 - c), me) for a in range(na)]
        return mine, first, landed, passed, from_sibling

    def start(ins, outs, sems):
        mine, first, _, _, _ = plan(ins, outs, sems)
        for cp in mine + first:
            cp.start()

    def mid(ins, outs, sems):
        _, _, landed, passed, _ = plan(ins, outs, sems)
        for over_ici, onward in zip(landed, passed):
            for cp, fwd in zip(over_ici, onward):
                cp.wait_recv()
                fwd.start()

    def finish(ins, outs, sems):
        mine, first, _, passed, from_sibling = plan(ins, outs, sems)
        for cp in from_sibling:
            cp.wait_recv()
        for cp in first + [fwd for onward in passed for fwd in onward]:
            cp.wait_send()
        for cp in mine:
            cp.wait()

    return _Carry(carry_ins, [_sds((N_DEV,) + src.shape, src.dtype) for src, _, _ in items],
                  [pltpu.SemaphoreType.DMA((7, na)), pltpu.SemaphoreType.DMA((7, na)),
                   pltpu.SemaphoreType.DMA((na,))], start, finish, mid, aliases)


def _exchange_carry(scattered, replicated=()):
    items = list(scattered) + [(a, (0, a.shape[0]), None) for a in replicated]
    na, ns = len(items), len(scattered)
    carry_ins, windows, aliases = _split_items(items)

    def plan(ins, outs, sems):
        send_sems, recv_sems, local_sems = sems
        me = _slot(*_position())

        def source(a, j):
            return _row_window(ins[a].at[j] if a < ns else ins[a], windows[a])

        def copy(a, j, i):
            return pltpu.make_async_remote_copy(src_ref=source(a, j), dst_ref=_row_window(outs[a].at[i], windows[a]),
                                                send_sem=send_sems.at[j, a], recv_sem=recv_sems.at[i, a],
                                                device_id=(j >> 2, (j >> 1) & 1, j & 1), device_id_type=MESH)

        def own(a, j):
            return pltpu.make_async_copy(source(a, j), _row_window(outs[a].at[j], windows[a]), local_sems.at[a])

        return me, copy, own

    def start(ins, outs, sems):
        me, copy, own = plan(ins, outs, sems)
        for a in range(na):
            for j in range(N_DEV):
                @pl.when(me == j)
                def _():
                    own(a, j).start()

                @pl.when(me != j)
                def _():
                    copy(a, j, me).start()

    def finish(ins, outs, sems):
        me, copy, own = plan(ins, outs, sems)
        for a in range(na):
            for j in range(N_DEV):
                @pl.when(me == j)
                def _():
                    for i in range(N_DEV):
                        if i != j:
                            copy(a, j, i).wait_recv()
                    own(a, j).wait()

                @pl.when(me != j)
                def _():
                    copy(a, j, me).wait_send()

    return _Carry(carry_ins, [_sds((N_DEV,) + src.shape[-2:], src.dtype) for src, _, _ in items],
                  [pltpu.SemaphoreType.DMA((N_DEV, na)), pltpu.SemaphoreType.DMA((N_DEV, na)),
                   pltpu.SemaphoreType.DMA((na,))], start, finish, None, aliases)


HBM_ARRAY = pl.BlockSpec(memory_space=pltpu.HBM)
SEMAPHORES = pl.BlockSpec(memory_space=pltpu.SEMAPHORE)
DATAFLOW = pltpu.SideEffectType.DATAFLOW_SIDE_EFFECTING


def _exchange_copy(parts_ref, land_ref, send_sems, recv_sems, me, j):
    return pltpu.make_async_remote_copy(src_ref=parts_ref.at[j], dst_ref=land_ref.at[me], send_sem=send_sems.at[j],
                                        recv_sem=recv_sems.at[me], device_id=(j >> 2, (j >> 1) & 1, j & 1),
                                        device_id_type=MESH)


def _exchange_start(parts, name):
    def body(parts_ref, land_ref, send_sems, recv_sems, parts_thru, land_thru, token):
        me = _slot(*_position())
        for j in range(N_DEV):
            @pl.when(me == j)
            def _():
                pltpu.make_async_copy(parts_ref.at[j], land_ref.at[j], send_sems.at[j]).start()

            @pl.when(me != j)
            def _():
                _exchange_copy(parts_ref, land_ref, send_sems, recv_sems, me, j).start()
        token[...] = jnp.zeros_like(token)

    return pl.pallas_call(
        body, name=name,
        out_shape=(pltpu.SemaphoreType.DMA((N_DEV,)), pltpu.SemaphoreType.DMA((N_DEV,)),
                   pltpu.HBM(parts.shape, parts.dtype), pltpu.HBM(parts.shape, parts.dtype), _sds((8, 128), F32)),
        in_specs=(HBM_ARRAY, HBM_ARRAY),
        out_specs=(SEMAPHORES, SEMAPHORES, HBM_ARRAY, HBM_ARRAY, pl.BlockSpec(memory_space=pltpu.VMEM)),
        input_output_aliases={0: 2, 1: 3}, compiler_params=pltpu.CompilerParams(has_side_effects=DATAFLOW),
    )(pltpu.with_memory_space_constraint(parts, pltpu.HBM),
      pltpu.with_memory_space_constraint(lax.empty(parts.shape, parts.dtype), pltpu.HBM))


def _exchange_wait(send_sems, recv_sems, parts_thru, land_thru, after, name):
    def body(parts_ref, land_ref, send_sems, recv_sems, after_ref, parts_dead, got_ref):
        me = _slot(*_position())
        for j in range(N_DEV):
            @pl.when(me == j)
            def _():
                pltpu.make_async_copy(parts_ref.at[j], land_ref.at[j], send_sems.at[j]).wait()

            @pl.when(me != j)
            def _():
                both = pltpu.make_async_remote_copy(src_ref=parts_ref.at[j], dst_ref=land_ref.at[j],
                                                    send_sem=send_sems.at[j], recv_sem=recv_sems.at[j],
                                                    device_id=(j >> 2, (j >> 1) & 1, j & 1), device_id_type=MESH)
                both.wait_send()
                both.wait_recv()

    return pl.pallas_call(
        body, name=name, out_shape=(pltpu.HBM(parts_thru.shape, parts_thru.dtype),
                                    pltpu.HBM(parts_thru.shape, parts_thru.dtype)),
        in_specs=(HBM_ARRAY, HBM_ARRAY, SEMAPHORES, SEMAPHORES, pl.BlockSpec(memory_space=pl.ANY)),
        out_specs=(HBM_ARRAY, HBM_ARRAY), input_output_aliases={0: 0, 1: 1},
        compiler_params=pltpu.CompilerParams(has_side_effects=DATAFLOW),
    )(parts_thru, land_thru, send_sems, recv_sems, after)[1]


class _Mesh:
    def __init__(self, shards):
        self.shards, self.full, self.received, self.cache, self.pending, self.tokens = shards, {}, {}, {}, {}, {}

    def fetch(self, wanted):
        items = []
        for want in wanted:
            name, r0, r1 = want if isinstance(want, tuple) else (want, 0, self.shards[want].shape[0])
            items.append((self.shards[name], (r0, r1), self.full.get(name)))
        return _gather_carry(items)

    def fetched(self, wanted, results):
        self.full.update(zip([want[0] if isinstance(want, tuple) else want for want in wanted], results))

    def send(self, *payloads):
        return _exchange_carry([(parts, rows or (0, parts.shape[1]), self.received.get(name))
                                for name, parts, rows in payloads])

    def sent(self, names, results):
        self.received.update(zip(names, results))

    def send_apart(self, name, parts):
        *self.pending[name], self.tokens[name] = _exchange_start(parts, "exchange_" + name + "_start")
        return self.tokens[name]

    def sent_apart(self, name, after):
        self.received[name] = _exchange_wait(*self.pending.pop(name), after, "exchange_" + name + "_wait")

    def w(self, key):
        if key not in self.cache:
            self.cache[key] = self._layout(key)
        return self.cache[key]

    def _layout(self, key):
        if key in ("gu1", "gu2"):
            return self.full[key]
        if key in ("d1", "d2"):
            return self.full[key].reshape(4, FS, D)
        if key in ("out", "q", "o"):
            return self.full[key].reshape(D, D)
        if key == "kv":
            return self.full["kv"]
        if key == "convw":
            rows = self.full["conv"][:, :3, :].transpose(1, 0, 2).reshape(3, D)
            return jnp.concatenate([rows, jnp.zeros((5, D), F32)], axis=0)
        assert key == "win_t", key
        return self.full["win"].reshape(-1, D)


def _forward_backward(x, mem, target, g, rel_bias, sinks, ex):
    s = x.shape[0]
    def fetching(wanted, call, *args, **kw):
        res, got = call(*args, carry=ex.fetch(wanted), **kw)
        ex.fetched(wanted, got)
        return res

    h1 = fetching(["gu1", "conv"], _rmsnorm, x, g["ffn1"], "norm_ffn1")
    gu1, a1 = fetching(["d1", ("win", 0, 400)], _ffn_up, h1, ex.w("gu1").reshape(2, 4, FS, D), "ffn1_up")
    x1, h2 = fetching([("win", 400, 832)], _mm_res_norm, a1, ex.w("d1"), x, g["mix"], 0.5, "ffn1_down")
    pa, q, kv = fetching(["gu2"], _in_proj, h2, ex.w("win_t"), "in_proj")
    biasm = _bias_build(rel_bias, "bias_build")
    attn, lse = fetching(["out", "kv", "o"], _swa_fwd, q, kv, biasm, sinks, "swa_fwd")
    merged = fetching(["q"], _conv_merge_fwd, pa, attn, ex.w("convw"), "conv_merge_fwd")
    (x2, h3), _ = _mm_res_norm(merged[None], ex.w("out")[None], x1, g["xattn"], 1.0, "out_proj")
    q2 = _mm_nn(h3, ex.w("q")[None], "xattn_q")[0][0]
    mh, _ = _rmsnorm(mem, g["mem"], "norm_mem")
    kv2 = _mm_nn(mh, ex.w("kv"), "xattn_kv")[0]
    o, lse2 = _xattn_fwd(q2, kv2, "xattn_fwd")
    (x3, h4), _ = _mm_res_norm(o[None], ex.w("o")[None], x2, g["ffn2"], 1.0, "xattn_o")
    gu2, a2 = fetching(["d2"], _ffn_up, h4, ex.w("gu2").reshape(2, 4, FS, D), "ffn2_up")
    dx4, dx4b, loss, d_final = _ffn_down_loss(a2, ex.w("d2"), x3, g["final"], target, "ffn2_down_loss")
    def sending(payloads, call, *args, **kw):
        res, got = call(*args, carry=ex.send(*payloads), **kw)
        ex.sent([name for name, _, _ in payloads], got)
        return res

    dw_d2 = _mm_tn(a2, dx4b[None], "dw_ffn2_down", scale=0.5)[0].reshape(N_DEV, -1, D)
    dgu2 = sending([("d2", dw_d2, None)], _ffn_down_bwd, dx4b, ex.w("d2"), gu2, "ffn2_down_bwd").reshape(8, s, FS)
    dw_gu2 = _mm_tn(dgu2, h4[None], "dw_ffn2_up", scale=0.5)[0]
    dx3, dx3b, d_ffn2 = sending([("gu2", dw_gu2, (0, 400))], _mm_acc_rms_bwd, dgu2, ex.w("gu2"), "ffn2_up_bwd",
                                x=x3, gain=g["ffn2"], dres=dx4, scale=0.5)
    do, _ = _mm_acc(dx3b[None], ex.w("o")[None], "xattn_o_bwd", BF16, bt=True)
    dw_o = _mm_tn(o[None], dx3b[None], "dw_xattn_o")[0].reshape(N_DEV, -1, D)
    dq2, dkv2 = sending([("o", dw_o, None)], _xattn_bwd, q2, kv2, o, do, lse2, "xattn_bwd")
    dkv2b = dkv2.astype(BF16)
    dw_q = _mm_tn(h3[None], dq2[None], "dw_xattn_q")[0].reshape(N_DEV, -1, D)
    dx2, dx2b, d_xattn = sending([("q", dw_q, None)], _mm_acc_rms_bwd, dq2[None], ex.w("q")[None],
                                 "xattn_q_bwd", x=x2, gain=g["xattn"], dres=dx3, bt=True)
    dw_kv = _mm_tn(mh[None], dkv2b, "dw_xattn_kv")[0]
    (_, _, d_mem), _ = _mm_acc_rms_bwd(dkv2b, ex.w("kv"), "xattn_kv_bwd", x=mem, gain=g["mem"],
                                       dres=jnp.zeros_like(mem), bt=True)
    dmerged, _ = _mm_acc(dx2b[None], ex.w("out")[None], "out_proj_bwd", BF16, bt=True)
    dw_out = _mm_tn(merged[None], dx2b[None], "dw_out_proj")[0].reshape(N_DEV, -1, D)
    dattn, dpa, d_convw = sending([("kv", dw_kv, None)], _conv_merge_bwd,
                                  dmerged, pa, attn, ex.w("convw"), "conv_merge_bwd")
    dq, dkv, dbias, d_sinks = sending([("gu2", dw_gu2, (400, FS)), ("out", dw_out, None)], _swa_bwd,
                                      q, kv, attn, dattn, lse, biasm, sinks, "swa_bwd")
    d_relb = _bias_bwd(dbias, "bias_bwd")
    w_rows = ex.w("win_t").shape[0]
    dw_in = _mm_tn_rows(dpa, h2, "dw_in_proj_a", w_rows, NQ + NKV)
    dw_in = _mm_tn_rows(dq[None], h2, "dw_in_proj_q", w_rows, 0, begun=dw_in)
    dw_in = _mm_tn_rows(dkv[None], h2, "dw_in_proj_kv", w_rows, NQ, begun=dw_in).reshape(N_DEV, -1, D)
    (dx1, dx1b, d_mix), _ = _in_proj_bwd(dpa, dq, dkv, ex.w("win_t"), "in_proj_bwd", x=x1, gain=g["mix"], dres=dx2,
                                         behind=ex.send_apart("win", dw_in))
    dw_d1 = _mm_tn(a1, dx1b[None], "dw_ffn1_down", scale=0.5)[0].reshape(N_DEV, -1, D)
    dgu1 = _ffn_down_bwd(dx1b, ex.w("d1"), gu1, "ffn1_down_bwd", behind=ex.send_apart("d1", dw_d1))[0]
    dgu1 = dgu1.reshape(8, s, FS)
    dw_gu1 = _mm_tn(dgu1, h1[None], "dw_ffn1_up", scale=0.5)[0]
    (dx0, _, d_ffn1), _ = _mm_acc_rms_bwd(dgu1, ex.w("gu1"), "ffn1_up_bwd", x=x, gain=g["ffn1"], dres=dx1,
                                          scale=0.5, behind=ex.send_apart("gu1", dw_gu1))

    relb_row = jnp.concatenate([d_relb[:, :REL_BUCKETS].T.reshape(1, REL_BUCKETS * N_HEADS), d_sinks[:, :N_HEADS],
                                jnp.zeros((1, D - REL_BUCKETS * N_HEADS - N_HEADS), F32)], axis=1)
    loss_row = jnp.concatenate([loss[0:1, 0:1], jnp.zeros((1, D - 1), F32)], axis=1)
    small = jnp.concatenate([d_ffn1, d_mix, d_xattn, d_mem, d_ffn2, d_final, relb_row, loss_row, d_convw[0:3],
                             jnp.zeros((SMALL_ROWS - ROW_CONV - 3, D), F32)], axis=0)
    return dx0, small


def _pack_small(norms, final, relb, sinks, conv_local, me):
    relb_row = jnp.concatenate([relb.reshape(1, -1), sinks.reshape(1, -1),
                                jnp.zeros((1, D - REL_BUCKETS * N_HEADS - N_HEADS), F32)], axis=1)
    conv_rows = lax.dynamic_update_slice(jnp.zeros((3, D), F32), conv_local.reshape(3, -1), (0, 128 * me))
    return jnp.concatenate(list(norms) + [final.reshape(1, D), relb_row, jnp.zeros((1, D), F32), conv_rows,
                                          jnp.zeros((SMALL_ROWS - ROW_CONV - 3, D), F32)], axis=0)


def kernel(x, mem, positions, rel_bias, ffn1_norm, ffn1_w_gu, ffn1_w_down, mix_norm, w_in, sinks, conv_w, w_out, xattn_norm, mem_norm, xattn_wq, xattn_wkv, xattn_wo, ffn2_norm, ffn2_w_gu, ffn2_w_down, final_norm, loss_target, m_rel_bias, m_ffn1_norm, m_ffn1_w_gu, m_ffn1_w_down, m_mix_norm, m_w_in, m_sinks, m_conv_w, m_w_out, m_xattn_norm, m_mem_norm, m_xattn_wq, m_xattn_wkv, m_xattn_wo, m_ffn2_norm, m_ffn2_w_gu, m_ffn2_w_down, m_final_norm, v_rel_bias, v_ffn1_norm, v_ffn1_w_gu, v_ffn1_w_down, v_mix_norm, v_w_in, v_sinks, v_conv_w, v_w_out, v_xattn_norm, v_mem_norm, v_xattn_wq, v_xattn_wkv, v_xattn_wo, v_ffn2_norm, v_ffn2_w_gu, v_ffn2_w_down, v_final_norm):
    del positions
    me = _slot(*_position())
    big = dict(gu1=(ffn1_w_gu, m_ffn1_w_gu, v_ffn1_w_gu), d1=(ffn1_w_down, m_ffn1_w_down, v_ffn1_w_down),
               win=(w_in, m_w_in, v_w_in), out=(w_out, m_w_out, v_w_out), q=(xattn_wq, m_xattn_wq, v_xattn_wq),
               kv=(xattn_wkv, m_xattn_wkv, v_xattn_wkv), o=(xattn_wo, m_xattn_wo, v_xattn_wo),
               gu2=(ffn2_w_gu, m_ffn2_w_gu, v_ffn2_w_gu), d2=(ffn2_w_down, m_ffn2_w_down, v_ffn2_w_down))
    order = list(big)
    transposed = ("gu1", "gu2", "win")
    local = {k: tuple(t[0].T if k in transposed else t[0] for t in big[k]) for k in order}
    shards = {k: local[k][0].astype(BF16) for k in order}
    shards["conv"] = jnp.concatenate([conv_w[0], jnp.zeros((5, 128), F32)], axis=0)
    ex = _Mesh(shards)
    gains = dict(ffn1=ffn1_norm, mix=mix_norm, xattn=xattn_norm, mem=mem_norm, ffn2=ffn2_norm,
                 final=final_norm.reshape(1, D))
    dx, small = _forward_backward(x[0], mem[0], loss_target[0], gains, rel_bias, sinks, ex)
    apart = ("d1", "win", "gu1")
    big_out = {k: _adamw(ex.received[k], *local[k], "adamw_" + k, behind=ex.tokens["gu1"])
               for k in order if k not in apart}
    for k in apart[:-1]:
        ex.sent_apart(k, after=sum(big_out[j][1][0:1, 0:1] for j in big_out))
        big_out[k] = _adamw(ex.received[k], *local[k], "adamw_" + k)
    spare = sum(big_out[k][1][0:1, 0:1] for k in big_out)
    small = lax.dynamic_update_slice(small, spare, (SMALL_ROWS - 1, 0))
    small_parts = _run_alone(_exchange_carry([], [small]), "exchange_small")[0]
    packed = [_pack_small(norms, final, relb, sk, conv, me) for norms, final, relb, sk, conv in (
        ((ffn1_norm, mix_norm, xattn_norm, mem_norm, ffn2_norm), final_norm, rel_bias, sinks, conv_w),
        ((m_ffn1_norm, m_mix_norm, m_xattn_norm, m_mem_norm, m_ffn2_norm), m_final_norm, m_rel_bias, m_sinks, m_conv_w),
        ((v_ffn1_norm, v_mix_norm, v_xattn_norm, v_mem_norm, v_ffn2_norm), v_final_norm, v_rel_bias, v_sinks, v_conv_w))]
    small_out = _adamw(small_parts, *packed, "adamw_small")
    done = [dx[0:1, 0:1], small_out[1][0:1, 0:1]] + [big_out[k][1][0:1, 0:1] for k in big_out]
    ex.sent_apart("gu1", after=sum(done))
    big_out["gu1"] = _adamw(ex.received["gu1"], *local["gu1"], "adamw_gu1")
    big_out = {k: [t.T if k in transposed else t for t in big_out[k]] for k in order}

    def unpack(t):
        conv = lax.dynamic_slice(t[ROW_CONV:ROW_CONV + 3], (0, 128 * me), (3, 128))[None]
        nrel = REL_BUCKETS * N_HEADS
        return dict(ffn1_norm=t[0:1], mix_norm=t[1:2], xattn_norm=t[2:3], mem_norm=t[3:4], ffn2_norm=t[4:5],
                    final_norm=t[5], rel_bias=t[ROW_RELB, :nrel].reshape(REL_BUCKETS, N_HEADS),
                    sinks=t[ROW_RELB:ROW_RELB + 1, nrel:nrel + N_HEADS], conv_w=conv)

    names = dict(gu1="ffn1_w_gu", d1="ffn1_w_down", win="w_in", out="w_out", q="xattn_wq", kv="xattn_wkv",
                 o="xattn_wo", gu2="ffn2_w_gu", d2="ffn2_w_down")
    results = []
    for idx in range(4):
        leaves = unpack(small_out[idx])
        leaves.update({names[k]: big_out[k][idx][None] for k in order})
        results.append(leaves)
    weights = ("rel_bias", "ffn1_norm", "ffn1_w_gu", "ffn1_w_down", "mix_norm", "w_in", "sinks", "conv_w", "w_out",
               "xattn_norm", "mem_norm", "xattn_wq", "xattn_wkv", "xattn_wo", "ffn2_norm", "ffn2_w_gu", "ffn2_w_down",
               "final_norm")
    loss = small_out[0][ROW_LOSS, 0]
    return (loss, dx[None], *[leaves[n] for leaves in results for n in weights])
```

```python
import math

import numpy as np
import jax
import jax.numpy as jnp
from jax import lax
from jax.experimental import pallas as pl
from jax.experimental.pallas import tpu as pltpu

F32, BF16 = jnp.float32, jnp.bfloat16
MESH = pl.DeviceIdType.MESH

D = 1024
N_DEV = 8
D_FF = 2816
FS = D_FF // 4
HEAD = 64
N_HEADS, N_KV = 16, 4
BLK = 128
NQ, NKV = N_HEADS * HEAD, 2 * N_KV * HEAD
XH, XHD = 4, 256
REL_BUCKETS, REL_EXACT, REL_MAX_DIST = 32, 16, 128
EPS, NEG = 1e-6, -1e30
ADAM_LR, ADAM_B1, ADAM_B2, ADAM_EPS, ADAM_WD, ADAM_STEP = 0.001, 0.9, 0.999, 1e-08, 0.01, 10
VMEM_LIMIT_V7X = 56 * 2**20
SMALL_ROWS = 16
ROW_RELB, ROW_LOSS, ROW_CONV = 6, 7, 8


def _bucket_thresholds():
    n = np.arange(REL_MAX_DIST)
    nf = np.maximum(n, 1).astype(np.float32)
    large = REL_EXACT + (np.log(nf / np.float32(REL_EXACT)) / np.float32(math.log(REL_MAX_DIST / REL_EXACT))
                         * np.float32(REL_BUCKETS - REL_EXACT)).astype(np.int32)
    b = np.where(n < REL_EXACT, n, np.minimum(large, REL_BUCKETS - 1))
    return [int(np.argmax(b >= REL_EXACT + k)) for k in range(1, REL_BUCKETS - REL_EXACT)]


BUCKET_THRESHOLDS = _bucket_thresholds()


HBM_SPEC = pl.BlockSpec(memory_space=pl.ANY)


class _Carry:
    def __init__(self, ins, outs, sems, start, finish, mid=None, aliases=None):
        self.ins, self.outs, self.sems = list(ins), list(outs), list(sems)
        self.start, self.finish, self.mid, self.aliases = start, finish, mid, dict(aliases or {})


def _pcall(body, *, name, grid, in_specs, out_specs, out_shape, scratch=(), carry=None, aliases=None, behind=None):
    params = pltpu.CompilerParams(dimension_semantics=("arbitrary",) * len(grid), vmem_limit_bytes=VMEM_LIMIT_V7X)
    if carry is None and behind is not None:
        n_in = len(in_specs)
        call = pl.pallas_call(lambda *refs: body(*refs[:n_in], *refs[n_in + 1:]), name=name, grid=grid,
                              in_specs=list(in_specs) + [pl.BlockSpec((8, 128), lambda *_: (0, 0))],
                              out_specs=out_specs, out_shape=out_shape, scratch_shapes=list(scratch),
                              compiler_params=params, input_output_aliases=aliases or {})
        return lambda *args: call(*args, behind)
    if carry is None:
        return pl.pallas_call(body, name=name, grid=grid, in_specs=in_specs, out_specs=out_specs,
                              out_shape=out_shape, scratch_shapes=list(scratch), compiler_params=params,
                              input_output_aliases=aliases or {})
    assert aliases is None and behind is None, name
    single = not isinstance(out_shape, (list, tuple))
    own_specs, own_shapes = ([out_specs], [out_shape]) if single else (list(out_specs), list(out_shape))
    n_in, n_out, n_scr = len(in_specs), len(own_shapes), len(scratch)
    n_cin, n_cout = len(carry.ins), len(carry.outs)
    steps = math.prod(grid)
    mid_step = max(steps - 1 - max(steps // 8, 1), 0)

    def carrying(*refs):
        ins, refs = refs[:n_in], refs[n_in:]
        cins, refs = refs[:n_cin], refs[n_cin:]
        outs, refs = refs[:n_out], refs[n_out:]
        couts, refs = refs[:n_cout], refs[n_cout:]
        scr, csems = refs[:n_scr], refs[n_scr:]
        step = 0
        for axis, size in enumerate(grid):
            step = step * size + pl.program_id(axis)

        @pl.when(step == 0)
        def _():
            carry.start(cins, couts, csems)

        body(*ins, *outs, *scr)
        if carry.mid is not None:
            @pl.when(step == mid_step)
            def _():
                carry.mid(cins, couts, csems)

        @pl.when(step == steps - 1)
        def _():
            carry.finish(cins, couts, csems)

    call = pl.pallas_call(carrying, name=name, grid=grid, in_specs=list(in_specs) + [HBM_SPEC] * n_cin,
                          out_specs=own_specs + [HBM_SPEC] * n_cout, out_shape=own_shapes + carry.outs,
                          scratch_shapes=list(scratch) + carry.sems, compiler_params=params,
                          input_output_aliases={n_in + i: n_out + o for i, o in carry.aliases.items()})

    def run(*args):
        res = call(*args, *carry.ins)
        return (res[0] if single else res[:n_out]), res[n_out:]

    return run


def _run_alone(carry, name):
    n_cin, n_cout = len(carry.ins), len(carry.outs)

    def body(*refs):
        cins, couts, csems = refs[:n_cin], refs[n_cin:n_cin + n_cout], refs[n_cin + n_cout:]
        carry.start(cins, couts, csems)
        if carry.mid is not None:
            carry.mid(cins, couts, csems)
        carry.finish(cins, couts, csems)

    return pl.pallas_call(body, name=name, in_specs=[HBM_SPEC] * n_cin, out_specs=[HBM_SPEC] * n_cout,
                          out_shape=carry.outs, scratch_shapes=carry.sems,
                          input_output_aliases=carry.aliases)(*carry.ins)


def _dot(a, b):
    return jnp.dot(a, b, preferred_element_type=F32)


def _dot_nt(a, b):
    return lax.dot_general(a, b, (((1,), (1,)), ((), ())), preferred_element_type=F32)


def _dot_tn(a, b):
    return lax.dot_general(a, b, (((0,), (0,)), ((), ())), preferred_element_type=F32)


def _sds(shape, dtype):
    return jax.ShapeDtypeStruct(tuple(shape), dtype)


ROW_CHUNK = 256


def _row_chunks(tm):
    return [slice(r, min(r + ROW_CHUNK, tm)) for r in range(0, tm, ROW_CHUNK)]


def _carried(call, args, carry):
    return call(*args) if carry is not None else (call(*args), ())


def _rmsnorm(x, g, name, carry=None):
    m, d = x.shape
    tm = min(512, m)

    def body(x_ref, g_ref, h_ref):
        xv = x_ref[...]
        r = lax.rsqrt(jnp.mean(xv * xv, axis=-1, keepdims=True) + EPS)
        h_ref[...] = (xv * r * g_ref[...]).astype(BF16)

    call = _pcall(body, name=name, grid=(m // tm,), carry=carry,
                  in_specs=[pl.BlockSpec((tm, d), lambda i: (i, 0)), pl.BlockSpec((1, d), lambda i: (0, 0))],
                  out_specs=pl.BlockSpec((tm, d), lambda i: (i, 0)), out_shape=_sds((m, d), BF16))
    return _carried(call, (x, g), carry)


def _mm_nn(a, b, name, tm=1024, bt=False, carry=None):
    m, k = a.shape
    nj = b.shape[0]
    n = b.shape[1] if bt else b.shape[2]
    tm = min(tm, m)
    dot = _dot_nt if bt else _dot

    def body(a_ref, b_ref, o_ref):
        o_ref[...] = dot(a_ref[...], b_ref[...]).astype(BF16)

    call = _pcall(body, name=name, grid=(nj, m // tm),
                  in_specs=[pl.BlockSpec((tm, k), lambda j, i: (i, 0)),
                            pl.BlockSpec((None,) + b.shape[1:], lambda j, i: (j, 0, 0))],
                  out_specs=pl.BlockSpec((None, tm, n), lambda j, i: (j, i, 0)),
                  out_shape=_sds((nj, m, n), BF16), carry=carry)
    return _carried(call, (a, b), carry)


def _load_once(src_hbm, dst_vmem, sem):
    @pl.when(pl.program_id(0) == 0)
    def _():
        load = pltpu.make_async_copy(src_hbm, dst_vmem, sem)
        load.start()
        load.wait()


def _resident(w):
    return [pltpu.VMEM(w.shape, w.dtype), pltpu.SemaphoreType.DMA(())]


def _ffn_up(h, w4, name, tm=512, carry=None):
    s, d = h.shape
    tm = min(tm, s)

    def body(h_ref, w_hbm, gu_ref, a_ref, w_ref, w_sem):
        _load_once(w_hbm, w_ref, w_sem)
        for p in range(4):
            for rows in _row_chunks(tm):
                hv = h_ref[rows, :]
                g = _dot_nt(hv, w_ref[0, p])
                u = _dot_nt(hv, w_ref[1, p])
                gu_ref[0, p, rows, :] = g.astype(BF16)
                gu_ref[1, p, rows, :] = u.astype(BF16)
                a_ref[p, rows, :] = (g * jax.nn.sigmoid(g) * u).astype(BF16)

    call = _pcall(body, name=name, grid=(s // tm,),
                  in_specs=[pl.BlockSpec((tm, d), lambda i: (i, 0)), HBM_SPEC],
                  out_specs=[pl.BlockSpec((2, 4, tm, FS), lambda i: (0, 0, i, 0)),
                             pl.BlockSpec((4, tm, FS), lambda i: (0, i, 0))],
                  out_shape=[_sds((2, 4, s, FS), BF16), _sds((4, s, FS), BF16)], scratch=_resident(w4), carry=carry)
    return _carried(call, (h, w4), carry)


N_SEG = 5
IN_PROJ_WEIGHTS = [pltpu.VMEM((NQ, D), BF16), pltpu.VMEM((NKV, D), BF16), pltpu.VMEM((N_SEG, D, D), BF16),
                   pltpu.SemaphoreType.DMA((2 + N_SEG,))]


def _load_in_proj(w_hbm, wq_ref, wkv_ref, wa_ref, sems):
    @pl.when(pl.program_id(0) == 0)
    def _():
        loads = [pltpu.make_async_copy(w_hbm.at[pl.ds(0, NQ)], wq_ref, sems.at[0]),
                 pltpu.make_async_copy(w_hbm.at[pl.ds(NQ, NKV)], wkv_ref, sems.at[1])]
        loads += [pltpu.make_async_copy(w_hbm.at[pl.ds(NQ + NKV + D * j, D)], wa_ref.at[j], sems.at[2 + j])
                  for j in range(N_SEG)]
        for load in loads:
            load.start()
        for load in loads:
            load.wait()


def _in_proj(h, w_in_t, name, tm=512, carry=None):
    s, d = h.shape
    tm = min(tm, s)

    def body(h_ref, w_hbm, pa_ref, q_ref, kv_ref, wq_ref, wkv_ref, wa_ref, sems):
        _load_in_proj(w_hbm, wq_ref, wkv_ref, wa_ref, sems)
        hv = h_ref[...]
        q_ref[...] = _dot_nt(hv, wq_ref[...]).astype(BF16)
        kv_ref[...] = _dot_nt(hv, wkv_ref[...]).astype(BF16)
        for j in range(N_SEG):
            pa_ref[j] = _dot_nt(hv, wa_ref[j]).astype(BF16)

    call = _pcall(body, name=name, grid=(s // tm,), carry=carry,
                  in_specs=[pl.BlockSpec((tm, d), lambda i: (i, 0)), HBM_SPEC],
                  out_specs=[pl.BlockSpec((N_SEG, tm, d), lambda i: (0, i, 0)),
                             pl.BlockSpec((tm, NQ), lambda i: (i, 0)), pl.BlockSpec((tm, NKV), lambda i: (i, 0))],
                  out_shape=[_sds((N_SEG, s, d), BF16), _sds((s, NQ), BF16), _sds((s, NKV), BF16)],
                  scratch=IN_PROJ_WEIGHTS)
    return _carried(call, (h, w_in_t), carry)


def _mm_res_norm(a, w, xres, gain, scale, name, tm=512, carry=None):
    npart, s, kp = a.shape
    tm = min(tm, s)

    def body(a_ref, w_ref, x_ref, g_ref, xo_ref, h_ref):
        for rows in _row_chunks(tm):
            acc = _dot(a_ref[0, rows, :], w_ref[0])
            for p in range(1, npart):
                acc = acc + _dot(a_ref[p, rows, :], w_ref[p])
            xn = x_ref[rows, :] + scale * acc
            xo_ref[rows, :] = xn
            r = lax.rsqrt(jnp.mean(xn * xn, axis=-1, keepdims=True) + EPS)
            h_ref[rows, :] = (xn * r * g_ref[...]).astype(BF16)

    call = _pcall(body, name=name, grid=(s // tm,),
                  in_specs=[pl.BlockSpec((npart, tm, kp), lambda i: (0, i, 0)),
                            pl.BlockSpec((npart, kp, D), lambda i: (0, 0, 0)),
                            pl.BlockSpec((tm, D), lambda i: (i, 0)),
                            pl.BlockSpec((1, D), lambda i: (0, 0))],
                  out_specs=[pl.BlockSpec((tm, D), lambda i: (i, 0)), pl.BlockSpec((tm, D), lambda i: (i, 0))],
                  out_shape=[_sds((s, D), F32), _sds((s, D), BF16)], carry=carry)
    return _carried(call, (a, w, xres, gain), carry)


def _ffn_down_loss(a, w, xres, gain, target, name, tm=512):
    npart, s, kp = a.shape
    tm = min(tm, s)

    def body(a_ref, w_ref, x_ref, g_ref, t_ref, dx_ref, dxb_ref, loss_ref, dg_ref):
        @pl.when(pl.program_id(0) == 0)
        def _():
            loss_ref[...] = jnp.zeros_like(loss_ref)
            dg_ref[...] = jnp.zeros_like(dg_ref)

        for rows in _row_chunks(tm):
            acc = _dot(a_ref[0, rows, :], w_ref[0])
            for p in range(1, npart):
                acc = acc + _dot(a_ref[p, rows, :], w_ref[p])
            xn = x_ref[rows, :] + 0.5 * acc
            r = lax.rsqrt(jnp.mean(xn * xn, axis=-1, keepdims=True) + EPS)
            xh = xn * r
            gv = g_ref[...]
            err = xh * gv - t_ref[rows, :]
            part = 0.5 * jnp.sum(jnp.mean(err * err, axis=-1, keepdims=True), axis=0, keepdims=True)
            dy = err * (1.0 / D)
            dyg = dy * gv
            dxn = r * (dyg - xh * jnp.mean(dyg * xh, axis=-1, keepdims=True))
            dx_ref[rows, :] = dxn
            dxb_ref[rows, :] = dxn.astype(BF16)
            loss_ref[...] += jnp.broadcast_to(part, loss_ref.shape)
            dg_ref[...] += jnp.sum(dy * xh, axis=0, keepdims=True)

    return _pcall(body, name=name, grid=(s // tm,),
                  in_specs=[pl.BlockSpec((npart, tm, kp), lambda i: (0, i, 0)),
                            pl.BlockSpec((npart, kp, D), lambda i: (0, 0, 0)),
                            pl.BlockSpec((tm, D), lambda i: (i, 0)),
                            pl.BlockSpec((1, D), lambda i: (0, 0)),
                            pl.BlockSpec((tm, D), lambda i: (i, 0))],
                  out_specs=[pl.BlockSpec((tm, D), lambda i: (i, 0)), pl.BlockSpec((tm, D), lambda i: (i, 0)),
                             pl.BlockSpec((8, 128), lambda i: (0, 0)), pl.BlockSpec((1, D), lambda i: (0, 0))],
                  out_shape=[_sds((s, D), F32), _sds((s, D), BF16), _sds((8, 128), F32), _sds((1, D), F32)],
                  )(a, w, xres, gain, target)


def _window_tiles():
    i = lax.broadcasted_iota(jnp.int32, (BLK, BLK), 0)
    j = lax.broadcasted_iota(jnp.int32, (BLK, BLK), 1)
    rel = (i - j) & (BLK - 1)
    large = jnp.full_like(rel, REL_EXACT)
    for t in BUCKET_THRESHOLDS:
        large = large + (rel >= t).astype(jnp.int32)
    return j <= i, jnp.where(rel < REL_EXACT, rel, large)


def _bias_build(rel_bias, name):
    def body(rb_ref, o_ref):
        _, bucket = _window_tiles()

        def per_head(h, carry):
            acc = jnp.zeros((BLK, BLK), F32)
            for b in range(REL_BUCKETS):
                acc = jnp.where(bucket == b, rb_ref[b, h], acc)
            o_ref[h] = acc
            return carry

        lax.fori_loop(0, N_HEADS, per_head, 0)

    return _pcall(body, name=name, grid=(1,),
                  in_specs=[pl.BlockSpec(memory_space=pltpu.SMEM)],
                  out_specs=pl.BlockSpec((N_HEADS, BLK, BLK), lambda i: (0, 0, 0)),
                  out_shape=_sds((N_HEADS, BLK, BLK), F32))(rel_bias)


def _bias_bwd(dbias, name):
    def body(db_ref, o_ref):
        _, bucket = _window_tiles()
        lane = lax.broadcasted_iota(jnp.int32, (N_HEADS, 128), 1)

        def per_bucket(b, out):
            mb = (bucket == b).astype(F32)
            per_col = jnp.sum(db_ref[...] * mb[None, :, :], axis=1)
            return jnp.where(lane == b, jnp.sum(per_col, axis=1, keepdims=True), out)

        o_ref[...] = lax.fori_loop(0, REL_BUCKETS, per_bucket, jnp.zeros((N_HEADS, 128), F32))

    return _pcall(body, name=name, grid=(1,),
                  in_specs=[pl.BlockSpec((N_HEADS, BLK, BLK), lambda i: (0, 0, 0))],
                  out_specs=pl.BlockSpec((N_HEADS, 128), lambda i: (0, 0)),
                  out_shape=_sds((N_HEADS, 128), F32))(dbias)


PAIR = 2 * HEAD
GROUP = N_HEADS // N_KV
SWA_SCALE = HEAD ** -0.5


def _window_masks(n):
    i = lax.broadcasted_iota(jnp.int32, (GROUP * BLK, BLK), 0) & (BLK - 1)
    j = lax.broadcasted_iota(jnp.int32, (GROUP * BLK, BLK), 1)
    return j <= i, jnp.logical_and(n == 0, j > i), j < HEAD


def _kv_twice(ref, base, g, low):
    slab = ref[:, base + PAIR * (g // 2): base + PAIR * (g // 2 + 1)]
    swapped = pltpu.roll(slab, HEAD, 1)
    return jnp.where(low, slab, swapped) if g % 2 == 0 else jnp.where(low, swapped, slab)


def _stack_heads(ref, g, low):
    parts = []
    for r in range(2):
        slab = ref[:, PAIR * (2 * g + r): PAIR * (2 * g + r + 1)]
        zero = jnp.zeros_like(slab)
        parts += [jnp.where(low, slab, zero), jnp.where(low, zero, slab)]
    return jnp.concatenate(parts, axis=0)


def _unstack_heads(t, low):
    return [jnp.where(low, t[2 * r * BLK:(2 * r + 1) * BLK], t[(2 * r + 1) * BLK:(2 * r + 2) * BLK])
            for r in range(2)]


def _head_rows(t, k):
    return t[k * BLK:(k + 1) * BLK]


def _per_head_column(values):
    head = lax.broadcasted_iota(jnp.int32, (GROUP * BLK, 1), 0) // BLK
    col = jnp.full((GROUP * BLK, 1), values[0], F32)
    for k in range(1, GROUP):
        col = jnp.where(head == k, values[k], col)
    return col


def _window_logits(q4, kc, kp, bias4, own, absent):
    sc = jnp.where(own, _dot_nt(q4, kc), _dot_nt(q4, kp)) * SWA_SCALE + bias4
    return jnp.where(absent, NEG, sc)


def _split_window(t, own):
    zero = jnp.zeros_like(t)
    return jnp.where(own, t, zero), jnp.where(own, zero, t)


def _swa_fwd(q, kv, bias, sinks, name, carry=None):
    s = q.shape[0]
    nb = s // BLK
    kvw = 2 * N_KV * HEAD

    def body(q_ref, kc_ref, kp_ref, b_ref, sk_ref, o_ref, lse_ref):
        own, absent, low4 = _window_masks(pl.program_id(0))
        low = low4[:BLK]
        lane = lax.broadcasted_iota(jnp.int32, (BLK, 128), 1)
        lse_t = jnp.zeros((BLK, 128), F32)
        for g in range(N_KV):
            q4 = _stack_heads(q_ref, g, low)
            kc, kp = _kv_twice(kc_ref, 0, g, low), _kv_twice(kp_ref, 0, g, low)
            vc, vp = _kv_twice(kc_ref, N_KV * HEAD, g, low), _kv_twice(kp_ref, N_KV * HEAD, g, low)
            bias4 = b_ref[GROUP * g:GROUP * (g + 1)].reshape(GROUP * BLK, BLK)
            sc = _window_logits(q4, kc, kp, bias4, own, absent)
            sk = _per_head_column([sk_ref[0, GROUP * g + k] for k in range(GROUP)])
            m = jnp.maximum(jnp.max(sc, axis=1, keepdims=True), sk)
            p = jnp.exp(sc - m)
            l = jnp.sum(p, axis=1, keepdims=True) + jnp.exp(sk - m)
            p_own, p_prev = _split_window(p.astype(BF16), own)
            out = (_dot(p_own, vc) + _dot(p_prev, vp)) * (1.0 / l)
            for r, slab in enumerate(_unstack_heads(out, low)):
                o_ref[:, PAIR * (2 * g + r): PAIR * (2 * g + r + 1)] = slab.astype(BF16)
            lse4 = m + jnp.log(l)
            for k in range(GROUP):
                lse_t = jnp.where(lane == GROUP * g + k, _head_rows(lse4, k), lse_t)
        lse_ref[...] = lse_t

    call = _pcall(body, name=name, grid=(nb,),
                  in_specs=[pl.BlockSpec((BLK, D), lambda n: (n, 0)),
                            pl.BlockSpec((BLK, kvw), lambda n: (n, 0)),
                            pl.BlockSpec((BLK, kvw), lambda n: (jnp.maximum(n - 1, 0), 0)),
                            pl.BlockSpec((N_HEADS, BLK, BLK), lambda n: (0, 0, 0)),
                            pl.BlockSpec(memory_space=pltpu.SMEM)],
                  out_specs=[pl.BlockSpec((BLK, D), lambda n: (n, 0)), pl.BlockSpec((BLK, 128), lambda n: (n, 0))],
                  out_shape=[_sds((s, D), BF16), _sds((s, 128), F32)], carry=carry)
    return _carried(call, (q, kv, kv, bias, sinks), carry)


def _fold_halves(t, g, low):
    folded = jnp.where(low, t, 0.0) + pltpu.roll(jnp.where(low, 0.0, t), HEAD, 1)
    return folded if g % 2 == 0 else pltpu.roll(folded, HEAD, 1)


def _swa_bwd(q, kv, attn, dattn, lse, bias, sinks, name, carry=None):
    s = q.shape[0]
    nb = s // BLK
    kvw = 2 * N_KV * HEAD
    voff = N_KV * HEAD

    def body(q_ref, kc_ref, kp_ref, o_ref, do_ref, lse_ref, b_ref, skrow_ref, dq_ref, dkv_ref, dbias_ref, dsk_ref,
             dq_hold, kv_hold, dq_new, kv_prev, kv_cur):
        n = pl.program_id(0)

        @pl.when(n == 0)
        def _():
            dbias_ref[...] = jnp.zeros_like(dbias_ref)
            dsk_ref[...] = jnp.zeros_like(dsk_ref)
            dq_hold[...] = jnp.zeros_like(dq_hold)
            kv_hold[...] = jnp.zeros_like(kv_hold)

        @pl.when(n < nb)
        def _():
            own, absent, low4 = _window_masks(n)
            low = low4[:BLK]
            lane = lax.broadcasted_iota(jnp.int32, (BLK, 128), 1)
            delta_t = jnp.zeros((BLK, 128), F32)
            ones = jnp.ones((PAIR, 128), BF16)
            for pair_of_kv in range(N_KV // 2):
                slab_grads = [jnp.zeros((BLK, PAIR), F32) for _ in range(4)]
                for g in (2 * pair_of_kv, 2 * pair_of_kv + 1):
                    q4, do4 = _stack_heads(q_ref, g, low), _stack_heads(do_ref, g, low)
                    kc, kp = _kv_twice(kc_ref, 0, g, low), _kv_twice(kp_ref, 0, g, low)
                    vc, vp = _kv_twice(kc_ref, voff, g, low), _kv_twice(kp_ref, voff, g, low)
                    o_slabs = [o_ref[:, PAIR * (2 * g + r): PAIR * (2 * g + r + 1)] for r in range(2)]
                    o4 = jnp.concatenate([o_slabs[0], o_slabs[0], o_slabs[1], o_slabs[1]], axis=0)
                    delta = _dot(do4 * o4, ones)
                    heads = range(GROUP * g, GROUP * (g + 1))
                    lse4 = jnp.concatenate([lse_ref[:, h:h + 1] for h in heads], axis=0)
                    bias4 = b_ref[GROUP * g:GROUP * (g + 1)].reshape(GROUP * BLK, BLK)
                    p = jnp.exp(_window_logits(q4, kc, kp, bias4, own, absent) - lse4)
                    dp = jnp.where(own, _dot_nt(do4, vc), _dot_nt(do4, vp))
                    ds = p * (dp - delta)
                    dbias_ref[GROUP * g:GROUP * (g + 1)] += ds.reshape(GROUP, BLK, BLK)
                    for k, h in enumerate(heads):
                        delta_t = jnp.where(lane == h, _head_rows(delta, k), delta_t)
                    ds_own, ds_prev = _split_window((ds * SWA_SCALE).astype(BF16), own)
                    p_own, p_prev = _split_window(p.astype(BF16), own)
                    dq4 = _dot(ds_own, kc) + _dot(ds_prev, kp)
                    for r, slab in enumerate(_unstack_heads(dq4, low)):
                        dq_new[:, PAIR * (2 * g + r): PAIR * (2 * g + r + 1)] = slab
                    grads = [_dot_tn(ds_own, q4), _dot_tn(ds_prev, q4), _dot_tn(p_own, do4), _dot_tn(p_prev, do4)]
                    slab_grads = [t + _fold_halves(dk, g, low) for t, dk in zip(slab_grads, grads)]
                ks = slice(PAIR * pair_of_kv, PAIR * (pair_of_kv + 1))
                vs = slice(voff + PAIR * pair_of_kv, voff + PAIR * (pair_of_kv + 1))
                kv_cur[:, ks], kv_prev[:, ks], kv_cur[:, vs], kv_prev[:, vs] = slab_grads
            dsk_ref[...] -= jnp.sum(jnp.exp(skrow_ref[...] - lse_ref[...]) * delta_t, axis=0, keepdims=True)

        @pl.when(n == nb)
        def _():
            kv_prev[...] = jnp.zeros_like(kv_prev)

        dq_ref[...] = dq_hold[...].astype(BF16)
        dkv_ref[...] = (kv_hold[...] + kv_prev[...]).astype(BF16)

        @pl.when(n < nb)
        def _():
            dq_hold[...] = dq_new[...]
            kv_hold[...] = kv_cur[...]

    def cur(n):
        return jnp.minimum(n, nb - 1)

    call = _pcall(body, name=name, grid=(nb + 1,), carry=carry,
                  in_specs=[pl.BlockSpec((BLK, D), lambda n: (cur(n), 0)),
                            pl.BlockSpec((BLK, kvw), lambda n: (cur(n), 0)),
                            pl.BlockSpec((BLK, kvw), lambda n: (jnp.maximum(cur(n) - 1, 0), 0)),
                            pl.BlockSpec((BLK, D), lambda n: (cur(n), 0)),
                            pl.BlockSpec((BLK, D), lambda n: (cur(n), 0)),
                            pl.BlockSpec((BLK, 128), lambda n: (cur(n), 0)),
                            pl.BlockSpec((N_HEADS, BLK, BLK), lambda n: (0, 0, 0)),
                            pl.BlockSpec((1, 128), lambda n: (0, 0))],
                  out_specs=[pl.BlockSpec((BLK, D), lambda n: (jnp.maximum(n - 1, 0), 0)),
                             pl.BlockSpec((BLK, kvw), lambda n: (jnp.maximum(n - 1, 0), 0)),
                             pl.BlockSpec((N_HEADS, BLK, BLK), lambda n: (0, 0, 0)),
                             pl.BlockSpec((1, 128), lambda n: (0, 0))],
                  out_shape=[_sds((s, D), BF16), _sds((s, kvw), BF16), _sds((N_HEADS, BLK, BLK), F32),
                             _sds((1, 128), F32)],
                  scratch=[pltpu.VMEM((BLK, D), F32), pltpu.VMEM((BLK, kvw), F32), pltpu.VMEM((BLK, D), F32),
                           pltpu.VMEM((BLK, kvw), F32), pltpu.VMEM((BLK, kvw), F32)])
    sink_row = jnp.pad(sinks, ((0, 0), (0, 128 - N_HEADS)))
    return _carried(call, (q, kv, kv, attn, dattn, lse, bias, sink_row), carry)


HALO = 16
CW = D


def _conv_taps(cu, halo_cu, first_tile):
    row = lax.broadcasted_iota(jnp.int32, cu.shape, 0)
    halo_cu = jnp.where(first_tile, 0.0, halo_cu)
    c1 = jnp.where(row == 0, halo_cu[HALO - 1:HALO], pltpu.roll(cu, 1, 0))
    c2 = jnp.where(row == 0, halo_cu[HALO - 2:HALO - 1],
                   jnp.where(row == 1, halo_cu[HALO - 1:HALO], pltpu.roll(cu, 2, 0)))
    return c1, c2


def _conv_merge_fwd(pa, attn, convw, name, ts=256, carry=None):
    _, s, _ = pa.shape
    ts = min(ts, s)
    hb = ts // HALO

    def body(pa_ref, hp_ref, at_ref, w_ref, o_ref):
        i = pl.program_id(1)
        cu = pa_ref[0].astype(F32) * pa_ref[2].astype(F32)
        c1, c2 = _conv_taps(cu, hp_ref[0].astype(F32) * hp_ref[2].astype(F32), i == 0)
        w = w_ref[...]
        c3 = w[0:1] * c2 + w[1:2] * c1 + w[2:3] * cu
        conv = pa_ref[1].astype(F32) * c3
        o_ref[...] = (jax.nn.sigmoid(pa_ref[3].astype(F32)) * at_ref[...].astype(F32)
                      + jax.nn.sigmoid(pa_ref[4].astype(F32)) * conv).astype(BF16)

    call = _pcall(body, name=name, grid=(D // CW, s // ts), carry=carry,
                  in_specs=[pl.BlockSpec((5, ts, CW), lambda c, i: (0, i, c)),
                            pl.BlockSpec((5, HALO, CW), lambda c, i: (0, jnp.maximum(i * hb - 1, 0), c)),
                            pl.BlockSpec((ts, CW), lambda c, i: (i, c)),
                            pl.BlockSpec((8, CW), lambda c, i: (0, c))],
                  out_specs=pl.BlockSpec((ts, CW), lambda c, i: (i, c)),
                  out_shape=_sds((s, D), BF16))
    return _carried(call, (pa, pa, attn, convw), carry)


def _conv_merge_bwd(dmerged, pa, attn, convw, name, ts=256, carry=None):
    _, s, _ = pa.shape
    ts = min(ts, s)
    hb = ts // HALO
    last_hb = s // HALO - 1

    def body(dm_ref, pa_ref, at_ref, w_ref, hp_ref, hn_ref, dmn_ref, dat_ref, dpa_ref, dw_ref):
        i = pl.program_id(1)
        last = i == pl.num_programs(1) - 1
        dm = dm_ref[...].astype(F32)
        cp, bp, u = pa_ref[0].astype(F32), pa_ref[1].astype(F32), pa_ref[2].astype(F32)
        sa = jax.nn.sigmoid(pa_ref[3].astype(F32))
        sc = jax.nn.sigmoid(pa_ref[4].astype(F32))
        at = at_ref[...].astype(F32)
        cu = cp * u
        c1, c2 = _conv_taps(cu, hp_ref[0].astype(F32) * hp_ref[2].astype(F32), i == 0)
        w = w_ref[...]
        c3 = w[0:1] * c2 + w[1:2] * c1 + w[2:3] * cu
        dconv = dm * sc
        dc3 = dconv * bp
        nxt = dmn_ref[...].astype(F32) * jax.nn.sigmoid(hn_ref[4].astype(F32)) * hn_ref[1].astype(F32)
        nxt = jnp.where(last, 0.0, nxt)
        row = lax.broadcasted_iota(jnp.int32, dc3.shape, 0)
        d1 = jnp.where(row == ts - 1, nxt[0:1], pltpu.roll(dc3, ts - 1, 0))
        d2 = jnp.where(row == ts - 2, nxt[0:1], jnp.where(row == ts - 1, nxt[1:2], pltpu.roll(dc3, ts - 2, 0)))
        dcu = w[2:3] * dc3 + w[1:2] * d1 + w[0:1] * d2
        dat_ref[...] = (dm * sa).astype(BF16)
        dpa_ref[0] = (dcu * u).astype(BF16)
        dpa_ref[1] = (dconv * c3).astype(BF16)
        dpa_ref[2] = (dcu * cp).astype(BF16)
        dpa_ref[3] = (dm * at * sa * (1.0 - sa)).astype(BF16)
        dpa_ref[4] = (dm * bp * c3 * sc * (1.0 - sc)).astype(BF16)

        @pl.when(i == 0)
        def _():
            dw_ref[...] = jnp.zeros_like(dw_ref)

        dw_ref[0:1, :] += jnp.sum(dc3 * c2, axis=0, keepdims=True)
        dw_ref[1:2, :] += jnp.sum(dc3 * c1, axis=0, keepdims=True)
        dw_ref[2:3, :] += jnp.sum(dc3 * cu, axis=0, keepdims=True)

    call = _pcall(body, name=name, grid=(D // CW, s // ts), carry=carry,
                  in_specs=[pl.BlockSpec((ts, CW), lambda c, i: (i, c)),
                            pl.BlockSpec((5, ts, CW), lambda c, i: (0, i, c)),
                            pl.BlockSpec((ts, CW), lambda c, i: (i, c)),
                            pl.BlockSpec((8, CW), lambda c, i: (0, c)),
                            pl.BlockSpec((5, HALO, CW), lambda c, i: (0, jnp.maximum(i * hb - 1, 0), c)),
                            pl.BlockSpec((5, HALO, CW), lambda c, i: (0, jnp.minimum((i + 1) * hb, last_hb), c)),
                            pl.BlockSpec((HALO, CW), lambda c, i: (jnp.minimum((i + 1) * hb, last_hb), c))],
                  out_specs=[pl.BlockSpec((ts, CW), lambda c, i: (i, c)),
                             pl.BlockSpec((5, ts, CW), lambda c, i: (0, i, c)),
                             pl.BlockSpec((8, CW), lambda c, i: (0, c))],
                  out_shape=[_sds((s, D), BF16), _sds((5, s, D), BF16), _sds((8, D), F32)])
    return _carried(call, (dmerged, pa, attn, convw, pa, pa, dmerged), carry)


def _xattn_fwd(q, kv, name, tq=512):
    s, _ = q.shape
    nm = kv.shape[1]
    tq = min(tq, s)

    def body(q_ref, kv_ref, o_ref, lse_ref):
        lane = lax.broadcasted_iota(jnp.int32, (tq, 128), 1)
        lse_t = jnp.zeros((tq, 128), F32)
        for h in range(XH):
            hs = slice(XHD * h, XHD * (h + 1))
            sc = _dot_nt(q_ref[:, hs], kv_ref[h]) * (XHD ** -0.5)
            m = jnp.max(sc, axis=1, keepdims=True)
            p = jnp.exp(sc - m)
            l = jnp.sum(p, axis=1, keepdims=True)
            o_ref[:, hs] = (_dot(p.astype(BF16), kv_ref[XH + h]) * (1.0 / l)).astype(BF16)
            lse_t = jnp.where(lane == h, m + jnp.log(l), lse_t)
        lse_ref[...] = lse_t

    return _pcall(body, name=name, grid=(s // tq,),
                  in_specs=[pl.BlockSpec((tq, D), lambda i: (i, 0)), pl.BlockSpec((2 * XH, nm, XHD), lambda i: (0, 0, 0))],
                  out_specs=[pl.BlockSpec((tq, D), lambda i: (i, 0)), pl.BlockSpec((tq, 128), lambda i: (i, 0))],
                  out_shape=[_sds((s, D), BF16), _sds((s, 128), F32)])(q, kv)


def _xattn_bwd(q, kv, o, do, lse, name, tq=512, carry=None):
    s, _ = q.shape
    nm = kv.shape[1]
    tq = min(tq, s)

    def body(q_ref, kv_ref, o_ref, do_ref, lse_ref, dq_ref, dkv_ref):
        @pl.when(pl.program_id(0) == 0)
        def _():
            dkv_ref[...] = jnp.zeros_like(dkv_ref)

        for h in range(XH):
            hs = slice(XHD * h, XHD * (h + 1))
            qh, kh, vh, dob = q_ref[:, hs], kv_ref[h], kv_ref[XH + h], do_ref[:, hs]
            p = jnp.exp(_dot_nt(qh, kh) * (XHD ** -0.5) - lse_ref[:, h:h + 1])
            dp = _dot_nt(dob, vh)
            delta = jnp.sum(dob.astype(F32) * o_ref[:, hs].astype(F32), axis=1, keepdims=True)
            dsb = (p * (dp - delta) * (XHD ** -0.5)).astype(BF16)
            dq_ref[:, hs] = _dot(dsb, kh).astype(BF16)
            dkv_ref[h] += _dot_tn(dsb, qh)
            dkv_ref[XH + h] += _dot_tn(p.astype(BF16), dob)

    call = _pcall(body, name=name, grid=(s // tq,), carry=carry,
                  in_specs=[pl.BlockSpec((tq, D), lambda i: (i, 0)), pl.BlockSpec((2 * XH, nm, XHD), lambda i: (0, 0, 0)),
                            pl.BlockSpec((tq, D), lambda i: (i, 0)), pl.BlockSpec((tq, D), lambda i: (i, 0)),
                            pl.BlockSpec((tq, 128), lambda i: (i, 0))],
                  out_specs=[pl.BlockSpec((tq, D), lambda i: (i, 0)), pl.BlockSpec((2 * XH, nm, XHD), lambda i: (0, 0, 0))],
                  out_shape=[_sds((s, D), BF16), _sds((2 * XH, nm, XHD), F32)])
    return _carried(call, (q, kv, o, do, lse), carry)


def _ffn_down_bwd(dxb, wd4, gu4, name, tm=512, carry=None, behind=None):
    s, _ = dxb.shape
    tm = min(tm, s)

    def body(dx_ref, w_hbm, gu_ref, o_ref, w_ref, w_sem):
        _load_once(w_hbm, w_ref, w_sem)
        for p in range(4):
            for rows in _row_chunks(tm):
                da = _dot_nt(dx_ref[rows, :], w_ref[p])
                g = gu_ref[0, p, rows, :].astype(F32)
                u = gu_ref[1, p, rows, :].astype(F32)
                sg = jax.nn.sigmoid(g)
                t = da * sg
                o_ref[0, p, rows, :] = (t * u * (1.0 + g - g * sg)).astype(BF16)
                o_ref[1, p, rows, :] = (t * g).astype(BF16)

    block = pl.BlockSpec((2, 4, tm, FS), lambda i: (0, 0, i, 0))
    call = _pcall(body, name=name, grid=(s // tm,), carry=carry, behind=behind,
                  in_specs=[pl.BlockSpec((tm, D), lambda i: (i, 0)), HBM_SPEC, block],
                  out_specs=block, out_shape=_sds((2, 4, s, FS), BF16), scratch=_resident(wd4))
    return _carried(call, (dxb, wd4, gu4), carry)


def _mm_tn(a, b, name, scale=1.0, carry=None):
    pa_n, s, m = a.shape
    pb_n, _, n = b.shape
    po = max(pa_n, pb_n)
    tn = n if po >= 4 else min(n, 256)

    def body(a_ref, b_ref, o_ref):
        o_ref[...] = (scale * _dot_tn(a_ref[...], b_ref[...])).astype(BF16)

    call = _pcall(body, name=name, grid=(po, n // tn), carry=carry,
                  in_specs=[pl.BlockSpec((None, s, m), lambda o, j: (o if pa_n > 1 else 0, 0, 0)),
                            pl.BlockSpec((None, s, tn), lambda o, j: (o if pb_n > 1 else 0, 0, j))],
                  out_specs=pl.BlockSpec((None, m, tn), lambda o, j: (o, 0, j)),
                  out_shape=_sds((po, m, n), BF16))
    return _carried(call, (a, b), carry)


def _mm_tn_rows(a, b, name, total_rows, row0, begun=None, tm=512, carry=None):
    p, s, m = a.shape
    n = b.shape[1]
    tm = min(tm, m)
    tiles = m // tm
    assert row0 % tm == 0 and m % tm == 0, (row0, m, tm)

    def body(a_ref, b_ref, *rest):
        rest[-1][...] = _dot_tn(a_ref[...], b_ref[...]).astype(BF16)

    in_specs = [pl.BlockSpec((None, s, tm), lambda o, i: (o, 0, i)), pl.BlockSpec((s, n), lambda o, i: (0, 0))]
    call = _pcall(body, name=name, grid=(p, tiles), in_specs=in_specs + ([HBM_SPEC] if begun is not None else []),
                  out_specs=pl.BlockSpec((tm, n), lambda o, i: (row0 // tm + o * tiles + i, 0)),
                  out_shape=_sds((total_rows, n), BF16), aliases={2: 0} if begun is not None else None, carry=carry)
    return _carried(call, (a, b, begun) if begun is not None else (a, b), carry)


def _sum_dots(a_ref, b_ref, nj, bt, rows=slice(None)):
    dot = _dot_nt if bt else _dot
    acc = dot(a_ref[0, rows, :], b_ref[0])
    for j in range(1, nj):
        acc = acc + dot(a_ref[j, rows, :], b_ref[j])
    return acc


def _mm_acc(a, b, name, out_dtype, tm=512, bt=False, carry=None):
    nj, s, k = a.shape
    n = b.shape[1] if bt else b.shape[2]
    tm = min(tm, s)

    def body(a_ref, b_ref, o_ref):
        o_ref[...] = _sum_dots(a_ref, b_ref, nj, bt).astype(out_dtype)

    call = _pcall(body, name=name, grid=(s // tm,), carry=carry,
                  in_specs=[pl.BlockSpec((nj, tm, k), lambda i: (0, i, 0)),
                            pl.BlockSpec(b.shape, lambda i: (0, 0, 0))],
                  out_specs=pl.BlockSpec((tm, n), lambda i: (i, 0)), out_shape=_sds((s, n), out_dtype))
    return _carried(call, (a, b), carry)


def _rms_bwd_call(name, acts, weights, scratch, load, dh_rows, *, x, gain, dres, tm, carry, behind=None):
    s, n = x.shape
    tm = min(tm, s)
    n_act, n_w = len(acts), len(weights)

    def body(*refs):
        act_refs, w_refs = refs[:n_act], refs[n_act:n_act + n_w]
        x_ref, g_ref, r_ref, dx_ref, dxb_ref, dg_ref = refs[n_act + n_w:n_act + n_w + 6]
        held = refs[n_act + n_w + 6:]
        load(w_refs, held)

        @pl.when(pl.program_id(0) == 0)
        def _():
            dg_ref[...] = jnp.zeros_like(dg_ref)

        for rows in _row_chunks(tm):
            dh = dh_rows(act_refs, held, rows)
            xv = x_ref[rows, :]
            r = lax.rsqrt(jnp.mean(xv * xv, axis=-1, keepdims=True) + EPS)
            xh = xv * r
            dyg = dh * g_ref[...]
            dx = r_ref[rows, :] + r * (dyg - xh * jnp.mean(dyg * xh, axis=-1, keepdims=True))
            dx_ref[rows, :] = dx
            dxb_ref[rows, :] = dx.astype(BF16)
            dg_ref[...] += jnp.sum(dh * xh, axis=0, keepdims=True)

    def tile(a):
        return (pl.BlockSpec((tm, a.shape[1]), lambda i: (i, 0)) if a.ndim == 2
                else pl.BlockSpec((a.shape[0], tm, a.shape[2]), lambda i: (0, i, 0)))

    row = pl.BlockSpec((tm, n), lambda i: (i, 0))
    in_specs = [tile(a) for a in acts] + [HBM_SPEC] * n_w + [row, pl.BlockSpec((1, n), lambda i: (0, 0)), row]
    call = _pcall(body, name=name, grid=(s // tm,), in_specs=in_specs, carry=carry, behind=behind,
                  out_specs=[row, row, pl.BlockSpec((1, n), lambda i: (0, 0))],
                  out_shape=[_sds((s, n), F32), _sds((s, n), BF16), _sds((1, n), F32)], scratch=scratch)
    return _carried(call, tuple(acts) + tuple(weights) + (x, gain, dres), carry)


def _mm_acc_rms_bwd(a, b, name, *, x, gain, dres, scale=None, tm=512, bt=False, carry=None, behind=None):
    def load(w_refs, held):
        _load_once(w_refs[0], held[0], held[1])

    def dh_rows(act_refs, held, rows):
        dh = _sum_dots(act_refs[0], held[0], a.shape[0], bt, rows)
        return dh if scale is None else scale * dh

    return _rms_bwd_call(name, [a], [b], _resident(b), load, dh_rows, x=x, gain=gain, dres=dres, tm=tm, carry=carry,
                         behind=behind)


def _in_proj_bwd(dpa, dq, dkv, w_in_t, name, *, x, gain, dres, tm=512, carry=None, behind=None):
    def load(w_refs, held):
        _load_in_proj(w_refs[0], *held)

    def dh_rows(act_refs, held, rows):
        dpa_ref, dq_ref, dkv_ref = act_refs
        wq_ref, wkv_ref, wa_ref, _ = held
        dh = _dot(dq_ref[rows, :], wq_ref[...]) + _dot(dkv_ref[rows, :], wkv_ref[...])
        return dh + _sum_dots(dpa_ref, wa_ref, N_SEG, False, rows)

    return _rms_bwd_call(name, [dpa, dq, dkv], [w_in_t], IN_PROJ_WEIGHTS, load, dh_rows, x=x, gain=gain, dres=dres,
                         tm=tm, carry=carry, behind=behind)


def _adam(w, g, m, v):
    m2 = ADAM_B1 * m + (1.0 - ADAM_B1) * g
    v2 = ADAM_B2 * v + (1.0 - ADAM_B2) * (g * g)
    m_hat = m2 / (1.0 - ADAM_B1 ** ADAM_STEP)
    v_hat = v2 / (1.0 - ADAM_B2 ** ADAM_STEP)
    delta = -ADAM_LR * (m_hat / (jnp.sqrt(v_hat) + ADAM_EPS) + ADAM_WD * w)
    return delta, m2, v2


def _adamw(parts, w, m, v, name, behind=None):
    _, r, c = parts.shape
    tr = max(t for t in range(16, 257, 16) if r % t == 0)

    def body(p_ref, w_ref, m_ref, v_ref, g_ref, d_ref, m2_ref, v2_ref):
        g = p_ref[0].astype(F32)
        for i in range(1, N_DEV):
            g = g + p_ref[i].astype(F32)
        delta, m2, v2 = _adam(w_ref[...], g, m_ref[...], v_ref[...])
        g_ref[...] = g
        d_ref[...] = delta
        m2_ref[...] = m2
        v2_ref[...] = v2

    blk = pl.BlockSpec((tr, c), lambda i: (i, 0))
    return _pcall(body, name=name, grid=(r // tr,), behind=behind,
                  in_specs=[pl.BlockSpec((N_DEV, tr, c), lambda i: (0, i, 0)), blk, blk, blk],
                  out_specs=[blk] * 4, out_shape=[_sds((r, c), F32)] * 4)(parts, w, m, v)


def _position():
    return lax.axis_index("x"), lax.axis_index("y"), lax.axis_index("c")


def _slot(px, py, pc):
    return 4 * px + 2 * py + pc


def _row_window(ref, rows):
    r0, r1 = rows
    return ref if (r0, r1) == (0, ref.shape[0]) else ref.at[pl.ds(r0, r1 - r0)]


def _split_items(items):
    sources = [src for src, _, _ in items]
    begun = [(a, dest) for a, (_, _, dest) in enumerate(items) if dest is not None]
    aliases = {len(sources) + k: a for k, (a, _) in enumerate(begun)}
    return sources + [dest for _, dest in begun], [rows for _, rows, _ in items], aliases


def _gather_carry(items):
    na = len(items)
    carry_ins, windows, aliases = _split_items(items)

    def plan(ins, outs, sems):
        send_sems, recv_sems, local_sems = sems
        x, y, c = _position()
        me, sibling = (x, y, c), (x, y, 1 - c)
        chips = [(1 - x, y), (x, 1 - y), (1 - x, 1 - y)]
        ins = [_row_window(ins[a], windows[a]) for a in range(na)]

        def block_rows(a, block):
            return _row_window(outs[a].at[_slot(*block)], windows[a])

        def copy(a, k, block, to, src=None):
            rows = block_rows(a, block)
            return pltpu.make_async_remote_copy(src_ref=rows if src is None else src, dst_ref=rows,
                                                send_sem=send_sems.at[k, a], recv_sem=recv_sems.at[k, a],
                                                device_id=to, device_id_type=MESH)

        mine = [pltpu.make_async_copy(ins[a], block_rows(a, me), local_sems.at[a]) for a in range(na)]
        first = [copy(a, 0, me, sibling, src=ins[a]) for a in range(na)]
        for j, chip in enumerate(chips):
            first += [copy(a, 1 + j, me, (*chip, c), src=ins[a]) for a in range(na)]
        landed = [[copy(a, 1 + j, (*chip, c), me) for a in range(na)] for j, chip in enumerate(chips)]
        passed = [[copy(a, 4 + j, (*chip, c), sibling) for a in range(na)] for j, chip in enumerate(chips)]
        from_sibling = [copy(a, 0, sibling, me) for a in range(na)]
        for j, chip in enumerate(chips):
            from_sibling += [copy(a, 4 + j, (*chip, 1 - c), me) for a in range(na)]
        return mine, first, landed, passed, from_sibling

    def start(ins, outs, sems):
        mine, first, _, _, _ = plan(ins, outs, sems)
        for cp in mine + first:
            cp.start()

    def mid(ins, outs, sems):
        _, _, landed, passed, _ = plan(ins, outs, sems)
        for over_ici, onward in zip(landed, passed):
            for cp, fwd in zip(over_ici, onward):
                cp.wait_recv()
                fwd.start()

    def finish(ins, outs, sems):
        mine, first, _, passed, from_sibling = plan(ins, outs, sems)
        for cp in from_sibling:
            cp.wait_recv()
        for cp in first + [fwd for onward in passed for fwd in onward]:
            cp.wait_send()
        for cp in mine:
            cp.wait()

    return _Carry(carry_ins, [_sds((N_DEV,) + src.shape, src.dtype) for src, _, _ in items],
                  [pltpu.SemaphoreType.DMA((7, na)), pltpu.SemaphoreType.DMA((7, na)),
                   pltpu.SemaphoreType.DMA((na,))], start, finish, mid, aliases)


def _exchange_carry(scattered, replicated=()):
    items = list(scattered) + [(a, (0, a.shape[0]), None) for a in replicated]
    na, ns = len(items), len(scattered)
    carry_ins, windows, aliases = _split_items(items)

    def plan(ins, outs, sems):
        send_sems, recv_sems, local_sems = sems
        me = _slot(*_position())

        def source(a, j):
            return _row_window(ins[a].at[j] if a < ns else ins[a], windows[a])

        def copy(a, j, i):
            return pltpu.make_async_remote_copy(src_ref=source(a, j), dst_ref=_row_window(outs[a].at[i], windows[a]),
                                                send_sem=send_sems.at[j, a], recv_sem=recv_sems.at[i, a],
                                                device_id=(j >> 2, (j >> 1) & 1, j & 1), device_id_type=MESH)

        def own(a, j):
            return pltpu.make_async_copy(source(a, j), _row_window(outs[a].at[j], windows[a]), local_sems.at[a])

        return me, copy, own

    def start(ins, outs, sems):
        me, copy, own = plan(ins, outs, sems)
        for a in range(na):
            for j in range(N_DEV):
                @pl.when(me == j)
                def _():
                    own(a, j).start()

                @pl.when(me != j)
                def _():
                    copy(a, j, me).start()

    def finish(ins, outs, sems):
        me, copy, own = plan(ins, outs, sems)
        for a in range(na):
            for j in range(N_DEV):
                @pl.when(me == j)
                def _():
                    for i in range(N_DEV):
                        if i != j:
                            copy(a, j, i).wait_recv()
                    own(a, j).wait()

                @pl.when(me != j)
                def _():
                    copy(a, j, me).wait_send()

    return _Carry(carry_ins, [_sds((N_DEV,) + src.shape[-2:], src.dtype) for src, _, _ in items],
                  [pltpu.SemaphoreType.DMA((N_DEV, na)), pltpu.SemaphoreType.DMA((N_DEV, na)),
                   pltpu.SemaphoreType.DMA((na,))], start, finish, None, aliases)


HBM_ARRAY = pl.BlockSpec(memory_space=pltpu.HBM)
SEMAPHORES = pl.BlockSpec(memory_space=pltpu.SEMAPHORE)
DATAFLOW = pltpu.SideEffectType.DATAFLOW_SIDE_EFFECTING


def _exchange_copy(parts_ref, land_ref, send_sems, recv_sems, me, j):
    return pltpu.make_async_remote_copy(src_ref=parts_ref.at[j], dst_ref=land_ref.at[me], send_sem=send_sems.at[j],
                                        recv_sem=recv_sems.at[me], device_id=(j >> 2, (j >> 1) & 1, j & 1),
                                        device_id_type=MESH)


def _exchange_start(parts, name):
    def body(parts_ref, land_ref, send_sems, recv_sems, parts_thru, land_thru, token):
        me = _slot(*_position())
        for j in range(N_DEV):
            @pl.when(me == j)
            def _():
                pltpu.make_async_copy(parts_ref.at[j], land_ref.at[j], send_sems.at[j]).start()

            @pl.when(me != j)
            def _():
                _exchange_copy(parts_ref, land_ref, send_sems, recv_sems, me, j).start()
        token[...] = jnp.zeros_like(token)

    return pl.pallas_call(
        body, name=name,
        out_shape=(pltpu.SemaphoreType.DMA((N_DEV,)), pltpu.SemaphoreType.DMA((N_DEV,)),
                   pltpu.HBM(parts.shape, parts.dtype), pltpu.HBM(parts.shape, parts.dtype), _sds((8, 128), F32)),
        in_specs=(HBM_ARRAY, HBM_ARRAY),
        out_specs=(SEMAPHORES, SEMAPHORES, HBM_ARRAY, HBM_ARRAY, pl.BlockSpec(memory_space=pltpu.VMEM)),
        input_output_aliases={0: 2, 1: 3}, compiler_params=pltpu.CompilerParams(has_side_effects=DATAFLOW),
    )(pltpu.with_memory_space_constraint(parts, pltpu.HBM),
      pltpu.with_memory_space_constraint(lax.empty(parts.shape, parts.dtype), pltpu.HBM))


def _exchange_wait(send_sems, recv_sems, parts_thru, land_thru, after, name):
    def body(parts_ref, land_ref, send_sems, recv_sems, after_ref, parts_dead, got_ref):
        me = _slot(*_position())
        for j in range(N_DEV):
            @pl.when(me == j)
            def _():
                pltpu.make_async_copy(parts_ref.at[j], land_ref.at[j], send_sems.at[j]).wait()

            @pl.when(me != j)
            def _():
                both = pltpu.make_async_remote_copy(src_ref=parts_ref.at[j], dst_ref=land_ref.at[j],
                                                    send_sem=send_sems.at[j], recv_sem=recv_sems.at[j],
                                                    device_id=(j >> 2, (j >> 1) & 1, j & 1), device_id_type=MESH)
                both.wait_send()
                both.wait_recv()

    return pl.pallas_call(
        body, name=name, out_shape=(pltpu.HBM(parts_thru.shape, parts_thru.dtype),
                                    pltpu.HBM(parts_thru.shape, parts_thru.dtype)),
        in_specs=(HBM_ARRAY, HBM_ARRAY, SEMAPHORES, SEMAPHORES, pl.BlockSpec(memory_space=pl.ANY)),
        out_specs=(HBM_ARRAY, HBM_ARRAY), input_output_aliases={0: 0, 1: 1},
        compiler_params=pltpu.CompilerParams(has_side_effects=DATAFLOW),
    )(parts_thru, land_thru, send_sems, recv_sems, after)[1]


class _Mesh:
    def __init__(self, shards):
        self.shards, self.full, self.received, self.cache, self.pending, self.tokens = shards, {}, {}, {}, {}, {}

    def fetch(self, wanted):
        items = []
        for want in wanted:
            name, r0, r1 = want if isinstance(want, tuple) else (want, 0, self.shards[want].shape[0])
            items.append((self.shards[name], (r0, r1), self.full.get(name)))
        return _gather_carry(items)

    def fetched(self, wanted, results):
        self.full.update(zip([want[0] if isinstance(want, tuple) else want for want in wanted], results))

    def send(self, *payloads):
        return _exchange_carry([(parts, rows or (0, parts.shape[1]), self.received.get(name))
                                for name, parts, rows in payloads])

    def sent(self, names, results):
        self.received.update(zip(names, results))

    def send_apart(self, name, parts):
        *self.pending[name], self.tokens[name] = _exchange_start(parts, "exchange_" + name + "_start")
        return self.tokens[name]

    def sent_apart(self, name, after):
        self.received[name] = _exchange_wait(*self.pending.pop(name), after, "exchange_" + name + "_wait")

    def w(self, key):
        if key not in self.cache:
            self.cache[key] = self._layout(key)
        return self.cache[key]

    def _layout(self, key):
        if key in ("gu1", "gu2"):
            return self.full[key]
        if key in ("d1", "d2"):
            return self.full[key].reshape(4, FS, D)
        if key in ("out", "q", "o"):
            return self.full[key].reshape(D, D)
        if key == "kv":
            return self.full["kv"]
        if key == "convw":
            rows = self.full["conv"][:, :3, :].transpose(1, 0, 2).reshape(3, D)
            return jnp.concatenate([rows, jnp.zeros((5, D), F32)], axis=0)
        assert key == "win_t", key
        return self.full["win"].reshape(-1, D)


def _forward_backward(x, mem, target, g, rel_bias, sinks, ex):
    s = x.shape[0]
    def fetching(wanted, call, *args, **kw):
        res, got = call(*args, carry=ex.fetch(wanted), **kw)
        ex.fetched(wanted, got)
        return res

    h1 = fetching(["gu1", "conv"], _rmsnorm, x, g["ffn1"], "norm_ffn1")
    gu1, a1 = fetching(["d1", ("win", 0, 400)], _ffn_up, h1, ex.w("gu1").reshape(2, 4, FS, D), "ffn1_up")
    x1, h2 = fetching([("win", 400, 832)], _mm_res_norm, a1, ex.w("d1"), x, g["mix"], 0.5, "ffn1_down")
    pa, q, kv = fetching(["gu2"], _in_proj, h2, ex.w("win_t"), "in_proj")
    biasm = _bias_build(rel_bias, "bias_build")
    attn, lse = fetching(["out", "kv", "o"], _swa_fwd, q, kv, biasm, sinks, "swa_fwd")
    merged = fetching(["q"], _conv_merge_fwd, pa, attn, ex.w("convw"), "conv_merge_fwd")
    (x2, h3), _ = _mm_res_norm(merged[None], ex.w("out")[None], x1, g["xattn"], 1.0, "out_proj")
    q2 = _mm_nn(h3, ex.w("q")[None], "xattn_q")[0][0]
    mh, _ = _rmsnorm(mem, g["mem"], "norm_mem")
    kv2 = _mm_nn(mh, ex.w("kv"), "xattn_kv")[0]
    o, lse2 = _xattn_fwd(q2, kv2, "xattn_fwd")
    (x3, h4), _ = _mm_res_norm(o[None], ex.w("o")[None], x2, g["ffn2"], 1.0, "xattn_o")
    gu2, a2 = fetching(["d2"], _ffn_up, h4, ex.w("gu2").reshape(2, 4, FS, D), "ffn2_up")
    dx4, dx4b, loss, d_final = _ffn_down_loss(a2, ex.w("d2"), x3, g["final"], target, "ffn2_down_loss")
    def sending(payloads, call, *args, **kw):
        res, got = call(*args, carry=ex.send(*payloads), **kw)
        ex.sent([name for name, _, _ in payloads], got)
        return res

    dw_d2 = _mm_tn(a2, dx4b[None], "dw_ffn2_down", scale=0.5)[0].reshape(N_DEV, -1, D)
    dgu2 = sending([("d2", dw_d2, (0, 288))], _ffn_down_bwd, dx4b, ex.w("d2"), gu2, "ffn2_down_bwd").reshape(8, s, FS)
    dw_gu2 = sending([("d2", dw_d2, (288, 352))], _mm_tn, dgu2, h4[None], "dw_ffn2_up", scale=0.5)
    dx3, dx3b, d_ffn2 = sending([("gu2", dw_gu2, (0, 368))], _mm_acc_rms_bwd, dgu2, ex.w("gu2"), "ffn2_up_bwd",
                                x=x3, gain=g["ffn2"], dres=dx4, scale=0.5)
    do, _ = _mm_acc(dx3b[None], ex.w("o")[None], "xattn_o_bwd", BF16, bt=True)
    dw_o = _mm_tn(o[None], dx3b[None], "dw_xattn_o")[0].reshape(N_DEV, -1, D)
    (dq2, dkv2), _ = _xattn_bwd(q2, kv2, o, do, lse2, "xattn_bwd")
    dkv2b = dkv2.astype(BF16)
    dw_q = _mm_tn(h3[None], dq2[None], "dw_xattn_q")[0].reshape(N_DEV, -1, D)
    (dx2, dx2b, d_xattn), _ = _mm_acc_rms_bwd(dq2[None], ex.w("q")[None], "xattn_q_bwd", x=x2, gain=g["xattn"],
                                              dres=dx3, bt=True)
    dw_kv = _mm_tn(mh[None], dkv2b, "dw_xattn_kv")[0]
    (_, _, d_mem), _ = _mm_acc_rms_bwd(dkv2b, ex.w("kv"), "xattn_kv_bwd", x=mem, gain=g["mem"],
                                       dres=jnp.zeros_like(mem), bt=True)
    dmerged, _ = _mm_acc(dx2b[None], ex.w("out")[None], "out_proj_bwd", BF16, bt=True)
    dw_out = _mm_tn(merged[None], dx2b[None], "dw_out_proj")[0].reshape(N_DEV, -1, D)
    dattn, dpa, d_convw = sending([("kv", dw_kv, None)], _conv_merge_bwd,
                                  dmerged, pa, attn, ex.w("convw"), "conv_merge_bwd")
    dq, dkv, dbias, d_sinks = sending([("gu2", dw_gu2, (368, FS)), ("out", dw_out, None)], _swa_bwd,
                                      q, kv, attn, dattn, lse, biasm, sinks, "swa_bwd")
    d_relb = _bias_bwd(dbias, "bias_bwd")
    w_rows = ex.w("win_t").shape[0]
    dw_in = sending([("o", dw_o, None), ("q", dw_q, None)], _mm_tn_rows, dpa, h2, "dw_in_proj_a", w_rows, NQ + NKV)
    dw_in = _mm_tn_rows(dq[None], h2, "dw_in_proj_q", w_rows, 0, begun=dw_in)[0]
    dw_in = _mm_tn_rows(dkv[None], h2, "dw_in_proj_kv", w_rows, NQ, begun=dw_in)[0].reshape(N_DEV, -1, D)
    (dx1, dx1b, d_mix), _ = _in_proj_bwd(dpa, dq, dkv, ex.w("win_t"), "in_proj_bwd", x=x1, gain=g["mix"], dres=dx2,
                                         behind=ex.send_apart("win", dw_in))
    dw_d1 = _mm_tn(a1, dx1b[None], "dw_ffn1_down", scale=0.5)[0].reshape(N_DEV, -1, D)
    dgu1 = _ffn_down_bwd(dx1b, ex.w("d1"), gu1, "ffn1_down_bwd", behind=ex.send_apart("d1", dw_d1))[0]
    dgu1 = dgu1.reshape(8, s, FS)
    dw_gu1 = _mm_tn(dgu1, h1[None], "dw_ffn1_up", scale=0.5)[0]
    (dx0, _, d_ffn1), _ = _mm_acc_rms_bwd(dgu1, ex.w("gu1"), "ffn1_up_bwd", x=x, gain=g["ffn1"], dres=dx1,
                                          scale=0.5, behind=ex.send_apart("gu1", dw_gu1))

    relb_row = jnp.concatenate([d_relb[:, :REL_BUCKETS].T.reshape(1, REL_BUCKETS * N_HEADS), d_sinks[:, :N_HEADS],
                                jnp.zeros((1, D - REL_BUCKETS * N_HEADS - N_HEADS), F32)], axis=1)
    loss_row = jnp.concatenate([loss[0:1, 0:1], jnp.zeros((1, D - 1), F32)], axis=1)
    small = jnp.concatenate([d_ffn1, d_mix, d_xattn, d_mem, d_ffn2, d_final, relb_row, loss_row, d_convw[0:3],
                             jnp.zeros((SMALL_ROWS - ROW_CONV - 3, D), F32)], axis=0)
    return dx0, small


def _pack_small(norms, final, relb, sinks, conv_local, me):
    relb_row = jnp.concatenate([relb.reshape(1, -1), sinks.reshape(1, -1),
                                jnp.zeros((1, D - REL_BUCKETS * N_HEADS - N_HEADS), F32)], axis=1)
    conv_rows = lax.dynamic_update_slice(jnp.zeros((3, D), F32), conv_local.reshape(3, -1), (0, 128 * me))
    return jnp.concatenate(list(norms) + [final.reshape(1, D), relb_row, jnp.zeros((1, D), F32), conv_rows,
                                          jnp.zeros((SMALL_ROWS - ROW_CONV - 3, D), F32)], axis=0)


def kernel(x, mem, positions, rel_bias, ffn1_norm, ffn1_w_gu, ffn1_w_down, mix_norm, w_in, sinks, conv_w, w_out, xattn_norm, mem_norm, xattn_wq, xattn_wkv, xattn_wo, ffn2_norm, ffn2_w_gu, ffn2_w_down, final_norm, loss_target, m_rel_bias, m_ffn1_norm, m_ffn1_w_gu, m_ffn1_w_down, m_mix_norm, m_w_in, m_sinks, m_conv_w, m_w_out, m_xattn_norm, m_mem_norm, m_xattn_wq, m_xattn_wkv, m_xattn_wo, m_ffn2_norm, m_ffn2_w_gu, m_ffn2_w_down, m_final_norm, v_rel_bias, v_ffn1_norm, v_ffn1_w_gu, v_ffn1_w_down, v_mix_norm, v_w_in, v_sinks, v_conv_w, v_w_out, v_xattn_norm, v_mem_norm, v_xattn_wq, v_xattn_wkv, v_xattn_wo, v_ffn2_norm, v_ffn2_w_gu, v_ffn2_w_down, v_final_norm):
    del positions
    me = _slot(*_position())
    big = dict(gu1=(ffn1_w_gu, m_ffn1_w_gu, v_ffn1_w_gu), d1=(ffn1_w_down, m_ffn1_w_down, v_ffn1_w_down),
               win=(w_in, m_w_in, v_w_in), out=(w_out, m_w_out, v_w_out), q=(xattn_wq, m_xattn_wq, v_xattn_wq),
               kv=(xattn_wkv, m_xattn_wkv, v_xattn_wkv), o=(xattn_wo, m_xattn_wo, v_xattn_wo),
               gu2=(ffn2_w_gu, m_ffn2_w_gu, v_ffn2_w_gu), d2=(ffn2_w_down, m_ffn2_w_down, v_ffn2_w_down))
    order = list(big)
    transposed = ("gu1", "gu2", "win")
    local = {k: tuple(t[0].T if k in transposed else t[0] for t in big[k]) for k in order}
    shards = {k: local[k][0].astype(BF16) for k in order}
    shards["conv"] = jnp.concatenate([conv_w[0], jnp.zeros((5, 128), F32)], axis=0)
    ex = _Mesh(shards)
    gains = dict(ffn1=ffn1_norm, mix=mix_norm, xattn=xattn_norm, mem=mem_norm, ffn2=ffn2_norm,
                 final=final_norm.reshape(1, D))
    dx, small = _forward_backward(x[0], mem[0], loss_target[0], gains, rel_bias, sinks, ex)
    apart = ("d1", "win", "gu1")
    big_out = {k: _adamw(ex.received[k], *local[k], "adamw_" + k, behind=ex.tokens["gu1"])
               for k in order if k not in apart}
    for k in apart[:-1]:
        ex.sent_apart(k, after=sum(big_out[j][1][0:1, 0:1] for j in big_out))
        big_out[k] = _adamw(ex.received[k], *local[k], "adamw_" + k)
    spare = sum(big_out[k][1][0:1, 0:1] for k in big_out)
    small = lax.dynamic_update_slice(small, spare, (SMALL_ROWS - 1, 0))
    small_parts = _run_alone(_exchange_carry([], [small]), "exchange_small")[0]
    packed = [_pack_small(norms, final, relb, sk, conv, me) for norms, final, relb, sk, conv in (
        ((ffn1_norm, mix_norm, xattn_norm, mem_norm, ffn2_norm), final_norm, rel_bias, sinks, conv_w),
        ((m_ffn1_norm, m_mix_norm, m_xattn_norm, m_mem_norm, m_ffn2_norm), m_final_norm, m_rel_bias, m_sinks, m_conv_w),
        ((v_ffn1_norm, v_mix_norm, v_xattn_norm, v_mem_norm, v_ffn2_norm), v_final_norm, v_rel_bias, v_sinks, v_conv_w))]
    small_out = _adamw(small_parts, *packed, "adamw_small")
    done = [dx[0:1, 0:1], small_out[1][0:1, 0:1]] + [big_out[k][1][0:1, 0:1] for k in big_out]
    ex.sent_apart("gu1", after=sum(done))
    big_out["gu1"] = _adamw(ex.received["gu1"], *local["gu1"], "adamw_gu1")
    big_out = {k: [t.T if k in transposed else t for t in big_out[k]] for k in order}

    def unpack(t):
        conv = lax.dynamic_slice(t[ROW_CONV:ROW_CONV + 3], (0, 128 * me), (3, 128))[None]
        nrel = REL_BUCKETS * N_HEADS
        return dict(ffn1_norm=t[0:1], mix_norm=t[1:2], xattn_norm=t[2:3], mem_norm=t[3:4], ffn2_norm=t[4:5],
                    final_norm=t[5], rel_bias=t[ROW_RELB, :nrel].reshape(REL_BUCKETS, N_HEADS),
                    sinks=t[ROW_RELB:ROW_RELB + 1, nrel:nrel + N_HEADS], conv_w=conv)

    names = dict(gu1="ffn1_w_gu", d1="ffn1_w_down", win="w_in", out="w_out", q="xattn_wq", kv="xattn_wkv",
                 o="xattn_wo", gu2="ffn2_w_gu", d2="ffn2_w_down")
    results = []
    for idx in range(4):
        leaves = unpack(small_out[idx])
        leaves.update({names[k]: big_out[k][idx][None] for k in order})
        results.append(leaves)
    weights = ("rel_bias", "ffn1_norm", "ffn1_w_gu", "ffn1_w_down", "mix_norm", "w_in", "sinks", "conv_w", "w_out",
               "xattn_norm", "mem_norm", "xattn_wq", "xattn_wkv", "xattn_wo", "ffn2_norm", "ffn2_w_gu", "ffn2_w_down",
               "final_norm")
    loss = small_out[0][ROW_LOSS, 0]
    return (loss, dx[None], *[leaves[n] for leaves in results for n in weights])
```

```python
import math

import numpy as np
import jax
import jax.numpy as jnp
from jax import lax
from jax.experimental import pallas as pl
from jax.experimental.pallas import tpu as pltpu

F32, BF16 = jnp.float32, jnp.bfloat16
MESH = pl.DeviceIdType.MESH

D = 1024
N_DEV = 8
D_FF = 2816
FS = D_FF // 4
HEAD = 64
N_HEADS, N_KV = 16, 4
BLK = 128
NQ, NKV = N_HEADS * HEAD, 2 * N_KV * HEAD
XH, XHD = 4, 256
REL_BUCKETS, REL_EXACT, REL_MAX_DIST = 32, 16, 128
EPS, NEG = 1e-6, -1e30
ADAM_LR, ADAM_B1, ADAM_B2, ADAM_EPS, ADAM_WD, ADAM_STEP = 0.001, 0.9, 0.999, 1e-08, 0.01, 10
VMEM_LIMIT_V7X = 56 * 2**20
SMALL_ROWS = 16
ROW_RELB, ROW_LOSS, ROW_CONV = 6, 7, 8


def _bucket_thresholds():
    n = np.arange(REL_MAX_DIST)
    nf = np.maximum(n, 1).astype(np.float32)
    large = REL_EXACT + (np.log(nf / np.float32(REL_EXACT)) / np.float32(math.log(REL_MAX_DIST / REL_EXACT))
                         * np.float32(REL_BUCKETS - REL_EXACT)).astype(np.int32)
    b = np.where(n < REL_EXACT, n, np.minimum(large, REL_BUCKETS - 1))
    return [int(np.argmax(b >= REL_EXACT + k)) for k in range(1, REL_BUCKETS - REL_EXACT)]


BUCKET_THRESHOLDS = _bucket_thresholds()


HBM_SPEC = pl.BlockSpec(memory_space=pl.ANY)


class _Carry:
    def __init__(self, ins, outs, sems, start, finish, mid=None, aliases=None):
        self.ins, self.outs, self.sems = list(ins), list(outs), list(sems)
        self.start, self.finish, self.mid, self.aliases = start, finish, mid, dict(aliases or {})


def _pcall(body, *, name, grid, in_specs, out_specs, out_shape, scratch=(), carry=None, aliases=None, behind=None):
    params = pltpu.CompilerParams(dimension_semantics=("arbitrary",) * len(grid), vmem_limit_bytes=VMEM_LIMIT_V7X)
    if carry is None and behind is not None:
        n_in = len(in_specs)
        call = pl.pallas_call(lambda *refs: body(*refs[:n_in], *refs[n_in + 1:]), name=name, grid=grid,
                              in_specs=list(in_specs) + [pl.BlockSpec((8, 128), lambda *_: (0, 0))],
                              out_specs=out_specs, out_shape=out_shape, scratch_shapes=list(scratch),
                              compiler_params=params, input_output_aliases=aliases or {})
        return lambda *args: call(*args, behind)
    if carry is None:
        return pl.pallas_call(body, name=name, grid=grid, in_specs=in_specs, out_specs=out_specs,
                              out_shape=out_shape, scratch_shapes=list(scratch), compiler_params=params,
                              input_output_aliases=aliases or {})
    assert aliases is None and behind is None, name
    single = not isinstance(out_shape, (list, tuple))
    own_specs, own_shapes = ([out_specs], [out_shape]) if single else (list(out_specs), list(out_shape))
    n_in, n_out, n_scr = len(in_specs), len(own_shapes), len(scratch)
    n_cin, n_cout = len(carry.ins), len(carry.outs)
    steps = math.prod(grid)
    mid_step = max(steps - 1 - max(steps // 8, 1), 0)

    def carrying(*refs):
        ins, refs = refs[:n_in], refs[n_in:]
        cins, refs = refs[:n_cin], refs[n_cin:]
        outs, refs = refs[:n_out], refs[n_out:]
        couts, refs = refs[:n_cout], refs[n_cout:]
        scr, csems = refs[:n_scr], refs[n_scr:]
        step = 0
        for axis, size in enumerate(grid):
            step = step * size + pl.program_id(axis)

        @pl.when(step == 0)
        def _():
            carry.start(cins, couts, csems)

        body(*ins, *outs, *scr)
        if carry.mid is not None:
            @pl.when(step == mid_step)
            def _():
                carry.mid(cins, couts, csems)

        @pl.when(step == steps - 1)
        def _():
            carry.finish(cins, couts, csems)

    call = pl.pallas_call(carrying, name=name, grid=grid, in_specs=list(in_specs) + [HBM_SPEC] * n_cin,
                          out_specs=own_specs + [HBM_SPEC] * n_cout, out_shape=own_shapes + carry.outs,
                          scratch_shapes=list(scratch) + carry.sems, compiler_params=params,
                          input_output_aliases={n_in + i: n_out + o for i, o in carry.aliases.items()})

    def run(*args):
        res = call(*args, *carry.ins)
        return (res[0] if single else res[:n_out]), res[n_out:]

    return run


def _run_alone(carry, name):
    n_cin, n_cout = len(carry.ins), len(carry.outs)

    def body(*refs):
        cins, couts, csems = refs[:n_cin], refs[n_cin:n_cin + n_cout], refs[n_cin + n_cout:]
        carry.start(cins, couts, csems)
        if carry.mid is not None:
            carry.mid(cins, couts, csems)
        carry.finish(cins, couts, csems)

    return pl.pallas_call(body, name=name, in_specs=[HBM_SPEC] * n_cin, out_specs=[HBM_SPEC] * n_cout,
                          out_shape=carry.outs, scratch_shapes=carry.sems,
                          input_output_aliases=carry.aliases)(*carry.ins)


def _dot(a, b):
    return jnp.dot(a, b, preferred_element_type=F32)


def _dot_nt(a, b):
    return lax.dot_general(a, b, (((1,), (1,)), ((), ())), preferred_element_type=F32)


def _dot_tn(a, b):
    return lax.dot_general(a, b, (((0,), (0,)), ((), ())), preferred_element_type=F32)


def _sds(shape, dtype):
    return jax.ShapeDtypeStruct(tuple(shape), dtype)


ROW_CHUNK = 256


def _row_chunks(tm):
    return [slice(r, min(r + ROW_CHUNK, tm)) for r in range(0, tm, ROW_CHUNK)]


def _carried(call, args, carry):
    return call(*args) if carry is not None else (call(*args), ())


def _rmsnorm(x, g, name, carry=None):
    m, d = x.shape
    tm = min(512, m)

    def body(x_ref, g_ref, h_ref):
        xv = x_ref[...]
        r = lax.rsqrt(jnp.mean(xv * xv, axis=-1, keepdims=True) + EPS)
        h_ref[...] = (xv * r * g_ref[...]).astype(BF16)

    call = _pcall(body, name=name, grid=(m // tm,), carry=carry,
                  in_specs=[pl.BlockSpec((tm, d), lambda i: (i, 0)), pl.BlockSpec((1, d), lambda i: (0, 0))],
                  out_specs=pl.BlockSpec((tm, d), lambda i: (i, 0)), out_shape=_sds((m, d), BF16))
    return _carried(call, (x, g), carry)


def _mm_nn(a, b, name, tm=1024, bt=False, carry=None):
    m, k = a.shape
    nj = b.shape[0]
    n = b.shape[1] if bt else b.shape[2]
    tm = min(tm, m)
    dot = _dot_nt if bt else _dot

    def body(a_ref, b_ref, o_ref):
        o_ref[...] = dot(a_ref[...], b_ref[...]).astype(BF16)

    call = _pcall(body, name=name, grid=(nj, m // tm),
                  in_specs=[pl.BlockSpec((tm, k), lambda j, i: (i, 0)),
                            pl.BlockSpec((None,) + b.shape[1:], lambda j, i: (j, 0, 0))],
                  out_specs=pl.BlockSpec((None, tm, n), lambda j, i: (j, i, 0)),
                  out_shape=_sds((nj, m, n), BF16), carry=carry)
    return _carried(call, (a, b), carry)


def _load_once(src_hbm, dst_vmem, sem):
    @pl.when(pl.program_id(0) == 0)
    def _():
        load = pltpu.make_async_copy(src_hbm, dst_vmem, sem)
        load.start()
        load.wait()


def _resident(w):
    return [pltpu.VMEM(w.shape, w.dtype), pltpu.SemaphoreType.DMA(())]


def _ffn_up(h, w4, name, tm=512, carry=None):
    s, d = h.shape
    tm = min(tm, s)

    def body(h_ref, w_hbm, gu_ref, a_ref, w_ref, w_sem):
        _load_once(w_hbm, w_ref, w_sem)
        for p in range(4):
            for rows in _row_chunks(tm):
                hv = h_ref[rows, :]
                g = _dot_nt(hv, w_ref[0, p])
                u = _dot_nt(hv, w_ref[1, p])
                gu_ref[0, p, rows, :] = g.astype(BF16)
                gu_ref[1, p, rows, :] = u.astype(BF16)
                a_ref[p, rows, :] = (g * jax.nn.sigmoid(g) * u).astype(BF16)

    call = _pcall(body, name=name, grid=(s // tm,),
                  in_specs=[pl.BlockSpec((tm, d), lambda i: (i, 0)), HBM_SPEC],
                  out_specs=[pl.BlockSpec((2, 4, tm, FS), lambda i: (0, 0, i, 0)),
                             pl.BlockSpec((4, tm, FS), lambda i: (0, i, 0))],
                  out_shape=[_sds((2, 4, s, FS), BF16), _sds((4, s, FS), BF16)], scratch=_resident(w4), carry=carry)
    return _carried(call, (h, w4), carry)


N_SEG = 5
IN_PROJ_WEIGHTS = [pltpu.VMEM((NQ, D), BF16), pltpu.VMEM((NKV, D), BF16), pltpu.VMEM((N_SEG, D, D), BF16),
                   pltpu.SemaphoreType.DMA((2 + N_SEG,))]


def _load_in_proj(w_hbm, wq_ref, wkv_ref, wa_ref, sems):
    @pl.when(pl.program_id(0) == 0)
    def _():
        loads = [pltpu.make_async_copy(w_hbm.at[pl.ds(0, NQ)], wq_ref, sems.at[0]),
                 pltpu.make_async_copy(w_hbm.at[pl.ds(NQ, NKV)], wkv_ref, sems.at[1])]
        loads += [pltpu.make_async_copy(w_hbm.at[pl.ds(NQ + NKV + D * j, D)], wa_ref.at[j], sems.at[2 + j])
                  for j in range(N_SEG)]
        for load in loads:
            load.start()
        for load in loads:
            load.wait()


def _in_proj(h, w_in_t, name, tm=512, carry=None):
    s, d = h.shape
    tm = min(tm, s)

    def body(h_ref, w_hbm, pa_ref, q_ref, kv_ref, wq_ref, wkv_ref, wa_ref, sems):
        _load_in_proj(w_hbm, wq_ref, wkv_ref, wa_ref, sems)
        hv = h_ref[...]
        q_ref[...] = _dot_nt(hv, wq_ref[...]).astype(BF16)
        kv_ref[...] = _dot_nt(hv, wkv_ref[...]).astype(BF16)
        for j in range(N_SEG):
            pa_ref[j] = _dot_nt(hv, wa_ref[j]).astype(BF16)

    call = _pcall(body, name=name, grid=(s // tm,), carry=carry,
                  in_specs=[pl.BlockSpec((tm, d), lambda i: (i, 0)), HBM_SPEC],
                  out_specs=[pl.BlockSpec((N_SEG, tm, d), lambda i: (0, i, 0)),
                             pl.BlockSpec((tm, NQ), lambda i: (i, 0)), pl.BlockSpec((tm, NKV), lambda i: (i, 0))],
                  out_shape=[_sds((N_SEG, s, d), BF16), _sds((s, NQ), BF16), _sds((s, NKV), BF16)],
                  scratch=IN_PROJ_WEIGHTS)
    return _carried(call, (h, w_in_t), carry)


def _mm_res_norm(a, w, xres, gain, scale, name, tm=512, carry=None):
    npart, s, kp = a.shape
    tm = min(tm, s)

    def body(a_ref, w_ref, x_ref, g_ref, xo_ref, h_ref):
        for rows in _row_chunks(tm):
            acc = _dot(a_ref[0, rows, :], w_ref[0])
            for p in range(1, npart):
                acc = acc + _dot(a_ref[p, rows, :], w_ref[p])
            xn = x_ref[rows, :] + scale * acc
            xo_ref[rows, :] = xn
            r = lax.rsqrt(jnp.mean(xn * xn, axis=-1, keepdims=True) + EPS)
            h_ref[rows, :] = (xn * r * g_ref[...]).astype(BF16)

    call = _pcall(body, name=name, grid=(s // tm,),
                  in_specs=[pl.BlockSpec((npart, tm, kp), lambda i: (0, i, 0)),
                            pl.BlockSpec((npart, kp, D), lambda i: (0, 0, 0)),
                            pl.BlockSpec((tm, D), lambda i: (i, 0)),
                            pl.BlockSpec((1, D), lambda i: (0, 0))],
                  out_specs=[pl.BlockSpec((tm, D), lambda i: (i, 0)), pl.BlockSpec((tm, D), lambda i: (i, 0))],
                  out_shape=[_sds((s, D), F32), _sds((s, D), BF16)], carry=carry)
    return _carried(call, (a, w, xres, gain), carry)


def _ffn_down_loss(a, w, xres, gain, target, name, tm=512):
    npart, s, kp = a.shape
    tm = min(tm, s)

    def body(a_ref, w_ref, x_ref, g_ref, t_ref, dx_ref, dxb_ref, loss_ref, dg_ref):
        @pl.when(pl.program_id(0) == 0)
        def _():
            loss_ref[...] = jnp.zeros_like(loss_ref)
            dg_ref[...] = jnp.zeros_like(dg_ref)

        for rows in _row_chunks(tm):
            acc = _dot(a_ref[0, rows, :], w_ref[0])
            for p in range(1, npart):
                acc = acc + _dot(a_ref[p, rows, :], w_ref[p])
            xn = x_ref[rows, :] + 0.5 * acc
            r = lax.rsqrt(jnp.mean(xn * xn, axis=-1, keepdims=True) + EPS)
            xh = xn * r
            gv = g_ref[...]
            err = xh * gv - t_ref[rows, :]
            part = 0.5 * jnp.sum(jnp.mean(err * err, axis=-1, keepdims=True), axis=0, keepdims=True)
            dy = err * (1.0 / D)
            dyg = dy * gv
            dxn = r * (dyg - xh * jnp.mean(dyg * xh, axis=-1, keepdims=True))
            dx_ref[rows, :] = dxn
            dxb_ref[rows, :] = dxn.astype(BF16)
            loss_ref[...] += jnp.broadcast_to(part, loss_ref.shape)
            dg_ref[...] += jnp.sum(dy * xh, axis=0, keepdims=True)

    return _pcall(body, name=name, grid=(s // tm,),
                  in_specs=[pl.BlockSpec((npart, tm, kp), lambda i: (0, i, 0)),
                            pl.BlockSpec((npart, kp, D), lambda i: (0, 0, 0)),
                            pl.BlockSpec((tm, D), lambda i: (i, 0)),
                            pl.BlockSpec((1, D), lambda i: (0, 0)),
                            pl.BlockSpec((tm, D), lambda i: (i, 0))],
                  out_specs=[pl.BlockSpec((tm, D), lambda i: (i, 0)), pl.BlockSpec((tm, D), lambda i: (i, 0)),
                             pl.BlockSpec((8, 128), lambda i: (0, 0)), pl.BlockSpec((1, D), lambda i: (0, 0))],
                  out_shape=[_sds((s, D), F32), _sds((s, D), BF16), _sds((8, 128), F32), _sds((1, D), F32)],
                  )(a, w, xres, gain, target)


def _window_tiles():
    i = lax.broadcasted_iota(jnp.int32, (BLK, BLK), 0)
    j = lax.broadcasted_iota(jnp.int32, (BLK, BLK), 1)
    rel = (i - j) & (BLK - 1)
    large = jnp.full_like(rel, REL_EXACT)
    for t in BUCKET_THRESHOLDS:
        large = large + (rel >= t).astype(jnp.int32)
    return j <= i, jnp.where(rel < REL_EXACT, rel, large)


def _bias_build(rel_bias, name):
    def body(rb_ref, o_ref):
        _, bucket = _window_tiles()

        def per_head(h, carry):
            acc = jnp.zeros((BLK, BLK), F32)
            for b in range(REL_BUCKETS):
                acc = jnp.where(bucket == b, rb_ref[b, h], acc)
            o_ref[h] = acc
            return carry

        lax.fori_loop(0, N_HEADS, per_head, 0)

    return _pcall(body, name=name, grid=(1,),
                  in_specs=[pl.BlockSpec(memory_space=pltpu.SMEM)],
                  out_specs=pl.BlockSpec((N_HEADS, BLK, BLK), lambda i: (0, 0, 0)),
                  out_shape=_sds((N_HEADS, BLK, BLK), F32))(rel_bias)


def _bias_bwd(dbias, name):
    def body(db_ref, o_ref):
        _, bucket = _window_tiles()
        lane = lax.broadcasted_iota(jnp.int32, (N_HEADS, 128), 1)

        def per_bucket(b, out):
            mb = (bucket == b).astype(F32)
            per_col = jnp.sum(db_ref[...] * mb[None, :, :], axis=1)
            return jnp.where(lane == b, jnp.sum(per_col, axis=1, keepdims=True), out)

        o_ref[...] = lax.fori_loop(0, REL_BUCKETS, per_bucket, jnp.zeros((N_HEADS, 128), F32))

    return _pcall(body, name=name, grid=(1,),
                  in_specs=[pl.BlockSpec((N_HEADS, BLK, BLK), lambda i: (0, 0, 0))],
                  out_specs=pl.BlockSpec((N_HEADS, 128), lambda i: (0, 0)),
                  out_shape=_sds((N_HEADS, 128), F32))(dbias)


PAIR = 2 * HEAD
GROUP = N_HEADS // N_KV
SWA_SCALE = HEAD ** -0.5


def _window_masks(n):
    i = lax.broadcasted_iota(jnp.int32, (GROUP * BLK, BLK), 0) & (BLK - 1)
    j = lax.broadcasted_iota(jnp.int32, (GROUP * BLK, BLK), 1)
    return j <= i, jnp.logical_and(n == 0, j > i), j < HEAD


def _kv_twice(ref, base, g, low):
    slab = ref[:, base + PAIR * (g // 2): base + PAIR * (g // 2 + 1)]
    swapped = pltpu.roll(slab, HEAD, 1)
    return jnp.where(low, slab, swapped) if g % 2 == 0 else jnp.where(low, swapped, slab)


def _stack_heads(ref, g, low):
    parts = []
    for r in range(2):
        slab = ref[:, PAIR * (2 * g + r): PAIR * (2 * g + r + 1)]
        zero = jnp.zeros_like(slab)
        parts += [jnp.where(low, slab, zero), jnp.where(low, zero, slab)]
    return jnp.concatenate(parts, axis=0)


def _unstack_heads(t, low):
    return [jnp.where(low, t[2 * r * BLK:(2 * r + 1) * BLK], t[(2 * r + 1) * BLK:(2 * r + 2) * BLK])
            for r in range(2)]


def _head_rows(t, k):
    return t[k * BLK:(k + 1) * BLK]


def _per_head_column(values):
    head = lax.broadcasted_iota(jnp.int32, (GROUP * BLK, 1), 0) // BLK
    col = jnp.full((GROUP * BLK, 1), values[0], F32)
    for k in range(1, GROUP):
        col = jnp.where(head == k, values[k], col)
    return col


def _window_logits(q4, kc, kp, bias4, own, absent):
    sc = jnp.where(own, _dot_nt(q4, kc), _dot_nt(q4, kp)) * SWA_SCALE + bias4
    return jnp.where(absent, NEG, sc)


def _split_window(t, own):
    zero = jnp.zeros_like(t)
    return jnp.where(own, t, zero), jnp.where(own, zero, t)


def _swa_fwd(q, kv, bias, sinks, name, carry=None):
    s = q.shape[0]
    nb = s // BLK
    kvw = 2 * N_KV * HEAD

    def body(q_ref, kc_ref, kp_ref, b_ref, sk_ref, o_ref, lse_ref):
        own, absent, low4 = _window_masks(pl.program_id(0))
        low = low4[:BLK]
        lane = lax.broadcasted_iota(jnp.int32, (BLK, 128), 1)
        lse_t = jnp.zeros((BLK, 128), F32)
        for g in range(N_KV):
            q4 = _stack_heads(q_ref, g, low)
            kc, kp = _kv_twice(kc_ref, 0, g, low), _kv_twice(kp_ref, 0, g, low)
            vc, vp = _kv_twice(kc_ref, N_KV * HEAD, g, low), _kv_twice(kp_ref, N_KV * HEAD, g, low)
            bias4 = b_ref[GROUP * g:GROUP * (g + 1)].reshape(GROUP * BLK, BLK)
            sc = _window_logits(q4, kc, kp, bias4, own, absent)
            sk = _per_head_column([sk_ref[0, GROUP * g + k] for k in range(GROUP)])
            m = jnp.maximum(jnp.max(sc, axis=1, keepdims=True), sk)
            p = jnp.exp(sc - m)
            l = jnp.sum(p, axis=1, keepdims=True) + jnp.exp(sk - m)
            p_own, p_prev = _split_window(p.astype(BF16), own)
            out = (_dot(p_own, vc) + _dot(p_prev, vp)) * (1.0 / l)
            for r, slab in enumerate(_unstack_heads(out, low)):
                o_ref[:, PAIR * (2 * g + r): PAIR * (2 * g + r + 1)] = slab.astype(BF16)
            lse4 = m + jnp.log(l)
            for k in range(GROUP):
                lse_t = jnp.where(lane == GROUP * g + k, _head_rows(lse4, k), lse_t)
        lse_ref[...] = lse_t

    call = _pcall(body, name=name, grid=(nb,),
                  in_specs=[pl.BlockSpec((BLK, D), lambda n: (n, 0)),
                            pl.BlockSpec((BLK, kvw), lambda n: (n, 0)),
                            pl.BlockSpec((BLK, kvw), lambda n: (jnp.maximum(n - 1, 0), 0)),
                            pl.BlockSpec((N_HEADS, BLK, BLK), lambda n: (0, 0, 0)),
                            pl.BlockSpec(memory_space=pltpu.SMEM)],
                  out_specs=[pl.BlockSpec((BLK, D), lambda n: (n, 0)), pl.BlockSpec((BLK, 128), lambda n: (n, 0))],
                  out_shape=[_sds((s, D), BF16), _sds((s, 128), F32)], carry=carry)
    return _carried(call, (q, kv, kv, bias, sinks), carry)


def _fold_halves(t, g, low):
    folded = jnp.where(low, t, 0.0) + pltpu.roll(jnp.where(low, 0.0, t), HEAD, 1)
    return folded if g % 2 == 0 else pltpu.roll(folded, HEAD, 1)


def _swa_bwd(q, kv, attn, dattn, lse, bias, sinks, name, carry=None):
    s = q.shape[0]
    nb = s // BLK
    kvw = 2 * N_KV * HEAD
    voff = N_KV * HEAD

    def body(q_ref, kc_ref, kp_ref, o_ref, do_ref, lse_ref, b_ref, skrow_ref, dq_ref, dkv_ref, dbias_ref, dsk_ref,
             dq_hold, kv_hold, dq_new, kv_prev, kv_cur):
        n = pl.program_id(0)

        @pl.when(n == 0)
        def _():
            dbias_ref[...] = jnp.zeros_like(dbias_ref)
            dsk_ref[...] = jnp.zeros_like(dsk_ref)
            dq_hold[...] = jnp.zeros_like(dq_hold)
            kv_hold[...] = jnp.zeros_like(kv_hold)

        @pl.when(n < nb)
        def _():
            own, absent, low4 = _window_masks(n)
            low = low4[:BLK]
            lane = lax.broadcasted_iota(jnp.int32, (BLK, 128), 1)
            delta_t = jnp.zeros((BLK, 128), F32)
            ones = jnp.ones((PAIR, 128), BF16)
            for pair_of_kv in range(N_KV // 2):
                slab_grads = [jnp.zeros((BLK, PAIR), F32) for _ in range(4)]
                for g in (2 * pair_of_kv, 2 * pair_of_kv + 1):
                    q4, do4 = _stack_heads(q_ref, g, low), _stack_heads(do_ref, g, low)
                    kc, kp = _kv_twice(kc_ref, 0, g, low), _kv_twice(kp_ref, 0, g, low)
                    vc, vp = _kv_twice(kc_ref, voff, g, low), _kv_twice(kp_ref, voff, g, low)
                    o_slabs = [o_ref[:, PAIR * (2 * g + r): PAIR * (2 * g + r + 1)] for r in range(2)]
                    o4 = jnp.concatenate([o_slabs[0], o_slabs[0], o_slabs[1], o_slabs[1]], axis=0)
                    delta = _dot(do4 * o4, ones)
                    heads = range(GROUP * g, GROUP * (g + 1))
                    lse4 = jnp.concatenate([lse_ref[:, h:h + 1] for h in heads], axis=0)
                    bias4 = b_ref[GROUP * g:GROUP * (g + 1)].reshape(GROUP * BLK, BLK)
                    p = jnp.exp(_window_logits(q4, kc, kp, bias4, own, absent) - lse4)
                    dp = jnp.where(own, _dot_nt(do4, vc), _dot_nt(do4, vp))
                    ds = p * (dp - delta)
                    dbias_ref[GROUP * g:GROUP * (g + 1)] += ds.reshape(GROUP, BLK, BLK)
                    for k, h in enumerate(heads):
                        delta_t = jnp.where(lane == h, _head_rows(delta, k), delta_t)
                    ds_own, ds_prev = _split_window((ds * SWA_SCALE).astype(BF16), own)
                    p_own, p_prev = _split_window(p.astype(BF16), own)
                    dq4 = _dot(ds_own, kc) + _dot(ds_prev, kp)
                    for r, slab in enumerate(_unstack_heads(dq4, low)):
                        dq_new[:, PAIR * (2 * g + r): PAIR * (2 * g + r + 1)] = slab
                    grads = [_dot_tn(ds_own, q4), _dot_tn(ds_prev, q4), _dot_tn(p_own, do4), _dot_tn(p_prev, do4)]
                    slab_grads = [t + _fold_halves(dk, g, low) for t, dk in zip(slab_grads, grads)]
                ks = slice(PAIR * pair_of_kv, PAIR * (pair_of_kv + 1))
                vs = slice(voff + PAIR * pair_of_kv, voff + PAIR * (pair_of_kv + 1))
                kv_cur[:, ks], kv_prev[:, ks], kv_cur[:, vs], kv_prev[:, vs] = slab_grads
            dsk_ref[...] -= jnp.sum(jnp.exp(skrow_ref[...] - lse_ref[...]) * delta_t, axis=0, keepdims=True)

        @pl.when(n == nb)
        def _():
            kv_prev[...] = jnp.zeros_like(kv_prev)

        dq_ref[...] = dq_hold[...].astype(BF16)
        dkv_ref[...] = (kv_hold[...] + kv_prev[...]).astype(BF16)

        @pl.when(n < nb)
        def _():
            dq_hold[...] = dq_new[...]
            kv_hold[...] = kv_cur[...]

    def cur(n):
        return jnp.minimum(n, nb - 1)

    call = _pcall(body, name=name, grid=(nb + 1,), carry=carry,
                  in_specs=[pl.BlockSpec((BLK, D), lambda n: (cur(n), 0)),
                            pl.BlockSpec((BLK, kvw), lambda n: (cur(n), 0)),
                            pl.BlockSpec((BLK, kvw), lambda n: (jnp.maximum(cur(n) - 1, 0), 0)),
                            pl.BlockSpec((BLK, D), lambda n: (cur(n), 0)),
                            pl.BlockSpec((BLK, D), lambda n: (cur(n), 0)),
                            pl.BlockSpec((BLK, 128), lambda n: (cur(n), 0)),
                            pl.BlockSpec((N_HEADS, BLK, BLK), lambda n: (0, 0, 0)),
                            pl.BlockSpec((1, 128), lambda n: (0, 0))],
                  out_specs=[pl.BlockSpec((BLK, D), lambda n: (jnp.maximum(n - 1, 0), 0)),
                             pl.BlockSpec((BLK, kvw), lambda n: (jnp.maximum(n - 1, 0), 0)),
                             pl.BlockSpec((N_HEADS, BLK, BLK), lambda n: (0, 0, 0)),
                             pl.BlockSpec((1, 128), lambda n: (0, 0))],
                  out_shape=[_sds((s, D), BF16), _sds((s, kvw), BF16), _sds((N_HEADS, BLK, BLK), F32),
                             _sds((1, 128), F32)],
                  scratch=[pltpu.VMEM((BLK, D), F32), pltpu.VMEM((BLK, kvw), F32), pltpu.VMEM((BLK, D), F32),
                           pltpu.VMEM((BLK, kvw), F32), pltpu.VMEM((BLK, kvw), F32)])
    sink_row = jnp.pad(sinks, ((0, 0), (0, 128 - N_HEADS)))
    return _carried(call, (q, kv, kv, attn, dattn, lse, bias, sink_row), carry)


HALO = 16
CW = D


def _conv_taps(cu, halo_cu, first_tile):
    row = lax.broadcasted_iota(jnp.int32, cu.shape, 0)
    halo_cu = jnp.where(first_tile, 0.0, halo_cu)
    c1 = jnp.where(row == 0, halo_cu[HALO - 1:HALO], pltpu.roll(cu, 1, 0))
    c2 = jnp.where(row == 0, halo_cu[HALO - 2:HALO - 1],
                   jnp.where(row == 1, halo_cu[HALO - 1:HALO], pltpu.roll(cu, 2, 0)))
    return c1, c2


def _conv_merge_fwd(pa, attn, convw, name, ts=256, carry=None):
    _, s, _ = pa.shape
    ts = min(ts, s)
    hb = ts // HALO

    def body(pa_ref, hp_ref, at_ref, w_ref, o_ref):
        i = pl.program_id(1)
        cu = pa_ref[0].astype(F32) * pa_ref[2].astype(F32)
        c1, c2 = _conv_taps(cu, hp_ref[0].astype(F32) * hp_ref[2].astype(F32), i == 0)
        w = w_ref[...]
        c3 = w[0:1] * c2 + w[1:2] * c1 + w[2:3] * cu
        conv = pa_ref[1].astype(F32) * c3
        o_ref[...] = (jax.nn.sigmoid(pa_ref[3].astype(F32)) * at_ref[...].astype(F32)
                      + jax.nn.sigmoid(pa_ref[4].astype(F32)) * conv).astype(BF16)

    call = _pcall(body, name=name, grid=(D // CW, s // ts), carry=carry,
                  in_specs=[pl.BlockSpec((5, ts, CW), lambda c, i: (0, i, c)),
                            pl.BlockSpec((5, HALO, CW), lambda c, i: (0, jnp.maximum(i * hb - 1, 0), c)),
                            pl.BlockSpec((ts, CW), lambda c, i: (i, c)),
                            pl.BlockSpec((8, CW), lambda c, i: (0, c))],
                  out_specs=pl.BlockSpec((ts, CW), lambda c, i: (i, c)),
                  out_shape=_sds((s, D), BF16))
    return _carried(call, (pa, pa, attn, convw), carry)


def _conv_merge_bwd(dmerged, pa, attn, convw, name, ts=256, carry=None):
    _, s, _ = pa.shape
    ts = min(ts, s)
    hb = ts // HALO
    last_hb = s // HALO - 1

    def body(dm_ref, pa_ref, at_ref, w_ref, hp_ref, hn_ref, dmn_ref, dat_ref, dpa_ref, dw_ref):
        i = pl.program_id(1)
        last = i == pl.num_programs(1) - 1
        dm = dm_ref[...].astype(F32)
        cp, bp, u = pa_ref[0].astype(F32), pa_ref[1].astype(F32), pa_ref[2].astype(F32)
        sa = jax.nn.sigmoid(pa_ref[3].astype(F32))
        sc = jax.nn.sigmoid(pa_ref[4].astype(F32))
        at = at_ref[...].astype(F32)
        cu = cp * u
        c1, c2 = _conv_taps(cu, hp_ref[0].astype(F32) * hp_ref[2].astype(F32), i == 0)
        w = w_ref[...]
        c3 = w[0:1] * c2 + w[1:2] * c1 + w[2:3] * cu
        dconv = dm * sc
        dc3 = dconv * bp
        nxt = dmn_ref[...].astype(F32) * jax.nn.sigmoid(hn_ref[4].astype(F32)) * hn_ref[1].astype(F32)
        nxt = jnp.where(last, 0.0, nxt)
        row = lax.broadcasted_iota(jnp.int32, dc3.shape, 0)
        d1 = jnp.where(row == ts - 1, nxt[0:1], pltpu.roll(dc3, ts - 1, 0))
        d2 = jnp.where(row == ts - 2, nxt[0:1], jnp.where(row == ts - 1, nxt[1:2], pltpu.roll(dc3, ts - 2, 0)))
        dcu = w[2:3] * dc3 + w[1:2] * d1 + w[0:1] * d2
        dat_ref[...] = (dm * sa).astype(BF16)
        dpa_ref[0] = (dcu * u).astype(BF16)
        dpa_ref[1] = (dconv * c3).astype(BF16)
        dpa_ref[2] = (dcu * cp).astype(BF16)
        dpa_ref[3] = (dm * at * sa * (1.0 - sa)).astype(BF16)
        dpa_ref[4] = (dm * bp * c3 * sc * (1.0 - sc)).astype(BF16)

        @pl.when(i == 0)
        def _():
            dw_ref[...] = jnp.zeros_like(dw_ref)

        dw_ref[0:1, :] += jnp.sum(dc3 * c2, axis=0, keepdims=True)
        dw_ref[1:2, :] += jnp.sum(dc3 * c1, axis=0, keepdims=True)
        dw_ref[2:3, :] += jnp.sum(dc3 * cu, axis=0, keepdims=True)

    call = _pcall(body, name=name, grid=(D // CW, s // ts), carry=carry,
                  in_specs=[pl.BlockSpec((ts, CW), lambda c, i: (i, c)),
                            pl.BlockSpec((5, ts, CW), lambda c, i: (0, i, c)),
                            pl.BlockSpec((ts, CW), lambda c, i: (i, c)),
                            pl.BlockSpec((8, CW), lambda c, i: (0, c)),
                            pl.BlockSpec((5, HALO, CW), lambda c, i: (0, jnp.maximum(i * hb - 1, 0), c)),
                            pl.BlockSpec((5, HALO, CW), lambda c, i: (0, jnp.minimum((i + 1) * hb, last_hb), c)),
                            pl.BlockSpec((HALO, CW), lambda c, i: (jnp.minimum((i + 1) * hb, last_hb), c))],
                  out_specs=[pl.BlockSpec((ts, CW), lambda c, i: (i, c)),
                             pl.BlockSpec((5, ts, CW), lambda c, i: (0, i, c)),
                             pl.BlockSpec((8, CW), lambda c, i: (0, c))],
                  out_shape=[_sds((s, D), BF16), _sds((5, s, D), BF16), _sds((8, D), F32)])
    return _carried(call, (dmerged, pa, attn, convw, pa, pa, dmerged), carry)


def _xattn_fwd(q, kv, name, tq=512):
    s, _ = q.shape
    nm = kv.shape[1]
    tq = min(tq, s)

    def body(q_ref, kv_ref, o_ref, lse_ref):
        lane = lax.broadcasted_iota(jnp.int32, (tq, 128), 1)
        lse_t = jnp.zeros((tq, 128), F32)
        for h in range(XH):
            hs = slice(XHD * h, XHD * (h + 1))
            sc = _dot_nt(q_ref[:, hs], kv_ref[h]) * (XHD ** -0.5)
            m = jnp.max(sc, axis=1, keepdims=True)
            p = jnp.exp(sc - m)
            l = jnp.sum(p, axis=1, keepdims=True)
            o_ref[:, hs] = (_dot(p.astype(BF16), kv_ref[XH + h]) * (1.0 / l)).astype(BF16)
            lse_t = jnp.where(lane == h, m + jnp.log(l), lse_t)
        lse_ref[...] = lse_t

    return _pcall(body, name=name, grid=(s // tq,),
                  in_specs=[pl.BlockSpec((tq, D), lambda i: (i, 0)), pl.BlockSpec((2 * XH, nm, XHD), lambda i: (0, 0, 0))],
                  out_specs=[pl.BlockSpec((tq, D), lambda i: (i, 0)), pl.BlockSpec((tq, 128), lambda i: (i, 0))],
                  out_shape=[_sds((s, D), BF16), _sds((s, 128), F32)])(q, kv)


def _xattn_bwd(q, kv, o, do, lse, name, tq=512, carry=None):
    s, _ = q.shape
    nm = kv.shape[1]
    tq = min(tq, s)

    def body(q_ref, kv_ref, o_ref, do_ref, lse_ref, dq_ref, dkv_ref):
        @pl.when(pl.program_id(0) == 0)
        def _():
            dkv_ref[...] = jnp.zeros_like(dkv_ref)

        for h in range(XH):
            hs = slice(XHD * h, XHD * (h + 1))
            qh, kh, vh, dob = q_ref[:, hs], kv_ref[h], kv_ref[XH + h], do_ref[:, hs]
            p = jnp.exp(_dot_nt(qh, kh) * (XHD ** -0.5) - lse_ref[:, h:h + 1])
            dp = _dot_nt(dob, vh)
            delta = jnp.sum(dob.astype(F32) * o_ref[:, hs].astype(F32), axis=1, keepdims=True)
            dsb = (p * (dp - delta) * (XHD ** -0.5)).astype(BF16)
            dq_ref[:, hs] = _dot(dsb, kh).astype(BF16)
            dkv_ref[h] += _dot_tn(dsb, qh)
            dkv_ref[XH + h] += _dot_tn(p.astype(BF16), dob)

    call = _pcall(body, name=name, grid=(s // tq,), carry=carry,
                  in_specs=[pl.BlockSpec((tq, D), lambda i: (i, 0)), pl.BlockSpec((2 * XH, nm, XHD), lambda i: (0, 0, 0)),
                            pl.BlockSpec((tq, D), lambda i: (i, 0)), pl.BlockSpec((tq, D), lambda i: (i, 0)),
                            pl.BlockSpec((tq, 128), lambda i: (i, 0))],
                  out_specs=[pl.BlockSpec((tq, D), lambda i: (i, 0)), pl.BlockSpec((2 * XH, nm, XHD), lambda i: (0, 0, 0))],
                  out_shape=[_sds((s, D), BF16), _sds((2 * XH, nm, XHD), F32)])
    return _carried(call, (q, kv, o, do, lse), carry)


def _ffn_down_bwd(dxb, wd4, gu4, name, tm=512, carry=None, behind=None):
    s, _ = dxb.shape
    tm = min(tm, s)

    def body(dx_ref, w_hbm, gu_ref, o_ref, w_ref, w_sem):
        _load_once(w_hbm, w_ref, w_sem)
        for p in range(4):
            for rows in _row_chunks(tm):
                da = _dot_nt(dx_ref[rows, :], w_ref[p])
                g = gu_ref[0, p, rows, :].astype(F32)
                u = gu_ref[1, p, rows, :].astype(F32)
                sg = jax.nn.sigmoid(g)
                t = da * sg
                o_ref[0, p, rows, :] = (t * u * (1.0 + g - g * sg)).astype(BF16)
                o_ref[1, p, rows, :] = (t * g).astype(BF16)

    block = pl.BlockSpec((2, 4, tm, FS), lambda i: (0, 0, i, 0))
    call = _pcall(body, name=name, grid=(s // tm,), carry=carry, behind=behind,
                  in_specs=[pl.BlockSpec((tm, D), lambda i: (i, 0)), HBM_SPEC, block],
                  out_specs=block, out_shape=_sds((2, 4, s, FS), BF16), scratch=_resident(wd4))
    return _carried(call, (dxb, wd4, gu4), carry)


def _mm_tn(a, b, name, scale=1.0, carry=None):
    pa_n, s, m = a.shape
    pb_n, _, n = b.shape
    po = max(pa_n, pb_n)
    tn = n if po >= 4 else min(n, 256)

    def body(a_ref, b_ref, o_ref):
        o_ref[...] = (scale * _dot_tn(a_ref[...], b_ref[...])).astype(BF16)

    call = _pcall(body, name=name, grid=(po, n // tn), carry=carry,
                  in_specs=[pl.BlockSpec((None, s, m), lambda o, j: (o if pa_n > 1 else 0, 0, 0)),
                            pl.BlockSpec((None, s, tn), lambda o, j: (o if pb_n > 1 else 0, 0, j))],
                  out_specs=pl.BlockSpec((None, m, tn), lambda o, j: (o, 0, j)),
                  out_shape=_sds((po, m, n), BF16))
    return _carried(call, (a, b), carry)


def _mm_tn_rows(a, b, name, total_rows, row0, begun=None, tm=512, carry=None):
    p, s, m = a.shape
    n = b.shape[1]
    tm = min(tm, m)
    tiles = m // tm
    assert row0 % tm == 0 and m % tm == 0, (row0, m, tm)

    def body(a_ref, b_ref, *rest):
        rest[-1][...] = _dot_tn(a_ref[...], b_ref[...]).astype(BF16)

    in_specs = [pl.BlockSpec((None, s, tm), lambda o, i: (o, 0, i)), pl.BlockSpec((s, n), lambda o, i: (0, 0))]
    call = _pcall(body, name=name, grid=(p, tiles), in_specs=in_specs + ([HBM_SPEC] if begun is not None else []),
                  out_specs=pl.BlockSpec((tm, n), lambda o, i: (row0 // tm + o * tiles + i, 0)),
                  out_shape=_sds((total_rows, n), BF16), aliases={2: 0} if begun is not None else None, carry=carry)
    return _carried(call, (a, b, begun) if begun is not None else (a, b), carry)


def _sum_dots(a_ref, b_ref, nj, bt, rows=slice(None)):
    dot = _dot_nt if bt else _dot
    acc = dot(a_ref[0, rows, :], b_ref[0])
    for j in range(1, nj):
        acc = acc + dot(a_ref[j, rows, :], b_ref[j])
    return acc


def _mm_acc(a, b, name, out_dtype, tm=512, bt=False, carry=None):
    nj, s, k = a.shape
    n = b.shape[1] if bt else b.shape[2]
    tm = min(tm, s)

    def body(a_ref, b_ref, o_ref):
        o_ref[...] = _sum_dots(a_ref, b_ref, nj, bt).astype(out_dtype)

    call = _pcall(body, name=name, grid=(s // tm,), carry=carry,
                  in_specs=[pl.BlockSpec((nj, tm, k), lambda i: (0, i, 0)),
                            pl.BlockSpec(b.shape, lambda i: (0, 0, 0))],
                  out_specs=pl.BlockSpec((tm, n), lambda i: (i, 0)), out_shape=_sds((s, n), out_dtype))
    return _carried(call, (a, b), carry)


def _rms_bwd_call(name, acts, weights, scratch, load, dh_rows, *, x, gain, dres, tm, carry, behind=None):
    s, n = x.shape
    tm = min(tm, s)
    n_act, n_w = len(acts), len(weights)

    def body(*refs):
        act_refs, w_refs = refs[:n_act], refs[n_act:n_act + n_w]
        x_ref, g_ref, r_ref, dx_ref, dxb_ref, dg_ref = refs[n_act + n_w:n_act + n_w + 6]
        held = refs[n_act + n_w + 6:]
        load(w_refs, held)

        @pl.when(pl.program_id(0) == 0)
        def _():
            dg_ref[...] = jnp.zeros_like(dg_ref)

        for rows in _row_chunks(tm):
            dh = dh_rows(act_refs, held, rows)
            xv = x_ref[rows, :]
            r = lax.rsqrt(jnp.mean(xv * xv, axis=-1, keepdims=True) + EPS)
            xh = xv * r
            dyg = dh * g_ref[...]
            dx = r_ref[rows, :] + r * (dyg - xh * jnp.mean(dyg * xh, axis=-1, keepdims=True))
            dx_ref[rows, :] = dx
            dxb_ref[rows, :] = dx.astype(BF16)
            dg_ref[...] += jnp.sum(dh * xh, axis=0, keepdims=True)

    def tile(a):
        return (pl.BlockSpec((tm, a.shape[1]), lambda i: (i, 0)) if a.ndim == 2
                else pl.BlockSpec((a.shape[0], tm, a.shape[2]), lambda i: (0, i, 0)))

    row = pl.BlockSpec((tm, n), lambda i: (i, 0))
    in_specs = [tile(a) for a in acts] + [HBM_SPEC] * n_w + [row, pl.BlockSpec((1, n), lambda i: (0, 0)), row]
    call = _pcall(body, name=name, grid=(s // tm,), in_specs=in_specs, carry=carry, behind=behind,
                  out_specs=[row, row, pl.BlockSpec((1, n), lambda i: (0, 0))],
                  out_shape=[_sds((s, n), F32), _sds((s, n), BF16), _sds((1, n), F32)], scratch=scratch)
    return _carried(call, tuple(acts) + tuple(weights) + (x, gain, dres), carry)


def _mm_acc_rms_bwd(a, b, name, *, x, gain, dres, scale=None, tm=512, bt=False, carry=None, behind=None):
    def load(w_refs, held):
        _load_once(w_refs[0], held[0], held[1])

    def dh_rows(act_refs, held, rows):
        dh = _sum_dots(act_refs[0], held[0], a.shape[0], bt, rows)
        return dh if scale is None else scale * dh

    return _rms_bwd_call(name, [a], [b], _resident(b), load, dh_rows, x=x, gain=gain, dres=dres, tm=tm, carry=carry,
                         behind=behind)


def _in_proj_bwd(dpa, dq, dkv, w_in_t, name, *, x, gain, dres, tm=512, carry=None, behind=None):
    def load(w_refs, held):
        _load_in_proj(w_refs[0], *held)

    def dh_rows(act_refs, held, rows):
        dpa_ref, dq_ref, dkv_ref = act_refs
        wq_ref, wkv_ref, wa_ref, _ = held
        dh = _dot(dq_ref[rows, :], wq_ref[...]) + _dot(dkv_ref[rows, :], wkv_ref[...])
        return dh + _sum_dots(dpa_ref, wa_ref, N_SEG, False, rows)

    return _rms_bwd_call(name, [dpa, dq, dkv], [w_in_t], IN_PROJ_WEIGHTS, load, dh_rows, x=x, gain=gain, dres=dres,
                         tm=tm, carry=carry, behind=behind)


def _adam(w, g, m, v):
    m2 = ADAM_B1 * m + (1.0 - ADAM_B1) * g
    v2 = ADAM_B2 * v + (1.0 - ADAM_B2) * (g * g)
    m_hat = m2 / (1.0 - ADAM_B1 ** ADAM_STEP)
    v_hat = v2 / (1.0 - ADAM_B2 ** ADAM_STEP)
    delta = -ADAM_LR * (m_hat / (jnp.sqrt(v_hat) + ADAM_EPS) + ADAM_WD * w)
    return delta, m2, v2


def _adamw(parts, w, m, v, name, behind=None, rows_apart=0):
    _, r, c = parts.shape
    tr = max(t for t in range(16, 257, 16) if r % t == 0)
    assert rows_apart == 0 or tr == r, (name, r)

    def body(p_ref, w_ref, m_ref, v_ref, *outs):
        g = p_ref[0].astype(F32)
        for i in range(1, N_DEV):
            g = g + p_ref[i].astype(F32)
        delta, m2, v2 = _adam(w_ref[...], g, m_ref[...], v_ref[...])
        results = (g, delta, m2, v2)
        for o_ref, t in zip(outs[:4], results):
            o_ref[...] = t
        for row in range(rows_apart):
            for o_ref, t in zip(outs[4 + 4 * row:8 + 4 * row], results):
                o_ref[...] = t[row:row + 1]

    blk = pl.BlockSpec((tr, c), lambda i: (i, 0))
    one = pl.BlockSpec((1, c), lambda i: (0, 0))
    res = _pcall(body, name=name, grid=(r // tr,), behind=behind,
                 in_specs=[pl.BlockSpec((N_DEV, tr, c), lambda i: (0, i, 0)), blk, blk, blk],
                 out_specs=[blk] * 4 + [one] * (4 * rows_apart),
                 out_shape=[_sds((r, c), F32)] * 4 + [_sds((1, c), F32)] * (4 * rows_apart))(parts, w, m, v)
    return res if rows_apart == 0 else (res[:4], [res[4 + 4 * row:8 + 4 * row] for row in range(rows_apart)])


def _position():
    return lax.axis_index("x"), lax.axis_index("y"), lax.axis_index("c")


def _slot(px, py, pc):
    return 4 * px + 2 * py + pc


def _row_window(ref, rows):
    r0, r1 = rows
    return ref if (r0, r1) == (0, ref.shape[0]) else ref.at[pl.ds(r0, r1 - r0)]


def _split_items(items):
    sources = [src for src, _, _ in items]
    begun = [(a, dest) for a, (_, _, dest) in enumerate(items) if dest is not None]
    aliases = {len(sources) + k: a for k, (a, _) in enumerate(begun)}
    return sources + [dest for _, dest in begun], [rows for _, rows, _ in items], aliases


def _gather_carry(items):
    na = len(items)
    carry_ins, windows, aliases = _split_items(items)

    def plan(ins, outs, sems):
        send_sems, recv_sems, local_sems = sems
        x, y, c = _position()
        me, sibling = (x, y, c), (x, y, 1 - c)
        chips = [(1 - x, y), (x, 1 - y), (1 - x, 1 - y)]
        ins = [_row_window(ins[a], windows[a]) for a in range(na)]

        def block_rows(a, block):
            return _row_window(outs[a].at[_slot(*block)], windows[a])

        def copy(a, k, block, to, src=None):
            rows = block_rows(a, block)
            return pltpu.make_async_remote_copy(src_ref=rows if src is None else src, dst_ref=rows,
                                                send_sem=send_sems.at[k, a], recv_sem=recv_sems.at[k, a],
                                                device_id=to, device_id_type=MESH)

        mine = [pltpu.make_async_copy(ins[a], block_rows(a, me), local_sems.at[a]) for a in range(na)]
        first = [copy(a, 0, me, sibling, src=ins[a]) for a in range(na)]
        for j, chip in enumerate(chips):
            first += [copy(a, 1 + j, me, (*chip, c), src=ins[a]) for a in range(na)]
        landed = [[copy(a, 1 + j, (*chip, c), me) for a in range(na)] for j, chip in enumerate(chips)]
        passed = [[copy(a, 4 + j, (*chip, c), sibling) for a in range(na)] for j, chip in enumerate(chips)]
        from_sibling = [copy(a, 0, sibling, me) for a in range(na)]
        for j, chip in enumerate(chips):
            from_sibling += [copy(a, 4 + j, (*chip, 1 - c), me) for a in range(na)]
        return mine, first, landed, passed, from_sibling

    def start(ins, outs, sems):
        mine, first, _, _, _ = plan(ins, outs, sems)
        for cp in mine + first:
            cp.start()

    def mid(ins, outs, sems):
        _, _, landed, passed, _ = plan(ins, outs, sems)
        for over_ici, onward in zip(landed, passed):
            for cp, fwd in zip(over_ici, onward):
                cp.wait_recv()
                fwd.start()

    def finish(ins, outs, sems):
        mine, first, _, passed, from_sibling = plan(ins, outs, sems)
        for cp in from_sibling:
            cp.wait_recv()
        for cp in first + [fwd for onward in passed for fwd in onward]:
            cp.wait_send()
        for cp in mine:
            cp.wait()

    return _Carry(carry_ins, [_sds((N_DEV,) + src.shape, src.dtype) for src, _, _ in items],
                  [pltpu.SemaphoreType.DMA((7, na)), pltpu.SemaphoreType.DMA((7, na)),
                   pltpu.SemaphoreType.DMA((na,))], start, finish, mid, aliases)


def _exchange_carry(scattered, replicated=()):
    items = list(scattered) + [(a, (0, a.shape[0]), None) for a in replicated]
    na, ns = len(items), len(scattered)
    carry_ins, windows, aliases = _split_items(items)

    def plan(ins, outs, sems):
        send_sems, recv_sems, local_sems = sems
        me = _slot(*_position())

        def source(a, j):
            return _row_window(ins[a].at[j] if a < ns else ins[a], windows[a])

        def copy(a, j, i):
            return pltpu.make_async_remote_copy(src_ref=source(a, j), dst_ref=_row_window(outs[a].at[i], windows[a]),
                                                send_sem=send_sems.at[j, a], recv_sem=recv_sems.at[i, a],
                                                device_id=(j >> 2, (j >> 1) & 1, j & 1), device_id_type=MESH)

        def own(a, j):
            return pltpu.make_async_copy(source(a, j), _row_window(outs[a].at[j], windows[a]), local_sems.at[a])

        return me, copy, own

    def start(ins, outs, sems):
        me, copy, own = plan(ins, outs, sems)
        for a in range(na):
            for j in range(N_DEV):
                @pl.when(me == j)
                def _():
                    own(a, j).start()

                @pl.when(me != j)
                def _():
                    copy(a, j, me).start()

    def finish(ins, outs, sems):
        me, copy, own = plan(ins, outs, sems)
        for a in range(na):
            for j in range(N_DEV):
                @pl.when(me == j)
                def _():
                    for i in range(N_DEV):
                        if i != j:
                            copy(a, j, i).wait_recv()
                    own(a, j).wait()

                @pl.when(me != j)
                def _():
                    copy(a, j, me).wait_send()

    return _Carry(carry_ins, [_sds((N_DEV,) + src.shape[-2:], src.dtype) for src, _, _ in items],
                  [pltpu.SemaphoreType.DMA((N_DEV, na)), pltpu.SemaphoreType.DMA((N_DEV, na)),
                   pltpu.SemaphoreType.DMA((na,))], start, finish, None, aliases)


HBM_ARRAY = pl.BlockSpec(memory_space=pltpu.HBM)
SEMAPHORES = pl.BlockSpec(memory_space=pltpu.SEMAPHORE)
DATAFLOW = pltpu.SideEffectType.DATAFLOW_SIDE_EFFECTING


def _exchange_copy(parts_ref, land_ref, send_sems, recv_sems, me, j):
    return pltpu.make_async_remote_copy(src_ref=parts_ref.at[j], dst_ref=land_ref.at[me], send_sem=send_sems.at[j],
                                        recv_sem=recv_sems.at[me], device_id=(j >> 2, (j >> 1) & 1, j & 1),
                                        device_id_type=MESH)


def _exchange_start(parts, name):
    def body(parts_ref, land_ref, send_sems, recv_sems, parts_thru, land_thru, token):
        me = _slot(*_position())
        for j in range(N_DEV):
            @pl.when(me == j)
            def _():
                pltpu.make_async_copy(parts_ref.at[j], land_ref.at[j], send_sems.at[j]).start()

            @pl.when(me != j)
            def _():
                _exchange_copy(parts_ref, land_ref, send_sems, recv_sems, me, j).start()
        token[...] = jnp.zeros_like(token)

    return pl.pallas_call(
        body, name=name,
        out_shape=(pltpu.SemaphoreType.DMA((N_DEV,)), pltpu.SemaphoreType.DMA((N_DEV,)),
                   pltpu.HBM(parts.shape, parts.dtype), pltpu.HBM(parts.shape, parts.dtype), _sds((8, 128), F32)),
        in_specs=(HBM_ARRAY, HBM_ARRAY),
        out_specs=(SEMAPHORES, SEMAPHORES, HBM_ARRAY, HBM_ARRAY, pl.BlockSpec(memory_space=pltpu.VMEM)),
        input_output_aliases={0: 2, 1: 3}, compiler_params=pltpu.CompilerParams(has_side_effects=DATAFLOW),
    )(pltpu.with_memory_space_constraint(parts, pltpu.HBM),
      pltpu.with_memory_space_constraint(lax.empty(parts.shape, parts.dtype), pltpu.HBM))


def _exchange_wait(send_sems, recv_sems, parts_thru, land_thru, after, name):
    def body(parts_ref, land_ref, send_sems, recv_sems, after_ref, parts_dead, got_ref):
        me = _slot(*_position())
        for j in range(N_DEV):
            @pl.when(me == j)
            def _():
                pltpu.make_async_copy(parts_ref.at[j], land_ref.at[j], send_sems.at[j]).wait()

            @pl.when(me != j)
            def _():
                both = pltpu.make_async_remote_copy(src_ref=parts_ref.at[j], dst_ref=land_ref.at[j],
                                                    send_sem=send_sems.at[j], recv_sem=recv_sems.at[j],
                                                    device_id=(j >> 2, (j >> 1) & 1, j & 1), device_id_type=MESH)
                both.wait_send()
                both.wait_recv()

    return pl.pallas_call(
        body, name=name, out_shape=(pltpu.HBM(parts_thru.shape, parts_thru.dtype),
                                    pltpu.HBM(parts_thru.shape, parts_thru.dtype)),
        in_specs=(HBM_ARRAY, HBM_ARRAY, SEMAPHORES, SEMAPHORES, pl.BlockSpec(memory_space=pl.ANY)),
        out_specs=(HBM_ARRAY, HBM_ARRAY), input_output_aliases={0: 0, 1: 1},
        compiler_params=pltpu.CompilerParams(has_side_effects=DATAFLOW),
    )(parts_thru, land_thru, send_sems, recv_sems, after)[1]


class _Mesh:
    def __init__(self, shards):
        self.shards, self.full, self.received, self.cache, self.pending, self.tokens = shards, {}, {}, {}, {}, {}

    def fetch(self, wanted):
        items = []
        for want in wanted:
            name, r0, r1 = want if isinstance(want, tuple) else (want, 0, self.shards[want].shape[0])
            items.append((self.shards[name], (r0, r1), self.full.get(name)))
        return _gather_carry(items)

    def fetched(self, wanted, results):
        self.full.update(zip([want[0] if isinstance(want, tuple) else want for want in wanted], results))

    def send(self, *payloads):
        return _exchange_carry([(parts, rows or (0, parts.shape[1]), self.received.get(name))
                                for name, parts, rows in payloads])

    def sent(self, names, results):
        self.received.update(zip(names, results))

    def send_apart(self, name, parts):
        *self.pending[name], self.tokens[name] = _exchange_start(parts, "exchange_" + name + "_start")
        return self.tokens[name]

    def sent_apart(self, name, after):
        self.received[name] = _exchange_wait(*self.pending.pop(name), after, "exchange_" + name + "_wait")

    def w(self, key):
        if key not in self.cache:
            self.cache[key] = self._layout(key)
        return self.cache[key]

    def _layout(self, key):
        if key in ("gu1", "gu2"):
            return self.full[key]
        if key in ("d1", "d2"):
            return self.full[key].reshape(4, FS, D)
        if key in ("out", "q", "o"):
            return self.full[key].reshape(D, D)
        if key == "kv":
            return self.full["kv"]
        if key == "convw":
            rows = self.full["conv"][:, :3, :].transpose(1, 0, 2).reshape(3, D)
            return jnp.concatenate([rows, jnp.zeros((5, D), F32)], axis=0)
        assert key == "win_t", key
        return self.full["win"].reshape(-1, D)


def _forward_backward(x, mem, target, g, rel_bias, sinks, ex):
    s = x.shape[0]
    def fetching(wanted, call, *args, **kw):
        res, got = call(*args, carry=ex.fetch(wanted), **kw)
        ex.fetched(wanted, got)
        return res

    h1 = fetching(["gu1", "conv"], _rmsnorm, x, g["ffn1"], "norm_ffn1")
    gu1, a1 = fetching(["d1", ("win", 0, 400)], _ffn_up, h1, ex.w("gu1").reshape(2, 4, FS, D), "ffn1_up")
    x1, h2 = fetching([("win", 400, 832)], _mm_res_norm, a1, ex.w("d1"), x, g["mix"], 0.5, "ffn1_down")
    pa, q, kv = fetching(["gu2"], _in_proj, h2, ex.w("win_t"), "in_proj")
    biasm = _bias_build(rel_bias, "bias_build")
    attn, lse = fetching(["out", "kv", "o"], _swa_fwd, q, kv, biasm, sinks, "swa_fwd")
    merged = fetching(["q"], _conv_merge_fwd, pa, attn, ex.w("convw"), "conv_merge_fwd")
    (x2, h3), _ = _mm_res_norm(merged[None], ex.w("out")[None], x1, g["xattn"], 1.0, "out_proj")
    q2 = _mm_nn(h3, ex.w("q")[None], "xattn_q")[0][0]
    mh, _ = _rmsnorm(mem, g["mem"], "norm_mem")
    kv2 = _mm_nn(mh, ex.w("kv"), "xattn_kv")[0]
    o, lse2 = _xattn_fwd(q2, kv2, "xattn_fwd")
    (x3, h4), _ = _mm_res_norm(o[None], ex.w("o")[None], x2, g["ffn2"], 1.0, "xattn_o")
    gu2, a2 = fetching(["d2"], _ffn_up, h4, ex.w("gu2").reshape(2, 4, FS, D), "ffn2_up")
    dx4, dx4b, loss, d_final = _ffn_down_loss(a2, ex.w("d2"), x3, g["final"], target, "ffn2_down_loss")
    def sending(payloads, call, *args, **kw):
        res, got = call(*args, carry=ex.send(*payloads), **kw)
        ex.sent([name for name, _, _ in payloads], got)
        return res

    dw_d2 = _mm_tn(a2, dx4b[None], "dw_ffn2_down", scale=0.5)[0].reshape(N_DEV, -1, D)
    dgu2 = sending([("d2", dw_d2, (0, 288))], _ffn_down_bwd, dx4b, ex.w("d2"), gu2, "ffn2_down_bwd").reshape(8, s, FS)
    dw_gu2 = sending([("d2", dw_d2, (288, 352))], _mm_tn, dgu2, h4[None], "dw_ffn2_up", scale=0.5)
    dx3, dx3b, d_ffn2 = sending([("gu2", dw_gu2, (0, 368))], _mm_acc_rms_bwd, dgu2, ex.w("gu2"), "ffn2_up_bwd",
                                x=x3, gain=g["ffn2"], dres=dx4, scale=0.5)
    do, _ = _mm_acc(dx3b[None], ex.w("o")[None], "xattn_o_bwd", BF16, bt=True)
    dw_o = _mm_tn(o[None], dx3b[None], "dw_xattn_o")[0].reshape(N_DEV, -1, D)
    (dq2, dkv2), _ = _xattn_bwd(q2, kv2, o, do, lse2, "xattn_bwd")
    dkv2b = dkv2.astype(BF16)
    dw_q = _mm_tn(h3[None], dq2[None], "dw_xattn_q")[0].reshape(N_DEV, -1, D)
    (dx2, dx2b, d_xattn), _ = _mm_acc_rms_bwd(dq2[None], ex.w("q")[None], "xattn_q_bwd", x=x2, gain=g["xattn"],
                                              dres=dx3, bt=True)
    dw_kv = _mm_tn(mh[None], dkv2b, "dw_xattn_kv")[0]
    (_, _, d_mem), _ = _mm_acc_rms_bwd(dkv2b, ex.w("kv"), "xattn_kv_bwd", x=mem, gain=g["mem"],
                                       dres=jnp.zeros_like(mem), bt=True)
    dmerged, _ = _mm_acc(dx2b[None], ex.w("out")[None], "out_proj_bwd", BF16, bt=True)
    dw_out = _mm_tn(merged[None], dx2b[None], "dw_out_proj")[0].reshape(N_DEV, -1, D)
    dattn, dpa, d_convw = sending([("kv", dw_kv, None)], _conv_merge_bwd,
                                  dmerged, pa, attn, ex.w("convw"), "conv_merge_bwd")
    dq, dkv, dbias, d_sinks = sending([("gu2", dw_gu2, (368, FS)), ("out", dw_out, None)], _swa_bwd,
                                      q, kv, attn, dattn, lse, biasm, sinks, "swa_bwd")
    d_relb = _bias_bwd(dbias, "bias_bwd")
    w_rows = ex.w("win_t").shape[0]
    dw_in = sending([("o", dw_o, None), ("q", dw_q, None)], _mm_tn_rows, dpa, h2, "dw_in_proj_a", w_rows, NQ + NKV)
    dw_in = _mm_tn_rows(dq[None], h2, "dw_in_proj_q", w_rows, 0, begun=dw_in)[0]
    dw_in = _mm_tn_rows(dkv[None], h2, "dw_in_proj_kv", w_rows, NQ, begun=dw_in)[0].reshape(N_DEV, -1, D)
    (dx1, dx1b, d_mix), _ = _in_proj_bwd(dpa, dq, dkv, ex.w("win_t"), "in_proj_bwd", x=x1, gain=g["mix"], dres=dx2,
                                         behind=ex.send_apart("win", dw_in))
    dw_d1 = _mm_tn(a1, dx1b[None], "dw_ffn1_down", scale=0.5)[0].reshape(N_DEV, -1, D)
    dgu1 = _ffn_down_bwd(dx1b, ex.w("d1"), gu1, "ffn1_down_bwd", behind=ex.send_apart("d1", dw_d1))[0]
    dgu1 = dgu1.reshape(8, s, FS)
    dw_gu1 = _mm_tn(dgu1, h1[None], "dw_ffn1_up", scale=0.5)[0]
    (dx0, _, d_ffn1), _ = _mm_acc_rms_bwd(dgu1, ex.w("gu1"), "ffn1_up_bwd", x=x, gain=g["ffn1"], dres=dx1,
                                          scale=0.5, behind=ex.send_apart("gu1", dw_gu1))

    relb_row = jnp.concatenate([d_relb[:, :REL_BUCKETS].T.reshape(1, REL_BUCKETS * N_HEADS), d_sinks[:, :N_HEADS],
                                jnp.zeros((1, D - REL_BUCKETS * N_HEADS - N_HEADS), F32)], axis=1)
    loss_row = jnp.concatenate([loss[0:1, 0:1], jnp.zeros((1, D - 1), F32)], axis=1)
    small = jnp.concatenate([d_ffn1, d_mix, d_xattn, d_mem, d_ffn2, d_final, relb_row, loss_row, d_convw[0:3],
                             jnp.zeros((SMALL_ROWS - ROW_CONV - 3, D), F32)], axis=0)
    return dx0, small


def _pack_small(norms, final, relb, sinks, conv_local, me):
    relb_row = jnp.concatenate([relb.reshape(1, -1), sinks.reshape(1, -1),
                                jnp.zeros((1, D - REL_BUCKETS * N_HEADS - N_HEADS), F32)], axis=1)
    conv_rows = lax.dynamic_update_slice(jnp.zeros((3, D), F32), conv_local.reshape(3, -1), (0, 128 * me))
    return jnp.concatenate(list(norms) + [final.reshape(1, D), relb_row, jnp.zeros((1, D), F32), conv_rows,
                                          jnp.zeros((SMALL_ROWS - ROW_CONV - 3, D), F32)], axis=0)


def kernel(x, mem, positions, rel_bias, ffn1_norm, ffn1_w_gu, ffn1_w_down, mix_norm, w_in, sinks, conv_w, w_out, xattn_norm, mem_norm, xattn_wq, xattn_wkv, xattn_wo, ffn2_norm, ffn2_w_gu, ffn2_w_down, final_norm, loss_target, m_rel_bias, m_ffn1_norm, m_ffn1_w_gu, m_ffn1_w_down, m_mix_norm, m_w_in, m_sinks, m_conv_w, m_w_out, m_xattn_norm, m_mem_norm, m_xattn_wq, m_xattn_wkv, m_xattn_wo, m_ffn2_norm, m_ffn2_w_gu, m_ffn2_w_down, m_final_norm, v_rel_bias, v_ffn1_norm, v_ffn1_w_gu, v_ffn1_w_down, v_mix_norm, v_w_in, v_sinks, v_conv_w, v_w_out, v_xattn_norm, v_mem_norm, v_xattn_wq, v_xattn_wkv, v_xattn_wo, v_ffn2_norm, v_ffn2_w_gu, v_ffn2_w_down, v_final_norm):
    del positions
    me = _slot(*_position())
    big = dict(gu1=(ffn1_w_gu, m_ffn1_w_gu, v_ffn1_w_gu), d1=(ffn1_w_down, m_ffn1_w_down, v_ffn1_w_down),
               win=(w_in, m_w_in, v_w_in), out=(w_out, m_w_out, v_w_out), q=(xattn_wq, m_xattn_wq, v_xattn_wq),
               kv=(xattn_wkv, m_xattn_wkv, v_xattn_wkv), o=(xattn_wo, m_xattn_wo, v_xattn_wo),
               gu2=(ffn2_w_gu, m_ffn2_w_gu, v_ffn2_w_gu), d2=(ffn2_w_down, m_ffn2_w_down, v_ffn2_w_down))
    order = list(big)
    transposed = ("gu1", "gu2", "win")
    local = {k: tuple(t[0].T if k in transposed else t[0] for t in big[k]) for k in order}
    shards = {k: local[k][0].astype(BF16) for k in order}
    shards["conv"] = jnp.concatenate([conv_w[0], jnp.zeros((5, 128), F32)], axis=0)
    ex = _Mesh(shards)
    gains = dict(ffn1=ffn1_norm, mix=mix_norm, xattn=xattn_norm, mem=mem_norm, ffn2=ffn2_norm,
                 final=final_norm.reshape(1, D))
    dx, small = _forward_backward(x[0], mem[0], loss_target[0], gains, rel_bias, sinks, ex)
    apart = ("d1", "win", "gu1")
    big_out = {k: _adamw(ex.received[k], *local[k], "adamw_" + k, behind=ex.tokens["gu1"])
               for k in order if k not in apart}
    for k in apart[:-1]:
        ex.sent_apart(k, after=sum(big_out[j][1][0:1, 0:1] for j in big_out))
        big_out[k] = _adamw(ex.received[k], *local[k], "adamw_" + k)
    spare = sum(big_out[k][1][0:1, 0:1] for k in big_out)
    small = lax.dynamic_update_slice(small, spare, (SMALL_ROWS - 1, 0))
    small_parts = _run_alone(_exchange_carry([], [small]), "exchange_small")[0]
    packed = [_pack_small(norms, final, relb, sk, conv, me) for norms, final, relb, sk, conv in (
        ((ffn1_norm, mix_norm, xattn_norm, mem_norm, ffn2_norm), final_norm, rel_bias, sinks, conv_w),
        ((m_ffn1_norm, m_mix_norm, m_xattn_norm, m_mem_norm, m_ffn2_norm), m_final_norm, m_rel_bias, m_sinks, m_conv_w),
        ((v_ffn1_norm, v_mix_norm, v_xattn_norm, v_mem_norm, v_ffn2_norm), v_final_norm, v_rel_bias, v_sinks, v_conv_w))]
    small_out, norm_rows = _adamw(small_parts, *packed, "adamw_small", rows_apart=5)
    done = [dx[0:1, 0:1], small_out[1][0:1, 0:1]] + [big_out[k][1][0:1, 0:1] for k in big_out]
    ex.sent_apart("gu1", after=sum(done))
    big_out["gu1"] = _adamw(ex.received["gu1"], *local["gu1"], "adamw_gu1")
    big_out = {k: [t.T if k in transposed else t for t in big_out[k]] for k in order}

    def unpack(t, idx):
        conv = lax.dynamic_slice(t[ROW_CONV:ROW_CONV + 3], (0, 128 * me), (3, 128))[None]
        nrel = REL_BUCKETS * N_HEADS
        norms = dict(zip(("ffn1_norm", "mix_norm", "xattn_norm", "mem_norm", "ffn2_norm"),
                         (rows[idx] for rows in norm_rows)))
        return dict(norms, final_norm=t[5], rel_bias=t[ROW_RELB, :nrel].reshape(REL_BUCKETS, N_HEADS),
                    sinks=t[ROW_RELB:ROW_RELB + 1, nrel:nrel + N_HEADS], conv_w=conv)

    names = dict(gu1="ffn1_w_gu", d1="ffn1_w_down", win="w_in", out="w_out", q="xattn_wq", kv="xattn_wkv",
                 o="xattn_wo", gu2="ffn2_w_gu", d2="ffn2_w_down")
    results = []
    for idx in range(4):
        leaves = unpack(small_out[idx], idx)
        leaves.update({names[k]: big_out[k][idx][None] for k in order})
        results.append(leaves)
    weights = ("rel_bias", "ffn1_norm", "ffn1_w_gu", "ffn1_w_down", "mix_norm", "w_in", "sinks", "conv_w", "w_out",
               "xattn_norm", "mem_norm", "xattn_wq", "xattn_wkv", "xattn_wo", "ffn2_norm", "ffn2_w_gu", "ffn2_w_down",
               "final_norm")
    loss = small_out[0][ROW_LOSS, 0]
    return (loss, dx[None], *[leaves[n] for leaves in results for n in weights])
```

```python
import math

import numpy as np
import jax
import jax.numpy as jnp
from jax import lax
from jax.experimental import pallas as pl
from jax.experimental.pallas import tpu as pltpu

F32, BF16 = jnp.float32, jnp.bfloat16
MESH = pl.DeviceIdType.MESH

D = 1024
N_DEV = 8
D_FF = 2816
FS = D_FF // 4
HEAD = 64
N_HEADS, N_KV = 16, 4
BLK = 128
NQ, NKV = N_HEADS * HEAD, 2 * N_KV * HEAD
XH, XHD = 4, 256
REL_BUCKETS, REL_EXACT, REL_MAX_DIST = 32, 16, 128
EPS, NEG = 1e-6, -1e30
ADAM_LR, ADAM_B1, ADAM_B2, ADAM_EPS, ADAM_WD, ADAM_STEP = 0.001, 0.9, 0.999, 1e-08, 0.01, 10
VMEM_LIMIT_V7X = 56 * 2**20
SMALL_ROWS = 16
ROW_RELB, ROW_LOSS, ROW_CONV = 6, 7, 8


def _bucket_thresholds():
    n = np.arange(REL_MAX_DIST)
    nf = np.maximum(n, 1).astype(np.float32)
    large = REL_EXACT + (np.log(nf / np.float32(REL_EXACT)) / np.float32(math.log(REL_MAX_DIST / REL_EXACT))
                         * np.float32(REL_BUCKETS - REL_EXACT)).astype(np.int32)
    b = np.where(n < REL_EXACT, n, np.minimum(large, REL_BUCKETS - 1))
    return [int(np.argmax(b >= REL_EXACT + k)) for k in range(1, REL_BUCKETS - REL_EXACT)]


BUCKET_THRESHOLDS = _bucket_thresholds()


HBM_SPEC = pl.BlockSpec(memory_space=pl.ANY)


class _Carry:
    def __init__(self, ins, outs, sems, start, finish, mid=None, aliases=None):
        self.ins, self.outs, self.sems = list(ins), list(outs), list(sems)
        self.start, self.finish, self.mid, self.aliases = start, finish, mid, dict(aliases or {})


def _pcall(body, *, name, grid, in_specs, out_specs, out_shape, scratch=(), carry=None, aliases=None, behind=None):
    params = pltpu.CompilerParams(dimension_semantics=("arbitrary",) * len(grid), vmem_limit_bytes=VMEM_LIMIT_V7X)
    if carry is None and behind is not None:
        n_in = len(in_specs)
        call = pl.pallas_call(lambda *refs: body(*refs[:n_in], *refs[n_in + 1:]), name=name, grid=grid,
                              in_specs=list(in_specs) + [pl.BlockSpec((8, 128), lambda *_: (0, 0))],
                              out_specs=out_specs, out_shape=out_shape, scratch_shapes=list(scratch),
                              compiler_params=params, input_output_aliases=aliases or {})
        return lambda *args: call(*args, behind)
    if carry is None:
        return pl.pallas_call(body, name=name, grid=grid, in_specs=in_specs, out_specs=out_specs,
                              out_shape=out_shape, scratch_shapes=list(scratch), compiler_params=params,
                              input_output_aliases=aliases or {})
    assert aliases is None and behind is None, name
    single = not isinstance(out_shape, (list, tuple))
    own_specs, own_shapes = ([out_specs], [out_shape]) if single else (list(out_specs), list(out_shape))
    n_in, n_out, n_scr = len(in_specs), len(own_shapes), len(scratch)
    n_cin, n_cout = len(carry.ins), len(carry.outs)
    steps = math.prod(grid)
    mid_step = max(steps - 1 - max(steps // 8, 1), 0)

    def carrying(*refs):
        ins, refs = refs[:n_in], refs[n_in:]
        cins, refs = refs[:n_cin], refs[n_cin:]
        outs, refs = refs[:n_out], refs[n_out:]
        couts, refs = refs[:n_cout], refs[n_cout:]
        scr, csems = refs[:n_scr], refs[n_scr:]
        step = 0
        for axis, size in enumerate(grid):
            step = step * size + pl.program_id(axis)

        @pl.when(step == 0)
        def _():
            carry.start(cins, couts, csems)

        body(*ins, *outs, *scr)
        if carry.mid is not None:
            @pl.when(step == mid_step)
            def _():
                carry.mid(cins, couts, csems)

        @pl.when(step == steps - 1)
        def _():
            carry.finish(cins, couts, csems)

    call = pl.pallas_call(carrying, name=name, grid=grid, in_specs=list(in_specs) + [HBM_SPEC] * n_cin,
                          out_specs=own_specs + [HBM_SPEC] * n_cout, out_shape=own_shapes + carry.outs,
                          scratch_shapes=list(scratch) + carry.sems, compiler_params=params,
                          input_output_aliases={n_in + i: n_out + o for i, o in carry.aliases.items()})

    def run(*args):
        res = call(*args, *carry.ins)
        return (res[0] if single else res[:n_out]), res[n_out:]

    return run


def _run_alone(carry, name, after=()):
    n_cin, n_cout, n_after = len(carry.ins), len(carry.outs), len(after)

    def body(*refs):
        cins, refs = refs[:n_cin], refs[n_cin + n_after:]
        couts, csems = refs[:n_cout], refs[n_cout:]
        carry.start(cins, couts, csems)
        if carry.mid is not None:
            carry.mid(cins, couts, csems)
        carry.finish(cins, couts, csems)

    return pl.pallas_call(body, name=name, in_specs=[HBM_SPEC] * (n_cin + n_after), out_specs=[HBM_SPEC] * n_cout,
                          out_shape=carry.outs, scratch_shapes=carry.sems,
                          input_output_aliases=carry.aliases)(*carry.ins, *after)


def _dot(a, b):
    return jnp.dot(a, b, preferred_element_type=F32)


def _dot_nt(a, b):
    return lax.dot_general(a, b, (((1,), (1,)), ((), ())), preferred_element_type=F32)


def _dot_tn(a, b):
    return lax.dot_general(a, b, (((0,), (0,)), ((), ())), preferred_element_type=F32)


def _sds(shape, dtype):
    return jax.ShapeDtypeStruct(tuple(shape), dtype)


ROW_CHUNK = 256


def _row_chunks(tm):
    return [slice(r, min(r + ROW_CHUNK, tm)) for r in range(0, tm, ROW_CHUNK)]


def _carried(call, args, carry):
    return call(*args) if carry is not None else (call(*args), ())


def _rmsnorm(x, g, name, carry=None):
    m, d = x.shape
    tm = min(512, m)

    def body(x_ref, g_ref, h_ref):
        xv = x_ref[...]
        r = lax.rsqrt(jnp.mean(xv * xv, axis=-1, keepdims=True) + EPS)
        h_ref[...] = (xv * r * g_ref[...]).astype(BF16)

    call = _pcall(body, name=name, grid=(m // tm,), carry=carry,
                  in_specs=[pl.BlockSpec((tm, d), lambda i: (i, 0)), pl.BlockSpec((1, d), lambda i: (0, 0))],
                  out_specs=pl.BlockSpec((tm, d), lambda i: (i, 0)), out_shape=_sds((m, d), BF16))
    return _carried(call, (x, g), carry)


def _mm_nn(a, b, name, tm=1024, bt=False, carry=None):
    m, k = a.shape
    nj = b.shape[0]
    n = b.shape[1] if bt else b.shape[2]
    tm = min(tm, m)
    dot = _dot_nt if bt else _dot

    def body(a_ref, b_ref, o_ref):
        o_ref[...] = dot(a_ref[...], b_ref[...]).astype(BF16)

    call = _pcall(body, name=name, grid=(nj, m // tm),
                  in_specs=[pl.BlockSpec((tm, k), lambda j, i: (i, 0)),
                            pl.BlockSpec((None,) + b.shape[1:], lambda j, i: (j, 0, 0))],
                  out_specs=pl.BlockSpec((None, tm, n), lambda j, i: (j, i, 0)),
                  out_shape=_sds((nj, m, n), BF16), carry=carry)
    return _carried(call, (a, b), carry)


def _load_once(src_hbm, dst_vmem, sem):
    @pl.when(pl.program_id(0) == 0)
    def _():
        load = pltpu.make_async_copy(src_hbm, dst_vmem, sem)
        load.start()
        load.wait()


def _resident(w):
    return [pltpu.VMEM(w.shape, w.dtype), pltpu.SemaphoreType.DMA(())]


def _ffn_up(h, w4, name, tm=512, carry=None):
    s, d = h.shape
    tm = min(tm, s)

    def body(h_ref, w_hbm, gu_ref, a_ref, w_ref, w_sem):
        _load_once(w_hbm, w_ref, w_sem)
        for p in range(4):
            for rows in _row_chunks(tm):
                hv = h_ref[rows, :]
                g = _dot_nt(hv, w_ref[0, p])
                u = _dot_nt(hv, w_ref[1, p])
                gu_ref[0, p, rows, :] = g.astype(BF16)
                gu_ref[1, p, rows, :] = u.astype(BF16)
                a_ref[p, rows, :] = (g * jax.nn.sigmoid(g) * u).astype(BF16)

    call = _pcall(body, name=name, grid=(s // tm,),
                  in_specs=[pl.BlockSpec((tm, d), lambda i: (i, 0)), HBM_SPEC],
                  out_specs=[pl.BlockSpec((2, 4, tm, FS), lambda i: (0, 0, i, 0)),
                             pl.BlockSpec((4, tm, FS), lambda i: (0, i, 0))],
                  out_shape=[_sds((2, 4, s, FS), BF16), _sds((4, s, FS), BF16)], scratch=_resident(w4), carry=carry)
    return _carried(call, (h, w4), carry)


N_SEG = 5
IN_PROJ_WEIGHTS = [pltpu.VMEM((NQ, D), BF16), pltpu.VMEM((NKV, D), BF16), pltpu.VMEM((N_SEG, D, D), BF16),
                   pltpu.SemaphoreType.DMA((2 + N_SEG,))]


def _load_in_proj(w_hbm, wq_ref, wkv_ref, wa_ref, sems):
    @pl.when(pl.program_id(0) == 0)
    def _():
        loads = [pltpu.make_async_copy(w_hbm.at[pl.ds(0, NQ)], wq_ref, sems.at[0]),
                 pltpu.make_async_copy(w_hbm.at[pl.ds(NQ, NKV)], wkv_ref, sems.at[1])]
        loads += [pltpu.make_async_copy(w_hbm.at[pl.ds(NQ + NKV + D * j, D)], wa_ref.at[j], sems.at[2 + j])
                  for j in range(N_SEG)]
        for load in loads:
            load.start()
        for load in loads:
            load.wait()


def _in_proj(h, w_in_t, name, tm=512, carry=None):
    s, d = h.shape
    tm = min(tm, s)

    def body(h_ref, w_hbm, pa_ref, q_ref, kv_ref, wq_ref, wkv_ref, wa_ref, sems):
        _load_in_proj(w_hbm, wq_ref, wkv_ref, wa_ref, sems)
        hv = h_ref[...]
        q_ref[...] = _dot_nt(hv, wq_ref[...]).astype(BF16)
        kv_ref[...] = _dot_nt(hv, wkv_ref[...]).astype(BF16)
        for j in range(N_SEG):
            pa_ref[j] = _dot_nt(hv, wa_ref[j]).astype(BF16)

    call = _pcall(body, name=name, grid=(s // tm,), carry=carry,
                  in_specs=[pl.BlockSpec((tm, d), lambda i: (i, 0)), HBM_SPEC],
                  out_specs=[pl.BlockSpec((N_SEG, tm, d), lambda i: (0, i, 0)),
                             pl.BlockSpec((tm, NQ), lambda i: (i, 0)), pl.BlockSpec((tm, NKV), lambda i: (i, 0))],
                  out_shape=[_sds((N_SEG, s, d), BF16), _sds((s, NQ), BF16), _sds((s, NKV), BF16)],
                  scratch=IN_PROJ_WEIGHTS)
    return _carried(call, (h, w_in_t), carry)


def _mm_res_norm(a, w, xres, gain, scale, name, tm=512, carry=None):
    npart, s, kp = a.shape
    tm = min(tm, s)

    def body(a_ref, w_ref, x_ref, g_ref, xo_ref, h_ref):
        for rows in _row_chunks(tm):
            acc = _dot(a_ref[0, rows, :], w_ref[0])
            for p in range(1, npart):
                acc = acc + _dot(a_ref[p, rows, :], w_ref[p])
            xn = x_ref[rows, :] + scale * acc
            xo_ref[rows, :] = xn
            r = lax.rsqrt(jnp.mean(xn * xn, axis=-1, keepdims=True) + EPS)
            h_ref[rows, :] = (xn * r * g_ref[...]).astype(BF16)

    call = _pcall(body, name=name, grid=(s // tm,),
                  in_specs=[pl.BlockSpec((npart, tm, kp), lambda i: (0, i, 0)),
                            pl.BlockSpec((npart, kp, D), lambda i: (0, 0, 0)),
                            pl.BlockSpec((tm, D), lambda i: (i, 0)),
                            pl.BlockSpec((1, D), lambda i: (0, 0))],
                  out_specs=[pl.BlockSpec((tm, D), lambda i: (i, 0)), pl.BlockSpec((tm, D), lambda i: (i, 0))],
                  out_shape=[_sds((s, D), F32), _sds((s, D), BF16)], carry=carry)
    return _carried(call, (a, w, xres, gain), carry)


def _ffn_down_loss(a, w, xres, gain, target, name, tm=512):
    npart, s, kp = a.shape
    tm = min(tm, s)

    def body(a_ref, w_ref, x_ref, g_ref, t_ref, dx_ref, dxb_ref, loss_ref, dg_ref):
        @pl.when(pl.program_id(0) == 0)
        def _():
            loss_ref[...] = jnp.zeros_like(loss_ref)
            dg_ref[...] = jnp.zeros_like(dg_ref)

        for rows in _row_chunks(tm):
            acc = _dot(a_ref[0, rows, :], w_ref[0])
            for p in range(1, npart):
                acc = acc + _dot(a_ref[p, rows, :], w_ref[p])
            xn = x_ref[rows, :] + 0.5 * acc
            r = lax.rsqrt(jnp.mean(xn * xn, axis=-1, keepdims=True) + EPS)
            xh = xn * r
            gv = g_ref[...]
            err = xh * gv - t_ref[rows, :]
            part = 0.5 * jnp.sum(jnp.mean(err * err, axis=-1, keepdims=True), axis=0, keepdims=True)
            dy = err * (1.0 / D)
            dyg = dy * gv
            dxn = r * (dyg - xh * jnp.mean(dyg * xh, axis=-1, keepdims=True))
            dx_ref[rows, :] = dxn
            dxb_ref[rows, :] = dxn.astype(BF16)
            loss_ref[...] += jnp.broadcast_to(part, loss_ref.shape)
            dg_ref[...] += jnp.sum(dy * xh, axis=0, keepdims=True)

    return _pcall(body, name=name, grid=(s // tm,),
                  in_specs=[pl.BlockSpec((npart, tm, kp), lambda i: (0, i, 0)),
                            pl.BlockSpec((npart, kp, D), lambda i: (0, 0, 0)),
                            pl.BlockSpec((tm, D), lambda i: (i, 0)),
                            pl.BlockSpec((1, D), lambda i: (0, 0)),
                            pl.BlockSpec((tm, D), lambda i: (i, 0))],
                  out_specs=[pl.BlockSpec((tm, D), lambda i: (i, 0)), pl.BlockSpec((tm, D), lambda i: (i, 0)),
                             pl.BlockSpec((8, 128), lambda i: (0, 0)), pl.BlockSpec((1, D), lambda i: (0, 0))],
                  out_shape=[_sds((s, D), F32), _sds((s, D), BF16), _sds((8, 128), F32), _sds((1, D), F32)],
                  )(a, w, xres, gain, target)


def _window_tiles():
    i = lax.broadcasted_iota(jnp.int32, (BLK, BLK), 0)
    j = lax.broadcasted_iota(jnp.int32, (BLK, BLK), 1)
    rel = (i - j) & (BLK - 1)
    large = jnp.full_like(rel, REL_EXACT)
    for t in BUCKET_THRESHOLDS:
        large = large + (rel >= t).astype(jnp.int32)
    return j <= i, jnp.where(rel < REL_EXACT, rel, large)


def _bias_build(rel_bias, name):
    def body(rb_ref, o_ref):
        _, bucket = _window_tiles()

        def per_head(h, carry):
            acc = jnp.zeros((BLK, BLK), F32)
            for b in range(REL_BUCKETS):
                acc = jnp.where(bucket == b, rb_ref[b, h], acc)
            o_ref[h] = acc
            return carry

        lax.fori_loop(0, N_HEADS, per_head, 0)

    return _pcall(body, name=name, grid=(1,),
                  in_specs=[pl.BlockSpec(memory_space=pltpu.SMEM)],
                  out_specs=pl.BlockSpec((N_HEADS, BLK, BLK), lambda i: (0, 0, 0)),
                  out_shape=_sds((N_HEADS, BLK, BLK), F32))(rel_bias)


def _bias_bwd(dbias, name):
    def body(db_ref, o_ref):
        _, bucket = _window_tiles()
        lane = lax.broadcasted_iota(jnp.int32, (N_HEADS, 128), 1)

        def per_bucket(b, out):
            mb = (bucket == b).astype(F32)
            per_col = jnp.sum(db_ref[...] * mb[None, :, :], axis=1)
            return jnp.where(lane == b, jnp.sum(per_col, axis=1, keepdims=True), out)

        o_ref[...] = lax.fori_loop(0, REL_BUCKETS, per_bucket, jnp.zeros((N_HEADS, 128), F32))

    return _pcall(body, name=name, grid=(1,),
                  in_specs=[pl.BlockSpec((N_HEADS, BLK, BLK), lambda i: (0, 0, 0))],
                  out_specs=pl.BlockSpec((N_HEADS, 128), lambda i: (0, 0)),
                  out_shape=_sds((N_HEADS, 128), F32))(dbias)


PAIR = 2 * HEAD
GROUP = N_HEADS // N_KV
SWA_SCALE = HEAD ** -0.5


def _window_masks(n):
    i = lax.broadcasted_iota(jnp.int32, (GROUP * BLK, BLK), 0) & (BLK - 1)
    j = lax.broadcasted_iota(jnp.int32, (GROUP * BLK, BLK), 1)
    return j <= i, jnp.logical_and(n == 0, j > i), j < HEAD


def _kv_twice(ref, base, g, low):
    slab = ref[:, base + PAIR * (g // 2): base + PAIR * (g // 2 + 1)]
    swapped = pltpu.roll(slab, HEAD, 1)
    return jnp.where(low, slab, swapped) if g % 2 == 0 else jnp.where(low, swapped, slab)


def _stack_heads(ref, g, low):
    parts = []
    for r in range(2):
        slab = ref[:, PAIR * (2 * g + r): PAIR * (2 * g + r + 1)]
        zero = jnp.zeros_like(slab)
        parts += [jnp.where(low, slab, zero), jnp.where(low, zero, slab)]
    return jnp.concatenate(parts, axis=0)


def _unstack_heads(t, low):
    return [jnp.where(low, t[2 * r * BLK:(2 * r + 1) * BLK], t[(2 * r + 1) * BLK:(2 * r + 2) * BLK])
            for r in range(2)]


def _head_rows(t, k):
    return t[k * BLK:(k + 1) * BLK]


def _per_head_column(values):
    head = lax.broadcasted_iota(jnp.int32, (GROUP * BLK, 1), 0) // BLK
    col = jnp.full((GROUP * BLK, 1), values[0], F32)
    for k in range(1, GROUP):
        col = jnp.where(head == k, values[k], col)
    return col


def _window_logits(q4, kc, kp, bias4, own, absent):
    sc = jnp.where(own, _dot_nt(q4, kc), _dot_nt(q4, kp)) * SWA_SCALE + bias4
    return jnp.where(absent, NEG, sc)


def _split_window(t, own):
    zero = jnp.zeros_like(t)
    return jnp.where(own, t, zero), jnp.where(own, zero, t)


def _swa_fwd(q, kv, bias, sinks, name, carry=None):
    s = q.shape[0]
    nb = s // BLK
    kvw = 2 * N_KV * HEAD

    def body(q_ref, kc_ref, kp_ref, b_ref, sk_ref, o_ref, lse_ref):
        own, absent, low4 = _window_masks(pl.program_id(0))
        low = low4[:BLK]
        lane = lax.broadcasted_iota(jnp.int32, (BLK, 128), 1)
        lse_t = jnp.zeros((BLK, 128), F32)
        for g in range(N_KV):
            q4 = _stack_heads(q_ref, g, low)
            kc, kp = _kv_twice(kc_ref, 0, g, low), _kv_twice(kp_ref, 0, g, low)
            vc, vp = _kv_twice(kc_ref, N_KV * HEAD, g, low), _kv_twice(kp_ref, N_KV * HEAD, g, low)
            bias4 = b_ref[GROUP * g:GROUP * (g + 1)].reshape(GROUP * BLK, BLK)
            sc = _window_logits(q4, kc, kp, bias4, own, absent)
            sk = _per_head_column([sk_ref[0, GROUP * g + k] for k in range(GROUP)])
            m = jnp.maximum(jnp.max(sc, axis=1, keepdims=True), sk)
            p = jnp.exp(sc - m)
            l = jnp.sum(p, axis=1, keepdims=True) + jnp.exp(sk - m)
            p_own, p_prev = _split_window(p.astype(BF16), own)
            out = (_dot(p_own, vc) + _dot(p_prev, vp)) * (1.0 / l)
            for r, slab in enumerate(_unstack_heads(out, low)):
                o_ref[:, PAIR * (2 * g + r): PAIR * (2 * g + r + 1)] = slab.astype(BF16)
            lse4 = m + jnp.log(l)
            for k in range(GROUP):
                lse_t = jnp.where(lane == GROUP * g + k, _head_rows(lse4, k), lse_t)
        lse_ref[...] = lse_t

    call = _pcall(body, name=name, grid=(nb,),
                  in_specs=[pl.BlockSpec((BLK, D), lambda n: (n, 0)),
                            pl.BlockSpec((BLK, kvw), lambda n: (n, 0)),
                            pl.BlockSpec((BLK, kvw), lambda n: (jnp.maximum(n - 1, 0), 0)),
                            pl.BlockSpec((N_HEADS, BLK, BLK), lambda n: (0, 0, 0)),
                            pl.BlockSpec(memory_space=pltpu.SMEM)],
                  out_specs=[pl.BlockSpec((BLK, D), lambda n: (n, 0)), pl.BlockSpec((BLK, 128), lambda n: (n, 0))],
                  out_shape=[_sds((s, D), BF16), _sds((s, 128), F32)], carry=carry)
    return _carried(call, (q, kv, kv, bias, sinks), carry)


def _fold_halves(t, g, low):
    folded = jnp.where(low, t, 0.0) + pltpu.roll(jnp.where(low, 0.0, t), HEAD, 1)
    return folded if g % 2 == 0 else pltpu.roll(folded, HEAD, 1)


def _swa_bwd(q, kv, attn, dattn, lse, bias, sinks, name, carry=None):
    s = q.shape[0]
    nb = s // BLK
    kvw = 2 * N_KV * HEAD
    voff = N_KV * HEAD

    def body(q_ref, kc_ref, kp_ref, o_ref, do_ref, lse_ref, b_ref, skrow_ref, dq_ref, dkv_ref, dbias_ref, dsk_ref,
             dq_hold, kv_hold, dq_new, kv_prev, kv_cur):
        n = pl.program_id(0)

        @pl.when(n == 0)
        def _():
            dbias_ref[...] = jnp.zeros_like(dbias_ref)
            dsk_ref[...] = jnp.zeros_like(dsk_ref)
            dq_hold[...] = jnp.zeros_like(dq_hold)
            kv_hold[...] = jnp.zeros_like(kv_hold)

        @pl.when(n < nb)
        def _():
            own, absent, low4 = _window_masks(n)
            low = low4[:BLK]
            lane = lax.broadcasted_iota(jnp.int32, (BLK, 128), 1)
            delta_t = jnp.zeros((BLK, 128), F32)
            ones = jnp.ones((PAIR, 128), BF16)
            for pair_of_kv in range(N_KV // 2):
                slab_grads = [jnp.zeros((BLK, PAIR), F32) for _ in range(4)]
                for g in (2 * pair_of_kv, 2 * pair_of_kv + 1):
                    q4, do4 = _stack_heads(q_ref, g, low), _stack_heads(do_ref, g, low)
                    kc, kp = _kv_twice(kc_ref, 0, g, low), _kv_twice(kp_ref, 0, g, low)
                    vc, vp = _kv_twice(kc_ref, voff, g, low), _kv_twice(kp_ref, voff, g, low)
                    o_slabs = [o_ref[:, PAIR * (2 * g + r): PAIR * (2 * g + r + 1)] for r in range(2)]
                    o4 = jnp.concatenate([o_slabs[0], o_slabs[0], o_slabs[1], o_slabs[1]], axis=0)
                    delta = _dot(do4 * o4, ones)
                    heads = range(GROUP * g, GROUP * (g + 1))
                    lse4 = jnp.concatenate([lse_ref[:, h:h + 1] for h in heads], axis=0)
                    bias4 = b_ref[GROUP * g:GROUP * (g + 1)].reshape(GROUP * BLK, BLK)
                    p = jnp.exp(_window_logits(q4, kc, kp, bias4, own, absent) - lse4)
                    dp = jnp.where(own, _dot_nt(do4, vc), _dot_nt(do4, vp))
                    ds = p * (dp - delta)
                    dbias_ref[GROUP * g:GROUP * (g + 1)] += ds.reshape(GROUP, BLK, BLK)
                    for k, h in enumerate(heads):
                        delta_t = jnp.where(lane == h, _head_rows(delta, k), delta_t)
                    ds_own, ds_prev = _split_window((ds * SWA_SCALE).astype(BF16), own)
                    p_own, p_prev = _split_window(p.astype(BF16), own)
                    dq4 = _dot(ds_own, kc) + _dot(ds_prev, kp)
                    for r, slab in enumerate(_unstack_heads(dq4, low)):
                        dq_new[:, PAIR * (2 * g + r): PAIR * (2 * g + r + 1)] = slab
                    grads = [_dot_tn(ds_own, q4), _dot_tn(ds_prev, q4), _dot_tn(p_own, do4), _dot_tn(p_prev, do4)]
                    slab_grads = [t + _fold_halves(dk, g, low) for t, dk in zip(slab_grads, grads)]
                ks = slice(PAIR * pair_of_kv, PAIR * (pair_of_kv + 1))
                vs = slice(voff + PAIR * pair_of_kv, voff + PAIR * (pair_of_kv + 1))
                kv_cur[:, ks], kv_prev[:, ks], kv_cur[:, vs], kv_prev[:, vs] = slab_grads
            dsk_ref[...] -= jnp.sum(jnp.exp(skrow_ref[...] - lse_ref[...]) * delta_t, axis=0, keepdims=True)

        @pl.when(n == nb)
        def _():
            kv_prev[...] = jnp.zeros_like(kv_prev)

        dq_ref[...] = dq_hold[...].astype(BF16)
        dkv_ref[...] = (kv_hold[...] + kv_prev[...]).astype(BF16)

        @pl.when(n < nb)
        def _():
            dq_hold[...] = dq_new[...]
            kv_hold[...] = kv_cur[...]

    def cur(n):
        return jnp.minimum(n, nb - 1)

    call = _pcall(body, name=name, grid=(nb + 1,), carry=carry,
                  in_specs=[pl.BlockSpec((BLK, D), lambda n: (cur(n), 0)),
                            pl.BlockSpec((BLK, kvw), lambda n: (cur(n), 0)),
                            pl.BlockSpec((BLK, kvw), lambda n: (jnp.maximum(cur(n) - 1, 0), 0)),
                            pl.BlockSpec((BLK, D), lambda n: (cur(n), 0)),
                            pl.BlockSpec((BLK, D), lambda n: (cur(n), 0)),
                            pl.BlockSpec((BLK, 128), lambda n: (cur(n), 0)),
                            pl.BlockSpec((N_HEADS, BLK, BLK), lambda n: (0, 0, 0)),
                            pl.BlockSpec((1, 128), lambda n: (0, 0))],
                  out_specs=[pl.BlockSpec((BLK, D), lambda n: (jnp.maximum(n - 1, 0), 0)),
                             pl.BlockSpec((BLK, kvw), lambda n: (jnp.maximum(n - 1, 0), 0)),
                             pl.BlockSpec((N_HEADS, BLK, BLK), lambda n: (0, 0, 0)),
                             pl.BlockSpec((1, 128), lambda n: (0, 0))],
                  out_shape=[_sds((s, D), BF16), _sds((s, kvw), BF16), _sds((N_HEADS, BLK, BLK), F32),
                             _sds((1, 128), F32)],
                  scratch=[pltpu.VMEM((BLK, D), F32), pltpu.VMEM((BLK, kvw), F32), pltpu.VMEM((BLK, D), F32),
                           pltpu.VMEM((BLK, kvw), F32), pltpu.VMEM((BLK, kvw), F32)])
    sink_row = jnp.pad(sinks, ((0, 0), (0, 128 - N_HEADS)))
    return _carried(call, (q, kv, kv, attn, dattn, lse, bias, sink_row), carry)


HALO = 16
CW = D


def _conv_taps(cu, halo_cu, first_tile):
    row = lax.broadcasted_iota(jnp.int32, cu.shape, 0)
    halo_cu = jnp.where(first_tile, 0.0, halo_cu)
    c1 = jnp.where(row == 0, halo_cu[HALO - 1:HALO], pltpu.roll(cu, 1, 0))
    c2 = jnp.where(row == 0, halo_cu[HALO - 2:HALO - 1],
                   jnp.where(row == 1, halo_cu[HALO - 1:HALO], pltpu.roll(cu, 2, 0)))
    return c1, c2


def _conv_merge_fwd(pa, attn, convw, name, ts=256, carry=None):
    _, s, _ = pa.shape
    ts = min(ts, s)
    hb = ts // HALO

    def body(pa_ref, hp_ref, at_ref, w_ref, o_ref):
        i = pl.program_id(1)
        cu = pa_ref[0].astype(F32) * pa_ref[2].astype(F32)
        c1, c2 = _conv_taps(cu, hp_ref[0].astype(F32) * hp_ref[2].astype(F32), i == 0)
        w = w_ref[...]
        c3 = w[0:1] * c2 + w[1:2] * c1 + w[2:3] * cu
        conv = pa_ref[1].astype(F32) * c3
        o_ref[...] = (jax.nn.sigmoid(pa_ref[3].astype(F32)) * at_ref[...].astype(F32)
                      + jax.nn.sigmoid(pa_ref[4].astype(F32)) * conv).astype(BF16)

    call = _pcall(body, name=name, grid=(D // CW, s // ts), carry=carry,
                  in_specs=[pl.BlockSpec((5, ts, CW), lambda c, i: (0, i, c)),
                            pl.BlockSpec((5, HALO, CW), lambda c, i: (0, jnp.maximum(i * hb - 1, 0), c)),
                            pl.BlockSpec((ts, CW), lambda c, i: (i, c)),
                            pl.BlockSpec((8, CW), lambda c, i: (0, c))],
                  out_specs=pl.BlockSpec((ts, CW), lambda c, i: (i, c)),
                  out_shape=_sds((s, D), BF16))
    return _carried(call, (pa, pa, attn, convw), carry)


def _conv_merge_bwd(dmerged, pa, attn, convw, name, ts=256, carry=None):
    _, s, _ = pa.shape
    ts = min(ts, s)
    hb = ts // HALO
    last_hb = s // HALO - 1

    def body(dm_ref, pa_ref, at_ref, w_ref, hp_ref, hn_ref, dmn_ref, dat_ref, dpa_ref, dw_ref):
        i = pl.program_id(1)
        last = i == pl.num_programs(1) - 1
        dm = dm_ref[...].astype(F32)
        cp, bp, u = pa_ref[0].astype(F32), pa_ref[1].astype(F32), pa_ref[2].astype(F32)
        sa = jax.nn.sigmoid(pa_ref[3].astype(F32))
        sc = jax.nn.sigmoid(pa_ref[4].astype(F32))
        at = at_ref[...].astype(F32)
        cu = cp * u
        c1, c2 = _conv_taps(cu, hp_ref[0].astype(F32) * hp_ref[2].astype(F32), i == 0)
        w = w_ref[...]
        c3 = w[0:1] * c2 + w[1:2] * c1 + w[2:3] * cu
        dconv = dm * sc
        dc3 = dconv * bp
        nxt = dmn_ref[...].astype(F32) * jax.nn.sigmoid(hn_ref[4].astype(F32)) * hn_ref[1].astype(F32)
        nxt = jnp.where(last, 0.0, nxt)
        row = lax.broadcasted_iota(jnp.int32, dc3.shape, 0)
        d1 = jnp.where(row == ts - 1, nxt[0:1], pltpu.roll(dc3, ts - 1, 0))
        d2 = jnp.where(row == ts - 2, nxt[0:1], jnp.where(row == ts - 1, nxt[1:2], pltpu.roll(dc3, ts - 2, 0)))
        dcu = w[2:3] * dc3 + w[1:2] * d1 + w[0:1] * d2
        dat_ref[...] = (dm * sa).astype(BF16)
        dpa_ref[0] = (dcu * u).astype(BF16)
        dpa_ref[1] = (dconv * c3).astype(BF16)
        dpa_ref[2] = (dcu * cp).astype(BF16)
        dpa_ref[3] = (dm * at * sa * (1.0 - sa)).astype(BF16)
        dpa_ref[4] = (dm * bp * c3 * sc * (1.0 - sc)).astype(BF16)

        @pl.when(i == 0)
        def _():
            dw_ref[...] = jnp.zeros_like(dw_ref)

        dw_ref[0:1, :] += jnp.sum(dc3 * c2, axis=0, keepdims=True)
        dw_ref[1:2, :] += jnp.sum(dc3 * c1, axis=0, keepdims=True)
        dw_ref[2:3, :] += jnp.sum(dc3 * cu, axis=0, keepdims=True)

    call = _pcall(body, name=name, grid=(D // CW, s // ts), carry=carry,
                  in_specs=[pl.BlockSpec((ts, CW), lambda c, i: (i, c)),
                            pl.BlockSpec((5, ts, CW), lambda c, i: (0, i, c)),
                            pl.BlockSpec((ts, CW), lambda c, i: (i, c)),
                            pl.BlockSpec((8, CW), lambda c, i: (0, c)),
                            pl.BlockSpec((5, HALO, CW), lambda c, i: (0, jnp.maximum(i * hb - 1, 0), c)),
                            pl.BlockSpec((5, HALO, CW), lambda c, i: (0, jnp.minimum((i + 1) * hb, last_hb), c)),
                            pl.BlockSpec((HALO, CW), lambda c, i: (jnp.minimum((i + 1) * hb, last_hb), c))],
                  out_specs=[pl.BlockSpec((ts, CW), lambda c, i: (i, c)),
                             pl.BlockSpec((5, ts, CW), lambda c, i: (0, i, c)),
                             pl.BlockSpec((8, CW), lambda c, i: (0, c))],
                  out_shape=[_sds((s, D), BF16), _sds((5, s, D), BF16), _sds((8, D), F32)])
    return _carried(call, (dmerged, pa, attn, convw, pa, pa, dmerged), carry)


def _xattn_fwd(q, kv, name, tq=512):
    s, _ = q.shape
    nm = kv.shape[1]
    tq = min(tq, s)

    def body(q_ref, kv_ref, o_ref, lse_ref):
        lane = lax.broadcasted_iota(jnp.int32, (tq, 128), 1)
        lse_t = jnp.zeros((tq, 128), F32)
        for h in range(XH):
            hs = slice(XHD * h, XHD * (h + 1))
            sc = _dot_nt(q_ref[:, hs], kv_ref[h]) * (XHD ** -0.5)
            m = jnp.max(sc, axis=1, keepdims=True)
            p = jnp.exp(sc - m)
            l = jnp.sum(p, axis=1, keepdims=True)
            o_ref[:, hs] = (_dot(p.astype(BF16), kv_ref[XH + h]) * (1.0 / l)).astype(BF16)
            lse_t = jnp.where(lane == h, m + jnp.log(l), lse_t)
        lse_ref[...] = lse_t

    return _pcall(body, name=name, grid=(s // tq,),
                  in_specs=[pl.BlockSpec((tq, D), lambda i: (i, 0)), pl.BlockSpec((2 * XH, nm, XHD), lambda i: (0, 0, 0))],
                  out_specs=[pl.BlockSpec((tq, D), lambda i: (i, 0)), pl.BlockSpec((tq, 128), lambda i: (i, 0))],
                  out_shape=[_sds((s, D), BF16), _sds((s, 128), F32)])(q, kv)


def _xattn_bwd(q, kv, o, do, lse, name, tq=512, carry=None):
    s, _ = q.shape
    nm = kv.shape[1]
    tq = min(tq, s)

    def body(q_ref, kv_ref, o_ref, do_ref, lse_ref, dq_ref, dkv_ref):
        @pl.when(pl.program_id(0) == 0)
        def _():
            dkv_ref[...] = jnp.zeros_like(dkv_ref)

        for h in range(XH):
            hs = slice(XHD * h, XHD * (h + 1))
            qh, kh, vh, dob = q_ref[:, hs], kv_ref[h], kv_ref[XH + h], do_ref[:, hs]
            p = jnp.exp(_dot_nt(qh, kh) * (XHD ** -0.5) - lse_ref[:, h:h + 1])
            dp = _dot_nt(dob, vh)
            delta = jnp.sum(dob.astype(F32) * o_ref[:, hs].astype(F32), axis=1, keepdims=True)
            dsb = (p * (dp - delta) * (XHD ** -0.5)).astype(BF16)
            dq_ref[:, hs] = _dot(dsb, kh).astype(BF16)
            dkv_ref[h] += _dot_tn(dsb, qh)
            dkv_ref[XH + h] += _dot_tn(p.astype(BF16), dob)

    call = _pcall(body, name=name, grid=(s // tq,), carry=carry,
                  in_specs=[pl.BlockSpec((tq, D), lambda i: (i, 0)), pl.BlockSpec((2 * XH, nm, XHD), lambda i: (0, 0, 0)),
                            pl.BlockSpec((tq, D), lambda i: (i, 0)), pl.BlockSpec((tq, D), lambda i: (i, 0)),
                            pl.BlockSpec((tq, 128), lambda i: (i, 0))],
                  out_specs=[pl.BlockSpec((tq, D), lambda i: (i, 0)), pl.BlockSpec((2 * XH, nm, XHD), lambda i: (0, 0, 0))],
                  out_shape=[_sds((s, D), BF16), _sds((2 * XH, nm, XHD), F32)])
    return _carried(call, (q, kv, o, do, lse), carry)


def _ffn_down_bwd(dxb, wd4, gu4, name, tm=512, carry=None, behind=None):
    s, _ = dxb.shape
    tm = min(tm, s)

    def body(dx_ref, w_hbm, gu_ref, o_ref, w_ref, w_sem):
        _load_once(w_hbm, w_ref, w_sem)
        for p in range(4):
            for rows in _row_chunks(tm):
                da = _dot_nt(dx_ref[rows, :], w_ref[p])
                g = gu_ref[0, p, rows, :].astype(F32)
                u = gu_ref[1, p, rows, :].astype(F32)
                sg = jax.nn.sigmoid(g)
                t = da * sg
                o_ref[0, p, rows, :] = (t * u * (1.0 + g - g * sg)).astype(BF16)
                o_ref[1, p, rows, :] = (t * g).astype(BF16)

    block = pl.BlockSpec((2, 4, tm, FS), lambda i: (0, 0, i, 0))
    call = _pcall(body, name=name, grid=(s // tm,), carry=carry, behind=behind,
                  in_specs=[pl.BlockSpec((tm, D), lambda i: (i, 0)), HBM_SPEC, block],
                  out_specs=block, out_shape=_sds((2, 4, s, FS), BF16), scratch=_resident(wd4))
    return _carried(call, (dxb, wd4, gu4), carry)


def _mm_tn(a, b, name, scale=1.0, carry=None):
    pa_n, s, m = a.shape
    pb_n, _, n = b.shape
    po = max(pa_n, pb_n)
    tn = n if po >= 4 else min(n, 256)

    def body(a_ref, b_ref, o_ref):
        o_ref[...] = (scale * _dot_tn(a_ref[...], b_ref[...])).astype(BF16)

    call = _pcall(body, name=name, grid=(po, n // tn), carry=carry,
                  in_specs=[pl.BlockSpec((None, s, m), lambda o, j: (o if pa_n > 1 else 0, 0, 0)),
                            pl.BlockSpec((None, s, tn), lambda o, j: (o if pb_n > 1 else 0, 0, j))],
                  out_specs=pl.BlockSpec((None, m, tn), lambda o, j: (o, 0, j)),
                  out_shape=_sds((po, m, n), BF16))
    return _carried(call, (a, b), carry)


def _mm_tn_rows(a, b, name, total_rows, row0, begun=None, tm=512, carry=None):
    p, s, m = a.shape
    n = b.shape[1]
    tm = min(tm, m)
    tiles = m // tm
    assert row0 % tm == 0 and m % tm == 0, (row0, m, tm)

    def body(a_ref, b_ref, *rest):
        rest[-1][...] = _dot_tn(a_ref[...], b_ref[...]).astype(BF16)

    in_specs = [pl.BlockSpec((None, s, tm), lambda o, i: (o, 0, i)), pl.BlockSpec((s, n), lambda o, i: (0, 0))]
    call = _pcall(body, name=name, grid=(p, tiles), in_specs=in_specs + ([HBM_SPEC] if begun is not None else []),
                  out_specs=pl.BlockSpec((tm, n), lambda o, i: (row0 // tm + o * tiles + i, 0)),
                  out_shape=_sds((total_rows, n), BF16), aliases={2: 0} if begun is not None else None, carry=carry)
    return _carried(call, (a, b, begun) if begun is not None else (a, b), carry)


def _sum_dots(a_ref, b_ref, nj, bt, rows=slice(None)):
    dot = _dot_nt if bt else _dot
    acc = dot(a_ref[0, rows, :], b_ref[0])
    for j in range(1, nj):
        acc = acc + dot(a_ref[j, rows, :], b_ref[j])
    return acc


def _mm_acc(a, b, name, out_dtype, tm=512, bt=False, carry=None):
    nj, s, k = a.shape
    n = b.shape[1] if bt else b.shape[2]
    tm = min(tm, s)

    def body(a_ref, b_ref, o_ref):
        o_ref[...] = _sum_dots(a_ref, b_ref, nj, bt).astype(out_dtype)

    call = _pcall(body, name=name, grid=(s // tm,), carry=carry,
                  in_specs=[pl.BlockSpec((nj, tm, k), lambda i: (0, i, 0)),
                            pl.BlockSpec(b.shape, lambda i: (0, 0, 0))],
                  out_specs=pl.BlockSpec((tm, n), lambda i: (i, 0)), out_shape=_sds((s, n), out_dtype))
    return _carried(call, (a, b), carry)


def _rms_bwd_call(name, acts, weights, scratch, load, dh_rows, *, x, gain, dres, tm, carry, behind=None):
    s, n = x.shape
    tm = min(tm, s)
    n_act, n_w = len(acts), len(weights)

    def body(*refs):
        act_refs, w_refs = refs[:n_act], refs[n_act:n_act + n_w]
        x_ref, g_ref, r_ref, dx_ref, dxb_ref, dg_ref = refs[n_act + n_w:n_act + n_w + 6]
        held = refs[n_act + n_w + 6:]
        load(w_refs, held)

        @pl.when(pl.program_id(0) == 0)
        def _():
            dg_ref[...] = jnp.zeros_like(dg_ref)

        for rows in _row_chunks(tm):
            dh = dh_rows(act_refs, held, rows)
            xv = x_ref[rows, :]
            r = lax.rsqrt(jnp.mean(xv * xv, axis=-1, keepdims=True) + EPS)
            xh = xv * r
            dyg = dh * g_ref[...]
            dx = r_ref[rows, :] + r * (dyg - xh * jnp.mean(dyg * xh, axis=-1, keepdims=True))
            dx_ref[rows, :] = dx
            dxb_ref[rows, :] = dx.astype(BF16)
            dg_ref[...] += jnp.sum(dh * xh, axis=0, keepdims=True)

    def tile(a):
        return (pl.BlockSpec((tm, a.shape[1]), lambda i: (i, 0)) if a.ndim == 2
                else pl.BlockSpec((a.shape[0], tm, a.shape[2]), lambda i: (0, i, 0)))

    row = pl.BlockSpec((tm, n), lambda i: (i, 0))
    in_specs = [tile(a) for a in acts] + [HBM_SPEC] * n_w + [row, pl.BlockSpec((1, n), lambda i: (0, 0)), row]
    call = _pcall(body, name=name, grid=(s // tm,), in_specs=in_specs, carry=carry, behind=behind,
                  out_specs=[row, row, pl.BlockSpec((1, n), lambda i: (0, 0))],
                  out_shape=[_sds((s, n), F32), _sds((s, n), BF16), _sds((1, n), F32)], scratch=scratch)
    return _carried(call, tuple(acts) + tuple(weights) + (x, gain, dres), carry)


def _mm_acc_rms_bwd(a, b, name, *, x, gain, dres, scale=None, tm=512, bt=False, carry=None, behind=None):
    def load(w_refs, held):
        _load_once(w_refs[0], held[0], held[1])

    def dh_rows(act_refs, held, rows):
        dh = _sum_dots(act_refs[0], held[0], a.shape[0], bt, rows)
        return dh if scale is None else scale * dh

    return _rms_bwd_call(name, [a], [b], _resident(b), load, dh_rows, x=x, gain=gain, dres=dres, tm=tm, carry=carry,
                         behind=behind)


def _in_proj_bwd(dpa, dq, dkv, w_in_t, name, *, x, gain, dres, tm=512, carry=None, behind=None):
    def load(w_refs, held):
        _load_in_proj(w_refs[0], *held)

    def dh_rows(act_refs, held, rows):
        dpa_ref, dq_ref, dkv_ref = act_refs
        wq_ref, wkv_ref, wa_ref, _ = held
        dh = _dot(dq_ref[rows, :], wq_ref[...]) + _dot(dkv_ref[rows, :], wkv_ref[...])
        return dh + _sum_dots(dpa_ref, wa_ref, N_SEG, False, rows)

    return _rms_bwd_call(name, [dpa, dq, dkv], [w_in_t], IN_PROJ_WEIGHTS, load, dh_rows, x=x, gain=gain, dres=dres,
                         tm=tm, carry=carry, behind=behind)


def _adam(w, g, m, v):
    m2 = ADAM_B1 * m + (1.0 - ADAM_B1) * g
    v2 = ADAM_B2 * v + (1.0 - ADAM_B2) * (g * g)
    m_hat = m2 / (1.0 - ADAM_B1 ** ADAM_STEP)
    v_hat = v2 / (1.0 - ADAM_B2 ** ADAM_STEP)
    delta = -ADAM_LR * (m_hat / (jnp.sqrt(v_hat) + ADAM_EPS) + ADAM_WD * w)
    return delta, m2, v2


def _adamw(parts, w, m, v, name, behind=None):
    _, r, c = parts.shape
    tr = max(t for t in range(16, 257, 16) if r % t == 0)

    def body(p_ref, w_ref, m_ref, v_ref, g_ref, d_ref, m2_ref, v2_ref):
        g = p_ref[0].astype(F32)
        for i in range(1, N_DEV):
            g = g + p_ref[i].astype(F32)
        delta, m2, v2 = _adam(w_ref[...], g, m_ref[...], v_ref[...])
        g_ref[...] = g
        d_ref[...] = delta
        m2_ref[...] = m2
        v2_ref[...] = v2

    blk = pl.BlockSpec((tr, c), lambda i: (i, 0))
    return _pcall(body, name=name, grid=(r // tr,), behind=behind,
                  in_specs=[pl.BlockSpec((N_DEV, tr, c), lambda i: (0, i, 0)), blk, blk, blk],
                  out_specs=[blk] * 4, out_shape=[_sds((r, c), F32)] * 4)(parts, w, m, v)


def _position():
    return lax.axis_index("x"), lax.axis_index("y"), lax.axis_index("c")


def _slot(px, py, pc):
    return 4 * px + 2 * py + pc


def _row_window(ref, rows):
    r0, r1 = rows
    return ref if (r0, r1) == (0, ref.shape[0]) else ref.at[pl.ds(r0, r1 - r0)]


def _split_items(items):
    sources = [src for src, _, _ in items]
    begun = [(a, dest) for a, (_, _, dest) in enumerate(items) if dest is not None]
    aliases = {len(sources) + k: a for k, (a, _) in enumerate(begun)}
    return sources + [dest for _, dest in begun], [rows for _, rows, _ in items], aliases


def _gather_carry(items):
    na = len(items)
    carry_ins, windows, aliases = _split_items(items)

    def plan(ins, outs, sems):
        send_sems, recv_sems, local_sems = sems
        x, y, c = _position()
        me, sibling = (x, y, c), (x, y, 1 - c)
        chips = [(1 - x, y), (x, 1 - y), (1 - x, 1 - y)]
        ins = [_row_window(ins[a], windows[a]) for a in range(na)]

        def block_rows(a, block):
            return _row_window(outs[a].at[_slot(*block)], windows[a])

        def copy(a, k, block, to, src=None):
            rows = block_rows(a, block)
            return pltpu.make_async_remote_copy(src_ref=rows if src is None else src, dst_ref=rows,
                                                send_sem=send_sems.at[k, a], recv_sem=recv_sems.at[k, a],
                                                device_id=to, device_id_type=MESH)

        mine = [pltpu.make_async_copy(ins[a], block_rows(a, me), local_sems.at[a]) for a in range(na)]
        first = [copy(a, 0, me, sibling, src=ins[a]) for a in range(na)]
        for j, chip in enumerate(chips):
            first += [copy(a, 1 + j, me, (*chip, c), src=ins[a]) for a in range(na)]
        landed = [[copy(a, 1 + j, (*chip, c), me) for a in range(na)] for j, chip in enumerate(chips)]
        passed = [[copy(a, 4 + j, (*chip, c), sibling) for a in range(na)] for j, chip in enumerate(chips)]
        from_sibling = [copy(a, 0, sibling, me) for a in range(na)]
        for j, chip in enumerate(chips):
            from_sibling += [copy(a, 4 + j, (*chip, 1 - c), me) for a in range(na)]
        return mine, first, landed, passed, from_sibling

    def start(ins, outs, sems):
        mine, first, _, _, _ = plan(ins, outs, sems)
        for cp in mine + first:
            cp.start()

    def mid(ins, outs, sems):
        _, _, landed, passed, _ = plan(ins, outs, sems)
        for over_ici, onward in zip(landed, passed):
            for cp, fwd in zip(over_ici, onward):
                cp.wait_recv()
                fwd.start()

    def finish(ins, outs, sems):
        mine, first, _, passed, from_sibling = plan(ins, outs, sems)
        for cp in from_sibling:
            cp.wait_recv()
        for cp in first + [fwd for onward in passed for fwd in onward]:
            cp.wait_send()
        for cp in mine:
            cp.wait()

    return _Carry(carry_ins, [_sds((N_DEV,) + src.shape, src.dtype) for src, _, _ in items],
                  [pltpu.SemaphoreType.DMA((7, na)), pltpu.SemaphoreType.DMA((7, na)),
                   pltpu.SemaphoreType.DMA((na,))], start, finish, mid, aliases)


def _exchange_carry(scattered, replicated=()):
    items = list(scattered) + [(a, (0, a.shape[0]), None) for a in replicated]
    na, ns = len(items), len(scattered)
    carry_ins, windows, aliases = _split_items(items)

    def plan(ins, outs, sems):
        send_sems, recv_sems, local_sems = sems
        me = _slot(*_position())

        def source(a, j):
            return _row_window(ins[a].at[j] if a < ns else ins[a], windows[a])

        def copy(a, j, i):
            return pltpu.make_async_remote_copy(src_ref=source(a, j), dst_ref=_row_window(outs[a].at[i], windows[a]),
                                                send_sem=send_sems.at[j, a], recv_sem=recv_sems.at[i, a],
                                                device_id=(j >> 2, (j >> 1) & 1, j & 1), device_id_type=MESH)

        def own(a, j):
            return pltpu.make_async_copy(source(a, j), _row_window(outs[a].at[j], windows[a]), local_sems.at[a])

        return me, copy, own

    def start(ins, outs, sems):
        me, copy, own = plan(ins, outs, sems)
        for a in range(na):
            for j in range(N_DEV):
                @pl.when(me == j)
                def _():
                    own(a, j).start()

                @pl.when(me != j)
                def _():
                    copy(a, j, me).start()

    def finish(ins, outs, sems):
        me, copy, own = plan(ins, outs, sems)
        for a in range(na):
            for j in range(N_DEV):
                @pl.when(me == j)
                def _():
                    for i in range(N_DEV):
                        if i != j:
                            copy(a, j, i).wait_recv()
                    own(a, j).wait()

                @pl.when(me != j)
                def _():
                    copy(a, j, me).wait_send()

    return _Carry(carry_ins, [_sds((N_DEV,) + src.shape[-2:], src.dtype) for src, _, _ in items],
                  [pltpu.SemaphoreType.DMA((N_DEV, na)), pltpu.SemaphoreType.DMA((N_DEV, na)),
                   pltpu.SemaphoreType.DMA((na,))], start, finish, None, aliases)


HBM_ARRAY = pl.BlockSpec(memory_space=pltpu.HBM)
SEMAPHORES = pl.BlockSpec(memory_space=pltpu.SEMAPHORE)
DATAFLOW = pltpu.SideEffectType.DATAFLOW_SIDE_EFFECTING


def _exchange_copy(parts_ref, land_ref, send_sems, recv_sems, me, j):
    return pltpu.make_async_remote_copy(src_ref=parts_ref.at[j], dst_ref=land_ref.at[me], send_sem=send_sems.at[j],
                                        recv_sem=recv_sems.at[me], device_id=(j >> 2, (j >> 1) & 1, j & 1),
                                        device_id_type=MESH)


def _exchange_start(parts, name):
    def body(parts_ref, land_ref, send_sems, recv_sems, parts_thru, land_thru, token):
        me = _slot(*_position())
        for j in range(N_DEV):
            @pl.when(me == j)
            def _():
                pltpu.make_async_copy(parts_ref.at[j], land_ref.at[j], send_sems.at[j]).start()

            @pl.when(me != j)
            def _():
                _exchange_copy(parts_ref, land_ref, send_sems, recv_sems, me, j).start()
        token[...] = jnp.zeros_like(token)

    return pl.pallas_call(
        body, name=name,
        out_shape=(pltpu.SemaphoreType.DMA((N_DEV,)), pltpu.SemaphoreType.DMA((N_DEV,)),
                   pltpu.HBM(parts.shape, parts.dtype), pltpu.HBM(parts.shape, parts.dtype), _sds((8, 128), F32)),
        in_specs=(HBM_ARRAY, HBM_ARRAY),
        out_specs=(SEMAPHORES, SEMAPHORES, HBM_ARRAY, HBM_ARRAY, pl.BlockSpec(memory_space=pltpu.VMEM)),
        input_output_aliases={0: 2, 1: 3}, compiler_params=pltpu.CompilerParams(has_side_effects=DATAFLOW),
    )(pltpu.with_memory_space_constraint(parts, pltpu.HBM),
      pltpu.with_memory_space_constraint(lax.empty(parts.shape, parts.dtype), pltpu.HBM))


def _exchange_wait(send_sems, recv_sems, parts_thru, land_thru, after, name):
    def body(parts_ref, land_ref, send_sems, recv_sems, *rest):
        me = _slot(*_position())
        for j in range(N_DEV):
            @pl.when(me == j)
            def _():
                pltpu.make_async_copy(parts_ref.at[j], land_ref.at[j], send_sems.at[j]).wait()

            @pl.when(me != j)
            def _():
                both = pltpu.make_async_remote_copy(src_ref=parts_ref.at[j], dst_ref=land_ref.at[j],
                                                    send_sem=send_sems.at[j], recv_sem=recv_sems.at[j],
                                                    device_id=(j >> 2, (j >> 1) & 1, j & 1), device_id_type=MESH)
                both.wait_send()
                both.wait_recv()

    return pl.pallas_call(
        body, name=name, out_shape=(pltpu.HBM(parts_thru.shape, parts_thru.dtype),
                                    pltpu.HBM(parts_thru.shape, parts_thru.dtype)),
        in_specs=(HBM_ARRAY, HBM_ARRAY, SEMAPHORES, SEMAPHORES) + (pl.BlockSpec(memory_space=pl.ANY),) * len(after),
        out_specs=(HBM_ARRAY, HBM_ARRAY), input_output_aliases={0: 0, 1: 1},
        compiler_params=pltpu.CompilerParams(has_side_effects=DATAFLOW),
    )(parts_thru, land_thru, send_sems, recv_sems, *after)[1]


class _Mesh:
    def __init__(self, shards):
        self.shards, self.full, self.received, self.cache, self.pending, self.tokens = shards, {}, {}, {}, {}, {}

    def fetch(self, wanted):
        items = []
        for want in wanted:
            name, r0, r1 = want if isinstance(want, tuple) else (want, 0, self.shards[want].shape[0])
            items.append((self.shards[name], (r0, r1), self.full.get(name)))
        return _gather_carry(items)

    def fetched(self, wanted, results):
        self.full.update(zip([want[0] if isinstance(want, tuple) else want for want in wanted], results))

    def send(self, *payloads):
        return _exchange_carry([(parts, rows or (0, parts.shape[1]), self.received.get(name))
                                for name, parts, rows in payloads])

    def sent(self, names, results):
        self.received.update(zip(names, results))

    def send_apart(self, name, parts):
        *self.pending[name], self.tokens[name] = _exchange_start(parts, "exchange_" + name + "_start")
        return self.tokens[name]

    def sent_apart(self, name, after):
        self.received[name] = _exchange_wait(*self.pending.pop(name), after, "exchange_" + name + "_wait")

    def w(self, key):
        if key not in self.cache:
            self.cache[key] = self._layout(key)
        return self.cache[key]

    def _layout(self, key):
        if key in ("gu1", "gu2"):
            return self.full[key]
        if key in ("d1", "d2"):
            return self.full[key].reshape(4, FS, D)
        if key in ("out", "q", "o"):
            return self.full[key].reshape(D, D)
        if key == "kv":
            return self.full["kv"]
        if key == "convw":
            rows = self.full["conv"][:, :3, :].transpose(1, 0, 2).reshape(3, D)
            return jnp.concatenate([rows, jnp.zeros((5, D), F32)], axis=0)
        assert key == "win_t", key
        return self.full["win"].reshape(-1, D)


def _forward_backward(x, mem, target, g, rel_bias, sinks, ex):
    s = x.shape[0]
    def fetching(wanted, call, *args, **kw):
        res, got = call(*args, carry=ex.fetch(wanted), **kw)
        ex.fetched(wanted, got)
        return res

    h1 = fetching(["gu1", "conv"], _rmsnorm, x, g["ffn1"], "norm_ffn1")
    gu1, a1 = fetching(["d1", ("win", 0, 400)], _ffn_up, h1, ex.w("gu1").reshape(2, 4, FS, D), "ffn1_up")
    x1, h2 = fetching([("win", 400, 832)], _mm_res_norm, a1, ex.w("d1"), x, g["mix"], 0.5, "ffn1_down")
    pa, q, kv = fetching(["gu2"], _in_proj, h2, ex.w("win_t"), "in_proj")
    biasm = _bias_build(rel_bias, "bias_build")
    attn, lse = fetching(["out", "kv", "o"], _swa_fwd, q, kv, biasm, sinks, "swa_fwd")
    merged = fetching(["q"], _conv_merge_fwd, pa, attn, ex.w("convw"), "conv_merge_fwd")
    (x2, h3), _ = _mm_res_norm(merged[None], ex.w("out")[None], x1, g["xattn"], 1.0, "out_proj")
    q2 = _mm_nn(h3, ex.w("q")[None], "xattn_q")[0][0]
    mh, _ = _rmsnorm(mem, g["mem"], "norm_mem")
    kv2 = _mm_nn(mh, ex.w("kv"), "xattn_kv")[0]
    o, lse2 = _xattn_fwd(q2, kv2, "xattn_fwd")
    (x3, h4), _ = _mm_res_norm(o[None], ex.w("o")[None], x2, g["ffn2"], 1.0, "xattn_o")
    gu2, a2 = fetching(["d2"], _ffn_up, h4, ex.w("gu2").reshape(2, 4, FS, D), "ffn2_up")
    dx4, dx4b, loss, d_final = _ffn_down_loss(a2, ex.w("d2"), x3, g["final"], target, "ffn2_down_loss")
    def sending(payloads, call, *args, **kw):
        res, got = call(*args, carry=ex.send(*payloads), **kw)
        ex.sent([name for name, _, _ in payloads], got)
        return res

    dw_d2 = _mm_tn(a2, dx4b[None], "dw_ffn2_down", scale=0.5)[0].reshape(N_DEV, -1, D)
    dgu2 = sending([("d2", dw_d2, (0, 288))], _ffn_down_bwd, dx4b, ex.w("d2"), gu2, "ffn2_down_bwd").reshape(8, s, FS)
    dw_gu2 = sending([("d2", dw_d2, (288, 352))], _mm_tn, dgu2, h4[None], "dw_ffn2_up", scale=0.5)
    dx3, dx3b, d_ffn2 = sending([("gu2", dw_gu2, (0, 368))], _mm_acc_rms_bwd, dgu2, ex.w("gu2"), "ffn2_up_bwd",
                                x=x3, gain=g["ffn2"], dres=dx4, scale=0.5)
    do, _ = _mm_acc(dx3b[None], ex.w("o")[None], "xattn_o_bwd", BF16, bt=True)
    dw_o = _mm_tn(o[None], dx3b[None], "dw_xattn_o")[0].reshape(N_DEV, -1, D)
    (dq2, dkv2), _ = _xattn_bwd(q2, kv2, o, do, lse2, "xattn_bwd")
    dkv2b = dkv2.astype(BF16)
    dw_q = _mm_tn(h3[None], dq2[None], "dw_xattn_q")[0].reshape(N_DEV, -1, D)
    (dx2, dx2b, d_xattn), _ = _mm_acc_rms_bwd(dq2[None], ex.w("q")[None], "xattn_q_bwd", x=x2, gain=g["xattn"],
                                              dres=dx3, bt=True)
    dw_kv = _mm_tn(mh[None], dkv2b, "dw_xattn_kv")[0]
    (_, _, d_mem), _ = _mm_acc_rms_bwd(dkv2b, ex.w("kv"), "xattn_kv_bwd", x=mem, gain=g["mem"],
                                       dres=jnp.zeros_like(mem), bt=True)
    dmerged, _ = _mm_acc(dx2b[None], ex.w("out")[None], "out_proj_bwd", BF16, bt=True)
    dw_out = _mm_tn(merged[None], dx2b[None], "dw_out_proj")[0].reshape(N_DEV, -1, D)
    dattn, dpa, d_convw = sending([("kv", dw_kv, None)], _conv_merge_bwd,
                                  dmerged, pa, attn, ex.w("convw"), "conv_merge_bwd")
    dq, dkv, dbias, d_sinks = sending([("gu2", dw_gu2, (368, FS)), ("out", dw_out, None)], _swa_bwd,
                                      q, kv, attn, dattn, lse, biasm, sinks, "swa_bwd")
    d_relb = _bias_bwd(dbias, "bias_bwd")
    w_rows = ex.w("win_t").shape[0]
    dw_in = sending([("o", dw_o, None), ("q", dw_q, None)], _mm_tn_rows, dpa, h2, "dw_in_proj_a", w_rows, NQ + NKV)
    dw_in = _mm_tn_rows(dq[None], h2, "dw_in_proj_q", w_rows, 0, begun=dw_in)[0]
    dw_in = _mm_tn_rows(dkv[None], h2, "dw_in_proj_kv", w_rows, NQ, begun=dw_in)[0].reshape(N_DEV, -1, D)
    (dx1, dx1b, d_mix), _ = _in_proj_bwd(dpa, dq, dkv, ex.w("win_t"), "in_proj_bwd", x=x1, gain=g["mix"], dres=dx2,
                                         behind=ex.send_apart("win", dw_in))
    dw_d1 = _mm_tn(a1, dx1b[None], "dw_ffn1_down", scale=0.5)[0].reshape(N_DEV, -1, D)
    dgu1 = _ffn_down_bwd(dx1b, ex.w("d1"), gu1, "ffn1_down_bwd", behind=ex.send_apart("d1", dw_d1))[0]
    dgu1 = dgu1.reshape(8, s, FS)
    dw_gu1 = _mm_tn(dgu1, h1[None], "dw_ffn1_up", scale=0.5)[0]
    (dx0, _, d_ffn1), _ = _mm_acc_rms_bwd(dgu1, ex.w("gu1"), "ffn1_up_bwd", x=x, gain=g["ffn1"], dres=dx1,
                                          scale=0.5, behind=ex.send_apart("gu1", dw_gu1))

    relb_row = jnp.concatenate([d_relb[:, :REL_BUCKETS].T.reshape(1, REL_BUCKETS * N_HEADS), d_sinks[:, :N_HEADS],
                                jnp.zeros((1, D - REL_BUCKETS * N_HEADS - N_HEADS), F32)], axis=1)
    loss_row = jnp.concatenate([loss[0:1, 0:1], jnp.zeros((1, D - 1), F32)], axis=1)
    small = jnp.concatenate([d_ffn1, d_mix, d_xattn, d_mem, d_ffn2, d_final, relb_row, loss_row, d_convw[0:3],
                             jnp.zeros((SMALL_ROWS - ROW_CONV - 3, D), F32)], axis=0)
    return dx0, small


def _pack_small(norms, final, relb, sinks, conv_local, me):
    relb_row = jnp.concatenate([relb.reshape(1, -1), sinks.reshape(1, -1),
                                jnp.zeros((1, D - REL_BUCKETS * N_HEADS - N_HEADS), F32)], axis=1)
    conv_rows = lax.dynamic_update_slice(jnp.zeros((3, D), F32), conv_local.reshape(3, -1), (0, 128 * me))
    return jnp.concatenate(list(norms) + [final.reshape(1, D), relb_row, jnp.zeros((1, D), F32), conv_rows,
                                          jnp.zeros((SMALL_ROWS - ROW_CONV - 3, D), F32)], axis=0)


def kernel(x, mem, positions, rel_bias, ffn1_norm, ffn1_w_gu, ffn1_w_down, mix_norm, w_in, sinks, conv_w, w_out, xattn_norm, mem_norm, xattn_wq, xattn_wkv, xattn_wo, ffn2_norm, ffn2_w_gu, ffn2_w_down, final_norm, loss_target, m_rel_bias, m_ffn1_norm, m_ffn1_w_gu, m_ffn1_w_down, m_mix_norm, m_w_in, m_sinks, m_conv_w, m_w_out, m_xattn_norm, m_mem_norm, m_xattn_wq, m_xattn_wkv, m_xattn_wo, m_ffn2_norm, m_ffn2_w_gu, m_ffn2_w_down, m_final_norm, v_rel_bias, v_ffn1_norm, v_ffn1_w_gu, v_ffn1_w_down, v_mix_norm, v_w_in, v_sinks, v_conv_w, v_w_out, v_xattn_norm, v_mem_norm, v_xattn_wq, v_xattn_wkv, v_xattn_wo, v_ffn2_norm, v_ffn2_w_gu, v_ffn2_w_down, v_final_norm):
    del positions
    me = _slot(*_position())
    big = dict(gu1=(ffn1_w_gu, m_ffn1_w_gu, v_ffn1_w_gu), d1=(ffn1_w_down, m_ffn1_w_down, v_ffn1_w_down),
               win=(w_in, m_w_in, v_w_in), out=(w_out, m_w_out, v_w_out), q=(xattn_wq, m_xattn_wq, v_xattn_wq),
               kv=(xattn_wkv, m_xattn_wkv, v_xattn_wkv), o=(xattn_wo, m_xattn_wo, v_xattn_wo),
               gu2=(ffn2_w_gu, m_ffn2_w_gu, v_ffn2_w_gu), d2=(ffn2_w_down, m_ffn2_w_down, v_ffn2_w_down))
    order = list(big)
    transposed = ("gu1", "gu2", "win")
    local = {k: tuple(t[0].T if k in transposed else t[0] for t in big[k]) for k in order}
    shards = {k: local[k][0].astype(BF16) for k in order}
    shards["conv"] = jnp.concatenate([conv_w[0], jnp.zeros((5, 128), F32)], axis=0)
    ex = _Mesh(shards)
    gains = dict(ffn1=ffn1_norm, mix=mix_norm, xattn=xattn_norm, mem=mem_norm, ffn2=ffn2_norm,
                 final=final_norm.reshape(1, D))
    dx, small = _forward_backward(x[0], mem[0], loss_target[0], gains, rel_bias, sinks, ex)
    apart = ("d1", "win", "gu1")
    big_out = {k: _adamw(ex.received[k], *local[k], "adamw_" + k, behind=ex.tokens["gu1"])
               for k in order if k not in apart}
    for k in apart[:-1]:
        ex.sent_apart(k, after=[big_out[j][1] for j in big_out])
        big_out[k] = _adamw(ex.received[k], *local[k], "adamw_" + k)
    small_parts = _run_alone(_exchange_carry([], [small]), "exchange_small", after=[big_out[k][1] for k in big_out])[0]
    packed = [_pack_small(norms, final, relb, sk, conv, me) for norms, final, relb, sk, conv in (
        ((ffn1_norm, mix_norm, xattn_norm, mem_norm, ffn2_norm), final_norm, rel_bias, sinks, conv_w),
        ((m_ffn1_norm, m_mix_norm, m_xattn_norm, m_mem_norm, m_ffn2_norm), m_final_norm, m_rel_bias, m_sinks, m_conv_w),
        ((v_ffn1_norm, v_mix_norm, v_xattn_norm, v_mem_norm, v_ffn2_norm), v_final_norm, v_rel_bias, v_sinks, v_conv_w))]
    small_out = _adamw(small_parts, *packed, "adamw_small")
    ex.sent_apart("gu1", after=[dx, small_out[1]] + [big_out[k][1] for k in big_out])
    big_out["gu1"] = _adamw(ex.received["gu1"], *local["gu1"], "adamw_gu1")
    big_out = {k: [t.T if k in transposed else t for t in big_out[k]] for k in order}

    def unpack(t):
        conv = lax.dynamic_slice(t[ROW_CONV:ROW_CONV + 3], (0, 128 * me), (3, 128))[None]
        nrel = REL_BUCKETS * N_HEADS
        return dict(ffn1_norm=t[0:1], mix_norm=t[1:2], xattn_norm=t[2:3], mem_norm=t[3:4], ffn2_norm=t[4:5],
                    final_norm=t[5], rel_bias=t[ROW_RELB, :nrel].reshape(REL_BUCKETS, N_HEADS),
                    sinks=t[ROW_RELB:ROW_RELB + 1, nrel:nrel + N_HEADS], conv_w=conv)

    names = dict(gu1="ffn1_w_gu", d1="ffn1_w_down", win="w_in", out="w_out", q="xattn_wq", kv="xattn_wkv",
                 o="xattn_wo", gu2="ffn2_w_gu", d2="ffn2_w_down")
    results = []
    for idx in range(4):
        leaves = unpack(small_out[idx])
        leaves.update({names[k]: big_out[k][idx][None] for k in order})
        results.append(leaves)
    weights = ("rel_bias", "ffn1_norm", "ffn1_w_gu", "ffn1_w_down", "mix_norm", "w_in", "sinks", "conv_w", "w_out",
               "xattn_norm", "mem_norm", "xattn_wq", "xattn_wkv", "xattn_wo", "ffn2_norm", "ffn2_w_gu", "ffn2_w_down",
               "final_norm")
    loss = small_out[0][ROW_LOSS, 0]
    return (loss, dx[None], *[leaves[n] for leaves in results for n in weights])
```

```python
import math

import numpy as np
import jax
import jax.numpy as jnp
from jax import lax
from jax.experimental import pallas as pl
from jax.experimental.pallas import tpu as pltpu

F32, BF16 = jnp.float32, jnp.bfloat16
MESH = pl.DeviceIdType.MESH

D = 1024
N_DEV = 8
D_FF = 2816
FS = D_FF // 4
HEAD = 64
N_HEADS, N_KV = 16, 4
BLK = 128
NQ, NKV = N_HEADS * HEAD, 2 * N_KV * HEAD
XH, XHD = 4, 256
REL_BUCKETS, REL_EXACT, REL_MAX_DIST = 32, 16, 128
EPS, NEG = 1e-6, -1e30
ADAM_LR, ADAM_B1, ADAM_B2, ADAM_EPS, ADAM_WD, ADAM_STEP = 0.001, 0.9, 0.999, 1e-08, 0.01, 10
VMEM_LIMIT_V7X = 56 * 2**20
SMALL_ROWS = 16
ROW_RELB, ROW_LOSS, ROW_CONV = 6, 7, 8


def _bucket_thresholds():
    n = np.arange(REL_MAX_DIST)
    nf = np.maximum(n, 1).astype(np.float32)
    large = REL_EXACT + (np.log(nf / np.float32(REL_EXACT)) / np.float32(math.log(REL_MAX_DIST / REL_EXACT))
                         * np.float32(REL_BUCKETS - REL_EXACT)).astype(np.int32)
    b = np.where(n < REL_EXACT, n, np.minimum(large, REL_BUCKETS - 1))
    return [int(np.argmax(b >= REL_EXACT + k)) for k in range(1, REL_BUCKETS - REL_EXACT)]


BUCKET_THRESHOLDS = _bucket_thresholds()


HBM_SPEC = pl.BlockSpec(memory_space=pl.ANY)


class _Carry:
    def __init__(self, ins, outs, sems, start, finish, mid=None, aliases=None):
        self.ins, self.outs, self.sems = list(ins), list(outs), list(sems)
        self.start, self.finish, self.mid, self.aliases = start, finish, mid, dict(aliases or {})


def _pcall(body, *, name, grid, in_specs, out_specs, out_shape, scratch=(), carry=None, aliases=None, behind=None):
    params = pltpu.CompilerParams(dimension_semantics=("arbitrary",) * len(grid), vmem_limit_bytes=VMEM_LIMIT_V7X)
    if carry is None and behind is not None:
        n_in = len(in_specs)
        call = pl.pallas_call(lambda *refs: body(*refs[:n_in], *refs[n_in + 1:]), name=name, grid=grid,
                              in_specs=list(in_specs) + [pl.BlockSpec((8, 128), lambda *_: (0, 0))],
                              out_specs=out_specs, out_shape=out_shape, scratch_shapes=list(scratch),
                              compiler_params=params, input_output_aliases=aliases or {})
        return lambda *args: call(*args, behind)
    if carry is None:
        return pl.pallas_call(body, name=name, grid=grid, in_specs=in_specs, out_specs=out_specs,
                              out_shape=out_shape, scratch_shapes=list(scratch), compiler_params=params,
                              input_output_aliases=aliases or {})
    assert aliases is None and behind is None, name
    single = not isinstance(out_shape, (list, tuple))
    own_specs, own_shapes = ([out_specs], [out_shape]) if single else (list(out_specs), list(out_shape))
    n_in, n_out, n_scr = len(in_specs), len(own_shapes), len(scratch)
    n_cin, n_cout = len(carry.ins), len(carry.outs)
    steps = math.prod(grid)
    mid_step = max(steps - 1 - max(steps // 8, 1), 0)

    def carrying(*refs):
        ins, refs = refs[:n_in], refs[n_in:]
        cins, refs = refs[:n_cin], refs[n_cin:]
        outs, refs = refs[:n_out], refs[n_out:]
        couts, refs = refs[:n_cout], refs[n_cout:]
        scr, csems = refs[:n_scr], refs[n_scr:]
        step = 0
        for axis, size in enumerate(grid):
            step = step * size + pl.program_id(axis)

        @pl.when(step == 0)
        def _():
            carry.start(cins, couts, csems)

        body(*ins, *outs, *scr)
        if carry.mid is not None:
            @pl.when(step == mid_step)
            def _():
                carry.mid(cins, couts, csems)

        @pl.when(step == steps - 1)
        def _():
            carry.finish(cins, couts, csems)

    call = pl.pallas_call(carrying, name=name, grid=grid, in_specs=list(in_specs) + [HBM_SPEC] * n_cin,
                          out_specs=own_specs + [HBM_SPEC] * n_cout, out_shape=own_shapes + carry.outs,
                          scratch_shapes=list(scratch) + carry.sems, compiler_params=params,
                          input_output_aliases={n_in + i: n_out + o for i, o in carry.aliases.items()})

    def run(*args):
        res = call(*args, *carry.ins)
        return (res[0] if single else res[:n_out]), res[n_out:]

    return run


def _run_alone(carry, name, after=()):
    n_cin, n_cout, n_after = len(carry.ins), len(carry.outs), len(after)

    def body(*refs):
        cins, refs = refs[:n_cin], refs[n_cin + n_after:]
        couts, csems = refs[:n_cout], refs[n_cout:]
        carry.start(cins, couts, csems)
        if carry.mid is not None:
            carry.mid(cins, couts, csems)
        carry.finish(cins, couts, csems)

    return pl.pallas_call(body, name=name, in_specs=[HBM_SPEC] * (n_cin + n_after), out_specs=[HBM_SPEC] * n_cout,
                          out_shape=carry.outs, scratch_shapes=carry.sems,
                          input_output_aliases=carry.aliases)(*carry.ins, *after)


def _dot(a, b):
    return jnp.dot(a, b, preferred_element_type=F32)


def _dot_nt(a, b):
    return lax.dot_general(a, b, (((1,), (1,)), ((), ())), preferred_element_type=F32)


def _dot_tn(a, b):
    return lax.dot_general(a, b, (((0,), (0,)), ((), ())), preferred_element_type=F32)


def _sds(shape, dtype):
    return jax.ShapeDtypeStruct(tuple(shape), dtype)


ROW_CHUNK = 256


def _row_chunks(tm):
    return [slice(r, min(r + ROW_CHUNK, tm)) for r in range(0, tm, ROW_CHUNK)]


def _carried(call, args, carry):
    return call(*args) if carry is not None else (call(*args), ())


def _rmsnorm(x, g, name, carry=None):
    m, d = x.shape
    tm = min(512, m)

    def body(x_ref, g_ref, h_ref):
        xv = x_ref[...]
        r = lax.rsqrt(jnp.mean(xv * xv, axis=-1, keepdims=True) + EPS)
        h_ref[...] = (xv * r * g_ref[...]).astype(BF16)

    call = _pcall(body, name=name, grid=(m // tm,), carry=carry,
                  in_specs=[pl.BlockSpec((tm, d), lambda i: (i, 0)), pl.BlockSpec((1, d), lambda i: (0, 0))],
                  out_specs=pl.BlockSpec((tm, d), lambda i: (i, 0)), out_shape=_sds((m, d), BF16))
    return _carried(call, (x, g), carry)


def _mm_nn(a, b, name, tm=1024, bt=False, carry=None):
    m, k = a.shape
    nj = b.shape[0]
    n = b.shape[1] if bt else b.shape[2]
    tm = min(tm, m)
    dot = _dot_nt if bt else _dot

    def body(a_ref, b_ref, o_ref):
        o_ref[...] = dot(a_ref[...], b_ref[...]).astype(BF16)

    call = _pcall(body, name=name, grid=(nj, m // tm),
                  in_specs=[pl.BlockSpec((tm, k), lambda j, i: (i, 0)),
                            pl.BlockSpec((None,) + b.shape[1:], lambda j, i: (j, 0, 0))],
                  out_specs=pl.BlockSpec((None, tm, n), lambda j, i: (j, i, 0)),
                  out_shape=_sds((nj, m, n), BF16), carry=carry)
    return _carried(call, (a, b), carry)


def _load_once(src_hbm, dst_vmem, sem):
    @pl.when(pl.program_id(0) == 0)
    def _():
        load = pltpu.make_async_copy(src_hbm, dst_vmem, sem)
        load.start()
        load.wait()


def _resident(w):
    return [pltpu.VMEM(w.shape, w.dtype), pltpu.SemaphoreType.DMA(())]


def _ffn_up(h, w4, name, tm=512, carry=None):
    s, d = h.shape
    tm = min(tm, s)

    def body(h_ref, w_hbm, gu_ref, a_ref, w_ref, w_sem):
        _load_once(w_hbm, w_ref, w_sem)
        for p in range(4):
            for rows in _row_chunks(tm):
                hv = h_ref[rows, :]
                g = _dot_nt(hv, w_ref[0, p])
                u = _dot_nt(hv, w_ref[1, p])
                gu_ref[0, p, rows, :] = g.astype(BF16)
                gu_ref[1, p, rows, :] = u.astype(BF16)
                a_ref[p, rows, :] = (g * jax.nn.sigmoid(g) * u).astype(BF16)

    call = _pcall(body, name=name, grid=(s // tm,),
                  in_specs=[pl.BlockSpec((tm, d), lambda i: (i, 0)), HBM_SPEC],
                  out_specs=[pl.BlockSpec((2, 4, tm, FS), lambda i: (0, 0, i, 0)),
                             pl.BlockSpec((4, tm, FS), lambda i: (0, i, 0))],
                  out_shape=[_sds((2, 4, s, FS), BF16), _sds((4, s, FS), BF16)], scratch=_resident(w4), carry=carry)
    return _carried(call, (h, w4), carry)


N_SEG = 5
IN_PROJ_WEIGHTS = [pltpu.VMEM((NQ, D), BF16), pltpu.VMEM((NKV, D), BF16), pltpu.VMEM((N_SEG, D, D), BF16),
                   pltpu.SemaphoreType.DMA((2 + N_SEG,))]


def _load_in_proj(w_hbm, wq_ref, wkv_ref, wa_ref, sems):
    @pl.when(pl.program_id(0) == 0)
    def _():
        loads = [pltpu.make_async_copy(w_hbm.at[pl.ds(0, NQ)], wq_ref, sems.at[0]),
                 pltpu.make_async_copy(w_hbm.at[pl.ds(NQ, NKV)], wkv_ref, sems.at[1])]
        loads += [pltpu.make_async_copy(w_hbm.at[pl.ds(NQ + NKV + D * j, D)], wa_ref.at[j], sems.at[2 + j])
                  for j in range(N_SEG)]
        for load in loads:
            load.start()
        for load in loads:
            load.wait()


def _in_proj(h, w_in_t, name, tm=512, carry=None):
    s, d = h.shape
    tm = min(tm, s)

    def body(h_ref, w_hbm, pa_ref, q_ref, kv_ref, wq_ref, wkv_ref, wa_ref, sems):
        _load_in_proj(w_hbm, wq_ref, wkv_ref, wa_ref, sems)
        hv = h_ref[...]
        q_ref[...] = _dot_nt(hv, wq_ref[...]).astype(BF16)
        kv_ref[...] = _dot_nt(hv, wkv_ref[...]).astype(BF16)
        for j in range(N_SEG):
            pa_ref[j] = _dot_nt(hv, wa_ref[j]).astype(BF16)

    call = _pcall(body, name=name, grid=(s // tm,), carry=carry,
                  in_specs=[pl.BlockSpec((tm, d), lambda i: (i, 0)), HBM_SPEC],
                  out_specs=[pl.BlockSpec((N_SEG, tm, d), lambda i: (0, i, 0)),
                             pl.BlockSpec((tm, NQ), lambda i: (i, 0)), pl.BlockSpec((tm, NKV), lambda i: (i, 0))],
                  out_shape=[_sds((N_SEG, s, d), BF16), _sds((s, NQ), BF16), _sds((s, NKV), BF16)],
                  scratch=IN_PROJ_WEIGHTS)
    return _carried(call, (h, w_in_t), carry)


def _mm_res_norm(a, w, xres, gain, scale, name, tm=512, carry=None):
    npart, s, kp = a.shape
    tm = min(tm, s)

    def body(a_ref, w_ref, x_ref, g_ref, xo_ref, h_ref):
        for rows in _row_chunks(tm):
            acc = _dot(a_ref[0, rows, :], w_ref[0])
            for p in range(1, npart):
                acc = acc + _dot(a_ref[p, rows, :], w_ref[p])
            xn = x_ref[rows, :] + scale * acc
            xo_ref[rows, :] = xn
            r = lax.rsqrt(jnp.mean(xn * xn, axis=-1, keepdims=True) + EPS)
            h_ref[rows, :] = (xn * r * g_ref[...]).astype(BF16)

    call = _pcall(body, name=name, grid=(s // tm,),
                  in_specs=[pl.BlockSpec((npart, tm, kp), lambda i: (0, i, 0)),
                            pl.BlockSpec((npart, kp, D), lambda i: (0, 0, 0)),
                            pl.BlockSpec((tm, D), lambda i: (i, 0)),
                            pl.BlockSpec((1, D), lambda i: (0, 0))],
                  out_specs=[pl.BlockSpec((tm, D), lambda i: (i, 0)), pl.BlockSpec((tm, D), lambda i: (i, 0))],
                  out_shape=[_sds((s, D), F32), _sds((s, D), BF16)], carry=carry)
    return _carried(call, (a, w, xres, gain), carry)


def _ffn_down_loss(a, w, xres, gain, target, name, tm=512):
    npart, s, kp = a.shape
    tm = min(tm, s)

    def body(a_ref, w_ref, x_ref, g_ref, t_ref, dx_ref, dxb_ref, loss_ref, dg_ref):
        @pl.when(pl.program_id(0) == 0)
        def _():
            loss_ref[...] = jnp.zeros_like(loss_ref)
            dg_ref[...] = jnp.zeros_like(dg_ref)

        for rows in _row_chunks(tm):
            acc = _dot(a_ref[0, rows, :], w_ref[0])
            for p in range(1, npart):
                acc = acc + _dot(a_ref[p, rows, :], w_ref[p])
            xn = x_ref[rows, :] + 0.5 * acc
            r = lax.rsqrt(jnp.mean(xn * xn, axis=-1, keepdims=True) + EPS)
            xh = xn * r
            gv = g_ref[...]
            err = xh * gv - t_ref[rows, :]
            part = 0.5 * jnp.sum(jnp.mean(err * err, axis=-1, keepdims=True), axis=0, keepdims=True)
            dy = err * (1.0 / D)
            dyg = dy * gv
            dxn = r * (dyg - xh * jnp.mean(dyg * xh, axis=-1, keepdims=True))
            dx_ref[rows, :] = dxn
            dxb_ref[rows, :] = dxn.astype(BF16)
            loss_ref[...] += jnp.broadcast_to(part, loss_ref.shape)
            dg_ref[...] += jnp.sum(dy * xh, axis=0, keepdims=True)

    return _pcall(body, name=name, grid=(s // tm,),
                  in_specs=[pl.BlockSpec((npart, tm, kp), lambda i: (0, i, 0)),
                            pl.BlockSpec((npart, kp, D), lambda i: (0, 0, 0)),
                            pl.BlockSpec((tm, D), lambda i: (i, 0)),
                            pl.BlockSpec((1, D), lambda i: (0, 0)),
                            pl.BlockSpec((tm, D), lambda i: (i, 0))],
                  out_specs=[pl.BlockSpec((tm, D), lambda i: (i, 0)), pl.BlockSpec((tm, D), lambda i: (i, 0)),
                             pl.BlockSpec((8, 128), lambda i: (0, 0)), pl.BlockSpec((1, D), lambda i: (0, 0))],
                  out_shape=[_sds((s, D), F32), _sds((s, D), BF16), _sds((8, 128), F32), _sds((1, D), F32)],
                  )(a, w, xres, gain, target)


def _window_tiles():
    i = lax.broadcasted_iota(jnp.int32, (BLK, BLK), 0)
    j = lax.broadcasted_iota(jnp.int32, (BLK, BLK), 1)
    rel = (i - j) & (BLK - 1)
    large = jnp.full_like(rel, REL_EXACT)
    for t in BUCKET_THRESHOLDS:
        large = large + (rel >= t).astype(jnp.int32)
    return j <= i, jnp.where(rel < REL_EXACT, rel, large)


def _bias_build(rel_bias, name):
    def body(rb_ref, o_ref):
        _, bucket = _window_tiles()

        def per_head(h, carry):
            acc = jnp.zeros((BLK, BLK), F32)
            for b in range(REL_BUCKETS):
                acc = jnp.where(bucket == b, rb_ref[b, h], acc)
            o_ref[h] = acc
            return carry

        lax.fori_loop(0, N_HEADS, per_head, 0)

    return _pcall(body, name=name, grid=(1,),
                  in_specs=[pl.BlockSpec(memory_space=pltpu.SMEM)],
                  out_specs=pl.BlockSpec((N_HEADS, BLK, BLK), lambda i: (0, 0, 0)),
                  out_shape=_sds((N_HEADS, BLK, BLK), F32))(rel_bias)


def _bias_bwd(dbias, name):
    def body(db_ref, o_ref):
        _, bucket = _window_tiles()
        lane = lax.broadcasted_iota(jnp.int32, (N_HEADS, 128), 1)

        def per_bucket(b, out):
            mb = (bucket == b).astype(F32)
            per_col = jnp.sum(db_ref[...] * mb[None, :, :], axis=1)
            return jnp.where(lane == b, jnp.sum(per_col, axis=1, keepdims=True), out)

        o_ref[...] = lax.fori_loop(0, REL_BUCKETS, per_bucket, jnp.zeros((N_HEADS, 128), F32))

    return _pcall(body, name=name, grid=(1,),
                  in_specs=[pl.BlockSpec((N_HEADS, BLK, BLK), lambda i: (0, 0, 0))],
                  out_specs=pl.BlockSpec((N_HEADS, 128), lambda i: (0, 0)),
                  out_shape=_sds((N_HEADS, 128), F32))(dbias)


PAIR = 2 * HEAD
GROUP = N_HEADS // N_KV
SWA_SCALE = HEAD ** -0.5


def _window_masks(n):
    i = lax.broadcasted_iota(jnp.int32, (GROUP * BLK, BLK), 0) & (BLK - 1)
    j = lax.broadcasted_iota(jnp.int32, (GROUP * BLK, BLK), 1)
    return j <= i, jnp.logical_and(n == 0, j > i), j < HEAD


def _kv_twice(ref, base, g, low):
    slab = ref[:, base + PAIR * (g // 2): base + PAIR * (g // 2 + 1)]
    swapped = pltpu.roll(slab, HEAD, 1)
    return jnp.where(low, slab, swapped) if g % 2 == 0 else jnp.where(low, swapped, slab)


def _stack_heads(ref, g, low):
    parts = []
    for r in range(2):
        slab = ref[:, PAIR * (2 * g + r): PAIR * (2 * g + r + 1)]
        zero = jnp.zeros_like(slab)
        parts += [jnp.where(low, slab, zero), jnp.where(low, zero, slab)]
    return jnp.concatenate(parts, axis=0)


def _unstack_heads(t, low):
    return [jnp.where(low, t[2 * r * BLK:(2 * r + 1) * BLK], t[(2 * r + 1) * BLK:(2 * r + 2) * BLK])
            for r in range(2)]


def _head_rows(t, k):
    return t[k * BLK:(k + 1) * BLK]


def _per_head_column(values):
    head = lax.broadcasted_iota(jnp.int32, (GROUP * BLK, 1), 0) // BLK
    col = jnp.full((GROUP * BLK, 1), values[0], F32)
    for k in range(1, GROUP):
        col = jnp.where(head == k, values[k], col)
    return col


def _window_logits(q4, kc, kp, bias4, own, absent):
    sc = jnp.where(own, _dot_nt(q4, kc), _dot_nt(q4, kp)) * SWA_SCALE + bias4
    return jnp.where(absent, NEG, sc)


def _split_window(t, own):
    zero = jnp.zeros_like(t)
    return jnp.where(own, t, zero), jnp.where(own, zero, t)


def _swa_fwd(q, kv, bias, sinks, name, carry=None):
    s = q.shape[0]
    nb = s // BLK
    kvw = 2 * N_KV * HEAD

    def body(q_ref, kc_ref, kp_ref, b_ref, sk_ref, o_ref, lse_ref):
        own, absent, low4 = _window_masks(pl.program_id(0))
        low = low4[:BLK]
        lane = lax.broadcasted_iota(jnp.int32, (BLK, 128), 1)
        lse_t = jnp.zeros((BLK, 128), F32)
        for g in range(N_KV):
            q4 = _stack_heads(q_ref, g, low)
            kc, kp = _kv_twice(kc_ref, 0, g, low), _kv_twice(kp_ref, 0, g, low)
            vc, vp = _kv_twice(kc_ref, N_KV * HEAD, g, low), _kv_twice(kp_ref, N_KV * HEAD, g, low)
            bias4 = b_ref[GROUP * g:GROUP * (g + 1)].reshape(GROUP * BLK, BLK)
            sc = _window_logits(q4, kc, kp, bias4, own, absent)
            sk = _per_head_column([sk_ref[0, GROUP * g + k] for k in range(GROUP)])
            m = jnp.maximum(jnp.max(sc, axis=1, keepdims=True), sk)
            p = jnp.exp(sc - m)
            l = jnp.sum(p, axis=1, keepdims=True) + jnp.exp(sk - m)
            p_own, p_prev = _split_window(p.astype(BF16), own)
            out = (_dot(p_own, vc) + _dot(p_prev, vp)) * (1.0 / l)
            for r, slab in enumerate(_unstack_heads(out, low)):
                o_ref[:, PAIR * (2 * g + r): PAIR * (2 * g + r + 1)] = slab.astype(BF16)
            lse4 = m + jnp.log(l)
            for k in range(GROUP):
                lse_t = jnp.where(lane == GROUP * g + k, _head_rows(lse4, k), lse_t)
        lse_ref[...] = lse_t

    call = _pcall(body, name=name, grid=(nb,),
                  in_specs=[pl.BlockSpec((BLK, D), lambda n: (n, 0)),
                            pl.BlockSpec((BLK, kvw), lambda n: (n, 0)),
                            pl.BlockSpec((BLK, kvw), lambda n: (jnp.maximum(n - 1, 0), 0)),
                            pl.BlockSpec((N_HEADS, BLK, BLK), lambda n: (0, 0, 0)),
                            pl.BlockSpec(memory_space=pltpu.SMEM)],
                  out_specs=[pl.BlockSpec((BLK, D), lambda n: (n, 0)), pl.BlockSpec((BLK, 128), lambda n: (n, 0))],
                  out_shape=[_sds((s, D), BF16), _sds((s, 128), F32)], carry=carry)
    return _carried(call, (q, kv, kv, bias, sinks), carry)


def _fold_halves(t, g, low):
    folded = jnp.where(low, t, 0.0) + pltpu.roll(jnp.where(low, 0.0, t), HEAD, 1)
    return folded if g % 2 == 0 else pltpu.roll(folded, HEAD, 1)


def _swa_bwd(q, kv, attn, dattn, lse, bias, sinks, name, carry=None):
    s = q.shape[0]
    nb = s // BLK
    kvw = 2 * N_KV * HEAD
    voff = N_KV * HEAD

    def body(q_ref, kc_ref, kp_ref, o_ref, do_ref, lse_ref, b_ref, skrow_ref, dq_ref, dkv_ref, dbias_ref, dsk_ref,
             dq_hold, kv_hold, dq_new, kv_prev, kv_cur):
        n = pl.program_id(0)

        @pl.when(n == 0)
        def _():
            dbias_ref[...] = jnp.zeros_like(dbias_ref)
            dsk_ref[...] = jnp.zeros_like(dsk_ref)
            dq_hold[...] = jnp.zeros_like(dq_hold)
            kv_hold[...] = jnp.zeros_like(kv_hold)

        @pl.when(n < nb)
        def _():
            own, absent, low4 = _window_masks(n)
            low = low4[:BLK]
            lane = lax.broadcasted_iota(jnp.int32, (BLK, 128), 1)
            delta_t = jnp.zeros((BLK, 128), F32)
            ones = jnp.ones((PAIR, 128), BF16)
            for pair_of_kv in range(N_KV // 2):
                slab_grads = [jnp.zeros((BLK, PAIR), F32) for _ in range(4)]
                for g in (2 * pair_of_kv, 2 * pair_of_kv + 1):
                    q4, do4 = _stack_heads(q_ref, g, low), _stack_heads(do_ref, g, low)
                    kc, kp = _kv_twice(kc_ref, 0, g, low), _kv_twice(kp_ref, 0, g, low)
                    vc, vp = _kv_twice(kc_ref, voff, g, low), _kv_twice(kp_ref, voff, g, low)
                    o_slabs = [o_ref[:, PAIR * (2 * g + r): PAIR * (2 * g + r + 1)] for r in range(2)]
                    o4 = jnp.concatenate([o_slabs[0], o_slabs[0], o_slabs[1], o_slabs[1]], axis=0)
                    delta = _dot(do4 * o4, ones)
                    heads = range(GROUP * g, GROUP * (g + 1))
                    lse4 = jnp.concatenate([lse_ref[:, h:h + 1] for h in heads], axis=0)
                    bias4 = b_ref[GROUP * g:GROUP * (g + 1)].reshape(GROUP * BLK, BLK)
                    p = jnp.exp(_window_logits(q4, kc, kp, bias4, own, absent) - lse4)
                    dp = jnp.where(own, _dot_nt(do4, vc), _dot_nt(do4, vp))
                    ds = p * (dp - delta)
                    dbias_ref[GROUP * g:GROUP * (g + 1)] += ds.reshape(GROUP, BLK, BLK)
                    for k, h in enumerate(heads):
                        delta_t = jnp.where(lane == h, _head_rows(delta, k), delta_t)
                    ds_own, ds_prev = _split_window((ds * SWA_SCALE).astype(BF16), own)
                    p_own, p_prev = _split_window(p.astype(BF16), own)
                    dq4 = _dot(ds_own, kc) + _dot(ds_prev, kp)
                    for r, slab in enumerate(_unstack_heads(dq4, low)):
                        dq_new[:, PAIR * (2 * g + r): PAIR * (2 * g + r + 1)] = slab
                    grads = [_dot_tn(ds_own, q4), _dot_tn(ds_prev, q4), _dot_tn(p_own, do4), _dot_tn(p_prev, do4)]
                    slab_grads = [t + _fold_halves(dk, g, low) for t, dk in zip(slab_grads, grads)]
                ks = slice(PAIR * pair_of_kv, PAIR * (pair_of_kv + 1))
                vs = slice(voff + PAIR * pair_of_kv, voff + PAIR * (pair_of_kv + 1))
                kv_cur[:, ks], kv_prev[:, ks], kv_cur[:, vs], kv_prev[:, vs] = slab_grads
            dsk_ref[...] -= jnp.sum(jnp.exp(skrow_ref[...] - lse_ref[...]) * delta_t, axis=0, keepdims=True)

        @pl.when(n == nb)
        def _():
            kv_prev[...] = jnp.zeros_like(kv_prev)

        dq_ref[...] = dq_hold[...].astype(BF16)
        dkv_ref[...] = (kv_hold[...] + kv_prev[...]).astype(BF16)

        @pl.when(n < nb)
        def _():
            dq_hold[...] = dq_new[...]
            kv_hold[...] = kv_cur[...]

    def cur(n):
        return jnp.minimum(n, nb - 1)

    call = _pcall(body, name=name, grid=(nb + 1,), carry=carry,
                  in_specs=[pl.BlockSpec((BLK, D), lambda n: (cur(n), 0)),
                            pl.BlockSpec((BLK, kvw), lambda n: (cur(n), 0)),
                            pl.BlockSpec((BLK, kvw), lambda n: (jnp.maximum(cur(n) - 1, 0), 0)),
                            pl.BlockSpec((BLK, D), lambda n: (cur(n), 0)),
                            pl.BlockSpec((BLK, D), lambda n: (cur(n), 0)),
                            pl.BlockSpec((BLK, 128), lambda n: (cur(n), 0)),
                            pl.BlockSpec((N_HEADS, BLK, BLK), lambda n: (0, 0, 0)),
                            pl.BlockSpec((1, 128), lambda n: (0, 0))],
                  out_specs=[pl.BlockSpec((BLK, D), lambda n: (jnp.maximum(n - 1, 0), 0)),
                             pl.BlockSpec((BLK, kvw), lambda n: (jnp.maximum(n - 1, 0), 0)),
                             pl.BlockSpec((N_HEADS, BLK, BLK), lambda n: (0, 0, 0)),
                             pl.BlockSpec((1, 128), lambda n: (0, 0))],
                  out_shape=[_sds((s, D), BF16), _sds((s, kvw), BF16), _sds((N_HEADS, BLK, BLK), F32),
                             _sds((1, 128), F32)],
                  scratch=[pltpu.VMEM((BLK, D), F32), pltpu.VMEM((BLK, kvw), F32), pltpu.VMEM((BLK, D), F32),
                           pltpu.VMEM((BLK, kvw), F32), pltpu.VMEM((BLK, kvw), F32)])
    sink_row = jnp.pad(sinks, ((0, 0), (0, 128 - N_HEADS)))
    return _carried(call, (q, kv, kv, attn, dattn, lse, bias, sink_row), carry)


HALO = 16
CW = D


def _conv_taps(cu, halo_cu, first_tile):
    row = lax.broadcasted_iota(jnp.int32, cu.shape, 0)
    halo_cu = jnp.where(first_tile, 0.0, halo_cu)
    c1 = jnp.where(row == 0, halo_cu[HALO - 1:HALO], pltpu.roll(cu, 1, 0))
    c2 = jnp.where(row == 0, halo_cu[HALO - 2:HALO - 1],
                   jnp.where(row == 1, halo_cu[HALO - 1:HALO], pltpu.roll(cu, 2, 0)))
    return c1, c2


def _conv_merge_fwd(pa, attn, convw, name, ts=256, carry=None):
    _, s, _ = pa.shape
    ts = min(ts, s)
    hb = ts // HALO

    def body(pa_ref, hp_ref, at_ref, w_ref, o_ref):
        i = pl.program_id(1)
        cu = pa_ref[0].astype(F32) * pa_ref[2].astype(F32)
        c1, c2 = _conv_taps(cu, hp_ref[0].astype(F32) * hp_ref[2].astype(F32), i == 0)
        w = w_ref[...]
        c3 = w[0:1] * c2 + w[1:2] * c1 + w[2:3] * cu
        conv = pa_ref[1].astype(F32) * c3
        o_ref[...] = (jax.nn.sigmoid(pa_ref[3].astype(F32)) * at_ref[...].astype(F32)
                      + jax.nn.sigmoid(pa_ref[4].astype(F32)) * conv).astype(BF16)

    call = _pcall(body, name=name, grid=(D // CW, s // ts), carry=carry,
                  in_specs=[pl.BlockSpec((5, ts, CW), lambda c, i: (0, i, c)),
                            pl.BlockSpec((5, HALO, CW), lambda c, i: (0, jnp.maximum(i * hb - 1, 0), c)),
                            pl.BlockSpec((ts, CW), lambda c, i: (i, c)),
                            pl.BlockSpec((8, CW), lambda c, i: (0, c))],
                  out_specs=pl.BlockSpec((ts, CW), lambda c, i: (i, c)),
                  out_shape=_sds((s, D), BF16))
    return _carried(call, (pa, pa, attn, convw), carry)


def _conv_merge_bwd(dmerged, pa, attn, convw, name, ts=256, carry=None):
    _, s, _ = pa.shape
    ts = min(ts, s)
    hb = ts // HALO
    last_hb = s // HALO - 1

    def body(dm_ref, pa_ref, at_ref, w_ref, hp_ref, hn_ref, dmn_ref, dat_ref, dpa_ref, dw_ref):
        i = pl.program_id(1)
        last = i == pl.num_programs(1) - 1
        dm = dm_ref[...].astype(F32)
        cp, bp, u = pa_ref[0].astype(F32), pa_ref[1].astype(F32), pa_ref[2].astype(F32)
        sa = jax.nn.sigmoid(pa_ref[3].astype(F32))
        sc = jax.nn.sigmoid(pa_ref[4].astype(F32))
        at = at_ref[...].astype(F32)
        cu = cp * u
        c1, c2 = _conv_taps(cu, hp_ref[0].astype(F32) * hp_ref[2].astype(F32), i == 0)
        w = w_ref[...]
        c3 = w[0:1] * c2 + w[1:2] * c1 + w[2:3] * cu
        dconv = dm * sc
        dc3 = dconv * bp
        nxt = dmn_ref[...].astype(F32) * jax.nn.sigmoid(hn_ref[4].astype(F32)) * hn_ref[1].astype(F32)
        nxt = jnp.where(last, 0.0, nxt)
        row = lax.broadcasted_iota(jnp.int32, dc3.shape, 0)
        d1 = jnp.where(row == ts - 1, nxt[0:1], pltpu.roll(dc3, ts - 1, 0))
        d2 = jnp.where(row == ts - 2, nxt[0:1], jnp.where(row == ts - 1, nxt[1:2], pltpu.roll(dc3, ts - 2, 0)))
        dcu = w[2:3] * dc3 + w[1:2] * d1 + w[0:1] * d2
        dat_ref[...] = (dm * sa).astype(BF16)
        dpa_ref[0] = (dcu * u).astype(BF16)
        dpa_ref[1] = (dconv * c3).astype(BF16)
        dpa_ref[2] = (dcu * cp).astype(BF16)
        dpa_ref[3] = (dm * at * sa * (1.0 - sa)).astype(BF16)
        dpa_ref[4] = (dm * bp * c3 * sc * (1.0 - sc)).astype(BF16)

        @pl.when(i == 0)
        def _():
            dw_ref[...] = jnp.zeros_like(dw_ref)

        dw_ref[0:1, :] += jnp.sum(dc3 * c2, axis=0, keepdims=True)
        dw_ref[1:2, :] += jnp.sum(dc3 * c1, axis=0, keepdims=True)
        dw_ref[2:3, :] += jnp.sum(dc3 * cu, axis=0, keepdims=True)

    call = _pcall(body, name=name, grid=(D // CW, s // ts), carry=carry,
                  in_specs=[pl.BlockSpec((ts, CW), lambda c, i: (i, c)),
                            pl.BlockSpec((5, ts, CW), lambda c, i: (0, i, c)),
                            pl.BlockSpec((ts, CW), lambda c, i: (i, c)),
                            pl.BlockSpec((8, CW), lambda c, i: (0, c)),
                            pl.BlockSpec((5, HALO, CW), lambda c, i: (0, jnp.maximum(i * hb - 1, 0), c)),
                            pl.BlockSpec((5, HALO, CW), lambda c, i: (0, jnp.minimum((i + 1) * hb, last_hb), c)),
                            pl.BlockSpec((HALO, CW), lambda c, i: (jnp.minimum((i + 1) * hb, last_hb), c))],
                  out_specs=[pl.BlockSpec((ts, CW), lambda c, i: (i, c)),
                             pl.BlockSpec((5, ts, CW), lambda c, i: (0, i, c)),
                             pl.BlockSpec((8, CW), lambda c, i: (0, c))],
                  out_shape=[_sds((s, D), BF16), _sds((5, s, D), BF16), _sds((8, D), F32)])
    return _carried(call, (dmerged, pa, attn, convw, pa, pa, dmerged), carry)


def _xattn_fwd(q, kv, name, tq=1024):
    s, _ = q.shape
    nm = kv.shape[1]
    tq = min(tq, s)

    def body(q_ref, kv_ref, o_ref, lse_ref):
        lane = lax.broadcasted_iota(jnp.int32, (tq, 128), 1)
        lse_t = jnp.zeros((tq, 128), F32)
        for h in range(XH):
            hs = slice(XHD * h, XHD * (h + 1))
            sc = _dot_nt(q_ref[:, hs], kv_ref[h]) * (XHD ** -0.5)
            m = jnp.max(sc, axis=1, keepdims=True)
            p = jnp.exp(sc - m)
            l = jnp.sum(p, axis=1, keepdims=True)
            o_ref[:, hs] = (_dot(p.astype(BF16), kv_ref[XH + h]) * (1.0 / l)).astype(BF16)
            lse_t = jnp.where(lane == h, m + jnp.log(l), lse_t)
        lse_ref[...] = lse_t

    return _pcall(body, name=name, grid=(s // tq,),
                  in_specs=[pl.BlockSpec((tq, D), lambda i: (i, 0)), pl.BlockSpec((2 * XH, nm, XHD), lambda i: (0, 0, 0))],
                  out_specs=[pl.BlockSpec((tq, D), lambda i: (i, 0)), pl.BlockSpec((tq, 128), lambda i: (i, 0))],
                  out_shape=[_sds((s, D), BF16), _sds((s, 128), F32)])(q, kv)


def _xattn_bwd(q, kv, o, do, lse, name, tq=512, carry=None):
    s, _ = q.shape
    nm = kv.shape[1]
    tq = min(tq, s)

    def body(q_ref, kv_ref, o_ref, do_ref, lse_ref, dq_ref, dkv_ref):
        @pl.when(pl.program_id(0) == 0)
        def _():
            dkv_ref[...] = jnp.zeros_like(dkv_ref)

        for h in range(XH):
            hs = slice(XHD * h, XHD * (h + 1))
            qh, kh, vh, dob = q_ref[:, hs], kv_ref[h], kv_ref[XH + h], do_ref[:, hs]
            p = jnp.exp(_dot_nt(qh, kh) * (XHD ** -0.5) - lse_ref[:, h:h + 1])
            dp = _dot_nt(dob, vh)
            delta = jnp.sum(dob.astype(F32) * o_ref[:, hs].astype(F32), axis=1, keepdims=True)
            dsb = (p * (dp - delta) * (XHD ** -0.5)).astype(BF16)
            dq_ref[:, hs] = _dot(dsb, kh).astype(BF16)
            dkv_ref[h] += _dot_tn(dsb, qh)
            dkv_ref[XH + h] += _dot_tn(p.astype(BF16), dob)

    call = _pcall(body, name=name, grid=(s // tq,), carry=carry,
                  in_specs=[pl.BlockSpec((tq, D), lambda i: (i, 0)), pl.BlockSpec((2 * XH, nm, XHD), lambda i: (0, 0, 0)),
                            pl.BlockSpec((tq, D), lambda i: (i, 0)), pl.BlockSpec((tq, D), lambda i: (i, 0)),
                            pl.BlockSpec((tq, 128), lambda i: (i, 0))],
                  out_specs=[pl.BlockSpec((tq, D), lambda i: (i, 0)), pl.BlockSpec((2 * XH, nm, XHD), lambda i: (0, 0, 0))],
                  out_shape=[_sds((s, D), BF16), _sds((2 * XH, nm, XHD), F32)])
    return _carried(call, (q, kv, o, do, lse), carry)


def _ffn_down_bwd(dxb, wd4, gu4, name, tm=512, carry=None, behind=None):
    s, _ = dxb.shape
    tm = min(tm, s)

    def body(dx_ref, w_hbm, gu_ref, o_ref, w_ref, w_sem):
        _load_once(w_hbm, w_ref, w_sem)
        for p in range(4):
            for rows in _row_chunks(tm):
                da = _dot_nt(dx_ref[rows, :], w_ref[p])
                g = gu_ref[0, p, rows, :].astype(F32)
                u = gu_ref[1, p, rows, :].astype(F32)
                sg = jax.nn.sigmoid(g)
                t = da * sg
                o_ref[0, p, rows, :] = (t * u * (1.0 + g - g * sg)).astype(BF16)
                o_ref[1, p, rows, :] = (t * g).astype(BF16)

    block = pl.BlockSpec((2, 4, tm, FS), lambda i: (0, 0, i, 0))
    call = _pcall(body, name=name, grid=(s // tm,), carry=carry, behind=behind,
                  in_specs=[pl.BlockSpec((tm, D), lambda i: (i, 0)), HBM_SPEC, block],
                  out_specs=block, out_shape=_sds((2, 4, s, FS), BF16), scratch=_resident(wd4))
    return _carried(call, (dxb, wd4, gu4), carry)


def _mm_tn(a, b, name, scale=1.0, carry=None):
    pa_n, s, m = a.shape
    pb_n, _, n = b.shape
    po = max(pa_n, pb_n)
    tk = 1024
    if po == 1 and s > tk and s % tk == 0:
        def body_k(a_ref, b_ref, o_ref, acc_ref):
            k = pl.program_id(0)
            part = _dot_tn(a_ref[...], b_ref[...])

            @pl.when(k == 0)
            def _():
                acc_ref[...] = part

            @pl.when(k > 0)
            def _():
                acc_ref[...] += part

            @pl.when(k == s // tk - 1)
            def _():
                o_ref[...] = (scale * acc_ref[...]).astype(BF16)

        call = _pcall(body_k, name=name, grid=(s // tk,), carry=carry,
                      in_specs=[pl.BlockSpec((None, tk, m), lambda k: (0, k, 0)),
                                pl.BlockSpec((None, tk, n), lambda k: (0, k, 0))],
                      out_specs=pl.BlockSpec((None, m, n), lambda k: (0, 0, 0)),
                      out_shape=_sds((1, m, n), BF16), scratch=[pltpu.VMEM((m, n), F32)])
        return _carried(call, (a, b), carry)
    tn = n if po >= 4 else min(n, 256)

    def body(a_ref, b_ref, o_ref):
        o_ref[...] = (scale * _dot_tn(a_ref[...], b_ref[...])).astype(BF16)

    call = _pcall(body, name=name, grid=(po, n // tn), carry=carry,
                  in_specs=[pl.BlockSpec((None, s, m), lambda o, j: (o if pa_n > 1 else 0, 0, 0)),
                            pl.BlockSpec((None, s, tn), lambda o, j: (o if pb_n > 1 else 0, 0, j))],
                  out_specs=pl.BlockSpec((None, m, tn), lambda o, j: (o, 0, j)),
                  out_shape=_sds((po, m, n), BF16))
    return _carried(call, (a, b), carry)


def _mm_tn_rows(a, b, name, total_rows, row0, begun=None, tm=512, carry=None):
    p, s, m = a.shape
    n = b.shape[1]
    tm = min(tm, m)
    tiles = m // tm
    assert row0 % tm == 0 and m % tm == 0, (row0, m, tm)

    def body(a_ref, b_ref, *rest):
        rest[-1][...] = _dot_tn(a_ref[...], b_ref[...]).astype(BF16)

    in_specs = [pl.BlockSpec((None, s, tm), lambda o, i: (o, 0, i)), pl.BlockSpec((s, n), lambda o, i: (0, 0))]
    call = _pcall(body, name=name, grid=(p, tiles), in_specs=in_specs + ([HBM_SPEC] if begun is not None else []),
                  out_specs=pl.BlockSpec((tm, n), lambda o, i: (row0 // tm + o * tiles + i, 0)),
                  out_shape=_sds((total_rows, n), BF16), aliases={2: 0} if begun is not None else None, carry=carry)
    return _carried(call, (a, b, begun) if begun is not None else (a, b), carry)


def _sum_dots(a_ref, b_ref, nj, bt, rows=slice(None)):
    dot = _dot_nt if bt else _dot
    acc = dot(a_ref[0, rows, :], b_ref[0])
    for j in range(1, nj):
        acc = acc + dot(a_ref[j, rows, :], b_ref[j])
    return acc


def _mm_acc(a, b, name, out_dtype, tm=1024, bt=False, carry=None):
    nj, s, k = a.shape
    n = b.shape[1] if bt else b.shape[2]
    tm = min(tm, s)

    def body(a_ref, b_ref, o_ref):
        o_ref[...] = _sum_dots(a_ref, b_ref, nj, bt).astype(out_dtype)

    call = _pcall(body, name=name, grid=(s // tm,), carry=carry,
                  in_specs=[pl.BlockSpec((nj, tm, k), lambda i: (0, i, 0)),
                            pl.BlockSpec(b.shape, lambda i: (0, 0, 0))],
                  out_specs=pl.BlockSpec((tm, n), lambda i: (i, 0)), out_shape=_sds((s, n), out_dtype))
    return _carried(call, (a, b), carry)


def _rms_bwd_call(name, acts, weights, scratch, load, dh_rows, *, x, gain, dres, tm, carry, behind=None):
    s, n = x.shape
    tm = min(tm, s)
    n_act, n_w = len(acts), len(weights)

    def body(*refs):
        act_refs, w_refs = refs[:n_act], refs[n_act:n_act + n_w]
        x_ref, g_ref, r_ref, dx_ref, dxb_ref, dg_ref = refs[n_act + n_w:n_act + n_w + 6]
        held = refs[n_act + n_w + 6:]
        load(w_refs, held)

        @pl.when(pl.program_id(0) == 0)
        def _():
            dg_ref[...] = jnp.zeros_like(dg_ref)

        for rows in _row_chunks(tm):
            dh = dh_rows(act_refs, held, rows)
            xv = x_ref[rows, :]
            r = lax.rsqrt(jnp.mean(xv * xv, axis=-1, keepdims=True) + EPS)
            xh = xv * r
            dyg = dh * g_ref[...]
            dx = r_ref[rows, :] + r * (dyg - xh * jnp.mean(dyg * xh, axis=-1, keepdims=True))
            dx_ref[rows, :] = dx
            dxb_ref[rows, :] = dx.astype(BF16)
            dg_ref[...] += jnp.sum(dh * xh, axis=0, keepdims=True)

    def tile(a):
        return (pl.BlockSpec((tm, a.shape[1]), lambda i: (i, 0)) if a.ndim == 2
                else pl.BlockSpec((a.shape[0], tm, a.shape[2]), lambda i: (0, i, 0)))

    row = pl.BlockSpec((tm, n), lambda i: (i, 0))
    in_specs = [tile(a) for a in acts] + [HBM_SPEC] * n_w + [row, pl.BlockSpec((1, n), lambda i: (0, 0)), row]
    call = _pcall(body, name=name, grid=(s // tm,), in_specs=in_specs, carry=carry, behind=behind,
                  out_specs=[row, row, pl.BlockSpec((1, n), lambda i: (0, 0))],
                  out_shape=[_sds((s, n), F32), _sds((s, n), BF16), _sds((1, n), F32)], scratch=scratch)
    return _carried(call, tuple(acts) + tuple(weights) + (x, gain, dres), carry)


def _mm_acc_rms_bwd(a, b, name, *, x, gain, dres, scale=None, tm=512, bt=False, carry=None, behind=None):
    def load(w_refs, held):
        _load_once(w_refs[0], held[0], held[1])

    def dh_rows(act_refs, held, rows):
        dh = _sum_dots(act_refs[0], held[0], a.shape[0], bt, rows)
        return dh if scale is None else scale * dh

    return _rms_bwd_call(name, [a], [b], _resident(b), load, dh_rows, x=x, gain=gain, dres=dres, tm=tm, carry=carry,
                         behind=behind)


def _in_proj_bwd(dpa, dq, dkv, w_in_t, name, *, x, gain, dres, tm=512, carry=None, behind=None):
    def load(w_refs, held):
        _load_in_proj(w_refs[0], *held)

    def dh_rows(act_refs, held, rows):
        dpa_ref, dq_ref, dkv_ref = act_refs
        wq_ref, wkv_ref, wa_ref, _ = held
        dh = _dot(dq_ref[rows, :], wq_ref[...]) + _dot(dkv_ref[rows, :], wkv_ref[...])
        return dh + _sum_dots(dpa_ref, wa_ref, N_SEG, False, rows)

    return _rms_bwd_call(name, [dpa, dq, dkv], [w_in_t], IN_PROJ_WEIGHTS, load, dh_rows, x=x, gain=gain, dres=dres,
                         tm=tm, carry=carry, behind=behind)


def _adam(w, g, m, v):
    m2 = ADAM_B1 * m + (1.0 - ADAM_B1) * g
    v2 = ADAM_B2 * v + (1.0 - ADAM_B2) * (g * g)
    m_hat = m2 / (1.0 - ADAM_B1 ** ADAM_STEP)
    v_hat = v2 / (1.0 - ADAM_B2 ** ADAM_STEP)
    delta = -ADAM_LR * (m_hat / (jnp.sqrt(v_hat) + ADAM_EPS) + ADAM_WD * w)
    return delta, m2, v2


def _adamw(parts, w, m, v, name, behind=None):
    _, r, c = parts.shape
    tr = max(t for t in range(16, 257, 16) if r % t == 0)

    def body(p_ref, w_ref, m_ref, v_ref, g_ref, d_ref, m2_ref, v2_ref):
        g = p_ref[0].astype(F32)
        for i in range(1, N_DEV):
            g = g + p_ref[i].astype(F32)
        delta, m2, v2 = _adam(w_ref[...], g, m_ref[...], v_ref[...])
        g_ref[...] = g
        d_ref[...] = delta
        m2_ref[...] = m2
        v2_ref[...] = v2

    blk = pl.BlockSpec((tr, c), lambda i: (i, 0))
    return _pcall(body, name=name, grid=(r // tr,), behind=behind,
                  in_specs=[pl.BlockSpec((N_DEV, tr, c), lambda i: (0, i, 0)), blk, blk, blk],
                  out_specs=[blk] * 4, out_shape=[_sds((r, c), F32)] * 4)(parts, w, m, v)


def _position():
    return lax.axis_index("x"), lax.axis_index("y"), lax.axis_index("c")


def _slot(px, py, pc):
    return 4 * px + 2 * py + pc


def _row_window(ref, rows):
    r0, r1 = rows
    return ref if (r0, r1) == (0, ref.shape[0]) else ref.at[pl.ds(r0, r1 - r0)]


def _split_items(items):
    sources = [src for src, _, _ in items]
    begun = [(a, dest) for a, (_, _, dest) in enumerate(items) if dest is not None]
    aliases = {len(sources) + k: a for k, (a, _) in enumerate(begun)}
    return sources + [dest for _, dest in begun], [rows for _, rows, _ in items], aliases


def _gather_carry(items):
    na = len(items)
    carry_ins, windows, aliases = _split_items(items)

    def plan(ins, outs, sems):
        send_sems, recv_sems, local_sems = sems
        x, y, c = _position()
        me, sibling = (x, y, c), (x, y, 1 - c)
        chips = [(1 - x, y), (x, 1 - y), (1 - x, 1 - y)]
        ins = [_row_window(ins[a], windows[a]) for a in range(na)]

        def block_rows(a, block):
            return _row_window(outs[a].at[_slot(*block)], windows[a])

        def copy(a, k, block, to, src=None):
            rows = block_rows(a, block)
            return pltpu.make_async_remote_copy(src_ref=rows if src is None else src, dst_ref=rows,
                                                send_sem=send_sems.at[k, a], recv_sem=recv_sems.at[k, a],
                                                device_id=to, device_id_type=MESH)

        mine = [pltpu.make_async_copy(ins[a], block_rows(a, me), local_sems.at[a]) for a in range(na)]
        first = [copy(a, 0, me, sibling, src=ins[a]) for a in range(na)]
        for j, chip in enumerate(chips):
            first += [copy(a, 1 + j, me, (*chip, c), src=ins[a]) for a in range(na)]
        landed = [[copy(a, 1 + j, (*chip, c), me) for a in range(na)] for j, chip in enumerate(chips)]
        passed = [[copy(a, 4 + j, (*chip, c), sibling) for a in range(na)] for j, chip in enumerate(chips)]
        from_sibling = [copy(a, 0, sibling, me) for a in range(na)]
        for j, chip in enumerate(chips):
            from_sibling += [copy(a, 4 + j, (*chip, 1 - c), me) for a in range(na)]
        return mine, first, landed, passed, from_sibling

    def start(ins, outs, sems):
        mine, first, _, _, _ = plan(ins, outs, sems)
        for cp in mine + first:
            cp.start()

    def mid(ins, outs, sems):
        _, _, landed, passed, _ = plan(ins, outs, sems)
        for over_ici, onward in zip(landed, passed):
            for cp, fwd in zip(over_ici, onward):
                cp.wait_recv()
                fwd.start()

    def finish(ins, outs, sems):
        mine, first, _, passed, from_sibling = plan(ins, outs, sems)
        for cp in from_sibling:
            cp.wait_recv()
        for cp in first + [fwd for onward in passed for fwd in onward]:
            cp.wait_send()
        for cp in mine:
            cp.wait()

    return _Carry(carry_ins, [_sds((N_DEV,) + src.shape, src.dtype) for src, _, _ in items],
                  [pltpu.SemaphoreType.DMA((7, na)), pltpu.SemaphoreType.DMA((7, na)),
                   pltpu.SemaphoreType.DMA((na,))], start, finish, mid, aliases)


def _exchange_carry(scattered, replicated=()):
    items = list(scattered) + [(a, (0, a.shape[0]), None) for a in replicated]
    na, ns = len(items), len(scattered)
    carry_ins, windows, aliases = _split_items(items)

    def plan(ins, outs, sems):
        send_sems, recv_sems, local_sems = sems
        me = _slot(*_position())

        def source(a, j):
            return _row_window(ins[a].at[j] if a < ns else ins[a], windows[a])

        def copy(a, j, i):
            return pltpu.make_async_remote_copy(src_ref=source(a, j), dst_ref=_row_window(outs[a].at[i], windows[a]),
                                                send_sem=send_sems.at[j, a], recv_sem=recv_sems.at[i, a],
                                                device_id=(j >> 2, (j >> 1) & 1, j & 1), device_id_type=MESH)

        def own(a, j):
            return pltpu.make_async_copy(source(a, j), _row_window(outs[a].at[j], windows[a]), local_sems.at[a])

        return me, copy, own

    def start(ins, outs, sems):
        me, copy, own = plan(ins, outs, sems)
        for a in range(na):
            for j in range(N_DEV):
                @pl.when(me == j)
                def _():
                    own(a, j).start()

                @pl.when(me != j)
                def _():
                    copy(a, j, me).start()

    def finish(ins, outs, sems):
        me, copy, own = plan(ins, outs, sems)
        for a in range(na):
            for j in range(N_DEV):
                @pl.when(me == j)
                def _():
                    for i in range(N_DEV):
                        if i != j:
                            copy(a, j, i).wait_recv()
                    own(a, j).wait()

                @pl.when(me != j)
                def _():
                    copy(a, j, me).wait_send()

    return _Carry(carry_ins, [_sds((N_DEV,) + src.shape[-2:], src.dtype) for src, _, _ in items],
                  [pltpu.SemaphoreType.DMA((N_DEV, na)), pltpu.SemaphoreType.DMA((N_DEV, na)),
                   pltpu.SemaphoreType.DMA((na,))], start, finish, None, aliases)


HBM_ARRAY = pl.BlockSpec(memory_space=pltpu.HBM)
SEMAPHORES = pl.BlockSpec(memory_space=pltpu.SEMAPHORE)
DATAFLOW = pltpu.SideEffectType.DATAFLOW_SIDE_EFFECTING


def _exchange_copy(parts_ref, land_ref, send_sems, recv_sems, me, j):
    return pltpu.make_async_remote_copy(src_ref=parts_ref.at[j], dst_ref=land_ref.at[me], send_sem=send_sems.at[j],
                                        recv_sem=recv_sems.at[me], device_id=(j >> 2, (j >> 1) & 1, j & 1),
                                        device_id_type=MESH)


def _exchange_start(parts, name):
    def body(parts_ref, land_ref, send_sems, recv_sems, parts_thru, land_thru, token):
        me = _slot(*_position())
        for j in range(N_DEV):
            @pl.when(me == j)
            def _():
                pltpu.make_async_copy(parts_ref.at[j], land_ref.at[j], send_sems.at[j]).start()

            @pl.when(me != j)
            def _():
                _exchange_copy(parts_ref, land_ref, send_sems, recv_sems, me, j).start()
        token[...] = jnp.zeros_like(token)

    return pl.pallas_call(
        body, name=name,
        out_shape=(pltpu.SemaphoreType.DMA((N_DEV,)), pltpu.SemaphoreType.DMA((N_DEV,)),
                   pltpu.HBM(parts.shape, parts.dtype), pltpu.HBM(parts.shape, parts.dtype), _sds((8, 128), F32)),
        in_specs=(HBM_ARRAY, HBM_ARRAY),
        out_specs=(SEMAPHORES, SEMAPHORES, HBM_ARRAY, HBM_ARRAY, pl.BlockSpec(memory_space=pltpu.VMEM)),
        input_output_aliases={0: 2, 1: 3}, compiler_params=pltpu.CompilerParams(has_side_effects=DATAFLOW),
    )(pltpu.with_memory_space_constraint(parts, pltpu.HBM),
      pltpu.with_memory_space_constraint(lax.empty(parts.shape, parts.dtype), pltpu.HBM))


def _exchange_wait(send_sems, recv_sems, parts_thru, land_thru, after, name):
    def body(parts_ref, land_ref, send_sems, recv_sems, *rest):
        me = _slot(*_position())
        for j in range(N_DEV):
            @pl.when(me == j)
            def _():
                pltpu.make_async_copy(parts_ref.at[j], land_ref.at[j], send_sems.at[j]).wait()

            @pl.when(me != j)
            def _():
                both = pltpu.make_async_remote_copy(src_ref=parts_ref.at[j], dst_ref=land_ref.at[j],
                                                    send_sem=send_sems.at[j], recv_sem=recv_sems.at[j],
                                                    device_id=(j >> 2, (j >> 1) & 1, j & 1), device_id_type=MESH)
                both.wait_send()
                both.wait_recv()

    return pl.pallas_call(
        body, name=name, out_shape=(pltpu.HBM(parts_thru.shape, parts_thru.dtype),
                                    pltpu.HBM(parts_thru.shape, parts_thru.dtype)),
        in_specs=(HBM_ARRAY, HBM_ARRAY, SEMAPHORES, SEMAPHORES) + (pl.BlockSpec(memory_space=pl.ANY),) * len(after),
        out_specs=(HBM_ARRAY, HBM_ARRAY), input_output_aliases={0: 0, 1: 1},
        compiler_params=pltpu.CompilerParams(has_side_effects=DATAFLOW),
    )(parts_thru, land_thru, send_sems, recv_sems, *after)[1]


class _Mesh:
    def __init__(self, shards):
        self.shards, self.full, self.received, self.cache, self.pending, self.tokens = shards, {}, {}, {}, {}, {}

    def fetch(self, wanted):
        items = []
        for want in wanted:
            name, r0, r1 = want if isinstance(want, tuple) else (want, 0, self.shards[want].shape[0])
            items.append((self.shards[name], (r0, r1), self.full.get(name)))
        return _gather_carry(items)

    def fetched(self, wanted, results):
        self.full.update(zip([want[0] if isinstance(want, tuple) else want for want in wanted], results))

    def send(self, *payloads):
        return _exchange_carry([(parts, rows or (0, parts.shape[1]), self.received.get(name))
                                for name, parts, rows in payloads])

    def sent(self, names, results):
        self.received.update(zip(names, results))

    def send_apart(self, name, parts):
        *self.pending[name], self.tokens[name] = _exchange_start(parts, "exchange_" + name + "_start")
        return self.tokens[name]

    def sent_apart(self, name, after):
        self.received[name] = _exchange_wait(*self.pending.pop(name), after, "exchange_" + name + "_wait")

    def w(self, key):
        if key not in self.cache:
            self.cache[key] = self._layout(key)
        return self.cache[key]

    def _layout(self, key):
        if key in ("gu1", "gu2"):
            return self.full[key]
        if key in ("d1", "d2"):
            return self.full[key].reshape(4, FS, D)
        if key in ("out", "q", "o"):
            return self.full[key].reshape(D, D)
        if key == "kv":
            return self.full["kv"]
        if key == "convw":
            rows = self.full["conv"][:, :3, :].transpose(1, 0, 2).reshape(3, D)
            return jnp.concatenate([rows, jnp.zeros((5, D), F32)], axis=0)
        assert key == "win_t", key
        return self.full["win"].reshape(-1, D)


def _forward_backward(x, mem, target, g, rel_bias, sinks, ex):
    s = x.shape[0]
    def fetching(wanted, call, *args, **kw):
        res, got = call(*args, carry=ex.fetch(wanted), **kw)
        ex.fetched(wanted, got)
        return res

    h1 = fetching(["gu1", "conv"], _rmsnorm, x, g["ffn1"], "norm_ffn1")
    gu1, a1 = fetching(["d1", ("win", 0, 400)], _ffn_up, h1, ex.w("gu1").reshape(2, 4, FS, D), "ffn1_up")
    x1, h2 = fetching([("win", 400, 832)], _mm_res_norm, a1, ex.w("d1"), x, g["mix"], 0.5, "ffn1_down")
    pa, q, kv = fetching(["gu2"], _in_proj, h2, ex.w("win_t"), "in_proj")
    biasm = _bias_build(rel_bias, "bias_build")
    attn, lse = fetching(["out", "kv", "o"], _swa_fwd, q, kv, biasm, sinks, "swa_fwd")
    merged = fetching(["q"], _conv_merge_fwd, pa, attn, ex.w("convw"), "conv_merge_fwd")
    (x2, h3), _ = _mm_res_norm(merged[None], ex.w("out")[None], x1, g["xattn"], 1.0, "out_proj")
    q2 = _mm_nn(h3, ex.w("q")[None], "xattn_q")[0][0]
    mh, _ = _rmsnorm(mem, g["mem"], "norm_mem")
    kv2 = _mm_nn(mh, ex.w("kv"), "xattn_kv")[0]
    o, lse2 = _xattn_fwd(q2, kv2, "xattn_fwd")
    (x3, h4), _ = _mm_res_norm(o[None], ex.w("o")[None], x2, g["ffn2"], 1.0, "xattn_o")
    gu2, a2 = fetching(["d2"], _ffn_up, h4, ex.w("gu2").reshape(2, 4, FS, D), "ffn2_up")
    dx4, dx4b, loss, d_final = _ffn_down_loss(a2, ex.w("d2"), x3, g["final"], target, "ffn2_down_loss")
    def sending(payloads, call, *args, **kw):
        res, got = call(*args, carry=ex.send(*payloads), **kw)
        ex.sent([name for name, _, _ in payloads], got)
        return res

    dw_d2 = _mm_tn(a2, dx4b[None], "dw_ffn2_down", scale=0.5)[0].reshape(N_DEV, -1, D)
    dgu2 = sending([("d2", dw_d2, (0, 288))], _ffn_down_bwd, dx4b, ex.w("d2"), gu2, "ffn2_down_bwd").reshape(8, s, FS)
    dw_gu2 = sending([("d2", dw_d2, (288, 352))], _mm_tn, dgu2, h4[None], "dw_ffn2_up", scale=0.5)
    dx3, dx3b, d_ffn2 = sending([("gu2", dw_gu2, (0, 368))], _mm_acc_rms_bwd, dgu2, ex.w("gu2"), "ffn2_up_bwd",
                                x=x3, gain=g["ffn2"], dres=dx4, scale=0.5)
    do, _ = _mm_acc(dx3b[None], ex.w("o")[None], "xattn_o_bwd", BF16, bt=True)
    dw_o = _mm_tn(o[None], dx3b[None], "dw_xattn_o")[0].reshape(N_DEV, -1, D)
    (dq2, dkv2), _ = _xattn_bwd(q2, kv2, o, do, lse2, "xattn_bwd")
    dkv2b = dkv2.astype(BF16)
    dw_q = _mm_tn(h3[None], dq2[None], "dw_xattn_q")[0].reshape(N_DEV, -1, D)
    (dx2, dx2b, d_xattn), _ = _mm_acc_rms_bwd(dq2[None], ex.w("q")[None], "xattn_q_bwd", x=x2, gain=g["xattn"],
                                              dres=dx3, bt=True)
    dw_kv = _mm_tn(mh[None], dkv2b, "dw_xattn_kv")[0]
    (_, _, d_mem), _ = _mm_acc_rms_bwd(dkv2b, ex.w("kv"), "xattn_kv_bwd", x=mem, gain=g["mem"],
                                       dres=jnp.zeros_like(mem), bt=True)
    dmerged, _ = _mm_acc(dx2b[None], ex.w("out")[None], "out_proj_bwd", BF16, bt=True)
    dw_out = _mm_tn(merged[None], dx2b[None], "dw_out_proj")[0].reshape(N_DEV, -1, D)
    dattn, dpa, d_convw = sending([("kv", dw_kv, None)], _conv_merge_bwd,
                                  dmerged, pa, attn, ex.w("convw"), "conv_merge_bwd")
    dq, dkv, dbias, d_sinks = sending([("gu2", dw_gu2, (368, FS)), ("out", dw_out, None)], _swa_bwd,
                                      q, kv, attn, dattn, lse, biasm, sinks, "swa_bwd")
    d_relb = _bias_bwd(dbias, "bias_bwd")
    w_rows = ex.w("win_t").shape[0]
    dw_in = sending([("o", dw_o, None), ("q", dw_q, None)], _mm_tn_rows, dpa, h2, "dw_in_proj_a", w_rows, NQ + NKV)
    dw_in = _mm_tn_rows(dq[None], h2, "dw_in_proj_q", w_rows, 0, begun=dw_in)[0]
    dw_in = _mm_tn_rows(dkv[None], h2, "dw_in_proj_kv", w_rows, NQ, begun=dw_in)[0].reshape(N_DEV, -1, D)
    (dx1, dx1b, d_mix), _ = _in_proj_bwd(dpa, dq, dkv, ex.w("win_t"), "in_proj_bwd", x=x1, gain=g["mix"], dres=dx2,
                                         behind=ex.send_apart("win", dw_in))
    dw_d1 = _mm_tn(a1, dx1b[None], "dw_ffn1_down", scale=0.5)[0].reshape(N_DEV, -1, D)
    dgu1 = _ffn_down_bwd(dx1b, ex.w("d1"), gu1, "ffn1_down_bwd", behind=ex.send_apart("d1", dw_d1))[0]
    dgu1 = dgu1.reshape(8, s, FS)
    dw_gu1 = _mm_tn(dgu1, h1[None], "dw_ffn1_up", scale=0.5)[0]
    (dx0, _, d_ffn1), _ = _mm_acc_rms_bwd(dgu1, ex.w("gu1"), "ffn1_up_bwd", x=x, gain=g["ffn1"], dres=dx1,
                                          scale=0.5, behind=ex.send_apart("gu1", dw_gu1))

    relb_row = jnp.concatenate([d_relb[:, :REL_BUCKETS].T.reshape(1, REL_BUCKETS * N_HEADS), d_sinks[:, :N_HEADS],
                                jnp.zeros((1, D - REL_BUCKETS * N_HEADS - N_HEADS), F32)], axis=1)
    loss_row = jnp.concatenate([loss[0:1, 0:1], jnp.zeros((1, D - 1), F32)], axis=1)
    small = jnp.concatenate([d_ffn1, d_mix, d_xattn, d_mem, d_ffn2, d_final, relb_row, loss_row, d_convw[0:3],
                             jnp.zeros((SMALL_ROWS - ROW_CONV - 3, D), F32)], axis=0)
    return dx0, small


def _pack_small(norms, final, relb, sinks, conv_local, me):
    relb_row = jnp.concatenate([relb.reshape(1, -1), sinks.reshape(1, -1),
                                jnp.zeros((1, D - REL_BUCKETS * N_HEADS - N_HEADS), F32)], axis=1)
    conv_rows = lax.dynamic_update_slice(jnp.zeros((3, D), F32), conv_local.reshape(3, -1), (0, 128 * me))
    return jnp.concatenate(list(norms) + [final.reshape(1, D), relb_row, jnp.zeros((1, D), F32), conv_rows,
                                          jnp.zeros((SMALL_ROWS - ROW_CONV - 3, D), F32)], axis=0)


def kernel(x, mem, positions, rel_bias, ffn1_norm, ffn1_w_gu, ffn1_w_down, mix_norm, w_in, sinks, conv_w, w_out, xattn_norm, mem_norm, xattn_wq, xattn_wkv, xattn_wo, ffn2_norm, ffn2_w_gu, ffn2_w_down, final_norm, loss_target, m_rel_bias, m_ffn1_norm, m_ffn1_w_gu, m_ffn1_w_down, m_mix_norm, m_w_in, m_sinks, m_conv_w, m_w_out, m_xattn_norm, m_mem_norm, m_xattn_wq, m_xattn_wkv, m_xattn_wo, m_ffn2_norm, m_ffn2_w_gu, m_ffn2_w_down, m_final_norm, v_rel_bias, v_ffn1_norm, v_ffn1_w_gu, v_ffn1_w_down, v_mix_norm, v_w_in, v_sinks, v_conv_w, v_w_out, v_xattn_norm, v_mem_norm, v_xattn_wq, v_xattn_wkv, v_xattn_wo, v_ffn2_norm, v_ffn2_w_gu, v_ffn2_w_down, v_final_norm):
    del positions
    me = _slot(*_position())
    big = dict(gu1=(ffn1_w_gu, m_ffn1_w_gu, v_ffn1_w_gu), d1=(ffn1_w_down, m_ffn1_w_down, v_ffn1_w_down),
               win=(w_in, m_w_in, v_w_in), out=(w_out, m_w_out, v_w_out), q=(xattn_wq, m_xattn_wq, v_xattn_wq),
               kv=(xattn_wkv, m_xattn_wkv, v_xattn_wkv), o=(xattn_wo, m_xattn_wo, v_xattn_wo),
               gu2=(ffn2_w_gu, m_ffn2_w_gu, v_ffn2_w_gu), d2=(ffn2_w_down, m_ffn2_w_down, v_ffn2_w_down))
    order = list(big)
    transposed = ("gu1", "gu2", "win")
    local = {k: tuple(t[0].T if k in transposed else t[0] for t in big[k]) for k in order}
    shards = {k: local[k][0].astype(BF16) for k in order}
    shards["conv"] = jnp.concatenate([conv_w[0], jnp.zeros((5, 128), F32)], axis=0)
    ex = _Mesh(shards)
    gains = dict(ffn1=ffn1_norm, mix=mix_norm, xattn=xattn_norm, mem=mem_norm, ffn2=ffn2_norm,
                 final=final_norm.reshape(1, D))
    dx, small = _forward_backward(x[0], mem[0], loss_target[0], gains, rel_bias, sinks, ex)
    apart = ("d1", "win", "gu1")
    big_out = {k: _adamw(ex.received[k], *local[k], "adamw_" + k, behind=ex.tokens["gu1"])
               for k in order if k not in apart}
    for k in apart[:-1]:
        ex.sent_apart(k, after=[big_out[j][1] for j in big_out])
        big_out[k] = _adamw(ex.received[k], *local[k], "adamw_" + k)
    small_parts = _run_alone(_exchange_carry([], [small]), "exchange_small", after=[big_out[k][1] for k in big_out])[0]
    packed = [_pack_small(norms, final, relb, sk, conv, me) for norms, final, relb, sk, conv in (
        ((ffn1_norm, mix_norm, xattn_norm, mem_norm, ffn2_norm), final_norm, rel_bias, sinks, conv_w),
        ((m_ffn1_norm, m_mix_norm, m_xattn_norm, m_mem_norm, m_ffn2_norm), m_final_norm, m_rel_bias, m_sinks, m_conv_w),
        ((v_ffn1_norm, v_mix_norm, v_xattn_norm, v_mem_norm, v_ffn2_norm), v_final_norm, v_rel_bias, v_sinks, v_conv_w))]
    small_out = _adamw(small_parts, *packed, "adamw_small")
    ex.sent_apart("gu1", after=[dx, small_out[1]] + [big_out[k][1] for k in big_out])
    big_out["gu1"] = _adamw(ex.received["gu1"], *local["gu1"], "adamw_gu1")
    big_out = {k: [t.T if k in transposed else t for t in big_out[k]] for k in order}

    def unpack(t):
        conv = lax.dynamic_slice(t[ROW_CONV:ROW_CONV + 3], (0, 128 * me), (3, 128))[None]
        nrel = REL_BUCKETS * N_HEADS
        return dict(ffn1_norm=t[0:1], mix_norm=t[1:2], xattn_norm=t[2:3], mem_norm=t[3:4], ffn2_norm=t[4:5],
                    final_norm=t[5], rel_bias=t[ROW_RELB, :nrel].reshape(REL_BUCKETS, N_HEADS),
                    sinks=t[ROW_RELB:ROW_RELB + 1, nrel:nrel + N_HEADS], conv_w=conv)

    names = dict(gu1="ffn1_w_gu", d1="ffn1_w_down", win="w_in", out="w_out", q="xattn_wq", kv="xattn_wkv",
                 o="xattn_wo", gu2="ffn2_w_gu", d2="ffn2_w_down")
    results = []
    for idx in range(4):
        leaves = unpack(small_out[idx])
        leaves.update({names[k]: big_out[k][idx][None] for k in order})
        results.append(leaves)
    weights = ("rel_bias", "ffn1_norm", "ffn1_w_gu", "ffn1_w_down", "mix_norm", "w_in", "sinks", "conv_w", "w_out",
               "xattn_norm", "mem_norm", "xattn_wq", "xattn_wkv", "xattn_wo", "ffn2_norm", "ffn2_w_gu", "ffn2_w_down",
               "final_norm")
    loss = small_out[0][ROW_LOSS, 0]
    return (loss, dx[None], *[leaves[n] for leaves in results for n in weights])
```

```python
import math

import numpy as np
import jax
import jax.numpy as jnp
from jax import lax
from jax.experimental import pallas as pl
from jax.experimental.pallas import tpu as pltpu

F32, BF16 = jnp.float32, jnp.bfloat16
MESH = pl.DeviceIdType.MESH

D = 1024
N_DEV = 8
D_FF = 2816
FS = D_FF // 4
HEAD = 64
N_HEADS, N_KV = 16, 4
BLK = 128
NQ, NKV = N_HEADS * HEAD, 2 * N_KV * HEAD
XH, XHD = 4, 256
REL_BUCKETS, REL_EXACT, REL_MAX_DIST = 32, 16, 128
EPS, NEG = 1e-6, -1e30
ADAM_LR, ADAM_B1, ADAM_B2, ADAM_EPS, ADAM_WD, ADAM_STEP = 0.001, 0.9, 0.999, 1e-08, 0.01, 10
VMEM_LIMIT_V7X = 56 * 2**20
SMALL_ROWS = 16
ROW_RELB, ROW_LOSS, ROW_CONV = 6, 7, 8


def _bucket_thresholds():
    n = np.arange(REL_MAX_DIST)
    nf = np.maximum(n, 1).astype(np.float32)
    large = REL_EXACT + (np.log(nf / np.float32(REL_EXACT)) / np.float32(math.log(REL_MAX_DIST / REL_EXACT))
                         * np.float32(REL_BUCKETS - REL_EXACT)).astype(np.int32)
    b = np.where(n < REL_EXACT, n, np.minimum(large, REL_BUCKETS - 1))
    return [int(np.argmax(b >= REL_EXACT + k)) for k in range(1, REL_BUCKETS - REL_EXACT)]


BUCKET_THRESHOLDS = _bucket_thresholds()


HBM_SPEC = pl.BlockSpec(memory_space=pl.ANY)


class _Carry:
    def __init__(self, ins, outs, sems, start, finish, mid=None, aliases=None):
        self.ins, self.outs, self.sems = list(ins), list(outs), list(sems)
        self.start, self.finish, self.mid, self.aliases = start, finish, mid, dict(aliases or {})


def _pcall(body, *, name, grid, in_specs, out_specs, out_shape, scratch=(), carry=None, aliases=None, behind=None):
    params = pltpu.CompilerParams(dimension_semantics=("arbitrary",) * len(grid), vmem_limit_bytes=VMEM_LIMIT_V7X)
    if carry is None and behind is not None:
        n_in = len(in_specs)
        call = pl.pallas_call(lambda *refs: body(*refs[:n_in], *refs[n_in + 1:]), name=name, grid=grid,
                              in_specs=list(in_specs) + [pl.BlockSpec((8, 128), lambda *_: (0, 0))],
                              out_specs=out_specs, out_shape=out_shape, scratch_shapes=list(scratch),
                              compiler_params=params, input_output_aliases=aliases or {})
        return lambda *args: call(*args, behind)
    if carry is None:
        return pl.pallas_call(body, name=name, grid=grid, in_specs=in_specs, out_specs=out_specs,
                              out_shape=out_shape, scratch_shapes=list(scratch), compiler_params=params,
                              input_output_aliases=aliases or {})
    assert aliases is None and behind is None, name
    single = not isinstance(out_shape, (list, tuple))
    own_specs, own_shapes = ([out_specs], [out_shape]) if single else (list(out_specs), list(out_shape))
    n_in, n_out, n_scr = len(in_specs), len(own_shapes), len(scratch)
    n_cin, n_cout = len(carry.ins), len(carry.outs)
    steps = math.prod(grid)
    mid_step = max(steps - 1 - max(steps // 8, 1), 0)

    def carrying(*refs):
        ins, refs = refs[:n_in], refs[n_in:]
        cins, refs = refs[:n_cin], refs[n_cin:]
        outs, refs = refs[:n_out], refs[n_out:]
        couts, refs = refs[:n_cout], refs[n_cout:]
        scr, csems = refs[:n_scr], refs[n_scr:]
        step = 0
        for axis, size in enumerate(grid):
            step = step * size + pl.program_id(axis)

        @pl.when(step == 0)
        def _():
            carry.start(cins, couts, csems)

        body(*ins, *outs, *scr)
        if carry.mid is not None:
            @pl.when(step == mid_step)
            def _():
                carry.mid(cins, couts, csems)

        @pl.when(step == steps - 1)
        def _():
            carry.finish(cins, couts, csems)

    call = pl.pallas_call(carrying, name=name, grid=grid, in_specs=list(in_specs) + [HBM_SPEC] * n_cin,
                          out_specs=own_specs + [HBM_SPEC] * n_cout, out_shape=own_shapes + carry.outs,
                          scratch_shapes=list(scratch) + carry.sems, compiler_params=params,
                          input_output_aliases={n_in + i: n_out + o for i, o in carry.aliases.items()})

    def run(*args):
        res = call(*args, *carry.ins)
        return (res[0] if single else res[:n_out]), res[n_out:]

    return run


def _run_alone(carry, name, after=()):
    n_cin, n_cout, n_after = len(carry.ins), len(carry.outs), len(after)

    def body(*refs):
        cins, refs = refs[:n_cin], refs[n_cin + n_after:]
        couts, csems = refs[:n_cout], refs[n_cout:]
        carry.start(cins, couts, csems)
        if carry.mid is not None:
            carry.mid(cins, couts, csems)
        carry.finish(cins, couts, csems)

    return pl.pallas_call(body, name=name, in_specs=[HBM_SPEC] * (n_cin + n_after), out_specs=[HBM_SPEC] * n_cout,
                          out_shape=carry.outs, scratch_shapes=carry.sems,
                          input_output_aliases=carry.aliases)(*carry.ins, *after)


def _dot(a, b):
    return jnp.dot(a, b, preferred_element_type=F32)


def _dot_nt(a, b):
    return lax.dot_general(a, b, (((1,), (1,)), ((), ())), preferred_element_type=F32)


def _dot_tn(a, b):
    return lax.dot_general(a, b, (((0,), (0,)), ((), ())), preferred_element_type=F32)


def _sds(shape, dtype):
    return jax.ShapeDtypeStruct(tuple(shape), dtype)


ROW_CHUNK = 256


def _row_chunks(tm):
    return [slice(r, min(r + ROW_CHUNK, tm)) for r in range(0, tm, ROW_CHUNK)]


def _carried(call, args, carry):
    return call(*args) if carry is not None else (call(*args), ())


def _rmsnorm(x, g, name, carry=None):
    m, d = x.shape
    tm = min(512, m)

    def body(x_ref, g_ref, h_ref):
        xv = x_ref[...]
        r = lax.rsqrt(jnp.mean(xv * xv, axis=-1, keepdims=True) + EPS)
        h_ref[...] = (xv * r * g_ref[...]).astype(BF16)

    call = _pcall(body, name=name, grid=(m // tm,), carry=carry,
                  in_specs=[pl.BlockSpec((tm, d), lambda i: (i, 0)), pl.BlockSpec((1, d), lambda i: (0, 0))],
                  out_specs=pl.BlockSpec((tm, d), lambda i: (i, 0)), out_shape=_sds((m, d), BF16))
    return _carried(call, (x, g), carry)


def _mm_nn(a, b, name, tm=1024, bt=False, carry=None):
    m, k = a.shape
    nj = b.shape[0]
    n = b.shape[1] if bt else b.shape[2]
    tm = min(tm, m)
    dot = _dot_nt if bt else _dot

    def body(a_ref, b_ref, o_ref):
        o_ref[...] = dot(a_ref[...], b_ref[...]).astype(BF16)

    call = _pcall(body, name=name, grid=(nj, m // tm),
                  in_specs=[pl.BlockSpec((tm, k), lambda j, i: (i, 0)),
                            pl.BlockSpec((None,) + b.shape[1:], lambda j, i: (j, 0, 0))],
                  out_specs=pl.BlockSpec((None, tm, n), lambda j, i: (j, i, 0)),
                  out_shape=_sds((nj, m, n), BF16), carry=carry)
    return _carried(call, (a, b), carry)


def _norm_mm(x, g, w, name):
    m, d = x.shape
    nj, _, n = w.shape

    def body(x_ref, g_ref, w_ref, h_ref, o_ref):
        xv = x_ref[...]
        r = lax.rsqrt(jnp.mean(xv * xv, axis=-1, keepdims=True) + EPS)
        h = (xv * r * g_ref[...]).astype(BF16)
        h_ref[...] = h
        for j in range(nj):
            o_ref[j] = _dot(h, w_ref[j]).astype(BF16)

    return _pcall(body, name=name, grid=(1,),
                  in_specs=[pl.BlockSpec((m, d), lambda i: (0, 0)), pl.BlockSpec((1, d), lambda i: (0, 0)),
                            pl.BlockSpec(w.shape, lambda i: (0, 0, 0))],
                  out_specs=[pl.BlockSpec((m, d), lambda i: (0, 0)), pl.BlockSpec((nj, m, n), lambda i: (0, 0, 0))],
                  out_shape=[_sds((m, d), BF16), _sds((nj, m, n), BF16)])(x, g, w)


def _load_once(src_hbm, dst_vmem, sem):
    @pl.when(pl.program_id(0) == 0)
    def _():
        load = pltpu.make_async_copy(src_hbm, dst_vmem, sem)
        load.start()
        load.wait()


def _resident(w):
    return [pltpu.VMEM(w.shape, w.dtype), pltpu.SemaphoreType.DMA(())]


def _ffn_up(h, w4, name, tm=512, carry=None):
    s, d = h.shape
    tm = min(tm, s)

    def body(h_ref, w_hbm, gu_ref, a_ref, w_ref, w_sem):
        _load_once(w_hbm, w_ref, w_sem)
        for p in range(4):
            for rows in _row_chunks(tm):
                hv = h_ref[rows, :]
                g = _dot_nt(hv, w_ref[0, p])
                u = _dot_nt(hv, w_ref[1, p])
                gu_ref[0, p, rows, :] = g.astype(BF16)
                gu_ref[1, p, rows, :] = u.astype(BF16)
                a_ref[p, rows, :] = (g * jax.nn.sigmoid(g) * u).astype(BF16)

    call = _pcall(body, name=name, grid=(s // tm,),
                  in_specs=[pl.BlockSpec((tm, d), lambda i: (i, 0)), HBM_SPEC],
                  out_specs=[pl.BlockSpec((2, 4, tm, FS), lambda i: (0, 0, i, 0)),
                             pl.BlockSpec((4, tm, FS), lambda i: (0, i, 0))],
                  out_shape=[_sds((2, 4, s, FS), BF16), _sds((4, s, FS), BF16)], scratch=_resident(w4), carry=carry)
    return _carried(call, (h, w4), carry)


N_SEG = 5
IN_PROJ_WEIGHTS = [pltpu.VMEM((NQ, D), BF16), pltpu.VMEM((NKV, D), BF16), pltpu.VMEM((N_SEG, D, D), BF16),
                   pltpu.SemaphoreType.DMA((2 + N_SEG,))]


def _load_in_proj(w_hbm, wq_ref, wkv_ref, wa_ref, sems):
    @pl.when(pl.program_id(0) == 0)
    def _():
        loads = [pltpu.make_async_copy(w_hbm.at[pl.ds(0, NQ)], wq_ref, sems.at[0]),
                 pltpu.make_async_copy(w_hbm.at[pl.ds(NQ, NKV)], wkv_ref, sems.at[1])]
        loads += [pltpu.make_async_copy(w_hbm.at[pl.ds(NQ + NKV + D * j, D)], wa_ref.at[j], sems.at[2 + j])
                  for j in range(N_SEG)]
        for load in loads:
            load.start()
        for load in loads:
            load.wait()


def _in_proj(h, w_in_t, name, tm=512, carry=None):
    s, d = h.shape
    tm = min(tm, s)

    def body(h_ref, w_hbm, pa_ref, q_ref, kv_ref, wq_ref, wkv_ref, wa_ref, sems):
        _load_in_proj(w_hbm, wq_ref, wkv_ref, wa_ref, sems)
        hv = h_ref[...]
        q_ref[...] = _dot_nt(hv, wq_ref[...]).astype(BF16)
        kv_ref[...] = _dot_nt(hv, wkv_ref[...]).astype(BF16)
        for j in range(N_SEG):
            pa_ref[j] = _dot_nt(hv, wa_ref[j]).astype(BF16)

    call = _pcall(body, name=name, grid=(s // tm,), carry=carry,
                  in_specs=[pl.BlockSpec((tm, d), lambda i: (i, 0)), HBM_SPEC],
                  out_specs=[pl.BlockSpec((N_SEG, tm, d), lambda i: (0, i, 0)),
                             pl.BlockSpec((tm, NQ), lambda i: (i, 0)), pl.BlockSpec((tm, NKV), lambda i: (i, 0))],
                  out_shape=[_sds((N_SEG, s, d), BF16), _sds((s, NQ), BF16), _sds((s, NKV), BF16)],
                  scratch=IN_PROJ_WEIGHTS)
    return _carried(call, (h, w_in_t), carry)


def _mm_res_norm(a, w, xres, gain, scale, name, tm=512, carry=None):
    npart, s, kp = a.shape
    tm = min(tm, s)

    def body(a_ref, w_ref, x_ref, g_ref, xo_ref, h_ref):
        for rows in _row_chunks(tm):
            acc = _dot(a_ref[0, rows, :], w_ref[0])
            for p in range(1, npart):
                acc = acc + _dot(a_ref[p, rows, :], w_ref[p])
            xn = x_ref[rows, :] + scale * acc
            xo_ref[rows, :] = xn
            r = lax.rsqrt(jnp.mean(xn * xn, axis=-1, keepdims=True) + EPS)
            h_ref[rows, :] = (xn * r * g_ref[...]).astype(BF16)

    call = _pcall(body, name=name, grid=(s // tm,),
                  in_specs=[pl.BlockSpec((npart, tm, kp), lambda i: (0, i, 0)),
                            pl.BlockSpec((npart, kp, D), lambda i: (0, 0, 0)),
                            pl.BlockSpec((tm, D), lambda i: (i, 0)),
                            pl.BlockSpec((1, D), lambda i: (0, 0))],
                  out_specs=[pl.BlockSpec((tm, D), lambda i: (i, 0)), pl.BlockSpec((tm, D), lambda i: (i, 0))],
                  out_shape=[_sds((s, D), F32), _sds((s, D), BF16)], carry=carry)
    return _carried(call, (a, w, xres, gain), carry)


def _ffn_down_loss(a, w, xres, gain, target, name, tm=512):
    npart, s, kp = a.shape
    tm = min(tm, s)

    def body(a_ref, w_ref, x_ref, g_ref, t_ref, dx_ref, dxb_ref, loss_ref, dg_ref):
        @pl.when(pl.program_id(0) == 0)
        def _():
            loss_ref[...] = jnp.zeros_like(loss_ref)
            dg_ref[...] = jnp.zeros_like(dg_ref)

        for rows in _row_chunks(tm):
            acc = _dot(a_ref[0, rows, :], w_ref[0])
            for p in range(1, npart):
                acc = acc + _dot(a_ref[p, rows, :], w_ref[p])
            xn = x_ref[rows, :] + 0.5 * acc
            r = lax.rsqrt(jnp.mean(xn * xn, axis=-1, keepdims=True) + EPS)
            xh = xn * r
            gv = g_ref[...]
            err = xh * gv - t_ref[rows, :]
            part = 0.5 * jnp.sum(jnp.mean(err * err, axis=-1, keepdims=True), axis=0, keepdims=True)
            dy = err * (1.0 / D)
            dyg = dy * gv
            dxn = r * (dyg - xh * jnp.mean(dyg * xh, axis=-1, keepdims=True))
            dx_ref[rows, :] = dxn
            dxb_ref[rows, :] = dxn.astype(BF16)
            loss_ref[...] += jnp.broadcast_to(part, loss_ref.shape)
            dg_ref[...] += jnp.sum(dy * xh, axis=0, keepdims=True)

    return _pcall(body, name=name, grid=(s // tm,),
                  in_specs=[pl.BlockSpec((npart, tm, kp), lambda i: (0, i, 0)),
                            pl.BlockSpec((npart, kp, D), lambda i: (0, 0, 0)),
                            pl.BlockSpec((tm, D), lambda i: (i, 0)),
                            pl.BlockSpec((1, D), lambda i: (0, 0)),
                            pl.BlockSpec((tm, D), lambda i: (i, 0))],
                  out_specs=[pl.BlockSpec((tm, D), lambda i: (i, 0)), pl.BlockSpec((tm, D), lambda i: (i, 0)),
                             pl.BlockSpec((8, 128), lambda i: (0, 0)), pl.BlockSpec((1, D), lambda i: (0, 0))],
                  out_shape=[_sds((s, D), F32), _sds((s, D), BF16), _sds((8, 128), F32), _sds((1, D), F32)],
                  )(a, w, xres, gain, target)


def _window_tiles():
    i = lax.broadcasted_iota(jnp.int32, (BLK, BLK), 0)
    j = lax.broadcasted_iota(jnp.int32, (BLK, BLK), 1)
    rel = (i - j) & (BLK - 1)
    large = jnp.full_like(rel, REL_EXACT)
    for t in BUCKET_THRESHOLDS:
        large = large + (rel >= t).astype(jnp.int32)
    return j <= i, jnp.where(rel < REL_EXACT, rel, large)


def _bias_build(rel_bias, name):
    def body(rb_ref, o_ref):
        _, bucket = _window_tiles()

        def per_head(h, carry):
            acc = jnp.zeros((BLK, BLK), F32)
            for b in range(REL_BUCKETS):
                acc = jnp.where(bucket == b, rb_ref[b, h], acc)
            o_ref[h] = acc
            return carry

        lax.fori_loop(0, N_HEADS, per_head, 0)

    return _pcall(body, name=name, grid=(1,),
                  in_specs=[pl.BlockSpec(memory_space=pltpu.SMEM)],
                  out_specs=pl.BlockSpec((N_HEADS, BLK, BLK), lambda i: (0, 0, 0)),
                  out_shape=_sds((N_HEADS, BLK, BLK), F32))(rel_bias)


def _bias_bwd(dbias, name):
    def body(db_ref, o_ref):
        _, bucket = _window_tiles()
        lane = lax.broadcasted_iota(jnp.int32, (N_HEADS, 128), 1)

        def per_bucket(b, out):
            mb = (bucket == b).astype(F32)
            per_col = jnp.sum(db_ref[...] * mb[None, :, :], axis=1)
            return jnp.where(lane == b, jnp.sum(per_col, axis=1, keepdims=True), out)

        o_ref[...] = lax.fori_loop(0, REL_BUCKETS, per_bucket, jnp.zeros((N_HEADS, 128), F32))

    return _pcall(body, name=name, grid=(1,),
                  in_specs=[pl.BlockSpec((N_HEADS, BLK, BLK), lambda i: (0, 0, 0))],
                  out_specs=pl.BlockSpec((N_HEADS, 128), lambda i: (0, 0)),
                  out_shape=_sds((N_HEADS, 128), F32))(dbias)


PAIR = 2 * HEAD
GROUP = N_HEADS // N_KV
SWA_SCALE = HEAD ** -0.5


def _window_masks(n):
    i = lax.broadcasted_iota(jnp.int32, (GROUP * BLK, BLK), 0) & (BLK - 1)
    j = lax.broadcasted_iota(jnp.int32, (GROUP * BLK, BLK), 1)
    return j <= i, jnp.logical_and(n == 0, j > i), j < HEAD


def _kv_twice(ref, base, g, low):
    slab = ref[:, base + PAIR * (g // 2): base + PAIR * (g // 2 + 1)]
    swapped = pltpu.roll(slab, HEAD, 1)
    return jnp.where(low, slab, swapped) if g % 2 == 0 else jnp.where(low, swapped, slab)


def _stack_heads(ref, g, low):
    parts = []
    for r in range(2):
        slab = ref[:, PAIR * (2 * g + r): PAIR * (2 * g + r + 1)]
        zero = jnp.zeros_like(slab)
        parts += [jnp.where(low, slab, zero), jnp.where(low, zero, slab)]
    return jnp.concatenate(parts, axis=0)


def _unstack_heads(t, low):
    return [jnp.where(low, t[2 * r * BLK:(2 * r + 1) * BLK], t[(2 * r + 1) * BLK:(2 * r + 2) * BLK])
            for r in range(2)]


def _head_rows(t, k):
    return t[k * BLK:(k + 1) * BLK]


def _per_head_column(values):
    head = lax.broadcasted_iota(jnp.int32, (GROUP * BLK, 1), 0) // BLK
    col = jnp.full((GROUP * BLK, 1), values[0], F32)
    for k in range(1, GROUP):
        col = jnp.where(head == k, values[k], col)
    return col


def _window_logits(q4, kc, kp, bias4, own, absent):
    sc = jnp.where(own, _dot_nt(q4, kc), _dot_nt(q4, kp)) * SWA_SCALE + bias4
    return jnp.where(absent, NEG, sc)


def _split_window(t, own):
    zero = jnp.zeros_like(t)
    return jnp.where(own, t, zero), jnp.where(own, zero, t)


def _swa_fwd(q, kv, bias, sinks, name, carry=None):
    s = q.shape[0]
    nb = s // BLK
    kvw = 2 * N_KV * HEAD

    def body(q_ref, kc_ref, kp_ref, b_ref, sk_ref, o_ref, lse_ref):
        own, absent, low4 = _window_masks(pl.program_id(0))
        low = low4[:BLK]
        lane = lax.broadcasted_iota(jnp.int32, (BLK, 128), 1)
        lse_t = jnp.zeros((BLK, 128), F32)
        for g in range(N_KV):
            q4 = _stack_heads(q_ref, g, low)
            kc, kp = _kv_twice(kc_ref, 0, g, low), _kv_twice(kp_ref, 0, g, low)
            vc, vp = _kv_twice(kc_ref, N_KV * HEAD, g, low), _kv_twice(kp_ref, N_KV * HEAD, g, low)
            bias4 = b_ref[GROUP * g:GROUP * (g + 1)].reshape(GROUP * BLK, BLK)
            sc = _window_logits(q4, kc, kp, bias4, own, absent)
            sk = _per_head_column([sk_ref[0, GROUP * g + k] for k in range(GROUP)])
            m = jnp.maximum(jnp.max(sc, axis=1, keepdims=True), sk)
            p = jnp.exp(sc - m)
            l = jnp.sum(p, axis=1, keepdims=True) + jnp.exp(sk - m)
            p_own, p_prev = _split_window(p.astype(BF16), own)
            out = (_dot(p_own, vc) + _dot(p_prev, vp)) * (1.0 / l)
            for r, slab in enumerate(_unstack_heads(out, low)):
                o_ref[:, PAIR * (2 * g + r): PAIR * (2 * g + r + 1)] = slab.astype(BF16)
            lse4 = m + jnp.log(l)
            for k in range(GROUP):
                lse_t = jnp.where(lane == GROUP * g + k, _head_rows(lse4, k), lse_t)
        lse_ref[...] = lse_t

    call = _pcall(body, name=name, grid=(nb,),
                  in_specs=[pl.BlockSpec((BLK, D), lambda n: (n, 0)),
                            pl.BlockSpec((BLK, kvw), lambda n: (n, 0)),
                            pl.BlockSpec((BLK, kvw), lambda n: (jnp.maximum(n - 1, 0), 0)),
                            pl.BlockSpec((N_HEADS, BLK, BLK), lambda n: (0, 0, 0)),
                            pl.BlockSpec(memory_space=pltpu.SMEM)],
                  out_specs=[pl.BlockSpec((BLK, D), lambda n: (n, 0)), pl.BlockSpec((BLK, 128), lambda n: (n, 0))],
                  out_shape=[_sds((s, D), BF16), _sds((s, 128), F32)], carry=carry)
    return _carried(call, (q, kv, kv, bias, sinks), carry)


def _fold_halves(t, g, low):
    folded = jnp.where(low, t, 0.0) + pltpu.roll(jnp.where(low, 0.0, t), HEAD, 1)
    return folded if g % 2 == 0 else pltpu.roll(folded, HEAD, 1)


def _swa_bwd(q, kv, attn, dattn, lse, bias, sinks, name, carry=None):
    s = q.shape[0]
    nb = s // BLK
    kvw = 2 * N_KV * HEAD
    voff = N_KV * HEAD

    def body(q_ref, kc_ref, kp_ref, o_ref, do_ref, lse_ref, b_ref, skrow_ref, dq_ref, dkv_ref, dbias_ref, dsk_ref,
             dq_hold, kv_hold, dq_new, kv_prev, kv_cur):
        n = pl.program_id(0)

        @pl.when(n == 0)
        def _():
            dbias_ref[...] = jnp.zeros_like(dbias_ref)
            dsk_ref[...] = jnp.zeros_like(dsk_ref)
            dq_hold[...] = jnp.zeros_like(dq_hold)
            kv_hold[...] = jnp.zeros_like(kv_hold)

        @pl.when(n < nb)
        def _():
            own, absent, low4 = _window_masks(n)
            low = low4[:BLK]
            lane = lax.broadcasted_iota(jnp.int32, (BLK, 128), 1)
            delta_t = jnp.zeros((BLK, 128), F32)
            ones = jnp.ones((PAIR, 128), BF16)
            for pair_of_kv in range(N_KV // 2):
                slab_grads = [jnp.zeros((BLK, PAIR), F32) for _ in range(4)]
                for g in (2 * pair_of_kv, 2 * pair_of_kv + 1):
                    q4, do4 = _stack_heads(q_ref, g, low), _stack_heads(do_ref, g, low)
                    kc, kp = _kv_twice(kc_ref, 0, g, low), _kv_twice(kp_ref, 0, g, low)
                    vc, vp = _kv_twice(kc_ref, voff, g, low), _kv_twice(kp_ref, voff, g, low)
                    o_slabs = [o_ref[:, PAIR * (2 * g + r): PAIR * (2 * g + r + 1)] for r in range(2)]
                    o4 = jnp.concatenate([o_slabs[0], o_slabs[0], o_slabs[1], o_slabs[1]], axis=0)
                    delta = _dot(do4 * o4, ones)
                    heads = range(GROUP * g, GROUP * (g + 1))
                    lse4 = jnp.concatenate([lse_ref[:, h:h + 1] for h in heads], axis=0)
                    bias4 = b_ref[GROUP * g:GROUP * (g + 1)].reshape(GROUP * BLK, BLK)
                    p = jnp.exp(_window_logits(q4, kc, kp, bias4, own, absent) - lse4)
                    dp = jnp.where(own, _dot_nt(do4, vc), _dot_nt(do4, vp))
                    ds = p * (dp - delta)
                    dbias_ref[GROUP * g:GROUP * (g + 1)] += ds.reshape(GROUP, BLK, BLK)
                    for k, h in enumerate(heads):
                        delta_t = jnp.where(lane == h, _head_rows(delta, k), delta_t)
                    ds_own, ds_prev = _split_window((ds * SWA_SCALE).astype(BF16), own)
                    p_own, p_prev = _split_window(p.astype(BF16), own)
                    dq4 = _dot(ds_own, kc) + _dot(ds_prev, kp)
                    for r, slab in enumerate(_unstack_heads(dq4, low)):
                        dq_new[:, PAIR * (2 * g + r): PAIR * (2 * g + r + 1)] = slab
                    grads = [_dot_tn(ds_own, q4), _dot_tn(ds_prev, q4), _dot_tn(p_own, do4), _dot_tn(p_prev, do4)]
                    slab_grads = [t + _fold_halves(dk, g, low) for t, dk in zip(slab_grads, grads)]
                ks = slice(PAIR * pair_of_kv, PAIR * (pair_of_kv + 1))
                vs = slice(voff + PAIR * pair_of_kv, voff + PAIR * (pair_of_kv + 1))
                kv_cur[:, ks], kv_prev[:, ks], kv_cur[:, vs], kv_prev[:, vs] = slab_grads
            dsk_ref[...] -= jnp.sum(jnp.exp(skrow_ref[...] - lse_ref[...]) * delta_t, axis=0, keepdims=True)

        @pl.when(n == nb)
        def _():
            kv_prev[...] = jnp.zeros_like(kv_prev)

        dq_ref[...] = dq_hold[...].astype(BF16)
        dkv_ref[...] = (kv_hold[...] + kv_prev[...]).astype(BF16)

        @pl.when(n < nb)
        def _():
            dq_hold[...] = dq_new[...]
            kv_hold[...] = kv_cur[...]

    def cur(n):
        return jnp.minimum(n, nb - 1)

    call = _pcall(body, name=name, grid=(nb + 1,), carry=carry,
                  in_specs=[pl.BlockSpec((BLK, D), lambda n: (cur(n), 0)),
                            pl.BlockSpec((BLK, kvw), lambda n: (cur(n), 0)),
                            pl.BlockSpec((BLK, kvw), lambda n: (jnp.maximum(cur(n) - 1, 0), 0)),
                            pl.BlockSpec((BLK, D), lambda n: (cur(n), 0)),
                            pl.BlockSpec((BLK, D), lambda n: (cur(n), 0)),
                            pl.BlockSpec((BLK, 128), lambda n: (cur(n), 0)),
                            pl.BlockSpec((N_HEADS, BLK, BLK), lambda n: (0, 0, 0)),
                            pl.BlockSpec((1, 128), lambda n: (0, 0))],
                  out_specs=[pl.BlockSpec((BLK, D), lambda n: (jnp.maximum(n - 1, 0), 0)),
                             pl.BlockSpec((BLK, kvw), lambda n: (jnp.maximum(n - 1, 0), 0)),
                             pl.BlockSpec((N_HEADS, BLK, BLK), lambda n: (0, 0, 0)),
                             pl.BlockSpec((1, 128), lambda n: (0, 0))],
                  out_shape=[_sds((s, D), BF16), _sds((s, kvw), BF16), _sds((N_HEADS, BLK, BLK), F32),
                             _sds((1, 128), F32)],
                  scratch=[pltpu.VMEM((BLK, D), F32), pltpu.VMEM((BLK, kvw), F32), pltpu.VMEM((BLK, D), F32),
                           pltpu.VMEM((BLK, kvw), F32), pltpu.VMEM((BLK, kvw), F32)])
    sink_row = jnp.pad(sinks, ((0, 0), (0, 128 - N_HEADS)))
    return _carried(call, (q, kv, kv, attn, dattn, lse, bias, sink_row), carry)


HALO = 16
CW = D


def _conv_taps(cu, halo_cu, first_tile):
    row = lax.broadcasted_iota(jnp.int32, cu.shape, 0)
    halo_cu = jnp.where(first_tile, 0.0, halo_cu)
    c1 = jnp.where(row == 0, halo_cu[HALO - 1:HALO], pltpu.roll(cu, 1, 0))
    c2 = jnp.where(row == 0, halo_cu[HALO - 2:HALO - 1],
                   jnp.where(row == 1, halo_cu[HALO - 1:HALO], pltpu.roll(cu, 2, 0)))
    return c1, c2


def _conv_merge_fwd(pa, attn, convw, name, ts=256, carry=None):
    _, s, _ = pa.shape
    ts = min(ts, s)
    hb = ts // HALO

    def body(pa_ref, hp_ref, at_ref, w_ref, o_ref):
        i = pl.program_id(1)
        cu = pa_ref[0].astype(F32) * pa_ref[2].astype(F32)
        c1, c2 = _conv_taps(cu, hp_ref[0].astype(F32) * hp_ref[2].astype(F32), i == 0)
        w = w_ref[...]
        c3 = w[0:1] * c2 + w[1:2] * c1 + w[2:3] * cu
        conv = pa_ref[1].astype(F32) * c3
        o_ref[...] = (jax.nn.sigmoid(pa_ref[3].astype(F32)) * at_ref[...].astype(F32)
                      + jax.nn.sigmoid(pa_ref[4].astype(F32)) * conv).astype(BF16)

    call = _pcall(body, name=name, grid=(D // CW, s // ts), carry=carry,
                  in_specs=[pl.BlockSpec((5, ts, CW), lambda c, i: (0, i, c)),
                            pl.BlockSpec((5, HALO, CW), lambda c, i: (0, jnp.maximum(i * hb - 1, 0), c)),
                            pl.BlockSpec((ts, CW), lambda c, i: (i, c)),
                            pl.BlockSpec((8, CW), lambda c, i: (0, c))],
                  out_specs=pl.BlockSpec((ts, CW), lambda c, i: (i, c)),
                  out_shape=_sds((s, D), BF16))
    return _carried(call, (pa, pa, attn, convw), carry)


def _conv_merge_bwd(dmerged, pa, attn, convw, name, ts=256, carry=None):
    _, s, _ = pa.shape
    ts = min(ts, s)
    hb = ts // HALO
    last_hb = s // HALO - 1

    def body(dm_ref, pa_ref, at_ref, w_ref, hp_ref, hn_ref, dmn_ref, dat_ref, dpa_ref, dw_ref):
        i = pl.program_id(1)
        last = i == pl.num_programs(1) - 1
        dm = dm_ref[...].astype(F32)
        cp, bp, u = pa_ref[0].astype(F32), pa_ref[1].astype(F32), pa_ref[2].astype(F32)
        sa = jax.nn.sigmoid(pa_ref[3].astype(F32))
        sc = jax.nn.sigmoid(pa_ref[4].astype(F32))
        at = at_ref[...].astype(F32)
        cu = cp * u
        c1, c2 = _conv_taps(cu, hp_ref[0].astype(F32) * hp_ref[2].astype(F32), i == 0)
        w = w_ref[...]
        c3 = w[0:1] * c2 + w[1:2] * c1 + w[2:3] * cu
        dconv = dm * sc
        dc3 = dconv * bp
        nxt = dmn_ref[...].astype(F32) * jax.nn.sigmoid(hn_ref[4].astype(F32)) * hn_ref[1].astype(F32)
        nxt = jnp.where(last, 0.0, nxt)
        row = lax.broadcasted_iota(jnp.int32, dc3.shape, 0)
        d1 = jnp.where(row == ts - 1, nxt[0:1], pltpu.roll(dc3, ts - 1, 0))
        d2 = jnp.where(row == ts - 2, nxt[0:1], jnp.where(row == ts - 1, nxt[1:2], pltpu.roll(dc3, ts - 2, 0)))
        dcu = w[2:3] * dc3 + w[1:2] * d1 + w[0:1] * d2
        dat_ref[...] = (dm * sa).astype(BF16)
        dpa_ref[0] = (dcu * u).astype(BF16)
        dpa_ref[1] = (dconv * c3).astype(BF16)
        dpa_ref[2] = (dcu * cp).astype(BF16)
        dpa_ref[3] = (dm * at * sa * (1.0 - sa)).astype(BF16)
        dpa_ref[4] = (dm * bp * c3 * sc * (1.0 - sc)).astype(BF16)

        @pl.when(i == 0)
        def _():
            dw_ref[...] = jnp.zeros_like(dw_ref)

        dw_ref[0:1, :] += jnp.sum(dc3 * c2, axis=0, keepdims=True)
        dw_ref[1:2, :] += jnp.sum(dc3 * c1, axis=0, keepdims=True)
        dw_ref[2:3, :] += jnp.sum(dc3 * cu, axis=0, keepdims=True)

    call = _pcall(body, name=name, grid=(D // CW, s // ts), carry=carry,
                  in_specs=[pl.BlockSpec((ts, CW), lambda c, i: (i, c)),
                            pl.BlockSpec((5, ts, CW), lambda c, i: (0, i, c)),
                            pl.BlockSpec((ts, CW), lambda c, i: (i, c)),
                            pl.BlockSpec((8, CW), lambda c, i: (0, c)),
                            pl.BlockSpec((5, HALO, CW), lambda c, i: (0, jnp.maximum(i * hb - 1, 0), c)),
                            pl.BlockSpec((5, HALO, CW), lambda c, i: (0, jnp.minimum((i + 1) * hb, last_hb), c)),
                            pl.BlockSpec((HALO, CW), lambda c, i: (jnp.minimum((i + 1) * hb, last_hb), c))],
                  out_specs=[pl.BlockSpec((ts, CW), lambda c, i: (i, c)),
                             pl.BlockSpec((5, ts, CW), lambda c, i: (0, i, c)),
                             pl.BlockSpec((8, CW), lambda c, i: (0, c))],
                  out_shape=[_sds((s, D), BF16), _sds((5, s, D), BF16), _sds((8, D), F32)])
    return _carried(call, (dmerged, pa, attn, convw, pa, pa, dmerged), carry)


def _xattn_fwd(q, kv, name, tq=1024):
    s, _ = q.shape
    nm = kv.shape[1]
    tq = min(tq, s)

    def body(q_ref, kv_ref, o_ref, lse_ref):
        lane = lax.broadcasted_iota(jnp.int32, (tq, 128), 1)
        lse_t = jnp.zeros((tq, 128), F32)
        for h in range(XH):
            hs = slice(XHD * h, XHD * (h + 1))
            sc = _dot_nt(q_ref[:, hs], kv_ref[h]) * (XHD ** -0.5)
            m = jnp.max(sc, axis=1, keepdims=True)
            p = jnp.exp(sc - m)
            l = jnp.sum(p, axis=1, keepdims=True)
            o_ref[:, hs] = (_dot(p.astype(BF16), kv_ref[XH + h]) * (1.0 / l)).astype(BF16)
            lse_t = jnp.where(lane == h, m + jnp.log(l), lse_t)
        lse_ref[...] = lse_t

    return _pcall(body, name=name, grid=(s // tq,),
                  in_specs=[pl.BlockSpec((tq, D), lambda i: (i, 0)), pl.BlockSpec((2 * XH, nm, XHD), lambda i: (0, 0, 0))],
                  out_specs=[pl.BlockSpec((tq, D), lambda i: (i, 0)), pl.BlockSpec((tq, 128), lambda i: (i, 0))],
                  out_shape=[_sds((s, D), BF16), _sds((s, 128), F32)])(q, kv)


def _xattn_bwd(q, kv, o, do, lse, name, tq=512, carry=None):
    s, _ = q.shape
    nm = kv.shape[1]
    tq = min(tq, s)

    def body(q_ref, kv_ref, o_ref, do_ref, lse_ref, dq_ref, dkv_ref):
        @pl.when(pl.program_id(0) == 0)
        def _():
            dkv_ref[...] = jnp.zeros_like(dkv_ref)

        for h in range(XH):
            hs = slice(XHD * h, XHD * (h + 1))
            qh, kh, vh, dob = q_ref[:, hs], kv_ref[h], kv_ref[XH + h], do_ref[:, hs]
            p = jnp.exp(_dot_nt(qh, kh) * (XHD ** -0.5) - lse_ref[:, h:h + 1])
            dp = _dot_nt(dob, vh)
            delta = jnp.sum(dob.astype(F32) * o_ref[:, hs].astype(F32), axis=1, keepdims=True)
            dsb = (p * (dp - delta) * (XHD ** -0.5)).astype(BF16)
            dq_ref[:, hs] = _dot(dsb, kh).astype(BF16)
            dkv_ref[h] += _dot_tn(dsb, qh)
            dkv_ref[XH + h] += _dot_tn(p.astype(BF16), dob)

    call = _pcall(body, name=name, grid=(s // tq,), carry=carry,
                  in_specs=[pl.BlockSpec((tq, D), lambda i: (i, 0)), pl.BlockSpec((2 * XH, nm, XHD), lambda i: (0, 0, 0)),
                            pl.BlockSpec((tq, D), lambda i: (i, 0)), pl.BlockSpec((tq, D), lambda i: (i, 0)),
                            pl.BlockSpec((tq, 128), lambda i: (i, 0))],
                  out_specs=[pl.BlockSpec((tq, D), lambda i: (i, 0)), pl.BlockSpec((2 * XH, nm, XHD), lambda i: (0, 0, 0))],
                  out_shape=[_sds((s, D), BF16), _sds((2 * XH, nm, XHD), F32)])
    return _carried(call, (q, kv, o, do, lse), carry)


def _ffn_down_bwd(dxb, wd4, gu4, name, tm=512, carry=None, behind=None):
    s, _ = dxb.shape
    tm = min(tm, s)

    def body(dx_ref, w_hbm, gu_ref, o_ref, w_ref, w_sem):
        _load_once(w_hbm, w_ref, w_sem)
        for p in range(4):
            for rows in _row_chunks(tm):
                da = _dot_nt(dx_ref[rows, :], w_ref[p])
                g = gu_ref[0, p, rows, :].astype(F32)
                u = gu_ref[1, p, rows, :].astype(F32)
                sg = jax.nn.sigmoid(g)
                t = da * sg
                o_ref[0, p, rows, :] = (t * u * (1.0 + g - g * sg)).astype(BF16)
                o_ref[1, p, rows, :] = (t * g).astype(BF16)

    block = pl.BlockSpec((2, 4, tm, FS), lambda i: (0, 0, i, 0))
    call = _pcall(body, name=name, grid=(s // tm,), carry=carry, behind=behind,
                  in_specs=[pl.BlockSpec((tm, D), lambda i: (i, 0)), HBM_SPEC, block],
                  out_specs=block, out_shape=_sds((2, 4, s, FS), BF16), scratch=_resident(wd4))
    return _carried(call, (dxb, wd4, gu4), carry)


def _mm_tn(a, b, name, scale=1.0, carry=None):
    pa_n, s, m = a.shape
    pb_n, _, n = b.shape
    po = max(pa_n, pb_n)
    tk = 1024
    if po == 1 and s > tk and s % tk == 0:
        def body_k(a_ref, b_ref, o_ref, acc_ref):
            k = pl.program_id(0)
            part = _dot_tn(a_ref[...], b_ref[...])

            @pl.when(k == 0)
            def _():
                acc_ref[...] = part

            @pl.when(k > 0)
            def _():
                acc_ref[...] += part

            @pl.when(k == s // tk - 1)
            def _():
                o_ref[...] = (scale * acc_ref[...]).astype(BF16)

        call = _pcall(body_k, name=name, grid=(s // tk,), carry=carry,
                      in_specs=[pl.BlockSpec((None, tk, m), lambda k: (0, k, 0)),
                                pl.BlockSpec((None, tk, n), lambda k: (0, k, 0))],
                      out_specs=pl.BlockSpec((None, m, n), lambda k: (0, 0, 0)),
                      out_shape=_sds((1, m, n), BF16), scratch=[pltpu.VMEM((m, n), F32)])
        return _carried(call, (a, b), carry)
    tn = n if po >= 4 else min(n, 256)

    def body(a_ref, b_ref, o_ref):
        o_ref[...] = (scale * _dot_tn(a_ref[...], b_ref[...])).astype(BF16)

    call = _pcall(body, name=name, grid=(po, n // tn), carry=carry,
                  in_specs=[pl.BlockSpec((None, s, m), lambda o, j: (o if pa_n > 1 else 0, 0, 0)),
                            pl.BlockSpec((None, s, tn), lambda o, j: (o if pb_n > 1 else 0, 0, j))],
                  out_specs=pl.BlockSpec((None, m, tn), lambda o, j: (o, 0, j)),
                  out_shape=_sds((po, m, n), BF16))
    return _carried(call, (a, b), carry)


def _mm_tn_rows(a, b, name, total_rows, row0, begun=None, tm=512, carry=None):
    p, s, m = a.shape
    n = b.shape[1]
    tm = min(tm, m)
    tiles = m // tm
    assert row0 % tm == 0 and m % tm == 0, (row0, m, tm)

    def body(a_ref, b_ref, *rest):
        rest[-1][...] = _dot_tn(a_ref[...], b_ref[...]).astype(BF16)

    in_specs = [pl.BlockSpec((None, s, tm), lambda o, i: (o, 0, i)), pl.BlockSpec((s, n), lambda o, i: (0, 0))]
    call = _pcall(body, name=name, grid=(p, tiles), in_specs=in_specs + ([HBM_SPEC] if begun is not None else []),
                  out_specs=pl.BlockSpec((tm, n), lambda o, i: (row0 // tm + o * tiles + i, 0)),
                  out_shape=_sds((total_rows, n), BF16), aliases={2: 0} if begun is not None else None, carry=carry)
    return _carried(call, (a, b, begun) if begun is not None else (a, b), carry)


def _sum_dots(a_ref, b_ref, nj, bt, rows=slice(None)):
    dot = _dot_nt if bt else _dot
    acc = dot(a_ref[0, rows, :], b_ref[0])
    for j in range(1, nj):
        acc = acc + dot(a_ref[j, rows, :], b_ref[j])
    return acc


def _mm_acc(a, b, name, out_dtype, tm=1024, bt=False, carry=None):
    nj, s, k = a.shape
    n = b.shape[1] if bt else b.shape[2]
    tm = min(tm, s)

    def body(a_ref, b_ref, o_ref):
        o_ref[...] = _sum_dots(a_ref, b_ref, nj, bt).astype(out_dtype)

    call = _pcall(body, name=name, grid=(s // tm,), carry=carry,
                  in_specs=[pl.BlockSpec((nj, tm, k), lambda i: (0, i, 0)),
                            pl.BlockSpec(b.shape, lambda i: (0, 0, 0))],
                  out_specs=pl.BlockSpec((tm, n), lambda i: (i, 0)), out_shape=_sds((s, n), out_dtype))
    return _carried(call, (a, b), carry)


def _rms_bwd_call(name, acts, weights, scratch, load, dh_rows, *, x, gain, dres, tm, carry, behind=None):
    s, n = x.shape
    tm = min(tm, s)
    n_act, n_w = len(acts), len(weights)

    def body(*refs):
        act_refs, w_refs = refs[:n_act], refs[n_act:n_act + n_w]
        x_ref, g_ref, r_ref, dx_ref, dxb_ref, dg_ref = refs[n_act + n_w:n_act + n_w + 6]
        held = refs[n_act + n_w + 6:]
        load(w_refs, held)

        @pl.when(pl.program_id(0) == 0)
        def _():
            dg_ref[...] = jnp.zeros_like(dg_ref)

        for rows in _row_chunks(tm):
            dh = dh_rows(act_refs, held, rows)
            xv = x_ref[rows, :]
            r = lax.rsqrt(jnp.mean(xv * xv, axis=-1, keepdims=True) + EPS)
            xh = xv * r
            dyg = dh * g_ref[...]
            dx = r_ref[rows, :] + r * (dyg - xh * jnp.mean(dyg * xh, axis=-1, keepdims=True))
            dx_ref[rows, :] = dx
            dxb_ref[rows, :] = dx.astype(BF16)
            dg_ref[...] += jnp.sum(dh * xh, axis=0, keepdims=True)

    def tile(a):
        return (pl.BlockSpec((tm, a.shape[1]), lambda i: (i, 0)) if a.ndim == 2
                else pl.BlockSpec((a.shape[0], tm, a.shape[2]), lambda i: (0, i, 0)))

    row = pl.BlockSpec((tm, n), lambda i: (i, 0))
    in_specs = [tile(a) for a in acts] + [HBM_SPEC] * n_w + [row, pl.BlockSpec((1, n), lambda i: (0, 0)), row]
    call = _pcall(body, name=name, grid=(s // tm,), in_specs=in_specs, carry=carry, behind=behind,
                  out_specs=[row, row, pl.BlockSpec((1, n), lambda i: (0, 0))],
                  out_shape=[_sds((s, n), F32), _sds((s, n), BF16), _sds((1, n), F32)], scratch=scratch)
    return _carried(call, tuple(acts) + tuple(weights) + (x, gain, dres), carry)


def _mm_acc_rms_bwd(a, b, name, *, x, gain, dres, scale=None, tm=512, bt=False, carry=None, behind=None):
    def load(w_refs, held):
        _load_once(w_refs[0], held[0], held[1])

    def dh_rows(act_refs, held, rows):
        dh = _sum_dots(act_refs[0], held[0], a.shape[0], bt, rows)
        return dh if scale is None else scale * dh

    return _rms_bwd_call(name, [a], [b], _resident(b), load, dh_rows, x=x, gain=gain, dres=dres, tm=tm, carry=carry,
                         behind=behind)


def _in_proj_bwd(dpa, dq, dkv, w_in_t, name, *, x, gain, dres, tm=512, carry=None, behind=None):
    def load(w_refs, held):
        _load_in_proj(w_refs[0], *held)

    def dh_rows(act_refs, held, rows):
        dpa_ref, dq_ref, dkv_ref = act_refs
        wq_ref, wkv_ref, wa_ref, _ = held
        dh = _dot(dq_ref[rows, :], wq_ref[...]) + _dot(dkv_ref[rows, :], wkv_ref[...])
        return dh + _sum_dots(dpa_ref, wa_ref, N_SEG, False, rows)

    return _rms_bwd_call(name, [dpa, dq, dkv], [w_in_t], IN_PROJ_WEIGHTS, load, dh_rows, x=x, gain=gain, dres=dres,
                         tm=tm, carry=carry, behind=behind)


def _adam(w, g, m, v):
    m2 = ADAM_B1 * m + (1.0 - ADAM_B1) * g
    v2 = ADAM_B2 * v + (1.0 - ADAM_B2) * (g * g)
    m_hat = m2 / (1.0 - ADAM_B1 ** ADAM_STEP)
    v_hat = v2 / (1.0 - ADAM_B2 ** ADAM_STEP)
    delta = -ADAM_LR * (m_hat / (jnp.sqrt(v_hat) + ADAM_EPS) + ADAM_WD * w)
    return delta, m2, v2


def _adamw(parts, w, m, v, name, behind=None):
    _, r, c = parts.shape
    tr = max(t for t in range(16, 257, 16) if r % t == 0)

    def body(p_ref, w_ref, m_ref, v_ref, g_ref, d_ref, m2_ref, v2_ref):
        g = p_ref[0].astype(F32)
        for i in range(1, N_DEV):
            g = g + p_ref[i].astype(F32)
        delta, m2, v2 = _adam(w_ref[...], g, m_ref[...], v_ref[...])
        g_ref[...] = g
        d_ref[...] = delta
        m2_ref[...] = m2
        v2_ref[...] = v2

    blk = pl.BlockSpec((tr, c), lambda i: (i, 0))
    return _pcall(body, name=name, grid=(r // tr,), behind=behind,
                  in_specs=[pl.BlockSpec((N_DEV, tr, c), lambda i: (0, i, 0)), blk, blk, blk],
                  out_specs=[blk] * 4, out_shape=[_sds((r, c), F32)] * 4)(parts, w, m, v)


def _position():
    return lax.axis_index("x"), lax.axis_index("y"), lax.axis_index("c")


def _slot(px, py, pc):
    return 4 * px + 2 * py + pc


def _row_window(ref, rows):
    r0, r1 = rows
    return ref if (r0, r1) == (0, ref.shape[0]) else ref.at[pl.ds(r0, r1 - r0)]


def _split_items(items):
    sources = [src for src, _, _ in items]
    begun = [(a, dest) for a, (_, _, dest) in enumerate(items) if dest is not None]
    aliases = {len(sources) + k: a for k, (a, _) in enumerate(begun)}
    return sources + [dest for _, dest in begun], [rows for _, rows, _ in items], aliases


def _gather_carry(items):
    na = len(items)
    carry_ins, windows, aliases = _split_items(items)

    def plan(ins, outs, sems):
        send_sems, recv_sems, local_sems = sems
        x, y, c = _position()
        me, sibling = (x, y, c), (x, y, 1 - c)
        chips = [(1 - x, y), (x, 1 - y), (1 - x, 1 - y)]
        ins = [_row_window(ins[a], windows[a]) for a in range(na)]

        def block_rows(a, block):
            return _row_window(outs[a].at[_slot(*block)], windows[a])

        def copy(a, k, block, to, src=None):
            rows = block_rows(a, block)
            return pltpu.make_async_remote_copy(src_ref=rows if src is None else src, dst_ref=rows,
                                                send_sem=send_sems.at[k, a], recv_sem=recv_sems.at[k, a],
                                                device_id=to, device_id_type=MESH)

        mine = [pltpu.make_async_copy(ins[a], block_rows(a, me), local_sems.at[a]) for a in range(na)]
        first = [copy(a, 0, me, sibling, src=ins[a]) for a in range(na)]
        for j, chip in enumerate(chips):
            first += [copy(a, 1 + j, me, (*chip, c), src=ins[a]) for a in range(na)]
        landed = [[copy(a, 1 + j, (*chip, c), me) for a in range(na)] for j, chip in enumerate(chips)]
        passed = [[copy(a, 4 + j, (*chip, c), sibling) for a in range(na)] for j, chip in enumerate(chips)]
        from_sibling = [copy(a, 0, sibling, me) for a in range(na)]
        for j, chip in enumerate(chips):
            from_sibling += [copy(a, 4 + j, (*chip, 1 - c), me) for a in range(na)]
        return mine, first, landed, passed, from_sibling

    def start(ins, outs, sems):
        mine, first, _, _, _ = plan(ins, outs, sems)
        for cp in mine + first:
            cp.start()

    def mid(ins, outs, sems):
        _, _, landed, passed, _ = plan(ins, outs, sems)
        for over_ici, onward in zip(landed, passed):
            for cp, fwd in zip(over_ici, onward):
                cp.wait_recv()
                fwd.start()

    def finish(ins, outs, sems):
        mine, first, _, passed, from_sibling = plan(ins, outs, sems)
        for cp in from_sibling:
            cp.wait_recv()
        for cp in first + [fwd for onward in passed for fwd in onward]:
            cp.wait_send()
        for cp in mine:
            cp.wait()

    return _Carry(carry_ins, [_sds((N_DEV,) + src.shape, src.dtype) for src, _, _ in items],
                  [pltpu.SemaphoreType.DMA((7, na)), pltpu.SemaphoreType.DMA((7, na)),
                   pltpu.SemaphoreType.DMA((na,))], start, finish, mid, aliases)


def _exchange_carry(scattered, replicated=()):
    items = list(scattered) + [(a, (0, a.shape[0]), None) for a in replicated]
    na, ns = len(items), len(scattered)
    carry_ins, windows, aliases = _split_items(items)

    def plan(ins, outs, sems):
        send_sems, recv_sems, local_sems = sems
        me = _slot(*_position())

        def source(a, j):
            return _row_window(ins[a].at[j] if a < ns else ins[a], windows[a])

        def copy(a, j, i):
            return pltpu.make_async_remote_copy(src_ref=source(a, j), dst_ref=_row_window(outs[a].at[i], windows[a]),
                                                send_sem=send_sems.at[j, a], recv_sem=recv_sems.at[i, a],
                                                device_id=(j >> 2, (j >> 1) & 1, j & 1), device_id_type=MESH)

        def own(a, j):
            return pltpu.make_async_copy(source(a, j), _row_window(outs[a].at[j], windows[a]), local_sems.at[a])

        return me, copy, own

    def start(ins, outs, sems):
        me, copy, own = plan(ins, outs, sems)
        for a in range(na):
            for j in range(N_DEV):
                @pl.when(me == j)
                def _():
                    own(a, j).start()

                @pl.when(me != j)
                def _():
                    copy(a, j, me).start()

    def finish(ins, outs, sems):
        me, copy, own = plan(ins, outs, sems)
        for a in range(na):
            for j in range(N_DEV):
                @pl.when(me == j)
                def _():
                    for i in range(N_DEV):
                        if i != j:
                            copy(a, j, i).wait_recv()
                    own(a, j).wait()

                @pl.when(me != j)
                def _():
                    copy(a, j, me).wait_send()

    return _Carry(carry_ins, [_sds((N_DEV,) + src.shape[-2:], src.dtype) for src, _, _ in items],
                  [pltpu.SemaphoreType.DMA((N_DEV, na)), pltpu.SemaphoreType.DMA((N_DEV, na)),
                   pltpu.SemaphoreType.DMA((na,))], start, finish, None, aliases)


HBM_ARRAY = pl.BlockSpec(memory_space=pltpu.HBM)
SEMAPHORES = pl.BlockSpec(memory_space=pltpu.SEMAPHORE)
DATAFLOW = pltpu.SideEffectType.DATAFLOW_SIDE_EFFECTING


def _exchange_copy(parts_ref, land_ref, send_sems, recv_sems, me, j):
    return pltpu.make_async_remote_copy(src_ref=parts_ref.at[j], dst_ref=land_ref.at[me], send_sem=send_sems.at[j],
                                        recv_sem=recv_sems.at[me], device_id=(j >> 2, (j >> 1) & 1, j & 1),
                                        device_id_type=MESH)


def _exchange_start(parts, name):
    def body(parts_ref, land_ref, send_sems, recv_sems, parts_thru, land_thru, token):
        me = _slot(*_position())
        for j in range(N_DEV):
            @pl.when(me == j)
            def _():
                pltpu.make_async_copy(parts_ref.at[j], land_ref.at[j], send_sems.at[j]).start()

            @pl.when(me != j)
            def _():
                _exchange_copy(parts_ref, land_ref, send_sems, recv_sems, me, j).start()
        token[...] = jnp.zeros_like(token)

    return pl.pallas_call(
        body, name=name,
        out_shape=(pltpu.SemaphoreType.DMA((N_DEV,)), pltpu.SemaphoreType.DMA((N_DEV,)),
                   pltpu.HBM(parts.shape, parts.dtype), pltpu.HBM(parts.shape, parts.dtype), _sds((8, 128), F32)),
        in_specs=(HBM_ARRAY, HBM_ARRAY),
        out_specs=(SEMAPHORES, SEMAPHORES, HBM_ARRAY, HBM_ARRAY, pl.BlockSpec(memory_space=pltpu.VMEM)),
        input_output_aliases={0: 2, 1: 3}, compiler_params=pltpu.CompilerParams(has_side_effects=DATAFLOW),
    )(pltpu.with_memory_space_constraint(parts, pltpu.HBM),
      pltpu.with_memory_space_constraint(lax.empty(parts.shape, parts.dtype), pltpu.HBM))


def _exchange_wait(send_sems, recv_sems, parts_thru, land_thru, after, name):
    def body(parts_ref, land_ref, send_sems, recv_sems, *rest):
        me = _slot(*_position())
        for j in range(N_DEV):
            @pl.when(me == j)
            def _():
                pltpu.make_async_copy(parts_ref.at[j], land_ref.at[j], send_sems.at[j]).wait()

            @pl.when(me != j)
            def _():
                both = pltpu.make_async_remote_copy(src_ref=parts_ref.at[j], dst_ref=land_ref.at[j],
                                                    send_sem=send_sems.at[j], recv_sem=recv_sems.at[j],
                                                    device_id=(j >> 2, (j >> 1) & 1, j & 1), device_id_type=MESH)
                both.wait_send()
                both.wait_recv()

    return pl.pallas_call(
        body, name=name, out_shape=(pltpu.HBM(parts_thru.shape, parts_thru.dtype),
                                    pltpu.HBM(parts_thru.shape, parts_thru.dtype)),
        in_specs=(HBM_ARRAY, HBM_ARRAY, SEMAPHORES, SEMAPHORES) + (pl.BlockSpec(memory_space=pl.ANY),) * len(after),
        out_specs=(HBM_ARRAY, HBM_ARRAY), input_output_aliases={0: 0, 1: 1},
        compiler_params=pltpu.CompilerParams(has_side_effects=DATAFLOW),
    )(parts_thru, land_thru, send_sems, recv_sems, *after)[1]


class _Mesh:
    def __init__(self, shards):
        self.shards, self.full, self.received, self.cache, self.pending, self.tokens = shards, {}, {}, {}, {}, {}

    def fetch(self, wanted):
        items = []
        for want in wanted:
            name, r0, r1 = want if isinstance(want, tuple) else (want, 0, self.shards[want].shape[0])
            items.append((self.shards[name], (r0, r1), self.full.get(name)))
        return _gather_carry(items)

    def fetched(self, wanted, results):
        self.full.update(zip([want[0] if isinstance(want, tuple) else want for want in wanted], results))

    def send(self, *payloads):
        return _exchange_carry([(parts, rows or (0, parts.shape[1]), self.received.get(name))
                                for name, parts, rows in payloads])

    def sent(self, names, results):
        self.received.update(zip(names, results))

    def send_apart(self, name, parts):
        *self.pending[name], self.tokens[name] = _exchange_start(parts, "exchange_" + name + "_start")
        return self.tokens[name]

    def sent_apart(self, name, after):
        self.received[name] = _exchange_wait(*self.pending.pop(name), after, "exchange_" + name + "_wait")

    def w(self, key):
        if key not in self.cache:
            self.cache[key] = self._layout(key)
        return self.cache[key]

    def _layout(self, key):
        if key in ("gu1", "gu2"):
            return self.full[key]
        if key in ("d1", "d2"):
            return self.full[key].reshape(4, FS, D)
        if key in ("out", "q", "o"):
            return self.full[key].reshape(D, D)
        if key == "kv":
            return self.full["kv"]
        if key == "convw":
            rows = self.full["conv"][:, :3, :].transpose(1, 0, 2).reshape(3, D)
            return jnp.concatenate([rows, jnp.zeros((5, D), F32)], axis=0)
        assert key == "win_t", key
        return self.full["win"].reshape(-1, D)


def _forward_backward(x, mem, target, g, rel_bias, sinks, ex):
    s = x.shape[0]
    def fetching(wanted, call, *args, **kw):
        res, got = call(*args, carry=ex.fetch(wanted), **kw)
        ex.fetched(wanted, got)
        return res

    h1 = fetching(["gu1", "conv"], _rmsnorm, x, g["ffn1"], "norm_ffn1")
    gu1, a1 = fetching(["d1", ("win", 0, 400)], _ffn_up, h1, ex.w("gu1").reshape(2, 4, FS, D), "ffn1_up")
    x1, h2 = fetching([("win", 400, 832)], _mm_res_norm, a1, ex.w("d1"), x, g["mix"], 0.5, "ffn1_down")
    pa, q, kv = fetching(["gu2"], _in_proj, h2, ex.w("win_t"), "in_proj")
    biasm = _bias_build(rel_bias, "bias_build")
    attn, lse = fetching(["out", "kv", "o"], _swa_fwd, q, kv, biasm, sinks, "swa_fwd")
    merged = fetching(["q"], _conv_merge_fwd, pa, attn, ex.w("convw"), "conv_merge_fwd")
    (x2, h3), _ = _mm_res_norm(merged[None], ex.w("out")[None], x1, g["xattn"], 1.0, "out_proj")
    q2 = _mm_nn(h3, ex.w("q")[None], "xattn_q")[0][0]
    mh, kv2 = _norm_mm(mem, g["mem"], ex.w("kv"), "xattn_kv")
    o, lse2 = _xattn_fwd(q2, kv2, "xattn_fwd")
    (x3, h4), _ = _mm_res_norm(o[None], ex.w("o")[None], x2, g["ffn2"], 1.0, "xattn_o")
    gu2, a2 = fetching(["d2"], _ffn_up, h4, ex.w("gu2").reshape(2, 4, FS, D), "ffn2_up")
    dx4, dx4b, loss, d_final = _ffn_down_loss(a2, ex.w("d2"), x3, g["final"], target, "ffn2_down_loss")
    def sending(payloads, call, *args, **kw):
        res, got = call(*args, carry=ex.send(*payloads), **kw)
        ex.sent([name for name, _, _ in payloads], got)
        return res

    dw_d2 = _mm_tn(a2, dx4b[None], "dw_ffn2_down", scale=0.5)[0].reshape(N_DEV, -1, D)
    dgu2 = sending([("d2", dw_d2, (0, 288))], _ffn_down_bwd, dx4b, ex.w("d2"), gu2, "ffn2_down_bwd").reshape(8, s, FS)
    dw_gu2 = sending([("d2", dw_d2, (288, 352))], _mm_tn, dgu2, h4[None], "dw_ffn2_up", scale=0.5)
    dx3, dx3b, d_ffn2 = sending([("gu2", dw_gu2, (0, 368))], _mm_acc_rms_bwd, dgu2, ex.w("gu2"), "ffn2_up_bwd",
                                x=x3, gain=g["ffn2"], dres=dx4, scale=0.5)
    do, _ = _mm_acc(dx3b[None], ex.w("o")[None], "xattn_o_bwd", BF16, bt=True)
    dw_o = _mm_tn(o[None], dx3b[None], "dw_xattn_o")[0].reshape(N_DEV, -1, D)
    (dq2, dkv2), _ = _xattn_bwd(q2, kv2, o, do, lse2, "xattn_bwd")
    dkv2b = dkv2.astype(BF16)
    dw_q = _mm_tn(h3[None], dq2[None], "dw_xattn_q")[0].reshape(N_DEV, -1, D)
    (dx2, dx2b, d_xattn), _ = _mm_acc_rms_bwd(dq2[None], ex.w("q")[None], "xattn_q_bwd", x=x2, gain=g["xattn"],
                                              dres=dx3, bt=True)
    dw_kv = _mm_tn(mh[None], dkv2b, "dw_xattn_kv")[0]
    (_, _, d_mem), _ = _mm_acc_rms_bwd(dkv2b, ex.w("kv"), "xattn_kv_bwd", x=mem, gain=g["mem"],
                                       dres=jnp.zeros_like(mem), bt=True)
    dmerged, _ = _mm_acc(dx2b[None], ex.w("out")[None], "out_proj_bwd", BF16, bt=True)
    dw_out = _mm_tn(merged[None], dx2b[None], "dw_out_proj")[0].reshape(N_DEV, -1, D)
    dattn, dpa, d_convw = sending([("kv", dw_kv, None)], _conv_merge_bwd,
                                  dmerged, pa, attn, ex.w("convw"), "conv_merge_bwd")
    dq, dkv, dbias, d_sinks = sending([("gu2", dw_gu2, (368, FS)), ("out", dw_out, None)], _swa_bwd,
                                      q, kv, attn, dattn, lse, biasm, sinks, "swa_bwd")
    d_relb = _bias_bwd(dbias, "bias_bwd")
    w_rows = ex.w("win_t").shape[0]
    dw_in = sending([("o", dw_o, None), ("q", dw_q, None)], _mm_tn_rows, dpa, h2, "dw_in_proj_a", w_rows, NQ + NKV)
    dw_in = _mm_tn_rows(dq[None], h2, "dw_in_proj_q", w_rows, 0, begun=dw_in)[0]
    dw_in = _mm_tn_rows(dkv[None], h2, "dw_in_proj_kv", w_rows, NQ, begun=dw_in)[0].reshape(N_DEV, -1, D)
    (dx1, dx1b, d_mix), _ = _in_proj_bwd(dpa, dq, dkv, ex.w("win_t"), "in_proj_bwd", x=x1, gain=g["mix"], dres=dx2,
                                         behind=ex.send_apart("win", dw_in))
    dw_d1 = _mm_tn(a1, dx1b[None], "dw_ffn1_down", scale=0.5)[0].reshape(N_DEV, -1, D)
    dgu1 = _ffn_down_bwd(dx1b, ex.w("d1"), gu1, "ffn1_down_bwd", behind=ex.send_apart("d1", dw_d1))[0]
    dgu1 = dgu1.reshape(8, s, FS)
    dw_gu1 = _mm_tn(dgu1, h1[None], "dw_ffn1_up", scale=0.5)[0]
    (dx0, _, d_ffn1), _ = _mm_acc_rms_bwd(dgu1, ex.w("gu1"), "ffn1_up_bwd", x=x, gain=g["ffn1"], dres=dx1,
                                          scale=0.5, behind=ex.send_apart("gu1", dw_gu1))

    relb_row = jnp.concatenate([d_relb[:, :REL_BUCKETS].T.reshape(1, REL_BUCKETS * N_HEADS), d_sinks[:, :N_HEADS],
                                jnp.zeros((1, D - REL_BUCKETS * N_HEADS - N_HEADS), F32)], axis=1)
    loss_row = jnp.concatenate([loss[0:1, 0:1], jnp.zeros((1, D - 1), F32)], axis=1)
    small = jnp.concatenate([d_ffn1, d_mix, d_xattn, d_mem, d_ffn2, d_final, relb_row, loss_row, d_convw[0:3],
                             jnp.zeros((SMALL_ROWS - ROW_CONV - 3, D), F32)], axis=0)
    return dx0, small


def _pack_small(norms, final, relb, sinks, conv_local, me):
    relb_row = jnp.concatenate([relb.reshape(1, -1), sinks.reshape(1, -1),
                                jnp.zeros((1, D - REL_BUCKETS * N_HEADS - N_HEADS), F32)], axis=1)
    conv_rows = lax.dynamic_update_slice(jnp.zeros((3, D), F32), conv_local.reshape(3, -1), (0, 128 * me))
    return jnp.concatenate(list(norms) + [final.reshape(1, D), relb_row, jnp.zeros((1, D), F32), conv_rows,
                                          jnp.zeros((SMALL_ROWS - ROW_CONV - 3, D), F32)], axis=0)


def kernel(x, mem, positions, rel_bias, ffn1_norm, ffn1_w_gu, ffn1_w_down, mix_norm, w_in, sinks, conv_w, w_out, xattn_norm, mem_norm, xattn_wq, xattn_wkv, xattn_wo, ffn2_norm, ffn2_w_gu, ffn2_w_down, final_norm, loss_target, m_rel_bias, m_ffn1_norm, m_ffn1_w_gu, m_ffn1_w_down, m_mix_norm, m_w_in, m_sinks, m_conv_w, m_w_out, m_xattn_norm, m_mem_norm, m_xattn_wq, m_xattn_wkv, m_xattn_wo, m_ffn2_norm, m_ffn2_w_gu, m_ffn2_w_down, m_final_norm, v_rel_bias, v_ffn1_norm, v_ffn1_w_gu, v_ffn1_w_down, v_mix_norm, v_w_in, v_sinks, v_conv_w, v_w_out, v_xattn_norm, v_mem_norm, v_xattn_wq, v_xattn_wkv, v_xattn_wo, v_ffn2_norm, v_ffn2_w_gu, v_ffn2_w_down, v_final_norm):
    del positions
    me = _slot(*_position())
    big = dict(gu1=(ffn1_w_gu, m_ffn1_w_gu, v_ffn1_w_gu), d1=(ffn1_w_down, m_ffn1_w_down, v_ffn1_w_down),
               win=(w_in, m_w_in, v_w_in), out=(w_out, m_w_out, v_w_out), q=(xattn_wq, m_xattn_wq, v_xattn_wq),
               kv=(xattn_wkv, m_xattn_wkv, v_xattn_wkv), o=(xattn_wo, m_xattn_wo, v_xattn_wo),
               gu2=(ffn2_w_gu, m_ffn2_w_gu, v_ffn2_w_gu), d2=(ffn2_w_down, m_ffn2_w_down, v_ffn2_w_down))
    order = list(big)
    transposed = ("gu1", "gu2", "win")
    local = {k: tuple(t[0].T if k in transposed else t[0] for t in big[k]) for k in order}
    shards = {k: local[k][0].astype(BF16) for k in order}
    shards["conv"] = jnp.concatenate([conv_w[0], jnp.zeros((5, 128), F32)], axis=0)
    ex = _Mesh(shards)
    gains = dict(ffn1=ffn1_norm, mix=mix_norm, xattn=xattn_norm, mem=mem_norm, ffn2=ffn2_norm,
                 final=final_norm.reshape(1, D))
    dx, small = _forward_backward(x[0], mem[0], loss_target[0], gains, rel_bias, sinks, ex)
    apart = ("d1", "win", "gu1")
    big_out = {k: _adamw(ex.received[k], *local[k], "adamw_" + k, behind=ex.tokens["gu1"])
               for k in order if k not in apart}
    for k in apart[:-1]:
        ex.sent_apart(k, after=[big_out[j][1] for j in big_out])
        big_out[k] = _adamw(ex.received[k], *local[k], "adamw_" + k)
    small_parts = _run_alone(_exchange_carry([], [small]), "exchange_small", after=[big_out[k][1] for k in big_out])[0]
    packed = [_pack_small(norms, final, relb, sk, conv, me) for norms, final, relb, sk, conv in (
        ((ffn1_norm, mix_norm, xattn_norm, mem_norm, ffn2_norm), final_norm, rel_bias, sinks, conv_w),
        ((m_ffn1_norm, m_mix_norm, m_xattn_norm, m_mem_norm, m_ffn2_norm), m_final_norm, m_rel_bias, m_sinks, m_conv_w),
        ((v_ffn1_norm, v_mix_norm, v_xattn_norm, v_mem_norm, v_ffn2_norm), v_final_norm, v_rel_bias, v_sinks, v_conv_w))]
    small_out = _adamw(small_parts, *packed, "adamw_small")
    ex.sent_apart("gu1", after=[dx, small_out[1]] + [big_out[k][1] for k in big_out])
    big_out["gu1"] = _adamw(ex.received["gu1"], *local["gu1"], "adamw_gu1")
    big_out = {k: [t.T if k in transposed else t for t in big_out[k]] for k in order}

    def unpack(t):
        conv = lax.dynamic_slice(t[ROW_CONV:ROW_CONV + 3], (0, 128 * me), (3, 128))[None]
        nrel = REL_BUCKETS * N_HEADS
        return dict(ffn1_norm=t[0:1], mix_norm=t[1:2], xattn_norm=t[2:3], mem_norm=t[3:4], ffn2_norm=t[4:5],
                    final_norm=t[5], rel_bias=t[ROW_RELB, :nrel].reshape(REL_BUCKETS, N_HEADS),
                    sinks=t[ROW_RELB:ROW_RELB + 1, nrel:nrel + N_HEADS], conv_w=conv)

    names = dict(gu1="ffn1_w_gu", d1="ffn1_w_down", win="w_in", out="w_out", q="xattn_wq", kv="xattn_wkv",
                 o="xattn_wo", gu2="ffn2_w_gu", d2="ffn2_w_down")
    results = []
    for idx in range(4):
        leaves = unpack(small_out[idx])
        leaves.update({names[k]: big_out[k][idx][None] for k in order})
        results.append(leaves)
    weights = ("rel_bias", "ffn1_norm", "ffn1_w_gu", "ffn1_w_down", "mix_norm", "w_in", "sinks", "conv_w", "w_out",
               "xattn_norm", "mem_norm", "xattn_wq", "xattn_wkv", "xattn_wo", "ffn2_norm", "ffn2_w_gu", "ffn2_w_down",
               "final_norm")
    loss = small_out[0][ROW_LOSS, 0]
    return (loss, dx[None], *[leaves[n] for leaves in results for n in weights])
```

```python
import math

import numpy as np
import jax
import jax.numpy as jnp
from jax import lax
from jax.experimental import pallas as pl
from jax.experimental.pallas import tpu as pltpu

F32, BF16 = jnp.float32, jnp.bfloat16
MESH = pl.DeviceIdType.MESH

D = 1024
N_DEV = 8
D_FF = 2816
FS = D_FF // 4
HEAD = 64
N_HEADS, N_KV = 16, 4
BLK = 128
NQ, NKV = N_HEADS * HEAD, 2 * N_KV * HEAD
XH, XHD = 4, 256
REL_BUCKETS, REL_EXACT, REL_MAX_DIST = 32, 16, 128
EPS, NEG = 1e-6, -1e30
ADAM_LR, ADAM_B1, ADAM_B2, ADAM_EPS, ADAM_WD, ADAM_STEP = 0.001, 0.9, 0.999, 1e-08, 0.01, 10
VMEM_LIMIT_V7X = 56 * 2**20
SMALL_ROWS = 16
ROW_RELB, ROW_LOSS, ROW_CONV = 6, 7, 8


def _bucket_thresholds():
    n = np.arange(REL_MAX_DIST)
    nf = np.maximum(n, 1).astype(np.float32)
    large = REL_EXACT + (np.log(nf / np.float32(REL_EXACT)) / np.float32(math.log(REL_MAX_DIST / REL_EXACT))
                         * np.float32(REL_BUCKETS - REL_EXACT)).astype(np.int32)
    b = np.where(n < REL_EXACT, n, np.minimum(large, REL_BUCKETS - 1))
    return [int(np.argmax(b >= REL_EXACT + k)) for k in range(1, REL_BUCKETS - REL_EXACT)]


BUCKET_THRESHOLDS = _bucket_thresholds()


HBM_SPEC = pl.BlockSpec(memory_space=pl.ANY)


class _Carry:
    def __init__(self, ins, outs, sems, start, finish, mid=None, aliases=None):
        self.ins, self.outs, self.sems = list(ins), list(outs), list(sems)
        self.start, self.finish, self.mid, self.aliases = start, finish, mid, dict(aliases or {})


def _pcall(body, *, name, grid, in_specs, out_specs, out_shape, scratch=(), carry=None, aliases=None, behind=None):
    params = pltpu.CompilerParams(dimension_semantics=("arbitrary",) * len(grid), vmem_limit_bytes=VMEM_LIMIT_V7X)
    if carry is None and behind is not None:
        n_in = len(in_specs)
        call = pl.pallas_call(lambda *refs: body(*refs[:n_in], *refs[n_in + 1:]), name=name, grid=grid,
                              in_specs=list(in_specs) + [pl.BlockSpec((8, 128), lambda *_: (0, 0))],
                              out_specs=out_specs, out_shape=out_shape, scratch_shapes=list(scratch),
                              compiler_params=params, input_output_aliases=aliases or {})
        return lambda *args: call(*args, behind)
    if carry is None:
        return pl.pallas_call(body, name=name, grid=grid, in_specs=in_specs, out_specs=out_specs,
                              out_shape=out_shape, scratch_shapes=list(scratch), compiler_params=params,
                              input_output_aliases=aliases or {})
    assert aliases is None and behind is None, name
    single = not isinstance(out_shape, (list, tuple))
    own_specs, own_shapes = ([out_specs], [out_shape]) if single else (list(out_specs), list(out_shape))
    n_in, n_out, n_scr = len(in_specs), len(own_shapes), len(scratch)
    n_cin, n_cout = len(carry.ins), len(carry.outs)
    steps = math.prod(grid)
    mid_step = max(steps - 1 - max(steps // 8, 1), 0)

    def carrying(*refs):
        ins, refs = refs[:n_in], refs[n_in:]
        cins, refs = refs[:n_cin], refs[n_cin:]
        outs, refs = refs[:n_out], refs[n_out:]
        couts, refs = refs[:n_cout], refs[n_cout:]
        scr, csems = refs[:n_scr], refs[n_scr:]
        step = 0
        for axis, size in enumerate(grid):
            step = step * size + pl.program_id(axis)

        @pl.when(step == 0)
        def _():
            carry.start(cins, couts, csems)

        body(*ins, *outs, *scr)
        if carry.mid is not None:
            @pl.when(step == mid_step)
            def _():
                carry.mid(cins, couts, csems)

        @pl.when(step == steps - 1)
        def _():
            carry.finish(cins, couts, csems)

    call = pl.pallas_call(carrying, name=name, grid=grid, in_specs=list(in_specs) + [HBM_SPEC] * n_cin,
                          out_specs=own_specs + [HBM_SPEC] * n_cout, out_shape=own_shapes + carry.outs,
                          scratch_shapes=list(scratch) + carry.sems, compiler_params=params,
                          input_output_aliases={n_in + i: n_out + o for i, o in carry.aliases.items()})

    def run(*args):
        res = call(*args, *carry.ins)
        return (res[0] if single else res[:n_out]), res[n_out:]

    return run


def _run_alone(carry, name, after=()):
    n_cin, n_cout, n_after = len(carry.ins), len(carry.outs), len(after)

    def body(*refs):
        cins, refs = refs[:n_cin], refs[n_cin + n_after:]
        couts, csems = refs[:n_cout], refs[n_cout:]
        carry.start(cins, couts, csems)
        if carry.mid is not None:
            carry.mid(cins, couts, csems)
        carry.finish(cins, couts, csems)

    return pl.pallas_call(body, name=name, in_specs=[HBM_SPEC] * (n_cin + n_after), out_specs=[HBM_SPEC] * n_cout,
                          out_shape=carry.outs, scratch_shapes=carry.sems,
                          input_output_aliases=carry.aliases)(*carry.ins, *after)


def _dot(a, b):
    return jnp.dot(a, b, preferred_element_type=F32)


def _dot_nt(a, b):
    return lax.dot_general(a, b, (((1,), (1,)), ((), ())), preferred_element_type=F32)


def _dot_tn(a, b):
    return lax.dot_general(a, b, (((0,), (0,)), ((), ())), preferred_element_type=F32)


def _sds(shape, dtype):
    return jax.ShapeDtypeStruct(tuple(shape), dtype)


ROW_CHUNK = 256


def _row_chunks(tm):
    return [slice(r, min(r + ROW_CHUNK, tm)) for r in range(0, tm, ROW_CHUNK)]


def _carried(call, args, carry):
    return call(*args) if carry is not None else (call(*args), ())


def _rmsnorm(x, g, name, carry=None):
    m, d = x.shape
    tm = min(512, m)

    def body(x_ref, g_ref, h_ref):
        xv = x_ref[...]
        r = lax.rsqrt(jnp.mean(xv * xv, axis=-1, keepdims=True) + EPS)
        h_ref[...] = (xv * r * g_ref[...]).astype(BF16)

    call = _pcall(body, name=name, grid=(m // tm,), carry=carry,
                  in_specs=[pl.BlockSpec((tm, d), lambda i: (i, 0)), pl.BlockSpec((1, d), lambda i: (0, 0))],
                  out_specs=pl.BlockSpec((tm, d), lambda i: (i, 0)), out_shape=_sds((m, d), BF16))
    return _carried(call, (x, g), carry)


def _mm_nn(a, b, name, tm=1024, bt=False, carry=None):
    m, k = a.shape
    nj = b.shape[0]
    n = b.shape[1] if bt else b.shape[2]
    tm = min(tm, m)
    dot = _dot_nt if bt else _dot

    def body(a_ref, b_ref, o_ref):
        o_ref[...] = dot(a_ref[...], b_ref[...]).astype(BF16)

    call = _pcall(body, name=name, grid=(nj, m // tm),
                  in_specs=[pl.BlockSpec((tm, k), lambda j, i: (i, 0)),
                            pl.BlockSpec((None,) + b.shape[1:], lambda j, i: (j, 0, 0))],
                  out_specs=pl.BlockSpec((None, tm, n), lambda j, i: (j, i, 0)),
                  out_shape=_sds((nj, m, n), BF16), carry=carry)
    return _carried(call, (a, b), carry)


def _norm_mm(x, g, w, name):
    m, d = x.shape
    nj, _, n = w.shape

    def body(x_ref, g_ref, w_ref, h_ref, o_ref):
        xv = x_ref[...]
        r = lax.rsqrt(jnp.mean(xv * xv, axis=-1, keepdims=True) + EPS)
        h = (xv * r * g_ref[...]).astype(BF16)
        h_ref[...] = h
        for j in range(nj):
            o_ref[j] = _dot(h, w_ref[j]).astype(BF16)

    return _pcall(body, name=name, grid=(1,),
                  in_specs=[pl.BlockSpec((m, d), lambda i: (0, 0)), pl.BlockSpec((1, d), lambda i: (0, 0)),
                            pl.BlockSpec(w.shape, lambda i: (0, 0, 0))],
                  out_specs=[pl.BlockSpec((m, d), lambda i: (0, 0)), pl.BlockSpec((nj, m, n), lambda i: (0, 0, 0))],
                  out_shape=[_sds((m, d), BF16), _sds((nj, m, n), BF16)])(x, g, w)


def _norm_mm_bwd(dy, h, w, x, name):
    nj, m, n = dy.shape
    d = x.shape[1]

    def body(dy_ref, h_ref, w_ref, x_ref, dw_ref, dg_ref):
        dh = None
        for j in range(nj):
            dyb = dy_ref[j].astype(BF16)
            dw_ref[j] = _dot_tn(h_ref[...], dyb).astype(BF16)
            part = _dot_nt(dyb, w_ref[j])
            dh = part if dh is None else dh + part
        xv = x_ref[...]
        xh = xv * lax.rsqrt(jnp.mean(xv * xv, axis=-1, keepdims=True) + EPS)
        dg_ref[...] = jnp.sum(dh * xh, axis=0, keepdims=True)

    return _pcall(body, name=name, grid=(1,),
                  in_specs=[pl.BlockSpec((nj, m, n), lambda i: (0, 0, 0)), pl.BlockSpec((m, d), lambda i: (0, 0)),
                            pl.BlockSpec((nj, d, n), lambda i: (0, 0, 0)), pl.BlockSpec((m, d), lambda i: (0, 0))],
                  out_specs=[pl.BlockSpec((nj, d, n), lambda i: (0, 0, 0)), pl.BlockSpec((1, d), lambda i: (0, 0))],
                  out_shape=[_sds((nj, d, n), BF16), _sds((1, d), F32)])(dy, h, w, x)


def _load_once(src_hbm, dst_vmem, sem):
    @pl.when(pl.program_id(0) == 0)
    def _():
        load = pltpu.make_async_copy(src_hbm, dst_vmem, sem)
        load.start()
        load.wait()


def _resident(w):
    return [pltpu.VMEM(w.shape, w.dtype), pltpu.SemaphoreType.DMA(())]


def _ffn_up(h, w4, name, tm=512, carry=None):
    s, d = h.shape
    tm = min(tm, s)

    def body(h_ref, w_hbm, gu_ref, a_ref, w_ref, w_sem):
        _load_once(w_hbm, w_ref, w_sem)
        for p in range(4):
            for rows in _row_chunks(tm):
                hv = h_ref[rows, :]
                g = _dot_nt(hv, w_ref[0, p])
                u = _dot_nt(hv, w_ref[1, p])
                gu_ref[0, p, rows, :] = g.astype(BF16)
                gu_ref[1, p, rows, :] = u.astype(BF16)
                a_ref[p, rows, :] = (g * jax.nn.sigmoid(g) * u).astype(BF16)

    call = _pcall(body, name=name, grid=(s // tm,),
                  in_specs=[pl.BlockSpec((tm, d), lambda i: (i, 0)), HBM_SPEC],
                  out_specs=[pl.BlockSpec((2, 4, tm, FS), lambda i: (0, 0, i, 0)),
                             pl.BlockSpec((4, tm, FS), lambda i: (0, i, 0))],
                  out_shape=[_sds((2, 4, s, FS), BF16), _sds((4, s, FS), BF16)], scratch=_resident(w4), carry=carry)
    return _carried(call, (h, w4), carry)


N_SEG = 5
IN_PROJ_WEIGHTS = [pltpu.VMEM((NQ, D), BF16), pltpu.VMEM((NKV, D), BF16), pltpu.VMEM((N_SEG, D, D), BF16),
                   pltpu.SemaphoreType.DMA((2 + N_SEG,))]


def _load_in_proj(w_hbm, wq_ref, wkv_ref, wa_ref, sems):
    @pl.when(pl.program_id(0) == 0)
    def _():
        loads = [pltpu.make_async_copy(w_hbm.at[pl.ds(0, NQ)], wq_ref, sems.at[0]),
                 pltpu.make_async_copy(w_hbm.at[pl.ds(NQ, NKV)], wkv_ref, sems.at[1])]
        loads += [pltpu.make_async_copy(w_hbm.at[pl.ds(NQ + NKV + D * j, D)], wa_ref.at[j], sems.at[2 + j])
                  for j in range(N_SEG)]
        for load in loads:
            load.start()
        for load in loads:
            load.wait()


def _in_proj(h, w_in_t, name, tm=512, carry=None):
    s, d = h.shape
    tm = min(tm, s)

    def body(h_ref, w_hbm, pa_ref, q_ref, kv_ref, wq_ref, wkv_ref, wa_ref, sems):
        _load_in_proj(w_hbm, wq_ref, wkv_ref, wa_ref, sems)
        hv = h_ref[...]
        q_ref[...] = _dot_nt(hv, wq_ref[...]).astype(BF16)
        kv_ref[...] = _dot_nt(hv, wkv_ref[...]).astype(BF16)
        for j in range(N_SEG):
            pa_ref[j] = _dot_nt(hv, wa_ref[j]).astype(BF16)

    call = _pcall(body, name=name, grid=(s // tm,), carry=carry,
                  in_specs=[pl.BlockSpec((tm, d), lambda i: (i, 0)), HBM_SPEC],
                  out_specs=[pl.BlockSpec((N_SEG, tm, d), lambda i: (0, i, 0)),
                             pl.BlockSpec((tm, NQ), lambda i: (i, 0)), pl.BlockSpec((tm, NKV), lambda i: (i, 0))],
                  out_shape=[_sds((N_SEG, s, d), BF16), _sds((s, NQ), BF16), _sds((s, NKV), BF16)],
                  scratch=IN_PROJ_WEIGHTS)
    return _carried(call, (h, w_in_t), carry)


def _mm_res_norm(a, w, xres, gain, scale, name, tm=512, carry=None):
    npart, s, kp = a.shape
    tm = min(tm, s)

    def body(a_ref, w_ref, x_ref, g_ref, xo_ref, h_ref):
        for rows in _row_chunks(tm):
            acc = _dot(a_ref[0, rows, :], w_ref[0])
            for p in range(1, npart):
                acc = acc + _dot(a_ref[p, rows, :], w_ref[p])
            xn = x_ref[rows, :] + scale * acc
            xo_ref[rows, :] = xn
            r = lax.rsqrt(jnp.mean(xn * xn, axis=-1, keepdims=True) + EPS)
            h_ref[rows, :] = (xn * r * g_ref[...]).astype(BF16)

    call = _pcall(body, name=name, grid=(s // tm,),
                  in_specs=[pl.BlockSpec((npart, tm, kp), lambda i: (0, i, 0)),
                            pl.BlockSpec((npart, kp, D), lambda i: (0, 0, 0)),
                            pl.BlockSpec((tm, D), lambda i: (i, 0)),
                            pl.BlockSpec((1, D), lambda i: (0, 0))],
                  out_specs=[pl.BlockSpec((tm, D), lambda i: (i, 0)), pl.BlockSpec((tm, D), lambda i: (i, 0))],
                  out_shape=[_sds((s, D), F32), _sds((s, D), BF16)], carry=carry)
    return _carried(call, (a, w, xres, gain), carry)


def _ffn_down_loss(a, w, xres, gain, target, name, tm=512):
    npart, s, kp = a.shape
    tm = min(tm, s)

    def body(a_ref, w_ref, x_ref, g_ref, t_ref, dx_ref, dxb_ref, loss_ref, dg_ref):
        @pl.when(pl.program_id(0) == 0)
        def _():
            loss_ref[...] = jnp.zeros_like(loss_ref)
            dg_ref[...] = jnp.zeros_like(dg_ref)

        for rows in _row_chunks(tm):
            acc = _dot(a_ref[0, rows, :], w_ref[0])
            for p in range(1, npart):
                acc = acc + _dot(a_ref[p, rows, :], w_ref[p])
            xn = x_ref[rows, :] + 0.5 * acc
            r = lax.rsqrt(jnp.mean(xn * xn, axis=-1, keepdims=True) + EPS)
            xh = xn * r
            gv = g_ref[...]
            err = xh * gv - t_ref[rows, :]
            part = 0.5 * jnp.sum(jnp.mean(err * err, axis=-1, keepdims=True), axis=0, keepdims=True)
            dy = err * (1.0 / D)
            dyg = dy * gv
            dxn = r * (dyg - xh * jnp.mean(dyg * xh, axis=-1, keepdims=True))
            dx_ref[rows, :] = dxn
            dxb_ref[rows, :] = dxn.astype(BF16)
            loss_ref[...] += jnp.broadcast_to(part, loss_ref.shape)
            dg_ref[...] += jnp.sum(dy * xh, axis=0, keepdims=True)

    return _pcall(body, name=name, grid=(s // tm,),
                  in_specs=[pl.BlockSpec((npart, tm, kp), lambda i: (0, i, 0)),
                            pl.BlockSpec((npart, kp, D), lambda i: (0, 0, 0)),
                            pl.BlockSpec((tm, D), lambda i: (i, 0)),
                            pl.BlockSpec((1, D), lambda i: (0, 0)),
                            pl.BlockSpec((tm, D), lambda i: (i, 0))],
                  out_specs=[pl.BlockSpec((tm, D), lambda i: (i, 0)), pl.BlockSpec((tm, D), lambda i: (i, 0)),
                             pl.BlockSpec((8, 128), lambda i: (0, 0)), pl.BlockSpec((1, D), lambda i: (0, 0))],
                  out_shape=[_sds((s, D), F32), _sds((s, D), BF16), _sds((8, 128), F32), _sds((1, D), F32)],
                  )(a, w, xres, gain, target)


def _window_tiles():
    i = lax.broadcasted_iota(jnp.int32, (BLK, BLK), 0)
    j = lax.broadcasted_iota(jnp.int32, (BLK, BLK), 1)
    rel = (i - j) & (BLK - 1)
    large = jnp.full_like(rel, REL_EXACT)
    for t in BUCKET_THRESHOLDS:
        large = large + (rel >= t).astype(jnp.int32)
    return j <= i, jnp.where(rel < REL_EXACT, rel, large)


def _bias_build(rel_bias, name):
    def body(rb_ref, o_ref):
        _, bucket = _window_tiles()

        def per_head(h, carry):
            acc = jnp.zeros((BLK, BLK), F32)
            for b in range(REL_BUCKETS):
                acc = jnp.where(bucket == b, rb_ref[b, h], acc)
            o_ref[h] = acc
            return carry

        lax.fori_loop(0, N_HEADS, per_head, 0)

    return _pcall(body, name=name, grid=(1,),
                  in_specs=[pl.BlockSpec(memory_space=pltpu.SMEM)],
                  out_specs=pl.BlockSpec((N_HEADS, BLK, BLK), lambda i: (0, 0, 0)),
                  out_shape=_sds((N_HEADS, BLK, BLK), F32))(rel_bias)


def _bias_bwd(dbias, name):
    def body(db_ref, o_ref):
        _, bucket = _window_tiles()
        lane = lax.broadcasted_iota(jnp.int32, (N_HEADS, 128), 1)

        def per_bucket(b, out):
            mb = (bucket == b).astype(F32)
            per_col = jnp.sum(db_ref[...] * mb[None, :, :], axis=1)
            return jnp.where(lane == b, jnp.sum(per_col, axis=1, keepdims=True), out)

        o_ref[...] = lax.fori_loop(0, REL_BUCKETS, per_bucket, jnp.zeros((N_HEADS, 128), F32))

    return _pcall(body, name=name, grid=(1,),
                  in_specs=[pl.BlockSpec((N_HEADS, BLK, BLK), lambda i: (0, 0, 0))],
                  out_specs=pl.BlockSpec((N_HEADS, 128), lambda i: (0, 0)),
                  out_shape=_sds((N_HEADS, 128), F32))(dbias)


PAIR = 2 * HEAD
GROUP = N_HEADS // N_KV
SWA_SCALE = HEAD ** -0.5


def _window_masks(n):
    i = lax.broadcasted_iota(jnp.int32, (GROUP * BLK, BLK), 0) & (BLK - 1)
    j = lax.broadcasted_iota(jnp.int32, (GROUP * BLK, BLK), 1)
    return j <= i, jnp.logical_and(n == 0, j > i), j < HEAD


def _kv_twice(ref, base, g, low):
    slab = ref[:, base + PAIR * (g // 2): base + PAIR * (g // 2 + 1)]
    swapped = pltpu.roll(slab, HEAD, 1)
    return jnp.where(low, slab, swapped) if g % 2 == 0 else jnp.where(low, swapped, slab)


def _stack_heads(ref, g, low):
    parts = []
    for r in range(2):
        slab = ref[:, PAIR * (2 * g + r): PAIR * (2 * g + r + 1)]
        zero = jnp.zeros_like(slab)
        parts += [jnp.where(low, slab, zero), jnp.where(low, zero, slab)]
    return jnp.concatenate(parts, axis=0)


def _unstack_heads(t, low):
    return [jnp.where(low, t[2 * r * BLK:(2 * r + 1) * BLK], t[(2 * r + 1) * BLK:(2 * r + 2) * BLK])
            for r in range(2)]


def _head_rows(t, k):
    return t[k * BLK:(k + 1) * BLK]


def _per_head_column(values):
    head = lax.broadcasted_iota(jnp.int32, (GROUP * BLK, 1), 0) // BLK
    col = jnp.full((GROUP * BLK, 1), values[0], F32)
    for k in range(1, GROUP):
        col = jnp.where(head == k, values[k], col)
    return col


def _window_logits(q4, kc, kp, bias4, own, absent):
    sc = jnp.where(own, _dot_nt(q4, kc), _dot_nt(q4, kp)) * SWA_SCALE + bias4
    return jnp.where(absent, NEG, sc)


def _split_window(t, own):
    zero = jnp.zeros_like(t)
    return jnp.where(own, t, zero), jnp.where(own, zero, t)


def _swa_fwd(q, kv, bias, sinks, name, carry=None):
    s = q.shape[0]
    nb = s // BLK
    kvw = 2 * N_KV * HEAD

    def body(q_ref, kc_ref, kp_ref, b_ref, sk_ref, o_ref, lse_ref):
        own, absent, low4 = _window_masks(pl.program_id(0))
        low = low4[:BLK]
        lane = lax.broadcasted_iota(jnp.int32, (BLK, 128), 1)
        lse_t = jnp.zeros((BLK, 128), F32)
        for g in range(N_KV):
            q4 = _stack_heads(q_ref, g, low)
            kc, kp = _kv_twice(kc_ref, 0, g, low), _kv_twice(kp_ref, 0, g, low)
            vc, vp = _kv_twice(kc_ref, N_KV * HEAD, g, low), _kv_twice(kp_ref, N_KV * HEAD, g, low)
            bias4 = b_ref[GROUP * g:GROUP * (g + 1)].reshape(GROUP * BLK, BLK)
            sc = _window_logits(q4, kc, kp, bias4, own, absent)
            sk = _per_head_column([sk_ref[0, GROUP * g + k] for k in range(GROUP)])
            m = jnp.maximum(jnp.max(sc, axis=1, keepdims=True), sk)
            p = jnp.exp(sc - m)
            l = jnp.sum(p, axis=1, keepdims=True) + jnp.exp(sk - m)
            p_own, p_prev = _split_window(p.astype(BF16), own)
            out = (_dot(p_own, vc) + _dot(p_prev, vp)) * (1.0 / l)
            for r, slab in enumerate(_unstack_heads(out, low)):
                o_ref[:, PAIR * (2 * g + r): PAIR * (2 * g + r + 1)] = slab.astype(BF16)
            lse4 = m + jnp.log(l)
            for k in range(GROUP):
                lse_t = jnp.where(lane == GROUP * g + k, _head_rows(lse4, k), lse_t)
        lse_ref[...] = lse_t

    call = _pcall(body, name=name, grid=(nb,),
                  in_specs=[pl.BlockSpec((BLK, D), lambda n: (n, 0)),
                            pl.BlockSpec((BLK, kvw), lambda n: (n, 0)),
                            pl.BlockSpec((BLK, kvw), lambda n: (jnp.maximum(n - 1, 0), 0)),
                            pl.BlockSpec((N_HEADS, BLK, BLK), lambda n: (0, 0, 0)),
                            pl.BlockSpec(memory_space=pltpu.SMEM)],
                  out_specs=[pl.BlockSpec((BLK, D), lambda n: (n, 0)), pl.BlockSpec((BLK, 128), lambda n: (n, 0))],
                  out_shape=[_sds((s, D), BF16), _sds((s, 128), F32)], carry=carry)
    return _carried(call, (q, kv, kv, bias, sinks), carry)


def _fold_halves(t, g, low):
    folded = jnp.where(low, t, 0.0) + pltpu.roll(jnp.where(low, 0.0, t), HEAD, 1)
    return folded if g % 2 == 0 else pltpu.roll(folded, HEAD, 1)


def _swa_bwd(q, kv, attn, dattn, lse, bias, sinks, name, carry=None):
    s = q.shape[0]
    nb = s // BLK
    kvw = 2 * N_KV * HEAD
    voff = N_KV * HEAD

    def body(q_ref, kc_ref, kp_ref, o_ref, do_ref, lse_ref, b_ref, skrow_ref, dq_ref, dkv_ref, dbias_ref, dsk_ref,
             dq_hold, kv_hold, dq_new, kv_prev, kv_cur):
        n = pl.program_id(0)

        @pl.when(n == 0)
        def _():
            dbias_ref[...] = jnp.zeros_like(dbias_ref)
            dsk_ref[...] = jnp.zeros_like(dsk_ref)
            dq_hold[...] = jnp.zeros_like(dq_hold)
            kv_hold[...] = jnp.zeros_like(kv_hold)

        @pl.when(n < nb)
        def _():
            own, absent, low4 = _window_masks(n)
            low = low4[:BLK]
            lane = lax.broadcasted_iota(jnp.int32, (BLK, 128), 1)
            delta_t = jnp.zeros((BLK, 128), F32)
            ones = jnp.ones((PAIR, 128), BF16)
            for pair_of_kv in range(N_KV // 2):
                slab_grads = [jnp.zeros((BLK, PAIR), F32) for _ in range(4)]
                for g in (2 * pair_of_kv, 2 * pair_of_kv + 1):
                    q4, do4 = _stack_heads(q_ref, g, low), _stack_heads(do_ref, g, low)
                    kc, kp = _kv_twice(kc_ref, 0, g, low), _kv_twice(kp_ref, 0, g, low)
                    vc, vp = _kv_twice(kc_ref, voff, g, low), _kv_twice(kp_ref, voff, g, low)
                    o_slabs = [o_ref[:, PAIR * (2 * g + r): PAIR * (2 * g + r + 1)] for r in range(2)]
                    o4 = jnp.concatenate([o_slabs[0], o_slabs[0], o_slabs[1], o_slabs[1]], axis=0)
                    delta = _dot(do4 * o4, ones)
                    heads = range(GROUP * g, GROUP * (g + 1))
                    lse4 = jnp.concatenate([lse_ref[:, h:h + 1] for h in heads], axis=0)
                    bias4 = b_ref[GROUP * g:GROUP * (g + 1)].reshape(GROUP * BLK, BLK)
                    p = jnp.exp(_window_logits(q4, kc, kp, bias4, own, absent) - lse4)
                    dp = jnp.where(own, _dot_nt(do4, vc), _dot_nt(do4, vp))
                    ds = p * (dp - delta)
                    dbias_ref[GROUP * g:GROUP * (g + 1)] += ds.reshape(GROUP, BLK, BLK)
                    for k, h in enumerate(heads):
                        delta_t = jnp.where(lane == h, _head_rows(delta, k), delta_t)
                    ds_own, ds_prev = _split_window((ds * SWA_SCALE).astype(BF16), own)
                    p_own, p_prev = _split_window(p.astype(BF16), own)
                    dq4 = _dot(ds_own, kc) + _dot(ds_prev, kp)
                    for r, slab in enumerate(_unstack_heads(dq4, low)):
                        dq_new[:, PAIR * (2 * g + r): PAIR * (2 * g + r + 1)] = slab
                    grads = [_dot_tn(ds_own, q4), _dot_tn(ds_prev, q4), _dot_tn(p_own, do4), _dot_tn(p_prev, do4)]
                    slab_grads = [t + _fold_halves(dk, g, low) for t, dk in zip(slab_grads, grads)]
                ks = slice(PAIR * pair_of_kv, PAIR * (pair_of_kv + 1))
                vs = slice(voff + PAIR * pair_of_kv, voff + PAIR * (pair_of_kv + 1))
                kv_cur[:, ks], kv_prev[:, ks], kv_cur[:, vs], kv_prev[:, vs] = slab_grads
            dsk_ref[...] -= jnp.sum(jnp.exp(skrow_ref[...] - lse_ref[...]) * delta_t, axis=0, keepdims=True)

        @pl.when(n == nb)
        def _():
            kv_prev[...] = jnp.zeros_like(kv_prev)

        dq_ref[...] = dq_hold[...].astype(BF16)
        dkv_ref[...] = (kv_hold[...] + kv_prev[...]).astype(BF16)

        @pl.when(n < nb)
        def _():
            dq_hold[...] = dq_new[...]
            kv_hold[...] = kv_cur[...]

    def cur(n):
        return jnp.minimum(n, nb - 1)

    call = _pcall(body, name=name, grid=(nb + 1,), carry=carry,
                  in_specs=[pl.BlockSpec((BLK, D), lambda n: (cur(n), 0)),
                            pl.BlockSpec((BLK, kvw), lambda n: (cur(n), 0)),
                            pl.BlockSpec((BLK, kvw), lambda n: (jnp.maximum(cur(n) - 1, 0), 0)),
                            pl.BlockSpec((BLK, D), lambda n: (cur(n), 0)),
                            pl.BlockSpec((BLK, D), lambda n: (cur(n), 0)),
                            pl.BlockSpec((BLK, 128), lambda n: (cur(n), 0)),
                            pl.BlockSpec((N_HEADS, BLK, BLK), lambda n: (0, 0, 0)),
                            pl.BlockSpec((1, 128), lambda n: (0, 0))],
                  out_specs=[pl.BlockSpec((BLK, D), lambda n: (jnp.maximum(n - 1, 0), 0)),
                             pl.BlockSpec((BLK, kvw), lambda n: (jnp.maximum(n - 1, 0), 0)),
                             pl.BlockSpec((N_HEADS, BLK, BLK), lambda n: (0, 0, 0)),
                             pl.BlockSpec((1, 128), lambda n: (0, 0))],
                  out_shape=[_sds((s, D), BF16), _sds((s, kvw), BF16), _sds((N_HEADS, BLK, BLK), F32),
                             _sds((1, 128), F32)],
                  scratch=[pltpu.VMEM((BLK, D), F32), pltpu.VMEM((BLK, kvw), F32), pltpu.VMEM((BLK, D), F32),
                           pltpu.VMEM((BLK, kvw), F32), pltpu.VMEM((BLK, kvw), F32)])
    sink_row = jnp.pad(sinks, ((0, 0), (0, 128 - N_HEADS)))
    return _carried(call, (q, kv, kv, attn, dattn, lse, bias, sink_row), carry)


HALO = 16
CW = D


def _conv_taps(cu, halo_cu, first_tile):
    row = lax.broadcasted_iota(jnp.int32, cu.shape, 0)
    halo_cu = jnp.where(first_tile, 0.0, halo_cu)
    c1 = jnp.where(row == 0, halo_cu[HALO - 1:HALO], pltpu.roll(cu, 1, 0))
    c2 = jnp.where(row == 0, halo_cu[HALO - 2:HALO - 1],
                   jnp.where(row == 1, halo_cu[HALO - 1:HALO], pltpu.roll(cu, 2, 0)))
    return c1, c2


def _conv_merge_fwd(pa, attn, convw, name, ts=256, carry=None):
    _, s, _ = pa.shape
    ts = min(ts, s)
    hb = ts // HALO

    def body(pa_ref, hp_ref, at_ref, w_ref, o_ref):
        i = pl.program_id(1)
        cu = pa_ref[0].astype(F32) * pa_ref[2].astype(F32)
        c1, c2 = _conv_taps(cu, hp_ref[0].astype(F32) * hp_ref[2].astype(F32), i == 0)
        w = w_ref[...]
        c3 = w[0:1] * c2 + w[1:2] * c1 + w[2:3] * cu
        conv = pa_ref[1].astype(F32) * c3
        o_ref[...] = (jax.nn.sigmoid(pa_ref[3].astype(F32)) * at_ref[...].astype(F32)
                      + jax.nn.sigmoid(pa_ref[4].astype(F32)) * conv).astype(BF16)

    call = _pcall(body, name=name, grid=(D // CW, s // ts), carry=carry,
                  in_specs=[pl.BlockSpec((5, ts, CW), lambda c, i: (0, i, c)),
                            pl.BlockSpec((5, HALO, CW), lambda c, i: (0, jnp.maximum(i * hb - 1, 0), c)),
                            pl.BlockSpec((ts, CW), lambda c, i: (i, c)),
                            pl.BlockSpec((8, CW), lambda c, i: (0, c))],
                  out_specs=pl.BlockSpec((ts, CW), lambda c, i: (i, c)),
                  out_shape=_sds((s, D), BF16))
    return _carried(call, (pa, pa, attn, convw), carry)


def _conv_merge_bwd(dmerged, pa, attn, convw, name, ts=256, carry=None):
    _, s, _ = pa.shape
    ts = min(ts, s)
    hb = ts // HALO
    last_hb = s // HALO - 1

    def body(dm_ref, pa_ref, at_ref, w_ref, hp_ref, hn_ref, dmn_ref, dat_ref, dpa_ref, dw_ref):
        i = pl.program_id(1)
        last = i == pl.num_programs(1) - 1
        dm = dm_ref[...].astype(F32)
        cp, bp, u = pa_ref[0].astype(F32), pa_ref[1].astype(F32), pa_ref[2].astype(F32)
        sa = jax.nn.sigmoid(pa_ref[3].astype(F32))
        sc = jax.nn.sigmoid(pa_ref[4].astype(F32))
        at = at_ref[...].astype(F32)
        cu = cp * u
        c1, c2 = _conv_taps(cu, hp_ref[0].astype(F32) * hp_ref[2].astype(F32), i == 0)
        w = w_ref[...]
        c3 = w[0:1] * c2 + w[1:2] * c1 + w[2:3] * cu
        dconv = dm * sc
        dc3 = dconv * bp
        nxt = dmn_ref[...].astype(F32) * jax.nn.sigmoid(hn_ref[4].astype(F32)) * hn_ref[1].astype(F32)
        nxt = jnp.where(last, 0.0, nxt)
        row = lax.broadcasted_iota(jnp.int32, dc3.shape, 0)
        d1 = jnp.where(row == ts - 1, nxt[0:1], pltpu.roll(dc3, ts - 1, 0))
        d2 = jnp.where(row == ts - 2, nxt[0:1], jnp.where(row == ts - 1, nxt[1:2], pltpu.roll(dc3, ts - 2, 0)))
        dcu = w[2:3] * dc3 + w[1:2] * d1 + w[0:1] * d2
        dat_ref[...] = (dm * sa).astype(BF16)
        dpa_ref[0] = (dcu * u).astype(BF16)
        dpa_ref[1] = (dconv * c3).astype(BF16)
        dpa_ref[2] = (dcu * cp).astype(BF16)
        dpa_ref[3] = (dm * at * sa * (1.0 - sa)).astype(BF16)
        dpa_ref[4] = (dm * bp * c3 * sc * (1.0 - sc)).astype(BF16)

        @pl.when(i == 0)
        def _():
            dw_ref[...] = jnp.zeros_like(dw_ref)

        dw_ref[0:1, :] += jnp.sum(dc3 * c2, axis=0, keepdims=True)
        dw_ref[1:2, :] += jnp.sum(dc3 * c1, axis=0, keepdims=True)
        dw_ref[2:3, :] += jnp.sum(dc3 * cu, axis=0, keepdims=True)

    call = _pcall(body, name=name, grid=(D // CW, s // ts), carry=carry,
                  in_specs=[pl.BlockSpec((ts, CW), lambda c, i: (i, c)),
                            pl.BlockSpec((5, ts, CW), lambda c, i: (0, i, c)),
                            pl.BlockSpec((ts, CW), lambda c, i: (i, c)),
                            pl.BlockSpec((8, CW), lambda c, i: (0, c)),
                            pl.BlockSpec((5, HALO, CW), lambda c, i: (0, jnp.maximum(i * hb - 1, 0), c)),
                            pl.BlockSpec((5, HALO, CW), lambda c, i: (0, jnp.minimum((i + 1) * hb, last_hb), c)),
                            pl.BlockSpec((HALO, CW), lambda c, i: (jnp.minimum((i + 1) * hb, last_hb), c))],
                  out_specs=[pl.BlockSpec((ts, CW), lambda c, i: (i, c)),
                             pl.BlockSpec((5, ts, CW), lambda c, i: (0, i, c)),
                             pl.BlockSpec((8, CW), lambda c, i: (0, c))],
                  out_shape=[_sds((s, D), BF16), _sds((5, s, D), BF16), _sds((8, D), F32)])
    return _carried(call, (dmerged, pa, attn, convw, pa, pa, dmerged), carry)


def _xattn_fwd(q, kv, name, tq=1024):
    s, _ = q.shape
    nm = kv.shape[1]
    tq = min(tq, s)

    def body(q_ref, kv_ref, o_ref, lse_ref):
        lane = lax.broadcasted_iota(jnp.int32, (tq, 128), 1)
        lse_t = jnp.zeros((tq, 128), F32)
        for h in range(XH):
            hs = slice(XHD * h, XHD * (h + 1))
            sc = _dot_nt(q_ref[:, hs], kv_ref[h]) * (XHD ** -0.5)
            m = jnp.max(sc, axis=1, keepdims=True)
            p = jnp.exp(sc - m)
            l = jnp.sum(p, axis=1, keepdims=True)
            o_ref[:, hs] = (_dot(p.astype(BF16), kv_ref[XH + h]) * (1.0 / l)).astype(BF16)
            lse_t = jnp.where(lane == h, m + jnp.log(l), lse_t)
        lse_ref[...] = lse_t

    return _pcall(body, name=name, grid=(s // tq,),
                  in_specs=[pl.BlockSpec((tq, D), lambda i: (i, 0)), pl.BlockSpec((2 * XH, nm, XHD), lambda i: (0, 0, 0))],
                  out_specs=[pl.BlockSpec((tq, D), lambda i: (i, 0)), pl.BlockSpec((tq, 128), lambda i: (i, 0))],
                  out_shape=[_sds((s, D), BF16), _sds((s, 128), F32)])(q, kv)


def _xattn_bwd(q, kv, o, do, lse, name, tq=512, carry=None):
    s, _ = q.shape
    nm = kv.shape[1]
    tq = min(tq, s)

    def body(q_ref, kv_ref, o_ref, do_ref, lse_ref, dq_ref, dkv_ref):
        @pl.when(pl.program_id(0) == 0)
        def _():
            dkv_ref[...] = jnp.zeros_like(dkv_ref)

        for h in range(XH):
            hs = slice(XHD * h, XHD * (h + 1))
            qh, kh, vh, dob = q_ref[:, hs], kv_ref[h], kv_ref[XH + h], do_ref[:, hs]
            p = jnp.exp(_dot_nt(qh, kh) * (XHD ** -0.5) - lse_ref[:, h:h + 1])
            dp = _dot_nt(dob, vh)
            delta = jnp.sum(dob.astype(F32) * o_ref[:, hs].astype(F32), axis=1, keepdims=True)
            dsb = (p * (dp - delta) * (XHD ** -0.5)).astype(BF16)
            dq_ref[:, hs] = _dot(dsb, kh).astype(BF16)
            dkv_ref[h] += _dot_tn(dsb, qh)
            dkv_ref[XH + h] += _dot_tn(p.astype(BF16), dob)

    call = _pcall(body, name=name, grid=(s // tq,), carry=carry,
                  in_specs=[pl.BlockSpec((tq, D), lambda i: (i, 0)), pl.BlockSpec((2 * XH, nm, XHD), lambda i: (0, 0, 0)),
                            pl.BlockSpec((tq, D), lambda i: (i, 0)), pl.BlockSpec((tq, D), lambda i: (i, 0)),
                            pl.BlockSpec((tq, 128), lambda i: (i, 0))],
                  out_specs=[pl.BlockSpec((tq, D), lambda i: (i, 0)), pl.BlockSpec((2 * XH, nm, XHD), lambda i: (0, 0, 0))],
                  out_shape=[_sds((s, D), BF16), _sds((2 * XH, nm, XHD), F32)])
    return _carried(call, (q, kv, o, do, lse), carry)


def _ffn_down_bwd(dxb, wd4, gu4, name, tm=512, carry=None, behind=None):
    s, _ = dxb.shape
    tm = min(tm, s)

    def body(dx_ref, w_hbm, gu_ref, o_ref, w_ref, w_sem):
        _load_once(w_hbm, w_ref, w_sem)
        for p in range(4):
            for rows in _row_chunks(tm):
                da = _dot_nt(dx_ref[rows, :], w_ref[p])
                g = gu_ref[0, p, rows, :].astype(F32)
                u = gu_ref[1, p, rows, :].astype(F32)
                sg = jax.nn.sigmoid(g)
                t = da * sg
                o_ref[0, p, rows, :] = (t * u * (1.0 + g - g * sg)).astype(BF16)
                o_ref[1, p, rows, :] = (t * g).astype(BF16)

    block = pl.BlockSpec((2, 4, tm, FS), lambda i: (0, 0, i, 0))
    call = _pcall(body, name=name, grid=(s // tm,), carry=carry, behind=behind,
                  in_specs=[pl.BlockSpec((tm, D), lambda i: (i, 0)), HBM_SPEC, block],
                  out_specs=block, out_shape=_sds((2, 4, s, FS), BF16), scratch=_resident(wd4))
    return _carried(call, (dxb, wd4, gu4), carry)


def _mm_tn(a, b, name, scale=1.0, carry=None):
    pa_n, s, m = a.shape
    pb_n, _, n = b.shape
    po = max(pa_n, pb_n)
    tk = 1024
    if po == 1 and s > tk and s % tk == 0:
        def body_k(a_ref, b_ref, o_ref, acc_ref):
            k = pl.program_id(0)
            part = _dot_tn(a_ref[...], b_ref[...])

            @pl.when(k == 0)
            def _():
                acc_ref[...] = part

            @pl.when(k > 0)
            def _():
                acc_ref[...] += part

            @pl.when(k == s // tk - 1)
            def _():
                o_ref[...] = (scale * acc_ref[...]).astype(BF16)

        call = _pcall(body_k, name=name, grid=(s // tk,), carry=carry,
                      in_specs=[pl.BlockSpec((None, tk, m), lambda k: (0, k, 0)),
                                pl.BlockSpec((None, tk, n), lambda k: (0, k, 0))],
                      out_specs=pl.BlockSpec((None, m, n), lambda k: (0, 0, 0)),
                      out_shape=_sds((1, m, n), BF16), scratch=[pltpu.VMEM((m, n), F32)])
        return _carried(call, (a, b), carry)
    tn = n if po >= 4 else min(n, 256)

    def body(a_ref, b_ref, o_ref):
        o_ref[...] = (scale * _dot_tn(a_ref[...], b_ref[...])).astype(BF16)

    call = _pcall(body, name=name, grid=(po, n // tn), carry=carry,
                  in_specs=[pl.BlockSpec((None, s, m), lambda o, j: (o if pa_n > 1 else 0, 0, 0)),
                            pl.BlockSpec((None, s, tn), lambda o, j: (o if pb_n > 1 else 0, 0, j))],
                  out_specs=pl.BlockSpec((None, m, tn), lambda o, j: (o, 0, j)),
                  out_shape=_sds((po, m, n), BF16))
    return _carried(call, (a, b), carry)


def _mm_tn_rows(a, b, name, total_rows, row0, begun=None, tm=512, carry=None):
    p, s, m = a.shape
    n = b.shape[1]
    tm = min(tm, m)
    tiles = m // tm
    assert row0 % tm == 0 and m % tm == 0, (row0, m, tm)

    def body(a_ref, b_ref, *rest):
        rest[-1][...] = _dot_tn(a_ref[...], b_ref[...]).astype(BF16)

    in_specs = [pl.BlockSpec((None, s, tm), lambda o, i: (o, 0, i)), pl.BlockSpec((s, n), lambda o, i: (0, 0))]
    call = _pcall(body, name=name, grid=(p, tiles), in_specs=in_specs + ([HBM_SPEC] if begun is not None else []),
                  out_specs=pl.BlockSpec((tm, n), lambda o, i: (row0 // tm + o * tiles + i, 0)),
                  out_shape=_sds((total_rows, n), BF16), aliases={2: 0} if begun is not None else None, carry=carry)
    return _carried(call, (a, b, begun) if begun is not None else (a, b), carry)


def _sum_dots(a_ref, b_ref, nj, bt, rows=slice(None)):
    dot = _dot_nt if bt else _dot
    acc = dot(a_ref[0, rows, :], b_ref[0])
    for j in range(1, nj):
        acc = acc + dot(a_ref[j, rows, :], b_ref[j])
    return acc


def _mm_acc(a, b, name, out_dtype, tm=1024, bt=False, carry=None):
    nj, s, k = a.shape
    n = b.shape[1] if bt else b.shape[2]
    tm = min(tm, s)

    def body(a_ref, b_ref, o_ref):
        o_ref[...] = _sum_dots(a_ref, b_ref, nj, bt).astype(out_dtype)

    call = _pcall(body, name=name, grid=(s // tm,), carry=carry,
                  in_specs=[pl.BlockSpec((nj, tm, k), lambda i: (0, i, 0)),
                            pl.BlockSpec(b.shape, lambda i: (0, 0, 0))],
                  out_specs=pl.BlockSpec((tm, n), lambda i: (i, 0)), out_shape=_sds((s, n), out_dtype))
    return _carried(call, (a, b), carry)


def _rms_bwd_call(name, acts, weights, scratch, load, dh_rows, *, x, gain, dres, tm, carry, behind=None):
    s, n = x.shape
    tm = min(tm, s)
    n_act, n_w = len(acts), len(weights)

    def body(*refs):
        act_refs, w_refs = refs[:n_act], refs[n_act:n_act + n_w]
        x_ref, g_ref, r_ref, dx_ref, dxb_ref, dg_ref = refs[n_act + n_w:n_act + n_w + 6]
        held = refs[n_act + n_w + 6:]
        load(w_refs, held)

        @pl.when(pl.program_id(0) == 0)
        def _():
            dg_ref[...] = jnp.zeros_like(dg_ref)

        for rows in _row_chunks(tm):
            dh = dh_rows(act_refs, held, rows)
            xv = x_ref[rows, :]
            r = lax.rsqrt(jnp.mean(xv * xv, axis=-1, keepdims=True) + EPS)
            xh = xv * r
            dyg = dh * g_ref[...]
            dx = r_ref[rows, :] + r * (dyg - xh * jnp.mean(dyg * xh, axis=-1, keepdims=True))
            dx_ref[rows, :] = dx
            dxb_ref[rows, :] = dx.astype(BF16)
            dg_ref[...] += jnp.sum(dh * xh, axis=0, keepdims=True)

    def tile(a):
        return (pl.BlockSpec((tm, a.shape[1]), lambda i: (i, 0)) if a.ndim == 2
                else pl.BlockSpec((a.shape[0], tm, a.shape[2]), lambda i: (0, i, 0)))

    row = pl.BlockSpec((tm, n), lambda i: (i, 0))
    in_specs = [tile(a) for a in acts] + [HBM_SPEC] * n_w + [row, pl.BlockSpec((1, n), lambda i: (0, 0)), row]
    call = _pcall(body, name=name, grid=(s // tm,), in_specs=in_specs, carry=carry, behind=behind,
                  out_specs=[row, row, pl.BlockSpec((1, n), lambda i: (0, 0))],
                  out_shape=[_sds((s, n), F32), _sds((s, n), BF16), _sds((1, n), F32)], scratch=scratch)
    return _carried(call, tuple(acts) + tuple(weights) + (x, gain, dres), carry)


def _mm_acc_rms_bwd(a, b, name, *, x, gain, dres, scale=None, tm=512, bt=False, carry=None, behind=None):
    def load(w_refs, held):
        _load_once(w_refs[0], held[0], held[1])

    def dh_rows(act_refs, held, rows):
        dh = _sum_dots(act_refs[0], held[0], a.shape[0], bt, rows)
        return dh if scale is None else scale * dh

    return _rms_bwd_call(name, [a], [b], _resident(b), load, dh_rows, x=x, gain=gain, dres=dres, tm=tm, carry=carry,
                         behind=behind)


def _in_proj_bwd(dpa, dq, dkv, w_in_t, name, *, x, gain, dres, tm=512, carry=None, behind=None):
    def load(w_refs, held):
        _load_in_proj(w_refs[0], *held)

    def dh_rows(act_refs, held, rows):
        dpa_ref, dq_ref, dkv_ref = act_refs
        wq_ref, wkv_ref, wa_ref, _ = held
        dh = _dot(dq_ref[rows, :], wq_ref[...]) + _dot(dkv_ref[rows, :], wkv_ref[...])
        return dh + _sum_dots(dpa_ref, wa_ref, N_SEG, False, rows)

    return _rms_bwd_call(name, [dpa, dq, dkv], [w_in_t], IN_PROJ_WEIGHTS, load, dh_rows, x=x, gain=gain, dres=dres,
                         tm=tm, carry=carry, behind=behind)


def _adam(w, g, m, v):
    m2 = ADAM_B1 * m + (1.0 - ADAM_B1) * g
    v2 = ADAM_B2 * v + (1.0 - ADAM_B2) * (g * g)
    m_hat = m2 / (1.0 - ADAM_B1 ** ADAM_STEP)
    v_hat = v2 / (1.0 - ADAM_B2 ** ADAM_STEP)
    delta = -ADAM_LR * (m_hat / (jnp.sqrt(v_hat) + ADAM_EPS) + ADAM_WD * w)
    return delta, m2, v2


def _adamw(parts, w, m, v, name, behind=None):
    _, r, c = parts.shape
    tr = max(t for t in range(16, 257, 16) if r % t == 0)

    def body(p_ref, w_ref, m_ref, v_ref, g_ref, d_ref, m2_ref, v2_ref):
        g = p_ref[0].astype(F32)
        for i in range(1, N_DEV):
            g = g + p_ref[i].astype(F32)
        delta, m2, v2 = _adam(w_ref[...], g, m_ref[...], v_ref[...])
        g_ref[...] = g
        d_ref[...] = delta
        m2_ref[...] = m2
        v2_ref[...] = v2

    blk = pl.BlockSpec((tr, c), lambda i: (i, 0))
    return _pcall(body, name=name, grid=(r // tr,), behind=behind,
                  in_specs=[pl.BlockSpec((N_DEV, tr, c), lambda i: (0, i, 0)), blk, blk, blk],
                  out_specs=[blk] * 4, out_shape=[_sds((r, c), F32)] * 4)(parts, w, m, v)


def _position():
    return lax.axis_index("x"), lax.axis_index("y"), lax.axis_index("c")


def _slot(px, py, pc):
    return 4 * px + 2 * py + pc


def _row_window(ref, rows):
    r0, r1 = rows
    return ref if (r0, r1) == (0, ref.shape[0]) else ref.at[pl.ds(r0, r1 - r0)]


def _split_items(items):
    sources = [src for src, _, _ in items]
    begun = [(a, dest) for a, (_, _, dest) in enumerate(items) if dest is not None]
    aliases = {len(sources) + k: a for k, (a, _) in enumerate(begun)}
    return sources + [dest for _, dest in begun], [rows for _, rows, _ in items], aliases


def _gather_carry(items):
    na = len(items)
    carry_ins, windows, aliases = _split_items(items)

    def plan(ins, outs, sems):
        send_sems, recv_sems, local_sems = sems
        x, y, c = _position()
        me, sibling = (x, y, c), (x, y, 1 - c)
        chips = [(1 - x, y), (x, 1 - y), (1 - x, 1 - y)]
        ins = [_row_window(ins[a], windows[a]) for a in range(na)]

        def block_rows(a, block):
            return _row_window(outs[a].at[_slot(*block)], windows[a])

        def copy(a, k, block, to, src=None):
            rows = block_rows(a, block)
            return pltpu.make_async_remote_copy(src_ref=rows if src is None else src, dst_ref=rows,
                                                send_sem=send_sems.at[k, a], recv_sem=recv_sems.at[k, a],
                                                device_id=to, device_id_type=MESH)

        mine = [pltpu.make_async_copy(ins[a], block_rows(a, me), local_sems.at[a]) for a in range(na)]
        first = [copy(a, 0, me, sibling, src=ins[a]) for a in range(na)]
        for j, chip in enumerate(chips):
            first += [copy(a, 1 + j, me, (*chip, c), src=ins[a]) for a in range(na)]
        landed = [[copy(a, 1 + j, (*chip, c), me) for a in range(na)] for j, chip in enumerate(chips)]
        passed = [[copy(a, 4 + j, (*chip, c), sibling) for a in range(na)] for j, chip in enumerate(chips)]
        from_sibling = [copy(a, 0, sibling, me) for a in range(na)]
        for j, chip in enumerate(chips):
            from_sibling += [copy(a, 4 + j, (*chip, 1 - c), me) for a in range(na)]
        return mine, first, landed, passed, from_sibling

    def start(ins, outs, sems):
        mine, first, _, _, _ = plan(ins, outs, sems)
        for cp in mine + first:
            cp.start()

    def mid(ins, outs, sems):
        _, _, landed, passed, _ = plan(ins, outs, sems)
        for over_ici, onward in zip(landed, passed):
            for cp, fwd in zip(over_ici, onward):
                cp.wait_recv()
                fwd.start()

    def finish(ins, outs, sems):
        mine, first, _, passed, from_sibling = plan(ins, outs, sems)
        for cp in from_sibling:
            cp.wait_recv()
        for cp in first + [fwd for onward in passed for fwd in onward]:
            cp.wait_send()
        for cp in mine:
            cp.wait()

    return _Carry(carry_ins, [_sds((N_DEV,) + src.shape, src.dtype) for src, _, _ in items],
                  [pltpu.SemaphoreType.DMA((7, na)), pltpu.SemaphoreType.DMA((7, na)),
                   pltpu.SemaphoreType.DMA((na,))], start, finish, mid, aliases)


def _exchange_carry(scattered, replicated=()):
    items = list(scattered) + [(a, (0, a.shape[0]), None) for a in replicated]
    na, ns = len(items), len(scattered)
    carry_ins, windows, aliases = _split_items(items)

    def plan(ins, outs, sems):
        send_sems, recv_sems, local_sems = sems
        me = _slot(*_position())

        def source(a, j):
            return _row_window(ins[a].at[j] if a < ns else ins[a], windows[a])

        def copy(a, j, i):
            return pltpu.make_async_remote_copy(src_ref=source(a, j), dst_ref=_row_window(outs[a].at[i], windows[a]),
                                                send_sem=send_sems.at[j, a], recv_sem=recv_sems.at[i, a],
                                                device_id=(j >> 2, (j >> 1) & 1, j & 1), device_id_type=MESH)

        def own(a, j):
            return pltpu.make_async_copy(source(a, j), _row_window(outs[a].at[j], windows[a]), local_sems.at[a])

        return me, copy, own

    def start(ins, outs, sems):
        me, copy, own = plan(ins, outs, sems)
        for a in range(na):
            for j in range(N_DEV):
                @pl.when(me == j)
                def _():
                    own(a, j).start()

                @pl.when(me != j)
                def _():
                    copy(a, j, me).start()

    def finish(ins, outs, sems):
        me, copy, own = plan(ins, outs, sems)
        for a in range(na):
            for j in range(N_DEV):
                @pl.when(me == j)
                def _():
                    for i in range(N_DEV):
                        if i != j:
                            copy(a, j, i).wait_recv()
                    own(a, j).wait()

                @pl.when(me != j)
                def _():
                    copy(a, j, me).wait_send()

    return _Carry(carry_ins, [_sds((N_DEV,) + src.shape[-2:], src.dtype) for src, _, _ in items],
                  [pltpu.SemaphoreType.DMA((N_DEV, na)), pltpu.SemaphoreType.DMA((N_DEV, na)),
                   pltpu.SemaphoreType.DMA((na,))], start, finish, None, aliases)


HBM_ARRAY = pl.BlockSpec(memory_space=pltpu.HBM)
SEMAPHORES = pl.BlockSpec(memory_space=pltpu.SEMAPHORE)
DATAFLOW = pltpu.SideEffectType.DATAFLOW_SIDE_EFFECTING


def _exchange_copy(parts_ref, land_ref, send_sems, recv_sems, me, j):
    return pltpu.make_async_remote_copy(src_ref=parts_ref.at[j], dst_ref=land_ref.at[me], send_sem=send_sems.at[j],
                                        recv_sem=recv_sems.at[me], device_id=(j >> 2, (j >> 1) & 1, j & 1),
                                        device_id_type=MESH)


def _exchange_start(parts, name):
    def body(parts_ref, land_ref, send_sems, recv_sems, parts_thru, land_thru, token):
        me = _slot(*_position())
        for j in range(N_DEV):
            @pl.when(me == j)
            def _():
                pltpu.make_async_copy(parts_ref.at[j], land_ref.at[j], send_sems.at[j]).start()

            @pl.when(me != j)
            def _():
                _exchange_copy(parts_ref, land_ref, send_sems, recv_sems, me, j).start()
        token[...] = jnp.zeros_like(token)

    return pl.pallas_call(
        body, name=name,
        out_shape=(pltpu.SemaphoreType.DMA((N_DEV,)), pltpu.SemaphoreType.DMA((N_DEV,)),
                   pltpu.HBM(parts.shape, parts.dtype), pltpu.HBM(parts.shape, parts.dtype), _sds((8, 128), F32)),
        in_specs=(HBM_ARRAY, HBM_ARRAY),
        out_specs=(SEMAPHORES, SEMAPHORES, HBM_ARRAY, HBM_ARRAY, pl.BlockSpec(memory_space=pltpu.VMEM)),
        input_output_aliases={0: 2, 1: 3}, compiler_params=pltpu.CompilerParams(has_side_effects=DATAFLOW),
    )(pltpu.with_memory_space_constraint(parts, pltpu.HBM),
      pltpu.with_memory_space_constraint(lax.empty(parts.shape, parts.dtype), pltpu.HBM))


def _exchange_wait(send_sems, recv_sems, parts_thru, land_thru, after, name):
    def body(parts_ref, land_ref, send_sems, recv_sems, *rest):
        me = _slot(*_position())
        for j in range(N_DEV):
            @pl.when(me == j)
            def _():
                pltpu.make_async_copy(parts_ref.at[j], land_ref.at[j], send_sems.at[j]).wait()

            @pl.when(me != j)
            def _():
                both = pltpu.make_async_remote_copy(src_ref=parts_ref.at[j], dst_ref=land_ref.at[j],
                                                    send_sem=send_sems.at[j], recv_sem=recv_sems.at[j],
                                                    device_id=(j >> 2, (j >> 1) & 1, j & 1), device_id_type=MESH)
                both.wait_send()
                both.wait_recv()

    return pl.pallas_call(
        body, name=name, out_shape=(pltpu.HBM(parts_thru.shape, parts_thru.dtype),
                                    pltpu.HBM(parts_thru.shape, parts_thru.dtype)),
        in_specs=(HBM_ARRAY, HBM_ARRAY, SEMAPHORES, SEMAPHORES) + (pl.BlockSpec(memory_space=pl.ANY),) * len(after),
        out_specs=(HBM_ARRAY, HBM_ARRAY), input_output_aliases={0: 0, 1: 1},
        compiler_params=pltpu.CompilerParams(has_side_effects=DATAFLOW),
    )(parts_thru, land_thru, send_sems, recv_sems, *after)[1]


class _Mesh:
    def __init__(self, shards):
        self.shards, self.full, self.received, self.cache, self.pending, self.tokens = shards, {}, {}, {}, {}, {}

    def fetch(self, wanted):
        items = []
        for want in wanted:
            name, r0, r1 = want if isinstance(want, tuple) else (want, 0, self.shards[want].shape[0])
            items.append((self.shards[name], (r0, r1), self.full.get(name)))
        return _gather_carry(items)

    def fetched(self, wanted, results):
        self.full.update(zip([want[0] if isinstance(want, tuple) else want for want in wanted], results))

    def send(self, *payloads):
        return _exchange_carry([(parts, rows or (0, parts.shape[1]), self.received.get(name))
                                for name, parts, rows in payloads])

    def sent(self, names, results):
        self.received.update(zip(names, results))

    def send_apart(self, name, parts):
        *self.pending[name], self.tokens[name] = _exchange_start(parts, "exchange_" + name + "_start")
        return self.tokens[name]

    def sent_apart(self, name, after):
        self.received[name] = _exchange_wait(*self.pending.pop(name), after, "exchange_" + name + "_wait")

    def w(self, key):
        if key not in self.cache:
            self.cache[key] = self._layout(key)
        return self.cache[key]

    def _layout(self, key):
        if key in ("gu1", "gu2"):
            return self.full[key]
        if key in ("d1", "d2"):
            return self.full[key].reshape(4, FS, D)
        if key in ("out", "q", "o"):
            return self.full[key].reshape(D, D)
        if key == "kv":
            return self.full["kv"]
        if key == "convw":
            rows = self.full["conv"][:, :3, :].transpose(1, 0, 2).reshape(3, D)
            return jnp.concatenate([rows, jnp.zeros((5, D), F32)], axis=0)
        assert key == "win_t", key
        return self.full["win"].reshape(-1, D)


def _forward_backward(x, mem, target, g, rel_bias, sinks, ex):
    s = x.shape[0]
    def fetching(wanted, call, *args, **kw):
        res, got = call(*args, carry=ex.fetch(wanted), **kw)
        ex.fetched(wanted, got)
        return res

    h1 = fetching(["gu1", "conv"], _rmsnorm, x, g["ffn1"], "norm_ffn1")
    gu1, a1 = fetching(["d1", ("win", 0, 400)], _ffn_up, h1, ex.w("gu1").reshape(2, 4, FS, D), "ffn1_up")
    x1, h2 = fetching([("win", 400, 832)], _mm_res_norm, a1, ex.w("d1"), x, g["mix"], 0.5, "ffn1_down")
    pa, q, kv = fetching(["gu2"], _in_proj, h2, ex.w("win_t"), "in_proj")
    biasm = _bias_build(rel_bias, "bias_build")
    attn, lse = fetching(["out", "kv", "o"], _swa_fwd, q, kv, biasm, sinks, "swa_fwd")
    merged = fetching(["q"], _conv_merge_fwd, pa, attn, ex.w("convw"), "conv_merge_fwd")
    (x2, h3), _ = _mm_res_norm(merged[None], ex.w("out")[None], x1, g["xattn"], 1.0, "out_proj")
    q2 = _mm_nn(h3, ex.w("q")[None], "xattn_q")[0][0]
    mh, kv2 = _norm_mm(mem, g["mem"], ex.w("kv"), "xattn_kv")
    o, lse2 = _xattn_fwd(q2, kv2, "xattn_fwd")
    (x3, h4), _ = _mm_res_norm(o[None], ex.w("o")[None], x2, g["ffn2"], 1.0, "xattn_o")
    gu2, a2 = fetching(["d2"], _ffn_up, h4, ex.w("gu2").reshape(2, 4, FS, D), "ffn2_up")
    dx4, dx4b, loss, d_final = _ffn_down_loss(a2, ex.w("d2"), x3, g["final"], target, "ffn2_down_loss")
    def sending(payloads, call, *args, **kw):
        res, got = call(*args, carry=ex.send(*payloads), **kw)
        ex.sent([name for name, _, _ in payloads], got)
        return res

    dw_d2 = _mm_tn(a2, dx4b[None], "dw_ffn2_down", scale=0.5)[0].reshape(N_DEV, -1, D)
    dgu2 = sending([("d2", dw_d2, (0, 288))], _ffn_down_bwd, dx4b, ex.w("d2"), gu2, "ffn2_down_bwd").reshape(8, s, FS)
    dw_gu2 = sending([("d2", dw_d2, (288, 352))], _mm_tn, dgu2, h4[None], "dw_ffn2_up", scale=0.5)
    dx3, dx3b, d_ffn2 = sending([("gu2", dw_gu2, (0, 368))], _mm_acc_rms_bwd, dgu2, ex.w("gu2"), "ffn2_up_bwd",
                                x=x3, gain=g["ffn2"], dres=dx4, scale=0.5)
    do, _ = _mm_acc(dx3b[None], ex.w("o")[None], "xattn_o_bwd", BF16, bt=True)
    dw_o = _mm_tn(o[None], dx3b[None], "dw_xattn_o")[0].reshape(N_DEV, -1, D)
    (dq2, dkv2), _ = _xattn_bwd(q2, kv2, o, do, lse2, "xattn_bwd")
    dw_q = _mm_tn(h3[None], dq2[None], "dw_xattn_q")[0].reshape(N_DEV, -1, D)
    (dx2, dx2b, d_xattn), _ = _mm_acc_rms_bwd(dq2[None], ex.w("q")[None], "xattn_q_bwd", x=x2, gain=g["xattn"],
                                              dres=dx3, bt=True)
    dw_kv, d_mem = _norm_mm_bwd(dkv2, mh, ex.w("kv"), mem, "xattn_kv_bwd")
    dmerged, _ = _mm_acc(dx2b[None], ex.w("out")[None], "out_proj_bwd", BF16, bt=True)
    dw_out = _mm_tn(merged[None], dx2b[None], "dw_out_proj")[0].reshape(N_DEV, -1, D)
    dattn, dpa, d_convw = sending([("kv", dw_kv, None)], _conv_merge_bwd,
                                  dmerged, pa, attn, ex.w("convw"), "conv_merge_bwd")
    dq, dkv, dbias, d_sinks = sending([("gu2", dw_gu2, (368, FS)), ("out", dw_out, None)], _swa_bwd,
                                      q, kv, attn, dattn, lse, biasm, sinks, "swa_bwd")
    d_relb = _bias_bwd(dbias, "bias_bwd")
    w_rows = ex.w("win_t").shape[0]
    dw_in = sending([("o", dw_o, None), ("q", dw_q, None)], _mm_tn_rows, dpa, h2, "dw_in_proj_a", w_rows, NQ + NKV)
    dw_in = _mm_tn_rows(dq[None], h2, "dw_in_proj_q", w_rows, 0, begun=dw_in)[0]
    dw_in = _mm_tn_rows(dkv[None], h2, "dw_in_proj_kv", w_rows, NQ, begun=dw_in)[0].reshape(N_DEV, -1, D)
    (dx1, dx1b, d_mix), _ = _in_proj_bwd(dpa, dq, dkv, ex.w("win_t"), "in_proj_bwd", x=x1, gain=g["mix"], dres=dx2,
                                         behind=ex.send_apart("win", dw_in))
    dw_d1 = _mm_tn(a1, dx1b[None], "dw_ffn1_down", scale=0.5)[0].reshape(N_DEV, -1, D)
    dgu1 = _ffn_down_bwd(dx1b, ex.w("d1"), gu1, "ffn1_down_bwd", behind=ex.send_apart("d1", dw_d1))[0]
    dgu1 = dgu1.reshape(8, s, FS)
    dw_gu1 = _mm_tn(dgu1, h1[None], "dw_ffn1_up", scale=0.5)[0]
    (dx0, _, d_ffn1), _ = _mm_acc_rms_bwd(dgu1, ex.w("gu1"), "ffn1_up_bwd", x=x, gain=g["ffn1"], dres=dx1,
                                          scale=0.5, behind=ex.send_apart("gu1", dw_gu1))

    relb_row = jnp.concatenate([d_relb[:, :REL_BUCKETS].T.reshape(1, REL_BUCKETS * N_HEADS), d_sinks[:, :N_HEADS],
                                jnp.zeros((1, D - REL_BUCKETS * N_HEADS - N_HEADS), F32)], axis=1)
    loss_row = jnp.concatenate([loss[0:1, 0:1], jnp.zeros((1, D - 1), F32)], axis=1)
    small = jnp.concatenate([d_ffn1, d_mix, d_xattn, d_mem, d_ffn2, d_final, relb_row, loss_row, d_convw[0:3],
                             jnp.zeros((SMALL_ROWS - ROW_CONV - 3, D), F32)], axis=0)
    return dx0, small


def _pack_small(norms, final, relb, sinks, conv_local, me):
    relb_row = jnp.concatenate([relb.reshape(1, -1), sinks.reshape(1, -1),
                                jnp.zeros((1, D - REL_BUCKETS * N_HEADS - N_HEADS), F32)], axis=1)
    conv_rows = lax.dynamic_update_slice(jnp.zeros((3, D), F32), conv_local.reshape(3, -1), (0, 128 * me))
    return jnp.concatenate(list(norms) + [final.reshape(1, D), relb_row, jnp.zeros((1, D), F32), conv_rows,
                                          jnp.zeros((SMALL_ROWS - ROW_CONV - 3, D), F32)], axis=0)


def kernel(x, mem, positions, rel_bias, ffn1_norm, ffn1_w_gu, ffn1_w_down, mix_norm, w_in, sinks, conv_w, w_out, xattn_norm, mem_norm, xattn_wq, xattn_wkv, xattn_wo, ffn2_norm, ffn2_w_gu, ffn2_w_down, final_norm, loss_target, m_rel_bias, m_ffn1_norm, m_ffn1_w_gu, m_ffn1_w_down, m_mix_norm, m_w_in, m_sinks, m_conv_w, m_w_out, m_xattn_norm, m_mem_norm, m_xattn_wq, m_xattn_wkv, m_xattn_wo, m_ffn2_norm, m_ffn2_w_gu, m_ffn2_w_down, m_final_norm, v_rel_bias, v_ffn1_norm, v_ffn1_w_gu, v_ffn1_w_down, v_mix_norm, v_w_in, v_sinks, v_conv_w, v_w_out, v_xattn_norm, v_mem_norm, v_xattn_wq, v_xattn_wkv, v_xattn_wo, v_ffn2_norm, v_ffn2_w_gu, v_ffn2_w_down, v_final_norm):
    del positions
    me = _slot(*_position())
    big = dict(gu1=(ffn1_w_gu, m_ffn1_w_gu, v_ffn1_w_gu), d1=(ffn1_w_down, m_ffn1_w_down, v_ffn1_w_down),
               win=(w_in, m_w_in, v_w_in), out=(w_out, m_w_out, v_w_out), q=(xattn_wq, m_xattn_wq, v_xattn_wq),
               kv=(xattn_wkv, m_xattn_wkv, v_xattn_wkv), o=(xattn_wo, m_xattn_wo, v_xattn_wo),
               gu2=(ffn2_w_gu, m_ffn2_w_gu, v_ffn2_w_gu), d2=(ffn2_w_down, m_ffn2_w_down, v_ffn2_w_down))
    order = list(big)
    transposed = ("gu1", "gu2", "win")
    local = {k: tuple(t[0].T if k in transposed else t[0] for t in big[k]) for k in order}
    shards = {k: local[k][0].astype(BF16) for k in order}
    shards["conv"] = jnp.concatenate([conv_w[0], jnp.zeros((5, 128), F32)], axis=0)
    ex = _Mesh(shards)
    gains = dict(ffn1=ffn1_norm, mix=mix_norm, xattn=xattn_norm, mem=mem_norm, ffn2=ffn2_norm,
                 final=final_norm.reshape(1, D))
    dx, small = _forward_backward(x[0], mem[0], loss_target[0], gains, rel_bias, sinks, ex)
    apart = ("d1", "win", "gu1")
    big_out = {k: _adamw(ex.received[k], *local[k], "adamw_" + k, behind=ex.tokens["gu1"])
               for k in order if k not in apart}
    for k in apart[:-1]:
        ex.sent_apart(k, after=[big_out[j][1] for j in big_out])
        big_out[k] = _adamw(ex.received[k], *local[k], "adamw_" + k)
    small_parts = _run_alone(_exchange_carry([], [small]), "exchange_small", after=[big_out[k][1] for k in big_out])[0]
    packed = [_pack_small(norms, final, relb, sk, conv, me) for norms, final, relb, sk, conv in (
        ((ffn1_norm, mix_norm, xattn_norm, mem_norm, ffn2_norm), final_norm, rel_bias, sinks, conv_w),
        ((m_ffn1_norm, m_mix_norm, m_xattn_norm, m_mem_norm, m_ffn2_norm), m_final_norm, m_rel_bias, m_sinks, m_conv_w),
        ((v_ffn1_norm, v_mix_norm, v_xattn_norm, v_mem_norm, v_ffn2_norm), v_final_norm, v_rel_bias, v_sinks, v_conv_w))]
    small_out = _adamw(small_parts, *packed, "adamw_small")
    ex.sent_apart("gu1", after=[dx, small_out[1]] + [big_out[k][1] for k in big_out])
    big_out["gu1"] = _adamw(ex.received["gu1"], *local["gu1"], "adamw_gu1")
    big_out = {k: [t.T if k in transposed else t for t in big_out[k]] for k in order}

    def unpack(t):
        conv = lax.dynamic_slice(t[ROW_CONV:ROW_CONV + 3], (0, 128 * me), (3, 128))[None]
        nrel = REL_BUCKETS * N_HEADS
        return dict(ffn1_norm=t[0:1], mix_norm=t[1:2], xattn_norm=t[2:3], mem_norm=t[3:4], ffn2_norm=t[4:5],
                    final_norm=t[5], rel_bias=t[ROW_RELB, :nrel].reshape(REL_BUCKETS, N_HEADS),
                    sinks=t[ROW_RELB:ROW_RELB + 1, nrel:nrel + N_HEADS], conv_w=conv)

    names = dict(gu1="ffn1_w_gu", d1="ffn1_w_down", win="w_in", out="w_out", q="xattn_wq", kv="xattn_wkv",
                 o="xattn_wo", gu2="ffn2_w_gu", d2="ffn2_w_down")
    results = []
    for idx in range(4):
        leaves = unpack(small_out[idx])
        leaves.update({names[k]: big_out[k][idx][None] for k in order})
        results.append(leaves)
    weights = ("rel_bias", "ffn1_norm", "ffn1_w_gu", "ffn1_w_down", "mix_norm", "w_in", "sinks", "conv_w", "w_out",
               "xattn_norm", "mem_norm", "xattn_wq", "xattn_wkv", "xattn_wo", "ffn2_norm", "ffn2_w_gu", "ffn2_w_down",
               "final_norm")
    loss = small_out[0][ROW_LOSS, 0]
    return (loss, dx[None], *[leaves[n] for leaves in results for n in weights])
```

```python
import math

import numpy as np
import jax
import jax.numpy as jnp
from jax import lax
from jax.experimental import pallas as pl
from jax.experimental.pallas import tpu as pltpu

F32, BF16 = jnp.float32, jnp.bfloat16
MESH = pl.DeviceIdType.MESH

D = 1024
N_DEV = 8
D_FF = 2816
FS = D_FF // 4
HEAD = 64
N_HEADS, N_KV = 16, 4
BLK = 128
NQ, NKV = N_HEADS * HEAD, 2 * N_KV * HEAD
XH, XHD = 4, 256
REL_BUCKETS, REL_EXACT, REL_MAX_DIST = 32, 16, 128
EPS, NEG = 1e-6, -1e30
ADAM_LR, ADAM_B1, ADAM_B2, ADAM_EPS, ADAM_WD, ADAM_STEP = 0.001, 0.9, 0.999, 1e-08, 0.01, 10
VMEM_LIMIT_V7X = 56 * 2**20
SMALL_ROWS = 16
ROW_RELB, ROW_LOSS, ROW_CONV = 6, 7, 8


def _bucket_thresholds():
    n = np.arange(REL_MAX_DIST)
    nf = np.maximum(n, 1).astype(np.float32)
    large = REL_EXACT + (np.log(nf / np.float32(REL_EXACT)) / np.float32(math.log(REL_MAX_DIST / REL_EXACT))
                         * np.float32(REL_BUCKETS - REL_EXACT)).astype(np.int32)
    b = np.where(n < REL_EXACT, n, np.minimum(large, REL_BUCKETS - 1))
    return [int(np.argmax(b >= REL_EXACT + k)) for k in range(1, REL_BUCKETS - REL_EXACT)]


BUCKET_THRESHOLDS = _bucket_thresholds()


HBM_SPEC = pl.BlockSpec(memory_space=pl.ANY)


class _Carry:
    def __init__(self, ins, outs, sems, start, finish, mid=None, aliases=None):
        self.ins, self.outs, self.sems = list(ins), list(outs), list(sems)
        self.start, self.finish, self.mid, self.aliases = start, finish, mid, dict(aliases or {})


def _pcall(body, *, name, grid, in_specs, out_specs, out_shape, scratch=(), carry=None, aliases=None, behind=None):
    params = pltpu.CompilerParams(dimension_semantics=("arbitrary",) * len(grid), vmem_limit_bytes=VMEM_LIMIT_V7X)
    if carry is None and behind is not None:
        n_in = len(in_specs)
        call = pl.pallas_call(lambda *refs: body(*refs[:n_in], *refs[n_in + 1:]), name=name, grid=grid,
                              in_specs=list(in_specs) + [pl.BlockSpec((8, 128), lambda *_: (0, 0))],
                              out_specs=out_specs, out_shape=out_shape, scratch_shapes=list(scratch),
                              compiler_params=params, input_output_aliases=aliases or {})
        return lambda *args: call(*args, behind)
    if carry is None:
        return pl.pallas_call(body, name=name, grid=grid, in_specs=in_specs, out_specs=out_specs,
                              out_shape=out_shape, scratch_shapes=list(scratch), compiler_params=params,
                              input_output_aliases=aliases or {})
    assert aliases is None and behind is None, name
    single = not isinstance(out_shape, (list, tuple))
    own_specs, own_shapes = ([out_specs], [out_shape]) if single else (list(out_specs), list(out_shape))
    n_in, n_out, n_scr = len(in_specs), len(own_shapes), len(scratch)
    n_cin, n_cout = len(carry.ins), len(carry.outs)
    steps = math.prod(grid)
    mid_step = max(steps - 1 - max(steps // 8, 1), 0)

    def carrying(*refs):
        ins, refs = refs[:n_in], refs[n_in:]
        cins, refs = refs[:n_cin], refs[n_cin:]
        outs, refs = refs[:n_out], refs[n_out:]
        couts, refs = refs[:n_cout], refs[n_cout:]
        scr, csems = refs[:n_scr], refs[n_scr:]
        step = 0
        for axis, size in enumerate(grid):
            step = step * size + pl.program_id(axis)

        @pl.when(step == 0)
        def _():
            carry.start(cins, couts, csems)

        body(*ins, *outs, *scr)
        if carry.mid is not None:
            @pl.when(step == mid_step)
            def _():
                carry.mid(cins, couts, csems)

        @pl.when(step == steps - 1)
        def _():
            carry.finish(cins, couts, csems)

    call = pl.pallas_call(carrying, name=name, grid=grid, in_specs=list(in_specs) + [HBM_SPEC] * n_cin,
                          out_specs=own_specs + [HBM_SPEC] * n_cout, out_shape=own_shapes + carry.outs,
                          scratch_shapes=list(scratch) + carry.sems, compiler_params=params,
                          input_output_aliases={n_in + i: n_out + o for i, o in carry.aliases.items()})

    def run(*args):
        res = call(*args, *carry.ins)
        return (res[0] if single else res[:n_out]), res[n_out:]

    return run


def _run_alone(carry, name, after=()):
    n_cin, n_cout, n_after = len(carry.ins), len(carry.outs), len(after)

    def body(*refs):
        cins, refs = refs[:n_cin], refs[n_cin + n_after:]
        couts, csems = refs[:n_cout], refs[n_cout:]
        carry.start(cins, couts, csems)
        if carry.mid is not None:
            carry.mid(cins, couts, csems)
        carry.finish(cins, couts, csems)

    return pl.pallas_call(body, name=name, in_specs=[HBM_SPEC] * (n_cin + n_after), out_specs=[HBM_SPEC] * n_cout,
                          out_shape=carry.outs, scratch_shapes=carry.sems,
                          input_output_aliases=carry.aliases)(*carry.ins, *after)


def _dot(a, b):
    return jnp.dot(a, b, preferred_element_type=F32)


def _dot_nt(a, b):
    return lax.dot_general(a, b, (((1,), (1,)), ((), ())), preferred_element_type=F32)


def _dot_tn(a, b):
    return lax.dot_general(a, b, (((0,), (0,)), ((), ())), preferred_element_type=F32)


def _sds(shape, dtype):
    return jax.ShapeDtypeStruct(tuple(shape), dtype)


ROW_CHUNK = 256


def _row_chunks(tm):
    return [slice(r, min(r + ROW_CHUNK, tm)) for r in range(0, tm, ROW_CHUNK)]


def _carried(call, args, carry):
    return call(*args) if carry is not None else (call(*args), ())


def _rmsnorm(x, g, name, carry=None):
    m, d = x.shape
    tm = min(512, m)

    def body(x_ref, g_ref, h_ref):
        xv = x_ref[...]
        r = lax.rsqrt(jnp.mean(xv * xv, axis=-1, keepdims=True) + EPS)
        h_ref[...] = (xv * r * g_ref[...]).astype(BF16)

    call = _pcall(body, name=name, grid=(m // tm,), carry=carry,
                  in_specs=[pl.BlockSpec((tm, d), lambda i: (i, 0)), pl.BlockSpec((1, d), lambda i: (0, 0))],
                  out_specs=pl.BlockSpec((tm, d), lambda i: (i, 0)), out_shape=_sds((m, d), BF16))
    return _carried(call, (x, g), carry)


def _norm_mm(x, g, w, name):
    m, d = x.shape
    nj, _, n = w.shape

    def body(x_ref, g_ref, w_ref, h_ref, o_ref):
        xv = x_ref[...]
        r = lax.rsqrt(jnp.mean(xv * xv, axis=-1, keepdims=True) + EPS)
        h = (xv * r * g_ref[...]).astype(BF16)
        h_ref[...] = h
        for j in range(nj):
            o_ref[j] = _dot(h, w_ref[j]).astype(BF16)

    return _pcall(body, name=name, grid=(1,),
                  in_specs=[pl.BlockSpec((m, d), lambda i: (0, 0)), pl.BlockSpec((1, d), lambda i: (0, 0)),
                            pl.BlockSpec(w.shape, lambda i: (0, 0, 0))],
                  out_specs=[pl.BlockSpec((m, d), lambda i: (0, 0)), pl.BlockSpec((nj, m, n), lambda i: (0, 0, 0))],
                  out_shape=[_sds((m, d), BF16), _sds((nj, m, n), BF16)])(x, g, w)


def _norm_mm_bwd(dy, h, w, x, name):
    nj, m, n = dy.shape
    d = x.shape[1]

    def body(dy_ref, h_ref, w_ref, x_ref, dw_ref, dg_ref):
        dh = None
        for j in range(nj):
            dyb = dy_ref[j].astype(BF16)
            dw_ref[j] = _dot_tn(h_ref[...], dyb).astype(BF16)
            part = _dot_nt(dyb, w_ref[j])
            dh = part if dh is None else dh + part
        xv = x_ref[...]
        xh = xv * lax.rsqrt(jnp.mean(xv * xv, axis=-1, keepdims=True) + EPS)
        dg_ref[...] = jnp.sum(dh * xh, axis=0, keepdims=True)

    return _pcall(body, name=name, grid=(1,),
                  in_specs=[pl.BlockSpec((nj, m, n), lambda i: (0, 0, 0)), pl.BlockSpec((m, d), lambda i: (0, 0)),
                            pl.BlockSpec((nj, d, n), lambda i: (0, 0, 0)), pl.BlockSpec((m, d), lambda i: (0, 0))],
                  out_specs=[pl.BlockSpec((nj, d, n), lambda i: (0, 0, 0)), pl.BlockSpec((1, d), lambda i: (0, 0))],
                  out_shape=[_sds((nj, d, n), BF16), _sds((1, d), F32)])(dy, h, w, x)


def _load_once(src_hbm, dst_vmem, sem):
    @pl.when(pl.program_id(0) == 0)
    def _():
        load = pltpu.make_async_copy(src_hbm, dst_vmem, sem)
        load.start()
        load.wait()


def _resident(w):
    return [pltpu.VMEM(w.shape, w.dtype), pltpu.SemaphoreType.DMA(())]


def _ffn_up(h, w4, name, tm=512, carry=None):
    s, d = h.shape
    tm = min(tm, s)

    def body(h_ref, w_hbm, gu_ref, a_ref, w_ref, w_sem):
        _load_once(w_hbm, w_ref, w_sem)
        for p in range(4):
            for rows in _row_chunks(tm):
                hv = h_ref[rows, :]
                g = _dot_nt(hv, w_ref[0, p])
                u = _dot_nt(hv, w_ref[1, p])
                gu_ref[0, p, rows, :] = g.astype(BF16)
                gu_ref[1, p, rows, :] = u.astype(BF16)
                a_ref[p, rows, :] = (g * jax.nn.sigmoid(g) * u).astype(BF16)

    call = _pcall(body, name=name, grid=(s // tm,),
                  in_specs=[pl.BlockSpec((tm, d), lambda i: (i, 0)), HBM_SPEC],
                  out_specs=[pl.BlockSpec((2, 4, tm, FS), lambda i: (0, 0, i, 0)),
                             pl.BlockSpec((4, tm, FS), lambda i: (0, i, 0))],
                  out_shape=[_sds((2, 4, s, FS), BF16), _sds((4, s, FS), BF16)], scratch=_resident(w4), carry=carry)
    return _carried(call, (h, w4), carry)


N_SEG = 5
IN_PROJ_WEIGHTS = [pltpu.VMEM((NQ, D), BF16), pltpu.VMEM((NKV, D), BF16), pltpu.VMEM((N_SEG, D, D), BF16),
                   pltpu.SemaphoreType.DMA((2 + N_SEG,))]


def _load_in_proj(w_hbm, wq_ref, wkv_ref, wa_ref, sems):
    @pl.when(pl.program_id(0) == 0)
    def _():
        loads = [pltpu.make_async_copy(w_hbm.at[pl.ds(0, NQ)], wq_ref, sems.at[0]),
                 pltpu.make_async_copy(w_hbm.at[pl.ds(NQ, NKV)], wkv_ref, sems.at[1])]
        loads += [pltpu.make_async_copy(w_hbm.at[pl.ds(NQ + NKV + D * j, D)], wa_ref.at[j], sems.at[2 + j])
                  for j in range(N_SEG)]
        for load in loads:
            load.start()
        for load in loads:
            load.wait()


def _in_proj(h, w_in_t, name, tm=512, carry=None):
    s, d = h.shape
    tm = min(tm, s)

    def body(h_ref, w_hbm, pa_ref, q_ref, kv_ref, wq_ref, wkv_ref, wa_ref, sems):
        _load_in_proj(w_hbm, wq_ref, wkv_ref, wa_ref, sems)
        hv = h_ref[...]
        q_ref[...] = _dot_nt(hv, wq_ref[...]).astype(BF16)
        kv_ref[...] = _dot_nt(hv, wkv_ref[...]).astype(BF16)
        for j in range(N_SEG):
            pa_ref[j] = _dot_nt(hv, wa_ref[j]).astype(BF16)

    call = _pcall(body, name=name, grid=(s // tm,), carry=carry,
                  in_specs=[pl.BlockSpec((tm, d), lambda i: (i, 0)), HBM_SPEC],
                  out_specs=[pl.BlockSpec((N_SEG, tm, d), lambda i: (0, i, 0)),
                             pl.BlockSpec((tm, NQ), lambda i: (i, 0)), pl.BlockSpec((tm, NKV), lambda i: (i, 0))],
                  out_shape=[_sds((N_SEG, s, d), BF16), _sds((s, NQ), BF16), _sds((s, NKV), BF16)],
                  scratch=IN_PROJ_WEIGHTS)
    return _carried(call, (h, w_in_t), carry)


def _mm_res_norm(a, w, xres, gain, scale, name, tm=512, carry=None):
    npart, s, kp = a.shape
    tm = min(tm, s)

    def body(a_ref, w_ref, x_ref, g_ref, xo_ref, h_ref):
        for rows in _row_chunks(tm):
            acc = _dot(a_ref[0, rows, :], w_ref[0])
            for p in range(1, npart):
                acc = acc + _dot(a_ref[p, rows, :], w_ref[p])
            xn = x_ref[rows, :] + scale * acc
            xo_ref[rows, :] = xn
            r = lax.rsqrt(jnp.mean(xn * xn, axis=-1, keepdims=True) + EPS)
            h_ref[rows, :] = (xn * r * g_ref[...]).astype(BF16)

    call = _pcall(body, name=name, grid=(s // tm,),
                  in_specs=[pl.BlockSpec((npart, tm, kp), lambda i: (0, i, 0)),
                            pl.BlockSpec((npart, kp, D), lambda i: (0, 0, 0)),
                            pl.BlockSpec((tm, D), lambda i: (i, 0)),
                            pl.BlockSpec((1, D), lambda i: (0, 0))],
                  out_specs=[pl.BlockSpec((tm, D), lambda i: (i, 0)), pl.BlockSpec((tm, D), lambda i: (i, 0))],
                  out_shape=[_sds((s, D), F32), _sds((s, D), BF16)], carry=carry)
    return _carried(call, (a, w, xres, gain), carry)


def _ffn_down_loss(a, w, xres, gain, target, name, tm=512):
    npart, s, kp = a.shape
    tm = min(tm, s)

    def body(a_ref, w_ref, x_ref, g_ref, t_ref, dx_ref, dxb_ref, loss_ref, dg_ref):
        @pl.when(pl.program_id(0) == 0)
        def _():
            loss_ref[...] = jnp.zeros_like(loss_ref)
            dg_ref[...] = jnp.zeros_like(dg_ref)

        for rows in _row_chunks(tm):
            acc = _dot(a_ref[0, rows, :], w_ref[0])
            for p in range(1, npart):
                acc = acc + _dot(a_ref[p, rows, :], w_ref[p])
            xn = x_ref[rows, :] + 0.5 * acc
            r = lax.rsqrt(jnp.mean(xn * xn, axis=-1, keepdims=True) + EPS)
            xh = xn * r
            gv = g_ref[...]
            err = xh * gv - t_ref[rows, :]
            part = 0.5 * jnp.sum(jnp.mean(err * err, axis=-1, keepdims=True), axis=0, keepdims=True)
            dy = err * (1.0 / D)
            dyg = dy * gv
            dxn = r * (dyg - xh * jnp.mean(dyg * xh, axis=-1, keepdims=True))
            dx_ref[rows, :] = dxn
            dxb_ref[rows, :] = dxn.astype(BF16)
            loss_ref[...] += jnp.broadcast_to(part, loss_ref.shape)
            dg_ref[...] += jnp.sum(dy * xh, axis=0, keepdims=True)

    return _pcall(body, name=name, grid=(s // tm,),
                  in_specs=[pl.BlockSpec((npart, tm, kp), lambda i: (0, i, 0)),
                            pl.BlockSpec((npart, kp, D), lambda i: (0, 0, 0)),
                            pl.BlockSpec((tm, D), lambda i: (i, 0)),
                            pl.BlockSpec((1, D), lambda i: (0, 0)),
                            pl.BlockSpec((tm, D), lambda i: (i, 0))],
                  out_specs=[pl.BlockSpec((tm, D), lambda i: (i, 0)), pl.BlockSpec((tm, D), lambda i: (i, 0)),
                             pl.BlockSpec((8, 128), lambda i: (0, 0)), pl.BlockSpec((1, D), lambda i: (0, 0))],
                  out_shape=[_sds((s, D), F32), _sds((s, D), BF16), _sds((8, 128), F32), _sds((1, D), F32)],
                  )(a, w, xres, gain, target)


def _window_tiles():
    i = lax.broadcasted_iota(jnp.int32, (BLK, BLK), 0)
    j = lax.broadcasted_iota(jnp.int32, (BLK, BLK), 1)
    rel = (i - j) & (BLK - 1)
    large = jnp.full_like(rel, REL_EXACT)
    for t in BUCKET_THRESHOLDS:
        large = large + (rel >= t).astype(jnp.int32)
    return j <= i, jnp.where(rel < REL_EXACT, rel, large)


def _bias_build(rel_bias, name):
    def body(rb_ref, o_ref):
        _, bucket = _window_tiles()

        def per_head(h, carry):
            acc = jnp.zeros((BLK, BLK), F32)
            for b in range(REL_BUCKETS):
                acc = jnp.where(bucket == b, rb_ref[b, h], acc)
            o_ref[h] = acc
            return carry

        lax.fori_loop(0, N_HEADS, per_head, 0)

    return _pcall(body, name=name, grid=(1,),
                  in_specs=[pl.BlockSpec(memory_space=pltpu.SMEM)],
                  out_specs=pl.BlockSpec((N_HEADS, BLK, BLK), lambda i: (0, 0, 0)),
                  out_shape=_sds((N_HEADS, BLK, BLK), F32))(rel_bias)


def _bias_bwd(dbias, name):
    def body(db_ref, o_ref):
        _, bucket = _window_tiles()
        lane = lax.broadcasted_iota(jnp.int32, (N_HEADS, 128), 1)

        def per_bucket(b, out):
            mb = (bucket == b).astype(F32)
            per_col = jnp.sum(db_ref[...] * mb[None, :, :], axis=1)
            return jnp.where(lane == b, jnp.sum(per_col, axis=1, keepdims=True), out)

        o_ref[...] = lax.fori_loop(0, REL_BUCKETS, per_bucket, jnp.zeros((N_HEADS, 128), F32))

    return _pcall(body, name=name, grid=(1,),
                  in_specs=[pl.BlockSpec((N_HEADS, BLK, BLK), lambda i: (0, 0, 0))],
                  out_specs=pl.BlockSpec((N_HEADS, 128), lambda i: (0, 0)),
                  out_shape=_sds((N_HEADS, 128), F32))(dbias)


PAIR = 2 * HEAD
GROUP = N_HEADS // N_KV
SWA_SCALE = HEAD ** -0.5


def _window_masks(n):
    i = lax.broadcasted_iota(jnp.int32, (GROUP * BLK, BLK), 0) & (BLK - 1)
    j = lax.broadcasted_iota(jnp.int32, (GROUP * BLK, BLK), 1)
    return j <= i, jnp.logical_and(n == 0, j > i), j < HEAD


def _kv_twice(ref, base, g, low):
    slab = ref[:, base + PAIR * (g // 2): base + PAIR * (g // 2 + 1)]
    swapped = pltpu.roll(slab, HEAD, 1)
    return jnp.where(low, slab, swapped) if g % 2 == 0 else jnp.where(low, swapped, slab)


def _stack_heads(ref, g, low):
    parts = []
    for r in range(2):
        slab = ref[:, PAIR * (2 * g + r): PAIR * (2 * g + r + 1)]
        zero = jnp.zeros_like(slab)
        parts += [jnp.where(low, slab, zero), jnp.where(low, zero, slab)]
    return jnp.concatenate(parts, axis=0)


def _unstack_heads(t, low):
    return [jnp.where(low, t[2 * r * BLK:(2 * r + 1) * BLK], t[(2 * r + 1) * BLK:(2 * r + 2) * BLK])
            for r in range(2)]


def _head_rows(t, k):
    return t[k * BLK:(k + 1) * BLK]


def _per_head_column(values):
    head = lax.broadcasted_iota(jnp.int32, (GROUP * BLK, 1), 0) // BLK
    col = jnp.full((GROUP * BLK, 1), values[0], F32)
    for k in range(1, GROUP):
        col = jnp.where(head == k, values[k], col)
    return col


def _window_logits(q4, kc, kp, bias4, own, absent):
    sc = jnp.where(own, _dot_nt(q4, kc), _dot_nt(q4, kp)) * SWA_SCALE + bias4
    return jnp.where(absent, NEG, sc)


def _split_window(t, own):
    zero = jnp.zeros_like(t)
    return jnp.where(own, t, zero), jnp.where(own, zero, t)


def _swa_fwd(q, kv, bias, sinks, name, carry=None):
    s = q.shape[0]
    nb = s // BLK
    kvw = 2 * N_KV * HEAD

    def body(q_ref, kc_ref, kp_ref, b_ref, sk_ref, o_ref, lse_ref):
        own, absent, low4 = _window_masks(pl.program_id(0))
        low = low4[:BLK]
        lane = lax.broadcasted_iota(jnp.int32, (BLK, 128), 1)
        lse_t = jnp.zeros((BLK, 128), F32)
        for g in range(N_KV):
            q4 = _stack_heads(q_ref, g, low)
            kc, kp = _kv_twice(kc_ref, 0, g, low), _kv_twice(kp_ref, 0, g, low)
            vc, vp = _kv_twice(kc_ref, N_KV * HEAD, g, low), _kv_twice(kp_ref, N_KV * HEAD, g, low)
            bias4 = b_ref[GROUP * g:GROUP * (g + 1)].reshape(GROUP * BLK, BLK)
            sc = _window_logits(q4, kc, kp, bias4, own, absent)
            sk = _per_head_column([sk_ref[0, GROUP * g + k] for k in range(GROUP)])
            m = jnp.maximum(jnp.max(sc, axis=1, keepdims=True), sk)
            p = jnp.exp(sc - m)
            l = jnp.sum(p, axis=1, keepdims=True) + jnp.exp(sk - m)
            p_own, p_prev = _split_window(p.astype(BF16), own)
            out = (_dot(p_own, vc) + _dot(p_prev, vp)) * (1.0 / l)
            for r, slab in enumerate(_unstack_heads(out, low)):
                o_ref[:, PAIR * (2 * g + r): PAIR * (2 * g + r + 1)] = slab.astype(BF16)
            lse4 = m + jnp.log(l)
            for k in range(GROUP):
                lse_t = jnp.where(lane == GROUP * g + k, _head_rows(lse4, k), lse_t)
        lse_ref[...] = lse_t

    call = _pcall(body, name=name, grid=(nb,),
                  in_specs=[pl.BlockSpec((BLK, D), lambda n: (n, 0)),
                            pl.BlockSpec((BLK, kvw), lambda n: (n, 0)),
                            pl.BlockSpec((BLK, kvw), lambda n: (jnp.maximum(n - 1, 0), 0)),
                            pl.BlockSpec((N_HEADS, BLK, BLK), lambda n: (0, 0, 0)),
                            pl.BlockSpec(memory_space=pltpu.SMEM)],
                  out_specs=[pl.BlockSpec((BLK, D), lambda n: (n, 0)), pl.BlockSpec((BLK, 128), lambda n: (n, 0))],
                  out_shape=[_sds((s, D), BF16), _sds((s, 128), F32)], carry=carry)
    return _carried(call, (q, kv, kv, bias, sinks), carry)


def _fold_halves(t, g, low):
    folded = jnp.where(low, t, 0.0) + pltpu.roll(jnp.where(low, 0.0, t), HEAD, 1)
    return folded if g % 2 == 0 else pltpu.roll(folded, HEAD, 1)


def _swa_bwd(q, kv, attn, dattn, lse, bias, sinks, name, carry=None):
    s = q.shape[0]
    nb = s // BLK
    kvw = 2 * N_KV * HEAD
    voff = N_KV * HEAD

    def body(q_ref, kc_ref, kp_ref, o_ref, do_ref, lse_ref, b_ref, skrow_ref, dq_ref, dkv_ref, dbias_ref, dsk_ref,
             dq_hold, kv_hold, dq_new, kv_prev, kv_cur):
        n = pl.program_id(0)

        @pl.when(n == 0)
        def _():
            dbias_ref[...] = jnp.zeros_like(dbias_ref)
            dsk_ref[...] = jnp.zeros_like(dsk_ref)
            dq_hold[...] = jnp.zeros_like(dq_hold)
            kv_hold[...] = jnp.zeros_like(kv_hold)

        @pl.when(n < nb)
        def _():
            own, absent, low4 = _window_masks(n)
            low = low4[:BLK]
            lane = lax.broadcasted_iota(jnp.int32, (BLK, 128), 1)
            delta_t = jnp.zeros((BLK, 128), F32)
            ones = jnp.ones((PAIR, 128), BF16)
            for pair_of_kv in range(N_KV // 2):
                slab_grads = [jnp.zeros((BLK, PAIR), F32) for _ in range(4)]
                for g in (2 * pair_of_kv, 2 * pair_of_kv + 1):
                    q4, do4 = _stack_heads(q_ref, g, low), _stack_heads(do_ref, g, low)
                    kc, kp = _kv_twice(kc_ref, 0, g, low), _kv_twice(kp_ref, 0, g, low)
                    vc, vp = _kv_twice(kc_ref, voff, g, low), _kv_twice(kp_ref, voff, g, low)
                    o_slabs = [o_ref[:, PAIR * (2 * g + r): PAIR * (2 * g + r + 1)] for r in range(2)]
                    o4 = jnp.concatenate([o_slabs[0], o_slabs[0], o_slabs[1], o_slabs[1]], axis=0)
                    delta = _dot(do4 * o4, ones)
                    heads = range(GROUP * g, GROUP * (g + 1))
                    lse4 = jnp.concatenate([lse_ref[:, h:h + 1] for h in heads], axis=0)
                    bias4 = b_ref[GROUP * g:GROUP * (g + 1)].reshape(GROUP * BLK, BLK)
                    p = jnp.exp(_window_logits(q4, kc, kp, bias4, own, absent) - lse4)
                    dp = jnp.where(own, _dot_nt(do4, vc), _dot_nt(do4, vp))
                    ds = p * (dp - delta)
                    dbias_ref[GROUP * g:GROUP * (g + 1)] += ds.reshape(GROUP, BLK, BLK)
                    for k, h in enumerate(heads):
                        delta_t = jnp.where(lane == h, _head_rows(delta, k), delta_t)
                    ds_own, ds_prev = _split_window((ds * SWA_SCALE).astype(BF16), own)
                    p_own, p_prev = _split_window(p.astype(BF16), own)
                    dq4 = _dot(ds_own, kc) + _dot(ds_prev, kp)
                    for r, slab in enumerate(_unstack_heads(dq4, low)):
                        dq_new[:, PAIR * (2 * g + r): PAIR * (2 * g + r + 1)] = slab
                    grads = [_dot_tn(ds_own, q4), _dot_tn(ds_prev, q4), _dot_tn(p_own, do4), _dot_tn(p_prev, do4)]
                    slab_grads = [t + _fold_halves(dk, g, low) for t, dk in zip(slab_grads, grads)]
                ks = slice(PAIR * pair_of_kv, PAIR * (pair_of_kv + 1))
                vs = slice(voff + PAIR * pair_of_kv, voff + PAIR * (pair_of_kv + 1))
                kv_cur[:, ks], kv_prev[:, ks], kv_cur[:, vs], kv_prev[:, vs] = slab_grads
            dsk_ref[...] -= jnp.sum(jnp.exp(skrow_ref[...] - lse_ref[...]) * delta_t, axis=0, keepdims=True)

        @pl.when(n == nb)
        def _():
            kv_prev[...] = jnp.zeros_like(kv_prev)

        dq_ref[...] = dq_hold[...].astype(BF16)
        dkv_ref[...] = (kv_hold[...] + kv_prev[...]).astype(BF16)

        @pl.when(n < nb)
        def _():
            dq_hold[...] = dq_new[...]
            kv_hold[...] = kv_cur[...]

    def cur(n):
        return jnp.minimum(n, nb - 1)

    call = _pcall(body, name=name, grid=(nb + 1,), carry=carry,
                  in_specs=[pl.BlockSpec((BLK, D), lambda n: (cur(n), 0)),
                            pl.BlockSpec((BLK, kvw), lambda n: (cur(n), 0)),
                            pl.BlockSpec((BLK, kvw), lambda n: (jnp.maximum(cur(n) - 1, 0), 0)),
                            pl.BlockSpec((BLK, D), lambda n: (cur(n), 0)),
                            pl.BlockSpec((BLK, D), lambda n: (cur(n), 0)),
                            pl.BlockSpec((BLK, 128), lambda n: (cur(n), 0)),
                            pl.BlockSpec((N_HEADS, BLK, BLK), lambda n: (0, 0, 0)),
                            pl.BlockSpec((1, 128), lambda n: (0, 0))],
                  out_specs=[pl.BlockSpec((BLK, D), lambda n: (jnp.maximum(n - 1, 0), 0)),
                             pl.BlockSpec((BLK, kvw), lambda n: (jnp.maximum(n - 1, 0), 0)),
                             pl.BlockSpec((N_HEADS, BLK, BLK), lambda n: (0, 0, 0)),
                             pl.BlockSpec((1, 128), lambda n: (0, 0))],
                  out_shape=[_sds((s, D), BF16), _sds((s, kvw), BF16), _sds((N_HEADS, BLK, BLK), F32),
                             _sds((1, 128), F32)],
                  scratch=[pltpu.VMEM((BLK, D), F32), pltpu.VMEM((BLK, kvw), F32), pltpu.VMEM((BLK, D), F32),
                           pltpu.VMEM((BLK, kvw), F32), pltpu.VMEM((BLK, kvw), F32)])
    sink_row = jnp.pad(sinks, ((0, 0), (0, 128 - N_HEADS)))
    return _carried(call, (q, kv, kv, attn, dattn, lse, bias, sink_row), carry)


HALO = 16
CW = D


def _conv_taps(cu, halo_cu, first_tile):
    row = lax.broadcasted_iota(jnp.int32, cu.shape, 0)
    halo_cu = jnp.where(first_tile, 0.0, halo_cu)
    c1 = jnp.where(row == 0, halo_cu[HALO - 1:HALO], pltpu.roll(cu, 1, 0))
    c2 = jnp.where(row == 0, halo_cu[HALO - 2:HALO - 1],
                   jnp.where(row == 1, halo_cu[HALO - 1:HALO], pltpu.roll(cu, 2, 0)))
    return c1, c2


def _conv_merge_fwd(pa, attn, convw, name, ts=256, carry=None):
    _, s, _ = pa.shape
    ts = min(ts, s)
    hb = ts // HALO

    def body(pa_ref, hp_ref, at_ref, w_ref, o_ref):
        i = pl.program_id(1)
        cu = pa_ref[0].astype(F32) * pa_ref[2].astype(F32)
        c1, c2 = _conv_taps(cu, hp_ref[0].astype(F32) * hp_ref[2].astype(F32), i == 0)
        w = w_ref[...]
        c3 = w[0:1] * c2 + w[1:2] * c1 + w[2:3] * cu
        conv = pa_ref[1].astype(F32) * c3
        o_ref[...] = (jax.nn.sigmoid(pa_ref[3].astype(F32)) * at_ref[...].astype(F32)
                      + jax.nn.sigmoid(pa_ref[4].astype(F32)) * conv).astype(BF16)

    call = _pcall(body, name=name, grid=(D // CW, s // ts), carry=carry,
                  in_specs=[pl.BlockSpec((5, ts, CW), lambda c, i: (0, i, c)),
                            pl.BlockSpec((5, HALO, CW), lambda c, i: (0, jnp.maximum(i * hb - 1, 0), c)),
                            pl.BlockSpec((ts, CW), lambda c, i: (i, c)),
                            pl.BlockSpec((8, CW), lambda c, i: (0, c))],
                  out_specs=pl.BlockSpec((ts, CW), lambda c, i: (i, c)),
                  out_shape=_sds((s, D), BF16))
    return _carried(call, (pa, pa, attn, convw), carry)


def _conv_merge_bwd(dmerged, pa, attn, convw, name, ts=256, carry=None):
    _, s, _ = pa.shape
    ts = min(ts, s)
    hb = ts // HALO
    last_hb = s // HALO - 1

    def body(dm_ref, pa_ref, at_ref, w_ref, hp_ref, hn_ref, dmn_ref, dat_ref, dpa_ref, dw_ref):
        i = pl.program_id(1)
        last = i == pl.num_programs(1) - 1
        dm = dm_ref[...].astype(F32)
        cp, bp, u = pa_ref[0].astype(F32), pa_ref[1].astype(F32), pa_ref[2].astype(F32)
        sa = jax.nn.sigmoid(pa_ref[3].astype(F32))
        sc = jax.nn.sigmoid(pa_ref[4].astype(F32))
        at = at_ref[...].astype(F32)
        cu = cp * u
        c1, c2 = _conv_taps(cu, hp_ref[0].astype(F32) * hp_ref[2].astype(F32), i == 0)
        w = w_ref[...]
        c3 = w[0:1] * c2 + w[1:2] * c1 + w[2:3] * cu
        dconv = dm * sc
        dc3 = dconv * bp
        nxt = dmn_ref[...].astype(F32) * jax.nn.sigmoid(hn_ref[4].astype(F32)) * hn_ref[1].astype(F32)
        nxt = jnp.where(last, 0.0, nxt)
        row = lax.broadcasted_iota(jnp.int32, dc3.shape, 0)
        d1 = jnp.where(row == ts - 1, nxt[0:1], pltpu.roll(dc3, ts - 1, 0))
        d2 = jnp.where(row == ts - 2, nxt[0:1], jnp.where(row == ts - 1, nxt[1:2], pltpu.roll(dc3, ts - 2, 0)))
        dcu = w[2:3] * dc3 + w[1:2] * d1 + w[0:1] * d2
        dat_ref[...] = (dm * sa).astype(BF16)
        dpa_ref[0] = (dcu * u).astype(BF16)
        dpa_ref[1] = (dconv * c3).astype(BF16)
        dpa_ref[2] = (dcu * cp).astype(BF16)
        dpa_ref[3] = (dm * at * sa * (1.0 - sa)).astype(BF16)
        dpa_ref[4] = (dm * bp * c3 * sc * (1.0 - sc)).astype(BF16)

        @pl.when(i == 0)
        def _():
            dw_ref[...] = jnp.zeros_like(dw_ref)

        dw_ref[0:1, :] += jnp.sum(dc3 * c2, axis=0, keepdims=True)
        dw_ref[1:2, :] += jnp.sum(dc3 * c1, axis=0, keepdims=True)
        dw_ref[2:3, :] += jnp.sum(dc3 * cu, axis=0, keepdims=True)

    call = _pcall(body, name=name, grid=(D // CW, s // ts), carry=carry,
                  in_specs=[pl.BlockSpec((ts, CW), lambda c, i: (i, c)),
                            pl.BlockSpec((5, ts, CW), lambda c, i: (0, i, c)),
                            pl.BlockSpec((ts, CW), lambda c, i: (i, c)),
                            pl.BlockSpec((8, CW), lambda c, i: (0, c)),
                            pl.BlockSpec((5, HALO, CW), lambda c, i: (0, jnp.maximum(i * hb - 1, 0), c)),
                            pl.BlockSpec((5, HALO, CW), lambda c, i: (0, jnp.minimum((i + 1) * hb, last_hb), c)),
                            pl.BlockSpec((HALO, CW), lambda c, i: (jnp.minimum((i + 1) * hb, last_hb), c))],
                  out_specs=[pl.BlockSpec((ts, CW), lambda c, i: (i, c)),
                             pl.BlockSpec((5, ts, CW), lambda c, i: (0, i, c)),
                             pl.BlockSpec((8, CW), lambda c, i: (0, c))],
                  out_shape=[_sds((s, D), BF16), _sds((5, s, D), BF16), _sds((8, D), F32)])
    return _carried(call, (dmerged, pa, attn, convw, pa, pa, dmerged), carry)


def _xattn_fwd(hx, wq, kv, name, tq=1024):
    s, _ = hx.shape
    nm = kv.shape[1]
    tq = min(tq, s)

    def body(h_ref, wq_ref, kv_ref, q_ref, o_ref, lse_ref):
        q_ref[...] = _dot(h_ref[...], wq_ref[...]).astype(BF16)
        lane = lax.broadcasted_iota(jnp.int32, (tq, 128), 1)
        lse_t = jnp.zeros((tq, 128), F32)
        for h in range(XH):
            hs = slice(XHD * h, XHD * (h + 1))
            sc = _dot_nt(q_ref[:, hs], kv_ref[h]) * (XHD ** -0.5)
            m = jnp.max(sc, axis=1, keepdims=True)
            p = jnp.exp(sc - m)
            l = jnp.sum(p, axis=1, keepdims=True)
            o_ref[:, hs] = (_dot(p.astype(BF16), kv_ref[XH + h]) * (1.0 / l)).astype(BF16)
            lse_t = jnp.where(lane == h, m + jnp.log(l), lse_t)
        lse_ref[...] = lse_t

    return _pcall(body, name=name, grid=(s // tq,),
                  in_specs=[pl.BlockSpec((tq, D), lambda i: (i, 0)), pl.BlockSpec((D, D), lambda i: (0, 0)),
                            pl.BlockSpec((2 * XH, nm, XHD), lambda i: (0, 0, 0))],
                  out_specs=[pl.BlockSpec((tq, D), lambda i: (i, 0)), pl.BlockSpec((tq, D), lambda i: (i, 0)),
                             pl.BlockSpec((tq, 128), lambda i: (i, 0))],
                  out_shape=[_sds((s, D), BF16), _sds((s, D), BF16), _sds((s, 128), F32)])(hx, wq, kv)


def _xattn_bwd(q, kv, o, do, lse, name, tq=512, carry=None):
    s, _ = q.shape
    nm = kv.shape[1]
    tq = min(tq, s)

    def body(q_ref, kv_ref, o_ref, do_ref, lse_ref, dq_ref, dkv_ref):
        @pl.when(pl.program_id(0) == 0)
        def _():
            dkv_ref[...] = jnp.zeros_like(dkv_ref)

        for h in range(XH):
            hs = slice(XHD * h, XHD * (h + 1))
            qh, kh, vh, dob = q_ref[:, hs], kv_ref[h], kv_ref[XH + h], do_ref[:, hs]
            p = jnp.exp(_dot_nt(qh, kh) * (XHD ** -0.5) - lse_ref[:, h:h + 1])
            dp = _dot_nt(dob, vh)
            delta = jnp.sum(dob.astype(F32) * o_ref[:, hs].astype(F32), axis=1, keepdims=True)
            dsb = (p * (dp - delta) * (XHD ** -0.5)).astype(BF16)
            dq_ref[:, hs] = _dot(dsb, kh).astype(BF16)
            dkv_ref[h] += _dot_tn(dsb, qh)
            dkv_ref[XH + h] += _dot_tn(p.astype(BF16), dob)

    call = _pcall(body, name=name, grid=(s // tq,), carry=carry,
                  in_specs=[pl.BlockSpec((tq, D), lambda i: (i, 0)), pl.BlockSpec((2 * XH, nm, XHD), lambda i: (0, 0, 0)),
                            pl.BlockSpec((tq, D), lambda i: (i, 0)), pl.BlockSpec((tq, D), lambda i: (i, 0)),
                            pl.BlockSpec((tq, 128), lambda i: (i, 0))],
                  out_specs=[pl.BlockSpec((tq, D), lambda i: (i, 0)), pl.BlockSpec((2 * XH, nm, XHD), lambda i: (0, 0, 0))],
                  out_shape=[_sds((s, D), BF16), _sds((2 * XH, nm, XHD), F32)])
    return _carried(call, (q, kv, o, do, lse), carry)


def _ffn_down_bwd(dxb, wd4, gu4, name, tm=512, carry=None, behind=None):
    s, _ = dxb.shape
    tm = min(tm, s)

    def body(dx_ref, w_hbm, gu_ref, o_ref, w_ref, w_sem):
        _load_once(w_hbm, w_ref, w_sem)
        for p in range(4):
            for rows in _row_chunks(tm):
                da = _dot_nt(dx_ref[rows, :], w_ref[p])
                g = gu_ref[0, p, rows, :].astype(F32)
                u = gu_ref[1, p, rows, :].astype(F32)
                sg = jax.nn.sigmoid(g)
                t = da * sg
                o_ref[0, p, rows, :] = (t * u * (1.0 + g - g * sg)).astype(BF16)
                o_ref[1, p, rows, :] = (t * g).astype(BF16)

    block = pl.BlockSpec((2, 4, tm, FS), lambda i: (0, 0, i, 0))
    call = _pcall(body, name=name, grid=(s // tm,), carry=carry, behind=behind,
                  in_specs=[pl.BlockSpec((tm, D), lambda i: (i, 0)), HBM_SPEC, block],
                  out_specs=block, out_shape=_sds((2, 4, s, FS), BF16), scratch=_resident(wd4))
    return _carried(call, (dxb, wd4, gu4), carry)


def _mm_tn(a, b, name, scale=1.0, carry=None):
    pa_n, s, m = a.shape
    pb_n, _, n = b.shape
    po = max(pa_n, pb_n)
    tk = 1024
    if po == 1 and s > tk and s % tk == 0:
        def body_k(a_ref, b_ref, o_ref, acc_ref):
            k = pl.program_id(0)
            part = _dot_tn(a_ref[...], b_ref[...])

            @pl.when(k == 0)
            def _():
                acc_ref[...] = part

            @pl.when(k > 0)
            def _():
                acc_ref[...] += part

            @pl.when(k == s // tk - 1)
            def _():
                o_ref[...] = (scale * acc_ref[...]).astype(BF16)

        call = _pcall(body_k, name=name, grid=(s // tk,), carry=carry,
                      in_specs=[pl.BlockSpec((None, tk, m), lambda k: (0, k, 0)),
                                pl.BlockSpec((None, tk, n), lambda k: (0, k, 0))],
                      out_specs=pl.BlockSpec((None, m, n), lambda k: (0, 0, 0)),
                      out_shape=_sds((1, m, n), BF16), scratch=[pltpu.VMEM((m, n), F32)])
        return _carried(call, (a, b), carry)
    tn = n if po >= 4 else min(n, 256)

    def body(a_ref, b_ref, o_ref):
        o_ref[...] = (scale * _dot_tn(a_ref[...], b_ref[...])).astype(BF16)

    call = _pcall(body, name=name, grid=(po, n // tn), carry=carry,
                  in_specs=[pl.BlockSpec((None, s, m), lambda o, j: (o if pa_n > 1 else 0, 0, 0)),
                            pl.BlockSpec((None, s, tn), lambda o, j: (o if pb_n > 1 else 0, 0, j))],
                  out_specs=pl.BlockSpec((None, m, tn), lambda o, j: (o, 0, j)),
                  out_shape=_sds((po, m, n), BF16))
    return _carried(call, (a, b), carry)


def _mm_tn_rows(a, b, name, total_rows, row0, begun=None, tm=512, carry=None):
    p, s, m = a.shape
    n = b.shape[1]
    tm = min(tm, m)
    tiles = m // tm
    assert row0 % tm == 0 and m % tm == 0, (row0, m, tm)

    def body(a_ref, b_ref, *rest):
        rest[-1][...] = _dot_tn(a_ref[...], b_ref[...]).astype(BF16)

    in_specs = [pl.BlockSpec((None, s, tm), lambda o, i: (o, 0, i)), pl.BlockSpec((s, n), lambda o, i: (0, 0))]
    call = _pcall(body, name=name, grid=(p, tiles), in_specs=in_specs + ([HBM_SPEC] if begun is not None else []),
                  out_specs=pl.BlockSpec((tm, n), lambda o, i: (row0 // tm + o * tiles + i, 0)),
                  out_shape=_sds((total_rows, n), BF16), aliases={2: 0} if begun is not None else None, carry=carry)
    return _carried(call, (a, b, begun) if begun is not None else (a, b), carry)


def _sum_dots(a_ref, b_ref, nj, bt, rows=slice(None)):
    dot = _dot_nt if bt else _dot
    acc = dot(a_ref[0, rows, :], b_ref[0])
    for j in range(1, nj):
        acc = acc + dot(a_ref[j, rows, :], b_ref[j])
    return acc


def _mm_acc(a, b, name, out_dtype, tm=1024, bt=False, carry=None):
    nj, s, k = a.shape
    n = b.shape[1] if bt else b.shape[2]
    tm = min(tm, s)

    def body(a_ref, b_ref, o_ref):
        o_ref[...] = _sum_dots(a_ref, b_ref, nj, bt).astype(out_dtype)

    call = _pcall(body, name=name, grid=(s // tm,), carry=carry,
                  in_specs=[pl.BlockSpec((nj, tm, k), lambda i: (0, i, 0)),
                            pl.BlockSpec(b.shape, lambda i: (0, 0, 0))],
                  out_specs=pl.BlockSpec((tm, n), lambda i: (i, 0)), out_shape=_sds((s, n), out_dtype))
    return _carried(call, (a, b), carry)


def _rms_bwd_call(name, acts, weights, scratch, load, dh_rows, *, x, gain, dres, tm, carry, behind=None):
    s, n = x.shape
    tm = min(tm, s)
    n_act, n_w = len(acts), len(weights)

    def body(*refs):
        act_refs, w_refs = refs[:n_act], refs[n_act:n_act + n_w]
        x_ref, g_ref, r_ref, dx_ref, dxb_ref, dg_ref = refs[n_act + n_w:n_act + n_w + 6]
        held = refs[n_act + n_w + 6:]
        load(w_refs, held)

        @pl.when(pl.program_id(0) == 0)
        def _():
            dg_ref[...] = jnp.zeros_like(dg_ref)

        for rows in _row_chunks(tm):
            dh = dh_rows(act_refs, held, rows)
            xv = x_ref[rows, :]
            r = lax.rsqrt(jnp.mean(xv * xv, axis=-1, keepdims=True) + EPS)
            xh = xv * r
            dyg = dh * g_ref[...]
            dx = r_ref[rows, :] + r * (dyg - xh * jnp.mean(dyg * xh, axis=-1, keepdims=True))
            dx_ref[rows, :] = dx
            dxb_ref[rows, :] = dx.astype(BF16)
            dg_ref[...] += jnp.sum(dh * xh, axis=0, keepdims=True)

    def tile(a):
        return (pl.BlockSpec((tm, a.shape[1]), lambda i: (i, 0)) if a.ndim == 2
                else pl.BlockSpec((a.shape[0], tm, a.shape[2]), lambda i: (0, i, 0)))

    row = pl.BlockSpec((tm, n), lambda i: (i, 0))
    in_specs = [tile(a) for a in acts] + [HBM_SPEC] * n_w + [row, pl.BlockSpec((1, n), lambda i: (0, 0)), row]
    call = _pcall(body, name=name, grid=(s // tm,), in_specs=in_specs, carry=carry, behind=behind,
                  out_specs=[row, row, pl.BlockSpec((1, n), lambda i: (0, 0))],
                  out_shape=[_sds((s, n), F32), _sds((s, n), BF16), _sds((1, n), F32)], scratch=scratch)
    return _carried(call, tuple(acts) + tuple(weights) + (x, gain, dres), carry)


def _mm_acc_rms_bwd(a, b, name, *, x, gain, dres, scale=None, tm=512, bt=False, carry=None, behind=None):
    def load(w_refs, held):
        _load_once(w_refs[0], held[0], held[1])

    def dh_rows(act_refs, held, rows):
        dh = _sum_dots(act_refs[0], held[0], a.shape[0], bt, rows)
        return dh if scale is None else scale * dh

    return _rms_bwd_call(name, [a], [b], _resident(b), load, dh_rows, x=x, gain=gain, dres=dres, tm=tm, carry=carry,
                         behind=behind)


def _in_proj_bwd(dpa, dq, dkv, w_in_t, name, *, x, gain, dres, tm=512, carry=None, behind=None):
    def load(w_refs, held):
        _load_in_proj(w_refs[0], *held)

    def dh_rows(act_refs, held, rows):
        dpa_ref, dq_ref, dkv_ref = act_refs
        wq_ref, wkv_ref, wa_ref, _ = held
        dh = _dot(dq_ref[rows, :], wq_ref[...]) + _dot(dkv_ref[rows, :], wkv_ref[...])
        return dh + _sum_dots(dpa_ref, wa_ref, N_SEG, False, rows)

    return _rms_bwd_call(name, [dpa, dq, dkv], [w_in_t], IN_PROJ_WEIGHTS, load, dh_rows, x=x, gain=gain, dres=dres,
                         tm=tm, carry=carry, behind=behind)


def _adam(w, g, m, v):
    m2 = ADAM_B1 * m + (1.0 - ADAM_B1) * g
    v2 = ADAM_B2 * v + (1.0 - ADAM_B2) * (g * g)
    m_hat = m2 / (1.0 - ADAM_B1 ** ADAM_STEP)
    v_hat = v2 / (1.0 - ADAM_B2 ** ADAM_STEP)
    delta = -ADAM_LR * (m_hat / (jnp.sqrt(v_hat) + ADAM_EPS) + ADAM_WD * w)
    return delta, m2, v2


def _adamw(parts, w, m, v, name, behind=None):
    _, r, c = parts.shape
    tr = max(t for t in range(16, 257, 16) if r % t == 0)

    def body(p_ref, w_ref, m_ref, v_ref, g_ref, d_ref, m2_ref, v2_ref):
        g = p_ref[0].astype(F32)
        for i in range(1, N_DEV):
            g = g + p_ref[i].astype(F32)
        delta, m2, v2 = _adam(w_ref[...], g, m_ref[...], v_ref[...])
        g_ref[...] = g
        d_ref[...] = delta
        m2_ref[...] = m2
        v2_ref[...] = v2

    blk = pl.BlockSpec((tr, c), lambda i: (i, 0))
    return _pcall(body, name=name, grid=(r // tr,), behind=behind,
                  in_specs=[pl.BlockSpec((N_DEV, tr, c), lambda i: (0, i, 0)), blk, blk, blk],
                  out_specs=[blk] * 4, out_shape=[_sds((r, c), F32)] * 4)(parts, w, m, v)


def _position():
    return lax.axis_index("x"), lax.axis_index("y"), lax.axis_index("c")


def _slot(px, py, pc):
    return 4 * px + 2 * py + pc


def _row_window(ref, rows):
    r0, r1 = rows
    return ref if (r0, r1) == (0, ref.shape[0]) else ref.at[pl.ds(r0, r1 - r0)]


def _split_items(items):
    sources = [src for src, _, _ in items]
    begun = [(a, dest) for a, (_, _, dest) in enumerate(items) if dest is not None]
    aliases = {len(sources) + k: a for k, (a, _) in enumerate(begun)}
    return sources + [dest for _, dest in begun], [rows for _, rows, _ in items], aliases


def _gather_carry(items):
    na = len(items)
    carry_ins, windows, aliases = _split_items(items)

    def plan(ins, outs, sems):
        send_sems, recv_sems, local_sems = sems
        x, y, c = _position()
        me, sibling = (x, y, c), (x, y, 1 - c)
        chips = [(1 - x, y), (x, 1 - y), (1 - x, 1 - y)]
        ins = [_row_window(ins[a], windows[a]) for a in range(na)]

        def block_rows(a, block):
            return _row_window(outs[a].at[_slot(*block)], windows[a])

        def copy(a, k, block, to, src=None):
            rows = block_rows(a, block)
            return pltpu.make_async_remote_copy(src_ref=rows if src is None else src, dst_ref=rows,
                                                send_sem=send_sems.at[k, a], recv_sem=recv_sems.at[k, a],
                                                device_id=to, device_id_type=MESH)

        mine = [pltpu.make_async_copy(ins[a], block_rows(a, me), local_sems.at[a]) for a in range(na)]
        first = [copy(a, 0, me, sibling, src=ins[a]) for a in range(na)]
        for j, chip in enumerate(chips):
            first += [copy(a, 1 + j, me, (*chip, c), src=ins[a]) for a in range(na)]
        landed = [[copy(a, 1 + j, (*chip, c), me) for a in range(na)] for j, chip in enumerate(chips)]
        passed = [[copy(a, 4 + j, (*chip, c), sibling) for a in range(na)] for j, chip in enumerate(chips)]
        from_sibling = [copy(a, 0, sibling, me) for a in range(na)]
        for j, chip in enumerate(chips):
            from_sibling += [copy(a, 4 + j, (*chip, 1 - c), me) for a in range(na)]
        return mine, first, landed, passed, from_sibling

    def start(ins, outs, sems):
        mine, first, _, _, _ = plan(ins, outs, sems)
        for cp in mine + first:
            cp.start()

    def mid(ins, outs, sems):
        _, _, landed, passed, _ = plan(ins, outs, sems)
        for over_ici, onward in zip(landed, passed):
            for cp, fwd in zip(over_ici, onward):
                cp.wait_recv()
                fwd.start()

    def finish(ins, outs, sems):
        mine, first, _, passed, from_sibling = plan(ins, outs, sems)
        for cp in from_sibling:
            cp.wait_recv()
        for cp in first + [fwd for onward in passed for fwd in onward]:
            cp.wait_send()
        for cp in mine:
            cp.wait()

    return _Carry(carry_ins, [_sds((N_DEV,) + src.shape, src.dtype) for src, _, _ in items],
                  [pltpu.SemaphoreType.DMA((7, na)), pltpu.SemaphoreType.DMA((7, na)),
                   pltpu.SemaphoreType.DMA((na,))], start, finish, mid, aliases)


def _exchange_carry(scattered, replicated=()):
    items = list(scattered) + [(a, (0, a.shape[0]), None) for a in replicated]
    na, ns = len(items), len(scattered)
    carry_ins, windows, aliases = _split_items(items)

    def plan(ins, outs, sems):
        send_sems, recv_sems, local_sems = sems
        me = _slot(*_position())

        def source(a, j):
            return _row_window(ins[a].at[j] if a < ns else ins[a], windows[a])

        def copy(a, j, i):
            return pltpu.make_async_remote_copy(src_ref=source(a, j), dst_ref=_row_window(outs[a].at[i], windows[a]),
                                                send_sem=send_sems.at[j, a], recv_sem=recv_sems.at[i, a],
                                                device_id=(j >> 2, (j >> 1) & 1, j & 1), device_id_type=MESH)

        def own(a, j):
            return pltpu.make_async_copy(source(a, j), _row_window(outs[a].at[j], windows[a]), local_sems.at[a])

        return me, copy, own

    def start(ins, outs, sems):
        me, copy, own = plan(ins, outs, sems)
        for a in range(na):
            for j in range(N_DEV):
                @pl.when(me == j)
                def _():
                    own(a, j).start()

                @pl.when(me != j)
                def _():
                    copy(a, j, me).start()

    def finish(ins, outs, sems):
        me, copy, own = plan(ins, outs, sems)
        for a in range(na):
            for j in range(N_DEV):
                @pl.when(me == j)
                def _():
                    for i in range(N_DEV):
                        if i != j:
                            copy(a, j, i).wait_recv()
                    own(a, j).wait()

                @pl.when(me != j)
                def _():
                    copy(a, j, me).wait_send()

    return _Carry(carry_ins, [_sds((N_DEV,) + src.shape[-2:], src.dtype) for src, _, _ in items],
                  [pltpu.SemaphoreType.DMA((N_DEV, na)), pltpu.SemaphoreType.DMA((N_DEV, na)),
                   pltpu.SemaphoreType.DMA((na,))], start, finish, None, aliases)


HBM_ARRAY = pl.BlockSpec(memory_space=pltpu.HBM)
SEMAPHORES = pl.BlockSpec(memory_space=pltpu.SEMAPHORE)
DATAFLOW = pltpu.SideEffectType.DATAFLOW_SIDE_EFFECTING


def _exchange_copy(parts_ref, land_ref, send_sems, recv_sems, me, j):
    return pltpu.make_async_remote_copy(src_ref=parts_ref.at[j], dst_ref=land_ref.at[me], send_sem=send_sems.at[j],
                                        recv_sem=recv_sems.at[me], device_id=(j >> 2, (j >> 1) & 1, j & 1),
                                        device_id_type=MESH)


def _exchange_start(parts, name):
    def body(parts_ref, land_ref, send_sems, recv_sems, parts_thru, land_thru, token):
        me = _slot(*_position())
        for j in range(N_DEV):
            @pl.when(me == j)
            def _():
                pltpu.make_async_copy(parts_ref.at[j], land_ref.at[j], send_sems.at[j]).start()

            @pl.when(me != j)
            def _():
                _exchange_copy(parts_ref, land_ref, send_sems, recv_sems, me, j).start()
        token[...] = jnp.zeros_like(token)

    return pl.pallas_call(
        body, name=name,
        out_shape=(pltpu.SemaphoreType.DMA((N_DEV,)), pltpu.SemaphoreType.DMA((N_DEV,)),
                   pltpu.HBM(parts.shape, parts.dtype), pltpu.HBM(parts.shape, parts.dtype), _sds((8, 128), F32)),
        in_specs=(HBM_ARRAY, HBM_ARRAY),
        out_specs=(SEMAPHORES, SEMAPHORES, HBM_ARRAY, HBM_ARRAY, pl.BlockSpec(memory_space=pltpu.VMEM)),
        input_output_aliases={0: 2, 1: 3}, compiler_params=pltpu.CompilerParams(has_side_effects=DATAFLOW),
    )(pltpu.with_memory_space_constraint(parts, pltpu.HBM),
      pltpu.with_memory_space_constraint(lax.empty(parts.shape, parts.dtype), pltpu.HBM))


def _exchange_wait(send_sems, recv_sems, parts_thru, land_thru, after, name):
    def body(parts_ref, land_ref, send_sems, recv_sems, *rest):
        me = _slot(*_position())
        for j in range(N_DEV):
            @pl.when(me == j)
            def _():
                pltpu.make_async_copy(parts_ref.at[j], land_ref.at[j], send_sems.at[j]).wait()

            @pl.when(me != j)
            def _():
                both = pltpu.make_async_remote_copy(src_ref=parts_ref.at[j], dst_ref=land_ref.at[j],
                                                    send_sem=send_sems.at[j], recv_sem=recv_sems.at[j],
                                                    device_id=(j >> 2, (j >> 1) & 1, j & 1), device_id_type=MESH)
                both.wait_send()
                both.wait_recv()

    return pl.pallas_call(
        body, name=name, out_shape=(pltpu.HBM(parts_thru.shape, parts_thru.dtype),
                                    pltpu.HBM(parts_thru.shape, parts_thru.dtype)),
        in_specs=(HBM_ARRAY, HBM_ARRAY, SEMAPHORES, SEMAPHORES) + (pl.BlockSpec(memory_space=pl.ANY),) * len(after),
        out_specs=(HBM_ARRAY, HBM_ARRAY), input_output_aliases={0: 0, 1: 1},
        compiler_params=pltpu.CompilerParams(has_side_effects=DATAFLOW),
    )(parts_thru, land_thru, send_sems, recv_sems, *after)[1]


class _Mesh:
    def __init__(self, shards):
        self.shards, self.full, self.received, self.cache, self.pending, self.tokens = shards, {}, {}, {}, {}, {}

    def fetch(self, wanted):
        items = []
        for want in wanted:
            name, r0, r1 = want if isinstance(want, tuple) else (want, 0, self.shards[want].shape[0])
            items.append((self.shards[name], (r0, r1), self.full.get(name)))
        return _gather_carry(items)

    def fetched(self, wanted, results):
        self.full.update(zip([want[0] if isinstance(want, tuple) else want for want in wanted], results))

    def send(self, *payloads):
        return _exchange_carry([(parts, rows or (0, parts.shape[1]), self.received.get(name))
                                for name, parts, rows in payloads])

    def sent(self, names, results):
        self.received.update(zip(names, results))

    def send_apart(self, name, parts):
        *self.pending[name], self.tokens[name] = _exchange_start(parts, "exchange_" + name + "_start")
        return self.tokens[name]

    def sent_apart(self, name, after):
        self.received[name] = _exchange_wait(*self.pending.pop(name), after, "exchange_" + name + "_wait")

    def w(self, key):
        if key not in self.cache:
            self.cache[key] = self._layout(key)
        return self.cache[key]

    def _layout(self, key):
        if key in ("gu1", "gu2"):
            return self.full[key]
        if key in ("d1", "d2"):
            return self.full[key].reshape(4, FS, D)
        if key in ("out", "q", "o"):
            return self.full[key].reshape(D, D)
        if key == "kv":
            return self.full["kv"]
        if key == "convw":
            rows = self.full["conv"][:, :3, :].transpose(1, 0, 2).reshape(3, D)
            return jnp.concatenate([rows, jnp.zeros((5, D), F32)], axis=0)
        assert key == "win_t", key
        return self.full["win"].reshape(-1, D)


def _forward_backward(x, mem, target, g, rel_bias, sinks, ex):
    s = x.shape[0]
    def fetching(wanted, call, *args, **kw):
        res, got = call(*args, carry=ex.fetch(wanted), **kw)
        ex.fetched(wanted, got)
        return res

    h1 = fetching(["gu1", "conv"], _rmsnorm, x, g["ffn1"], "norm_ffn1")
    gu1, a1 = fetching(["d1", ("win", 0, 400)], _ffn_up, h1, ex.w("gu1").reshape(2, 4, FS, D), "ffn1_up")
    x1, h2 = fetching([("win", 400, 832)], _mm_res_norm, a1, ex.w("d1"), x, g["mix"], 0.5, "ffn1_down")
    pa, q, kv = fetching(["gu2"], _in_proj, h2, ex.w("win_t"), "in_proj")
    biasm = _bias_build(rel_bias, "bias_build")
    attn, lse = fetching(["out", "kv", "o"], _swa_fwd, q, kv, biasm, sinks, "swa_fwd")
    merged = fetching(["q"], _conv_merge_fwd, pa, attn, ex.w("convw"), "conv_merge_fwd")
    (x2, h3), _ = _mm_res_norm(merged[None], ex.w("out")[None], x1, g["xattn"], 1.0, "out_proj")
    mh, kv2 = _norm_mm(mem, g["mem"], ex.w("kv"), "xattn_kv")
    q2, o, lse2 = _xattn_fwd(h3, ex.w("q"), kv2, "xattn_fwd")
    (x3, h4), _ = _mm_res_norm(o[None], ex.w("o")[None], x2, g["ffn2"], 1.0, "xattn_o")
    gu2, a2 = fetching(["d2"], _ffn_up, h4, ex.w("gu2").reshape(2, 4, FS, D), "ffn2_up")
    dx4, dx4b, loss, d_final = _ffn_down_loss(a2, ex.w("d2"), x3, g["final"], target, "ffn2_down_loss")
    def sending(payloads, call, *args, **kw):
        res, got = call(*args, carry=ex.send(*payloads), **kw)
        ex.sent([name for name, _, _ in payloads], got)
        return res

    dw_d2 = _mm_tn(a2, dx4b[None], "dw_ffn2_down", scale=0.5)[0].reshape(N_DEV, -1, D)
    dgu2 = sending([("d2", dw_d2, (0, 288))], _ffn_down_bwd, dx4b, ex.w("d2"), gu2, "ffn2_down_bwd").reshape(8, s, FS)
    dw_gu2 = sending([("d2", dw_d2, (288, 352))], _mm_tn, dgu2, h4[None], "dw_ffn2_up", scale=0.5)
    dx3, dx3b, d_ffn2 = sending([("gu2", dw_gu2, (0, 368))], _mm_acc_rms_bwd, dgu2, ex.w("gu2"), "ffn2_up_bwd",
                                x=x3, gain=g["ffn2"], dres=dx4, scale=0.5)
    do, _ = _mm_acc(dx3b[None], ex.w("o")[None], "xattn_o_bwd", BF16, bt=True)
    dw_o = _mm_tn(o[None], dx3b[None], "dw_xattn_o")[0].reshape(N_DEV, -1, D)
    (dq2, dkv2), _ = _xattn_bwd(q2, kv2, o, do, lse2, "xattn_bwd")
    dw_q = _mm_tn(h3[None], dq2[None], "dw_xattn_q")[0].reshape(N_DEV, -1, D)
    (dx2, dx2b, d_xattn), _ = _mm_acc_rms_bwd(dq2[None], ex.w("q")[None], "xattn_q_bwd", x=x2, gain=g["xattn"],
                                              dres=dx3, bt=True)
    dw_kv, d_mem = _norm_mm_bwd(dkv2, mh, ex.w("kv"), mem, "xattn_kv_bwd")
    dmerged, _ = _mm_acc(dx2b[None], ex.w("out")[None], "out_proj_bwd", BF16, bt=True)
    dw_out = _mm_tn(merged[None], dx2b[None], "dw_out_proj")[0].reshape(N_DEV, -1, D)
    dattn, dpa, d_convw = sending([("kv", dw_kv, None)], _conv_merge_bwd,
                                  dmerged, pa, attn, ex.w("convw"), "conv_merge_bwd")
    dq, dkv, dbias, d_sinks = sending([("gu2", dw_gu2, (368, FS)), ("out", dw_out, None)], _swa_bwd,
                                      q, kv, attn, dattn, lse, biasm, sinks, "swa_bwd")
    d_relb = _bias_bwd(dbias, "bias_bwd")
    w_rows = ex.w("win_t").shape[0]
    dw_in = sending([("o", dw_o, None), ("q", dw_q, None)], _mm_tn_rows, dpa, h2, "dw_in_proj_a", w_rows, NQ + NKV)
    dw_in = _mm_tn_rows(dq[None], h2, "dw_in_proj_q", w_rows, 0, begun=dw_in)[0]
    dw_in = _mm_tn_rows(dkv[None], h2, "dw_in_proj_kv", w_rows, NQ, begun=dw_in)[0].reshape(N_DEV, -1, D)
    (dx1, dx1b, d_mix), _ = _in_proj_bwd(dpa, dq, dkv, ex.w("win_t"), "in_proj_bwd", x=x1, gain=g["mix"], dres=dx2,
                                         behind=ex.send_apart("win", dw_in))
    dw_d1 = _mm_tn(a1, dx1b[None], "dw_ffn1_down", scale=0.5)[0].reshape(N_DEV, -1, D)
    dgu1 = _ffn_down_bwd(dx1b, ex.w("d1"), gu1, "ffn1_down_bwd", behind=ex.send_apart("d1", dw_d1))[0]
    dgu1 = dgu1.reshape(8, s, FS)
    dw_gu1 = _mm_tn(dgu1, h1[None], "dw_ffn1_up", scale=0.5)[0]
    (dx0, _, d_ffn1), _ = _mm_acc_rms_bwd(dgu1, ex.w("gu1"), "ffn1_up_bwd", x=x, gain=g["ffn1"], dres=dx1,
                                          scale=0.5, behind=ex.send_apart("gu1", dw_gu1))

    relb_row = jnp.concatenate([d_relb[:, :REL_BUCKETS].T.reshape(1, REL_BUCKETS * N_HEADS), d_sinks[:, :N_HEADS],
                                jnp.zeros((1, D - REL_BUCKETS * N_HEADS - N_HEADS), F32)], axis=1)
    loss_row = jnp.concatenate([loss[0:1, 0:1], jnp.zeros((1, D - 1), F32)], axis=1)
    small = jnp.concatenate([d_ffn1, d_mix, d_xattn, d_mem, d_ffn2, d_final, relb_row, loss_row, d_convw[0:3],
                             jnp.zeros((SMALL_ROWS - ROW_CONV - 3, D), F32)], axis=0)
    return dx0, small


def _pack_small(norms, final, relb, sinks, conv_local, me):
    relb_row = jnp.concatenate([relb.reshape(1, -1), sinks.reshape(1, -1),
                                jnp.zeros((1, D - REL_BUCKETS * N_HEADS - N_HEADS), F32)], axis=1)
    conv_rows = lax.dynamic_update_slice(jnp.zeros((3, D), F32), conv_local.reshape(3, -1), (0, 128 * me))
    return jnp.concatenate(list(norms) + [final.reshape(1, D), relb_row, jnp.zeros((1, D), F32), conv_rows,
                                          jnp.zeros((SMALL_ROWS - ROW_CONV - 3, D), F32)], axis=0)


def kernel(x, mem, positions, rel_bias, ffn1_norm, ffn1_w_gu, ffn1_w_down, mix_norm, w_in, sinks, conv_w, w_out, xattn_norm, mem_norm, xattn_wq, xattn_wkv, xattn_wo, ffn2_norm, ffn2_w_gu, ffn2_w_down, final_norm, loss_target, m_rel_bias, m_ffn1_norm, m_ffn1_w_gu, m_ffn1_w_down, m_mix_norm, m_w_in, m_sinks, m_conv_w, m_w_out, m_xattn_norm, m_mem_norm, m_xattn_wq, m_xattn_wkv, m_xattn_wo, m_ffn2_norm, m_ffn2_w_gu, m_ffn2_w_down, m_final_norm, v_rel_bias, v_ffn1_norm, v_ffn1_w_gu, v_ffn1_w_down, v_mix_norm, v_w_in, v_sinks, v_conv_w, v_w_out, v_xattn_norm, v_mem_norm, v_xattn_wq, v_xattn_wkv, v_xattn_wo, v_ffn2_norm, v_ffn2_w_gu, v_ffn2_w_down, v_final_norm):
    del positions
    me = _slot(*_position())
    big = dict(gu1=(ffn1_w_gu, m_ffn1_w_gu, v_ffn1_w_gu), d1=(ffn1_w_down, m_ffn1_w_down, v_ffn1_w_down),
               win=(w_in, m_w_in, v_w_in), out=(w_out, m_w_out, v_w_out), q=(xattn_wq, m_xattn_wq, v_xattn_wq),
               kv=(xattn_wkv, m_xattn_wkv, v_xattn_wkv), o=(xattn_wo, m_xattn_wo, v_xattn_wo),
               gu2=(ffn2_w_gu, m_ffn2_w_gu, v_ffn2_w_gu), d2=(ffn2_w_down, m_ffn2_w_down, v_ffn2_w_down))
    order = list(big)
    transposed = ("gu1", "gu2", "win")
    local = {k: tuple(t[0].T if k in transposed else t[0] for t in big[k]) for k in order}
    shards = {k: local[k][0].astype(BF16) for k in order}
    shards["conv"] = jnp.concatenate([conv_w[0], jnp.zeros((5, 128), F32)], axis=0)
    ex = _Mesh(shards)
    gains = dict(ffn1=ffn1_norm, mix=mix_norm, xattn=xattn_norm, mem=mem_norm, ffn2=ffn2_norm,
                 final=final_norm.reshape(1, D))
    dx, small = _forward_backward(x[0], mem[0], loss_target[0], gains, rel_bias, sinks, ex)
    apart = ("d1", "win", "gu1")
    big_out = {k: _adamw(ex.received[k], *local[k], "adamw_" + k, behind=ex.tokens["gu1"])
               for k in order if k not in apart}
    for k in apart[:-1]:
        ex.sent_apart(k, after=[big_out[j][1] for j in big_out])
        big_out[k] = _adamw(ex.received[k], *local[k], "adamw_" + k)
    small_parts = _run_alone(_exchange_carry([], [small]), "exchange_small", after=[big_out[k][1] for k in big_out])[0]
    packed = [_pack_small(norms, final, relb, sk, conv, me) for norms, final, relb, sk, conv in (
        ((ffn1_norm, mix_norm, xattn_norm, mem_norm, ffn2_norm), final_norm, rel_bias, sinks, conv_w),
        ((m_ffn1_norm, m_mix_norm, m_xattn_norm, m_mem_norm, m_ffn2_norm), m_final_norm, m_rel_bias, m_sinks, m_conv_w),
        ((v_ffn1_norm, v_mix_norm, v_xattn_norm, v_mem_norm, v_ffn2_norm), v_final_norm, v_rel_bias, v_sinks, v_conv_w))]
    small_out = _adamw(small_parts, *packed, "adamw_small")
    ex.sent_apart("gu1", after=[dx, small_out[1]] + [big_out[k][1] for k in big_out])
    big_out["gu1"] = _adamw(ex.received["gu1"], *local["gu1"], "adamw_gu1")
    big_out = {k: [t.T if k in transposed else t for t in big_out[k]] for k in order}

    def unpack(t):
        conv = lax.dynamic_slice(t[ROW_CONV:ROW_CONV + 3], (0, 128 * me), (3, 128))[None]
        nrel = REL_BUCKETS * N_HEADS
        return dict(ffn1_norm=t[0:1], mix_norm=t[1:2], xattn_norm=t[2:3], mem_norm=t[3:4], ffn2_norm=t[4:5],
                    final_norm=t[5], rel_bias=t[ROW_RELB, :nrel].reshape(REL_BUCKETS, N_HEADS),
                    sinks=t[ROW_RELB:ROW_RELB + 1, nrel:nrel + N_HEADS], conv_w=conv)

    names = dict(gu1="ffn1_w_gu", d1="ffn1_w_down", win="w_in", out="w_out", q="xattn_wq", kv="xattn_wkv",
                 o="xattn_wo", gu2="ffn2_w_gu", d2="ffn2_w_down")
    results = []
    for idx in range(4):
        leaves = unpack(small_out[idx])
        leaves.update({names[k]: big_out[k][idx][None] for k in order})
        results.append(leaves)
    weights = ("rel_bias", "ffn1_norm", "ffn1_w_gu", "ffn1_w_down", "mix_norm", "w_in", "sinks", "conv_w", "w_out",
               "xattn_norm", "mem_norm", "xattn_wq", "xattn_wkv", "xattn_wo", "ffn2_norm", "ffn2_w_gu", "ffn2_w_down",
               "final_norm")
    loss = small_out[0][ROW_LOSS, 0]
    return (loss, dx[None], *[leaves[n] for leaves in results for n in weights])
```

```python
import math

import numpy as np
import jax
import jax.numpy as jnp
from jax import lax
from jax.experimental import pallas as pl
from jax.experimental.pallas import tpu as pltpu

F32, BF16 = jnp.float32, jnp.bfloat16
MESH = pl.DeviceIdType.MESH

D = 1024
N_DEV = 8
D_FF = 2816
FS = D_FF // 4
HEAD = 64
N_HEADS, N_KV = 16, 4
BLK = 128
NQ, NKV = N_HEADS * HEAD, 2 * N_KV * HEAD
XH, XHD = 4, 256
REL_BUCKETS, REL_EXACT, REL_MAX_DIST = 32, 16, 128
EPS, NEG = 1e-6, -1e30
ADAM_LR, ADAM_B1, ADAM_B2, ADAM_EPS, ADAM_WD, ADAM_STEP = 0.001, 0.9, 0.999, 1e-08, 0.01, 10
VMEM_LIMIT_V7X = 56 * 2**20
SMALL_ROWS = 16
ROW_RELB, ROW_LOSS, ROW_CONV = 6, 7, 8


def _bucket_thresholds():
    n = np.arange(REL_MAX_DIST)
    nf = np.maximum(n, 1).astype(np.float32)
    large = REL_EXACT + (np.log(nf / np.float32(REL_EXACT)) / np.float32(math.log(REL_MAX_DIST / REL_EXACT))
                         * np.float32(REL_BUCKETS - REL_EXACT)).astype(np.int32)
    b = np.where(n < REL_EXACT, n, np.minimum(large, REL_BUCKETS - 1))
    return [int(np.argmax(b >= REL_EXACT + k)) for k in range(1, REL_BUCKETS - REL_EXACT)]


BUCKET_THRESHOLDS = _bucket_thresholds()


HBM_SPEC = pl.BlockSpec(memory_space=pl.ANY)


class _Carry:
    def __init__(self, ins, outs, sems, start, finish, mid=None, aliases=None):
        self.ins, self.outs, self.sems = list(ins), list(outs), list(sems)
        self.start, self.finish, self.mid, self.aliases = start, finish, mid, dict(aliases or {})


def _pcall(body, *, name, grid, in_specs, out_specs, out_shape, scratch=(), carry=None, aliases=None, behind=None):
    params = pltpu.CompilerParams(dimension_semantics=("arbitrary",) * len(grid), vmem_limit_bytes=VMEM_LIMIT_V7X)
    if carry is None and behind is not None:
        n_in = len(in_specs)
        call = pl.pallas_call(lambda *refs: body(*refs[:n_in], *refs[n_in + 1:]), name=name, grid=grid,
                              in_specs=list(in_specs) + [pl.BlockSpec((8, 128), lambda *_: (0, 0))],
                              out_specs=out_specs, out_shape=out_shape, scratch_shapes=list(scratch),
                              compiler_params=params, input_output_aliases=aliases or {})
        return lambda *args: call(*args, behind)
    if carry is None:
        return pl.pallas_call(body, name=name, grid=grid, in_specs=in_specs, out_specs=out_specs,
                              out_shape=out_shape, scratch_shapes=list(scratch), compiler_params=params,
                              input_output_aliases=aliases or {})
    assert aliases is None and behind is None, name
    single = not isinstance(out_shape, (list, tuple))
    own_specs, own_shapes = ([out_specs], [out_shape]) if single else (list(out_specs), list(out_shape))
    n_in, n_out, n_scr = len(in_specs), len(own_shapes), len(scratch)
    n_cin, n_cout = len(carry.ins), len(carry.outs)
    steps = math.prod(grid)
    mid_step = max(steps - 1 - max(steps // 8, 1), 0)

    def carrying(*refs):
        ins, refs = refs[:n_in], refs[n_in:]
        cins, refs = refs[:n_cin], refs[n_cin:]
        outs, refs = refs[:n_out], refs[n_out:]
        couts, refs = refs[:n_cout], refs[n_cout:]
        scr, csems = refs[:n_scr], refs[n_scr:]
        step = 0
        for axis, size in enumerate(grid):
            step = step * size + pl.program_id(axis)

        @pl.when(step == 0)
        def _():
            carry.start(cins, couts, csems)

        body(*ins, *outs, *scr)
        if carry.mid is not None:
            @pl.when(step == mid_step)
            def _():
                carry.mid(cins, couts, csems)

        @pl.when(step == steps - 1)
        def _():
            carry.finish(cins, couts, csems)

    call = pl.pallas_call(carrying, name=name, grid=grid, in_specs=list(in_specs) + [HBM_SPEC] * n_cin,
                          out_specs=own_specs + [HBM_SPEC] * n_cout, out_shape=own_shapes + carry.outs,
                          scratch_shapes=list(scratch) + carry.sems, compiler_params=params,
                          input_output_aliases={n_in + i: n_out + o for i, o in carry.aliases.items()})

    def run(*args):
        res = call(*args, *carry.ins)
        return (res[0] if single else res[:n_out]), res[n_out:]

    return run


def _run_alone(carry, name, after=()):
    n_cin, n_cout, n_after = len(carry.ins), len(carry.outs), len(after)

    def body(*refs):
        cins, refs = refs[:n_cin], refs[n_cin + n_after:]
        couts, csems = refs[:n_cout], refs[n_cout:]
        carry.start(cins, couts, csems)
        if carry.mid is not None:
            carry.mid(cins, couts, csems)
        carry.finish(cins, couts, csems)

    return pl.pallas_call(body, name=name, in_specs=[HBM_SPEC] * (n_cin + n_after), out_specs=[HBM_SPEC] * n_cout,
                          out_shape=carry.outs, scratch_shapes=carry.sems,
                          input_output_aliases=carry.aliases)(*carry.ins, *after)


def _dot(a, b):
    return jnp.dot(a, b, preferred_element_type=F32)


def _dot_nt(a, b):
    return lax.dot_general(a, b, (((1,), (1,)), ((), ())), preferred_element_type=F32)


def _dot_tn(a, b):
    return lax.dot_general(a, b, (((0,), (0,)), ((), ())), preferred_element_type=F32)


def _sds(shape, dtype):
    return jax.ShapeDtypeStruct(tuple(shape), dtype)


ROW_CHUNK = 256


def _row_chunks(tm):
    return [slice(r, min(r + ROW_CHUNK, tm)) for r in range(0, tm, ROW_CHUNK)]


def _carried(call, args, carry):
    return call(*args) if carry is not None else (call(*args), ())


def _rmsnorm(x, g, name, carry=None):
    m, d = x.shape
    tm = min(512, m)

    def body(x_ref, g_ref, h_ref):
        xv = x_ref[...]
        r = lax.rsqrt(jnp.mean(xv * xv, axis=-1, keepdims=True) + EPS)
        h_ref[...] = (xv * r * g_ref[...]).astype(BF16)

    call = _pcall(body, name=name, grid=(m // tm,), carry=carry,
                  in_specs=[pl.BlockSpec((tm, d), lambda i: (i, 0)), pl.BlockSpec((1, d), lambda i: (0, 0))],
                  out_specs=pl.BlockSpec((tm, d), lambda i: (i, 0)), out_shape=_sds((m, d), BF16))
    return _carried(call, (x, g), carry)


def _norm_mm(x, g, w, name):
    m, d = x.shape
    nj, _, n = w.shape

    def body(x_ref, g_ref, w_ref, h_ref, o_ref):
        xv = x_ref[...]
        r = lax.rsqrt(jnp.mean(xv * xv, axis=-1, keepdims=True) + EPS)
        h = (xv * r * g_ref[...]).astype(BF16)
        h_ref[...] = h
        for j in range(nj):
            o_ref[j] = _dot(h, w_ref[j]).astype(BF16)

    return _pcall(body, name=name, grid=(1,),
                  in_specs=[pl.BlockSpec((m, d), lambda i: (0, 0)), pl.BlockSpec((1, d), lambda i: (0, 0)),
                            pl.BlockSpec(w.shape, lambda i: (0, 0, 0))],
                  out_specs=[pl.BlockSpec((m, d), lambda i: (0, 0)), pl.BlockSpec((nj, m, n), lambda i: (0, 0, 0))],
                  out_shape=[_sds((m, d), BF16), _sds((nj, m, n), BF16)])(x, g, w)


def _norm_mm_bwd(dy, h, w, x, name):
    nj, m, n = dy.shape
    d = x.shape[1]

    def body(dy_ref, h_ref, w_ref, x_ref, dw_ref, dg_ref):
        dh = None
        for j in range(nj):
            dyb = dy_ref[j].astype(BF16)
            dw_ref[j] = _dot_tn(h_ref[...], dyb).astype(BF16)
            part = _dot_nt(dyb, w_ref[j])
            dh = part if dh is None else dh + part
        xv = x_ref[...]
        xh = xv * lax.rsqrt(jnp.mean(xv * xv, axis=-1, keepdims=True) + EPS)
        dg_ref[...] = jnp.sum(dh * xh, axis=0, keepdims=True)

    return _pcall(body, name=name, grid=(1,),
                  in_specs=[pl.BlockSpec((nj, m, n), lambda i: (0, 0, 0)), pl.BlockSpec((m, d), lambda i: (0, 0)),
                            pl.BlockSpec((nj, d, n), lambda i: (0, 0, 0)), pl.BlockSpec((m, d), lambda i: (0, 0))],
                  out_specs=[pl.BlockSpec((nj, d, n), lambda i: (0, 0, 0)), pl.BlockSpec((1, d), lambda i: (0, 0))],
                  out_shape=[_sds((nj, d, n), BF16), _sds((1, d), F32)])(dy, h, w, x)


def _load_once(src_hbm, dst_vmem, sem):
    @pl.when(pl.program_id(0) == 0)
    def _():
        load = pltpu.make_async_copy(src_hbm, dst_vmem, sem)
        load.start()
        load.wait()


def _resident(w):
    return [pltpu.VMEM(w.shape, w.dtype), pltpu.SemaphoreType.DMA(())]


def _ffn_up(h, w4, name, tm=512, carry=None):
    s, d = h.shape
    tm = min(tm, s)

    def body(h_ref, w_hbm, gu_ref, a_ref, w_ref, w_sem):
        _load_once(w_hbm, w_ref, w_sem)
        for p in range(4):
            for rows in _row_chunks(tm):
                hv = h_ref[rows, :]
                g = _dot_nt(hv, w_ref[0, p])
                u = _dot_nt(hv, w_ref[1, p])
                gu_ref[0, p, rows, :] = g.astype(BF16)
                gu_ref[1, p, rows, :] = u.astype(BF16)
                a_ref[p, rows, :] = (g * jax.nn.sigmoid(g) * u).astype(BF16)

    call = _pcall(body, name=name, grid=(s // tm,),
                  in_specs=[pl.BlockSpec((tm, d), lambda i: (i, 0)), HBM_SPEC],
                  out_specs=[pl.BlockSpec((2, 4, tm, FS), lambda i: (0, 0, i, 0)),
                             pl.BlockSpec((4, tm, FS), lambda i: (0, i, 0))],
                  out_shape=[_sds((2, 4, s, FS), BF16), _sds((4, s, FS), BF16)], scratch=_resident(w4), carry=carry)
    return _carried(call, (h, w4), carry)


N_SEG = 5
IN_PROJ_WEIGHTS = [pltpu.VMEM((NQ, D), BF16), pltpu.VMEM((NKV, D), BF16), pltpu.VMEM((N_SEG, D, D), BF16),
                   pltpu.SemaphoreType.DMA((2 + N_SEG,))]


def _load_in_proj(w_hbm, wq_ref, wkv_ref, wa_ref, sems):
    @pl.when(pl.program_id(0) == 0)
    def _():
        loads = [pltpu.make_async_copy(w_hbm.at[pl.ds(0, NQ)], wq_ref, sems.at[0]),
                 pltpu.make_async_copy(w_hbm.at[pl.ds(NQ, NKV)], wkv_ref, sems.at[1])]
        loads += [pltpu.make_async_copy(w_hbm.at[pl.ds(NQ + NKV + D * j, D)], wa_ref.at[j], sems.at[2 + j])
                  for j in range(N_SEG)]
        for load in loads:
            load.start()
        for load in loads:
            load.wait()


def _in_proj(h, w_in_t, name, tm=512, carry=None):
    s, d = h.shape
    tm = min(tm, s)

    def body(h_ref, w_hbm, pa_ref, q_ref, kv_ref, wq_ref, wkv_ref, wa_ref, sems):
        _load_in_proj(w_hbm, wq_ref, wkv_ref, wa_ref, sems)
        hv = h_ref[...]
        q_ref[...] = _dot_nt(hv, wq_ref[...]).astype(BF16)
        kv_ref[...] = _dot_nt(hv, wkv_ref[...]).astype(BF16)
        for j in range(N_SEG):
            pa_ref[j] = _dot_nt(hv, wa_ref[j]).astype(BF16)

    call = _pcall(body, name=name, grid=(s // tm,), carry=carry,
                  in_specs=[pl.BlockSpec((tm, d), lambda i: (i, 0)), HBM_SPEC],
                  out_specs=[pl.BlockSpec((N_SEG, tm, d), lambda i: (0, i, 0)),
                             pl.BlockSpec((tm, NQ), lambda i: (i, 0)), pl.BlockSpec((tm, NKV), lambda i: (i, 0))],
                  out_shape=[_sds((N_SEG, s, d), BF16), _sds((s, NQ), BF16), _sds((s, NKV), BF16)],
                  scratch=IN_PROJ_WEIGHTS)
    return _carried(call, (h, w_in_t), carry)


def _mm_res_norm(a, w, xres, gain, scale, name, tm=512, carry=None):
    npart, s, kp = a.shape
    tm = min(tm, s)

    def body(a_ref, w_ref, x_ref, g_ref, xo_ref, h_ref):
        for rows in _row_chunks(tm):
            acc = _dot(a_ref[0, rows, :], w_ref[0])
            for p in range(1, npart):
                acc = acc + _dot(a_ref[p, rows, :], w_ref[p])
            xn = x_ref[rows, :] + scale * acc
            xo_ref[rows, :] = xn
            r = lax.rsqrt(jnp.mean(xn * xn, axis=-1, keepdims=True) + EPS)
            h_ref[rows, :] = (xn * r * g_ref[...]).astype(BF16)

    call = _pcall(body, name=name, grid=(s // tm,),
                  in_specs=[pl.BlockSpec((npart, tm, kp), lambda i: (0, i, 0)),
                            pl.BlockSpec((npart, kp, D), lambda i: (0, 0, 0)),
                            pl.BlockSpec((tm, D), lambda i: (i, 0)),
                            pl.BlockSpec((1, D), lambda i: (0, 0))],
                  out_specs=[pl.BlockSpec((tm, D), lambda i: (i, 0)), pl.BlockSpec((tm, D), lambda i: (i, 0))],
                  out_shape=[_sds((s, D), F32), _sds((s, D), BF16)], carry=carry)
    return _carried(call, (a, w, xres, gain), carry)


def _ffn_down_loss(a, w, xres, gain, target, name, tm=512):
    npart, s, kp = a.shape
    tm = min(tm, s)

    def body(a_ref, w_ref, x_ref, g_ref, t_ref, dx_ref, dxb_ref, loss_ref, dg_ref):
        @pl.when(pl.program_id(0) == 0)
        def _():
            loss_ref[...] = jnp.zeros_like(loss_ref)
            dg_ref[...] = jnp.zeros_like(dg_ref)

        for rows in _row_chunks(tm):
            acc = _dot(a_ref[0, rows, :], w_ref[0])
            for p in range(1, npart):
                acc = acc + _dot(a_ref[p, rows, :], w_ref[p])
            xn = x_ref[rows, :] + 0.5 * acc
            r = lax.rsqrt(jnp.mean(xn * xn, axis=-1, keepdims=True) + EPS)
            xh = xn * r
            gv = g_ref[...]
            err = xh * gv - t_ref[rows, :]
            part = 0.5 * jnp.sum(jnp.mean(err * err, axis=-1, keepdims=True), axis=0, keepdims=True)
            dy = err * (1.0 / D)
            dyg = dy * gv
            dxn = r * (dyg - xh * jnp.mean(dyg * xh, axis=-1, keepdims=True))
            dx_ref[rows, :] = dxn
            dxb_ref[rows, :] = dxn.astype(BF16)
            loss_ref[...] += jnp.broadcast_to(part, loss_ref.shape)
            dg_ref[...] += jnp.sum(dy * xh, axis=0, keepdims=True)

    return _pcall(body, name=name, grid=(s // tm,),
                  in_specs=[pl.BlockSpec((npart, tm, kp), lambda i: (0, i, 0)),
                            pl.BlockSpec((npart, kp, D), lambda i: (0, 0, 0)),
                            pl.BlockSpec((tm, D), lambda i: (i, 0)),
                            pl.BlockSpec((1, D), lambda i: (0, 0)),
                            pl.BlockSpec((tm, D), lambda i: (i, 0))],
                  out_specs=[pl.BlockSpec((tm, D), lambda i: (i, 0)), pl.BlockSpec((tm, D), lambda i: (i, 0)),
                             pl.BlockSpec((8, 128), lambda i: (0, 0)), pl.BlockSpec((1, D), lambda i: (0, 0))],
                  out_shape=[_sds((s, D), F32), _sds((s, D), BF16), _sds((8, 128), F32), _sds((1, D), F32)],
                  )(a, w, xres, gain, target)


def _window_tiles():
    i = lax.broadcasted_iota(jnp.int32, (BLK, BLK), 0)
    j = lax.broadcasted_iota(jnp.int32, (BLK, BLK), 1)
    rel = (i - j) & (BLK - 1)
    large = jnp.full_like(rel, REL_EXACT)
    for t in BUCKET_THRESHOLDS:
        large = large + (rel >= t).astype(jnp.int32)
    return j <= i, jnp.where(rel < REL_EXACT, rel, large)


def _bias_build(rel_bias, name):
    def body(rb_ref, o_ref):
        _, bucket = _window_tiles()

        def per_head(h, carry):
            acc = jnp.zeros((BLK, BLK), F32)
            for b in range(REL_BUCKETS):
                acc = jnp.where(bucket == b, rb_ref[b, h], acc)
            o_ref[h] = acc
            return carry

        lax.fori_loop(0, N_HEADS, per_head, 0)

    return _pcall(body, name=name, grid=(1,),
                  in_specs=[pl.BlockSpec(memory_space=pltpu.SMEM)],
                  out_specs=pl.BlockSpec((N_HEADS, BLK, BLK), lambda i: (0, 0, 0)),
                  out_shape=_sds((N_HEADS, BLK, BLK), F32))(rel_bias)


def _bias_bwd(dbias, name):
    def body(db_ref, o_ref):
        _, bucket = _window_tiles()
        lane = lax.broadcasted_iota(jnp.int32, (N_HEADS, 128), 1)

        def per_bucket(b, out):
            mb = (bucket == b).astype(F32)
            per_col = jnp.sum(db_ref[...] * mb[None, :, :], axis=1)
            return jnp.where(lane == b, jnp.sum(per_col, axis=1, keepdims=True), out)

        o_ref[...] = lax.fori_loop(0, REL_BUCKETS, per_bucket, jnp.zeros((N_HEADS, 128), F32))

    return _pcall(body, name=name, grid=(1,),
                  in_specs=[pl.BlockSpec((N_HEADS, BLK, BLK), lambda i: (0, 0, 0))],
                  out_specs=pl.BlockSpec((N_HEADS, 128), lambda i: (0, 0)),
                  out_shape=_sds((N_HEADS, 128), F32))(dbias)


PAIR = 2 * HEAD
GROUP = N_HEADS // N_KV
SWA_SCALE = HEAD ** -0.5


def _window_masks(n):
    i = lax.broadcasted_iota(jnp.int32, (GROUP * BLK, BLK), 0) & (BLK - 1)
    j = lax.broadcasted_iota(jnp.int32, (GROUP * BLK, BLK), 1)
    return j <= i, jnp.logical_and(n == 0, j > i), j < HEAD


def _kv_twice(ref, base, g, low):
    slab = ref[:, base + PAIR * (g // 2): base + PAIR * (g // 2 + 1)]
    swapped = pltpu.roll(slab, HEAD, 1)
    return jnp.where(low, slab, swapped) if g % 2 == 0 else jnp.where(low, swapped, slab)


def _stack_heads(ref, g, low):
    parts = []
    for r in range(2):
        slab = ref[:, PAIR * (2 * g + r): PAIR * (2 * g + r + 1)]
        zero = jnp.zeros_like(slab)
        parts += [jnp.where(low, slab, zero), jnp.where(low, zero, slab)]
    return jnp.concatenate(parts, axis=0)


def _unstack_heads(t, low):
    return [jnp.where(low, t[2 * r * BLK:(2 * r + 1) * BLK], t[(2 * r + 1) * BLK:(2 * r + 2) * BLK])
            for r in range(2)]


def _head_rows(t, k):
    return t[k * BLK:(k + 1) * BLK]


def _per_head_column(values):
    head = lax.broadcasted_iota(jnp.int32, (GROUP * BLK, 1), 0) // BLK
    col = jnp.full((GROUP * BLK, 1), values[0], F32)
    for k in range(1, GROUP):
        col = jnp.where(head == k, values[k], col)
    return col


def _window_logits(q4, kc, kp, bias4, own, absent):
    sc = jnp.where(own, _dot_nt(q4, kc), _dot_nt(q4, kp)) * SWA_SCALE + bias4
    return jnp.where(absent, NEG, sc)


def _split_window(t, own):
    zero = jnp.zeros_like(t)
    return jnp.where(own, t, zero), jnp.where(own, zero, t)


def _swa_fwd(q, kv, bias, sinks, name, carry=None):
    s = q.shape[0]
    nb = s // BLK
    kvw = 2 * N_KV * HEAD

    def body(q_ref, kc_ref, kp_ref, b_ref, sk_ref, o_ref, lse_ref):
        own, absent, low4 = _window_masks(pl.program_id(0))
        low = low4[:BLK]
        lane = lax.broadcasted_iota(jnp.int32, (BLK, 128), 1)
        lse_t = jnp.zeros((BLK, 128), F32)
        for g in range(N_KV):
            q4 = _stack_heads(q_ref, g, low)
            kc, kp = _kv_twice(kc_ref, 0, g, low), _kv_twice(kp_ref, 0, g, low)
            vc, vp = _kv_twice(kc_ref, N_KV * HEAD, g, low), _kv_twice(kp_ref, N_KV * HEAD, g, low)
            bias4 = b_ref[GROUP * g:GROUP * (g + 1)].reshape(GROUP * BLK, BLK)
            sc = _window_logits(q4, kc, kp, bias4, own, absent)
            sk = _per_head_column([sk_ref[0, GROUP * g + k] for k in range(GROUP)])
            m = jnp.maximum(jnp.max(sc, axis=1, keepdims=True), sk)
            p = jnp.exp(sc - m)
            l = jnp.sum(p, axis=1, keepdims=True) + jnp.exp(sk - m)
            p_own, p_prev = _split_window(p.astype(BF16), own)
            out = (_dot(p_own, vc) + _dot(p_prev, vp)) * (1.0 / l)
            for r, slab in enumerate(_unstack_heads(out, low)):
                o_ref[:, PAIR * (2 * g + r): PAIR * (2 * g + r + 1)] = slab.astype(BF16)
            lse4 = m + jnp.log(l)
            for k in range(GROUP):
                lse_t = jnp.where(lane == GROUP * g + k, _head_rows(lse4, k), lse_t)
        lse_ref[...] = lse_t

    call = _pcall(body, name=name, grid=(nb,),
                  in_specs=[pl.BlockSpec((BLK, D), lambda n: (n, 0)),
                            pl.BlockSpec((BLK, kvw), lambda n: (n, 0)),
                            pl.BlockSpec((BLK, kvw), lambda n: (jnp.maximum(n - 1, 0), 0)),
                            pl.BlockSpec((N_HEADS, BLK, BLK), lambda n: (0, 0, 0)),
                            pl.BlockSpec(memory_space=pltpu.SMEM)],
                  out_specs=[pl.BlockSpec((BLK, D), lambda n: (n, 0)), pl.BlockSpec((BLK, 128), lambda n: (n, 0))],
                  out_shape=[_sds((s, D), BF16), _sds((s, 128), F32)], carry=carry)
    return _carried(call, (q, kv, kv, bias, sinks), carry)


def _fold_halves(t, g, low):
    folded = jnp.where(low, t, 0.0) + pltpu.roll(jnp.where(low, 0.0, t), HEAD, 1)
    return folded if g % 2 == 0 else pltpu.roll(folded, HEAD, 1)


def _swa_bwd(q, kv, attn, dattn, lse, bias, sinks, name, carry=None):
    s = q.shape[0]
    nb = s // BLK
    kvw = 2 * N_KV * HEAD
    voff = N_KV * HEAD

    def body(q_ref, kc_ref, kp_ref, o_ref, do_ref, lse_ref, b_ref, skrow_ref, dq_ref, dkv_ref, dbias_ref, dsk_ref,
             dq_hold, kv_hold, dq_new, kv_prev, kv_cur):
        n = pl.program_id(0)

        @pl.when(n == 0)
        def _():
            dbias_ref[...] = jnp.zeros_like(dbias_ref)
            dsk_ref[...] = jnp.zeros_like(dsk_ref)
            dq_hold[...] = jnp.zeros_like(dq_hold)
            kv_hold[...] = jnp.zeros_like(kv_hold)

        @pl.when(n < nb)
        def _():
            own, absent, low4 = _window_masks(n)
            low = low4[:BLK]
            lane = lax.broadcasted_iota(jnp.int32, (BLK, 128), 1)
            delta_t = jnp.zeros((BLK, 128), F32)
            ones = jnp.ones((PAIR, 128), BF16)
            for pair_of_kv in range(N_KV // 2):
                slab_grads = [jnp.zeros((BLK, PAIR), F32) for _ in range(4)]
                for g in (2 * pair_of_kv, 2 * pair_of_kv + 1):
                    q4, do4 = _stack_heads(q_ref, g, low), _stack_heads(do_ref, g, low)
                    kc, kp = _kv_twice(kc_ref, 0, g, low), _kv_twice(kp_ref, 0, g, low)
                    vc, vp = _kv_twice(kc_ref, voff, g, low), _kv_twice(kp_ref, voff, g, low)
                    o_slabs = [o_ref[:, PAIR * (2 * g + r): PAIR * (2 * g + r + 1)] for r in range(2)]
                    o4 = jnp.concatenate([o_slabs[0], o_slabs[0], o_slabs[1], o_slabs[1]], axis=0)
                    delta = _dot(do4 * o4, ones)
                    heads = range(GROUP * g, GROUP * (g + 1))
                    lse4 = jnp.concatenate([lse_ref[:, h:h + 1] for h in heads], axis=0)
                    bias4 = b_ref[GROUP * g:GROUP * (g + 1)].reshape(GROUP * BLK, BLK)
                    p = jnp.exp(_window_logits(q4, kc, kp, bias4, own, absent) - lse4)
                    dp = jnp.where(own, _dot_nt(do4, vc), _dot_nt(do4, vp))
                    ds = p * (dp - delta)
                    dbias_ref[GROUP * g:GROUP * (g + 1)] += ds.reshape(GROUP, BLK, BLK)
                    for k, h in enumerate(heads):
                        delta_t = jnp.where(lane == h, _head_rows(delta, k), delta_t)
                    ds_own, ds_prev = _split_window((ds * SWA_SCALE).astype(BF16), own)
                    p_own, p_prev = _split_window(p.astype(BF16), own)
                    dq4 = _dot(ds_own, kc) + _dot(ds_prev, kp)
                    for r, slab in enumerate(_unstack_heads(dq4, low)):
                        dq_new[:, PAIR * (2 * g + r): PAIR * (2 * g + r + 1)] = slab
                    grads = [_dot_tn(ds_own, q4), _dot_tn(ds_prev, q4), _dot_tn(p_own, do4), _dot_tn(p_prev, do4)]
                    slab_grads = [t + _fold_halves(dk, g, low) for t, dk in zip(slab_grads, grads)]
                ks = slice(PAIR * pair_of_kv, PAIR * (pair_of_kv + 1))
                vs = slice(voff + PAIR * pair_of_kv, voff + PAIR * (pair_of_kv + 1))
                kv_cur[:, ks], kv_prev[:, ks], kv_cur[:, vs], kv_prev[:, vs] = slab_grads
            dsk_ref[...] -= jnp.sum(jnp.exp(skrow_ref[...] - lse_ref[...]) * delta_t, axis=0, keepdims=True)

        @pl.when(n == nb)
        def _():
            kv_prev[...] = jnp.zeros_like(kv_prev)

        dq_ref[...] = dq_hold[...].astype(BF16)
        dkv_ref[...] = (kv_hold[...] + kv_prev[...]).astype(BF16)

        @pl.when(n < nb)
        def _():
            dq_hold[...] = dq_new[...]
            kv_hold[...] = kv_cur[...]

    def cur(n):
        return jnp.minimum(n, nb - 1)

    call = _pcall(body, name=name, grid=(nb + 1,), carry=carry,
                  in_specs=[pl.BlockSpec((BLK, D), lambda n: (cur(n), 0)),
                            pl.BlockSpec((BLK, kvw), lambda n: (cur(n), 0)),
                            pl.BlockSpec((BLK, kvw), lambda n: (jnp.maximum(cur(n) - 1, 0), 0)),
                            pl.BlockSpec((BLK, D), lambda n: (cur(n), 0)),
                            pl.BlockSpec((BLK, D), lambda n: (cur(n), 0)),
                            pl.BlockSpec((BLK, 128), lambda n: (cur(n), 0)),
                            pl.BlockSpec((N_HEADS, BLK, BLK), lambda n: (0, 0, 0)),
                            pl.BlockSpec((1, 128), lambda n: (0, 0))],
                  out_specs=[pl.BlockSpec((BLK, D), lambda n: (jnp.maximum(n - 1, 0), 0)),
                             pl.BlockSpec((BLK, kvw), lambda n: (jnp.maximum(n - 1, 0), 0)),
                             pl.BlockSpec((N_HEADS, BLK, BLK), lambda n: (0, 0, 0)),
                             pl.BlockSpec((1, 128), lambda n: (0, 0))],
                  out_shape=[_sds((s, D), BF16), _sds((s, kvw), BF16), _sds((N_HEADS, BLK, BLK), F32),
                             _sds((1, 128), F32)],
                  scratch=[pltpu.VMEM((BLK, D), F32), pltpu.VMEM((BLK, kvw), F32), pltpu.VMEM((BLK, D), F32),
                           pltpu.VMEM((BLK, kvw), F32), pltpu.VMEM((BLK, kvw), F32)])
    sink_row = jnp.pad(sinks, ((0, 0), (0, 128 - N_HEADS)))
    return _carried(call, (q, kv, kv, attn, dattn, lse, bias, sink_row), carry)


HALO = 16
CW = D


def _conv_taps(cu, halo_cu, first_tile):
    row = lax.broadcasted_iota(jnp.int32, cu.shape, 0)
    halo_cu = jnp.where(first_tile, 0.0, halo_cu)
    c1 = jnp.where(row == 0, halo_cu[HALO - 1:HALO], pltpu.roll(cu, 1, 0))
    c2 = jnp.where(row == 0, halo_cu[HALO - 2:HALO - 1],
                   jnp.where(row == 1, halo_cu[HALO - 1:HALO], pltpu.roll(cu, 2, 0)))
    return c1, c2


def _conv_merge_fwd(pa, attn, convw, name, ts=256, carry=None):
    _, s, _ = pa.shape
    ts = min(ts, s)
    hb = ts // HALO

    def body(pa_ref, hp_ref, at_ref, w_ref, o_ref):
        i = pl.program_id(1)
        cu = pa_ref[0].astype(F32) * pa_ref[2].astype(F32)
        c1, c2 = _conv_taps(cu, hp_ref[0].astype(F32) * hp_ref[2].astype(F32), i == 0)
        w = w_ref[...]
        c3 = w[0:1] * c2 + w[1:2] * c1 + w[2:3] * cu
        conv = pa_ref[1].astype(F32) * c3
        o_ref[...] = (jax.nn.sigmoid(pa_ref[3].astype(F32)) * at_ref[...].astype(F32)
                      + jax.nn.sigmoid(pa_ref[4].astype(F32)) * conv).astype(BF16)

    call = _pcall(body, name=name, grid=(D // CW, s // ts), carry=carry,
                  in_specs=[pl.BlockSpec((5, ts, CW), lambda c, i: (0, i, c)),
                            pl.BlockSpec((5, HALO, CW), lambda c, i: (0, jnp.maximum(i * hb - 1, 0), c)),
                            pl.BlockSpec((ts, CW), lambda c, i: (i, c)),
                            pl.BlockSpec((8, CW), lambda c, i: (0, c))],
                  out_specs=pl.BlockSpec((ts, CW), lambda c, i: (i, c)),
                  out_shape=_sds((s, D), BF16))
    return _carried(call, (pa, pa, attn, convw), carry)


def _conv_merge_bwd(dmerged, pa, attn, convw, name, ts=256, carry=None):
    _, s, _ = pa.shape
    ts = min(ts, s)
    hb = ts // HALO
    last_hb = s // HALO - 1

    def body(dm_ref, pa_ref, at_ref, w_ref, hp_ref, hn_ref, dmn_ref, dat_ref, dpa_ref, dw_ref):
        i = pl.program_id(1)
        last = i == pl.num_programs(1) - 1
        dm = dm_ref[...].astype(F32)
        cp, bp, u = pa_ref[0].astype(F32), pa_ref[1].astype(F32), pa_ref[2].astype(F32)
        sa = jax.nn.sigmoid(pa_ref[3].astype(F32))
        sc = jax.nn.sigmoid(pa_ref[4].astype(F32))
        at = at_ref[...].astype(F32)
        cu = cp * u
        c1, c2 = _conv_taps(cu, hp_ref[0].astype(F32) * hp_ref[2].astype(F32), i == 0)
        w = w_ref[...]
        c3 = w[0:1] * c2 + w[1:2] * c1 + w[2:3] * cu
        dconv = dm * sc
        dc3 = dconv * bp
        nxt = dmn_ref[...].astype(F32) * jax.nn.sigmoid(hn_ref[4].astype(F32)) * hn_ref[1].astype(F32)
        nxt = jnp.where(last, 0.0, nxt)
        row = lax.broadcasted_iota(jnp.int32, dc3.shape, 0)
        d1 = jnp.where(row == ts - 1, nxt[0:1], pltpu.roll(dc3, ts - 1, 0))
        d2 = jnp.where(row == ts - 2, nxt[0:1], jnp.where(row == ts - 1, nxt[1:2], pltpu.roll(dc3, ts - 2, 0)))
        dcu = w[2:3] * dc3 + w[1:2] * d1 + w[0:1] * d2
        dat_ref[...] = (dm * sa).astype(BF16)
        dpa_ref[0] = (dcu * u).astype(BF16)
        dpa_ref[1] = (dconv * c3).astype(BF16)
        dpa_ref[2] = (dcu * cp).astype(BF16)
        dpa_ref[3] = (dm * at * sa * (1.0 - sa)).astype(BF16)
        dpa_ref[4] = (dm * bp * c3 * sc * (1.0 - sc)).astype(BF16)

        @pl.when(i == 0)
        def _():
            dw_ref[...] = jnp.zeros_like(dw_ref)

        dw_ref[0:1, :] += jnp.sum(dc3 * c2, axis=0, keepdims=True)
        dw_ref[1:2, :] += jnp.sum(dc3 * c1, axis=0, keepdims=True)
        dw_ref[2:3, :] += jnp.sum(dc3 * cu, axis=0, keepdims=True)

    call = _pcall(body, name=name, grid=(D // CW, s // ts), carry=carry,
                  in_specs=[pl.BlockSpec((ts, CW), lambda c, i: (i, c)),
                            pl.BlockSpec((5, ts, CW), lambda c, i: (0, i, c)),
                            pl.BlockSpec((ts, CW), lambda c, i: (i, c)),
                            pl.BlockSpec((8, CW), lambda c, i: (0, c)),
                            pl.BlockSpec((5, HALO, CW), lambda c, i: (0, jnp.maximum(i * hb - 1, 0), c)),
                            pl.BlockSpec((5, HALO, CW), lambda c, i: (0, jnp.minimum((i + 1) * hb, last_hb), c)),
                            pl.BlockSpec((HALO, CW), lambda c, i: (jnp.minimum((i + 1) * hb, last_hb), c))],
                  out_specs=[pl.BlockSpec((ts, CW), lambda c, i: (i, c)),
                             pl.BlockSpec((5, ts, CW), lambda c, i: (0, i, c)),
                             pl.BlockSpec((8, CW), lambda c, i: (0, c))],
                  out_shape=[_sds((s, D), BF16), _sds((5, s, D), BF16), _sds((8, D), F32)])
    return _carried(call, (dmerged, pa, attn, convw, pa, pa, dmerged), carry)


def _xattn_fwd(hx, wq, kv, name, tq=1024):
    s, _ = hx.shape
    nm = kv.shape[1]
    tq = min(tq, s)

    def body(h_ref, wq_ref, kv_ref, q_ref, o_ref, lse_ref):
        q_ref[...] = _dot(h_ref[...], wq_ref[...]).astype(BF16)
        lane = lax.broadcasted_iota(jnp.int32, (tq, 128), 1)
        lse_t = jnp.zeros((tq, 128), F32)
        for h in range(XH):
            hs = slice(XHD * h, XHD * (h + 1))
            sc = _dot_nt(q_ref[:, hs], kv_ref[h]) * (XHD ** -0.5)
            m = jnp.max(sc, axis=1, keepdims=True)
            p = jnp.exp(sc - m)
            l = jnp.sum(p, axis=1, keepdims=True)
            o_ref[:, hs] = (_dot(p.astype(BF16), kv_ref[XH + h]) * (1.0 / l)).astype(BF16)
            lse_t = jnp.where(lane == h, m + jnp.log(l), lse_t)
        lse_ref[...] = lse_t

    return _pcall(body, name=name, grid=(s // tq,),
                  in_specs=[pl.BlockSpec((tq, D), lambda i: (i, 0)), pl.BlockSpec((D, D), lambda i: (0, 0)),
                            pl.BlockSpec((2 * XH, nm, XHD), lambda i: (0, 0, 0))],
                  out_specs=[pl.BlockSpec((tq, D), lambda i: (i, 0)), pl.BlockSpec((tq, D), lambda i: (i, 0)),
                             pl.BlockSpec((tq, 128), lambda i: (i, 0))],
                  out_shape=[_sds((s, D), BF16), _sds((s, D), BF16), _sds((s, 128), F32)])(hx, wq, kv)


def _xattn_bwd(q, kv, o, dy, wo, lse, name, tq=512, carry=None):
    s, _ = q.shape
    nm = kv.shape[1]
    tq = min(tq, s)

    def body(q_ref, kv_ref, o_ref, dy_ref, wo_ref, lse_ref, dq_ref, dkv_ref, do_ref):
        @pl.when(pl.program_id(0) == 0)
        def _():
            dkv_ref[...] = jnp.zeros_like(dkv_ref)

        do_ref[...] = _dot_nt(dy_ref[...], wo_ref[...]).astype(BF16)
        for h in range(XH):
            hs = slice(XHD * h, XHD * (h + 1))
            qh, kh, vh, dob = q_ref[:, hs], kv_ref[h], kv_ref[XH + h], do_ref[:, hs]
            p = jnp.exp(_dot_nt(qh, kh) * (XHD ** -0.5) - lse_ref[:, h:h + 1])
            dp = _dot_nt(dob, vh)
            delta = jnp.sum(dob.astype(F32) * o_ref[:, hs].astype(F32), axis=1, keepdims=True)
            dsb = (p * (dp - delta) * (XHD ** -0.5)).astype(BF16)
            dq_ref[:, hs] = _dot(dsb, kh).astype(BF16)
            dkv_ref[h] += _dot_tn(dsb, qh)
            dkv_ref[XH + h] += _dot_tn(p.astype(BF16), dob)

    call = _pcall(body, name=name, grid=(s // tq,), carry=carry,
                  in_specs=[pl.BlockSpec((tq, D), lambda i: (i, 0)), pl.BlockSpec((2 * XH, nm, XHD), lambda i: (0, 0, 0)),
                            pl.BlockSpec((tq, D), lambda i: (i, 0)), pl.BlockSpec((tq, D), lambda i: (i, 0)),
                            pl.BlockSpec((D, D), lambda i: (0, 0)), pl.BlockSpec((tq, 128), lambda i: (i, 0))],
                  out_specs=[pl.BlockSpec((tq, D), lambda i: (i, 0)), pl.BlockSpec((2 * XH, nm, XHD), lambda i: (0, 0, 0))],
                  out_shape=[_sds((s, D), BF16), _sds((2 * XH, nm, XHD), F32)], scratch=[pltpu.VMEM((tq, D), BF16)])
    return _carried(call, (q, kv, o, dy, wo, lse), carry)


def _ffn_down_bwd(dxb, wd4, gu4, name, tm=512, carry=None, behind=None):
    s, _ = dxb.shape
    tm = min(tm, s)

    def body(dx_ref, w_hbm, gu_ref, o_ref, w_ref, w_sem):
        _load_once(w_hbm, w_ref, w_sem)
        for p in range(4):
            for rows in _row_chunks(tm):
                da = _dot_nt(dx_ref[rows, :], w_ref[p])
                g = gu_ref[0, p, rows, :].astype(F32)
                u = gu_ref[1, p, rows, :].astype(F32)
                sg = jax.nn.sigmoid(g)
                t = da * sg
                o_ref[0, p, rows, :] = (t * u * (1.0 + g - g * sg)).astype(BF16)
                o_ref[1, p, rows, :] = (t * g).astype(BF16)

    block = pl.BlockSpec((2, 4, tm, FS), lambda i: (0, 0, i, 0))
    call = _pcall(body, name=name, grid=(s // tm,), carry=carry, behind=behind,
                  in_specs=[pl.BlockSpec((tm, D), lambda i: (i, 0)), HBM_SPEC, block],
                  out_specs=block, out_shape=_sds((2, 4, s, FS), BF16), scratch=_resident(wd4))
    return _carried(call, (dxb, wd4, gu4), carry)


def _mm_tn(a, b, name, scale=1.0, carry=None):
    pa_n, s, m = a.shape
    pb_n, _, n = b.shape
    po = max(pa_n, pb_n)
    tk = 1024
    if po == 1 and s > tk and s % tk == 0:
        def body_k(a_ref, b_ref, o_ref, acc_ref):
            k = pl.program_id(0)
            part = _dot_tn(a_ref[...], b_ref[...])

            @pl.when(k == 0)
            def _():
                acc_ref[...] = part

            @pl.when(k > 0)
            def _():
                acc_ref[...] += part

            @pl.when(k == s // tk - 1)
            def _():
                o_ref[...] = (scale * acc_ref[...]).astype(BF16)

        call = _pcall(body_k, name=name, grid=(s // tk,), carry=carry,
                      in_specs=[pl.BlockSpec((None, tk, m), lambda k: (0, k, 0)),
                                pl.BlockSpec((None, tk, n), lambda k: (0, k, 0))],
                      out_specs=pl.BlockSpec((None, m, n), lambda k: (0, 0, 0)),
                      out_shape=_sds((1, m, n), BF16), scratch=[pltpu.VMEM((m, n), F32)])
        return _carried(call, (a, b), carry)
    tn = n if po >= 4 else min(n, 256)

    def body(a_ref, b_ref, o_ref):
        o_ref[...] = (scale * _dot_tn(a_ref[...], b_ref[...])).astype(BF16)

    call = _pcall(body, name=name, grid=(po, n // tn), carry=carry,
                  in_specs=[pl.BlockSpec((None, s, m), lambda o, j: (o if pa_n > 1 else 0, 0, 0)),
                            pl.BlockSpec((None, s, tn), lambda o, j: (o if pb_n > 1 else 0, 0, j))],
                  out_specs=pl.BlockSpec((None, m, tn), lambda o, j: (o, 0, j)),
                  out_shape=_sds((po, m, n), BF16))
    return _carried(call, (a, b), carry)


def _mm_tn_rows(a, b, name, total_rows, row0, begun=None, tm=512, carry=None):
    p, s, m = a.shape
    n = b.shape[1]
    tm = min(tm, m)
    tiles = m // tm
    assert row0 % tm == 0 and m % tm == 0, (row0, m, tm)

    def body(a_ref, b_ref, *rest):
        rest[-1][...] = _dot_tn(a_ref[...], b_ref[...]).astype(BF16)

    in_specs = [pl.BlockSpec((None, s, tm), lambda o, i: (o, 0, i)), pl.BlockSpec((s, n), lambda o, i: (0, 0))]
    call = _pcall(body, name=name, grid=(p, tiles), in_specs=in_specs + ([HBM_SPEC] if begun is not None else []),
                  out_specs=pl.BlockSpec((tm, n), lambda o, i: (row0 // tm + o * tiles + i, 0)),
                  out_shape=_sds((total_rows, n), BF16), aliases={2: 0} if begun is not None else None, carry=carry)
    return _carried(call, (a, b, begun) if begun is not None else (a, b), carry)


def _sum_dots(a_ref, b_ref, nj, bt, rows=slice(None)):
    dot = _dot_nt if bt else _dot
    acc = dot(a_ref[0, rows, :], b_ref[0])
    for j in range(1, nj):
        acc = acc + dot(a_ref[j, rows, :], b_ref[j])
    return acc


def _mm_acc(a, b, name, out_dtype, tm=1024, bt=False, carry=None):
    nj, s, k = a.shape
    n = b.shape[1] if bt else b.shape[2]
    tm = min(tm, s)

    def body(a_ref, b_ref, o_ref):
        o_ref[...] = _sum_dots(a_ref, b_ref, nj, bt).astype(out_dtype)

    call = _pcall(body, name=name, grid=(s // tm,), carry=carry,
                  in_specs=[pl.BlockSpec((nj, tm, k), lambda i: (0, i, 0)),
                            pl.BlockSpec(b.shape, lambda i: (0, 0, 0))],
                  out_specs=pl.BlockSpec((tm, n), lambda i: (i, 0)), out_shape=_sds((s, n), out_dtype))
    return _carried(call, (a, b), carry)


def _rms_bwd_call(name, acts, weights, scratch, load, dh_rows, *, x, gain, dres, tm, carry, behind=None):
    s, n = x.shape
    tm = min(tm, s)
    n_act, n_w = len(acts), len(weights)

    def body(*refs):
        act_refs, w_refs = refs[:n_act], refs[n_act:n_act + n_w]
        x_ref, g_ref, r_ref, dx_ref, dxb_ref, dg_ref = refs[n_act + n_w:n_act + n_w + 6]
        held = refs[n_act + n_w + 6:]
        load(w_refs, held)

        @pl.when(pl.program_id(0) == 0)
        def _():
            dg_ref[...] = jnp.zeros_like(dg_ref)

        for rows in _row_chunks(tm):
            dh = dh_rows(act_refs, held, rows)
            xv = x_ref[rows, :]
            r = lax.rsqrt(jnp.mean(xv * xv, axis=-1, keepdims=True) + EPS)
            xh = xv * r
            dyg = dh * g_ref[...]
            dx = r_ref[rows, :] + r * (dyg - xh * jnp.mean(dyg * xh, axis=-1, keepdims=True))
            dx_ref[rows, :] = dx
            dxb_ref[rows, :] = dx.astype(BF16)
            dg_ref[...] += jnp.sum(dh * xh, axis=0, keepdims=True)

    def tile(a):
        return (pl.BlockSpec((tm, a.shape[1]), lambda i: (i, 0)) if a.ndim == 2
                else pl.BlockSpec((a.shape[0], tm, a.shape[2]), lambda i: (0, i, 0)))

    row = pl.BlockSpec((tm, n), lambda i: (i, 0))
    in_specs = [tile(a) for a in acts] + [HBM_SPEC] * n_w + [row, pl.BlockSpec((1, n), lambda i: (0, 0)), row]
    call = _pcall(body, name=name, grid=(s // tm,), in_specs=in_specs, carry=carry, behind=behind,
                  out_specs=[row, row, pl.BlockSpec((1, n), lambda i: (0, 0))],
                  out_shape=[_sds((s, n), F32), _sds((s, n), BF16), _sds((1, n), F32)], scratch=scratch)
    return _carried(call, tuple(acts) + tuple(weights) + (x, gain, dres), carry)


def _mm_acc_rms_bwd(a, b, name, *, x, gain, dres, scale=None, tm=512, bt=False, carry=None, behind=None):
    def load(w_refs, held):
        _load_once(w_refs[0], held[0], held[1])

    def dh_rows(act_refs, held, rows):
        dh = _sum_dots(act_refs[0], held[0], a.shape[0], bt, rows)
        return dh if scale is None else scale * dh

    return _rms_bwd_call(name, [a], [b], _resident(b), load, dh_rows, x=x, gain=gain, dres=dres, tm=tm, carry=carry,
                         behind=behind)


def _in_proj_bwd(dpa, dq, dkv, w_in_t, name, *, x, gain, dres, tm=512, carry=None, behind=None):
    def load(w_refs, held):
        _load_in_proj(w_refs[0], *held)

    def dh_rows(act_refs, held, rows):
        dpa_ref, dq_ref, dkv_ref = act_refs
        wq_ref, wkv_ref, wa_ref, _ = held
        dh = _dot(dq_ref[rows, :], wq_ref[...]) + _dot(dkv_ref[rows, :], wkv_ref[...])
        return dh + _sum_dots(dpa_ref, wa_ref, N_SEG, False, rows)

    return _rms_bwd_call(name, [dpa, dq, dkv], [w_in_t], IN_PROJ_WEIGHTS, load, dh_rows, x=x, gain=gain, dres=dres,
                         tm=tm, carry=carry, behind=behind)


def _adam(w, g, m, v):
    m2 = ADAM_B1 * m + (1.0 - ADAM_B1) * g
    v2 = ADAM_B2 * v + (1.0 - ADAM_B2) * (g * g)
    m_hat = m2 / (1.0 - ADAM_B1 ** ADAM_STEP)
    v_hat = v2 / (1.0 - ADAM_B2 ** ADAM_STEP)
    delta = -ADAM_LR * (m_hat / (jnp.sqrt(v_hat) + ADAM_EPS) + ADAM_WD * w)
    return delta, m2, v2


def _adamw(parts, w, m, v, name, behind=None):
    _, r, c = parts.shape
    tr = max(t for t in range(16, 257, 16) if r % t == 0)

    def body(p_ref, w_ref, m_ref, v_ref, g_ref, d_ref, m2_ref, v2_ref):
        g = p_ref[0].astype(F32)
        for i in range(1, N_DEV):
            g = g + p_ref[i].astype(F32)
        delta, m2, v2 = _adam(w_ref[...], g, m_ref[...], v_ref[...])
        g_ref[...] = g
        d_ref[...] = delta
        m2_ref[...] = m2
        v2_ref[...] = v2

    blk = pl.BlockSpec((tr, c), lambda i: (i, 0))
    return _pcall(body, name=name, grid=(r // tr,), behind=behind,
                  in_specs=[pl.BlockSpec((N_DEV, tr, c), lambda i: (0, i, 0)), blk, blk, blk],
                  out_specs=[blk] * 4, out_shape=[_sds((r, c), F32)] * 4)(parts, w, m, v)


def _position():
    return lax.axis_index("x"), lax.axis_index("y"), lax.axis_index("c")


def _slot(px, py, pc):
    return 4 * px + 2 * py + pc


def _row_window(ref, rows):
    r0, r1 = rows
    return ref if (r0, r1) == (0, ref.shape[0]) else ref.at[pl.ds(r0, r1 - r0)]


def _split_items(items):
    sources = [src for src, _, _ in items]
    begun = [(a, dest) for a, (_, _, dest) in enumerate(items) if dest is not None]
    aliases = {len(sources) + k: a for k, (a, _) in enumerate(begun)}
    return sources + [dest for _, dest in begun], [rows for _, rows, _ in items], aliases


def _gather_carry(items):
    na = len(items)
    carry_ins, windows, aliases = _split_items(items)

    def plan(ins, outs, sems):
        send_sems, recv_sems, local_sems = sems
        x, y, c = _position()
        me, sibling = (x, y, c), (x, y, 1 - c)
        chips = [(1 - x, y), (x, 1 - y), (1 - x, 1 - y)]
        ins = [_row_window(ins[a], windows[a]) for a in range(na)]

        def block_rows(a, block):
            return _row_window(outs[a].at[_slot(*block)], windows[a])

        def copy(a, k, block, to, src=None):
            rows = block_rows(a, block)
            return pltpu.make_async_remote_copy(src_ref=rows if src is None else src, dst_ref=rows,
                                                send_sem=send_sems.at[k, a], recv_sem=recv_sems.at[k, a],
                                                device_id=to, device_id_type=MESH)

        mine = [pltpu.make_async_copy(ins[a], block_rows(a, me), local_sems.at[a]) for a in range(na)]
        first = [copy(a, 0, me, sibling, src=ins[a]) for a in range(na)]
        for j, chip in enumerate(chips):
            first += [copy(a, 1 + j, me, (*chip, c), src=ins[a]) for a in range(na)]
        landed = [[copy(a, 1 + j, (*chip, c), me) for a in range(na)] for j, chip in enumerate(chips)]
        passed = [[copy(a, 4 + j, (*chip, c), sibling) for a in range(na)] for j, chip in enumerate(chips)]
        from_sibling = [copy(a, 0, sibling, me) for a in range(na)]
        for j, chip in enumerate(chips):
            from_sibling += [copy(a, 4 + j, (*chip, 1 - c), me) for a in range(na)]
        return mine, first, landed, passed, from_sibling

    def start(ins, outs, sems):
        mine, first, _, _, _ = plan(ins, outs, sems)
        for cp in mine + first:
            cp.start()

    def mid(ins, outs, sems):
        _, _, landed, passed, _ = plan(ins, outs, sems)
        for over_ici, onward in zip(landed, passed):
            for cp, fwd in zip(over_ici, onward):
                cp.wait_recv()
                fwd.start()

    def finish(ins, outs, sems):
        mine, first, _, passed, from_sibling = plan(ins, outs, sems)
        for cp in from_sibling:
            cp.wait_recv()
        for cp in first + [fwd for onward in passed for fwd in onward]:
            cp.wait_send()
        for cp in mine:
            cp.wait()

    return _Carry(carry_ins, [_sds((N_DEV,) + src.shape, src.dtype) for src, _, _ in items],
                  [pltpu.SemaphoreType.DMA((7, na)), pltpu.SemaphoreType.DMA((7, na)),
                   pltpu.SemaphoreType.DMA((na,))], start, finish, mid, aliases)


def _exchange_carry(scattered, replicated=()):
    items = list(scattered) + [(a, (0, a.shape[0]), None) for a in replicated]
    na, ns = len(items), len(scattered)
    carry_ins, windows, aliases = _split_items(items)

    def plan(ins, outs, sems):
        send_sems, recv_sems, local_sems = sems
        me = _slot(*_position())

        def source(a, j):
            return _row_window(ins[a].at[j] if a < ns else ins[a], windows[a])

        def copy(a, j, i):
            return pltpu.make_async_remote_copy(src_ref=source(a, j), dst_ref=_row_window(outs[a].at[i], windows[a]),
                                                send_sem=send_sems.at[j, a], recv_sem=recv_sems.at[i, a],
                                                device_id=(j >> 2, (j >> 1) & 1, j & 1), device_id_type=MESH)

        def own(a, j):
            return pltpu.make_async_copy(source(a, j), _row_window(outs[a].at[j], windows[a]), local_sems.at[a])

        return me, copy, own

    def start(ins, outs, sems):
        me, copy, own = plan(ins, outs, sems)
        for a in range(na):
            for j in range(N_DEV):
                @pl.when(me == j)
                def _():
                    own(a, j).start()

                @pl.when(me != j)
                def _():
                    copy(a, j, me).start()

    def finish(ins, outs, sems):
        me, copy, own = plan(ins, outs, sems)
        for a in range(na):
            for j in range(N_DEV):
                @pl.when(me == j)
                def _():
                    for i in range(N_DEV):
                        if i != j:
                            copy(a, j, i).wait_recv()
                    own(a, j).wait()

                @pl.when(me != j)
                def _():
                    copy(a, j, me).wait_send()

    return _Carry(carry_ins, [_sds((N_DEV,) + src.shape[-2:], src.dtype) for src, _, _ in items],
                  [pltpu.SemaphoreType.DMA((N_DEV, na)), pltpu.SemaphoreType.DMA((N_DEV, na)),
                   pltpu.SemaphoreType.DMA((na,))], start, finish, None, aliases)


HBM_ARRAY = pl.BlockSpec(memory_space=pltpu.HBM)
SEMAPHORES = pl.BlockSpec(memory_space=pltpu.SEMAPHORE)
DATAFLOW = pltpu.SideEffectType.DATAFLOW_SIDE_EFFECTING


def _exchange_copy(parts_ref, land_ref, send_sems, recv_sems, me, j):
    return pltpu.make_async_remote_copy(src_ref=parts_ref.at[j], dst_ref=land_ref.at[me], send_sem=send_sems.at[j],
                                        recv_sem=recv_sems.at[me], device_id=(j >> 2, (j >> 1) & 1, j & 1),
                                        device_id_type=MESH)


def _exchange_start(parts, name):
    def body(parts_ref, land_ref, send_sems, recv_sems, parts_thru, land_thru, token):
        me = _slot(*_position())
        for j in range(N_DEV):
            @pl.when(me == j)
            def _():
                pltpu.make_async_copy(parts_ref.at[j], land_ref.at[j], send_sems.at[j]).start()

            @pl.when(me != j)
            def _():
                _exchange_copy(parts_ref, land_ref, send_sems, recv_sems, me, j).start()
        token[...] = jnp.zeros_like(token)

    return pl.pallas_call(
        body, name=name,
        out_shape=(pltpu.SemaphoreType.DMA((N_DEV,)), pltpu.SemaphoreType.DMA((N_DEV,)),
                   pltpu.HBM(parts.shape, parts.dtype), pltpu.HBM(parts.shape, parts.dtype), _sds((8, 128), F32)),
        in_specs=(HBM_ARRAY, HBM_ARRAY),
        out_specs=(SEMAPHORES, SEMAPHORES, HBM_ARRAY, HBM_ARRAY, pl.BlockSpec(memory_space=pltpu.VMEM)),
        input_output_aliases={0: 2, 1: 3}, compiler_params=pltpu.CompilerParams(has_side_effects=DATAFLOW),
    )(pltpu.with_memory_space_constraint(parts, pltpu.HBM),
      pltpu.with_memory_space_constraint(lax.empty(parts.shape, parts.dtype), pltpu.HBM))


def _exchange_wait(send_sems, recv_sems, parts_thru, land_thru, after, name):
    def body(parts_ref, land_ref, send_sems, recv_sems, *rest):
        me = _slot(*_position())
        for j in range(N_DEV):
            @pl.when(me == j)
            def _():
                pltpu.make_async_copy(parts_ref.at[j], land_ref.at[j], send_sems.at[j]).wait()

            @pl.when(me != j)
            def _():
                both = pltpu.make_async_remote_copy(src_ref=parts_ref.at[j], dst_ref=land_ref.at[j],
                                                    send_sem=send_sems.at[j], recv_sem=recv_sems.at[j],
                                                    device_id=(j >> 2, (j >> 1) & 1, j & 1), device_id_type=MESH)
                both.wait_send()
                both.wait_recv()

    return pl.pallas_call(
        body, name=name, out_shape=(pltpu.HBM(parts_thru.shape, parts_thru.dtype),
                                    pltpu.HBM(parts_thru.shape, parts_thru.dtype)),
        in_specs=(HBM_ARRAY, HBM_ARRAY, SEMAPHORES, SEMAPHORES) + (pl.BlockSpec(memory_space=pl.ANY),) * len(after),
        out_specs=(HBM_ARRAY, HBM_ARRAY), input_output_aliases={0: 0, 1: 1},
        compiler_params=pltpu.CompilerParams(has_side_effects=DATAFLOW),
    )(parts_thru, land_thru, send_sems, recv_sems, *after)[1]


class _Mesh:
    def __init__(self, shards):
        self.shards, self.full, self.received, self.cache, self.pending, self.tokens = shards, {}, {}, {}, {}, {}

    def fetch(self, wanted):
        items = []
        for want in wanted:
            name, r0, r1 = want if isinstance(want, tuple) else (want, 0, self.shards[want].shape[0])
            items.append((self.shards[name], (r0, r1), self.full.get(name)))
        return _gather_carry(items)

    def fetched(self, wanted, results):
        self.full.update(zip([want[0] if isinstance(want, tuple) else want for want in wanted], results))

    def send(self, *payloads):
        return _exchange_carry([(parts, rows or (0, parts.shape[1]), self.received.get(name))
                                for name, parts, rows in payloads])

    def sent(self, names, results):
        self.received.update(zip(names, results))

    def send_apart(self, name, parts):
        *self.pending[name], self.tokens[name] = _exchange_start(parts, "exchange_" + name + "_start")
        return self.tokens[name]

    def sent_apart(self, name, after):
        self.received[name] = _exchange_wait(*self.pending.pop(name), after, "exchange_" + name + "_wait")

    def w(self, key):
        if key not in self.cache:
            self.cache[key] = self._layout(key)
        return self.cache[key]

    def _layout(self, key):
        if key in ("gu1", "gu2"):
            return self.full[key]
        if key in ("d1", "d2"):
            return self.full[key].reshape(4, FS, D)
        if key in ("out", "q", "o"):
            return self.full[key].reshape(D, D)
        if key == "kv":
            return self.full["kv"]
        if key == "convw":
            rows = self.full["conv"][:, :3, :].transpose(1, 0, 2).reshape(3, D)
            return jnp.concatenate([rows, jnp.zeros((5, D), F32)], axis=0)
        assert key == "win_t", key
        return self.full["win"].reshape(-1, D)


def _forward_backward(x, mem, target, g, rel_bias, sinks, ex):
    s = x.shape[0]
    def fetching(wanted, call, *args, **kw):
        res, got = call(*args, carry=ex.fetch(wanted), **kw)
        ex.fetched(wanted, got)
        return res

    h1 = fetching(["gu1", "conv"], _rmsnorm, x, g["ffn1"], "norm_ffn1")
    gu1, a1 = fetching(["d1", ("win", 0, 400)], _ffn_up, h1, ex.w("gu1").reshape(2, 4, FS, D), "ffn1_up")
    x1, h2 = fetching([("win", 400, 832)], _mm_res_norm, a1, ex.w("d1"), x, g["mix"], 0.5, "ffn1_down")
    pa, q, kv = fetching(["gu2"], _in_proj, h2, ex.w("win_t"), "in_proj")
    biasm = _bias_build(rel_bias, "bias_build")
    attn, lse = fetching(["out", "kv", "o"], _swa_fwd, q, kv, biasm, sinks, "swa_fwd")
    merged = fetching(["q"], _conv_merge_fwd, pa, attn, ex.w("convw"), "conv_merge_fwd")
    (x2, h3), _ = _mm_res_norm(merged[None], ex.w("out")[None], x1, g["xattn"], 1.0, "out_proj")
    mh, kv2 = _norm_mm(mem, g["mem"], ex.w("kv"), "xattn_kv")
    q2, o, lse2 = _xattn_fwd(h3, ex.w("q"), kv2, "xattn_fwd")
    (x3, h4), _ = _mm_res_norm(o[None], ex.w("o")[None], x2, g["ffn2"], 1.0, "xattn_o")
    gu2, a2 = fetching(["d2"], _ffn_up, h4, ex.w("gu2").reshape(2, 4, FS, D), "ffn2_up")
    dx4, dx4b, loss, d_final = _ffn_down_loss(a2, ex.w("d2"), x3, g["final"], target, "ffn2_down_loss")
    def sending(payloads, call, *args, **kw):
        res, got = call(*args, carry=ex.send(*payloads), **kw)
        ex.sent([name for name, _, _ in payloads], got)
        return res

    dw_d2 = _mm_tn(a2, dx4b[None], "dw_ffn2_down", scale=0.5)[0].reshape(N_DEV, -1, D)
    dgu2 = sending([("d2", dw_d2, (0, 288))], _ffn_down_bwd, dx4b, ex.w("d2"), gu2, "ffn2_down_bwd").reshape(8, s, FS)
    dw_gu2 = sending([("d2", dw_d2, (288, 352))], _mm_tn, dgu2, h4[None], "dw_ffn2_up", scale=0.5)
    dx3, dx3b, d_ffn2 = sending([("gu2", dw_gu2, (0, 368))], _mm_acc_rms_bwd, dgu2, ex.w("gu2"), "ffn2_up_bwd",
                                x=x3, gain=g["ffn2"], dres=dx4, scale=0.5)
    dw_o = _mm_tn(o[None], dx3b[None], "dw_xattn_o")[0].reshape(N_DEV, -1, D)
    (dq2, dkv2), _ = _xattn_bwd(q2, kv2, o, dx3b, ex.w("o"), lse2, "xattn_bwd")
    dw_q = _mm_tn(h3[None], dq2[None], "dw_xattn_q")[0].reshape(N_DEV, -1, D)
    (dx2, dx2b, d_xattn), _ = _mm_acc_rms_bwd(dq2[None], ex.w("q")[None], "xattn_q_bwd", x=x2, gain=g["xattn"],
                                              dres=dx3, bt=True)
    dw_kv, d_mem = _norm_mm_bwd(dkv2, mh, ex.w("kv"), mem, "xattn_kv_bwd")
    dmerged, _ = _mm_acc(dx2b[None], ex.w("out")[None], "out_proj_bwd", BF16, bt=True)
    dw_out = _mm_tn(merged[None], dx2b[None], "dw_out_proj")[0].reshape(N_DEV, -1, D)
    dattn, dpa, d_convw = sending([("kv", dw_kv, None)], _conv_merge_bwd,
                                  dmerged, pa, attn, ex.w("convw"), "conv_merge_bwd")
    dq, dkv, dbias, d_sinks = sending([("gu2", dw_gu2, (368, FS)), ("out", dw_out, None)], _swa_bwd,
                                      q, kv, attn, dattn, lse, biasm, sinks, "swa_bwd")
    d_relb = _bias_bwd(dbias, "bias_bwd")
    w_rows = ex.w("win_t").shape[0]
    dw_in = sending([("o", dw_o, None), ("q", dw_q, None)], _mm_tn_rows, dpa, h2, "dw_in_proj_a", w_rows, NQ + NKV)
    dw_in = _mm_tn_rows(dq[None], h2, "dw_in_proj_q", w_rows, 0, begun=dw_in)[0]
    dw_in = _mm_tn_rows(dkv[None], h2, "dw_in_proj_kv", w_rows, NQ, begun=dw_in)[0].reshape(N_DEV, -1, D)
    (dx1, dx1b, d_mix), _ = _in_proj_bwd(dpa, dq, dkv, ex.w("win_t"), "in_proj_bwd", x=x1, gain=g["mix"], dres=dx2,
                                         behind=ex.send_apart("win", dw_in))
    dw_d1 = _mm_tn(a1, dx1b[None], "dw_ffn1_down", scale=0.5)[0].reshape(N_DEV, -1, D)
    dgu1 = _ffn_down_bwd(dx1b, ex.w("d1"), gu1, "ffn1_down_bwd", behind=ex.send_apart("d1", dw_d1))[0]
    dgu1 = dgu1.reshape(8, s, FS)
    dw_gu1 = _mm_tn(dgu1, h1[None], "dw_ffn1_up", scale=0.5)[0]
    (dx0, _, d_ffn1), _ = _mm_acc_rms_bwd(dgu1, ex.w("gu1"), "ffn1_up_bwd", x=x, gain=g["ffn1"], dres=dx1,
                                          scale=0.5, behind=ex.send_apart("gu1", dw_gu1))

    relb_row = jnp.concatenate([d_relb[:, :REL_BUCKETS].T.reshape(1, REL_BUCKETS * N_HEADS), d_sinks[:, :N_HEADS],
                                jnp.zeros((1, D - REL_BUCKETS * N_HEADS - N_HEADS), F32)], axis=1)
    loss_row = jnp.concatenate([loss[0:1, 0:1], jnp.zeros((1, D - 1), F32)], axis=1)
    small = jnp.concatenate([d_ffn1, d_mix, d_xattn, d_mem, d_ffn2, d_final, relb_row, loss_row, d_convw[0:3],
                             jnp.zeros((SMALL_ROWS - ROW_CONV - 3, D), F32)], axis=0)
    return dx0, small


def _pack_small(norms, final, relb, sinks, conv_local, me):
    relb_row = jnp.concatenate([relb.reshape(1, -1), sinks.reshape(1, -1),
                                jnp.zeros((1, D - REL_BUCKETS * N_HEADS - N_HEADS), F32)], axis=1)
    conv_rows = lax.dynamic_update_slice(jnp.zeros((3, D), F32), conv_local.reshape(3, -1), (0, 128 * me))
    return jnp.concatenate(list(norms) + [final.reshape(1, D), relb_row, jnp.zeros((1, D), F32), conv_rows,
                                          jnp.zeros((SMALL_ROWS - ROW_CONV - 3, D), F32)], axis=0)


def kernel(x, mem, positions, rel_bias, ffn1_norm, ffn1_w_gu, ffn1_w_down, mix_norm, w_in, sinks, conv_w, w_out, xattn_norm, mem_norm, xattn_wq, xattn_wkv, xattn_wo, ffn2_norm, ffn2_w_gu, ffn2_w_down, final_norm, loss_target, m_rel_bias, m_ffn1_norm, m_ffn1_w_gu, m_ffn1_w_down, m_mix_norm, m_w_in, m_sinks, m_conv_w, m_w_out, m_xattn_norm, m_mem_norm, m_xattn_wq, m_xattn_wkv, m_xattn_wo, m_ffn2_norm, m_ffn2_w_gu, m_ffn2_w_down, m_final_norm, v_rel_bias, v_ffn1_norm, v_ffn1_w_gu, v_ffn1_w_down, v_mix_norm, v_w_in, v_sinks, v_conv_w, v_w_out, v_xattn_norm, v_mem_norm, v_xattn_wq, v_xattn_wkv, v_xattn_wo, v_ffn2_norm, v_ffn2_w_gu, v_ffn2_w_down, v_final_norm):
    del positions
    me = _slot(*_position())
    big = dict(gu1=(ffn1_w_gu, m_ffn1_w_gu, v_ffn1_w_gu), d1=(ffn1_w_down, m_ffn1_w_down, v_ffn1_w_down),
               win=(w_in, m_w_in, v_w_in), out=(w_out, m_w_out, v_w_out), q=(xattn_wq, m_xattn_wq, v_xattn_wq),
               kv=(xattn_wkv, m_xattn_wkv, v_xattn_wkv), o=(xattn_wo, m_xattn_wo, v_xattn_wo),
               gu2=(ffn2_w_gu, m_ffn2_w_gu, v_ffn2_w_gu), d2=(ffn2_w_down, m_ffn2_w_down, v_ffn2_w_down))
    order = list(big)
    transposed = ("gu1", "gu2", "win")
    local = {k: tuple(t[0].T if k in transposed else t[0] for t in big[k]) for k in order}
    shards = {k: local[k][0].astype(BF16) for k in order}
    shards["conv"] = jnp.concatenate([conv_w[0], jnp.zeros((5, 128), F32)], axis=0)
    ex = _Mesh(shards)
    gains = dict(ffn1=ffn1_norm, mix=mix_norm, xattn=xattn_norm, mem=mem_norm, ffn2=ffn2_norm,
                 final=final_norm.reshape(1, D))
    dx, small = _forward_backward(x[0], mem[0], loss_target[0], gains, rel_bias, sinks, ex)
    apart = ("d1", "win", "gu1")
    big_out = {k: _adamw(ex.received[k], *local[k], "adamw_" + k, behind=ex.tokens["gu1"])
               for k in order if k not in apart}
    for k in apart[:-1]:
        ex.sent_apart(k, after=[big_out[j][1] for j in big_out])
        big_out[k] = _adamw(ex.received[k], *local[k], "adamw_" + k)
    small_parts = _run_alone(_exchange_carry([], [small]), "exchange_small", after=[big_out[k][1] for k in big_out])[0]
    packed = [_pack_small(norms, final, relb, sk, conv, me) for norms, final, relb, sk, conv in (
        ((ffn1_norm, mix_norm, xattn_norm, mem_norm, ffn2_norm), final_norm, rel_bias, sinks, conv_w),
        ((m_ffn1_norm, m_mix_norm, m_xattn_norm, m_mem_norm, m_ffn2_norm), m_final_norm, m_rel_bias, m_sinks, m_conv_w),
        ((v_ffn1_norm, v_mix_norm, v_xattn_norm, v_mem_norm, v_ffn2_norm), v_final_norm, v_rel_bias, v_sinks, v_conv_w))]
    small_out = _adamw(small_parts, *packed, "adamw_small")
    ex.sent_apart("gu1", after=[dx, small_out[1]] + [big_out[k][1] for k in big_out])
    big_out["gu1"] = _adamw(ex.received["gu1"], *local["gu1"], "adamw_gu1")
    big_out = {k: [t.T if k in transposed else t for t in big_out[k]] for k in order}

    def unpack(t):
        conv = lax.dynamic_slice(t[ROW_CONV:ROW_CONV + 3], (0, 128 * me), (3, 128))[None]
        nrel = REL_BUCKETS * N_HEADS
        return dict(ffn1_norm=t[0:1], mix_norm=t[1:2], xattn_norm=t[2:3], mem_norm=t[3:4], ffn2_norm=t[4:5],
                    final_norm=t[5], rel_bias=t[ROW_RELB, :nrel].reshape(REL_BUCKETS, N_HEADS),
                    sinks=t[ROW_RELB:ROW_RELB + 1, nrel:nrel + N_HEADS], conv_w=conv)

    names = dict(gu1="ffn1_w_gu", d1="ffn1_w_down", win="w_in", out="w_out", q="xattn_wq", kv="xattn_wkv",
                 o="xattn_wo", gu2="ffn2_w_gu", d2="ffn2_w_down")
    results = []
    for idx in range(4):
        leaves = unpack(small_out[idx])
        leaves.update({names[k]: big_out[k][idx][None] for k in order})
        results.append(leaves)
    weights = ("rel_bias", "ffn1_norm", "ffn1_w_gu", "ffn1_w_down", "mix_norm", "w_in", "sinks", "conv_w", "w_out",
               "xattn_norm", "mem_norm", "xattn_wq", "xattn_wkv", "xattn_wo", "ffn2_norm", "ffn2_w_gu", "ffn2_w_down",
               "final_norm")
    loss = small_out[0][ROW_LOSS, 0]
    return (loss, dx[None], *[leaves[n] for leaves in results for n in weights])
```

```python
import math

import numpy as np
import jax
import jax.numpy as jnp
from jax import lax
from jax.experimental import pallas as pl
from jax.experimental.pallas import tpu as pltpu

F32, BF16 = jnp.float32, jnp.bfloat16
MESH = pl.DeviceIdType.MESH

D = 1024
N_DEV = 8
D_FF = 2816
FS = D_FF // 4
HEAD = 64
N_HEADS, N_KV = 16, 4
BLK = 128
NQ, NKV = N_HEADS * HEAD, 2 * N_KV * HEAD
XH, XHD = 4, 256
REL_BUCKETS, REL_EXACT, REL_MAX_DIST = 32, 16, 128
EPS, NEG = 1e-6, -1e30
ADAM_LR, ADAM_B1, ADAM_B2, ADAM_EPS, ADAM_WD, ADAM_STEP = 0.001, 0.9, 0.999, 1e-08, 0.01, 10
VMEM_LIMIT_V7X = 56 * 2**20
SMALL_ROWS = 16
ROW_RELB, ROW_LOSS, ROW_CONV = 6, 7, 8


def _bucket_thresholds():
    n = np.arange(REL_MAX_DIST)
    nf = np.maximum(n, 1).astype(np.float32)
    large = REL_EXACT + (np.log(nf / np.float32(REL_EXACT)) / np.float32(math.log(REL_MAX_DIST / REL_EXACT))
                         * np.float32(REL_BUCKETS - REL_EXACT)).astype(np.int32)
    b = np.where(n < REL_EXACT, n, np.minimum(large, REL_BUCKETS - 1))
    return [int(np.argmax(b >= REL_EXACT + k)) for k in range(1, REL_BUCKETS - REL_EXACT)]


BUCKET_THRESHOLDS = _bucket_thresholds()


HBM_SPEC = pl.BlockSpec(memory_space=pl.ANY)


class _Carry:
    def __init__(self, ins, outs, sems, start, finish, mid=None, aliases=None):
        self.ins, self.outs, self.sems = list(ins), list(outs), list(sems)
        self.start, self.finish, self.mid, self.aliases = start, finish, mid, dict(aliases or {})


def _pcall(body, *, name, grid, in_specs, out_specs, out_shape, scratch=(), carry=None, aliases=None, behind=None):
    params = pltpu.CompilerParams(dimension_semantics=("arbitrary",) * len(grid), vmem_limit_bytes=VMEM_LIMIT_V7X)
    if carry is None and behind is not None:
        n_in = len(in_specs)
        call = pl.pallas_call(lambda *refs: body(*refs[:n_in], *refs[n_in + 1:]), name=name, grid=grid,
                              in_specs=list(in_specs) + [pl.BlockSpec((8, 128), lambda *_: (0, 0))],
                              out_specs=out_specs, out_shape=out_shape, scratch_shapes=list(scratch),
                              compiler_params=params, input_output_aliases=aliases or {})
        return lambda *args: call(*args, behind)
    if carry is None:
        return pl.pallas_call(body, name=name, grid=grid, in_specs=in_specs, out_specs=out_specs,
                              out_shape=out_shape, scratch_shapes=list(scratch), compiler_params=params,
                              input_output_aliases=aliases or {})
    assert aliases is None and behind is None, name
    single = not isinstance(out_shape, (list, tuple))
    own_specs, own_shapes = ([out_specs], [out_shape]) if single else (list(out_specs), list(out_shape))
    n_in, n_out, n_scr = len(in_specs), len(own_shapes), len(scratch)
    n_cin, n_cout = len(carry.ins), len(carry.outs)
    steps = math.prod(grid)
    mid_step = max(steps - 1 - max(steps // 8, 1), 0)

    def carrying(*refs):
        ins, refs = refs[:n_in], refs[n_in:]
        cins, refs = refs[:n_cin], refs[n_cin:]
        outs, refs = refs[:n_out], refs[n_out:]
        couts, refs = refs[:n_cout], refs[n_cout:]
        scr, csems = refs[:n_scr], refs[n_scr:]
        step = 0
        for axis, size in enumerate(grid):
            step = step * size + pl.program_id(axis)

        @pl.when(step == 0)
        def _():
            carry.start(cins, couts, csems)

        body(*ins, *outs, *scr)
        if carry.mid is not None:
            @pl.when(step == mid_step)
            def _():
                carry.mid(cins, couts, csems)

        @pl.when(step == steps - 1)
        def _():
            carry.finish(cins, couts, csems)

    call = pl.pallas_call(carrying, name=name, grid=grid, in_specs=list(in_specs) + [HBM_SPEC] * n_cin,
                          out_specs=own_specs + [HBM_SPEC] * n_cout, out_shape=own_shapes + carry.outs,
                          scratch_shapes=list(scratch) + carry.sems, compiler_params=params,
                          input_output_aliases={n_in + i: n_out + o for i, o in carry.aliases.items()})

    def run(*args):
        res = call(*args, *carry.ins)
        return (res[0] if single else res[:n_out]), res[n_out:]

    return run


def _run_alone(carry, name, after=()):
    n_cin, n_cout, n_after = len(carry.ins), len(carry.outs), len(after)

    def body(*refs):
        cins, refs = refs[:n_cin], refs[n_cin + n_after:]
        couts, csems = refs[:n_cout], refs[n_cout:]
        carry.start(cins, couts, csems)
        if carry.mid is not None:
            carry.mid(cins, couts, csems)
        carry.finish(cins, couts, csems)

    return pl.pallas_call(body, name=name, in_specs=[HBM_SPEC] * (n_cin + n_after), out_specs=[HBM_SPEC] * n_cout,
                          out_shape=carry.outs, scratch_shapes=carry.sems,
                          input_output_aliases=carry.aliases)(*carry.ins, *after)


def _dot(a, b):
    return jnp.dot(a, b, preferred_element_type=F32)


def _dot_nt(a, b):
    return lax.dot_general(a, b, (((1,), (1,)), ((), ())), preferred_element_type=F32)


def _dot_tn(a, b):
    return lax.dot_general(a, b, (((0,), (0,)), ((), ())), preferred_element_type=F32)


def _sds(shape, dtype):
    return jax.ShapeDtypeStruct(tuple(shape), dtype)


ROW_CHUNK = 256


def _row_chunks(tm):
    return [slice(r, min(r + ROW_CHUNK, tm)) for r in range(0, tm, ROW_CHUNK)]


def _carried(call, args, carry):
    return call(*args) if carry is not None else (call(*args), ())


def _rmsnorm(x, g, name, carry=None):
    m, d = x.shape
    tm = min(512, m)

    def body(x_ref, g_ref, h_ref):
        xv = x_ref[...]
        r = lax.rsqrt(jnp.mean(xv * xv, axis=-1, keepdims=True) + EPS)
        h_ref[...] = (xv * r * g_ref[...]).astype(BF16)

    call = _pcall(body, name=name, grid=(m // tm,), carry=carry,
                  in_specs=[pl.BlockSpec((tm, d), lambda i: (i, 0)), pl.BlockSpec((1, d), lambda i: (0, 0))],
                  out_specs=pl.BlockSpec((tm, d), lambda i: (i, 0)), out_shape=_sds((m, d), BF16))
    return _carried(call, (x, g), carry)


def _norm_mm(x, g, w, name):
    m, d = x.shape
    nj, _, n = w.shape

    def body(x_ref, g_ref, w_ref, h_ref, o_ref):
        xv = x_ref[...]
        r = lax.rsqrt(jnp.mean(xv * xv, axis=-1, keepdims=True) + EPS)
        h = (xv * r * g_ref[...]).astype(BF16)
        h_ref[...] = h
        for j in range(nj):
            o_ref[j] = _dot(h, w_ref[j]).astype(BF16)

    return _pcall(body, name=name, grid=(1,),
                  in_specs=[pl.BlockSpec((m, d), lambda i: (0, 0)), pl.BlockSpec((1, d), lambda i: (0, 0)),
                            pl.BlockSpec(w.shape, lambda i: (0, 0, 0))],
                  out_specs=[pl.BlockSpec((m, d), lambda i: (0, 0)), pl.BlockSpec((nj, m, n), lambda i: (0, 0, 0))],
                  out_shape=[_sds((m, d), BF16), _sds((nj, m, n), BF16)])(x, g, w)


def _norm_mm_bwd(dy, h, w, x, name):
    nj, m, n = dy.shape
    d = x.shape[1]

    def body(dy_ref, h_ref, w_ref, x_ref, dw_ref, dg_ref):
        dh = None
        for j in range(nj):
            dyb = dy_ref[j].astype(BF16)
            dw_ref[j] = _dot_tn(h_ref[...], dyb).astype(BF16)
            part = _dot_nt(dyb, w_ref[j])
            dh = part if dh is None else dh + part
        xv = x_ref[...]
        xh = xv * lax.rsqrt(jnp.mean(xv * xv, axis=-1, keepdims=True) + EPS)
        dg_ref[...] = jnp.sum(dh * xh, axis=0, keepdims=True)

    return _pcall(body, name=name, grid=(1,),
                  in_specs=[pl.BlockSpec((nj, m, n), lambda i: (0, 0, 0)), pl.BlockSpec((m, d), lambda i: (0, 0)),
                            pl.BlockSpec((nj, d, n), lambda i: (0, 0, 0)), pl.BlockSpec((m, d), lambda i: (0, 0))],
                  out_specs=[pl.BlockSpec((nj, d, n), lambda i: (0, 0, 0)), pl.BlockSpec((1, d), lambda i: (0, 0))],
                  out_shape=[_sds((nj, d, n), BF16), _sds((1, d), F32)])(dy, h, w, x)


def _load_once(src_hbm, dst_vmem, sem):
    @pl.when(pl.program_id(0) == 0)
    def _():
        load = pltpu.make_async_copy(src_hbm, dst_vmem, sem)
        load.start()
        load.wait()


def _resident(w):
    return [pltpu.VMEM(w.shape, w.dtype), pltpu.SemaphoreType.DMA(())]


def _ffn_up(h, w4, name, tm=512, carry=None):
    s, d = h.shape
    tm = min(tm, s)

    def body(h_ref, w_hbm, gu_ref, a_ref, w_ref, w_sem):
        _load_once(w_hbm, w_ref, w_sem)
        for p in range(4):
            for rows in _row_chunks(tm):
                hv = h_ref[rows, :]
                g = _dot_nt(hv, w_ref[0, p])
                u = _dot_nt(hv, w_ref[1, p])
                gu_ref[0, p, rows, :] = g.astype(BF16)
                gu_ref[1, p, rows, :] = u.astype(BF16)
                a_ref[p, rows, :] = (g * jax.nn.sigmoid(g) * u).astype(BF16)

    call = _pcall(body, name=name, grid=(s // tm,),
                  in_specs=[pl.BlockSpec((tm, d), lambda i: (i, 0)), HBM_SPEC],
                  out_specs=[pl.BlockSpec((2, 4, tm, FS), lambda i: (0, 0, i, 0)),
                             pl.BlockSpec((4, tm, FS), lambda i: (0, i, 0))],
                  out_shape=[_sds((2, 4, s, FS), BF16), _sds((4, s, FS), BF16)], scratch=_resident(w4), carry=carry)
    return _carried(call, (h, w4), carry)


N_SEG = 5
IN_PROJ_WEIGHTS = [pltpu.VMEM((NQ, D), BF16), pltpu.VMEM((NKV, D), BF16), pltpu.VMEM((N_SEG, D, D), BF16),
                   pltpu.SemaphoreType.DMA((2 + N_SEG,))]


def _load_in_proj(w_hbm, wq_ref, wkv_ref, wa_ref, sems):
    @pl.when(pl.program_id(0) == 0)
    def _():
        loads = [pltpu.make_async_copy(w_hbm.at[pl.ds(0, NQ)], wq_ref, sems.at[0]),
                 pltpu.make_async_copy(w_hbm.at[pl.ds(NQ, NKV)], wkv_ref, sems.at[1])]
        loads += [pltpu.make_async_copy(w_hbm.at[pl.ds(NQ + NKV + D * j, D)], wa_ref.at[j], sems.at[2 + j])
                  for j in range(N_SEG)]
        for load in loads:
            load.start()
        for load in loads:
            load.wait()


def _in_proj(h, w_in_t, name, tm=512, carry=None):
    s, d = h.shape
    tm = min(tm, s)

    def body(h_ref, w_hbm, pa_ref, q_ref, kv_ref, wq_ref, wkv_ref, wa_ref, sems):
        _load_in_proj(w_hbm, wq_ref, wkv_ref, wa_ref, sems)
        hv = h_ref[...]
        q_ref[...] = _dot_nt(hv, wq_ref[...]).astype(BF16)
        kv_ref[...] = _dot_nt(hv, wkv_ref[...]).astype(BF16)
        for j in range(N_SEG):
            pa_ref[j] = _dot_nt(hv, wa_ref[j]).astype(BF16)

    call = _pcall(body, name=name, grid=(s // tm,), carry=carry,
                  in_specs=[pl.BlockSpec((tm, d), lambda i: (i, 0)), HBM_SPEC],
                  out_specs=[pl.BlockSpec((N_SEG, tm, d), lambda i: (0, i, 0)),
                             pl.BlockSpec((tm, NQ), lambda i: (i, 0)), pl.BlockSpec((tm, NKV), lambda i: (i, 0))],
                  out_shape=[_sds((N_SEG, s, d), BF16), _sds((s, NQ), BF16), _sds((s, NKV), BF16)],
                  scratch=IN_PROJ_WEIGHTS)
    return _carried(call, (h, w_in_t), carry)


def _mm_res_norm(a, w, xres, gain, scale, name, tm=512, carry=None):
    npart, s, kp = a.shape
    tm = min(tm, s)

    def body(a_ref, w_ref, x_ref, g_ref, xo_ref, h_ref):
        for rows in _row_chunks(tm):
            acc = _dot(a_ref[0, rows, :], w_ref[0])
            for p in range(1, npart):
                acc = acc + _dot(a_ref[p, rows, :], w_ref[p])
            xn = x_ref[rows, :] + scale * acc
            xo_ref[rows, :] = xn
            r = lax.rsqrt(jnp.mean(xn * xn, axis=-1, keepdims=True) + EPS)
            h_ref[rows, :] = (xn * r * g_ref[...]).astype(BF16)

    call = _pcall(body, name=name, grid=(s // tm,),
                  in_specs=[pl.BlockSpec((npart, tm, kp), lambda i: (0, i, 0)),
                            pl.BlockSpec((npart, kp, D), lambda i: (0, 0, 0)),
                            pl.BlockSpec((tm, D), lambda i: (i, 0)),
                            pl.BlockSpec((1, D), lambda i: (0, 0))],
                  out_specs=[pl.BlockSpec((tm, D), lambda i: (i, 0)), pl.BlockSpec((tm, D), lambda i: (i, 0))],
                  out_shape=[_sds((s, D), F32), _sds((s, D), BF16)], carry=carry)
    return _carried(call, (a, w, xres, gain), carry)


def _ffn_down_loss(a, w, xres, gain, target, name, tm=512):
    npart, s, kp = a.shape
    tm = min(tm, s)

    def body(a_ref, w_ref, x_ref, g_ref, t_ref, dx_ref, dxb_ref, loss_ref, dg_ref):
        @pl.when(pl.program_id(0) == 0)
        def _():
            loss_ref[...] = jnp.zeros_like(loss_ref)
            dg_ref[...] = jnp.zeros_like(dg_ref)

        for rows in _row_chunks(tm):
            acc = _dot(a_ref[0, rows, :], w_ref[0])
            for p in range(1, npart):
                acc = acc + _dot(a_ref[p, rows, :], w_ref[p])
            xn = x_ref[rows, :] + 0.5 * acc
            r = lax.rsqrt(jnp.mean(xn * xn, axis=-1, keepdims=True) + EPS)
            xh = xn * r
            gv = g_ref[...]
            err = xh * gv - t_ref[rows, :]
            part = 0.5 * jnp.sum(jnp.mean(err * err, axis=-1, keepdims=True), axis=0, keepdims=True)
            dy = err * (1.0 / D)
            dyg = dy * gv
            dxn = r * (dyg - xh * jnp.mean(dyg * xh, axis=-1, keepdims=True))
            dx_ref[rows, :] = dxn
            dxb_ref[rows, :] = dxn.astype(BF16)
            loss_ref[...] += jnp.broadcast_to(part, loss_ref.shape)
            dg_ref[...] += jnp.sum(dy * xh, axis=0, keepdims=True)

    return _pcall(body, name=name, grid=(s // tm,),
                  in_specs=[pl.BlockSpec((npart, tm, kp), lambda i: (0, i, 0)),
                            pl.BlockSpec((npart, kp, D), lambda i: (0, 0, 0)),
                            pl.BlockSpec((tm, D), lambda i: (i, 0)),
                            pl.BlockSpec((1, D), lambda i: (0, 0)),
                            pl.BlockSpec((tm, D), lambda i: (i, 0))],
                  out_specs=[pl.BlockSpec((tm, D), lambda i: (i, 0)), pl.BlockSpec((tm, D), lambda i: (i, 0)),
                             pl.BlockSpec((8, 128), lambda i: (0, 0)), pl.BlockSpec((1, D), lambda i: (0, 0))],
                  out_shape=[_sds((s, D), F32), _sds((s, D), BF16), _sds((8, 128), F32), _sds((1, D), F32)],
                  )(a, w, xres, gain, target)


def _window_tiles():
    i = lax.broadcasted_iota(jnp.int32, (BLK, BLK), 0)
    j = lax.broadcasted_iota(jnp.int32, (BLK, BLK), 1)
    rel = (i - j) & (BLK - 1)
    large = jnp.full_like(rel, REL_EXACT)
    for t in BUCKET_THRESHOLDS:
        large = large + (rel >= t).astype(jnp.int32)
    return j <= i, jnp.where(rel < REL_EXACT, rel, large)


def _bias_build(rel_bias, name):
    def body(rb_ref, o_ref):
        _, bucket = _window_tiles()

        def per_head(h, carry):
            acc = jnp.zeros((BLK, BLK), F32)
            for b in range(REL_BUCKETS):
                acc = jnp.where(bucket == b, rb_ref[b, h], acc)
            o_ref[h] = acc
            return carry

        lax.fori_loop(0, N_HEADS, per_head, 0)

    return _pcall(body, name=name, grid=(1,),
                  in_specs=[pl.BlockSpec(memory_space=pltpu.SMEM)],
                  out_specs=pl.BlockSpec((N_HEADS, BLK, BLK), lambda i: (0, 0, 0)),
                  out_shape=_sds((N_HEADS, BLK, BLK), F32))(rel_bias)


def _bias_bwd(dbias, name):
    def body(db_ref, o_ref):
        _, bucket = _window_tiles()
        lane = lax.broadcasted_iota(jnp.int32, (N_HEADS, 128), 1)

        def per_bucket(b, out):
            mb = (bucket == b).astype(F32)
            per_col = jnp.sum(db_ref[...] * mb[None, :, :], axis=1)
            return jnp.where(lane == b, jnp.sum(per_col, axis=1, keepdims=True), out)

        o_ref[...] = lax.fori_loop(0, REL_BUCKETS, per_bucket, jnp.zeros((N_HEADS, 128), F32))

    return _pcall(body, name=name, grid=(1,),
                  in_specs=[pl.BlockSpec((N_HEADS, BLK, BLK), lambda i: (0, 0, 0))],
                  out_specs=pl.BlockSpec((N_HEADS, 128), lambda i: (0, 0)),
                  out_shape=_sds((N_HEADS, 128), F32))(dbias)


PAIR = 2 * HEAD
GROUP = N_HEADS // N_KV
SWA_SCALE = HEAD ** -0.5


def _window_masks(n):
    i = lax.broadcasted_iota(jnp.int32, (GROUP * BLK, BLK), 0) & (BLK - 1)
    j = lax.broadcasted_iota(jnp.int32, (GROUP * BLK, BLK), 1)
    return j <= i, jnp.logical_and(n == 0, j > i), j < HEAD


def _kv_twice(ref, base, g, low):
    slab = ref[:, base + PAIR * (g // 2): base + PAIR * (g // 2 + 1)]
    swapped = pltpu.roll(slab, HEAD, 1)
    return jnp.where(low, slab, swapped) if g % 2 == 0 else jnp.where(low, swapped, slab)


def _stack_heads(ref, g, low):
    parts = []
    for r in range(2):
        slab = ref[:, PAIR * (2 * g + r): PAIR * (2 * g + r + 1)]
        zero = jnp.zeros_like(slab)
        parts += [jnp.where(low, slab, zero), jnp.where(low, zero, slab)]
    return jnp.concatenate(parts, axis=0)


def _unstack_heads(t, low):
    return [jnp.where(low, t[2 * r * BLK:(2 * r + 1) * BLK], t[(2 * r + 1) * BLK:(2 * r + 2) * BLK])
            for r in range(2)]


def _head_rows(t, k):
    return t[k * BLK:(k + 1) * BLK]


def _per_head_column(values):
    head = lax.broadcasted_iota(jnp.int32, (GROUP * BLK, 1), 0) // BLK
    col = jnp.full((GROUP * BLK, 1), values[0], F32)
    for k in range(1, GROUP):
        col = jnp.where(head == k, values[k], col)
    return col


def _window_logits(q4, kc, kp, bias4, own, absent):
    sc = jnp.where(own, _dot_nt(q4, kc), _dot_nt(q4, kp)) * SWA_SCALE + bias4
    return jnp.where(absent, NEG, sc)


def _split_window(t, own):
    zero = jnp.zeros_like(t)
    return jnp.where(own, t, zero), jnp.where(own, zero, t)


def _swa_fwd(q, kv, bias, sinks, name, carry=None):
    s = q.shape[0]
    nb = s // BLK
    kvw = 2 * N_KV * HEAD

    def body(q_ref, kc_ref, kp_ref, b_ref, sk_ref, o_ref, lse_ref):
        own, absent, low4 = _window_masks(pl.program_id(0))
        low = low4[:BLK]
        lane = lax.broadcasted_iota(jnp.int32, (BLK, 128), 1)
        lse_t = jnp.zeros((BLK, 128), F32)
        for g in range(N_KV):
            q4 = _stack_heads(q_ref, g, low)
            kc, kp = _kv_twice(kc_ref, 0, g, low), _kv_twice(kp_ref, 0, g, low)
            vc, vp = _kv_twice(kc_ref, N_KV * HEAD, g, low), _kv_twice(kp_ref, N_KV * HEAD, g, low)
            bias4 = b_ref[GROUP * g:GROUP * (g + 1)].reshape(GROUP * BLK, BLK)
            sc = _window_logits(q4, kc, kp, bias4, own, absent)
            sk = _per_head_column([sk_ref[0, GROUP * g + k] for k in range(GROUP)])
            m = jnp.maximum(jnp.max(sc, axis=1, keepdims=True), sk)
            p = jnp.exp(sc - m)
            l = jnp.sum(p, axis=1, keepdims=True) + jnp.exp(sk - m)
            p_own, p_prev = _split_window(p.astype(BF16), own)
            out = (_dot(p_own, vc) + _dot(p_prev, vp)) * (1.0 / l)
            for r, slab in enumerate(_unstack_heads(out, low)):
                o_ref[:, PAIR * (2 * g + r): PAIR * (2 * g + r + 1)] = slab.astype(BF16)
            lse4 = m + jnp.log(l)
            for k in range(GROUP):
                lse_t = jnp.where(lane == GROUP * g + k, _head_rows(lse4, k), lse_t)
        lse_ref[...] = lse_t

    call = _pcall(body, name=name, grid=(nb,),
                  in_specs=[pl.BlockSpec((BLK, D), lambda n: (n, 0)),
                            pl.BlockSpec((BLK, kvw), lambda n: (n, 0)),
                            pl.BlockSpec((BLK, kvw), lambda n: (jnp.maximum(n - 1, 0), 0)),
                            pl.BlockSpec((N_HEADS, BLK, BLK), lambda n: (0, 0, 0)),
                            pl.BlockSpec(memory_space=pltpu.SMEM)],
                  out_specs=[pl.BlockSpec((BLK, D), lambda n: (n, 0)), pl.BlockSpec((BLK, 128), lambda n: (n, 0))],
                  out_shape=[_sds((s, D), BF16), _sds((s, 128), F32)], carry=carry)
    return _carried(call, (q, kv, kv, bias, sinks), carry)


def _fold_halves(t, g, low):
    folded = jnp.where(low, t, 0.0) + pltpu.roll(jnp.where(low, 0.0, t), HEAD, 1)
    return folded if g % 2 == 0 else pltpu.roll(folded, HEAD, 1)


def _swa_bwd(q, kv, attn, dattn, lse, bias, sinks, name, carry=None):
    s = q.shape[0]
    nb = s // BLK
    kvw = 2 * N_KV * HEAD
    voff = N_KV * HEAD

    def body(q_ref, kc_ref, kp_ref, o_ref, do_ref, lse_ref, b_ref, skrow_ref, dq_ref, dkv_ref, dbias_ref, dsk_ref,
             dq_hold, kv_hold, dq_new, kv_prev, kv_cur):
        n = pl.program_id(0)

        @pl.when(n == 0)
        def _():
            dbias_ref[...] = jnp.zeros_like(dbias_ref)
            dsk_ref[...] = jnp.zeros_like(dsk_ref)
            dq_hold[...] = jnp.zeros_like(dq_hold)
            kv_hold[...] = jnp.zeros_like(kv_hold)

        @pl.when(n < nb)
        def _():
            own, absent, low4 = _window_masks(n)
            low = low4[:BLK]
            lane = lax.broadcasted_iota(jnp.int32, (BLK, 128), 1)
            delta_t = jnp.zeros((BLK, 128), F32)
            ones = jnp.ones((PAIR, 128), BF16)
            for pair_of_kv in range(N_KV // 2):
                slab_grads = [jnp.zeros((BLK, PAIR), F32) for _ in range(4)]
                for g in (2 * pair_of_kv, 2 * pair_of_kv + 1):
                    q4, do4 = _stack_heads(q_ref, g, low), _stack_heads(do_ref, g, low)
                    kc, kp = _kv_twice(kc_ref, 0, g, low), _kv_twice(kp_ref, 0, g, low)
                    vc, vp = _kv_twice(kc_ref, voff, g, low), _kv_twice(kp_ref, voff, g, low)
                    o_slabs = [o_ref[:, PAIR * (2 * g + r): PAIR * (2 * g + r + 1)] for r in range(2)]
                    o4 = jnp.concatenate([o_slabs[0], o_slabs[0], o_slabs[1], o_slabs[1]], axis=0)
                    delta = _dot(do4 * o4, ones)
                    heads = range(GROUP * g, GROUP * (g + 1))
                    lse4 = jnp.concatenate([lse_ref[:, h:h + 1] for h in heads], axis=0)
                    bias4 = b_ref[GROUP * g:GROUP * (g + 1)].reshape(GROUP * BLK, BLK)
                    p = jnp.exp(_window_logits(q4, kc, kp, bias4, own, absent) - lse4)
                    dp = jnp.where(own, _dot_nt(do4, vc), _dot_nt(do4, vp))
                    ds = p * (dp - delta)
                    dbias_ref[GROUP * g:GROUP * (g + 1)] += ds.reshape(GROUP, BLK, BLK)
                    for k, h in enumerate(heads):
                        delta_t = jnp.where(lane == h, _head_rows(delta, k), delta_t)
                    ds_own, ds_prev = _split_window((ds * SWA_SCALE).astype(BF16), own)
                    p_own, p_prev = _split_window(p.astype(BF16), own)
                    dq4 = _dot(ds_own, kc) + _dot(ds_prev, kp)
                    for r, slab in enumerate(_unstack_heads(dq4, low)):
                        dq_new[:, PAIR * (2 * g + r): PAIR * (2 * g + r + 1)] = slab
                    grads = [_dot_tn(ds_own, q4), _dot_tn(ds_prev, q4), _dot_tn(p_own, do4), _dot_tn(p_prev, do4)]
                    slab_grads = [t + _fold_halves(dk, g, low) for t, dk in zip(slab_grads, grads)]
                ks = slice(PAIR * pair_of_kv, PAIR * (pair_of_kv + 1))
                vs = slice(voff + PAIR * pair_of_kv, voff + PAIR * (pair_of_kv + 1))
                kv_cur[:, ks], kv_prev[:, ks], kv_cur[:, vs], kv_prev[:, vs] = slab_grads
            dsk_ref[...] -= jnp.sum(jnp.exp(skrow_ref[...] - lse_ref[...]) * delta_t, axis=0, keepdims=True)

        @pl.when(n == nb)
        def _():
            kv_prev[...] = jnp.zeros_like(kv_prev)

        dq_ref[...] = dq_hold[...].astype(BF16)
        dkv_ref[...] = (kv_hold[...] + kv_prev[...]).astype(BF16)

        @pl.when(n < nb)
        def _():
            dq_hold[...] = dq_new[...]
            kv_hold[...] = kv_cur[...]

    def cur(n):
        return jnp.minimum(n, nb - 1)

    call = _pcall(body, name=name, grid=(nb + 1,), carry=carry,
                  in_specs=[pl.BlockSpec((BLK, D), lambda n: (cur(n), 0)),
                            pl.BlockSpec((BLK, kvw), lambda n: (cur(n), 0)),
                            pl.BlockSpec((BLK, kvw), lambda n: (jnp.maximum(cur(n) - 1, 0), 0)),
                            pl.BlockSpec((BLK, D), lambda n: (cur(n), 0)),
                            pl.BlockSpec((BLK, D), lambda n: (cur(n), 0)),
                            pl.BlockSpec((BLK, 128), lambda n: (cur(n), 0)),
                            pl.BlockSpec((N_HEADS, BLK, BLK), lambda n: (0, 0, 0)),
                            pl.BlockSpec((1, 128), lambda n: (0, 0))],
                  out_specs=[pl.BlockSpec((BLK, D), lambda n: (jnp.maximum(n - 1, 0), 0)),
                             pl.BlockSpec((BLK, kvw), lambda n: (jnp.maximum(n - 1, 0), 0)),
                             pl.BlockSpec((N_HEADS, BLK, BLK), lambda n: (0, 0, 0)),
                             pl.BlockSpec((1, 128), lambda n: (0, 0))],
                  out_shape=[_sds((s, D), BF16), _sds((s, kvw), BF16), _sds((N_HEADS, BLK, BLK), F32),
                             _sds((1, 128), F32)],
                  scratch=[pltpu.VMEM((BLK, D), F32), pltpu.VMEM((BLK, kvw), F32), pltpu.VMEM((BLK, D), F32),
                           pltpu.VMEM((BLK, kvw), F32), pltpu.VMEM((BLK, kvw), F32)])
    sink_row = jnp.pad(sinks, ((0, 0), (0, 128 - N_HEADS)))
    return _carried(call, (q, kv, kv, attn, dattn, lse, bias, sink_row), carry)


HALO = 16
CW = D


def _conv_taps(cu, halo_cu, first_tile):
    row = lax.broadcasted_iota(jnp.int32, cu.shape, 0)
    halo_cu = jnp.where(first_tile, 0.0, halo_cu)
    c1 = jnp.where(row == 0, halo_cu[HALO - 1:HALO], pltpu.roll(cu, 1, 0))
    c2 = jnp.where(row == 0, halo_cu[HALO - 2:HALO - 1],
                   jnp.where(row == 1, halo_cu[HALO - 1:HALO], pltpu.roll(cu, 2, 0)))
    return c1, c2


def _conv_merge_fwd(pa, attn, convw, name, ts=256, carry=None):
    _, s, _ = pa.shape
    ts = min(ts, s)
    hb = ts // HALO

    def body(pa_ref, hp_ref, at_ref, w_ref, o_ref):
        i = pl.program_id(1)
        cu = pa_ref[0].astype(F32) * pa_ref[2].astype(F32)
        c1, c2 = _conv_taps(cu, hp_ref[0].astype(F32) * hp_ref[2].astype(F32), i == 0)
        w = w_ref[...]
        c3 = w[0:1] * c2 + w[1:2] * c1 + w[2:3] * cu
        conv = pa_ref[1].astype(F32) * c3
        o_ref[...] = (jax.nn.sigmoid(pa_ref[3].astype(F32)) * at_ref[...].astype(F32)
                      + jax.nn.sigmoid(pa_ref[4].astype(F32)) * conv).astype(BF16)

    call = _pcall(body, name=name, grid=(D // CW, s // ts), carry=carry,
                  in_specs=[pl.BlockSpec((5, ts, CW), lambda c, i: (0, i, c)),
                            pl.BlockSpec((5, HALO, CW), lambda c, i: (0, jnp.maximum(i * hb - 1, 0), c)),
                            pl.BlockSpec((ts, CW), lambda c, i: (i, c)),
                            pl.BlockSpec((8, CW), lambda c, i: (0, c))],
                  out_specs=pl.BlockSpec((ts, CW), lambda c, i: (i, c)),
                  out_shape=_sds((s, D), BF16))
    return _carried(call, (pa, pa, attn, convw), carry)


def _conv_merge_bwd(dmerged, pa, attn, convw, name, ts=256, carry=None):
    _, s, _ = pa.shape
    ts = min(ts, s)
    hb = ts // HALO
    last_hb = s // HALO - 1

    def body(dm_ref, pa_ref, at_ref, w_ref, hp_ref, hn_ref, dmn_ref, dat_ref, dpa_ref, dw_ref):
        i = pl.program_id(1)
        last = i == pl.num_programs(1) - 1
        dm = dm_ref[...].astype(F32)
        cp, bp, u = pa_ref[0].astype(F32), pa_ref[1].astype(F32), pa_ref[2].astype(F32)
        sa = jax.nn.sigmoid(pa_ref[3].astype(F32))
        sc = jax.nn.sigmoid(pa_ref[4].astype(F32))
        at = at_ref[...].astype(F32)
        cu = cp * u
        c1, c2 = _conv_taps(cu, hp_ref[0].astype(F32) * hp_ref[2].astype(F32), i == 0)
        w = w_ref[...]
        c3 = w[0:1] * c2 + w[1:2] * c1 + w[2:3] * cu
        dconv = dm * sc
        dc3 = dconv * bp
        nxt = dmn_ref[...].astype(F32) * jax.nn.sigmoid(hn_ref[4].astype(F32)) * hn_ref[1].astype(F32)
        nxt = jnp.where(last, 0.0, nxt)
        row = lax.broadcasted_iota(jnp.int32, dc3.shape, 0)
        d1 = jnp.where(row == ts - 1, nxt[0:1], pltpu.roll(dc3, ts - 1, 0))
        d2 = jnp.where(row == ts - 2, nxt[0:1], jnp.where(row == ts - 1, nxt[1:2], pltpu.roll(dc3, ts - 2, 0)))
        dcu = w[2:3] * dc3 + w[1:2] * d1 + w[0:1] * d2
        dat_ref[...] = (dm * sa).astype(BF16)
        dpa_ref[0] = (dcu * u).astype(BF16)
        dpa_ref[1] = (dconv * c3).astype(BF16)
        dpa_ref[2] = (dcu * cp).astype(BF16)
        dpa_ref[3] = (dm * at * sa * (1.0 - sa)).astype(BF16)
        dpa_ref[4] = (dm * bp * c3 * sc * (1.0 - sc)).astype(BF16)

        @pl.when(i == 0)
        def _():
            dw_ref[...] = jnp.zeros_like(dw_ref)

        dw_ref[0:1, :] += jnp.sum(dc3 * c2, axis=0, keepdims=True)
        dw_ref[1:2, :] += jnp.sum(dc3 * c1, axis=0, keepdims=True)
        dw_ref[2:3, :] += jnp.sum(dc3 * cu, axis=0, keepdims=True)

    call = _pcall(body, name=name, grid=(D // CW, s // ts), carry=carry,
                  in_specs=[pl.BlockSpec((ts, CW), lambda c, i: (i, c)),
                            pl.BlockSpec((5, ts, CW), lambda c, i: (0, i, c)),
                            pl.BlockSpec((ts, CW), lambda c, i: (i, c)),
                            pl.BlockSpec((8, CW), lambda c, i: (0, c)),
                            pl.BlockSpec((5, HALO, CW), lambda c, i: (0, jnp.maximum(i * hb - 1, 0), c)),
                            pl.BlockSpec((5, HALO, CW), lambda c, i: (0, jnp.minimum((i + 1) * hb, last_hb), c)),
                            pl.BlockSpec((HALO, CW), lambda c, i: (jnp.minimum((i + 1) * hb, last_hb), c))],
                  out_specs=[pl.BlockSpec((ts, CW), lambda c, i: (i, c)),
                             pl.BlockSpec((5, ts, CW), lambda c, i: (0, i, c)),
                             pl.BlockSpec((8, CW), lambda c, i: (0, c))],
                  out_shape=[_sds((s, D), BF16), _sds((5, s, D), BF16), _sds((8, D), F32)])
    return _carried(call, (dmerged, pa, attn, convw, pa, pa, dmerged), carry)


def _xattn_fwd(hx, wq, kv, name, tq=1024):
    s, _ = hx.shape
    nm = kv.shape[1]
    tq = min(tq, s)

    def body(h_ref, wq_ref, kv_ref, q_ref, o_ref, lse_ref):
        q_ref[...] = _dot(h_ref[...], wq_ref[...]).astype(BF16)
        lane = lax.broadcasted_iota(jnp.int32, (tq, 128), 1)
        lse_t = jnp.zeros((tq, 128), F32)
        for h in range(XH):
            hs = slice(XHD * h, XHD * (h + 1))
            sc = _dot_nt(q_ref[:, hs], kv_ref[h]) * (XHD ** -0.5)
            m = jnp.max(sc, axis=1, keepdims=True)
            p = jnp.exp(sc - m)
            l = jnp.sum(p, axis=1, keepdims=True)
            o_ref[:, hs] = (_dot(p.astype(BF16), kv_ref[XH + h]) * (1.0 / l)).astype(BF16)
            lse_t = jnp.where(lane == h, m + jnp.log(l), lse_t)
        lse_ref[...] = lse_t

    return _pcall(body, name=name, grid=(s // tq,),
                  in_specs=[pl.BlockSpec((tq, D), lambda i: (i, 0)), pl.BlockSpec((D, D), lambda i: (0, 0)),
                            pl.BlockSpec((2 * XH, nm, XHD), lambda i: (0, 0, 0))],
                  out_specs=[pl.BlockSpec((tq, D), lambda i: (i, 0)), pl.BlockSpec((tq, D), lambda i: (i, 0)),
                             pl.BlockSpec((tq, 128), lambda i: (i, 0))],
                  out_shape=[_sds((s, D), BF16), _sds((s, D), BF16), _sds((s, 128), F32)])(hx, wq, kv)


def _xattn_bwd(q, kv, o, dy, wo, lse, name, tq=1024, carry=None):
    s, _ = q.shape
    nm = kv.shape[1]
    tq = min(tq, s)

    def body(q_ref, kv_ref, o_ref, dy_ref, wo_ref, lse_ref, dq_ref, dkv_ref, do_ref):
        @pl.when(pl.program_id(0) == 0)
        def _():
            dkv_ref[...] = jnp.zeros_like(dkv_ref)

        do_ref[...] = _dot_nt(dy_ref[...], wo_ref[...]).astype(BF16)
        for h in range(XH):
            hs = slice(XHD * h, XHD * (h + 1))
            qh, kh, vh, dob = q_ref[:, hs], kv_ref[h], kv_ref[XH + h], do_ref[:, hs]
            p = jnp.exp(_dot_nt(qh, kh) * (XHD ** -0.5) - lse_ref[:, h:h + 1])
            dp = _dot_nt(dob, vh)
            delta = jnp.sum(dob.astype(F32) * o_ref[:, hs].astype(F32), axis=1, keepdims=True)
            dsb = (p * (dp - delta) * (XHD ** -0.5)).astype(BF16)
            dq_ref[:, hs] = _dot(dsb, kh).astype(BF16)
            dkv_ref[h] += _dot_tn(dsb, qh)
            dkv_ref[XH + h] += _dot_tn(p.astype(BF16), dob)

    call = _pcall(body, name=name, grid=(s // tq,), carry=carry,
                  in_specs=[pl.BlockSpec((tq, D), lambda i: (i, 0)), pl.BlockSpec((2 * XH, nm, XHD), lambda i: (0, 0, 0)),
                            pl.BlockSpec((tq, D), lambda i: (i, 0)), pl.BlockSpec((tq, D), lambda i: (i, 0)),
                            pl.BlockSpec((D, D), lambda i: (0, 0)), pl.BlockSpec((tq, 128), lambda i: (i, 0))],
                  out_specs=[pl.BlockSpec((tq, D), lambda i: (i, 0)), pl.BlockSpec((2 * XH, nm, XHD), lambda i: (0, 0, 0))],
                  out_shape=[_sds((s, D), BF16), _sds((2 * XH, nm, XHD), F32)], scratch=[pltpu.VMEM((tq, D), BF16)])
    return _carried(call, (q, kv, o, dy, wo, lse), carry)


def _ffn_down_bwd(dxb, wd4, gu4, name, tm=512, carry=None, behind=None):
    s, _ = dxb.shape
    tm = min(tm, s)

    def body(dx_ref, w_hbm, gu_ref, o_ref, w_ref, w_sem):
        _load_once(w_hbm, w_ref, w_sem)
        for p in range(4):
            for rows in _row_chunks(tm):
                da = _dot_nt(dx_ref[rows, :], w_ref[p])
                g = gu_ref[0, p, rows, :].astype(F32)
                u = gu_ref[1, p, rows, :].astype(F32)
                sg = jax.nn.sigmoid(g)
                t = da * sg
                o_ref[0, p, rows, :] = (t * u * (1.0 + g - g * sg)).astype(BF16)
                o_ref[1, p, rows, :] = (t * g).astype(BF16)

    block = pl.BlockSpec((2, 4, tm, FS), lambda i: (0, 0, i, 0))
    call = _pcall(body, name=name, grid=(s // tm,), carry=carry, behind=behind,
                  in_specs=[pl.BlockSpec((tm, D), lambda i: (i, 0)), HBM_SPEC, block],
                  out_specs=block, out_shape=_sds((2, 4, s, FS), BF16), scratch=_resident(wd4))
    return _carried(call, (dxb, wd4, gu4), carry)


def _mm_tn(a, b, name, scale=1.0, carry=None):
    pa_n, s, m = a.shape
    pb_n, _, n = b.shape
    po = max(pa_n, pb_n)
    tk = 1024
    if po == 1 and s > tk and s % tk == 0:
        def body_k(a_ref, b_ref, o_ref, acc_ref):
            k = pl.program_id(0)
            part = _dot_tn(a_ref[...], b_ref[...])

            @pl.when(k == 0)
            def _():
                acc_ref[...] = part

            @pl.when(k > 0)
            def _():
                acc_ref[...] += part

            @pl.when(k == s // tk - 1)
            def _():
                o_ref[...] = (scale * acc_ref[...]).astype(BF16)

        call = _pcall(body_k, name=name, grid=(s // tk,), carry=carry,
                      in_specs=[pl.BlockSpec((None, tk, m), lambda k: (0, k, 0)),
                                pl.BlockSpec((None, tk, n), lambda k: (0, k, 0))],
                      out_specs=pl.BlockSpec((None, m, n), lambda k: (0, 0, 0)),
                      out_shape=_sds((1, m, n), BF16), scratch=[pltpu.VMEM((m, n), F32)])
        return _carried(call, (a, b), carry)
    tn = n if po >= 4 else min(n, 256)

    def body(a_ref, b_ref, o_ref):
        o_ref[...] = (scale * _dot_tn(a_ref[...], b_ref[...])).astype(BF16)

    call = _pcall(body, name=name, grid=(po, n // tn), carry=carry,
                  in_specs=[pl.BlockSpec((None, s, m), lambda o, j: (o if pa_n > 1 else 0, 0, 0)),
                            pl.BlockSpec((None, s, tn), lambda o, j: (o if pb_n > 1 else 0, 0, j))],
                  out_specs=pl.BlockSpec((None, m, tn), lambda o, j: (o, 0, j)),
                  out_shape=_sds((po, m, n), BF16))
    return _carried(call, (a, b), carry)


def _mm_tn_rows(a, b, name, total_rows, row0, begun=None, tm=512, carry=None):
    p, s, m = a.shape
    n = b.shape[1]
    tm = min(tm, m)
    tiles = m // tm
    assert row0 % tm == 0 and m % tm == 0, (row0, m, tm)

    def body(a_ref, b_ref, *rest):
        rest[-1][...] = _dot_tn(a_ref[...], b_ref[...]).astype(BF16)

    in_specs = [pl.BlockSpec((None, s, tm), lambda o, i: (o, 0, i)), pl.BlockSpec((s, n), lambda o, i: (0, 0))]
    call = _pcall(body, name=name, grid=(p, tiles), in_specs=in_specs + ([HBM_SPEC] if begun is not None else []),
                  out_specs=pl.BlockSpec((tm, n), lambda o, i: (row0 // tm + o * tiles + i, 0)),
                  out_shape=_sds((total_rows, n), BF16), aliases={2: 0} if begun is not None else None, carry=carry)
    return _carried(call, (a, b, begun) if begun is not None else (a, b), carry)


def _sum_dots(a_ref, b_ref, nj, bt, rows=slice(None)):
    dot = _dot_nt if bt else _dot
    acc = dot(a_ref[0, rows, :], b_ref[0])
    for j in range(1, nj):
        acc = acc + dot(a_ref[j, rows, :], b_ref[j])
    return acc


def _mm_acc(a, b, name, out_dtype, tm=1024, bt=False, carry=None):
    nj, s, k = a.shape
    n = b.shape[1] if bt else b.shape[2]
    tm = min(tm, s)

    def body(a_ref, b_ref, o_ref):
        o_ref[...] = _sum_dots(a_ref, b_ref, nj, bt).astype(out_dtype)

    call = _pcall(body, name=name, grid=(s // tm,), carry=carry,
                  in_specs=[pl.BlockSpec((nj, tm, k), lambda i: (0, i, 0)),
                            pl.BlockSpec(b.shape, lambda i: (0, 0, 0))],
                  out_specs=pl.BlockSpec((tm, n), lambda i: (i, 0)), out_shape=_sds((s, n), out_dtype))
    return _carried(call, (a, b), carry)


def _rms_bwd_call(name, acts, weights, scratch, load, dh_rows, *, x, gain, dres, tm, carry, behind=None):
    s, n = x.shape
    tm = min(tm, s)
    n_act, n_w = len(acts), len(weights)

    def body(*refs):
        act_refs, w_refs = refs[:n_act], refs[n_act:n_act + n_w]
        x_ref, g_ref, r_ref, dx_ref, dxb_ref, dg_ref = refs[n_act + n_w:n_act + n_w + 6]
        held = refs[n_act + n_w + 6:]
        load(w_refs, held)

        @pl.when(pl.program_id(0) == 0)
        def _():
            dg_ref[...] = jnp.zeros_like(dg_ref)

        for rows in _row_chunks(tm):
            dh = dh_rows(act_refs, held, rows)
            xv = x_ref[rows, :]
            r = lax.rsqrt(jnp.mean(xv * xv, axis=-1, keepdims=True) + EPS)
            xh = xv * r
            dyg = dh * g_ref[...]
            dx = r_ref[rows, :] + r * (dyg - xh * jnp.mean(dyg * xh, axis=-1, keepdims=True))
            dx_ref[rows, :] = dx
            dxb_ref[rows, :] = dx.astype(BF16)
            dg_ref[...] += jnp.sum(dh * xh, axis=0, keepdims=True)

    def tile(a):
        return (pl.BlockSpec((tm, a.shape[1]), lambda i: (i, 0)) if a.ndim == 2
                else pl.BlockSpec((a.shape[0], tm, a.shape[2]), lambda i: (0, i, 0)))

    row = pl.BlockSpec((tm, n), lambda i: (i, 0))
    in_specs = [tile(a) for a in acts] + [HBM_SPEC] * n_w + [row, pl.BlockSpec((1, n), lambda i: (0, 0)), row]
    call = _pcall(body, name=name, grid=(s // tm,), in_specs=in_specs, carry=carry, behind=behind,
                  out_specs=[row, row, pl.BlockSpec((1, n), lambda i: (0, 0))],
                  out_shape=[_sds((s, n), F32), _sds((s, n), BF16), _sds((1, n), F32)], scratch=scratch)
    return _carried(call, tuple(acts) + tuple(weights) + (x, gain, dres), carry)


def _mm_acc_rms_bwd(a, b, name, *, x, gain, dres, scale=None, tm=512, bt=False, carry=None, behind=None):
    def load(w_refs, held):
        _load_once(w_refs[0], held[0], held[1])

    def dh_rows(act_refs, held, rows):
        dh = _sum_dots(act_refs[0], held[0], a.shape[0], bt, rows)
        return dh if scale is None else scale * dh

    return _rms_bwd_call(name, [a], [b], _resident(b), load, dh_rows, x=x, gain=gain, dres=dres, tm=tm, carry=carry,
                         behind=behind)


def _in_proj_bwd(dpa, dq, dkv, w_in_t, name, *, x, gain, dres, tm=512, carry=None, behind=None):
    def load(w_refs, held):
        _load_in_proj(w_refs[0], *held)

    def dh_rows(act_refs, held, rows):
        dpa_ref, dq_ref, dkv_ref = act_refs
        wq_ref, wkv_ref, wa_ref, _ = held
        dh = _dot(dq_ref[rows, :], wq_ref[...]) + _dot(dkv_ref[rows, :], wkv_ref[...])
        return dh + _sum_dots(dpa_ref, wa_ref, N_SEG, False, rows)

    return _rms_bwd_call(name, [dpa, dq, dkv], [w_in_t], IN_PROJ_WEIGHTS, load, dh_rows, x=x, gain=gain, dres=dres,
                         tm=tm, carry=carry, behind=behind)


def _adam(w, g, m, v):
    m2 = ADAM_B1 * m + (1.0 - ADAM_B1) * g
    v2 = ADAM_B2 * v + (1.0 - ADAM_B2) * (g * g)
    m_hat = m2 / (1.0 - ADAM_B1 ** ADAM_STEP)
    v_hat = v2 / (1.0 - ADAM_B2 ** ADAM_STEP)
    delta = -ADAM_LR * (m_hat / (jnp.sqrt(v_hat) + ADAM_EPS) + ADAM_WD * w)
    return delta, m2, v2


def _adamw(parts, w, m, v, name, behind=None):
    _, r, c = parts.shape
    tr = max(t for t in range(16, 257, 16) if r % t == 0)

    def body(p_ref, w_ref, m_ref, v_ref, g_ref, d_ref, m2_ref, v2_ref):
        g = p_ref[0].astype(F32)
        for i in range(1, N_DEV):
            g = g + p_ref[i].astype(F32)
        delta, m2, v2 = _adam(w_ref[...], g, m_ref[...], v_ref[...])
        g_ref[...] = g
        d_ref[...] = delta
        m2_ref[...] = m2
        v2_ref[...] = v2

    blk = pl.BlockSpec((tr, c), lambda i: (i, 0))
    return _pcall(body, name=name, grid=(r // tr,), behind=behind,
                  in_specs=[pl.BlockSpec((N_DEV, tr, c), lambda i: (0, i, 0)), blk, blk, blk],
                  out_specs=[blk] * 4, out_shape=[_sds((r, c), F32)] * 4)(parts, w, m, v)


def _position():
    return lax.axis_index("x"), lax.axis_index("y"), lax.axis_index("c")


def _slot(px, py, pc):
    return 4 * px + 2 * py + pc


def _row_window(ref, rows):
    r0, r1 = rows
    return ref if (r0, r1) == (0, ref.shape[0]) else ref.at[pl.ds(r0, r1 - r0)]


def _split_items(items):
    sources = [src for src, _, _ in items]
    begun = [(a, dest) for a, (_, _, dest) in enumerate(items) if dest is not None]
    aliases = {len(sources) + k: a for k, (a, _) in enumerate(begun)}
    return sources + [dest for _, dest in begun], [rows for _, rows, _ in items], aliases


def _gather_carry(items):
    na = len(items)
    carry_ins, windows, aliases = _split_items(items)

    def plan(ins, outs, sems):
        send_sems, recv_sems, local_sems = sems
        x, y, c = _position()
        me, sibling = (x, y, c), (x, y, 1 - c)
        chips = [(1 - x, y), (x, 1 - y), (1 - x, 1 - y)]
        ins = [_row_window(ins[a], windows[a]) for a in range(na)]

        def block_rows(a, block):
            return _row_window(outs[a].at[_slot(*block)], windows[a])

        def copy(a, k, block, to, src=None):
            rows = block_rows(a, block)
            return pltpu.make_async_remote_copy(src_ref=rows if src is None else src, dst_ref=rows,
                                                send_sem=send_sems.at[k, a], recv_sem=recv_sems.at[k, a],
                                                device_id=to, device_id_type=MESH)

        mine = [pltpu.make_async_copy(ins[a], block_rows(a, me), local_sems.at[a]) for a in range(na)]
        first = [copy(a, 0, me, sibling, src=ins[a]) for a in range(na)]
        for j, chip in enumerate(chips):
            first += [copy(a, 1 + j, me, (*chip, c), src=ins[a]) for a in range(na)]
        landed = [[copy(a, 1 + j, (*chip, c), me) for a in range(na)] for j, chip in enumerate(chips)]
        passed = [[copy(a, 4 + j, (*chip, c), sibling) for a in range(na)] for j, chip in enumerate(chips)]
        from_sibling = [copy(a, 0, sibling, me) for a in range(na)]
        for j, chip in enumerate(chips):
            from_sibling += [copy(a, 4 + j, (*chip, 1 - c), me) for a in range(na)]
        return mine, first, landed, passed, from_sibling

    def start(ins, outs, sems):
        mine, first, _, _, _ = plan(ins, outs, sems)
        for cp in mine + first:
            cp.start()

    def mid(ins, outs, sems):
        _, _, landed, passed, _ = plan(ins, outs, sems)
        for over_ici, onward in zip(landed, passed):
            for cp, fwd in zip(over_ici, onward):
                cp.wait_recv()
                fwd.start()

    def finish(ins, outs, sems):
        mine, first, _, passed, from_sibling = plan(ins, outs, sems)
        for cp in from_sibling:
            cp.wait_recv()
        for cp in first + [fwd for onward in passed for fwd in onward]:
            cp.wait_send()
        for cp in mine:
            cp.wait()

    return _Carry(carry_ins, [_sds((N_DEV,) + src.shape, src.dtype) for src, _, _ in items],
                  [pltpu.SemaphoreType.DMA((7, na)), pltpu.SemaphoreType.DMA((7, na)),
                   pltpu.SemaphoreType.DMA((na,))], start, finish, mid, aliases)


def _exchange_carry(scattered, replicated=()):
    items = list(scattered) + [(a, (0, a.shape[0]), None) for a in replicated]
    na, ns = len(items), len(scattered)
    carry_ins, windows, aliases = _split_items(items)

    def plan(ins, outs, sems):
        send_sems, recv_sems, local_sems = sems
        me = _slot(*_position())

        def source(a, j):
            return _row_window(ins[a].at[j] if a < ns else ins[a], windows[a])

        def copy(a, j, i):
            return pltpu.make_async_remote_copy(src_ref=source(a, j), dst_ref=_row_window(outs[a].at[i], windows[a]),
                                                send_sem=send_sems.at[j, a], recv_sem=recv_sems.at[i, a],
                                                device_id=(j >> 2, (j >> 1) & 1, j & 1), device_id_type=MESH)

        def own(a, j):
            return pltpu.make_async_copy(source(a, j), _row_window(outs[a].at[j], windows[a]), local_sems.at[a])

        return me, copy, own

    def start(ins, outs, sems):
        me, copy, own = plan(ins, outs, sems)
        for a in range(na):
            for j in range(N_DEV):
                @pl.when(me == j)
                def _():
                    own(a, j).start()

                @pl.when(me != j)
                def _():
                    copy(a, j, me).start()

    def finish(ins, outs, sems):
        me, copy, own = plan(ins, outs, sems)
        for a in range(na):
            for j in range(N_DEV):
                @pl.when(me == j)
                def _():
                    for i in range(N_DEV):
                        if i != j:
                            copy(a, j, i).wait_recv()
                    own(a, j).wait()

                @pl.when(me != j)
                def _():
                    copy(a, j, me).wait_send()

    return _Carry(carry_ins, [_sds((N_DEV,) + src.shape[-2:], src.dtype) for src, _, _ in items],
                  [pltpu.SemaphoreType.DMA((N_DEV, na)), pltpu.SemaphoreType.DMA((N_DEV, na)),
                   pltpu.SemaphoreType.DMA((na,))], start, finish, None, aliases)


HBM_ARRAY = pl.BlockSpec(memory_space=pltpu.HBM)
SEMAPHORES = pl.BlockSpec(memory_space=pltpu.SEMAPHORE)
DATAFLOW = pltpu.SideEffectType.DATAFLOW_SIDE_EFFECTING


def _exchange_copy(parts_ref, land_ref, send_sems, recv_sems, me, j):
    return pltpu.make_async_remote_copy(src_ref=parts_ref.at[j], dst_ref=land_ref.at[me], send_sem=send_sems.at[j],
                                        recv_sem=recv_sems.at[me], device_id=(j >> 2, (j >> 1) & 1, j & 1),
                                        device_id_type=MESH)


def _exchange_start(parts, name):
    def body(parts_ref, land_ref, send_sems, recv_sems, parts_thru, land_thru, token):
        me = _slot(*_position())
        for j in range(N_DEV):
            @pl.when(me == j)
            def _():
                pltpu.make_async_copy(parts_ref.at[j], land_ref.at[j], send_sems.at[j]).start()

            @pl.when(me != j)
            def _():
                _exchange_copy(parts_ref, land_ref, send_sems, recv_sems, me, j).start()
        token[...] = jnp.zeros_like(token)

    return pl.pallas_call(
        body, name=name,
        out_shape=(pltpu.SemaphoreType.DMA((N_DEV,)), pltpu.SemaphoreType.DMA((N_DEV,)),
                   pltpu.HBM(parts.shape, parts.dtype), pltpu.HBM(parts.shape, parts.dtype), _sds((8, 128), F32)),
        in_specs=(HBM_ARRAY, HBM_ARRAY),
        out_specs=(SEMAPHORES, SEMAPHORES, HBM_ARRAY, HBM_ARRAY, pl.BlockSpec(memory_space=pltpu.VMEM)),
        input_output_aliases={0: 2, 1: 3}, compiler_params=pltpu.CompilerParams(has_side_effects=DATAFLOW),
    )(pltpu.with_memory_space_constraint(parts, pltpu.HBM),
      pltpu.with_memory_space_constraint(lax.empty(parts.shape, parts.dtype), pltpu.HBM))


def _exchange_wait(send_sems, recv_sems, parts_thru, land_thru, after, name):
    def body(parts_ref, land_ref, send_sems, recv_sems, *rest):
        me = _slot(*_position())
        for j in range(N_DEV):
            @pl.when(me == j)
            def _():
                pltpu.make_async_copy(parts_ref.at[j], land_ref.at[j], send_sems.at[j]).wait()

            @pl.when(me != j)
            def _():
                both = pltpu.make_async_remote_copy(src_ref=parts_ref.at[j], dst_ref=land_ref.at[j],
                                                    send_sem=send_sems.at[j], recv_sem=recv_sems.at[j],
                                                    device_id=(j >> 2, (j >> 1) & 1, j & 1), device_id_type=MESH)
                both.wait_send()
                both.wait_recv()

    return pl.pallas_call(
        body, name=name, out_shape=(pltpu.HBM(parts_thru.shape, parts_thru.dtype),
                                    pltpu.HBM(parts_thru.shape, parts_thru.dtype)),
        in_specs=(HBM_ARRAY, HBM_ARRAY, SEMAPHORES, SEMAPHORES) + (pl.BlockSpec(memory_space=pl.ANY),) * len(after),
        out_specs=(HBM_ARRAY, HBM_ARRAY), input_output_aliases={0: 0, 1: 1},
        compiler_params=pltpu.CompilerParams(has_side_effects=DATAFLOW),
    )(parts_thru, land_thru, send_sems, recv_sems, *after)[1]


class _Mesh:
    def __init__(self, shards):
        self.shards, self.full, self.received, self.cache, self.pending, self.tokens = shards, {}, {}, {}, {}, {}

    def fetch(self, wanted):
        items = []
        for want in wanted:
            name, r0, r1 = want if isinstance(want, tuple) else (want, 0, self.shards[want].shape[0])
            items.append((self.shards[name], (r0, r1), self.full.get(name)))
        return _gather_carry(items)

    def fetched(self, wanted, results):
        self.full.update(zip([want[0] if isinstance(want, tuple) else want for want in wanted], results))

    def send(self, *payloads):
        return _exchange_carry([(parts, rows or (0, parts.shape[1]), self.received.get(name))
                                for name, parts, rows in payloads])

    def sent(self, names, results):
        self.received.update(zip(names, results))

    def send_apart(self, name, parts):
        *self.pending[name], self.tokens[name] = _exchange_start(parts, "exchange_" + name + "_start")
        return self.tokens[name]

    def sent_apart(self, name, after):
        self.received[name] = _exchange_wait(*self.pending.pop(name), after, "exchange_" + name + "_wait")

    def w(self, key):
        if key not in self.cache:
            self.cache[key] = self._layout(key)
        return self.cache[key]

    def _layout(self, key):
        if key in ("gu1", "gu2"):
            return self.full[key]
        if key in ("d1", "d2"):
            return self.full[key].reshape(4, FS, D)
        if key in ("out", "q", "o"):
            return self.full[key].reshape(D, D)
        if key == "kv":
            return self.full["kv"]
        if key == "convw":
            rows = self.full["conv"][:, :3, :].transpose(1, 0, 2).reshape(3, D)
            return jnp.concatenate([rows, jnp.zeros((5, D), F32)], axis=0)
        assert key == "win_t", key
        return self.full["win"].reshape(-1, D)


def _forward_backward(x, mem, target, g, rel_bias, sinks, ex):
    s = x.shape[0]
    def fetching(wanted, call, *args, **kw):
        res, got = call(*args, carry=ex.fetch(wanted), **kw)
        ex.fetched(wanted, got)
        return res

    h1 = fetching(["gu1", "conv"], _rmsnorm, x, g["ffn1"], "norm_ffn1")
    gu1, a1 = fetching(["d1", ("win", 0, 400)], _ffn_up, h1, ex.w("gu1").reshape(2, 4, FS, D), "ffn1_up")
    x1, h2 = fetching([("win", 400, 832)], _mm_res_norm, a1, ex.w("d1"), x, g["mix"], 0.5, "ffn1_down")
    pa, q, kv = fetching(["gu2"], _in_proj, h2, ex.w("win_t"), "in_proj")
    biasm = _bias_build(rel_bias, "bias_build")
    attn, lse = fetching(["out", "kv", "o"], _swa_fwd, q, kv, biasm, sinks, "swa_fwd")
    merged = fetching(["q"], _conv_merge_fwd, pa, attn, ex.w("convw"), "conv_merge_fwd")
    (x2, h3), _ = _mm_res_norm(merged[None], ex.w("out")[None], x1, g["xattn"], 1.0, "out_proj")
    mh, kv2 = _norm_mm(mem, g["mem"], ex.w("kv"), "xattn_kv")
    q2, o, lse2 = _xattn_fwd(h3, ex.w("q"), kv2, "xattn_fwd")
    (x3, h4), _ = _mm_res_norm(o[None], ex.w("o")[None], x2, g["ffn2"], 1.0, "xattn_o")
    gu2, a2 = fetching(["d2"], _ffn_up, h4, ex.w("gu2").reshape(2, 4, FS, D), "ffn2_up")
    dx4, dx4b, loss, d_final = _ffn_down_loss(a2, ex.w("d2"), x3, g["final"], target, "ffn2_down_loss")
    def sending(payloads, call, *args, **kw):
        res, got = call(*args, carry=ex.send(*payloads), **kw)
        ex.sent([name for name, _, _ in payloads], got)
        return res

    dw_d2 = _mm_tn(a2, dx4b[None], "dw_ffn2_down", scale=0.5)[0].reshape(N_DEV, -1, D)
    dgu2 = sending([("d2", dw_d2, (0, 288))], _ffn_down_bwd, dx4b, ex.w("d2"), gu2, "ffn2_down_bwd").reshape(8, s, FS)
    dw_gu2 = sending([("d2", dw_d2, (288, 352))], _mm_tn, dgu2, h4[None], "dw_ffn2_up", scale=0.5)
    dx3, dx3b, d_ffn2 = sending([("gu2", dw_gu2, (0, 368))], _mm_acc_rms_bwd, dgu2, ex.w("gu2"), "ffn2_up_bwd",
                                x=x3, gain=g["ffn2"], dres=dx4, scale=0.5)
    dw_o = _mm_tn(o[None], dx3b[None], "dw_xattn_o")[0].reshape(N_DEV, -1, D)
    (dq2, dkv2), _ = _xattn_bwd(q2, kv2, o, dx3b, ex.w("o"), lse2, "xattn_bwd")
    dw_q = _mm_tn(h3[None], dq2[None], "dw_xattn_q")[0].reshape(N_DEV, -1, D)
    (dx2, dx2b, d_xattn), _ = _mm_acc_rms_bwd(dq2[None], ex.w("q")[None], "xattn_q_bwd", x=x2, gain=g["xattn"],
                                              dres=dx3, bt=True)
    dw_kv, d_mem = _norm_mm_bwd(dkv2, mh, ex.w("kv"), mem, "xattn_kv_bwd")
    dmerged, _ = _mm_acc(dx2b[None], ex.w("out")[None], "out_proj_bwd", BF16, bt=True)
    dw_out = _mm_tn(merged[None], dx2b[None], "dw_out_proj")[0].reshape(N_DEV, -1, D)
    dattn, dpa, d_convw = sending([("kv", dw_kv, None)], _conv_merge_bwd,
                                  dmerged, pa, attn, ex.w("convw"), "conv_merge_bwd")
    dq, dkv, dbias, d_sinks = sending([("gu2", dw_gu2, (368, FS)), ("out", dw_out, None)], _swa_bwd,
                                      q, kv, attn, dattn, lse, biasm, sinks, "swa_bwd")
    d_relb = _bias_bwd(dbias, "bias_bwd")
    w_rows = ex.w("win_t").shape[0]
    dw_in = sending([("o", dw_o, None), ("q", dw_q, None)], _mm_tn_rows, dpa, h2, "dw_in_proj_a", w_rows, NQ + NKV)
    dw_in = _mm_tn_rows(dq[None], h2, "dw_in_proj_q", w_rows, 0, begun=dw_in)[0]
    dw_in = _mm_tn_rows(dkv[None], h2, "dw_in_proj_kv", w_rows, NQ, begun=dw_in)[0].reshape(N_DEV, -1, D)
    (dx1, dx1b, d_mix), _ = _in_proj_bwd(dpa, dq, dkv, ex.w("win_t"), "in_proj_bwd", x=x1, gain=g["mix"], dres=dx2,
                                         behind=ex.send_apart("win", dw_in))
    dw_d1 = _mm_tn(a1, dx1b[None], "dw_ffn1_down", scale=0.5)[0].reshape(N_DEV, -1, D)
    dgu1 = _ffn_down_bwd(dx1b, ex.w("d1"), gu1, "ffn1_down_bwd", behind=ex.send_apart("d1", dw_d1))[0]
    dgu1 = dgu1.reshape(8, s, FS)
    dw_gu1 = _mm_tn(dgu1, h1[None], "dw_ffn1_up", scale=0.5)[0]
    (dx0, _, d_ffn1), _ = _mm_acc_rms_bwd(dgu1, ex.w("gu1"), "ffn1_up_bwd", x=x, gain=g["ffn1"], dres=dx1,
                                          scale=0.5, behind=ex.send_apart("gu1", dw_gu1))

    relb_row = jnp.concatenate([d_relb[:, :REL_BUCKETS].T.reshape(1, REL_BUCKETS * N_HEADS), d_sinks[:, :N_HEADS],
                                jnp.zeros((1, D - REL_BUCKETS * N_HEADS - N_HEADS), F32)], axis=1)
    loss_row = jnp.concatenate([loss[0:1, 0:1], jnp.zeros((1, D - 1), F32)], axis=1)
    small = jnp.concatenate([d_ffn1, d_mix, d_xattn, d_mem, d_ffn2, d_final, relb_row, loss_row, d_convw[0:3],
                             jnp.zeros((SMALL_ROWS - ROW_CONV - 3, D), F32)], axis=0)
    return dx0, small


def _pack_small(norms, final, relb, sinks, conv_local, me):
    relb_row = jnp.concatenate([relb.reshape(1, -1), sinks.reshape(1, -1),
                                jnp.zeros((1, D - REL_BUCKETS * N_HEADS - N_HEADS), F32)], axis=1)
    conv_rows = lax.dynamic_update_slice(jnp.zeros((3, D), F32), conv_local.reshape(3, -1), (0, 128 * me))
    return jnp.concatenate(list(norms) + [final.reshape(1, D), relb_row, jnp.zeros((1, D), F32), conv_rows,
                                          jnp.zeros((SMALL_ROWS - ROW_CONV - 3, D), F32)], axis=0)


def kernel(x, mem, positions, rel_bias, ffn1_norm, ffn1_w_gu, ffn1_w_down, mix_norm, w_in, sinks, conv_w, w_out, xattn_norm, mem_norm, xattn_wq, xattn_wkv, xattn_wo, ffn2_norm, ffn2_w_gu, ffn2_w_down, final_norm, loss_target, m_rel_bias, m_ffn1_norm, m_ffn1_w_gu, m_ffn1_w_down, m_mix_norm, m_w_in, m_sinks, m_conv_w, m_w_out, m_xattn_norm, m_mem_norm, m_xattn_wq, m_xattn_wkv, m_xattn_wo, m_ffn2_norm, m_ffn2_w_gu, m_ffn2_w_down, m_final_norm, v_rel_bias, v_ffn1_norm, v_ffn1_w_gu, v_ffn1_w_down, v_mix_norm, v_w_in, v_sinks, v_conv_w, v_w_out, v_xattn_norm, v_mem_norm, v_xattn_wq, v_xattn_wkv, v_xattn_wo, v_ffn2_norm, v_ffn2_w_gu, v_ffn2_w_down, v_final_norm):
    del positions
    me = _slot(*_position())
    big = dict(gu1=(ffn1_w_gu, m_ffn1_w_gu, v_ffn1_w_gu), d1=(ffn1_w_down, m_ffn1_w_down, v_ffn1_w_down),
               win=(w_in, m_w_in, v_w_in), out=(w_out, m_w_out, v_w_out), q=(xattn_wq, m_xattn_wq, v_xattn_wq),
               kv=(xattn_wkv, m_xattn_wkv, v_xattn_wkv), o=(xattn_wo, m_xattn_wo, v_xattn_wo),
               gu2=(ffn2_w_gu, m_ffn2_w_gu, v_ffn2_w_gu), d2=(ffn2_w_down, m_ffn2_w_down, v_ffn2_w_down))
    order = list(big)
    transposed = ("gu1", "gu2", "win")
    local = {k: tuple(t[0].T if k in transposed else t[0] for t in big[k]) for k in order}
    shards = {k: local[k][0].astype(BF16) for k in order}
    shards["conv"] = jnp.concatenate([conv_w[0], jnp.zeros((5, 128), F32)], axis=0)
    ex = _Mesh(shards)
    gains = dict(ffn1=ffn1_norm, mix=mix_norm, xattn=xattn_norm, mem=mem_norm, ffn2=ffn2_norm,
                 final=final_norm.reshape(1, D))
    dx, small = _forward_backward(x[0], mem[0], loss_target[0], gains, rel_bias, sinks, ex)
    apart = ("d1", "win", "gu1")
    big_out = {k: _adamw(ex.received[k], *local[k], "adamw_" + k, behind=ex.tokens["gu1"])
               for k in order if k not in apart}
    for k in apart[:-1]:
        ex.sent_apart(k, after=[big_out[j][1] for j in big_out])
        big_out[k] = _adamw(ex.received[k], *local[k], "adamw_" + k)
    small_parts = _run_alone(_exchange_carry([], [small]), "exchange_small", after=[big_out[k][1] for k in big_out])[0]
    packed = [_pack_small(norms, final, relb, sk, conv, me) for norms, final, relb, sk, conv in (
        ((ffn1_norm, mix_norm, xattn_norm, mem_norm, ffn2_norm), final_norm, rel_bias, sinks, conv_w),
        ((m_ffn1_norm, m_mix_norm, m_xattn_norm, m_mem_norm, m_ffn2_norm), m_final_norm, m_rel_bias, m_sinks, m_conv_w),
        ((v_ffn1_norm, v_mix_norm, v_xattn_norm, v_mem_norm, v_ffn2_norm), v_final_norm, v_rel_bias, v_sinks, v_conv_w))]
    small_out = _adamw(small_parts, *packed, "adamw_small")
    ex.sent_apart("gu1", after=[dx, small_out[1]] + [big_out[k][1] for k in big_out])
    big_out["gu1"] = _adamw(ex.received["gu1"], *local["gu1"], "adamw_gu1")
    big_out = {k: [t.T if k in transposed else t for t in big_out[k]] for k in order}

    def unpack(t):
        conv = lax.dynamic_slice(t[ROW_CONV:ROW_CONV + 3], (0, 128 * me), (3, 128))[None]
        nrel = REL_BUCKETS * N_HEADS
        return dict(ffn1_norm=t[0:1], mix_norm=t[1:2], xattn_norm=t[2:3], mem_norm=t[3:4], ffn2_norm=t[4:5],
                    final_norm=t[5], rel_bias=t[ROW_RELB, :nrel].reshape(REL_BUCKETS, N_HEADS),
                    sinks=t[ROW_RELB:ROW_RELB + 1, nrel:nrel + N_HEADS], conv_w=conv)

    names = dict(gu1="ffn1_w_gu", d1="ffn1_w_down", win="w_in", out="w_out", q="xattn_wq", kv="xattn_wkv",
                 o="xattn_wo", gu2="ffn2_w_gu", d2="ffn2_w_down")
    results = []
    for idx in range(4):
        leaves = unpack(small_out[idx])
        leaves.update({names[k]: big_out[k][idx][None] for k in order})
        results.append(leaves)
    weights = ("rel_bias", "ffn1_norm", "ffn1_w_gu", "ffn1_w_down", "mix_norm", "w_in", "sinks", "conv_w", "w_out",
               "xattn_norm", "mem_norm", "xattn_wq", "xattn_wkv", "xattn_wo", "ffn2_norm", "ffn2_w_gu", "ffn2_w_down",
               "final_norm")
    loss = small_out[0][ROW_LOSS, 0]
    return (loss, dx[None], *[leaves[n] for leaves in results for n in weights])
```

```python
import math

import numpy as np
import jax
import jax.numpy as jnp
from jax import lax
from jax.experimental import pallas as pl
from jax.experimental.pallas import tpu as pltpu

F32, BF16 = jnp.float32, jnp.bfloat16
MESH = pl.DeviceIdType.MESH

D = 1024
N_DEV = 8
D_FF = 2816
FS = D_FF // 4
HEAD = 64
N_HEADS, N_KV = 16, 4
BLK = 128
NQ, NKV = N_HEADS * HEAD, 2 * N_KV * HEAD
XH, XHD = 4, 256
REL_BUCKETS, REL_EXACT, REL_MAX_DIST = 32, 16, 128
EPS, NEG = 1e-6, -1e30
ADAM_LR, ADAM_B1, ADAM_B2, ADAM_EPS, ADAM_WD, ADAM_STEP = 0.001, 0.9, 0.999, 1e-08, 0.01, 10
VMEM_LIMIT_V7X = 56 * 2**20
SMALL_ROWS = 16
ROW_RELB, ROW_LOSS, ROW_CONV = 6, 7, 8


def _bucket_thresholds():
    n = np.arange(REL_MAX_DIST)
    nf = np.maximum(n, 1).astype(np.float32)
    large = REL_EXACT + (np.log(nf / np.float32(REL_EXACT)) / np.float32(math.log(REL_MAX_DIST / REL_EXACT))
                         * np.float32(REL_BUCKETS - REL_EXACT)).astype(np.int32)
    b = np.where(n < REL_EXACT, n, np.minimum(large, REL_BUCKETS - 1))
    return [int(np.argmax(b >= REL_EXACT + k)) for k in range(1, REL_BUCKETS - REL_EXACT)]


BUCKET_THRESHOLDS = _bucket_thresholds()


HBM_SPEC = pl.BlockSpec(memory_space=pl.ANY)


class _Carry:
    def __init__(self, ins, outs, sems, start, finish, mid=None, aliases=None):
        self.ins, self.outs, self.sems = list(ins), list(outs), list(sems)
        self.start, self.finish, self.mid, self.aliases = start, finish, mid, dict(aliases or {})


def _pcall(body, *, name, grid, in_specs, out_specs, out_shape, scratch=(), carry=None, aliases=None, behind=None):
    params = pltpu.CompilerParams(dimension_semantics=("arbitrary",) * len(grid), vmem_limit_bytes=VMEM_LIMIT_V7X)
    if carry is None and behind is not None:
        n_in = len(in_specs)
        call = pl.pallas_call(lambda *refs: body(*refs[:n_in], *refs[n_in + 1:]), name=name, grid=grid,
                              in_specs=list(in_specs) + [pl.BlockSpec((8, 128), lambda *_: (0, 0))],
                              out_specs=out_specs, out_shape=out_shape, scratch_shapes=list(scratch),
                              compiler_params=params, input_output_aliases=aliases or {})
        return lambda *args: call(*args, behind)
    if carry is None:
        return pl.pallas_call(body, name=name, grid=grid, in_specs=in_specs, out_specs=out_specs,
                              out_shape=out_shape, scratch_shapes=list(scratch), compiler_params=params,
                              input_output_aliases=aliases or {})
    assert aliases is None and behind is None, name
    single = not isinstance(out_shape, (list, tuple))
    own_specs, own_shapes = ([out_specs], [out_shape]) if single else (list(out_specs), list(out_shape))
    n_in, n_out, n_scr = len(in_specs), len(own_shapes), len(scratch)
    n_cin, n_cout = len(carry.ins), len(carry.outs)
    steps = math.prod(grid)
    mid_step = max(steps - 1 - max(steps // 8, 1), 0)

    def carrying(*refs):
        ins, refs = refs[:n_in], refs[n_in:]
        cins, refs = refs[:n_cin], refs[n_cin:]
        outs, refs = refs[:n_out], refs[n_out:]
        couts, refs = refs[:n_cout], refs[n_cout:]
        scr, csems = refs[:n_scr], refs[n_scr:]
        step = 0
        for axis, size in enumerate(grid):
            step = step * size + pl.program_id(axis)

        @pl.when(step == 0)
        def _():
            carry.start(cins, couts, csems)

        body(*ins, *outs, *scr)
        if carry.mid is not None:
            @pl.when(step == mid_step)
            def _():
                carry.mid(cins, couts, csems)

        @pl.when(step == steps - 1)
        def _():
            carry.finish(cins, couts, csems)

    call = pl.pallas_call(carrying, name=name, grid=grid, in_specs=list(in_specs) + [HBM_SPEC] * n_cin,
                          out_specs=own_specs + [HBM_SPEC] * n_cout, out_shape=own_shapes + carry.outs,
                          scratch_shapes=list(scratch) + carry.sems, compiler_params=params,
                          input_output_aliases={n_in + i: n_out + o for i, o in carry.aliases.items()})

    def run(*args):
        res = call(*args, *carry.ins)
        return (res[0] if single else res[:n_out]), res[n_out:]

    return run


def _run_alone(carry, name, after=()):
    n_cin, n_cout, n_after = len(carry.ins), len(carry.outs), len(after)

    def body(*refs):
        cins, refs = refs[:n_cin], refs[n_cin + n_after:]
        couts, csems = refs[:n_cout], refs[n_cout:]
        carry.start(cins, couts, csems)
        if carry.mid is not None:
            carry.mid(cins, couts, csems)
        carry.finish(cins, couts, csems)

    return pl.pallas_call(body, name=name, in_specs=[HBM_SPEC] * (n_cin + n_after), out_specs=[HBM_SPEC] * n_cout,
                          out_shape=carry.outs, scratch_shapes=carry.sems,
                          input_output_aliases=carry.aliases)(*carry.ins, *after)


def _dot(a, b):
    return jnp.dot(a, b, preferred_element_type=F32)


def _dot_nt(a, b):
    return lax.dot_general(a, b, (((1,), (1,)), ((), ())), preferred_element_type=F32)


def _dot_tn(a, b):
    return lax.dot_general(a, b, (((0,), (0,)), ((), ())), preferred_element_type=F32)


def _sds(shape, dtype):
    return jax.ShapeDtypeStruct(tuple(shape), dtype)


ROW_CHUNK = 256


def _row_chunks(tm):
    return [slice(r, min(r + ROW_CHUNK, tm)) for r in range(0, tm, ROW_CHUNK)]


def _carried(call, args, carry):
    return call(*args) if carry is not None else (call(*args), ())


def _rmsnorm(x, g, name, carry=None):
    m, d = x.shape
    tm = min(512, m)

    def body(x_ref, g_ref, h_ref):
        xv = x_ref[...]
        r = lax.rsqrt(jnp.mean(xv * xv, axis=-1, keepdims=True) + EPS)
        h_ref[...] = (xv * r * g_ref[...]).astype(BF16)

    call = _pcall(body, name=name, grid=(m // tm,), carry=carry,
                  in_specs=[pl.BlockSpec((tm, d), lambda i: (i, 0)), pl.BlockSpec((1, d), lambda i: (0, 0))],
                  out_specs=pl.BlockSpec((tm, d), lambda i: (i, 0)), out_shape=_sds((m, d), BF16))
    return _carried(call, (x, g), carry)


def _norm_mm(x, g, w, name):
    m, d = x.shape
    nj, _, n = w.shape

    def body(x_ref, g_ref, w_ref, h_ref, o_ref):
        xv = x_ref[...]
        r = lax.rsqrt(jnp.mean(xv * xv, axis=-1, keepdims=True) + EPS)
        h = (xv * r * g_ref[...]).astype(BF16)
        h_ref[...] = h
        for j in range(nj):
            o_ref[j] = _dot(h, w_ref[j]).astype(BF16)

    return _pcall(body, name=name, grid=(1,),
                  in_specs=[pl.BlockSpec((m, d), lambda i: (0, 0)), pl.BlockSpec((1, d), lambda i: (0, 0)),
                            pl.BlockSpec(w.shape, lambda i: (0, 0, 0))],
                  out_specs=[pl.BlockSpec((m, d), lambda i: (0, 0)), pl.BlockSpec((nj, m, n), lambda i: (0, 0, 0))],
                  out_shape=[_sds((m, d), BF16), _sds((nj, m, n), BF16)])(x, g, w)


def _norm_mm_bwd(dy, h, w, x, name):
    nj, m, n = dy.shape
    d = x.shape[1]

    def body(dy_ref, h_ref, w_ref, x_ref, dw_ref, dg_ref):
        dh = None
        for j in range(nj):
            dyb = dy_ref[j].astype(BF16)
            dw_ref[j] = _dot_tn(h_ref[...], dyb).astype(BF16)
            part = _dot_nt(dyb, w_ref[j])
            dh = part if dh is None else dh + part
        xv = x_ref[...]
        xh = xv * lax.rsqrt(jnp.mean(xv * xv, axis=-1, keepdims=True) + EPS)
        dg_ref[...] = jnp.sum(dh * xh, axis=0, keepdims=True)

    return _pcall(body, name=name, grid=(1,),
                  in_specs=[pl.BlockSpec((nj, m, n), lambda i: (0, 0, 0)), pl.BlockSpec((m, d), lambda i: (0, 0)),
                            pl.BlockSpec((nj, d, n), lambda i: (0, 0, 0)), pl.BlockSpec((m, d), lambda i: (0, 0))],
                  out_specs=[pl.BlockSpec((nj, d, n), lambda i: (0, 0, 0)), pl.BlockSpec((1, d), lambda i: (0, 0))],
                  out_shape=[_sds((nj, d, n), BF16), _sds((1, d), F32)])(dy, h, w, x)


def _load_once(src_hbm, dst_vmem, sem):
    @pl.when(pl.program_id(0) == 0)
    def _():
        load = pltpu.make_async_copy(src_hbm, dst_vmem, sem)
        load.start()
        load.wait()


def _resident(w):
    return [pltpu.VMEM(w.shape, w.dtype), pltpu.SemaphoreType.DMA(())]


def _ffn_up(h, w4, name, tm=512, carry=None):
    s, d = h.shape
    tm = min(tm, s)

    def body(h_ref, w_hbm, gu_ref, a_ref, w_ref, w_sem):
        _load_once(w_hbm, w_ref, w_sem)
        for p in range(4):
            for rows in _row_chunks(tm):
                hv = h_ref[rows, :]
                g = _dot_nt(hv, w_ref[0, p])
                u = _dot_nt(hv, w_ref[1, p])
                gu_ref[0, p, rows, :] = g.astype(BF16)
                gu_ref[1, p, rows, :] = u.astype(BF16)
                a_ref[p, rows, :] = (g * jax.nn.sigmoid(g) * u).astype(BF16)

    call = _pcall(body, name=name, grid=(s // tm,),
                  in_specs=[pl.BlockSpec((tm, d), lambda i: (i, 0)), HBM_SPEC],
                  out_specs=[pl.BlockSpec((2, 4, tm, FS), lambda i: (0, 0, i, 0)),
                             pl.BlockSpec((4, tm, FS), lambda i: (0, i, 0))],
                  out_shape=[_sds((2, 4, s, FS), BF16), _sds((4, s, FS), BF16)], scratch=_resident(w4), carry=carry)
    return _carried(call, (h, w4), carry)


N_SEG = 5
IN_PROJ_WEIGHTS = [pltpu.VMEM((NQ, D), BF16), pltpu.VMEM((NKV, D), BF16), pltpu.VMEM((N_SEG, D, D), BF16),
                   pltpu.SemaphoreType.DMA((2 + N_SEG,))]


def _load_in_proj(w_hbm, wq_ref, wkv_ref, wa_ref, sems):
    @pl.when(pl.program_id(0) == 0)
    def _():
        loads = [pltpu.make_async_copy(w_hbm.at[pl.ds(0, NQ)], wq_ref, sems.at[0]),
                 pltpu.make_async_copy(w_hbm.at[pl.ds(NQ, NKV)], wkv_ref, sems.at[1])]
        loads += [pltpu.make_async_copy(w_hbm.at[pl.ds(NQ + NKV + D * j, D)], wa_ref.at[j], sems.at[2 + j])
                  for j in range(N_SEG)]
        for load in loads:
            load.start()
        for load in loads:
            load.wait()


def _in_proj(h, w_in_t, name, tm=512, carry=None):
    s, d = h.shape
    tm = min(tm, s)

    def body(h_ref, w_hbm, pa_ref, q_ref, kv_ref, wq_ref, wkv_ref, wa_ref, sems):
        _load_in_proj(w_hbm, wq_ref, wkv_ref, wa_ref, sems)
        hv = h_ref[...]
        q_ref[...] = _dot_nt(hv, wq_ref[...]).astype(BF16)
        kv_ref[...] = _dot_nt(hv, wkv_ref[...]).astype(BF16)
        for j in range(N_SEG):
            pa_ref[j] = _dot_nt(hv, wa_ref[j]).astype(BF16)

    call = _pcall(body, name=name, grid=(s // tm,), carry=carry,
                  in_specs=[pl.BlockSpec((tm, d), lambda i: (i, 0)), HBM_SPEC],
                  out_specs=[pl.BlockSpec((N_SEG, tm, d), lambda i: (0, i, 0)),
                             pl.BlockSpec((tm, NQ), lambda i: (i, 0)), pl.BlockSpec((tm, NKV), lambda i: (i, 0))],
                  out_shape=[_sds((N_SEG, s, d), BF16), _sds((s, NQ), BF16), _sds((s, NKV), BF16)],
                  scratch=IN_PROJ_WEIGHTS)
    return _carried(call, (h, w_in_t), carry)


def _mm_res_norm(a, w, xres, gain, scale, name, tm=512, carry=None):
    npart, s, kp = a.shape
    tm = min(tm, s)

    def body(a_ref, w_ref, x_ref, g_ref, xo_ref, h_ref):
        for rows in _row_chunks(tm):
            acc = _dot(a_ref[0, rows, :], w_ref[0])
            for p in range(1, npart):
                acc = acc + _dot(a_ref[p, rows, :], w_ref[p])
            xn = x_ref[rows, :] + scale * acc
            xo_ref[rows, :] = xn
            r = lax.rsqrt(jnp.mean(xn * xn, axis=-1, keepdims=True) + EPS)
            h_ref[rows, :] = (xn * r * g_ref[...]).astype(BF16)

    call = _pcall(body, name=name, grid=(s // tm,),
                  in_specs=[pl.BlockSpec((npart, tm, kp), lambda i: (0, i, 0)),
                            pl.BlockSpec((npart, kp, D), lambda i: (0, 0, 0)),
                            pl.BlockSpec((tm, D), lambda i: (i, 0)),
                            pl.BlockSpec((1, D), lambda i: (0, 0))],
                  out_specs=[pl.BlockSpec((tm, D), lambda i: (i, 0)), pl.BlockSpec((tm, D), lambda i: (i, 0))],
                  out_shape=[_sds((s, D), F32), _sds((s, D), BF16)], carry=carry)
    return _carried(call, (a, w, xres, gain), carry)


def _ffn_down_loss(a, w, xres, gain, target, name, tm=512):
    npart, s, kp = a.shape
    tm = min(tm, s)

    def body(a_ref, w_ref, x_ref, g_ref, t_ref, dx_ref, dxb_ref, loss_ref, dg_ref):
        @pl.when(pl.program_id(0) == 0)
        def _():
            loss_ref[...] = jnp.zeros_like(loss_ref)
            dg_ref[...] = jnp.zeros_like(dg_ref)

        for rows in _row_chunks(tm):
            acc = _dot(a_ref[0, rows, :], w_ref[0])
            for p in range(1, npart):
                acc = acc + _dot(a_ref[p, rows, :], w_ref[p])
            xn = x_ref[rows, :] + 0.5 * acc
            r = lax.rsqrt(jnp.mean(xn * xn, axis=-1, keepdims=True) + EPS)
            xh = xn * r
            gv = g_ref[...]
            err = xh * gv - t_ref[rows, :]
            part = 0.5 * jnp.sum(jnp.mean(err * err, axis=-1, keepdims=True), axis=0, keepdims=True)
            dy = err * (1.0 / D)
            dyg = dy * gv
            dxn = r * (dyg - xh * jnp.mean(dyg * xh, axis=-1, keepdims=True))
            dx_ref[rows, :] = dxn
            dxb_ref[rows, :] = dxn.astype(BF16)
            loss_ref[...] += jnp.broadcast_to(part, loss_ref.shape)
            dg_ref[...] += jnp.sum(dy * xh, axis=0, keepdims=True)

    return _pcall(body, name=name, grid=(s // tm,),
                  in_specs=[pl.BlockSpec((npart, tm, kp), lambda i: (0, i, 0)),
                            pl.BlockSpec((npart, kp, D), lambda i: (0, 0, 0)),
                            pl.BlockSpec((tm, D), lambda i: (i, 0)),
                            pl.BlockSpec((1, D), lambda i: (0, 0)),
                            pl.BlockSpec((tm, D), lambda i: (i, 0))],
                  out_specs=[pl.BlockSpec((tm, D), lambda i: (i, 0)), pl.BlockSpec((tm, D), lambda i: (i, 0)),
                             pl.BlockSpec((8, 128), lambda i: (0, 0)), pl.BlockSpec((1, D), lambda i: (0, 0))],
                  out_shape=[_sds((s, D), F32), _sds((s, D), BF16), _sds((8, 128), F32), _sds((1, D), F32)],
                  )(a, w, xres, gain, target)


def _window_tiles():
    i = lax.broadcasted_iota(jnp.int32, (BLK, BLK), 0)
    j = lax.broadcasted_iota(jnp.int32, (BLK, BLK), 1)
    rel = (i - j) & (BLK - 1)
    large = jnp.full_like(rel, REL_EXACT)
    for t in BUCKET_THRESHOLDS:
        large = large + (rel >= t).astype(jnp.int32)
    return j <= i, jnp.where(rel < REL_EXACT, rel, large)


def _bias_build(rel_bias, name):
    def body(rb_ref, o_ref):
        _, bucket = _window_tiles()

        def per_head(h, carry):
            acc = jnp.zeros((BLK, BLK), F32)
            for b in range(REL_BUCKETS):
                acc = jnp.where(bucket == b, rb_ref[b, h], acc)
            o_ref[h] = acc
            return carry

        lax.fori_loop(0, N_HEADS, per_head, 0)

    return _pcall(body, name=name, grid=(1,),
                  in_specs=[pl.BlockSpec(memory_space=pltpu.SMEM)],
                  out_specs=pl.BlockSpec((N_HEADS, BLK, BLK), lambda i: (0, 0, 0)),
                  out_shape=_sds((N_HEADS, BLK, BLK), F32))(rel_bias)


def _bias_bwd(dbias, name):
    def body(db_ref, o_ref):
        _, bucket = _window_tiles()
        lane = lax.broadcasted_iota(jnp.int32, (N_HEADS, 128), 1)

        def per_bucket(b, out):
            mb = (bucket == b).astype(F32)
            per_col = jnp.sum(db_ref[...] * mb[None, :, :], axis=1)
            return jnp.where(lane == b, jnp.sum(per_col, axis=1, keepdims=True), out)

        o_ref[...] = lax.fori_loop(0, REL_BUCKETS, per_bucket, jnp.zeros((N_HEADS, 128), F32))

    return _pcall(body, name=name, grid=(1,),
                  in_specs=[pl.BlockSpec((N_HEADS, BLK, BLK), lambda i: (0, 0, 0))],
                  out_specs=pl.BlockSpec((N_HEADS, 128), lambda i: (0, 0)),
                  out_shape=_sds((N_HEADS, 128), F32))(dbias)


PAIR = 2 * HEAD
GROUP = N_HEADS // N_KV
SWA_SCALE = HEAD ** -0.5


def _window_masks(n):
    i = lax.broadcasted_iota(jnp.int32, (GROUP * BLK, BLK), 0) & (BLK - 1)
    j = lax.broadcasted_iota(jnp.int32, (GROUP * BLK, BLK), 1)
    return j <= i, jnp.logical_and(n == 0, j > i), j < HEAD


def _kv_twice(ref, base, g, low):
    slab = ref[:, base + PAIR * (g // 2): base + PAIR * (g // 2 + 1)]
    swapped = pltpu.roll(slab, HEAD, 1)
    return jnp.where(low, slab, swapped) if g % 2 == 0 else jnp.where(low, swapped, slab)


def _stack_heads(ref, g, low):
    parts = []
    for r in range(2):
        slab = ref[:, PAIR * (2 * g + r): PAIR * (2 * g + r + 1)]
        zero = jnp.zeros_like(slab)
        parts += [jnp.where(low, slab, zero), jnp.where(low, zero, slab)]
    return jnp.concatenate(parts, axis=0)


def _unstack_heads(t, low):
    return [jnp.where(low, t[2 * r * BLK:(2 * r + 1) * BLK], t[(2 * r + 1) * BLK:(2 * r + 2) * BLK])
            for r in range(2)]


def _head_rows(t, k):
    return t[k * BLK:(k + 1) * BLK]


def _per_head_column(values):
    head = lax.broadcasted_iota(jnp.int32, (GROUP * BLK, 1), 0) // BLK
    col = jnp.full((GROUP * BLK, 1), values[0], F32)
    for k in range(1, GROUP):
        col = jnp.where(head == k, values[k], col)
    return col


def _window_logits(q4, kc, kp, bias4, own, absent):
    sc = jnp.where(own, _dot_nt(q4, kc), _dot_nt(q4, kp)) * SWA_SCALE + bias4
    return jnp.where(absent, NEG, sc)


def _split_window(t, own):
    zero = jnp.zeros_like(t)
    return jnp.where(own, t, zero), jnp.where(own, zero, t)


def _swa_fwd(q, kv, bias, sinks, name, carry=None):
    s = q.shape[0]
    nb = s // BLK
    kvw = 2 * N_KV * HEAD

    def body(q_ref, kc_ref, kp_ref, b_ref, sk_ref, o_ref, lse_ref):
        own, absent, low4 = _window_masks(pl.program_id(0))
        low = low4[:BLK]
        lane = lax.broadcasted_iota(jnp.int32, (BLK, 128), 1)
        lse_t = jnp.zeros((BLK, 128), F32)
        for g in range(N_KV):
            q4 = _stack_heads(q_ref, g, low)
            kc, kp = _kv_twice(kc_ref, 0, g, low), _kv_twice(kp_ref, 0, g, low)
            vc, vp = _kv_twice(kc_ref, N_KV * HEAD, g, low), _kv_twice(kp_ref, N_KV * HEAD, g, low)
            bias4 = b_ref[GROUP * g:GROUP * (g + 1)].reshape(GROUP * BLK, BLK)
            sc = _window_logits(q4, kc, kp, bias4, own, absent)
            sk = _per_head_column([sk_ref[0, GROUP * g + k] for k in range(GROUP)])
            m = jnp.maximum(jnp.max(sc, axis=1, keepdims=True), sk)
            p = jnp.exp(sc - m)
            l = jnp.sum(p, axis=1, keepdims=True) + jnp.exp(sk - m)
            p_own, p_prev = _split_window(p.astype(BF16), own)
            out = (_dot(p_own, vc) + _dot(p_prev, vp)) * (1.0 / l)
            for r, slab in enumerate(_unstack_heads(out, low)):
                o_ref[:, PAIR * (2 * g + r): PAIR * (2 * g + r + 1)] = slab.astype(BF16)
            lse4 = m + jnp.log(l)
            for k in range(GROUP):
                lse_t = jnp.where(lane == GROUP * g + k, _head_rows(lse4, k), lse_t)
        lse_ref[...] = lse_t

    call = _pcall(body, name=name, grid=(nb,),
                  in_specs=[pl.BlockSpec((BLK, D), lambda n: (n, 0)),
                            pl.BlockSpec((BLK, kvw), lambda n: (n, 0)),
                            pl.BlockSpec((BLK, kvw), lambda n: (jnp.maximum(n - 1, 0), 0)),
                            pl.BlockSpec((N_HEADS, BLK, BLK), lambda n: (0, 0, 0)),
                            pl.BlockSpec(memory_space=pltpu.SMEM)],
                  out_specs=[pl.BlockSpec((BLK, D), lambda n: (n, 0)), pl.BlockSpec((BLK, 128), lambda n: (n, 0))],
                  out_shape=[_sds((s, D), BF16), _sds((s, 128), F32)], carry=carry)
    return _carried(call, (q, kv, kv, bias, sinks), carry)


def _fold_halves(t, g, low):
    folded = jnp.where(low, t, 0.0) + pltpu.roll(jnp.where(low, 0.0, t), HEAD, 1)
    return folded if g % 2 == 0 else pltpu.roll(folded, HEAD, 1)


def _swa_bwd(q, kv, attn, dattn, lse, bias, sinks, name, carry=None):
    s = q.shape[0]
    nb = s // BLK
    kvw = 2 * N_KV * HEAD
    voff = N_KV * HEAD

    def body(q_ref, kc_ref, kp_ref, o_ref, do_ref, lse_ref, b_ref, skrow_ref, dq_ref, dkv_ref, dbias_ref, dsk_ref,
             dq_hold, kv_hold, dq_new, kv_prev, kv_cur):
        n = pl.program_id(0)

        @pl.when(n == 0)
        def _():
            dbias_ref[...] = jnp.zeros_like(dbias_ref)
            dsk_ref[...] = jnp.zeros_like(dsk_ref)
            dq_hold[...] = jnp.zeros_like(dq_hold)
            kv_hold[...] = jnp.zeros_like(kv_hold)

        @pl.when(n < nb)
        def _():
            own, absent, low4 = _window_masks(n)
            low = low4[:BLK]
            lane = lax.broadcasted_iota(jnp.int32, (BLK, 128), 1)
            delta_t = jnp.zeros((BLK, 128), F32)
            ones = jnp.ones((PAIR, 128), BF16)
            for pair_of_kv in range(N_KV // 2):
                slab_grads = [jnp.zeros((BLK, PAIR), F32) for _ in range(4)]
                for g in (2 * pair_of_kv, 2 * pair_of_kv + 1):
                    q4, do4 = _stack_heads(q_ref, g, low), _stack_heads(do_ref, g, low)
                    kc, kp = _kv_twice(kc_ref, 0, g, low), _kv_twice(kp_ref, 0, g, low)
                    vc, vp = _kv_twice(kc_ref, voff, g, low), _kv_twice(kp_ref, voff, g, low)
                    o_slabs = [o_ref[:, PAIR * (2 * g + r): PAIR * (2 * g + r + 1)] for r in range(2)]
                    o4 = jnp.concatenate([o_slabs[0], o_slabs[0], o_slabs[1], o_slabs[1]], axis=0)
                    delta = _dot(do4 * o4, ones)
                    heads = range(GROUP * g, GROUP * (g + 1))
                    lse4 = jnp.concatenate([lse_ref[:, h:h + 1] for h in heads], axis=0)
                    bias4 = b_ref[GROUP * g:GROUP * (g + 1)].reshape(GROUP * BLK, BLK)
                    p = jnp.exp(_window_logits(q4, kc, kp, bias4, own, absent) - lse4)
                    dp = jnp.where(own, _dot_nt(do4, vc), _dot_nt(do4, vp))
                    ds = p * (dp - delta)
                    dbias_ref[GROUP * g:GROUP * (g + 1)] += ds.reshape(GROUP, BLK, BLK)
                    for k, h in enumerate(heads):
                        delta_t = jnp.where(lane == h, _head_rows(delta, k), delta_t)
                    ds_own, ds_prev = _split_window((ds * SWA_SCALE).astype(BF16), own)
                    p_own, p_prev = _split_window(p.astype(BF16), own)
                    dq4 = _dot(ds_own, kc) + _dot(ds_prev, kp)
                    for r, slab in enumerate(_unstack_heads(dq4, low)):
                        dq_new[:, PAIR * (2 * g + r): PAIR * (2 * g + r + 1)] = slab
                    grads = [_dot_tn(ds_own, q4), _dot_tn(ds_prev, q4), _dot_tn(p_own, do4), _dot_tn(p_prev, do4)]
                    slab_grads = [t + _fold_halves(dk, g, low) for t, dk in zip(slab_grads, grads)]
                ks = slice(PAIR * pair_of_kv, PAIR * (pair_of_kv + 1))
                vs = slice(voff + PAIR * pair_of_kv, voff + PAIR * (pair_of_kv + 1))
                kv_cur[:, ks], kv_prev[:, ks], kv_cur[:, vs], kv_prev[:, vs] = slab_grads
            dsk_ref[...] -= jnp.sum(jnp.exp(skrow_ref[...] - lse_ref[...]) * delta_t, axis=0, keepdims=True)

        @pl.when(n == nb)
        def _():
            kv_prev[...] = jnp.zeros_like(kv_prev)

        dq_ref[...] = dq_hold[...].astype(BF16)
        dkv_ref[...] = (kv_hold[...] + kv_prev[...]).astype(BF16)

        @pl.when(n < nb)
        def _():
            dq_hold[...] = dq_new[...]
            kv_hold[...] = kv_cur[...]

    def cur(n):
        return jnp.minimum(n, nb - 1)

    call = _pcall(body, name=name, grid=(nb + 1,), carry=carry,
                  in_specs=[pl.BlockSpec((BLK, D), lambda n: (cur(n), 0)),
                            pl.BlockSpec((BLK, kvw), lambda n: (cur(n), 0)),
                            pl.BlockSpec((BLK, kvw), lambda n: (jnp.maximum(cur(n) - 1, 0), 0)),
                            pl.BlockSpec((BLK, D), lambda n: (cur(n), 0)),
                            pl.BlockSpec((BLK, D), lambda n: (cur(n), 0)),
                            pl.BlockSpec((BLK, 128), lambda n: (cur(n), 0)),
                            pl.BlockSpec((N_HEADS, BLK, BLK), lambda n: (0, 0, 0)),
                            pl.BlockSpec((1, 128), lambda n: (0, 0))],
                  out_specs=[pl.BlockSpec((BLK, D), lambda n: (jnp.maximum(n - 1, 0), 0)),
                             pl.BlockSpec((BLK, kvw), lambda n: (jnp.maximum(n - 1, 0), 0)),
                             pl.BlockSpec((N_HEADS, BLK, BLK), lambda n: (0, 0, 0)),
                             pl.BlockSpec((1, 128), lambda n: (0, 0))],
                  out_shape=[_sds((s, D), BF16), _sds((s, kvw), BF16), _sds((N_HEADS, BLK, BLK), F32),
                             _sds((1, 128), F32)],
                  scratch=[pltpu.VMEM((BLK, D), F32), pltpu.VMEM((BLK, kvw), F32), pltpu.VMEM((BLK, D), F32),
                           pltpu.VMEM((BLK, kvw), F32), pltpu.VMEM((BLK, kvw), F32)])
    sink_row = jnp.pad(sinks, ((0, 0), (0, 128 - N_HEADS)))
    return _carried(call, (q, kv, kv, attn, dattn, lse, bias, sink_row), carry)


HALO = 16
CW = D


def _conv_taps(cu, halo_cu, first_tile):
    row = lax.broadcasted_iota(jnp.int32, cu.shape, 0)
    halo_cu = jnp.where(first_tile, 0.0, halo_cu)
    c1 = jnp.where(row == 0, halo_cu[HALO - 1:HALO], pltpu.roll(cu, 1, 0))
    c2 = jnp.where(row == 0, halo_cu[HALO - 2:HALO - 1],
                   jnp.where(row == 1, halo_cu[HALO - 1:HALO], pltpu.roll(cu, 2, 0)))
    return c1, c2


def _conv_merge_fwd(pa, attn, convw, name, ts=256, carry=None):
    _, s, _ = pa.shape
    ts = min(ts, s)
    hb = ts // HALO

    def body(pa_ref, hp_ref, at_ref, w_ref, o_ref):
        i = pl.program_id(1)
        cu = pa_ref[0].astype(F32) * pa_ref[2].astype(F32)
        c1, c2 = _conv_taps(cu, hp_ref[0].astype(F32) * hp_ref[2].astype(F32), i == 0)
        w = w_ref[...]
        c3 = w[0:1] * c2 + w[1:2] * c1 + w[2:3] * cu
        conv = pa_ref[1].astype(F32) * c3
        o_ref[...] = (jax.nn.sigmoid(pa_ref[3].astype(F32)) * at_ref[...].astype(F32)
                      + jax.nn.sigmoid(pa_ref[4].astype(F32)) * conv).astype(BF16)

    call = _pcall(body, name=name, grid=(D // CW, s // ts), carry=carry,
                  in_specs=[pl.BlockSpec((5, ts, CW), lambda c, i: (0, i, c)),
                            pl.BlockSpec((5, HALO, CW), lambda c, i: (0, jnp.maximum(i * hb - 1, 0), c)),
                            pl.BlockSpec((ts, CW), lambda c, i: (i, c)),
                            pl.BlockSpec((8, CW), lambda c, i: (0, c))],
                  out_specs=pl.BlockSpec((ts, CW), lambda c, i: (i, c)),
                  out_shape=_sds((s, D), BF16))
    return _carried(call, (pa, pa, attn, convw), carry)


def _conv_merge_bwd(dmerged, pa, attn, convw, name, ts=256, carry=None):
    _, s, _ = pa.shape
    ts = min(ts, s)
    hb = ts // HALO
    last_hb = s // HALO - 1

    def body(dm_ref, pa_ref, at_ref, w_ref, hp_ref, hn_ref, dmn_ref, dat_ref, dpa_ref, dw_ref):
        i = pl.program_id(1)
        last = i == pl.num_programs(1) - 1
        dm = dm_ref[...].astype(F32)
        cp, bp, u = pa_ref[0].astype(F32), pa_ref[1].astype(F32), pa_ref[2].astype(F32)
        sa = jax.nn.sigmoid(pa_ref[3].astype(F32))
        sc = jax.nn.sigmoid(pa_ref[4].astype(F32))
        at = at_ref[...].astype(F32)
        cu = cp * u
        c1, c2 = _conv_taps(cu, hp_ref[0].astype(F32) * hp_ref[2].astype(F32), i == 0)
        w = w_ref[...]
        c3 = w[0:1] * c2 + w[1:2] * c1 + w[2:3] * cu
        dconv = dm * sc
        dc3 = dconv * bp
        nxt = dmn_ref[...].astype(F32) * jax.nn.sigmoid(hn_ref[4].astype(F32)) * hn_ref[1].astype(F32)
        nxt = jnp.where(last, 0.0, nxt)
        row = lax.broadcasted_iota(jnp.int32, dc3.shape, 0)
        d1 = jnp.where(row == ts - 1, nxt[0:1], pltpu.roll(dc3, ts - 1, 0))
        d2 = jnp.where(row == ts - 2, nxt[0:1], jnp.where(row == ts - 1, nxt[1:2], pltpu.roll(dc3, ts - 2, 0)))
        dcu = w[2:3] * dc3 + w[1:2] * d1 + w[0:1] * d2
        dat_ref[...] = (dm * sa).astype(BF16)
        dpa_ref[0] = (dcu * u).astype(BF16)
        dpa_ref[1] = (dconv * c3).astype(BF16)
        dpa_ref[2] = (dcu * cp).astype(BF16)
        dpa_ref[3] = (dm * at * sa * (1.0 - sa)).astype(BF16)
        dpa_ref[4] = (dm * bp * c3 * sc * (1.0 - sc)).astype(BF16)

        @pl.when(i == 0)
        def _():
            dw_ref[...] = jnp.zeros_like(dw_ref)

        dw_ref[0:1, :] += jnp.sum(dc3 * c2, axis=0, keepdims=True)
        dw_ref[1:2, :] += jnp.sum(dc3 * c1, axis=0, keepdims=True)
        dw_ref[2:3, :] += jnp.sum(dc3 * cu, axis=0, keepdims=True)

    call = _pcall(body, name=name, grid=(D // CW, s // ts), carry=carry,
                  in_specs=[pl.BlockSpec((ts, CW), lambda c, i: (i, c)),
                            pl.BlockSpec((5, ts, CW), lambda c, i: (0, i, c)),
                            pl.BlockSpec((ts, CW), lambda c, i: (i, c)),
                            pl.BlockSpec((8, CW), lambda c, i: (0, c)),
                            pl.BlockSpec((5, HALO, CW), lambda c, i: (0, jnp.maximum(i * hb - 1, 0), c)),
                            pl.BlockSpec((5, HALO, CW), lambda c, i: (0, jnp.minimum((i + 1) * hb, last_hb), c)),
                            pl.BlockSpec((HALO, CW), lambda c, i: (jnp.minimum((i + 1) * hb, last_hb), c))],
                  out_specs=[pl.BlockSpec((ts, CW), lambda c, i: (i, c)),
                             pl.BlockSpec((5, ts, CW), lambda c, i: (0, i, c)),
                             pl.BlockSpec((8, CW), lambda c, i: (0, c))],
                  out_shape=[_sds((s, D), BF16), _sds((5, s, D), BF16), _sds((8, D), F32)])
    return _carried(call, (dmerged, pa, attn, convw, pa, pa, dmerged), carry)


def _xattn_fwd(hx, wq, kv, name, tq=1024):
    s, _ = hx.shape
    nm = kv.shape[1]
    tq = min(tq, s)

    def body(h_ref, wq_ref, kv_ref, q_ref, o_ref, lse_ref):
        q_ref[...] = _dot(h_ref[...], wq_ref[...]).astype(BF16)
        lane = lax.broadcasted_iota(jnp.int32, (tq, 128), 1)
        lse_t = jnp.zeros((tq, 128), F32)
        for h in range(XH):
            hs = slice(XHD * h, XHD * (h + 1))
            sc = _dot_nt(q_ref[:, hs], kv_ref[h]) * (XHD ** -0.5)
            m = jnp.max(sc, axis=1, keepdims=True)
            p = jnp.exp(sc - m)
            l = jnp.sum(p, axis=1, keepdims=True)
            o_ref[:, hs] = (_dot(p.astype(BF16), kv_ref[XH + h]) * (1.0 / l)).astype(BF16)
            lse_t = jnp.where(lane == h, m + jnp.log(l), lse_t)
        lse_ref[...] = lse_t

    return _pcall(body, name=name, grid=(s // tq,),
                  in_specs=[pl.BlockSpec((tq, D), lambda i: (i, 0)), pl.BlockSpec((D, D), lambda i: (0, 0)),
                            pl.BlockSpec((2 * XH, nm, XHD), lambda i: (0, 0, 0))],
                  out_specs=[pl.BlockSpec((tq, D), lambda i: (i, 0)), pl.BlockSpec((tq, D), lambda i: (i, 0)),
                             pl.BlockSpec((tq, 128), lambda i: (i, 0))],
                  out_shape=[_sds((s, D), BF16), _sds((s, D), BF16), _sds((s, 128), F32)])(hx, wq, kv)


def _xattn_bwd(q, kv, o, dy, wo, lse, name, tq=1024, carry=None):
    s, _ = q.shape
    nm = kv.shape[1]
    tq = min(tq, s)

    def body(q_ref, kv_ref, o_ref, dy_ref, wo_ref, lse_ref, dq_ref, dkv_ref, do_ref):
        @pl.when(pl.program_id(0) == 0)
        def _():
            dkv_ref[...] = jnp.zeros_like(dkv_ref)

        do_ref[...] = _dot_nt(dy_ref[...], wo_ref[...]).astype(BF16)
        for h in range(XH):
            hs = slice(XHD * h, XHD * (h + 1))
            qh, kh, vh, dob = q_ref[:, hs], kv_ref[h], kv_ref[XH + h], do_ref[:, hs]
            p = jnp.exp(_dot_nt(qh, kh) * (XHD ** -0.5) - lse_ref[:, h:h + 1])
            dp = _dot_nt(dob, vh)
            delta = jnp.sum(dob.astype(F32) * o_ref[:, hs].astype(F32), axis=1, keepdims=True)
            dsb = (p * (dp - delta) * (XHD ** -0.5)).astype(BF16)
            dq_ref[:, hs] = _dot(dsb, kh).astype(BF16)
            dkv_ref[h] += _dot_tn(dsb, qh)
            dkv_ref[XH + h] += _dot_tn(p.astype(BF16), dob)

    call = _pcall(body, name=name, grid=(s // tq,), carry=carry,
                  in_specs=[pl.BlockSpec((tq, D), lambda i: (i, 0)), pl.BlockSpec((2 * XH, nm, XHD), lambda i: (0, 0, 0)),
                            pl.BlockSpec((tq, D), lambda i: (i, 0)), pl.BlockSpec((tq, D), lambda i: (i, 0)),
                            pl.BlockSpec((D, D), lambda i: (0, 0)), pl.BlockSpec((tq, 128), lambda i: (i, 0))],
                  out_specs=[pl.BlockSpec((tq, D), lambda i: (i, 0)), pl.BlockSpec((2 * XH, nm, XHD), lambda i: (0, 0, 0))],
                  out_shape=[_sds((s, D), BF16), _sds((2 * XH, nm, XHD), F32)], scratch=[pltpu.VMEM((tq, D), BF16)])
    return _carried(call, (q, kv, o, dy, wo, lse), carry)


def _ffn_down_bwd(dxb, wd4, gu4, name, tm=512, carry=None, behind=None):
    s, _ = dxb.shape
    tm = min(tm, s)

    steps = s // tm

    def body(dx_ref, w_hbm, gu_hbm, o_ref, w_ref, w_sem, gu_ring, gu_sems):
        i = pl.program_id(0)

        def fetch(step):
            slot = step % 3
            return pltpu.make_async_copy(gu_hbm.at[:, :, pl.ds(step * tm, tm), :], gu_ring.at[slot], gu_sems.at[slot])

        @pl.when(i == 0)
        def _():
            fetch(0).start()
            if steps > 1:
                fetch(1).start()

        @pl.when(i + 2 < steps)
        def _():
            fetch(i + 2).start()

        _load_once(w_hbm, w_ref, w_sem)
        fetch(i).wait()
        gu_ref = gu_ring.at[i % 3]
        for p in range(4):
            for rows in _row_chunks(tm):
                da = _dot_nt(dx_ref[rows, :], w_ref[p])
                g = gu_ref[0, p, rows, :].astype(F32)
                u = gu_ref[1, p, rows, :].astype(F32)
                sg = jax.nn.sigmoid(g)
                t = da * sg
                o_ref[0, p, rows, :] = (t * u * (1.0 + g - g * sg)).astype(BF16)
                o_ref[1, p, rows, :] = (t * g).astype(BF16)

    block = pl.BlockSpec((2, 4, tm, FS), lambda i: (0, 0, i, 0))
    call = _pcall(body, name=name, grid=(s // tm,), carry=carry, behind=behind,
                  in_specs=[pl.BlockSpec((tm, D), lambda i: (i, 0)), HBM_SPEC, HBM_SPEC],
                  out_specs=block, out_shape=_sds((2, 4, s, FS), BF16),
                  scratch=_resident(wd4) + [pltpu.VMEM((3, 2, 4, tm, FS), BF16), pltpu.SemaphoreType.DMA((3,))])
    return _carried(call, (dxb, wd4, gu4), carry)


def _mm_tn(a, b, name, scale=1.0, carry=None):
    pa_n, s, m = a.shape
    pb_n, _, n = b.shape
    po = max(pa_n, pb_n)
    tk = 1024
    if po == 1 and s > tk and s % tk == 0:
        def body_k(a_ref, b_ref, o_ref, acc_ref):
            k = pl.program_id(0)
            part = _dot_tn(a_ref[...], b_ref[...])

            @pl.when(k == 0)
            def _():
                acc_ref[...] = part

            @pl.when(k > 0)
            def _():
                acc_ref[...] += part

            @pl.when(k == s // tk - 1)
            def _():
                o_ref[...] = (scale * acc_ref[...]).astype(BF16)

        call = _pcall(body_k, name=name, grid=(s // tk,), carry=carry,
                      in_specs=[pl.BlockSpec((None, tk, m), lambda k: (0, k, 0)),
                                pl.BlockSpec((None, tk, n), lambda k: (0, k, 0))],
                      out_specs=pl.BlockSpec((None, m, n), lambda k: (0, 0, 0)),
                      out_shape=_sds((1, m, n), BF16), scratch=[pltpu.VMEM((m, n), F32)])
        return _carried(call, (a, b), carry)
    tn = n if po >= 4 else min(n, 256)

    def body(a_ref, b_ref, o_ref):
        o_ref[...] = (scale * _dot_tn(a_ref[...], b_ref[...])).astype(BF16)

    call = _pcall(body, name=name, grid=(po, n // tn), carry=carry,
                  in_specs=[pl.BlockSpec((None, s, m), lambda o, j: (o if pa_n > 1 else 0, 0, 0)),
                            pl.BlockSpec((None, s, tn), lambda o, j: (o if pb_n > 1 else 0, 0, j))],
                  out_specs=pl.BlockSpec((None, m, tn), lambda o, j: (o, 0, j)),
                  out_shape=_sds((po, m, n), BF16))
    return _carried(call, (a, b), carry)


def _mm_tn_rows(a, b, name, total_rows, row0, begun=None, tm=512, carry=None):
    p, s, m = a.shape
    n = b.shape[1]
    tm = min(tm, m)
    tiles = m // tm
    assert row0 % tm == 0 and m % tm == 0, (row0, m, tm)

    def body(a_ref, b_ref, *rest):
        rest[-1][...] = _dot_tn(a_ref[...], b_ref[...]).astype(BF16)

    in_specs = [pl.BlockSpec((None, s, tm), lambda o, i: (o, 0, i)), pl.BlockSpec((s, n), lambda o, i: (0, 0))]
    call = _pcall(body, name=name, grid=(p, tiles), in_specs=in_specs + ([HBM_SPEC] if begun is not None else []),
                  out_specs=pl.BlockSpec((tm, n), lambda o, i: (row0 // tm + o * tiles + i, 0)),
                  out_shape=_sds((total_rows, n), BF16), aliases={2: 0} if begun is not None else None, carry=carry)
    return _carried(call, (a, b, begun) if begun is not None else (a, b), carry)


def _sum_dots(a_ref, b_ref, nj, bt, rows=slice(None)):
    dot = _dot_nt if bt else _dot
    acc = dot(a_ref[0, rows, :], b_ref[0])
    for j in range(1, nj):
        acc = acc + dot(a_ref[j, rows, :], b_ref[j])
    return acc


def _mm_acc(a, b, name, out_dtype, tm=1024, bt=False, carry=None):
    nj, s, k = a.shape
    n = b.shape[1] if bt else b.shape[2]
    tm = min(tm, s)

    def body(a_ref, b_ref, o_ref):
        o_ref[...] = _sum_dots(a_ref, b_ref, nj, bt).astype(out_dtype)

    call = _pcall(body, name=name, grid=(s // tm,), carry=carry,
                  in_specs=[pl.BlockSpec((nj, tm, k), lambda i: (0, i, 0)),
                            pl.BlockSpec(b.shape, lambda i: (0, 0, 0))],
                  out_specs=pl.BlockSpec((tm, n), lambda i: (i, 0)), out_shape=_sds((s, n), out_dtype))
    return _carried(call, (a, b), carry)


def _rms_bwd_call(name, acts, weights, scratch, load, dh_rows, *, x, gain, dres, tm, carry, behind=None):
    s, n = x.shape
    tm = min(tm, s)
    n_act, n_w = len(acts), len(weights)

    def body(*refs):
        act_refs, w_refs = refs[:n_act], refs[n_act:n_act + n_w]
        x_ref, g_ref, r_ref, dx_ref, dxb_ref, dg_ref = refs[n_act + n_w:n_act + n_w + 6]
        held = refs[n_act + n_w + 6:]
        load(w_refs, held)

        @pl.when(pl.program_id(0) == 0)
        def _():
            dg_ref[...] = jnp.zeros_like(dg_ref)

        for rows in _row_chunks(tm):
            dh = dh_rows(act_refs, held, rows)
            xv = x_ref[rows, :]
            r = lax.rsqrt(jnp.mean(xv * xv, axis=-1, keepdims=True) + EPS)
            xh = xv * r
            dyg = dh * g_ref[...]
            dx = r_ref[rows, :] + r * (dyg - xh * jnp.mean(dyg * xh, axis=-1, keepdims=True))
            dx_ref[rows, :] = dx
            dxb_ref[rows, :] = dx.astype(BF16)
            dg_ref[...] += jnp.sum(dh * xh, axis=0, keepdims=True)

    def tile(a):
        return (pl.BlockSpec((tm, a.shape[1]), lambda i: (i, 0)) if a.ndim == 2
                else pl.BlockSpec((a.shape[0], tm, a.shape[2]), lambda i: (0, i, 0)))

    row = pl.BlockSpec((tm, n), lambda i: (i, 0))
    in_specs = [tile(a) for a in acts] + [HBM_SPEC] * n_w + [row, pl.BlockSpec((1, n), lambda i: (0, 0)), row]
    call = _pcall(body, name=name, grid=(s // tm,), in_specs=in_specs, carry=carry, behind=behind,
                  out_specs=[row, row, pl.BlockSpec((1, n), lambda i: (0, 0))],
                  out_shape=[_sds((s, n), F32), _sds((s, n), BF16), _sds((1, n), F32)], scratch=scratch)
    return _carried(call, tuple(acts) + tuple(weights) + (x, gain, dres), carry)


def _mm_acc_rms_bwd(a, b, name, *, x, gain, dres, scale=None, tm=512, bt=False, carry=None, behind=None):
    def load(w_refs, held):
        _load_once(w_refs[0], held[0], held[1])

    def dh_rows(act_refs, held, rows):
        dh = _sum_dots(act_refs[0], held[0], a.shape[0], bt, rows)
        return dh if scale is None else scale * dh

    return _rms_bwd_call(name, [a], [b], _resident(b), load, dh_rows, x=x, gain=gain, dres=dres, tm=tm, carry=carry,
                         behind=behind)


def _in_proj_bwd(dpa, dq, dkv, w_in_t, name, *, x, gain, dres, tm=512, carry=None, behind=None):
    def load(w_refs, held):
        _load_in_proj(w_refs[0], *held)

    def dh_rows(act_refs, held, rows):
        dpa_ref, dq_ref, dkv_ref = act_refs
        wq_ref, wkv_ref, wa_ref, _ = held
        dh = _dot(dq_ref[rows, :], wq_ref[...]) + _dot(dkv_ref[rows, :], wkv_ref[...])
        return dh + _sum_dots(dpa_ref, wa_ref, N_SEG, False, rows)

    return _rms_bwd_call(name, [dpa, dq, dkv], [w_in_t], IN_PROJ_WEIGHTS, load, dh_rows, x=x, gain=gain, dres=dres,
                         tm=tm, carry=carry, behind=behind)


def _adam(w, g, m, v):
    m2 = ADAM_B1 * m + (1.0 - ADAM_B1) * g
    v2 = ADAM_B2 * v + (1.0 - ADAM_B2) * (g * g)
    m_hat = m2 / (1.0 - ADAM_B1 ** ADAM_STEP)
    v_hat = v2 / (1.0 - ADAM_B2 ** ADAM_STEP)
    delta = -ADAM_LR * (m_hat / (jnp.sqrt(v_hat) + ADAM_EPS) + ADAM_WD * w)
    return delta, m2, v2


def _adamw(parts, w, m, v, name, behind=None):
    _, r, c = parts.shape
    tr = max(t for t in range(16, 257, 16) if r % t == 0)

    def body(p_ref, w_ref, m_ref, v_ref, g_ref, d_ref, m2_ref, v2_ref):
        g = p_ref[0].astype(F32)
        for i in range(1, N_DEV):
            g = g + p_ref[i].astype(F32)
        delta, m2, v2 = _adam(w_ref[...], g, m_ref[...], v_ref[...])
        g_ref[...] = g
        d_ref[...] = delta
        m2_ref[...] = m2
        v2_ref[...] = v2

    blk = pl.BlockSpec((tr, c), lambda i: (i, 0))
    return _pcall(body, name=name, grid=(r // tr,), behind=behind,
                  in_specs=[pl.BlockSpec((N_DEV, tr, c), lambda i: (0, i, 0)), blk, blk, blk],
                  out_specs=[blk] * 4, out_shape=[_sds((r, c), F32)] * 4)(parts, w, m, v)


def _position():
    return lax.axis_index("x"), lax.axis_index("y"), lax.axis_index("c")


def _slot(px, py, pc):
    return 4 * px + 2 * py + pc


def _row_window(ref, rows):
    r0, r1 = rows
    return ref if (r0, r1) == (0, ref.shape[0]) else ref.at[pl.ds(r0, r1 - r0)]


def _split_items(items):
    sources = [src for src, _, _ in items]
    begun = [(a, dest) for a, (_, _, dest) in enumerate(items) if dest is not None]
    aliases = {len(sources) + k: a for k, (a, _) in enumerate(begun)}
    return sources + [dest for _, dest in begun], [rows for _, rows, _ in items], aliases


def _gather_carry(items):
    na = len(items)
    carry_ins, windows, aliases = _split_items(items)

    def plan(ins, outs, sems):
        send_sems, recv_sems, local_sems = sems
        x, y, c = _position()
        me, sibling = (x, y, c), (x, y, 1 - c)
        chips = [(1 - x, y), (x, 1 - y), (1 - x, 1 - y)]
        ins = [_row_window(ins[a], windows[a]) for a in range(na)]

        def block_rows(a, block):
            return _row_window(outs[a].at[_slot(*block)], windows[a])

        def copy(a, k, block, to, src=None):
            rows = block_rows(a, block)
            return pltpu.make_async_remote_copy(src_ref=rows if src is None else src, dst_ref=rows,
                                                send_sem=send_sems.at[k, a], recv_sem=recv_sems.at[k, a],
                                                device_id=to, device_id_type=MESH)

        mine = [pltpu.make_async_copy(ins[a], block_rows(a, me), local_sems.at[a]) for a in range(na)]
        first = [copy(a, 0, me, sibling, src=ins[a]) for a in range(na)]
        for j, chip in enumerate(chips):
            first += [copy(a, 1 + j, me, (*chip, c), src=ins[a]) for a in range(na)]
        landed = [[copy(a, 1 + j, (*chip, c), me) for a in range(na)] for j, chip in enumerate(chips)]
        passed = [[copy(a, 4 + j, (*chip, c), sibling) for a in range(na)] for j, chip in enumerate(chips)]
        from_sibling = [copy(a, 0, sibling, me) for a in range(na)]
        for j, chip in enumerate(chips):
            from_sibling += [copy(a, 4 + j, (*chip, 1 - c), me) for a in range(na)]
        return mine, first, landed, passed, from_sibling

    def start(ins, outs, sems):
        mine, first, _, _, _ = plan(ins, outs, sems)
        for cp in mine + first:
            cp.start()

    def mid(ins, outs, sems):
        _, _, landed, passed, _ = plan(ins, outs, sems)
        for over_ici, onward in zip(landed, passed):
            for cp, fwd in zip(over_ici, onward):
                cp.wait_recv()
                fwd.start()

    def finish(ins, outs, sems):
        mine, first, _, passed, from_sibling = plan(ins, outs, sems)
        for cp in from_sibling:
            cp.wait_recv()
        for cp in first + [fwd for onward in passed for fwd in onward]:
            cp.wait_send()
        for cp in mine:
            cp.wait()

    return _Carry(carry_ins, [_sds((N_DEV,) + src.shape, src.dtype) for src, _, _ in items],
                  [pltpu.SemaphoreType.DMA((7, na)), pltpu.SemaphoreType.DMA((7, na)),
                   pltpu.SemaphoreType.DMA((na,))], start, finish, mid, aliases)


def _exchange_carry(scattered, replicated=()):
    items = list(scattered) + [(a, (0, a.shape[0]), None) for a in replicated]
    na, ns = len(items), len(scattered)
    carry_ins, windows, aliases = _split_items(items)

    def plan(ins, outs, sems):
        send_sems, recv_sems, local_sems = sems
        me = _slot(*_position())

        def source(a, j):
            return _row_window(ins[a].at[j] if a < ns else ins[a], windows[a])

        def copy(a, j, i):
            return pltpu.make_async_remote_copy(src_ref=source(a, j), dst_ref=_row_window(outs[a].at[i], windows[a]),
                                                send_sem=send_sems.at[j, a], recv_sem=recv_sems.at[i, a],
                                                device_id=(j >> 2, (j >> 1) & 1, j & 1), device_id_type=MESH)

        def own(a, j):
            return pltpu.make_async_copy(source(a, j), _row_window(outs[a].at[j], windows[a]), local_sems.at[a])

        return me, copy, own

    def start(ins, outs, sems):
        me, copy, own = plan(ins, outs, sems)
        for a in range(na):
            for j in range(N_DEV):
                @pl.when(me == j)
                def _():
                    own(a, j).start()

                @pl.when(me != j)
                def _():
                    copy(a, j, me).start()

    def finish(ins, outs, sems):
        me, copy, own = plan(ins, outs, sems)
        for a in range(na):
            for j in range(N_DEV):
                @pl.when(me == j)
                def _():
                    for i in range(N_DEV):
                        if i != j:
                            copy(a, j, i).wait_recv()
                    own(a, j).wait()

                @pl.when(me != j)
                def _():
                    copy(a, j, me).wait_send()

    return _Carry(carry_ins, [_sds((N_DEV,) + src.shape[-2:], src.dtype) for src, _, _ in items],
                  [pltpu.SemaphoreType.DMA((N_DEV, na)), pltpu.SemaphoreType.DMA((N_DEV, na)),
                   pltpu.SemaphoreType.DMA((na,))], start, finish, None, aliases)


HBM_ARRAY = pl.BlockSpec(memory_space=pltpu.HBM)
SEMAPHORES = pl.BlockSpec(memory_space=pltpu.SEMAPHORE)
DATAFLOW = pltpu.SideEffectType.DATAFLOW_SIDE_EFFECTING


def _exchange_copy(parts_ref, land_ref, send_sems, recv_sems, me, j):
    return pltpu.make_async_remote_copy(src_ref=parts_ref.at[j], dst_ref=land_ref.at[me], send_sem=send_sems.at[j],
                                        recv_sem=recv_sems.at[me], device_id=(j >> 2, (j >> 1) & 1, j & 1),
                                        device_id_type=MESH)


def _exchange_start(parts, name):
    def body(parts_ref, land_ref, send_sems, recv_sems, parts_thru, land_thru, token):
        me = _slot(*_position())
        for j in range(N_DEV):
            @pl.when(me == j)
            def _():
                pltpu.make_async_copy(parts_ref.at[j], land_ref.at[j], send_sems.at[j]).start()

            @pl.when(me != j)
            def _():
                _exchange_copy(parts_ref, land_ref, send_sems, recv_sems, me, j).start()
        token[...] = jnp.zeros_like(token)

    return pl.pallas_call(
        body, name=name,
        out_shape=(pltpu.SemaphoreType.DMA((N_DEV,)), pltpu.SemaphoreType.DMA((N_DEV,)),
                   pltpu.HBM(parts.shape, parts.dtype), pltpu.HBM(parts.shape, parts.dtype), _sds((8, 128), F32)),
        in_specs=(HBM_ARRAY, HBM_ARRAY),
        out_specs=(SEMAPHORES, SEMAPHORES, HBM_ARRAY, HBM_ARRAY, pl.BlockSpec(memory_space=pltpu.VMEM)),
        input_output_aliases={0: 2, 1: 3}, compiler_params=pltpu.CompilerParams(has_side_effects=DATAFLOW),
    )(pltpu.with_memory_space_constraint(parts, pltpu.HBM),
      pltpu.with_memory_space_constraint(lax.empty(parts.shape, parts.dtype), pltpu.HBM))


def _exchange_wait(send_sems, recv_sems, parts_thru, land_thru, after, name):
    def body(parts_ref, land_ref, send_sems, recv_sems, *rest):
        me = _slot(*_position())
        for j in range(N_DEV):
            @pl.when(me == j)
            def _():
                pltpu.make_async_copy(parts_ref.at[j], land_ref.at[j], send_sems.at[j]).wait()

            @pl.when(me != j)
            def _():
                both = pltpu.make_async_remote_copy(src_ref=parts_ref.at[j], dst_ref=land_ref.at[j],
                                                    send_sem=send_sems.at[j], recv_sem=recv_sems.at[j],
                                                    device_id=(j >> 2, (j >> 1) & 1, j & 1), device_id_type=MESH)
                both.wait_send()
                both.wait_recv()

    return pl.pallas_call(
        body, name=name, out_shape=(pltpu.HBM(parts_thru.shape, parts_thru.dtype),
                                    pltpu.HBM(parts_thru.shape, parts_thru.dtype)),
        in_specs=(HBM_ARRAY, HBM_ARRAY, SEMAPHORES, SEMAPHORES) + (pl.BlockSpec(memory_space=pl.ANY),) * len(after),
        out_specs=(HBM_ARRAY, HBM_ARRAY), input_output_aliases={0: 0, 1: 1},
        compiler_params=pltpu.CompilerParams(has_side_effects=DATAFLOW),
    )(parts_thru, land_thru, send_sems, recv_sems, *after)[1]


class _Mesh:
    def __init__(self, shards):
        self.shards, self.full, self.received, self.cache, self.pending, self.tokens = shards, {}, {}, {}, {}, {}

    def fetch(self, wanted):
        items = []
        for want in wanted:
            name, r0, r1 = want if isinstance(want, tuple) else (want, 0, self.shards[want].shape[0])
            items.append((self.shards[name], (r0, r1), self.full.get(name)))
        return _gather_carry(items)

    def fetched(self, wanted, results):
        self.full.update(zip([want[0] if isinstance(want, tuple) else want for want in wanted], results))

    def send(self, *payloads):
        return _exchange_carry([(parts, rows or (0, parts.shape[1]), self.received.get(name))
                                for name, parts, rows in payloads])

    def sent(self, names, results):
        self.received.update(zip(names, results))

    def send_apart(self, name, parts):
        *self.pending[name], self.tokens[name] = _exchange_start(parts, "exchange_" + name + "_start")
        return self.tokens[name]

    def sent_apart(self, name, after):
        self.received[name] = _exchange_wait(*self.pending.pop(name), after, "exchange_" + name + "_wait")

    def w(self, key):
        if key not in self.cache:
            self.cache[key] = self._layout(key)
        return self.cache[key]

    def _layout(self, key):
        if key in ("gu1", "gu2"):
            return self.full[key]
        if key in ("d1", "d2"):
            return self.full[key].reshape(4, FS, D)
        if key in ("out", "q", "o"):
            return self.full[key].reshape(D, D)
        if key == "kv":
            return self.full["kv"]
        if key == "convw":
            rows = self.full["conv"][:, :3, :].transpose(1, 0, 2).reshape(3, D)
            return jnp.concatenate([rows, jnp.zeros((5, D), F32)], axis=0)
        assert key == "win_t", key
        return self.full["win"].reshape(-1, D)


def _forward_backward(x, mem, target, g, rel_bias, sinks, ex):
    s = x.shape[0]
    def fetching(wanted, call, *args, **kw):
        res, got = call(*args, carry=ex.fetch(wanted), **kw)
        ex.fetched(wanted, got)
        return res

    h1 = fetching(["gu1", "conv"], _rmsnorm, x, g["ffn1"], "norm_ffn1")
    gu1, a1 = fetching(["d1", ("win", 0, 400)], _ffn_up, h1, ex.w("gu1").reshape(2, 4, FS, D), "ffn1_up")
    x1, h2 = fetching([("win", 400, 832)], _mm_res_norm, a1, ex.w("d1"), x, g["mix"], 0.5, "ffn1_down")
    pa, q, kv = fetching(["gu2"], _in_proj, h2, ex.w("win_t"), "in_proj")
    biasm = _bias_build(rel_bias, "bias_build")
    attn, lse = fetching(["out", "kv", "o"], _swa_fwd, q, kv, biasm, sinks, "swa_fwd")
    merged = fetching(["q"], _conv_merge_fwd, pa, attn, ex.w("convw"), "conv_merge_fwd")
    (x2, h3), _ = _mm_res_norm(merged[None], ex.w("out")[None], x1, g["xattn"], 1.0, "out_proj")
    mh, kv2 = _norm_mm(mem, g["mem"], ex.w("kv"), "xattn_kv")
    q2, o, lse2 = _xattn_fwd(h3, ex.w("q"), kv2, "xattn_fwd")
    (x3, h4), _ = _mm_res_norm(o[None], ex.w("o")[None], x2, g["ffn2"], 1.0, "xattn_o")
    gu2, a2 = fetching(["d2"], _ffn_up, h4, ex.w("gu2").reshape(2, 4, FS, D), "ffn2_up")
    dx4, dx4b, loss, d_final = _ffn_down_loss(a2, ex.w("d2"), x3, g["final"], target, "ffn2_down_loss")
    def sending(payloads, call, *args, **kw):
        res, got = call(*args, carry=ex.send(*payloads), **kw)
        ex.sent([name for name, _, _ in payloads], got)
        return res

    dw_d2 = _mm_tn(a2, dx4b[None], "dw_ffn2_down", scale=0.5)[0].reshape(N_DEV, -1, D)
    dgu2 = sending([("d2", dw_d2, (0, 288))], _ffn_down_bwd, dx4b, ex.w("d2"), gu2, "ffn2_down_bwd").reshape(8, s, FS)
    dw_gu2 = sending([("d2", dw_d2, (288, 352))], _mm_tn, dgu2, h4[None], "dw_ffn2_up", scale=0.5)
    dx3, dx3b, d_ffn2 = sending([("gu2", dw_gu2, (0, 368))], _mm_acc_rms_bwd, dgu2, ex.w("gu2"), "ffn2_up_bwd",
                                x=x3, gain=g["ffn2"], dres=dx4, scale=0.5)
    dw_o = _mm_tn(o[None], dx3b[None], "dw_xattn_o")[0].reshape(N_DEV, -1, D)
    (dq2, dkv2), _ = _xattn_bwd(q2, kv2, o, dx3b, ex.w("o"), lse2, "xattn_bwd")
    dw_q = _mm_tn(h3[None], dq2[None], "dw_xattn_q")[0].reshape(N_DEV, -1, D)
    (dx2, dx2b, d_xattn), _ = _mm_acc_rms_bwd(dq2[None], ex.w("q")[None], "xattn_q_bwd", x=x2, gain=g["xattn"],
                                              dres=dx3, bt=True)
    dw_kv, d_mem = _norm_mm_bwd(dkv2, mh, ex.w("kv"), mem, "xattn_kv_bwd")
    dmerged, _ = _mm_acc(dx2b[None], ex.w("out")[None], "out_proj_bwd", BF16, bt=True)
    dw_out = _mm_tn(merged[None], dx2b[None], "dw_out_proj")[0].reshape(N_DEV, -1, D)
    dattn, dpa, d_convw = sending([("kv", dw_kv, None)], _conv_merge_bwd,
                                  dmerged, pa, attn, ex.w("convw"), "conv_merge_bwd")
    dq, dkv, dbias, d_sinks = sending([("gu2", dw_gu2, (368, FS)), ("out", dw_out, None)], _swa_bwd,
                                      q, kv, attn, dattn, lse, biasm, sinks, "swa_bwd")
    d_relb = _bias_bwd(dbias, "bias_bwd")
    w_rows = ex.w("win_t").shape[0]
    dw_in = sending([("o", dw_o, None), ("q", dw_q, None)], _mm_tn_rows, dpa, h2, "dw_in_proj_a", w_rows, NQ + NKV)
    dw_in = _mm_tn_rows(dq[None], h2, "dw_in_proj_q", w_rows, 0, begun=dw_in)[0]
    dw_in = _mm_tn_rows(dkv[None], h2, "dw_in_proj_kv", w_rows, NQ, begun=dw_in)[0].reshape(N_DEV, -1, D)
    (dx1, dx1b, d_mix), _ = _in_proj_bwd(dpa, dq, dkv, ex.w("win_t"), "in_proj_bwd", x=x1, gain=g["mix"], dres=dx2,
                                         behind=ex.send_apart("win", dw_in))
    dw_d1 = _mm_tn(a1, dx1b[None], "dw_ffn1_down", scale=0.5)[0].reshape(N_DEV, -1, D)
    dgu1 = _ffn_down_bwd(dx1b, ex.w("d1"), gu1, "ffn1_down_bwd", behind=ex.send_apart("d1", dw_d1))[0]
    dgu1 = dgu1.reshape(8, s, FS)
    dw_gu1 = _mm_tn(dgu1, h1[None], "dw_ffn1_up", scale=0.5)[0]
    (dx0, _, d_ffn1), _ = _mm_acc_rms_bwd(dgu1, ex.w("gu1"), "ffn1_up_bwd", x=x, gain=g["ffn1"], dres=dx1,
                                          scale=0.5, behind=ex.send_apart("gu1", dw_gu1))

    relb_row = jnp.concatenate([d_relb[:, :REL_BUCKETS].T.reshape(1, REL_BUCKETS * N_HEADS), d_sinks[:, :N_HEADS],
                                jnp.zeros((1, D - REL_BUCKETS * N_HEADS - N_HEADS), F32)], axis=1)
    loss_row = jnp.concatenate([loss[0:1, 0:1], jnp.zeros((1, D - 1), F32)], axis=1)
    small = jnp.concatenate([d_ffn1, d_mix, d_xattn, d_mem, d_ffn2, d_final, relb_row, loss_row, d_convw[0:3],
                             jnp.zeros((SMALL_ROWS - ROW_CONV - 3, D), F32)], axis=0)
    return dx0, small


def _pack_small(norms, final, relb, sinks, conv_local, me):
    relb_row = jnp.concatenate([relb.reshape(1, -1), sinks.reshape(1, -1),
                                jnp.zeros((1, D - REL_BUCKETS * N_HEADS - N_HEADS), F32)], axis=1)
    conv_rows = lax.dynamic_update_slice(jnp.zeros((3, D), F32), conv_local.reshape(3, -1), (0, 128 * me))
    return jnp.concatenate(list(norms) + [final.reshape(1, D), relb_row, jnp.zeros((1, D), F32), conv_rows,
                                          jnp.zeros((SMALL_ROWS - ROW_CONV - 3, D), F32)], axis=0)


def kernel(x, mem, positions, rel_bias, ffn1_norm, ffn1_w_gu, ffn1_w_down, mix_norm, w_in, sinks, conv_w, w_out, xattn_norm, mem_norm, xattn_wq, xattn_wkv, xattn_wo, ffn2_norm, ffn2_w_gu, ffn2_w_down, final_norm, loss_target, m_rel_bias, m_ffn1_norm, m_ffn1_w_gu, m_ffn1_w_down, m_mix_norm, m_w_in, m_sinks, m_conv_w, m_w_out, m_xattn_norm, m_mem_norm, m_xattn_wq, m_xattn_wkv, m_xattn_wo, m_ffn2_norm, m_ffn2_w_gu, m_ffn2_w_down, m_final_norm, v_rel_bias, v_ffn1_norm, v_ffn1_w_gu, v_ffn1_w_down, v_mix_norm, v_w_in, v_sinks, v_conv_w, v_w_out, v_xattn_norm, v_mem_norm, v_xattn_wq, v_xattn_wkv, v_xattn_wo, v_ffn2_norm, v_ffn2_w_gu, v_ffn2_w_down, v_final_norm):
    del positions
    me = _slot(*_position())
    big = dict(gu1=(ffn1_w_gu, m_ffn1_w_gu, v_ffn1_w_gu), d1=(ffn1_w_down, m_ffn1_w_down, v_ffn1_w_down),
               win=(w_in, m_w_in, v_w_in), out=(w_out, m_w_out, v_w_out), q=(xattn_wq, m_xattn_wq, v_xattn_wq),
               kv=(xattn_wkv, m_xattn_wkv, v_xattn_wkv), o=(xattn_wo, m_xattn_wo, v_xattn_wo),
               gu2=(ffn2_w_gu, m_ffn2_w_gu, v_ffn2_w_gu), d2=(ffn2_w_down, m_ffn2_w_down, v_ffn2_w_down))
    order = list(big)
    transposed = ("gu1", "gu2", "win")
    local = {k: tuple(t[0].T if k in transposed else t[0] for t in big[k]) for k in order}
    shards = {k: local[k][0].astype(BF16) for k in order}
    shards["conv"] = jnp.concatenate([conv_w[0], jnp.zeros((5, 128), F32)], axis=0)
    ex = _Mesh(shards)
    gains = dict(ffn1=ffn1_norm, mix=mix_norm, xattn=xattn_norm, mem=mem_norm, ffn2=ffn2_norm,
                 final=final_norm.reshape(1, D))
    dx, small = _forward_backward(x[0], mem[0], loss_target[0], gains, rel_bias, sinks, ex)
    apart = ("d1", "win", "gu1")
    big_out = {k: _adamw(ex.received[k], *local[k], "adamw_" + k, behind=ex.tokens["gu1"])
               for k in order if k not in apart}
    for k in apart[:-1]:
        ex.sent_apart(k, after=[big_out[j][1] for j in big_out])
        big_out[k] = _adamw(ex.received[k], *local[k], "adamw_" + k)
    small_parts = _run_alone(_exchange_carry([], [small]), "exchange_small", after=[big_out[k][1] for k in big_out])[0]
    packed = [_pack_small(norms, final, relb, sk, conv, me) for norms, final, relb, sk, conv in (
        ((ffn1_norm, mix_norm, xattn_norm, mem_norm, ffn2_norm), final_norm, rel_bias, sinks, conv_w),
        ((m_ffn1_norm, m_mix_norm, m_xattn_norm, m_mem_norm, m_ffn2_norm), m_final_norm, m_rel_bias, m_sinks, m_conv_w),
        ((v_ffn1_norm, v_mix_norm, v_xattn_norm, v_mem_norm, v_ffn2_norm), v_final_norm, v_rel_bias, v_sinks, v_conv_w))]
    small_out = _adamw(small_parts, *packed, "adamw_small")
    ex.sent_apart("gu1", after=[dx, small_out[1]] + [big_out[k][1] for k in big_out])
    big_out["gu1"] = _adamw(ex.received["gu1"], *local["gu1"], "adamw_gu1")
    big_out = {k: [t.T if k in transposed else t for t in big_out[k]] for k in order}

    def unpack(t):
        conv = lax.dynamic_slice(t[ROW_CONV:ROW_CONV + 3], (0, 128 * me), (3, 128))[None]
        nrel = REL_BUCKETS * N_HEADS
        return dict(ffn1_norm=t[0:1], mix_norm=t[1:2], xattn_norm=t[2:3], mem_norm=t[3:4], ffn2_norm=t[4:5],
                    final_norm=t[5], rel_bias=t[ROW_RELB, :nrel].reshape(REL_BUCKETS, N_HEADS),
                    sinks=t[ROW_RELB:ROW_RELB + 1, nrel:nrel + N_HEADS], conv_w=conv)

    names = dict(gu1="ffn1_w_gu", d1="ffn1_w_down", win="w_in", out="w_out", q="xattn_wq", kv="xattn_wkv",
                 o="xattn_wo", gu2="ffn2_w_gu", d2="ffn2_w_down")
    results = []
    for idx in range(4):
        leaves = unpack(small_out[idx])
        leaves.update({names[k]: big_out[k][idx][None] for k in order})
        results.append(leaves)
    weights = ("rel_bias", "ffn1_norm", "ffn1_w_gu", "ffn1_w_down", "mix_norm", "w_in", "sinks", "conv_w", "w_out",
               "xattn_norm", "mem_norm", "xattn_wq", "xattn_wkv", "xattn_wo", "ffn2_norm", "ffn2_w_gu", "ffn2_w_down",
               "final_norm")
    loss = small_out[0][ROW_LOSS, 0]
    return (loss, dx[None], *[leaves[n] for leaves in results for n in weights])
```
